```python
import jax, jax.numpy as jnp
from jax import lax
import numpy as np

D_MODEL = 1024
BATCH = 8
SEQ = 4096
DEPTH = 4

CHUNK = 64
QBLOCK = 128
N_A_LAYERS = DEPTH // 2
N_B_LAYERS = DEPTH - N_A_LAYERS

GDN_HEAD_DIM = 128
GDN_HEADS = D_MODEL // GDN_HEAD_DIM
GDN_WIDTH = GDN_HEADS * GDN_HEAD_DIM
CONV_K = 4
GDN_PROJ = 4 * GDN_WIDTH + 2 * GDN_HEADS

MLA_HEADS = D_MODEL // 128
QK_NOPE = 128
QK_ROPE = 64
QK_HEAD = QK_NOPE + QK_ROPE
V_HEAD = 128
KV_LORA = D_MODEL // 4
Q_LORA = 3 * D_MODEL // 8
ROPE_BASE = 10000.0

D_FF = ((8 * D_MODEL // 3 + 127) // 128) * 128
N_MOD = 9
EPS = 1e-6
MAX_POS_OFFSET = 2048

kernel_name = "hybrid_gdn_mla_yoco_macaron_adaln"


def rms_norm(t, g):
    tf = t.astype(jnp.float32)
    y = tf * lax.rsqrt(jnp.mean(tf * tf, axis=-1, keepdims=True) + EPS)
    return (y * g.astype(jnp.float32)).astype(t.dtype)


def l2_norm(t):
    tf = t.astype(jnp.float32)
    return tf * lax.rsqrt(jnp.sum(tf * tf, axis=-1, keepdims=True) + EPS)


def modulate(t, g, shift, scale):
    return rms_norm(t, g) * (1 + scale[:, None, :]) + shift[:, None, :]


def swiglu(h, w_in, w_out):
    gate, up = jnp.split(h @ w_in, 2, axis=-1)
    return (jax.nn.silu(gate) * up) @ w_out


def rotary(t, positions):
    half = t.shape[-1] // 2
    inv_freq = ROPE_BASE ** (-jnp.arange(half, dtype=jnp.float32) / half)
    ang = positions.astype(jnp.float32)[..., None] * inv_freq
    cos = jnp.cos(ang)[:, :, None, :]
    sin = jnp.sin(ang)[:, :, None, :]
    tf = t.astype(jnp.float32)
    t1, t2 = tf[..., :half], tf[..., half:]
    return jnp.concatenate([t1 * cos - t2 * sin, t2 * cos + t1 * sin], axis=-1).astype(t.dtype)


def causal_depthwise_conv(t, w):
    k = w.shape[0]
    return lax.conv_general_dilated(
        t, w[:, None, :].astype(t.dtype), window_strides=(1,), padding=[(k - 1, 0)],
        dimension_numbers=("NWC", "WIO", "NWC"), feature_group_count=t.shape[-1])


def gated_delta_rule(q, k, v, g, beta):
    b_, s_, h_, dk = q.shape
    dv = v.shape[-1]
    nc = s_ // CHUNK
    q = q.astype(jnp.float32) * (dk ** -0.5)

    def to_chunks(t):
        return t.astype(jnp.float32).reshape(b_, nc, CHUNK, h_, -1).transpose(1, 0, 3, 2, 4)

    qc, kc, vc = to_chunks(q), to_chunks(k), to_chunks(v)
    gc = g.astype(jnp.float32).reshape(b_, nc, CHUNK, h_).transpose(1, 0, 3, 2)
    bc = beta.astype(jnp.float32).reshape(b_, nc, CHUNK, h_).transpose(1, 0, 3, 2)
    gcum = jnp.cumsum(gc, axis=-1)

    idx = jnp.arange(CHUNK)
    incl = idx[:, None] >= idx[None, :]
    strict = idx[:, None] > idx[None, :]
    diff = gcum[..., :, None] - gcum[..., None, :]
    decay = jnp.where(incl, jnp.exp(jnp.where(incl, diff, 0.0)), 0.0)

    kb = kc * bc[..., None]
    lower = jnp.where(strict, jnp.einsum("nbhid,nbhjd->nbhij", kb, kc) * decay, 0.0)
    eye = jnp.eye(CHUNK, dtype=jnp.float32)
    rhs = jnp.concatenate([vc * bc[..., None], kb * jnp.exp(gcum)[..., None]], axis=-1)
    sol = lax.linalg.triangular_solve(eye + lower, rhs, left_side=True, lower=True)
    u, w = sol[..., :dv], sol[..., dv:]
    attn_intra = jnp.where(incl, jnp.einsum("nbhid,nbhjd->nbhij", qc, kc) * decay, 0.0)

    def step(state, inp):
        qi, ki, ui, wi, gi, ai = inp
        v_new = ui - jnp.einsum("bhck,bhkv->bhcv", wi, state)
        o = jnp.einsum("bhck,bhkv->bhcv", qi * jnp.exp(gi)[..., None], state) \
            + jnp.einsum("bhcj,bhjv->bhcv", ai, v_new)
        g_last = gi[..., -1]
        k_dec = ki * jnp.exp(g_last[..., None] - gi)[..., None]
        state = state * jnp.exp(g_last)[..., None, None] + jnp.einsum("bhck,bhcv->bhkv", k_dec, v_new)
        return state, o

    state0 = jnp.zeros((b_, h_, dk, dv), jnp.float32)
    _, o = lax.scan(step, state0, (qc, kc, u, w, gcum, attn_intra))
    return o.transpose(1, 0, 3, 2, 4).reshape(b_, s_, h_, dv)


def gated_deltanet(h, w_in, conv_w, a_log, dt_bias, norm_g, w_out):
    b_, s_, _ = h.shape
    proj = h @ w_in
    qkv = proj[..., :3 * GDN_WIDTH]
    z = proj[..., 3 * GDN_WIDTH:4 * GDN_WIDTH]
    b_logit = proj[..., 4 * GDN_WIDTH:4 * GDN_WIDTH + GDN_HEADS]
    a_logit = proj[..., 4 * GDN_WIDTH + GDN_HEADS:]
    qkv = jax.nn.silu(causal_depthwise_conv(qkv, conv_w))
    q, k, v = [t.reshape(b_, s_, GDN_HEADS, GDN_HEAD_DIM) for t in jnp.split(qkv, 3, axis=-1)]
    q, k = l2_norm(q), l2_norm(k)
    beta = jax.nn.sigmoid(b_logit.astype(jnp.float32))
    g = -jnp.exp(a_log.astype(jnp.float32)) * jax.nn.softplus(
        a_logit.astype(jnp.float32) + dt_bias.astype(jnp.float32))
    o = gated_delta_rule(q, k, v, g, beta)
    zf = z.reshape(b_, s_, GDN_HEADS, GDN_HEAD_DIM).astype(jnp.float32)
    o = rms_norm(o, norm_g) * jax.nn.silu(zf)
    return o.reshape(b_, s_, GDN_WIDTH).astype(h.dtype) @ w_out


def mla_shared_kv(x, c, ada_w, ada_b, norm_g, w_dkv, kv_norm_g, w_ukv, k_norm_g, positions):
    b_, s_, _ = x.shape
    shift, scale = jnp.split(jax.nn.silu(c) @ ada_w + ada_b, 2, axis=-1)
    h = modulate(x, norm_g, shift, scale)
    ckv = h @ w_dkv
    latent = rms_norm(ckv[..., :KV_LORA], kv_norm_g)
    k_rope = ckv[..., KV_LORA:]
    kv = (latent @ w_ukv).reshape(b_, s_, MLA_HEADS, QK_NOPE + V_HEAD)
    k_nope, v = kv[..., :QK_NOPE], kv[..., QK_NOPE:]
    k = jnp.concatenate(
        [k_nope, jnp.broadcast_to(k_rope[:, :, None, :], (b_, s_, MLA_HEADS, QK_ROPE))], axis=-1)
    k = rms_norm(k, k_norm_g)
    k = jnp.concatenate([k[..., :QK_NOPE], rotary(k[..., QK_NOPE:], positions)], axis=-1)
    return k, v


def block_causal_attention(q, k, v):
    b_, s_, h_, dq = q.shape
    nqb = s_ // QBLOCK
    qb = q.reshape(b_, nqb, QBLOCK, h_, dq).transpose(1, 0, 2, 3, 4)
    key_chunk = jnp.arange(s_) // CHUNK
    scale = QK_HEAD ** -0.5

    def one_block(args):
        qi, blk = args
        s = jnp.einsum("bqhd,bkhd->bhqk", qi, k).astype(jnp.float32) * scale
        q_chunk = (blk * QBLOCK + jnp.arange(QBLOCK)) // CHUNK
        mask = key_chunk[None, :] <= q_chunk[:, None]
        s = jnp.where(mask, s, jnp.finfo(jnp.float32).min)
        p = jax.nn.softmax(s, axis=-1)
        return jnp.einsum("bhqk,bkhd->bqhd", p.astype(v.dtype), v)

    o = lax.map(one_block, (qb, jnp.arange(nqb)))
    return o.transpose(1, 0, 2, 3, 4).reshape(b_, s_, h_ * v.shape[-1])


def mla_attention(h, k, v, positions, w_dq, q_lora_norm_g, w_uq, q_norm_g, w_out):
    b_, s_, _ = h.shape
    q = rms_norm(h @ w_dq, q_lora_norm_g) @ w_uq
    q = rms_norm(q.reshape(b_, s_, MLA_HEADS, QK_HEAD), q_norm_g)
    q = jnp.concatenate([q[..., :QK_NOPE], rotary(q[..., QK_NOPE:], positions)], axis=-1)
    return block_causal_attention(q, k, v) @ w_out


def _fwd_setup_inputs(seed: int = 0) -> dict:
    key = jax.random.key(seed)
    ks = iter(jax.random.split(key, 40))
    f32 = jnp.float32

    def nrm(shape, fan_in, scale=1.0):
        return jax.random.normal(next(ks), shape, f32) * (scale * fan_in ** -0.5)

    def gain(shape):
        return 1.0 + 0.02 * jax.random.normal(next(ks), shape, f32)

    def bias(shape):
        return 0.02 * jax.random.normal(next(ks), shape, f32)

    x = jax.random.normal(next(ks), (BATCH, SEQ, D_MODEL), f32)
    c = jax.random.normal(next(ks), (BATCH, D_MODEL), f32)
    offs = jax.random.randint(next(ks), (BATCH, 1), 0, MAX_POS_OFFSET)
    positions = (offs + jnp.arange(SEQ)[None, :]).astype(jnp.int32)

    dt = jnp.exp(jax.random.uniform(next(ks), (N_A_LAYERS, GDN_HEADS), f32,
                                    np.log(1e-3), np.log(1e-1)))
    return {
        "x": x,
        "c": c,
        "positions": positions,
        "ada_w": nrm((DEPTH, D_MODEL, N_MOD * D_MODEL), D_MODEL, 0.5),
        "ada_b": bias((DEPTH, N_MOD * D_MODEL)),
        "norm_g": gain((DEPTH, 3, D_MODEL)),
        "ffn_w_in": nrm((DEPTH, 2, D_MODEL, 2 * D_FF), D_MODEL),
        "ffn_w_out": nrm((DEPTH, 2, D_FF, D_MODEL), D_FF),
        "gdn_w_in": nrm((N_A_LAYERS, D_MODEL, GDN_PROJ), D_MODEL),
        "gdn_conv_w": nrm((N_A_LAYERS, CONV_K, 3 * GDN_WIDTH), CONV_K),
        "gdn_a_log": jnp.log(jax.random.uniform(next(ks), (N_A_LAYERS, GDN_HEADS), f32, 1.0, 16.0)),
        "gdn_dt_bias": dt + jnp.log(-jnp.expm1(-dt)),
        "gdn_norm_g": gain((N_A_LAYERS, GDN_HEAD_DIM)),
        "gdn_w_out": nrm((N_A_LAYERS, GDN_WIDTH, D_MODEL), GDN_WIDTH),
        "kv_ada_w": nrm((D_MODEL, 2 * D_MODEL), D_MODEL, 0.5),
        "kv_ada_b": bias((2 * D_MODEL,)),
        "kv_norm_g": gain((D_MODEL,)),
        "mla_w_dkv": nrm((D_MODEL, KV_LORA + QK_ROPE), D_MODEL),
        "mla_kv_norm_g": gain((KV_LORA,)),
        "mla_w_ukv": nrm((KV_LORA, MLA_HEADS * (QK_NOPE + V_HEAD)), KV_LORA),
        "mla_k_norm_g": gain((QK_HEAD,)),
        "mla_w_dq": nrm((N_B_LAYERS, D_MODEL, Q_LORA), D_MODEL),
        "mla_q_lora_norm_g": gain((N_B_LAYERS, Q_LORA)),
        "mla_w_uq": nrm((N_B_LAYERS, Q_LORA, MLA_HEADS * QK_HEAD), Q_LORA),
        "mla_q_norm_g": gain((N_B_LAYERS, QK_HEAD)),
        "mla_w_out": nrm((N_B_LAYERS, MLA_HEADS * V_HEAD, D_MODEL), MLA_HEADS * V_HEAD),
    }


def _fwd_reference(x, c, positions, ada_w, ada_b, norm_g, ffn_w_in, ffn_w_out,
              gdn_w_in, gdn_conv_w, gdn_a_log, gdn_dt_bias, gdn_norm_g, gdn_w_out,
              kv_ada_w, kv_ada_b, kv_norm_g, mla_w_dkv, mla_kv_norm_g, mla_w_ukv, mla_k_norm_g,
              mla_w_dq, mla_q_lora_norm_g, mla_w_uq, mla_q_norm_g, mla_w_out):
    b_ = x.shape[0]
    c_act = jax.nn.silu(c)
    shared_k = shared_v = None
    for l in range(DEPTH):
        mod = (c_act @ ada_w[l] + ada_b[l]).reshape(b_, N_MOD, D_MODEL)
        h = modulate(x, norm_g[l, 0], mod[:, 0], mod[:, 1])
        x = x + 0.5 * mod[:, 2][:, None, :] * swiglu(h, ffn_w_in[l, 0], ffn_w_out[l, 0])
        h = modulate(x, norm_g[l, 1], mod[:, 3], mod[:, 4])
        if l < N_A_LAYERS:
            y = gated_deltanet(h, gdn_w_in[l], gdn_conv_w[l], gdn_a_log[l], gdn_dt_bias[l],
                               gdn_norm_g[l], gdn_w_out[l])
        else:
            j = l - N_A_LAYERS
            y = mla_attention(h, shared_k, shared_v, positions, mla_w_dq[j], mla_q_lora_norm_g[j],
                              mla_w_uq[j], mla_q_norm_g[j], mla_w_out[j])
        x = x + mod[:, 5][:, None, :] * y
        h = modulate(x, norm_g[l, 2], mod[:, 6], mod[:, 7])
        x = x + 0.5 * mod[:, 8][:, None, :] * swiglu(h, ffn_w_in[l, 1], ffn_w_out[l, 1])
        if l == N_A_LAYERS - 1:
            shared_k, shared_v = mla_shared_kv(x, c, kv_ada_w, kv_ada_b, kv_norm_g, mla_w_dkv,
                                               mla_kv_norm_g, mla_w_ukv, mla_k_norm_g, positions)
    return x


import jax as _jax
import jax.numpy as _jnp

TWIN_FORMAT = 'train_step'
FWD_PARAMS = ['x', 'c', 'positions', 'ada_w', 'ada_b', 'norm_g', 'ffn_w_in', 'ffn_w_out', 'gdn_w_in', 'gdn_conv_w', 'gdn_a_log', 'gdn_dt_bias', 'gdn_norm_g', 'gdn_w_out', 'kv_ada_w', 'kv_ada_b', 'kv_norm_g', 'mla_w_dkv', 'mla_kv_norm_g', 'mla_w_ukv', 'mla_k_norm_g', 'mla_w_dq', 'mla_q_lora_norm_g', 'mla_w_uq', 'mla_q_norm_g', 'mla_w_out']
TWIN_WEIGHTS = ['ada_w', 'ada_b', 'norm_g', 'ffn_w_in', 'ffn_w_out', 'gdn_w_in', 'gdn_conv_w', 'gdn_a_log', 'gdn_dt_bias', 'gdn_norm_g', 'gdn_w_out', 'kv_ada_w', 'kv_ada_b', 'kv_norm_g', 'mla_w_dkv', 'mla_kv_norm_g', 'mla_w_ukv', 'mla_k_norm_g', 'mla_w_dq', 'mla_q_lora_norm_g', 'mla_w_uq', 'mla_q_norm_g', 'mla_w_out']
TWIN_DIFF_INPUT = 'x'
TWIN_INPUTS = ['x', 'c', 'positions', 'ada_w', 'ada_b', 'norm_g', 'ffn_w_in', 'ffn_w_out', 'gdn_w_in', 'gdn_conv_w', 'gdn_a_log', 'gdn_dt_bias', 'gdn_norm_g', 'gdn_w_out', 'kv_ada_w', 'kv_ada_b', 'kv_norm_g', 'mla_w_dkv', 'mla_kv_norm_g', 'mla_w_ukv', 'mla_k_norm_g', 'mla_w_dq', 'mla_q_lora_norm_g', 'mla_w_uq', 'mla_q_norm_g', 'mla_w_out', 'loss_target', 'm_ada_w', 'm_ada_b', 'm_norm_g', 'm_ffn_w_in', 'm_ffn_w_out', 'm_gdn_w_in', 'm_gdn_conv_w', 'm_gdn_a_log', 'm_gdn_dt_bias', 'm_gdn_norm_g', 'm_gdn_w_out', 'm_kv_ada_w', 'm_kv_ada_b', 'm_kv_norm_g', 'm_mla_w_dkv', 'm_mla_kv_norm_g', 'm_mla_w_ukv', 'm_mla_k_norm_g', 'm_mla_w_dq', 'm_mla_q_lora_norm_g', 'm_mla_w_uq', 'm_mla_q_norm_g', 'm_mla_w_out', 'v_ada_w', 'v_ada_b', 'v_norm_g', 'v_ffn_w_in', 'v_ffn_w_out', 'v_gdn_w_in', 'v_gdn_conv_w', 'v_gdn_a_log', 'v_gdn_dt_bias', 'v_gdn_norm_g', 'v_gdn_w_out', 'v_kv_ada_w', 'v_kv_ada_b', 'v_kv_norm_g', 'v_mla_w_dkv', 'v_mla_kv_norm_g', 'v_mla_w_ukv', 'v_mla_k_norm_g', 'v_mla_w_dq', 'v_mla_q_lora_norm_g', 'v_mla_w_uq', 'v_mla_q_norm_g', 'v_mla_w_out']
TWIN_OUTPUTS = ['loss', 'grad_x', 'grad_ada_w', 'grad_ada_b', 'grad_norm_g', 'grad_ffn_w_in', 'grad_ffn_w_out', 'grad_gdn_w_in', 'grad_gdn_conv_w', 'grad_gdn_a_log', 'grad_gdn_dt_bias', 'grad_gdn_norm_g', 'grad_gdn_w_out', 'grad_kv_ada_w', 'grad_kv_ada_b', 'grad_kv_norm_g', 'grad_mla_w_dkv', 'grad_mla_kv_norm_g', 'grad_mla_w_ukv', 'grad_mla_k_norm_g', 'grad_mla_w_dq', 'grad_mla_q_lora_norm_g', 'grad_mla_w_uq', 'grad_mla_q_norm_g', 'grad_mla_w_out', 'delta_ada_w', 'delta_ada_b', 'delta_norm_g', 'delta_ffn_w_in', 'delta_ffn_w_out', 'delta_gdn_w_in', 'delta_gdn_conv_w', 'delta_gdn_a_log', 'delta_gdn_dt_bias', 'delta_gdn_norm_g', 'delta_gdn_w_out', 'delta_kv_ada_w', 'delta_kv_ada_b', 'delta_kv_norm_g', 'delta_mla_w_dkv', 'delta_mla_kv_norm_g', 'delta_mla_w_ukv', 'delta_mla_k_norm_g', 'delta_mla_w_dq', 'delta_mla_q_lora_norm_g', 'delta_mla_w_uq', 'delta_mla_q_norm_g', 'delta_mla_w_out', 'new_m_ada_w', 'new_m_ada_b', 'new_m_norm_g', 'new_m_ffn_w_in', 'new_m_ffn_w_out', 'new_m_gdn_w_in', 'new_m_gdn_conv_w', 'new_m_gdn_a_log', 'new_m_gdn_dt_bias', 'new_m_gdn_norm_g', 'new_m_gdn_w_out', 'new_m_kv_ada_w', 'new_m_kv_ada_b', 'new_m_kv_norm_g', 'new_m_mla_w_dkv', 'new_m_mla_kv_norm_g', 'new_m_mla_w_ukv', 'new_m_mla_k_norm_g', 'new_m_mla_w_dq', 'new_m_mla_q_lora_norm_g', 'new_m_mla_w_uq', 'new_m_mla_q_norm_g', 'new_m_mla_w_out', 'new_v_ada_w', 'new_v_ada_b', 'new_v_norm_g', 'new_v_ffn_w_in', 'new_v_ffn_w_out', 'new_v_gdn_w_in', 'new_v_gdn_conv_w', 'new_v_gdn_a_log', 'new_v_gdn_dt_bias', 'new_v_gdn_norm_g', 'new_v_gdn_w_out', 'new_v_kv_ada_w', 'new_v_kv_ada_b', 'new_v_kv_norm_g', 'new_v_mla_w_dkv', 'new_v_mla_kv_norm_g', 'new_v_mla_w_ukv', 'new_v_mla_k_norm_g', 'new_v_mla_w_dq', 'new_v_mla_q_lora_norm_g', 'new_v_mla_w_uq', 'new_v_mla_q_norm_g', 'new_v_mla_w_out']
TWIN_LEAF_KINDS = {'loss': 'loss', 'grad_x': 'grad_x', 'grad_ada_w': 'grad_w', 'grad_ada_b': 'grad_w', 'grad_norm_g': 'grad_w', 'grad_ffn_w_in': 'grad_w', 'grad_ffn_w_out': 'grad_w', 'grad_gdn_w_in': 'grad_w', 'grad_gdn_conv_w': 'grad_w', 'grad_gdn_a_log': 'grad_w', 'grad_gdn_dt_bias': 'grad_w', 'grad_gdn_norm_g': 'grad_w', 'grad_gdn_w_out': 'grad_w', 'grad_kv_ada_w': 'grad_w', 'grad_kv_ada_b': 'grad_w', 'grad_kv_norm_g': 'grad_w', 'grad_mla_w_dkv': 'grad_w', 'grad_mla_kv_norm_g': 'grad_w', 'grad_mla_w_ukv': 'grad_w', 'grad_mla_k_norm_g': 'grad_w', 'grad_mla_w_dq': 'grad_w', 'grad_mla_q_lora_norm_g': 'grad_w', 'grad_mla_w_uq': 'grad_w', 'grad_mla_q_norm_g': 'grad_w', 'grad_mla_w_out': 'grad_w', 'delta_ada_w': 'delta_w', 'delta_ada_b': 'delta_w', 'delta_norm_g': 'delta_w', 'delta_ffn_w_in': 'delta_w', 'delta_ffn_w_out': 'delta_w', 'delta_gdn_w_in': 'delta_w', 'delta_gdn_conv_w': 'delta_w', 'delta_gdn_a_log': 'delta_w', 'delta_gdn_dt_bias': 'delta_w', 'delta_gdn_norm_g': 'delta_w', 'delta_gdn_w_out': 'delta_w', 'delta_kv_ada_w': 'delta_w', 'delta_kv_ada_b': 'delta_w', 'delta_kv_norm_g': 'delta_w', 'delta_mla_w_dkv': 'delta_w', 'delta_mla_kv_norm_g': 'delta_w', 'delta_mla_w_ukv': 'delta_w', 'delta_mla_k_norm_g': 'delta_w', 'delta_mla_w_dq': 'delta_w', 'delta_mla_q_lora_norm_g': 'delta_w', 'delta_mla_w_uq': 'delta_w', 'delta_mla_q_norm_g': 'delta_w', 'delta_mla_w_out': 'delta_w', 'new_m_ada_w': 'new_m', 'new_m_ada_b': 'new_m', 'new_m_norm_g': 'new_m', 'new_m_ffn_w_in': 'new_m', 'new_m_ffn_w_out': 'new_m', 'new_m_gdn_w_in': 'new_m', 'new_m_gdn_conv_w': 'new_m', 'new_m_gdn_a_log': 'new_m', 'new_m_gdn_dt_bias': 'new_m', 'new_m_gdn_norm_g': 'new_m', 'new_m_gdn_w_out': 'new_m', 'new_m_kv_ada_w': 'new_m', 'new_m_kv_ada_b': 'new_m', 'new_m_kv_norm_g': 'new_m', 'new_m_mla_w_dkv': 'new_m', 'new_m_mla_kv_norm_g': 'new_m', 'new_m_mla_w_ukv': 'new_m', 'new_m_mla_k_norm_g': 'new_m', 'new_m_mla_w_dq': 'new_m', 'new_m_mla_q_lora_norm_g': 'new_m', 'new_m_mla_w_uq': 'new_m', 'new_m_mla_q_norm_g': 'new_m', 'new_m_mla_w_out': 'new_m', 'new_v_ada_w': 'new_v', 'new_v_ada_b': 'new_v', 'new_v_norm_g': 'new_v', 'new_v_ffn_w_in': 'new_v', 'new_v_ffn_w_out': 'new_v', 'new_v_gdn_w_in': 'new_v', 'new_v_gdn_conv_w': 'new_v', 'new_v_gdn_a_log': 'new_v', 'new_v_gdn_dt_bias': 'new_v', 'new_v_gdn_norm_g': 'new_v', 'new_v_gdn_w_out': 'new_v', 'new_v_kv_ada_w': 'new_v', 'new_v_kv_ada_b': 'new_v', 'new_v_kv_norm_g': 'new_v', 'new_v_mla_w_dkv': 'new_v', 'new_v_mla_kv_norm_g': 'new_v', 'new_v_mla_w_ukv': 'new_v', 'new_v_mla_k_norm_g': 'new_v', 'new_v_mla_w_dq': 'new_v', 'new_v_mla_q_lora_norm_g': 'new_v', 'new_v_mla_w_uq': 'new_v', 'new_v_mla_q_norm_g': 'new_v', 'new_v_mla_w_out': 'new_v'}


def _forward(args):
    return _fwd_reference(*[args[k] for k in FWD_PARAMS])


def _output_shape():
    def fwd():
        inp = _fwd_setup_inputs(0)
        return _fwd_reference(*[inp[k] for k in FWD_PARAMS])
    out = _jax.eval_shape(fwd)
    return out.shape, out.dtype

N_MICROBATCH = 1
ADAM_LR = 0.001
ADAM_B1 = 0.9
ADAM_B2 = 0.999
ADAM_EPS = 1e-08
ADAM_WD = 0.01
ADAM_STEP = 10
PER_EXAMPLE_BATCH_AXIS = {'x': 0, 'c': 0, 'positions': 0, 'loss_target': 0}
SHARED_INPUTS = []
_WEIGHT_DTYPES = {'ada_w': _jnp.float32, 'ada_b': _jnp.float32, 'norm_g': _jnp.float32, 'ffn_w_in': _jnp.float32, 'ffn_w_out': _jnp.float32, 'gdn_w_in': _jnp.float32, 'gdn_conv_w': _jnp.float32, 'gdn_a_log': _jnp.float32, 'gdn_dt_bias': _jnp.float32, 'gdn_norm_g': _jnp.float32, 'gdn_w_out': _jnp.float32, 'kv_ada_w': _jnp.float32, 'kv_ada_b': _jnp.float32, 'kv_norm_g': _jnp.float32, 'mla_w_dkv': _jnp.float32, 'mla_kv_norm_g': _jnp.float32, 'mla_w_ukv': _jnp.float32, 'mla_k_norm_g': _jnp.float32, 'mla_w_dq': _jnp.float32, 'mla_q_lora_norm_g': _jnp.float32, 'mla_w_uq': _jnp.float32, 'mla_q_norm_g': _jnp.float32, 'mla_w_out': _jnp.float32}
MOMENT_SCALE = {'ada_w': 3.292836e-01, 'ada_b': 8.271635e-01, 'norm_g': 8.648782e-01, 'ffn_w_in': 2.532738e-02, 'ffn_w_out': 4.212682e-02, 'gdn_w_in': 1.352502e-01, 'gdn_conv_w': 1.489692e-01, 'gdn_a_log': 4.047021e+00, 'gdn_dt_bias': 3.813532e+00, 'gdn_norm_g': 1.056024e+01, 'gdn_w_out': 1.661130e-01, 'kv_ada_w': 6.879954e-01, 'kv_ada_b': 1.350556e+00, 'kv_norm_g': 4.561894e-01, 'mla_w_dkv': 8.116595e-01, 'mla_kv_norm_g': 2.440965e+00, 'mla_w_ukv': 2.330514e-01, 'mla_k_norm_g': 1.465617e-01, 'mla_w_dq': 2.016680e-02, 'mla_q_lora_norm_g': 1.984378e-02, 'mla_w_uq': 1.012932e-02, 'mla_q_norm_g': 7.610787e-02, 'mla_w_out': 1.888367e-01}


def _to_microbatches(a, axis):
    t = _jnp.moveaxis(a, axis, 0)
    t = t.reshape((N_MICROBATCH, t.shape[0] // N_MICROBATCH) + t.shape[1:])
    return _jnp.moveaxis(t, 1, axis + 1)


def setup_inputs(seed: int = 0) -> dict:
    inp = _fwd_setup_inputs(seed)
    key = _jax.random.fold_in(_jax.random.key(seed), 7919)
    shape, _ = _output_shape()
    out = dict(inp)
    out["loss_target"] = _jax.random.normal(_jax.random.fold_in(key, 0), shape, _jnp.float32)
    for i, name in enumerate(TWIN_WEIGHTS):
        w = inp[name].astype(_jnp.float32)
        if MOMENT_SCALE is None:
            s = _jnp.sqrt(_jnp.mean(_jnp.square(w)) + 1e-30)
        else:
            s = MOMENT_SCALE[name]
        km, kv = _jax.random.split(_jax.random.fold_in(key, i + 1))
        out[name] = w
        out["m_" + name] = s * _jax.random.normal(km, w.shape, _jnp.float32)
        out["v_" + name] = (s * s) * _jax.random.uniform(kv, w.shape, _jnp.float32, 0.5, 1.5)
    if N_MICROBATCH > 1:
        for name, axis in PER_EXAMPLE_BATCH_AXIS.items():
            out[name] = _to_microbatches(out[name], axis)
    return {'x': out['x'], 'c': out['c'], 'positions': out['positions'], 'ada_w': out['ada_w'], 'ada_b': out['ada_b'], 'norm_g': out['norm_g'], 'ffn_w_in': out['ffn_w_in'], 'ffn_w_out': out['ffn_w_out'], 'gdn_w_in': out['gdn_w_in'], 'gdn_conv_w': out['gdn_conv_w'], 'gdn_a_log': out['gdn_a_log'], 'gdn_dt_bias': out['gdn_dt_bias'], 'gdn_norm_g': out['gdn_norm_g'], 'gdn_w_out': out['gdn_w_out'], 'kv_ada_w': out['kv_ada_w'], 'kv_ada_b': out['kv_ada_b'], 'kv_norm_g': out['kv_norm_g'], 'mla_w_dkv': out['mla_w_dkv'], 'mla_kv_norm_g': out['mla_kv_norm_g'], 'mla_w_ukv': out['mla_w_ukv'], 'mla_k_norm_g': out['mla_k_norm_g'], 'mla_w_dq': out['mla_w_dq'], 'mla_q_lora_norm_g': out['mla_q_lora_norm_g'], 'mla_w_uq': out['mla_w_uq'], 'mla_q_norm_g': out['mla_q_norm_g'], 'mla_w_out': out['mla_w_out'], 'loss_target': out['loss_target'], 'm_ada_w': out['m_ada_w'], 'm_ada_b': out['m_ada_b'], 'm_norm_g': out['m_norm_g'], 'm_ffn_w_in': out['m_ffn_w_in'], 'm_ffn_w_out': out['m_ffn_w_out'], 'm_gdn_w_in': out['m_gdn_w_in'], 'm_gdn_conv_w': out['m_gdn_conv_w'], 'm_gdn_a_log': out['m_gdn_a_log'], 'm_gdn_dt_bias': out['m_gdn_dt_bias'], 'm_gdn_norm_g': out['m_gdn_norm_g'], 'm_gdn_w_out': out['m_gdn_w_out'], 'm_kv_ada_w': out['m_kv_ada_w'], 'm_kv_ada_b': out['m_kv_ada_b'], 'm_kv_norm_g': out['m_kv_norm_g'], 'm_mla_w_dkv': out['m_mla_w_dkv'], 'm_mla_kv_norm_g': out['m_mla_kv_norm_g'], 'm_mla_w_ukv': out['m_mla_w_ukv'], 'm_mla_k_norm_g': out['m_mla_k_norm_g'], 'm_mla_w_dq': out['m_mla_w_dq'], 'm_mla_q_lora_norm_g': out['m_mla_q_lora_norm_g'], 'm_mla_w_uq': out['m_mla_w_uq'], 'm_mla_q_norm_g': out['m_mla_q_norm_g'], 'm_mla_w_out': out['m_mla_w_out'], 'v_ada_w': out['v_ada_w'], 'v_ada_b': out['v_ada_b'], 'v_norm_g': out['v_norm_g'], 'v_ffn_w_in': out['v_ffn_w_in'], 'v_ffn_w_out': out['v_ffn_w_out'], 'v_gdn_w_in': out['v_gdn_w_in'], 'v_gdn_conv_w': out['v_gdn_conv_w'], 'v_gdn_a_log': out['v_gdn_a_log'], 'v_gdn_dt_bias': out['v_gdn_dt_bias'], 'v_gdn_norm_g': out['v_gdn_norm_g'], 'v_gdn_w_out': out['v_gdn_w_out'], 'v_kv_ada_w': out['v_kv_ada_w'], 'v_kv_ada_b': out['v_kv_ada_b'], 'v_kv_norm_g': out['v_kv_norm_g'], 'v_mla_w_dkv': out['v_mla_w_dkv'], 'v_mla_kv_norm_g': out['v_mla_kv_norm_g'], 'v_mla_w_ukv': out['v_mla_w_ukv'], 'v_mla_k_norm_g': out['v_mla_k_norm_g'], 'v_mla_w_dq': out['v_mla_w_dq'], 'v_mla_q_lora_norm_g': out['v_mla_q_lora_norm_g'], 'v_mla_w_uq': out['v_mla_w_uq'], 'v_mla_q_norm_g': out['v_mla_q_norm_g'], 'v_mla_w_out': out['v_mla_w_out']}


def _loss(weights, diff, rest, loss_target):
    with _jax.named_scope("forward"):
        args = {**rest, TWIN_DIFF_INPUT: diff, **{k: w.astype(_WEIGHT_DTYPES[k]) for k, w in weights.items()}}
        y = _forward(args)
    with _jax.named_scope("loss_head"):
        err = _jnp.square(y.astype(_jnp.float32) - loss_target)
        return 0.5 * _jnp.sum(_jnp.mean(err, axis=-1)) if err.ndim else 0.5 * err


def _adamw(w, g, m, v):
    m = ADAM_B1 * m + (1.0 - ADAM_B1) * g
    v = ADAM_B2 * v + (1.0 - ADAM_B2) * _jnp.square(g)
    m_hat = m / (1.0 - ADAM_B1 ** ADAM_STEP)
    v_hat = v / (1.0 - ADAM_B2 ** ADAM_STEP)
    delta = -ADAM_LR * (m_hat / (_jnp.sqrt(v_hat) + ADAM_EPS) + ADAM_WD * w)
    return delta, m, v


def reference(x, c, positions, ada_w, ada_b, norm_g, ffn_w_in, ffn_w_out, gdn_w_in, gdn_conv_w, gdn_a_log, gdn_dt_bias, gdn_norm_g, gdn_w_out, kv_ada_w, kv_ada_b, kv_norm_g, mla_w_dkv, mla_kv_norm_g, mla_w_ukv, mla_k_norm_g, mla_w_dq, mla_q_lora_norm_g, mla_w_uq, mla_q_norm_g, mla_w_out, loss_target, m_ada_w, m_ada_b, m_norm_g, m_ffn_w_in, m_ffn_w_out, m_gdn_w_in, m_gdn_conv_w, m_gdn_a_log, m_gdn_dt_bias, m_gdn_norm_g, m_gdn_w_out, m_kv_ada_w, m_kv_ada_b, m_kv_norm_g, m_mla_w_dkv, m_mla_kv_norm_g, m_mla_w_ukv, m_mla_k_norm_g, m_mla_w_dq, m_mla_q_lora_norm_g, m_mla_w_uq, m_mla_q_norm_g, m_mla_w_out, v_ada_w, v_ada_b, v_norm_g, v_ffn_w_in, v_ffn_w_out, v_gdn_w_in, v_gdn_conv_w, v_gdn_a_log, v_gdn_dt_bias, v_gdn_norm_g, v_gdn_w_out, v_kv_ada_w, v_kv_ada_b, v_kv_norm_g, v_mla_w_dkv, v_mla_kv_norm_g, v_mla_w_ukv, v_mla_k_norm_g, v_mla_w_dq, v_mla_q_lora_norm_g, v_mla_w_uq, v_mla_q_norm_g, v_mla_w_out):
    given = dict(x=x, c=c, positions=positions, ada_w=ada_w, ada_b=ada_b, norm_g=norm_g, ffn_w_in=ffn_w_in, ffn_w_out=ffn_w_out, gdn_w_in=gdn_w_in, gdn_conv_w=gdn_conv_w, gdn_a_log=gdn_a_log, gdn_dt_bias=gdn_dt_bias, gdn_norm_g=gdn_norm_g, gdn_w_out=gdn_w_out, kv_ada_w=kv_ada_w, kv_ada_b=kv_ada_b, kv_norm_g=kv_norm_g, mla_w_dkv=mla_w_dkv, mla_kv_norm_g=mla_kv_norm_g, mla_w_ukv=mla_w_ukv, mla_k_norm_g=mla_k_norm_g, mla_w_dq=mla_w_dq, mla_q_lora_norm_g=mla_q_lora_norm_g, mla_w_uq=mla_w_uq, mla_q_norm_g=mla_q_norm_g, mla_w_out=mla_w_out, loss_target=loss_target, m_ada_w=m_ada_w, m_ada_b=m_ada_b, m_norm_g=m_norm_g, m_ffn_w_in=m_ffn_w_in, m_ffn_w_out=m_ffn_w_out, m_gdn_w_in=m_gdn_w_in, m_gdn_conv_w=m_gdn_conv_w, m_gdn_a_log=m_gdn_a_log, m_gdn_dt_bias=m_gdn_dt_bias, m_gdn_norm_g=m_gdn_norm_g, m_gdn_w_out=m_gdn_w_out, m_kv_ada_w=m_kv_ada_w, m_kv_ada_b=m_kv_ada_b, m_kv_norm_g=m_kv_norm_g, m_mla_w_dkv=m_mla_w_dkv, m_mla_kv_norm_g=m_mla_kv_norm_g, m_mla_w_ukv=m_mla_w_ukv, m_mla_k_norm_g=m_mla_k_norm_g, m_mla_w_dq=m_mla_w_dq, m_mla_q_lora_norm_g=m_mla_q_lora_norm_g, m_mla_w_uq=m_mla_w_uq, m_mla_q_norm_g=m_mla_q_norm_g, m_mla_w_out=m_mla_w_out, v_ada_w=v_ada_w, v_ada_b=v_ada_b, v_norm_g=v_norm_g, v_ffn_w_in=v_ffn_w_in, v_ffn_w_out=v_ffn_w_out, v_gdn_w_in=v_gdn_w_in, v_gdn_conv_w=v_gdn_conv_w, v_gdn_a_log=v_gdn_a_log, v_gdn_dt_bias=v_gdn_dt_bias, v_gdn_norm_g=v_gdn_norm_g, v_gdn_w_out=v_gdn_w_out, v_kv_ada_w=v_kv_ada_w, v_kv_ada_b=v_kv_ada_b, v_kv_norm_g=v_kv_norm_g, v_mla_w_dkv=v_mla_w_dkv, v_mla_kv_norm_g=v_mla_kv_norm_g, v_mla_w_ukv=v_mla_w_ukv, v_mla_k_norm_g=v_mla_k_norm_g, v_mla_w_dq=v_mla_w_dq, v_mla_q_lora_norm_g=v_mla_q_lora_norm_g, v_mla_w_uq=v_mla_w_uq, v_mla_q_norm_g=v_mla_q_norm_g, v_mla_w_out=v_mla_w_out)
    weights = {n: given[n] for n in TWIN_WEIGHTS}
    shared = {n: given[n] for n in SHARED_INPUTS}
    per_example = {n: given[n] for n in ['x', 'c', 'positions']}
    grad_fn = _jax.value_and_grad(_loss, argnums=(0, 1))

    def one_microbatch(ex, loss_target):
        ex = dict(ex)
        diff = ex.pop(TWIN_DIFF_INPUT)
        return grad_fn(weights, diff, {**shared, **ex}, loss_target)

    if N_MICROBATCH == 1:
        loss, (grad_w, grad_x) = one_microbatch(per_example, given["loss_target"])
    else:
        def body(carry, xs):
            loss_sum, grad_sum = carry
            l_k, (gw_k, gx_k) = one_microbatch(xs[0], xs[1])
            with _jax.named_scope("update"):
                return (loss_sum + l_k, _jax.tree.map(_jnp.add, grad_sum, gw_k)), gx_k

        init = (_jnp.zeros((), _jnp.float32), _jax.tree.map(_jnp.zeros_like, weights))
        (loss, grad_w), grad_x = _jax.lax.scan(body, init, (per_example, given["loss_target"]))
    with _jax.named_scope("update"):
        delta_w, new_m, new_v = {}, {}, {}
        for n in TWIN_WEIGHTS:
            delta_w[n], new_m[n], new_v[n] = _adamw(weights[n], grad_w[n], given["m_" + n], given["v_" + n])
    return (loss, grad_x, *[grad_w[n] for n in TWIN_WEIGHTS], *[delta_w[n] for n in TWIN_WEIGHTS],
            *[new_m[n] for n in TWIN_WEIGHTS], *[new_v[n] for n in TWIN_WEIGHTS])
```

```python
import functools
import math

import jax
import jax.numpy as jnp
from jax import lax
from jax.experimental import pallas as pl
from jax.experimental.pallas import tpu as pltpu

F32 = jnp.float32
BF16 = jnp.bfloat16

N_DEV = 8
D = 1024
D_FF = 2816
DEPTH = 4
N_A = 2
N_MOD = 9
HEADS = 8
HEAD = 128
CHUNK = 64
CONV_K = 4
KV_LORA = 256
Q_LORA = 384
NOPE = 128
ROPE = 64
QK_HEAD = NOPE + ROPE
HEAD_PAD = 256
ROPE_BASE = 10000.0
EPS = 1e-6
LR, B1, B2, ADAM_EPS, WD, STEP = 0.001, 0.9, 0.999, 1e-08, 0.01, 10

VMEM_LIMIT = 48 * 1024 * 1024
ROW_TILE = 256
MESH = pl.DeviceIdType.MESH

_NN = (((1,), (0,)), ((), ()))
_NT = (((1,), (1,)), ((), ()))
_TN = (((0,), (0,)), ((), ()))
_DIMS = {"nn": _NN, "nt": _NT, "tn": _TN}


def _params(dims=None):
    return pltpu.CompilerParams(dimension_semantics=dims, vmem_limit_bytes=VMEM_LIMIT)


def _tile(n, target):
    for t in range(target - target % 128, 0, -128):
        if n % t == 0:
            return t
    return n


def _matmul(pairs, form, name, out_dtype=F32, tm=1024, tn=1408, tk=512, boffs=None):
    a0, b0 = pairs[0]
    if form == "nn":
        m, n = a0.shape[0], b0.shape[1]
        ks = [a.shape[1] for a, _ in pairs]
    elif form == "nt":
        m, n = a0.shape[0], b0.shape[0]
        ks = [a.shape[1] for a, _ in pairs]
    else:
        m, n = a0.shape[1], b0.shape[1]
        ks = [a.shape[0] for a, _ in pairs]
    tm, tn = _tile(m, tm), _tile(n, tn)
    tk = min(_tile(k, tk) for k in ks)
    assert m % tm == 0 and n % tn == 0 and all(k % tk == 0 for k in ks), (name, m, n, ks)
    boffs = boffs or [0] * len(pairs)
    assert all(o % tk == 0 for o in boffs)
    steps = [k // tk for k in ks]
    starts = [sum(steps[:p]) for p in range(len(pairs))]
    nk = sum(steps)

    def kidx(p, k):
        return jnp.clip(k - starts[p], 0, steps[p] - 1)

    in_specs, args = [], []
    for p, (a, b) in enumerate(pairs):
        if form == "tn":
            in_specs.append(pl.BlockSpec((tk, tm), lambda i, j, k, p=p: (kidx(p, k), i)))
            in_specs.append(pl.BlockSpec((tk, tn), lambda i, j, k, p=p: (kidx(p, k), j)))
        elif form == "nn":
            in_specs.append(pl.BlockSpec((tm, tk), lambda i, j, k, p=p: (i, kidx(p, k))))
            in_specs.append(pl.BlockSpec((tk, tn), lambda i, j, k, p=p: (kidx(p, k), j)))
        else:
            in_specs.append(pl.BlockSpec((tm, tk), lambda i, j, k, p=p: (i, kidx(p, k))))
            in_specs.append(pl.BlockSpec((tn, tk), lambda i, j, k, p=p, o=boffs[p] // tk: (j, kidx(p, k) + o)))
        args += [a, b]
    dims = _DIMS[form]
    npairs = len(pairs)

    def body(*refs):
        o_ref = refs[2 * npairs]
        k = pl.program_id(2)

        def prod(p):
            return lax.dot_general(refs[2 * p][...].astype(BF16), refs[2 * p + 1][...].astype(BF16), dims,
                                   preferred_element_type=F32)

        if nk == 1:
            o_ref[...] = prod(0).astype(o_ref.dtype)
            return
        acc = refs[2 * npairs + 1]

        @pl.when(k == 0)
        def _():
            acc[...] = jnp.zeros_like(acc)

        for p in range(npairs):
            @pl.when((k >= starts[p]) & (k < starts[p] + steps[p]))
            def _(p=p):
                acc[...] += prod(p)

        @pl.when(k == nk - 1)
        def _():
            o_ref[...] = acc[...].astype(o_ref.dtype)

    return pl.pallas_call(
        body, name=name, grid=(m // tm, n // tn, nk), in_specs=in_specs,
        out_specs=pl.BlockSpec((tm, tn), lambda i, j, k: (i, j)),
        out_shape=jax.ShapeDtypeStruct((m, n), out_dtype),
        scratch_shapes=[] if nk == 1 else [pltpu.VMEM((tm, tn), F32)],
        compiler_params=_params(("parallel", "parallel", "arbitrary")),
    )(*args)


def _mm(a, b, form, name, **kw):
    return _matmul([(a, b)], form, name, **kw)


def _cols(spec, g):
    return spec[g] if isinstance(spec, list) else spec


def _rowwise_fwd(fn, rows, pars, outs, name, groups=1, ts=ROW_TILE):
    s = rows[0][0].shape[0]
    ts = min(ts, s)
    assert s % ts == 0
    nr, npar = len(rows), len(pars)

    def body(*refs):
        par_t = [r[...] for r in refs[nr:nr + npar]]
        out_refs = refs[nr + npar:]
        for g in range(groups):
            row_t = []
            for r, (_, spec) in zip(refs[:nr], rows):
                c0, w = _cols(spec, g)
                row_t.append(r[:, c0:c0 + w].astype(F32))
            res = fn(g, *row_t, *par_t)
            for o_ref, val, (_, _, spec) in zip(out_refs, res, outs):
                c0, w = _cols(spec, g)
                o_ref[:, c0:c0 + w] = val.astype(o_ref.dtype)

    return pl.pallas_call(
        body, name=name, grid=(s // ts,),
        in_specs=[pl.BlockSpec((ts, a.shape[1]), lambda i: (i, 0)) for a, _ in rows]
        + [pl.BlockSpec(p.shape, lambda i: (0, 0)) for p in pars],
        out_specs=[pl.BlockSpec((ts, w), lambda i: (i, 0)) for w, _, _ in outs],
        out_shape=[jax.ShapeDtypeStruct((s, w), dt) for w, dt, _ in outs],
        compiler_params=_params(("parallel",)),
    )(*[a for a, _ in rows], *pars)


def _rowwise_bwd(fn, rows, pars, outs, douts, gmap, gshapes, name, groups=1, add=None, par_grads=True,
                 ts=ROW_TILE):
    s = rows[0][0].shape[0]
    ts = min(ts, s)
    assert s % ts == 0
    nr, npar, nout, ng = len(rows), len(pars), len(outs), len(gshapes)
    add = add or {}
    add_keys = sorted(add)

    def body(*refs):
        row_refs = refs[:nr]
        par_refs = refs[nr:nr + npar]
        dout_refs = refs[nr + npar:nr + npar + nout]
        add_refs = refs[nr + npar + nout:nr + npar + nout + len(add_keys)]
        g_refs = refs[nr + npar + nout + len(add_keys):][:ng]
        pg_refs = refs[nr + npar + nout + len(add_keys) + ng:]
        par_t = [r[...] for r in par_refs]
        par_acc = [None] * npar
        shared_acc = {}
        for g in range(groups):
            row_t = []
            for r, (_, spec) in zip(row_refs, rows):
                c0, w = _cols(spec, g)
                row_t.append(r[:, c0:c0 + w].astype(F32))
            cts = []
            for r, (_, _, spec) in zip(dout_refs, outs):
                c0, w = _cols(spec, g)
                cts.append(r[:, c0:c0 + w].astype(F32))
            _, vjp = jax.vjp(lambda *t, g=g: tuple(fn(g, *t)), *row_t, *par_t)
            grads = vjp(tuple(cts))
            for k in range(nr):
                if gmap[k] is None:
                    continue
                gi, spec = gmap[k]
                if isinstance(spec, list) or groups == 1:
                    c0, w = _cols(spec, g)
                    val = grads[k]
                    if gi in add:
                        val = val + add_refs[add_keys.index(gi)][:, c0:c0 + w].astype(F32)
                    g_refs[gi][:, c0:c0 + w] = val.astype(g_refs[gi].dtype)
                else:
                    shared_acc[k] = grads[k] if k not in shared_acc else shared_acc[k] + grads[k]
            if par_grads:
                for k in range(npar):
                    pg = grads[nr + k]
                    par_acc[k] = pg if par_acc[k] is None else par_acc[k] + pg
        for k, val in shared_acc.items():
            gi, (c0, w) = gmap[k]
            assert gi not in add
            g_refs[gi][:, c0:c0 + w] = val.astype(g_refs[gi].dtype)
        if par_grads:
            first = pl.program_id(0) == 0
            for k in range(npar):
                @pl.when(first)
                def _(k=k):
                    pg_refs[k][...] = par_acc[k]

                @pl.when(jnp.logical_not(first))
                def _(k=k):
                    pg_refs[k][...] += par_acc[k]

    out_specs = [pl.BlockSpec((ts, w), lambda i: (i, 0)) for w, _ in gshapes]
    out_shape = [jax.ShapeDtypeStruct((s, w), dt) for w, dt in gshapes]
    if par_grads:
        out_specs += [pl.BlockSpec(p.shape, lambda i: (0, 0)) for p in pars]
        out_shape += [jax.ShapeDtypeStruct(p.shape, F32) for p in pars]
    return pl.pallas_call(
        body, name=name, grid=(s // ts,),
        in_specs=[pl.BlockSpec((ts, a.shape[1]), lambda i: (i, 0)) for a, _ in rows]
        + [pl.BlockSpec(p.shape, lambda i: (0, 0)) for p in pars]
        + [pl.BlockSpec((ts, a.shape[1]), lambda i: (i, 0)) for a in douts]
        + [pl.BlockSpec((ts, add[k].shape[1]), lambda i: (i, 0)) for k in add_keys],
        out_specs=out_specs, out_shape=out_shape,
        compiler_params=_params(("arbitrary",)),
    )(*[a for a, _ in rows], *pars, *douts, *[add[k] for k in add_keys])


def _sigmoid(x):
    return 1.0 / (1.0 + jnp.exp(-x))


def _silu(x):
    return x * _sigmoid(x)


def _softplus(x):
    return jnp.maximum(x, 0.0) + jnp.log(1.0 + jnp.exp(-jnp.abs(x)))


def _rms(t, g, n=None):
    n = n or t.shape[-1]
    return t * lax.rsqrt(jnp.sum(t * t, axis=-1, keepdims=True) / n + EPS) * g


def _modulate_fn(g, x, gain, scale, shift):
    return (_rms(x, gain) * (1.0 + scale) + shift,)


def _resgate_fn(g, x, y, gm):
    return (x + gm * y,)


def _gate_only_fn(g, y, gm):
    return (gm * y,)


def _gdn_gates_fn(g, b_logit, a_logit, a_log, dt_bias):
    return _sigmoid(b_logit), -jnp.exp(a_log) * _softplus(a_logit + dt_bias)


def _gdn_outnorm_fn(g, o, z, gain):
    return (_rms(o, gain) * _silu(z),)


def _rms_fn(g, t, gain):
    return (_rms(t, gain),)


@jax.custom_vjp
def _swap_halves(t):
    return pltpu.roll(t, 32, 1)


_swap_halves.defvjp(lambda t: (pltpu.roll(t, 32, 1), None), lambda _, ct: (pltpu.roll(ct, 96, 1),))


def _head_norm_rope_fn(g, nope, rope, cosf, sins, gain_n, gain_r):
    first = lax.broadcasted_iota(jnp.int32, rope.shape, 1) < ROPE
    ss = jnp.sum(nope * nope, axis=-1, keepdims=True) + jnp.sum(jnp.where(first, rope * rope, 0.0), axis=-1,
                                                                 keepdims=True)
    r = lax.rsqrt(ss / QK_HEAD + EPS)
    tn = nope * r * gain_n
    tr = rope * r * gain_r
    rot = jnp.where(first, tr * cosf + _swap_halves(tr) * sins, 0.0)
    return tn, rot


def _q_norm_rope_fn(g, nope, rope, cosf, sins, gain_n, gain_r):
    tn, rot = _head_norm_rope_fn(g, nope, rope, cosf, sins, gain_n, gain_r)
    return (jnp.concatenate([tn, rot], axis=1),)


def _k_norm_rope_fn(g, nope, val, rope, cosf, sins, gain_n, gain_r):
    tn, rot = _head_norm_rope_fn(g, nope, rope, cosf, sins, gain_n, gain_r)
    return jnp.concatenate([tn, rot], axis=1), val


def _loss_fn(g, y, target):
    e = y - target
    return (jnp.sum(e * e, axis=-1, keepdims=True) * (0.5 / D) * jnp.ones((1, 128), F32),)


def _ffn_in(h, wg, wu, name, tm=1024, tn=256):
    s = h.shape[0]
    tm = min(tm, s)

    def body(h_ref, wg_ref, wu_ref, g_ref, u_ref, a_ref):
        hb = h_ref[...]
        gate = jnp.dot(hb, wg_ref[...], preferred_element_type=F32)
        up = jnp.dot(hb, wu_ref[...], preferred_element_type=F32)
        g_ref[...] = gate.astype(BF16)
        u_ref[...] = up.astype(BF16)
        a_ref[...] = (_silu(gate) * up).astype(BF16)

    spec = pl.BlockSpec((tm, tn), lambda i, j: (i, j))
    return pl.pallas_call(
        body, name=name, grid=(s // tm, D_FF // tn),
        in_specs=[pl.BlockSpec((tm, D), lambda i, j: (i, 0)), pl.BlockSpec((D, tn), lambda i, j: (0, j)),
                  pl.BlockSpec((D, tn), lambda i, j: (0, j))],
        out_specs=[spec, spec, spec], out_shape=[jax.ShapeDtypeStruct((s, D_FF), BF16)] * 3,
        compiler_params=_params(("parallel", "parallel")),
    )(h, wg, wu)


def _ffn_bwd_act(dy, wo, gate, up, name, tm=1024, tn=256):
    s = dy.shape[0]
    tm = min(tm, s)

    def body(dy_ref, wo_ref, g_ref, u_ref, dg_ref, du_ref):
        dact = lax.dot_general(dy_ref[...], wo_ref[...], _NT, preferred_element_type=F32)
        gate = g_ref[...].astype(F32)
        up = u_ref[...].astype(F32)
        sg = _sigmoid(gate)
        dg_ref[...] = (dact * up * (sg * (1.0 + gate * (1.0 - sg)))).astype(BF16)
        du_ref[...] = (dact * (gate * sg)).astype(BF16)

    spec = pl.BlockSpec((tm, tn), lambda i, j: (i, j))
    return pl.pallas_call(
        body, name=name, grid=(s // tm, D_FF // tn),
        in_specs=[pl.BlockSpec((tm, D), lambda i, j: (i, 0)), pl.BlockSpec((tn, D), lambda i, j: (j, 0)), spec, spec],
        out_specs=[spec, spec], out_shape=[jax.ShapeDtypeStruct((s, D_FF), BF16)] * 2,
        compiler_params=_params(("parallel", "parallel")),
    )(dy, wo, gate, up)


def _shift_down(x, d):
    rows = lax.broadcasted_iota(jnp.int32, x.shape, 0)
    return jnp.where(rows >= d, pltpu.roll(x, d, 0), 0.0)


def _shift_up(x, d):
    n = x.shape[0]
    rows = lax.broadcasted_iota(jnp.int32, x.shape, 0)
    return jnp.where(rows < n - d, pltpu.roll(x, n - d, 0), 0.0)


def _conv_post(pre, is_qk):
    a = _silu(pre)
    l2 = a * lax.rsqrt(jnp.sum(a * a, axis=-1, keepdims=True) + EPS)
    return jnp.where(is_qk, l2, a)


def _conv_pre(x, w):
    pre = x * w[CONV_K - 1:CONV_K, :]
    for j in range(CONV_K - 1):
        pre = pre + _shift_down(x, CONV_K - 1 - j) * w[j:j + 1, :]
    return pre


def _gdn_conv_fwd(pm, conv_w, name):
    s = pm.shape[0]
    nblk = 3 * D // HEAD

    def body(x_ref, w_ref, o_ref):
        is_qk = pl.program_id(0) < 2 * HEADS
        o_ref[...] = _conv_post(_conv_pre(x_ref[...], w_ref[...]), is_qk)

    return pl.pallas_call(
        body, name=name, grid=(nblk,),
        in_specs=[pl.BlockSpec((s, HEAD), lambda c: (0, c)), pl.BlockSpec((CONV_K, HEAD), lambda c: (0, c))],
        out_specs=pl.BlockSpec((s, HEAD), lambda c: (0, c)),
        out_shape=jax.ShapeDtypeStruct((s, 3 * D), F32), compiler_params=_params(("parallel",)),
    )(pm, conv_w)


def _gdn_conv_bwd(pm, conv_w, dout, part, name):
    s = pm.shape[0]
    off = part * HEADS

    def body(x_ref, w_ref, d_ref, dx_ref, dw_ref):
        x, w = x_ref[...], w_ref[...]
        _, vjp = jax.vjp(lambda p: _conv_post(p, part < 2), _conv_pre(x, w))
        dpre, = vjp(d_ref[...])
        dx = dpre * w[CONV_K - 1:CONV_K, :]
        rows = [None] * CONV_K
        rows[CONV_K - 1] = jnp.sum(dpre * x, axis=0, keepdims=True)
        for j in range(CONV_K - 1):
            dx = dx + _shift_up(dpre, CONV_K - 1 - j) * w[j:j + 1, :]
            rows[j] = jnp.sum(dpre * _shift_down(x, CONV_K - 1 - j), axis=0, keepdims=True)
        dx_ref[...] = dx
        dw_ref[...] = jnp.concatenate(rows, axis=0)

    return pl.pallas_call(
        body, name=name, grid=(HEADS,),
        in_specs=[pl.BlockSpec((s, HEAD), lambda c: (0, c + off)), pl.BlockSpec((CONV_K, HEAD), lambda c: (0, c + off)),
                  pl.BlockSpec((s, HEAD), lambda c: (0, c))],
        out_specs=[pl.BlockSpec((s, HEAD), lambda c: (0, c)), pl.BlockSpec((CONV_K, HEAD), lambda c: (0, c))],
        out_shape=[jax.ShapeDtypeStruct((s, D), F32), jax.ShapeDtypeStruct((CONV_K, D), F32)],
        compiler_params=_params(("parallel",)),
    )(pm, conv_w, dout)


def _make_dot(hi):
    def raw(a, b, dims):
        if hi:
            return lax.dot_general(a, b, dims, preferred_element_type=F32, precision=lax.Precision.HIGHEST)
        return lax.dot_general(a.astype(BF16), b.astype(BF16), dims, preferred_element_type=F32)

    @functools.partial(jax.custom_vjp, nondiff_argnums=(2,))
    def dot(a, b, form):
        return raw(a, b, _DIMS[form])

    def fwd(a, b, form):
        return raw(a, b, _DIMS[form]), (a, b)

    def bwd(form, res, ct):
        a, b = res
        if form == "nn":
            return raw(ct, b, _NT), raw(a, ct, _TN)
        if form == "nt":
            return raw(ct, b, _NN), raw(ct, a, _TN)
        return raw(b, ct, _NT), raw(a, ct, _NN)

    dot.defvjp(fwd, bwd)
    return dot


_dot = _make_dot(False)
_dot_hi = _make_dot(True)


def _tri_inv_raw(low):
    n = low.shape[0]
    i = lax.broadcasted_iota(jnp.int32, (n, n), 0)
    j = lax.broadcasted_iota(jnp.int32, (n, n), 1)
    eye = (i == j).astype(F32)
    hdot = functools.partial(jnp.dot, preferred_element_type=F32, precision=lax.Precision.HIGHEST)
    same16 = (i // 16) == (j // 16)
    neg = jnp.where(same16, -low, 0.0)
    inv = eye + neg
    power = neg
    for _ in range(3):
        power = hdot(power, power)
        inv = hdot(inv, eye + power)
    for blk in (32, 64):
        off = jnp.where(((i // blk) == (j // blk)) & ((i // (blk // 2)) != (j // (blk // 2))), low, 0.0)
        inv = inv - hdot(inv, hdot(off, inv))
    return inv


@jax.custom_vjp
def _tri_inv(low):
    return _tri_inv_raw(low)


def _tri_inv_fwd(low):
    inv = _tri_inv_raw(low)
    return inv, inv


def _tri_inv_bwd(inv, ct):
    hdot = functools.partial(lax.dot_general, preferred_element_type=F32, precision=lax.Precision.HIGHEST)
    return (-hdot(hdot(inv, ct, _TN), inv, _NT),)


_tri_inv.defvjp(_tri_inv_fwd, _tri_inv_bwd)


def _gdn_chunk(q, k, v, beta, gc, gr, state):
    n = q.shape[0]
    i = lax.broadcasted_iota(jnp.int32, (n, n), 0)
    j = lax.broadcasted_iota(jnp.int32, (n, n), 1)
    incl, strict = i >= j, i > j
    qs = q * (HEAD ** -0.5)
    decay = jnp.where(incl, jnp.exp(jnp.where(incl, gc - gr, 0.0)), 0.0)
    kb = k * beta
    low = jnp.where(strict, _dot(kb, k, "nt") * decay, 0.0)
    inv = _tri_inv(low)
    u = _dot_hi(inv, v * beta, "nn")
    w = _dot_hi(inv, kb * jnp.exp(gc), "nn")
    attn = jnp.where(incl, _dot(qs, k, "nt") * decay, 0.0)
    v_new = u - _dot(w, state, "nn")
    o = _dot(qs * jnp.exp(gc), state, "nn") + _dot(attn, v_new, "nn")
    last = lax.broadcasted_iota(jnp.int32, gc.shape, 0) == n - 1
    g_last = jnp.sum(jnp.where(last, gc, 0.0), axis=0, keepdims=True)
    k_dec = k * jnp.exp(g_last - gc)
    return o, state * jnp.exp(g_last) + _dot(k_dec, v_new, "tn")


def _chunk_decay(g_blk, gt_blk):
    n = CHUNK
    i = lax.broadcasted_iota(jnp.int32, (n, n), 0)
    j = lax.broadcasted_iota(jnp.int32, (n, n), 1)
    hdot = functools.partial(jnp.dot, preferred_element_type=F32, precision=lax.Precision.HIGHEST)
    return hdot((i >= j).astype(F32), g_blk), hdot(gt_blk, (i <= j).astype(F32))


def _gdn_specs(s):
    nc = s // CHUNK
    return nc, [
        pl.BlockSpec((CHUNK, D), lambda n: (n, 0)), pl.BlockSpec((CHUNK, D), lambda n: (n, 1)),
        pl.BlockSpec((CHUNK, D), lambda n: (n, 2)), pl.BlockSpec((CHUNK, HEAD), lambda n: (n, 0)),
        pl.BlockSpec((CHUNK, HEAD), lambda n: (n, 0)), pl.BlockSpec((None, HEADS, CHUNK), lambda n: (n, 0, 0))]


def _gdn_scan_fwd(qkv, g, beta, gt, name):
    s = qkv.shape[0]
    nc, in_specs = _gdn_specs(s)

    def body(q_ref, k_ref, v_ref, g_ref, b_ref, gt_ref, o_ref, st_ref, state):
        @pl.when(pl.program_id(0) == 0)
        def _():
            state[...] = jnp.zeros_like(state)

        gcum, gcum_t = _chunk_decay(g_ref[...], gt_ref[...])
        beta_blk = b_ref[...]
        st_ref[...] = state[...]
        for h in range(HEADS):
            cs = slice(h * HEAD, (h + 1) * HEAD)
            o, new = _gdn_chunk(q_ref[:, cs], k_ref[:, cs], v_ref[:, cs], beta_blk[:, h:h + 1], gcum[:, h:h + 1],
                                gcum_t[h:h + 1, :], state[h])
            o_ref[:, cs] = o
            state[h] = new

    return pl.pallas_call(
        body, name=name, grid=(nc,), in_specs=in_specs,
        out_specs=[pl.BlockSpec((CHUNK, D), lambda n: (n, 0)),
                   pl.BlockSpec((None, HEADS, HEAD, HEAD), lambda n: (n, 0, 0, 0))],
        out_shape=[jax.ShapeDtypeStruct((s, D), F32), jax.ShapeDtypeStruct((nc, HEADS, HEAD, HEAD), F32)],
        scratch_shapes=[pltpu.VMEM((HEADS, HEAD, HEAD), F32)],
        compiler_params=_params(("arbitrary",)),
    )(qkv, qkv, qkv, g, beta, gt)


def _gdn_scan_bwd(qkv, g, beta, gt, states, do, name):
    s = qkv.shape[0]
    nc, in_specs = _gdn_specs(s)
    rev = lambda spec: pl.BlockSpec(spec.block_shape, lambda n, f=spec.index_map: f(nc - 1 - n))
    in_specs = [rev(sp) for sp in in_specs]
    in_specs += [pl.BlockSpec((None, HEADS, HEAD, HEAD), lambda n: (nc - 1 - n, 0, 0, 0)),
                 pl.BlockSpec((CHUNK, D), lambda n: (nc - 1 - n, 0))]

    def body(q_ref, k_ref, v_ref, g_ref, b_ref, gt_ref, st_ref, do_ref, dq_ref, dk_ref, dv_ref, dg_ref, db_ref,
             dgt_ref, dstate):
        @pl.when(pl.program_id(0) == 0)
        def _():
            dstate[...] = jnp.zeros_like(dstate)

        gcum, gcum_t = _chunk_decay(g_ref[...], gt_ref[...])
        beta_blk = b_ref[...]
        lane = lax.broadcasted_iota(jnp.int32, (CHUNK, HEAD), 1)
        sub = lax.broadcasted_iota(jnp.int32, (HEADS, CHUNK), 0)
        dgc_all = jnp.zeros((CHUNK, HEAD), F32)
        db_all = jnp.zeros((CHUNK, HEAD), F32)
        dgr_all = jnp.zeros((HEADS, CHUNK), F32)
        for h in range(HEADS):
            cs = slice(h * HEAD, (h + 1) * HEAD)
            _, vjp = jax.vjp(_gdn_chunk, q_ref[:, cs], k_ref[:, cs], v_ref[:, cs], beta_blk[:, h:h + 1],
                             gcum[:, h:h + 1], gcum_t[h:h + 1, :], st_ref[h])
            dq, dk, dv, db, dgc, dgr, dst = vjp((do_ref[:, cs], dstate[h]))
            dq_ref[:, cs] = dq
            dk_ref[:, cs] = dk
            dv_ref[:, cs] = dv
            dstate[h] = dst
            db_all = jnp.where(lane == h, db, db_all)
            dgc_all = jnp.where(lane == h, dgc, dgc_all)
            dgr_all = jnp.where(sub == h, dgr, dgr_all)
        i = lax.broadcasted_iota(jnp.int32, (CHUNK, CHUNK), 0)
        j = lax.broadcasted_iota(jnp.int32, (CHUNK, CHUNK), 1)
        hdot = functools.partial(jnp.dot, preferred_element_type=F32, precision=lax.Precision.HIGHEST)
        dg_ref[...] = hdot((i <= j).astype(F32), dgc_all)
        dgt_ref[...] = hdot(dgr_all, (i >= j).astype(F32))
        db_ref[...] = db_all

    blk = pl.BlockSpec((CHUNK, D), lambda n: (nc - 1 - n, 0))
    gblk = pl.BlockSpec((CHUNK, HEAD), lambda n: (nc - 1 - n, 0))
    return pl.pallas_call(
        body, name=name, grid=(nc,), in_specs=in_specs,
        out_specs=[blk, blk, blk, gblk, gblk, pl.BlockSpec((None, HEADS, CHUNK), lambda n: (nc - 1 - n, 0, 0))],
        out_shape=[jax.ShapeDtypeStruct((s, D), F32)] * 3 + [jax.ShapeDtypeStruct((s, HEAD), F32)] * 2
        + [jax.ShapeDtypeStruct((nc, HEADS, CHUNK), F32)],
        scratch_shapes=[pltpu.VMEM((HEADS, HEAD, HEAD), F32)],
        compiler_params=_params(("arbitrary",)),
    )(qkv, qkv, qkv, g, beta, gt, states, do)


ATT_TILE = 256
ATT_SCALE = QK_HEAD ** -0.5


def _att_mask(i_blk, j_blk):
    qpos = i_blk * ATT_TILE + lax.broadcasted_iota(jnp.int32, (ATT_TILE, ATT_TILE), 0)
    kpos = j_blk * ATT_TILE + lax.broadcasted_iota(jnp.int32, (ATT_TILE, ATT_TILE), 1)
    return (kpos // CHUNK) <= (qpos // CHUNK)


def _attn_fwd(q, k, v, name):
    s = q.shape[0]
    nb = s // ATT_TILE
    t = ATT_TILE

    def body(q_ref, k_ref, v_ref, o_ref, lse_ref, m_s, l_s, acc):
        i, j = pl.program_id(1), pl.program_id(2)

        @pl.when(j == 0)
        def _():
            m_s[...] = jnp.full_like(m_s, -jnp.inf)
            l_s[...] = jnp.zeros_like(l_s)
            acc[...] = jnp.zeros_like(acc)

        @pl.when(j <= i)
        def _():
            sc = lax.dot_general(q_ref[...], k_ref[...], _NT, preferred_element_type=F32) * ATT_SCALE
            sc = jnp.where(_att_mask(i, j), sc, -jnp.inf)
            m_new = jnp.maximum(m_s[...], jnp.max(sc, axis=-1, keepdims=True))
            alpha = jnp.exp(m_s[...] - m_new)
            p = jnp.exp(sc - m_new)
            l_s[...] = alpha * l_s[...] + jnp.sum(p, axis=-1, keepdims=True)
            acc[...] = alpha * acc[...] + jnp.dot(p.astype(BF16), v_ref[...], preferred_element_type=F32)
            m_s[...] = m_new

        @pl.when(j == nb - 1)
        def _():
            o_ref[...] = acc[...] / l_s[...]
            lse_ref[...] = m_s[...] + jnp.log(l_s[...])

    return pl.pallas_call(
        body, name=name, grid=(HEADS, nb, nb),
        in_specs=[pl.BlockSpec((t, HEAD_PAD), lambda h, i, j: (i, h)),
                  pl.BlockSpec((t, HEAD_PAD), lambda h, i, j: (jnp.minimum(j, i), h)),
                  pl.BlockSpec((t, HEAD), lambda h, i, j: (jnp.minimum(j, i), h))],
        out_specs=[pl.BlockSpec((t, HEAD), lambda h, i, j: (i, h)),
                   pl.BlockSpec((None, t, 1), lambda h, i, j: (h, i, 0))],
        out_shape=[jax.ShapeDtypeStruct((s, HEADS * HEAD), F32), jax.ShapeDtypeStruct((HEADS, s, 1), F32)],
        scratch_shapes=[pltpu.VMEM((t, 1), F32), pltpu.VMEM((t, 1), F32), pltpu.VMEM((t, HEAD), F32)],
        compiler_params=_params(("parallel", "parallel", "arbitrary")),
    )(q, k, v)


def _attn_bwd_dq(q, k, v, do, o, lse, name):
    s = q.shape[0]
    nb = s // ATT_TILE
    t = ATT_TILE

    def body(q_ref, k_ref, v_ref, do_ref, o_ref, lse_ref, dq_ref, dl_ref, acc, dl_s):
        i, j = pl.program_id(1), pl.program_id(2)

        @pl.when(j == 0)
        def _():
            acc[...] = jnp.zeros_like(acc)
            dl_s[...] = jnp.sum(do_ref[...] * o_ref[...], axis=-1, keepdims=True)

        @pl.when(j <= i)
        def _():
            sc = lax.dot_general(q_ref[...], k_ref[...], _NT, preferred_element_type=F32) * ATT_SCALE
            p = jnp.where(_att_mask(i, j), jnp.exp(sc - lse_ref[...]), 0.0)
            dp = lax.dot_general(do_ref[...].astype(BF16), v_ref[...], _NT, preferred_element_type=F32)
            ds = p * (dp - dl_s[...]) * ATT_SCALE
            acc[...] += jnp.dot(ds.astype(BF16), k_ref[...], preferred_element_type=F32)

        @pl.when(j == nb - 1)
        def _():
            dq_ref[...] = acc[...]
            dl_ref[...] = dl_s[...]

    return pl.pallas_call(
        body, name=name, grid=(HEADS, nb, nb),
        in_specs=[pl.BlockSpec((t, HEAD_PAD), lambda h, i, j: (i, h)),
                  pl.BlockSpec((t, HEAD_PAD), lambda h, i, j: (jnp.minimum(j, i), h)),
                  pl.BlockSpec((t, HEAD), lambda h, i, j: (jnp.minimum(j, i), h)),
                  pl.BlockSpec((t, HEAD), lambda h, i, j: (i, h)),
                  pl.BlockSpec((t, HEAD), lambda h, i, j: (i, h)),
                  pl.BlockSpec((None, t, 1), lambda h, i, j: (h, i, 0))],
        out_specs=[pl.BlockSpec((t, HEAD_PAD), lambda h, i, j: (i, h)),
                   pl.BlockSpec((None, t, 1), lambda h, i, j: (h, i, 0))],
        out_shape=[jax.ShapeDtypeStruct((s, HEADS * HEAD_PAD), F32), jax.ShapeDtypeStruct((HEADS, s, 1), F32)],
        scratch_shapes=[pltpu.VMEM((t, HEAD_PAD), F32), pltpu.VMEM((t, 1), F32)],
        compiler_params=_params(("parallel", "parallel", "arbitrary")),
    )(q, k, v, do, o, lse)


def _attn_bwd_dkv(q, k, v, do, lse, delta, name):
    s = q.shape[0]
    nb = s // ATT_TILE
    t = ATT_TILE

    def body(q_ref, k_ref, v_ref, do_ref, lse_ref, dl_ref, dk_ref, dv_ref, dk_acc, dv_acc):
        j, i = pl.program_id(1), pl.program_id(2)

        @pl.when(i == 0)
        def _():
            dk_acc[...] = jnp.zeros_like(dk_acc)
            dv_acc[...] = jnp.zeros_like(dv_acc)

        @pl.when(i >= j)
        def _():
            sc = lax.dot_general(q_ref[...], k_ref[...], _NT, preferred_element_type=F32) * ATT_SCALE
            p = jnp.where(_att_mask(i, j), jnp.exp(sc - lse_ref[...]), 0.0)
            dob = do_ref[...].astype(BF16)
            dv_acc[...] += lax.dot_general(p.astype(BF16), dob, _TN, preferred_element_type=F32)
            dp = lax.dot_general(dob, v_ref[...], _NT, preferred_element_type=F32)
            ds = p * (dp - dl_ref[...]) * ATT_SCALE
            dk_acc[...] += lax.dot_general(ds.astype(BF16), q_ref[...], _TN, preferred_element_type=F32)

        @pl.when(i == nb - 1)
        def _():
            dk_ref[...] = dk_acc[...]
            dv_ref[...] = dv_acc[...]

    qi = lambda h, j, i: jnp.maximum(i, j)
    return pl.pallas_call(
        body, name=name, grid=(HEADS, nb, nb),
        in_specs=[pl.BlockSpec((t, HEAD_PAD), lambda h, j, i: (qi(h, j, i), h)),
                  pl.BlockSpec((t, HEAD_PAD), lambda h, j, i: (j, h)),
                  pl.BlockSpec((t, HEAD), lambda h, j, i: (j, h)),
                  pl.BlockSpec((t, HEAD), lambda h, j, i: (qi(h, j, i), h)),
                  pl.BlockSpec((None, t, 1), lambda h, j, i: (h, qi(h, j, i), 0)),
                  pl.BlockSpec((None, t, 1), lambda h, j, i: (h, qi(h, j, i), 0))],
        out_specs=[pl.BlockSpec((t, HEAD_PAD), lambda h, j, i: (j, h)),
                   pl.BlockSpec((t, HEAD), lambda h, j, i: (j, h))],
        out_shape=[jax.ShapeDtypeStruct((s, HEADS * HEAD_PAD), F32), jax.ShapeDtypeStruct((s, HEADS * HEAD), F32)],
        scratch_shapes=[pltpu.VMEM((t, HEAD_PAD), F32), pltpu.VMEM((t, HEAD), F32)],
        compiler_params=_params(("parallel", "parallel", "arbitrary")),
    )(q, k, v, do, lse, delta)


def _rope_tables(positions):
    half = ROPE // 2
    inv_freq = ROPE_BASE ** (-jnp.arange(half, dtype=F32) / half)
    ang = positions.astype(F32)[:, None] * inv_freq
    cos, sin = jnp.cos(ang), jnp.sin(ang)
    return jnp.concatenate([cos] * 4, axis=1), jnp.concatenate([-sin, sin] * 2, axis=1)


def _loss_and_grad(y, target, name):
    s = y.shape[0]
    ts = min(ROW_TILE, s)

    def body(y_ref, t_ref, dy_ref, l_ref):
        e = y_ref[...] - t_ref[...]
        dy_ref[...] = e * (1.0 / D)
        part = jnp.sum(jnp.sum(e * e, axis=-1, keepdims=True) * (0.5 / D), axis=0, keepdims=True)
        part = part * jnp.ones((1, 128), F32)

        @pl.when(pl.program_id(0) == 0)
        def _():
            l_ref[...] = part

        @pl.when(pl.program_id(0) > 0)
        def _():
            l_ref[...] += part

    return pl.pallas_call(
        body, name=name, grid=(s // ts,),
        in_specs=[pl.BlockSpec((ts, D), lambda i: (i, 0))] * 2,
        out_specs=[pl.BlockSpec((ts, D), lambda i: (i, 0)), pl.BlockSpec((1, 128), lambda i: (0, 0))],
        out_shape=[jax.ShapeDtypeStruct((s, D), F32), jax.ShapeDtypeStruct((1, 128), F32)],
        compiler_params=_params(("arbitrary",)),
    )(y, target)


ANY = pl.BlockSpec(memory_space=pl.ANY)


def _all_gather(shard, name):
    def body(x_ref, out_ref, send_sems, recv_sems, local_sem):
        x, y, c = lax.axis_index("x"), lax.axis_index("y"), lax.axis_index("c")
        me, sibling = (x, y, c), (x, y, 1 - c)
        chips = [(1 - x, y), (x, 1 - y), (1 - x, 1 - y)]

        def rows(px, py, pc):
            return out_ref.at[4 * px + 2 * py + pc]

        def copy(k, block, to, src=None):
            return pltpu.make_async_remote_copy(
                src_ref=rows(*block) if src is None else src, dst_ref=rows(*block),
                send_sem=send_sems.at[k], recv_sem=recv_sems.at[k], device_id=to, device_id_type=MESH)

        mine = pltpu.make_async_copy(x_ref, rows(*me), local_sem)
        mine.start()
        first = [copy(0, me, sibling, src=x_ref)]
        first += [copy(1 + j, me, (*chip, c), src=x_ref) for j, chip in enumerate(chips)]
        for cp in first:
            cp.start()
        passed = [copy(4 + j, (*chip, c), sibling) for j, chip in enumerate(chips)]
        for j, chip in enumerate(chips):
            copy(1 + j, (*chip, c), me).wait_recv()
            passed[j].start()
        copy(0, sibling, me).wait_recv()
        for j, chip in enumerate(chips):
            copy(4 + j, (*chip, 1 - c), me).wait_recv()
        for cp in first + passed:
            cp.wait_send()
        mine.wait()

    return pl.pallas_call(
        body, name=name, out_shape=jax.ShapeDtypeStruct((N_DEV,) + shard.shape, shard.dtype),
        in_specs=[ANY], out_specs=ANY,
        scratch_shapes=[pltpu.SemaphoreType.DMA((7,)), pltpu.SemaphoreType.DMA((7,)), pltpu.SemaphoreType.DMA],
    )(shard)


def _exchange(blocks, name):
    def body(x_ref, out_ref, send_sems, recv_sems, local_sem):
        x, y, c = lax.axis_index("x"), lax.axis_index("y"), lax.axis_index("c")
        me = 4 * x + 2 * y + c
        mine = pltpu.make_async_copy(x_ref.at[me], out_ref.at[me], local_sem)
        mine.start()
        copies = []
        for k in range(1, N_DEV):
            px = 1 - x if k & 4 else x
            py = 1 - y if k & 2 else y
            pc = 1 - c if k & 1 else c
            peer = 4 * px + 2 * py + pc
            cp = pltpu.make_async_remote_copy(
                src_ref=x_ref.at[peer], dst_ref=out_ref.at[me], send_sem=send_sems.at[k - 1],
                recv_sem=recv_sems.at[k - 1], device_id=(px, py, pc), device_id_type=MESH)
            cp.start()
            copies.append((cp, pltpu.make_async_remote_copy(
                src_ref=x_ref.at[peer], dst_ref=out_ref.at[peer], send_sem=send_sems.at[k - 1],
                recv_sem=recv_sems.at[k - 1], device_id=(px, py, pc), device_id_type=MESH)))
        for cp, landing in copies:
            landing.wait_recv()
        for cp, landing in copies:
            cp.wait_send()
        mine.wait()

    return pl.pallas_call(
        body, name=name, out_shape=jax.ShapeDtypeStruct(blocks.shape, blocks.dtype),
        in_specs=[ANY], out_specs=ANY,
        scratch_shapes=[pltpu.SemaphoreType.DMA((7,)), pltpu.SemaphoreType.DMA((7,)), pltpu.SemaphoreType.DMA],
    )(blocks)


def _adamw(parts, w, m, v, name, tr=128):
    n, r, wd = parts.shape
    tr = tr if r % tr == 0 else r

    def body(p_ref, w_ref, m_ref, v_ref, g_ref, d_ref, nm_ref, nv_ref):
        g = p_ref[0]
        for k in range(1, n):
            g = g + p_ref[k]
        m_new = B1 * m_ref[...] + (1.0 - B1) * g
        v_new = B2 * v_ref[...] + (1.0 - B2) * (g * g)
        m_hat = m_new / (1.0 - B1 ** STEP)
        v_hat = v_new / (1.0 - B2 ** STEP)
        g_ref[...] = g
        d_ref[...] = -LR * (m_hat / (jnp.sqrt(v_hat) + ADAM_EPS) + WD * w_ref[...])
        nm_ref[...] = m_new
        nv_ref[...] = v_new

    blk = pl.BlockSpec((tr, wd), lambda i: (i, 0))
    return pl.pallas_call(
        body, name=name, grid=(r // tr,),
        in_specs=[pl.BlockSpec((n, tr, wd), lambda i: (0, i, 0)), blk, blk, blk],
        out_specs=[blk] * 4, out_shape=[jax.ShapeDtypeStruct((r, wd), F32)] * 4,
        compiler_params=_params(("parallel",)),
    )(parts, w, m, v)


def _outer8(ct, dm, name):
    k, n = ct.shape[0], dm.shape[1]

    def body(c_ref, d_ref, o_ref):
        cv, dv = c_ref[...], d_ref[...]
        acc = cv[:, 0:1] * dv[0:1, :]
        for s in range(1, N_DEV):
            acc = acc + cv[:, s:s + 1] * dv[s:s + 1, :]
        o_ref[...] = acc

    tk = 256
    return pl.pallas_call(
        body, name=name, grid=(k // tk,),
        in_specs=[pl.BlockSpec((tk, N_DEV), lambda i: (i, 0)), pl.BlockSpec((N_DEV, n), lambda i: (0, 0))],
        out_specs=pl.BlockSpec((tk, n), lambda i: (i, 0)), out_shape=jax.ShapeDtypeStruct((k, n), F32),
        compiler_params=_params(("parallel",)),
    )(ct, dm)


FULL = (0, D)
C128 = (0, 128)
HEAD_NOPE = [(h * HEAD_PAD, NOPE) for h in range(HEADS)]
HEAD_ROPE = [(h * HEAD_PAD + NOPE, 128) for h in range(HEADS)]
HEAD_ALL = [(h * HEAD_PAD, HEAD_PAD) for h in range(HEADS)]
HEAD_V = [(h * HEAD, HEAD) for h in range(HEADS)]


def _modulate(x, gain, scale, shift):
    return _rowwise_fwd(_modulate_fn, [(x, FULL)], [gain, scale, shift], [(D, BF16, FULL)], "modulate")[0]


def _modulate_bwd(x, gain, scale, shift, dh, dx_in):
    return _rowwise_bwd(_modulate_fn, [(x, FULL)], [gain, scale, shift], [(D, BF16, FULL)], [dh], [(0, FULL)],
                        [(D, F32)], "modulate_bwd", add={0: dx_in})


def _residual(x, y, gm):
    return _rowwise_fwd(_resgate_fn, [(x, FULL), (y, FULL)], [gm], [(D, F32, FULL)], "residual")[0]


def _residual_bwd(y, gm, dxn):
    return _rowwise_bwd(_gate_only_fn, [(y, FULL)], [gm], [(D, F32, FULL)], [dxn], [(0, FULL)], [(D, BF16)],
                        "residual_bwd")


def _ffn_fwd(x, p):
    h = _modulate(x, p["gain"], p["scale"], p["shift"])
    gate, up, act = _ffn_in(h, p["wg"], p["wu"], "ffn_in")
    y = _mm(act, p["wo"], "nn", "ffn_out")
    return _residual(x, y, p["gm"]), dict(x=x, h=h, gate=gate, up=up, act=act, y=y)


def _ffn_bwd(t, p, dxn):
    dy, dgm = _residual_bwd(t["y"], p["gm"], dxn)
    dgate, dup = _ffn_bwd_act(dy, p["wo"], t["gate"], t["up"], "ffn_bwd_act")
    dwo = _mm(t["act"], dy, "tn", "ffn_dwo")
    dh = _matmul([(dgate, p["wg"]), (dup, p["wu"])], "nt", "ffn_dh")
    dwg = _mm(t["h"], dgate, "tn", "ffn_dwi")
    dwu = _mm(t["h"], dup, "tn", "ffn_dwi")
    dx, dgain, dscale, dshift = _modulate_bwd(t["x"], p["gain"], p["scale"], p["shift"], dh, dxn)
    return dx, dict(gain=dgain, scale=dscale, shift=dshift, gm=dgm, wg=dwg, wu=dwu, wo=dwo)


def _pad128(t):
    return jnp.pad(t, ((0, 0), (0, 128 - t.shape[1])))


def _gdn_fwd(x, p):
    s = x.shape[0]
    h = _modulate(x, p["gain"], p["scale"], p["shift"])
    pm = _mm(h, p["w_main"], "nn", "gdn_proj")
    tail = _mm(h, p["w_tail"], "nn", "gdn_proj_tail")
    qkv = _gdn_conv_fwd(pm, p["conv_w"], "gdn_conv")
    beta, g = _rowwise_fwd(_gdn_gates_fn, [(tail, C128), (tail, (128, 128))], [p["a_log"], p["dt_bias"]],
                           [(128, F32, C128)] * 2, "gdn_gates")
    gt = g[:, :HEADS].reshape(s // CHUNK, CHUNK, HEADS).transpose(0, 2, 1)
    o, states = _gdn_scan_fwd(qkv, g, beta, gt, "gdn_scan")
    on, = _rowwise_fwd(_gdn_outnorm_fn, [(o, HEAD_V), (pm, [(3 * D + h_ * HEAD, HEAD) for h_ in range(HEADS)])],
                       [p["norm_g"]], [(D, BF16, HEAD_V)], "gdn_outnorm", groups=HEADS)
    y = _mm(on, p["w_out"], "nn", "mix_out")
    t = dict(x=x, h=h, pm=pm, tail=tail, qkv=qkv, beta=beta, g=g, gt=gt, o=o, states=states, on=on, y=y)
    return _residual(x, y, p["gm"]), t


def _gdn_bwd(t, p, dxn):
    s = dxn.shape[0]
    zc = [(3 * D + h_ * HEAD, HEAD) for h_ in range(HEADS)]
    dy, dgm = _residual_bwd(t["y"], p["gm"], dxn)
    dw_out = _mm(t["on"], dy, "tn", "mix_dwo")
    don = _mm(dy, p["w_out"], "nt", "mix_dout")
    do, dz, dnorm_g = _rowwise_bwd(_gdn_outnorm_fn, [(t["o"], HEAD_V), (t["pm"], zc)], [p["norm_g"]],
                                   [(D, BF16, HEAD_V)], [don], [(0, HEAD_V), (1, HEAD_V)], [(D, F32), (D, F32)],
                                   "gdn_outnorm_bwd", groups=HEADS)
    dq, dk, dv, dg, dbeta, dgt = _gdn_scan_bwd(t["qkv"], t["g"], t["beta"], t["gt"], t["states"], do,
                                               "gdn_scan_bwd")
    dg = dg + _pad128(dgt.transpose(0, 2, 1).reshape(s, HEADS))
    dtail, da_log, ddt = _rowwise_bwd(_gdn_gates_fn, [(t["tail"], C128), (t["tail"], (128, 128))],
                                      [p["a_log"], p["dt_bias"]], [(128, F32, C128)] * 2, [dbeta, dg],
                                      [(0, C128), (0, (128, 128))], [(256, F32)], "gdn_gates_bwd")
    dxs, dcw = [], []
    for part, d in enumerate((dq, dk, dv)):
        dx_, dw_ = _gdn_conv_bwd(t["pm"], p["conv_w"], d, part, "gdn_conv_bwd")
        dxs.append(dx_)
        dcw.append(dw_)
    pieces = dxs + [dz]
    dh = _matmul([(d, p["w_main"]) for d in pieces] + [(dtail, p["w_tail"])], "nt", "gdn_dh",
                 boffs=[0, D, 2 * D, 3 * D, 0], tk=256)
    dw_main = [_mm(t["h"], d, "tn", "gdn_dwi") for d in pieces]
    dw_tail = _mm(t["h"], dtail, "tn", "gdn_dwi_tail")
    dx, dgain, dscale, dshift = _modulate_bwd(t["x"], p["gain"], p["scale"], p["shift"], dh, dxn)
    return dx, dict(gain=dgain, scale=dscale, shift=dshift, gm=dgm, w_main=jnp.concatenate(dw_main, axis=1),
                    w_tail=dw_tail, conv_w=jnp.concatenate(dcw, axis=1), a_log=da_log, dt_bias=ddt,
                    norm_g=dnorm_g, w_out=dw_out)


def _q_rows(q2, cosf, sins):
    return [(q2, HEAD_NOPE), (q2, HEAD_ROPE), (cosf, C128), (sins, C128)]


def _mla_fwd(x, p, kv):
    h = _modulate(x, p["gain"], p["scale"], p["shift"])
    cq = _mm(h, p["w_dq"], "nn", "mla_dq")
    cqn, = _rowwise_fwd(_rms_fn, [(cq, (0, Q_LORA))], [p["q_lora_g"]], [(Q_LORA, BF16, (0, Q_LORA))], "mla_qlora_norm")
    q2 = _mm(cqn, p["w_uq"], "nn", "mla_uq")
    qn, = _rowwise_fwd(_q_norm_rope_fn, _q_rows(q2, kv["cosf"], kv["sins"]), [p["q_gn"], p["q_gr"]],
                       [(HEADS * HEAD_PAD, BF16, HEAD_ALL)], "mla_q_norm", groups=HEADS)
    o, lse = _attn_fwd(qn, kv["kn"], kv["vb"], "mla_attn")
    y = _mm(o, p["w_out"], "nn", "mix_out")
    return _residual(x, y, p["gm"]), dict(x=x, h=h, cq=cq, cqn=cqn, q2=q2, qn=qn, o=o, lse=lse, y=y)


def _mla_bwd(t, p, kv, dxn):
    dy, dgm = _residual_bwd(t["y"], p["gm"], dxn)
    dw_out = _mm(t["o"], dy, "tn", "mix_dwo")
    do = _mm(dy, p["w_out"], "nt", "mix_dout")
    dq, delta = _attn_bwd_dq(t["qn"], kv["kn"], kv["vb"], do, t["o"], t["lse"], "mla_attn_dq")
    dk, dv = _attn_bwd_dkv(t["qn"], kv["kn"], kv["vb"], do, t["lse"], delta, "mla_attn_dkv")
    dq2, dq_gn, dq_gr = _rowwise_bwd(_q_norm_rope_fn, _q_rows(t["q2"], kv["cosf"], kv["sins"]), [p["q_gn"], p["q_gr"]],
                                     [(HEADS * HEAD_PAD, BF16, HEAD_ALL)], [dq],
                                     [(0, HEAD_NOPE), (0, HEAD_ROPE), None, None], [(HEADS * HEAD_PAD, F32)],
                                     "mla_q_norm_bwd", groups=HEADS)
    dw_uq = _mm(t["cqn"], dq2, "tn", "mla_dwuq")
    dcqn = _mm(dq2, p["w_uq"], "nt", "mla_dcq")
    dcq, dq_lora_g = _rowwise_bwd(_rms_fn, [(t["cq"], (0, Q_LORA))], [p["q_lora_g"]], [(Q_LORA, BF16, (0, Q_LORA))],
                                  [dcqn], [(0, (0, Q_LORA))], [(Q_LORA, F32)], "mla_qlora_norm_bwd")
    dw_dq = _mm(t["h"], dcq, "tn", "mla_dwdq")
    dh = _mm(dcq, p["w_dq"], "nt", "mla_dh")
    dx, dgain, dscale, dshift = _modulate_bwd(t["x"], p["gain"], p["scale"], p["shift"], dh, dxn)
    grads = dict(gain=dgain, scale=dscale, shift=dshift, gm=dgm, w_dq=dw_dq, q_lora_g=dq_lora_g, w_uq=dw_uq,
                 q_gn=dq_gn, q_gr=dq_gr, w_out=dw_out)
    return dx, grads, dk, dv


def _k_rows(kvp, ckv, cosf, sins):
    return [(kvp, HEAD_NOPE), (kvp, HEAD_ROPE), (ckv, (KV_LORA, 128)), (cosf, C128), (sins, C128)]


def _kv_fwd(x, p, cosf, sins):
    h = _modulate(x, p["gain"], p["scale"], p["shift"])
    ckv = _mm(h, p["w_dkv"], "nn", "kv_down")
    lat, = _rowwise_fwd(_rms_fn, [(ckv, (0, KV_LORA))], [p["kv_g"]], [(KV_LORA, BF16, (0, KV_LORA))], "kv_norm")
    kvp = _mm(lat, p["w_ukv"], "nn", "kv_up")
    kn, vb = _rowwise_fwd(_k_norm_rope_fn, _k_rows(kvp, ckv, cosf, sins), [p["k_gn"], p["k_gr"]],
                          [(HEADS * HEAD_PAD, BF16, HEAD_ALL), (HEADS * HEAD, BF16, HEAD_V)], "kv_k_norm",
                          groups=HEADS)
    return dict(x=x, h=h, ckv=ckv, lat=lat, kvp=kvp, kn=kn, vb=vb, cosf=cosf, sins=sins)


def _kv_bwd(t, p, dk, dv, dx_in):
    dkvp, drope, dk_gn, dk_gr = _rowwise_bwd(
        _k_norm_rope_fn, _k_rows(t["kvp"], t["ckv"], t["cosf"], t["sins"]), [p["k_gn"], p["k_gr"]],
        [(HEADS * HEAD_PAD, BF16, HEAD_ALL), (HEADS * HEAD, BF16, HEAD_V)], [dk, dv],
        [(0, HEAD_NOPE), (0, HEAD_ROPE), (1, C128), None, None], [(HEADS * HEAD_PAD, F32), (128, F32)],
        "kv_k_norm_bwd", groups=HEADS)
    dw_ukv = _mm(t["lat"], dkvp, "tn", "kv_dwukv")
    dlat = _mm(dkvp, p["w_ukv"], "nt", "kv_dlat")
    dckv, dkv_g = _rowwise_bwd(_rms_fn, [(t["ckv"], (0, KV_LORA))], [p["kv_g"]], [(KV_LORA, BF16, (0, KV_LORA))],
                               [dlat], [(0, (0, KV_LORA))], [(KV_LORA, F32)], "kv_norm_bwd")
    dw_dkv = jnp.concatenate([_mm(t["h"], dckv, "tn", "kv_dwdkv"), _mm(t["h"], drope, "tn", "kv_dwdkv_rope")], axis=1)
    dh = _matmul([(dckv, p["w_dkv"]), (drope, p["w_dkv"])], "nt", "kv_dh", boffs=[0, KV_LORA], tk=128)
    dx, dgain, dscale, dshift = _modulate_bwd(t["x"], p["gain"], p["scale"], p["shift"], dh, dx_in)
    return dx, dict(gain=dgain, scale=dscale, shift=dshift, w_dkv=dw_dkv, kv_g=dkv_g, w_ukv=dw_ukv, k_gn=dk_gn,
                    k_gr=dk_gr)


WEIGHTS = ["ada_w", "ada_b", "norm_g", "ffn_w_in", "ffn_w_out", "gdn_w_in", "gdn_conv_w", "gdn_a_log", "gdn_dt_bias",
           "gdn_norm_g", "gdn_w_out", "kv_ada_w", "kv_ada_b", "kv_norm_g", "mla_w_dkv", "mla_kv_norm_g", "mla_w_ukv",
           "mla_k_norm_g", "mla_w_dq", "mla_q_lora_norm_g", "mla_w_uq", "mla_q_norm_g", "mla_w_out"]
EXCHANGED = {"ffn_w_in": (8192, 704), "ffn_w_out": (2816, 1024), "gdn_w_in": (2048, 514), "gdn_w_out": (256, 1024),
             "mla_w_dkv": (128, 320), "mla_w_ukv": (256, 256), "mla_w_dq": (256, 384), "mla_w_uq": (768, 192),
             "mla_w_out": (256, 1024)}
SMALL = [("ada_b", 4 * N_MOD * D), ("kv_ada_b", 2 * D), ("norm_g", DEPTH * 3 * D), ("gdn_conv_w", N_A * CONV_K * 3 * D),
         ("gdn_a_log", N_A * HEADS), ("gdn_dt_bias", N_A * HEADS), ("gdn_norm_g", N_A * HEAD), ("kv_norm_g", D),
         ("mla_kv_norm_g", KV_LORA), ("mla_k_norm_g", QK_HEAD), ("mla_q_lora_norm_g", 2 * Q_LORA),
         ("mla_q_norm_g", 2 * QK_HEAD)]
SMALL_REPLICATED = [n for n, _ in SMALL if n not in ("norm_g", "gdn_conv_w")]


def _silu_fn(g, t):
    return (_silu(t),)


def _dup_rope(t):
    return jnp.concatenate([t[..., :NOPE], t[..., NOPE:], t[..., NOPE:]], axis=-1)


def _fold_rope(t):
    return jnp.concatenate([t[..., :NOPE], t[..., NOPE:QK_HEAD] + t[..., QK_HEAD:]], axis=-1)


def _pack(pieces, rows):
    flat = jnp.concatenate([p.reshape(-1).astype(F32) for p in pieces])
    return jnp.pad(flat, (0, rows * 128 - flat.shape[0])).reshape(rows, 128)


def _step(a):
    me = 4 * lax.axis_index("x") + 2 * lax.axis_index("y") + lax.axis_index("c")
    x = a["x"][0]
    cosf, sins = _rope_tables(a["positions"][0])

    n_cw, n_ng = N_A * CONV_K * 3 * HEAD, DEPTH * 3 * HEAD
    small_all = _all_gather(_pack([a["gdn_conv_w"], a["norm_g"], a["c"]], 44), "gather_small").reshape(N_DEV, -1)
    conv_w = small_all[:, :n_cw].reshape(N_DEV, N_A, CONV_K, 3 * HEAD).transpose(1, 2, 0, 3).reshape(N_A, CONV_K, 3 * D)
    norm_g = small_all[:, n_cw:n_cw + n_ng].reshape(N_DEV, DEPTH, 3, HEAD).transpose(1, 2, 0, 3).reshape(DEPTH, 3, D)
    c_all = small_all[:, n_cw + n_ng:n_cw + n_ng + D]

    c_act, = _rowwise_fwd(_silu_fn, [(c_all, FULL)], [], [(D, F32, FULL)], "c_act")
    n_ada = N_MOD * D // N_DEV
    parts = [_mm(c_act, a["ada_w"][l], "nn", "mod_proj") for l in range(DEPTH)]
    parts.append(_mm(c_act, a["kv_ada_w"], "nn", "mod_proj_kv"))
    mod_recv = _exchange(jnp.concatenate(parts, axis=1)[:, None, :], "exchange_mod")[:, 0]
    mod = mod_recv[:, :DEPTH * n_ada].reshape(N_DEV, DEPTH, n_ada).transpose(1, 0, 2).reshape(DEPTH, N_MOD * D)
    mod = (mod + a["ada_b"]).reshape(DEPTH, N_MOD, D)
    kvmod = mod_recv[:, DEPTH * n_ada:].reshape(2 * D) + a["kv_ada_b"]

    def gather(name):
        return _all_gather(a[name].astype(BF16).reshape(EXCHANGED[name]), "gather_" + name)

    g_ffn_in = gather("ffn_w_in").reshape(N_DEV, DEPTH, 2, D, 2 * D_FF // N_DEV)
    g_ffn_out = gather("ffn_w_out").reshape(N_DEV, DEPTH, 2, D_FF // N_DEV, D)
    g_gdn_in = gather("gdn_w_in").reshape(N_DEV, N_A, D, 514)
    g_gdn_out = gather("gdn_w_out").reshape(N_DEV, N_A, D // N_DEV, D)
    g_dkv = gather("mla_w_dkv")
    g_ukv = gather("mla_w_ukv")
    g_dq = gather("mla_w_dq").reshape(N_DEV, 2, D // N_DEV, Q_LORA)
    g_uq = gather("mla_w_uq").reshape(N_DEV, 2, Q_LORA, QK_HEAD)
    g_mo = gather("mla_w_out").reshape(N_DEV, 2, D // N_DEV, D)

    def row(v):
        return v[None]

    def ffn_params(l, i):
        w = g_ffn_in[:, l, i]
        k = 0 if i == 0 else 6
        return dict(gain=row(norm_g[l, 0 if i == 0 else 2]), shift=row(mod[l, k]), scale=row(mod[l, k + 1]),
                    gm=0.5 * row(mod[l, k + 2]),
                    wg=w[:N_DEV // 2].transpose(1, 0, 2).reshape(D, D_FF),
                    wu=w[N_DEV // 2:].transpose(1, 0, 2).reshape(D, D_FF),
                    wo=g_ffn_out[:, l, i].reshape(D_FF, D))

    def gdn_params(l):
        w = g_gdn_in[:, l].transpose(1, 0, 2).reshape(D, 4 * D + 2 * HEADS)
        pad = lambda t: jnp.pad(t, ((0, 0), (0, 128 - HEADS)))
        return dict(gain=row(norm_g[l, 1]), shift=row(mod[l, 3]), scale=row(mod[l, 4]), gm=row(mod[l, 5]),
                    w_main=w[:, :4 * D],
                    w_tail=jnp.concatenate([pad(w[:, 4 * D:4 * D + HEADS]), pad(w[:, 4 * D + HEADS:])], axis=1),
                    conv_w=conv_w[l], a_log=_pad128(row(a["gdn_a_log"][l])), dt_bias=_pad128(row(a["gdn_dt_bias"][l])),
                    norm_g=row(a["gdn_norm_g"][l]), w_out=g_gdn_out[:, l].reshape(D, D))

    def mla_params(l):
        j = l - N_A
        uq = g_uq[:, j].transpose(1, 0, 2)
        qg = _dup_rope(a["mla_q_norm_g"][j])
        return dict(gain=row(norm_g[l, 1]), shift=row(mod[l, 3]), scale=row(mod[l, 4]), gm=row(mod[l, 5]),
                    w_dq=g_dq[:, j].reshape(D, Q_LORA), q_lora_g=row(a["mla_q_lora_norm_g"][j]),
                    w_uq=_dup_rope(uq).reshape(Q_LORA, HEADS * HEAD_PAD), q_gn=row(qg[:NOPE]), q_gr=row(qg[NOPE:]),
                    w_out=g_mo[:, j].reshape(D, D))

    w_dkv = g_dkv.reshape(D, KV_LORA + ROPE)
    kg = _dup_rope(a["mla_k_norm_g"])
    kv_p = dict(gain=row(a["kv_norm_g"]), shift=row(kvmod[:D]), scale=row(kvmod[D:]),
                w_dkv=jnp.concatenate([w_dkv, w_dkv[:, KV_LORA:]], axis=1), kv_g=row(a["mla_kv_norm_g"]),
                w_ukv=g_ukv.transpose(1, 0, 2).reshape(KV_LORA, HEADS * 2 * HEAD), k_gn=row(kg[:NOPE]), k_gr=row(kg[NOPE:]))

    tapes, kv = [], None
    for l in range(DEPTH):
        p1, pm_, p2 = ffn_params(l, 0), (gdn_params(l) if l < N_A else mla_params(l)), ffn_params(l, 1)
        x, t1 = _ffn_fwd(x, p1)
        x, tm_ = _gdn_fwd(x, pm_) if l < N_A else _mla_fwd(x, pm_, kv)
        x, t2 = _ffn_fwd(x, p2)
        tapes.append((p1, t1, pm_, tm_, p2, t2))
        if l == N_A - 1:
            kv = _kv_fwd(x, kv_p, cosf, sins)
    dx, loss_blk = _loss_and_grad(x, a["loss_target"][0], "loss")
    loss = lax.psum(loss_blk[0, 0], ("x", "y", "c"))

    grads = [None] * DEPTH
    dk_sum = dv_sum = kv_grads = None
    for l in reversed(range(DEPTH)):
        p1, t1, pm_, tm_, p2, t2 = tapes[l]
        if l == N_A - 1:
            dx, kv_grads = _kv_bwd(kv, kv_p, dk_sum, dv_sum, dx)
        dx, g2 = _ffn_bwd(t2, p2, dx)
        if l < N_A:
            dx, gm_ = _gdn_bwd(tm_, pm_, dx)
        else:
            dx, gm_, dk, dv = _mla_bwd(tm_, pm_, kv, dx)
            dk_sum = dk if dk_sum is None else dk_sum + dk
            dv_sum = dv if dv_sum is None else dv_sum + dv
        dx, g1 = _ffn_bwd(t1, p1, dx)
        grads[l] = (g1, gm_, g2)

    def by_cols(g, n):
        return g.reshape(g.shape[0], -1, n).transpose(1, 0, 2)

    stack = {}
    n_in = 2 * D_FF // N_DEV
    ffn_in = [[jnp.concatenate([by_cols(g["wg"], n_in), by_cols(g["wu"], n_in)], axis=0)
               for g in (grads[l][0], grads[l][2])] for l in range(DEPTH)]
    stack["ffn_w_in"] = jnp.stack([jnp.stack(r) for r in ffn_in]).transpose(2, 0, 1, 3, 4)
    ffn_out = [[g["wo"].reshape(N_DEV, D_FF // N_DEV, D) for g in (grads[l][0], grads[l][2])] for l in range(DEPTH)]
    stack["ffn_w_out"] = jnp.stack([jnp.stack(r) for r in ffn_out]).transpose(2, 0, 1, 3, 4)
    gdn_in = [jnp.concatenate([grads[l][1]["w_main"], grads[l][1]["w_tail"][:, :HEADS],
                               grads[l][1]["w_tail"][:, 128:128 + HEADS]], axis=1) for l in range(N_A)]
    stack["gdn_w_in"] = jnp.stack([by_cols(g, 514) for g in gdn_in]).transpose(1, 0, 2, 3)
    stack["gdn_w_out"] = jnp.stack([grads[l][1]["w_out"].reshape(N_DEV, D // N_DEV, D) for l in range(N_A)]).transpose(1, 0, 2, 3)
    d_dkv = kv_grads["w_dkv"]
    stack["mla_w_dkv"] = jnp.concatenate([d_dkv[:, :KV_LORA], d_dkv[:, KV_LORA:KV_LORA + ROPE] + d_dkv[:, KV_LORA + ROPE:]],
                                         axis=1).reshape(N_DEV, D // N_DEV, KV_LORA + ROPE)
    stack["mla_w_ukv"] = by_cols(kv_grads["w_ukv"], 2 * HEAD)
    mla = [grads[l][1] for l in range(N_A, DEPTH)]
    stack["mla_w_dq"] = jnp.stack([g["w_dq"].reshape(N_DEV, D // N_DEV, Q_LORA) for g in mla]).transpose(1, 0, 2, 3)
    stack["mla_w_uq"] = jnp.stack([_fold_rope(g["w_uq"].reshape(Q_LORA, HEADS, HEAD_PAD)).transpose(1, 0, 2)
                                   for g in mla]).transpose(1, 0, 2, 3)
    stack["mla_w_out"] = jnp.stack([g["w_out"].reshape(N_DEV, D // N_DEV, D) for g in mla]).transpose(1, 0, 2, 3)

    out = {}
    for name, shape2d in EXCHANGED.items():
        recv = _exchange(stack[name].reshape((N_DEV,) + shape2d), "exchange_" + name)
        out[name] = _adamw(recv, a[name].reshape(shape2d), a["m_" + name].reshape(shape2d),
                           a["v_" + name].reshape(shape2d), "adamw")

    def dmod(l):
        g1, gm_, g2 = grads[l]
        return jnp.concatenate([g1["shift"], g1["scale"], 0.5 * g1["gm"], gm_["shift"], gm_["scale"], gm_["gm"],
                                g2["shift"], g2["scale"], 0.5 * g2["gm"]], axis=1)

    gdn = [grads[l][1] for l in range(N_A)]
    small = {
        "ada_b": jnp.concatenate([dmod(l) for l in range(DEPTH)], axis=0),
        "kv_ada_b": jnp.concatenate([kv_grads["shift"], kv_grads["scale"]], axis=1),
        "norm_g": jnp.stack([jnp.concatenate([grads[l][0]["gain"], grads[l][1]["gain"], grads[l][2]["gain"]], axis=0)
                             for l in range(DEPTH)]),
        "gdn_conv_w": jnp.stack([g["conv_w"] for g in gdn]),
        "gdn_a_log": jnp.stack([g["a_log"][0, :HEADS] for g in gdn]),
        "gdn_dt_bias": jnp.stack([g["dt_bias"][0, :HEADS] for g in gdn]),
        "gdn_norm_g": jnp.stack([g["norm_g"][0] for g in gdn]),
        "kv_norm_g": kv_grads["gain"],
        "mla_kv_norm_g": kv_grads["kv_g"],
        "mla_k_norm_g": _fold_rope(jnp.concatenate([kv_grads["k_gn"], kv_grads["k_gr"]], axis=1)),
        "mla_q_lora_norm_g": jnp.stack([g["q_lora_g"][0] for g in mla]),
        "mla_q_norm_g": jnp.stack([_fold_rope(jnp.concatenate([g["q_gn"], g["q_gr"]], axis=1))[0] for g in mla]),
    }
    rows = 616
    assert sum(n for _, n in SMALL) <= rows * 128 and all(small[n].size == k for n, k in SMALL)
    small_recv = _all_gather(_pack([small[n] for n, _ in SMALL], rows), "gather_small_grads")
    zero = lambda n, k: jnp.zeros((k,), F32)
    packed = {pre: _pack([a[pre + n] if n in SMALL_REPLICATED else zero(n, k) for n, k in SMALL], rows)
              for pre in ("", "m_", "v_")}
    res = _adamw(small_recv, packed[""], packed["m_"], packed["v_"], "adamw_small")
    offs = {}
    o = 0
    for n, k in SMALL:
        offs[n] = o
        o += k
    for n, k in SMALL:
        if n in SMALL_REPLICATED:
            out[n] = [r.reshape(-1)[offs[n]:offs[n] + k] for r in res]
    gsum = res[0].reshape(-1)
    g_norm = lax.dynamic_slice_in_dim(gsum[offs["norm_g"]:offs["norm_g"] + DEPTH * 3 * D].reshape(DEPTH * 3, D),
                                      me * HEAD, HEAD, axis=1)
    g_conv = lax.dynamic_slice_in_dim(
        gsum[offs["gdn_conv_w"]:offs["gdn_conv_w"] + N_A * CONV_K * 3 * D].reshape(N_A * CONV_K, 3 * D),
        me * 3 * HEAD, 3 * HEAD, axis=1)
    res2 = _adamw(_pack([g_norm, g_conv], 36)[None], *[_pack([a[pre + "norm_g"], a[pre + "gdn_conv_w"]], 36)
                                                      for pre in ("", "m_", "v_")], "adamw_small")
    out["norm_g"] = [r.reshape(-1)[:n_ng] for r in res2]
    out["gdn_conv_w"] = [r.reshape(-1)[n_ng:n_ng + n_cw] for r in res2]

    c_act_t = c_act.T
    all_small = small_recv.reshape(N_DEV, -1)
    dmod_all = all_small[:, :DEPTH * N_MOD * D].reshape(N_DEV, DEPTH, N_MOD * D)
    dmod_mine = lax.dynamic_slice_in_dim(dmod_all, me * n_ada, n_ada, axis=2)
    g_ada = jnp.concatenate([_outer8(c_act_t, dmod_mine[:, l], "ada_grad") for l in range(DEPTH)], axis=0)
    out["ada_w"] = _adamw(g_ada[None], *[a[pre + "ada_w"].reshape(DEPTH * D, n_ada) for pre in ("", "m_", "v_")], "adamw")
    dkv_all = all_small[:, offs["kv_ada_b"]:offs["kv_ada_b"] + 2 * D]
    g_kv = _outer8(c_act_t, lax.dynamic_slice_in_dim(dkv_all, me * (2 * D // N_DEV), 2 * D // N_DEV, axis=1), "ada_grad")
    out["kv_ada_w"] = _adamw(g_kv[None], *[a[pre + "kv_ada_w"] for pre in ("", "m_", "v_")], "adamw")

    result = [loss, dx[None]]
    for k in range(4):
        result += [out[n][k].reshape(a[n].shape) for n in WEIGHTS]
    return tuple(result)


def kernel(x, c, positions, ada_w, ada_b, norm_g, ffn_w_in, ffn_w_out, gdn_w_in, gdn_conv_w, gdn_a_log, gdn_dt_bias, gdn_norm_g, gdn_w_out, kv_ada_w, kv_ada_b, kv_norm_g, mla_w_dkv, mla_kv_norm_g, mla_w_ukv, mla_k_norm_g, mla_w_dq, mla_q_lora_norm_g, mla_w_uq, mla_q_norm_g, mla_w_out, loss_target, m_ada_w, m_ada_b, m_norm_g, m_ffn_w_in, m_ffn_w_out, m_gdn_w_in, m_gdn_conv_w, m_gdn_a_log, m_gdn_dt_bias, m_gdn_norm_g, m_gdn_w_out, m_kv_ada_w, m_kv_ada_b, m_kv_norm_g, m_mla_w_dkv, m_mla_kv_norm_g, m_mla_w_ukv, m_mla_k_norm_g, m_mla_w_dq, m_mla_q_lora_norm_g, m_mla_w_uq, m_mla_q_norm_g, m_mla_w_out, v_ada_w, v_ada_b, v_norm_g, v_ffn_w_in, v_ffn_w_out, v_gdn_w_in, v_gdn_conv_w, v_gdn_a_log, v_gdn_dt_bias, v_gdn_norm_g, v_gdn_w_out, v_kv_ada_w, v_kv_ada_b, v_kv_norm_g, v_mla_w_dkv, v_mla_kv_norm_g, v_mla_w_ukv, v_mla_k_norm_g, v_mla_w_dq, v_mla_q_lora_norm_g, v_mla_w_uq, v_mla_q_norm_g, v_mla_w_out):
    return _step(dict(locals()))
```

```python
import functools
import math

import jax
import jax.numpy as jnp
from jax import lax
from jax.experimental import pallas as pl
from jax.experimental.pallas import tpu as pltpu

F32 = jnp.float32
BF16 = jnp.bfloat16

N_DEV = 8
D = 1024
D_FF = 2816
DEPTH = 4
N_A = 2
N_MOD = 9
HEADS = 8
HEAD = 128
CHUNK = 64
CONV_K = 4
KV_LORA = 256
Q_LORA = 384
NOPE = 128
ROPE = 64
QK_HEAD = NOPE + ROPE
HEAD_PAD = 256
ROPE_BASE = 10000.0
EPS = 1e-6
LR, B1, B2, ADAM_EPS, WD, STEP = 0.001, 0.9, 0.999, 1e-08, 0.01, 10

VMEM_LIMIT = 48 * 1024 * 1024
ROW_TILE = 256
MESH = pl.DeviceIdType.MESH

_NN = (((1,), (0,)), ((), ()))
_NT = (((1,), (1,)), ((), ()))
_TN = (((0,), (0,)), ((), ()))
_DIMS = {"nn": _NN, "nt": _NT, "tn": _TN}


def _params(dims=None):
    return pltpu.CompilerParams(dimension_semantics=dims, vmem_limit_bytes=VMEM_LIMIT)


def _tile(n, target):
    for t in range(target - target % 128, 0, -128):
        if n % t == 0:
            return t
    return n


def _matmul(pairs, form, name, out_dtype=F32, tm=1024, tn=1408, tk=512, boffs=None):
    a0, b0 = pairs[0]
    if form == "nn":
        m, n = a0.shape[0], b0.shape[1]
        ks = [a.shape[1] for a, _ in pairs]
    elif form == "nt":
        m, n = a0.shape[0], b0.shape[0]
        ks = [a.shape[1] for a, _ in pairs]
    else:
        m, n = a0.shape[1], b0.shape[1]
        ks = [a.shape[0] for a, _ in pairs]
    tm, tn = _tile(m, tm), _tile(n, tn)
    tk = min(_tile(k, tk) for k in ks)
    assert m % tm == 0 and n % tn == 0 and all(k % tk == 0 for k in ks), (name, m, n, ks)
    boffs = boffs or [0] * len(pairs)
    assert all(o % tk == 0 for o in boffs)
    steps = [k // tk for k in ks]
    starts = [sum(steps[:p]) for p in range(len(pairs))]
    nk = sum(steps)

    def kidx(p, k):
        return jnp.clip(k - starts[p], 0, steps[p] - 1)

    in_specs, args = [], []
    for p, (a, b) in enumerate(pairs):
        if form == "tn":
            in_specs.append(pl.BlockSpec((tk, tm), lambda i, j, k, p=p: (kidx(p, k), i)))
            in_specs.append(pl.BlockSpec((tk, tn), lambda i, j, k, p=p: (kidx(p, k), j)))
        elif form == "nn":
            in_specs.append(pl.BlockSpec((tm, tk), lambda i, j, k, p=p: (i, kidx(p, k))))
            in_specs.append(pl.BlockSpec((tk, tn), lambda i, j, k, p=p: (kidx(p, k), j)))
        else:
            in_specs.append(pl.BlockSpec((tm, tk), lambda i, j, k, p=p: (i, kidx(p, k))))
            in_specs.append(pl.BlockSpec((tn, tk), lambda i, j, k, p=p, o=boffs[p] // tk: (j, kidx(p, k) + o)))
        args += [a, b]
    dims = _DIMS[form]
    npairs = len(pairs)

    def body(*refs):
        o_ref = refs[2 * npairs]
        k = pl.program_id(2)

        def prod(p):
            return lax.dot_general(refs[2 * p][...].astype(BF16), refs[2 * p + 1][...].astype(BF16), dims,
                                   preferred_element_type=F32)

        if nk == 1:
            o_ref[...] = prod(0).astype(o_ref.dtype)
            return
        acc = refs[2 * npairs + 1]

        @pl.when(k == 0)
        def _():
            acc[...] = jnp.zeros_like(acc)

        for p in range(npairs):
            @pl.when((k >= starts[p]) & (k < starts[p] + steps[p]))
            def _(p=p):
                acc[...] += prod(p)

        @pl.when(k == nk - 1)
        def _():
            o_ref[...] = acc[...].astype(o_ref.dtype)

    return pl.pallas_call(
        body, name=name, grid=(m // tm, n // tn, nk), in_specs=in_specs,
        out_specs=pl.BlockSpec((tm, tn), lambda i, j, k: (i, j)),
        out_shape=jax.ShapeDtypeStruct((m, n), out_dtype),
        scratch_shapes=[] if nk == 1 else [pltpu.VMEM((tm, tn), F32)],
        compiler_params=_params(("parallel", "parallel", "arbitrary")),
    )(*args)


def _mm(a, b, form, name, **kw):
    return _matmul([(a, b)], form, name, **kw)


def _cols(spec, g):
    return spec[g] if isinstance(spec, list) else spec


def _rowwise_fwd(fn, rows, pars, outs, name, groups=1, ts=ROW_TILE):
    s = rows[0][0].shape[0]
    ts = min(ts, s)
    assert s % ts == 0
    nr, npar = len(rows), len(pars)

    def body(*refs):
        par_t = [r[...] for r in refs[nr:nr + npar]]
        out_refs = refs[nr + npar:]
        for g in range(groups):
            row_t = []
            for r, (_, spec) in zip(refs[:nr], rows):
                c0, w = _cols(spec, g)
                row_t.append(r[:, c0:c0 + w].astype(F32))
            res = fn(g, *row_t, *par_t)
            for o_ref, val, (_, _, spec) in zip(out_refs, res, outs):
                c0, w = _cols(spec, g)
                o_ref[:, c0:c0 + w] = val.astype(o_ref.dtype)

    return pl.pallas_call(
        body, name=name, grid=(s // ts,),
        in_specs=[pl.BlockSpec((ts, a.shape[1]), lambda i: (i, 0)) for a, _ in rows]
        + [pl.BlockSpec(p.shape, lambda i: (0, 0)) for p in pars],
        out_specs=[pl.BlockSpec((ts, w), lambda i: (i, 0)) for w, _, _ in outs],
        out_shape=[jax.ShapeDtypeStruct((s, w), dt) for w, dt, _ in outs],
        compiler_params=_params(("parallel",)),
    )(*[a for a, _ in rows], *pars)


def _rowwise_bwd(fn, rows, pars, outs, douts, gmap, gshapes, name, groups=1, add=None, par_grads=True,
                 ts=ROW_TILE):
    s = rows[0][0].shape[0]
    ts = min(ts, s)
    assert s % ts == 0
    nr, npar, nout, ng = len(rows), len(pars), len(outs), len(gshapes)
    add = add or {}
    add_keys = sorted(add)

    def body(*refs):
        row_refs = refs[:nr]
        par_refs = refs[nr:nr + npar]
        dout_refs = refs[nr + npar:nr + npar + nout]
        add_refs = refs[nr + npar + nout:nr + npar + nout + len(add_keys)]
        g_refs = refs[nr + npar + nout + len(add_keys):][:ng]
        pg_refs = refs[nr + npar + nout + len(add_keys) + ng:]
        par_t = [r[...] for r in par_refs]
        par_acc = [None] * npar
        shared_acc = {}
        for g in range(groups):
            row_t = []
            for r, (_, spec) in zip(row_refs, rows):
                c0, w = _cols(spec, g)
                row_t.append(r[:, c0:c0 + w].astype(F32))
            cts = []
            for r, (_, _, spec) in zip(dout_refs, outs):
                c0, w = _cols(spec, g)
                cts.append(r[:, c0:c0 + w].astype(F32))
            _, vjp = jax.vjp(lambda *t, g=g: tuple(fn(g, *t)), *row_t, *par_t)
            grads = vjp(tuple(cts))
            for k in range(nr):
                if gmap[k] is None:
                    continue
                gi, spec = gmap[k]
                if isinstance(spec, list) or groups == 1:
                    c0, w = _cols(spec, g)
                    val = grads[k]
                    if gi in add:
                        val = val + add_refs[add_keys.index(gi)][:, c0:c0 + w].astype(F32)
                    g_refs[gi][:, c0:c0 + w] = val.astype(g_refs[gi].dtype)
                else:
                    shared_acc[k] = grads[k] if k not in shared_acc else shared_acc[k] + grads[k]
            if par_grads:
                for k in range(npar):
                    pg = grads[nr + k]
                    par_acc[k] = pg if par_acc[k] is None else par_acc[k] + pg
        for k, val in shared_acc.items():
            gi, (c0, w) = gmap[k]
            assert gi not in add
            g_refs[gi][:, c0:c0 + w] = val.astype(g_refs[gi].dtype)
        if par_grads:
            first = pl.program_id(0) == 0
            for k in range(npar):
                @pl.when(first)
                def _(k=k):
                    pg_refs[k][...] = par_acc[k]

                @pl.when(jnp.logical_not(first))
                def _(k=k):
                    pg_refs[k][...] += par_acc[k]

    out_specs = [pl.BlockSpec((ts, w), lambda i: (i, 0)) for w, _ in gshapes]
    out_shape = [jax.ShapeDtypeStruct((s, w), dt) for w, dt in gshapes]
    if par_grads:
        out_specs += [pl.BlockSpec(p.shape, lambda i: (0, 0)) for p in pars]
        out_shape += [jax.ShapeDtypeStruct(p.shape, F32) for p in pars]
    return pl.pallas_call(
        body, name=name, grid=(s // ts,),
        in_specs=[pl.BlockSpec((ts, a.shape[1]), lambda i: (i, 0)) for a, _ in rows]
        + [pl.BlockSpec(p.shape, lambda i: (0, 0)) for p in pars]
        + [pl.BlockSpec((ts, a.shape[1]), lambda i: (i, 0)) for a in douts]
        + [pl.BlockSpec((ts, add[k].shape[1]), lambda i: (i, 0)) for k in add_keys],
        out_specs=out_specs, out_shape=out_shape,
        compiler_params=_params(("arbitrary",)),
    )(*[a for a, _ in rows], *pars, *douts, *[add[k] for k in add_keys])


def _sigmoid(x):
    return 1.0 / (1.0 + jnp.exp(-x))


def _silu(x):
    return x * _sigmoid(x)


def _softplus(x):
    return jnp.maximum(x, 0.0) + jnp.log(1.0 + jnp.exp(-jnp.abs(x)))


def _rms(t, g, n=None):
    n = n or t.shape[-1]
    return t * lax.rsqrt(jnp.sum(t * t, axis=-1, keepdims=True) / n + EPS) * g


def _modulate_fn(g, x, gain, scale, shift):
    return (_rms(x, gain) * (1.0 + scale) + shift,)


def _resgate_fn(g, x, y, gm):
    return (x + gm * y,)


def _gate_only_fn(g, y, gm):
    return (gm * y,)


def _gdn_gates_fn(g, b_logit, a_logit, a_log, dt_bias):
    return _sigmoid(b_logit), -jnp.exp(a_log) * _softplus(a_logit + dt_bias)


def _gdn_outnorm_fn(g, o, z, gain):
    return (_rms(o, gain) * _silu(z),)


def _rms_fn(g, t, gain):
    return (_rms(t, gain),)


@jax.custom_vjp
def _swap_halves(t):
    return pltpu.roll(t, 32, 1)


_swap_halves.defvjp(lambda t: (pltpu.roll(t, 32, 1), None), lambda _, ct: (pltpu.roll(ct, 96, 1),))


def _head_norm_rope_fn(g, nope, rope, cosf, sins, gain_n, gain_r):
    first = lax.broadcasted_iota(jnp.int32, rope.shape, 1) < ROPE
    ss = jnp.sum(nope * nope, axis=-1, keepdims=True) + jnp.sum(jnp.where(first, rope * rope, 0.0), axis=-1,
                                                                 keepdims=True)
    r = lax.rsqrt(ss / QK_HEAD + EPS)
    tn = nope * r * gain_n
    tr = rope * r * gain_r
    rot = jnp.where(first, tr * cosf + _swap_halves(tr) * sins, 0.0)
    return tn, rot


def _q_norm_rope_fn(g, nope, rope, cosf, sins, gain_n, gain_r):
    tn, rot = _head_norm_rope_fn(g, nope, rope, cosf, sins, gain_n, gain_r)
    return (jnp.concatenate([tn, rot], axis=1),)


def _k_norm_rope_fn(g, nope, val, rope, cosf, sins, gain_n, gain_r):
    tn, rot = _head_norm_rope_fn(g, nope, rope, cosf, sins, gain_n, gain_r)
    return jnp.concatenate([tn, rot], axis=1), val


def _loss_fn(g, y, target):
    e = y - target
    return (jnp.sum(e * e, axis=-1, keepdims=True) * (0.5 / D) * jnp.ones((1, 128), F32),)


def _ffn_in(h, wg, wu, name, tm=1024, tn=256):
    s = h.shape[0]
    tm = min(tm, s)

    def body(h_ref, wg_ref, wu_ref, g_ref, u_ref, a_ref):
        hb = h_ref[...]
        gate = jnp.dot(hb, wg_ref[...], preferred_element_type=F32)
        up = jnp.dot(hb, wu_ref[...], preferred_element_type=F32)
        g_ref[...] = gate.astype(BF16)
        u_ref[...] = up.astype(BF16)
        a_ref[...] = (_silu(gate) * up).astype(BF16)

    spec = pl.BlockSpec((tm, tn), lambda i, j: (i, j))
    return pl.pallas_call(
        body, name=name, grid=(s // tm, D_FF // tn),
        in_specs=[pl.BlockSpec((tm, D), lambda i, j: (i, 0)), pl.BlockSpec((D, tn), lambda i, j: (0, j)),
                  pl.BlockSpec((D, tn), lambda i, j: (0, j))],
        out_specs=[spec, spec, spec], out_shape=[jax.ShapeDtypeStruct((s, D_FF), BF16)] * 3,
        compiler_params=_params(("parallel", "parallel")),
    )(h, wg, wu)


def _ffn_bwd_act(dy, wo, gate, up, name, tm=1024, tn=256):
    s = dy.shape[0]
    tm = min(tm, s)

    def body(dy_ref, wo_ref, g_ref, u_ref, dg_ref, du_ref):
        dact = lax.dot_general(dy_ref[...], wo_ref[...], _NT, preferred_element_type=F32)
        gate = g_ref[...].astype(F32)
        up = u_ref[...].astype(F32)
        sg = _sigmoid(gate)
        dg_ref[...] = (dact * up * (sg * (1.0 + gate * (1.0 - sg)))).astype(BF16)
        du_ref[...] = (dact * (gate * sg)).astype(BF16)

    spec = pl.BlockSpec((tm, tn), lambda i, j: (i, j))
    return pl.pallas_call(
        body, name=name, grid=(s // tm, D_FF // tn),
        in_specs=[pl.BlockSpec((tm, D), lambda i, j: (i, 0)), pl.BlockSpec((tn, D), lambda i, j: (j, 0)), spec, spec],
        out_specs=[spec, spec], out_shape=[jax.ShapeDtypeStruct((s, D_FF), BF16)] * 2,
        compiler_params=_params(("parallel", "parallel")),
    )(dy, wo, gate, up)


def _shift_down(x, d):
    rows = lax.broadcasted_iota(jnp.int32, x.shape, 0)
    return jnp.where(rows >= d, pltpu.roll(x, d, 0), 0.0)


def _shift_up(x, d):
    n = x.shape[0]
    rows = lax.broadcasted_iota(jnp.int32, x.shape, 0)
    return jnp.where(rows < n - d, pltpu.roll(x, n - d, 0), 0.0)


def _conv_post(pre, is_qk):
    a = _silu(pre)
    l2 = a * lax.rsqrt(jnp.sum(a * a, axis=-1, keepdims=True) + EPS)
    return jnp.where(is_qk, l2, a)


def _conv_pre(x, w):
    pre = x * w[CONV_K - 1:CONV_K, :]
    for j in range(CONV_K - 1):
        pre = pre + _shift_down(x, CONV_K - 1 - j) * w[j:j + 1, :]
    return pre


def _gdn_conv_fwd(pm, conv_w, name):
    s = pm.shape[0]
    nblk = 3 * D // HEAD

    def body(x_ref, w_ref, o_ref):
        is_qk = pl.program_id(0) < 2 * HEADS
        o_ref[...] = _conv_post(_conv_pre(x_ref[...], w_ref[...]), is_qk)

    return pl.pallas_call(
        body, name=name, grid=(nblk,),
        in_specs=[pl.BlockSpec((s, HEAD), lambda c: (0, c)), pl.BlockSpec((CONV_K, HEAD), lambda c: (0, c))],
        out_specs=pl.BlockSpec((s, HEAD), lambda c: (0, c)),
        out_shape=jax.ShapeDtypeStruct((s, 3 * D), F32), compiler_params=_params(("parallel",)),
    )(pm, conv_w)


def _gdn_conv_bwd(pm, conv_w, dout, part, name):
    s = pm.shape[0]
    off = part * HEADS

    def body(x_ref, w_ref, d_ref, dx_ref, dw_ref):
        x, w = x_ref[...], w_ref[...]
        _, vjp = jax.vjp(lambda p: _conv_post(p, part < 2), _conv_pre(x, w))
        dpre, = vjp(d_ref[...])
        dx = dpre * w[CONV_K - 1:CONV_K, :]
        rows = [None] * CONV_K
        rows[CONV_K - 1] = jnp.sum(dpre * x, axis=0, keepdims=True)
        for j in range(CONV_K - 1):
            dx = dx + _shift_up(dpre, CONV_K - 1 - j) * w[j:j + 1, :]
            rows[j] = jnp.sum(dpre * _shift_down(x, CONV_K - 1 - j), axis=0, keepdims=True)
        dx_ref[...] = dx
        dw_ref[...] = jnp.concatenate(rows, axis=0)

    return pl.pallas_call(
        body, name=name, grid=(HEADS,),
        in_specs=[pl.BlockSpec((s, HEAD), lambda c: (0, c + off)), pl.BlockSpec((CONV_K, HEAD), lambda c: (0, c + off)),
                  pl.BlockSpec((s, HEAD), lambda c: (0, c))],
        out_specs=[pl.BlockSpec((s, HEAD), lambda c: (0, c)), pl.BlockSpec((CONV_K, HEAD), lambda c: (0, c))],
        out_shape=[jax.ShapeDtypeStruct((s, D), F32), jax.ShapeDtypeStruct((CONV_K, D), F32)],
        compiler_params=_params(("parallel",)),
    )(pm, conv_w, dout)


def _make_dot(hi):
    def raw(a, b, dims):
        if hi:
            return lax.dot_general(a, b, dims, preferred_element_type=F32, precision=lax.Precision.HIGHEST)
        return lax.dot_general(a.astype(BF16), b.astype(BF16), dims, preferred_element_type=F32)

    @functools.partial(jax.custom_vjp, nondiff_argnums=(2,))
    def dot(a, b, form):
        return raw(a, b, _DIMS[form])

    def fwd(a, b, form):
        return raw(a, b, _DIMS[form]), (a, b)

    def bwd(form, res, ct):
        a, b = res
        if form == "nn":
            return raw(ct, b, _NT), raw(a, ct, _TN)
        if form == "nt":
            return raw(ct, b, _NN), raw(ct, a, _TN)
        return raw(b, ct, _NT), raw(a, ct, _NN)

    dot.defvjp(fwd, bwd)
    return dot


_dot = _make_dot(False)
_dot_hi = _make_dot(True)


def _tri_inv_raw(low):
    n = low.shape[0]
    i = lax.broadcasted_iota(jnp.int32, (n, n), 0)
    j = lax.broadcasted_iota(jnp.int32, (n, n), 1)
    eye = (i == j).astype(F32)
    hdot = functools.partial(jnp.dot, preferred_element_type=F32, precision=lax.Precision.HIGHEST)
    same16 = (i // 16) == (j // 16)
    neg = jnp.where(same16, -low, 0.0)
    inv = eye + neg
    power = neg
    for _ in range(3):
        power = hdot(power, power)
        inv = hdot(inv, eye + power)
    for blk in (32, 64):
        off = jnp.where(((i // blk) == (j // blk)) & ((i // (blk // 2)) != (j // (blk // 2))), low, 0.0)
        inv = inv - hdot(inv, hdot(off, inv))
    return inv


@jax.custom_vjp
def _tri_inv(low):
    return _tri_inv_raw(low)


def _tri_inv_fwd(low):
    inv = _tri_inv_raw(low)
    return inv, inv


def _tri_inv_bwd(inv, ct):
    hdot = functools.partial(lax.dot_general, preferred_element_type=F32, precision=lax.Precision.HIGHEST)
    return (-hdot(hdot(inv, ct, _TN), inv, _NT),)


_tri_inv.defvjp(_tri_inv_fwd, _tri_inv_bwd)


def _gdn_chunk(q, k, v, beta, gc, gr, state):
    n = q.shape[0]
    i = lax.broadcasted_iota(jnp.int32, (n, n), 0)
    j = lax.broadcasted_iota(jnp.int32, (n, n), 1)
    incl, strict = i >= j, i > j
    qs = q * (HEAD ** -0.5)
    decay = jnp.where(incl, jnp.exp(jnp.where(incl, gc - gr, 0.0)), 0.0)
    kb = k * beta
    low = jnp.where(strict, _dot(kb, k, "nt") * decay, 0.0)
    inv = _tri_inv(low)
    u = _dot_hi(inv, v * beta, "nn")
    w = _dot_hi(inv, kb * jnp.exp(gc), "nn")
    attn = jnp.where(incl, _dot(qs, k, "nt") * decay, 0.0)
    v_new = u - _dot(w, state, "nn")
    o = _dot(qs * jnp.exp(gc), state, "nn") + _dot(attn, v_new, "nn")
    last = lax.broadcasted_iota(jnp.int32, gc.shape, 0) == n - 1
    g_last = jnp.sum(jnp.where(last, gc, 0.0), axis=0, keepdims=True)
    k_dec = k * jnp.exp(g_last - gc)
    return o, state * jnp.exp(g_last) + _dot(k_dec, v_new, "tn")


def _chunk_decay(g_blk, gt_blk):
    n = CHUNK
    i = lax.broadcasted_iota(jnp.int32, (n, n), 0)
    j = lax.broadcasted_iota(jnp.int32, (n, n), 1)
    hdot = functools.partial(jnp.dot, preferred_element_type=F32, precision=lax.Precision.HIGHEST)
    return hdot((i >= j).astype(F32), g_blk), hdot(gt_blk, (i <= j).astype(F32))


def _gdn_specs(s):
    nc = s // CHUNK
    return nc, [
        pl.BlockSpec((CHUNK, D), lambda n: (n, 0)), pl.BlockSpec((CHUNK, D), lambda n: (n, 1)),
        pl.BlockSpec((CHUNK, D), lambda n: (n, 2)), pl.BlockSpec((CHUNK, HEAD), lambda n: (n, 0)),
        pl.BlockSpec((CHUNK, HEAD), lambda n: (n, 0)), pl.BlockSpec((None, HEADS, CHUNK), lambda n: (n, 0, 0))]


def _gdn_scan_fwd(qkv, g, beta, gt, name):
    s = qkv.shape[0]
    nc, in_specs = _gdn_specs(s)

    def body(q_ref, k_ref, v_ref, g_ref, b_ref, gt_ref, o_ref, st_ref, state):
        @pl.when(pl.program_id(0) == 0)
        def _():
            state[...] = jnp.zeros_like(state)

        gcum, gcum_t = _chunk_decay(g_ref[...], gt_ref[...])
        beta_blk = b_ref[...]
        st_ref[...] = state[...]
        for h in range(HEADS):
            cs = slice(h * HEAD, (h + 1) * HEAD)
            o, new = _gdn_chunk(q_ref[:, cs], k_ref[:, cs], v_ref[:, cs], beta_blk[:, h:h + 1], gcum[:, h:h + 1],
                                gcum_t[h:h + 1, :], state[h])
            o_ref[:, cs] = o
            state[h] = new

    return pl.pallas_call(
        body, name=name, grid=(nc,), in_specs=in_specs,
        out_specs=[pl.BlockSpec((CHUNK, D), lambda n: (n, 0)),
                   pl.BlockSpec((None, HEADS, HEAD, HEAD), lambda n: (n, 0, 0, 0))],
        out_shape=[jax.ShapeDtypeStruct((s, D), F32), jax.ShapeDtypeStruct((nc, HEADS, HEAD, HEAD), F32)],
        scratch_shapes=[pltpu.VMEM((HEADS, HEAD, HEAD), F32)],
        compiler_params=_params(("arbitrary",)),
    )(qkv, qkv, qkv, g, beta, gt)


def _gdn_scan_bwd(qkv, g, beta, gt, states, do, name):
    s = qkv.shape[0]
    nc, in_specs = _gdn_specs(s)
    rev = lambda spec: pl.BlockSpec(spec.block_shape, lambda n, f=spec.index_map: f(nc - 1 - n))
    in_specs = [rev(sp) for sp in in_specs]
    in_specs += [pl.BlockSpec((None, HEADS, HEAD, HEAD), lambda n: (nc - 1 - n, 0, 0, 0)),
                 pl.BlockSpec((CHUNK, D), lambda n: (nc - 1 - n, 0))]

    def body(q_ref, k_ref, v_ref, g_ref, b_ref, gt_ref, st_ref, do_ref, dq_ref, dk_ref, dv_ref, dg_ref, db_ref,
             dgt_ref, dstate):
        @pl.when(pl.program_id(0) == 0)
        def _():
            dstate[...] = jnp.zeros_like(dstate)

        gcum, gcum_t = _chunk_decay(g_ref[...], gt_ref[...])
        beta_blk = b_ref[...]
        lane = lax.broadcasted_iota(jnp.int32, (CHUNK, HEAD), 1)
        sub = lax.broadcasted_iota(jnp.int32, (HEADS, CHUNK), 0)
        dgc_all = jnp.zeros((CHUNK, HEAD), F32)
        db_all = jnp.zeros((CHUNK, HEAD), F32)
        dgr_all = jnp.zeros((HEADS, CHUNK), F32)
        for h in range(HEADS):
            cs = slice(h * HEAD, (h + 1) * HEAD)
            _, vjp = jax.vjp(_gdn_chunk, q_ref[:, cs], k_ref[:, cs], v_ref[:, cs], beta_blk[:, h:h + 1],
                             gcum[:, h:h + 1], gcum_t[h:h + 1, :], st_ref[h])
            dq, dk, dv, db, dgc, dgr, dst = vjp((do_ref[:, cs], dstate[h]))
            dq_ref[:, cs] = dq
            dk_ref[:, cs] = dk
            dv_ref[:, cs] = dv
            dstate[h] = dst
            db_all = jnp.where(lane == h, db, db_all)
            dgc_all = jnp.where(lane == h, dgc, dgc_all)
            dgr_all = jnp.where(sub == h, dgr, dgr_all)
        i = lax.broadcasted_iota(jnp.int32, (CHUNK, CHUNK), 0)
        j = lax.broadcasted_iota(jnp.int32, (CHUNK, CHUNK), 1)
        hdot = functools.partial(jnp.dot, preferred_element_type=F32, precision=lax.Precision.HIGHEST)
        dg_ref[...] = hdot((i <= j).astype(F32), dgc_all)
        dgt_ref[...] = hdot(dgr_all, (i >= j).astype(F32))
        db_ref[...] = db_all

    blk = pl.BlockSpec((CHUNK, D), lambda n: (nc - 1 - n, 0))
    gblk = pl.BlockSpec((CHUNK, HEAD), lambda n: (nc - 1 - n, 0))
    return pl.pallas_call(
        body, name=name, grid=(nc,), in_specs=in_specs,
        out_specs=[blk, blk, blk, gblk, gblk, pl.BlockSpec((None, HEADS, CHUNK), lambda n: (nc - 1 - n, 0, 0))],
        out_shape=[jax.ShapeDtypeStruct((s, D), F32)] * 3 + [jax.ShapeDtypeStruct((s, HEAD), F32)] * 2
        + [jax.ShapeDtypeStruct((nc, HEADS, CHUNK), F32)],
        scratch_shapes=[pltpu.VMEM((HEADS, HEAD, HEAD), F32)],
        compiler_params=_params(("arbitrary",)),
    )(qkv, qkv, qkv, g, beta, gt, states, do)


ATT_TILE = 512
ATT_SCALE = QK_HEAD ** -0.5


def _att_mask(t):
    qpos = lax.broadcasted_iota(jnp.int32, (t, t), 0)
    kpos = lax.broadcasted_iota(jnp.int32, (t, t), 1)
    return (kpos // CHUNK) <= (qpos // CHUNK)


def _att_pairs(nb, by_query):
    if by_query:
        pairs = [(i, j) for i in range(nb) for j in range(i + 1)]
    else:
        pairs = [(j, i) for j in range(nb) for i in range(j, nb)]
    return jnp.array([a for a, _ in pairs], jnp.int32), jnp.array([b for _, b in pairs], jnp.int32)


def _attn_fwd(q, k, v, name):
    s = q.shape[0]
    t = min(ATT_TILE, s)
    nb = s // t
    ii, jj = _att_pairs(nb, by_query=True)

    def body(ii_ref, jj_ref, q_ref, k_ref, v_ref, o_ref, lse_ref, m_s, l_s, acc):
        step = pl.program_id(1)
        i, j = ii_ref[step], jj_ref[step]

        @pl.when(j == 0)
        def _():
            m_s[...] = jnp.full_like(m_s, -jnp.inf)
            l_s[...] = jnp.zeros_like(l_s)
            acc[...] = jnp.zeros_like(acc)

        sc = lax.dot_general(q_ref[...], k_ref[...], _NT, preferred_element_type=F32) * ATT_SCALE
        sc = lax.cond(i == j, lambda u: jnp.where(_att_mask(t), u, -jnp.inf), lambda u: u, sc)
        m_new = jnp.maximum(m_s[...], jnp.max(sc, axis=-1, keepdims=True))
        alpha = jnp.exp(m_s[...] - m_new)
        p = jnp.exp(sc - m_new)
        l_s[...] = alpha * l_s[...] + jnp.sum(p, axis=-1, keepdims=True)
        acc[...] = alpha * acc[...] + jnp.dot(p.astype(BF16), v_ref[...], preferred_element_type=F32)
        m_s[...] = m_new

        @pl.when(j == i)
        def _():
            o_ref[...] = acc[...] / l_s[...]
            lse_ref[...] = m_s[...] + jnp.log(l_s[...])

    grid_spec = pltpu.PrefetchScalarGridSpec(
        num_scalar_prefetch=2, grid=(HEADS, len(ii)),
        in_specs=[pl.BlockSpec((t, HEAD_PAD), lambda h, n, ir, jr: (ir[n], h)),
                  pl.BlockSpec((t, HEAD_PAD), lambda h, n, ir, jr: (jr[n], h)),
                  pl.BlockSpec((t, HEAD), lambda h, n, ir, jr: (jr[n], h))],
        out_specs=[pl.BlockSpec((t, HEAD), lambda h, n, ir, jr: (ir[n], h)),
                   pl.BlockSpec((None, t, 1), lambda h, n, ir, jr: (h, ir[n], 0))],
        scratch_shapes=[pltpu.VMEM((t, 1), F32), pltpu.VMEM((t, 1), F32), pltpu.VMEM((t, HEAD), F32)])
    return pl.pallas_call(
        body, name=name, grid_spec=grid_spec,
        out_shape=[jax.ShapeDtypeStruct((s, HEADS * HEAD), F32), jax.ShapeDtypeStruct((HEADS, s, 1), F32)],
        compiler_params=_params(("parallel", "arbitrary")),
    )(ii, jj, q, k, v)


def _attn_bwd(q, k, v, do, o, lse, name):
    s = q.shape[0]
    t = min(ATT_TILE, s)
    nb = s // t
    jj, ii = _att_pairs(nb, by_query=False)

    def body(jj_ref, ii_ref, q_ref, k_ref, v_ref, do_ref, o_ref, lse_ref, dq_ref, dk_ref, dv_ref, dk_acc, dv_acc):
        step = pl.program_id(1)
        i, j = ii_ref[step], jj_ref[step]

        @pl.when(step == 0)
        def _():
            dq_ref[...] = jnp.zeros_like(dq_ref)

        @pl.when(i == j)
        def _():
            dk_acc[...] = jnp.zeros_like(dk_acc)
            dv_acc[...] = jnp.zeros_like(dv_acc)

        sc = lax.dot_general(q_ref[...], k_ref[...], _NT, preferred_element_type=F32) * ATT_SCALE
        p = jnp.exp(sc - lse_ref[...])
        p = lax.cond(i == j, lambda u: jnp.where(_att_mask(t), u, 0.0), lambda u: u, p)
        do_f = do_ref[...]
        dob = do_f.astype(BF16)
        delta = jnp.sum(do_f * o_ref[...], axis=-1, keepdims=True)
        dv_acc[...] += lax.dot_general(p.astype(BF16), dob, _TN, preferred_element_type=F32)
        dp = lax.dot_general(dob, v_ref[...], _NT, preferred_element_type=F32)
        ds = (p * (dp - delta) * ATT_SCALE).astype(BF16)
        dk_acc[...] += lax.dot_general(ds, q_ref[...], _TN, preferred_element_type=F32)
        rows = pl.ds(pl.multiple_of(i * t, t), t)
        dq_ref[rows, :] += jnp.dot(ds, k_ref[...], preferred_element_type=F32)

        @pl.when(i == nb - 1)
        def _():
            dk_ref[...] = dk_acc[...]
            dv_ref[...] = dv_acc[...]

    grid_spec = pltpu.PrefetchScalarGridSpec(
        num_scalar_prefetch=2, grid=(HEADS, len(jj)),
        in_specs=[pl.BlockSpec((t, HEAD_PAD), lambda h, n, jr, ir: (ir[n], h)),
                  pl.BlockSpec((t, HEAD_PAD), lambda h, n, jr, ir: (jr[n], h)),
                  pl.BlockSpec((t, HEAD), lambda h, n, jr, ir: (jr[n], h)),
                  pl.BlockSpec((t, HEAD), lambda h, n, jr, ir: (ir[n], h)),
                  pl.BlockSpec((t, HEAD), lambda h, n, jr, ir: (ir[n], h)),
                  pl.BlockSpec((None, t, 1), lambda h, n, jr, ir: (h, ir[n], 0))],
        out_specs=[pl.BlockSpec((s, HEAD_PAD), lambda h, n, jr, ir: (0, h)),
                   pl.BlockSpec((t, HEAD_PAD), lambda h, n, jr, ir: (jr[n], h)),
                   pl.BlockSpec((t, HEAD), lambda h, n, jr, ir: (jr[n], h))],
        scratch_shapes=[pltpu.VMEM((t, HEAD_PAD), F32), pltpu.VMEM((t, HEAD), F32)])
    return pl.pallas_call(
        body, name=name, grid_spec=grid_spec,
        out_shape=[jax.ShapeDtypeStruct((s, HEADS * HEAD_PAD), F32)] * 2 + [jax.ShapeDtypeStruct((s, HEADS * HEAD), F32)],
        compiler_params=_params(("parallel", "arbitrary")),
    )(jj, ii, q, k, v, do, o, lse)


def _rope_tables(positions):
    half = ROPE // 2
    inv_freq = ROPE_BASE ** (-jnp.arange(half, dtype=F32) / half)
    ang = positions.astype(F32)[:, None] * inv_freq
    cos, sin = jnp.cos(ang), jnp.sin(ang)
    return jnp.concatenate([cos] * 4, axis=1), jnp.concatenate([-sin, sin] * 2, axis=1)


def _loss_and_grad(y, target, name):
    s = y.shape[0]
    ts = min(ROW_TILE, s)

    def body(y_ref, t_ref, dy_ref, l_ref):
        e = y_ref[...] - t_ref[...]
        dy_ref[...] = e * (1.0 / D)
        part = jnp.sum(jnp.sum(e * e, axis=-1, keepdims=True) * (0.5 / D), axis=0, keepdims=True)
        part = part * jnp.ones((1, 128), F32)

        @pl.when(pl.program_id(0) == 0)
        def _():
            l_ref[...] = part

        @pl.when(pl.program_id(0) > 0)
        def _():
            l_ref[...] += part

    return pl.pallas_call(
        body, name=name, grid=(s // ts,),
        in_specs=[pl.BlockSpec((ts, D), lambda i: (i, 0))] * 2,
        out_specs=[pl.BlockSpec((ts, D), lambda i: (i, 0)), pl.BlockSpec((1, 128), lambda i: (0, 0))],
        out_shape=[jax.ShapeDtypeStruct((s, D), F32), jax.ShapeDtypeStruct((1, 128), F32)],
        compiler_params=_params(("arbitrary",)),
    )(y, target)


ANY = pl.BlockSpec(memory_space=pl.ANY)


def _all_gather(shard, name):
    def body(x_ref, out_ref, send_sems, recv_sems, local_sem):
        x, y, c = lax.axis_index("x"), lax.axis_index("y"), lax.axis_index("c")
        me, sibling = (x, y, c), (x, y, 1 - c)
        chips = [(1 - x, y), (x, 1 - y), (1 - x, 1 - y)]

        def rows(px, py, pc):
            return out_ref.at[4 * px + 2 * py + pc]

        def copy(k, block, to, src=None):
            return pltpu.make_async_remote_copy(
                src_ref=rows(*block) if src is None else src, dst_ref=rows(*block),
                send_sem=send_sems.at[k], recv_sem=recv_sems.at[k], device_id=to, device_id_type=MESH)

        mine = pltpu.make_async_copy(x_ref, rows(*me), local_sem)
        mine.start()
        first = [copy(0, me, sibling, src=x_ref)]
        first += [copy(1 + j, me, (*chip, c), src=x_ref) for j, chip in enumerate(chips)]
        for cp in first:
            cp.start()
        passed = [copy(4 + j, (*chip, c), sibling) for j, chip in enumerate(chips)]
        for j, chip in enumerate(chips):
            copy(1 + j, (*chip, c), me).wait_recv()
            passed[j].start()
        copy(0, sibling, me).wait_recv()
        for j, chip in enumerate(chips):
            copy(4 + j, (*chip, 1 - c), me).wait_recv()
        for cp in first + passed:
            cp.wait_send()
        mine.wait()

    return pl.pallas_call(
        body, name=name, out_shape=jax.ShapeDtypeStruct((N_DEV,) + shard.shape, shard.dtype),
        in_specs=[ANY], out_specs=ANY,
        scratch_shapes=[pltpu.SemaphoreType.DMA((7,)), pltpu.SemaphoreType.DMA((7,)), pltpu.SemaphoreType.DMA],
    )(shard)


def _exchange(blocks, name):
    def body(x_ref, out_ref, send_sems, recv_sems, local_sem):
        x, y, c = lax.axis_index("x"), lax.axis_index("y"), lax.axis_index("c")
        me = 4 * x + 2 * y + c
        mine = pltpu.make_async_copy(x_ref.at[me], out_ref.at[me], local_sem)
        mine.start()
        copies = []
        for k in range(1, N_DEV):
            px = 1 - x if k & 4 else x
            py = 1 - y if k & 2 else y
            pc = 1 - c if k & 1 else c
            peer = 4 * px + 2 * py + pc
            cp = pltpu.make_async_remote_copy(
                src_ref=x_ref.at[peer], dst_ref=out_ref.at[me], send_sem=send_sems.at[k - 1],
                recv_sem=recv_sems.at[k - 1], device_id=(px, py, pc), device_id_type=MESH)
            cp.start()
            copies.append((cp, pltpu.make_async_remote_copy(
                src_ref=x_ref.at[peer], dst_ref=out_ref.at[peer], send_sem=send_sems.at[k - 1],
                recv_sem=recv_sems.at[k - 1], device_id=(px, py, pc), device_id_type=MESH)))
        for cp, landing in copies:
            landing.wait_recv()
        for cp, landing in copies:
            cp.wait_send()
        mine.wait()

    return pl.pallas_call(
        body, name=name, out_shape=jax.ShapeDtypeStruct(blocks.shape, blocks.dtype),
        in_specs=[ANY], out_specs=ANY,
        scratch_shapes=[pltpu.SemaphoreType.DMA((7,)), pltpu.SemaphoreType.DMA((7,)), pltpu.SemaphoreType.DMA],
    )(blocks)


def _adamw(parts, w, m, v, name, tr=128):
    n, r, wd = parts.shape
    tr = tr if r % tr == 0 else r

    def body(p_ref, w_ref, m_ref, v_ref, g_ref, d_ref, nm_ref, nv_ref):
        g = p_ref[0].astype(F32)
        for k in range(1, n):
            g = g + p_ref[k].astype(F32)
        m_new = B1 * m_ref[...] + (1.0 - B1) * g
        v_new = B2 * v_ref[...] + (1.0 - B2) * (g * g)
        m_hat = m_new / (1.0 - B1 ** STEP)
        v_hat = v_new / (1.0 - B2 ** STEP)
        g_ref[...] = g
        d_ref[...] = -LR * (m_hat / (jnp.sqrt(v_hat) + ADAM_EPS) + WD * w_ref[...])
        nm_ref[...] = m_new
        nv_ref[...] = v_new

    blk = pl.BlockSpec((tr, wd), lambda i: (i, 0))
    return pl.pallas_call(
        body, name=name, grid=(r // tr,),
        in_specs=[pl.BlockSpec((n, tr, wd), lambda i: (0, i, 0)), blk, blk, blk],
        out_specs=[blk] * 4, out_shape=[jax.ShapeDtypeStruct((r, wd), F32)] * 4,
        compiler_params=_params(("parallel",)),
    )(parts, w, m, v)


def _outer8(ct, dm, name):
    k, n = ct.shape[0], dm.shape[1]

    def body(c_ref, d_ref, o_ref):
        cv, dv = c_ref[...], d_ref[...]
        acc = cv[:, 0:1] * dv[0:1, :]
        for s in range(1, N_DEV):
            acc = acc + cv[:, s:s + 1] * dv[s:s + 1, :]
        o_ref[...] = acc

    tk = 256
    return pl.pallas_call(
        body, name=name, grid=(k // tk,),
        in_specs=[pl.BlockSpec((tk, N_DEV), lambda i: (i, 0)), pl.BlockSpec((N_DEV, n), lambda i: (0, 0))],
        out_specs=pl.BlockSpec((tk, n), lambda i: (i, 0)), out_shape=jax.ShapeDtypeStruct((k, n), F32),
        compiler_params=_params(("parallel",)),
    )(ct, dm)


FULL = (0, D)
C128 = (0, 128)
HEAD_NOPE = [(h * HEAD_PAD, NOPE) for h in range(HEADS)]
HEAD_ROPE = [(h * HEAD_PAD + NOPE, 128) for h in range(HEADS)]
HEAD_ALL = [(h * HEAD_PAD, HEAD_PAD) for h in range(HEADS)]
HEAD_V = [(h * HEAD, HEAD) for h in range(HEADS)]


def _modulate(x, gain, scale, shift):
    return _rowwise_fwd(_modulate_fn, [(x, FULL)], [gain, scale, shift], [(D, BF16, FULL)], "modulate")[0]


def _modulate_bwd(x, gain, scale, shift, dh, dx_in):
    return _rowwise_bwd(_modulate_fn, [(x, FULL)], [gain, scale, shift], [(D, BF16, FULL)], [dh], [(0, FULL)],
                        [(D, F32)], "modulate_bwd", add={0: dx_in})


def _residual(x, y, gm):
    return _rowwise_fwd(_resgate_fn, [(x, FULL), (y, FULL)], [gm], [(D, F32, FULL)], "residual")[0]


def _residual_bwd(y, gm, dxn):
    return _rowwise_bwd(_gate_only_fn, [(y, FULL)], [gm], [(D, F32, FULL)], [dxn], [(0, FULL)], [(D, BF16)],
                        "residual_bwd")


def _ffn_fwd(x, p):
    h = _modulate(x, p["gain"], p["scale"], p["shift"])
    gate, up, act = _ffn_in(h, p["wg"], p["wu"], "ffn_in")
    y = _mm(act, p["wo"], "nn", "ffn_out")
    return _residual(x, y, p["gm"]), dict(x=x, h=h, gate=gate, up=up, act=act, y=y)


def _ffn_bwd(t, p, dxn):
    dy, dgm = _residual_bwd(t["y"], p["gm"], dxn)
    dgate, dup = _ffn_bwd_act(dy, p["wo"], t["gate"], t["up"], "ffn_bwd_act")
    dwo = _mm(t["act"], dy, "tn", "ffn_dwo")
    dh = _matmul([(dgate, p["wg"]), (dup, p["wu"])], "nt", "ffn_dh")
    dwg = _mm(t["h"], dgate, "tn", "ffn_dwi")
    dwu = _mm(t["h"], dup, "tn", "ffn_dwi")
    dx, dgain, dscale, dshift = _modulate_bwd(t["x"], p["gain"], p["scale"], p["shift"], dh, dxn)
    return dx, dict(gain=dgain, scale=dscale, shift=dshift, gm=dgm, wg=dwg, wu=dwu, wo=dwo)


def _pad128(t):
    return jnp.pad(t, ((0, 0), (0, 128 - t.shape[1])))


def _gdn_fwd(x, p):
    s = x.shape[0]
    h = _modulate(x, p["gain"], p["scale"], p["shift"])
    pm = _mm(h, p["w_main"], "nn", "gdn_proj")
    tail = _mm(h, p["w_tail"], "nn", "gdn_proj_tail")
    qkv = _gdn_conv_fwd(pm, p["conv_w"], "gdn_conv")
    beta, g = _rowwise_fwd(_gdn_gates_fn, [(tail, C128), (tail, (128, 128))], [p["a_log"], p["dt_bias"]],
                           [(128, F32, C128)] * 2, "gdn_gates")
    gt = g[:, :HEADS].reshape(s // CHUNK, CHUNK, HEADS).transpose(0, 2, 1)
    o, states = _gdn_scan_fwd(qkv, g, beta, gt, "gdn_scan")
    on, = _rowwise_fwd(_gdn_outnorm_fn, [(o, HEAD_V), (pm, [(3 * D + h_ * HEAD, HEAD) for h_ in range(HEADS)])],
                       [p["norm_g"]], [(D, BF16, HEAD_V)], "gdn_outnorm", groups=HEADS)
    y = _mm(on, p["w_out"], "nn", "mix_out")
    t = dict(x=x, h=h, pm=pm, tail=tail, qkv=qkv, beta=beta, g=g, gt=gt, o=o, states=states, on=on, y=y)
    return _residual(x, y, p["gm"]), t


def _gdn_bwd(t, p, dxn):
    s = dxn.shape[0]
    zc = [(3 * D + h_ * HEAD, HEAD) for h_ in range(HEADS)]
    dy, dgm = _residual_bwd(t["y"], p["gm"], dxn)
    dw_out = _mm(t["on"], dy, "tn", "mix_dwo")
    don = _mm(dy, p["w_out"], "nt", "mix_dout")
    do, dz, dnorm_g = _rowwise_bwd(_gdn_outnorm_fn, [(t["o"], HEAD_V), (t["pm"], zc)], [p["norm_g"]],
                                   [(D, BF16, HEAD_V)], [don], [(0, HEAD_V), (1, HEAD_V)], [(D, F32), (D, F32)],
                                   "gdn_outnorm_bwd", groups=HEADS)
    dq, dk, dv, dg, dbeta, dgt = _gdn_scan_bwd(t["qkv"], t["g"], t["beta"], t["gt"], t["states"], do,
                                               "gdn_scan_bwd")
    dg = dg + _pad128(dgt.transpose(0, 2, 1).reshape(s, HEADS))
    dtail, da_log, ddt = _rowwise_bwd(_gdn_gates_fn, [(t["tail"], C128), (t["tail"], (128, 128))],
                                      [p["a_log"], p["dt_bias"]], [(128, F32, C128)] * 2, [dbeta, dg],
                                      [(0, C128), (0, (128, 128))], [(256, F32)], "gdn_gates_bwd")
    dxs, dcw = [], []
    for part, d in enumerate((dq, dk, dv)):
        dx_, dw_ = _gdn_conv_bwd(t["pm"], p["conv_w"], d, part, "gdn_conv_bwd")
        dxs.append(dx_)
        dcw.append(dw_)
    pieces = dxs + [dz]
    dh = _matmul([(d, p["w_main"]) for d in pieces] + [(dtail, p["w_tail"])], "nt", "gdn_dh",
                 boffs=[0, D, 2 * D, 3 * D, 0], tk=256)
    dw_main = [_mm(t["h"], d, "tn", "gdn_dwi") for d in pieces]
    dw_tail = _mm(t["h"], dtail, "tn", "gdn_dwi_tail")
    dx, dgain, dscale, dshift = _modulate_bwd(t["x"], p["gain"], p["scale"], p["shift"], dh, dxn)
    return dx, dict(gain=dgain, scale=dscale, shift=dshift, gm=dgm, w_main=jnp.concatenate(dw_main, axis=1),
                    w_tail=dw_tail, conv_w=jnp.concatenate(dcw, axis=1), a_log=da_log, dt_bias=ddt,
                    norm_g=dnorm_g, w_out=dw_out)


def _q_rows(q2, cosf, sins):
    return [(q2, HEAD_NOPE), (q2, HEAD_ROPE), (cosf, C128), (sins, C128)]


def _mla_fwd(x, p, kv):
    h = _modulate(x, p["gain"], p["scale"], p["shift"])
    cq = _mm(h, p["w_dq"], "nn", "mla_dq")
    cqn, = _rowwise_fwd(_rms_fn, [(cq, (0, Q_LORA))], [p["q_lora_g"]], [(Q_LORA, BF16, (0, Q_LORA))], "mla_qlora_norm")
    q2 = _mm(cqn, p["w_uq"], "nn", "mla_uq")
    qn, = _rowwise_fwd(_q_norm_rope_fn, _q_rows(q2, kv["cosf"], kv["sins"]), [p["q_gn"], p["q_gr"]],
                       [(HEADS * HEAD_PAD, BF16, HEAD_ALL)], "mla_q_norm", groups=HEADS)
    o, lse = _attn_fwd(qn, kv["kn"], kv["vb"], "mla_attn")
    y = _mm(o, p["w_out"], "nn", "mix_out")
    return _residual(x, y, p["gm"]), dict(x=x, h=h, cq=cq, cqn=cqn, q2=q2, qn=qn, o=o, lse=lse, y=y)


def _mla_bwd(t, p, kv, dxn):
    dy, dgm = _residual_bwd(t["y"], p["gm"], dxn)
    dw_out = _mm(t["o"], dy, "tn", "mix_dwo")
    do = _mm(dy, p["w_out"], "nt", "mix_dout")
    dq, dk, dv = _attn_bwd(t["qn"], kv["kn"], kv["vb"], do, t["o"], t["lse"], "mla_attn_bwd")
    dq2, dq_gn, dq_gr = _rowwise_bwd(_q_norm_rope_fn, _q_rows(t["q2"], kv["cosf"], kv["sins"]), [p["q_gn"], p["q_gr"]],
                                     [(HEADS * HEAD_PAD, BF16, HEAD_ALL)], [dq],
                                     [(0, HEAD_NOPE), (0, HEAD_ROPE), None, None], [(HEADS * HEAD_PAD, F32)],
                                     "mla_q_norm_bwd", groups=HEADS)
    dw_uq = _mm(t["cqn"], dq2, "tn", "mla_dwuq")
    dcqn = _mm(dq2, p["w_uq"], "nt", "mla_dcq")
    dcq, dq_lora_g = _rowwise_bwd(_rms_fn, [(t["cq"], (0, Q_LORA))], [p["q_lora_g"]], [(Q_LORA, BF16, (0, Q_LORA))],
                                  [dcqn], [(0, (0, Q_LORA))], [(Q_LORA, F32)], "mla_qlora_norm_bwd")
    dw_dq = _mm(t["h"], dcq, "tn", "mla_dwdq")
    dh = _mm(dcq, p["w_dq"], "nt", "mla_dh")
    dx, dgain, dscale, dshift = _modulate_bwd(t["x"], p["gain"], p["scale"], p["shift"], dh, dxn)
    grads = dict(gain=dgain, scale=dscale, shift=dshift, gm=dgm, w_dq=dw_dq, q_lora_g=dq_lora_g, w_uq=dw_uq,
                 q_gn=dq_gn, q_gr=dq_gr, w_out=dw_out)
    return dx, grads, dk, dv


def _k_rows(kvp, ckv, cosf, sins):
    return [(kvp, HEAD_NOPE), (kvp, HEAD_ROPE), (ckv, (KV_LORA, 128)), (cosf, C128), (sins, C128)]


def _kv_fwd(x, p, cosf, sins):
    h = _modulate(x, p["gain"], p["scale"], p["shift"])
    ckv = _mm(h, p["w_dkv"], "nn", "kv_down")
    lat, = _rowwise_fwd(_rms_fn, [(ckv, (0, KV_LORA))], [p["kv_g"]], [(KV_LORA, BF16, (0, KV_LORA))], "kv_norm")
    kvp = _mm(lat, p["w_ukv"], "nn", "kv_up")
    kn, vb = _rowwise_fwd(_k_norm_rope_fn, _k_rows(kvp, ckv, cosf, sins), [p["k_gn"], p["k_gr"]],
                          [(HEADS * HEAD_PAD, BF16, HEAD_ALL), (HEADS * HEAD, BF16, HEAD_V)], "kv_k_norm",
                          groups=HEADS)
    return dict(x=x, h=h, ckv=ckv, lat=lat, kvp=kvp, kn=kn, vb=vb, cosf=cosf, sins=sins)


def _kv_bwd(t, p, dk, dv, dx_in):
    dkvp, drope, dk_gn, dk_gr = _rowwise_bwd(
        _k_norm_rope_fn, _k_rows(t["kvp"], t["ckv"], t["cosf"], t["sins"]), [p["k_gn"], p["k_gr"]],
        [(HEADS * HEAD_PAD, BF16, HEAD_ALL), (HEADS * HEAD, BF16, HEAD_V)], [dk, dv],
        [(0, HEAD_NOPE), (0, HEAD_ROPE), (1, C128), None, None], [(HEADS * HEAD_PAD, F32), (128, F32)],
        "kv_k_norm_bwd", groups=HEADS)
    dw_ukv = _mm(t["lat"], dkvp, "tn", "kv_dwukv")
    dlat = _mm(dkvp, p["w_ukv"], "nt", "kv_dlat")
    dckv, dkv_g = _rowwise_bwd(_rms_fn, [(t["ckv"], (0, KV_LORA))], [p["kv_g"]], [(KV_LORA, BF16, (0, KV_LORA))],
                               [dlat], [(0, (0, KV_LORA))], [(KV_LORA, F32)], "kv_norm_bwd")
    dw_dkv = jnp.concatenate([_mm(t["h"], dckv, "tn", "kv_dwdkv"), _mm(t["h"], drope, "tn", "kv_dwdkv_rope")], axis=1)
    dh = _matmul([(dckv, p["w_dkv"]), (drope, p["w_dkv"])], "nt", "kv_dh", boffs=[0, KV_LORA], tk=128)
    dx, dgain, dscale, dshift = _modulate_bwd(t["x"], p["gain"], p["scale"], p["shift"], dh, dx_in)
    return dx, dict(gain=dgain, scale=dscale, shift=dshift, w_dkv=dw_dkv, kv_g=dkv_g, w_ukv=dw_ukv, k_gn=dk_gn,
                    k_gr=dk_gr)


WEIGHTS = ["ada_w", "ada_b", "norm_g", "ffn_w_in", "ffn_w_out", "gdn_w_in", "gdn_conv_w", "gdn_a_log", "gdn_dt_bias",
           "gdn_norm_g", "gdn_w_out", "kv_ada_w", "kv_ada_b", "kv_norm_g", "mla_w_dkv", "mla_kv_norm_g", "mla_w_ukv",
           "mla_k_norm_g", "mla_w_dq", "mla_q_lora_norm_g", "mla_w_uq", "mla_q_norm_g", "mla_w_out"]
EXCHANGED = {"ffn_w_in": (8192, 704), "ffn_w_out": (2816, 1024), "gdn_w_in": (2048, 514), "gdn_w_out": (256, 1024),
             "mla_w_dkv": (128, 320), "mla_w_ukv": (256, 256), "mla_w_dq": (256, 384), "mla_w_uq": (768, 192),
             "mla_w_out": (256, 1024)}
SMALL = [("ada_b", 4 * N_MOD * D), ("kv_ada_b", 2 * D), ("norm_g", DEPTH * 3 * D), ("gdn_conv_w", N_A * CONV_K * 3 * D),
         ("gdn_a_log", N_A * HEADS), ("gdn_dt_bias", N_A * HEADS), ("gdn_norm_g", N_A * HEAD), ("kv_norm_g", D),
         ("mla_kv_norm_g", KV_LORA), ("mla_k_norm_g", QK_HEAD), ("mla_q_lora_norm_g", 2 * Q_LORA),
         ("mla_q_norm_g", 2 * QK_HEAD)]
SMALL_REPLICATED = [n for n, _ in SMALL if n not in ("norm_g", "gdn_conv_w")]


def _silu_fn(g, t):
    return (_silu(t),)


def _dup_rope(t):
    return jnp.concatenate([t[..., :NOPE], t[..., NOPE:], t[..., NOPE:]], axis=-1)


def _fold_rope(t):
    return jnp.concatenate([t[..., :NOPE], t[..., NOPE:QK_HEAD] + t[..., QK_HEAD:]], axis=-1)


def _pack(pieces, rows):
    flat = jnp.concatenate([p.reshape(-1).astype(F32) for p in pieces])
    return jnp.pad(flat, (0, rows * 128 - flat.shape[0])).reshape(rows, 128)


def _step(a):
    me = 4 * lax.axis_index("x") + 2 * lax.axis_index("y") + lax.axis_index("c")
    x = a["x"][0]
    cosf, sins = _rope_tables(a["positions"][0])

    n_cw, n_ng = N_A * CONV_K * 3 * HEAD, DEPTH * 3 * HEAD
    small_all = _all_gather(_pack([a["gdn_conv_w"], a["norm_g"], a["c"]], 44), "gather_small").reshape(N_DEV, -1)
    conv_w = small_all[:, :n_cw].reshape(N_DEV, N_A, CONV_K, 3 * HEAD).transpose(1, 2, 0, 3).reshape(N_A, CONV_K, 3 * D)
    norm_g = small_all[:, n_cw:n_cw + n_ng].reshape(N_DEV, DEPTH, 3, HEAD).transpose(1, 2, 0, 3).reshape(DEPTH, 3, D)
    c_all = small_all[:, n_cw + n_ng:n_cw + n_ng + D]

    c_act, = _rowwise_fwd(_silu_fn, [(c_all, FULL)], [], [(D, F32, FULL)], "c_act")
    n_ada = N_MOD * D // N_DEV
    parts = [_mm(c_act, a["ada_w"][l], "nn", "mod_proj") for l in range(DEPTH)]
    parts.append(_mm(c_act, a["kv_ada_w"], "nn", "mod_proj_kv"))
    mod_recv = _exchange(jnp.concatenate(parts, axis=1)[:, None, :], "exchange_mod")[:, 0]
    mod = mod_recv[:, :DEPTH * n_ada].reshape(N_DEV, DEPTH, n_ada).transpose(1, 0, 2).reshape(DEPTH, N_MOD * D)
    mod = (mod + a["ada_b"]).reshape(DEPTH, N_MOD, D)
    kvmod = mod_recv[:, DEPTH * n_ada:].reshape(2 * D) + a["kv_ada_b"]

    def gather(name):
        return _all_gather(a[name].astype(BF16).reshape(EXCHANGED[name]), "gather_" + name)

    g_ffn_in = gather("ffn_w_in").reshape(N_DEV, DEPTH, 2, D, 2 * D_FF // N_DEV)
    g_ffn_out = gather("ffn_w_out").reshape(N_DEV, DEPTH, 2, D_FF // N_DEV, D)
    g_gdn_in = gather("gdn_w_in").reshape(N_DEV, N_A, D, 514)
    g_gdn_out = gather("gdn_w_out").reshape(N_DEV, N_A, D // N_DEV, D)
    g_dkv = gather("mla_w_dkv")
    g_ukv = gather("mla_w_ukv")
    g_dq = gather("mla_w_dq").reshape(N_DEV, 2, D // N_DEV, Q_LORA)
    g_uq = gather("mla_w_uq").reshape(N_DEV, 2, Q_LORA, QK_HEAD)
    g_mo = gather("mla_w_out").reshape(N_DEV, 2, D // N_DEV, D)

    def row(v):
        return v[None]

    def ffn_params(l, i):
        w = g_ffn_in[:, l, i]
        k = 0 if i == 0 else 6
        return dict(gain=row(norm_g[l, 0 if i == 0 else 2]), shift=row(mod[l, k]), scale=row(mod[l, k + 1]),
                    gm=0.5 * row(mod[l, k + 2]),
                    wg=w[:N_DEV // 2].transpose(1, 0, 2).reshape(D, D_FF),
                    wu=w[N_DEV // 2:].transpose(1, 0, 2).reshape(D, D_FF),
                    wo=g_ffn_out[:, l, i].reshape(D_FF, D))

    def gdn_params(l):
        w = g_gdn_in[:, l].transpose(1, 0, 2).reshape(D, 4 * D + 2 * HEADS)
        pad = lambda t: jnp.pad(t, ((0, 0), (0, 128 - HEADS)))
        return dict(gain=row(norm_g[l, 1]), shift=row(mod[l, 3]), scale=row(mod[l, 4]), gm=row(mod[l, 5]),
                    w_main=w[:, :4 * D],
                    w_tail=jnp.concatenate([pad(w[:, 4 * D:4 * D + HEADS]), pad(w[:, 4 * D + HEADS:])], axis=1),
                    conv_w=conv_w[l], a_log=_pad128(row(a["gdn_a_log"][l])), dt_bias=_pad128(row(a["gdn_dt_bias"][l])),
                    norm_g=row(a["gdn_norm_g"][l]), w_out=g_gdn_out[:, l].reshape(D, D))

    def mla_params(l):
        j = l - N_A
        uq = g_uq[:, j].transpose(1, 0, 2)
        qg = _dup_rope(a["mla_q_norm_g"][j])
        return dict(gain=row(norm_g[l, 1]), shift=row(mod[l, 3]), scale=row(mod[l, 4]), gm=row(mod[l, 5]),
                    w_dq=g_dq[:, j].reshape(D, Q_LORA), q_lora_g=row(a["mla_q_lora_norm_g"][j]),
                    w_uq=_dup_rope(uq).reshape(Q_LORA, HEADS * HEAD_PAD), q_gn=row(qg[:NOPE]), q_gr=row(qg[NOPE:]),
                    w_out=g_mo[:, j].reshape(D, D))

    w_dkv = g_dkv.reshape(D, KV_LORA + ROPE)
    kg = _dup_rope(a["mla_k_norm_g"])
    kv_p = dict(gain=row(a["kv_norm_g"]), shift=row(kvmod[:D]), scale=row(kvmod[D:]),
                w_dkv=jnp.concatenate([w_dkv, w_dkv[:, KV_LORA:]], axis=1), kv_g=row(a["mla_kv_norm_g"]),
                w_ukv=g_ukv.transpose(1, 0, 2).reshape(KV_LORA, HEADS * 2 * HEAD), k_gn=row(kg[:NOPE]), k_gr=row(kg[NOPE:]))

    tapes, kv = [], None
    for l in range(DEPTH):
        p1, pm_, p2 = ffn_params(l, 0), (gdn_params(l) if l < N_A else mla_params(l)), ffn_params(l, 1)
        x, t1 = _ffn_fwd(x, p1)
        x, tm_ = _gdn_fwd(x, pm_) if l < N_A else _mla_fwd(x, pm_, kv)
        x, t2 = _ffn_fwd(x, p2)
        tapes.append((p1, t1, pm_, tm_, p2, t2))
        if l == N_A - 1:
            kv = _kv_fwd(x, kv_p, cosf, sins)
    dx, loss_blk = _loss_and_grad(x, a["loss_target"][0], "loss")
    loss = lax.psum(loss_blk[0, 0], ("x", "y", "c"))

    grads = [None] * DEPTH
    dk_sum = dv_sum = kv_grads = None
    for l in reversed(range(DEPTH)):
        p1, t1, pm_, tm_, p2, t2 = tapes[l]
        if l == N_A - 1:
            dx, kv_grads = _kv_bwd(kv, kv_p, dk_sum, dv_sum, dx)
        dx, g2 = _ffn_bwd(t2, p2, dx)
        if l < N_A:
            dx, gm_ = _gdn_bwd(tm_, pm_, dx)
        else:
            dx, gm_, dk, dv = _mla_bwd(tm_, pm_, kv, dx)
            dk_sum = dk if dk_sum is None else dk_sum + dk
            dv_sum = dv if dv_sum is None else dv_sum + dv
        dx, g1 = _ffn_bwd(t1, p1, dx)
        grads[l] = (g1, gm_, g2)

    def by_cols(g, n):
        return g.reshape(g.shape[0], -1, n).transpose(1, 0, 2)

    stack = {}
    n_in = 2 * D_FF // N_DEV
    ffn_in = [[jnp.concatenate([by_cols(g["wg"], n_in), by_cols(g["wu"], n_in)], axis=0)
               for g in (grads[l][0], grads[l][2])] for l in range(DEPTH)]
    stack["ffn_w_in"] = jnp.stack([jnp.stack(r) for r in ffn_in]).transpose(2, 0, 1, 3, 4)
    ffn_out = [[g["wo"].reshape(N_DEV, D_FF // N_DEV, D) for g in (grads[l][0], grads[l][2])] for l in range(DEPTH)]
    stack["ffn_w_out"] = jnp.stack([jnp.stack(r) for r in ffn_out]).transpose(2, 0, 1, 3, 4)
    gdn_in = [jnp.concatenate([grads[l][1]["w_main"], grads[l][1]["w_tail"][:, :HEADS],
                               grads[l][1]["w_tail"][:, 128:128 + HEADS]], axis=1) for l in range(N_A)]
    stack["gdn_w_in"] = jnp.stack([by_cols(g, 514) for g in gdn_in]).transpose(1, 0, 2, 3)
    stack["gdn_w_out"] = jnp.stack([grads[l][1]["w_out"].reshape(N_DEV, D // N_DEV, D) for l in range(N_A)]).transpose(1, 0, 2, 3)
    d_dkv = kv_grads["w_dkv"]
    stack["mla_w_dkv"] = jnp.concatenate([d_dkv[:, :KV_LORA], d_dkv[:, KV_LORA:KV_LORA + ROPE] + d_dkv[:, KV_LORA + ROPE:]],
                                         axis=1).reshape(N_DEV, D // N_DEV, KV_LORA + ROPE)
    stack["mla_w_ukv"] = by_cols(kv_grads["w_ukv"], 2 * HEAD)
    mla = [grads[l][1] for l in range(N_A, DEPTH)]
    stack["mla_w_dq"] = jnp.stack([g["w_dq"].reshape(N_DEV, D // N_DEV, Q_LORA) for g in mla]).transpose(1, 0, 2, 3)
    stack["mla_w_uq"] = jnp.stack([_fold_rope(g["w_uq"].reshape(Q_LORA, HEADS, HEAD_PAD)).transpose(1, 0, 2)
                                   for g in mla]).transpose(1, 0, 2, 3)
    stack["mla_w_out"] = jnp.stack([g["w_out"].reshape(N_DEV, D // N_DEV, D) for g in mla]).transpose(1, 0, 2, 3)

    out = {}
    for name, shape2d in EXCHANGED.items():
        recv = _exchange(stack[name].astype(BF16).reshape((N_DEV,) + shape2d), "exchange_" + name)
        out[name] = _adamw(recv, a[name].reshape(shape2d), a["m_" + name].reshape(shape2d),
                           a["v_" + name].reshape(shape2d), "adamw")

    def dmod(l):
        g1, gm_, g2 = grads[l]
        return jnp.concatenate([g1["shift"], g1["scale"], 0.5 * g1["gm"], gm_["shift"], gm_["scale"], gm_["gm"],
                                g2["shift"], g2["scale"], 0.5 * g2["gm"]], axis=1)

    gdn = [grads[l][1] for l in range(N_A)]
    small = {
        "ada_b": jnp.concatenate([dmod(l) for l in range(DEPTH)], axis=0),
        "kv_ada_b": jnp.concatenate([kv_grads["shift"], kv_grads["scale"]], axis=1),
        "norm_g": jnp.stack([jnp.concatenate([grads[l][0]["gain"], grads[l][1]["gain"], grads[l][2]["gain"]], axis=0)
                             for l in range(DEPTH)]),
        "gdn_conv_w": jnp.stack([g["conv_w"] for g in gdn]),
        "gdn_a_log": jnp.stack([g["a_log"][0, :HEADS] for g in gdn]),
        "gdn_dt_bias": jnp.stack([g["dt_bias"][0, :HEADS] for g in gdn]),
        "gdn_norm_g": jnp.stack([g["norm_g"][0] for g in gdn]),
        "kv_norm_g": kv_grads["gain"],
        "mla_kv_norm_g": kv_grads["kv_g"],
        "mla_k_norm_g": _fold_rope(jnp.concatenate([kv_grads["k_gn"], kv_grads["k_gr"]], axis=1)),
        "mla_q_lora_norm_g": jnp.stack([g["q_lora_g"][0] for g in mla]),
        "mla_q_norm_g": jnp.stack([_fold_rope(jnp.concatenate([g["q_gn"], g["q_gr"]], axis=1))[0] for g in mla]),
    }
    rows = 616
    assert sum(n for _, n in SMALL) <= rows * 128 and all(small[n].size == k for n, k in SMALL)
    small_recv = _all_gather(_pack([small[n] for n, _ in SMALL], rows), "gather_small_grads")
    zero = lambda n, k: jnp.zeros((k,), F32)
    packed = {pre: _pack([a[pre + n] if n in SMALL_REPLICATED else zero(n, k) for n, k in SMALL], rows)
              for pre in ("", "m_", "v_")}
    res = _adamw(small_recv, packed[""], packed["m_"], packed["v_"], "adamw_small")
    offs = {}
    o = 0
    for n, k in SMALL:
        offs[n] = o
        o += k
    for n, k in SMALL:
        if n in SMALL_REPLICATED:
            out[n] = [r.reshape(-1)[offs[n]:offs[n] + k] for r in res]
    gsum = res[0].reshape(-1)
    g_norm = lax.dynamic_slice_in_dim(gsum[offs["norm_g"]:offs["norm_g"] + DEPTH * 3 * D].reshape(DEPTH * 3, D),
                                      me * HEAD, HEAD, axis=1)
    g_conv = lax.dynamic_slice_in_dim(
        gsum[offs["gdn_conv_w"]:offs["gdn_conv_w"] + N_A * CONV_K * 3 * D].reshape(N_A * CONV_K, 3 * D),
        me * 3 * HEAD, 3 * HEAD, axis=1)
    res2 = _adamw(_pack([g_norm, g_conv], 36)[None], *[_pack([a[pre + "norm_g"], a[pre + "gdn_conv_w"]], 36)
                                                      for pre in ("", "m_", "v_")], "adamw_small")
    out["norm_g"] = [r.reshape(-1)[:n_ng] for r in res2]
    out["gdn_conv_w"] = [r.reshape(-1)[n_ng:n_ng + n_cw] for r in res2]

    c_act_t = c_act.T
    all_small = small_recv.reshape(N_DEV, -1)
    dmod_all = all_small[:, :DEPTH * N_MOD * D].reshape(N_DEV, DEPTH, N_MOD * D)
    dmod_mine = lax.dynamic_slice_in_dim(dmod_all, me * n_ada, n_ada, axis=2)
    g_ada = jnp.concatenate([_outer8(c_act_t, dmod_mine[:, l], "ada_grad") for l in range(DEPTH)], axis=0)
    out["ada_w"] = _adamw(g_ada[None], *[a[pre + "ada_w"].reshape(DEPTH * D, n_ada) for pre in ("", "m_", "v_")], "adamw")
    dkv_all = all_small[:, offs["kv_ada_b"]:offs["kv_ada_b"] + 2 * D]
    g_kv = _outer8(c_act_t, lax.dynamic_slice_in_dim(dkv_all, me * (2 * D // N_DEV), 2 * D // N_DEV, axis=1), "ada_grad")
    out["kv_ada_w"] = _adamw(g_kv[None], *[a[pre + "kv_ada_w"] for pre in ("", "m_", "v_")], "adamw")

    result = [loss, dx[None]]
    for k in range(4):
        result += [out[n][k].reshape(a[n].shape) for n in WEIGHTS]
    return tuple(result)


def kernel(x, c, positions, ada_w, ada_b, norm_g, ffn_w_in, ffn_w_out, gdn_w_in, gdn_conv_w, gdn_a_log, gdn_dt_bias, gdn_norm_g, gdn_w_out, kv_ada_w, kv_ada_b, kv_norm_g, mla_w_dkv, mla_kv_norm_g, mla_w_ukv, mla_k_norm_g, mla_w_dq, mla_q_lora_norm_g, mla_w_uq, mla_q_norm_g, mla_w_out, loss_target, m_ada_w, m_ada_b, m_norm_g, m_ffn_w_in, m_ffn_w_out, m_gdn_w_in, m_gdn_conv_w, m_gdn_a_log, m_gdn_dt_bias, m_gdn_norm_g, m_gdn_w_out, m_kv_ada_w, m_kv_ada_b, m_kv_norm_g, m_mla_w_dkv, m_mla_kv_norm_g, m_mla_w_ukv, m_mla_k_norm_g, m_mla_w_dq, m_mla_q_lora_norm_g, m_mla_w_uq, m_mla_q_norm_g, m_mla_w_out, v_ada_w, v_ada_b, v_norm_g, v_ffn_w_in, v_ffn_w_out, v_gdn_w_in, v_gdn_conv_w, v_gdn_a_log, v_gdn_dt_bias, v_gdn_norm_g, v_gdn_w_out, v_kv_ada_w, v_kv_ada_b, v_kv_norm_g, v_mla_w_dkv, v_mla_kv_norm_g, v_mla_w_ukv, v_mla_k_norm_g, v_mla_w_dq, v_mla_q_lora_norm_g, v_mla_w_uq, v_mla_q_norm_g, v_mla_w_out):
    return _step(dict(locals()))
```

```python
import functools
import math

import jax
import jax.numpy as jnp
from jax import lax
from jax.experimental import pallas as pl
from jax.experimental.pallas import tpu as pltpu

F32 = jnp.float32
BF16 = jnp.bfloat16

N_DEV = 8
D = 1024
D_FF = 2816
DEPTH = 4
N_A = 2
N_MOD = 9
HEADS = 8
HEAD = 128
CHUNK = 64
CONV_K = 4
KV_LORA = 256
Q_LORA = 384
NOPE = 128
ROPE = 64
QK_HEAD = NOPE + ROPE
HEAD_PAD = 256
ROPE_BASE = 10000.0
EPS = 1e-6
LR, B1, B2, ADAM_EPS, WD, STEP = 0.001, 0.9, 0.999, 1e-08, 0.01, 10

VMEM_LIMIT = 48 * 1024 * 1024
ROW_TILE = 256
MESH = pl.DeviceIdType.MESH

_NN = (((1,), (0,)), ((), ()))
_NT = (((1,), (1,)), ((), ()))
_TN = (((0,), (0,)), ((), ()))
_DIMS = {"nn": _NN, "nt": _NT, "tn": _TN}


def _params(dims=None):
    return pltpu.CompilerParams(dimension_semantics=dims, vmem_limit_bytes=VMEM_LIMIT)


def _tile(n, target):
    for t in range(target - target % 128, 0, -128):
        if n % t == 0:
            return t
    return n


def _matmul(pairs, form, name, out_dtype=F32, tm=1408, tn=1408, tk=1408, boffs=None):
    a0, b0 = pairs[0]
    if form == "nn":
        m, n = a0.shape[0], b0.shape[1]
        ks = [a.shape[1] for a, _ in pairs]
    elif form == "nt":
        m, n = a0.shape[0], b0.shape[0]
        ks = [a.shape[1] for a, _ in pairs]
    else:
        m, n = a0.shape[1], b0.shape[1]
        ks = [a.shape[0] for a, _ in pairs]
    tm, tn = _tile(m, tm), _tile(n, tn)
    tks = [_tile(k, tk) for k in ks]
    boffs = boffs or [0] * len(pairs)
    assert m % tm == 0 and n % tn == 0 and all(o % t == 0 for o, t in zip(boffs, tks)), (name, m, n, ks)
    steps = [k // t for k, t in zip(ks, tks)]
    starts = [sum(steps[:p]) for p in range(len(pairs))]
    nk = sum(steps)

    def kidx(p, k):
        return jnp.clip(k - starts[p], 0, steps[p] - 1)

    in_specs, args = [], []
    for p, (a, b) in enumerate(pairs):
        t = tks[p]
        if form == "tn":
            in_specs.append(pl.BlockSpec((t, tm), lambda i, j, k, p=p: (kidx(p, k), i)))
            in_specs.append(pl.BlockSpec((t, tn), lambda i, j, k, p=p: (kidx(p, k), j)))
        elif form == "nn":
            in_specs.append(pl.BlockSpec((tm, t), lambda i, j, k, p=p: (i, kidx(p, k))))
            in_specs.append(pl.BlockSpec((t, tn), lambda i, j, k, p=p: (kidx(p, k), j)))
        else:
            in_specs.append(pl.BlockSpec((tm, t), lambda i, j, k, p=p: (i, kidx(p, k))))
            in_specs.append(pl.BlockSpec((tn, t), lambda i, j, k, p=p, o=boffs[p] // t: (j, kidx(p, k) + o)))
        args += [a, b]
    dims = _DIMS[form]
    npairs = len(pairs)

    def body(*refs):
        o_ref = refs[2 * npairs]
        k = pl.program_id(2)

        def prod(p):
            return lax.dot_general(refs[2 * p][...].astype(BF16), refs[2 * p + 1][...].astype(BF16), dims,
                                   preferred_element_type=F32)

        if nk == 1:
            o_ref[...] = prod(0).astype(o_ref.dtype)
            return
        acc = refs[2 * npairs + 1]

        @pl.when(k == 0)
        def _():
            acc[...] = jnp.zeros_like(acc)

        for p in range(npairs):
            @pl.when((k >= starts[p]) & (k < starts[p] + steps[p]))
            def _(p=p):
                acc[...] += prod(p)

        @pl.when(k == nk - 1)
        def _():
            o_ref[...] = acc[...].astype(o_ref.dtype)

    return pl.pallas_call(
        body, name=name, grid=(m // tm, n // tn, nk), in_specs=in_specs,
        out_specs=pl.BlockSpec((tm, tn), lambda i, j, k: (i, j)),
        out_shape=jax.ShapeDtypeStruct((m, n), out_dtype),
        scratch_shapes=[] if nk == 1 else [pltpu.VMEM((tm, tn), F32)],
        compiler_params=_params(("parallel", "parallel", "arbitrary")),
    )(*args)


def _mm(a, b, form, name, **kw):
    return _matmul([(a, b)], form, name, **kw)


def _cols(spec, g):
    return spec[g] if isinstance(spec, list) else spec


def _rowwise_fwd(fn, rows, pars, outs, name, groups=1, ts=ROW_TILE):
    s = rows[0][0].shape[0]
    ts = min(ts, s)
    assert s % ts == 0
    nr, npar = len(rows), len(pars)

    def body(*refs):
        par_t = [r[...] for r in refs[nr:nr + npar]]
        out_refs = refs[nr + npar:]
        for g in range(groups):
            row_t = []
            for r, (_, spec) in zip(refs[:nr], rows):
                c0, w = _cols(spec, g)
                row_t.append(r[:, c0:c0 + w].astype(F32))
            res = fn(g, *row_t, *par_t)
            for o_ref, val, (_, _, spec) in zip(out_refs, res, outs):
                c0, w = _cols(spec, g)
                o_ref[:, c0:c0 + w] = val.astype(o_ref.dtype)

    return pl.pallas_call(
        body, name=name, grid=(s // ts,),
        in_specs=[pl.BlockSpec((ts, a.shape[1]), lambda i: (i, 0)) for a, _ in rows]
        + [pl.BlockSpec(p.shape, lambda i: (0, 0)) for p in pars],
        out_specs=[pl.BlockSpec((ts, w), lambda i: (i, 0)) for w, _, _ in outs],
        out_shape=[jax.ShapeDtypeStruct((s, w), dt) for w, dt, _ in outs],
        compiler_params=_params(("parallel",)),
    )(*[a for a, _ in rows], *pars)


def _rowwise_bwd(fn, rows, pars, outs, douts, gmap, gshapes, name, groups=1, add=None, par_grads=True,
                 ts=ROW_TILE):
    s = rows[0][0].shape[0]
    ts = min(ts, s)
    assert s % ts == 0
    nr, npar, nout, ng = len(rows), len(pars), len(outs), len(gshapes)
    add = add or {}
    add_keys = sorted(add)

    def body(*refs):
        row_refs = refs[:nr]
        par_refs = refs[nr:nr + npar]
        dout_refs = refs[nr + npar:nr + npar + nout]
        add_refs = refs[nr + npar + nout:nr + npar + nout + len(add_keys)]
        g_refs = refs[nr + npar + nout + len(add_keys):][:ng]
        pg_refs = refs[nr + npar + nout + len(add_keys) + ng:]
        par_t = [r[...] for r in par_refs]
        par_acc = [None] * npar
        shared_acc = {}
        for g in range(groups):
            row_t = []
            for r, (_, spec) in zip(row_refs, rows):
                c0, w = _cols(spec, g)
                row_t.append(r[:, c0:c0 + w].astype(F32))
            cts = []
            for r, (_, _, spec) in zip(dout_refs, outs):
                c0, w = _cols(spec, g)
                cts.append(r[:, c0:c0 + w].astype(F32))
            _, vjp = jax.vjp(lambda *t, g=g: tuple(fn(g, *t)), *row_t, *par_t)
            grads = vjp(tuple(cts))
            for k in range(nr):
                if gmap[k] is None:
                    continue
                gi, spec = gmap[k]
                if isinstance(spec, list) or groups == 1:
                    c0, w = _cols(spec, g)
                    val = grads[k]
                    if gi in add:
                        val = val + add_refs[add_keys.index(gi)][:, c0:c0 + w].astype(F32)
                    g_refs[gi][:, c0:c0 + w] = val.astype(g_refs[gi].dtype)
                else:
                    shared_acc[k] = grads[k] if k not in shared_acc else shared_acc[k] + grads[k]
            if par_grads:
                for k in range(npar):
                    pg = grads[nr + k]
                    par_acc[k] = pg if par_acc[k] is None else par_acc[k] + pg
        for k, val in shared_acc.items():
            gi, (c0, w) = gmap[k]
            assert gi not in add
            g_refs[gi][:, c0:c0 + w] = val.astype(g_refs[gi].dtype)
        if par_grads:
            first = pl.program_id(0) == 0
            for k in range(npar):
                @pl.when(first)
                def _(k=k):
                    pg_refs[k][...] = par_acc[k]

                @pl.when(jnp.logical_not(first))
                def _(k=k):
                    pg_refs[k][...] += par_acc[k]

    out_specs = [pl.BlockSpec((ts, w), lambda i: (i, 0)) for w, _ in gshapes]
    out_shape = [jax.ShapeDtypeStruct((s, w), dt) for w, dt in gshapes]
    if par_grads:
        out_specs += [pl.BlockSpec(p.shape, lambda i: (0, 0)) for p in pars]
        out_shape += [jax.ShapeDtypeStruct(p.shape, F32) for p in pars]
    return pl.pallas_call(
        body, name=name, grid=(s // ts,),
        in_specs=[pl.BlockSpec((ts, a.shape[1]), lambda i: (i, 0)) for a, _ in rows]
        + [pl.BlockSpec(p.shape, lambda i: (0, 0)) for p in pars]
        + [pl.BlockSpec((ts, a.shape[1]), lambda i: (i, 0)) for a in douts]
        + [pl.BlockSpec((ts, add[k].shape[1]), lambda i: (i, 0)) for k in add_keys],
        out_specs=out_specs, out_shape=out_shape,
        compiler_params=_params(("arbitrary",)),
    )(*[a for a, _ in rows], *pars, *douts, *[add[k] for k in add_keys])


def _sigmoid(x):
    return 1.0 / (1.0 + jnp.exp(-x))


def _silu(x):
    return x * _sigmoid(x)


def _softplus(x):
    return jnp.maximum(x, 0.0) + jnp.log(1.0 + jnp.exp(-jnp.abs(x)))


def _rms(t, g, n=None):
    n = n or t.shape[-1]
    return t * lax.rsqrt(jnp.sum(t * t, axis=-1, keepdims=True) / n + EPS) * g


def _modulate_fn(g, x, gain, scale, shift):
    return (_rms(x, gain) * (1.0 + scale) + shift,)


def _resgate_fn(g, x, y, gm):
    return (x + gm * y,)


def _gate_only_fn(g, y, gm):
    return (gm * y,)


def _gdn_gates_fn(g, b_logit, a_logit, a_log, dt_bias):
    gate = -jnp.exp(a_log) * _softplus(a_logit + dt_bias)
    n = gate.shape[0]
    i = lax.broadcasted_iota(jnp.int32, (n, n), 0)
    j = lax.broadcasted_iota(jnp.int32, (n, n), 1)
    tri = (((i // CHUNK) == (j // CHUNK)) & (i >= j)).astype(F32)
    gcum = lax.dot_general(tri, gate, _NN, preferred_element_type=F32, precision=lax.Precision.HIGHEST)
    return _sigmoid(b_logit), gcum


def _gdn_outnorm_fn(g, o, z, gain):
    return (_rms(o, gain) * _silu(z),)


def _rms_fn(g, t, gain):
    return (_rms(t, gain),)


@jax.custom_vjp
def _swap_halves(t):
    return pltpu.roll(t, 32, 1)


_swap_halves.defvjp(lambda t: (pltpu.roll(t, 32, 1), None), lambda _, ct: (pltpu.roll(ct, 96, 1),))


def _head_norm_rope_fn(g, nope, rope, cosf, sins, gain_n, gain_r):
    first = lax.broadcasted_iota(jnp.int32, rope.shape, 1) < ROPE
    ss = jnp.sum(nope * nope, axis=-1, keepdims=True) + jnp.sum(jnp.where(first, rope * rope, 0.0), axis=-1,
                                                                 keepdims=True)
    r = lax.rsqrt(ss / QK_HEAD + EPS)
    tn = nope * r * gain_n
    tr = rope * r * gain_r
    rot = jnp.where(first, tr * cosf + _swap_halves(tr) * sins, 0.0)
    return tn, rot


def _q_norm_rope_fn(g, nope, rope, cosf, sins, gain_n, gain_r):
    tn, rot = _head_norm_rope_fn(g, nope, rope, cosf, sins, gain_n, gain_r)
    return (jnp.concatenate([tn, rot], axis=1),)


def _k_norm_rope_fn(g, nope, val, rope, cosf, sins, gain_n, gain_r):
    tn, rot = _head_norm_rope_fn(g, nope, rope, cosf, sins, gain_n, gain_r)
    return jnp.concatenate([tn, rot], axis=1), val


def _loss_fn(g, y, target):
    e = y - target
    return (jnp.sum(e * e, axis=-1, keepdims=True) * (0.5 / D) * jnp.ones((1, 128), F32),)


def _ffn_in(h, wg, wu, name, tm=1024, tn=256):
    s = h.shape[0]
    tm = min(tm, s)

    def body(h_ref, wg_ref, wu_ref, g_ref, u_ref, a_ref):
        hb = h_ref[...]
        gate = jnp.dot(hb, wg_ref[...], preferred_element_type=F32)
        up = jnp.dot(hb, wu_ref[...], preferred_element_type=F32)
        g_ref[...] = gate.astype(BF16)
        u_ref[...] = up.astype(BF16)
        a_ref[...] = (_silu(gate) * up).astype(BF16)

    spec = pl.BlockSpec((tm, tn), lambda i, j: (i, j))
    return pl.pallas_call(
        body, name=name, grid=(s // tm, D_FF // tn),
        in_specs=[pl.BlockSpec((tm, D), lambda i, j: (i, 0)), pl.BlockSpec((D, tn), lambda i, j: (0, j)),
                  pl.BlockSpec((D, tn), lambda i, j: (0, j))],
        out_specs=[spec, spec, spec], out_shape=[jax.ShapeDtypeStruct((s, D_FF), BF16)] * 3,
        compiler_params=_params(("parallel", "parallel")),
    )(h, wg, wu)


def _ffn_bwd_act(dy, wo, gate, up, name, tm=1024, tn=256):
    s = dy.shape[0]
    tm = min(tm, s)

    def body(dy_ref, wo_ref, g_ref, u_ref, dg_ref, du_ref):
        dact = lax.dot_general(dy_ref[...], wo_ref[...], _NT, preferred_element_type=F32)
        gate = g_ref[...].astype(F32)
        up = u_ref[...].astype(F32)
        sg = _sigmoid(gate)
        dg_ref[...] = (dact * up * (sg * (1.0 + gate * (1.0 - sg)))).astype(BF16)
        du_ref[...] = (dact * (gate * sg)).astype(BF16)

    spec = pl.BlockSpec((tm, tn), lambda i, j: (i, j))
    return pl.pallas_call(
        body, name=name, grid=(s // tm, D_FF // tn),
        in_specs=[pl.BlockSpec((tm, D), lambda i, j: (i, 0)), pl.BlockSpec((tn, D), lambda i, j: (j, 0)), spec, spec],
        out_specs=[spec, spec], out_shape=[jax.ShapeDtypeStruct((s, D_FF), BF16)] * 2,
        compiler_params=_params(("parallel", "parallel")),
    )(dy, wo, gate, up)


def _shift_down(x, d):
    rows = lax.broadcasted_iota(jnp.int32, x.shape, 0)
    return jnp.where(rows >= d, pltpu.roll(x, d, 0), 0.0)


def _shift_up(x, d):
    n = x.shape[0]
    rows = lax.broadcasted_iota(jnp.int32, x.shape, 0)
    return jnp.where(rows < n - d, pltpu.roll(x, n - d, 0), 0.0)


def _conv_post(pre, is_qk):
    a = _silu(pre)
    l2 = a * lax.rsqrt(jnp.sum(a * a, axis=-1, keepdims=True) + EPS)
    return jnp.where(is_qk, l2, a)


def _conv_pre(x, w):
    pre = x * w[CONV_K - 1:CONV_K, :]
    for j in range(CONV_K - 1):
        pre = pre + _shift_down(x, CONV_K - 1 - j) * w[j:j + 1, :]
    return pre


def _gdn_conv_fwd(pm, conv_w, name):
    s = pm.shape[0]
    nblk = 3 * D // HEAD

    def body(x_ref, w_ref, o_ref):
        is_qk = pl.program_id(0) < 2 * HEADS
        o_ref[...] = _conv_post(_conv_pre(x_ref[...], w_ref[...]), is_qk)

    return pl.pallas_call(
        body, name=name, grid=(nblk,),
        in_specs=[pl.BlockSpec((s, HEAD), lambda c: (0, c)), pl.BlockSpec((CONV_K, HEAD), lambda c: (0, c))],
        out_specs=pl.BlockSpec((s, HEAD), lambda c: (0, c)),
        out_shape=jax.ShapeDtypeStruct((s, 3 * D), F32), compiler_params=_params(("parallel",)),
    )(pm, conv_w)


def _gdn_conv_bwd(pm, conv_w, dout, part, name):
    s = pm.shape[0]
    off = part * HEADS

    def body(x_ref, w_ref, d_ref, dx_ref, dw_ref):
        x, w = x_ref[...], w_ref[...]
        _, vjp = jax.vjp(lambda p: _conv_post(p, part < 2), _conv_pre(x, w))
        dpre, = vjp(d_ref[...])
        dx = dpre * w[CONV_K - 1:CONV_K, :]
        rows = [None] * CONV_K
        rows[CONV_K - 1] = jnp.sum(dpre * x, axis=0, keepdims=True)
        for j in range(CONV_K - 1):
            dx = dx + _shift_up(dpre, CONV_K - 1 - j) * w[j:j + 1, :]
            rows[j] = jnp.sum(dpre * _shift_down(x, CONV_K - 1 - j), axis=0, keepdims=True)
        dx_ref[...] = dx
        dw_ref[...] = jnp.concatenate(rows, axis=0)

    return pl.pallas_call(
        body, name=name, grid=(HEADS,),
        in_specs=[pl.BlockSpec((s, HEAD), lambda c: (0, c + off)), pl.BlockSpec((CONV_K, HEAD), lambda c: (0, c + off)),
                  pl.BlockSpec((s, HEAD), lambda c: (0, c))],
        out_specs=[pl.BlockSpec((s, HEAD), lambda c: (0, c)), pl.BlockSpec((CONV_K, HEAD), lambda c: (0, c))],
        out_shape=[jax.ShapeDtypeStruct((s, D), F32), jax.ShapeDtypeStruct((CONV_K, D), F32)],
        compiler_params=_params(("parallel",)),
    )(pm, conv_w, dout)


def _dot3(a, b, dims=_NN):
    ah, bh = a.astype(BF16), b.astype(BF16)
    al, bl = (a - ah.astype(F32)).astype(BF16), (b - bh.astype(F32)).astype(BF16)
    d = lambda u, v: lax.dot_general(u, v, dims, preferred_element_type=F32)
    return d(ah, bh) + (d(ah, bl) + d(al, bh))


def _make_dot(hi):
    def raw(a, b, dims):
        if hi:
            return _dot3(a, b, dims)
        return lax.dot_general(a.astype(BF16), b.astype(BF16), dims, preferred_element_type=F32)

    @functools.partial(jax.custom_vjp, nondiff_argnums=(2,))
    def dot(a, b, form):
        return raw(a, b, _DIMS[form])

    def fwd(a, b, form):
        return raw(a, b, _DIMS[form]), (a, b)

    def bwd(form, res, ct):
        a, b = res
        if form == "nn":
            return raw(ct, b, _NT), raw(a, ct, _TN)
        if form == "nt":
            return raw(ct, b, _NN), raw(ct, a, _TN)
        return raw(b, ct, _NT), raw(a, ct, _NN)

    dot.defvjp(fwd, bwd)
    return dot


_dot = _make_dot(False)
_dot_hi = _make_dot(True)


def _tri_inv_raw(low):
    n = low.shape[0]
    i = lax.broadcasted_iota(jnp.int32, (n, n), 0)
    j = lax.broadcasted_iota(jnp.int32, (n, n), 1)
    eye = (i == j).astype(F32)
    hdot = _dot3
    same16 = (i // 16) == (j // 16)
    neg = jnp.where(same16, -low, 0.0)
    inv = eye + neg
    power = neg
    for _ in range(3):
        power = hdot(power, power)
        inv = hdot(inv, eye + power)
    for blk in (32, 64):
        off = jnp.where(((i // blk) == (j // blk)) & ((i // (blk // 2)) != (j // (blk // 2))), low, 0.0)
        inv = inv - hdot(inv, hdot(off, inv))
    return inv


@jax.custom_vjp
def _tri_inv(low):
    return _tri_inv_raw(low)


def _tri_inv_fwd(low):
    inv = _tri_inv_raw(low)
    return inv, inv


def _tri_inv_bwd(inv, ct):
    return (-_dot3(_dot3(inv, ct, _TN), inv, _NT),)


_tri_inv.defvjp(_tri_inv_fwd, _tri_inv_bwd)


@jax.custom_vjp
def _tri_inv_given(low, inv):
    return inv


_tri_inv_given.defvjp(lambda low, inv: (inv, inv),
                      lambda inv, ct: (_tri_inv_bwd(inv, ct)[0], jnp.zeros_like(inv)))

GROUP = 4
N_GROUPS = HEADS // GROUP
GROWS = GROUP * CHUNK


def _gdn_group(q, k, v, beta, gc, gr, states, inv=None):
    n = q.shape[0]
    i = lax.broadcasted_iota(jnp.int32, (n, n), 0)
    j = lax.broadcasted_iota(jnp.int32, (n, n), 1)
    same = (i // CHUNK) == (j // CHUNK)
    incl, strict = same & (i >= j), same & (i > j)
    qs = q * (HEAD ** -0.5)
    decay = jnp.where(incl, jnp.exp(jnp.where(incl, gc - gr, 0.0)), 0.0)
    kb = k * beta
    eg = jnp.exp(gc)
    prod = _dot(jnp.concatenate([kb, qs], axis=0), k, "nt")
    low = jnp.where(strict, prod[:n] * decay, 0.0)
    attn = jnp.where(incl, prod[n:] * decay, 0.0)
    inv = _tri_inv(low) if inv is None else _tri_inv_given(low, inv)
    sol = _dot_hi(inv, jnp.concatenate([v * beta, kb * eg], axis=1), "nn")
    u, w, qg = sol[:, :HEAD], sol[:, HEAD:], qs * eg
    last = lax.broadcasted_iota(jnp.int32, (CHUNK, 1), 0) == CHUNK - 1
    v_new, o_state, carry = [], [], []
    for h, state in enumerate(states):
        rows = slice(h * CHUNK, (h + 1) * CHUNK)
        ws = _dot(jnp.concatenate([w[rows], qg[rows]], axis=0), state, "nn")
        v_new.append(u[rows] - ws[:CHUNK])
        o_state.append(ws[CHUNK:])
        g_last = jnp.sum(jnp.where(last, gc[rows], 0.0), axis=0, keepdims=True)
        carry.append((g_last, k[rows] * jnp.exp(g_last - gc[rows])))
    o = jnp.concatenate(o_state, axis=0) + _dot(attn, jnp.concatenate(v_new, axis=0), "nn")
    new = tuple(state * jnp.exp(g_last) + _dot(k_dec, vn, "tn")
                for state, (g_last, k_dec), vn in zip(states, carry, v_new))
    return o, new, inv


def _gdn_specs(s, rev):
    nc = s // CHUNK
    at = (lambda n: nc - 1 - n) if rev else (lambda n: n)
    return nc, at, [
        pl.BlockSpec((CHUNK, D), lambda n: (at(n), 0)), pl.BlockSpec((CHUNK, D), lambda n: (at(n), 1)),
        pl.BlockSpec((CHUNK, D), lambda n: (at(n), 2)), pl.BlockSpec((CHUNK, HEAD), lambda n: (at(n), 0)),
        pl.BlockSpec((CHUNK, HEAD), lambda n: (at(n), 0)),
        pl.BlockSpec((None, N_GROUPS, 1, GROWS), lambda n: (at(n), 0, 0, 0))]


def _group_operands(grp, q_ref, k_ref, v_ref, b_blk, gc_blk, gr_blk):
    heads = range(grp * GROUP, (grp + 1) * GROUP)
    stack = lambda ref: jnp.concatenate([ref[:, h * HEAD:(h + 1) * HEAD] for h in heads], axis=0)
    col = lambda blk: jnp.concatenate([blk[:, h:h + 1] for h in heads], axis=0)
    return stack(q_ref), stack(k_ref), stack(v_ref), col(b_blk), col(gc_blk), gr_blk[grp]


def _gdn_scan_fwd(qkv, beta, gcum, grow, name):
    s = qkv.shape[0]
    nc, _, in_specs = _gdn_specs(s, rev=False)

    def body(q_ref, k_ref, v_ref, b_ref, gc_ref, gr_ref, o_ref, st_ref, inv_ref, state):
        @pl.when(pl.program_id(0) == 0)
        def _():
            state[...] = jnp.zeros_like(state)

        b_blk, gc_blk, gr_blk = b_ref[...], gc_ref[...], gr_ref[...]
        old = [state[h] for h in range(HEADS)]
        res = [_gdn_group(*_group_operands(grp, q_ref, k_ref, v_ref, b_blk, gc_blk, gr_blk),
                          old[grp * GROUP:(grp + 1) * GROUP]) for grp in range(N_GROUPS)]
        for grp, (o, new, inv) in enumerate(res):
            inv_ref[grp] = inv
            for hh in range(GROUP):
                h = grp * GROUP + hh
                st_ref[h] = old[h]
                o_ref[:, h * HEAD:(h + 1) * HEAD] = o[hh * CHUNK:(hh + 1) * CHUNK]
                state[h] = new[hh]

    return pl.pallas_call(
        body, name=name, grid=(nc,), in_specs=in_specs,
        out_specs=[pl.BlockSpec((CHUNK, D), lambda n: (n, 0)),
                   pl.BlockSpec((None, HEADS, HEAD, HEAD), lambda n: (n, 0, 0, 0)),
                   pl.BlockSpec((None, N_GROUPS, GROWS, GROWS), lambda n: (n, 0, 0, 0))],
        out_shape=[jax.ShapeDtypeStruct((s, D), F32), jax.ShapeDtypeStruct((nc, HEADS, HEAD, HEAD), F32),
                   jax.ShapeDtypeStruct((nc, N_GROUPS, GROWS, GROWS), F32)],
        scratch_shapes=[pltpu.VMEM((HEADS, HEAD, HEAD), F32)],
        compiler_params=_params(("arbitrary",)),
    )(qkv, qkv, qkv, beta, gcum, grow)


def _gdn_scan_bwd(qkv, beta, gcum, grow, states, invs, do, name):
    s = qkv.shape[0]
    nc, at, in_specs = _gdn_specs(s, rev=True)
    in_specs += [pl.BlockSpec((None, HEADS, HEAD, HEAD), lambda n: (at(n), 0, 0, 0)),
                 pl.BlockSpec((None, N_GROUPS, GROWS, GROWS), lambda n: (at(n), 0, 0, 0)),
                 pl.BlockSpec((CHUNK, D), lambda n: (at(n), 0))]

    def body(q_ref, k_ref, v_ref, b_ref, gc_ref, gr_ref, st_ref, inv_ref, do_ref, dq_ref, dk_ref, dv_ref, db_ref,
             dgc_ref, dgr_ref, dstate):
        @pl.when(pl.program_id(0) == 0)
        def _():
            dstate[...] = jnp.zeros_like(dstate)

        b_blk, gc_blk, gr_blk = b_ref[...], gc_ref[...], gr_ref[...]
        dold = [dstate[h] for h in range(HEADS)]
        res = []
        for grp in range(N_GROUPS):
            heads = range(grp * GROUP, (grp + 1) * GROUP)
            inv = inv_ref[grp]
            _, vjp = jax.vjp(lambda q, k, v, b, gc, gr, *st, inv=inv: _gdn_group(q, k, v, b, gc, gr, st, inv)[:2],
                             *_group_operands(grp, q_ref, k_ref, v_ref, b_blk, gc_blk, gr_blk),
                             *[st_ref[h] for h in heads])
            d_out = jnp.concatenate([do_ref[:, h * HEAD:(h + 1) * HEAD] for h in heads], axis=0)
            res.append(vjp((d_out, tuple(dold[h] for h in heads))))
        lane = lax.broadcasted_iota(jnp.int32, (CHUNK, HEAD), 1)
        db_all = jnp.zeros((CHUNK, HEAD), F32)
        dgc_all = jnp.zeros((CHUNK, HEAD), F32)
        for grp, (dq, dk, dv, db, dgc, dgr, *dst) in enumerate(res):
            dgr_ref[grp] = dgr
            for hh in range(GROUP):
                h = grp * GROUP + hh
                cs, rows = slice(h * HEAD, (h + 1) * HEAD), slice(hh * CHUNK, (hh + 1) * CHUNK)
                dq_ref[:, cs] = dq[rows]
                dk_ref[:, cs] = dk[rows]
                dv_ref[:, cs] = dv[rows]
                dstate[h] = dst[hh]
                db_all = jnp.where(lane == h, db[rows], db_all)
                dgc_all = jnp.where(lane == h, dgc[rows], dgc_all)
        db_ref[...] = db_all
        dgc_ref[...] = dgc_all

    blk = pl.BlockSpec((CHUNK, D), lambda n: (at(n), 0))
    gblk = pl.BlockSpec((CHUNK, HEAD), lambda n: (at(n), 0))
    return pl.pallas_call(
        body, name=name, grid=(nc,), in_specs=in_specs,
        out_specs=[blk, blk, blk, gblk, gblk, pl.BlockSpec((None, N_GROUPS, 1, GROWS), lambda n: (at(n), 0, 0, 0))],
        out_shape=[jax.ShapeDtypeStruct((s, D), F32)] * 3 + [jax.ShapeDtypeStruct((s, HEAD), F32)] * 2
        + [jax.ShapeDtypeStruct((nc, N_GROUPS, 1, GROWS), F32)],
        scratch_shapes=[pltpu.VMEM((HEADS, HEAD, HEAD), F32)],
        compiler_params=_params(("arbitrary",)),
    )(qkv, qkv, qkv, beta, gcum, grow, states, invs, do)


ATT_TILE = 512
ATT_SCALE = QK_HEAD ** -0.5


def _att_mask(t):
    qpos = lax.broadcasted_iota(jnp.int32, (t, t), 0)
    kpos = lax.broadcasted_iota(jnp.int32, (t, t), 1)
    return (kpos // CHUNK) <= (qpos // CHUNK)


def _att_pairs(nb, by_query):
    if by_query:
        pairs = [(i, j) for i in range(nb) for j in range(i + 1)]
    else:
        pairs = [(j, i) for j in range(nb) for i in range(j, nb)]
    return jnp.array([a for a, _ in pairs], jnp.int32), jnp.array([b for _, b in pairs], jnp.int32)


def _attn_fwd(q, k, v, name):
    s = q.shape[0]
    t = min(ATT_TILE, s)
    nb = s // t
    ii, jj = _att_pairs(nb, by_query=True)

    def body(ii_ref, jj_ref, q_ref, k_ref, v_ref, o_ref, lse_ref, m_s, l_s, acc):
        step = pl.program_id(1)
        i, j = ii_ref[step], jj_ref[step]

        @pl.when(j == 0)
        def _():
            m_s[...] = jnp.full_like(m_s, -jnp.inf)
            l_s[...] = jnp.zeros_like(l_s)
            acc[...] = jnp.zeros_like(acc)

        sc = lax.dot_general(q_ref[...], k_ref[...], _NT, preferred_element_type=F32) * ATT_SCALE
        sc = lax.cond(i == j, lambda u: jnp.where(_att_mask(t), u, -jnp.inf), lambda u: u, sc)
        m_new = jnp.maximum(m_s[...], jnp.max(sc, axis=-1, keepdims=True))
        alpha = jnp.exp(m_s[...] - m_new)
        p = jnp.exp(sc - m_new)
        l_s[...] = alpha * l_s[...] + jnp.sum(p, axis=-1, keepdims=True)
        acc[...] = alpha * acc[...] + jnp.dot(p.astype(BF16), v_ref[...], preferred_element_type=F32)
        m_s[...] = m_new

        @pl.when(j == i)
        def _():
            o_ref[...] = acc[...] / l_s[...]
            lse_ref[...] = m_s[...] + jnp.log(l_s[...])

    grid_spec = pltpu.PrefetchScalarGridSpec(
        num_scalar_prefetch=2, grid=(HEADS, len(ii)),
        in_specs=[pl.BlockSpec((t, HEAD_PAD), lambda h, n, ir, jr: (ir[n], h)),
                  pl.BlockSpec((t, HEAD_PAD), lambda h, n, ir, jr: (jr[n], h)),
                  pl.BlockSpec((t, HEAD), lambda h, n, ir, jr: (jr[n], h))],
        out_specs=[pl.BlockSpec((t, HEAD), lambda h, n, ir, jr: (ir[n], h)),
                   pl.BlockSpec((None, t, 1), lambda h, n, ir, jr: (h, ir[n], 0))],
        scratch_shapes=[pltpu.VMEM((t, 1), F32), pltpu.VMEM((t, 1), F32), pltpu.VMEM((t, HEAD), F32)])
    return pl.pallas_call(
        body, name=name, grid_spec=grid_spec,
        out_shape=[jax.ShapeDtypeStruct((s, HEADS * HEAD), F32), jax.ShapeDtypeStruct((HEADS, s, 1), F32)],
        compiler_params=_params(("parallel", "arbitrary")),
    )(ii, jj, q, k, v)


def _attn_bwd(q, k, v, do, o, lse, name):
    s = q.shape[0]
    t = min(ATT_TILE, s)
    nb = s // t
    jj, ii = _att_pairs(nb, by_query=False)

    def body(jj_ref, ii_ref, q_ref, k_ref, v_ref, do_ref, o_ref, lse_ref, dq_ref, dk_ref, dv_ref, dk_acc, dv_acc):
        step = pl.program_id(1)
        i, j = ii_ref[step], jj_ref[step]

        @pl.when(step == 0)
        def _():
            dq_ref[...] = jnp.zeros_like(dq_ref)

        @pl.when(i == j)
        def _():
            dk_acc[...] = jnp.zeros_like(dk_acc)
            dv_acc[...] = jnp.zeros_like(dv_acc)

        sc = lax.dot_general(q_ref[...], k_ref[...], _NT, preferred_element_type=F32) * ATT_SCALE
        p = jnp.exp(sc - lse_ref[...])
        p = lax.cond(i == j, lambda u: jnp.where(_att_mask(t), u, 0.0), lambda u: u, p)
        do_f = do_ref[...]
        dob = do_f.astype(BF16)
        delta = jnp.sum(do_f * o_ref[...], axis=-1, keepdims=True)
        dv_acc[...] += lax.dot_general(p.astype(BF16), dob, _TN, preferred_element_type=F32)
        dp = lax.dot_general(dob, v_ref[...], _NT, preferred_element_type=F32)
        ds = (p * (dp - delta) * ATT_SCALE).astype(BF16)
        dk_acc[...] += lax.dot_general(ds, q_ref[...], _TN, preferred_element_type=F32)
        rows = pl.ds(pl.multiple_of(i * t, t), t)
        dq_ref[rows, :] += jnp.dot(ds, k_ref[...], preferred_element_type=F32)

        @pl.when(i == nb - 1)
        def _():
            dk_ref[...] = dk_acc[...]
            dv_ref[...] = dv_acc[...]

    grid_spec = pltpu.PrefetchScalarGridSpec(
        num_scalar_prefetch=2, grid=(HEADS, len(jj)),
        in_specs=[pl.BlockSpec((t, HEAD_PAD), lambda h, n, jr, ir: (ir[n], h)),
                  pl.BlockSpec((t, HEAD_PAD), lambda h, n, jr, ir: (jr[n], h)),
                  pl.BlockSpec((t, HEAD), lambda h, n, jr, ir: (jr[n], h)),
                  pl.BlockSpec((t, HEAD), lambda h, n, jr, ir: (ir[n], h)),
                  pl.BlockSpec((t, HEAD), lambda h, n, jr, ir: (ir[n], h)),
                  pl.BlockSpec((None, t, 1), lambda h, n, jr, ir: (h, ir[n], 0))],
        out_specs=[pl.BlockSpec((s, HEAD_PAD), lambda h, n, jr, ir: (0, h)),
                   pl.BlockSpec((t, HEAD_PAD), lambda h, n, jr, ir: (jr[n], h)),
                   pl.BlockSpec((t, HEAD), lambda h, n, jr, ir: (jr[n], h))],
        scratch_shapes=[pltpu.VMEM((t, HEAD_PAD), F32), pltpu.VMEM((t, HEAD), F32)])
    return pl.pallas_call(
        body, name=name, grid_spec=grid_spec,
        out_shape=[jax.ShapeDtypeStruct((s, HEADS * HEAD_PAD), F32)] * 2 + [jax.ShapeDtypeStruct((s, HEADS * HEAD), F32)],
        compiler_params=_params(("parallel", "arbitrary")),
    )(jj, ii, q, k, v, do, o, lse)


def _rope_tables(positions):
    half = ROPE // 2
    inv_freq = ROPE_BASE ** (-jnp.arange(half, dtype=F32) / half)
    ang = positions.astype(F32)[:, None] * inv_freq
    cos, sin = jnp.cos(ang), jnp.sin(ang)
    return jnp.concatenate([cos] * 4, axis=1), jnp.concatenate([-sin, sin] * 2, axis=1)


def _loss_and_grad(y, target, name):
    s = y.shape[0]
    ts = min(ROW_TILE, s)

    def body(y_ref, t_ref, dy_ref, l_ref):
        e = y_ref[...] - t_ref[...]
        dy_ref[...] = e * (1.0 / D)
        part = jnp.sum(jnp.sum(e * e, axis=-1, keepdims=True) * (0.5 / D), axis=0, keepdims=True)
        part = part * jnp.ones((1, 128), F32)

        @pl.when(pl.program_id(0) == 0)
        def _():
            l_ref[...] = part

        @pl.when(pl.program_id(0) > 0)
        def _():
            l_ref[...] += part

    return pl.pallas_call(
        body, name=name, grid=(s // ts,),
        in_specs=[pl.BlockSpec((ts, D), lambda i: (i, 0))] * 2,
        out_specs=[pl.BlockSpec((ts, D), lambda i: (i, 0)), pl.BlockSpec((1, 128), lambda i: (0, 0))],
        out_shape=[jax.ShapeDtypeStruct((s, D), F32), jax.ShapeDtypeStruct((1, 128), F32)],
        compiler_params=_params(("arbitrary",)),
    )(y, target)


ANY = pl.BlockSpec(memory_space=pl.ANY)


def _all_gather(shard, name):
    def body(x_ref, out_ref, send_sems, recv_sems, local_sem):
        x, y, c = lax.axis_index("x"), lax.axis_index("y"), lax.axis_index("c")
        me, sibling = (x, y, c), (x, y, 1 - c)
        chips = [(1 - x, y), (x, 1 - y), (1 - x, 1 - y)]

        def rows(px, py, pc):
            return out_ref.at[4 * px + 2 * py + pc]

        def copy(k, block, to, src=None):
            return pltpu.make_async_remote_copy(
                src_ref=rows(*block) if src is None else src, dst_ref=rows(*block),
                send_sem=send_sems.at[k], recv_sem=recv_sems.at[k], device_id=to, device_id_type=MESH)

        mine = pltpu.make_async_copy(x_ref, rows(*me), local_sem)
        mine.start()
        first = [copy(0, me, sibling, src=x_ref)]
        first += [copy(1 + j, me, (*chip, c), src=x_ref) for j, chip in enumerate(chips)]
        for cp in first:
            cp.start()
        passed = [copy(4 + j, (*chip, c), sibling) for j, chip in enumerate(chips)]
        for j, chip in enumerate(chips):
            copy(1 + j, (*chip, c), me).wait_recv()
            passed[j].start()
        copy(0, sibling, me).wait_recv()
        for j, chip in enumerate(chips):
            copy(4 + j, (*chip, 1 - c), me).wait_recv()
        for cp in first + passed:
            cp.wait_send()
        mine.wait()

    return pl.pallas_call(
        body, name=name, out_shape=jax.ShapeDtypeStruct((N_DEV,) + shard.shape, shard.dtype),
        in_specs=[ANY], out_specs=ANY,
        scratch_shapes=[pltpu.SemaphoreType.DMA((7,)), pltpu.SemaphoreType.DMA((7,)), pltpu.SemaphoreType.DMA],
    )(shard)


def _exchange(blocks, name):
    def body(x_ref, out_ref, send_sems, recv_sems, local_sem):
        x, y, c = lax.axis_index("x"), lax.axis_index("y"), lax.axis_index("c")
        me = 4 * x + 2 * y + c
        mine = pltpu.make_async_copy(x_ref.at[me], out_ref.at[me], local_sem)
        mine.start()
        copies = []
        for k in range(1, N_DEV):
            px = 1 - x if k & 4 else x
            py = 1 - y if k & 2 else y
            pc = 1 - c if k & 1 else c
            peer = 4 * px + 2 * py + pc
            cp = pltpu.make_async_remote_copy(
                src_ref=x_ref.at[peer], dst_ref=out_ref.at[me], send_sem=send_sems.at[k - 1],
                recv_sem=recv_sems.at[k - 1], device_id=(px, py, pc), device_id_type=MESH)
            cp.start()
            copies.append((cp, pltpu.make_async_remote_copy(
                src_ref=x_ref.at[peer], dst_ref=out_ref.at[peer], send_sem=send_sems.at[k - 1],
                recv_sem=recv_sems.at[k - 1], device_id=(px, py, pc), device_id_type=MESH)))
        for cp, landing in copies:
            landing.wait_recv()
        for cp, landing in copies:
            cp.wait_send()
        mine.wait()

    return pl.pallas_call(
        body, name=name, out_shape=jax.ShapeDtypeStruct(blocks.shape, blocks.dtype),
        in_specs=[ANY], out_specs=ANY,
        scratch_shapes=[pltpu.SemaphoreType.DMA((7,)), pltpu.SemaphoreType.DMA((7,)), pltpu.SemaphoreType.DMA],
    )(blocks)


def _adamw(parts, w, m, v, name, tr=128):
    n, r, wd = parts.shape
    tr = tr if r % tr == 0 else r

    def body(p_ref, w_ref, m_ref, v_ref, g_ref, d_ref, nm_ref, nv_ref):
        g = p_ref[0].astype(F32)
        for k in range(1, n):
            g = g + p_ref[k].astype(F32)
        m_new = B1 * m_ref[...] + (1.0 - B1) * g
        v_new = B2 * v_ref[...] + (1.0 - B2) * (g * g)
        m_hat = m_new / (1.0 - B1 ** STEP)
        v_hat = v_new / (1.0 - B2 ** STEP)
        g_ref[...] = g
        d_ref[...] = -LR * (m_hat / (jnp.sqrt(v_hat) + ADAM_EPS) + WD * w_ref[...])
        nm_ref[...] = m_new
        nv_ref[...] = v_new

    blk = pl.BlockSpec((tr, wd), lambda i: (i, 0))
    return pl.pallas_call(
        body, name=name, grid=(r // tr,),
        in_specs=[pl.BlockSpec((n, tr, wd), lambda i: (0, i, 0)), blk, blk, blk],
        out_specs=[blk] * 4, out_shape=[jax.ShapeDtypeStruct((r, wd), F32)] * 4,
        compiler_params=_params(("parallel",)),
    )(parts, w, m, v)


def _outer8(ct, dm, name):
    k, n = ct.shape[0], dm.shape[1]

    def body(c_ref, d_ref, o_ref):
        cv, dv = c_ref[...], d_ref[...]
        acc = cv[:, 0:1] * dv[0:1, :]
        for s in range(1, N_DEV):
            acc = acc + cv[:, s:s + 1] * dv[s:s + 1, :]
        o_ref[...] = acc

    tk = 256
    return pl.pallas_call(
        body, name=name, grid=(k // tk,),
        in_specs=[pl.BlockSpec((tk, N_DEV), lambda i: (i, 0)), pl.BlockSpec((N_DEV, n), lambda i: (0, 0))],
        out_specs=pl.BlockSpec((tk, n), lambda i: (i, 0)), out_shape=jax.ShapeDtypeStruct((k, n), F32),
        compiler_params=_params(("parallel",)),
    )(ct, dm)


FULL = (0, D)
C128 = (0, 128)
HEAD_NOPE = [(h * HEAD_PAD, NOPE) for h in range(HEADS)]
HEAD_ROPE = [(h * HEAD_PAD + NOPE, 128) for h in range(HEADS)]
HEAD_ALL = [(h * HEAD_PAD, HEAD_PAD) for h in range(HEADS)]
HEAD_V = [(h * HEAD, HEAD) for h in range(HEADS)]


def _modulate(x, gain, scale, shift):
    return _rowwise_fwd(_modulate_fn, [(x, FULL)], [gain, scale, shift], [(D, BF16, FULL)], "modulate")[0]


def _modulate_bwd(x, gain, scale, shift, dh, dx_in):
    return _rowwise_bwd(_modulate_fn, [(x, FULL)], [gain, scale, shift], [(D, BF16, FULL)], [dh], [(0, FULL)],
                        [(D, F32)], "modulate_bwd", add={0: dx_in})


def _residual(x, y, gm):
    return _rowwise_fwd(_resgate_fn, [(x, FULL), (y, FULL)], [gm], [(D, F32, FULL)], "residual")[0]


def _residual_bwd(y, gm, dxn):
    return _rowwise_bwd(_gate_only_fn, [(y, FULL)], [gm], [(D, F32, FULL)], [dxn], [(0, FULL)], [(D, BF16)],
                        "residual_bwd")


def _ffn_fwd(x, p):
    h = _modulate(x, p["gain"], p["scale"], p["shift"])
    gate, up, act = _ffn_in(h, p["wg"], p["wu"], "ffn_in")
    y = _mm(act, p["wo"], "nn", "ffn_out")
    return _residual(x, y, p["gm"]), dict(x=x, h=h, gate=gate, up=up, act=act, y=y)


def _ffn_bwd(t, p, dxn):
    dy, dgm = _residual_bwd(t["y"], p["gm"], dxn)
    dgate, dup = _ffn_bwd_act(dy, p["wo"], t["gate"], t["up"], "ffn_bwd_act")
    dwo = _mm(t["act"], dy, "tn", "ffn_dwo")
    dh = _matmul([(dgate, p["wg"]), (dup, p["wu"])], "nt", "ffn_dh")
    dwg = _mm(t["h"], dgate, "tn", "ffn_dwi")
    dwu = _mm(t["h"], dup, "tn", "ffn_dwi")
    dx, dgain, dscale, dshift = _modulate_bwd(t["x"], p["gain"], p["scale"], p["shift"], dh, dxn)
    return dx, dict(gain=dgain, scale=dscale, shift=dshift, gm=dgm, wg=dwg, wu=dwu, wo=dwo)


def _pad128(t):
    return jnp.pad(t, ((0, 0), (0, 128 - t.shape[1])))


def _gdn_fwd(x, p):
    s = x.shape[0]
    h = _modulate(x, p["gain"], p["scale"], p["shift"])
    pm = _mm(h, p["w_main"], "nn", "gdn_proj")
    tail = _mm(h, p["w_tail"], "nn", "gdn_proj_tail")
    qkv = _gdn_conv_fwd(pm, p["conv_w"], "gdn_conv")
    beta, gcum = _rowwise_fwd(_gdn_gates_fn, [(tail, C128), (tail, (128, 128))], [p["a_log"], p["dt_bias"]],
                              [(128, F32, C128)] * 2, "gdn_gates")
    grow = gcum[:, :HEADS].reshape(s // CHUNK, CHUNK, N_GROUPS, GROUP).transpose(0, 2, 3, 1)
    grow = grow.reshape(s // CHUNK, N_GROUPS, 1, GROWS)
    o, states, invs = _gdn_scan_fwd(qkv, beta, gcum, grow, "gdn_scan")
    on, = _rowwise_fwd(_gdn_outnorm_fn, [(o, HEAD_V), (pm, [(3 * D + h_ * HEAD, HEAD) for h_ in range(HEADS)])],
                       [p["norm_g"]], [(D, BF16, HEAD_V)], "gdn_outnorm", groups=HEADS)
    y = _mm(on, p["w_out"], "nn", "mix_out")
    t = dict(x=x, h=h, pm=pm, tail=tail, qkv=qkv, beta=beta, gcum=gcum, grow=grow, o=o, states=states, invs=invs,
             on=on, y=y)
    return _residual(x, y, p["gm"]), t


def _gdn_bwd(t, p, dxn):
    s = dxn.shape[0]
    zc = [(3 * D + h_ * HEAD, HEAD) for h_ in range(HEADS)]
    dy, dgm = _residual_bwd(t["y"], p["gm"], dxn)
    dw_out = _mm(t["on"], dy, "tn", "mix_dwo")
    don = _mm(dy, p["w_out"], "nt", "mix_dout")
    do, dz, dnorm_g = _rowwise_bwd(_gdn_outnorm_fn, [(t["o"], HEAD_V), (t["pm"], zc)], [p["norm_g"]],
                                   [(D, BF16, HEAD_V)], [don], [(0, HEAD_V), (1, HEAD_V)], [(D, F32), (D, F32)],
                                   "gdn_outnorm_bwd", groups=HEADS)
    dq, dk, dv, dbeta, dg, dgr = _gdn_scan_bwd(t["qkv"], t["beta"], t["gcum"], t["grow"], t["states"], t["invs"], do,
                                               "gdn_scan_bwd")
    dg = dg + _pad128(dgr.reshape(s // CHUNK, N_GROUPS, GROUP, CHUNK).transpose(0, 3, 1, 2).reshape(s, HEADS))
    dtail, da_log, ddt = _rowwise_bwd(_gdn_gates_fn, [(t["tail"], C128), (t["tail"], (128, 128))],
                                      [p["a_log"], p["dt_bias"]], [(128, F32, C128)] * 2, [dbeta, dg],
                                      [(0, C128), (0, (128, 128))], [(256, F32)], "gdn_gates_bwd")
    dxs, dcw = [], []
    for part, d in enumerate((dq, dk, dv)):
        dx_, dw_ = _gdn_conv_bwd(t["pm"], p["conv_w"], d, part, "gdn_conv_bwd")
        dxs.append(dx_)
        dcw.append(dw_)
    pieces = dxs + [dz]
    dh = _matmul([(d, p["w_main"]) for d in pieces] + [(dtail, p["w_tail"])], "nt", "gdn_dh",
                 boffs=[0, D, 2 * D, 3 * D, 0], tk=512)
    dw_main = [_mm(t["h"], d, "tn", "gdn_dwi") for d in pieces]
    dw_tail = _mm(t["h"], dtail, "tn", "gdn_dwi_tail")
    dx, dgain, dscale, dshift = _modulate_bwd(t["x"], p["gain"], p["scale"], p["shift"], dh, dxn)
    return dx, dict(gain=dgain, scale=dscale, shift=dshift, gm=dgm, w_main=jnp.concatenate(dw_main, axis=1),
                    w_tail=dw_tail, conv_w=jnp.concatenate(dcw, axis=1), a_log=da_log, dt_bias=ddt,
                    norm_g=dnorm_g, w_out=dw_out)


def _q_rows(q2, cosf, sins):
    return [(q2, HEAD_NOPE), (q2, HEAD_ROPE), (cosf, C128), (sins, C128)]


def _mla_fwd(x, p, kv):
    h = _modulate(x, p["gain"], p["scale"], p["shift"])
    cq = _mm(h, p["w_dq"], "nn", "mla_dq")
    cqn, = _rowwise_fwd(_rms_fn, [(cq, (0, Q_LORA))], [p["q_lora_g"]], [(Q_LORA, BF16, (0, Q_LORA))], "mla_qlora_norm")
    q2 = _mm(cqn, p["w_uq"], "nn", "mla_uq")
    qn, = _rowwise_fwd(_q_norm_rope_fn, _q_rows(q2, kv["cosf"], kv["sins"]), [p["q_gn"], p["q_gr"]],
                       [(HEADS * HEAD_PAD, BF16, HEAD_ALL)], "mla_q_norm", groups=HEADS)
    o, lse = _attn_fwd(qn, kv["kn"], kv["vb"], "mla_attn")
    y = _mm(o, p["w_out"], "nn", "mix_out")
    return _residual(x, y, p["gm"]), dict(x=x, h=h, cq=cq, cqn=cqn, q2=q2, qn=qn, o=o, lse=lse, y=y)


def _mla_bwd(t, p, kv, dxn):
    dy, dgm = _residual_bwd(t["y"], p["gm"], dxn)
    dw_out = _mm(t["o"], dy, "tn", "mix_dwo")
    do = _mm(dy, p["w_out"], "nt", "mix_dout")
    dq, dk, dv = _attn_bwd(t["qn"], kv["kn"], kv["vb"], do, t["o"], t["lse"], "mla_attn_bwd")
    dq2, dq_gn, dq_gr = _rowwise_bwd(_q_norm_rope_fn, _q_rows(t["q2"], kv["cosf"], kv["sins"]), [p["q_gn"], p["q_gr"]],
                                     [(HEADS * HEAD_PAD, BF16, HEAD_ALL)], [dq],
                                     [(0, HEAD_NOPE), (0, HEAD_ROPE), None, None], [(HEADS * HEAD_PAD, F32)],
                                     "mla_q_norm_bwd", groups=HEADS)
    dw_uq = _mm(t["cqn"], dq2, "tn", "mla_dwuq")
    dcqn = _mm(dq2, p["w_uq"], "nt", "mla_dcq")
    dcq, dq_lora_g = _rowwise_bwd(_rms_fn, [(t["cq"], (0, Q_LORA))], [p["q_lora_g"]], [(Q_LORA, BF16, (0, Q_LORA))],
                                  [dcqn], [(0, (0, Q_LORA))], [(Q_LORA, F32)], "mla_qlora_norm_bwd")
    dw_dq = _mm(t["h"], dcq, "tn", "mla_dwdq")
    dh = _mm(dcq, p["w_dq"], "nt", "mla_dh")
    dx, dgain, dscale, dshift = _modulate_bwd(t["x"], p["gain"], p["scale"], p["shift"], dh, dxn)
    grads = dict(gain=dgain, scale=dscale, shift=dshift, gm=dgm, w_dq=dw_dq, q_lora_g=dq_lora_g, w_uq=dw_uq,
                 q_gn=dq_gn, q_gr=dq_gr, w_out=dw_out)
    return dx, grads, dk, dv


def _k_rows(kvp, ckv, cosf, sins):
    return [(kvp, HEAD_NOPE), (kvp, HEAD_ROPE), (ckv, (KV_LORA, 128)), (cosf, C128), (sins, C128)]


def _kv_fwd(x, p, cosf, sins):
    h = _modulate(x, p["gain"], p["scale"], p["shift"])
    ckv = _mm(h, p["w_dkv"], "nn", "kv_down")
    lat, = _rowwise_fwd(_rms_fn, [(ckv, (0, KV_LORA))], [p["kv_g"]], [(KV_LORA, BF16, (0, KV_LORA))], "kv_norm")
    kvp = _mm(lat, p["w_ukv"], "nn", "kv_up")
    kn, vb = _rowwise_fwd(_k_norm_rope_fn, _k_rows(kvp, ckv, cosf, sins), [p["k_gn"], p["k_gr"]],
                          [(HEADS * HEAD_PAD, BF16, HEAD_ALL), (HEADS * HEAD, BF16, HEAD_V)], "kv_k_norm",
                          groups=HEADS)
    return dict(x=x, h=h, ckv=ckv, lat=lat, kvp=kvp, kn=kn, vb=vb, cosf=cosf, sins=sins)


def _kv_bwd(t, p, dk, dv, dx_in):
    dkvp, drope, dk_gn, dk_gr = _rowwise_bwd(
        _k_norm_rope_fn, _k_rows(t["kvp"], t["ckv"], t["cosf"], t["sins"]), [p["k_gn"], p["k_gr"]],
        [(HEADS * HEAD_PAD, BF16, HEAD_ALL), (HEADS * HEAD, BF16, HEAD_V)], [dk, dv],
        [(0, HEAD_NOPE), (0, HEAD_ROPE), (1, C128), None, None], [(HEADS * HEAD_PAD, F32), (128, F32)],
        "kv_k_norm_bwd", groups=HEADS)
    dw_ukv = _mm(t["lat"], dkvp, "tn", "kv_dwukv")
    dlat = _mm(dkvp, p["w_ukv"], "nt", "kv_dlat")
    dckv, dkv_g = _rowwise_bwd(_rms_fn, [(t["ckv"], (0, KV_LORA))], [p["kv_g"]], [(KV_LORA, BF16, (0, KV_LORA))],
                               [dlat], [(0, (0, KV_LORA))], [(KV_LORA, F32)], "kv_norm_bwd")
    dw_dkv = jnp.concatenate([_mm(t["h"], dckv, "tn", "kv_dwdkv"), _mm(t["h"], drope, "tn", "kv_dwdkv_rope")], axis=1)
    dh = _matmul([(dckv, p["w_dkv"]), (drope, p["w_dkv"])], "nt", "kv_dh", boffs=[0, KV_LORA])
    dx, dgain, dscale, dshift = _modulate_bwd(t["x"], p["gain"], p["scale"], p["shift"], dh, dx_in)
    return dx, dict(gain=dgain, scale=dscale, shift=dshift, w_dkv=dw_dkv, kv_g=dkv_g, w_ukv=dw_ukv, k_gn=dk_gn,
                    k_gr=dk_gr)


WEIGHTS = ["ada_w", "ada_b", "norm_g", "ffn_w_in", "ffn_w_out", "gdn_w_in", "gdn_conv_w", "gdn_a_log", "gdn_dt_bias",
           "gdn_norm_g", "gdn_w_out", "kv_ada_w", "kv_ada_b", "kv_norm_g", "mla_w_dkv", "mla_kv_norm_g", "mla_w_ukv",
           "mla_k_norm_g", "mla_w_dq", "mla_q_lora_norm_g", "mla_w_uq", "mla_q_norm_g", "mla_w_out"]
EXCHANGED = {"ffn_w_in": (8192, 704), "ffn_w_out": (2816, 1024), "gdn_w_in": (2048, 514), "gdn_w_out": (256, 1024),
             "mla_w_dkv": (128, 320), "mla_w_ukv": (256, 256), "mla_w_dq": (256, 384), "mla_w_uq": (768, 192),
             "mla_w_out": (256, 1024)}
SMALL = [("ada_b", 4 * N_MOD * D), ("kv_ada_b", 2 * D), ("norm_g", DEPTH * 3 * D), ("gdn_conv_w", N_A * CONV_K * 3 * D),
         ("gdn_a_log", N_A * HEADS), ("gdn_dt_bias", N_A * HEADS), ("gdn_norm_g", N_A * HEAD), ("kv_norm_g", D),
         ("mla_kv_norm_g", KV_LORA), ("mla_k_norm_g", QK_HEAD), ("mla_q_lora_norm_g", 2 * Q_LORA),
         ("mla_q_norm_g", 2 * QK_HEAD)]
SMALL_REPLICATED = [n for n, _ in SMALL if n not in ("norm_g", "gdn_conv_w")]


def _silu_fn(g, t):
    return (_silu(t),)


def _dup_rope(t):
    return jnp.concatenate([t[..., :NOPE], t[..., NOPE:], t[..., NOPE:]], axis=-1)


def _fold_rope(t):
    return jnp.concatenate([t[..., :NOPE], t[..., NOPE:QK_HEAD] + t[..., QK_HEAD:]], axis=-1)


def _pack(pieces, rows):
    flat = jnp.concatenate([p.reshape(-1).astype(F32) for p in pieces])
    return jnp.pad(flat, (0, rows * 128 - flat.shape[0])).reshape(rows, 128)


def _step(a):
    me = 4 * lax.axis_index("x") + 2 * lax.axis_index("y") + lax.axis_index("c")
    x = a["x"][0]
    cosf, sins = _rope_tables(a["positions"][0])

    n_cw, n_ng = N_A * CONV_K * 3 * HEAD, DEPTH * 3 * HEAD
    small_all = _all_gather(_pack([a["gdn_conv_w"], a["norm_g"], a["c"]], 44), "gather_small").reshape(N_DEV, -1)
    conv_w = small_all[:, :n_cw].reshape(N_DEV, N_A, CONV_K, 3 * HEAD).transpose(1, 2, 0, 3).reshape(N_A, CONV_K, 3 * D)
    norm_g = small_all[:, n_cw:n_cw + n_ng].reshape(N_DEV, DEPTH, 3, HEAD).transpose(1, 2, 0, 3).reshape(DEPTH, 3, D)
    c_all = small_all[:, n_cw + n_ng:n_cw + n_ng + D]

    c_act, = _rowwise_fwd(_silu_fn, [(c_all, FULL)], [], [(D, F32, FULL)], "c_act")
    n_ada = N_MOD * D // N_DEV
    parts = [_mm(c_act, a["ada_w"][l], "nn", "mod_proj") for l in range(DEPTH)]
    parts.append(_mm(c_act, a["kv_ada_w"], "nn", "mod_proj_kv"))
    mod_recv = _exchange(jnp.concatenate(parts, axis=1)[:, None, :], "exchange_mod")[:, 0]
    mod = mod_recv[:, :DEPTH * n_ada].reshape(N_DEV, DEPTH, n_ada).transpose(1, 0, 2).reshape(DEPTH, N_MOD * D)
    mod = (mod + a["ada_b"]).reshape(DEPTH, N_MOD, D)
    kvmod = mod_recv[:, DEPTH * n_ada:].reshape(2 * D) + a["kv_ada_b"]

    def gather(name):
        return _all_gather(a[name].astype(BF16).reshape(EXCHANGED[name]), "gather_" + name)

    g_ffn_in = gather("ffn_w_in").reshape(N_DEV, DEPTH, 2, D, 2 * D_FF // N_DEV)
    g_ffn_out = gather("ffn_w_out").reshape(N_DEV, DEPTH, 2, D_FF // N_DEV, D)
    g_gdn_in = gather("gdn_w_in").reshape(N_DEV, N_A, D, 514)
    g_gdn_out = gather("gdn_w_out").reshape(N_DEV, N_A, D // N_DEV, D)
    g_dkv = gather("mla_w_dkv")
    g_ukv = gather("mla_w_ukv")
    g_dq = gather("mla_w_dq").reshape(N_DEV, 2, D // N_DEV, Q_LORA)
    g_uq = gather("mla_w_uq").reshape(N_DEV, 2, Q_LORA, QK_HEAD)
    g_mo = gather("mla_w_out").reshape(N_DEV, 2, D // N_DEV, D)

    def row(v):
        return v[None]

    def ffn_params(l, i):
        w = g_ffn_in[:, l, i]
        k = 0 if i == 0 else 6
        return dict(gain=row(norm_g[l, 0 if i == 0 else 2]), shift=row(mod[l, k]), scale=row(mod[l, k + 1]),
                    gm=0.5 * row(mod[l, k + 2]),
                    wg=w[:N_DEV // 2].transpose(1, 0, 2).reshape(D, D_FF),
                    wu=w[N_DEV // 2:].transpose(1, 0, 2).reshape(D, D_FF),
                    wo=g_ffn_out[:, l, i].reshape(D_FF, D))

    def gdn_params(l):
        w = g_gdn_in[:, l].transpose(1, 0, 2).reshape(D, 4 * D + 2 * HEADS)
        pad = lambda t: jnp.pad(t, ((0, 0), (0, 128 - HEADS)))
        return dict(gain=row(norm_g[l, 1]), shift=row(mod[l, 3]), scale=row(mod[l, 4]), gm=row(mod[l, 5]),
                    w_main=w[:, :4 * D],
                    w_tail=jnp.concatenate([pad(w[:, 4 * D:4 * D + HEADS]), pad(w[:, 4 * D + HEADS:])], axis=1),
                    conv_w=conv_w[l], a_log=_pad128(row(a["gdn_a_log"][l])), dt_bias=_pad128(row(a["gdn_dt_bias"][l])),
                    norm_g=row(a["gdn_norm_g"][l]), w_out=g_gdn_out[:, l].reshape(D, D))

    def mla_params(l):
        j = l - N_A
        uq = g_uq[:, j].transpose(1, 0, 2)
        qg = _dup_rope(a["mla_q_norm_g"][j])
        return dict(gain=row(norm_g[l, 1]), shift=row(mod[l, 3]), scale=row(mod[l, 4]), gm=row(mod[l, 5]),
                    w_dq=g_dq[:, j].reshape(D, Q_LORA), q_lora_g=row(a["mla_q_lora_norm_g"][j]),
                    w_uq=_dup_rope(uq).reshape(Q_LORA, HEADS * HEAD_PAD), q_gn=row(qg[:NOPE]), q_gr=row(qg[NOPE:]),
                    w_out=g_mo[:, j].reshape(D, D))

    w_dkv = g_dkv.reshape(D, KV_LORA + ROPE)
    kg = _dup_rope(a["mla_k_norm_g"])
    kv_p = dict(gain=row(a["kv_norm_g"]), shift=row(kvmod[:D]), scale=row(kvmod[D:]),
                w_dkv=jnp.concatenate([w_dkv, w_dkv[:, KV_LORA:]], axis=1), kv_g=row(a["mla_kv_norm_g"]),
                w_ukv=g_ukv.transpose(1, 0, 2).reshape(KV_LORA, HEADS * 2 * HEAD), k_gn=row(kg[:NOPE]), k_gr=row(kg[NOPE:]))

    tapes, kv = [], None
    for l in range(DEPTH):
        p1, pm_, p2 = ffn_params(l, 0), (gdn_params(l) if l < N_A else mla_params(l)), ffn_params(l, 1)
        x, t1 = _ffn_fwd(x, p1)
        x, tm_ = _gdn_fwd(x, pm_) if l < N_A else _mla_fwd(x, pm_, kv)
        x, t2 = _ffn_fwd(x, p2)
        tapes.append((p1, t1, pm_, tm_, p2, t2))
        if l == N_A - 1:
            kv = _kv_fwd(x, kv_p, cosf, sins)
    dx, loss_blk = _loss_and_grad(x, a["loss_target"][0], "loss")
    loss = lax.psum(loss_blk[0, 0], ("x", "y", "c"))

    grads = [None] * DEPTH
    dk_sum = dv_sum = kv_grads = None
    for l in reversed(range(DEPTH)):
        p1, t1, pm_, tm_, p2, t2 = tapes[l]
        if l == N_A - 1:
            dx, kv_grads = _kv_bwd(kv, kv_p, dk_sum, dv_sum, dx)
        dx, g2 = _ffn_bwd(t2, p2, dx)
        if l < N_A:
            dx, gm_ = _gdn_bwd(tm_, pm_, dx)
        else:
            dx, gm_, dk, dv = _mla_bwd(tm_, pm_, kv, dx)
            dk_sum = dk if dk_sum is None else dk_sum + dk
            dv_sum = dv if dv_sum is None else dv_sum + dv
        dx, g1 = _ffn_bwd(t1, p1, dx)
        grads[l] = (g1, gm_, g2)

    def by_cols(g, n):
        return g.reshape(g.shape[0], -1, n).transpose(1, 0, 2)

    stack = {}
    n_in = 2 * D_FF // N_DEV
    ffn_in = [[jnp.concatenate([by_cols(g["wg"], n_in), by_cols(g["wu"], n_in)], axis=0)
               for g in (grads[l][0], grads[l][2])] for l in range(DEPTH)]
    stack["ffn_w_in"] = jnp.stack([jnp.stack(r) for r in ffn_in]).transpose(2, 0, 1, 3, 4)
    ffn_out = [[g["wo"].reshape(N_DEV, D_FF // N_DEV, D) for g in (grads[l][0], grads[l][2])] for l in range(DEPTH)]
    stack["ffn_w_out"] = jnp.stack([jnp.stack(r) for r in ffn_out]).transpose(2, 0, 1, 3, 4)
    gdn_in = [jnp.concatenate([grads[l][1]["w_main"], grads[l][1]["w_tail"][:, :HEADS],
                               grads[l][1]["w_tail"][:, 128:128 + HEADS]], axis=1) for l in range(N_A)]
    stack["gdn_w_in"] = jnp.stack([by_cols(g, 514) for g in gdn_in]).transpose(1, 0, 2, 3)
    stack["gdn_w_out"] = jnp.stack([grads[l][1]["w_out"].reshape(N_DEV, D // N_DEV, D) for l in range(N_A)]).transpose(1, 0, 2, 3)
    d_dkv = kv_grads["w_dkv"]
    stack["mla_w_dkv"] = jnp.concatenate([d_dkv[:, :KV_LORA], d_dkv[:, KV_LORA:KV_LORA + ROPE] + d_dkv[:, KV_LORA + ROPE:]],
                                         axis=1).reshape(N_DEV, D // N_DEV, KV_LORA + ROPE)
    stack["mla_w_ukv"] = by_cols(kv_grads["w_ukv"], 2 * HEAD)
    mla = [grads[l][1] for l in range(N_A, DEPTH)]
    stack["mla_w_dq"] = jnp.stack([g["w_dq"].reshape(N_DEV, D // N_DEV, Q_LORA) for g in mla]).transpose(1, 0, 2, 3)
    stack["mla_w_uq"] = jnp.stack([_fold_rope(g["w_uq"].reshape(Q_LORA, HEADS, HEAD_PAD)).transpose(1, 0, 2)
                                   for g in mla]).transpose(1, 0, 2, 3)
    stack["mla_w_out"] = jnp.stack([g["w_out"].reshape(N_DEV, D // N_DEV, D) for g in mla]).transpose(1, 0, 2, 3)

    out = {}
    for name, shape2d in EXCHANGED.items():
        recv = _exchange(stack[name].astype(BF16).reshape((N_DEV,) + shape2d), "exchange_" + name)
        out[name] = _adamw(recv, a[name].reshape(shape2d), a["m_" + name].reshape(shape2d),
                           a["v_" + name].reshape(shape2d), "adamw")

    def dmod(l):
        g1, gm_, g2 = grads[l]
        return jnp.concatenate([g1["shift"], g1["scale"], 0.5 * g1["gm"], gm_["shift"], gm_["scale"], gm_["gm"],
                                g2["shift"], g2["scale"], 0.5 * g2["gm"]], axis=1)

    gdn = [grads[l][1] for l in range(N_A)]
    small = {
        "ada_b": jnp.concatenate([dmod(l) for l in range(DEPTH)], axis=0),
        "kv_ada_b": jnp.concatenate([kv_grads["shift"], kv_grads["scale"]], axis=1),
        "norm_g": jnp.stack([jnp.concatenate([grads[l][0]["gain"], grads[l][1]["gain"], grads[l][2]["gain"]], axis=0)
                             for l in range(DEPTH)]),
        "gdn_conv_w": jnp.stack([g["conv_w"] for g in gdn]),
        "gdn_a_log": jnp.stack([g["a_log"][0, :HEADS] for g in gdn]),
        "gdn_dt_bias": jnp.stack([g["dt_bias"][0, :HEADS] for g in gdn]),
        "gdn_norm_g": jnp.stack([g["norm_g"][0] for g in gdn]),
        "kv_norm_g": kv_grads["gain"],
        "mla_kv_norm_g": kv_grads["kv_g"],
        "mla_k_norm_g": _fold_rope(jnp.concatenate([kv_grads["k_gn"], kv_grads["k_gr"]], axis=1)),
        "mla_q_lora_norm_g": jnp.stack([g["q_lora_g"][0] for g in mla]),
        "mla_q_norm_g": jnp.stack([_fold_rope(jnp.concatenate([g["q_gn"], g["q_gr"]], axis=1))[0] for g in mla]),
    }
    rows = 616
    assert sum(n for _, n in SMALL) <= rows * 128 and all(small[n].size == k for n, k in SMALL)
    small_recv = _all_gather(_pack([small[n] for n, _ in SMALL], rows), "gather_small_grads")
    zero = lambda n, k: jnp.zeros((k,), F32)
    packed = {pre: _pack([a[pre + n] if n in SMALL_REPLICATED else zero(n, k) for n, k in SMALL], rows)
              for pre in ("", "m_", "v_")}
    res = _adamw(small_recv, packed[""], packed["m_"], packed["v_"], "adamw_small")
    offs = {}
    o = 0
    for n, k in SMALL:
        offs[n] = o
        o += k
    for n, k in SMALL:
        if n in SMALL_REPLICATED:
            out[n] = [r.reshape(-1)[offs[n]:offs[n] + k] for r in res]
    gsum = res[0].reshape(-1)
    g_norm = lax.dynamic_slice_in_dim(gsum[offs["norm_g"]:offs["norm_g"] + DEPTH * 3 * D].reshape(DEPTH * 3, D),
                                      me * HEAD, HEAD, axis=1)
    g_conv = lax.dynamic_slice_in_dim(
        gsum[offs["gdn_conv_w"]:offs["gdn_conv_w"] + N_A * CONV_K * 3 * D].reshape(N_A * CONV_K, 3 * D),
        me * 3 * HEAD, 3 * HEAD, axis=1)
    res2 = _adamw(_pack([g_norm, g_conv], 36)[None], *[_pack([a[pre + "norm_g"], a[pre + "gdn_conv_w"]], 36)
                                                      for pre in ("", "m_", "v_")], "adamw_small")
    out["norm_g"] = [r.reshape(-1)[:n_ng] for r in res2]
    out["gdn_conv_w"] = [r.reshape(-1)[n_ng:n_ng + n_cw] for r in res2]

    c_act_t = c_act.T
    all_small = small_recv.reshape(N_DEV, -1)
    dmod_all = all_small[:, :DEPTH * N_MOD * D].reshape(N_DEV, DEPTH, N_MOD * D)
    dmod_mine = lax.dynamic_slice_in_dim(dmod_all, me * n_ada, n_ada, axis=2)
    g_ada = jnp.concatenate([_outer8(c_act_t, dmod_mine[:, l], "ada_grad") for l in range(DEPTH)], axis=0)
    out["ada_w"] = _adamw(g_ada[None], *[a[pre + "ada_w"].reshape(DEPTH * D, n_ada) for pre in ("", "m_", "v_")], "adamw")
    dkv_all = all_small[:, offs["kv_ada_b"]:offs["kv_ada_b"] + 2 * D]
    g_kv = _outer8(c_act_t, lax.dynamic_slice_in_dim(dkv_all, me * (2 * D // N_DEV), 2 * D // N_DEV, axis=1), "ada_grad")
    out["kv_ada_w"] = _adamw(g_kv[None], *[a[pre + "kv_ada_w"] for pre in ("", "m_", "v_")], "adamw")

    result = [loss, dx[None]]
    for k in range(4):
        result += [out[n][k].reshape(a[n].shape) for n in WEIGHTS]
    return tuple(result)


def kernel(x, c, positions, ada_w, ada_b, norm_g, ffn_w_in, ffn_w_out, gdn_w_in, gdn_conv_w, gdn_a_log, gdn_dt_bias, gdn_norm_g, gdn_w_out, kv_ada_w, kv_ada_b, kv_norm_g, mla_w_dkv, mla_kv_norm_g, mla_w_ukv, mla_k_norm_g, mla_w_dq, mla_q_lora_norm_g, mla_w_uq, mla_q_norm_g, mla_w_out, loss_target, m_ada_w, m_ada_b, m_norm_g, m_ffn_w_in, m_ffn_w_out, m_gdn_w_in, m_gdn_conv_w, m_gdn_a_log, m_gdn_dt_bias, m_gdn_norm_g, m_gdn_w_out, m_kv_ada_w, m_kv_ada_b, m_kv_norm_g, m_mla_w_dkv, m_mla_kv_norm_g, m_mla_w_ukv, m_mla_k_norm_g, m_mla_w_dq, m_mla_q_lora_norm_g, m_mla_w_uq, m_mla_q_norm_g, m_mla_w_out, v_ada_w, v_ada_b, v_norm_g, v_ffn_w_in, v_ffn_w_out, v_gdn_w_in, v_gdn_conv_w, v_gdn_a_log, v_gdn_dt_bias, v_gdn_norm_g, v_gdn_w_out, v_kv_ada_w, v_kv_ada_b, v_kv_norm_g, v_mla_w_dkv, v_mla_kv_norm_g, v_mla_w_ukv, v_mla_k_norm_g, v_mla_w_dq, v_mla_q_lora_norm_g, v_mla_w_uq, v_mla_q_norm_g, v_mla_w_out):
    return _step(dict(locals()))
```

```python
import functools
import math

import jax
import jax.numpy as jnp
from jax import lax
from jax.experimental import pallas as pl
from jax.experimental.pallas import tpu as pltpu

F32 = jnp.float32
BF16 = jnp.bfloat16

N_DEV = 8
D = 1024
D_FF = 2816
DEPTH = 4
N_A = 2
N_MOD = 9
HEADS = 8
HEAD = 128
CHUNK = 64
CONV_K = 4
KV_LORA = 256
Q_LORA = 384
NOPE = 128
ROPE = 64
QK_HEAD = NOPE + ROPE
HEAD_PAD = 256
ROPE_BASE = 10000.0
EPS = 1e-6
LR, B1, B2, ADAM_EPS, WD, STEP = 0.001, 0.9, 0.999, 1e-08, 0.01, 10

VMEM_LIMIT = 48 * 1024 * 1024
ROW_TILE = 256
MESH = pl.DeviceIdType.MESH

_NN = (((1,), (0,)), ((), ()))
_NT = (((1,), (1,)), ((), ()))
_TN = (((0,), (0,)), ((), ()))
_DIMS = {"nn": _NN, "nt": _NT, "tn": _TN}


def _params(dims=None):
    return pltpu.CompilerParams(dimension_semantics=dims, vmem_limit_bytes=VMEM_LIMIT)


def _tile(n, target):
    for t in range(target - target % 128, 0, -128):
        if n % t == 0:
            return t
    return n


def _matmul(pairs, form, name, out_dtype=F32, tm=1408, tn=1408, tk=1408, boffs=None):
    a0, b0 = pairs[0]
    if form == "nn":
        m, n = a0.shape[0], b0.shape[1]
        ks = [a.shape[1] for a, _ in pairs]
    elif form == "nt":
        m, n = a0.shape[0], b0.shape[0]
        ks = [a.shape[1] for a, _ in pairs]
    else:
        m, n = a0.shape[1], b0.shape[1]
        ks = [a.shape[0] for a, _ in pairs]
    tm, tn = _tile(m, tm), _tile(n, tn)
    tks = [_tile(k, tk) for k in ks]
    boffs = boffs or [0] * len(pairs)
    assert m % tm == 0 and n % tn == 0 and all(o % t == 0 for o, t in zip(boffs, tks)), (name, m, n, ks)
    steps = [k // t for k, t in zip(ks, tks)]
    starts = [sum(steps[:p]) for p in range(len(pairs))]
    nk = sum(steps)

    def kidx(p, k):
        return jnp.clip(k - starts[p], 0, steps[p] - 1)

    in_specs, args = [], []
    for p, (a, b) in enumerate(pairs):
        t = tks[p]
        if form == "tn":
            in_specs.append(pl.BlockSpec((t, tm), lambda i, j, k, p=p: (kidx(p, k), i)))
            in_specs.append(pl.BlockSpec((t, tn), lambda i, j, k, p=p: (kidx(p, k), j)))
        elif form == "nn":
            in_specs.append(pl.BlockSpec((tm, t), lambda i, j, k, p=p: (i, kidx(p, k))))
            in_specs.append(pl.BlockSpec((t, tn), lambda i, j, k, p=p: (kidx(p, k), j)))
        else:
            in_specs.append(pl.BlockSpec((tm, t), lambda i, j, k, p=p: (i, kidx(p, k))))
            in_specs.append(pl.BlockSpec((tn, t), lambda i, j, k, p=p, o=boffs[p] // t: (j, kidx(p, k) + o)))
        args += [a, b]
    dims = _DIMS[form]
    npairs = len(pairs)

    def body(*refs):
        o_ref = refs[2 * npairs]
        k = pl.program_id(2)

        def prod(p):
            return lax.dot_general(refs[2 * p][...].astype(BF16), refs[2 * p + 1][...].astype(BF16), dims,
                                   preferred_element_type=F32)

        if nk == 1:
            o_ref[...] = prod(0).astype(o_ref.dtype)
            return
        acc = refs[2 * npairs + 1]

        @pl.when(k == 0)
        def _():
            acc[...] = jnp.zeros_like(acc)

        for p in range(npairs):
            @pl.when((k >= starts[p]) & (k < starts[p] + steps[p]))
            def _(p=p):
                acc[...] += prod(p)

        @pl.when(k == nk - 1)
        def _():
            o_ref[...] = acc[...].astype(o_ref.dtype)

    return pl.pallas_call(
        body, name=name, grid=(m // tm, n // tn, nk), in_specs=in_specs,
        out_specs=pl.BlockSpec((tm, tn), lambda i, j, k: (i, j)),
        out_shape=jax.ShapeDtypeStruct((m, n), out_dtype),
        scratch_shapes=[] if nk == 1 else [pltpu.VMEM((tm, tn), F32)],
        compiler_params=_params(("parallel", "parallel", "arbitrary")),
    )(*args)


def _mm(a, b, form, name, **kw):
    return _matmul([(a, b)], form, name, **kw)


def _cols(spec, g):
    return spec[g] if isinstance(spec, list) else spec


def _rowwise_fwd(fn, rows, pars, outs, name, groups=1, ts=ROW_TILE):
    s = rows[0][0].shape[0]
    ts = min(ts, s)
    assert s % ts == 0
    nr, npar = len(rows), len(pars)

    def body(*refs):
        par_t = [r[...] for r in refs[nr:nr + npar]]
        out_refs = refs[nr + npar:]
        for g in range(groups):
            row_t = []
            for r, (_, spec) in zip(refs[:nr], rows):
                c0, w = _cols(spec, g)
                row_t.append(r[:, c0:c0 + w].astype(F32))
            res = fn(g, *row_t, *par_t)
            for o_ref, val, (_, _, spec) in zip(out_refs, res, outs):
                c0, w = _cols(spec, g)
                o_ref[:, c0:c0 + w] = val.astype(o_ref.dtype)

    return pl.pallas_call(
        body, name=name, grid=(s // ts,),
        in_specs=[pl.BlockSpec((ts, a.shape[1]), lambda i: (i, 0)) for a, _ in rows]
        + [pl.BlockSpec(p.shape, lambda i: (0, 0)) for p in pars],
        out_specs=[pl.BlockSpec((ts, w), lambda i: (i, 0)) for w, _, _ in outs],
        out_shape=[jax.ShapeDtypeStruct((s, w), dt) for w, dt, _ in outs],
        compiler_params=_params(("parallel",)),
    )(*[a for a, _ in rows], *pars)


def _rowwise_bwd(fn, rows, pars, outs, douts, gmap, gshapes, name, groups=1, add=None, par_grads=True,
                 ts=ROW_TILE):
    s = rows[0][0].shape[0]
    ts = min(ts, s)
    assert s % ts == 0
    nr, npar, nout, ng = len(rows), len(pars), len(outs), len(gshapes)
    add = add or {}
    add_keys = sorted(add)

    def body(*refs):
        row_refs = refs[:nr]
        par_refs = refs[nr:nr + npar]
        dout_refs = refs[nr + npar:nr + npar + nout]
        add_refs = refs[nr + npar + nout:nr + npar + nout + len(add_keys)]
        g_refs = refs[nr + npar + nout + len(add_keys):][:ng]
        pg_refs = refs[nr + npar + nout + len(add_keys) + ng:]
        par_t = [r[...] for r in par_refs]
        par_acc = [None] * npar
        shared_acc = {}
        for g in range(groups):
            row_t = []
            for r, (_, spec) in zip(row_refs, rows):
                c0, w = _cols(spec, g)
                row_t.append(r[:, c0:c0 + w].astype(F32))
            cts = []
            for r, (_, _, spec) in zip(dout_refs, outs):
                c0, w = _cols(spec, g)
                cts.append(r[:, c0:c0 + w].astype(F32))
            _, vjp = jax.vjp(lambda *t, g=g: tuple(fn(g, *t)), *row_t, *par_t)
            grads = vjp(tuple(cts))
            for k in range(nr):
                if gmap[k] is None:
                    continue
                gi, spec = gmap[k]
                if isinstance(spec, list) or groups == 1:
                    c0, w = _cols(spec, g)
                    val = grads[k]
                    if gi in add:
                        val = val + add_refs[add_keys.index(gi)][:, c0:c0 + w].astype(F32)
                    g_refs[gi][:, c0:c0 + w] = val.astype(g_refs[gi].dtype)
                else:
                    shared_acc[k] = grads[k] if k not in shared_acc else shared_acc[k] + grads[k]
            if par_grads:
                for k in range(npar):
                    pg = grads[nr + k]
                    par_acc[k] = pg if par_acc[k] is None else par_acc[k] + pg
        for k, val in shared_acc.items():
            gi, (c0, w) = gmap[k]
            assert gi not in add
            g_refs[gi][:, c0:c0 + w] = val.astype(g_refs[gi].dtype)
        if par_grads:
            first = pl.program_id(0) == 0
            for k in range(npar):
                @pl.when(first)
                def _(k=k):
                    pg_refs[k][...] = par_acc[k]

                @pl.when(jnp.logical_not(first))
                def _(k=k):
                    pg_refs[k][...] += par_acc[k]

    out_specs = [pl.BlockSpec((ts, w), lambda i: (i, 0)) for w, _ in gshapes]
    out_shape = [jax.ShapeDtypeStruct((s, w), dt) for w, dt in gshapes]
    if par_grads:
        out_specs += [pl.BlockSpec(p.shape, lambda i: (0, 0)) for p in pars]
        out_shape += [jax.ShapeDtypeStruct(p.shape, F32) for p in pars]
    return pl.pallas_call(
        body, name=name, grid=(s // ts,),
        in_specs=[pl.BlockSpec((ts, a.shape[1]), lambda i: (i, 0)) for a, _ in rows]
        + [pl.BlockSpec(p.shape, lambda i: (0, 0)) for p in pars]
        + [pl.BlockSpec((ts, a.shape[1]), lambda i: (i, 0)) for a in douts]
        + [pl.BlockSpec((ts, add[k].shape[1]), lambda i: (i, 0)) for k in add_keys],
        out_specs=out_specs, out_shape=out_shape,
        compiler_params=_params(("arbitrary",)),
    )(*[a for a, _ in rows], *pars, *douts, *[add[k] for k in add_keys])


def _sigmoid(x):
    return 1.0 / (1.0 + jnp.exp(-x))


def _silu(x):
    return x * _sigmoid(x)


def _softplus(x):
    return jnp.maximum(x, 0.0) + jnp.log(1.0 + jnp.exp(-jnp.abs(x)))


def _rms(t, g, n=None):
    n = n or t.shape[-1]
    return t * lax.rsqrt(jnp.sum(t * t, axis=-1, keepdims=True) / n + EPS) * g


def _modulate_fn(g, x, gain, scale, shift):
    return (_rms(x, gain) * (1.0 + scale) + shift,)


def _resgate_fn(g, x, y, gm):
    return (x + gm * y,)


def _gate_only_fn(g, y, gm):
    return (gm * y,)


def _gdn_gates_fn(g, b_logit, a_logit, a_log, dt_bias):
    gate = -jnp.exp(a_log) * _softplus(a_logit + dt_bias)
    n = gate.shape[0]
    i = lax.broadcasted_iota(jnp.int32, (n, n), 0)
    j = lax.broadcasted_iota(jnp.int32, (n, n), 1)
    tri = (((i // CHUNK) == (j // CHUNK)) & (i >= j)).astype(F32)
    gcum = lax.dot_general(tri, gate, _NN, preferred_element_type=F32, precision=lax.Precision.HIGHEST)
    return _sigmoid(b_logit), gcum


def _gdn_outnorm_fn(g, o, z, gain):
    return (_rms(o, gain) * _silu(z),)


def _rms_fn(g, t, gain):
    return (_rms(t, gain),)


@jax.custom_vjp
def _swap_halves(t):
    return pltpu.roll(t, 32, 1)


_swap_halves.defvjp(lambda t: (pltpu.roll(t, 32, 1), None), lambda _, ct: (pltpu.roll(ct, 96, 1),))


def _head_norm_rope_fn(g, nope, rope, cosf, sins, gain_n, gain_r):
    first = lax.broadcasted_iota(jnp.int32, rope.shape, 1) < ROPE
    ss = jnp.sum(nope * nope, axis=-1, keepdims=True) + jnp.sum(jnp.where(first, rope * rope, 0.0), axis=-1,
                                                                 keepdims=True)
    r = lax.rsqrt(ss / QK_HEAD + EPS)
    tn = nope * r * gain_n
    tr = rope * r * gain_r
    rot = jnp.where(first, tr * cosf + _swap_halves(tr) * sins, 0.0)
    return tn, rot


def _q_norm_rope_fn(g, nope, rope, cosf, sins, gain_n, gain_r):
    tn, rot = _head_norm_rope_fn(g, nope, rope, cosf, sins, gain_n, gain_r)
    return (jnp.concatenate([tn, rot], axis=1),)


def _k_norm_rope_fn(g, nope, val, rope, cosf, sins, gain_n, gain_r):
    tn, rot = _head_norm_rope_fn(g, nope, rope, cosf, sins, gain_n, gain_r)
    return jnp.concatenate([tn, rot], axis=1), val


def _loss_fn(g, y, target):
    e = y - target
    return (jnp.sum(e * e, axis=-1, keepdims=True) * (0.5 / D) * jnp.ones((1, 128), F32),)


def _ffn_in(h, wg, wu, name, tm=1024, tn=256):
    s = h.shape[0]
    tm = min(tm, s)

    def body(h_ref, wg_ref, wu_ref, g_ref, u_ref, a_ref):
        hb = h_ref[...]
        gate = jnp.dot(hb, wg_ref[...], preferred_element_type=F32)
        up = jnp.dot(hb, wu_ref[...], preferred_element_type=F32)
        g_ref[...] = gate.astype(BF16)
        u_ref[...] = up.astype(BF16)
        a_ref[...] = (_silu(gate) * up).astype(BF16)

    spec = pl.BlockSpec((tm, tn), lambda i, j: (i, j))
    return pl.pallas_call(
        body, name=name, grid=(s // tm, D_FF // tn),
        in_specs=[pl.BlockSpec((tm, D), lambda i, j: (i, 0)), pl.BlockSpec((D, tn), lambda i, j: (0, j)),
                  pl.BlockSpec((D, tn), lambda i, j: (0, j))],
        out_specs=[spec, spec, spec], out_shape=[jax.ShapeDtypeStruct((s, D_FF), BF16)] * 3,
        compiler_params=_params(("parallel", "parallel")),
    )(h, wg, wu)


def _ffn_bwd_act(dy, wo, gate, up, name, tm=1024, tn=256):
    s = dy.shape[0]
    tm = min(tm, s)

    def body(dy_ref, wo_ref, g_ref, u_ref, dg_ref, du_ref):
        dact = lax.dot_general(dy_ref[...], wo_ref[...], _NT, preferred_element_type=F32)
        gate = g_ref[...].astype(F32)
        up = u_ref[...].astype(F32)
        sg = _sigmoid(gate)
        dg_ref[...] = (dact * up * (sg * (1.0 + gate * (1.0 - sg)))).astype(BF16)
        du_ref[...] = (dact * (gate * sg)).astype(BF16)

    spec = pl.BlockSpec((tm, tn), lambda i, j: (i, j))
    return pl.pallas_call(
        body, name=name, grid=(s // tm, D_FF // tn),
        in_specs=[pl.BlockSpec((tm, D), lambda i, j: (i, 0)), pl.BlockSpec((tn, D), lambda i, j: (j, 0)), spec, spec],
        out_specs=[spec, spec], out_shape=[jax.ShapeDtypeStruct((s, D_FF), BF16)] * 2,
        compiler_params=_params(("parallel", "parallel")),
    )(dy, wo, gate, up)


def _shift_down(x, d):
    rows = lax.broadcasted_iota(jnp.int32, x.shape, 0)
    return jnp.where(rows >= d, pltpu.roll(x, d, 0), 0.0)


def _shift_up(x, d):
    n = x.shape[0]
    rows = lax.broadcasted_iota(jnp.int32, x.shape, 0)
    return jnp.where(rows < n - d, pltpu.roll(x, n - d, 0), 0.0)


def _conv_post(pre, is_qk):
    a = _silu(pre)
    l2 = a * lax.rsqrt(jnp.sum(a * a, axis=-1, keepdims=True) + EPS)
    return jnp.where(is_qk, l2, a)


def _conv_pre(x, w):
    pre = x * w[CONV_K - 1:CONV_K, :]
    for j in range(CONV_K - 1):
        pre = pre + _shift_down(x, CONV_K - 1 - j) * w[j:j + 1, :]
    return pre


def _gdn_conv_fwd(pm, conv_w, name):
    s = pm.shape[0]
    nblk = 3 * D // HEAD

    def body(x_ref, w_ref, o_ref):
        is_qk = pl.program_id(0) < 2 * HEADS
        o_ref[...] = _conv_post(_conv_pre(x_ref[...], w_ref[...]), is_qk)

    return pl.pallas_call(
        body, name=name, grid=(nblk,),
        in_specs=[pl.BlockSpec((s, HEAD), lambda c: (0, c)), pl.BlockSpec((CONV_K, HEAD), lambda c: (0, c))],
        out_specs=pl.BlockSpec((s, HEAD), lambda c: (0, c)),
        out_shape=jax.ShapeDtypeStruct((s, 3 * D), F32), compiler_params=_params(("parallel",)),
    )(pm, conv_w)


def _gdn_conv_bwd(pm, conv_w, dout, part, name):
    s = pm.shape[0]
    off = part * HEADS

    def body(x_ref, w_ref, d_ref, dx_ref, dw_ref):
        x, w = x_ref[...], w_ref[...]
        _, vjp = jax.vjp(lambda p: _conv_post(p, part < 2), _conv_pre(x, w))
        dpre, = vjp(d_ref[...])
        dx = dpre * w[CONV_K - 1:CONV_K, :]
        rows = [None] * CONV_K
        rows[CONV_K - 1] = jnp.sum(dpre * x, axis=0, keepdims=True)
        for j in range(CONV_K - 1):
            dx = dx + _shift_up(dpre, CONV_K - 1 - j) * w[j:j + 1, :]
            rows[j] = jnp.sum(dpre * _shift_down(x, CONV_K - 1 - j), axis=0, keepdims=True)
        dx_ref[...] = dx
        dw_ref[...] = jnp.concatenate(rows, axis=0)

    return pl.pallas_call(
        body, name=name, grid=(HEADS,),
        in_specs=[pl.BlockSpec((s, HEAD), lambda c: (0, c + off)), pl.BlockSpec((CONV_K, HEAD), lambda c: (0, c + off)),
                  pl.BlockSpec((s, HEAD), lambda c: (0, c))],
        out_specs=[pl.BlockSpec((s, HEAD), lambda c: (0, c)), pl.BlockSpec((CONV_K, HEAD), lambda c: (0, c))],
        out_shape=[jax.ShapeDtypeStruct((s, D), F32), jax.ShapeDtypeStruct((CONV_K, D), F32)],
        compiler_params=_params(("parallel",)),
    )(pm, conv_w, dout)


def _dot3(a, b, dims=_NN):
    ah, bh = a.astype(BF16), b.astype(BF16)
    al, bl = (a - ah.astype(F32)).astype(BF16), (b - bh.astype(F32)).astype(BF16)
    d = lambda u, v: lax.dot_general(u, v, dims, preferred_element_type=F32)
    return d(ah, bh) + (d(ah, bl) + d(al, bh))


def _make_dot(hi):
    def raw(a, b, dims):
        if hi:
            return _dot3(a, b, dims)
        return lax.dot_general(a.astype(BF16), b.astype(BF16), dims, preferred_element_type=F32)

    @functools.partial(jax.custom_vjp, nondiff_argnums=(2,))
    def dot(a, b, form):
        return raw(a, b, _DIMS[form])

    def fwd(a, b, form):
        return raw(a, b, _DIMS[form]), (a, b)

    def bwd(form, res, ct):
        a, b = res
        if form == "nn":
            return raw(ct, b, _NT), raw(a, ct, _TN)
        if form == "nt":
            return raw(ct, b, _NN), raw(ct, a, _TN)
        return raw(b, ct, _NT), raw(a, ct, _NN)

    dot.defvjp(fwd, bwd)
    return dot


_dot = _make_dot(False)
_dot_hi = _make_dot(True)


def _tri_inv_raw(low):
    n = low.shape[0]
    i = lax.broadcasted_iota(jnp.int32, (n, n), 0)
    j = lax.broadcasted_iota(jnp.int32, (n, n), 1)
    eye = (i == j).astype(F32)
    hdot = _dot3
    same16 = (i // 16) == (j // 16)
    neg = jnp.where(same16, -low, 0.0)
    inv = eye + neg
    power = neg
    for _ in range(3):
        power = hdot(power, power)
        inv = hdot(inv, eye + power)
    for blk in (32, 64):
        off = jnp.where(((i // blk) == (j // blk)) & ((i // (blk // 2)) != (j // (blk // 2))), low, 0.0)
        inv = inv - hdot(inv, hdot(off, inv))
    return inv


@jax.custom_vjp
def _tri_inv(low):
    return _tri_inv_raw(low)


def _tri_inv_fwd(low):
    inv = _tri_inv_raw(low)
    return inv, inv


def _tri_inv_bwd(inv, ct):
    return (-_dot3(_dot3(inv, ct, _TN), inv, _NT),)


_tri_inv.defvjp(_tri_inv_fwd, _tri_inv_bwd)


@jax.custom_vjp
def _tri_inv_given(low, inv):
    return inv


_tri_inv_given.defvjp(lambda low, inv: (inv, inv),
                      lambda inv, ct: (_tri_inv_bwd(inv, ct)[0], jnp.zeros_like(inv)))

GROUP = 4
N_GROUPS = HEADS // GROUP
GROWS = GROUP * CHUNK


def _gdn_group(q, k, v, beta, gc, gr, states, inv=None):
    n = q.shape[0]
    i = lax.broadcasted_iota(jnp.int32, (n, n), 0)
    j = lax.broadcasted_iota(jnp.int32, (n, n), 1)
    same = (i // CHUNK) == (j // CHUNK)
    incl, strict = same & (i >= j), same & (i > j)
    qs = q * (HEAD ** -0.5)
    decay = jnp.where(incl, jnp.exp(jnp.where(incl, gc - gr, 0.0)), 0.0)
    kb = k * beta
    eg = jnp.exp(gc)
    prod = _dot(jnp.concatenate([kb, qs], axis=0), k, "nt")
    low = jnp.where(strict, prod[:n] * decay, 0.0)
    attn = jnp.where(incl, prod[n:] * decay, 0.0)
    inv = _tri_inv(low) if inv is None else _tri_inv_given(low, inv)
    sol = _dot_hi(inv, jnp.concatenate([v * beta, kb * eg], axis=1), "nn")
    u, w, qg = sol[:, :HEAD], sol[:, HEAD:], qs * eg
    last = lax.broadcasted_iota(jnp.int32, (CHUNK, 1), 0) == CHUNK - 1
    v_new, o_state, carry = [], [], []
    for h, state in enumerate(states):
        rows = slice(h * CHUNK, (h + 1) * CHUNK)
        ws = _dot(jnp.concatenate([w[rows], qg[rows]], axis=0), state, "nn")
        v_new.append(u[rows] - ws[:CHUNK])
        o_state.append(ws[CHUNK:])
        g_last = jnp.sum(jnp.where(last, gc[rows], 0.0), axis=0, keepdims=True)
        carry.append((g_last, k[rows] * jnp.exp(g_last - gc[rows])))
    o = jnp.concatenate(o_state, axis=0) + _dot(attn, jnp.concatenate(v_new, axis=0), "nn")
    new = tuple(state * jnp.exp(g_last) + _dot(k_dec, vn, "tn")
                for state, (g_last, k_dec), vn in zip(states, carry, v_new))
    return o, new, inv


def _gdn_specs(s, rev):
    nc = s // CHUNK
    at = (lambda n: nc - 1 - n) if rev else (lambda n: n)
    return nc, at, [
        pl.BlockSpec((CHUNK, D), lambda n: (at(n), 0)), pl.BlockSpec((CHUNK, D), lambda n: (at(n), 1)),
        pl.BlockSpec((CHUNK, D), lambda n: (at(n), 2)), pl.BlockSpec((CHUNK, HEAD), lambda n: (at(n), 0)),
        pl.BlockSpec((CHUNK, HEAD), lambda n: (at(n), 0)),
        pl.BlockSpec((None, N_GROUPS, 1, GROWS), lambda n: (at(n), 0, 0, 0))]


def _group_operands(grp, q_ref, k_ref, v_ref, b_blk, gc_blk, gr_blk):
    heads = range(grp * GROUP, (grp + 1) * GROUP)
    stack = lambda ref: jnp.concatenate([ref[:, h * HEAD:(h + 1) * HEAD] for h in heads], axis=0)
    col = lambda blk: jnp.concatenate([blk[:, h:h + 1] for h in heads], axis=0)
    return stack(q_ref), stack(k_ref), stack(v_ref), col(b_blk), col(gc_blk), gr_blk[grp]


def _gdn_scan_fwd(qkv, beta, gcum, grow, name):
    s = qkv.shape[0]
    nc, _, in_specs = _gdn_specs(s, rev=False)

    def body(q_ref, k_ref, v_ref, b_ref, gc_ref, gr_ref, o_ref, st_ref, inv_ref, state):
        @pl.when(pl.program_id(0) == 0)
        def _():
            state[...] = jnp.zeros_like(state)

        b_blk, gc_blk, gr_blk = b_ref[...], gc_ref[...], gr_ref[...]
        old = [state[h] for h in range(HEADS)]
        res = [_gdn_group(*_group_operands(grp, q_ref, k_ref, v_ref, b_blk, gc_blk, gr_blk),
                          old[grp * GROUP:(grp + 1) * GROUP]) for grp in range(N_GROUPS)]
        for grp, (o, new, inv) in enumerate(res):
            inv_ref[grp] = inv
            for hh in range(GROUP):
                h = grp * GROUP + hh
                st_ref[h] = old[h]
                o_ref[:, h * HEAD:(h + 1) * HEAD] = o[hh * CHUNK:(hh + 1) * CHUNK]
                state[h] = new[hh]

    return pl.pallas_call(
        body, name=name, grid=(nc,), in_specs=in_specs,
        out_specs=[pl.BlockSpec((CHUNK, D), lambda n: (n, 0)),
                   pl.BlockSpec((None, HEADS, HEAD, HEAD), lambda n: (n, 0, 0, 0)),
                   pl.BlockSpec((None, N_GROUPS, GROWS, GROWS), lambda n: (n, 0, 0, 0))],
        out_shape=[jax.ShapeDtypeStruct((s, D), F32), jax.ShapeDtypeStruct((nc, HEADS, HEAD, HEAD), F32),
                   jax.ShapeDtypeStruct((nc, N_GROUPS, GROWS, GROWS), F32)],
        scratch_shapes=[pltpu.VMEM((HEADS, HEAD, HEAD), F32)],
        compiler_params=_params(("arbitrary",)),
    )(qkv, qkv, qkv, beta, gcum, grow)


def _gdn_scan_bwd(qkv, beta, gcum, grow, states, invs, do, name):
    s = qkv.shape[0]
    nc, at, in_specs = _gdn_specs(s, rev=True)
    in_specs += [pl.BlockSpec((None, HEADS, HEAD, HEAD), lambda n: (at(n), 0, 0, 0)),
                 pl.BlockSpec((None, N_GROUPS, GROWS, GROWS), lambda n: (at(n), 0, 0, 0)),
                 pl.BlockSpec((CHUNK, D), lambda n: (at(n), 0))]

    def body(q_ref, k_ref, v_ref, b_ref, gc_ref, gr_ref, st_ref, inv_ref, do_ref, dq_ref, dk_ref, dv_ref, db_ref,
             dgc_ref, dgr_ref, dstate):
        @pl.when(pl.program_id(0) == 0)
        def _():
            dstate[...] = jnp.zeros_like(dstate)

        b_blk, gc_blk, gr_blk = b_ref[...], gc_ref[...], gr_ref[...]
        dold = [dstate[h] for h in range(HEADS)]
        res = []
        for grp in range(N_GROUPS):
            heads = range(grp * GROUP, (grp + 1) * GROUP)
            inv = inv_ref[grp]
            _, vjp = jax.vjp(lambda q, k, v, b, gc, gr, *st, inv=inv: _gdn_group(q, k, v, b, gc, gr, st, inv)[:2],
                             *_group_operands(grp, q_ref, k_ref, v_ref, b_blk, gc_blk, gr_blk),
                             *[st_ref[h] for h in heads])
            d_out = jnp.concatenate([do_ref[:, h * HEAD:(h + 1) * HEAD] for h in heads], axis=0)
            res.append(vjp((d_out, tuple(dold[h] for h in heads))))
        lane = lax.broadcasted_iota(jnp.int32, (CHUNK, HEAD), 1)
        db_all = jnp.zeros((CHUNK, HEAD), F32)
        dgc_all = jnp.zeros((CHUNK, HEAD), F32)
        for grp, (dq, dk, dv, db, dgc, dgr, *dst) in enumerate(res):
            dgr_ref[grp] = dgr
            for hh in range(GROUP):
                h = grp * GROUP + hh
                cs, rows = slice(h * HEAD, (h + 1) * HEAD), slice(hh * CHUNK, (hh + 1) * CHUNK)
                dq_ref[:, cs] = dq[rows]
                dk_ref[:, cs] = dk[rows]
                dv_ref[:, cs] = dv[rows]
                dstate[h] = dst[hh]
                db_all = jnp.where(lane == h, db[rows], db_all)
                dgc_all = jnp.where(lane == h, dgc[rows], dgc_all)
        db_ref[...] = db_all
        dgc_ref[...] = dgc_all

    blk = pl.BlockSpec((CHUNK, D), lambda n: (at(n), 0))
    gblk = pl.BlockSpec((CHUNK, HEAD), lambda n: (at(n), 0))
    return pl.pallas_call(
        body, name=name, grid=(nc,), in_specs=in_specs,
        out_specs=[blk, blk, blk, gblk, gblk, pl.BlockSpec((None, N_GROUPS, 1, GROWS), lambda n: (at(n), 0, 0, 0))],
        out_shape=[jax.ShapeDtypeStruct((s, D), F32)] * 3 + [jax.ShapeDtypeStruct((s, HEAD), F32)] * 2
        + [jax.ShapeDtypeStruct((nc, N_GROUPS, 1, GROWS), F32)],
        scratch_shapes=[pltpu.VMEM((HEADS, HEAD, HEAD), F32)],
        compiler_params=_params(("arbitrary",)),
    )(qkv, qkv, qkv, beta, gcum, grow, states, invs, do)


ATT_TILE = 512
ATT_SCALE = QK_HEAD ** -0.5


def _att_mask(t):
    qpos = lax.broadcasted_iota(jnp.int32, (t, t), 0)
    kpos = lax.broadcasted_iota(jnp.int32, (t, t), 1)
    return (kpos // CHUNK) <= (qpos // CHUNK)


def _att_pairs(nb, by_query):
    if by_query:
        pairs = [(i, j) for i in range(nb) for j in range(i + 1)]
    else:
        pairs = [(j, i) for j in range(nb) for i in range(j, nb)]
    return jnp.array([a for a, _ in pairs], jnp.int32), jnp.array([b for _, b in pairs], jnp.int32)


def _attn_fwd(q, k, v, name):
    s = q.shape[0]
    t = min(ATT_TILE, s)
    nb = s // t
    ii, jj = _att_pairs(nb, by_query=True)

    def body(ii_ref, jj_ref, q_ref, k_ref, v_ref, o_ref, lse_ref, m_s, l_s, acc):
        step = pl.program_id(1)
        i, j = ii_ref[step], jj_ref[step]

        @pl.when(j == 0)
        def _():
            m_s[...] = jnp.full_like(m_s, -jnp.inf)
            l_s[...] = jnp.zeros_like(l_s)
            acc[...] = jnp.zeros_like(acc)

        sc = lax.dot_general(q_ref[...], k_ref[...], _NT, preferred_element_type=F32) * ATT_SCALE
        sc = lax.cond(i == j, lambda u: jnp.where(_att_mask(t), u, -jnp.inf), lambda u: u, sc)
        m_new = jnp.maximum(m_s[...], jnp.max(sc, axis=-1, keepdims=True))
        alpha = jnp.exp(m_s[...] - m_new)
        p = jnp.exp(sc - m_new)
        l_s[...] = alpha * l_s[...] + jnp.sum(p, axis=-1, keepdims=True)
        acc[...] = alpha * acc[...] + jnp.dot(p.astype(BF16), v_ref[...], preferred_element_type=F32)
        m_s[...] = m_new

        @pl.when(j == i)
        def _():
            o_ref[...] = acc[...] / l_s[...]
            lse_ref[...] = m_s[...] + jnp.log(l_s[...])

    grid_spec = pltpu.PrefetchScalarGridSpec(
        num_scalar_prefetch=2, grid=(HEADS, len(ii)),
        in_specs=[pl.BlockSpec((t, HEAD_PAD), lambda h, n, ir, jr: (ir[n], h)),
                  pl.BlockSpec((t, HEAD_PAD), lambda h, n, ir, jr: (jr[n], h)),
                  pl.BlockSpec((t, HEAD), lambda h, n, ir, jr: (jr[n], h))],
        out_specs=[pl.BlockSpec((t, HEAD), lambda h, n, ir, jr: (ir[n], h)),
                   pl.BlockSpec((None, t, 1), lambda h, n, ir, jr: (h, ir[n], 0))],
        scratch_shapes=[pltpu.VMEM((t, 1), F32), pltpu.VMEM((t, 1), F32), pltpu.VMEM((t, HEAD), F32)])
    return pl.pallas_call(
        body, name=name, grid_spec=grid_spec,
        out_shape=[jax.ShapeDtypeStruct((s, HEADS * HEAD), F32), jax.ShapeDtypeStruct((HEADS, s, 1), F32)],
        compiler_params=_params(("parallel", "arbitrary")),
    )(ii, jj, q, k, v)


def _attn_bwd(q, k, v, do, o, lse, name):
    s = q.shape[0]
    t = min(ATT_TILE, s)
    nb = s // t
    jj, ii = _att_pairs(nb, by_query=False)

    def body(jj_ref, ii_ref, q_ref, k_ref, v_ref, do_ref, o_ref, lse_ref, dq_ref, dk_ref, dv_ref, dk_acc, dv_acc):
        step = pl.program_id(1)
        i, j = ii_ref[step], jj_ref[step]

        @pl.when(step == 0)
        def _():
            dq_ref[...] = jnp.zeros_like(dq_ref)

        @pl.when(i == j)
        def _():
            dk_acc[...] = jnp.zeros_like(dk_acc)
            dv_acc[...] = jnp.zeros_like(dv_acc)

        sc = lax.dot_general(q_ref[...], k_ref[...], _NT, preferred_element_type=F32) * ATT_SCALE
        p = jnp.exp(sc - lse_ref[...])
        p = lax.cond(i == j, lambda u: jnp.where(_att_mask(t), u, 0.0), lambda u: u, p)
        do_f = do_ref[...]
        dob = do_f.astype(BF16)
        delta = jnp.sum(do_f * o_ref[...], axis=-1, keepdims=True)
        dv_acc[...] += lax.dot_general(p.astype(BF16), dob, _TN, preferred_element_type=F32)
        dp = lax.dot_general(dob, v_ref[...], _NT, preferred_element_type=F32)
        ds = (p * (dp - delta) * ATT_SCALE).astype(BF16)
        dk_acc[...] += lax.dot_general(ds, q_ref[...], _TN, preferred_element_type=F32)
        rows = pl.ds(pl.multiple_of(i * t, t), t)
        dq_ref[rows, :] += jnp.dot(ds, k_ref[...], preferred_element_type=F32)

        @pl.when(i == nb - 1)
        def _():
            dk_ref[...] = dk_acc[...]
            dv_ref[...] = dv_acc[...]

    grid_spec = pltpu.PrefetchScalarGridSpec(
        num_scalar_prefetch=2, grid=(HEADS, len(jj)),
        in_specs=[pl.BlockSpec((t, HEAD_PAD), lambda h, n, jr, ir: (ir[n], h)),
                  pl.BlockSpec((t, HEAD_PAD), lambda h, n, jr, ir: (jr[n], h)),
                  pl.BlockSpec((t, HEAD), lambda h, n, jr, ir: (jr[n], h)),
                  pl.BlockSpec((t, HEAD), lambda h, n, jr, ir: (ir[n], h)),
                  pl.BlockSpec((t, HEAD), lambda h, n, jr, ir: (ir[n], h)),
                  pl.BlockSpec((None, t, 1), lambda h, n, jr, ir: (h, ir[n], 0))],
        out_specs=[pl.BlockSpec((s, HEAD_PAD), lambda h, n, jr, ir: (0, h)),
                   pl.BlockSpec((t, HEAD_PAD), lambda h, n, jr, ir: (jr[n], h)),
                   pl.BlockSpec((t, HEAD), lambda h, n, jr, ir: (jr[n], h))],
        scratch_shapes=[pltpu.VMEM((t, HEAD_PAD), F32), pltpu.VMEM((t, HEAD), F32)])
    return pl.pallas_call(
        body, name=name, grid_spec=grid_spec,
        out_shape=[jax.ShapeDtypeStruct((s, HEADS * HEAD_PAD), F32)] * 2 + [jax.ShapeDtypeStruct((s, HEADS * HEAD), F32)],
        compiler_params=_params(("parallel", "arbitrary")),
    )(jj, ii, q, k, v, do, o, lse)


def _rope_tables(positions):
    half = ROPE // 2
    inv_freq = ROPE_BASE ** (-jnp.arange(half, dtype=F32) / half)
    ang = positions.astype(F32)[:, None] * inv_freq
    cos, sin = jnp.cos(ang), jnp.sin(ang)
    return jnp.concatenate([cos] * 4, axis=1), jnp.concatenate([-sin, sin] * 2, axis=1)


def _loss_and_grad(y, target, name):
    s = y.shape[0]
    ts = min(ROW_TILE, s)

    def body(y_ref, t_ref, dy_ref, l_ref):
        e = y_ref[...] - t_ref[...]
        dy_ref[...] = e * (1.0 / D)
        part = jnp.sum(jnp.sum(e * e, axis=-1, keepdims=True) * (0.5 / D), axis=0, keepdims=True)
        part = part * jnp.ones((1, 128), F32)

        @pl.when(pl.program_id(0) == 0)
        def _():
            l_ref[...] = part

        @pl.when(pl.program_id(0) > 0)
        def _():
            l_ref[...] += part

    return pl.pallas_call(
        body, name=name, grid=(s // ts,),
        in_specs=[pl.BlockSpec((ts, D), lambda i: (i, 0))] * 2,
        out_specs=[pl.BlockSpec((ts, D), lambda i: (i, 0)), pl.BlockSpec((1, 128), lambda i: (0, 0))],
        out_shape=[jax.ShapeDtypeStruct((s, D), F32), jax.ShapeDtypeStruct((1, 128), F32)],
        compiler_params=_params(("arbitrary",)),
    )(y, target)


ANY = pl.BlockSpec(memory_space=pl.ANY)


def _all_gather(shard, name):
    def body(x_ref, out_ref, send_sems, recv_sems, local_sem):
        x, y, c = lax.axis_index("x"), lax.axis_index("y"), lax.axis_index("c")
        me, sibling = (x, y, c), (x, y, 1 - c)
        chips = [(1 - x, y), (x, 1 - y), (1 - x, 1 - y)]

        def rows(px, py, pc):
            return out_ref.at[4 * px + 2 * py + pc]

        def copy(k, block, to, src=None):
            return pltpu.make_async_remote_copy(
                src_ref=rows(*block) if src is None else src, dst_ref=rows(*block),
                send_sem=send_sems.at[k], recv_sem=recv_sems.at[k], device_id=to, device_id_type=MESH)

        mine = pltpu.make_async_copy(x_ref, rows(*me), local_sem)
        mine.start()
        first = [copy(0, me, sibling, src=x_ref)]
        first += [copy(1 + j, me, (*chip, c), src=x_ref) for j, chip in enumerate(chips)]
        for cp in first:
            cp.start()
        passed = [copy(4 + j, (*chip, c), sibling) for j, chip in enumerate(chips)]
        for j, chip in enumerate(chips):
            copy(1 + j, (*chip, c), me).wait_recv()
            passed[j].start()
        copy(0, sibling, me).wait_recv()
        for j, chip in enumerate(chips):
            copy(4 + j, (*chip, 1 - c), me).wait_recv()
        for cp in first + passed:
            cp.wait_send()
        mine.wait()

    return pl.pallas_call(
        body, name=name, out_shape=jax.ShapeDtypeStruct((N_DEV,) + shard.shape, shard.dtype),
        in_specs=[ANY], out_specs=ANY,
        scratch_shapes=[pltpu.SemaphoreType.DMA((7,)), pltpu.SemaphoreType.DMA((7,)), pltpu.SemaphoreType.DMA],
    )(shard)


def _exchange(blocks, name):
    def body(x_ref, out_ref, send_sems, recv_sems, local_sem):
        x, y, c = lax.axis_index("x"), lax.axis_index("y"), lax.axis_index("c")
        me = 4 * x + 2 * y + c
        mine = pltpu.make_async_copy(x_ref.at[me], out_ref.at[me], local_sem)
        mine.start()
        copies = []
        for k in range(1, N_DEV):
            px = 1 - x if k & 4 else x
            py = 1 - y if k & 2 else y
            pc = 1 - c if k & 1 else c
            peer = 4 * px + 2 * py + pc
            cp = pltpu.make_async_remote_copy(
                src_ref=x_ref.at[peer], dst_ref=out_ref.at[me], send_sem=send_sems.at[k - 1],
                recv_sem=recv_sems.at[k - 1], device_id=(px, py, pc), device_id_type=MESH)
            cp.start()
            copies.append((cp, pltpu.make_async_remote_copy(
                src_ref=x_ref.at[peer], dst_ref=out_ref.at[peer], send_sem=send_sems.at[k - 1],
                recv_sem=recv_sems.at[k - 1], device_id=(px, py, pc), device_id_type=MESH)))
        for cp, landing in copies:
            landing.wait_recv()
        for cp, landing in copies:
            cp.wait_send()
        mine.wait()

    return pl.pallas_call(
        body, name=name, out_shape=jax.ShapeDtypeStruct(blocks.shape, blocks.dtype),
        in_specs=[ANY], out_specs=ANY,
        scratch_shapes=[pltpu.SemaphoreType.DMA((7,)), pltpu.SemaphoreType.DMA((7,)), pltpu.SemaphoreType.DMA],
    )(blocks)


HBM = pl.BlockSpec(memory_space=pltpu.HBM)
SEM = pl.BlockSpec(memory_space=pltpu.SEMAPHORE)
EFFECT = pltpu.SideEffectType.DATAFLOW_SIDE_EFFECTING


def _peers():
    x, y, c = lax.axis_index("x"), lax.axis_index("y"), lax.axis_index("c")
    peers = []
    for k in range(1, N_DEV):
        px = 1 - x if k & 4 else x
        py = 1 - y if k & 2 else y
        pc = 1 - c if k & 1 else c
        peers.append(((px, py, pc), 4 * px + 2 * py + pc))
    return 4 * x + 2 * y + c, peers


def _send_start(src, name, gather):
    land_shape = ((N_DEV,) + src.shape) if gather else src.shape

    def body(src_ref, land_ref, send_sems, recv_sems, src_thru, land_thru, token):
        me, peers = _peers()
        for k, (dev, idx) in enumerate(peers):
            pltpu.make_async_remote_copy(
                src_ref=src_ref if gather else src_ref.at[idx], dst_ref=land_ref.at[me], send_sem=send_sems.at[k],
                recv_sem=recv_sems.at[k], device_id=dev, device_id_type=MESH).start()
        token[...] = jnp.zeros_like(token)

    return pl.pallas_call(
        body, name=name,
        out_shape=(pltpu.SemaphoreType.DMA((N_DEV - 1,)), pltpu.SemaphoreType.DMA((N_DEV - 1,)),
                   pltpu.HBM(src.shape, src.dtype), pltpu.HBM(land_shape, src.dtype),
                   jax.ShapeDtypeStruct((8, 128), F32)),
        in_specs=(HBM, HBM), out_specs=(SEM, SEM, HBM, HBM, pl.BlockSpec(memory_space=pltpu.VMEM)),
        input_output_aliases={0: 2, 1: 3}, compiler_params=pltpu.CompilerParams(has_side_effects=EFFECT),
    )(pltpu.with_memory_space_constraint(src, pltpu.HBM),
      pltpu.with_memory_space_constraint(lax.empty(land_shape, src.dtype), pltpu.HBM))


def _send_wait(handle, after, name, gather):
    send_sems, recv_sems, src_thru, land_thru, _ = handle

    def body(src_ref, land_ref, send_sems, recv_sems, after_ref, src_dead, got_ref):
        me, peers = _peers()
        for k, (dev, idx) in enumerate(peers):
            cp = pltpu.make_async_remote_copy(
                src_ref=src_ref if gather else src_ref.at[idx], dst_ref=land_ref.at[idx], send_sem=send_sems.at[k],
                recv_sem=recv_sems.at[k], device_id=dev, device_id_type=MESH)
            cp.wait_send()
            cp.wait_recv()

    return pl.pallas_call(
        body, name=name,
        out_shape=(pltpu.HBM(src_thru.shape, src_thru.dtype), pltpu.HBM(land_thru.shape, land_thru.dtype)),
        in_specs=(HBM, HBM, SEM, SEM, pl.BlockSpec(memory_space=pl.ANY)), out_specs=(HBM, HBM),
        input_output_aliases={0: 0, 1: 1}, compiler_params=pltpu.CompilerParams(has_side_effects=EFFECT),
    )(src_thru, land_thru, send_sems, recv_sems, after)[1]


def _adamw(parts, w, m, v, name, tr=128):
    n, r, wd = parts.shape
    tr = next((t for t in (tr, 64, 32, 16) if r % t == 0), r)

    def body(p_ref, w_ref, m_ref, v_ref, g_ref, d_ref, nm_ref, nv_ref):
        g = p_ref[0].astype(F32)
        for k in range(1, n):
            g = g + p_ref[k].astype(F32)
        m_new = B1 * m_ref[...] + (1.0 - B1) * g
        v_new = B2 * v_ref[...] + (1.0 - B2) * (g * g)
        m_hat = m_new / (1.0 - B1 ** STEP)
        v_hat = v_new / (1.0 - B2 ** STEP)
        g_ref[...] = g
        d_ref[...] = -LR * (m_hat / (jnp.sqrt(v_hat) + ADAM_EPS) + WD * w_ref[...])
        nm_ref[...] = m_new
        nv_ref[...] = v_new

    blk = pl.BlockSpec((tr, wd), lambda i: (i, 0))
    return pl.pallas_call(
        body, name=name, grid=(r // tr,),
        in_specs=[pl.BlockSpec((n, tr, wd), lambda i: (0, i, 0)), blk, blk, blk],
        out_specs=[blk] * 4, out_shape=[jax.ShapeDtypeStruct((r, wd), F32)] * 4,
        compiler_params=_params(("parallel",)),
    )(parts, w, m, v)


def _outer8(ct, dm, name):
    k, n = ct.shape[0], dm.shape[1]

    def body(c_ref, d_ref, o_ref):
        cv, dv = c_ref[...], d_ref[...]
        acc = cv[:, 0:1] * dv[0:1, :]
        for s in range(1, N_DEV):
            acc = acc + cv[:, s:s + 1] * dv[s:s + 1, :]
        o_ref[...] = acc

    tk = 256
    return pl.pallas_call(
        body, name=name, grid=(k // tk,),
        in_specs=[pl.BlockSpec((tk, N_DEV), lambda i: (i, 0)), pl.BlockSpec((N_DEV, n), lambda i: (0, 0))],
        out_specs=pl.BlockSpec((tk, n), lambda i: (i, 0)), out_shape=jax.ShapeDtypeStruct((k, n), F32),
        compiler_params=_params(("parallel",)),
    )(ct, dm)


FULL = (0, D)
C128 = (0, 128)
HEAD_NOPE = [(h * HEAD_PAD, NOPE) for h in range(HEADS)]
HEAD_ROPE = [(h * HEAD_PAD + NOPE, 128) for h in range(HEADS)]
HEAD_ALL = [(h * HEAD_PAD, HEAD_PAD) for h in range(HEADS)]
HEAD_V = [(h * HEAD, HEAD) for h in range(HEADS)]


def _modulate(x, gain, scale, shift):
    return _rowwise_fwd(_modulate_fn, [(x, FULL)], [gain, scale, shift], [(D, BF16, FULL)], "modulate")[0]


def _modulate_bwd(x, gain, scale, shift, dh, dx_in):
    return _rowwise_bwd(_modulate_fn, [(x, FULL)], [gain, scale, shift], [(D, BF16, FULL)], [dh], [(0, FULL)],
                        [(D, F32)], "modulate_bwd", add={0: dx_in})


def _residual(x, y, gm):
    return _rowwise_fwd(_resgate_fn, [(x, FULL), (y, FULL)], [gm], [(D, F32, FULL)], "residual")[0]


def _residual_bwd(y, gm, dxn):
    return _rowwise_bwd(_gate_only_fn, [(y, FULL)], [gm], [(D, F32, FULL)], [dxn], [(0, FULL)], [(D, BF16)],
                        "residual_bwd")


def _ffn_fwd(x, p):
    h = _modulate(x, p["gain"], p["scale"], p["shift"])
    gate, up, act = _ffn_in(h, p["wg"], p["wu"], "ffn_in")
    y = _mm(act, p["wo"], "nn", "ffn_out")
    return _residual(x, y, p["gm"]), dict(x=x, h=h, gate=gate, up=up, act=act, y=y)


def _ffn_bwd(t, p, dxn):
    dy, dgm = _residual_bwd(t["y"], p["gm"], dxn)
    dgate, dup = _ffn_bwd_act(dy, p["wo"], t["gate"], t["up"], "ffn_bwd_act")
    dwo = _mm(t["act"], dy, "tn", "ffn_dwo")
    dh = _matmul([(dgate, p["wg"]), (dup, p["wu"])], "nt", "ffn_dh")
    dwg = _mm(t["h"], dgate, "tn", "ffn_dwi")
    dwu = _mm(t["h"], dup, "tn", "ffn_dwi")
    dx, dgain, dscale, dshift = _modulate_bwd(t["x"], p["gain"], p["scale"], p["shift"], dh, dxn)
    return dx, dict(gain=dgain, scale=dscale, shift=dshift, gm=dgm, wg=dwg, wu=dwu, wo=dwo)


def _pad128(t):
    return jnp.pad(t, ((0, 0), (0, 128 - t.shape[1])))


def _gdn_fwd(x, p):
    s = x.shape[0]
    h = _modulate(x, p["gain"], p["scale"], p["shift"])
    pm = _mm(h, p["w_main"], "nn", "gdn_proj")
    tail = _mm(h, p["w_tail"], "nn", "gdn_proj_tail")
    qkv = _gdn_conv_fwd(pm, p["conv_w"], "gdn_conv")
    beta, gcum = _rowwise_fwd(_gdn_gates_fn, [(tail, C128), (tail, (128, 128))], [p["a_log"], p["dt_bias"]],
                              [(128, F32, C128)] * 2, "gdn_gates")
    grow = gcum[:, :HEADS].reshape(s // CHUNK, CHUNK, N_GROUPS, GROUP).transpose(0, 2, 3, 1)
    grow = grow.reshape(s // CHUNK, N_GROUPS, 1, GROWS)
    o, states, invs = _gdn_scan_fwd(qkv, beta, gcum, grow, "gdn_scan")
    on, = _rowwise_fwd(_gdn_outnorm_fn, [(o, HEAD_V), (pm, [(3 * D + h_ * HEAD, HEAD) for h_ in range(HEADS)])],
                       [p["norm_g"]], [(D, BF16, HEAD_V)], "gdn_outnorm", groups=HEADS)
    y = _mm(on, p["w_out"], "nn", "mix_out")
    t = dict(x=x, h=h, pm=pm, tail=tail, qkv=qkv, beta=beta, gcum=gcum, grow=grow, o=o, states=states, invs=invs,
             on=on, y=y)
    return _residual(x, y, p["gm"]), t


def _gdn_bwd(t, p, dxn):
    s = dxn.shape[0]
    zc = [(3 * D + h_ * HEAD, HEAD) for h_ in range(HEADS)]
    dy, dgm = _residual_bwd(t["y"], p["gm"], dxn)
    dw_out = _mm(t["on"], dy, "tn", "mix_dwo")
    don = _mm(dy, p["w_out"], "nt", "mix_dout")
    do, dz, dnorm_g = _rowwise_bwd(_gdn_outnorm_fn, [(t["o"], HEAD_V), (t["pm"], zc)], [p["norm_g"]],
                                   [(D, BF16, HEAD_V)], [don], [(0, HEAD_V), (1, HEAD_V)], [(D, F32), (D, F32)],
                                   "gdn_outnorm_bwd", groups=HEADS)
    dq, dk, dv, dbeta, dg, dgr = _gdn_scan_bwd(t["qkv"], t["beta"], t["gcum"], t["grow"], t["states"], t["invs"], do,
                                               "gdn_scan_bwd")
    dg = dg + _pad128(dgr.reshape(s // CHUNK, N_GROUPS, GROUP, CHUNK).transpose(0, 3, 1, 2).reshape(s, HEADS))
    dtail, da_log, ddt = _rowwise_bwd(_gdn_gates_fn, [(t["tail"], C128), (t["tail"], (128, 128))],
                                      [p["a_log"], p["dt_bias"]], [(128, F32, C128)] * 2, [dbeta, dg],
                                      [(0, C128), (0, (128, 128))], [(256, F32)], "gdn_gates_bwd")
    dxs, dcw = [], []
    for part, d in enumerate((dq, dk, dv)):
        dx_, dw_ = _gdn_conv_bwd(t["pm"], p["conv_w"], d, part, "gdn_conv_bwd")
        dxs.append(dx_)
        dcw.append(dw_)
    pieces = dxs + [dz]
    dh = _matmul([(d, p["w_main"]) for d in pieces] + [(dtail, p["w_tail"])], "nt", "gdn_dh",
                 boffs=[0, D, 2 * D, 3 * D, 0], tk=512)
    dw_main = [_mm(t["h"], d, "tn", "gdn_dwi") for d in pieces]
    dw_tail = _mm(t["h"], dtail, "tn", "gdn_dwi_tail")
    dx, dgain, dscale, dshift = _modulate_bwd(t["x"], p["gain"], p["scale"], p["shift"], dh, dxn)
    return dx, dict(gain=dgain, scale=dscale, shift=dshift, gm=dgm, w_main=jnp.concatenate(dw_main, axis=1),
                    w_tail=dw_tail, conv_w=jnp.concatenate(dcw, axis=1), a_log=da_log, dt_bias=ddt,
                    norm_g=dnorm_g, w_out=dw_out)


def _q_rows(q2, cosf, sins):
    return [(q2, HEAD_NOPE), (q2, HEAD_ROPE), (cosf, C128), (sins, C128)]


def _mla_fwd(x, p, kv):
    h = _modulate(x, p["gain"], p["scale"], p["shift"])
    cq = _mm(h, p["w_dq"], "nn", "mla_dq")
    cqn, = _rowwise_fwd(_rms_fn, [(cq, (0, Q_LORA))], [p["q_lora_g"]], [(Q_LORA, BF16, (0, Q_LORA))], "mla_qlora_norm")
    q2 = _mm(cqn, p["w_uq"], "nn", "mla_uq")
    qn, = _rowwise_fwd(_q_norm_rope_fn, _q_rows(q2, kv["cosf"], kv["sins"]), [p["q_gn"], p["q_gr"]],
                       [(HEADS * HEAD_PAD, BF16, HEAD_ALL)], "mla_q_norm", groups=HEADS)
    o, lse = _attn_fwd(qn, kv["kn"], kv["vb"], "mla_attn")
    y = _mm(o, p["w_out"], "nn", "mix_out")
    return _residual(x, y, p["gm"]), dict(x=x, h=h, cq=cq, cqn=cqn, q2=q2, qn=qn, o=o, lse=lse, y=y)


def _mla_bwd(t, p, kv, dxn):
    dy, dgm = _residual_bwd(t["y"], p["gm"], dxn)
    dw_out = _mm(t["o"], dy, "tn", "mix_dwo")
    do = _mm(dy, p["w_out"], "nt", "mix_dout")
    dq, dk, dv = _attn_bwd(t["qn"], kv["kn"], kv["vb"], do, t["o"], t["lse"], "mla_attn_bwd")
    dq2, dq_gn, dq_gr = _rowwise_bwd(_q_norm_rope_fn, _q_rows(t["q2"], kv["cosf"], kv["sins"]), [p["q_gn"], p["q_gr"]],
                                     [(HEADS * HEAD_PAD, BF16, HEAD_ALL)], [dq],
                                     [(0, HEAD_NOPE), (0, HEAD_ROPE), None, None], [(HEADS * HEAD_PAD, F32)],
                                     "mla_q_norm_bwd", groups=HEADS)
    dw_uq = _mm(t["cqn"], dq2, "tn", "mla_dwuq")
    dcqn = _mm(dq2, p["w_uq"], "nt", "mla_dcq")
    dcq, dq_lora_g = _rowwise_bwd(_rms_fn, [(t["cq"], (0, Q_LORA))], [p["q_lora_g"]], [(Q_LORA, BF16, (0, Q_LORA))],
                                  [dcqn], [(0, (0, Q_LORA))], [(Q_LORA, F32)], "mla_qlora_norm_bwd")
    dw_dq = _mm(t["h"], dcq, "tn", "mla_dwdq")
    dh = _mm(dcq, p["w_dq"], "nt", "mla_dh")
    dx, dgain, dscale, dshift = _modulate_bwd(t["x"], p["gain"], p["scale"], p["shift"], dh, dxn)
    grads = dict(gain=dgain, scale=dscale, shift=dshift, gm=dgm, w_dq=dw_dq, q_lora_g=dq_lora_g, w_uq=dw_uq,
                 q_gn=dq_gn, q_gr=dq_gr, w_out=dw_out)
    return dx, grads, dk, dv


def _k_rows(kvp, ckv, cosf, sins):
    return [(kvp, HEAD_NOPE), (kvp, HEAD_ROPE), (ckv, (KV_LORA, 128)), (cosf, C128), (sins, C128)]


def _kv_fwd(x, p, cosf, sins):
    h = _modulate(x, p["gain"], p["scale"], p["shift"])
    ckv = _mm(h, p["w_dkv"], "nn", "kv_down")
    lat, = _rowwise_fwd(_rms_fn, [(ckv, (0, KV_LORA))], [p["kv_g"]], [(KV_LORA, BF16, (0, KV_LORA))], "kv_norm")
    kvp = _mm(lat, p["w_ukv"], "nn", "kv_up")
    kn, vb = _rowwise_fwd(_k_norm_rope_fn, _k_rows(kvp, ckv, cosf, sins), [p["k_gn"], p["k_gr"]],
                          [(HEADS * HEAD_PAD, BF16, HEAD_ALL), (HEADS * HEAD, BF16, HEAD_V)], "kv_k_norm",
                          groups=HEADS)
    return dict(x=x, h=h, ckv=ckv, lat=lat, kvp=kvp, kn=kn, vb=vb, cosf=cosf, sins=sins)


def _kv_bwd(t, p, dk, dv, dx_in):
    dkvp, drope, dk_gn, dk_gr = _rowwise_bwd(
        _k_norm_rope_fn, _k_rows(t["kvp"], t["ckv"], t["cosf"], t["sins"]), [p["k_gn"], p["k_gr"]],
        [(HEADS * HEAD_PAD, BF16, HEAD_ALL), (HEADS * HEAD, BF16, HEAD_V)], [dk, dv],
        [(0, HEAD_NOPE), (0, HEAD_ROPE), (1, C128), None, None], [(HEADS * HEAD_PAD, F32), (128, F32)],
        "kv_k_norm_bwd", groups=HEADS)
    dw_ukv = _mm(t["lat"], dkvp, "tn", "kv_dwukv")
    dlat = _mm(dkvp, p["w_ukv"], "nt", "kv_dlat")
    dckv, dkv_g = _rowwise_bwd(_rms_fn, [(t["ckv"], (0, KV_LORA))], [p["kv_g"]], [(KV_LORA, BF16, (0, KV_LORA))],
                               [dlat], [(0, (0, KV_LORA))], [(KV_LORA, F32)], "kv_norm_bwd")
    dw_dkv = jnp.concatenate([_mm(t["h"], dckv, "tn", "kv_dwdkv"), _mm(t["h"], drope, "tn", "kv_dwdkv_rope")], axis=1)
    dh = _matmul([(dckv, p["w_dkv"]), (drope, p["w_dkv"])], "nt", "kv_dh", boffs=[0, KV_LORA])
    dx, dgain, dscale, dshift = _modulate_bwd(t["x"], p["gain"], p["scale"], p["shift"], dh, dx_in)
    return dx, dict(gain=dgain, scale=dscale, shift=dshift, w_dkv=dw_dkv, kv_g=dkv_g, w_ukv=dw_ukv, k_gn=dk_gn,
                    k_gr=dk_gr)


WEIGHTS = ["ada_w", "ada_b", "norm_g", "ffn_w_in", "ffn_w_out", "gdn_w_in", "gdn_conv_w", "gdn_a_log", "gdn_dt_bias",
           "gdn_norm_g", "gdn_w_out", "kv_ada_w", "kv_ada_b", "kv_norm_g", "mla_w_dkv", "mla_kv_norm_g", "mla_w_ukv",
           "mla_k_norm_g", "mla_w_dq", "mla_q_lora_norm_g", "mla_w_uq", "mla_q_norm_g", "mla_w_out"]
SMALL = [("ada_b", 4 * N_MOD * D), ("kv_ada_b", 2 * D), ("norm_g", DEPTH * 3 * D), ("gdn_conv_w", N_A * CONV_K * 3 * D),
         ("gdn_a_log", N_A * HEADS), ("gdn_dt_bias", N_A * HEADS), ("gdn_norm_g", N_A * HEAD), ("kv_norm_g", D),
         ("mla_kv_norm_g", KV_LORA), ("mla_k_norm_g", QK_HEAD), ("mla_q_lora_norm_g", 2 * Q_LORA),
         ("mla_q_norm_g", 2 * QK_HEAD)]
SMALL_REPLICATED = [n for n, _ in SMALL if n not in ("norm_g", "gdn_conv_w")]


def _silu_fn(g, t):
    return (_silu(t),)


def _dup_rope(t):
    return jnp.concatenate([t[..., :NOPE], t[..., NOPE:], t[..., NOPE:]], axis=-1)


def _fold_rope(t):
    return jnp.concatenate([t[..., :NOPE], t[..., NOPE:QK_HEAD] + t[..., QK_HEAD:]], axis=-1)


def _pack(pieces, rows):
    flat = jnp.concatenate([p.reshape(-1).astype(F32) for p in pieces])
    return jnp.pad(flat, (0, rows * 128 - flat.shape[0])).reshape(rows, 128)


def _step(a):
    me = 4 * lax.axis_index("x") + 2 * lax.axis_index("y") + lax.axis_index("c")
    x = a["x"][0]
    cosf, sins = _rope_tables(a["positions"][0])

    n_cw, n_ng = N_A * CONV_K * 3 * HEAD, DEPTH * 3 * HEAD
    small_all = _all_gather(_pack([a["gdn_conv_w"], a["norm_g"], a["c"]], 44), "gather_small").reshape(N_DEV, -1)
    conv_w = small_all[:, :n_cw].reshape(N_DEV, N_A, CONV_K, 3 * HEAD).transpose(1, 2, 0, 3).reshape(N_A, CONV_K, 3 * D)
    norm_g = small_all[:, n_cw:n_cw + n_ng].reshape(N_DEV, DEPTH, 3, HEAD).transpose(1, 2, 0, 3).reshape(DEPTH, 3, D)
    c_all = small_all[:, n_cw + n_ng:n_cw + n_ng + D]

    c_act, = _rowwise_fwd(_silu_fn, [(c_all, FULL)], [], [(D, F32, FULL)], "c_act")
    n_ada = N_MOD * D // N_DEV
    parts = [_mm(c_act, a["ada_w"][l], "nn", "mod_proj") for l in range(DEPTH)]
    parts.append(_mm(c_act, a["kv_ada_w"], "nn", "mod_proj_kv"))
    mod_recv = _exchange(jnp.concatenate(parts, axis=1)[:, None, :], "exchange_mod")[:, 0]
    mod = mod_recv[:, :DEPTH * n_ada].reshape(N_DEV, DEPTH, n_ada).transpose(1, 0, 2).reshape(DEPTH, N_MOD * D)
    mod = (mod + a["ada_b"]).reshape(DEPTH, N_MOD, D)
    kvmod = mod_recv[:, DEPTH * n_ada:].reshape(2 * D) + a["kv_ada_b"]

    n_in = 2 * D_FF // N_DEV
    n_gdn = (4 * D + 2 * HEADS) // N_DEV

    def layer_shards(l):
        sh = {"ffn_w_in": a["ffn_w_in"][l].reshape(2 * D, n_in), "ffn_w_out": a["ffn_w_out"][l].reshape(2 * D_FF // N_DEV, D)}
        if l < N_A:
            sh.update(gdn_w_in=a["gdn_w_in"][l], gdn_w_out=a["gdn_w_out"][l])
        else:
            j = l - N_A
            sh.update(mla_w_dq=a["mla_w_dq"][j], mla_w_uq=a["mla_w_uq"][j], mla_w_out=a["mla_w_out"][j])
        if l == N_A - 1:
            sh.update(mla_w_dkv=a["mla_w_dkv"], mla_w_ukv=a["mla_w_ukv"])
        return sh

    def zero_of(t):
        return jnp.minimum(jnp.abs(t.reshape(-1)[0].astype(F32)), 0.0)

    def start_layer(l, tie):
        pend = {}
        for name, w in layer_shards(l).items():
            src = (w + tie).astype(BF16)
            pend[name] = (_send_start(src, f"fetch_start_{name}_{l}", gather=True), src)
        return pend

    def finish_layer(l, pend, after):
        got = {}
        for name, (handle, src) in pend.items():
            land = _send_wait(handle, after, f"fetch_wait_{name}_{l}", gather=True)
            got[name] = lax.dynamic_update_slice(land, src[None], (me, 0, 0))
        tokens = sum(handle[4][0, 0] for handle, _ in pend.values())
        return got, tokens

    def row(v):
        return v[None]

    def ffn_params(l, i, w):
        w_in = w["ffn_w_in"].reshape(N_DEV, 2, D, n_in)[:, i]
        k = 0 if i == 0 else 6
        return dict(gain=row(norm_g[l, 0 if i == 0 else 2]), shift=row(mod[l, k]), scale=row(mod[l, k + 1]),
                    gm=0.5 * row(mod[l, k + 2]),
                    wg=w_in[:N_DEV // 2].transpose(1, 0, 2).reshape(D, D_FF),
                    wu=w_in[N_DEV // 2:].transpose(1, 0, 2).reshape(D, D_FF),
                    wo=w["ffn_w_out"].reshape(N_DEV, 2, D_FF // N_DEV, D)[:, i].reshape(D_FF, D))

    def gdn_params(l, w):
        w_in = w["gdn_w_in"].transpose(1, 0, 2).reshape(D, 4 * D + 2 * HEADS)
        pad = lambda t: jnp.pad(t, ((0, 0), (0, 128 - HEADS)))
        return dict(gain=row(norm_g[l, 1]), shift=row(mod[l, 3]), scale=row(mod[l, 4]), gm=row(mod[l, 5]),
                    w_main=w_in[:, :4 * D],
                    w_tail=jnp.concatenate([pad(w_in[:, 4 * D:4 * D + HEADS]), pad(w_in[:, 4 * D + HEADS:])], axis=1),
                    conv_w=conv_w[l], a_log=_pad128(row(a["gdn_a_log"][l])), dt_bias=_pad128(row(a["gdn_dt_bias"][l])),
                    norm_g=row(a["gdn_norm_g"][l]), w_out=w["gdn_w_out"].reshape(D, D))

    def mla_params(l, w):
        j = l - N_A
        uq = w["mla_w_uq"].transpose(1, 0, 2)
        qg = _dup_rope(a["mla_q_norm_g"][j])
        return dict(gain=row(norm_g[l, 1]), shift=row(mod[l, 3]), scale=row(mod[l, 4]), gm=row(mod[l, 5]),
                    w_dq=w["mla_w_dq"].reshape(D, Q_LORA), q_lora_g=row(a["mla_q_lora_norm_g"][j]),
                    w_uq=_dup_rope(uq).reshape(Q_LORA, HEADS * HEAD_PAD), q_gn=row(qg[:NOPE]), q_gr=row(qg[NOPE:]),
                    w_out=w["mla_w_out"].reshape(D, D))

    def kv_params(w):
        w_dkv = w["mla_w_dkv"].reshape(D, KV_LORA + ROPE)
        kg = _dup_rope(a["mla_k_norm_g"])
        return dict(gain=row(a["kv_norm_g"]), shift=row(kvmod[:D]), scale=row(kvmod[D:]),
                    w_dkv=jnp.concatenate([w_dkv, w_dkv[:, KV_LORA:]], axis=1), kv_g=row(a["mla_kv_norm_g"]),
                    w_ukv=w["mla_w_ukv"].transpose(1, 0, 2).reshape(KV_LORA, HEADS * 2 * HEAD), k_gn=row(kg[:NOPE]),
                    k_gr=row(kg[NOPE:]))

    tapes, kv, kv_p = [], None, None
    pend = start_layer(0, zero_of(mod))
    for l in range(DEPTH):
        w, tokens = finish_layer(l, pend, x)
        if l + 1 < DEPTH:
            pend = start_layer(l + 1, zero_of(w["ffn_w_in"]))
            tokens = tokens + sum(handle[4][0, 0] for handle, _ in pend.values())
        p1, pm_, p2 = ffn_params(l, 0, w), (gdn_params(l, w) if l < N_A else mla_params(l, w)), ffn_params(l, 1, w)
        p1["gain"] = p1["gain"] + tokens
        x, t1 = _ffn_fwd(x, p1)
        x, tm_ = _gdn_fwd(x, pm_) if l < N_A else _mla_fwd(x, pm_, kv)
        x, t2 = _ffn_fwd(x, p2)
        tapes.append((p1, t1, pm_, tm_, p2, t2))
        if l == N_A - 1:
            kv_p = kv_params(w)
            kv = _kv_fwd(x, kv_p, cosf, sins)
    dx, loss_blk = _loss_and_grad(x, a["loss_target"][0], "loss")
    loss = lax.psum(loss_blk[0, 0], ("x", "y", "c"))

    def by_cols(g, n):
        return g.reshape(g.shape[0], -1, n).transpose(1, 0, 2)

    def layer_grad_blocks(l, g1, gm_, g2):
        blocks = {
            "ffn_w_in": jnp.stack([jnp.concatenate([by_cols(g["wg"], n_in), by_cols(g["wu"], n_in)], axis=0)
                                   for g in (g1, g2)], axis=1).reshape(N_DEV, 2 * D, n_in),
            "ffn_w_out": jnp.stack([g["wo"].reshape(N_DEV, D_FF // N_DEV, D) for g in (g1, g2)],
                                   axis=1).reshape(N_DEV, 2 * D_FF // N_DEV, D)}
        if l < N_A:
            full = jnp.concatenate([gm_["w_main"], gm_["w_tail"][:, :HEADS], gm_["w_tail"][:, 128:128 + HEADS]], axis=1)
            blocks.update(gdn_w_in=by_cols(full, n_gdn), gdn_w_out=gm_["w_out"].reshape(N_DEV, D // N_DEV, D))
        else:
            blocks.update(mla_w_dq=gm_["w_dq"].reshape(N_DEV, D // N_DEV, Q_LORA),
                          mla_w_uq=_fold_rope(gm_["w_uq"].reshape(Q_LORA, HEADS, HEAD_PAD)).transpose(1, 0, 2),
                          mla_w_out=gm_["w_out"].reshape(N_DEV, D // N_DEV, D))
        return blocks

    grads = [None] * DEPTH
    sent = []
    dk_sum = dv_sum = kv_grads = None
    tokens = jnp.zeros((), F32)
    for l in reversed(range(DEPTH)):
        p1, t1, pm_, tm_, p2, t2 = tapes[l]
        if l == N_A - 1:
            dx, kv_grads = _kv_bwd(kv, kv_p, dk_sum, dv_sum, dx)
            d_dkv = kv_grads["w_dkv"]
            kv_blocks = {"mla_w_dkv": jnp.concatenate(
                [d_dkv[:, :KV_LORA], d_dkv[:, KV_LORA:KV_LORA + ROPE] + d_dkv[:, KV_LORA + ROPE:]],
                axis=1).reshape(N_DEV, D // N_DEV, KV_LORA + ROPE), "mla_w_ukv": by_cols(kv_grads["w_ukv"], 2 * HEAD)}
            for name, blocks in kv_blocks.items():
                src = blocks.astype(BF16)
                sent.append((name, None, _send_start(src, f"grad_start_{name}", gather=False), src))
                tokens = tokens + sent[-1][2][4][0, 0]
        p2 = dict(p2, gm=p2["gm"] + tokens)
        dx, g2 = _ffn_bwd(t2, p2, dx)
        if l < N_A:
            dx, gm_ = _gdn_bwd(tm_, pm_, dx)
        else:
            dx, gm_, dk, dv = _mla_bwd(tm_, pm_, kv, dx)
            dk_sum = dk if dk_sum is None else dk_sum + dk
            dv_sum = dv if dv_sum is None else dv_sum + dv
        dx, g1 = _ffn_bwd(t1, p1, dx)
        grads[l] = (g1, gm_, g2)
        tokens = jnp.zeros((), F32)
        for name, blocks in layer_grad_blocks(l, g1, gm_, g2).items():
            src = blocks.astype(BF16)
            sent.append((name, l, _send_start(src, f"grad_start_{name}_{l}", gather=False), src))
            tokens = tokens + sent[-1][2][4][0, 0]

    def layer_of(name, l, pre=""):
        if name in ("ffn_w_in", "ffn_w_out"):
            return a[pre + name][l].reshape(-1, a[name].shape[-1])
        if l is None:
            return a[pre + name]
        return a[pre + name][l if l < N_A else l - N_A]

    pieces = {}
    for name, l, handle, src in sent:
        recv = _send_wait(handle, dx, f"grad_wait_{name}" + ("" if l is None else f"_{l}"), gather=False)
        recv = lax.dynamic_update_slice(recv, lax.dynamic_slice_in_dim(src, me, 1, axis=0), (me, 0, 0))
        res = _adamw(recv, layer_of(name, l), layer_of(name, l, "m_"), layer_of(name, l, "v_"), "adamw")
        pieces.setdefault(name, {})[l] = res
    out = {}
    for name, by_layer in pieces.items():
        order = sorted(by_layer, key=lambda l: -1 if l is None else l)
        out[name] = [jnp.concatenate([by_layer[l][k] for l in order], axis=0) if len(order) > 1 else by_layer[order[0]][k]
                     for k in range(4)]

    def dmod(l):
        g1, gm_, g2 = grads[l]
        return jnp.concatenate([g1["shift"], g1["scale"], 0.5 * g1["gm"], gm_["shift"], gm_["scale"], gm_["gm"],
                                g2["shift"], g2["scale"], 0.5 * g2["gm"]], axis=1)

    gdn = [grads[l][1] for l in range(N_A)]
    mla = [grads[l][1] for l in range(N_A, DEPTH)]
    small = {
        "ada_b": jnp.concatenate([dmod(l) for l in range(DEPTH)], axis=0),
        "kv_ada_b": jnp.concatenate([kv_grads["shift"], kv_grads["scale"]], axis=1),
        "norm_g": jnp.stack([jnp.concatenate([grads[l][0]["gain"], grads[l][1]["gain"], grads[l][2]["gain"]], axis=0)
                             for l in range(DEPTH)]),
        "gdn_conv_w": jnp.stack([g["conv_w"] for g in gdn]),
        "gdn_a_log": jnp.stack([g["a_log"][0, :HEADS] for g in gdn]),
        "gdn_dt_bias": jnp.stack([g["dt_bias"][0, :HEADS] for g in gdn]),
        "gdn_norm_g": jnp.stack([g["norm_g"][0] for g in gdn]),
        "kv_norm_g": kv_grads["gain"],
        "mla_kv_norm_g": kv_grads["kv_g"],
        "mla_k_norm_g": _fold_rope(jnp.concatenate([kv_grads["k_gn"], kv_grads["k_gr"]], axis=1)),
        "mla_q_lora_norm_g": jnp.stack([g["q_lora_g"][0] for g in mla]),
        "mla_q_norm_g": jnp.stack([_fold_rope(jnp.concatenate([g["q_gn"], g["q_gr"]], axis=1))[0] for g in mla]),
    }
    rows = 616
    assert sum(n for _, n in SMALL) <= rows * 128 and all(small[n].size == k for n, k in SMALL)
    small_recv = _all_gather(_pack([small[n] for n, _ in SMALL], rows), "gather_small_grads")
    zero = lambda n, k: jnp.zeros((k,), F32)
    packed = {pre: _pack([a[pre + n] if n in SMALL_REPLICATED else zero(n, k) for n, k in SMALL], rows)
              for pre in ("", "m_", "v_")}
    res = _adamw(small_recv, packed[""], packed["m_"], packed["v_"], "adamw_small")
    offs = {}
    o = 0
    for n, k in SMALL:
        offs[n] = o
        o += k
    for n, k in SMALL:
        if n in SMALL_REPLICATED:
            out[n] = [r.reshape(-1)[offs[n]:offs[n] + k] for r in res]
    gsum = res[0].reshape(-1)
    g_norm = lax.dynamic_slice_in_dim(gsum[offs["norm_g"]:offs["norm_g"] + DEPTH * 3 * D].reshape(DEPTH * 3, D),
                                      me * HEAD, HEAD, axis=1)
    g_conv = lax.dynamic_slice_in_dim(
        gsum[offs["gdn_conv_w"]:offs["gdn_conv_w"] + N_A * CONV_K * 3 * D].reshape(N_A * CONV_K, 3 * D),
        me * 3 * HEAD, 3 * HEAD, axis=1)
    res2 = _adamw(_pack([g_norm, g_conv], 36)[None], *[_pack([a[pre + "norm_g"], a[pre + "gdn_conv_w"]], 36)
                                                      for pre in ("", "m_", "v_")], "adamw_small")
    out["norm_g"] = [r.reshape(-1)[:n_ng] for r in res2]
    out["gdn_conv_w"] = [r.reshape(-1)[n_ng:n_ng + n_cw] for r in res2]

    c_act_t = c_act.T
    all_small = small_recv.reshape(N_DEV, -1)
    dmod_all = all_small[:, :DEPTH * N_MOD * D].reshape(N_DEV, DEPTH, N_MOD * D)
    dmod_mine = lax.dynamic_slice_in_dim(dmod_all, me * n_ada, n_ada, axis=2)
    g_ada = jnp.concatenate([_outer8(c_act_t, dmod_mine[:, l], "ada_grad") for l in range(DEPTH)], axis=0)
    out["ada_w"] = _adamw(g_ada[None], *[a[pre + "ada_w"].reshape(DEPTH * D, n_ada) for pre in ("", "m_", "v_")], "adamw")
    dkv_all = all_small[:, offs["kv_ada_b"]:offs["kv_ada_b"] + 2 * D]
    g_kv = _outer8(c_act_t, lax.dynamic_slice_in_dim(dkv_all, me * (2 * D // N_DEV), 2 * D // N_DEV, axis=1), "ada_grad")
    out["kv_ada_w"] = _adamw(g_kv[None], *[a[pre + "kv_ada_w"] for pre in ("", "m_", "v_")], "adamw")

    result = [loss, dx[None]]
    for k in range(4):
        result += [out[n][k].reshape(a[n].shape) for n in WEIGHTS]
    return tuple(result)


def kernel(x, c, positions, ada_w, ada_b, norm_g, ffn_w_in, ffn_w_out, gdn_w_in, gdn_conv_w, gdn_a_log, gdn_dt_bias, gdn_norm_g, gdn_w_out, kv_ada_w, kv_ada_b, kv_norm_g, mla_w_dkv, mla_kv_norm_g, mla_w_ukv, mla_k_norm_g, mla_w_dq, mla_q_lora_norm_g, mla_w_uq, mla_q_norm_g, mla_w_out, loss_target, m_ada_w, m_ada_b, m_norm_g, m_ffn_w_in, m_ffn_w_out, m_gdn_w_in, m_gdn_conv_w, m_gdn_a_log, m_gdn_dt_bias, m_gdn_norm_g, m_gdn_w_out, m_kv_ada_w, m_kv_ada_b, m_kv_norm_g, m_mla_w_dkv, m_mla_kv_norm_g, m_mla_w_ukv, m_mla_k_norm_g, m_mla_w_dq, m_mla_q_lora_norm_g, m_mla_w_uq, m_mla_q_norm_g, m_mla_w_out, v_ada_w, v_ada_b, v_norm_g, v_ffn_w_in, v_ffn_w_out, v_gdn_w_in, v_gdn_conv_w, v_gdn_a_log, v_gdn_dt_bias, v_gdn_norm_g, v_gdn_w_out, v_kv_ada_w, v_kv_ada_b, v_kv_norm_g, v_mla_w_dkv, v_mla_kv_norm_g, v_mla_w_ukv, v_mla_k_norm_g, v_mla_w_dq, v_mla_q_lora_norm_g, v_mla_w_uq, v_mla_q_norm_g, v_mla_w_out):
    return _step(dict(locals()))
```

```python
import functools
import math

import jax
import jax.numpy as jnp
from jax import lax
from jax.experimental import pallas as pl
from jax.experimental.pallas import tpu as pltpu

F32 = jnp.float32
BF16 = jnp.bfloat16

N_DEV = 8
D = 1024
D_FF = 2816
DEPTH = 4
N_A = 2
N_MOD = 9
HEADS = 8
HEAD = 128
CHUNK = 64
CONV_K = 4
KV_LORA = 256
Q_LORA = 384
NOPE = 128
ROPE = 64
QK_HEAD = NOPE + ROPE
HEAD_PAD = 256
ROPE_BASE = 10000.0
EPS = 1e-6
LR, B1, B2, ADAM_EPS, WD, STEP = 0.001, 0.9, 0.999, 1e-08, 0.01, 10

VMEM_LIMIT = 48 * 1024 * 1024
ROW_TILE = 256
MESH = pl.DeviceIdType.MESH

_NN = (((1,), (0,)), ((), ()))
_NT = (((1,), (1,)), ((), ()))
_TN = (((0,), (0,)), ((), ()))
_DIMS = {"nn": _NN, "nt": _NT, "tn": _TN}


def _params(dims=None):
    return pltpu.CompilerParams(dimension_semantics=dims, vmem_limit_bytes=VMEM_LIMIT)


def _tile(n, target):
    for t in range(target - target % 128, 0, -128):
        if n % t == 0:
            return t
    return n


def _matmul(pairs, form, name, out_dtype=F32, tm=1408, tn=1408, tk=1408, boffs=None):
    a0, b0 = pairs[0]
    if form == "nn":
        m, n = a0.shape[0], b0.shape[1]
        ks = [a.shape[1] for a, _ in pairs]
    elif form == "nt":
        m, n = a0.shape[0], b0.shape[0]
        ks = [a.shape[1] for a, _ in pairs]
    else:
        m, n = a0.shape[1], b0.shape[1]
        ks = [a.shape[0] for a, _ in pairs]
    tm, tn = _tile(m, tm), _tile(n, tn)
    tks = [_tile(k, tk) for k in ks]
    boffs = boffs or [0] * len(pairs)
    assert m % tm == 0 and n % tn == 0 and all(o % t == 0 for o, t in zip(boffs, tks)), (name, m, n, ks)
    steps = [k // t for k, t in zip(ks, tks)]
    starts = [sum(steps[:p]) for p in range(len(pairs))]
    nk = sum(steps)

    def kidx(p, k):
        return jnp.clip(k - starts[p], 0, steps[p] - 1)

    in_specs, args = [], []
    for p, (a, b) in enumerate(pairs):
        t = tks[p]
        if form == "tn":
            in_specs.append(pl.BlockSpec((t, tm), lambda i, j, k, p=p: (kidx(p, k), i)))
            in_specs.append(pl.BlockSpec((t, tn), lambda i, j, k, p=p: (kidx(p, k), j)))
        elif form == "nn":
            in_specs.append(pl.BlockSpec((tm, t), lambda i, j, k, p=p: (i, kidx(p, k))))
            in_specs.append(pl.BlockSpec((t, tn), lambda i, j, k, p=p: (kidx(p, k), j)))
        else:
            in_specs.append(pl.BlockSpec((tm, t), lambda i, j, k, p=p: (i, kidx(p, k))))
            in_specs.append(pl.BlockSpec((tn, t), lambda i, j, k, p=p, o=boffs[p] // t: (j, kidx(p, k) + o)))
        args += [a, b]
    dims = _DIMS[form]
    npairs = len(pairs)

    def body(*refs):
        o_ref = refs[2 * npairs]
        k = pl.program_id(2)

        def prod(p):
            return lax.dot_general(refs[2 * p][...].astype(BF16), refs[2 * p + 1][...].astype(BF16), dims,
                                   preferred_element_type=F32)

        if nk == 1:
            o_ref[...] = prod(0).astype(o_ref.dtype)
            return
        acc = refs[2 * npairs + 1]

        @pl.when(k == 0)
        def _():
            acc[...] = jnp.zeros_like(acc)

        for p in range(npairs):
            @pl.when((k >= starts[p]) & (k < starts[p] + steps[p]))
            def _(p=p):
                acc[...] += prod(p)

        @pl.when(k == nk - 1)
        def _():
            o_ref[...] = acc[...].astype(o_ref.dtype)

    return pl.pallas_call(
        body, name=name, grid=(m // tm, n // tn, nk), in_specs=in_specs,
        out_specs=pl.BlockSpec((tm, tn), lambda i, j, k: (i, j)),
        out_shape=jax.ShapeDtypeStruct((m, n), out_dtype),
        scratch_shapes=[] if nk == 1 else [pltpu.VMEM((tm, tn), F32)],
        compiler_params=_params(("parallel", "parallel", "arbitrary")),
    )(*args)


def _mm(a, b, form, name, **kw):
    return _matmul([(a, b)], form, name, **kw)


def _cols(spec, g):
    return spec[g] if isinstance(spec, list) else spec


def _rowwise_fwd(fn, rows, pars, outs, name, groups=1, ts=ROW_TILE):
    s = rows[0][0].shape[0]
    ts = min(ts, s)
    assert s % ts == 0
    nr, npar = len(rows), len(pars)

    def body(*refs):
        par_t = [r[...] for r in refs[nr:nr + npar]]
        out_refs = refs[nr + npar:]
        for g in range(groups):
            row_t = []
            for r, (_, spec) in zip(refs[:nr], rows):
                c0, w = _cols(spec, g)
                row_t.append(r[:, c0:c0 + w].astype(F32))
            res = fn(g, *row_t, *par_t)
            for o_ref, val, (_, _, spec) in zip(out_refs, res, outs):
                c0, w = _cols(spec, g)
                o_ref[:, c0:c0 + w] = val.astype(o_ref.dtype)

    return pl.pallas_call(
        body, name=name, grid=(s // ts,),
        in_specs=[pl.BlockSpec((ts, a.shape[1]), lambda i: (i, 0)) for a, _ in rows]
        + [pl.BlockSpec(p.shape, lambda i: (0, 0)) for p in pars],
        out_specs=[pl.BlockSpec((ts, w), lambda i: (i, 0)) for w, _, _ in outs],
        out_shape=[jax.ShapeDtypeStruct((s, w), dt) for w, dt, _ in outs],
        compiler_params=_params(("parallel",)),
    )(*[a for a, _ in rows], *pars)


def _rowwise_bwd(fn, rows, pars, outs, douts, gmap, gshapes, name, groups=1, add=None, par_grads=True,
                 ts=ROW_TILE):
    s = rows[0][0].shape[0]
    ts = min(ts, s)
    assert s % ts == 0
    nr, npar, nout, ng = len(rows), len(pars), len(outs), len(gshapes)
    add = add or {}
    add_keys = sorted(add)

    def body(*refs):
        row_refs = refs[:nr]
        par_refs = refs[nr:nr + npar]
        dout_refs = refs[nr + npar:nr + npar + nout]
        add_refs = refs[nr + npar + nout:nr + npar + nout + len(add_keys)]
        g_refs = refs[nr + npar + nout + len(add_keys):][:ng]
        pg_refs = refs[nr + npar + nout + len(add_keys) + ng:]
        par_t = [r[...] for r in par_refs]
        par_acc = [None] * npar
        shared_acc = {}
        for g in range(groups):
            row_t = []
            for r, (_, spec) in zip(row_refs, rows):
                c0, w = _cols(spec, g)
                row_t.append(r[:, c0:c0 + w].astype(F32))
            cts = []
            for r, (_, _, spec) in zip(dout_refs, outs):
                c0, w = _cols(spec, g)
                cts.append(r[:, c0:c0 + w].astype(F32))
            _, vjp = jax.vjp(lambda *t, g=g: tuple(fn(g, *t)), *row_t, *par_t)
            grads = vjp(tuple(cts))
            for k in range(nr):
                if gmap[k] is None:
                    continue
                gi, spec = gmap[k]
                if isinstance(spec, list) or groups == 1:
                    c0, w = _cols(spec, g)
                    val = grads[k]
                    if gi in add:
                        val = val + add_refs[add_keys.index(gi)][:, c0:c0 + w].astype(F32)
                    g_refs[gi][:, c0:c0 + w] = val.astype(g_refs[gi].dtype)
                else:
                    shared_acc[k] = grads[k] if k not in shared_acc else shared_acc[k] + grads[k]
            if par_grads:
                for k in range(npar):
                    pg = grads[nr + k]
                    par_acc[k] = pg if par_acc[k] is None else par_acc[k] + pg
        for k, val in shared_acc.items():
            gi, (c0, w) = gmap[k]
            assert gi not in add
            g_refs[gi][:, c0:c0 + w] = val.astype(g_refs[gi].dtype)
        if par_grads:
            first = pl.program_id(0) == 0
            for k in range(npar):
                @pl.when(first)
                def _(k=k):
                    pg_refs[k][...] = par_acc[k]

                @pl.when(jnp.logical_not(first))
                def _(k=k):
                    pg_refs[k][...] += par_acc[k]

    out_specs = [pl.BlockSpec((ts, w), lambda i: (i, 0)) for w, _ in gshapes]
    out_shape = [jax.ShapeDtypeStruct((s, w), dt) for w, dt in gshapes]
    if par_grads:
        out_specs += [pl.BlockSpec(p.shape, lambda i: (0, 0)) for p in pars]
        out_shape += [jax.ShapeDtypeStruct(p.shape, F32) for p in pars]
    return pl.pallas_call(
        body, name=name, grid=(s // ts,),
        in_specs=[pl.BlockSpec((ts, a.shape[1]), lambda i: (i, 0)) for a, _ in rows]
        + [pl.BlockSpec(p.shape, lambda i: (0, 0)) for p in pars]
        + [pl.BlockSpec((ts, a.shape[1]), lambda i: (i, 0)) for a in douts]
        + [pl.BlockSpec((ts, add[k].shape[1]), lambda i: (i, 0)) for k in add_keys],
        out_specs=out_specs, out_shape=out_shape,
        compiler_params=_params(("arbitrary",)),
    )(*[a for a, _ in rows], *pars, *douts, *[add[k] for k in add_keys])


def _sigmoid(x):
    return 1.0 / (1.0 + jnp.exp(-x))


def _silu(x):
    return x * _sigmoid(x)


def _softplus(x):
    return jnp.maximum(x, 0.0) + jnp.log(1.0 + jnp.exp(-jnp.abs(x)))


def _rms(t, g, n=None):
    n = n or t.shape[-1]
    return t * lax.rsqrt(jnp.sum(t * t, axis=-1, keepdims=True) / n + EPS) * g


def _modulate_fn(g, x, gain, scale, shift):
    return (_rms(x, gain) * (1.0 + scale) + shift,)


def _resgate_fn(g, x, y, gm):
    return (x + gm * y,)


def _gate_only_fn(g, y, gm):
    return (gm * y,)


def _gdn_gates_fn(g, b_logit, a_logit, a_log, dt_bias):
    gate = -jnp.exp(a_log) * _softplus(a_logit + dt_bias)
    n = gate.shape[0]
    i = lax.broadcasted_iota(jnp.int32, (n, n), 0)
    j = lax.broadcasted_iota(jnp.int32, (n, n), 1)
    tri = (((i // CHUNK) == (j // CHUNK)) & (i >= j)).astype(F32)
    gcum = lax.dot_general(tri, gate, _NN, preferred_element_type=F32, precision=lax.Precision.HIGHEST)
    return _sigmoid(b_logit), gcum


def _gdn_outnorm_fn(g, o, z, gain):
    return (_rms(o, gain) * _silu(z),)


def _rms_fn(g, t, gain):
    return (_rms(t, gain),)


@jax.custom_vjp
def _swap_halves(t):
    return pltpu.roll(t, 32, 1)


_swap_halves.defvjp(lambda t: (pltpu.roll(t, 32, 1), None), lambda _, ct: (pltpu.roll(ct, 96, 1),))


def _head_norm_rope_fn(g, nope, rope, cosf, sins, gain_n, gain_r):
    first = lax.broadcasted_iota(jnp.int32, rope.shape, 1) < ROPE
    ss = jnp.sum(nope * nope, axis=-1, keepdims=True) + jnp.sum(jnp.where(first, rope * rope, 0.0), axis=-1,
                                                                 keepdims=True)
    r = lax.rsqrt(ss / QK_HEAD + EPS)
    tn = nope * r * gain_n
    tr = rope * r * gain_r
    rot = jnp.where(first, tr * cosf + _swap_halves(tr) * sins, 0.0)
    return tn, rot


def _q_norm_rope_fn(g, nope, rope, cosf, sins, gain_n, gain_r):
    tn, rot = _head_norm_rope_fn(g, nope, rope, cosf, sins, gain_n, gain_r)
    return (jnp.concatenate([tn, rot], axis=1),)


def _k_norm_rope_fn(g, nope, val, rope, cosf, sins, gain_n, gain_r):
    tn, rot = _head_norm_rope_fn(g, nope, rope, cosf, sins, gain_n, gain_r)
    return jnp.concatenate([tn, rot], axis=1), val


def _loss_fn(g, y, target):
    e = y - target
    return (jnp.sum(e * e, axis=-1, keepdims=True) * (0.5 / D) * jnp.ones((1, 128), F32),)


def _ffn_in(h, wg, wu, name, tm=1024, tn=256):
    s = h.shape[0]
    tm = min(tm, s)

    def body(h_ref, wg_ref, wu_ref, g_ref, u_ref, a_ref):
        hb = h_ref[...]
        gate = jnp.dot(hb, wg_ref[...], preferred_element_type=F32)
        up = jnp.dot(hb, wu_ref[...], preferred_element_type=F32)
        g_ref[...] = gate.astype(BF16)
        u_ref[...] = up.astype(BF16)
        a_ref[...] = (_silu(gate) * up).astype(BF16)

    spec = pl.BlockSpec((tm, tn), lambda i, j: (i, j))
    return pl.pallas_call(
        body, name=name, grid=(s // tm, D_FF // tn),
        in_specs=[pl.BlockSpec((tm, D), lambda i, j: (i, 0)), pl.BlockSpec((D, tn), lambda i, j: (0, j)),
                  pl.BlockSpec((D, tn), lambda i, j: (0, j))],
        out_specs=[spec, spec, spec], out_shape=[jax.ShapeDtypeStruct((s, D_FF), BF16)] * 3,
        compiler_params=_params(("parallel", "parallel")),
    )(h, wg, wu)


def _ffn_bwd_act(dy, wo, gate, up, name, tm=1024, tn=256):
    s = dy.shape[0]
    tm = min(tm, s)

    def body(dy_ref, wo_ref, g_ref, u_ref, dg_ref, du_ref):
        dact = lax.dot_general(dy_ref[...], wo_ref[...], _NT, preferred_element_type=F32)
        gate = g_ref[...].astype(F32)
        up = u_ref[...].astype(F32)
        sg = _sigmoid(gate)
        dg_ref[...] = (dact * up * (sg * (1.0 + gate * (1.0 - sg)))).astype(BF16)
        du_ref[...] = (dact * (gate * sg)).astype(BF16)

    spec = pl.BlockSpec((tm, tn), lambda i, j: (i, j))
    return pl.pallas_call(
        body, name=name, grid=(s // tm, D_FF // tn),
        in_specs=[pl.BlockSpec((tm, D), lambda i, j: (i, 0)), pl.BlockSpec((tn, D), lambda i, j: (j, 0)), spec, spec],
        out_specs=[spec, spec], out_shape=[jax.ShapeDtypeStruct((s, D_FF), BF16)] * 2,
        compiler_params=_params(("parallel", "parallel")),
    )(dy, wo, gate, up)


def _shift_down(x, d):
    rows = lax.broadcasted_iota(jnp.int32, x.shape, 0)
    return jnp.where(rows >= d, pltpu.roll(x, d, 0), 0.0)


def _shift_up(x, d):
    n = x.shape[0]
    rows = lax.broadcasted_iota(jnp.int32, x.shape, 0)
    return jnp.where(rows < n - d, pltpu.roll(x, n - d, 0), 0.0)


def _conv_post(pre, is_qk):
    a = _silu(pre)
    l2 = a * lax.rsqrt(jnp.sum(a * a, axis=-1, keepdims=True) + EPS)
    return jnp.where(is_qk, l2, a)


def _conv_pre(x, w):
    pre = x * w[CONV_K - 1:CONV_K, :]
    for j in range(CONV_K - 1):
        pre = pre + _shift_down(x, CONV_K - 1 - j) * w[j:j + 1, :]
    return pre


def _gdn_conv_fwd(pm, conv_w, name):
    s = pm.shape[0]
    nblk = 3 * D // HEAD

    def body(x_ref, w_ref, o_ref):
        is_qk = pl.program_id(0) < 2 * HEADS
        o_ref[...] = _conv_post(_conv_pre(x_ref[...], w_ref[...]), is_qk)

    return pl.pallas_call(
        body, name=name, grid=(nblk,),
        in_specs=[pl.BlockSpec((s, HEAD), lambda c: (0, c)), pl.BlockSpec((CONV_K, HEAD), lambda c: (0, c))],
        out_specs=pl.BlockSpec((s, HEAD), lambda c: (0, c)),
        out_shape=jax.ShapeDtypeStruct((s, 3 * D), F32), compiler_params=_params(("parallel",)),
    )(pm, conv_w)


def _gdn_conv_bwd(pm, conv_w, dout, part, name):
    s = pm.shape[0]
    off = part * HEADS

    def body(x_ref, w_ref, d_ref, dx_ref, dw_ref):
        x, w = x_ref[...], w_ref[...]
        _, vjp = jax.vjp(lambda p: _conv_post(p, part < 2), _conv_pre(x, w))
        dpre, = vjp(d_ref[...])
        dx = dpre * w[CONV_K - 1:CONV_K, :]
        rows = [None] * CONV_K
        rows[CONV_K - 1] = jnp.sum(dpre * x, axis=0, keepdims=True)
        for j in range(CONV_K - 1):
            dx = dx + _shift_up(dpre, CONV_K - 1 - j) * w[j:j + 1, :]
            rows[j] = jnp.sum(dpre * _shift_down(x, CONV_K - 1 - j), axis=0, keepdims=True)
        dx_ref[...] = dx
        dw_ref[...] = jnp.concatenate(rows, axis=0)

    return pl.pallas_call(
        body, name=name, grid=(HEADS,),
        in_specs=[pl.BlockSpec((s, HEAD), lambda c: (0, c + off)), pl.BlockSpec((CONV_K, HEAD), lambda c: (0, c + off)),
                  pl.BlockSpec((s, HEAD), lambda c: (0, c))],
        out_specs=[pl.BlockSpec((s, HEAD), lambda c: (0, c)), pl.BlockSpec((CONV_K, HEAD), lambda c: (0, c))],
        out_shape=[jax.ShapeDtypeStruct((s, D), F32), jax.ShapeDtypeStruct((CONV_K, D), F32)],
        compiler_params=_params(("parallel",)),
    )(pm, conv_w, dout)


def _dot3(a, b, dims=_NN):
    ah, bh = a.astype(BF16), b.astype(BF16)
    al, bl = (a - ah.astype(F32)).astype(BF16), (b - bh.astype(F32)).astype(BF16)
    d = lambda u, v: lax.dot_general(u, v, dims, preferred_element_type=F32)
    return d(ah, bh) + (d(ah, bl) + d(al, bh))


def _make_dot(hi):
    def raw(a, b, dims):
        if hi:
            return _dot3(a, b, dims)
        return lax.dot_general(a.astype(BF16), b.astype(BF16), dims, preferred_element_type=F32)

    @functools.partial(jax.custom_vjp, nondiff_argnums=(2,))
    def dot(a, b, form):
        return raw(a, b, _DIMS[form])

    def fwd(a, b, form):
        return raw(a, b, _DIMS[form]), (a, b)

    def bwd(form, res, ct):
        a, b = res
        if form == "nn":
            return raw(ct, b, _NT), raw(a, ct, _TN)
        if form == "nt":
            return raw(ct, b, _NN), raw(ct, a, _TN)
        return raw(b, ct, _NT), raw(a, ct, _NN)

    dot.defvjp(fwd, bwd)
    return dot


_dot = _make_dot(False)
_dot_hi = _make_dot(True)


def _tri_inv_raw(low):
    n = low.shape[0]
    i = lax.broadcasted_iota(jnp.int32, (n, n), 0)
    j = lax.broadcasted_iota(jnp.int32, (n, n), 1)
    eye = (i == j).astype(F32)
    hdot = _dot3
    same16 = (i // 16) == (j // 16)
    neg = jnp.where(same16, -low, 0.0)
    inv = eye + neg
    power = neg
    for _ in range(3):
        power = hdot(power, power)
        inv = hdot(inv, eye + power)
    for blk in (32, 64):
        off = jnp.where(((i // blk) == (j // blk)) & ((i // (blk // 2)) != (j // (blk // 2))), low, 0.0)
        inv = inv - hdot(inv, hdot(off, inv))
    return inv


@jax.custom_vjp
def _tri_inv(low):
    return _tri_inv_raw(low)


def _tri_inv_fwd(low):
    inv = _tri_inv_raw(low)
    return inv, inv


def _tri_inv_bwd(inv, ct):
    return (-_dot3(_dot3(inv, ct, _TN), inv, _NT),)


_tri_inv.defvjp(_tri_inv_fwd, _tri_inv_bwd)


@jax.custom_vjp
def _tri_inv_given(low, inv):
    return inv


_tri_inv_given.defvjp(lambda low, inv: (inv, inv),
                      lambda inv, ct: (_tri_inv_bwd(inv, ct)[0], jnp.zeros_like(inv)))

GROUP = 4
N_GROUPS = HEADS // GROUP
GROWS = GROUP * CHUNK


def _gdn_group(q, k, v, beta, gc, gr, states, inv=None):
    n = q.shape[0]
    i = lax.broadcasted_iota(jnp.int32, (n, n), 0)
    j = lax.broadcasted_iota(jnp.int32, (n, n), 1)
    same = (i // CHUNK) == (j // CHUNK)
    incl, strict = same & (i >= j), same & (i > j)
    qs = q * (HEAD ** -0.5)
    decay = jnp.where(incl, jnp.exp(jnp.where(incl, gc - gr, 0.0)), 0.0)
    kb = k * beta
    eg = jnp.exp(gc)
    prod = _dot(jnp.concatenate([kb, qs], axis=0), k, "nt")
    low = jnp.where(strict, prod[:n] * decay, 0.0)
    attn = jnp.where(incl, prod[n:] * decay, 0.0)
    inv = _tri_inv(low) if inv is None else _tri_inv_given(low, inv)
    sol = _dot_hi(inv, jnp.concatenate([v * beta, kb * eg], axis=1), "nn")
    u, w, qg = sol[:, :HEAD], sol[:, HEAD:], qs * eg
    last = lax.broadcasted_iota(jnp.int32, (CHUNK, 1), 0) == CHUNK - 1
    v_new, o_state, carry = [], [], []
    for h, state in enumerate(states):
        rows = slice(h * CHUNK, (h + 1) * CHUNK)
        ws = _dot(jnp.concatenate([w[rows], qg[rows]], axis=0), state, "nn")
        v_new.append(u[rows] - ws[:CHUNK])
        o_state.append(ws[CHUNK:])
        g_last = jnp.sum(jnp.where(last, gc[rows], 0.0), axis=0, keepdims=True)
        carry.append((g_last, k[rows] * jnp.exp(g_last - gc[rows])))
    o = jnp.concatenate(o_state, axis=0) + _dot(attn, jnp.concatenate(v_new, axis=0), "nn")
    new = tuple(state * jnp.exp(g_last) + _dot(k_dec, vn, "tn")
                for state, (g_last, k_dec), vn in zip(states, carry, v_new))
    return o, new, inv


def _gdn_specs(s, rev):
    nc = s // CHUNK
    at = (lambda n: nc - 1 - n) if rev else (lambda n: n)
    return nc, at, [
        pl.BlockSpec((CHUNK, D), lambda n: (at(n), 0)), pl.BlockSpec((CHUNK, D), lambda n: (at(n), 1)),
        pl.BlockSpec((CHUNK, D), lambda n: (at(n), 2)), pl.BlockSpec((CHUNK, HEAD), lambda n: (at(n), 0)),
        pl.BlockSpec((CHUNK, HEAD), lambda n: (at(n), 0)),
        pl.BlockSpec((None, N_GROUPS, 1, GROWS), lambda n: (at(n), 0, 0, 0))]


def _group_operands(grp, q_ref, k_ref, v_ref, b_blk, gc_blk, gr_blk):
    heads = range(grp * GROUP, (grp + 1) * GROUP)
    stack = lambda ref: jnp.concatenate([ref[:, h * HEAD:(h + 1) * HEAD] for h in heads], axis=0)
    col = lambda blk: jnp.concatenate([blk[:, h:h + 1] for h in heads], axis=0)
    return stack(q_ref), stack(k_ref), stack(v_ref), col(b_blk), col(gc_blk), gr_blk[grp]


def _gdn_scan_fwd(qkv, beta, gcum, grow, name):
    s = qkv.shape[0]
    nc, _, in_specs = _gdn_specs(s, rev=False)

    def body(q_ref, k_ref, v_ref, b_ref, gc_ref, gr_ref, o_ref, st_ref, inv_ref, state):
        @pl.when(pl.program_id(0) == 0)
        def _():
            state[...] = jnp.zeros_like(state)

        b_blk, gc_blk, gr_blk = b_ref[...], gc_ref[...], gr_ref[...]
        old = [state[h] for h in range(HEADS)]
        res = [_gdn_group(*_group_operands(grp, q_ref, k_ref, v_ref, b_blk, gc_blk, gr_blk),
                          old[grp * GROUP:(grp + 1) * GROUP]) for grp in range(N_GROUPS)]
        for grp, (o, new, inv) in enumerate(res):
            inv_ref[grp] = inv
            for hh in range(GROUP):
                h = grp * GROUP + hh
                st_ref[h] = old[h]
                o_ref[:, h * HEAD:(h + 1) * HEAD] = o[hh * CHUNK:(hh + 1) * CHUNK]
                state[h] = new[hh]

    return pl.pallas_call(
        body, name=name, grid=(nc,), in_specs=in_specs,
        out_specs=[pl.BlockSpec((CHUNK, D), lambda n: (n, 0)),
                   pl.BlockSpec((None, HEADS, HEAD, HEAD), lambda n: (n, 0, 0, 0)),
                   pl.BlockSpec((None, N_GROUPS, GROWS, GROWS), lambda n: (n, 0, 0, 0))],
        out_shape=[jax.ShapeDtypeStruct((s, D), F32), jax.ShapeDtypeStruct((nc, HEADS, HEAD, HEAD), F32),
                   jax.ShapeDtypeStruct((nc, N_GROUPS, GROWS, GROWS), F32)],
        scratch_shapes=[pltpu.VMEM((HEADS, HEAD, HEAD), F32)],
        compiler_params=_params(("arbitrary",)),
    )(qkv, qkv, qkv, beta, gcum, grow)


def _gdn_scan_bwd(qkv, beta, gcum, grow, states, invs, do, name):
    s = qkv.shape[0]
    nc, at, in_specs = _gdn_specs(s, rev=True)
    in_specs += [pl.BlockSpec((None, HEADS, HEAD, HEAD), lambda n: (at(n), 0, 0, 0)),
                 pl.BlockSpec((None, N_GROUPS, GROWS, GROWS), lambda n: (at(n), 0, 0, 0)),
                 pl.BlockSpec((CHUNK, D), lambda n: (at(n), 0))]

    def body(q_ref, k_ref, v_ref, b_ref, gc_ref, gr_ref, st_ref, inv_ref, do_ref, dq_ref, dk_ref, dv_ref, db_ref,
             dgc_ref, dgr_ref, dstate):
        @pl.when(pl.program_id(0) == 0)
        def _():
            dstate[...] = jnp.zeros_like(dstate)

        b_blk, gc_blk, gr_blk = b_ref[...], gc_ref[...], gr_ref[...]
        dold = [dstate[h] for h in range(HEADS)]
        res = []
        for grp in range(N_GROUPS):
            heads = range(grp * GROUP, (grp + 1) * GROUP)
            inv = inv_ref[grp]
            _, vjp = jax.vjp(lambda q, k, v, b, gc, gr, *st, inv=inv: _gdn_group(q, k, v, b, gc, gr, st, inv)[:2],
                             *_group_operands(grp, q_ref, k_ref, v_ref, b_blk, gc_blk, gr_blk),
                             *[st_ref[h] for h in heads])
            d_out = jnp.concatenate([do_ref[:, h * HEAD:(h + 1) * HEAD] for h in heads], axis=0)
            res.append(vjp((d_out, tuple(dold[h] for h in heads))))
        lane = lax.broadcasted_iota(jnp.int32, (CHUNK, HEAD), 1)
        db_all = jnp.zeros((CHUNK, HEAD), F32)
        dgc_all = jnp.zeros((CHUNK, HEAD), F32)
        for grp, (dq, dk, dv, db, dgc, dgr, *dst) in enumerate(res):
            dgr_ref[grp] = dgr
            for hh in range(GROUP):
                h = grp * GROUP + hh
                cs, rows = slice(h * HEAD, (h + 1) * HEAD), slice(hh * CHUNK, (hh + 1) * CHUNK)
                dq_ref[:, cs] = dq[rows]
                dk_ref[:, cs] = dk[rows]
                dv_ref[:, cs] = dv[rows]
                dstate[h] = dst[hh]
                db_all = jnp.where(lane == h, db[rows], db_all)
                dgc_all = jnp.where(lane == h, dgc[rows], dgc_all)
        db_ref[...] = db_all
        dgc_ref[...] = dgc_all

    blk = pl.BlockSpec((CHUNK, D), lambda n: (at(n), 0))
    gblk = pl.BlockSpec((CHUNK, HEAD), lambda n: (at(n), 0))
    return pl.pallas_call(
        body, name=name, grid=(nc,), in_specs=in_specs,
        out_specs=[blk, blk, blk, gblk, gblk, pl.BlockSpec((None, N_GROUPS, 1, GROWS), lambda n: (at(n), 0, 0, 0))],
        out_shape=[jax.ShapeDtypeStruct((s, D), F32)] * 3 + [jax.ShapeDtypeStruct((s, HEAD), F32)] * 2
        + [jax.ShapeDtypeStruct((nc, N_GROUPS, 1, GROWS), F32)],
        scratch_shapes=[pltpu.VMEM((HEADS, HEAD, HEAD), F32)],
        compiler_params=_params(("arbitrary",)),
    )(qkv, qkv, qkv, beta, gcum, grow, states, invs, do)


ATT_TILE = 512
ATT_SCALE = QK_HEAD ** -0.5


def _att_mask(t):
    qpos = lax.broadcasted_iota(jnp.int32, (t, t), 0)
    kpos = lax.broadcasted_iota(jnp.int32, (t, t), 1)
    return (kpos // CHUNK) <= (qpos // CHUNK)


def _att_pairs(nb, by_query):
    if by_query:
        pairs = [(i, j) for i in range(nb) for j in range(i + 1)]
    else:
        pairs = [(j, i) for j in range(nb) for i in range(j, nb)]
    return jnp.array([a for a, _ in pairs], jnp.int32), jnp.array([b for _, b in pairs], jnp.int32)


def _attn_fwd(q, k, v, name):
    s = q.shape[0]
    t = min(ATT_TILE, s)
    nb = s // t
    ii, jj = _att_pairs(nb, by_query=True)

    def body(ii_ref, jj_ref, q_ref, k_ref, v_ref, o_ref, lse_ref, m_s, l_s, acc):
        step = pl.program_id(1)
        i, j = ii_ref[step], jj_ref[step]

        @pl.when(j == 0)
        def _():
            m_s[...] = jnp.full_like(m_s, -jnp.inf)
            l_s[...] = jnp.zeros_like(l_s)
            acc[...] = jnp.zeros_like(acc)

        sc = lax.dot_general(q_ref[...], k_ref[...], _NT, preferred_element_type=F32) * ATT_SCALE
        sc = lax.cond(i == j, lambda u: jnp.where(_att_mask(t), u, -jnp.inf), lambda u: u, sc)
        m_new = jnp.maximum(m_s[...], jnp.max(sc, axis=-1, keepdims=True))
        alpha = jnp.exp(m_s[...] - m_new)
        p = jnp.exp(sc - m_new)
        l_s[...] = alpha * l_s[...] + jnp.sum(p, axis=-1, keepdims=True)
        acc[...] = alpha * acc[...] + jnp.dot(p.astype(BF16), v_ref[...], preferred_element_type=F32)
        m_s[...] = m_new

        @pl.when(j == i)
        def _():
            o_ref[...] = acc[...] / l_s[...]
            lse_ref[...] = m_s[...] + jnp.log(l_s[...])

    grid_spec = pltpu.PrefetchScalarGridSpec(
        num_scalar_prefetch=2, grid=(HEADS, len(ii)),
        in_specs=[pl.BlockSpec((t, HEAD_PAD), lambda h, n, ir, jr: (ir[n], h)),
                  pl.BlockSpec((t, HEAD_PAD), lambda h, n, ir, jr: (jr[n], h)),
                  pl.BlockSpec((t, HEAD), lambda h, n, ir, jr: (jr[n], h))],
        out_specs=[pl.BlockSpec((t, HEAD), lambda h, n, ir, jr: (ir[n], h)),
                   pl.BlockSpec((None, t, 1), lambda h, n, ir, jr: (h, ir[n], 0))],
        scratch_shapes=[pltpu.VMEM((t, 1), F32), pltpu.VMEM((t, 1), F32), pltpu.VMEM((t, HEAD), F32)])
    return pl.pallas_call(
        body, name=name, grid_spec=grid_spec,
        out_shape=[jax.ShapeDtypeStruct((s, HEADS * HEAD), F32), jax.ShapeDtypeStruct((HEADS, s, 1), F32)],
        compiler_params=_params(("parallel", "arbitrary")),
    )(ii, jj, q, k, v)


def _attn_bwd(q, k, v, do, o, lse, name):
    s = q.shape[0]
    t = min(ATT_TILE, s)
    nb = s // t
    jj, ii = _att_pairs(nb, by_query=False)

    def body(jj_ref, ii_ref, q_ref, k_ref, v_ref, do_ref, o_ref, lse_ref, dq_ref, dk_ref, dv_ref, dk_acc, dv_acc):
        step = pl.program_id(1)
        i, j = ii_ref[step], jj_ref[step]

        @pl.when(step == 0)
        def _():
            dq_ref[...] = jnp.zeros_like(dq_ref)

        @pl.when(i == j)
        def _():
            dk_acc[...] = jnp.zeros_like(dk_acc)
            dv_acc[...] = jnp.zeros_like(dv_acc)

        sc = lax.dot_general(q_ref[...], k_ref[...], _NT, preferred_element_type=F32) * ATT_SCALE
        p = jnp.exp(sc - lse_ref[...])
        p = lax.cond(i == j, lambda u: jnp.where(_att_mask(t), u, 0.0), lambda u: u, p)
        do_f = do_ref[...]
        dob = do_f.astype(BF16)
        delta = jnp.sum(do_f * o_ref[...], axis=-1, keepdims=True)
        dv_acc[...] += lax.dot_general(p.astype(BF16), dob, _TN, preferred_element_type=F32)
        dp = lax.dot_general(dob, v_ref[...], _NT, preferred_element_type=F32)
        ds = (p * (dp - delta) * ATT_SCALE).astype(BF16)
        dk_acc[...] += lax.dot_general(ds, q_ref[...], _TN, preferred_element_type=F32)
        rows = pl.ds(pl.multiple_of(i * t, t), t)
        dq_ref[rows, :] += jnp.dot(ds, k_ref[...], preferred_element_type=F32)

        @pl.when(i == nb - 1)
        def _():
            dk_ref[...] = dk_acc[...]
            dv_ref[...] = dv_acc[...]

    grid_spec = pltpu.PrefetchScalarGridSpec(
        num_scalar_prefetch=2, grid=(HEADS, len(jj)),
        in_specs=[pl.BlockSpec((t, HEAD_PAD), lambda h, n, jr, ir: (ir[n], h)),
                  pl.BlockSpec((t, HEAD_PAD), lambda h, n, jr, ir: (jr[n], h)),
                  pl.BlockSpec((t, HEAD), lambda h, n, jr, ir: (jr[n], h)),
                  pl.BlockSpec((t, HEAD), lambda h, n, jr, ir: (ir[n], h)),
                  pl.BlockSpec((t, HEAD), lambda h, n, jr, ir: (ir[n], h)),
                  pl.BlockSpec((None, t, 1), lambda h, n, jr, ir: (h, ir[n], 0))],
        out_specs=[pl.BlockSpec((s, HEAD_PAD), lambda h, n, jr, ir: (0, h)),
                   pl.BlockSpec((t, HEAD_PAD), lambda h, n, jr, ir: (jr[n], h)),
                   pl.BlockSpec((t, HEAD), lambda h, n, jr, ir: (jr[n], h))],
        scratch_shapes=[pltpu.VMEM((t, HEAD_PAD), F32), pltpu.VMEM((t, HEAD), F32)])
    return pl.pallas_call(
        body, name=name, grid_spec=grid_spec,
        out_shape=[jax.ShapeDtypeStruct((s, HEADS * HEAD_PAD), F32)] * 2 + [jax.ShapeDtypeStruct((s, HEADS * HEAD), F32)],
        compiler_params=_params(("parallel", "arbitrary")),
    )(jj, ii, q, k, v, do, o, lse)


def _rope_tables(positions):
    half = ROPE // 2
    inv_freq = ROPE_BASE ** (-jnp.arange(half, dtype=F32) / half)
    ang = positions.astype(F32)[:, None] * inv_freq
    cos, sin = jnp.cos(ang), jnp.sin(ang)
    return jnp.concatenate([cos] * 4, axis=1), jnp.concatenate([-sin, sin] * 2, axis=1)


def _loss_and_grad(y, target, name):
    s = y.shape[0]
    ts = min(ROW_TILE, s)

    def body(y_ref, t_ref, dy_ref, l_ref):
        e = y_ref[...] - t_ref[...]
        dy_ref[...] = e * (1.0 / D)
        part = jnp.sum(jnp.sum(e * e, axis=-1, keepdims=True) * (0.5 / D), axis=0, keepdims=True)
        part = part * jnp.ones((1, 128), F32)

        @pl.when(pl.program_id(0) == 0)
        def _():
            l_ref[...] = part

        @pl.when(pl.program_id(0) > 0)
        def _():
            l_ref[...] += part

    return pl.pallas_call(
        body, name=name, grid=(s // ts,),
        in_specs=[pl.BlockSpec((ts, D), lambda i: (i, 0))] * 2,
        out_specs=[pl.BlockSpec((ts, D), lambda i: (i, 0)), pl.BlockSpec((1, 128), lambda i: (0, 0))],
        out_shape=[jax.ShapeDtypeStruct((s, D), F32), jax.ShapeDtypeStruct((1, 128), F32)],
        compiler_params=_params(("arbitrary",)),
    )(y, target)


ANY = pl.BlockSpec(memory_space=pl.ANY)


def _all_gather(shard, name):
    def body(x_ref, out_ref, send_sems, recv_sems, local_sem):
        x, y, c = lax.axis_index("x"), lax.axis_index("y"), lax.axis_index("c")
        me, sibling = (x, y, c), (x, y, 1 - c)
        chips = [(1 - x, y), (x, 1 - y), (1 - x, 1 - y)]

        def rows(px, py, pc):
            return out_ref.at[4 * px + 2 * py + pc]

        def copy(k, block, to, src=None):
            return pltpu.make_async_remote_copy(
                src_ref=rows(*block) if src is None else src, dst_ref=rows(*block),
                send_sem=send_sems.at[k], recv_sem=recv_sems.at[k], device_id=to, device_id_type=MESH)

        mine = pltpu.make_async_copy(x_ref, rows(*me), local_sem)
        mine.start()
        first = [copy(0, me, sibling, src=x_ref)]
        first += [copy(1 + j, me, (*chip, c), src=x_ref) for j, chip in enumerate(chips)]
        for cp in first:
            cp.start()
        passed = [copy(4 + j, (*chip, c), sibling) for j, chip in enumerate(chips)]
        for j, chip in enumerate(chips):
            copy(1 + j, (*chip, c), me).wait_recv()
            passed[j].start()
        copy(0, sibling, me).wait_recv()
        for j, chip in enumerate(chips):
            copy(4 + j, (*chip, 1 - c), me).wait_recv()
        for cp in first + passed:
            cp.wait_send()
        mine.wait()

    return pl.pallas_call(
        body, name=name, out_shape=jax.ShapeDtypeStruct((N_DEV,) + shard.shape, shard.dtype),
        in_specs=[ANY], out_specs=ANY,
        scratch_shapes=[pltpu.SemaphoreType.DMA((7,)), pltpu.SemaphoreType.DMA((7,)), pltpu.SemaphoreType.DMA],
    )(shard)


def _exchange(blocks, name):
    def body(x_ref, out_ref, send_sems, recv_sems, local_sem):
        x, y, c = lax.axis_index("x"), lax.axis_index("y"), lax.axis_index("c")
        me = 4 * x + 2 * y + c
        mine = pltpu.make_async_copy(x_ref.at[me], out_ref.at[me], local_sem)
        mine.start()
        copies = []
        for k in range(1, N_DEV):
            px = 1 - x if k & 4 else x
            py = 1 - y if k & 2 else y
            pc = 1 - c if k & 1 else c
            peer = 4 * px + 2 * py + pc
            cp = pltpu.make_async_remote_copy(
                src_ref=x_ref.at[peer], dst_ref=out_ref.at[me], send_sem=send_sems.at[k - 1],
                recv_sem=recv_sems.at[k - 1], device_id=(px, py, pc), device_id_type=MESH)
            cp.start()
            copies.append((cp, pltpu.make_async_remote_copy(
                src_ref=x_ref.at[peer], dst_ref=out_ref.at[peer], send_sem=send_sems.at[k - 1],
                recv_sem=recv_sems.at[k - 1], device_id=(px, py, pc), device_id_type=MESH)))
        for cp, landing in copies:
            landing.wait_recv()
        for cp, landing in copies:
            cp.wait_send()
        mine.wait()

    return pl.pallas_call(
        body, name=name, out_shape=jax.ShapeDtypeStruct(blocks.shape, blocks.dtype),
        in_specs=[ANY], out_specs=ANY,
        scratch_shapes=[pltpu.SemaphoreType.DMA((7,)), pltpu.SemaphoreType.DMA((7,)), pltpu.SemaphoreType.DMA],
    )(blocks)


HBM = pl.BlockSpec(memory_space=pltpu.HBM)
SEM = pl.BlockSpec(memory_space=pltpu.SEMAPHORE)
EFFECT = pltpu.SideEffectType.DATAFLOW_SIDE_EFFECTING


def _peers():
    x, y, c = lax.axis_index("x"), lax.axis_index("y"), lax.axis_index("c")
    peers = []
    for k in range(1, N_DEV):
        px = 1 - x if k & 4 else x
        py = 1 - y if k & 2 else y
        pc = 1 - c if k & 1 else c
        peers.append(((px, py, pc), 4 * px + 2 * py + pc))
    return 4 * x + 2 * y + c, peers


def _send_start(srcs, name, gather):
    n = len(srcs)
    lands = [((N_DEV,) + s.shape) if gather else s.shape for s in srcs]

    def body(*refs):
        src_refs, land_refs = refs[:n], refs[n:2 * n]
        send_sems, recv_sems, token = refs[2 * n], refs[2 * n + 1], refs[-1]
        me, peers = _peers()
        for i in range(n):
            for k, (dev, idx) in enumerate(peers):
                pltpu.make_async_remote_copy(
                    src_ref=src_refs[i] if gather else src_refs[i].at[idx], dst_ref=land_refs[i].at[me],
                    send_sem=send_sems.at[7 * i + k], recv_sem=recv_sems.at[7 * i + k], device_id=dev,
                    device_id_type=MESH).start()
        token[...] = jnp.zeros_like(token)

    res = pl.pallas_call(
        body, name=name,
        out_shape=(pltpu.SemaphoreType.DMA((7 * n,)), pltpu.SemaphoreType.DMA((7 * n,)),
                   *[pltpu.HBM(s.shape, s.dtype) for s in srcs],
                   *[pltpu.HBM(shape, s.dtype) for shape, s in zip(lands, srcs)],
                   jax.ShapeDtypeStruct((8, 128), F32)),
        in_specs=(HBM,) * (2 * n), out_specs=(SEM, SEM) + (HBM,) * (2 * n) + (pl.BlockSpec(memory_space=pltpu.VMEM),),
        input_output_aliases={i: 2 + i for i in range(2 * n)},
        compiler_params=pltpu.CompilerParams(has_side_effects=EFFECT),
    )(*[pltpu.with_memory_space_constraint(s, pltpu.HBM) for s in srcs],
      *[pltpu.with_memory_space_constraint(lax.empty(shape, s.dtype), pltpu.HBM) for shape, s in zip(lands, srcs)])
    return dict(sems=res[:2], srcs=res[2:2 + n], lands=res[2 + n:2 + 2 * n], token=res[-1][0, 0])


def _send_wait(handle, after, name, gather):
    n = len(handle["srcs"])

    def body(*refs):
        src_refs, land_refs = refs[:n], refs[n:2 * n]
        send_sems, recv_sems = refs[2 * n], refs[2 * n + 1]
        me, peers = _peers()
        for i in range(n):
            for k, (dev, idx) in enumerate(peers):
                cp = pltpu.make_async_remote_copy(
                    src_ref=src_refs[i] if gather else src_refs[i].at[idx], dst_ref=land_refs[i].at[idx],
                    send_sem=send_sems.at[7 * i + k], recv_sem=recv_sems.at[7 * i + k], device_id=dev,
                    device_id_type=MESH)
                cp.wait_send()
                cp.wait_recv()

    both = list(handle["srcs"]) + list(handle["lands"])
    res = pl.pallas_call(
        body, name=name, out_shape=tuple(pltpu.HBM(t.shape, t.dtype) for t in both),
        in_specs=(HBM,) * (2 * n) + (SEM, SEM, pl.BlockSpec(memory_space=pl.ANY)), out_specs=(HBM,) * (2 * n),
        input_output_aliases={i: i for i in range(2 * n)},
        compiler_params=pltpu.CompilerParams(has_side_effects=EFFECT),
    )(*both, *handle["sems"], after)
    return res[:n], res[n:]


def _adamw(parts, w, m, v, name, tr=128):
    pieces = len(parts)
    n, r, wd = parts[0].shape
    tr = next((t for t in (tr, 64, 32, 16) if r % t == 0), r)
    nrt = r // tr

    def body(*refs):
        w_ref, m_ref, v_ref, g_ref, d_ref, nm_ref, nv_ref = refs[pieces:]

        def update(p_ref):
            g = p_ref[0].astype(F32)
            for k in range(1, n):
                g = g + p_ref[k].astype(F32)
            m_new = B1 * m_ref[...] + (1.0 - B1) * g
            v_new = B2 * v_ref[...] + (1.0 - B2) * (g * g)
            m_hat = m_new / (1.0 - B1 ** STEP)
            v_hat = v_new / (1.0 - B2 ** STEP)
            g_ref[...] = g
            d_ref[...] = -LR * (m_hat / (jnp.sqrt(v_hat) + ADAM_EPS) + WD * w_ref[...])
            nm_ref[...] = m_new
            nv_ref[...] = v_new

        for p in range(pieces):
            pl.when(pl.program_id(0) == p)(functools.partial(update, refs[p]))

    part_spec = lambda p: pl.BlockSpec((n, tr, wd), lambda l, i: (0, jnp.clip(i + (l - p) * nrt, 0, nrt - 1), 0))
    blk = pl.BlockSpec((tr, wd), lambda l, i: (l * nrt + i, 0))
    return pl.pallas_call(
        body, name=name, grid=(pieces, nrt),
        in_specs=[part_spec(p) for p in range(pieces)] + [blk, blk, blk],
        out_specs=[blk] * 4, out_shape=[jax.ShapeDtypeStruct((pieces * r, wd), F32)] * 4,
        compiler_params=_params(("arbitrary", "arbitrary")),
    )(*parts, w, m, v)


def _outer8(ct, dm, name):
    k, n = ct.shape[0], dm.shape[1]

    def body(c_ref, d_ref, o_ref):
        cv, dv = c_ref[...], d_ref[...]
        acc = cv[:, 0:1] * dv[0:1, :]
        for s in range(1, N_DEV):
            acc = acc + cv[:, s:s + 1] * dv[s:s + 1, :]
        o_ref[...] = acc

    tk = 256
    return pl.pallas_call(
        body, name=name, grid=(k // tk,),
        in_specs=[pl.BlockSpec((tk, N_DEV), lambda i: (i, 0)), pl.BlockSpec((N_DEV, n), lambda i: (0, 0))],
        out_specs=pl.BlockSpec((tk, n), lambda i: (i, 0)), out_shape=jax.ShapeDtypeStruct((k, n), F32),
        compiler_params=_params(("parallel",)),
    )(ct, dm)


FULL = (0, D)
C128 = (0, 128)
HEAD_NOPE = [(h * HEAD_PAD, NOPE) for h in range(HEADS)]
HEAD_ROPE = [(h * HEAD_PAD + NOPE, 128) for h in range(HEADS)]
HEAD_ALL = [(h * HEAD_PAD, HEAD_PAD) for h in range(HEADS)]
HEAD_V = [(h * HEAD, HEAD) for h in range(HEADS)]


def _modulate(x, gain, scale, shift):
    return _rowwise_fwd(_modulate_fn, [(x, FULL)], [gain, scale, shift], [(D, BF16, FULL)], "modulate")[0]


def _modulate_bwd(x, gain, scale, shift, dh, dx_in):
    return _rowwise_bwd(_modulate_fn, [(x, FULL)], [gain, scale, shift], [(D, BF16, FULL)], [dh], [(0, FULL)],
                        [(D, F32)], "modulate_bwd", add={0: dx_in})


def _residual(x, y, gm):
    return _rowwise_fwd(_resgate_fn, [(x, FULL), (y, FULL)], [gm], [(D, F32, FULL)], "residual")[0]


def _residual_bwd(y, gm, dxn):
    return _rowwise_bwd(_gate_only_fn, [(y, FULL)], [gm], [(D, F32, FULL)], [dxn], [(0, FULL)], [(D, BF16)],
                        "residual_bwd")


def _ffn_fwd(x, p):
    h = _modulate(x, p["gain"], p["scale"], p["shift"])
    gate, up, act = _ffn_in(h, p["wg"], p["wu"], "ffn_in")
    y = _mm(act, p["wo"], "nn", "ffn_out")
    return _residual(x, y, p["gm"]), dict(x=x, h=h, gate=gate, up=up, act=act, y=y)


def _ffn_bwd(t, p, dxn):
    dy, dgm = _residual_bwd(t["y"], p["gm"], dxn)
    dgate, dup = _ffn_bwd_act(dy, p["wo"], t["gate"], t["up"], "ffn_bwd_act")
    dwo = _mm(t["act"], dy, "tn", "ffn_dwo")
    dh = _matmul([(dgate, p["wg"]), (dup, p["wu"])], "nt", "ffn_dh")
    dwg = _mm(t["h"], dgate, "tn", "ffn_dwi")
    dwu = _mm(t["h"], dup, "tn", "ffn_dwi")
    dx, dgain, dscale, dshift = _modulate_bwd(t["x"], p["gain"], p["scale"], p["shift"], dh, dxn)
    return dx, dict(gain=dgain, scale=dscale, shift=dshift, gm=dgm, wg=dwg, wu=dwu, wo=dwo)


def _pad128(t):
    return jnp.pad(t, ((0, 0), (0, 128 - t.shape[1])))


def _gdn_fwd(x, p):
    s = x.shape[0]
    h = _modulate(x, p["gain"], p["scale"], p["shift"])
    pm = _mm(h, p["w_main"], "nn", "gdn_proj")
    tail = _mm(h, p["w_tail"], "nn", "gdn_proj_tail")
    qkv = _gdn_conv_fwd(pm, p["conv_w"], "gdn_conv")
    beta, gcum = _rowwise_fwd(_gdn_gates_fn, [(tail, C128), (tail, (128, 128))], [p["a_log"], p["dt_bias"]],
                              [(128, F32, C128)] * 2, "gdn_gates")
    grow = gcum[:, :HEADS].reshape(s // CHUNK, CHUNK, N_GROUPS, GROUP).transpose(0, 2, 3, 1)
    grow = grow.reshape(s // CHUNK, N_GROUPS, 1, GROWS)
    o, states, invs = _gdn_scan_fwd(qkv, beta, gcum, grow, "gdn_scan")
    on, = _rowwise_fwd(_gdn_outnorm_fn, [(o, HEAD_V), (pm, [(3 * D + h_ * HEAD, HEAD) for h_ in range(HEADS)])],
                       [p["norm_g"]], [(D, BF16, HEAD_V)], "gdn_outnorm", groups=HEADS)
    y = _mm(on, p["w_out"], "nn", "mix_out")
    t = dict(x=x, h=h, pm=pm, tail=tail, qkv=qkv, beta=beta, gcum=gcum, grow=grow, o=o, states=states, invs=invs,
             on=on, y=y)
    return _residual(x, y, p["gm"]), t


def _gdn_bwd(t, p, dxn):
    s = dxn.shape[0]
    zc = [(3 * D + h_ * HEAD, HEAD) for h_ in range(HEADS)]
    dy, dgm = _residual_bwd(t["y"], p["gm"], dxn)
    dw_out = _mm(t["on"], dy, "tn", "mix_dwo")
    don = _mm(dy, p["w_out"], "nt", "mix_dout")
    do, dz, dnorm_g = _rowwise_bwd(_gdn_outnorm_fn, [(t["o"], HEAD_V), (t["pm"], zc)], [p["norm_g"]],
                                   [(D, BF16, HEAD_V)], [don], [(0, HEAD_V), (1, HEAD_V)], [(D, F32), (D, F32)],
                                   "gdn_outnorm_bwd", groups=HEADS)
    dq, dk, dv, dbeta, dg, dgr = _gdn_scan_bwd(t["qkv"], t["beta"], t["gcum"], t["grow"], t["states"], t["invs"], do,
                                               "gdn_scan_bwd")
    dg = dg + _pad128(dgr.reshape(s // CHUNK, N_GROUPS, GROUP, CHUNK).transpose(0, 3, 1, 2).reshape(s, HEADS))
    dtail, da_log, ddt = _rowwise_bwd(_gdn_gates_fn, [(t["tail"], C128), (t["tail"], (128, 128))],
                                      [p["a_log"], p["dt_bias"]], [(128, F32, C128)] * 2, [dbeta, dg],
                                      [(0, C128), (0, (128, 128))], [(256, F32)], "gdn_gates_bwd")
    dxs, dcw = [], []
    for part, d in enumerate((dq, dk, dv)):
        dx_, dw_ = _gdn_conv_bwd(t["pm"], p["conv_w"], d, part, "gdn_conv_bwd")
        dxs.append(dx_)
        dcw.append(dw_)
    pieces = dxs + [dz]
    dh = _matmul([(d, p["w_main"]) for d in pieces] + [(dtail, p["w_tail"])], "nt", "gdn_dh",
                 boffs=[0, D, 2 * D, 3 * D, 0], tk=512)
    dw_main = [_mm(t["h"], d, "tn", "gdn_dwi") for d in pieces]
    dw_tail = _mm(t["h"], dtail, "tn", "gdn_dwi_tail")
    dx, dgain, dscale, dshift = _modulate_bwd(t["x"], p["gain"], p["scale"], p["shift"], dh, dxn)
    return dx, dict(gain=dgain, scale=dscale, shift=dshift, gm=dgm, w_main=jnp.concatenate(dw_main, axis=1),
                    w_tail=dw_tail, conv_w=jnp.concatenate(dcw, axis=1), a_log=da_log, dt_bias=ddt,
                    norm_g=dnorm_g, w_out=dw_out)


def _q_rows(q2, cosf, sins):
    return [(q2, HEAD_NOPE), (q2, HEAD_ROPE), (cosf, C128), (sins, C128)]


def _mla_fwd(x, p, kv):
    h = _modulate(x, p["gain"], p["scale"], p["shift"])
    cq = _mm(h, p["w_dq"], "nn", "mla_dq")
    cqn, = _rowwise_fwd(_rms_fn, [(cq, (0, Q_LORA))], [p["q_lora_g"]], [(Q_LORA, BF16, (0, Q_LORA))], "mla_qlora_norm")
    q2 = _mm(cqn, p["w_uq"], "nn", "mla_uq")
    qn, = _rowwise_fwd(_q_norm_rope_fn, _q_rows(q2, kv["cosf"], kv["sins"]), [p["q_gn"], p["q_gr"]],
                       [(HEADS * HEAD_PAD, BF16, HEAD_ALL)], "mla_q_norm", groups=HEADS)
    o, lse = _attn_fwd(qn, kv["kn"], kv["vb"], "mla_attn")
    y = _mm(o, p["w_out"], "nn", "mix_out")
    return _residual(x, y, p["gm"]), dict(x=x, h=h, cq=cq, cqn=cqn, q2=q2, qn=qn, o=o, lse=lse, y=y)


def _mla_bwd(t, p, kv, dxn):
    dy, dgm = _residual_bwd(t["y"], p["gm"], dxn)
    dw_out = _mm(t["o"], dy, "tn", "mix_dwo")
    do = _mm(dy, p["w_out"], "nt", "mix_dout")
    dq, dk, dv = _attn_bwd(t["qn"], kv["kn"], kv["vb"], do, t["o"], t["lse"], "mla_attn_bwd")
    dq2, dq_gn, dq_gr = _rowwise_bwd(_q_norm_rope_fn, _q_rows(t["q2"], kv["cosf"], kv["sins"]), [p["q_gn"], p["q_gr"]],
                                     [(HEADS * HEAD_PAD, BF16, HEAD_ALL)], [dq],
                                     [(0, HEAD_NOPE), (0, HEAD_ROPE), None, None], [(HEADS * HEAD_PAD, F32)],
                                     "mla_q_norm_bwd", groups=HEADS)
    dw_uq = _mm(t["cqn"], dq2, "tn", "mla_dwuq")
    dcqn = _mm(dq2, p["w_uq"], "nt", "mla_dcq")
    dcq, dq_lora_g = _rowwise_bwd(_rms_fn, [(t["cq"], (0, Q_LORA))], [p["q_lora_g"]], [(Q_LORA, BF16, (0, Q_LORA))],
                                  [dcqn], [(0, (0, Q_LORA))], [(Q_LORA, F32)], "mla_qlora_norm_bwd")
    dw_dq = _mm(t["h"], dcq, "tn", "mla_dwdq")
    dh = _mm(dcq, p["w_dq"], "nt", "mla_dh")
    dx, dgain, dscale, dshift = _modulate_bwd(t["x"], p["gain"], p["scale"], p["shift"], dh, dxn)
    grads = dict(gain=dgain, scale=dscale, shift=dshift, gm=dgm, w_dq=dw_dq, q_lora_g=dq_lora_g, w_uq=dw_uq,
                 q_gn=dq_gn, q_gr=dq_gr, w_out=dw_out)
    return dx, grads, dk, dv


def _k_rows(kvp, ckv, cosf, sins):
    return [(kvp, HEAD_NOPE), (kvp, HEAD_ROPE), (ckv, (KV_LORA, 128)), (cosf, C128), (sins, C128)]


def _kv_fwd(x, p, cosf, sins):
    h = _modulate(x, p["gain"], p["scale"], p["shift"])
    ckv = _mm(h, p["w_dkv"], "nn", "kv_down")
    lat, = _rowwise_fwd(_rms_fn, [(ckv, (0, KV_LORA))], [p["kv_g"]], [(KV_LORA, BF16, (0, KV_LORA))], "kv_norm")
    kvp = _mm(lat, p["w_ukv"], "nn", "kv_up")
    kn, vb = _rowwise_fwd(_k_norm_rope_fn, _k_rows(kvp, ckv, cosf, sins), [p["k_gn"], p["k_gr"]],
                          [(HEADS * HEAD_PAD, BF16, HEAD_ALL), (HEADS * HEAD, BF16, HEAD_V)], "kv_k_norm",
                          groups=HEADS)
    return dict(x=x, h=h, ckv=ckv, lat=lat, kvp=kvp, kn=kn, vb=vb, cosf=cosf, sins=sins)


def _kv_bwd(t, p, dk, dv, dx_in):
    dkvp, drope, dk_gn, dk_gr = _rowwise_bwd(
        _k_norm_rope_fn, _k_rows(t["kvp"], t["ckv"], t["cosf"], t["sins"]), [p["k_gn"], p["k_gr"]],
        [(HEADS * HEAD_PAD, BF16, HEAD_ALL), (HEADS * HEAD, BF16, HEAD_V)], [dk, dv],
        [(0, HEAD_NOPE), (0, HEAD_ROPE), (1, C128), None, None], [(HEADS * HEAD_PAD, F32), (128, F32)],
        "kv_k_norm_bwd", groups=HEADS)
    dw_ukv = _mm(t["lat"], dkvp, "tn", "kv_dwukv")
    dlat = _mm(dkvp, p["w_ukv"], "nt", "kv_dlat")
    dckv, dkv_g = _rowwise_bwd(_rms_fn, [(t["ckv"], (0, KV_LORA))], [p["kv_g"]], [(KV_LORA, BF16, (0, KV_LORA))],
                               [dlat], [(0, (0, KV_LORA))], [(KV_LORA, F32)], "kv_norm_bwd")
    dw_dkv = jnp.concatenate([_mm(t["h"], dckv, "tn", "kv_dwdkv"), _mm(t["h"], drope, "tn", "kv_dwdkv_rope")], axis=1)
    dh = _matmul([(dckv, p["w_dkv"]), (drope, p["w_dkv"])], "nt", "kv_dh", boffs=[0, KV_LORA])
    dx, dgain, dscale, dshift = _modulate_bwd(t["x"], p["gain"], p["scale"], p["shift"], dh, dx_in)
    return dx, dict(gain=dgain, scale=dscale, shift=dshift, w_dkv=dw_dkv, kv_g=dkv_g, w_ukv=dw_ukv, k_gn=dk_gn,
                    k_gr=dk_gr)


WEIGHTS = ["ada_w", "ada_b", "norm_g", "ffn_w_in", "ffn_w_out", "gdn_w_in", "gdn_conv_w", "gdn_a_log", "gdn_dt_bias",
           "gdn_norm_g", "gdn_w_out", "kv_ada_w", "kv_ada_b", "kv_norm_g", "mla_w_dkv", "mla_kv_norm_g", "mla_w_ukv",
           "mla_k_norm_g", "mla_w_dq", "mla_q_lora_norm_g", "mla_w_uq", "mla_q_norm_g", "mla_w_out"]
SMALL = [("ada_b", 4 * N_MOD * D), ("kv_ada_b", 2 * D), ("norm_g", DEPTH * 3 * D), ("gdn_conv_w", N_A * CONV_K * 3 * D),
         ("gdn_a_log", N_A * HEADS), ("gdn_dt_bias", N_A * HEADS), ("gdn_norm_g", N_A * HEAD), ("kv_norm_g", D),
         ("mla_kv_norm_g", KV_LORA), ("mla_k_norm_g", QK_HEAD), ("mla_q_lora_norm_g", 2 * Q_LORA),
         ("mla_q_norm_g", 2 * QK_HEAD)]
SMALL_REPLICATED = [n for n, _ in SMALL if n not in ("norm_g", "gdn_conv_w")]


def _silu_fn(g, t):
    return (_silu(t),)


def _dup_rope(t):
    return jnp.concatenate([t[..., :NOPE], t[..., NOPE:], t[..., NOPE:]], axis=-1)


def _fold_rope(t):
    return jnp.concatenate([t[..., :NOPE], t[..., NOPE:QK_HEAD] + t[..., QK_HEAD:]], axis=-1)


def _pack(pieces, rows):
    flat = jnp.concatenate([p.reshape(-1).astype(F32) for p in pieces])
    return jnp.pad(flat, (0, rows * 128 - flat.shape[0])).reshape(rows, 128)


def _step(a):
    me = 4 * lax.axis_index("x") + 2 * lax.axis_index("y") + lax.axis_index("c")
    x = a["x"][0]
    cosf, sins = _rope_tables(a["positions"][0])

    n_cw, n_ng = N_A * CONV_K * 3 * HEAD, DEPTH * 3 * HEAD
    small_all = _all_gather(_pack([a["gdn_conv_w"], a["norm_g"], a["c"]], 44), "gather_small").reshape(N_DEV, -1)
    conv_w = small_all[:, :n_cw].reshape(N_DEV, N_A, CONV_K, 3 * HEAD).transpose(1, 2, 0, 3).reshape(N_A, CONV_K, 3 * D)
    norm_g = small_all[:, n_cw:n_cw + n_ng].reshape(N_DEV, DEPTH, 3, HEAD).transpose(1, 2, 0, 3).reshape(DEPTH, 3, D)
    c_all = small_all[:, n_cw + n_ng:n_cw + n_ng + D]

    c_act, = _rowwise_fwd(_silu_fn, [(c_all, FULL)], [], [(D, F32, FULL)], "c_act")
    n_ada = N_MOD * D // N_DEV
    parts = [_mm(c_act, a["ada_w"][l], "nn", "mod_proj") for l in range(DEPTH)]
    parts.append(_mm(c_act, a["kv_ada_w"], "nn", "mod_proj_kv"))
    mod_recv = _exchange(jnp.concatenate(parts, axis=1)[:, None, :], "exchange_mod")[:, 0]
    mod = mod_recv[:, :DEPTH * n_ada].reshape(N_DEV, DEPTH, n_ada).transpose(1, 0, 2).reshape(DEPTH, N_MOD * D)
    mod = (mod + a["ada_b"]).reshape(DEPTH, N_MOD, D)
    kvmod = mod_recv[:, DEPTH * n_ada:].reshape(2 * D) + a["kv_ada_b"]

    n_in = 2 * D_FF // N_DEV
    n_gdn = (4 * D + 2 * HEADS) // N_DEV

    stages = [(l, part) for l in range(DEPTH) for part in range(3)]

    def stage_shards(l, part):
        if part != 1:
            sh = {"ffn_w_in": a["ffn_w_in"][l, part // 2], "ffn_w_out": a["ffn_w_out"][l, part // 2]}
            if part == 2 and l == N_A - 1:
                sh.update(mla_w_dkv=a["mla_w_dkv"], mla_w_ukv=a["mla_w_ukv"])
            return sh
        if l < N_A:
            return {"gdn_w_in": a["gdn_w_in"][l], "gdn_w_out": a["gdn_w_out"][l]}
        j = l - N_A
        return {"mla_w_dq": a["mla_w_dq"][j], "mla_w_uq": a["mla_w_uq"][j], "mla_w_out": a["mla_w_out"][j]}

    def zero_of(t):
        return jnp.minimum(jnp.abs(t[(0,) * t.ndim].astype(F32)), 0.0)

    def start_stage(l, part, tie):
        sh = stage_shards(l, part)
        return list(sh), _send_start([(w + tie).astype(BF16) for w in sh.values()], f"fetch_start_{l}_{part}", gather=True)

    def finish_stage(l, part, names, handle, after):
        srcs, lands = _send_wait(handle, after, f"fetch_wait_{l}_{part}", gather=True)
        return {n: lax.dynamic_update_slice(land, src[None], (me, 0, 0)) for n, src, land in zip(names, srcs, lands)}

    def row(v):
        return v[None]

    def ffn_params(l, i, w):
        w_in = w["ffn_w_in"]
        k = 0 if i == 0 else 6
        return dict(gain=row(norm_g[l, 0 if i == 0 else 2]), shift=row(mod[l, k]), scale=row(mod[l, k + 1]),
                    gm=0.5 * row(mod[l, k + 2]),
                    wg=w_in[:N_DEV // 2].transpose(1, 0, 2).reshape(D, D_FF),
                    wu=w_in[N_DEV // 2:].transpose(1, 0, 2).reshape(D, D_FF),
                    wo=w["ffn_w_out"].reshape(D_FF, D))

    def gdn_params(l, w):
        w_in = w["gdn_w_in"].transpose(1, 0, 2).reshape(D, 4 * D + 2 * HEADS)
        pad = lambda t: jnp.pad(t, ((0, 0), (0, 128 - HEADS)))
        return dict(gain=row(norm_g[l, 1]), shift=row(mod[l, 3]), scale=row(mod[l, 4]), gm=row(mod[l, 5]),
                    w_main=w_in[:, :4 * D],
                    w_tail=jnp.concatenate([pad(w_in[:, 4 * D:4 * D + HEADS]), pad(w_in[:, 4 * D + HEADS:])], axis=1),
                    conv_w=conv_w[l], a_log=_pad128(row(a["gdn_a_log"][l])), dt_bias=_pad128(row(a["gdn_dt_bias"][l])),
                    norm_g=row(a["gdn_norm_g"][l]), w_out=w["gdn_w_out"].reshape(D, D))

    def mla_params(l, w):
        j = l - N_A
        uq = w["mla_w_uq"].transpose(1, 0, 2)
        qg = _dup_rope(a["mla_q_norm_g"][j])
        return dict(gain=row(norm_g[l, 1]), shift=row(mod[l, 3]), scale=row(mod[l, 4]), gm=row(mod[l, 5]),
                    w_dq=w["mla_w_dq"].reshape(D, Q_LORA), q_lora_g=row(a["mla_q_lora_norm_g"][j]),
                    w_uq=_dup_rope(uq).reshape(Q_LORA, HEADS * HEAD_PAD), q_gn=row(qg[:NOPE]), q_gr=row(qg[NOPE:]),
                    w_out=w["mla_w_out"].reshape(D, D))

    def kv_params(w):
        w_dkv = w["mla_w_dkv"].reshape(D, KV_LORA + ROPE)
        kg = _dup_rope(a["mla_k_norm_g"])
        return dict(gain=row(a["kv_norm_g"]), shift=row(kvmod[:D]), scale=row(kvmod[D:]),
                    w_dkv=jnp.concatenate([w_dkv, w_dkv[:, KV_LORA:]], axis=1), kv_g=row(a["mla_kv_norm_g"]),
                    w_ukv=w["mla_w_ukv"].transpose(1, 0, 2).reshape(KV_LORA, HEADS * 2 * HEAD), k_gn=row(kg[:NOPE]),
                    k_gr=row(kg[NOPE:]))

    tapes, kv, kv_p = [[] for _ in range(DEPTH)], None, None
    pend = start_stage(0, 0, 0.0)
    for n, (l, part) in enumerate(stages):
        names, handle = pend
        w = finish_stage(l, part, names, handle, x)
        token = handle["token"]
        if n + 1 < len(stages):
            pend = start_stage(*stages[n + 1], zero_of(w[names[0]]))
            token = token + pend[1]["token"]
        if part != 1:
            p = ffn_params(l, part // 2, w)
        else:
            p = gdn_params(l, w) if l < N_A else mla_params(l, w)
        p["gain"] = p["gain"] + token
        if part != 1:
            x, t = _ffn_fwd(x, p)
        else:
            x, t = _gdn_fwd(x, p) if l < N_A else _mla_fwd(x, p, kv)
        tapes[l] += [p, t]
        if part == 2 and l == N_A - 1:
            kv_p = kv_params(w)
            kv = _kv_fwd(x, kv_p, cosf, sins)
    dx, loss_blk = _loss_and_grad(x, a["loss_target"][0], "loss")
    loss = lax.psum(loss_blk[0, 0], ("x", "y", "c"))

    def by_cols(g, n):
        return g.reshape(g.shape[0], -1, n).transpose(1, 0, 2)

    def ffn_blocks(g):
        return {"ffn_w_in": jnp.concatenate([by_cols(g["wg"], n_in), by_cols(g["wu"], n_in)], axis=0),
                "ffn_w_out": g["wo"].reshape(N_DEV, D_FF // N_DEV, D)}

    def mixer_blocks(l, g):
        if l < N_A:
            full = jnp.concatenate([g["w_main"], g["w_tail"][:, :HEADS], g["w_tail"][:, 128:128 + HEADS]], axis=1)
            return {"gdn_w_in": by_cols(full, n_gdn), "gdn_w_out": g["w_out"].reshape(N_DEV, D // N_DEV, D)}
        return {"mla_w_dq": g["w_dq"].reshape(N_DEV, D // N_DEV, Q_LORA),
                "mla_w_uq": _fold_rope(g["w_uq"].reshape(Q_LORA, HEADS, HEAD_PAD)).transpose(1, 0, 2),
                "mla_w_out": g["w_out"].reshape(N_DEV, D // N_DEV, D)}

    sent = []

    def send(key, blocks):
        handle = _send_start([b.astype(BF16) for b in blocks.values()], "grad_start_" + "_".join(map(str, key)),
                             gather=False)
        sent.append((key, list(blocks), handle))
        return handle["token"]

    grads = [None] * DEPTH
    dk_sum = dv_sum = kv_grads = None
    token = jnp.zeros((), F32)
    for l in reversed(range(DEPTH)):
        p1, t1, pm_, tm_, p2, t2 = tapes[l]
        if l == N_A - 1:
            dx, kv_grads = _kv_bwd(kv, kv_p, dk_sum, dv_sum, dx)
            d_dkv = kv_grads["w_dkv"]
            token = token + send((l, 3), {
                "mla_w_dkv": jnp.concatenate(
                    [d_dkv[:, :KV_LORA], d_dkv[:, KV_LORA:KV_LORA + ROPE] + d_dkv[:, KV_LORA + ROPE:]],
                    axis=1).reshape(N_DEV, D // N_DEV, KV_LORA + ROPE),
                "mla_w_ukv": by_cols(kv_grads["w_ukv"], 2 * HEAD)})
        dx, g2 = _ffn_bwd(t2, dict(p2, gm=p2["gm"] + token), dx)
        token = send((l, 2), ffn_blocks(g2))
        pm_ = dict(pm_, gm=pm_["gm"] + token)
        if l < N_A:
            dx, gm_ = _gdn_bwd(tm_, pm_, dx)
        else:
            dx, gm_, dk, dv = _mla_bwd(tm_, pm_, kv, dx)
            dk_sum = dk if dk_sum is None else dk_sum + dk
            dv_sum = dv if dv_sum is None else dv_sum + dv
        token = send((l, 1), mixer_blocks(l, gm_))
        dx, g1 = _ffn_bwd(t1, dict(p1, gm=p1["gm"] + token), dx)
        token = send((l, 0), ffn_blocks(g1))
        grads[l] = (g1, gm_, g2)

    pieces = {}
    for key, names, handle in sent:
        srcs, lands = _send_wait(handle, dx, "grad_wait_" + "_".join(map(str, key)), gather=False)
        for name, src, land in zip(names, srcs, lands):
            own = lax.dynamic_slice_in_dim(src, me, 1, axis=0)
            pieces.setdefault(name, []).append((key, lax.dynamic_update_slice(land, own, (me, 0, 0))))
    out = {}
    for name, parts in pieces.items():
        wide = a[name].shape[-1]
        out[name] = _adamw([p for _, p in sorted(parts, key=lambda kp: kp[0])], a[name].reshape(-1, wide),
                           a["m_" + name].reshape(-1, wide), a["v_" + name].reshape(-1, wide), "adamw")

    def dmod(l):
        g1, gm_, g2 = grads[l]
        return jnp.concatenate([g1["shift"], g1["scale"], 0.5 * g1["gm"], gm_["shift"], gm_["scale"], gm_["gm"],
                                g2["shift"], g2["scale"], 0.5 * g2["gm"]], axis=1)

    gdn = [grads[l][1] for l in range(N_A)]
    mla = [grads[l][1] for l in range(N_A, DEPTH)]
    small = {
        "ada_b": jnp.concatenate([dmod(l) for l in range(DEPTH)], axis=0),
        "kv_ada_b": jnp.concatenate([kv_grads["shift"], kv_grads["scale"]], axis=1),
        "norm_g": jnp.stack([jnp.concatenate([grads[l][0]["gain"], grads[l][1]["gain"], grads[l][2]["gain"]], axis=0)
                             for l in range(DEPTH)]),
        "gdn_conv_w": jnp.stack([g["conv_w"] for g in gdn]),
        "gdn_a_log": jnp.stack([g["a_log"][0, :HEADS] for g in gdn]),
        "gdn_dt_bias": jnp.stack([g["dt_bias"][0, :HEADS] for g in gdn]),
        "gdn_norm_g": jnp.stack([g["norm_g"][0] for g in gdn]),
        "kv_norm_g": kv_grads["gain"],
        "mla_kv_norm_g": kv_grads["kv_g"],
        "mla_k_norm_g": _fold_rope(jnp.concatenate([kv_grads["k_gn"], kv_grads["k_gr"]], axis=1)),
        "mla_q_lora_norm_g": jnp.stack([g["q_lora_g"][0] for g in mla]),
        "mla_q_norm_g": jnp.stack([_fold_rope(jnp.concatenate([g["q_gn"], g["q_gr"]], axis=1))[0] for g in mla]),
    }
    rows = 616
    assert sum(n for _, n in SMALL) <= rows * 128 and all(small[n].size == k for n, k in SMALL)
    small_recv = _all_gather(_pack([small[n] for n, _ in SMALL], rows), "gather_small_grads")
    zero = lambda n, k: jnp.zeros((k,), F32)
    packed = {pre: _pack([a[pre + n] if n in SMALL_REPLICATED else zero(n, k) for n, k in SMALL], rows)
              for pre in ("", "m_", "v_")}
    res = _adamw([small_recv], packed[""], packed["m_"], packed["v_"], "adamw_small")
    offs = {}
    o = 0
    for n, k in SMALL:
        offs[n] = o
        o += k
    for n, k in SMALL:
        if n in SMALL_REPLICATED:
            out[n] = [r.reshape(-1)[offs[n]:offs[n] + k] for r in res]
    gsum = res[0].reshape(-1)
    g_norm = lax.dynamic_slice_in_dim(gsum[offs["norm_g"]:offs["norm_g"] + DEPTH * 3 * D].reshape(DEPTH * 3, D),
                                      me * HEAD, HEAD, axis=1)
    g_conv = lax.dynamic_slice_in_dim(
        gsum[offs["gdn_conv_w"]:offs["gdn_conv_w"] + N_A * CONV_K * 3 * D].reshape(N_A * CONV_K, 3 * D),
        me * 3 * HEAD, 3 * HEAD, axis=1)
    res2 = _adamw([_pack([g_norm, g_conv], 36)[None]], *[_pack([a[pre + "norm_g"], a[pre + "gdn_conv_w"]], 36)
                                                      for pre in ("", "m_", "v_")], "adamw_small")
    out["norm_g"] = [r.reshape(-1)[:n_ng] for r in res2]
    out["gdn_conv_w"] = [r.reshape(-1)[n_ng:n_ng + n_cw] for r in res2]

    c_act_t = c_act.T
    all_small = small_recv.reshape(N_DEV, -1)
    dmod_all = all_small[:, :DEPTH * N_MOD * D].reshape(N_DEV, DEPTH, N_MOD * D)
    dmod_mine = lax.dynamic_slice_in_dim(dmod_all, me * n_ada, n_ada, axis=2)
    g_ada = [_outer8(c_act_t, dmod_mine[:, l], "ada_grad")[None] for l in range(DEPTH)]
    out["ada_w"] = _adamw(g_ada, *[a[pre + "ada_w"].reshape(DEPTH * D, n_ada) for pre in ("", "m_", "v_")], "adamw")
    dkv_all = all_small[:, offs["kv_ada_b"]:offs["kv_ada_b"] + 2 * D]
    g_kv = _outer8(c_act_t, lax.dynamic_slice_in_dim(dkv_all, me * (2 * D // N_DEV), 2 * D // N_DEV, axis=1), "ada_grad")
    out["kv_ada_w"] = _adamw([g_kv[None]], *[a[pre + "kv_ada_w"] for pre in ("", "m_", "v_")], "adamw")

    result = [loss, dx[None]]
    for k in range(4):
        result += [out[n][k].reshape(a[n].shape) for n in WEIGHTS]
    return tuple(result)


def kernel(x, c, positions, ada_w, ada_b, norm_g, ffn_w_in, ffn_w_out, gdn_w_in, gdn_conv_w, gdn_a_log, gdn_dt_bias, gdn_norm_g, gdn_w_out, kv_ada_w, kv_ada_b, kv_norm_g, mla_w_dkv, mla_kv_norm_g, mla_w_ukv, mla_k_norm_g, mla_w_dq, mla_q_lora_norm_g, mla_w_uq, mla_q_norm_g, mla_w_out, loss_target, m_ada_w, m_ada_b, m_norm_g, m_ffn_w_in, m_ffn_w_out, m_gdn_w_in, m_gdn_conv_w, m_gdn_a_log, m_gdn_dt_bias, m_gdn_norm_g, m_gdn_w_out, m_kv_ada_w, m_kv_ada_b, m_kv_norm_g, m_mla_w_dkv, m_mla_kv_norm_g, m_mla_w_ukv, m_mla_k_norm_g, m_mla_w_dq, m_mla_q_lora_norm_g, m_mla_w_uq, m_mla_q_norm_g, m_mla_w_out, v_ada_w, v_ada_b, v_norm_g, v_ffn_w_in, v_ffn_w_out, v_gdn_w_in, v_gdn_conv_w, v_gdn_a_log, v_gdn_dt_bias, v_gdn_norm_g, v_gdn_w_out, v_kv_ada_w, v_kv_ada_b, v_kv_norm_g, v_mla_w_dkv, v_mla_kv_norm_g, v_mla_w_ukv, v_mla_k_norm_g, v_mla_w_dq, v_mla_q_lora_norm_g, v_mla_w_uq, v_mla_q_norm_g, v_mla_w_out):
    return _step(dict(locals()))
```

```python
import functools
import math

import jax
import jax.numpy as jnp
from jax import lax
from jax.experimental import pallas as pl
from jax.experimental.pallas import tpu as pltpu

F32 = jnp.float32
BF16 = jnp.bfloat16

N_DEV = 8
D = 1024
D_FF = 2816
DEPTH = 4
N_A = 2
N_MOD = 9
HEADS = 8
HEAD = 128
CHUNK = 64
CONV_K = 4
KV_LORA = 256
Q_LORA = 384
NOPE = 128
ROPE = 64
QK_HEAD = NOPE + ROPE
HEAD_PAD = 256
ROPE_BASE = 10000.0
EPS = 1e-6
LR, B1, B2, ADAM_EPS, WD, STEP = 0.001, 0.9, 0.999, 1e-08, 0.01, 10

VMEM_LIMIT = 48 * 1024 * 1024
ROW_TILE = 256
MESH = pl.DeviceIdType.MESH

_NN = (((1,), (0,)), ((), ()))
_NT = (((1,), (1,)), ((), ()))
_TN = (((0,), (0,)), ((), ()))
_DIMS = {"nn": _NN, "nt": _NT, "tn": _TN}


def _params(dims=None):
    return pltpu.CompilerParams(dimension_semantics=dims, vmem_limit_bytes=VMEM_LIMIT)


def _tile(n, target):
    for t in range(target - target % 128, 0, -128):
        if n % t == 0:
            return t
    return n


_TIE_SPEC1 = pl.BlockSpec((8, 128), lambda i: (0, 0))
_TIE_SPEC2 = pl.BlockSpec((8, 128), lambda i, j: (0, 0))
_TIE_SPEC3 = pl.BlockSpec((8, 128), lambda i, j, k: (0, 0))


def _matmul(pairs, form, name, out_dtype=F32, tm=1408, tn=1408, tk=1408, boffs=None, resid=None, ties=()):
    a0, b0 = pairs[0]
    if form == "nn":
        m, n = a0.shape[0], b0.shape[1]
        ks = [a.shape[1] for a, _ in pairs]
    elif form == "nt":
        m, n = a0.shape[0], b0.shape[0]
        ks = [a.shape[1] for a, _ in pairs]
    else:
        m, n = a0.shape[1], b0.shape[1]
        ks = [a.shape[0] for a, _ in pairs]
    tm, tn = _tile(m, tm), _tile(n, tn)
    tks = [_tile(k, tk) for k in ks]
    boffs = boffs or [0] * len(pairs)
    assert m % tm == 0 and n % tn == 0 and all(o % t == 0 for o, t in zip(boffs, tks)), (name, m, n, ks)
    steps = [k // t for k, t in zip(ks, tks)]
    starts = [sum(steps[:p]) for p in range(len(pairs))]
    nk = sum(steps)

    def kidx(p, k):
        return jnp.clip(k - starts[p], 0, steps[p] - 1)

    in_specs, args = [], []
    for p, (a, b) in enumerate(pairs):
        t = tks[p]
        if form == "tn":
            in_specs.append(pl.BlockSpec((t, tm), lambda i, j, k, p=p: (kidx(p, k), i)))
            in_specs.append(pl.BlockSpec((t, tn), lambda i, j, k, p=p: (kidx(p, k), j)))
        elif form == "nn":
            in_specs.append(pl.BlockSpec((tm, t), lambda i, j, k, p=p: (i, kidx(p, k))))
            in_specs.append(pl.BlockSpec((t, tn), lambda i, j, k, p=p: (kidx(p, k), j)))
        else:
            in_specs.append(pl.BlockSpec((tm, t), lambda i, j, k, p=p: (i, kidx(p, k))))
            in_specs.append(pl.BlockSpec((tn, t), lambda i, j, k, p=p, o=boffs[p] // t: (j, kidx(p, k) + o)))
        args += [a, b]
    dims = _DIMS[form]
    npairs = len(pairs)
    nin = 2 * npairs + len(ties) + (2 if resid else 0)
    out_blk = pl.BlockSpec((tm, tn), lambda i, j, k: (i, j))
    in_specs += [_TIE_SPEC3] * len(ties)
    args += list(ties)
    if resid:
        in_specs += [out_blk, pl.BlockSpec((1, tn), lambda i, j, k: (0, j))]
        args += list(resid)

    def body(*refs):
        o_ref = refs[nin]
        k = pl.program_id(2)

        def prod(p):
            return lax.dot_general(refs[2 * p][...].astype(BF16), refs[2 * p + 1][...].astype(BF16), dims,
                                   preferred_element_type=F32)

        def finish(y):
            o_ref[...] = y.astype(o_ref.dtype)
            if resid:
                refs[nin + 1][...] = refs[nin - 2][...] + refs[nin - 1][...] * y

        if nk == 1:
            finish(prod(0))
            return
        acc = refs[-1]

        @pl.when(k == 0)
        def _():
            acc[...] = jnp.zeros_like(acc)

        for p in range(npairs):
            @pl.when((k >= starts[p]) & (k < starts[p] + steps[p]))
            def _(p=p):
                acc[...] += prod(p)

        @pl.when(k == nk - 1)
        def _():
            finish(acc[...])

    res = pl.pallas_call(
        body, name=name, grid=(m // tm, n // tn, nk), in_specs=in_specs,
        out_specs=[out_blk, out_blk] if resid else out_blk,
        out_shape=[jax.ShapeDtypeStruct((m, n), out_dtype)] * 2 if resid else jax.ShapeDtypeStruct((m, n), out_dtype),
        scratch_shapes=[] if nk == 1 else [pltpu.VMEM((tm, tn), F32)],
        compiler_params=_params(("parallel", "parallel", "arbitrary")),
    )(*args)
    return res


def _mm(a, b, form, name, **kw):
    return _matmul([(a, b)], form, name, **kw)


def _cols(spec, g):
    return spec[g] if isinstance(spec, list) else spec


def _rowwise_fwd(fn, rows, pars, outs, name, groups=1, ts=ROW_TILE, ties=()):
    s = rows[0][0].shape[0]
    ts = min(ts, s)
    assert s % ts == 0
    nr, npar = len(rows), len(pars)

    def body(*refs):
        par_t = [r[...] for r in refs[nr:nr + npar]]
        out_refs = refs[nr + npar + len(ties):]
        for g in range(groups):
            row_t = []
            for r, (_, spec) in zip(refs[:nr], rows):
                c0, w = _cols(spec, g)
                row_t.append(r[:, c0:c0 + w].astype(F32))
            res = fn(g, *row_t, *par_t)
            for o_ref, val, (_, _, spec) in zip(out_refs, res, outs):
                c0, w = _cols(spec, g)
                o_ref[:, c0:c0 + w] = val.astype(o_ref.dtype)

    return pl.pallas_call(
        body, name=name, grid=(s // ts,),
        in_specs=[pl.BlockSpec((ts, a.shape[1]), lambda i: (i, 0)) for a, _ in rows]
        + [pl.BlockSpec(p.shape, lambda i: (0, 0)) for p in pars] + [_TIE_SPEC1] * len(ties),
        out_specs=[pl.BlockSpec((ts, w), lambda i: (i, 0)) for w, _, _ in outs],
        out_shape=[jax.ShapeDtypeStruct((s, w), dt) for w, dt, _ in outs],
        compiler_params=_params(("parallel",)),
    )(*[a for a, _ in rows], *pars, *ties)


def _rowwise_bwd(fn, rows, pars, outs, douts, gmap, gshapes, name, groups=1, add=None, par_grads=True,
                 ts=ROW_TILE):
    s = rows[0][0].shape[0]
    ts = min(ts, s)
    assert s % ts == 0
    nr, npar, nout, ng = len(rows), len(pars), len(outs), len(gshapes)
    add = add or {}
    add_keys = sorted(add)

    def body(*refs):
        row_refs = refs[:nr]
        par_refs = refs[nr:nr + npar]
        dout_refs = refs[nr + npar:nr + npar + nout]
        add_refs = refs[nr + npar + nout:nr + npar + nout + len(add_keys)]
        g_refs = refs[nr + npar + nout + len(add_keys):][:ng]
        pg_refs = refs[nr + npar + nout + len(add_keys) + ng:]
        par_t = [r[...] for r in par_refs]
        par_acc = [None] * npar
        shared_acc = {}
        for g in range(groups):
            row_t = []
            for r, (_, spec) in zip(row_refs, rows):
                c0, w = _cols(spec, g)
                row_t.append(r[:, c0:c0 + w].astype(F32))
            cts = []
            for r, (_, _, spec) in zip(dout_refs, outs):
                c0, w = _cols(spec, g)
                cts.append(r[:, c0:c0 + w].astype(F32))
            _, vjp = jax.vjp(lambda *t, g=g: tuple(fn(g, *t)), *row_t, *par_t)
            grads = vjp(tuple(cts))
            for k in range(nr):
                if gmap[k] is None:
                    continue
                gi, spec = gmap[k]
                if isinstance(spec, list) or groups == 1:
                    c0, w = _cols(spec, g)
                    val = grads[k]
                    if gi in add:
                        val = val + add_refs[add_keys.index(gi)][:, c0:c0 + w].astype(F32)
                    g_refs[gi][:, c0:c0 + w] = val.astype(g_refs[gi].dtype)
                else:
                    shared_acc[k] = grads[k] if k not in shared_acc else shared_acc[k] + grads[k]
            if par_grads:
                for k in range(npar):
                    pg = grads[nr + k]
                    par_acc[k] = pg if par_acc[k] is None else par_acc[k] + pg
        for k, val in shared_acc.items():
            gi, (c0, w) = gmap[k]
            assert gi not in add
            g_refs[gi][:, c0:c0 + w] = val.astype(g_refs[gi].dtype)
        if par_grads:
            first = pl.program_id(0) == 0
            for k in range(npar):
                @pl.when(first)
                def _(k=k):
                    pg_refs[k][...] = par_acc[k]

                @pl.when(jnp.logical_not(first))
                def _(k=k):
                    pg_refs[k][...] += par_acc[k]

    out_specs = [pl.BlockSpec((ts, w), lambda i: (i, 0)) for w, _ in gshapes]
    out_shape = [jax.ShapeDtypeStruct((s, w), dt) for w, dt in gshapes]
    if par_grads:
        out_specs += [pl.BlockSpec(p.shape, lambda i: (0, 0)) for p in pars]
        out_shape += [jax.ShapeDtypeStruct(p.shape, F32) for p in pars]
    return pl.pallas_call(
        body, name=name, grid=(s // ts,),
        in_specs=[pl.BlockSpec((ts, a.shape[1]), lambda i: (i, 0)) for a, _ in rows]
        + [pl.BlockSpec(p.shape, lambda i: (0, 0)) for p in pars]
        + [pl.BlockSpec((ts, a.shape[1]), lambda i: (i, 0)) for a in douts]
        + [pl.BlockSpec((ts, add[k].shape[1]), lambda i: (i, 0)) for k in add_keys],
        out_specs=out_specs, out_shape=out_shape,
        compiler_params=_params(("arbitrary",)),
    )(*[a for a, _ in rows], *pars, *douts, *[add[k] for k in add_keys])


def _sigmoid(x):
    return 1.0 / (1.0 + jnp.exp(-x))


def _silu(x):
    return x * _sigmoid(x)


def _softplus(x):
    return jnp.maximum(x, 0.0) + jnp.log(1.0 + jnp.exp(-jnp.abs(x)))


def _rms(t, g, n=None):
    n = n or t.shape[-1]
    return t * lax.rsqrt(jnp.sum(t * t, axis=-1, keepdims=True) / n + EPS) * g


def _modulate_fn(g, x, gain, scale, shift):
    return (_rms(x, gain) * (1.0 + scale) + shift,)


def _resgate_fn(g, x, y, gm):
    return (x + gm * y,)


def _gate_only_fn(g, y, gm):
    return (gm * y,)


def _gdn_gates_fn(g, b_logit, a_logit, a_log, dt_bias):
    gate = -jnp.exp(a_log) * _softplus(a_logit + dt_bias)
    n = gate.shape[0]
    i = lax.broadcasted_iota(jnp.int32, (n, n), 0)
    j = lax.broadcasted_iota(jnp.int32, (n, n), 1)
    tri = (((i // CHUNK) == (j // CHUNK)) & (i >= j)).astype(F32)
    gcum = lax.dot_general(tri, gate, _NN, preferred_element_type=F32, precision=lax.Precision.HIGHEST)
    return _sigmoid(b_logit), gcum


def _gdn_outnorm_fn(g, o, z, gain):
    return (_rms(o, gain) * _silu(z),)


def _rms_fn(g, t, gain):
    return (_rms(t, gain),)


@jax.custom_vjp
def _swap_halves(t):
    return pltpu.roll(t, 32, 1)


_swap_halves.defvjp(lambda t: (pltpu.roll(t, 32, 1), None), lambda _, ct: (pltpu.roll(ct, 96, 1),))


def _head_norm_rope_fn(g, nope, rope, cosf, sins, gain_n, gain_r):
    first = lax.broadcasted_iota(jnp.int32, rope.shape, 1) < ROPE
    ss = jnp.sum(nope * nope, axis=-1, keepdims=True) + jnp.sum(jnp.where(first, rope * rope, 0.0), axis=-1,
                                                                 keepdims=True)
    r = lax.rsqrt(ss / QK_HEAD + EPS)
    tn = nope * r * gain_n
    tr = rope * r * gain_r
    rot = jnp.where(first, tr * cosf + _swap_halves(tr) * sins, 0.0)
    return tn, rot


def _q_norm_rope_fn(g, nope, rope, cosf, sins, gain_n, gain_r):
    tn, rot = _head_norm_rope_fn(g, nope, rope, cosf, sins, gain_n, gain_r)
    return (jnp.concatenate([tn, rot], axis=1),)


def _k_norm_rope_fn(g, nope, val, rope, cosf, sins, gain_n, gain_r):
    tn, rot = _head_norm_rope_fn(g, nope, rope, cosf, sins, gain_n, gain_r)
    return jnp.concatenate([tn, rot], axis=1), val


def _loss_fn(g, y, target):
    e = y - target
    return (jnp.sum(e * e, axis=-1, keepdims=True) * (0.5 / D) * jnp.ones((1, 128), F32),)


def _ffn_in(h, wg, wu, name, tm=1024, tn=256):
    s = h.shape[0]
    tm = min(tm, s)

    def body(h_ref, wg_ref, wu_ref, g_ref, u_ref, a_ref):
        hb = h_ref[...]
        gate = jnp.dot(hb, wg_ref[...], preferred_element_type=F32)
        up = jnp.dot(hb, wu_ref[...], preferred_element_type=F32)
        g_ref[...] = gate.astype(BF16)
        u_ref[...] = up.astype(BF16)
        a_ref[...] = (_silu(gate) * up).astype(BF16)

    spec = pl.BlockSpec((tm, tn), lambda i, j: (i, j))
    return pl.pallas_call(
        body, name=name, grid=(s // tm, D_FF // tn),
        in_specs=[pl.BlockSpec((tm, D), lambda i, j: (i, 0)), pl.BlockSpec((D, tn), lambda i, j: (0, j)),
                  pl.BlockSpec((D, tn), lambda i, j: (0, j))],
        out_specs=[spec, spec, spec], out_shape=[jax.ShapeDtypeStruct((s, D_FF), BF16)] * 3,
        compiler_params=_params(("parallel", "parallel")),
    )(h, wg, wu)


def _ffn_bwd_act(dy, wo, gate, up, name, tm=1024, tn=256, ties=()):
    s = dy.shape[0]
    tm = min(tm, s)

    def body(dy_ref, wo_ref, g_ref, u_ref, *rest):
        dg_ref, du_ref = rest[-2:]
        dact = lax.dot_general(dy_ref[...], wo_ref[...], _NT, preferred_element_type=F32)
        gate = g_ref[...].astype(F32)
        up = u_ref[...].astype(F32)
        sg = _sigmoid(gate)
        dg_ref[...] = (dact * up * (sg * (1.0 + gate * (1.0 - sg)))).astype(BF16)
        du_ref[...] = (dact * (gate * sg)).astype(BF16)

    spec = pl.BlockSpec((tm, tn), lambda i, j: (i, j))
    return pl.pallas_call(
        body, name=name, grid=(s // tm, D_FF // tn),
        in_specs=[pl.BlockSpec((tm, D), lambda i, j: (i, 0)), pl.BlockSpec((tn, D), lambda i, j: (j, 0)), spec, spec]
        + [_TIE_SPEC2] * len(ties),
        out_specs=[spec, spec], out_shape=[jax.ShapeDtypeStruct((s, D_FF), BF16)] * 2,
        compiler_params=_params(("parallel", "parallel")),
    )(dy, wo, gate, up, *ties)


def _shift_down(x, d):
    rows = lax.broadcasted_iota(jnp.int32, x.shape, 0)
    return jnp.where(rows >= d, pltpu.roll(x, d, 0), 0.0)


def _shift_up(x, d):
    n = x.shape[0]
    rows = lax.broadcasted_iota(jnp.int32, x.shape, 0)
    return jnp.where(rows < n - d, pltpu.roll(x, n - d, 0), 0.0)


def _conv_post(pre, is_qk):
    a = _silu(pre)
    l2 = a * lax.rsqrt(jnp.sum(a * a, axis=-1, keepdims=True) + EPS)
    return jnp.where(is_qk, l2, a)


def _conv_pre(x, w):
    pre = x * w[CONV_K - 1:CONV_K, :]
    for j in range(CONV_K - 1):
        pre = pre + _shift_down(x, CONV_K - 1 - j) * w[j:j + 1, :]
    return pre


def _gdn_conv_fwd(pm, conv_w, name):
    s = pm.shape[0]
    nblk = 3 * D // HEAD

    def body(x_ref, w_ref, o_ref):
        is_qk = pl.program_id(0) < 2 * HEADS
        o_ref[...] = _conv_post(_conv_pre(x_ref[...], w_ref[...]), is_qk)

    return pl.pallas_call(
        body, name=name, grid=(nblk,),
        in_specs=[pl.BlockSpec((s, HEAD), lambda c: (0, c)), pl.BlockSpec((CONV_K, HEAD), lambda c: (0, c))],
        out_specs=pl.BlockSpec((s, HEAD), lambda c: (0, c)),
        out_shape=jax.ShapeDtypeStruct((s, 3 * D), F32), compiler_params=_params(("parallel",)),
    )(pm, conv_w)


def _gdn_conv_bwd(pm, conv_w, dout, part, name):
    s = pm.shape[0]
    off = part * HEADS

    def body(x_ref, w_ref, d_ref, dx_ref, dw_ref):
        x, w = x_ref[...], w_ref[...]
        _, vjp = jax.vjp(lambda p: _conv_post(p, part < 2), _conv_pre(x, w))
        dpre, = vjp(d_ref[...])
        dx = dpre * w[CONV_K - 1:CONV_K, :]
        rows = [None] * CONV_K
        rows[CONV_K - 1] = jnp.sum(dpre * x, axis=0, keepdims=True)
        for j in range(CONV_K - 1):
            dx = dx + _shift_up(dpre, CONV_K - 1 - j) * w[j:j + 1, :]
            rows[j] = jnp.sum(dpre * _shift_down(x, CONV_K - 1 - j), axis=0, keepdims=True)
        dx_ref[...] = dx
        dw_ref[...] = jnp.concatenate(rows, axis=0)

    return pl.pallas_call(
        body, name=name, grid=(HEADS,),
        in_specs=[pl.BlockSpec((s, HEAD), lambda c: (0, c + off)), pl.BlockSpec((CONV_K, HEAD), lambda c: (0, c + off)),
                  pl.BlockSpec((s, HEAD), lambda c: (0, c))],
        out_specs=[pl.BlockSpec((s, HEAD), lambda c: (0, c)), pl.BlockSpec((CONV_K, HEAD), lambda c: (0, c))],
        out_shape=[jax.ShapeDtypeStruct((s, D), F32), jax.ShapeDtypeStruct((CONV_K, D), F32)],
        compiler_params=_params(("parallel",)),
    )(pm, conv_w, dout)


def _dot3(a, b, dims=_NN):
    ah, bh = a.astype(BF16), b.astype(BF16)
    al, bl = (a - ah.astype(F32)).astype(BF16), (b - bh.astype(F32)).astype(BF16)
    d = lambda u, v: lax.dot_general(u, v, dims, preferred_element_type=F32)
    return d(ah, bh) + (d(ah, bl) + d(al, bh))


def _make_dot(hi):
    def raw(a, b, dims):
        if hi:
            return _dot3(a, b, dims)
        return lax.dot_general(a.astype(BF16), b.astype(BF16), dims, preferred_element_type=F32)

    @functools.partial(jax.custom_vjp, nondiff_argnums=(2,))
    def dot(a, b, form):
        return raw(a, b, _DIMS[form])

    def fwd(a, b, form):
        return raw(a, b, _DIMS[form]), (a, b)

    def bwd(form, res, ct):
        a, b = res
        if form == "nn":
            return raw(ct, b, _NT), raw(a, ct, _TN)
        if form == "nt":
            return raw(ct, b, _NN), raw(ct, a, _TN)
        return raw(b, ct, _NT), raw(a, ct, _NN)

    dot.defvjp(fwd, bwd)
    return dot


_dot = _make_dot(False)
_dot_hi = _make_dot(True)


def _tri_inv_raw(low):
    n = low.shape[0]
    i = lax.broadcasted_iota(jnp.int32, (n, n), 0)
    j = lax.broadcasted_iota(jnp.int32, (n, n), 1)
    eye = (i == j).astype(F32)
    hdot = _dot3
    same16 = (i // 16) == (j // 16)
    neg = jnp.where(same16, -low, 0.0)
    inv = eye + neg
    power = neg
    for _ in range(3):
        power = hdot(power, power)
        inv = hdot(inv, eye + power)
    for blk in (32, 64):
        off = jnp.where(((i // blk) == (j // blk)) & ((i // (blk // 2)) != (j // (blk // 2))), low, 0.0)
        inv = inv - hdot(inv, hdot(off, inv))
    return inv


@jax.custom_vjp
def _tri_inv(low):
    return _tri_inv_raw(low)


def _tri_inv_fwd(low):
    inv = _tri_inv_raw(low)
    return inv, inv


def _tri_inv_bwd(inv, ct):
    return (-_dot3(_dot3(inv, ct, _TN), inv, _NT),)


_tri_inv.defvjp(_tri_inv_fwd, _tri_inv_bwd)


@jax.custom_vjp
def _tri_inv_given(low, inv):
    return inv


_tri_inv_given.defvjp(lambda low, inv: (inv, inv),
                      lambda inv, ct: (_tri_inv_bwd(inv, ct)[0], jnp.zeros_like(inv)))

GROUP = 4
N_GROUPS = HEADS // GROUP
GROWS = GROUP * CHUNK


def _gdn_group(q, k, v, beta, gc, gr, states, inv=None):
    n = q.shape[0]
    i = lax.broadcasted_iota(jnp.int32, (n, n), 0)
    j = lax.broadcasted_iota(jnp.int32, (n, n), 1)
    same = (i // CHUNK) == (j // CHUNK)
    incl, strict = same & (i >= j), same & (i > j)
    qs = q * (HEAD ** -0.5)
    decay = jnp.where(incl, jnp.exp(jnp.where(incl, gc - gr, 0.0)), 0.0)
    kb = k * beta
    eg = jnp.exp(gc)
    prod = _dot(jnp.concatenate([kb, qs], axis=0), k, "nt")
    low = jnp.where(strict, prod[:n] * decay, 0.0)
    attn = jnp.where(incl, prod[n:] * decay, 0.0)
    inv = _tri_inv(low) if inv is None else _tri_inv_given(low, inv)
    sol = _dot_hi(inv, jnp.concatenate([v * beta, kb * eg], axis=1), "nn")
    u, w, qg = sol[:, :HEAD], sol[:, HEAD:], qs * eg
    last = lax.broadcasted_iota(jnp.int32, (CHUNK, 1), 0) == CHUNK - 1
    v_new, o_state, carry = [], [], []
    for h, state in enumerate(states):
        rows = slice(h * CHUNK, (h + 1) * CHUNK)
        ws = _dot(jnp.concatenate([w[rows], qg[rows]], axis=0), state, "nn")
        v_new.append(u[rows] - ws[:CHUNK])
        o_state.append(ws[CHUNK:])
        g_last = jnp.sum(jnp.where(last, gc[rows], 0.0), axis=0, keepdims=True)
        carry.append((g_last, k[rows] * jnp.exp(g_last - gc[rows])))
    o = jnp.concatenate(o_state, axis=0) + _dot(attn, jnp.concatenate(v_new, axis=0), "nn")
    new = tuple(state * jnp.exp(g_last) + _dot(k_dec, vn, "tn")
                for state, (g_last, k_dec), vn in zip(states, carry, v_new))
    return o, new, inv


def _gdn_specs(s, rev):
    nc = s // CHUNK
    at = (lambda n: nc - 1 - n) if rev else (lambda n: n)
    return nc, at, [
        pl.BlockSpec((CHUNK, D), lambda n: (at(n), 0)), pl.BlockSpec((CHUNK, D), lambda n: (at(n), 1)),
        pl.BlockSpec((CHUNK, D), lambda n: (at(n), 2)), pl.BlockSpec((CHUNK, HEAD), lambda n: (at(n), 0)),
        pl.BlockSpec((CHUNK, HEAD), lambda n: (at(n), 0)),
        pl.BlockSpec((None, N_GROUPS, 1, GROWS), lambda n: (at(n), 0, 0, 0))]


def _group_operands(grp, q_ref, k_ref, v_ref, b_blk, gc_blk, gr_blk):
    heads = range(grp * GROUP, (grp + 1) * GROUP)
    stack = lambda ref: jnp.concatenate([ref[:, h * HEAD:(h + 1) * HEAD] for h in heads], axis=0)
    col = lambda blk: jnp.concatenate([blk[:, h:h + 1] for h in heads], axis=0)
    return stack(q_ref), stack(k_ref), stack(v_ref), col(b_blk), col(gc_blk), gr_blk[grp]


def _gdn_scan_fwd(qkv, beta, gcum, grow, name):
    s = qkv.shape[0]
    nc, _, in_specs = _gdn_specs(s, rev=False)

    def body(q_ref, k_ref, v_ref, b_ref, gc_ref, gr_ref, o_ref, st_ref, inv_ref, state):
        @pl.when(pl.program_id(0) == 0)
        def _():
            state[...] = jnp.zeros_like(state)

        b_blk, gc_blk, gr_blk = b_ref[...], gc_ref[...], gr_ref[...]
        old = [state[h] for h in range(HEADS)]
        res = [_gdn_group(*_group_operands(grp, q_ref, k_ref, v_ref, b_blk, gc_blk, gr_blk),
                          old[grp * GROUP:(grp + 1) * GROUP]) for grp in range(N_GROUPS)]
        for grp, (o, new, inv) in enumerate(res):
            inv_ref[grp] = inv
            for hh in range(GROUP):
                h = grp * GROUP + hh
                st_ref[h] = old[h]
                o_ref[:, h * HEAD:(h + 1) * HEAD] = o[hh * CHUNK:(hh + 1) * CHUNK]
                state[h] = new[hh]

    return pl.pallas_call(
        body, name=name, grid=(nc,), in_specs=in_specs,
        out_specs=[pl.BlockSpec((CHUNK, D), lambda n: (n, 0)),
                   pl.BlockSpec((None, HEADS, HEAD, HEAD), lambda n: (n, 0, 0, 0)),
                   pl.BlockSpec((None, N_GROUPS, GROWS, GROWS), lambda n: (n, 0, 0, 0))],
        out_shape=[jax.ShapeDtypeStruct((s, D), F32), jax.ShapeDtypeStruct((nc, HEADS, HEAD, HEAD), F32),
                   jax.ShapeDtypeStruct((nc, N_GROUPS, GROWS, GROWS), F32)],
        scratch_shapes=[pltpu.VMEM((HEADS, HEAD, HEAD), F32)],
        compiler_params=_params(("arbitrary",)),
    )(qkv, qkv, qkv, beta, gcum, grow)


def _gdn_scan_bwd(qkv, beta, gcum, grow, states, invs, do, name):
    s = qkv.shape[0]
    nc, at, in_specs = _gdn_specs(s, rev=True)
    in_specs += [pl.BlockSpec((None, HEADS, HEAD, HEAD), lambda n: (at(n), 0, 0, 0)),
                 pl.BlockSpec((None, N_GROUPS, GROWS, GROWS), lambda n: (at(n), 0, 0, 0)),
                 pl.BlockSpec((CHUNK, D), lambda n: (at(n), 0))]

    def body(q_ref, k_ref, v_ref, b_ref, gc_ref, gr_ref, st_ref, inv_ref, do_ref, dq_ref, dk_ref, dv_ref, db_ref,
             dgc_ref, dgr_ref, dstate):
        @pl.when(pl.program_id(0) == 0)
        def _():
            dstate[...] = jnp.zeros_like(dstate)

        b_blk, gc_blk, gr_blk = b_ref[...], gc_ref[...], gr_ref[...]
        dold = [dstate[h] for h in range(HEADS)]
        res = []
        for grp in range(N_GROUPS):
            heads = range(grp * GROUP, (grp + 1) * GROUP)
            inv = inv_ref[grp]
            _, vjp = jax.vjp(lambda q, k, v, b, gc, gr, *st, inv=inv: _gdn_group(q, k, v, b, gc, gr, st, inv)[:2],
                             *_group_operands(grp, q_ref, k_ref, v_ref, b_blk, gc_blk, gr_blk),
                             *[st_ref[h] for h in heads])
            d_out = jnp.concatenate([do_ref[:, h * HEAD:(h + 1) * HEAD] for h in heads], axis=0)
            res.append(vjp((d_out, tuple(dold[h] for h in heads))))
        lane = lax.broadcasted_iota(jnp.int32, (CHUNK, HEAD), 1)
        db_all = jnp.zeros((CHUNK, HEAD), F32)
        dgc_all = jnp.zeros((CHUNK, HEAD), F32)
        for grp, (dq, dk, dv, db, dgc, dgr, *dst) in enumerate(res):
            dgr_ref[grp] = dgr
            for hh in range(GROUP):
                h = grp * GROUP + hh
                cs, rows = slice(h * HEAD, (h + 1) * HEAD), slice(hh * CHUNK, (hh + 1) * CHUNK)
                dq_ref[:, cs] = dq[rows]
                dk_ref[:, cs] = dk[rows]
                dv_ref[:, cs] = dv[rows]
                dstate[h] = dst[hh]
                db_all = jnp.where(lane == h, db[rows], db_all)
                dgc_all = jnp.where(lane == h, dgc[rows], dgc_all)
        db_ref[...] = db_all
        dgc_ref[...] = dgc_all

    blk = pl.BlockSpec((CHUNK, D), lambda n: (at(n), 0))
    gblk = pl.BlockSpec((CHUNK, HEAD), lambda n: (at(n), 0))
    return pl.pallas_call(
        body, name=name, grid=(nc,), in_specs=in_specs,
        out_specs=[blk, blk, blk, gblk, gblk, pl.BlockSpec((None, N_GROUPS, 1, GROWS), lambda n: (at(n), 0, 0, 0))],
        out_shape=[jax.ShapeDtypeStruct((s, D), F32)] * 3 + [jax.ShapeDtypeStruct((s, HEAD), F32)] * 2
        + [jax.ShapeDtypeStruct((nc, N_GROUPS, 1, GROWS), F32)],
        scratch_shapes=[pltpu.VMEM((HEADS, HEAD, HEAD), F32)],
        compiler_params=_params(("arbitrary",)),
    )(qkv, qkv, qkv, beta, gcum, grow, states, invs, do)


ATT_TILE = 512
ATT_SCALE = QK_HEAD ** -0.5


def _att_mask(t):
    qpos = lax.broadcasted_iota(jnp.int32, (t, t), 0)
    kpos = lax.broadcasted_iota(jnp.int32, (t, t), 1)
    return (kpos // CHUNK) <= (qpos // CHUNK)


def _att_pairs(nb, by_query):
    if by_query:
        pairs = [(i, j) for i in range(nb) for j in range(i + 1)]
    else:
        pairs = [(j, i) for j in range(nb) for i in range(j, nb)]
    return jnp.array([a for a, _ in pairs], jnp.int32), jnp.array([b for _, b in pairs], jnp.int32)


def _attn_fwd(q, k, v, name):
    s = q.shape[0]
    t = min(ATT_TILE, s)
    nb = s // t
    ii, jj = _att_pairs(nb, by_query=True)

    def body(ii_ref, jj_ref, q_ref, k_ref, v_ref, o_ref, lse_ref, m_s, l_s, acc):
        step = pl.program_id(1)
        i, j = ii_ref[step], jj_ref[step]

        @pl.when(j == 0)
        def _():
            m_s[...] = jnp.full_like(m_s, -jnp.inf)
            l_s[...] = jnp.zeros_like(l_s)
            acc[...] = jnp.zeros_like(acc)

        sc = lax.dot_general(q_ref[...], k_ref[...], _NT, preferred_element_type=F32) * ATT_SCALE
        sc = lax.cond(i == j, lambda u: jnp.where(_att_mask(t), u, -jnp.inf), lambda u: u, sc)
        m_new = jnp.maximum(m_s[...], jnp.max(sc, axis=-1, keepdims=True))
        alpha = jnp.exp(m_s[...] - m_new)
        p = jnp.exp(sc - m_new)
        l_s[...] = alpha * l_s[...] + jnp.sum(p, axis=-1, keepdims=True)
        acc[...] = alpha * acc[...] + jnp.dot(p.astype(BF16), v_ref[...], preferred_element_type=F32)
        m_s[...] = m_new

        @pl.when(j == i)
        def _():
            o_ref[...] = acc[...] / l_s[...]
            lse_ref[...] = m_s[...] + jnp.log(l_s[...])

    grid_spec = pltpu.PrefetchScalarGridSpec(
        num_scalar_prefetch=2, grid=(HEADS, len(ii)),
        in_specs=[pl.BlockSpec((t, HEAD_PAD), lambda h, n, ir, jr: (ir[n], h)),
                  pl.BlockSpec((t, HEAD_PAD), lambda h, n, ir, jr: (jr[n], h)),
                  pl.BlockSpec((t, HEAD), lambda h, n, ir, jr: (jr[n], h))],
        out_specs=[pl.BlockSpec((t, HEAD), lambda h, n, ir, jr: (ir[n], h)),
                   pl.BlockSpec((None, t, 1), lambda h, n, ir, jr: (h, ir[n], 0))],
        scratch_shapes=[pltpu.VMEM((t, 1), F32), pltpu.VMEM((t, 1), F32), pltpu.VMEM((t, HEAD), F32)])
    return pl.pallas_call(
        body, name=name, grid_spec=grid_spec,
        out_shape=[jax.ShapeDtypeStruct((s, HEADS * HEAD), F32), jax.ShapeDtypeStruct((HEADS, s, 1), F32)],
        compiler_params=_params(("parallel", "arbitrary")),
    )(ii, jj, q, k, v)


def _attn_bwd(q, k, v, do, o, lse, name):
    s = q.shape[0]
    t = min(ATT_TILE, s)
    nb = s // t
    jj, ii = _att_pairs(nb, by_query=False)

    def body(jj_ref, ii_ref, q_ref, k_ref, v_ref, do_ref, o_ref, lse_ref, dq_ref, dk_ref, dv_ref, dk_acc, dv_acc):
        step = pl.program_id(1)
        i, j = ii_ref[step], jj_ref[step]

        @pl.when(step == 0)
        def _():
            dq_ref[...] = jnp.zeros_like(dq_ref)

        @pl.when(i == j)
        def _():
            dk_acc[...] = jnp.zeros_like(dk_acc)
            dv_acc[...] = jnp.zeros_like(dv_acc)

        sc = lax.dot_general(q_ref[...], k_ref[...], _NT, preferred_element_type=F32) * ATT_SCALE
        p = jnp.exp(sc - lse_ref[...])
        p = lax.cond(i == j, lambda u: jnp.where(_att_mask(t), u, 0.0), lambda u: u, p)
        do_f = do_ref[...]
        dob = do_f.astype(BF16)
        delta = jnp.sum(do_f * o_ref[...], axis=-1, keepdims=True)
        dv_acc[...] += lax.dot_general(p.astype(BF16), dob, _TN, preferred_element_type=F32)
        dp = lax.dot_general(dob, v_ref[...], _NT, preferred_element_type=F32)
        ds = (p * (dp - delta) * ATT_SCALE).astype(BF16)
        dk_acc[...] += lax.dot_general(ds, q_ref[...], _TN, preferred_element_type=F32)
        rows = pl.ds(pl.multiple_of(i * t, t), t)
        dq_ref[rows, :] += jnp.dot(ds, k_ref[...], preferred_element_type=F32)

        @pl.when(i == nb - 1)
        def _():
            dk_ref[...] = dk_acc[...]
            dv_ref[...] = dv_acc[...]

    grid_spec = pltpu.PrefetchScalarGridSpec(
        num_scalar_prefetch=2, grid=(HEADS, len(jj)),
        in_specs=[pl.BlockSpec((t, HEAD_PAD), lambda h, n, jr, ir: (ir[n], h)),
                  pl.BlockSpec((t, HEAD_PAD), lambda h, n, jr, ir: (jr[n], h)),
                  pl.BlockSpec((t, HEAD), lambda h, n, jr, ir: (jr[n], h)),
                  pl.BlockSpec((t, HEAD), lambda h, n, jr, ir: (ir[n], h)),
                  pl.BlockSpec((t, HEAD), lambda h, n, jr, ir: (ir[n], h)),
                  pl.BlockSpec((None, t, 1), lambda h, n, jr, ir: (h, ir[n], 0))],
        out_specs=[pl.BlockSpec((s, HEAD_PAD), lambda h, n, jr, ir: (0, h)),
                   pl.BlockSpec((t, HEAD_PAD), lambda h, n, jr, ir: (jr[n], h)),
                   pl.BlockSpec((t, HEAD), lambda h, n, jr, ir: (jr[n], h))],
        scratch_shapes=[pltpu.VMEM((t, HEAD_PAD), F32), pltpu.VMEM((t, HEAD), F32)])
    return pl.pallas_call(
        body, name=name, grid_spec=grid_spec,
        out_shape=[jax.ShapeDtypeStruct((s, HEADS * HEAD_PAD), F32)] * 2 + [jax.ShapeDtypeStruct((s, HEADS * HEAD), F32)],
        compiler_params=_params(("parallel", "arbitrary")),
    )(jj, ii, q, k, v, do, o, lse)


def _rope_tables(positions):
    half = ROPE // 2
    inv_freq = ROPE_BASE ** (-jnp.arange(half, dtype=F32) / half)
    ang = positions.astype(F32)[:, None] * inv_freq
    cos, sin = jnp.cos(ang), jnp.sin(ang)
    return jnp.concatenate([cos] * 4, axis=1), jnp.concatenate([-sin, sin] * 2, axis=1)


def _loss_and_grad(y, target, name):
    s = y.shape[0]
    ts = min(ROW_TILE, s)

    def body(y_ref, t_ref, dy_ref, l_ref):
        e = y_ref[...] - t_ref[...]
        dy_ref[...] = e * (1.0 / D)
        part = jnp.sum(jnp.sum(e * e, axis=-1, keepdims=True) * (0.5 / D), axis=0, keepdims=True)
        part = part * jnp.ones((1, 128), F32)

        @pl.when(pl.program_id(0) == 0)
        def _():
            l_ref[...] = part

        @pl.when(pl.program_id(0) > 0)
        def _():
            l_ref[...] += part

    return pl.pallas_call(
        body, name=name, grid=(s // ts,),
        in_specs=[pl.BlockSpec((ts, D), lambda i: (i, 0))] * 2,
        out_specs=[pl.BlockSpec((ts, D), lambda i: (i, 0)), pl.BlockSpec((1, 128), lambda i: (0, 0))],
        out_shape=[jax.ShapeDtypeStruct((s, D), F32), jax.ShapeDtypeStruct((1, 128), F32)],
        compiler_params=_params(("arbitrary",)),
    )(y, target)


ANY = pl.BlockSpec(memory_space=pl.ANY)


def _all_gather(shard, name):
    def body(x_ref, out_ref, send_sems, recv_sems, local_sem):
        x, y, c = lax.axis_index("x"), lax.axis_index("y"), lax.axis_index("c")
        me, sibling = (x, y, c), (x, y, 1 - c)
        chips = [(1 - x, y), (x, 1 - y), (1 - x, 1 - y)]

        def rows(px, py, pc):
            return out_ref.at[4 * px + 2 * py + pc]

        def copy(k, block, to, src=None):
            return pltpu.make_async_remote_copy(
                src_ref=rows(*block) if src is None else src, dst_ref=rows(*block),
                send_sem=send_sems.at[k], recv_sem=recv_sems.at[k], device_id=to, device_id_type=MESH)

        mine = pltpu.make_async_copy(x_ref, rows(*me), local_sem)
        mine.start()
        first = [copy(0, me, sibling, src=x_ref)]
        first += [copy(1 + j, me, (*chip, c), src=x_ref) for j, chip in enumerate(chips)]
        for cp in first:
            cp.start()
        passed = [copy(4 + j, (*chip, c), sibling) for j, chip in enumerate(chips)]
        for j, chip in enumerate(chips):
            copy(1 + j, (*chip, c), me).wait_recv()
            passed[j].start()
        copy(0, sibling, me).wait_recv()
        for j, chip in enumerate(chips):
            copy(4 + j, (*chip, 1 - c), me).wait_recv()
        for cp in first + passed:
            cp.wait_send()
        mine.wait()

    return pl.pallas_call(
        body, name=name, out_shape=jax.ShapeDtypeStruct((N_DEV,) + shard.shape, shard.dtype),
        in_specs=[ANY], out_specs=ANY,
        scratch_shapes=[pltpu.SemaphoreType.DMA((7,)), pltpu.SemaphoreType.DMA((7,)), pltpu.SemaphoreType.DMA],
    )(shard)


def _exchange(blocks, name):
    def body(x_ref, out_ref, send_sems, recv_sems, local_sem):
        x, y, c = lax.axis_index("x"), lax.axis_index("y"), lax.axis_index("c")
        me = 4 * x + 2 * y + c
        mine = pltpu.make_async_copy(x_ref.at[me], out_ref.at[me], local_sem)
        mine.start()
        copies = []
        for k in range(1, N_DEV):
            px = 1 - x if k & 4 else x
            py = 1 - y if k & 2 else y
            pc = 1 - c if k & 1 else c
            peer = 4 * px + 2 * py + pc
            cp = pltpu.make_async_remote_copy(
                src_ref=x_ref.at[peer], dst_ref=out_ref.at[me], send_sem=send_sems.at[k - 1],
                recv_sem=recv_sems.at[k - 1], device_id=(px, py, pc), device_id_type=MESH)
            cp.start()
            copies.append((cp, pltpu.make_async_remote_copy(
                src_ref=x_ref.at[peer], dst_ref=out_ref.at[peer], send_sem=send_sems.at[k - 1],
                recv_sem=recv_sems.at[k - 1], device_id=(px, py, pc), device_id_type=MESH)))
        for cp, landing in copies:
            landing.wait_recv()
        for cp, landing in copies:
            cp.wait_send()
        mine.wait()

    return pl.pallas_call(
        body, name=name, out_shape=jax.ShapeDtypeStruct(blocks.shape, blocks.dtype),
        in_specs=[ANY], out_specs=ANY,
        scratch_shapes=[pltpu.SemaphoreType.DMA((7,)), pltpu.SemaphoreType.DMA((7,)), pltpu.SemaphoreType.DMA],
    )(blocks)


HBM = pl.BlockSpec(memory_space=pltpu.HBM)
SEM = pl.BlockSpec(memory_space=pltpu.SEMAPHORE)
EFFECT = pltpu.SideEffectType.DATAFLOW_SIDE_EFFECTING


def _peers():
    x, y, c = lax.axis_index("x"), lax.axis_index("y"), lax.axis_index("c")
    peers = []
    for k in range(1, N_DEV):
        px = 1 - x if k & 4 else x
        py = 1 - y if k & 2 else y
        pc = 1 - c if k & 1 else c
        peers.append(((px, py, pc), 4 * px + 2 * py + pc))
    return 4 * x + 2 * y + c, peers


def _send_start(srcs, name, gather):
    n = len(srcs)
    lands = [((N_DEV,) + s.shape) if gather else s.shape for s in srcs]

    def body(*refs):
        src_refs, land_refs = refs[:n], refs[n:2 * n]
        send_sems, recv_sems, token = refs[2 * n], refs[2 * n + 1], refs[-1]
        me, peers = _peers()
        for i in range(n):
            for k, (dev, idx) in enumerate(peers):
                pltpu.make_async_remote_copy(
                    src_ref=src_refs[i] if gather else src_refs[i].at[idx], dst_ref=land_refs[i].at[me],
                    send_sem=send_sems.at[7 * i + k], recv_sem=recv_sems.at[7 * i + k], device_id=dev,
                    device_id_type=MESH).start()
        token[...] = jnp.zeros_like(token)

    res = pl.pallas_call(
        body, name=name,
        out_shape=(pltpu.SemaphoreType.DMA((7 * n,)), pltpu.SemaphoreType.DMA((7 * n,)),
                   *[pltpu.HBM(s.shape, s.dtype) for s in srcs],
                   *[pltpu.HBM(shape, s.dtype) for shape, s in zip(lands, srcs)],
                   jax.ShapeDtypeStruct((8, 128), F32)),
        in_specs=(HBM,) * (2 * n), out_specs=(SEM, SEM) + (HBM,) * (2 * n) + (pl.BlockSpec(memory_space=pltpu.VMEM),),
        input_output_aliases={i: 2 + i for i in range(2 * n)},
        compiler_params=pltpu.CompilerParams(has_side_effects=EFFECT),
    )(*[pltpu.with_memory_space_constraint(s, pltpu.HBM) for s in srcs],
      *[pltpu.with_memory_space_constraint(lax.empty(shape, s.dtype), pltpu.HBM) for shape, s in zip(lands, srcs)])
    return dict(sems=res[:2], srcs=res[2:2 + n], lands=res[2 + n:2 + 2 * n], token=res[-1])


def _send_wait(handle, after, name, gather):
    n = len(handle["srcs"])

    def body(*refs):
        src_refs, land_refs = refs[:n], refs[n:2 * n]
        send_sems, recv_sems = refs[2 * n], refs[2 * n + 1]
        me, peers = _peers()
        for i in range(n):
            for k, (dev, idx) in enumerate(peers):
                cp = pltpu.make_async_remote_copy(
                    src_ref=src_refs[i] if gather else src_refs[i].at[idx], dst_ref=land_refs[i].at[idx],
                    send_sem=send_sems.at[7 * i + k], recv_sem=recv_sems.at[7 * i + k], device_id=dev,
                    device_id_type=MESH)
                cp.wait_send()
                cp.wait_recv()

    both = list(handle["srcs"]) + list(handle["lands"])
    res = pl.pallas_call(
        body, name=name, out_shape=tuple(pltpu.HBM(t.shape, t.dtype) for t in both),
        in_specs=(HBM,) * (2 * n) + (SEM, SEM, pl.BlockSpec(memory_space=pl.ANY)), out_specs=(HBM,) * (2 * n),
        input_output_aliases={i: i for i in range(2 * n)},
        compiler_params=pltpu.CompilerParams(has_side_effects=EFFECT),
    )(*both, *handle["sems"], after)
    return res[:n], res[n:]


def _adamw(parts, w, m, v, name, tr=128):
    pieces = len(parts)
    n, r, wd = parts[0].shape
    tr = next((t for t in (tr, 64, 32, 16) if r % t == 0), r)
    nrt = r // tr

    def body(*refs):
        w_ref, m_ref, v_ref, g_ref, d_ref, nm_ref, nv_ref = refs[pieces:]

        def update(p_ref):
            g = p_ref[0].astype(F32)
            for k in range(1, n):
                g = g + p_ref[k].astype(F32)
            m_new = B1 * m_ref[...] + (1.0 - B1) * g
            v_new = B2 * v_ref[...] + (1.0 - B2) * (g * g)
            m_hat = m_new / (1.0 - B1 ** STEP)
            v_hat = v_new / (1.0 - B2 ** STEP)
            g_ref[...] = g
            d_ref[...] = -LR * (m_hat / (jnp.sqrt(v_hat) + ADAM_EPS) + WD * w_ref[...])
            nm_ref[...] = m_new
            nv_ref[...] = v_new

        for p in range(pieces):
            pl.when(pl.program_id(0) == p)(functools.partial(update, refs[p]))

    part_spec = lambda p: pl.BlockSpec((n, tr, wd), lambda l, i: (0, jnp.clip(i + (l - p) * nrt, 0, nrt - 1), 0))
    blk = pl.BlockSpec((tr, wd), lambda l, i: (l * nrt + i, 0))
    return pl.pallas_call(
        body, name=name, grid=(pieces, nrt),
        in_specs=[part_spec(p) for p in range(pieces)] + [blk, blk, blk],
        out_specs=[blk] * 4, out_shape=[jax.ShapeDtypeStruct((pieces * r, wd), F32)] * 4,
        compiler_params=_params(("arbitrary", "arbitrary")),
    )(*parts, w, m, v)


def _outer8(ct, dm, name):
    k, n = ct.shape[0], dm.shape[1]

    def body(c_ref, d_ref, o_ref):
        cv, dv = c_ref[...], d_ref[...]
        acc = cv[:, 0:1] * dv[0:1, :]
        for s in range(1, N_DEV):
            acc = acc + cv[:, s:s + 1] * dv[s:s + 1, :]
        o_ref[...] = acc

    tk = 256
    return pl.pallas_call(
        body, name=name, grid=(k // tk,),
        in_specs=[pl.BlockSpec((tk, N_DEV), lambda i: (i, 0)), pl.BlockSpec((N_DEV, n), lambda i: (0, 0))],
        out_specs=pl.BlockSpec((tk, n), lambda i: (i, 0)), out_shape=jax.ShapeDtypeStruct((k, n), F32),
        compiler_params=_params(("parallel",)),
    )(ct, dm)


FULL = (0, D)
C128 = (0, 128)
HEAD_NOPE = [(h * HEAD_PAD, NOPE) for h in range(HEADS)]
HEAD_ROPE = [(h * HEAD_PAD + NOPE, 128) for h in range(HEADS)]
HEAD_ALL = [(h * HEAD_PAD, HEAD_PAD) for h in range(HEADS)]
HEAD_V = [(h * HEAD, HEAD) for h in range(HEADS)]


def _modulate(x, p, ties=()):
    return _rowwise_fwd(_modulate_fn, [(x, FULL)], [p["gain"], p["scale"], p["shift"]], [(D, BF16, FULL)], "modulate",
                        ties=ties)[0]


def _residual_bwd(y, gm, dxn):
    return _rowwise_bwd(_gate_only_fn, [(y, FULL)], [gm], [(D, F32, FULL)], [dxn], [(0, FULL)], [(D, BF16)],
                        "residual_bwd")


def _modulate_bwd(x, p, dh, dx_in, prev=None):
    pars = [p["gain"], p["scale"], p["shift"]]
    if prev is None:
        return list(_rowwise_bwd(_modulate_fn, [(x, FULL)], pars, [(D, BF16, FULL)], [dh], [(0, FULL)], [(D, F32)],
                                 "modulate_bwd", add={0: dx_in})) + [None]
    s = x.shape[0]
    ts = min(ROW_TILE, s)

    def body(x_ref, g_ref, sc_ref, sh_ref, dh_ref, din_ref, y_ref, gm_ref, dx_ref, dy_ref, dg_ref, dsc_ref, dsh_ref,
             dgm_ref):
        _, vjp = jax.vjp(lambda *t: _modulate_fn(0, *t)[0], x_ref[...], g_ref[...], sc_ref[...], sh_ref[...])
        dxm, dg, dsc, dsh = vjp(dh_ref[...])
        dx = dxm + din_ref[...]
        dx_ref[...] = dx
        dy_ref[...] = (gm_ref[...] * dx).astype(BF16)
        sums = (dg, dsc, dsh, jnp.sum(dx * y_ref[...], axis=0, keepdims=True))
        first = pl.program_id(0) == 0
        for ref, val in zip((dg_ref, dsc_ref, dsh_ref, dgm_ref), sums):
            @pl.when(first)
            def _(ref=ref, val=val):
                ref[...] = val

            @pl.when(jnp.logical_not(first))
            def _(ref=ref, val=val):
                ref[...] += val

    blk = pl.BlockSpec((ts, D), lambda i: (i, 0))
    vec = pl.BlockSpec((1, D), lambda i: (0, 0))
    dx, dy, dg, dsc, dsh, dgm = pl.pallas_call(
        body, name="modulate_bwd_chain", grid=(s // ts,),
        in_specs=[blk, vec, vec, vec, blk, blk, blk, vec], out_specs=[blk, blk, vec, vec, vec, vec],
        out_shape=[jax.ShapeDtypeStruct((s, D), F32), jax.ShapeDtypeStruct((s, D), BF16)]
        + [jax.ShapeDtypeStruct((1, D), F32)] * 4,
        compiler_params=_params(("arbitrary",)),
    )(x, *pars, dh, dx_in, prev[0], prev[1])
    return [dx, dg, dsc, dsh, (dy, dgm)]


def _ffn_fwd(x, p, ties=()):
    h = _modulate(x, p, ties)
    gate, up, act = _ffn_in(h, p["wg"], p["wu"], "ffn_in")
    y, xn = _matmul([(act, p["wo"])], "nn", "ffn_out", resid=(x, p["gm"]))
    return xn, dict(x=x, h=h, gate=gate, up=up, act=act, y=y)


def _ffn_bwd(t, p, dxn, res=None, prev=None, ties=()):
    dy, dgm = res or _residual_bwd(t["y"], p["gm"], dxn)
    dgate, dup = _ffn_bwd_act(dy, p["wo"], t["gate"], t["up"], "ffn_bwd_act", ties=ties)
    dwo = _mm(t["act"], dy, "tn", "ffn_dwo", ties=ties)
    dh = _matmul([(dgate, p["wg"]), (dup, p["wu"])], "nt", "ffn_dh")
    dwg = _mm(t["h"], dgate, "tn", "ffn_dwi")
    dwu = _mm(t["h"], dup, "tn", "ffn_dwi")
    dx, dgain, dscale, dshift, res_prev = _modulate_bwd(t["x"], p, dh, dxn, prev)
    return dx, dict(gain=dgain, scale=dscale, shift=dshift, gm=dgm, wg=dwg, wu=dwu, wo=dwo), res_prev


def _pad128(t):
    return jnp.pad(t, ((0, 0), (0, 128 - t.shape[1])))


def _gdn_fwd(x, p, ties=()):
    s = x.shape[0]
    h = _modulate(x, p, ties)
    pm = _mm(h, p["w_main"], "nn", "gdn_proj")
    tail = _mm(h, p["w_tail"], "nn", "gdn_proj_tail")
    qkv = _gdn_conv_fwd(pm, p["conv_w"], "gdn_conv")
    beta, gcum = _rowwise_fwd(_gdn_gates_fn, [(tail, C128), (tail, (128, 128))], [p["a_log"], p["dt_bias"]],
                              [(128, F32, C128)] * 2, "gdn_gates")
    grow = gcum[:, :HEADS].reshape(s // CHUNK, CHUNK, N_GROUPS, GROUP).transpose(0, 2, 3, 1)
    grow = grow.reshape(s // CHUNK, N_GROUPS, 1, GROWS)
    o, states, invs = _gdn_scan_fwd(qkv, beta, gcum, grow, "gdn_scan")
    on, = _rowwise_fwd(_gdn_outnorm_fn, [(o, HEAD_V), (pm, [(3 * D + h_ * HEAD, HEAD) for h_ in range(HEADS)])],
                       [p["norm_g"]], [(D, BF16, HEAD_V)], "gdn_outnorm", groups=HEADS)
    y, xn = _matmul([(on, p["w_out"])], "nn", "mix_out", resid=(x, p["gm"]))
    t = dict(x=x, h=h, pm=pm, tail=tail, qkv=qkv, beta=beta, gcum=gcum, grow=grow, o=o, states=states, invs=invs,
             on=on, y=y)
    return xn, t


def _gdn_bwd(t, p, dxn, res=None, prev=None, ties=()):
    s = dxn.shape[0]
    zc = [(3 * D + h_ * HEAD, HEAD) for h_ in range(HEADS)]
    dy, dgm = res or _residual_bwd(t["y"], p["gm"], dxn)
    dw_out = _mm(t["on"], dy, "tn", "mix_dwo", ties=ties)
    don = _mm(dy, p["w_out"], "nt", "mix_dout", ties=ties)
    do, dz, dnorm_g = _rowwise_bwd(_gdn_outnorm_fn, [(t["o"], HEAD_V), (t["pm"], zc)], [p["norm_g"]],
                                   [(D, BF16, HEAD_V)], [don], [(0, HEAD_V), (1, HEAD_V)], [(D, F32), (D, F32)],
                                   "gdn_outnorm_bwd", groups=HEADS)
    dq, dk, dv, dbeta, dg, dgr = _gdn_scan_bwd(t["qkv"], t["beta"], t["gcum"], t["grow"], t["states"], t["invs"], do,
                                               "gdn_scan_bwd")
    dg = dg + _pad128(dgr.reshape(s // CHUNK, N_GROUPS, GROUP, CHUNK).transpose(0, 3, 1, 2).reshape(s, HEADS))
    dtail, da_log, ddt = _rowwise_bwd(_gdn_gates_fn, [(t["tail"], C128), (t["tail"], (128, 128))],
                                      [p["a_log"], p["dt_bias"]], [(128, F32, C128)] * 2, [dbeta, dg],
                                      [(0, C128), (0, (128, 128))], [(256, F32)], "gdn_gates_bwd")
    dxs, dcw = [], []
    for part, d in enumerate((dq, dk, dv)):
        dx_, dw_ = _gdn_conv_bwd(t["pm"], p["conv_w"], d, part, "gdn_conv_bwd")
        dxs.append(dx_)
        dcw.append(dw_)
    pieces = dxs + [dz]
    dh = _matmul([(d, p["w_main"]) for d in pieces] + [(dtail, p["w_tail"])], "nt", "gdn_dh",
                 boffs=[0, D, 2 * D, 3 * D, 0], tk=512)
    dw_main = [_mm(t["h"], d, "tn", "gdn_dwi") for d in pieces]
    dw_tail = _mm(t["h"], dtail, "tn", "gdn_dwi_tail")
    dx, dgain, dscale, dshift, res_prev = _modulate_bwd(t["x"], p, dh, dxn, prev)
    return dx, dict(gain=dgain, scale=dscale, shift=dshift, gm=dgm, w_main=jnp.concatenate(dw_main, axis=1),
                    w_tail=dw_tail, conv_w=jnp.concatenate(dcw, axis=1), a_log=da_log, dt_bias=ddt,
                    norm_g=dnorm_g, w_out=dw_out), res_prev


def _q_rows(q2, cosf, sins):
    return [(q2, HEAD_NOPE), (q2, HEAD_ROPE), (cosf, C128), (sins, C128)]


def _mla_fwd(x, p, kv, ties=()):
    h = _modulate(x, p, ties)
    cq = _mm(h, p["w_dq"], "nn", "mla_dq")
    cqn, = _rowwise_fwd(_rms_fn, [(cq, (0, Q_LORA))], [p["q_lora_g"]], [(Q_LORA, BF16, (0, Q_LORA))], "mla_qlora_norm")
    q2 = _mm(cqn, p["w_uq"], "nn", "mla_uq")
    qn, = _rowwise_fwd(_q_norm_rope_fn, _q_rows(q2, kv["cosf"], kv["sins"]), [p["q_gn"], p["q_gr"]],
                       [(HEADS * HEAD_PAD, BF16, HEAD_ALL)], "mla_q_norm", groups=HEADS)
    o, lse = _attn_fwd(qn, kv["kn"], kv["vb"], "mla_attn")
    y, xn = _matmul([(o, p["w_out"])], "nn", "mix_out", resid=(x, p["gm"]))
    return xn, dict(x=x, h=h, cq=cq, cqn=cqn, q2=q2, qn=qn, o=o, lse=lse, y=y)


def _mla_bwd(t, p, kv, dxn, res=None, prev=None, ties=()):
    dy, dgm = res or _residual_bwd(t["y"], p["gm"], dxn)
    dw_out = _mm(t["o"], dy, "tn", "mix_dwo", ties=ties)
    do = _mm(dy, p["w_out"], "nt", "mix_dout", ties=ties)
    dq, dk, dv = _attn_bwd(t["qn"], kv["kn"], kv["vb"], do, t["o"], t["lse"], "mla_attn_bwd")
    dq2, dq_gn, dq_gr = _rowwise_bwd(_q_norm_rope_fn, _q_rows(t["q2"], kv["cosf"], kv["sins"]), [p["q_gn"], p["q_gr"]],
                                     [(HEADS * HEAD_PAD, BF16, HEAD_ALL)], [dq],
                                     [(0, HEAD_NOPE), (0, HEAD_ROPE), None, None], [(HEADS * HEAD_PAD, F32)],
                                     "mla_q_norm_bwd", groups=HEADS)
    dw_uq = _mm(t["cqn"], dq2, "tn", "mla_dwuq")
    dcqn = _mm(dq2, p["w_uq"], "nt", "mla_dcq")
    dcq, dq_lora_g = _rowwise_bwd(_rms_fn, [(t["cq"], (0, Q_LORA))], [p["q_lora_g"]], [(Q_LORA, BF16, (0, Q_LORA))],
                                  [dcqn], [(0, (0, Q_LORA))], [(Q_LORA, F32)], "mla_qlora_norm_bwd")
    dw_dq = _mm(t["h"], dcq, "tn", "mla_dwdq")
    dh = _mm(dcq, p["w_dq"], "nt", "mla_dh")
    dx, dgain, dscale, dshift, res_prev = _modulate_bwd(t["x"], p, dh, dxn, prev)
    grads = dict(gain=dgain, scale=dscale, shift=dshift, gm=dgm, w_dq=dw_dq, q_lora_g=dq_lora_g, w_uq=dw_uq,
                 q_gn=dq_gn, q_gr=dq_gr, w_out=dw_out)
    return dx, grads, res_prev, dk, dv


def _k_rows(kvp, ckv, cosf, sins):
    return [(kvp, HEAD_NOPE), (kvp, HEAD_ROPE), (ckv, (KV_LORA, 128)), (cosf, C128), (sins, C128)]


def _kv_fwd(x, p, cosf, sins):
    h = _modulate(x, p)
    ckv = _mm(h, p["w_dkv"], "nn", "kv_down")
    lat, = _rowwise_fwd(_rms_fn, [(ckv, (0, KV_LORA))], [p["kv_g"]], [(KV_LORA, BF16, (0, KV_LORA))], "kv_norm")
    kvp = _mm(lat, p["w_ukv"], "nn", "kv_up")
    kn, vb = _rowwise_fwd(_k_norm_rope_fn, _k_rows(kvp, ckv, cosf, sins), [p["k_gn"], p["k_gr"]],
                          [(HEADS * HEAD_PAD, BF16, HEAD_ALL), (HEADS * HEAD, BF16, HEAD_V)], "kv_k_norm",
                          groups=HEADS)
    return dict(x=x, h=h, ckv=ckv, lat=lat, kvp=kvp, kn=kn, vb=vb, cosf=cosf, sins=sins)


def _kv_bwd(t, p, dk, dv, dx_in, prev):
    dkvp, drope, dk_gn, dk_gr = _rowwise_bwd(
        _k_norm_rope_fn, _k_rows(t["kvp"], t["ckv"], t["cosf"], t["sins"]), [p["k_gn"], p["k_gr"]],
        [(HEADS * HEAD_PAD, BF16, HEAD_ALL), (HEADS * HEAD, BF16, HEAD_V)], [dk, dv],
        [(0, HEAD_NOPE), (0, HEAD_ROPE), (1, C128), None, None], [(HEADS * HEAD_PAD, F32), (128, F32)],
        "kv_k_norm_bwd", groups=HEADS)
    dw_ukv = _mm(t["lat"], dkvp, "tn", "kv_dwukv")
    dlat = _mm(dkvp, p["w_ukv"], "nt", "kv_dlat")
    dckv, dkv_g = _rowwise_bwd(_rms_fn, [(t["ckv"], (0, KV_LORA))], [p["kv_g"]], [(KV_LORA, BF16, (0, KV_LORA))],
                               [dlat], [(0, (0, KV_LORA))], [(KV_LORA, F32)], "kv_norm_bwd")
    dw_dkv = jnp.concatenate([_mm(t["h"], dckv, "tn", "kv_dwdkv"), _mm(t["h"], drope, "tn", "kv_dwdkv_rope")], axis=1)
    dh = _matmul([(dckv, p["w_dkv"]), (drope, p["w_dkv"])], "nt", "kv_dh", boffs=[0, KV_LORA])
    dx, dgain, dscale, dshift, res_prev = _modulate_bwd(t["x"], p, dh, dx_in, prev)
    return dx, dict(gain=dgain, scale=dscale, shift=dshift, w_dkv=dw_dkv, kv_g=dkv_g, w_ukv=dw_ukv, k_gn=dk_gn,
                    k_gr=dk_gr), res_prev


WEIGHTS = ["ada_w", "ada_b", "norm_g", "ffn_w_in", "ffn_w_out", "gdn_w_in", "gdn_conv_w", "gdn_a_log", "gdn_dt_bias",
           "gdn_norm_g", "gdn_w_out", "kv_ada_w", "kv_ada_b", "kv_norm_g", "mla_w_dkv", "mla_kv_norm_g", "mla_w_ukv",
           "mla_k_norm_g", "mla_w_dq", "mla_q_lora_norm_g", "mla_w_uq", "mla_q_norm_g", "mla_w_out"]
SMALL = [("ada_b", 4 * N_MOD * D), ("kv_ada_b", 2 * D), ("norm_g", DEPTH * 3 * D), ("gdn_conv_w", N_A * CONV_K * 3 * D),
         ("gdn_a_log", N_A * HEADS), ("gdn_dt_bias", N_A * HEADS), ("gdn_norm_g", N_A * HEAD), ("kv_norm_g", D),
         ("mla_kv_norm_g", KV_LORA), ("mla_k_norm_g", QK_HEAD), ("mla_q_lora_norm_g", 2 * Q_LORA),
         ("mla_q_norm_g", 2 * QK_HEAD)]
SMALL_REPLICATED = [n for n, _ in SMALL if n not in ("norm_g", "gdn_conv_w")]


def _silu_fn(g, t):
    return (_silu(t),)


def _dup_rope(t):
    return jnp.concatenate([t[..., :NOPE], t[..., NOPE:], t[..., NOPE:]], axis=-1)


def _fold_rope(t):
    return jnp.concatenate([t[..., :NOPE], t[..., NOPE:QK_HEAD] + t[..., QK_HEAD:]], axis=-1)


def _pack(pieces, rows):
    flat = jnp.concatenate([p.reshape(-1).astype(F32) for p in pieces])
    return jnp.pad(flat, (0, rows * 128 - flat.shape[0])).reshape(rows, 128)


def _step(a):
    me = 4 * lax.axis_index("x") + 2 * lax.axis_index("y") + lax.axis_index("c")
    x = a["x"][0]
    cosf, sins = _rope_tables(a["positions"][0])

    n_cw, n_ng = N_A * CONV_K * 3 * HEAD, DEPTH * 3 * HEAD
    small_all = _all_gather(_pack([a["gdn_conv_w"], a["norm_g"], a["c"]], 44), "gather_small").reshape(N_DEV, -1)
    conv_w = small_all[:, :n_cw].reshape(N_DEV, N_A, CONV_K, 3 * HEAD).transpose(1, 2, 0, 3).reshape(N_A, CONV_K, 3 * D)
    norm_g = small_all[:, n_cw:n_cw + n_ng].reshape(N_DEV, DEPTH, 3, HEAD).transpose(1, 2, 0, 3).reshape(DEPTH, 3, D)
    c_all = small_all[:, n_cw + n_ng:n_cw + n_ng + D]

    c_act, = _rowwise_fwd(_silu_fn, [(c_all, FULL)], [], [(D, F32, FULL)], "c_act")
    n_ada = N_MOD * D // N_DEV
    parts = [_mm(c_act, a["ada_w"][l], "nn", "mod_proj") for l in range(DEPTH)]
    parts.append(_mm(c_act, a["kv_ada_w"], "nn", "mod_proj_kv"))
    mod_recv = _exchange(jnp.concatenate(parts, axis=1)[:, None, :], "exchange_mod")[:, 0]
    mod = mod_recv[:, :DEPTH * n_ada].reshape(N_DEV, DEPTH, n_ada).transpose(1, 0, 2).reshape(DEPTH, N_MOD * D)
    mod = (mod + a["ada_b"]).reshape(DEPTH, N_MOD, D)
    kvmod = mod_recv[:, DEPTH * n_ada:].reshape(2 * D) + a["kv_ada_b"]

    n_in = 2 * D_FF // N_DEV
    n_gdn = (4 * D + 2 * HEADS) // N_DEV

    stages = [(l, part) for l in range(DEPTH) for part in range(3)]

    def stage_shards(l, part):
        if part != 1:
            sh = {"ffn_w_in": a["ffn_w_in"][l, part // 2], "ffn_w_out": a["ffn_w_out"][l, part // 2]}
            if part == 2 and l == N_A - 1:
                sh.update(mla_w_dkv=a["mla_w_dkv"], mla_w_ukv=a["mla_w_ukv"])
            return sh
        if l < N_A:
            return {"gdn_w_in": a["gdn_w_in"][l], "gdn_w_out": a["gdn_w_out"][l]}
        j = l - N_A
        return {"mla_w_dq": a["mla_w_dq"][j], "mla_w_uq": a["mla_w_uq"][j], "mla_w_out": a["mla_w_out"][j]}

    def zero_of(t):
        return jnp.minimum(jnp.abs(t[(0,) * t.ndim].astype(F32)), 0.0)

    def start_stage(l, part, tie):
        sh = stage_shards(l, part)
        return list(sh), _send_start([(w + tie).astype(BF16) for w in sh.values()], f"fetch_start_{l}_{part}", gather=True)

    def finish_stage(l, part, names, handle, after):
        srcs, lands = _send_wait(handle, after, f"fetch_wait_{l}_{part}", gather=True)
        return {n: lax.dynamic_update_slice(land, src[None], (me, 0, 0)) for n, src, land in zip(names, srcs, lands)}

    def row(v):
        return v[None]

    def ffn_params(l, i, w):
        w_in = w["ffn_w_in"]
        k = 0 if i == 0 else 6
        return dict(gain=row(norm_g[l, 0 if i == 0 else 2]), shift=row(mod[l, k]), scale=row(mod[l, k + 1]),
                    gm=0.5 * row(mod[l, k + 2]),
                    wg=w_in[:N_DEV // 2].transpose(1, 0, 2).reshape(D, D_FF),
                    wu=w_in[N_DEV // 2:].transpose(1, 0, 2).reshape(D, D_FF),
                    wo=w["ffn_w_out"].reshape(D_FF, D))

    def gdn_params(l, w):
        w_in = w["gdn_w_in"].transpose(1, 0, 2).reshape(D, 4 * D + 2 * HEADS)
        pad = lambda t: jnp.pad(t, ((0, 0), (0, 128 - HEADS)))
        return dict(gain=row(norm_g[l, 1]), shift=row(mod[l, 3]), scale=row(mod[l, 4]), gm=row(mod[l, 5]),
                    w_main=w_in[:, :4 * D],
                    w_tail=jnp.concatenate([pad(w_in[:, 4 * D:4 * D + HEADS]), pad(w_in[:, 4 * D + HEADS:])], axis=1),
                    conv_w=conv_w[l], a_log=_pad128(row(a["gdn_a_log"][l])), dt_bias=_pad128(row(a["gdn_dt_bias"][l])),
                    norm_g=row(a["gdn_norm_g"][l]), w_out=w["gdn_w_out"].reshape(D, D))

    def mla_params(l, w):
        j = l - N_A
        uq = w["mla_w_uq"].transpose(1, 0, 2)
        qg = _dup_rope(a["mla_q_norm_g"][j])
        return dict(gain=row(norm_g[l, 1]), shift=row(mod[l, 3]), scale=row(mod[l, 4]), gm=row(mod[l, 5]),
                    w_dq=w["mla_w_dq"].reshape(D, Q_LORA), q_lora_g=row(a["mla_q_lora_norm_g"][j]),
                    w_uq=_dup_rope(uq).reshape(Q_LORA, HEADS * HEAD_PAD), q_gn=row(qg[:NOPE]), q_gr=row(qg[NOPE:]),
                    w_out=w["mla_w_out"].reshape(D, D))

    def kv_params(w):
        w_dkv = w["mla_w_dkv"].reshape(D, KV_LORA + ROPE)
        kg = _dup_rope(a["mla_k_norm_g"])
        return dict(gain=row(a["kv_norm_g"]), shift=row(kvmod[:D]), scale=row(kvmod[D:]),
                    w_dkv=jnp.concatenate([w_dkv, w_dkv[:, KV_LORA:]], axis=1), kv_g=row(a["mla_kv_norm_g"]),
                    w_ukv=w["mla_w_ukv"].transpose(1, 0, 2).reshape(KV_LORA, HEADS * 2 * HEAD), k_gn=row(kg[:NOPE]),
                    k_gr=row(kg[NOPE:]))

    tapes, kv, kv_p = [[] for _ in range(DEPTH)], None, None
    pend = start_stage(0, 0, 0.0)
    for n, (l, part) in enumerate(stages):
        names, handle = pend
        w = finish_stage(l, part, names, handle, x if n else mod)
        ties = ()
        if n + 1 < len(stages):
            pend = start_stage(*stages[n + 1], zero_of(w[names[0]]))
            ties = (pend[1]["token"],)
        if part != 1:
            p = ffn_params(l, part // 2, w)
            x, t = _ffn_fwd(x, p, ties)
        else:
            p = gdn_params(l, w) if l < N_A else mla_params(l, w)
            x, t = _gdn_fwd(x, p, ties) if l < N_A else _mla_fwd(x, p, kv, ties)
        tapes[l] += [p, t]
        if part == 2 and l == N_A - 1:
            kv_p = kv_params(w)
            kv = _kv_fwd(x, kv_p, cosf, sins)
    dx, loss_blk = _loss_and_grad(x, a["loss_target"][0], "loss")
    loss = lax.psum(loss_blk[0, 0], ("x", "y", "c"))

    def by_cols(g, n):
        return g.reshape(g.shape[0], -1, n).transpose(1, 0, 2)

    def ffn_blocks(g):
        return {"ffn_w_in": jnp.concatenate([by_cols(g["wg"], n_in), by_cols(g["wu"], n_in)], axis=0),
                "ffn_w_out": g["wo"].reshape(N_DEV, D_FF // N_DEV, D)}

    def mixer_blocks(l, g):
        if l < N_A:
            full = jnp.concatenate([g["w_main"], g["w_tail"][:, :HEADS], g["w_tail"][:, 128:128 + HEADS]], axis=1)
            return {"gdn_w_in": by_cols(full, n_gdn), "gdn_w_out": g["w_out"].reshape(N_DEV, D // N_DEV, D)}
        return {"mla_w_dq": g["w_dq"].reshape(N_DEV, D // N_DEV, Q_LORA),
                "mla_w_uq": _fold_rope(g["w_uq"].reshape(Q_LORA, HEADS, HEAD_PAD)).transpose(1, 0, 2),
                "mla_w_out": g["w_out"].reshape(N_DEV, D // N_DEV, D)}

    sent = []

    def send(key, blocks):
        handle = _send_start([b.astype(BF16) for b in blocks.values()], "grad_start_" + "_".join(map(str, key)),
                             gather=False)
        sent.append((key, list(blocks), handle))
        return (handle["token"],)

    grads = [None] * DEPTH
    dk_sum = dv_sum = kv_grads = res = None
    ties = ()
    for l in reversed(range(DEPTH)):
        p1, t1, pm_, tm_, p2, t2 = tapes[l]
        if l == N_A - 1:
            dx, kv_grads, res = _kv_bwd(kv, kv_p, dk_sum, dv_sum, dx, (t2["y"], p2["gm"]))
            d_dkv = kv_grads["w_dkv"]
            ties += send((l, 3), {
                "mla_w_dkv": jnp.concatenate(
                    [d_dkv[:, :KV_LORA], d_dkv[:, KV_LORA:KV_LORA + ROPE] + d_dkv[:, KV_LORA + ROPE:]],
                    axis=1).reshape(N_DEV, D // N_DEV, KV_LORA + ROPE),
                "mla_w_ukv": by_cols(kv_grads["w_ukv"], 2 * HEAD)})
        dx, g2, res = _ffn_bwd(t2, p2, dx, res, (tm_["y"], pm_["gm"]), ties)
        ties = send((l, 2), ffn_blocks(g2))
        if l < N_A:
            dx, gm_, res = _gdn_bwd(tm_, pm_, dx, res, (t1["y"], p1["gm"]), ties)
        else:
            dx, gm_, res, dk, dv = _mla_bwd(tm_, pm_, kv, dx, res, (t1["y"], p1["gm"]), ties)
            dk_sum = dk if dk_sum is None else dk_sum + dk
            dv_sum = dv if dv_sum is None else dv_sum + dv
        ties = send((l, 1), mixer_blocks(l, gm_))
        prev = (tapes[l - 1][5]["y"], tapes[l - 1][4]["gm"]) if l > 0 and l != N_A else None
        dx, g1, res = _ffn_bwd(t1, p1, dx, res, prev, ties)
        ties = send((l, 0), ffn_blocks(g1))
        grads[l] = (g1, gm_, g2)

    out = {}
    def dmod(l):
        g1, gm_, g2 = grads[l]
        return jnp.concatenate([g1["shift"], g1["scale"], 0.5 * g1["gm"], gm_["shift"], gm_["scale"], gm_["gm"],
                                g2["shift"], g2["scale"], 0.5 * g2["gm"]], axis=1)

    gdn = [grads[l][1] for l in range(N_A)]
    mla = [grads[l][1] for l in range(N_A, DEPTH)]
    small = {
        "ada_b": jnp.concatenate([dmod(l) for l in range(DEPTH)], axis=0),
        "kv_ada_b": jnp.concatenate([kv_grads["shift"], kv_grads["scale"]], axis=1),
        "norm_g": jnp.stack([jnp.concatenate([grads[l][0]["gain"], grads[l][1]["gain"], grads[l][2]["gain"]], axis=0)
                             for l in range(DEPTH)]),
        "gdn_conv_w": jnp.stack([g["conv_w"] for g in gdn]),
        "gdn_a_log": jnp.stack([g["a_log"][0, :HEADS] for g in gdn]),
        "gdn_dt_bias": jnp.stack([g["dt_bias"][0, :HEADS] for g in gdn]),
        "gdn_norm_g": jnp.stack([g["norm_g"][0] for g in gdn]),
        "kv_norm_g": kv_grads["gain"],
        "mla_kv_norm_g": kv_grads["kv_g"],
        "mla_k_norm_g": _fold_rope(jnp.concatenate([kv_grads["k_gn"], kv_grads["k_gr"]], axis=1)),
        "mla_q_lora_norm_g": jnp.stack([g["q_lora_g"][0] for g in mla]),
        "mla_q_norm_g": jnp.stack([_fold_rope(jnp.concatenate([g["q_gn"], g["q_gr"]], axis=1))[0] for g in mla]),
    }
    rows = 616
    assert sum(n for _, n in SMALL) <= rows * 128 and all(small[n].size == k for n, k in SMALL)
    small_recv = _all_gather(_pack([small[n] for n, _ in SMALL], rows), "gather_small_grads")
    zero = lambda n, k: jnp.zeros((k,), F32)
    packed = {pre: _pack([a[pre + n] if n in SMALL_REPLICATED else zero(n, k) for n, k in SMALL], rows)
              for pre in ("", "m_", "v_")}
    res = _adamw([small_recv], packed[""], packed["m_"], packed["v_"], "adamw_small")
    offs = {}
    o = 0
    for n, k in SMALL:
        offs[n] = o
        o += k
    for n, k in SMALL:
        if n in SMALL_REPLICATED:
            out[n] = [r.reshape(-1)[offs[n]:offs[n] + k] for r in res]
    gsum = res[0].reshape(-1)
    g_norm = lax.dynamic_slice_in_dim(gsum[offs["norm_g"]:offs["norm_g"] + DEPTH * 3 * D].reshape(DEPTH * 3, D),
                                      me * HEAD, HEAD, axis=1)
    g_conv = lax.dynamic_slice_in_dim(
        gsum[offs["gdn_conv_w"]:offs["gdn_conv_w"] + N_A * CONV_K * 3 * D].reshape(N_A * CONV_K, 3 * D),
        me * 3 * HEAD, 3 * HEAD, axis=1)
    res2 = _adamw([_pack([g_norm, g_conv], 36)[None]], *[_pack([a[pre + "norm_g"], a[pre + "gdn_conv_w"]], 36)
                                                      for pre in ("", "m_", "v_")], "adamw_small")
    out["norm_g"] = [r.reshape(-1)[:n_ng] for r in res2]
    out["gdn_conv_w"] = [r.reshape(-1)[n_ng:n_ng + n_cw] for r in res2]

    c_act_t = c_act.T
    all_small = small_recv.reshape(N_DEV, -1)
    dmod_all = all_small[:, :DEPTH * N_MOD * D].reshape(N_DEV, DEPTH, N_MOD * D)
    dmod_mine = lax.dynamic_slice_in_dim(dmod_all, me * n_ada, n_ada, axis=2)
    g_ada = [_outer8(c_act_t, dmod_mine[:, l], "ada_grad")[None] for l in range(DEPTH)]
    out["ada_w"] = _adamw(g_ada, *[a[pre + "ada_w"].reshape(DEPTH * D, n_ada) for pre in ("", "m_", "v_")], "adamw")
    dkv_all = all_small[:, offs["kv_ada_b"]:offs["kv_ada_b"] + 2 * D]
    g_kv = _outer8(c_act_t, lax.dynamic_slice_in_dim(dkv_all, me * (2 * D // N_DEV), 2 * D // N_DEV, axis=1), "ada_grad")
    out["kv_ada_w"] = _adamw([g_kv[None]], *[a[pre + "kv_ada_w"] for pre in ("", "m_", "v_")], "adamw")

    pieces = {}
    for key, names, handle in sent:
        srcs, lands = _send_wait(handle, out["kv_ada_w"][0], "grad_wait_" + "_".join(map(str, key)), gather=False)
        for name, src, land in zip(names, srcs, lands):
            own = lax.dynamic_slice_in_dim(src, me, 1, axis=0)
            pieces.setdefault(name, []).append((key, lax.dynamic_update_slice(land, own, (me, 0, 0))))
    for name, parts in pieces.items():
        wide = a[name].shape[-1]
        out[name] = _adamw([p for _, p in sorted(parts, key=lambda kp: kp[0])], a[name].reshape(-1, wide),
                           a["m_" + name].reshape(-1, wide), a["v_" + name].reshape(-1, wide), "adamw")

    result = [loss, dx[None]]
    for k in range(4):
        result += [out[n][k].reshape(a[n].shape) for n in WEIGHTS]
    return tuple(result)


def kernel(x, c, positions, ada_w, ada_b, norm_g, ffn_w_in, ffn_w_out, gdn_w_in, gdn_conv_w, gdn_a_log, gdn_dt_bias, gdn_norm_g, gdn_w_out, kv_ada_w, kv_ada_b, kv_norm_g, mla_w_dkv, mla_kv_norm_g, mla_w_ukv, mla_k_norm_g, mla_w_dq, mla_q_lora_norm_g, mla_w_uq, mla_q_norm_g, mla_w_out, loss_target, m_ada_w, m_ada_b, m_norm_g, m_ffn_w_in, m_ffn_w_out, m_gdn_w_in, m_gdn_conv_w, m_gdn_a_log, m_gdn_dt_bias, m_gdn_norm_g, m_gdn_w_out, m_kv_ada_w, m_kv_ada_b, m_kv_norm_g, m_mla_w_dkv, m_mla_kv_norm_g, m_mla_w_ukv, m_mla_k_norm_g, m_mla_w_dq, m_mla_q_lora_norm_g, m_mla_w_uq, m_mla_q_norm_g, m_mla_w_out, v_ada_w, v_ada_b, v_norm_g, v_ffn_w_in, v_ffn_w_out, v_gdn_w_in, v_gdn_conv_w, v_gdn_a_log, v_gdn_dt_bias, v_gdn_norm_g, v_gdn_w_out, v_kv_ada_w, v_kv_ada_b, v_kv_norm_g, v_mla_w_dkv, v_mla_kv_norm_g, v_mla_w_ukv, v_mla_k_norm_g, v_mla_w_dq, v_mla_q_lora_norm_g, v_mla_w_uq, v_mla_q_norm_g, v_mla_w_out):
    return _step(dict(locals()))
```

```python
import functools
import math

import jax
import jax.numpy as jnp
from jax import lax
from jax.experimental import pallas as pl
from jax.experimental.pallas import tpu as pltpu

F32 = jnp.float32
BF16 = jnp.bfloat16

N_DEV = 8
D = 1024
D_FF = 2816
DEPTH = 4
N_A = 2
N_MOD = 9
HEADS = 8
HEAD = 128
CHUNK = 64
CONV_K = 4
KV_LORA = 256
Q_LORA = 384
NOPE = 128
ROPE = 64
QK_HEAD = NOPE + ROPE
HEAD_PAD = 256
ROPE_BASE = 10000.0
EPS = 1e-6
LR, B1, B2, ADAM_EPS, WD, STEP = 0.001, 0.9, 0.999, 1e-08, 0.01, 10

VMEM_LIMIT = 48 * 1024 * 1024
ROW_TILE = 256
MESH = pl.DeviceIdType.MESH

_NN = (((1,), (0,)), ((), ()))
_NT = (((1,), (1,)), ((), ()))
_TN = (((0,), (0,)), ((), ()))
_DIMS = {"nn": _NN, "nt": _NT, "tn": _TN}


def _params(dims=None):
    return pltpu.CompilerParams(dimension_semantics=dims, vmem_limit_bytes=VMEM_LIMIT)


def _tile(n, target):
    for t in range(target - target % 128, 0, -128):
        if n % t == 0:
            return t
    return n


_TIE_SPEC1 = pl.BlockSpec((8, 128), lambda i: (0, 0))
_TIE_SPEC2 = pl.BlockSpec((8, 128), lambda i, j: (0, 0))
_TIE_SPEC3 = pl.BlockSpec((8, 128), lambda i, j, k: (0, 0))


def _matmul(pairs, form, name, out_dtype=F32, tm=1408, tn=1408, tk=1408, boffs=None, resid=None, ties=()):
    a0, b0 = pairs[0]
    if form == "nn":
        m, n = a0.shape[0], b0.shape[1]
        ks = [a.shape[1] for a, _ in pairs]
    elif form == "nt":
        m, n = a0.shape[0], b0.shape[0]
        ks = [a.shape[1] for a, _ in pairs]
    else:
        m, n = a0.shape[1], b0.shape[1]
        ks = [a.shape[0] for a, _ in pairs]
    tm, tn = _tile(m, tm), _tile(n, tn)
    tks = [_tile(k, tk) for k in ks]
    boffs = boffs or [0] * len(pairs)
    assert m % tm == 0 and n % tn == 0 and all(o % t == 0 for o, t in zip(boffs, tks)), (name, m, n, ks)
    steps = [k // t for k, t in zip(ks, tks)]
    starts = [sum(steps[:p]) for p in range(len(pairs))]
    nk = sum(steps)

    def kidx(p, k):
        return jnp.clip(k - starts[p], 0, steps[p] - 1)

    in_specs, args = [], []
    for p, (a, b) in enumerate(pairs):
        t = tks[p]
        if form == "tn":
            in_specs.append(pl.BlockSpec((t, tm), lambda i, j, k, p=p: (kidx(p, k), i)))
            in_specs.append(pl.BlockSpec((t, tn), lambda i, j, k, p=p: (kidx(p, k), j)))
        elif form == "nn":
            in_specs.append(pl.BlockSpec((tm, t), lambda i, j, k, p=p: (i, kidx(p, k))))
            in_specs.append(pl.BlockSpec((t, tn), lambda i, j, k, p=p: (kidx(p, k), j)))
        else:
            in_specs.append(pl.BlockSpec((tm, t), lambda i, j, k, p=p: (i, kidx(p, k))))
            in_specs.append(pl.BlockSpec((tn, t), lambda i, j, k, p=p, o=boffs[p] // t: (j, kidx(p, k) + o)))
        args += [a, b]
    dims = _DIMS[form]
    npairs = len(pairs)
    nin = 2 * npairs + len(ties) + (2 if resid else 0)
    out_blk = pl.BlockSpec((tm, tn), lambda i, j, k: (i, j))
    in_specs += [_TIE_SPEC3] * len(ties)
    args += list(ties)
    if resid:
        in_specs += [out_blk, pl.BlockSpec((1, tn), lambda i, j, k: (0, j))]
        args += list(resid)

    def body(*refs):
        o_ref = refs[nin]
        k = pl.program_id(2)

        def prod(p):
            return lax.dot_general(refs[2 * p][...].astype(BF16), refs[2 * p + 1][...].astype(BF16), dims,
                                   preferred_element_type=F32)

        def finish(y):
            o_ref[...] = y.astype(o_ref.dtype)
            if resid:
                refs[nin + 1][...] = refs[nin - 2][...] + refs[nin - 1][...] * y

        if nk == 1:
            finish(prod(0))
            return
        acc = refs[-1]

        @pl.when(k == 0)
        def _():
            acc[...] = jnp.zeros_like(acc)

        for p in range(npairs):
            @pl.when((k >= starts[p]) & (k < starts[p] + steps[p]))
            def _(p=p):
                acc[...] += prod(p)

        @pl.when(k == nk - 1)
        def _():
            finish(acc[...])

    res = pl.pallas_call(
        body, name=name, grid=(m // tm, n // tn, nk), in_specs=in_specs,
        out_specs=[out_blk, out_blk] if resid else out_blk,
        out_shape=[jax.ShapeDtypeStruct((m, n), out_dtype)] * 2 if resid else jax.ShapeDtypeStruct((m, n), out_dtype),
        scratch_shapes=[] if nk == 1 else [pltpu.VMEM((tm, tn), F32)],
        compiler_params=_params(("parallel", "parallel", "arbitrary")),
    )(*args)
    return res


def _mm(a, b, form, name, **kw):
    return _matmul([(a, b)], form, name, **kw)


def _cols(spec, g):
    return spec[g] if isinstance(spec, list) else spec


def _rowwise_fwd(fn, rows, pars, outs, name, groups=1, ts=ROW_TILE, ties=()):
    s = rows[0][0].shape[0]
    ts = min(ts, s)
    assert s % ts == 0
    nr, npar = len(rows), len(pars)

    def body(*refs):
        par_t = [r[...] for r in refs[nr:nr + npar]]
        out_refs = refs[nr + npar + len(ties):]
        for g in range(groups):
            row_t = []
            for r, (_, spec) in zip(refs[:nr], rows):
                c0, w = _cols(spec, g)
                row_t.append(r[:, c0:c0 + w].astype(F32))
            res = fn(g, *row_t, *par_t)
            for o_ref, val, (_, _, spec) in zip(out_refs, res, outs):
                c0, w = _cols(spec, g)
                o_ref[:, c0:c0 + w] = val.astype(o_ref.dtype)

    return pl.pallas_call(
        body, name=name, grid=(s // ts,),
        in_specs=[pl.BlockSpec((ts, a.shape[1]), lambda i: (i, 0)) for a, _ in rows]
        + [pl.BlockSpec(p.shape, lambda i: (0, 0)) for p in pars] + [_TIE_SPEC1] * len(ties),
        out_specs=[pl.BlockSpec((ts, w), lambda i: (i, 0)) for w, _, _ in outs],
        out_shape=[jax.ShapeDtypeStruct((s, w), dt) for w, dt, _ in outs],
        compiler_params=_params(("parallel",)),
    )(*[a for a, _ in rows], *pars, *ties)


def _rowwise_bwd(fn, rows, pars, outs, douts, gmap, gshapes, name, groups=1, add=None, par_grads=True,
                 ts=ROW_TILE):
    s = rows[0][0].shape[0]
    ts = min(ts, s)
    assert s % ts == 0
    nr, npar, nout, ng = len(rows), len(pars), len(outs), len(gshapes)
    add = add or {}
    add_keys = sorted(add)

    def body(*refs):
        row_refs = refs[:nr]
        par_refs = refs[nr:nr + npar]
        dout_refs = refs[nr + npar:nr + npar + nout]
        add_refs = refs[nr + npar + nout:nr + npar + nout + len(add_keys)]
        g_refs = refs[nr + npar + nout + len(add_keys):][:ng]
        pg_refs = refs[nr + npar + nout + len(add_keys) + ng:]
        par_t = [r[...] for r in par_refs]
        par_acc = [None] * npar
        shared_acc = {}
        for g in range(groups):
            row_t = []
            for r, (_, spec) in zip(row_refs, rows):
                c0, w = _cols(spec, g)
                row_t.append(r[:, c0:c0 + w].astype(F32))
            cts = []
            for r, (_, _, spec) in zip(dout_refs, outs):
                c0, w = _cols(spec, g)
                cts.append(r[:, c0:c0 + w].astype(F32))
            _, vjp = jax.vjp(lambda *t, g=g: tuple(fn(g, *t)), *row_t, *par_t)
            grads = vjp(tuple(cts))
            for k in range(nr):
                if gmap[k] is None:
                    continue
                gi, spec = gmap[k]
                if isinstance(spec, list) or groups == 1:
                    c0, w = _cols(spec, g)
                    val = grads[k]
                    if gi in add:
                        val = val + add_refs[add_keys.index(gi)][:, c0:c0 + w].astype(F32)
                    g_refs[gi][:, c0:c0 + w] = val.astype(g_refs[gi].dtype)
                else:
                    shared_acc[k] = grads[k] if k not in shared_acc else shared_acc[k] + grads[k]
            if par_grads:
                for k in range(npar):
                    pg = grads[nr + k]
                    par_acc[k] = pg if par_acc[k] is None else par_acc[k] + pg
        for k, val in shared_acc.items():
            gi, (c0, w) = gmap[k]
            assert gi not in add
            g_refs[gi][:, c0:c0 + w] = val.astype(g_refs[gi].dtype)
        if par_grads:
            first = pl.program_id(0) == 0
            for k in range(npar):
                @pl.when(first)
                def _(k=k):
                    pg_refs[k][...] = par_acc[k]

                @pl.when(jnp.logical_not(first))
                def _(k=k):
                    pg_refs[k][...] += par_acc[k]

    out_specs = [pl.BlockSpec((ts, w), lambda i: (i, 0)) for w, _ in gshapes]
    out_shape = [jax.ShapeDtypeStruct((s, w), dt) for w, dt in gshapes]
    if par_grads:
        out_specs += [pl.BlockSpec(p.shape, lambda i: (0, 0)) for p in pars]
        out_shape += [jax.ShapeDtypeStruct(p.shape, F32) for p in pars]
    return pl.pallas_call(
        body, name=name, grid=(s // ts,),
        in_specs=[pl.BlockSpec((ts, a.shape[1]), lambda i: (i, 0)) for a, _ in rows]
        + [pl.BlockSpec(p.shape, lambda i: (0, 0)) for p in pars]
        + [pl.BlockSpec((ts, a.shape[1]), lambda i: (i, 0)) for a in douts]
        + [pl.BlockSpec((ts, add[k].shape[1]), lambda i: (i, 0)) for k in add_keys],
        out_specs=out_specs, out_shape=out_shape,
        compiler_params=_params(("arbitrary",)),
    )(*[a for a, _ in rows], *pars, *douts, *[add[k] for k in add_keys])


def _sigmoid(x):
    return 1.0 / (1.0 + jnp.exp(-x))


def _silu(x):
    return x * _sigmoid(x)


def _softplus(x):
    return jnp.maximum(x, 0.0) + jnp.log(1.0 + jnp.exp(-jnp.abs(x)))


def _rms(t, g, n=None):
    n = n or t.shape[-1]
    return t * lax.rsqrt(jnp.sum(t * t, axis=-1, keepdims=True) / n + EPS) * g


def _modulate_fn(g, x, gain, scale, shift):
    return (_rms(x, gain) * (1.0 + scale) + shift,)


def _resgate_fn(g, x, y, gm):
    return (x + gm * y,)


def _gate_only_fn(g, y, gm):
    return (gm * y,)


def _gdn_gates_fn(g, b_logit, a_logit, a_log, dt_bias):
    gate = -jnp.exp(a_log) * _softplus(a_logit + dt_bias)
    n = gate.shape[0]
    i = lax.broadcasted_iota(jnp.int32, (n, n), 0)
    j = lax.broadcasted_iota(jnp.int32, (n, n), 1)
    tri = (((i // CHUNK) == (j // CHUNK)) & (i >= j)).astype(F32)
    gcum = lax.dot_general(tri, gate, _NN, preferred_element_type=F32, precision=lax.Precision.HIGHEST)
    return _sigmoid(b_logit), gcum


def _gdn_outnorm_fn(g, o, z, gain):
    return (_rms(o, gain) * _silu(z),)


def _rms_fn(g, t, gain):
    return (_rms(t, gain),)


@jax.custom_vjp
def _swap_halves(t):
    return pltpu.roll(t, 32, 1)


_swap_halves.defvjp(lambda t: (pltpu.roll(t, 32, 1), None), lambda _, ct: (pltpu.roll(ct, 96, 1),))


def _head_norm_rope_fn(g, nope, rope, cosf, sins, gain_n, gain_r):
    first = lax.broadcasted_iota(jnp.int32, rope.shape, 1) < ROPE
    ss = jnp.sum(nope * nope, axis=-1, keepdims=True) + jnp.sum(jnp.where(first, rope * rope, 0.0), axis=-1,
                                                                 keepdims=True)
    r = lax.rsqrt(ss / QK_HEAD + EPS)
    tn = nope * r * gain_n
    tr = rope * r * gain_r
    rot = jnp.where(first, tr * cosf + _swap_halves(tr) * sins, 0.0)
    return tn, rot


def _q_norm_rope_fn(g, nope, rope, cosf, sins, gain_n, gain_r):
    tn, rot = _head_norm_rope_fn(g, nope, rope, cosf, sins, gain_n, gain_r)
    return (jnp.concatenate([tn, rot], axis=1),)


def _k_norm_rope_fn(g, nope, val, rope, cosf, sins, gain_n, gain_r):
    tn, rot = _head_norm_rope_fn(g, nope, rope, cosf, sins, gain_n, gain_r)
    return jnp.concatenate([tn, rot], axis=1), val


def _loss_fn(g, y, target):
    e = y - target
    return (jnp.sum(e * e, axis=-1, keepdims=True) * (0.5 / D) * jnp.ones((1, 128), F32),)


def _ffn_in(h, wg, wu, name, tm=512, tn=1408):
    s = h.shape[0]
    tm = min(tm, s)

    def body(h_ref, wg_ref, wu_ref, g_ref, u_ref, a_ref):
        hb = h_ref[...]
        gate = jnp.dot(hb, wg_ref[...], preferred_element_type=F32)
        up = jnp.dot(hb, wu_ref[...], preferred_element_type=F32)
        g_ref[...] = gate.astype(BF16)
        u_ref[...] = up.astype(BF16)
        a_ref[...] = (_silu(gate) * up).astype(BF16)

    spec = pl.BlockSpec((tm, tn), lambda j, i: (i, j))
    return pl.pallas_call(
        body, name=name, grid=(D_FF // tn, s // tm),
        in_specs=[pl.BlockSpec((tm, D), lambda j, i: (i, 0)), pl.BlockSpec((D, tn), lambda j, i: (0, j)),
                  pl.BlockSpec((D, tn), lambda j, i: (0, j))],
        out_specs=[spec, spec, spec], out_shape=[jax.ShapeDtypeStruct((s, D_FF), BF16)] * 3,
        compiler_params=_params(("parallel", "parallel")),
    )(h, wg, wu)


def _ffn_bwd_act(dy, wo, gate, up, name, tm=512, tn=1408, ties=()):
    s = dy.shape[0]
    tm = min(tm, s)

    def body(dy_ref, wo_ref, g_ref, u_ref, *rest):
        dg_ref, du_ref = rest[-2:]
        dact = lax.dot_general(dy_ref[...], wo_ref[...], _NT, preferred_element_type=F32)
        gate = g_ref[...].astype(F32)
        up = u_ref[...].astype(F32)
        sg = _sigmoid(gate)
        dg_ref[...] = (dact * up * (sg * (1.0 + gate * (1.0 - sg)))).astype(BF16)
        du_ref[...] = (dact * (gate * sg)).astype(BF16)

    spec = pl.BlockSpec((tm, tn), lambda j, i: (i, j))
    return pl.pallas_call(
        body, name=name, grid=(D_FF // tn, s // tm),
        in_specs=[pl.BlockSpec((tm, D), lambda j, i: (i, 0)), pl.BlockSpec((tn, D), lambda j, i: (j, 0)), spec, spec]
        + [_TIE_SPEC2] * len(ties),
        out_specs=[spec, spec], out_shape=[jax.ShapeDtypeStruct((s, D_FF), BF16)] * 2,
        compiler_params=_params(("parallel", "parallel")),
    )(dy, wo, gate, up, *ties)


def _shift_down(x, d):
    rows = lax.broadcasted_iota(jnp.int32, x.shape, 0)
    return jnp.where(rows >= d, pltpu.roll(x, d, 0), 0.0)


def _shift_up(x, d):
    n = x.shape[0]
    rows = lax.broadcasted_iota(jnp.int32, x.shape, 0)
    return jnp.where(rows < n - d, pltpu.roll(x, n - d, 0), 0.0)


def _conv_post(pre, is_qk):
    a = _silu(pre)
    l2 = a * lax.rsqrt(jnp.sum(a * a, axis=-1, keepdims=True) + EPS)
    return jnp.where(is_qk, l2, a)


def _conv_pre(x, w):
    pre = x * w[CONV_K - 1:CONV_K, :]
    for j in range(CONV_K - 1):
        pre = pre + _shift_down(x, CONV_K - 1 - j) * w[j:j + 1, :]
    return pre


def _gdn_conv_fwd(pm, conv_w, name):
    s = pm.shape[0]
    nblk = 3 * D // HEAD

    def body(x_ref, w_ref, o_ref):
        is_qk = pl.program_id(0) < 2 * HEADS
        o_ref[...] = _conv_post(_conv_pre(x_ref[...], w_ref[...]), is_qk)

    return pl.pallas_call(
        body, name=name, grid=(nblk,),
        in_specs=[pl.BlockSpec((s, HEAD), lambda c: (0, c)), pl.BlockSpec((CONV_K, HEAD), lambda c: (0, c))],
        out_specs=pl.BlockSpec((s, HEAD), lambda c: (0, c)),
        out_shape=jax.ShapeDtypeStruct((s, 3 * D), F32), compiler_params=_params(("parallel",)),
    )(pm, conv_w)


def _gdn_conv_bwd(pm, conv_w, dout, part, name):
    s = pm.shape[0]
    off = part * HEADS

    def body(x_ref, w_ref, d_ref, dx_ref, dw_ref):
        x, w = x_ref[...], w_ref[...]
        _, vjp = jax.vjp(lambda p: _conv_post(p, part < 2), _conv_pre(x, w))
        dpre, = vjp(d_ref[...])
        dx = dpre * w[CONV_K - 1:CONV_K, :]
        rows = [None] * CONV_K
        rows[CONV_K - 1] = jnp.sum(dpre * x, axis=0, keepdims=True)
        for j in range(CONV_K - 1):
            dx = dx + _shift_up(dpre, CONV_K - 1 - j) * w[j:j + 1, :]
            rows[j] = jnp.sum(dpre * _shift_down(x, CONV_K - 1 - j), axis=0, keepdims=True)
        dx_ref[...] = dx
        dw_ref[...] = jnp.concatenate(rows, axis=0)

    return pl.pallas_call(
        body, name=name, grid=(HEADS,),
        in_specs=[pl.BlockSpec((s, HEAD), lambda c: (0, c + off)), pl.BlockSpec((CONV_K, HEAD), lambda c: (0, c + off)),
                  pl.BlockSpec((s, HEAD), lambda c: (0, c))],
        out_specs=[pl.BlockSpec((s, HEAD), lambda c: (0, c)), pl.BlockSpec((CONV_K, HEAD), lambda c: (0, c))],
        out_shape=[jax.ShapeDtypeStruct((s, D), F32), jax.ShapeDtypeStruct((CONV_K, D), F32)],
        compiler_params=_params(("parallel",)),
    )(pm, conv_w, dout)


def _dot3(a, b, dims=_NN):
    ah, bh = a.astype(BF16), b.astype(BF16)
    al, bl = (a - ah.astype(F32)).astype(BF16), (b - bh.astype(F32)).astype(BF16)
    d = lambda u, v: lax.dot_general(u, v, dims, preferred_element_type=F32)
    return d(ah, bh) + (d(ah, bl) + d(al, bh))


def _make_dot(hi):
    def raw(a, b, dims):
        if hi:
            return _dot3(a, b, dims)
        return lax.dot_general(a.astype(BF16), b.astype(BF16), dims, preferred_element_type=F32)

    @functools.partial(jax.custom_vjp, nondiff_argnums=(2,))
    def dot(a, b, form):
        return raw(a, b, _DIMS[form])

    def fwd(a, b, form):
        return raw(a, b, _DIMS[form]), (a, b)

    def bwd(form, res, ct):
        a, b = res
        if form == "nn":
            return raw(ct, b, _NT), raw(a, ct, _TN)
        if form == "nt":
            return raw(ct, b, _NN), raw(ct, a, _TN)
        return raw(b, ct, _NT), raw(a, ct, _NN)

    dot.defvjp(fwd, bwd)
    return dot


_dot = _make_dot(False)
_dot_hi = _make_dot(True)


def _tri_inv_raw(low):
    n = low.shape[0]
    i = lax.broadcasted_iota(jnp.int32, (n, n), 0)
    j = lax.broadcasted_iota(jnp.int32, (n, n), 1)
    eye = (i == j).astype(F32)
    hdot = _dot3
    same16 = (i // 16) == (j // 16)
    neg = jnp.where(same16, -low, 0.0)
    inv = eye + neg
    power = neg
    for _ in range(3):
        power = hdot(power, power)
        inv = hdot(inv, eye + power)
    for blk in (32, 64):
        off = jnp.where(((i // blk) == (j // blk)) & ((i // (blk // 2)) != (j // (blk // 2))), low, 0.0)
        inv = inv - hdot(inv, hdot(off, inv))
    return inv


@jax.custom_vjp
def _tri_inv(low):
    return _tri_inv_raw(low)


def _tri_inv_fwd(low):
    inv = _tri_inv_raw(low)
    return inv, inv


def _tri_inv_bwd(inv, ct):
    return (-_dot3(_dot3(inv, ct, _TN), inv, _NT),)


_tri_inv.defvjp(_tri_inv_fwd, _tri_inv_bwd)


@jax.custom_vjp
def _tri_inv_given(low, inv):
    return inv


_tri_inv_given.defvjp(lambda low, inv: (inv, inv),
                      lambda inv, ct: (_tri_inv_bwd(inv, ct)[0], jnp.zeros_like(inv)))

GROUP = 4
N_GROUPS = HEADS // GROUP
GROWS = GROUP * CHUNK


def _gdn_group(q, k, v, beta, gc, gr, states, inv=None):
    n = q.shape[0]
    i = lax.broadcasted_iota(jnp.int32, (n, n), 0)
    j = lax.broadcasted_iota(jnp.int32, (n, n), 1)
    same = (i // CHUNK) == (j // CHUNK)
    incl, strict = same & (i >= j), same & (i > j)
    qs = q * (HEAD ** -0.5)
    decay = jnp.where(incl, jnp.exp(jnp.where(incl, gc - gr, 0.0)), 0.0)
    kb = k * beta
    eg = jnp.exp(gc)
    prod = _dot(jnp.concatenate([kb, qs], axis=0), k, "nt")
    low = jnp.where(strict, prod[:n] * decay, 0.0)
    attn = jnp.where(incl, prod[n:] * decay, 0.0)
    inv = _tri_inv(low) if inv is None else _tri_inv_given(low, inv)
    sol = _dot_hi(inv, jnp.concatenate([v * beta, kb * eg], axis=1), "nn")
    u, w, qg = sol[:, :HEAD], sol[:, HEAD:], qs * eg
    last = lax.broadcasted_iota(jnp.int32, (CHUNK, 1), 0) == CHUNK - 1
    v_new, o_state, carry = [], [], []
    for h, state in enumerate(states):
        rows = slice(h * CHUNK, (h + 1) * CHUNK)
        ws = _dot(jnp.concatenate([w[rows], qg[rows]], axis=0), state, "nn")
        v_new.append(u[rows] - ws[:CHUNK])
        o_state.append(ws[CHUNK:])
        g_last = jnp.sum(jnp.where(last, gc[rows], 0.0), axis=0, keepdims=True)
        carry.append((g_last, k[rows] * jnp.exp(g_last - gc[rows])))
    o = jnp.concatenate(o_state, axis=0) + _dot(attn, jnp.concatenate(v_new, axis=0), "nn")
    new = tuple(state * jnp.exp(g_last) + _dot(k_dec, vn, "tn")
                for state, (g_last, k_dec), vn in zip(states, carry, v_new))
    return o, new, inv


def _gdn_specs(s, rev):
    nc = s // CHUNK
    at = (lambda n: nc - 1 - n) if rev else (lambda n: n)
    return nc, at, [
        pl.BlockSpec((CHUNK, D), lambda n: (at(n), 0)), pl.BlockSpec((CHUNK, D), lambda n: (at(n), 1)),
        pl.BlockSpec((CHUNK, D), lambda n: (at(n), 2)), pl.BlockSpec((CHUNK, HEAD), lambda n: (at(n), 0)),
        pl.BlockSpec((CHUNK, HEAD), lambda n: (at(n), 0)),
        pl.BlockSpec((None, N_GROUPS, 1, GROWS), lambda n: (at(n), 0, 0, 0))]


def _group_operands(grp, q_ref, k_ref, v_ref, b_blk, gc_blk, gr_blk):
    heads = range(grp * GROUP, (grp + 1) * GROUP)
    stack = lambda ref: jnp.concatenate([ref[:, h * HEAD:(h + 1) * HEAD] for h in heads], axis=0)
    col = lambda blk: jnp.concatenate([blk[:, h:h + 1] for h in heads], axis=0)
    return stack(q_ref), stack(k_ref), stack(v_ref), col(b_blk), col(gc_blk), gr_blk[grp]


def _gdn_scan_fwd(qkv, beta, gcum, grow, name):
    s = qkv.shape[0]
    nc, _, in_specs = _gdn_specs(s, rev=False)

    def body(q_ref, k_ref, v_ref, b_ref, gc_ref, gr_ref, o_ref, st_ref, inv_ref, state):
        @pl.when(pl.program_id(0) == 0)
        def _():
            state[...] = jnp.zeros_like(state)

        b_blk, gc_blk, gr_blk = b_ref[...], gc_ref[...], gr_ref[...]
        old = [state[h] for h in range(HEADS)]
        res = [_gdn_group(*_group_operands(grp, q_ref, k_ref, v_ref, b_blk, gc_blk, gr_blk),
                          old[grp * GROUP:(grp + 1) * GROUP]) for grp in range(N_GROUPS)]
        for grp, (o, new, inv) in enumerate(res):
            inv_ref[grp] = inv
            for hh in range(GROUP):
                h = grp * GROUP + hh
                st_ref[h] = old[h]
                o_ref[:, h * HEAD:(h + 1) * HEAD] = o[hh * CHUNK:(hh + 1) * CHUNK]
                state[h] = new[hh]

    return pl.pallas_call(
        body, name=name, grid=(nc,), in_specs=in_specs,
        out_specs=[pl.BlockSpec((CHUNK, D), lambda n: (n, 0)),
                   pl.BlockSpec((None, HEADS, HEAD, HEAD), lambda n: (n, 0, 0, 0)),
                   pl.BlockSpec((None, N_GROUPS, GROWS, GROWS), lambda n: (n, 0, 0, 0))],
        out_shape=[jax.ShapeDtypeStruct((s, D), F32), jax.ShapeDtypeStruct((nc, HEADS, HEAD, HEAD), F32),
                   jax.ShapeDtypeStruct((nc, N_GROUPS, GROWS, GROWS), F32)],
        scratch_shapes=[pltpu.VMEM((HEADS, HEAD, HEAD), F32)],
        compiler_params=_params(("arbitrary",)),
    )(qkv, qkv, qkv, beta, gcum, grow)


def _gdn_scan_bwd(qkv, beta, gcum, grow, states, invs, do, name):
    s = qkv.shape[0]
    nc, at, in_specs = _gdn_specs(s, rev=True)
    in_specs += [pl.BlockSpec((None, HEADS, HEAD, HEAD), lambda n: (at(n), 0, 0, 0)),
                 pl.BlockSpec((None, N_GROUPS, GROWS, GROWS), lambda n: (at(n), 0, 0, 0)),
                 pl.BlockSpec((CHUNK, D), lambda n: (at(n), 0))]

    def body(q_ref, k_ref, v_ref, b_ref, gc_ref, gr_ref, st_ref, inv_ref, do_ref, dq_ref, dk_ref, dv_ref, db_ref,
             dgc_ref, dgr_ref, dstate):
        @pl.when(pl.program_id(0) == 0)
        def _():
            dstate[...] = jnp.zeros_like(dstate)

        b_blk, gc_blk, gr_blk = b_ref[...], gc_ref[...], gr_ref[...]
        dold = [dstate[h] for h in range(HEADS)]
        res = []
        for grp in range(N_GROUPS):
            heads = range(grp * GROUP, (grp + 1) * GROUP)
            inv = inv_ref[grp]
            _, vjp = jax.vjp(lambda q, k, v, b, gc, gr, *st, inv=inv: _gdn_group(q, k, v, b, gc, gr, st, inv)[:2],
                             *_group_operands(grp, q_ref, k_ref, v_ref, b_blk, gc_blk, gr_blk),
                             *[st_ref[h] for h in heads])
            d_out = jnp.concatenate([do_ref[:, h * HEAD:(h + 1) * HEAD] for h in heads], axis=0)
            res.append(vjp((d_out, tuple(dold[h] for h in heads))))
        lane = lax.broadcasted_iota(jnp.int32, (CHUNK, HEAD), 1)
        db_all = jnp.zeros((CHUNK, HEAD), F32)
        dgc_all = jnp.zeros((CHUNK, HEAD), F32)
        for grp, (dq, dk, dv, db, dgc, dgr, *dst) in enumerate(res):
            dgr_ref[grp] = dgr
            for hh in range(GROUP):
                h = grp * GROUP + hh
                cs, rows = slice(h * HEAD, (h + 1) * HEAD), slice(hh * CHUNK, (hh + 1) * CHUNK)
                dq_ref[:, cs] = dq[rows]
                dk_ref[:, cs] = dk[rows]
                dv_ref[:, cs] = dv[rows]
                dstate[h] = dst[hh]
                db_all = jnp.where(lane == h, db[rows], db_all)
                dgc_all = jnp.where(lane == h, dgc[rows], dgc_all)
        db_ref[...] = db_all
        dgc_ref[...] = dgc_all

    blk = pl.BlockSpec((CHUNK, D), lambda n: (at(n), 0))
    gblk = pl.BlockSpec((CHUNK, HEAD), lambda n: (at(n), 0))
    return pl.pallas_call(
        body, name=name, grid=(nc,), in_specs=in_specs,
        out_specs=[blk, blk, blk, gblk, gblk, pl.BlockSpec((None, N_GROUPS, 1, GROWS), lambda n: (at(n), 0, 0, 0))],
        out_shape=[jax.ShapeDtypeStruct((s, D), F32)] * 3 + [jax.ShapeDtypeStruct((s, HEAD), F32)] * 2
        + [jax.ShapeDtypeStruct((nc, N_GROUPS, 1, GROWS), F32)],
        scratch_shapes=[pltpu.VMEM((HEADS, HEAD, HEAD), F32)],
        compiler_params=_params(("arbitrary",)),
    )(qkv, qkv, qkv, beta, gcum, grow, states, invs, do)


ATT_TILE = 512
ATT_SCALE = QK_HEAD ** -0.5


def _att_mask(t):
    qpos = lax.broadcasted_iota(jnp.int32, (t, t), 0)
    kpos = lax.broadcasted_iota(jnp.int32, (t, t), 1)
    return (kpos // CHUNK) <= (qpos // CHUNK)


def _att_pairs(nb, by_query):
    if by_query:
        pairs = [(i, j) for i in range(nb) for j in range(i + 1)]
    else:
        pairs = [(j, i) for j in range(nb) for i in range(j, nb)]
    return jnp.array([a for a, _ in pairs], jnp.int32), jnp.array([b for _, b in pairs], jnp.int32)


def _attn_fwd(q, k, v, name):
    s = q.shape[0]
    t = min(ATT_TILE, s)
    nb = s // t
    ii, jj = _att_pairs(nb, by_query=True)

    def body(ii_ref, jj_ref, q_ref, k_ref, v_ref, o_ref, lse_ref, m_s, l_s, acc):
        step = pl.program_id(1)
        i, j = ii_ref[step], jj_ref[step]

        @pl.when(j == 0)
        def _():
            m_s[...] = jnp.full_like(m_s, -jnp.inf)
            l_s[...] = jnp.zeros_like(l_s)
            acc[...] = jnp.zeros_like(acc)

        sc = lax.dot_general(q_ref[...], k_ref[...], _NT, preferred_element_type=F32) * ATT_SCALE
        sc = lax.cond(i == j, lambda u: jnp.where(_att_mask(t), u, -jnp.inf), lambda u: u, sc)
        m_new = jnp.maximum(m_s[...], jnp.max(sc, axis=-1, keepdims=True))
        alpha = jnp.exp(m_s[...] - m_new)
        p = jnp.exp(sc - m_new)
        l_s[...] = alpha * l_s[...] + jnp.sum(p, axis=-1, keepdims=True)
        acc[...] = alpha * acc[...] + jnp.dot(p.astype(BF16), v_ref[...], preferred_element_type=F32)
        m_s[...] = m_new

        @pl.when(j == i)
        def _():
            o_ref[...] = acc[...] / l_s[...]
            lse_ref[...] = m_s[...] + jnp.log(l_s[...])

    grid_spec = pltpu.PrefetchScalarGridSpec(
        num_scalar_prefetch=2, grid=(HEADS, len(ii)),
        in_specs=[pl.BlockSpec((t, HEAD_PAD), lambda h, n, ir, jr: (ir[n], h)),
                  pl.BlockSpec((t, HEAD_PAD), lambda h, n, ir, jr: (jr[n], h)),
                  pl.BlockSpec((t, HEAD), lambda h, n, ir, jr: (jr[n], h))],
        out_specs=[pl.BlockSpec((t, HEAD), lambda h, n, ir, jr: (ir[n], h)),
                   pl.BlockSpec((None, t, 1), lambda h, n, ir, jr: (h, ir[n], 0))],
        scratch_shapes=[pltpu.VMEM((t, 1), F32), pltpu.VMEM((t, 1), F32), pltpu.VMEM((t, HEAD), F32)])
    return pl.pallas_call(
        body, name=name, grid_spec=grid_spec,
        out_shape=[jax.ShapeDtypeStruct((s, HEADS * HEAD), F32), jax.ShapeDtypeStruct((HEADS, s, 1), F32)],
        compiler_params=_params(("parallel", "arbitrary")),
    )(ii, jj, q, k, v)


def _attn_bwd(q, k, v, do, o, lse, name):
    s = q.shape[0]
    t = min(ATT_TILE, s)
    nb = s // t
    jj, ii = _att_pairs(nb, by_query=False)

    def body(jj_ref, ii_ref, q_ref, k_ref, v_ref, do_ref, o_ref, lse_ref, dq_ref, dk_ref, dv_ref, dk_acc, dv_acc):
        step = pl.program_id(1)
        i, j = ii_ref[step], jj_ref[step]

        @pl.when(step == 0)
        def _():
            dq_ref[...] = jnp.zeros_like(dq_ref)

        @pl.when(i == j)
        def _():
            dk_acc[...] = jnp.zeros_like(dk_acc)
            dv_acc[...] = jnp.zeros_like(dv_acc)

        sc = lax.dot_general(q_ref[...], k_ref[...], _NT, preferred_element_type=F32) * ATT_SCALE
        p = jnp.exp(sc - lse_ref[...])
        p = lax.cond(i == j, lambda u: jnp.where(_att_mask(t), u, 0.0), lambda u: u, p)
        do_f = do_ref[...]
        dob = do_f.astype(BF16)
        delta = jnp.sum(do_f * o_ref[...], axis=-1, keepdims=True)
        dv_acc[...] += lax.dot_general(p.astype(BF16), dob, _TN, preferred_element_type=F32)
        dp = lax.dot_general(dob, v_ref[...], _NT, preferred_element_type=F32)
        ds = (p * (dp - delta) * ATT_SCALE).astype(BF16)
        dk_acc[...] += lax.dot_general(ds, q_ref[...], _TN, preferred_element_type=F32)
        rows = pl.ds(pl.multiple_of(i * t, t), t)
        dq_ref[rows, :] += jnp.dot(ds, k_ref[...], preferred_element_type=F32)

        @pl.when(i == nb - 1)
        def _():
            dk_ref[...] = dk_acc[...]
            dv_ref[...] = dv_acc[...]

    grid_spec = pltpu.PrefetchScalarGridSpec(
        num_scalar_prefetch=2, grid=(HEADS, len(jj)),
        in_specs=[pl.BlockSpec((t, HEAD_PAD), lambda h, n, jr, ir: (ir[n], h)),
                  pl.BlockSpec((t, HEAD_PAD), lambda h, n, jr, ir: (jr[n], h)),
                  pl.BlockSpec((t, HEAD), lambda h, n, jr, ir: (jr[n], h)),
                  pl.BlockSpec((t, HEAD), lambda h, n, jr, ir: (ir[n], h)),
                  pl.BlockSpec((t, HEAD), lambda h, n, jr, ir: (ir[n], h)),
                  pl.BlockSpec((None, t, 1), lambda h, n, jr, ir: (h, ir[n], 0))],
        out_specs=[pl.BlockSpec((s, HEAD_PAD), lambda h, n, jr, ir: (0, h)),
                   pl.BlockSpec((t, HEAD_PAD), lambda h, n, jr, ir: (jr[n], h)),
                   pl.BlockSpec((t, HEAD), lambda h, n, jr, ir: (jr[n], h))],
        scratch_shapes=[pltpu.VMEM((t, HEAD_PAD), F32), pltpu.VMEM((t, HEAD), F32)])
    return pl.pallas_call(
        body, name=name, grid_spec=grid_spec,
        out_shape=[jax.ShapeDtypeStruct((s, HEADS * HEAD_PAD), F32)] * 2 + [jax.ShapeDtypeStruct((s, HEADS * HEAD), F32)],
        compiler_params=_params(("parallel", "arbitrary")),
    )(jj, ii, q, k, v, do, o, lse)


def _rope_tables(positions):
    half = ROPE // 2
    inv_freq = ROPE_BASE ** (-jnp.arange(half, dtype=F32) / half)
    ang = positions.astype(F32)[:, None] * inv_freq
    cos, sin = jnp.cos(ang), jnp.sin(ang)
    return jnp.concatenate([cos] * 4, axis=1), jnp.concatenate([-sin, sin] * 2, axis=1)


def _loss_and_grad(y, target, name):
    s = y.shape[0]
    ts = min(ROW_TILE, s)

    def body(y_ref, t_ref, dy_ref, l_ref):
        e = y_ref[...] - t_ref[...]
        dy_ref[...] = e * (1.0 / D)
        part = jnp.sum(jnp.sum(e * e, axis=-1, keepdims=True) * (0.5 / D), axis=0, keepdims=True)
        part = part * jnp.ones((1, 128), F32)

        @pl.when(pl.program_id(0) == 0)
        def _():
            l_ref[...] = part

        @pl.when(pl.program_id(0) > 0)
        def _():
            l_ref[...] += part

    return pl.pallas_call(
        body, name=name, grid=(s // ts,),
        in_specs=[pl.BlockSpec((ts, D), lambda i: (i, 0))] * 2,
        out_specs=[pl.BlockSpec((ts, D), lambda i: (i, 0)), pl.BlockSpec((1, 128), lambda i: (0, 0))],
        out_shape=[jax.ShapeDtypeStruct((s, D), F32), jax.ShapeDtypeStruct((1, 128), F32)],
        compiler_params=_params(("arbitrary",)),
    )(y, target)


ANY = pl.BlockSpec(memory_space=pl.ANY)


def _all_gather(shard, name):
    def body(x_ref, out_ref, send_sems, recv_sems, local_sem):
        x, y, c = lax.axis_index("x"), lax.axis_index("y"), lax.axis_index("c")
        me, sibling = (x, y, c), (x, y, 1 - c)
        chips = [(1 - x, y), (x, 1 - y), (1 - x, 1 - y)]

        def rows(px, py, pc):
            return out_ref.at[4 * px + 2 * py + pc]

        def copy(k, block, to, src=None):
            return pltpu.make_async_remote_copy(
                src_ref=rows(*block) if src is None else src, dst_ref=rows(*block),
                send_sem=send_sems.at[k], recv_sem=recv_sems.at[k], device_id=to, device_id_type=MESH)

        mine = pltpu.make_async_copy(x_ref, rows(*me), local_sem)
        mine.start()
        first = [copy(0, me, sibling, src=x_ref)]
        first += [copy(1 + j, me, (*chip, c), src=x_ref) for j, chip in enumerate(chips)]
        for cp in first:
            cp.start()
        passed = [copy(4 + j, (*chip, c), sibling) for j, chip in enumerate(chips)]
        for j, chip in enumerate(chips):
            copy(1 + j, (*chip, c), me).wait_recv()
            passed[j].start()
        copy(0, sibling, me).wait_recv()
        for j, chip in enumerate(chips):
            copy(4 + j, (*chip, 1 - c), me).wait_recv()
        for cp in first + passed:
            cp.wait_send()
        mine.wait()

    return pl.pallas_call(
        body, name=name, out_shape=jax.ShapeDtypeStruct((N_DEV,) + shard.shape, shard.dtype),
        in_specs=[ANY], out_specs=ANY,
        scratch_shapes=[pltpu.SemaphoreType.DMA((7,)), pltpu.SemaphoreType.DMA((7,)), pltpu.SemaphoreType.DMA],
    )(shard)


def _exchange(blocks, name):
    def body(x_ref, out_ref, send_sems, recv_sems, local_sem):
        x, y, c = lax.axis_index("x"), lax.axis_index("y"), lax.axis_index("c")
        me = 4 * x + 2 * y + c
        mine = pltpu.make_async_copy(x_ref.at[me], out_ref.at[me], local_sem)
        mine.start()
        copies = []
        for k in range(1, N_DEV):
            px = 1 - x if k & 4 else x
            py = 1 - y if k & 2 else y
            pc = 1 - c if k & 1 else c
            peer = 4 * px + 2 * py + pc
            cp = pltpu.make_async_remote_copy(
                src_ref=x_ref.at[peer], dst_ref=out_ref.at[me], send_sem=send_sems.at[k - 1],
                recv_sem=recv_sems.at[k - 1], device_id=(px, py, pc), device_id_type=MESH)
            cp.start()
            copies.append((cp, pltpu.make_async_remote_copy(
                src_ref=x_ref.at[peer], dst_ref=out_ref.at[peer], send_sem=send_sems.at[k - 1],
                recv_sem=recv_sems.at[k - 1], device_id=(px, py, pc), device_id_type=MESH)))
        for cp, landing in copies:
            landing.wait_recv()
        for cp, landing in copies:
            cp.wait_send()
        mine.wait()

    return pl.pallas_call(
        body, name=name, out_shape=jax.ShapeDtypeStruct(blocks.shape, blocks.dtype),
        in_specs=[ANY], out_specs=ANY,
        scratch_shapes=[pltpu.SemaphoreType.DMA((7,)), pltpu.SemaphoreType.DMA((7,)), pltpu.SemaphoreType.DMA],
    )(blocks)


HBM = pl.BlockSpec(memory_space=pltpu.HBM)
SEM = pl.BlockSpec(memory_space=pltpu.SEMAPHORE)
EFFECT = pltpu.SideEffectType.DATAFLOW_SIDE_EFFECTING


def _peers():
    x, y, c = lax.axis_index("x"), lax.axis_index("y"), lax.axis_index("c")
    peers = []
    for k in range(1, N_DEV):
        px = 1 - x if k & 4 else x
        py = 1 - y if k & 2 else y
        pc = 1 - c if k & 1 else c
        peers.append(((px, py, pc), 4 * px + 2 * py + pc))
    return 4 * x + 2 * y + c, peers


def _send_start(srcs, name, gather):
    n = len(srcs)
    lands = [((N_DEV,) + s.shape) if gather else s.shape for s in srcs]

    def body(*refs):
        src_refs, land_refs = refs[:n], refs[n:2 * n]
        send_sems, recv_sems, token = refs[2 * n], refs[2 * n + 1], refs[-1]
        me, peers = _peers()
        for i in range(n):
            for k, (dev, idx) in enumerate(peers):
                pltpu.make_async_remote_copy(
                    src_ref=src_refs[i] if gather else src_refs[i].at[idx], dst_ref=land_refs[i].at[me],
                    send_sem=send_sems.at[7 * i + k], recv_sem=recv_sems.at[7 * i + k], device_id=dev,
                    device_id_type=MESH).start()
        token[...] = jnp.zeros_like(token)

    res = pl.pallas_call(
        body, name=name,
        out_shape=(pltpu.SemaphoreType.DMA((7 * n,)), pltpu.SemaphoreType.DMA((7 * n,)),
                   *[pltpu.HBM(s.shape, s.dtype) for s in srcs],
                   *[pltpu.HBM(shape, s.dtype) for shape, s in zip(lands, srcs)],
                   jax.ShapeDtypeStruct((8, 128), F32)),
        in_specs=(HBM,) * (2 * n), out_specs=(SEM, SEM) + (HBM,) * (2 * n) + (pl.BlockSpec(memory_space=pltpu.VMEM),),
        input_output_aliases={i: 2 + i for i in range(2 * n)},
        compiler_params=pltpu.CompilerParams(has_side_effects=EFFECT),
    )(*[pltpu.with_memory_space_constraint(s, pltpu.HBM) for s in srcs],
      *[pltpu.with_memory_space_constraint(lax.empty(shape, s.dtype), pltpu.HBM) for shape, s in zip(lands, srcs)])
    return dict(sems=res[:2], srcs=res[2:2 + n], lands=res[2 + n:2 + 2 * n], token=res[-1])


def _send_wait(handle, after, name, gather):
    n = len(handle["srcs"])

    def body(*refs):
        src_refs, land_refs = refs[:n], refs[n:2 * n]
        send_sems, recv_sems = refs[2 * n], refs[2 * n + 1]
        me, peers = _peers()
        for i in range(n):
            for k, (dev, idx) in enumerate(peers):
                cp = pltpu.make_async_remote_copy(
                    src_ref=src_refs[i] if gather else src_refs[i].at[idx], dst_ref=land_refs[i].at[idx],
                    send_sem=send_sems.at[7 * i + k], recv_sem=recv_sems.at[7 * i + k], device_id=dev,
                    device_id_type=MESH)
                cp.wait_send()
                cp.wait_recv()

    both = list(handle["srcs"]) + list(handle["lands"])
    res = pl.pallas_call(
        body, name=name, out_shape=tuple(pltpu.HBM(t.shape, t.dtype) for t in both),
        in_specs=(HBM,) * (2 * n) + (SEM, SEM, pl.BlockSpec(memory_space=pl.ANY)), out_specs=(HBM,) * (2 * n),
        input_output_aliases={i: i for i in range(2 * n)},
        compiler_params=pltpu.CompilerParams(has_side_effects=EFFECT),
    )(*both, *handle["sems"], after)
    return res[:n], res[n:]


def _adamw(parts, w, m, v, name, tr=128):
    pieces = len(parts)
    n, r, wd = parts[0].shape
    tr = next((t for t in (tr, 64, 32, 16) if r % t == 0), r)
    nrt = r // tr

    def body(*refs):
        w_ref, m_ref, v_ref, g_ref, d_ref, nm_ref, nv_ref = refs[pieces:]

        def update(p_ref):
            g = p_ref[0].astype(F32)
            for k in range(1, n):
                g = g + p_ref[k].astype(F32)
            m_new = B1 * m_ref[...] + (1.0 - B1) * g
            v_new = B2 * v_ref[...] + (1.0 - B2) * (g * g)
            m_hat = m_new / (1.0 - B1 ** STEP)
            v_hat = v_new / (1.0 - B2 ** STEP)
            g_ref[...] = g
            d_ref[...] = -LR * (m_hat / (jnp.sqrt(v_hat) + ADAM_EPS) + WD * w_ref[...])
            nm_ref[...] = m_new
            nv_ref[...] = v_new

        for p in range(pieces):
            pl.when(pl.program_id(0) == p)(functools.partial(update, refs[p]))

    part_spec = lambda p: pl.BlockSpec((n, tr, wd), lambda l, i: (0, jnp.clip(i + (l - p) * nrt, 0, nrt - 1), 0))
    blk = pl.BlockSpec((tr, wd), lambda l, i: (l * nrt + i, 0))
    return pl.pallas_call(
        body, name=name, grid=(pieces, nrt),
        in_specs=[part_spec(p) for p in range(pieces)] + [blk, blk, blk],
        out_specs=[blk] * 4, out_shape=[jax.ShapeDtypeStruct((pieces * r, wd), F32)] * 4,
        compiler_params=_params(("arbitrary", "arbitrary")),
    )(*parts, w, m, v)


def _outer8(ct, dm, name):
    k, n = ct.shape[0], dm.shape[1]

    def body(c_ref, d_ref, o_ref):
        cv, dv = c_ref[...], d_ref[...]
        acc = cv[:, 0:1] * dv[0:1, :]
        for s in range(1, N_DEV):
            acc = acc + cv[:, s:s + 1] * dv[s:s + 1, :]
        o_ref[...] = acc

    tk = 256
    return pl.pallas_call(
        body, name=name, grid=(k // tk,),
        in_specs=[pl.BlockSpec((tk, N_DEV), lambda i: (i, 0)), pl.BlockSpec((N_DEV, n), lambda i: (0, 0))],
        out_specs=pl.BlockSpec((tk, n), lambda i: (i, 0)), out_shape=jax.ShapeDtypeStruct((k, n), F32),
        compiler_params=_params(("parallel",)),
    )(ct, dm)


FULL = (0, D)
C128 = (0, 128)
HEAD_NOPE = [(h * HEAD_PAD, NOPE) for h in range(HEADS)]
HEAD_ROPE = [(h * HEAD_PAD + NOPE, 128) for h in range(HEADS)]
HEAD_ALL = [(h * HEAD_PAD, HEAD_PAD) for h in range(HEADS)]
HEAD_V = [(h * HEAD, HEAD) for h in range(HEADS)]


def _modulate(x, p, ties=()):
    return _rowwise_fwd(_modulate_fn, [(x, FULL)], [p["gain"], p["scale"], p["shift"]], [(D, BF16, FULL)], "modulate",
                        ties=ties)[0]


def _residual_bwd(y, gm, dxn):
    return _rowwise_bwd(_gate_only_fn, [(y, FULL)], [gm], [(D, F32, FULL)], [dxn], [(0, FULL)], [(D, BF16)],
                        "residual_bwd")


def _modulate_bwd(x, p, dh, dx_in, prev=None):
    pars = [p["gain"], p["scale"], p["shift"]]
    if prev is None:
        return list(_rowwise_bwd(_modulate_fn, [(x, FULL)], pars, [(D, BF16, FULL)], [dh], [(0, FULL)], [(D, F32)],
                                 "modulate_bwd", add={0: dx_in})) + [None]
    s = x.shape[0]
    ts = min(ROW_TILE, s)

    def body(x_ref, g_ref, sc_ref, sh_ref, dh_ref, din_ref, y_ref, gm_ref, dx_ref, dy_ref, dg_ref, dsc_ref, dsh_ref,
             dgm_ref):
        _, vjp = jax.vjp(lambda *t: _modulate_fn(0, *t)[0], x_ref[...], g_ref[...], sc_ref[...], sh_ref[...])
        dxm, dg, dsc, dsh = vjp(dh_ref[...])
        dx = dxm + din_ref[...]
        dx_ref[...] = dx
        dy_ref[...] = (gm_ref[...] * dx).astype(BF16)
        sums = (dg, dsc, dsh, jnp.sum(dx * y_ref[...], axis=0, keepdims=True))
        first = pl.program_id(0) == 0
        for ref, val in zip((dg_ref, dsc_ref, dsh_ref, dgm_ref), sums):
            @pl.when(first)
            def _(ref=ref, val=val):
                ref[...] = val

            @pl.when(jnp.logical_not(first))
            def _(ref=ref, val=val):
                ref[...] += val

    blk = pl.BlockSpec((ts, D), lambda i: (i, 0))
    vec = pl.BlockSpec((1, D), lambda i: (0, 0))
    dx, dy, dg, dsc, dsh, dgm = pl.pallas_call(
        body, name="modulate_bwd_chain", grid=(s // ts,),
        in_specs=[blk, vec, vec, vec, blk, blk, blk, vec], out_specs=[blk, blk, vec, vec, vec, vec],
        out_shape=[jax.ShapeDtypeStruct((s, D), F32), jax.ShapeDtypeStruct((s, D), BF16)]
        + [jax.ShapeDtypeStruct((1, D), F32)] * 4,
        compiler_params=_params(("arbitrary",)),
    )(x, *pars, dh, dx_in, prev[0], prev[1])
    return [dx, dg, dsc, dsh, (dy, dgm)]


def _ffn_fwd(x, p, ties=()):
    h = _modulate(x, p, ties)
    gate, up, act = _ffn_in(h, p["wg"], p["wu"], "ffn_in")
    y, xn = _matmul([(act, p["wo"])], "nn", "ffn_out", resid=(x, p["gm"]))
    return xn, dict(x=x, h=h, gate=gate, up=up, act=act, y=y)


def _ffn_bwd(t, p, dxn, res=None, prev=None, ties=()):
    dy, dgm = res or _residual_bwd(t["y"], p["gm"], dxn)
    dgate, dup = _ffn_bwd_act(dy, p["wo"], t["gate"], t["up"], "ffn_bwd_act", ties=ties)
    dwo = _mm(t["act"], dy, "tn", "ffn_dwo", ties=ties)
    dh = _matmul([(dgate, p["wg"]), (dup, p["wu"])], "nt", "ffn_dh")
    dwg = _mm(t["h"], dgate, "tn", "ffn_dwi")
    dwu = _mm(t["h"], dup, "tn", "ffn_dwi")
    dx, dgain, dscale, dshift, res_prev = _modulate_bwd(t["x"], p, dh, dxn, prev)
    return dx, dict(gain=dgain, scale=dscale, shift=dshift, gm=dgm, wg=dwg, wu=dwu, wo=dwo), res_prev


def _pad128(t):
    return jnp.pad(t, ((0, 0), (0, 128 - t.shape[1])))


def _gdn_fwd(x, p, ties=()):
    s = x.shape[0]
    h = _modulate(x, p, ties)
    pm = _mm(h, p["w_main"], "nn", "gdn_proj")
    tail = _mm(h, p["w_tail"], "nn", "gdn_proj_tail")
    qkv = _gdn_conv_fwd(pm, p["conv_w"], "gdn_conv")
    beta, gcum = _rowwise_fwd(_gdn_gates_fn, [(tail, C128), (tail, (128, 128))], [p["a_log"], p["dt_bias"]],
                              [(128, F32, C128)] * 2, "gdn_gates")
    grow = gcum[:, :HEADS].reshape(s // CHUNK, CHUNK, N_GROUPS, GROUP).transpose(0, 2, 3, 1)
    grow = grow.reshape(s // CHUNK, N_GROUPS, 1, GROWS)
    o, states, invs = _gdn_scan_fwd(qkv, beta, gcum, grow, "gdn_scan")
    on, = _rowwise_fwd(_gdn_outnorm_fn, [(o, HEAD_V), (pm, [(3 * D + h_ * HEAD, HEAD) for h_ in range(HEADS)])],
                       [p["norm_g"]], [(D, BF16, HEAD_V)], "gdn_outnorm", groups=HEADS)
    y, xn = _matmul([(on, p["w_out"])], "nn", "mix_out", resid=(x, p["gm"]))
    t = dict(x=x, h=h, pm=pm, tail=tail, qkv=qkv, beta=beta, gcum=gcum, grow=grow, o=o, states=states, invs=invs,
             on=on, y=y)
    return xn, t


def _gdn_bwd(t, p, dxn, res=None, prev=None, ties=()):
    s = dxn.shape[0]
    zc = [(3 * D + h_ * HEAD, HEAD) for h_ in range(HEADS)]
    dy, dgm = res or _residual_bwd(t["y"], p["gm"], dxn)
    dw_out = _mm(t["on"], dy, "tn", "mix_dwo", ties=ties)
    don = _mm(dy, p["w_out"], "nt", "mix_dout", ties=ties)
    do, dz, dnorm_g = _rowwise_bwd(_gdn_outnorm_fn, [(t["o"], HEAD_V), (t["pm"], zc)], [p["norm_g"]],
                                   [(D, BF16, HEAD_V)], [don], [(0, HEAD_V), (1, HEAD_V)], [(D, F32), (D, F32)],
                                   "gdn_outnorm_bwd", groups=HEADS)
    dq, dk, dv, dbeta, dg, dgr = _gdn_scan_bwd(t["qkv"], t["beta"], t["gcum"], t["grow"], t["states"], t["invs"], do,
                                               "gdn_scan_bwd")
    dg = dg + _pad128(dgr.reshape(s // CHUNK, N_GROUPS, GROUP, CHUNK).transpose(0, 3, 1, 2).reshape(s, HEADS))
    dtail, da_log, ddt = _rowwise_bwd(_gdn_gates_fn, [(t["tail"], C128), (t["tail"], (128, 128))],
                                      [p["a_log"], p["dt_bias"]], [(128, F32, C128)] * 2, [dbeta, dg],
                                      [(0, C128), (0, (128, 128))], [(256, F32)], "gdn_gates_bwd")
    dxs, dcw = [], []
    for part, d in enumerate((dq, dk, dv)):
        dx_, dw_ = _gdn_conv_bwd(t["pm"], p["conv_w"], d, part, "gdn_conv_bwd")
        dxs.append(dx_)
        dcw.append(dw_)
    pieces = dxs + [dz]
    dh = _matmul([(d, p["w_main"]) for d in pieces] + [(dtail, p["w_tail"])], "nt", "gdn_dh",
                 boffs=[0, D, 2 * D, 3 * D, 0], tk=512)
    dw_main = [_mm(t["h"], d, "tn", "gdn_dwi") for d in pieces]
    dw_tail = _mm(t["h"], dtail, "tn", "gdn_dwi_tail")
    dx, dgain, dscale, dshift, res_prev = _modulate_bwd(t["x"], p, dh, dxn, prev)
    return dx, dict(gain=dgain, scale=dscale, shift=dshift, gm=dgm, w_main=jnp.concatenate(dw_main, axis=1),
                    w_tail=dw_tail, conv_w=jnp.concatenate(dcw, axis=1), a_log=da_log, dt_bias=ddt,
                    norm_g=dnorm_g, w_out=dw_out), res_prev


def _q_rows(q2, cosf, sins):
    return [(q2, HEAD_NOPE), (q2, HEAD_ROPE), (cosf, C128), (sins, C128)]


def _mla_fwd(x, p, kv, ties=()):
    h = _modulate(x, p, ties)
    cq = _mm(h, p["w_dq"], "nn", "mla_dq")
    cqn, = _rowwise_fwd(_rms_fn, [(cq, (0, Q_LORA))], [p["q_lora_g"]], [(Q_LORA, BF16, (0, Q_LORA))], "mla_qlora_norm")
    q2 = _mm(cqn, p["w_uq"], "nn", "mla_uq")
    qn, = _rowwise_fwd(_q_norm_rope_fn, _q_rows(q2, kv["cosf"], kv["sins"]), [p["q_gn"], p["q_gr"]],
                       [(HEADS * HEAD_PAD, BF16, HEAD_ALL)], "mla_q_norm", groups=HEADS)
    o, lse = _attn_fwd(qn, kv["kn"], kv["vb"], "mla_attn")
    y, xn = _matmul([(o, p["w_out"])], "nn", "mix_out", resid=(x, p["gm"]))
    return xn, dict(x=x, h=h, cq=cq, cqn=cqn, q2=q2, qn=qn, o=o, lse=lse, y=y)


def _mla_bwd(t, p, kv, dxn, res=None, prev=None, ties=()):
    dy, dgm = res or _residual_bwd(t["y"], p["gm"], dxn)
    dw_out = _mm(t["o"], dy, "tn", "mix_dwo", ties=ties)
    do = _mm(dy, p["w_out"], "nt", "mix_dout", ties=ties)
    dq, dk, dv = _attn_bwd(t["qn"], kv["kn"], kv["vb"], do, t["o"], t["lse"], "mla_attn_bwd")
    dq2, dq_gn, dq_gr = _rowwise_bwd(_q_norm_rope_fn, _q_rows(t["q2"], kv["cosf"], kv["sins"]), [p["q_gn"], p["q_gr"]],
                                     [(HEADS * HEAD_PAD, BF16, HEAD_ALL)], [dq],
                                     [(0, HEAD_NOPE), (0, HEAD_ROPE), None, None], [(HEADS * HEAD_PAD, F32)],
                                     "mla_q_norm_bwd", groups=HEADS)
    dw_uq = _mm(t["cqn"], dq2, "tn", "mla_dwuq")
    dcqn = _mm(dq2, p["w_uq"], "nt", "mla_dcq")
    dcq, dq_lora_g = _rowwise_bwd(_rms_fn, [(t["cq"], (0, Q_LORA))], [p["q_lora_g"]], [(Q_LORA, BF16, (0, Q_LORA))],
                                  [dcqn], [(0, (0, Q_LORA))], [(Q_LORA, F32)], "mla_qlora_norm_bwd")
    dw_dq = _mm(t["h"], dcq, "tn", "mla_dwdq")
    dh = _mm(dcq, p["w_dq"], "nt", "mla_dh")
    dx, dgain, dscale, dshift, res_prev = _modulate_bwd(t["x"], p, dh, dxn, prev)
    grads = dict(gain=dgain, scale=dscale, shift=dshift, gm=dgm, w_dq=dw_dq, q_lora_g=dq_lora_g, w_uq=dw_uq,
                 q_gn=dq_gn, q_gr=dq_gr, w_out=dw_out)
    return dx, grads, res_prev, dk, dv


def _k_rows(kvp, ckv, cosf, sins):
    return [(kvp, HEAD_NOPE), (kvp, HEAD_ROPE), (ckv, (KV_LORA, 128)), (cosf, C128), (sins, C128)]


def _kv_fwd(x, p, cosf, sins):
    h = _modulate(x, p)
    ckv = _mm(h, p["w_dkv"], "nn", "kv_down")
    lat, = _rowwise_fwd(_rms_fn, [(ckv, (0, KV_LORA))], [p["kv_g"]], [(KV_LORA, BF16, (0, KV_LORA))], "kv_norm")
    kvp = _mm(lat, p["w_ukv"], "nn", "kv_up")
    kn, vb = _rowwise_fwd(_k_norm_rope_fn, _k_rows(kvp, ckv, cosf, sins), [p["k_gn"], p["k_gr"]],
                          [(HEADS * HEAD_PAD, BF16, HEAD_ALL), (HEADS * HEAD, BF16, HEAD_V)], "kv_k_norm",
                          groups=HEADS)
    return dict(x=x, h=h, ckv=ckv, lat=lat, kvp=kvp, kn=kn, vb=vb, cosf=cosf, sins=sins)


def _kv_bwd(t, p, dk, dv, dx_in, prev):
    dkvp, drope, dk_gn, dk_gr = _rowwise_bwd(
        _k_norm_rope_fn, _k_rows(t["kvp"], t["ckv"], t["cosf"], t["sins"]), [p["k_gn"], p["k_gr"]],
        [(HEADS * HEAD_PAD, BF16, HEAD_ALL), (HEADS * HEAD, BF16, HEAD_V)], [dk, dv],
        [(0, HEAD_NOPE), (0, HEAD_ROPE), (1, C128), None, None], [(HEADS * HEAD_PAD, F32), (128, F32)],
        "kv_k_norm_bwd", groups=HEADS)
    dw_ukv = _mm(t["lat"], dkvp, "tn", "kv_dwukv")
    dlat = _mm(dkvp, p["w_ukv"], "nt", "kv_dlat")
    dckv, dkv_g = _rowwise_bwd(_rms_fn, [(t["ckv"], (0, KV_LORA))], [p["kv_g"]], [(KV_LORA, BF16, (0, KV_LORA))],
                               [dlat], [(0, (0, KV_LORA))], [(KV_LORA, F32)], "kv_norm_bwd")
    dw_dkv = jnp.concatenate([_mm(t["h"], dckv, "tn", "kv_dwdkv"), _mm(t["h"], drope, "tn", "kv_dwdkv_rope")], axis=1)
    dh = _matmul([(dckv, p["w_dkv"]), (drope, p["w_dkv"])], "nt", "kv_dh", boffs=[0, KV_LORA])
    dx, dgain, dscale, dshift, res_prev = _modulate_bwd(t["x"], p, dh, dx_in, prev)
    return dx, dict(gain=dgain, scale=dscale, shift=dshift, w_dkv=dw_dkv, kv_g=dkv_g, w_ukv=dw_ukv, k_gn=dk_gn,
                    k_gr=dk_gr), res_prev


WEIGHTS = ["ada_w", "ada_b", "norm_g", "ffn_w_in", "ffn_w_out", "gdn_w_in", "gdn_conv_w", "gdn_a_log", "gdn_dt_bias",
           "gdn_norm_g", "gdn_w_out", "kv_ada_w", "kv_ada_b", "kv_norm_g", "mla_w_dkv", "mla_kv_norm_g", "mla_w_ukv",
           "mla_k_norm_g", "mla_w_dq", "mla_q_lora_norm_g", "mla_w_uq", "mla_q_norm_g", "mla_w_out"]
SMALL = [("ada_b", 4 * N_MOD * D), ("kv_ada_b", 2 * D), ("norm_g", DEPTH * 3 * D), ("gdn_conv_w", N_A * CONV_K * 3 * D),
         ("gdn_a_log", N_A * HEADS), ("gdn_dt_bias", N_A * HEADS), ("gdn_norm_g", N_A * HEAD), ("kv_norm_g", D),
         ("mla_kv_norm_g", KV_LORA), ("mla_k_norm_g", QK_HEAD), ("mla_q_lora_norm_g", 2 * Q_LORA),
         ("mla_q_norm_g", 2 * QK_HEAD)]
SMALL_REPLICATED = [n for n, _ in SMALL if n not in ("norm_g", "gdn_conv_w")]


def _silu_fn(g, t):
    return (_silu(t),)


def _dup_rope(t):
    return jnp.concatenate([t[..., :NOPE], t[..., NOPE:], t[..., NOPE:]], axis=-1)


def _fold_rope(t):
    return jnp.concatenate([t[..., :NOPE], t[..., NOPE:QK_HEAD] + t[..., QK_HEAD:]], axis=-1)


def _pack(pieces, rows):
    flat = jnp.concatenate([p.reshape(-1).astype(F32) for p in pieces])
    return jnp.pad(flat, (0, rows * 128 - flat.shape[0])).reshape(rows, 128)


def _step(a):
    me = 4 * lax.axis_index("x") + 2 * lax.axis_index("y") + lax.axis_index("c")
    x = a["x"][0]
    cosf, sins = _rope_tables(a["positions"][0])

    n_in = 2 * D_FF // N_DEV
    n_gdn = (4 * D + 2 * HEADS) // N_DEV
    AHEAD = 2

    stages = [(l, part) for l in range(DEPTH) for part in range(3)]

    def stage_shards(l, part):
        if part != 1:
            sh = {"ffn_w_in": a["ffn_w_in"][l, part // 2], "ffn_w_out": a["ffn_w_out"][l, part // 2]}
            if part == 2 and l == N_A - 1:
                sh.update(mla_w_dkv=a["mla_w_dkv"], mla_w_ukv=a["mla_w_ukv"])
            return sh
        if l < N_A:
            return {"gdn_w_in": a["gdn_w_in"][l], "gdn_w_out": a["gdn_w_out"][l]}
        j = l - N_A
        return {"mla_w_dq": a["mla_w_dq"][j], "mla_w_uq": a["mla_w_uq"][j], "mla_w_out": a["mla_w_out"][j]}

    def zero_of(t):
        return jnp.minimum(jnp.abs(t[(0,) * t.ndim].astype(F32)), 0.0)

    def start_stage(l, part, tie):
        sh = stage_shards(l, part)
        return list(sh), _send_start([(w + tie).astype(BF16) for w in sh.values()], f"fetch_start_{l}_{part}", gather=True)

    def finish_stage(l, part, names, handle, after):
        srcs, lands = _send_wait(handle, after, f"fetch_wait_{l}_{part}", gather=True)
        return {n: lax.dynamic_update_slice(land, src[None], (me, 0, 0)) for n, src, land in zip(names, srcs, lands)}

    pending = [start_stage(0, 0, 0.0)]
    for l, part in stages[1:AHEAD]:
        pending.append(start_stage(l, part, pending[-1][1]["token"][0, 0]))

    n_cw, n_ng = N_A * CONV_K * 3 * HEAD, DEPTH * 3 * HEAD
    small_local = _pack([a["gdn_conv_w"], a["norm_g"], a["c"]], 44) + pending[-1][1]["token"][0, 0]
    small_all = _all_gather(small_local, "gather_small").reshape(N_DEV, -1)
    conv_w = small_all[:, :n_cw].reshape(N_DEV, N_A, CONV_K, 3 * HEAD).transpose(1, 2, 0, 3).reshape(N_A, CONV_K, 3 * D)
    norm_g = small_all[:, n_cw:n_cw + n_ng].reshape(N_DEV, DEPTH, 3, HEAD).transpose(1, 2, 0, 3).reshape(DEPTH, 3, D)
    c_all = small_all[:, n_cw + n_ng:n_cw + n_ng + D]

    c_act, = _rowwise_fwd(_silu_fn, [(c_all, FULL)], [], [(D, F32, FULL)], "c_act")
    n_ada = N_MOD * D // N_DEV
    parts = [_mm(c_act, a["ada_w"][l], "nn", "mod_proj") for l in range(DEPTH)]
    parts.append(_mm(c_act, a["kv_ada_w"], "nn", "mod_proj_kv"))
    mod_recv = _exchange(jnp.concatenate(parts, axis=1)[:, None, :], "exchange_mod")[:, 0]
    mod = mod_recv[:, :DEPTH * n_ada].reshape(N_DEV, DEPTH, n_ada).transpose(1, 0, 2).reshape(DEPTH, N_MOD * D)
    mod = (mod + a["ada_b"]).reshape(DEPTH, N_MOD, D)
    kvmod = mod_recv[:, DEPTH * n_ada:].reshape(2 * D) + a["kv_ada_b"]

    def row(v):
        return v[None]

    def ffn_params(l, i, w):
        w_in = w["ffn_w_in"]
        k = 0 if i == 0 else 6
        return dict(gain=row(norm_g[l, 0 if i == 0 else 2]), shift=row(mod[l, k]), scale=row(mod[l, k + 1]),
                    gm=0.5 * row(mod[l, k + 2]),
                    wg=w_in[:N_DEV // 2].transpose(1, 0, 2).reshape(D, D_FF),
                    wu=w_in[N_DEV // 2:].transpose(1, 0, 2).reshape(D, D_FF),
                    wo=w["ffn_w_out"].reshape(D_FF, D))

    def gdn_params(l, w):
        w_in = w["gdn_w_in"].transpose(1, 0, 2).reshape(D, 4 * D + 2 * HEADS)
        pad = lambda t: jnp.pad(t, ((0, 0), (0, 128 - HEADS)))
        return dict(gain=row(norm_g[l, 1]), shift=row(mod[l, 3]), scale=row(mod[l, 4]), gm=row(mod[l, 5]),
                    w_main=w_in[:, :4 * D],
                    w_tail=jnp.concatenate([pad(w_in[:, 4 * D:4 * D + HEADS]), pad(w_in[:, 4 * D + HEADS:])], axis=1),
                    conv_w=conv_w[l], a_log=_pad128(row(a["gdn_a_log"][l])), dt_bias=_pad128(row(a["gdn_dt_bias"][l])),
                    norm_g=row(a["gdn_norm_g"][l]), w_out=w["gdn_w_out"].reshape(D, D))

    def mla_params(l, w):
        j = l - N_A
        uq = w["mla_w_uq"].transpose(1, 0, 2)
        qg = _dup_rope(a["mla_q_norm_g"][j])
        return dict(gain=row(norm_g[l, 1]), shift=row(mod[l, 3]), scale=row(mod[l, 4]), gm=row(mod[l, 5]),
                    w_dq=w["mla_w_dq"].reshape(D, Q_LORA), q_lora_g=row(a["mla_q_lora_norm_g"][j]),
                    w_uq=_dup_rope(uq).reshape(Q_LORA, HEADS * HEAD_PAD), q_gn=row(qg[:NOPE]), q_gr=row(qg[NOPE:]),
                    w_out=w["mla_w_out"].reshape(D, D))

    def kv_params(w):
        w_dkv = w["mla_w_dkv"].reshape(D, KV_LORA + ROPE)
        kg = _dup_rope(a["mla_k_norm_g"])
        return dict(gain=row(a["kv_norm_g"]), shift=row(kvmod[:D]), scale=row(kvmod[D:]),
                    w_dkv=jnp.concatenate([w_dkv, w_dkv[:, KV_LORA:]], axis=1), kv_g=row(a["mla_kv_norm_g"]),
                    w_ukv=w["mla_w_ukv"].transpose(1, 0, 2).reshape(KV_LORA, HEADS * 2 * HEAD), k_gn=row(kg[:NOPE]),
                    k_gr=row(kg[NOPE:]))

    tapes, kv, kv_p = [[] for _ in range(DEPTH)], None, None
    for n, (l, part) in enumerate(stages):
        names, handle = pending.pop(0)
        w = finish_stage(l, part, names, handle, x if n else mod)
        ties = ()
        if n + AHEAD < len(stages):
            pending.append(start_stage(*stages[n + AHEAD], zero_of(w[names[0]])))
            ties = (pending[-1][1]["token"],)
        if part != 1:
            p = ffn_params(l, part // 2, w)
            x, t = _ffn_fwd(x, p, ties)
        else:
            p = gdn_params(l, w) if l < N_A else mla_params(l, w)
            x, t = _gdn_fwd(x, p, ties) if l < N_A else _mla_fwd(x, p, kv, ties)
        tapes[l] += [p, t]
        if part == 2 and l == N_A - 1:
            kv_p = kv_params(w)
            kv = _kv_fwd(x, kv_p, cosf, sins)
    dx, loss_blk = _loss_and_grad(x, a["loss_target"][0], "loss")
    loss = lax.psum(loss_blk[0, 0], ("x", "y", "c"))

    def by_cols(g, n):
        return g.reshape(g.shape[0], -1, n).transpose(1, 0, 2)

    def ffn_blocks(g):
        return {"ffn_w_in": jnp.concatenate([by_cols(g["wg"], n_in), by_cols(g["wu"], n_in)], axis=0),
                "ffn_w_out": g["wo"].reshape(N_DEV, D_FF // N_DEV, D)}

    def mixer_blocks(l, g):
        if l < N_A:
            full = jnp.concatenate([g["w_main"], g["w_tail"][:, :HEADS], g["w_tail"][:, 128:128 + HEADS]], axis=1)
            return {"gdn_w_in": by_cols(full, n_gdn), "gdn_w_out": g["w_out"].reshape(N_DEV, D // N_DEV, D)}
        return {"mla_w_dq": g["w_dq"].reshape(N_DEV, D // N_DEV, Q_LORA),
                "mla_w_uq": _fold_rope(g["w_uq"].reshape(Q_LORA, HEADS, HEAD_PAD)).transpose(1, 0, 2),
                "mla_w_out": g["w_out"].reshape(N_DEV, D // N_DEV, D)}

    sent = []

    def send(key, blocks):
        handle = _send_start([b.astype(BF16) for b in blocks.values()], "grad_start_" + "_".join(map(str, key)),
                             gather=False)
        sent.append((key, list(blocks), handle))
        return (handle["token"],)

    grads = [None] * DEPTH
    dk_sum = dv_sum = kv_grads = res = None
    ties = ()
    for l in reversed(range(DEPTH)):
        p1, t1, pm_, tm_, p2, t2 = tapes[l]
        if l == N_A - 1:
            dx, kv_grads, res = _kv_bwd(kv, kv_p, dk_sum, dv_sum, dx, (t2["y"], p2["gm"]))
            d_dkv = kv_grads["w_dkv"]
            ties += send((l, 3), {
                "mla_w_dkv": jnp.concatenate(
                    [d_dkv[:, :KV_LORA], d_dkv[:, KV_LORA:KV_LORA + ROPE] + d_dkv[:, KV_LORA + ROPE:]],
                    axis=1).reshape(N_DEV, D // N_DEV, KV_LORA + ROPE),
                "mla_w_ukv": by_cols(kv_grads["w_ukv"], 2 * HEAD)})
        dx, g2, res = _ffn_bwd(t2, p2, dx, res, (tm_["y"], pm_["gm"]), ties)
        ties = send((l, 2), ffn_blocks(g2))
        if l < N_A:
            dx, gm_, res = _gdn_bwd(tm_, pm_, dx, res, (t1["y"], p1["gm"]), ties)
        else:
            dx, gm_, res, dk, dv = _mla_bwd(tm_, pm_, kv, dx, res, (t1["y"], p1["gm"]), ties)
            dk_sum = dk if dk_sum is None else dk_sum + dk
            dv_sum = dv if dv_sum is None else dv_sum + dv
        ties = send((l, 1), mixer_blocks(l, gm_))
        prev = (tapes[l - 1][5]["y"], tapes[l - 1][4]["gm"]) if l > 0 and l != N_A else None
        dx, g1, res = _ffn_bwd(t1, p1, dx, res, prev, ties)
        ties = send((l, 0), ffn_blocks(g1))
        grads[l] = (g1, gm_, g2)

    out = {}
    def dmod(l):
        g1, gm_, g2 = grads[l]
        return jnp.concatenate([g1["shift"], g1["scale"], 0.5 * g1["gm"], gm_["shift"], gm_["scale"], gm_["gm"],
                                g2["shift"], g2["scale"], 0.5 * g2["gm"]], axis=1)

    gdn = [grads[l][1] for l in range(N_A)]
    mla = [grads[l][1] for l in range(N_A, DEPTH)]
    small = {
        "ada_b": jnp.concatenate([dmod(l) for l in range(DEPTH)], axis=0),
        "kv_ada_b": jnp.concatenate([kv_grads["shift"], kv_grads["scale"]], axis=1),
        "norm_g": jnp.stack([jnp.concatenate([grads[l][0]["gain"], grads[l][1]["gain"], grads[l][2]["gain"]], axis=0)
                             for l in range(DEPTH)]),
        "gdn_conv_w": jnp.stack([g["conv_w"] for g in gdn]),
        "gdn_a_log": jnp.stack([g["a_log"][0, :HEADS] for g in gdn]),
        "gdn_dt_bias": jnp.stack([g["dt_bias"][0, :HEADS] for g in gdn]),
        "gdn_norm_g": jnp.stack([g["norm_g"][0] for g in gdn]),
        "kv_norm_g": kv_grads["gain"],
        "mla_kv_norm_g": kv_grads["kv_g"],
        "mla_k_norm_g": _fold_rope(jnp.concatenate([kv_grads["k_gn"], kv_grads["k_gr"]], axis=1)),
        "mla_q_lora_norm_g": jnp.stack([g["q_lora_g"][0] for g in mla]),
        "mla_q_norm_g": jnp.stack([_fold_rope(jnp.concatenate([g["q_gn"], g["q_gr"]], axis=1))[0] for g in mla]),
    }
    rows = 616
    assert sum(n for _, n in SMALL) <= rows * 128 and all(small[n].size == k for n, k in SMALL)
    small_recv = _all_gather(_pack([small[n] for n, _ in SMALL], rows) + ties[0][0, 0], "gather_small_grads")
    zero = lambda n, k: jnp.zeros((k,), F32)
    packed = {pre: _pack([a[pre + n] if n in SMALL_REPLICATED else zero(n, k) for n, k in SMALL], rows)
              for pre in ("", "m_", "v_")}
    res = _adamw([small_recv], packed[""], packed["m_"], packed["v_"], "adamw_small")
    offs = {}
    o = 0
    for n, k in SMALL:
        offs[n] = o
        o += k
    for n, k in SMALL:
        if n in SMALL_REPLICATED:
            out[n] = [r.reshape(-1)[offs[n]:offs[n] + k] for r in res]
    gsum = res[0].reshape(-1)
    g_norm = lax.dynamic_slice_in_dim(gsum[offs["norm_g"]:offs["norm_g"] + DEPTH * 3 * D].reshape(DEPTH * 3, D),
                                      me * HEAD, HEAD, axis=1)
    g_conv = lax.dynamic_slice_in_dim(
        gsum[offs["gdn_conv_w"]:offs["gdn_conv_w"] + N_A * CONV_K * 3 * D].reshape(N_A * CONV_K, 3 * D),
        me * 3 * HEAD, 3 * HEAD, axis=1)
    res2 = _adamw([_pack([g_norm, g_conv], 36)[None]], *[_pack([a[pre + "norm_g"], a[pre + "gdn_conv_w"]], 36)
                                                      for pre in ("", "m_", "v_")], "adamw_small")
    out["norm_g"] = [r.reshape(-1)[:n_ng] for r in res2]
    out["gdn_conv_w"] = [r.reshape(-1)[n_ng:n_ng + n_cw] for r in res2]

    c_act_t = c_act.T
    all_small = small_recv.reshape(N_DEV, -1)
    dmod_all = all_small[:, :DEPTH * N_MOD * D].reshape(N_DEV, DEPTH, N_MOD * D)
    dmod_mine = lax.dynamic_slice_in_dim(dmod_all, me * n_ada, n_ada, axis=2)
    g_ada = [_outer8(c_act_t, dmod_mine[:, l], "ada_grad")[None] for l in range(DEPTH)]
    out["ada_w"] = _adamw(g_ada, *[a[pre + "ada_w"].reshape(DEPTH * D, n_ada) for pre in ("", "m_", "v_")], "adamw")
    dkv_all = all_small[:, offs["kv_ada_b"]:offs["kv_ada_b"] + 2 * D]
    g_kv = _outer8(c_act_t, lax.dynamic_slice_in_dim(dkv_all, me * (2 * D // N_DEV), 2 * D // N_DEV, axis=1), "ada_grad")
    out["kv_ada_w"] = _adamw([g_kv[None]], *[a[pre + "kv_ada_w"] for pre in ("", "m_", "v_")], "adamw")

    pieces = {}
    for key, names, handle in sent:
        srcs, lands = _send_wait(handle, out["kv_ada_w"][0], "grad_wait_" + "_".join(map(str, key)), gather=False)
        for name, src, land in zip(names, srcs, lands):
            own = lax.dynamic_slice_in_dim(src, me, 1, axis=0)
            pieces.setdefault(name, []).append((key, lax.dynamic_update_slice(land, own, (me, 0, 0))))
    for name, parts in pieces.items():
        wide = a[name].shape[-1]
        out[name] = _adamw([p for _, p in sorted(parts, key=lambda kp: kp[0])], a[name].reshape(-1, wide),
                           a["m_" + name].reshape(-1, wide), a["v_" + name].reshape(-1, wide), "adamw")

    result = [loss, dx[None]]
    for k in range(4):
        result += [out[n][k].reshape(a[n].shape) for n in WEIGHTS]
    return tuple(result)


def kernel(x, c, positions, ada_w, ada_b, norm_g, ffn_w_in, ffn_w_out, gdn_w_in, gdn_conv_w, gdn_a_log, gdn_dt_bias, gdn_norm_g, gdn_w_out, kv_ada_w, kv_ada_b, kv_norm_g, mla_w_dkv, mla_kv_norm_g, mla_w_ukv, mla_k_norm_g, mla_w_dq, mla_q_lora_norm_g, mla_w_uq, mla_q_norm_g, mla_w_out, loss_target, m_ada_w, m_ada_b, m_norm_g, m_ffn_w_in, m_ffn_w_out, m_gdn_w_in, m_gdn_conv_w, m_gdn_a_log, m_gdn_dt_bias, m_gdn_norm_g, m_gdn_w_out, m_kv_ada_w, m_kv_ada_b, m_kv_norm_g, m_mla_w_dkv, m_mla_kv_norm_g, m_mla_w_ukv, m_mla_k_norm_g, m_mla_w_dq, m_mla_q_lora_norm_g, m_mla_w_uq, m_mla_q_norm_g, m_mla_w_out, v_ada_w, v_ada_b, v_norm_g, v_ffn_w_in, v_ffn_w_out, v_gdn_w_in, v_gdn_conv_w, v_gdn_a_log, v_gdn_dt_bias, v_gdn_norm_g, v_gdn_w_out, v_kv_ada_w, v_kv_ada_b, v_kv_norm_g, v_mla_w_dkv, v_mla_kv_norm_g, v_mla_w_ukv, v_mla_k_norm_g, v_mla_w_dq, v_mla_q_lora_norm_g, v_mla_w_uq, v_mla_q_norm_g, v_mla_w_out):
    return _step(dict(locals()))
```

```python
import functools
import math

import jax
import jax.numpy as jnp
from jax import lax
from jax.experimental import pallas as pl
from jax.experimental.pallas import tpu as pltpu

F32 = jnp.float32
BF16 = jnp.bfloat16

N_DEV = 8
D = 1024
D_FF = 2816
DEPTH = 4
N_A = 2
N_MOD = 9
HEADS = 8
HEAD = 128
CHUNK = 64
CONV_K = 4
KV_LORA = 256
Q_LORA = 384
NOPE = 128
ROPE = 64
QK_HEAD = NOPE + ROPE
HEAD_PAD = 256
ROPE_BASE = 10000.0
EPS = 1e-6
LR, B1, B2, ADAM_EPS, WD, STEP = 0.001, 0.9, 0.999, 1e-08, 0.01, 10

VMEM_LIMIT = 48 * 1024 * 1024
ROW_TILE = 256
MESH = pl.DeviceIdType.MESH

_NN = (((1,), (0,)), ((), ()))
_NT = (((1,), (1,)), ((), ()))
_TN = (((0,), (0,)), ((), ()))
_DIMS = {"nn": _NN, "nt": _NT, "tn": _TN}


def _params(dims=None):
    return pltpu.CompilerParams(dimension_semantics=dims, vmem_limit_bytes=VMEM_LIMIT)


def _tile(n, target):
    for t in range(target - target % 128, 0, -128):
        if n % t == 0:
            return t
    return n


_TIE_SPEC1 = pl.BlockSpec((8, 128), lambda i: (0, 0))
_TIE_SPEC2 = pl.BlockSpec((8, 128), lambda i, j: (0, 0))
_TIE_SPEC3 = pl.BlockSpec((8, 128), lambda i, j, k: (0, 0))


def _matmul(pairs, form, name, out_dtype=F32, tm=1408, tn=1408, tk=1408, boffs=None, resid=None, ties=()):
    a0, b0 = pairs[0]
    if form == "nn":
        m, n = a0.shape[0], b0.shape[1]
        ks = [a.shape[1] for a, _ in pairs]
    elif form == "nt":
        m, n = a0.shape[0], b0.shape[0]
        ks = [a.shape[1] for a, _ in pairs]
    else:
        m, n = a0.shape[1], b0.shape[1]
        ks = [a.shape[0] for a, _ in pairs]
    tm, tn = _tile(m, tm), _tile(n, tn)
    tks = [_tile(k, tk) for k in ks]
    boffs = boffs or [0] * len(pairs)
    assert m % tm == 0 and n % tn == 0 and all(o % t == 0 for o, t in zip(boffs, tks)), (name, m, n, ks)
    steps = [k // t for k, t in zip(ks, tks)]
    starts = [sum(steps[:p]) for p in range(len(pairs))]
    nk = sum(steps)

    def kidx(p, k):
        return jnp.clip(k - starts[p], 0, steps[p] - 1)

    in_specs, args = [], []
    for p, (a, b) in enumerate(pairs):
        t = tks[p]
        if form == "tn":
            in_specs.append(pl.BlockSpec((t, tm), lambda i, j, k, p=p: (kidx(p, k), i)))
            in_specs.append(pl.BlockSpec((t, tn), lambda i, j, k, p=p: (kidx(p, k), j)))
        elif form == "nn":
            in_specs.append(pl.BlockSpec((tm, t), lambda i, j, k, p=p: (i, kidx(p, k))))
            in_specs.append(pl.BlockSpec((t, tn), lambda i, j, k, p=p: (kidx(p, k), j)))
        else:
            in_specs.append(pl.BlockSpec((tm, t), lambda i, j, k, p=p: (i, kidx(p, k))))
            in_specs.append(pl.BlockSpec((tn, t), lambda i, j, k, p=p, o=boffs[p] // t: (j, kidx(p, k) + o)))
        args += [a, b]
    dims = _DIMS[form]
    npairs = len(pairs)
    nin = 2 * npairs + len(ties) + (2 if resid else 0)
    out_blk = pl.BlockSpec((tm, tn), lambda i, j, k: (i, j))
    in_specs += [_TIE_SPEC3] * len(ties)
    args += list(ties)
    if resid:
        in_specs += [out_blk, pl.BlockSpec((1, tn), lambda i, j, k: (0, j))]
        args += list(resid)

    def body(*refs):
        o_ref = refs[nin]
        k = pl.program_id(2)

        def prod(p):
            return lax.dot_general(refs[2 * p][...].astype(BF16), refs[2 * p + 1][...].astype(BF16), dims,
                                   preferred_element_type=F32)

        def finish(y):
            o_ref[...] = y.astype(o_ref.dtype)
            if resid:
                refs[nin + 1][...] = refs[nin - 2][...] + refs[nin - 1][...] * y

        if nk == 1:
            finish(prod(0))
            return
        acc = refs[-1]

        @pl.when(k == 0)
        def _():
            acc[...] = jnp.zeros_like(acc)

        for p in range(npairs):
            @pl.when((k >= starts[p]) & (k < starts[p] + steps[p]))
            def _(p=p):
                acc[...] += prod(p)

        @pl.when(k == nk - 1)
        def _():
            finish(acc[...])

    res = pl.pallas_call(
        body, name=name, grid=(m // tm, n // tn, nk), in_specs=in_specs,
        out_specs=[out_blk, out_blk] if resid else out_blk,
        out_shape=[jax.ShapeDtypeStruct((m, n), out_dtype)] * 2 if resid else jax.ShapeDtypeStruct((m, n), out_dtype),
        scratch_shapes=[] if nk == 1 else [pltpu.VMEM((tm, tn), F32)],
        compiler_params=_params(("parallel", "parallel", "arbitrary")),
    )(*args)
    return res


def _mm(a, b, form, name, **kw):
    return _matmul([(a, b)], form, name, **kw)


def _cols(spec, g):
    return spec[g] if isinstance(spec, list) else spec


def _rowwise_fwd(fn, rows, pars, outs, name, groups=1, ts=ROW_TILE, ties=()):
    s = rows[0][0].shape[0]
    ts = min(ts, s)
    assert s % ts == 0
    nr, npar = len(rows), len(pars)

    def body(*refs):
        par_t = [r[...] for r in refs[nr:nr + npar]]
        out_refs = refs[nr + npar + len(ties):]
        for g in range(groups):
            row_t = []
            for r, (_, spec) in zip(refs[:nr], rows):
                c0, w = _cols(spec, g)
                row_t.append(r[:, c0:c0 + w].astype(F32))
            res = fn(g, *row_t, *par_t)
            for o_ref, val, (_, _, spec) in zip(out_refs, res, outs):
                c0, w = _cols(spec, g)
                o_ref[:, c0:c0 + w] = val.astype(o_ref.dtype)

    return pl.pallas_call(
        body, name=name, grid=(s // ts,),
        in_specs=[pl.BlockSpec((ts, a.shape[1]), lambda i: (i, 0)) for a, _ in rows]
        + [pl.BlockSpec(p.shape, lambda i: (0, 0)) for p in pars] + [_TIE_SPEC1] * len(ties),
        out_specs=[pl.BlockSpec((ts, w), lambda i: (i, 0)) for w, _, _ in outs],
        out_shape=[jax.ShapeDtypeStruct((s, w), dt) for w, dt, _ in outs],
        compiler_params=_params(("parallel",)),
    )(*[a for a, _ in rows], *pars, *ties)


def _rowwise_bwd(fn, rows, pars, outs, douts, gmap, gshapes, name, groups=1, add=None, par_grads=True,
                 ts=ROW_TILE):
    s = rows[0][0].shape[0]
    ts = min(ts, s)
    assert s % ts == 0
    nr, npar, nout, ng = len(rows), len(pars), len(outs), len(gshapes)
    add = add or {}
    add_keys = sorted(add)

    def body(*refs):
        row_refs = refs[:nr]
        par_refs = refs[nr:nr + npar]
        dout_refs = refs[nr + npar:nr + npar + nout]
        add_refs = refs[nr + npar + nout:nr + npar + nout + len(add_keys)]
        g_refs = refs[nr + npar + nout + len(add_keys):][:ng]
        pg_refs = refs[nr + npar + nout + len(add_keys) + ng:]
        par_t = [r[...] for r in par_refs]
        par_acc = [None] * npar
        shared_acc = {}
        for g in range(groups):
            row_t = []
            for r, (_, spec) in zip(row_refs, rows):
                c0, w = _cols(spec, g)
                row_t.append(r[:, c0:c0 + w].astype(F32))
            cts = []
            for r, (_, _, spec) in zip(dout_refs, outs):
                c0, w = _cols(spec, g)
                cts.append(r[:, c0:c0 + w].astype(F32))
            _, vjp = jax.vjp(lambda *t, g=g: tuple(fn(g, *t)), *row_t, *par_t)
            grads = vjp(tuple(cts))
            for k in range(nr):
                if gmap[k] is None:
                    continue
                gi, spec = gmap[k]
                if isinstance(spec, list) or groups == 1:
                    c0, w = _cols(spec, g)
                    val = grads[k]
                    if gi in add:
                        val = val + add_refs[add_keys.index(gi)][:, c0:c0 + w].astype(F32)
                    g_refs[gi][:, c0:c0 + w] = val.astype(g_refs[gi].dtype)
                else:
                    shared_acc[k] = grads[k] if k not in shared_acc else shared_acc[k] + grads[k]
            if par_grads:
                for k in range(npar):
                    pg = grads[nr + k]
                    par_acc[k] = pg if par_acc[k] is None else par_acc[k] + pg
        for k, val in shared_acc.items():
            gi, (c0, w) = gmap[k]
            assert gi not in add
            g_refs[gi][:, c0:c0 + w] = val.astype(g_refs[gi].dtype)
        if par_grads:
            first = pl.program_id(0) == 0
            for k in range(npar):
                @pl.when(first)
                def _(k=k):
                    pg_refs[k][...] = par_acc[k]

                @pl.when(jnp.logical_not(first))
                def _(k=k):
                    pg_refs[k][...] += par_acc[k]

    out_specs = [pl.BlockSpec((ts, w), lambda i: (i, 0)) for w, _ in gshapes]
    out_shape = [jax.ShapeDtypeStruct((s, w), dt) for w, dt in gshapes]
    if par_grads:
        out_specs += [pl.BlockSpec(p.shape, lambda i: (0, 0)) for p in pars]
        out_shape += [jax.ShapeDtypeStruct(p.shape, F32) for p in pars]
    return pl.pallas_call(
        body, name=name, grid=(s // ts,),
        in_specs=[pl.BlockSpec((ts, a.shape[1]), lambda i: (i, 0)) for a, _ in rows]
        + [pl.BlockSpec(p.shape, lambda i: (0, 0)) for p in pars]
        + [pl.BlockSpec((ts, a.shape[1]), lambda i: (i, 0)) for a in douts]
        + [pl.BlockSpec((ts, add[k].shape[1]), lambda i: (i, 0)) for k in add_keys],
        out_specs=out_specs, out_shape=out_shape,
        compiler_params=_params(("arbitrary",)),
    )(*[a for a, _ in rows], *pars, *douts, *[add[k] for k in add_keys])


def _sigmoid(x):
    return 1.0 / (1.0 + jnp.exp(-x))


def _silu(x):
    return x * _sigmoid(x)


def _softplus(x):
    return jnp.maximum(x, 0.0) + jnp.log(1.0 + jnp.exp(-jnp.abs(x)))


def _rms(t, g, n=None):
    n = n or t.shape[-1]
    return t * lax.rsqrt(jnp.sum(t * t, axis=-1, keepdims=True) / n + EPS) * g


def _modulate_fn(g, x, gain, scale, shift):
    return (_rms(x, gain) * (1.0 + scale) + shift,)


def _resgate_fn(g, x, y, gm):
    return (x + gm * y,)


def _gate_only_fn(g, y, gm):
    return (gm * y,)


def _gdn_gates_fn(g, b_logit, a_logit, a_log, dt_bias):
    gate = -jnp.exp(a_log) * _softplus(a_logit + dt_bias)
    n = gate.shape[0]
    i = lax.broadcasted_iota(jnp.int32, (n, n), 0)
    j = lax.broadcasted_iota(jnp.int32, (n, n), 1)
    tri = (((i // CHUNK) == (j // CHUNK)) & (i >= j)).astype(F32)
    gcum = lax.dot_general(tri, gate, _NN, preferred_element_type=F32, precision=lax.Precision.HIGHEST)
    return _sigmoid(b_logit), gcum


def _gdn_outnorm_fn(g, o, z, gain):
    return (_rms(o, gain) * _silu(z),)


def _rms_fn(g, t, gain):
    return (_rms(t, gain),)


@jax.custom_vjp
def _swap_halves(t):
    return pltpu.roll(t, 32, 1)


_swap_halves.defvjp(lambda t: (pltpu.roll(t, 32, 1), None), lambda _, ct: (pltpu.roll(ct, 96, 1),))


def _head_norm_rope_fn(g, nope, rope, cosf, sins, gain_n, gain_r):
    first = lax.broadcasted_iota(jnp.int32, rope.shape, 1) < ROPE
    ss = jnp.sum(nope * nope, axis=-1, keepdims=True) + jnp.sum(jnp.where(first, rope * rope, 0.0), axis=-1,
                                                                 keepdims=True)
    r = lax.rsqrt(ss / QK_HEAD + EPS)
    tn = nope * r * gain_n
    tr = rope * r * gain_r
    rot = jnp.where(first, tr * cosf + _swap_halves(tr) * sins, 0.0)
    return tn, rot


def _q_norm_rope_fn(g, nope, rope, cosf, sins, gain_n, gain_r):
    tn, rot = _head_norm_rope_fn(g, nope, rope, cosf, sins, gain_n, gain_r)
    return (jnp.concatenate([tn, rot], axis=1),)


def _k_norm_rope_fn(g, nope, val, rope, cosf, sins, gain_n, gain_r):
    tn, rot = _head_norm_rope_fn(g, nope, rope, cosf, sins, gain_n, gain_r)
    return jnp.concatenate([tn, rot], axis=1), val


def _loss_fn(g, y, target):
    e = y - target
    return (jnp.sum(e * e, axis=-1, keepdims=True) * (0.5 / D) * jnp.ones((1, 128), F32),)


def _ffn_in(h, wg, wu, name, tm=512, tn=1408):
    s = h.shape[0]
    tm = min(tm, s)

    def body(h_ref, wg_ref, wu_ref, g_ref, u_ref, a_ref):
        hb = h_ref[...]
        gate = jnp.dot(hb, wg_ref[...], preferred_element_type=F32)
        up = jnp.dot(hb, wu_ref[...], preferred_element_type=F32)
        g_ref[...] = gate.astype(BF16)
        u_ref[...] = up.astype(BF16)
        a_ref[...] = (_silu(gate) * up).astype(BF16)

    spec = pl.BlockSpec((tm, tn), lambda j, i: (i, j))
    return pl.pallas_call(
        body, name=name, grid=(D_FF // tn, s // tm),
        in_specs=[pl.BlockSpec((tm, D), lambda j, i: (i, 0)), pl.BlockSpec((D, tn), lambda j, i: (0, j)),
                  pl.BlockSpec((D, tn), lambda j, i: (0, j))],
        out_specs=[spec, spec, spec], out_shape=[jax.ShapeDtypeStruct((s, D_FF), BF16)] * 3,
        compiler_params=_params(("parallel", "parallel")),
    )(h, wg, wu)


def _ffn_bwd_act(dy, wo, gate, up, name, tm=512, tn=1408, ties=()):
    s = dy.shape[0]
    tm = min(tm, s)

    def body(dy_ref, wo_ref, g_ref, u_ref, *rest):
        dg_ref, du_ref = rest[-2:]
        dact = lax.dot_general(dy_ref[...], wo_ref[...], _NT, preferred_element_type=F32)
        gate = g_ref[...].astype(F32)
        up = u_ref[...].astype(F32)
        sg = _sigmoid(gate)
        dg_ref[...] = (dact * up * (sg * (1.0 + gate * (1.0 - sg)))).astype(BF16)
        du_ref[...] = (dact * (gate * sg)).astype(BF16)

    spec = pl.BlockSpec((tm, tn), lambda j, i: (i, j))
    return pl.pallas_call(
        body, name=name, grid=(D_FF // tn, s // tm),
        in_specs=[pl.BlockSpec((tm, D), lambda j, i: (i, 0)), pl.BlockSpec((tn, D), lambda j, i: (j, 0)), spec, spec]
        + [_TIE_SPEC2] * len(ties),
        out_specs=[spec, spec], out_shape=[jax.ShapeDtypeStruct((s, D_FF), BF16)] * 2,
        compiler_params=_params(("parallel", "parallel")),
    )(dy, wo, gate, up, *ties)


def _shift_down(x, d):
    rows = lax.broadcasted_iota(jnp.int32, x.shape, 0)
    return jnp.where(rows >= d, pltpu.roll(x, d, 0), 0.0)


def _shift_up(x, d):
    n = x.shape[0]
    rows = lax.broadcasted_iota(jnp.int32, x.shape, 0)
    return jnp.where(rows < n - d, pltpu.roll(x, n - d, 0), 0.0)


def _conv_post(pre, is_qk):
    a = _silu(pre)
    l2 = a * lax.rsqrt(jnp.sum(a * a, axis=-1, keepdims=True) + EPS)
    return jnp.where(is_qk, l2, a)


def _conv_pre(x, w):
    pre = x * w[CONV_K - 1:CONV_K, :]
    for j in range(CONV_K - 1):
        pre = pre + _shift_down(x, CONV_K - 1 - j) * w[j:j + 1, :]
    return pre


def _gdn_conv_fwd(pm, conv_w, name):
    s = pm.shape[0]
    nblk = 3 * D // HEAD

    def body(x_ref, w_ref, o_ref):
        is_qk = pl.program_id(0) < 2 * HEADS
        o_ref[...] = _conv_post(_conv_pre(x_ref[...], w_ref[...]), is_qk)

    return pl.pallas_call(
        body, name=name, grid=(nblk,),
        in_specs=[pl.BlockSpec((s, HEAD), lambda c: (0, c)), pl.BlockSpec((CONV_K, HEAD), lambda c: (0, c))],
        out_specs=pl.BlockSpec((s, HEAD), lambda c: (0, c)),
        out_shape=jax.ShapeDtypeStruct((s, 3 * D), F32), compiler_params=_params(("parallel",)),
    )(pm, conv_w)


def _gdn_conv_bwd(pm, conv_w, dout, part, name):
    s = pm.shape[0]
    off = part * HEADS

    def body(x_ref, w_ref, d_ref, dx_ref, dw_ref):
        x, w = x_ref[...], w_ref[...]
        _, vjp = jax.vjp(lambda p: _conv_post(p, part < 2), _conv_pre(x, w))
        dpre, = vjp(d_ref[...])
        dx = dpre * w[CONV_K - 1:CONV_K, :]
        rows = [None] * CONV_K
        rows[CONV_K - 1] = jnp.sum(dpre * x, axis=0, keepdims=True)
        for j in range(CONV_K - 1):
            dx = dx + _shift_up(dpre, CONV_K - 1 - j) * w[j:j + 1, :]
            rows[j] = jnp.sum(dpre * _shift_down(x, CONV_K - 1 - j), axis=0, keepdims=True)
        dx_ref[...] = dx
        dw_ref[...] = jnp.concatenate(rows, axis=0)

    return pl.pallas_call(
        body, name=name, grid=(HEADS,),
        in_specs=[pl.BlockSpec((s, HEAD), lambda c: (0, c + off)), pl.BlockSpec((CONV_K, HEAD), lambda c: (0, c + off)),
                  pl.BlockSpec((s, HEAD), lambda c: (0, c))],
        out_specs=[pl.BlockSpec((s, HEAD), lambda c: (0, c)), pl.BlockSpec((CONV_K, HEAD), lambda c: (0, c))],
        out_shape=[jax.ShapeDtypeStruct((s, D), F32), jax.ShapeDtypeStruct((CONV_K, D), F32)],
        compiler_params=_params(("parallel",)),
    )(pm, conv_w, dout)


def _dot3(a, b, dims=_NN):
    ah, bh = a.astype(BF16), b.astype(BF16)
    al, bl = (a - ah.astype(F32)).astype(BF16), (b - bh.astype(F32)).astype(BF16)
    d = lambda u, v: lax.dot_general(u, v, dims, preferred_element_type=F32)
    return d(ah, bh) + (d(ah, bl) + d(al, bh))


def _make_dot(hi):
    def raw(a, b, dims):
        if hi:
            return _dot3(a, b, dims)
        return lax.dot_general(a.astype(BF16), b.astype(BF16), dims, preferred_element_type=F32)

    @functools.partial(jax.custom_vjp, nondiff_argnums=(2,))
    def dot(a, b, form):
        return raw(a, b, _DIMS[form])

    def fwd(a, b, form):
        return raw(a, b, _DIMS[form]), (a, b)

    def bwd(form, res, ct):
        a, b = res
        if form == "nn":
            return raw(ct, b, _NT), raw(a, ct, _TN)
        if form == "nt":
            return raw(ct, b, _NN), raw(ct, a, _TN)
        return raw(b, ct, _NT), raw(a, ct, _NN)

    dot.defvjp(fwd, bwd)
    return dot


_dot = _make_dot(False)
_dot_hi = _make_dot(True)


def _tri_inv_raw(low):
    n = low.shape[0]
    i = lax.broadcasted_iota(jnp.int32, (n, n), 0)
    j = lax.broadcasted_iota(jnp.int32, (n, n), 1)
    eye = (i == j).astype(F32)
    hdot = _dot3
    same16 = (i // 16) == (j // 16)
    neg = jnp.where(same16, -low, 0.0)
    inv = eye + neg
    power = neg
    for _ in range(3):
        power = hdot(power, power)
        inv = hdot(inv, eye + power)
    for blk in (32, 64):
        off = jnp.where(((i // blk) == (j // blk)) & ((i // (blk // 2)) != (j // (blk // 2))), low, 0.0)
        inv = inv - hdot(inv, hdot(off, inv))
    return inv


@jax.custom_vjp
def _tri_inv(low):
    return _tri_inv_raw(low)


def _tri_inv_fwd(low):
    inv = _tri_inv_raw(low)
    return inv, inv


def _tri_inv_bwd(inv, ct):
    return (-_dot3(_dot3(inv, ct, _TN), inv, _NT),)


_tri_inv.defvjp(_tri_inv_fwd, _tri_inv_bwd)


@jax.custom_vjp
def _tri_inv_given(low, inv):
    return inv


_tri_inv_given.defvjp(lambda low, inv: (inv, inv),
                      lambda inv, ct: (_tri_inv_bwd(inv, ct)[0], jnp.zeros_like(inv)))

GROUP = 4
N_GROUPS = HEADS // GROUP
GROWS = GROUP * CHUNK


def _gdn_group(q, k, v, beta, gc, gr, states, inv=None):
    n = q.shape[0]
    i = lax.broadcasted_iota(jnp.int32, (n, n), 0)
    j = lax.broadcasted_iota(jnp.int32, (n, n), 1)
    same = (i // CHUNK) == (j // CHUNK)
    incl, strict = same & (i >= j), same & (i > j)
    qs = q * (HEAD ** -0.5)
    decay = jnp.where(incl, jnp.exp(jnp.where(incl, gc - gr, 0.0)), 0.0)
    kb = k * beta
    eg = jnp.exp(gc)
    prod = _dot(jnp.concatenate([kb, qs], axis=0), k, "nt")
    low = jnp.where(strict, prod[:n] * decay, 0.0)
    attn = jnp.where(incl, prod[n:] * decay, 0.0)
    inv = _tri_inv(low) if inv is None else _tri_inv_given(low, inv)
    sol = _dot_hi(inv, jnp.concatenate([v * beta, kb * eg], axis=1), "nn")
    u, w, qg = sol[:, :HEAD], sol[:, HEAD:], qs * eg
    last = lax.broadcasted_iota(jnp.int32, (CHUNK, 1), 0) == CHUNK - 1
    v_new, o_state, carry = [], [], []
    for h, state in enumerate(states):
        rows = slice(h * CHUNK, (h + 1) * CHUNK)
        ws = _dot(jnp.concatenate([w[rows], qg[rows]], axis=0), state, "nn")
        v_new.append(u[rows] - ws[:CHUNK])
        o_state.append(ws[CHUNK:])
        g_last = jnp.sum(jnp.where(last, gc[rows], 0.0), axis=0, keepdims=True)
        carry.append((g_last, k[rows] * jnp.exp(g_last - gc[rows])))
    o = jnp.concatenate(o_state, axis=0) + _dot(attn, jnp.concatenate(v_new, axis=0), "nn")
    new = tuple(state * jnp.exp(g_last) + _dot(k_dec, vn, "tn")
                for state, (g_last, k_dec), vn in zip(states, carry, v_new))
    return o, new, inv


def _gdn_specs(s, rev):
    nc = s // CHUNK
    at = (lambda n: nc - 1 - n) if rev else (lambda n: n)
    return nc, at, [
        pl.BlockSpec((CHUNK, D), lambda n: (at(n), 0)), pl.BlockSpec((CHUNK, D), lambda n: (at(n), 1)),
        pl.BlockSpec((CHUNK, D), lambda n: (at(n), 2)), pl.BlockSpec((CHUNK, HEAD), lambda n: (at(n), 0)),
        pl.BlockSpec((CHUNK, HEAD), lambda n: (at(n), 0)),
        pl.BlockSpec((None, N_GROUPS, 1, GROWS), lambda n: (at(n), 0, 0, 0))]


def _group_operands(grp, q_ref, k_ref, v_ref, b_blk, gc_blk, gr_blk):
    heads = range(grp * GROUP, (grp + 1) * GROUP)
    stack = lambda ref: jnp.concatenate([ref[:, h * HEAD:(h + 1) * HEAD] for h in heads], axis=0)
    col = lambda blk: jnp.concatenate([blk[:, h:h + 1] for h in heads], axis=0)
    return stack(q_ref), stack(k_ref), stack(v_ref), col(b_blk), col(gc_blk), gr_blk[grp]


def _gdn_scan_fwd(qkv, beta, gcum, grow, name):
    s = qkv.shape[0]
    nc, _, in_specs = _gdn_specs(s, rev=False)

    def body(q_ref, k_ref, v_ref, b_ref, gc_ref, gr_ref, o_ref, st_ref, inv_ref, state):
        @pl.when(pl.program_id(0) == 0)
        def _():
            state[...] = jnp.zeros_like(state)

        b_blk, gc_blk, gr_blk = b_ref[...], gc_ref[...], gr_ref[...]
        old = [state[h] for h in range(HEADS)]
        res = [_gdn_group(*_group_operands(grp, q_ref, k_ref, v_ref, b_blk, gc_blk, gr_blk),
                          old[grp * GROUP:(grp + 1) * GROUP]) for grp in range(N_GROUPS)]
        for grp, (o, new, inv) in enumerate(res):
            inv_ref[grp] = inv
            for hh in range(GROUP):
                h = grp * GROUP + hh
                st_ref[h] = old[h]
                o_ref[:, h * HEAD:(h + 1) * HEAD] = o[hh * CHUNK:(hh + 1) * CHUNK]
                state[h] = new[hh]

    return pl.pallas_call(
        body, name=name, grid=(nc,), in_specs=in_specs,
        out_specs=[pl.BlockSpec((CHUNK, D), lambda n: (n, 0)),
                   pl.BlockSpec((None, HEADS, HEAD, HEAD), lambda n: (n, 0, 0, 0)),
                   pl.BlockSpec((None, N_GROUPS, GROWS, GROWS), lambda n: (n, 0, 0, 0))],
        out_shape=[jax.ShapeDtypeStruct((s, D), F32), jax.ShapeDtypeStruct((nc, HEADS, HEAD, HEAD), F32),
                   jax.ShapeDtypeStruct((nc, N_GROUPS, GROWS, GROWS), F32)],
        scratch_shapes=[pltpu.VMEM((HEADS, HEAD, HEAD), F32)],
        compiler_params=_params(("arbitrary",)),
    )(qkv, qkv, qkv, beta, gcum, grow)


def _gdn_scan_bwd(qkv, beta, gcum, grow, states, invs, do, name):
    s = qkv.shape[0]
    nc, at, in_specs = _gdn_specs(s, rev=True)
    in_specs += [pl.BlockSpec((None, HEADS, HEAD, HEAD), lambda n: (at(n), 0, 0, 0)),
                 pl.BlockSpec((None, N_GROUPS, GROWS, GROWS), lambda n: (at(n), 0, 0, 0)),
                 pl.BlockSpec((CHUNK, D), lambda n: (at(n), 0))]

    def body(q_ref, k_ref, v_ref, b_ref, gc_ref, gr_ref, st_ref, inv_ref, do_ref, dq_ref, dk_ref, dv_ref, db_ref,
             dgc_ref, dgr_ref, dstate):
        @pl.when(pl.program_id(0) == 0)
        def _():
            dstate[...] = jnp.zeros_like(dstate)

        b_blk, gc_blk, gr_blk = b_ref[...], gc_ref[...], gr_ref[...]
        dold = [dstate[h] for h in range(HEADS)]
        res = []
        for grp in range(N_GROUPS):
            heads = range(grp * GROUP, (grp + 1) * GROUP)
            inv = inv_ref[grp]
            _, vjp = jax.vjp(lambda q, k, v, b, gc, gr, *st, inv=inv: _gdn_group(q, k, v, b, gc, gr, st, inv)[:2],
                             *_group_operands(grp, q_ref, k_ref, v_ref, b_blk, gc_blk, gr_blk),
                             *[st_ref[h] for h in heads])
            d_out = jnp.concatenate([do_ref[:, h * HEAD:(h + 1) * HEAD] for h in heads], axis=0)
            res.append(vjp((d_out, tuple(dold[h] for h in heads))))
        lane = lax.broadcasted_iota(jnp.int32, (CHUNK, HEAD), 1)
        db_all = jnp.zeros((CHUNK, HEAD), F32)
        dgc_all = jnp.zeros((CHUNK, HEAD), F32)
        for grp, (dq, dk, dv, db, dgc, dgr, *dst) in enumerate(res):
            dgr_ref[grp] = dgr
            for hh in range(GROUP):
                h = grp * GROUP + hh
                cs, rows = slice(h * HEAD, (h + 1) * HEAD), slice(hh * CHUNK, (hh + 1) * CHUNK)
                dq_ref[:, cs] = dq[rows]
                dk_ref[:, cs] = dk[rows]
                dv_ref[:, cs] = dv[rows]
                dstate[h] = dst[hh]
                db_all = jnp.where(lane == h, db[rows], db_all)
                dgc_all = jnp.where(lane == h, dgc[rows], dgc_all)
        db_ref[...] = db_all
        dgc_ref[...] = dgc_all

    blk = pl.BlockSpec((CHUNK, D), lambda n: (at(n), 0))
    gblk = pl.BlockSpec((CHUNK, HEAD), lambda n: (at(n), 0))
    return pl.pallas_call(
        body, name=name, grid=(nc,), in_specs=in_specs,
        out_specs=[blk, blk, blk, gblk, gblk, pl.BlockSpec((None, N_GROUPS, 1, GROWS), lambda n: (at(n), 0, 0, 0))],
        out_shape=[jax.ShapeDtypeStruct((s, D), F32)] * 3 + [jax.ShapeDtypeStruct((s, HEAD), F32)] * 2
        + [jax.ShapeDtypeStruct((nc, N_GROUPS, 1, GROWS), F32)],
        scratch_shapes=[pltpu.VMEM((HEADS, HEAD, HEAD), F32)],
        compiler_params=_params(("arbitrary",)),
    )(qkv, qkv, qkv, beta, gcum, grow, states, invs, do)


ATT_TILE = 512
ATT_SCALE = QK_HEAD ** -0.5


def _att_mask(t):
    qpos = lax.broadcasted_iota(jnp.int32, (t, t), 0)
    kpos = lax.broadcasted_iota(jnp.int32, (t, t), 1)
    return (kpos // CHUNK) <= (qpos // CHUNK)


ATT_STRIP = 32


def _att_strip_mask(r, t):
    kpos = lax.broadcasted_iota(jnp.int32, (ATT_STRIP, t), 1)
    return (kpos // CHUNK) <= (r * ATT_STRIP) // CHUNK


def _att_pairs(nb, by_query):
    if by_query:
        pairs = [(i, j) for i in range(nb) for j in range(i + 1)]
    else:
        pairs = [(j, i) for j in range(nb) for i in range(j, nb)]
    return jnp.array([a for a, _ in pairs], jnp.int32), jnp.array([b for _, b in pairs], jnp.int32)


def _attn_fwd(q, k, v, name):
    s = q.shape[0]
    t = min(ATT_TILE, s)
    nb = s // t
    ii, jj = _att_pairs(nb, by_query=True)

    def body(ii_ref, jj_ref, q_ref, k_ref, v_ref, o_ref, lse_ref, m_s, l_s, acc):
        step = pl.program_id(1)
        i, j = ii_ref[step], jj_ref[step]

        @pl.when(j == 0)
        def _():
            m_s[...] = jnp.full_like(m_s, -jnp.inf)
            l_s[...] = jnp.zeros_like(l_s)
            acc[...] = jnp.zeros_like(acc)

        sc = lax.dot_general(q_ref[...], k_ref[...], _NT, preferred_element_type=F32) * ATT_SCALE
        sc = lax.cond(i == j, lambda u: jnp.where(_att_mask(t), u, -jnp.inf), lambda u: u, sc)
        m_new = jnp.maximum(m_s[...], jnp.max(sc, axis=-1, keepdims=True))
        alpha = jnp.exp(m_s[...] - m_new)
        p = jnp.exp(sc - m_new)
        l_s[...] = alpha * l_s[...] + jnp.sum(p, axis=-1, keepdims=True)
        acc[...] = alpha * acc[...] + jnp.dot(p.astype(BF16), v_ref[...], preferred_element_type=F32)
        m_s[...] = m_new

        @pl.when(j == i)
        def _():
            o_ref[...] = acc[...] / l_s[...]
            lse_ref[...] = m_s[...] + jnp.log(l_s[...])

    grid_spec = pltpu.PrefetchScalarGridSpec(
        num_scalar_prefetch=2, grid=(HEADS, len(ii)),
        in_specs=[pl.BlockSpec((t, HEAD_PAD), lambda h, n, ir, jr: (ir[n], h)),
                  pl.BlockSpec((t, HEAD_PAD), lambda h, n, ir, jr: (jr[n], h)),
                  pl.BlockSpec((t, HEAD), lambda h, n, ir, jr: (jr[n], h))],
        out_specs=[pl.BlockSpec((t, HEAD), lambda h, n, ir, jr: (ir[n], h)),
                   pl.BlockSpec((None, t, 1), lambda h, n, ir, jr: (h, ir[n], 0))],
        scratch_shapes=[pltpu.VMEM((t, 1), F32), pltpu.VMEM((t, 1), F32), pltpu.VMEM((t, HEAD), F32)])
    return pl.pallas_call(
        body, name=name, grid_spec=grid_spec,
        out_shape=[jax.ShapeDtypeStruct((s, HEADS * HEAD), F32), jax.ShapeDtypeStruct((HEADS, s, 1), F32)],
        compiler_params=_params(("parallel", "arbitrary")),
    )(ii, jj, q, k, v)


def _attn_bwd(q, k, v, do, o, lse, name):
    s = q.shape[0]
    t = min(ATT_TILE, s)
    nb = s // t
    jj, ii = _att_pairs(nb, by_query=False)

    def body(jj_ref, ii_ref, q_ref, k_ref, v_ref, do_ref, o_ref, lse_ref, dq_ref, dk_ref, dv_ref, dk_acc, dv_acc,
             sc_s, dp_s, p_s, ds_s, dl_s):
        step = pl.program_id(1)
        i, j = ii_ref[step], jj_ref[step]

        @pl.when(step == 0)
        def _():
            dq_ref[...] = jnp.zeros_like(dq_ref)

        @pl.when(i == j)
        def _():
            dk_acc[...] = jnp.zeros_like(dk_acc)
            dv_acc[...] = jnp.zeros_like(dv_acc)

        do_f = do_ref[...]
        dob = do_f.astype(BF16)
        dl_s[...] = jnp.sum(do_f * o_ref[...], axis=-1, keepdims=True)
        sc_s[...] = lax.dot_general(q_ref[...], k_ref[...], _NT, preferred_element_type=F32)
        dp_s[...] = lax.dot_general(dob, v_ref[...], _NT, preferred_element_type=F32)

        def softmax_strips(diagonal):
            for r in range(t // ATT_STRIP):
                rows = slice(r * ATT_STRIP, (r + 1) * ATT_STRIP)
                p = jnp.exp(sc_s[rows, :] * ATT_SCALE - lse_ref[rows, :])
                if diagonal:
                    p = jnp.where(_att_strip_mask(r, t), p, 0.0)
                p_s[rows, :] = p.astype(BF16)
                ds_s[rows, :] = (p * (dp_s[rows, :] - dl_s[rows, :]) * ATT_SCALE).astype(BF16)

        pl.when(i == j)(functools.partial(softmax_strips, True))
        pl.when(i != j)(functools.partial(softmax_strips, False))
        ds = ds_s[...]
        dv_acc[...] += lax.dot_general(p_s[...], dob, _TN, preferred_element_type=F32)
        dk_acc[...] += lax.dot_general(ds, q_ref[...], _TN, preferred_element_type=F32)
        rows = pl.ds(pl.multiple_of(i * t, t), t)
        dq_ref[rows, :] += jnp.dot(ds, k_ref[...], preferred_element_type=F32)

        @pl.when(i == nb - 1)
        def _():
            dk_ref[...] = dk_acc[...]
            dv_ref[...] = dv_acc[...]

    grid_spec = pltpu.PrefetchScalarGridSpec(
        num_scalar_prefetch=2, grid=(HEADS, len(jj)),
        in_specs=[pl.BlockSpec((t, HEAD_PAD), lambda h, n, jr, ir: (ir[n], h)),
                  pl.BlockSpec((t, HEAD_PAD), lambda h, n, jr, ir: (jr[n], h)),
                  pl.BlockSpec((t, HEAD), lambda h, n, jr, ir: (jr[n], h)),
                  pl.BlockSpec((t, HEAD), lambda h, n, jr, ir: (ir[n], h)),
                  pl.BlockSpec((t, HEAD), lambda h, n, jr, ir: (ir[n], h)),
                  pl.BlockSpec((None, t, 1), lambda h, n, jr, ir: (h, ir[n], 0))],
        out_specs=[pl.BlockSpec((s, HEAD_PAD), lambda h, n, jr, ir: (0, h)),
                   pl.BlockSpec((t, HEAD_PAD), lambda h, n, jr, ir: (jr[n], h)),
                   pl.BlockSpec((t, HEAD), lambda h, n, jr, ir: (jr[n], h))],
        scratch_shapes=[pltpu.VMEM((t, HEAD_PAD), F32), pltpu.VMEM((t, HEAD), F32), pltpu.VMEM((t, t), F32),
                        pltpu.VMEM((t, t), F32), pltpu.VMEM((t, t), BF16), pltpu.VMEM((t, t), BF16),
                        pltpu.VMEM((t, 1), F32)])
    return pl.pallas_call(
        body, name=name, grid_spec=grid_spec,
        out_shape=[jax.ShapeDtypeStruct((s, HEADS * HEAD_PAD), F32)] * 2 + [jax.ShapeDtypeStruct((s, HEADS * HEAD), F32)],
        compiler_params=_params(("parallel", "arbitrary")),
    )(jj, ii, q, k, v, do, o, lse)


def _rope_tables(positions):
    half = ROPE // 2
    inv_freq = ROPE_BASE ** (-jnp.arange(half, dtype=F32) / half)
    ang = positions.astype(F32)[:, None] * inv_freq
    cos, sin = jnp.cos(ang), jnp.sin(ang)
    return jnp.concatenate([cos] * 4, axis=1), jnp.concatenate([-sin, sin] * 2, axis=1)


def _loss_and_grad(y, target, name):
    s = y.shape[0]
    ts = min(ROW_TILE, s)

    def body(y_ref, t_ref, dy_ref, l_ref):
        e = y_ref[...] - t_ref[...]
        dy_ref[...] = e * (1.0 / D)
        part = jnp.sum(jnp.sum(e * e, axis=-1, keepdims=True) * (0.5 / D), axis=0, keepdims=True)
        part = part * jnp.ones((1, 128), F32)

        @pl.when(pl.program_id(0) == 0)
        def _():
            l_ref[...] = part

        @pl.when(pl.program_id(0) > 0)
        def _():
            l_ref[...] += part

    return pl.pallas_call(
        body, name=name, grid=(s // ts,),
        in_specs=[pl.BlockSpec((ts, D), lambda i: (i, 0))] * 2,
        out_specs=[pl.BlockSpec((ts, D), lambda i: (i, 0)), pl.BlockSpec((1, 128), lambda i: (0, 0))],
        out_shape=[jax.ShapeDtypeStruct((s, D), F32), jax.ShapeDtypeStruct((1, 128), F32)],
        compiler_params=_params(("arbitrary",)),
    )(y, target)


ANY = pl.BlockSpec(memory_space=pl.ANY)


def _all_gather(shard, name):
    def body(x_ref, out_ref, send_sems, recv_sems, local_sem):
        x, y, c = lax.axis_index("x"), lax.axis_index("y"), lax.axis_index("c")
        me, sibling = (x, y, c), (x, y, 1 - c)
        chips = [(1 - x, y), (x, 1 - y), (1 - x, 1 - y)]

        def rows(px, py, pc):
            return out_ref.at[4 * px + 2 * py + pc]

        def copy(k, block, to, src=None):
            return pltpu.make_async_remote_copy(
                src_ref=rows(*block) if src is None else src, dst_ref=rows(*block),
                send_sem=send_sems.at[k], recv_sem=recv_sems.at[k], device_id=to, device_id_type=MESH)

        mine = pltpu.make_async_copy(x_ref, rows(*me), local_sem)
        mine.start()
        first = [copy(0, me, sibling, src=x_ref)]
        first += [copy(1 + j, me, (*chip, c), src=x_ref) for j, chip in enumerate(chips)]
        for cp in first:
            cp.start()
        passed = [copy(4 + j, (*chip, c), sibling) for j, chip in enumerate(chips)]
        for j, chip in enumerate(chips):
            copy(1 + j, (*chip, c), me).wait_recv()
            passed[j].start()
        copy(0, sibling, me).wait_recv()
        for j, chip in enumerate(chips):
            copy(4 + j, (*chip, 1 - c), me).wait_recv()
        for cp in first + passed:
            cp.wait_send()
        mine.wait()

    return pl.pallas_call(
        body, name=name, out_shape=jax.ShapeDtypeStruct((N_DEV,) + shard.shape, shard.dtype),
        in_specs=[ANY], out_specs=ANY,
        scratch_shapes=[pltpu.SemaphoreType.DMA((7,)), pltpu.SemaphoreType.DMA((7,)), pltpu.SemaphoreType.DMA],
    )(shard)


def _exchange(blocks, name):
    def body(x_ref, out_ref, send_sems, recv_sems, local_sem):
        x, y, c = lax.axis_index("x"), lax.axis_index("y"), lax.axis_index("c")
        me = 4 * x + 2 * y + c
        mine = pltpu.make_async_copy(x_ref.at[me], out_ref.at[me], local_sem)
        mine.start()
        copies = []
        for k in range(1, N_DEV):
            px = 1 - x if k & 4 else x
            py = 1 - y if k & 2 else y
            pc = 1 - c if k & 1 else c
            peer = 4 * px + 2 * py + pc
            cp = pltpu.make_async_remote_copy(
                src_ref=x_ref.at[peer], dst_ref=out_ref.at[me], send_sem=send_sems.at[k - 1],
                recv_sem=recv_sems.at[k - 1], device_id=(px, py, pc), device_id_type=MESH)
            cp.start()
            copies.append((cp, pltpu.make_async_remote_copy(
                src_ref=x_ref.at[peer], dst_ref=out_ref.at[peer], send_sem=send_sems.at[k - 1],
                recv_sem=recv_sems.at[k - 1], device_id=(px, py, pc), device_id_type=MESH)))
        for cp, landing in copies:
            landing.wait_recv()
        for cp, landing in copies:
            cp.wait_send()
        mine.wait()

    return pl.pallas_call(
        body, name=name, out_shape=jax.ShapeDtypeStruct(blocks.shape, blocks.dtype),
        in_specs=[ANY], out_specs=ANY,
        scratch_shapes=[pltpu.SemaphoreType.DMA((7,)), pltpu.SemaphoreType.DMA((7,)), pltpu.SemaphoreType.DMA],
    )(blocks)


HBM = pl.BlockSpec(memory_space=pltpu.HBM)
SEM = pl.BlockSpec(memory_space=pltpu.SEMAPHORE)
EFFECT = pltpu.SideEffectType.DATAFLOW_SIDE_EFFECTING


def _peers():
    x, y, c = lax.axis_index("x"), lax.axis_index("y"), lax.axis_index("c")
    peers = []
    for k in range(1, N_DEV):
        px = 1 - x if k & 4 else x
        py = 1 - y if k & 2 else y
        pc = 1 - c if k & 1 else c
        peers.append(((px, py, pc), 4 * px + 2 * py + pc))
    return 4 * x + 2 * y + c, peers


def _send_start(srcs, name, gather):
    n = len(srcs)
    lands = [((N_DEV,) + s.shape) if gather else s.shape for s in srcs]

    def body(*refs):
        src_refs, land_refs = refs[:n], refs[n:2 * n]
        send_sems, recv_sems, token = refs[2 * n], refs[2 * n + 1], refs[-1]
        me, peers = _peers()
        for i in range(n):
            for k, (dev, idx) in enumerate(peers):
                pltpu.make_async_remote_copy(
                    src_ref=src_refs[i] if gather else src_refs[i].at[idx], dst_ref=land_refs[i].at[me],
                    send_sem=send_sems.at[7 * i + k], recv_sem=recv_sems.at[7 * i + k], device_id=dev,
                    device_id_type=MESH).start()
        token[...] = jnp.zeros_like(token)

    res = pl.pallas_call(
        body, name=name,
        out_shape=(pltpu.SemaphoreType.DMA((7 * n,)), pltpu.SemaphoreType.DMA((7 * n,)),
                   *[pltpu.HBM(s.shape, s.dtype) for s in srcs],
                   *[pltpu.HBM(shape, s.dtype) for shape, s in zip(lands, srcs)],
                   jax.ShapeDtypeStruct((8, 128), F32)),
        in_specs=(HBM,) * (2 * n), out_specs=(SEM, SEM) + (HBM,) * (2 * n) + (pl.BlockSpec(memory_space=pltpu.VMEM),),
        input_output_aliases={i: 2 + i for i in range(2 * n)},
        compiler_params=pltpu.CompilerParams(has_side_effects=EFFECT),
    )(*[pltpu.with_memory_space_constraint(s, pltpu.HBM) for s in srcs],
      *[pltpu.with_memory_space_constraint(lax.empty(shape, s.dtype), pltpu.HBM) for shape, s in zip(lands, srcs)])
    return dict(sems=res[:2], srcs=res[2:2 + n], lands=res[2 + n:2 + 2 * n], token=res[-1])


def _send_wait(handle, after, name, gather):
    n = len(handle["srcs"])

    def body(*refs):
        src_refs, land_refs = refs[:n], refs[n:2 * n]
        send_sems, recv_sems = refs[2 * n], refs[2 * n + 1]
        me, peers = _peers()
        for i in range(n):
            for k, (dev, idx) in enumerate(peers):
                cp = pltpu.make_async_remote_copy(
                    src_ref=src_refs[i] if gather else src_refs[i].at[idx], dst_ref=land_refs[i].at[idx],
                    send_sem=send_sems.at[7 * i + k], recv_sem=recv_sems.at[7 * i + k], device_id=dev,
                    device_id_type=MESH)
                cp.wait_send()
                cp.wait_recv()

    both = list(handle["srcs"]) + list(handle["lands"])
    res = pl.pallas_call(
        body, name=name, out_shape=tuple(pltpu.HBM(t.shape, t.dtype) for t in both),
        in_specs=(HBM,) * (2 * n) + (SEM, SEM, pl.BlockSpec(memory_space=pl.ANY)), out_specs=(HBM,) * (2 * n),
        input_output_aliases={i: i for i in range(2 * n)},
        compiler_params=pltpu.CompilerParams(has_side_effects=EFFECT),
    )(*both, *handle["sems"], after)
    return res[:n], res[n:]


def _adamw(parts, w, m, v, name, tr=128):
    pieces = len(parts)
    n, r, wd = parts[0].shape
    tr = next((t for t in (tr, 64, 32, 16) if r % t == 0), r)
    nrt = r // tr

    def body(*refs):
        w_ref, m_ref, v_ref, g_ref, d_ref, nm_ref, nv_ref = refs[pieces:]

        def update(p_ref):
            g = p_ref[0].astype(F32)
            for k in range(1, n):
                g = g + p_ref[k].astype(F32)
            m_new = B1 * m_ref[...] + (1.0 - B1) * g
            v_new = B2 * v_ref[...] + (1.0 - B2) * (g * g)
            m_hat = m_new / (1.0 - B1 ** STEP)
            v_hat = v_new / (1.0 - B2 ** STEP)
            g_ref[...] = g
            d_ref[...] = -LR * (m_hat / (jnp.sqrt(v_hat) + ADAM_EPS) + WD * w_ref[...])
            nm_ref[...] = m_new
            nv_ref[...] = v_new

        for p in range(pieces):
            pl.when(pl.program_id(0) == p)(functools.partial(update, refs[p]))

    part_spec = lambda p: pl.BlockSpec((n, tr, wd), lambda l, i: (0, jnp.clip(i + (l - p) * nrt, 0, nrt - 1), 0))
    blk = pl.BlockSpec((tr, wd), lambda l, i: (l * nrt + i, 0))
    return pl.pallas_call(
        body, name=name, grid=(pieces, nrt),
        in_specs=[part_spec(p) for p in range(pieces)] + [blk, blk, blk],
        out_specs=[blk] * 4, out_shape=[jax.ShapeDtypeStruct((pieces * r, wd), F32)] * 4,
        compiler_params=_params(("arbitrary", "arbitrary")),
    )(*parts, w, m, v)


def _outer8(ct, dm, name):
    k, n = ct.shape[0], dm.shape[1]

    def body(c_ref, d_ref, o_ref):
        cv, dv = c_ref[...], d_ref[...]
        acc = cv[:, 0:1] * dv[0:1, :]
        for s in range(1, N_DEV):
            acc = acc + cv[:, s:s + 1] * dv[s:s + 1, :]
        o_ref[...] = acc

    tk = 256
    return pl.pallas_call(
        body, name=name, grid=(k // tk,),
        in_specs=[pl.BlockSpec((tk, N_DEV), lambda i: (i, 0)), pl.BlockSpec((N_DEV, n), lambda i: (0, 0))],
        out_specs=pl.BlockSpec((tk, n), lambda i: (i, 0)), out_shape=jax.ShapeDtypeStruct((k, n), F32),
        compiler_params=_params(("parallel",)),
    )(ct, dm)


FULL = (0, D)
C128 = (0, 128)
HEAD_NOPE = [(h * HEAD_PAD, NOPE) for h in range(HEADS)]
HEAD_ROPE = [(h * HEAD_PAD + NOPE, 128) for h in range(HEADS)]
HEAD_ALL = [(h * HEAD_PAD, HEAD_PAD) for h in range(HEADS)]
HEAD_V = [(h * HEAD, HEAD) for h in range(HEADS)]


def _modulate(x, p, ties=()):
    return _rowwise_fwd(_modulate_fn, [(x, FULL)], [p["gain"], p["scale"], p["shift"]], [(D, BF16, FULL)], "modulate",
                        ties=ties)[0]


def _residual_bwd(y, gm, dxn):
    return _rowwise_bwd(_gate_only_fn, [(y, FULL)], [gm], [(D, F32, FULL)], [dxn], [(0, FULL)], [(D, BF16)],
                        "residual_bwd")


def _modulate_bwd(x, p, dh, dx_in, prev=None):
    pars = [p["gain"], p["scale"], p["shift"]]
    if prev is None:
        return list(_rowwise_bwd(_modulate_fn, [(x, FULL)], pars, [(D, BF16, FULL)], [dh], [(0, FULL)], [(D, F32)],
                                 "modulate_bwd", add={0: dx_in})) + [None]
    s = x.shape[0]
    ts = min(ROW_TILE, s)

    def body(x_ref, g_ref, sc_ref, sh_ref, dh_ref, din_ref, y_ref, gm_ref, dx_ref, dy_ref, dg_ref, dsc_ref, dsh_ref,
             dgm_ref):
        _, vjp = jax.vjp(lambda *t: _modulate_fn(0, *t)[0], x_ref[...], g_ref[...], sc_ref[...], sh_ref[...])
        dxm, dg, dsc, dsh = vjp(dh_ref[...])
        dx = dxm + din_ref[...]
        dx_ref[...] = dx
        dy_ref[...] = (gm_ref[...] * dx).astype(BF16)
        sums = (dg, dsc, dsh, jnp.sum(dx * y_ref[...], axis=0, keepdims=True))
        first = pl.program_id(0) == 0
        for ref, val in zip((dg_ref, dsc_ref, dsh_ref, dgm_ref), sums):
            @pl.when(first)
            def _(ref=ref, val=val):
                ref[...] = val

            @pl.when(jnp.logical_not(first))
            def _(ref=ref, val=val):
                ref[...] += val

    blk = pl.BlockSpec((ts, D), lambda i: (i, 0))
    vec = pl.BlockSpec((1, D), lambda i: (0, 0))
    dx, dy, dg, dsc, dsh, dgm = pl.pallas_call(
        body, name="modulate_bwd_chain", grid=(s // ts,),
        in_specs=[blk, vec, vec, vec, blk, blk, blk, vec], out_specs=[blk, blk, vec, vec, vec, vec],
        out_shape=[jax.ShapeDtypeStruct((s, D), F32), jax.ShapeDtypeStruct((s, D), BF16)]
        + [jax.ShapeDtypeStruct((1, D), F32)] * 4,
        compiler_params=_params(("arbitrary",)),
    )(x, *pars, dh, dx_in, prev[0], prev[1])
    return [dx, dg, dsc, dsh, (dy, dgm)]


def _ffn_fwd(x, p, ties=()):
    h = _modulate(x, p, ties)
    gate, up, act = _ffn_in(h, p["wg"], p["wu"], "ffn_in")
    y, xn = _matmul([(act, p["wo"])], "nn", "ffn_out", resid=(x, p["gm"]))
    return xn, dict(x=x, h=h, gate=gate, up=up, act=act, y=y)


def _ffn_bwd(t, p, dxn, res=None, prev=None, ties=()):
    dy, dgm = res or _residual_bwd(t["y"], p["gm"], dxn)
    dgate, dup = _ffn_bwd_act(dy, p["wo"], t["gate"], t["up"], "ffn_bwd_act", ties=ties)
    dwo = _mm(t["act"], dy, "tn", "ffn_dwo", ties=ties)
    dh = _matmul([(dgate, p["wg"]), (dup, p["wu"])], "nt", "ffn_dh")
    dwg = _mm(t["h"], dgate, "tn", "ffn_dwi")
    dwu = _mm(t["h"], dup, "tn", "ffn_dwi")
    dx, dgain, dscale, dshift, res_prev = _modulate_bwd(t["x"], p, dh, dxn, prev)
    return dx, dict(gain=dgain, scale=dscale, shift=dshift, gm=dgm, wg=dwg, wu=dwu, wo=dwo), res_prev


def _pad128(t):
    return jnp.pad(t, ((0, 0), (0, 128 - t.shape[1])))


def _gdn_fwd(x, p, ties=()):
    s = x.shape[0]
    h = _modulate(x, p, ties)
    pm = _mm(h, p["w_main"], "nn", "gdn_proj")
    tail = _mm(h, p["w_tail"], "nn", "gdn_proj_tail")
    qkv = _gdn_conv_fwd(pm, p["conv_w"], "gdn_conv")
    beta, gcum = _rowwise_fwd(_gdn_gates_fn, [(tail, C128), (tail, (128, 128))], [p["a_log"], p["dt_bias"]],
                              [(128, F32, C128)] * 2, "gdn_gates")
    grow = gcum[:, :HEADS].reshape(s // CHUNK, CHUNK, N_GROUPS, GROUP).transpose(0, 2, 3, 1)
    grow = grow.reshape(s // CHUNK, N_GROUPS, 1, GROWS)
    o, states, invs = _gdn_scan_fwd(qkv, beta, gcum, grow, "gdn_scan")
    on, = _rowwise_fwd(_gdn_outnorm_fn, [(o, HEAD_V), (pm, [(3 * D + h_ * HEAD, HEAD) for h_ in range(HEADS)])],
                       [p["norm_g"]], [(D, BF16, HEAD_V)], "gdn_outnorm", groups=HEADS)
    y, xn = _matmul([(on, p["w_out"])], "nn", "mix_out", resid=(x, p["gm"]))
    t = dict(x=x, h=h, pm=pm, tail=tail, qkv=qkv, beta=beta, gcum=gcum, grow=grow, o=o, states=states, invs=invs,
             on=on, y=y)
    return xn, t


def _gdn_bwd(t, p, dxn, res=None, prev=None, ties=()):
    s = dxn.shape[0]
    zc = [(3 * D + h_ * HEAD, HEAD) for h_ in range(HEADS)]
    dy, dgm = res or _residual_bwd(t["y"], p["gm"], dxn)
    dw_out = _mm(t["on"], dy, "tn", "mix_dwo", ties=ties)
    don = _mm(dy, p["w_out"], "nt", "mix_dout", ties=ties)
    do, dz, dnorm_g = _rowwise_bwd(_gdn_outnorm_fn, [(t["o"], HEAD_V), (t["pm"], zc)], [p["norm_g"]],
                                   [(D, BF16, HEAD_V)], [don], [(0, HEAD_V), (1, HEAD_V)], [(D, F32), (D, F32)],
                                   "gdn_outnorm_bwd", groups=HEADS)
    dq, dk, dv, dbeta, dg, dgr = _gdn_scan_bwd(t["qkv"], t["beta"], t["gcum"], t["grow"], t["states"], t["invs"], do,
                                               "gdn_scan_bwd")
    dg = dg + _pad128(dgr.reshape(s // CHUNK, N_GROUPS, GROUP, CHUNK).transpose(0, 3, 1, 2).reshape(s, HEADS))
    dtail, da_log, ddt = _rowwise_bwd(_gdn_gates_fn, [(t["tail"], C128), (t["tail"], (128, 128))],
                                      [p["a_log"], p["dt_bias"]], [(128, F32, C128)] * 2, [dbeta, dg],
                                      [(0, C128), (0, (128, 128))], [(256, F32)], "gdn_gates_bwd")
    dxs, dcw = [], []
    for part, d in enumerate((dq, dk, dv)):
        dx_, dw_ = _gdn_conv_bwd(t["pm"], p["conv_w"], d, part, "gdn_conv_bwd")
        dxs.append(dx_)
        dcw.append(dw_)
    pieces = dxs + [dz]
    dh = _matmul([(d, p["w_main"]) for d in pieces] + [(dtail, p["w_tail"])], "nt", "gdn_dh",
                 boffs=[0, D, 2 * D, 3 * D, 0], tk=512)
    dw_main = [_mm(t["h"], d, "tn", "gdn_dwi") for d in pieces]
    dw_tail = _mm(t["h"], dtail, "tn", "gdn_dwi_tail")
    dx, dgain, dscale, dshift, res_prev = _modulate_bwd(t["x"], p, dh, dxn, prev)
    return dx, dict(gain=dgain, scale=dscale, shift=dshift, gm=dgm, w_main=jnp.concatenate(dw_main, axis=1),
                    w_tail=dw_tail, conv_w=jnp.concatenate(dcw, axis=1), a_log=da_log, dt_bias=ddt,
                    norm_g=dnorm_g, w_out=dw_out), res_prev


def _q_rows(q2, cosf, sins):
    return [(q2, HEAD_NOPE), (q2, HEAD_ROPE), (cosf, C128), (sins, C128)]


def _mla_fwd(x, p, kv, ties=()):
    h = _modulate(x, p, ties)
    cq = _mm(h, p["w_dq"], "nn", "mla_dq")
    cqn, = _rowwise_fwd(_rms_fn, [(cq, (0, Q_LORA))], [p["q_lora_g"]], [(Q_LORA, BF16, (0, Q_LORA))], "mla_qlora_norm")
    q2 = _mm(cqn, p["w_uq"], "nn", "mla_uq")
    qn, = _rowwise_fwd(_q_norm_rope_fn, _q_rows(q2, kv["cosf"], kv["sins"]), [p["q_gn"], p["q_gr"]],
                       [(HEADS * HEAD_PAD, BF16, HEAD_ALL)], "mla_q_norm", groups=HEADS)
    o, lse = _attn_fwd(qn, kv["kn"], kv["vb"], "mla_attn")
    y, xn = _matmul([(o, p["w_out"])], "nn", "mix_out", resid=(x, p["gm"]))
    return xn, dict(x=x, h=h, cq=cq, cqn=cqn, q2=q2, qn=qn, o=o, lse=lse, y=y)


def _mla_bwd(t, p, kv, dxn, res=None, prev=None, ties=()):
    dy, dgm = res or _residual_bwd(t["y"], p["gm"], dxn)
    dw_out = _mm(t["o"], dy, "tn", "mix_dwo", ties=ties)
    do = _mm(dy, p["w_out"], "nt", "mix_dout", ties=ties)
    dq, dk, dv = _attn_bwd(t["qn"], kv["kn"], kv["vb"], do, t["o"], t["lse"], "mla_attn_bwd")
    dq2, dq_gn, dq_gr = _rowwise_bwd(_q_norm_rope_fn, _q_rows(t["q2"], kv["cosf"], kv["sins"]), [p["q_gn"], p["q_gr"]],
                                     [(HEADS * HEAD_PAD, BF16, HEAD_ALL)], [dq],
                                     [(0, HEAD_NOPE), (0, HEAD_ROPE), None, None], [(HEADS * HEAD_PAD, F32)],
                                     "mla_q_norm_bwd", groups=HEADS)
    dw_uq = _mm(t["cqn"], dq2, "tn", "mla_dwuq")
    dcqn = _mm(dq2, p["w_uq"], "nt", "mla_dcq")
    dcq, dq_lora_g = _rowwise_bwd(_rms_fn, [(t["cq"], (0, Q_LORA))], [p["q_lora_g"]], [(Q_LORA, BF16, (0, Q_LORA))],
                                  [dcqn], [(0, (0, Q_LORA))], [(Q_LORA, F32)], "mla_qlora_norm_bwd")
    dw_dq = _mm(t["h"], dcq, "tn", "mla_dwdq")
    dh = _mm(dcq, p["w_dq"], "nt", "mla_dh")
    dx, dgain, dscale, dshift, res_prev = _modulate_bwd(t["x"], p, dh, dxn, prev)
    grads = dict(gain=dgain, scale=dscale, shift=dshift, gm=dgm, w_dq=dw_dq, q_lora_g=dq_lora_g, w_uq=dw_uq,
                 q_gn=dq_gn, q_gr=dq_gr, w_out=dw_out)
    return dx, grads, res_prev, dk, dv


def _k_rows(kvp, ckv, cosf, sins):
    return [(kvp, HEAD_NOPE), (kvp, HEAD_ROPE), (ckv, (KV_LORA, 128)), (cosf, C128), (sins, C128)]


def _kv_fwd(x, p, cosf, sins):
    h = _modulate(x, p)
    ckv = _mm(h, p["w_dkv"], "nn", "kv_down")
    lat, = _rowwise_fwd(_rms_fn, [(ckv, (0, KV_LORA))], [p["kv_g"]], [(KV_LORA, BF16, (0, KV_LORA))], "kv_norm")
    kvp = _mm(lat, p["w_ukv"], "nn", "kv_up")
    kn, vb = _rowwise_fwd(_k_norm_rope_fn, _k_rows(kvp, ckv, cosf, sins), [p["k_gn"], p["k_gr"]],
                          [(HEADS * HEAD_PAD, BF16, HEAD_ALL), (HEADS * HEAD, BF16, HEAD_V)], "kv_k_norm",
                          groups=HEADS)
    return dict(x=x, h=h, ckv=ckv, lat=lat, kvp=kvp, kn=kn, vb=vb, cosf=cosf, sins=sins)


def _kv_bwd(t, p, dk, dv, dx_in, prev):
    dkvp, drope, dk_gn, dk_gr = _rowwise_bwd(
        _k_norm_rope_fn, _k_rows(t["kvp"], t["ckv"], t["cosf"], t["sins"]), [p["k_gn"], p["k_gr"]],
        [(HEADS * HEAD_PAD, BF16, HEAD_ALL), (HEADS * HEAD, BF16, HEAD_V)], [dk, dv],
        [(0, HEAD_NOPE), (0, HEAD_ROPE), (1, C128), None, None], [(HEADS * HEAD_PAD, F32), (128, F32)],
        "kv_k_norm_bwd", groups=HEADS)
    dw_ukv = _mm(t["lat"], dkvp, "tn", "kv_dwukv")
    dlat = _mm(dkvp, p["w_ukv"], "nt", "kv_dlat")
    dckv, dkv_g = _rowwise_bwd(_rms_fn, [(t["ckv"], (0, KV_LORA))], [p["kv_g"]], [(KV_LORA, BF16, (0, KV_LORA))],
                               [dlat], [(0, (0, KV_LORA))], [(KV_LORA, F32)], "kv_norm_bwd")
    dw_dkv = jnp.concatenate([_mm(t["h"], dckv, "tn", "kv_dwdkv"), _mm(t["h"], drope, "tn", "kv_dwdkv_rope")], axis=1)
    dh = _matmul([(dckv, p["w_dkv"]), (drope, p["w_dkv"])], "nt", "kv_dh", boffs=[0, KV_LORA])
    dx, dgain, dscale, dshift, res_prev = _modulate_bwd(t["x"], p, dh, dx_in, prev)
    return dx, dict(gain=dgain, scale=dscale, shift=dshift, w_dkv=dw_dkv, kv_g=dkv_g, w_ukv=dw_ukv, k_gn=dk_gn,
                    k_gr=dk_gr), res_prev


WEIGHTS = ["ada_w", "ada_b", "norm_g", "ffn_w_in", "ffn_w_out", "gdn_w_in", "gdn_conv_w", "gdn_a_log", "gdn_dt_bias",
           "gdn_norm_g", "gdn_w_out", "kv_ada_w", "kv_ada_b", "kv_norm_g", "mla_w_dkv", "mla_kv_norm_g", "mla_w_ukv",
           "mla_k_norm_g", "mla_w_dq", "mla_q_lora_norm_g", "mla_w_uq", "mla_q_norm_g", "mla_w_out"]
SMALL = [("ada_b", 4 * N_MOD * D), ("kv_ada_b", 2 * D), ("norm_g", DEPTH * 3 * D), ("gdn_conv_w", N_A * CONV_K * 3 * D),
         ("gdn_a_log", N_A * HEADS), ("gdn_dt_bias", N_A * HEADS), ("gdn_norm_g", N_A * HEAD), ("kv_norm_g", D),
         ("mla_kv_norm_g", KV_LORA), ("mla_k_norm_g", QK_HEAD), ("mla_q_lora_norm_g", 2 * Q_LORA),
         ("mla_q_norm_g", 2 * QK_HEAD)]
SMALL_REPLICATED = [n for n, _ in SMALL if n not in ("norm_g", "gdn_conv_w")]


def _silu_fn(g, t):
    return (_silu(t),)


def _dup_rope(t):
    return jnp.concatenate([t[..., :NOPE], t[..., NOPE:], t[..., NOPE:]], axis=-1)


def _fold_rope(t):
    return jnp.concatenate([t[..., :NOPE], t[..., NOPE:QK_HEAD] + t[..., QK_HEAD:]], axis=-1)


def _pack(pieces, rows):
    flat = jnp.concatenate([p.reshape(-1).astype(F32) for p in pieces])
    return jnp.pad(flat, (0, rows * 128 - flat.shape[0])).reshape(rows, 128)


def _step(a):
    me = 4 * lax.axis_index("x") + 2 * lax.axis_index("y") + lax.axis_index("c")
    x = a["x"][0]
    cosf, sins = _rope_tables(a["positions"][0])

    n_in = 2 * D_FF // N_DEV
    n_gdn = (4 * D + 2 * HEADS) // N_DEV
    AHEAD = 2

    stages = [(l, part) for l in range(DEPTH) for part in range(3)]

    def stage_shards(l, part):
        if part != 1:
            sh = {"ffn_w_in": a["ffn_w_in"][l, part // 2], "ffn_w_out": a["ffn_w_out"][l, part // 2]}
            if part == 2 and l == N_A - 1:
                sh.update(mla_w_dkv=a["mla_w_dkv"], mla_w_ukv=a["mla_w_ukv"])
            return sh
        if l < N_A:
            return {"gdn_w_in": a["gdn_w_in"][l], "gdn_w_out": a["gdn_w_out"][l]}
        j = l - N_A
        return {"mla_w_dq": a["mla_w_dq"][j], "mla_w_uq": a["mla_w_uq"][j], "mla_w_out": a["mla_w_out"][j]}

    def zero_of(t):
        return jnp.minimum(jnp.abs(t[(0,) * t.ndim].astype(F32)), 0.0)

    def start_stage(l, part, tie):
        sh = stage_shards(l, part)
        return list(sh), _send_start([(w + tie).astype(BF16) for w in sh.values()], f"fetch_start_{l}_{part}", gather=True)

    def finish_stage(l, part, names, handle, after):
        srcs, lands = _send_wait(handle, after, f"fetch_wait_{l}_{part}", gather=True)
        return {n: lax.dynamic_update_slice(land, src[None], (me, 0, 0)) for n, src, land in zip(names, srcs, lands)}

    n_cw, n_ng = N_A * CONV_K * 3 * HEAD, DEPTH * 3 * HEAD
    small_all = _all_gather(_pack([a["gdn_conv_w"], a["norm_g"], a["c"]], 44), "gather_small").reshape(N_DEV, -1)
    conv_w = small_all[:, :n_cw].reshape(N_DEV, N_A, CONV_K, 3 * HEAD).transpose(1, 2, 0, 3).reshape(N_A, CONV_K, 3 * D)
    norm_g = small_all[:, n_cw:n_cw + n_ng].reshape(N_DEV, DEPTH, 3, HEAD).transpose(1, 2, 0, 3).reshape(DEPTH, 3, D)
    c_all = small_all[:, n_cw + n_ng:n_cw + n_ng + D]

    c_act, = _rowwise_fwd(_silu_fn, [(c_all, FULL)], [], [(D, F32, FULL)], "c_act")
    n_ada = N_MOD * D // N_DEV
    parts = [_mm(c_act, a["ada_w"][l], "nn", "mod_proj") for l in range(DEPTH)]
    parts.append(_mm(c_act, a["kv_ada_w"], "nn", "mod_proj_kv"))
    mod_recv = _exchange(jnp.concatenate(parts, axis=1)[:, None, :], "exchange_mod")[:, 0]
    mod = mod_recv[:, :DEPTH * n_ada].reshape(N_DEV, DEPTH, n_ada).transpose(1, 0, 2).reshape(DEPTH, N_MOD * D)
    mod = (mod + a["ada_b"]).reshape(DEPTH, N_MOD, D)
    kvmod = mod_recv[:, DEPTH * n_ada:].reshape(2 * D) + a["kv_ada_b"]

    def row(v):
        return v[None]

    def ffn_params(l, i, w):
        w_in = w["ffn_w_in"]
        k = 0 if i == 0 else 6
        return dict(gain=row(norm_g[l, 0 if i == 0 else 2]), shift=row(mod[l, k]), scale=row(mod[l, k + 1]),
                    gm=0.5 * row(mod[l, k + 2]),
                    wg=w_in[:N_DEV // 2].transpose(1, 0, 2).reshape(D, D_FF),
                    wu=w_in[N_DEV // 2:].transpose(1, 0, 2).reshape(D, D_FF),
                    wo=w["ffn_w_out"].reshape(D_FF, D))

    def gdn_params(l, w):
        w_in = w["gdn_w_in"].transpose(1, 0, 2).reshape(D, 4 * D + 2 * HEADS)
        pad = lambda t: jnp.pad(t, ((0, 0), (0, 128 - HEADS)))
        return dict(gain=row(norm_g[l, 1]), shift=row(mod[l, 3]), scale=row(mod[l, 4]), gm=row(mod[l, 5]),
                    w_main=w_in[:, :4 * D],
                    w_tail=jnp.concatenate([pad(w_in[:, 4 * D:4 * D + HEADS]), pad(w_in[:, 4 * D + HEADS:])], axis=1),
                    conv_w=conv_w[l], a_log=_pad128(row(a["gdn_a_log"][l])), dt_bias=_pad128(row(a["gdn_dt_bias"][l])),
                    norm_g=row(a["gdn_norm_g"][l]), w_out=w["gdn_w_out"].reshape(D, D))

    def mla_params(l, w):
        j = l - N_A
        uq = w["mla_w_uq"].transpose(1, 0, 2)
        qg = _dup_rope(a["mla_q_norm_g"][j])
        return dict(gain=row(norm_g[l, 1]), shift=row(mod[l, 3]), scale=row(mod[l, 4]), gm=row(mod[l, 5]),
                    w_dq=w["mla_w_dq"].reshape(D, Q_LORA), q_lora_g=row(a["mla_q_lora_norm_g"][j]),
                    w_uq=_dup_rope(uq).reshape(Q_LORA, HEADS * HEAD_PAD), q_gn=row(qg[:NOPE]), q_gr=row(qg[NOPE:]),
                    w_out=w["mla_w_out"].reshape(D, D))

    def kv_params(w):
        w_dkv = w["mla_w_dkv"].reshape(D, KV_LORA + ROPE)
        kg = _dup_rope(a["mla_k_norm_g"])
        return dict(gain=row(a["kv_norm_g"]), shift=row(kvmod[:D]), scale=row(kvmod[D:]),
                    w_dkv=jnp.concatenate([w_dkv, w_dkv[:, KV_LORA:]], axis=1), kv_g=row(a["mla_kv_norm_g"]),
                    w_ukv=w["mla_w_ukv"].transpose(1, 0, 2).reshape(KV_LORA, HEADS * 2 * HEAD), k_gn=row(kg[:NOPE]),
                    k_gr=row(kg[NOPE:]))

    tapes, kv, kv_p = [[] for _ in range(DEPTH)], None, None
    first = {name: _all_gather((w + zero_of(mod)).astype(BF16), "fetch_first_" + name)
             for name, w in stage_shards(0, 0).items()}
    pending = []
    for l, part in stages[1:1 + AHEAD]:
        tie = pending[-1][1]["token"][0, 0] if pending else zero_of(first["ffn_w_out"])
        pending.append(start_stage(l, part, tie))
    for n, (l, part) in enumerate(stages):
        if n == 0:
            w, ties = first, tuple(h["token"] for _, h in pending)
        else:
            names, handle = pending.pop(0)
            w = finish_stage(l, part, names, handle, x)
            ties = ()
            if n + AHEAD < len(stages):
                pending.append(start_stage(*stages[n + AHEAD], zero_of(w[names[0]])))
                ties = (pending[-1][1]["token"],)
        if part != 1:
            p = ffn_params(l, part // 2, w)
            x, t = _ffn_fwd(x, p, ties)
        else:
            p = gdn_params(l, w) if l < N_A else mla_params(l, w)
            x, t = _gdn_fwd(x, p, ties) if l < N_A else _mla_fwd(x, p, kv, ties)
        tapes[l] += [p, t]
        if part == 2 and l == N_A - 1:
            kv_p = kv_params(w)
            kv = _kv_fwd(x, kv_p, cosf, sins)
    dx, loss_blk = _loss_and_grad(x, a["loss_target"][0], "loss")
    loss = lax.psum(loss_blk[0, 0], ("x", "y", "c"))

    def by_cols(g, n):
        return g.reshape(g.shape[0], -1, n).transpose(1, 0, 2)

    def ffn_blocks(g):
        return {"ffn_w_in": jnp.concatenate([by_cols(g["wg"], n_in), by_cols(g["wu"], n_in)], axis=0),
                "ffn_w_out": g["wo"].reshape(N_DEV, D_FF // N_DEV, D)}

    def mixer_blocks(l, g):
        if l < N_A:
            full = jnp.concatenate([g["w_main"], g["w_tail"][:, :HEADS], g["w_tail"][:, 128:128 + HEADS]], axis=1)
            return {"gdn_w_in": by_cols(full, n_gdn), "gdn_w_out": g["w_out"].reshape(N_DEV, D // N_DEV, D)}
        return {"mla_w_dq": g["w_dq"].reshape(N_DEV, D // N_DEV, Q_LORA),
                "mla_w_uq": _fold_rope(g["w_uq"].reshape(Q_LORA, HEADS, HEAD_PAD)).transpose(1, 0, 2),
                "mla_w_out": g["w_out"].reshape(N_DEV, D // N_DEV, D)}

    sent = []

    def send(key, blocks, tie=0.0):
        handle = _send_start([(b + tie).astype(BF16) for b in blocks.values()], "grad_start_" + "_".join(map(str, key)),
                             gather=False)
        sent.append((key, list(blocks), handle))
        return (handle["token"],)

    grads = [None] * DEPTH
    dk_sum = dv_sum = kv_grads = res = None
    ties = ()
    for l in reversed(range(DEPTH)):
        p1, t1, pm_, tm_, p2, t2 = tapes[l]
        if l == N_A - 1:
            dx, kv_grads, res = _kv_bwd(kv, kv_p, dk_sum, dv_sum, dx, (t2["y"], p2["gm"]))
            d_dkv = kv_grads["w_dkv"]
            ties += send((l, 3), {
                "mla_w_dkv": jnp.concatenate(
                    [d_dkv[:, :KV_LORA], d_dkv[:, KV_LORA:KV_LORA + ROPE] + d_dkv[:, KV_LORA + ROPE:]],
                    axis=1).reshape(N_DEV, D // N_DEV, KV_LORA + ROPE),
                "mla_w_ukv": by_cols(kv_grads["w_ukv"], 2 * HEAD)})
        dx, g2, res = _ffn_bwd(t2, p2, dx, res, (tm_["y"], pm_["gm"]), ties)
        ties = send((l, 2), ffn_blocks(g2))
        if l < N_A:
            dx, gm_, res = _gdn_bwd(tm_, pm_, dx, res, (t1["y"], p1["gm"]), ties)
        else:
            dx, gm_, res, dk, dv = _mla_bwd(tm_, pm_, kv, dx, res, (t1["y"], p1["gm"]), ties)
            dk_sum = dk if dk_sum is None else dk_sum + dk
            dv_sum = dv if dv_sum is None else dv_sum + dv
        ties = send((l, 1), mixer_blocks(l, gm_))
        prev = (tapes[l - 1][5]["y"], tapes[l - 1][4]["gm"]) if l > 0 and l != N_A else None
        dx, g1, res = _ffn_bwd(t1, p1, dx, res, prev, ties)
        if l > 0:
            ties = send((l, 0), ffn_blocks(g1))
        grads[l] = (g1, gm_, g2)

    out = {}
    def dmod(l):
        g1, gm_, g2 = grads[l]
        return jnp.concatenate([g1["shift"], g1["scale"], 0.5 * g1["gm"], gm_["shift"], gm_["scale"], gm_["gm"],
                                g2["shift"], g2["scale"], 0.5 * g2["gm"]], axis=1)

    gdn = [grads[l][1] for l in range(N_A)]
    mla = [grads[l][1] for l in range(N_A, DEPTH)]
    small = {
        "ada_b": jnp.concatenate([dmod(l) for l in range(DEPTH)], axis=0),
        "kv_ada_b": jnp.concatenate([kv_grads["shift"], kv_grads["scale"]], axis=1),
        "norm_g": jnp.stack([jnp.concatenate([grads[l][0]["gain"], grads[l][1]["gain"], grads[l][2]["gain"]], axis=0)
                             for l in range(DEPTH)]),
        "gdn_conv_w": jnp.stack([g["conv_w"] for g in gdn]),
        "gdn_a_log": jnp.stack([g["a_log"][0, :HEADS] for g in gdn]),
        "gdn_dt_bias": jnp.stack([g["dt_bias"][0, :HEADS] for g in gdn]),
        "gdn_norm_g": jnp.stack([g["norm_g"][0] for g in gdn]),
        "kv_norm_g": kv_grads["gain"],
        "mla_kv_norm_g": kv_grads["kv_g"],
        "mla_k_norm_g": _fold_rope(jnp.concatenate([kv_grads["k_gn"], kv_grads["k_gr"]], axis=1)),
        "mla_q_lora_norm_g": jnp.stack([g["q_lora_g"][0] for g in mla]),
        "mla_q_norm_g": jnp.stack([_fold_rope(jnp.concatenate([g["q_gn"], g["q_gr"]], axis=1))[0] for g in mla]),
    }
    rows = 616
    assert sum(n for _, n in SMALL) <= rows * 128 and all(small[n].size == k for n, k in SMALL)
    small_recv = _all_gather(_pack([small[n] for n, _ in SMALL], rows), "gather_small_grads")
    small_recv = small_recv + send((0, 0), ffn_blocks(grads[0][0]), zero_of(small_recv))[0][0, 0]
    zero = lambda n, k: jnp.zeros((k,), F32)
    packed = {pre: _pack([a[pre + n] if n in SMALL_REPLICATED else zero(n, k) for n, k in SMALL], rows)
              for pre in ("", "m_", "v_")}
    res = _adamw([small_recv], packed[""], packed["m_"], packed["v_"], "adamw_small")
    offs = {}
    o = 0
    for n, k in SMALL:
        offs[n] = o
        o += k
    for n, k in SMALL:
        if n in SMALL_REPLICATED:
            out[n] = [r.reshape(-1)[offs[n]:offs[n] + k] for r in res]
    gsum = res[0].reshape(-1)
    g_norm = lax.dynamic_slice_in_dim(gsum[offs["norm_g"]:offs["norm_g"] + DEPTH * 3 * D].reshape(DEPTH * 3, D),
                                      me * HEAD, HEAD, axis=1)
    g_conv = lax.dynamic_slice_in_dim(
        gsum[offs["gdn_conv_w"]:offs["gdn_conv_w"] + N_A * CONV_K * 3 * D].reshape(N_A * CONV_K, 3 * D),
        me * 3 * HEAD, 3 * HEAD, axis=1)
    res2 = _adamw([_pack([g_norm, g_conv], 36)[None]], *[_pack([a[pre + "norm_g"], a[pre + "gdn_conv_w"]], 36)
                                                      for pre in ("", "m_", "v_")], "adamw_small")
    out["norm_g"] = [r.reshape(-1)[:n_ng] for r in res2]
    out["gdn_conv_w"] = [r.reshape(-1)[n_ng:n_ng + n_cw] for r in res2]

    c_act_t = c_act.T
    all_small = small_recv.reshape(N_DEV, -1)
    dmod_all = all_small[:, :DEPTH * N_MOD * D].reshape(N_DEV, DEPTH, N_MOD * D)
    dmod_mine = lax.dynamic_slice_in_dim(dmod_all, me * n_ada, n_ada, axis=2)
    g_ada = [_outer8(c_act_t, dmod_mine[:, l], "ada_grad")[None] for l in range(DEPTH)]
    out["ada_w"] = _adamw(g_ada, *[a[pre + "ada_w"].reshape(DEPTH * D, n_ada) for pre in ("", "m_", "v_")], "adamw")
    dkv_all = all_small[:, offs["kv_ada_b"]:offs["kv_ada_b"] + 2 * D]
    g_kv = _outer8(c_act_t, lax.dynamic_slice_in_dim(dkv_all, me * (2 * D // N_DEV), 2 * D // N_DEV, axis=1), "ada_grad")
    out["kv_ada_w"] = _adamw([g_kv[None]], *[a[pre + "kv_ada_w"] for pre in ("", "m_", "v_")], "adamw")

    pieces = {}
    for key, names, handle in sent:
        srcs, lands = _send_wait(handle, out["kv_ada_w"][0], "grad_wait_" + "_".join(map(str, key)), gather=False)
        for name, src, land in zip(names, srcs, lands):
            own = lax.dynamic_slice_in_dim(src, me, 1, axis=0)
            pieces.setdefault(name, []).append((key, lax.dynamic_update_slice(land, own, (me, 0, 0))))
    for name, parts in pieces.items():
        wide = a[name].shape[-1]
        out[name] = _adamw([p for _, p in sorted(parts, key=lambda kp: kp[0])], a[name].reshape(-1, wide),
                           a["m_" + name].reshape(-1, wide), a["v_" + name].reshape(-1, wide), "adamw")

    result = [loss, dx[None]]
    for k in range(4):
        result += [out[n][k].reshape(a[n].shape) for n in WEIGHTS]
    return tuple(result)


def kernel(x, c, positions, ada_w, ada_b, norm_g, ffn_w_in, ffn_w_out, gdn_w_in, gdn_conv_w, gdn_a_log, gdn_dt_bias, gdn_norm_g, gdn_w_out, kv_ada_w, kv_ada_b, kv_norm_g, mla_w_dkv, mla_kv_norm_g, mla_w_ukv, mla_k_norm_g, mla_w_dq, mla_q_lora_norm_g, mla_w_uq, mla_q_norm_g, mla_w_out, loss_target, m_ada_w, m_ada_b, m_norm_g, m_ffn_w_in, m_ffn_w_out, m_gdn_w_in, m_gdn_conv_w, m_gdn_a_log, m_gdn_dt_bias, m_gdn_norm_g, m_gdn_w_out, m_kv_ada_w, m_kv_ada_b, m_kv_norm_g, m_mla_w_dkv, m_mla_kv_norm_g, m_mla_w_ukv, m_mla_k_norm_g, m_mla_w_dq, m_mla_q_lora_norm_g, m_mla_w_uq, m_mla_q_norm_g, m_mla_w_out, v_ada_w, v_ada_b, v_norm_g, v_ffn_w_in, v_ffn_w_out, v_gdn_w_in, v_gdn_conv_w, v_gdn_a_log, v_gdn_dt_bias, v_gdn_norm_g, v_gdn_w_out, v_kv_ada_w, v_kv_ada_b, v_kv_norm_g, v_mla_w_dkv, v_mla_kv_norm_g, v_mla_w_ukv, v_mla_k_norm_g, v_mla_w_dq, v_mla_q_lora_norm_g, v_mla_w_uq, v_mla_q_norm_g, v_mla_w_out):
    return _step(dict(locals()))
```

```python
import functools
import math

import jax
import jax.numpy as jnp
from jax import lax
from jax.experimental import pallas as pl
from jax.experimental.pallas import tpu as pltpu

F32 = jnp.float32
BF16 = jnp.bfloat16

N_DEV = 8
D = 1024
D_FF = 2816
DEPTH = 4
N_A = 2
N_MOD = 9
HEADS = 8
HEAD = 128
CHUNK = 64
CONV_K = 4
KV_LORA = 256
Q_LORA = 384
NOPE = 128
ROPE = 64
QK_HEAD = NOPE + ROPE
HEAD_PAD = 256
ROPE_BASE = 10000.0
EPS = 1e-6
LR, B1, B2, ADAM_EPS, WD, STEP = 0.001, 0.9, 0.999, 1e-08, 0.01, 10

VMEM_LIMIT = 48 * 1024 * 1024
ROW_TILE = 256
MESH = pl.DeviceIdType.MESH

_NN = (((1,), (0,)), ((), ()))
_NT = (((1,), (1,)), ((), ()))
_TN = (((0,), (0,)), ((), ()))
_DIMS = {"nn": _NN, "nt": _NT, "tn": _TN}


def _params(dims=None):
    return pltpu.CompilerParams(dimension_semantics=dims, vmem_limit_bytes=VMEM_LIMIT)


def _tile(n, target):
    for t in range(target - target % 128, 0, -128):
        if n % t == 0:
            return t
    return n


_TIE_SPEC1 = pl.BlockSpec((8, 128), lambda i: (0, 0))
_TIE_SPEC2 = pl.BlockSpec((8, 128), lambda i, j: (0, 0))
_TIE_SPEC3 = pl.BlockSpec((8, 128), lambda i, j, k: (0, 0))


def _matmul(pairs, form, name, out_dtype=F32, tm=1408, tn=1408, tk=1408, boffs=None, resid=None, ties=()):
    a0, b0 = pairs[0]
    if form == "nn":
        m, n = a0.shape[0], b0.shape[1]
        ks = [a.shape[1] for a, _ in pairs]
    elif form == "nt":
        m, n = a0.shape[0], b0.shape[0]
        ks = [a.shape[1] for a, _ in pairs]
    else:
        m, n = a0.shape[1], b0.shape[1]
        ks = [a.shape[0] for a, _ in pairs]
    tm, tn = _tile(m, tm), _tile(n, tn)
    tks = [_tile(k, tk) for k in ks]
    boffs = boffs or [0] * len(pairs)
    assert m % tm == 0 and n % tn == 0 and all(o % t == 0 for o, t in zip(boffs, tks)), (name, m, n, ks)
    steps = [k // t for k, t in zip(ks, tks)]
    starts = [sum(steps[:p]) for p in range(len(pairs))]
    nk = sum(steps)

    def kidx(p, k):
        return jnp.clip(k - starts[p], 0, steps[p] - 1)

    in_specs, args = [], []
    for p, (a, b) in enumerate(pairs):
        t = tks[p]
        if form == "tn":
            in_specs.append(pl.BlockSpec((t, tm), lambda i, j, k, p=p: (kidx(p, k), i)))
            in_specs.append(pl.BlockSpec((t, tn), lambda i, j, k, p=p: (kidx(p, k), j)))
        elif form == "nn":
            in_specs.append(pl.BlockSpec((tm, t), lambda i, j, k, p=p: (i, kidx(p, k))))
            in_specs.append(pl.BlockSpec((t, tn), lambda i, j, k, p=p: (kidx(p, k), j)))
        else:
            in_specs.append(pl.BlockSpec((tm, t), lambda i, j, k, p=p: (i, kidx(p, k))))
            in_specs.append(pl.BlockSpec((tn, t), lambda i, j, k, p=p, o=boffs[p] // t: (j, kidx(p, k) + o)))
        args += [a, b]
    dims = _DIMS[form]
    npairs = len(pairs)
    nres = len(resid or ())
    nin = 2 * npairs + len(ties) + nres
    out_blk = pl.BlockSpec((tm, tn), lambda i, j, k: (i, j))
    in_specs += [_TIE_SPEC3] * len(ties)
    args += list(ties)
    if resid:
        assert nres == 2 or (nres == 5 and tn == n)
        in_specs += [out_blk] + [pl.BlockSpec((1, tn), lambda i, j, k: (0, j))] * (nres - 1)
        args += list(resid)

    def body(*refs):
        o_ref = refs[nin]
        k = pl.program_id(2)

        def prod(p):
            return lax.dot_general(refs[2 * p][...].astype(BF16), refs[2 * p + 1][...].astype(BF16), dims,
                                   preferred_element_type=F32)

        def finish(y):
            o_ref[...] = y.astype(o_ref.dtype)
            if resid:
                x_ref, gate_ref = refs[nin - nres], refs[nin - nres + 1]
                xn = x_ref[...] + gate_ref[...] * y
                refs[nin + 1][...] = xn
                if nres == 5:
                    gain, scale, shift = (r[...] for r in refs[nin - 3:nin])
                    refs[nin + 2][...] = _modulate_fn(0, xn, gain, scale, shift)[0].astype(BF16)

        if nk == 1:
            finish(prod(0))
            return
        acc = refs[-1]

        @pl.when(k == 0)
        def _():
            acc[...] = jnp.zeros_like(acc)

        for p in range(npairs):
            @pl.when((k >= starts[p]) & (k < starts[p] + steps[p]))
            def _(p=p):
                acc[...] += prod(p)

        @pl.when(k == nk - 1)
        def _():
            finish(acc[...])

    res = pl.pallas_call(
        body, name=name, grid=(m // tm, n // tn, nk), in_specs=in_specs,
        out_specs=[out_blk] * (2 + (nres == 5)) if resid else out_blk,
        out_shape=([jax.ShapeDtypeStruct((m, n), out_dtype)] * 2 + [jax.ShapeDtypeStruct((m, n), BF16)] * (nres == 5))
        if resid else jax.ShapeDtypeStruct((m, n), out_dtype),
        scratch_shapes=[] if nk == 1 else [pltpu.VMEM((tm, tn), F32)],
        compiler_params=_params(("parallel", "parallel", "arbitrary")),
    )(*args)
    return res


def _mm(a, b, form, name, **kw):
    return _matmul([(a, b)], form, name, **kw)


def _cols(spec, g):
    return spec[g] if isinstance(spec, list) else spec


def _rowwise_fwd(fn, rows, pars, outs, name, groups=1, ts=ROW_TILE, ties=()):
    s = rows[0][0].shape[0]
    ts = min(ts, s)
    assert s % ts == 0
    nr, npar = len(rows), len(pars)

    def body(*refs):
        par_t = [r[...] for r in refs[nr:nr + npar]]
        out_refs = refs[nr + npar + len(ties):]
        for g in range(groups):
            row_t = []
            for r, (_, spec) in zip(refs[:nr], rows):
                c0, w = _cols(spec, g)
                row_t.append(r[:, c0:c0 + w].astype(F32))
            res = fn(g, *row_t, *par_t)
            for o_ref, val, (_, _, spec) in zip(out_refs, res, outs):
                c0, w = _cols(spec, g)
                o_ref[:, c0:c0 + w] = val.astype(o_ref.dtype)

    return pl.pallas_call(
        body, name=name, grid=(s // ts,),
        in_specs=[pl.BlockSpec((ts, a.shape[1]), lambda i: (i, 0)) for a, _ in rows]
        + [pl.BlockSpec(p.shape, lambda i: (0, 0)) for p in pars] + [_TIE_SPEC1] * len(ties),
        out_specs=[pl.BlockSpec((ts, w), lambda i: (i, 0)) for w, _, _ in outs],
        out_shape=[jax.ShapeDtypeStruct((s, w), dt) for w, dt, _ in outs],
        compiler_params=_params(("parallel",)),
    )(*[a for a, _ in rows], *pars, *ties)


def _rowwise_bwd(fn, rows, pars, outs, douts, gmap, gshapes, name, groups=1, add=None, par_grads=True,
                 ts=ROW_TILE):
    s = rows[0][0].shape[0]
    ts = min(ts, s)
    assert s % ts == 0
    nr, npar, nout, ng = len(rows), len(pars), len(outs), len(gshapes)
    add = add or {}
    add_keys = sorted(add)

    def body(*refs):
        row_refs = refs[:nr]
        par_refs = refs[nr:nr + npar]
        dout_refs = refs[nr + npar:nr + npar + nout]
        add_refs = refs[nr + npar + nout:nr + npar + nout + len(add_keys)]
        g_refs = refs[nr + npar + nout + len(add_keys):][:ng]
        pg_refs = refs[nr + npar + nout + len(add_keys) + ng:]
        par_t = [r[...] for r in par_refs]
        par_acc = [None] * npar
        shared_acc = {}
        for g in range(groups):
            row_t = []
            for r, (_, spec) in zip(row_refs, rows):
                c0, w = _cols(spec, g)
                row_t.append(r[:, c0:c0 + w].astype(F32))
            cts = []
            for r, (_, _, spec) in zip(dout_refs, outs):
                c0, w = _cols(spec, g)
                cts.append(r[:, c0:c0 + w].astype(F32))
            _, vjp = jax.vjp(lambda *t, g=g: tuple(fn(g, *t)), *row_t, *par_t)
            grads = vjp(tuple(cts))
            for k in range(nr):
                if gmap[k] is None:
                    continue
                gi, spec = gmap[k]
                if isinstance(spec, list) or groups == 1:
                    c0, w = _cols(spec, g)
                    val = grads[k]
                    if gi in add:
                        val = val + add_refs[add_keys.index(gi)][:, c0:c0 + w].astype(F32)
                    g_refs[gi][:, c0:c0 + w] = val.astype(g_refs[gi].dtype)
                else:
                    shared_acc[k] = grads[k] if k not in shared_acc else shared_acc[k] + grads[k]
            if par_grads:
                for k in range(npar):
                    pg = grads[nr + k]
                    par_acc[k] = pg if par_acc[k] is None else par_acc[k] + pg
        for k, val in shared_acc.items():
            gi, (c0, w) = gmap[k]
            assert gi not in add
            g_refs[gi][:, c0:c0 + w] = val.astype(g_refs[gi].dtype)
        if par_grads:
            first = pl.program_id(0) == 0
            for k in range(npar):
                @pl.when(first)
                def _(k=k):
                    pg_refs[k][...] = par_acc[k]

                @pl.when(jnp.logical_not(first))
                def _(k=k):
                    pg_refs[k][...] += par_acc[k]

    out_specs = [pl.BlockSpec((ts, w), lambda i: (i, 0)) for w, _ in gshapes]
    out_shape = [jax.ShapeDtypeStruct((s, w), dt) for w, dt in gshapes]
    if par_grads:
        out_specs += [pl.BlockSpec(p.shape, lambda i: (0, 0)) for p in pars]
        out_shape += [jax.ShapeDtypeStruct(p.shape, F32) for p in pars]
    return pl.pallas_call(
        body, name=name, grid=(s // ts,),
        in_specs=[pl.BlockSpec((ts, a.shape[1]), lambda i: (i, 0)) for a, _ in rows]
        + [pl.BlockSpec(p.shape, lambda i: (0, 0)) for p in pars]
        + [pl.BlockSpec((ts, a.shape[1]), lambda i: (i, 0)) for a in douts]
        + [pl.BlockSpec((ts, add[k].shape[1]), lambda i: (i, 0)) for k in add_keys],
        out_specs=out_specs, out_shape=out_shape,
        compiler_params=_params(("arbitrary",)),
    )(*[a for a, _ in rows], *pars, *douts, *[add[k] for k in add_keys])


def _sigmoid(x):
    return 1.0 / (1.0 + jnp.exp(-x))


def _silu(x):
    return x * _sigmoid(x)


def _softplus(x):
    return jnp.maximum(x, 0.0) + jnp.log(1.0 + jnp.exp(-jnp.abs(x)))


def _rms(t, g, n=None):
    n = n or t.shape[-1]
    return t * lax.rsqrt(jnp.sum(t * t, axis=-1, keepdims=True) / n + EPS) * g


def _modulate_fn(g, x, gain, scale, shift):
    return (_rms(x, gain) * (1.0 + scale) + shift,)


def _resgate_fn(g, x, y, gm):
    return (x + gm * y,)


def _gate_only_fn(g, y, gm):
    return (gm * y,)


def _gdn_gates_fn(g, b_logit, a_logit, a_log, dt_bias):
    gate = -jnp.exp(a_log) * _softplus(a_logit + dt_bias)
    n = gate.shape[0]
    i = lax.broadcasted_iota(jnp.int32, (n, n), 0)
    j = lax.broadcasted_iota(jnp.int32, (n, n), 1)
    tri = (((i // CHUNK) == (j // CHUNK)) & (i >= j)).astype(F32)
    gcum = lax.dot_general(tri, gate, _NN, preferred_element_type=F32, precision=lax.Precision.HIGHEST)
    return _sigmoid(b_logit), gcum


def _gdn_outnorm_fn(g, o, z, gain):
    return (_rms(o, gain) * _silu(z),)


def _rms_fn(g, t, gain):
    return (_rms(t, gain),)


@jax.custom_vjp
def _swap_halves(t):
    return pltpu.roll(t, 32, 1)


_swap_halves.defvjp(lambda t: (pltpu.roll(t, 32, 1), None), lambda _, ct: (pltpu.roll(ct, 96, 1),))


def _head_norm_rope_fn(g, nope, rope, cosf, sins, gain_n, gain_r):
    first = lax.broadcasted_iota(jnp.int32, rope.shape, 1) < ROPE
    ss = jnp.sum(nope * nope, axis=-1, keepdims=True) + jnp.sum(jnp.where(first, rope * rope, 0.0), axis=-1,
                                                                 keepdims=True)
    r = lax.rsqrt(ss / QK_HEAD + EPS)
    tn = nope * r * gain_n
    tr = rope * r * gain_r
    rot = jnp.where(first, tr * cosf + _swap_halves(tr) * sins, 0.0)
    return tn, rot


def _q_norm_rope_fn(g, nope, rope, cosf, sins, gain_n, gain_r):
    tn, rot = _head_norm_rope_fn(g, nope, rope, cosf, sins, gain_n, gain_r)
    return (jnp.concatenate([tn, rot], axis=1),)


def _k_norm_rope_fn(g, nope, val, rope, cosf, sins, gain_n, gain_r):
    tn, rot = _head_norm_rope_fn(g, nope, rope, cosf, sins, gain_n, gain_r)
    return jnp.concatenate([tn, rot], axis=1), val


def _loss_fn(g, y, target):
    e = y - target
    return (jnp.sum(e * e, axis=-1, keepdims=True) * (0.5 / D) * jnp.ones((1, 128), F32),)


def _ffn_in(h, wg, wu, name, tm=512, tn=1408, ties=()):
    s = h.shape[0]
    tm = min(tm, s)

    def body(h_ref, wg_ref, wu_ref, *rest):
        g_ref, u_ref, a_ref = rest[-3:]
        hb = h_ref[...]
        gate = jnp.dot(hb, wg_ref[...], preferred_element_type=F32)
        up = jnp.dot(hb, wu_ref[...], preferred_element_type=F32)
        g_ref[...] = gate.astype(BF16)
        u_ref[...] = up.astype(BF16)
        a_ref[...] = (_silu(gate) * up).astype(BF16)

    spec = pl.BlockSpec((tm, tn), lambda j, i: (i, j))
    return pl.pallas_call(
        body, name=name, grid=(D_FF // tn, s // tm),
        in_specs=[pl.BlockSpec((tm, D), lambda j, i: (i, 0)), pl.BlockSpec((D, tn), lambda j, i: (0, j)),
                  pl.BlockSpec((D, tn), lambda j, i: (0, j))] + [_TIE_SPEC2] * len(ties),
        out_specs=[spec, spec, spec], out_shape=[jax.ShapeDtypeStruct((s, D_FF), BF16)] * 3,
        compiler_params=_params(("parallel", "parallel")),
    )(h, wg, wu, *ties)


def _ffn_bwd_act(dy, wo, gate, up, name, tm=512, tn=1408, ties=()):
    s = dy.shape[0]
    tm = min(tm, s)

    def body(dy_ref, wo_ref, g_ref, u_ref, *rest):
        dg_ref, du_ref = rest[-2:]
        dact = lax.dot_general(dy_ref[...], wo_ref[...], _NT, preferred_element_type=F32)
        gate = g_ref[...].astype(F32)
        up = u_ref[...].astype(F32)
        sg = _sigmoid(gate)
        dg_ref[...] = (dact * up * (sg * (1.0 + gate * (1.0 - sg)))).astype(BF16)
        du_ref[...] = (dact * (gate * sg)).astype(BF16)

    spec = pl.BlockSpec((tm, tn), lambda j, i: (i, j))
    return pl.pallas_call(
        body, name=name, grid=(D_FF // tn, s // tm),
        in_specs=[pl.BlockSpec((tm, D), lambda j, i: (i, 0)), pl.BlockSpec((tn, D), lambda j, i: (j, 0)), spec, spec]
        + [_TIE_SPEC2] * len(ties),
        out_specs=[spec, spec], out_shape=[jax.ShapeDtypeStruct((s, D_FF), BF16)] * 2,
        compiler_params=_params(("parallel", "parallel")),
    )(dy, wo, gate, up, *ties)


def _shift_down(x, d):
    rows = lax.broadcasted_iota(jnp.int32, x.shape, 0)
    return jnp.where(rows >= d, pltpu.roll(x, d, 0), 0.0)


def _shift_up(x, d):
    n = x.shape[0]
    rows = lax.broadcasted_iota(jnp.int32, x.shape, 0)
    return jnp.where(rows < n - d, pltpu.roll(x, n - d, 0), 0.0)


def _conv_post(pre, is_qk):
    a = _silu(pre)
    l2 = a * lax.rsqrt(jnp.sum(a * a, axis=-1, keepdims=True) + EPS)
    return jnp.where(is_qk, l2, a)


def _conv_pre(x, w):
    pre = x * w[CONV_K - 1:CONV_K, :]
    for j in range(CONV_K - 1):
        pre = pre + _shift_down(x, CONV_K - 1 - j) * w[j:j + 1, :]
    return pre


def _gdn_conv_fwd(pm, conv_w, name):
    s = pm.shape[0]
    nblk = 3 * D // HEAD

    def body(x_ref, w_ref, o_ref):
        is_qk = pl.program_id(0) < 2 * HEADS
        o_ref[...] = _conv_post(_conv_pre(x_ref[...], w_ref[...]), is_qk)

    return pl.pallas_call(
        body, name=name, grid=(nblk,),
        in_specs=[pl.BlockSpec((s, HEAD), lambda c: (0, c)), pl.BlockSpec((CONV_K, HEAD), lambda c: (0, c))],
        out_specs=pl.BlockSpec((s, HEAD), lambda c: (0, c)),
        out_shape=jax.ShapeDtypeStruct((s, 3 * D), F32), compiler_params=_params(("parallel",)),
    )(pm, conv_w)


def _gdn_conv_bwd(pm, conv_w, dout, part, name):
    s = pm.shape[0]
    off = part * HEADS

    def body(x_ref, w_ref, d_ref, dx_ref, dw_ref):
        x, w = x_ref[...], w_ref[...]
        _, vjp = jax.vjp(lambda p: _conv_post(p, part < 2), _conv_pre(x, w))
        dpre, = vjp(d_ref[...])
        dx = dpre * w[CONV_K - 1:CONV_K, :]
        rows = [None] * CONV_K
        rows[CONV_K - 1] = jnp.sum(dpre * x, axis=0, keepdims=True)
        for j in range(CONV_K - 1):
            dx = dx + _shift_up(dpre, CONV_K - 1 - j) * w[j:j + 1, :]
            rows[j] = jnp.sum(dpre * _shift_down(x, CONV_K - 1 - j), axis=0, keepdims=True)
        dx_ref[...] = dx
        dw_ref[...] = jnp.concatenate(rows, axis=0)

    return pl.pallas_call(
        body, name=name, grid=(HEADS,),
        in_specs=[pl.BlockSpec((s, HEAD), lambda c: (0, c + off)), pl.BlockSpec((CONV_K, HEAD), lambda c: (0, c + off)),
                  pl.BlockSpec((s, HEAD), lambda c: (0, c))],
        out_specs=[pl.BlockSpec((s, HEAD), lambda c: (0, c)), pl.BlockSpec((CONV_K, HEAD), lambda c: (0, c))],
        out_shape=[jax.ShapeDtypeStruct((s, D), F32), jax.ShapeDtypeStruct((CONV_K, D), F32)],
        compiler_params=_params(("parallel",)),
    )(pm, conv_w, dout)


def _dot3(a, b, dims=_NN):
    ah, bh = a.astype(BF16), b.astype(BF16)
    al, bl = (a - ah.astype(F32)).astype(BF16), (b - bh.astype(F32)).astype(BF16)
    d = lambda u, v: lax.dot_general(u, v, dims, preferred_element_type=F32)
    return d(ah, bh) + (d(ah, bl) + d(al, bh))


def _make_dot(hi):
    def raw(a, b, dims):
        if hi:
            return _dot3(a, b, dims)
        return lax.dot_general(a.astype(BF16), b.astype(BF16), dims, preferred_element_type=F32)

    @functools.partial(jax.custom_vjp, nondiff_argnums=(2,))
    def dot(a, b, form):
        return raw(a, b, _DIMS[form])

    def fwd(a, b, form):
        return raw(a, b, _DIMS[form]), (a, b)

    def bwd(form, res, ct):
        a, b = res
        if form == "nn":
            return raw(ct, b, _NT), raw(a, ct, _TN)
        if form == "nt":
            return raw(ct, b, _NN), raw(ct, a, _TN)
        return raw(b, ct, _NT), raw(a, ct, _NN)

    dot.defvjp(fwd, bwd)
    return dot


_dot = _make_dot(False)
_dot_hi = _make_dot(True)


def _tri_inv_raw(low):
    n = low.shape[0]
    i = lax.broadcasted_iota(jnp.int32, (n, n), 0)
    j = lax.broadcasted_iota(jnp.int32, (n, n), 1)
    eye = (i == j).astype(F32)
    hdot = _dot3
    same16 = (i // 16) == (j // 16)
    neg = jnp.where(same16, -low, 0.0)
    inv = eye + neg
    power = neg
    for _ in range(3):
        power = hdot(power, power)
        inv = hdot(inv, eye + power)
    for blk in (32, 64):
        off = jnp.where(((i // blk) == (j // blk)) & ((i // (blk // 2)) != (j // (blk // 2))), low, 0.0)
        inv = inv - hdot(inv, hdot(off, inv))
    return inv


@jax.custom_vjp
def _tri_inv(low):
    return _tri_inv_raw(low)


def _tri_inv_fwd(low):
    inv = _tri_inv_raw(low)
    return inv, inv


def _tri_inv_bwd(inv, ct):
    return (-_dot3(_dot3(inv, ct, _TN), inv, _NT),)


_tri_inv.defvjp(_tri_inv_fwd, _tri_inv_bwd)


@jax.custom_vjp
def _tri_inv_given(low, inv):
    return inv


_tri_inv_given.defvjp(lambda low, inv: (inv, inv),
                      lambda inv, ct: (_tri_inv_bwd(inv, ct)[0], jnp.zeros_like(inv)))

GROUP = 4
N_GROUPS = HEADS // GROUP
GROWS = GROUP * CHUNK


def _gdn_group(q, k, v, beta, gc, gr, states, inv=None):
    n = q.shape[0]
    i = lax.broadcasted_iota(jnp.int32, (n, n), 0)
    j = lax.broadcasted_iota(jnp.int32, (n, n), 1)
    same = (i // CHUNK) == (j // CHUNK)
    incl, strict = same & (i >= j), same & (i > j)
    qs = q * (HEAD ** -0.5)
    decay = jnp.where(incl, jnp.exp(jnp.where(incl, gc - gr, 0.0)), 0.0)
    kb = k * beta
    eg = jnp.exp(gc)
    prod = _dot(jnp.concatenate([kb, qs], axis=0), k, "nt")
    low = jnp.where(strict, prod[:n] * decay, 0.0)
    attn = jnp.where(incl, prod[n:] * decay, 0.0)
    inv = _tri_inv(low) if inv is None else _tri_inv_given(low, inv)
    sol = _dot_hi(inv, jnp.concatenate([v * beta, kb * eg], axis=1), "nn")
    u, w, qg = sol[:, :HEAD], sol[:, HEAD:], qs * eg
    last = lax.broadcasted_iota(jnp.int32, (CHUNK, 1), 0) == CHUNK - 1
    v_new, o_state, carry = [], [], []
    for h, state in enumerate(states):
        rows = slice(h * CHUNK, (h + 1) * CHUNK)
        ws = _dot(jnp.concatenate([w[rows], qg[rows]], axis=0), state, "nn")
        v_new.append(u[rows] - ws[:CHUNK])
        o_state.append(ws[CHUNK:])
        g_last = jnp.sum(jnp.where(last, gc[rows], 0.0), axis=0, keepdims=True)
        carry.append((g_last, k[rows] * jnp.exp(g_last - gc[rows])))
    o = jnp.concatenate(o_state, axis=0) + _dot(attn, jnp.concatenate(v_new, axis=0), "nn")
    new = tuple(state * jnp.exp(g_last) + _dot(k_dec, vn, "tn")
                for state, (g_last, k_dec), vn in zip(states, carry, v_new))
    return o, new, inv


def _gdn_specs(s, rev):
    nc = s // CHUNK
    at = (lambda n: nc - 1 - n) if rev else (lambda n: n)
    return nc, at, [
        pl.BlockSpec((CHUNK, D), lambda n: (at(n), 0)), pl.BlockSpec((CHUNK, D), lambda n: (at(n), 1)),
        pl.BlockSpec((CHUNK, D), lambda n: (at(n), 2)), pl.BlockSpec((CHUNK, HEAD), lambda n: (at(n), 0)),
        pl.BlockSpec((CHUNK, HEAD), lambda n: (at(n), 0)),
        pl.BlockSpec((None, N_GROUPS, 1, GROWS), lambda n: (at(n), 0, 0, 0))]


def _group_operands(grp, q_ref, k_ref, v_ref, b_blk, gc_blk, gr_blk):
    heads = range(grp * GROUP, (grp + 1) * GROUP)
    stack = lambda ref: jnp.concatenate([ref[:, h * HEAD:(h + 1) * HEAD] for h in heads], axis=0)
    col = lambda blk: jnp.concatenate([blk[:, h:h + 1] for h in heads], axis=0)
    return stack(q_ref), stack(k_ref), stack(v_ref), col(b_blk), col(gc_blk), gr_blk[grp]


def _gdn_scan_fwd(qkv, beta, gcum, grow, name):
    s = qkv.shape[0]
    nc, _, in_specs = _gdn_specs(s, rev=False)

    def body(q_ref, k_ref, v_ref, b_ref, gc_ref, gr_ref, o_ref, st_ref, inv_ref, state):
        @pl.when(pl.program_id(0) == 0)
        def _():
            state[...] = jnp.zeros_like(state)

        b_blk, gc_blk, gr_blk = b_ref[...], gc_ref[...], gr_ref[...]
        old = [state[h] for h in range(HEADS)]
        res = [_gdn_group(*_group_operands(grp, q_ref, k_ref, v_ref, b_blk, gc_blk, gr_blk),
                          old[grp * GROUP:(grp + 1) * GROUP]) for grp in range(N_GROUPS)]
        for grp, (o, new, inv) in enumerate(res):
            inv_ref[grp] = inv
            for hh in range(GROUP):
                h = grp * GROUP + hh
                st_ref[h] = old[h]
                o_ref[:, h * HEAD:(h + 1) * HEAD] = o[hh * CHUNK:(hh + 1) * CHUNK]
                state[h] = new[hh]

    return pl.pallas_call(
        body, name=name, grid=(nc,), in_specs=in_specs,
        out_specs=[pl.BlockSpec((CHUNK, D), lambda n: (n, 0)),
                   pl.BlockSpec((None, HEADS, HEAD, HEAD), lambda n: (n, 0, 0, 0)),
                   pl.BlockSpec((None, N_GROUPS, GROWS, GROWS), lambda n: (n, 0, 0, 0))],
        out_shape=[jax.ShapeDtypeStruct((s, D), F32), jax.ShapeDtypeStruct((nc, HEADS, HEAD, HEAD), F32),
                   jax.ShapeDtypeStruct((nc, N_GROUPS, GROWS, GROWS), F32)],
        scratch_shapes=[pltpu.VMEM((HEADS, HEAD, HEAD), F32)],
        compiler_params=_params(("arbitrary",)),
    )(qkv, qkv, qkv, beta, gcum, grow)


def _gdn_scan_bwd(qkv, beta, gcum, grow, states, invs, do, name):
    s = qkv.shape[0]
    nc, at, in_specs = _gdn_specs(s, rev=True)
    in_specs += [pl.BlockSpec((None, HEADS, HEAD, HEAD), lambda n: (at(n), 0, 0, 0)),
                 pl.BlockSpec((None, N_GROUPS, GROWS, GROWS), lambda n: (at(n), 0, 0, 0)),
                 pl.BlockSpec((CHUNK, D), lambda n: (at(n), 0))]

    def body(q_ref, k_ref, v_ref, b_ref, gc_ref, gr_ref, st_ref, inv_ref, do_ref, dq_ref, dk_ref, dv_ref, db_ref,
             dgc_ref, dgr_ref, dstate):
        @pl.when(pl.program_id(0) == 0)
        def _():
            dstate[...] = jnp.zeros_like(dstate)

        b_blk, gc_blk, gr_blk = b_ref[...], gc_ref[...], gr_ref[...]
        dold = [dstate[h] for h in range(HEADS)]
        res = []
        for grp in range(N_GROUPS):
            heads = range(grp * GROUP, (grp + 1) * GROUP)
            inv = inv_ref[grp]
            _, vjp = jax.vjp(lambda q, k, v, b, gc, gr, *st, inv=inv: _gdn_group(q, k, v, b, gc, gr, st, inv)[:2],
                             *_group_operands(grp, q_ref, k_ref, v_ref, b_blk, gc_blk, gr_blk),
                             *[st_ref[h] for h in heads])
            d_out = jnp.concatenate([do_ref[:, h * HEAD:(h + 1) * HEAD] for h in heads], axis=0)
            res.append(vjp((d_out, tuple(dold[h] for h in heads))))
        lane = lax.broadcasted_iota(jnp.int32, (CHUNK, HEAD), 1)
        db_all = jnp.zeros((CHUNK, HEAD), F32)
        dgc_all = jnp.zeros((CHUNK, HEAD), F32)
        for grp, (dq, dk, dv, db, dgc, dgr, *dst) in enumerate(res):
            dgr_ref[grp] = dgr
            for hh in range(GROUP):
                h = grp * GROUP + hh
                cs, rows = slice(h * HEAD, (h + 1) * HEAD), slice(hh * CHUNK, (hh + 1) * CHUNK)
                dq_ref[:, cs] = dq[rows]
                dk_ref[:, cs] = dk[rows]
                dv_ref[:, cs] = dv[rows]
                dstate[h] = dst[hh]
                db_all = jnp.where(lane == h, db[rows], db_all)
                dgc_all = jnp.where(lane == h, dgc[rows], dgc_all)
        db_ref[...] = db_all
        dgc_ref[...] = dgc_all

    blk = pl.BlockSpec((CHUNK, D), lambda n: (at(n), 0))
    gblk = pl.BlockSpec((CHUNK, HEAD), lambda n: (at(n), 0))
    return pl.pallas_call(
        body, name=name, grid=(nc,), in_specs=in_specs,
        out_specs=[blk, blk, blk, gblk, gblk, pl.BlockSpec((None, N_GROUPS, 1, GROWS), lambda n: (at(n), 0, 0, 0))],
        out_shape=[jax.ShapeDtypeStruct((s, D), F32)] * 3 + [jax.ShapeDtypeStruct((s, HEAD), F32)] * 2
        + [jax.ShapeDtypeStruct((nc, N_GROUPS, 1, GROWS), F32)],
        scratch_shapes=[pltpu.VMEM((HEADS, HEAD, HEAD), F32)],
        compiler_params=_params(("arbitrary",)),
    )(qkv, qkv, qkv, beta, gcum, grow, states, invs, do)


ATT_TILE = 512
ATT_SCALE = QK_HEAD ** -0.5


def _att_mask(t):
    qpos = lax.broadcasted_iota(jnp.int32, (t, t), 0)
    kpos = lax.broadcasted_iota(jnp.int32, (t, t), 1)
    return (kpos // CHUNK) <= (qpos // CHUNK)


ATT_STRIP = 32


def _att_strip_mask(r, t):
    kpos = lax.broadcasted_iota(jnp.int32, (ATT_STRIP, t), 1)
    return (kpos // CHUNK) <= (r * ATT_STRIP) // CHUNK


def _att_pairs(nb, by_query):
    if by_query:
        pairs = [(i, j) for i in range(nb) for j in range(i + 1)]
    else:
        pairs = [(j, i) for j in range(nb) for i in range(j, nb)]
    return jnp.array([a for a, _ in pairs], jnp.int32), jnp.array([b for _, b in pairs], jnp.int32)


def _attn_fwd(q, k, v, name):
    s = q.shape[0]
    t = min(ATT_TILE, s)
    nb = s // t
    ii, jj = _att_pairs(nb, by_query=True)

    def body(ii_ref, jj_ref, q_ref, k_ref, v_ref, o_ref, lse_ref, m_s, l_s, acc):
        step = pl.program_id(1)
        i, j = ii_ref[step], jj_ref[step]

        @pl.when(j == 0)
        def _():
            m_s[...] = jnp.full_like(m_s, -jnp.inf)
            l_s[...] = jnp.zeros_like(l_s)
            acc[...] = jnp.zeros_like(acc)

        sc = lax.dot_general(q_ref[...], k_ref[...], _NT, preferred_element_type=F32) * ATT_SCALE
        sc = lax.cond(i == j, lambda u: jnp.where(_att_mask(t), u, -jnp.inf), lambda u: u, sc)
        m_new = jnp.maximum(m_s[...], jnp.max(sc, axis=-1, keepdims=True))
        alpha = jnp.exp(m_s[...] - m_new)
        p = jnp.exp(sc - m_new)
        l_s[...] = alpha * l_s[...] + jnp.sum(p, axis=-1, keepdims=True)
        acc[...] = alpha * acc[...] + jnp.dot(p.astype(BF16), v_ref[...], preferred_element_type=F32)
        m_s[...] = m_new

        @pl.when(j == i)
        def _():
            o_ref[...] = acc[...] / l_s[...]
            lse_ref[...] = m_s[...] + jnp.log(l_s[...])

    grid_spec = pltpu.PrefetchScalarGridSpec(
        num_scalar_prefetch=2, grid=(HEADS, len(ii)),
        in_specs=[pl.BlockSpec((t, HEAD_PAD), lambda h, n, ir, jr: (ir[n], h)),
                  pl.BlockSpec((t, HEAD_PAD), lambda h, n, ir, jr: (jr[n], h)),
                  pl.BlockSpec((t, HEAD), lambda h, n, ir, jr: (jr[n], h))],
        out_specs=[pl.BlockSpec((t, HEAD), lambda h, n, ir, jr: (ir[n], h)),
                   pl.BlockSpec((None, t, 1), lambda h, n, ir, jr: (h, ir[n], 0))],
        scratch_shapes=[pltpu.VMEM((t, 1), F32), pltpu.VMEM((t, 1), F32), pltpu.VMEM((t, HEAD), F32)])
    return pl.pallas_call(
        body, name=name, grid_spec=grid_spec,
        out_shape=[jax.ShapeDtypeStruct((s, HEADS * HEAD), F32), jax.ShapeDtypeStruct((HEADS, s, 1), F32)],
        compiler_params=_params(("parallel", "arbitrary")),
    )(ii, jj, q, k, v)


def _attn_bwd(q, k, v, do, o, lse, name):
    s = q.shape[0]
    t = min(ATT_TILE, s)
    nb = s // t
    jj, ii = _att_pairs(nb, by_query=False)

    def body(jj_ref, ii_ref, q_ref, k_ref, v_ref, do_ref, o_ref, lse_ref, dq_ref, dk_ref, dv_ref, dk_acc, dv_acc,
             sc_s, dp_s, p_s, ds_s, dl_s):
        step = pl.program_id(1)
        i, j = ii_ref[step], jj_ref[step]

        @pl.when(step == 0)
        def _():
            dq_ref[...] = jnp.zeros_like(dq_ref)

        @pl.when(i == j)
        def _():
            dk_acc[...] = jnp.zeros_like(dk_acc)
            dv_acc[...] = jnp.zeros_like(dv_acc)

        do_f = do_ref[...]
        dob = do_f.astype(BF16)
        dl_s[...] = jnp.sum(do_f * o_ref[...], axis=-1, keepdims=True)
        sc_s[...] = lax.dot_general(q_ref[...], k_ref[...], _NT, preferred_element_type=F32)
        dp_s[...] = lax.dot_general(dob, v_ref[...], _NT, preferred_element_type=F32)

        def softmax_strips(diagonal):
            for r in range(t // ATT_STRIP):
                rows = slice(r * ATT_STRIP, (r + 1) * ATT_STRIP)
                p = jnp.exp(sc_s[rows, :] * ATT_SCALE - lse_ref[rows, :])
                if diagonal:
                    p = jnp.where(_att_strip_mask(r, t), p, 0.0)
                p_s[rows, :] = p.astype(BF16)
                ds_s[rows, :] = (p * (dp_s[rows, :] - dl_s[rows, :]) * ATT_SCALE).astype(BF16)

        pl.when(i == j)(functools.partial(softmax_strips, True))
        pl.when(i != j)(functools.partial(softmax_strips, False))
        ds = ds_s[...]
        dv_acc[...] += lax.dot_general(p_s[...], dob, _TN, preferred_element_type=F32)
        dk_acc[...] += lax.dot_general(ds, q_ref[...], _TN, preferred_element_type=F32)
        rows = pl.ds(pl.multiple_of(i * t, t), t)
        dq_ref[rows, :] += jnp.dot(ds, k_ref[...], preferred_element_type=F32)

        @pl.when(i == nb - 1)
        def _():
            dk_ref[...] = dk_acc[...]
            dv_ref[...] = dv_acc[...]

    grid_spec = pltpu.PrefetchScalarGridSpec(
        num_scalar_prefetch=2, grid=(HEADS, len(jj)),
        in_specs=[pl.BlockSpec((t, HEAD_PAD), lambda h, n, jr, ir: (ir[n], h)),
                  pl.BlockSpec((t, HEAD_PAD), lambda h, n, jr, ir: (jr[n], h)),
                  pl.BlockSpec((t, HEAD), lambda h, n, jr, ir: (jr[n], h)),
                  pl.BlockSpec((t, HEAD), lambda h, n, jr, ir: (ir[n], h)),
                  pl.BlockSpec((t, HEAD), lambda h, n, jr, ir: (ir[n], h)),
                  pl.BlockSpec((None, t, 1), lambda h, n, jr, ir: (h, ir[n], 0))],
        out_specs=[pl.BlockSpec((s, HEAD_PAD), lambda h, n, jr, ir: (0, h)),
                   pl.BlockSpec((t, HEAD_PAD), lambda h, n, jr, ir: (jr[n], h)),
                   pl.BlockSpec((t, HEAD), lambda h, n, jr, ir: (jr[n], h))],
        scratch_shapes=[pltpu.VMEM((t, HEAD_PAD), F32), pltpu.VMEM((t, HEAD), F32), pltpu.VMEM((t, t), F32),
                        pltpu.VMEM((t, t), F32), pltpu.VMEM((t, t), BF16), pltpu.VMEM((t, t), BF16),
                        pltpu.VMEM((t, 1), F32)])
    return pl.pallas_call(
        body, name=name, grid_spec=grid_spec,
        out_shape=[jax.ShapeDtypeStruct((s, HEADS * HEAD_PAD), F32)] * 2 + [jax.ShapeDtypeStruct((s, HEADS * HEAD), F32)],
        compiler_params=_params(("parallel", "arbitrary")),
    )(jj, ii, q, k, v, do, o, lse)


def _rope_tables(positions):
    half = ROPE // 2
    inv_freq = ROPE_BASE ** (-jnp.arange(half, dtype=F32) / half)
    ang = positions.astype(F32)[:, None] * inv_freq
    cos, sin = jnp.cos(ang), jnp.sin(ang)
    return jnp.concatenate([cos] * 4, axis=1), jnp.concatenate([-sin, sin] * 2, axis=1)


def _loss_and_grad(y, target, name):
    s = y.shape[0]
    ts = min(ROW_TILE, s)

    def body(y_ref, t_ref, dy_ref, l_ref):
        e = y_ref[...] - t_ref[...]
        dy_ref[...] = e * (1.0 / D)
        part = jnp.sum(jnp.sum(e * e, axis=-1, keepdims=True) * (0.5 / D), axis=0, keepdims=True)
        part = part * jnp.ones((1, 128), F32)

        @pl.when(pl.program_id(0) == 0)
        def _():
            l_ref[...] = part

        @pl.when(pl.program_id(0) > 0)
        def _():
            l_ref[...] += part

    return pl.pallas_call(
        body, name=name, grid=(s // ts,),
        in_specs=[pl.BlockSpec((ts, D), lambda i: (i, 0))] * 2,
        out_specs=[pl.BlockSpec((ts, D), lambda i: (i, 0)), pl.BlockSpec((1, 128), lambda i: (0, 0))],
        out_shape=[jax.ShapeDtypeStruct((s, D), F32), jax.ShapeDtypeStruct((1, 128), F32)],
        compiler_params=_params(("arbitrary",)),
    )(y, target)


ANY = pl.BlockSpec(memory_space=pl.ANY)


def _all_gather(shard, name):
    def body(x_ref, out_ref, send_sems, recv_sems, local_sem):
        x, y, c = lax.axis_index("x"), lax.axis_index("y"), lax.axis_index("c")
        me, sibling = (x, y, c), (x, y, 1 - c)
        chips = [(1 - x, y), (x, 1 - y), (1 - x, 1 - y)]

        def rows(px, py, pc):
            return out_ref.at[4 * px + 2 * py + pc]

        def copy(k, block, to, src=None):
            return pltpu.make_async_remote_copy(
                src_ref=rows(*block) if src is None else src, dst_ref=rows(*block),
                send_sem=send_sems.at[k], recv_sem=recv_sems.at[k], device_id=to, device_id_type=MESH)

        mine = pltpu.make_async_copy(x_ref, rows(*me), local_sem)
        mine.start()
        first = [copy(0, me, sibling, src=x_ref)]
        first += [copy(1 + j, me, (*chip, c), src=x_ref) for j, chip in enumerate(chips)]
        for cp in first:
            cp.start()
        passed = [copy(4 + j, (*chip, c), sibling) for j, chip in enumerate(chips)]
        for j, chip in enumerate(chips):
            copy(1 + j, (*chip, c), me).wait_recv()
            passed[j].start()
        copy(0, sibling, me).wait_recv()
        for j, chip in enumerate(chips):
            copy(4 + j, (*chip, 1 - c), me).wait_recv()
        for cp in first + passed:
            cp.wait_send()
        mine.wait()

    return pl.pallas_call(
        body, name=name, out_shape=jax.ShapeDtypeStruct((N_DEV,) + shard.shape, shard.dtype),
        in_specs=[ANY], out_specs=ANY,
        scratch_shapes=[pltpu.SemaphoreType.DMA((7,)), pltpu.SemaphoreType.DMA((7,)), pltpu.SemaphoreType.DMA],
    )(shard)


def _exchange(blocks, name):
    def body(x_ref, out_ref, send_sems, recv_sems, local_sem):
        x, y, c = lax.axis_index("x"), lax.axis_index("y"), lax.axis_index("c")
        me = 4 * x + 2 * y + c
        mine = pltpu.make_async_copy(x_ref.at[me], out_ref.at[me], local_sem)
        mine.start()
        copies = []
        for k in range(1, N_DEV):
            px = 1 - x if k & 4 else x
            py = 1 - y if k & 2 else y
            pc = 1 - c if k & 1 else c
            peer = 4 * px + 2 * py + pc
            cp = pltpu.make_async_remote_copy(
                src_ref=x_ref.at[peer], dst_ref=out_ref.at[me], send_sem=send_sems.at[k - 1],
                recv_sem=recv_sems.at[k - 1], device_id=(px, py, pc), device_id_type=MESH)
            cp.start()
            copies.append((cp, pltpu.make_async_remote_copy(
                src_ref=x_ref.at[peer], dst_ref=out_ref.at[peer], send_sem=send_sems.at[k - 1],
                recv_sem=recv_sems.at[k - 1], device_id=(px, py, pc), device_id_type=MESH)))
        for cp, landing in copies:
            landing.wait_recv()
        for cp, landing in copies:
            cp.wait_send()
        mine.wait()

    return pl.pallas_call(
        body, name=name, out_shape=jax.ShapeDtypeStruct(blocks.shape, blocks.dtype),
        in_specs=[ANY], out_specs=ANY,
        scratch_shapes=[pltpu.SemaphoreType.DMA((7,)), pltpu.SemaphoreType.DMA((7,)), pltpu.SemaphoreType.DMA],
    )(blocks)


HBM = pl.BlockSpec(memory_space=pltpu.HBM)
SEM = pl.BlockSpec(memory_space=pltpu.SEMAPHORE)
EFFECT = pltpu.SideEffectType.DATAFLOW_SIDE_EFFECTING


def _peers():
    x, y, c = lax.axis_index("x"), lax.axis_index("y"), lax.axis_index("c")
    peers = []
    for k in range(1, N_DEV):
        px = 1 - x if k & 4 else x
        py = 1 - y if k & 2 else y
        pc = 1 - c if k & 1 else c
        peers.append(((px, py, pc), 4 * px + 2 * py + pc))
    return 4 * x + 2 * y + c, peers


def _send_start(srcs, name, gather):
    n = len(srcs)
    lands = [((N_DEV,) + s.shape) if gather else s.shape for s in srcs]

    def body(*refs):
        src_refs, land_refs = refs[:n], refs[n:2 * n]
        send_sems, recv_sems, token = refs[2 * n], refs[2 * n + 1], refs[-1]
        me, peers = _peers()
        for i in range(n):
            for k, (dev, idx) in enumerate(peers):
                pltpu.make_async_remote_copy(
                    src_ref=src_refs[i] if gather else src_refs[i].at[idx], dst_ref=land_refs[i].at[me],
                    send_sem=send_sems.at[7 * i + k], recv_sem=recv_sems.at[7 * i + k], device_id=dev,
                    device_id_type=MESH).start()
        token[...] = jnp.zeros_like(token)

    res = pl.pallas_call(
        body, name=name,
        out_shape=(pltpu.SemaphoreType.DMA((7 * n,)), pltpu.SemaphoreType.DMA((7 * n,)),
                   *[pltpu.HBM(s.shape, s.dtype) for s in srcs],
                   *[pltpu.HBM(shape, s.dtype) for shape, s in zip(lands, srcs)],
                   jax.ShapeDtypeStruct((8, 128), F32)),
        in_specs=(HBM,) * (2 * n), out_specs=(SEM, SEM) + (HBM,) * (2 * n) + (pl.BlockSpec(memory_space=pltpu.VMEM),),
        input_output_aliases={i: 2 + i for i in range(2 * n)},
        compiler_params=pltpu.CompilerParams(has_side_effects=EFFECT),
    )(*[pltpu.with_memory_space_constraint(s, pltpu.HBM) for s in srcs],
      *[pltpu.with_memory_space_constraint(lax.empty(shape, s.dtype), pltpu.HBM) for shape, s in zip(lands, srcs)])
    return dict(sems=res[:2], srcs=res[2:2 + n], lands=res[2 + n:2 + 2 * n], token=res[-1])


def _send_wait(handle, after, name, gather):
    n = len(handle["srcs"])

    def body(*refs):
        src_refs, land_refs = refs[:n], refs[n:2 * n]
        send_sems, recv_sems = refs[2 * n], refs[2 * n + 1]
        me, peers = _peers()
        for i in range(n):
            for k, (dev, idx) in enumerate(peers):
                cp = pltpu.make_async_remote_copy(
                    src_ref=src_refs[i] if gather else src_refs[i].at[idx], dst_ref=land_refs[i].at[idx],
                    send_sem=send_sems.at[7 * i + k], recv_sem=recv_sems.at[7 * i + k], device_id=dev,
                    device_id_type=MESH)
                cp.wait_send()
                cp.wait_recv()

    both = list(handle["srcs"]) + list(handle["lands"])
    res = pl.pallas_call(
        body, name=name, out_shape=tuple(pltpu.HBM(t.shape, t.dtype) for t in both),
        in_specs=(HBM,) * (2 * n) + (SEM, SEM, pl.BlockSpec(memory_space=pl.ANY)), out_specs=(HBM,) * (2 * n),
        input_output_aliases={i: i for i in range(2 * n)},
        compiler_params=pltpu.CompilerParams(has_side_effects=EFFECT),
    )(*both, *handle["sems"], after)
    return res[:n], res[n:]


def _adamw(parts, w, m, v, name, tr=128):
    pieces = len(parts)
    n, r, wd = parts[0].shape
    tr = next((t for t in (tr, 64, 32, 16) if r % t == 0), r)
    nrt = r // tr

    def body(*refs):
        w_ref, m_ref, v_ref, g_ref, d_ref, nm_ref, nv_ref = refs[pieces:]

        def update(p_ref):
            g = p_ref[0].astype(F32)
            for k in range(1, n):
                g = g + p_ref[k].astype(F32)
            m_new = B1 * m_ref[...] + (1.0 - B1) * g
            v_new = B2 * v_ref[...] + (1.0 - B2) * (g * g)
            m_hat = m_new / (1.0 - B1 ** STEP)
            v_hat = v_new / (1.0 - B2 ** STEP)
            g_ref[...] = g
            d_ref[...] = -LR * (m_hat / (jnp.sqrt(v_hat) + ADAM_EPS) + WD * w_ref[...])
            nm_ref[...] = m_new
            nv_ref[...] = v_new

        for p in range(pieces):
            pl.when(pl.program_id(0) == p)(functools.partial(update, refs[p]))

    part_spec = lambda p: pl.BlockSpec((n, tr, wd), lambda l, i: (0, jnp.clip(i + (l - p) * nrt, 0, nrt - 1), 0))
    blk = pl.BlockSpec((tr, wd), lambda l, i: (l * nrt + i, 0))
    return pl.pallas_call(
        body, name=name, grid=(pieces, nrt),
        in_specs=[part_spec(p) for p in range(pieces)] + [blk, blk, blk],
        out_specs=[blk] * 4, out_shape=[jax.ShapeDtypeStruct((pieces * r, wd), F32)] * 4,
        compiler_params=_params(("arbitrary", "arbitrary")),
    )(*parts, w, m, v)


def _outer8(ct, dm, name):
    k, n = ct.shape[0], dm.shape[1]

    def body(c_ref, d_ref, o_ref):
        cv, dv = c_ref[...], d_ref[...]
        acc = cv[:, 0:1] * dv[0:1, :]
        for s in range(1, N_DEV):
            acc = acc + cv[:, s:s + 1] * dv[s:s + 1, :]
        o_ref[...] = acc

    tk = 256
    return pl.pallas_call(
        body, name=name, grid=(k // tk,),
        in_specs=[pl.BlockSpec((tk, N_DEV), lambda i: (i, 0)), pl.BlockSpec((N_DEV, n), lambda i: (0, 0))],
        out_specs=pl.BlockSpec((tk, n), lambda i: (i, 0)), out_shape=jax.ShapeDtypeStruct((k, n), F32),
        compiler_params=_params(("parallel",)),
    )(ct, dm)


FULL = (0, D)
C128 = (0, 128)
HEAD_NOPE = [(h * HEAD_PAD, NOPE) for h in range(HEADS)]
HEAD_ROPE = [(h * HEAD_PAD + NOPE, 128) for h in range(HEADS)]
HEAD_ALL = [(h * HEAD_PAD, HEAD_PAD) for h in range(HEADS)]
HEAD_V = [(h * HEAD, HEAD) for h in range(HEADS)]


def _modulate(x, p, ties=()):
    return _rowwise_fwd(_modulate_fn, [(x, FULL)], [p["gain"], p["scale"], p["shift"]], [(D, BF16, FULL)], "modulate",
                        ties=ties)[0]


def _residual_bwd(y, gm, dxn):
    return _rowwise_bwd(_gate_only_fn, [(y, FULL)], [gm], [(D, F32, FULL)], [dxn], [(0, FULL)], [(D, BF16)],
                        "residual_bwd")


def _modulate_bwd(x, p, dh, dx_in, prev=None):
    pars = [p["gain"], p["scale"], p["shift"]]
    if prev is None:
        return list(_rowwise_bwd(_modulate_fn, [(x, FULL)], pars, [(D, BF16, FULL)], [dh], [(0, FULL)], [(D, F32)],
                                 "modulate_bwd", add={0: dx_in})) + [None]
    s = x.shape[0]
    ts = min(ROW_TILE, s)

    def body(x_ref, g_ref, sc_ref, sh_ref, dh_ref, din_ref, y_ref, gm_ref, dx_ref, dy_ref, dg_ref, dsc_ref, dsh_ref,
             dgm_ref):
        _, vjp = jax.vjp(lambda *t: _modulate_fn(0, *t)[0], x_ref[...], g_ref[...], sc_ref[...], sh_ref[...])
        dxm, dg, dsc, dsh = vjp(dh_ref[...])
        dx = dxm + din_ref[...]
        dx_ref[...] = dx
        dy_ref[...] = (gm_ref[...] * dx).astype(BF16)
        sums = (dg, dsc, dsh, jnp.sum(dx * y_ref[...], axis=0, keepdims=True))
        first = pl.program_id(0) == 0
        for ref, val in zip((dg_ref, dsc_ref, dsh_ref, dgm_ref), sums):
            @pl.when(first)
            def _(ref=ref, val=val):
                ref[...] = val

            @pl.when(jnp.logical_not(first))
            def _(ref=ref, val=val):
                ref[...] += val

    blk = pl.BlockSpec((ts, D), lambda i: (i, 0))
    vec = pl.BlockSpec((1, D), lambda i: (0, 0))
    dx, dy, dg, dsc, dsh, dgm = pl.pallas_call(
        body, name="modulate_bwd_chain", grid=(s // ts,),
        in_specs=[blk, vec, vec, vec, blk, blk, blk, vec], out_specs=[blk, blk, vec, vec, vec, vec],
        out_shape=[jax.ShapeDtypeStruct((s, D), F32), jax.ShapeDtypeStruct((s, D), BF16)]
        + [jax.ShapeDtypeStruct((1, D), F32)] * 4,
        compiler_params=_params(("arbitrary",)),
    )(x, *pars, dh, dx_in, prev[0], prev[1])
    return [dx, dg, dsc, dsh, (dy, dgm)]


def _out_proj(a, w, x, p, nxt, name, **kw):
    res = _matmul([(a, w)], "nn", name, resid=(x, p["gm"]) + tuple(nxt or ()), **kw)
    return res[1], res[0], (res[2] if nxt else None)


def _ffn_fwd(x, p, ties=(), h=None, nxt=None):
    if h is None:
        h, ties = _modulate(x, p, ties), ()
    gate, up, act = _ffn_in(h, p["wg"], p["wu"], "ffn_in", ties=ties)
    xn, y, hn = _out_proj(act, p["wo"], x, p, nxt, "ffn_out", tm=512, tk=D_FF)
    return xn, dict(x=x, h=h, gate=gate, up=up, act=act, y=y), hn


def _ffn_bwd(t, p, dxn, res=None, prev=None, ties=()):
    dy, dgm = res or _residual_bwd(t["y"], p["gm"], dxn)
    dgate, dup = _ffn_bwd_act(dy, p["wo"], t["gate"], t["up"], "ffn_bwd_act", ties=ties)
    dwo = _mm(t["act"], dy, "tn", "ffn_dwo", ties=ties)
    dh = _matmul([(dgate, p["wg"]), (dup, p["wu"])], "nt", "ffn_dh")
    dwg = _mm(t["h"], dgate, "tn", "ffn_dwi")
    dwu = _mm(t["h"], dup, "tn", "ffn_dwi")
    dx, dgain, dscale, dshift, res_prev = _modulate_bwd(t["x"], p, dh, dxn, prev)
    return dx, dict(gain=dgain, scale=dscale, shift=dshift, gm=dgm, wg=dwg, wu=dwu, wo=dwo), res_prev


def _pad128(t):
    return jnp.pad(t, ((0, 0), (0, 128 - t.shape[1])))


def _gdn_fwd(x, p, ties=(), h=None, nxt=None):
    s = x.shape[0]
    if h is None:
        h, ties = _modulate(x, p, ties), ()
    pm = _mm(h, p["w_main"], "nn", "gdn_proj", ties=ties)
    tail = _mm(h, p["w_tail"], "nn", "gdn_proj_tail", ties=ties)
    qkv = _gdn_conv_fwd(pm, p["conv_w"], "gdn_conv")
    beta, gcum = _rowwise_fwd(_gdn_gates_fn, [(tail, C128), (tail, (128, 128))], [p["a_log"], p["dt_bias"]],
                              [(128, F32, C128)] * 2, "gdn_gates")
    grow = gcum[:, :HEADS].reshape(s // CHUNK, CHUNK, N_GROUPS, GROUP).transpose(0, 2, 3, 1)
    grow = grow.reshape(s // CHUNK, N_GROUPS, 1, GROWS)
    o, states, invs = _gdn_scan_fwd(qkv, beta, gcum, grow, "gdn_scan")
    on, = _rowwise_fwd(_gdn_outnorm_fn, [(o, HEAD_V), (pm, [(3 * D + h_ * HEAD, HEAD) for h_ in range(HEADS)])],
                       [p["norm_g"]], [(D, BF16, HEAD_V)], "gdn_outnorm", groups=HEADS)
    xn, y, hn = _out_proj(on, p["w_out"], x, p, nxt, "mix_out", tm=512)
    t = dict(x=x, h=h, pm=pm, tail=tail, qkv=qkv, beta=beta, gcum=gcum, grow=grow, o=o, states=states, invs=invs,
             on=on, y=y)
    return xn, t, hn


def _gdn_bwd(t, p, dxn, res=None, prev=None, ties=()):
    s = dxn.shape[0]
    zc = [(3 * D + h_ * HEAD, HEAD) for h_ in range(HEADS)]
    dy, dgm = res or _residual_bwd(t["y"], p["gm"], dxn)
    dw_out = _mm(t["on"], dy, "tn", "mix_dwo", ties=ties)
    don = _mm(dy, p["w_out"], "nt", "mix_dout", ties=ties)
    do, dz, dnorm_g = _rowwise_bwd(_gdn_outnorm_fn, [(t["o"], HEAD_V), (t["pm"], zc)], [p["norm_g"]],
                                   [(D, BF16, HEAD_V)], [don], [(0, HEAD_V), (1, HEAD_V)], [(D, F32), (D, F32)],
                                   "gdn_outnorm_bwd", groups=HEADS)
    dq, dk, dv, dbeta, dg, dgr = _gdn_scan_bwd(t["qkv"], t["beta"], t["gcum"], t["grow"], t["states"], t["invs"], do,
                                               "gdn_scan_bwd")
    dg = dg + _pad128(dgr.reshape(s // CHUNK, N_GROUPS, GROUP, CHUNK).transpose(0, 3, 1, 2).reshape(s, HEADS))
    dtail, da_log, ddt = _rowwise_bwd(_gdn_gates_fn, [(t["tail"], C128), (t["tail"], (128, 128))],
                                      [p["a_log"], p["dt_bias"]], [(128, F32, C128)] * 2, [dbeta, dg],
                                      [(0, C128), (0, (128, 128))], [(256, F32)], "gdn_gates_bwd")
    dxs, dcw = [], []
    for part, d in enumerate((dq, dk, dv)):
        dx_, dw_ = _gdn_conv_bwd(t["pm"], p["conv_w"], d, part, "gdn_conv_bwd")
        dxs.append(dx_)
        dcw.append(dw_)
    pieces = dxs + [dz]
    dh = _matmul([(d, p["w_main"]) for d in pieces] + [(dtail, p["w_tail"])], "nt", "gdn_dh",
                 boffs=[0, D, 2 * D, 3 * D, 0], tk=512)
    dw_main = [_mm(t["h"], d, "tn", "gdn_dwi") for d in pieces]
    dw_tail = _mm(t["h"], dtail, "tn", "gdn_dwi_tail")
    dx, dgain, dscale, dshift, res_prev = _modulate_bwd(t["x"], p, dh, dxn, prev)
    return dx, dict(gain=dgain, scale=dscale, shift=dshift, gm=dgm, w_main=jnp.concatenate(dw_main, axis=1),
                    w_tail=dw_tail, conv_w=jnp.concatenate(dcw, axis=1), a_log=da_log, dt_bias=ddt,
                    norm_g=dnorm_g, w_out=dw_out), res_prev


def _q_rows(q2, cosf, sins):
    return [(q2, HEAD_NOPE), (q2, HEAD_ROPE), (cosf, C128), (sins, C128)]


def _mla_fwd(x, p, kv, ties=(), h=None, nxt=None):
    if h is None:
        h, ties = _modulate(x, p, ties), ()
    cq = _mm(h, p["w_dq"], "nn", "mla_dq", ties=ties)
    cqn, = _rowwise_fwd(_rms_fn, [(cq, (0, Q_LORA))], [p["q_lora_g"]], [(Q_LORA, BF16, (0, Q_LORA))], "mla_qlora_norm")
    q2 = _mm(cqn, p["w_uq"], "nn", "mla_uq")
    qn, = _rowwise_fwd(_q_norm_rope_fn, _q_rows(q2, kv["cosf"], kv["sins"]), [p["q_gn"], p["q_gr"]],
                       [(HEADS * HEAD_PAD, BF16, HEAD_ALL)], "mla_q_norm", groups=HEADS)
    o, lse = _attn_fwd(qn, kv["kn"], kv["vb"], "mla_attn")
    xn, y, hn = _out_proj(o, p["w_out"], x, p, nxt, "mix_out", tm=512)
    return xn, dict(x=x, h=h, cq=cq, cqn=cqn, q2=q2, qn=qn, o=o, lse=lse, y=y), hn


def _mla_bwd(t, p, kv, dxn, res=None, prev=None, ties=()):
    dy, dgm = res or _residual_bwd(t["y"], p["gm"], dxn)
    dw_out = _mm(t["o"], dy, "tn", "mix_dwo", ties=ties)
    do = _mm(dy, p["w_out"], "nt", "mix_dout", ties=ties)
    dq, dk, dv = _attn_bwd(t["qn"], kv["kn"], kv["vb"], do, t["o"], t["lse"], "mla_attn_bwd")
    dq2, dq_gn, dq_gr = _rowwise_bwd(_q_norm_rope_fn, _q_rows(t["q2"], kv["cosf"], kv["sins"]), [p["q_gn"], p["q_gr"]],
                                     [(HEADS * HEAD_PAD, BF16, HEAD_ALL)], [dq],
                                     [(0, HEAD_NOPE), (0, HEAD_ROPE), None, None], [(HEADS * HEAD_PAD, F32)],
                                     "mla_q_norm_bwd", groups=HEADS)
    dw_uq = _mm(t["cqn"], dq2, "tn", "mla_dwuq")
    dcqn = _mm(dq2, p["w_uq"], "nt", "mla_dcq")
    dcq, dq_lora_g = _rowwise_bwd(_rms_fn, [(t["cq"], (0, Q_LORA))], [p["q_lora_g"]], [(Q_LORA, BF16, (0, Q_LORA))],
                                  [dcqn], [(0, (0, Q_LORA))], [(Q_LORA, F32)], "mla_qlora_norm_bwd")
    dw_dq = _mm(t["h"], dcq, "tn", "mla_dwdq")
    dh = _mm(dcq, p["w_dq"], "nt", "mla_dh")
    dx, dgain, dscale, dshift, res_prev = _modulate_bwd(t["x"], p, dh, dxn, prev)
    grads = dict(gain=dgain, scale=dscale, shift=dshift, gm=dgm, w_dq=dw_dq, q_lora_g=dq_lora_g, w_uq=dw_uq,
                 q_gn=dq_gn, q_gr=dq_gr, w_out=dw_out)
    return dx, grads, res_prev, dk, dv


def _k_rows(kvp, ckv, cosf, sins):
    return [(kvp, HEAD_NOPE), (kvp, HEAD_ROPE), (ckv, (KV_LORA, 128)), (cosf, C128), (sins, C128)]


def _kv_fwd(x, p, cosf, sins):
    h = _modulate(x, p)
    ckv = _mm(h, p["w_dkv"], "nn", "kv_down")
    lat, = _rowwise_fwd(_rms_fn, [(ckv, (0, KV_LORA))], [p["kv_g"]], [(KV_LORA, BF16, (0, KV_LORA))], "kv_norm")
    kvp = _mm(lat, p["w_ukv"], "nn", "kv_up")
    kn, vb = _rowwise_fwd(_k_norm_rope_fn, _k_rows(kvp, ckv, cosf, sins), [p["k_gn"], p["k_gr"]],
                          [(HEADS * HEAD_PAD, BF16, HEAD_ALL), (HEADS * HEAD, BF16, HEAD_V)], "kv_k_norm",
                          groups=HEADS)
    return dict(x=x, h=h, ckv=ckv, lat=lat, kvp=kvp, kn=kn, vb=vb, cosf=cosf, sins=sins)


def _kv_bwd(t, p, dk, dv, dx_in, prev):
    dkvp, drope, dk_gn, dk_gr = _rowwise_bwd(
        _k_norm_rope_fn, _k_rows(t["kvp"], t["ckv"], t["cosf"], t["sins"]), [p["k_gn"], p["k_gr"]],
        [(HEADS * HEAD_PAD, BF16, HEAD_ALL), (HEADS * HEAD, BF16, HEAD_V)], [dk, dv],
        [(0, HEAD_NOPE), (0, HEAD_ROPE), (1, C128), None, None], [(HEADS * HEAD_PAD, F32), (128, F32)],
        "kv_k_norm_bwd", groups=HEADS)
    dw_ukv = _mm(t["lat"], dkvp, "tn", "kv_dwukv")
    dlat = _mm(dkvp, p["w_ukv"], "nt", "kv_dlat")
    dckv, dkv_g = _rowwise_bwd(_rms_fn, [(t["ckv"], (0, KV_LORA))], [p["kv_g"]], [(KV_LORA, BF16, (0, KV_LORA))],
                               [dlat], [(0, (0, KV_LORA))], [(KV_LORA, F32)], "kv_norm_bwd")
    dw_dkv = jnp.concatenate([_mm(t["h"], dckv, "tn", "kv_dwdkv"), _mm(t["h"], drope, "tn", "kv_dwdkv_rope")], axis=1)
    dh = _matmul([(dckv, p["w_dkv"]), (drope, p["w_dkv"])], "nt", "kv_dh", boffs=[0, KV_LORA])
    dx, dgain, dscale, dshift, res_prev = _modulate_bwd(t["x"], p, dh, dx_in, prev)
    return dx, dict(gain=dgain, scale=dscale, shift=dshift, w_dkv=dw_dkv, kv_g=dkv_g, w_ukv=dw_ukv, k_gn=dk_gn,
                    k_gr=dk_gr), res_prev


WEIGHTS = ["ada_w", "ada_b", "norm_g", "ffn_w_in", "ffn_w_out", "gdn_w_in", "gdn_conv_w", "gdn_a_log", "gdn_dt_bias",
           "gdn_norm_g", "gdn_w_out", "kv_ada_w", "kv_ada_b", "kv_norm_g", "mla_w_dkv", "mla_kv_norm_g", "mla_w_ukv",
           "mla_k_norm_g", "mla_w_dq", "mla_q_lora_norm_g", "mla_w_uq", "mla_q_norm_g", "mla_w_out"]
SMALL = [("ada_b", 4 * N_MOD * D), ("kv_ada_b", 2 * D), ("norm_g", DEPTH * 3 * D), ("gdn_conv_w", N_A * CONV_K * 3 * D),
         ("gdn_a_log", N_A * HEADS), ("gdn_dt_bias", N_A * HEADS), ("gdn_norm_g", N_A * HEAD), ("kv_norm_g", D),
         ("mla_kv_norm_g", KV_LORA), ("mla_k_norm_g", QK_HEAD), ("mla_q_lora_norm_g", 2 * Q_LORA),
         ("mla_q_norm_g", 2 * QK_HEAD)]
SMALL_REPLICATED = [n for n, _ in SMALL if n not in ("norm_g", "gdn_conv_w")]


def _silu_fn(g, t):
    return (_silu(t),)


def _dup_rope(t):
    return jnp.concatenate([t[..., :NOPE], t[..., NOPE:], t[..., NOPE:]], axis=-1)


def _fold_rope(t):
    return jnp.concatenate([t[..., :NOPE], t[..., NOPE:QK_HEAD] + t[..., QK_HEAD:]], axis=-1)


def _pack(pieces, rows):
    flat = jnp.concatenate([p.reshape(-1).astype(F32) for p in pieces])
    return jnp.pad(flat, (0, rows * 128 - flat.shape[0])).reshape(rows, 128)


def _step(a):
    me = 4 * lax.axis_index("x") + 2 * lax.axis_index("y") + lax.axis_index("c")
    x = a["x"][0]
    cosf, sins = _rope_tables(a["positions"][0])

    n_in = 2 * D_FF // N_DEV
    n_gdn = (4 * D + 2 * HEADS) // N_DEV
    AHEAD = 2

    stages = [(l, part) for l in range(DEPTH) for part in range(3)]

    def stage_shards(l, part):
        if part != 1:
            sh = {"ffn_w_in": a["ffn_w_in"][l, part // 2], "ffn_w_out": a["ffn_w_out"][l, part // 2]}
            if part == 2 and l == N_A - 1:
                sh.update(mla_w_dkv=a["mla_w_dkv"], mla_w_ukv=a["mla_w_ukv"])
            return sh
        if l < N_A:
            return {"gdn_w_in": a["gdn_w_in"][l], "gdn_w_out": a["gdn_w_out"][l]}
        j = l - N_A
        return {"mla_w_dq": a["mla_w_dq"][j], "mla_w_uq": a["mla_w_uq"][j], "mla_w_out": a["mla_w_out"][j]}

    def zero_of(t):
        return jnp.minimum(jnp.abs(t[(0,) * t.ndim].astype(F32)), 0.0)

    def start_stage(l, part, tie):
        sh = stage_shards(l, part)
        return list(sh), _send_start([(w + tie).astype(BF16) for w in sh.values()], f"fetch_start_{l}_{part}", gather=True)

    def finish_stage(l, part, names, handle, after):
        srcs, lands = _send_wait(handle, after, f"fetch_wait_{l}_{part}", gather=True)
        return {n: lax.dynamic_update_slice(land, src[None], (me, 0, 0)) for n, src, land in zip(names, srcs, lands)}

    n_cw, n_ng = N_A * CONV_K * 3 * HEAD, DEPTH * 3 * HEAD
    small_all = _all_gather(_pack([a["gdn_conv_w"], a["norm_g"], a["c"]], 44), "gather_small").reshape(N_DEV, -1)
    conv_w = small_all[:, :n_cw].reshape(N_DEV, N_A, CONV_K, 3 * HEAD).transpose(1, 2, 0, 3).reshape(N_A, CONV_K, 3 * D)
    norm_g = small_all[:, n_cw:n_cw + n_ng].reshape(N_DEV, DEPTH, 3, HEAD).transpose(1, 2, 0, 3).reshape(DEPTH, 3, D)
    c_all = small_all[:, n_cw + n_ng:n_cw + n_ng + D]

    c_act, = _rowwise_fwd(_silu_fn, [(c_all, FULL)], [], [(D, F32, FULL)], "c_act")
    n_ada = N_MOD * D // N_DEV
    parts = [_mm(c_act, a["ada_w"][l], "nn", "mod_proj") for l in range(DEPTH)]
    parts.append(_mm(c_act, a["kv_ada_w"], "nn", "mod_proj_kv"))
    mod_recv = _exchange(jnp.concatenate(parts, axis=1)[:, None, :], "exchange_mod")[:, 0]
    mod = mod_recv[:, :DEPTH * n_ada].reshape(N_DEV, DEPTH, n_ada).transpose(1, 0, 2).reshape(DEPTH, N_MOD * D)
    mod = (mod + a["ada_b"]).reshape(DEPTH, N_MOD, D)
    kvmod = mod_recv[:, DEPTH * n_ada:].reshape(2 * D) + a["kv_ada_b"]

    def row(v):
        return v[None]

    def ffn_params(l, i, w):
        w_in = w["ffn_w_in"]
        k = 0 if i == 0 else 6
        return dict(gain=row(norm_g[l, 0 if i == 0 else 2]), shift=row(mod[l, k]), scale=row(mod[l, k + 1]),
                    gm=0.5 * row(mod[l, k + 2]),
                    wg=w_in[:N_DEV // 2].transpose(1, 0, 2).reshape(D, D_FF),
                    wu=w_in[N_DEV // 2:].transpose(1, 0, 2).reshape(D, D_FF),
                    wo=w["ffn_w_out"].reshape(D_FF, D))

    def gdn_params(l, w):
        w_in = w["gdn_w_in"].transpose(1, 0, 2).reshape(D, 4 * D + 2 * HEADS)
        pad = lambda t: jnp.pad(t, ((0, 0), (0, 128 - HEADS)))
        return dict(gain=row(norm_g[l, 1]), shift=row(mod[l, 3]), scale=row(mod[l, 4]), gm=row(mod[l, 5]),
                    w_main=w_in[:, :4 * D],
                    w_tail=jnp.concatenate([pad(w_in[:, 4 * D:4 * D + HEADS]), pad(w_in[:, 4 * D + HEADS:])], axis=1),
                    conv_w=conv_w[l], a_log=_pad128(row(a["gdn_a_log"][l])), dt_bias=_pad128(row(a["gdn_dt_bias"][l])),
                    norm_g=row(a["gdn_norm_g"][l]), w_out=w["gdn_w_out"].reshape(D, D))

    def mla_params(l, w):
        j = l - N_A
        uq = w["mla_w_uq"].transpose(1, 0, 2)
        qg = _dup_rope(a["mla_q_norm_g"][j])
        return dict(gain=row(norm_g[l, 1]), shift=row(mod[l, 3]), scale=row(mod[l, 4]), gm=row(mod[l, 5]),
                    w_dq=w["mla_w_dq"].reshape(D, Q_LORA), q_lora_g=row(a["mla_q_lora_norm_g"][j]),
                    w_uq=_dup_rope(uq).reshape(Q_LORA, HEADS * HEAD_PAD), q_gn=row(qg[:NOPE]), q_gr=row(qg[NOPE:]),
                    w_out=w["mla_w_out"].reshape(D, D))

    def kv_params(w):
        w_dkv = w["mla_w_dkv"].reshape(D, KV_LORA + ROPE)
        kg = _dup_rope(a["mla_k_norm_g"])
        return dict(gain=row(a["kv_norm_g"]), shift=row(kvmod[:D]), scale=row(kvmod[D:]),
                    w_dkv=jnp.concatenate([w_dkv, w_dkv[:, KV_LORA:]], axis=1), kv_g=row(a["mla_kv_norm_g"]),
                    w_ukv=w["mla_w_ukv"].transpose(1, 0, 2).reshape(KV_LORA, HEADS * 2 * HEAD), k_gn=row(kg[:NOPE]),
                    k_gr=row(kg[NOPE:]))

    tapes, kv, kv_p, h = [[] for _ in range(DEPTH)], None, None, None
    first = {name: _all_gather((w + zero_of(mod)).astype(BF16), "fetch_first_" + name)
             for name, w in stage_shards(0, 0).items()}
    pending = []
    for l, part in stages[1:1 + AHEAD]:
        tie = pending[-1][1]["token"][0, 0] if pending else zero_of(first["ffn_w_out"])
        pending.append(start_stage(l, part, tie))
    for n, (l, part) in enumerate(stages):
        if n == 0:
            w, ties = first, tuple(h["token"] for _, h in pending)
        else:
            names, handle = pending.pop(0)
            w = finish_stage(l, part, names, handle, x)
            ties = ()
            if n + AHEAD < len(stages):
                pending.append(start_stage(*stages[n + AHEAD], zero_of(w[names[0]])))
                ties = (pending[-1][1]["token"],)
        nxt = None
        if n + 1 < len(stages):
            l2, part2 = stages[n + 1]
            k2 = 3 * part2
            nxt = (row(norm_g[l2, part2]), row(mod[l2, k2 + 1]), row(mod[l2, k2]))
        if part != 1:
            p = ffn_params(l, part // 2, w)
            x, t, h = _ffn_fwd(x, p, ties, h, nxt)
        else:
            p = gdn_params(l, w) if l < N_A else mla_params(l, w)
            x, t, h = _gdn_fwd(x, p, ties, h, nxt) if l < N_A else _mla_fwd(x, p, kv, ties, h, nxt)
        tapes[l] += [p, t]
        if part == 2 and l == N_A - 1:
            kv_p = kv_params(w)
            kv = _kv_fwd(x, kv_p, cosf, sins)
    dx, loss_blk = _loss_and_grad(x, a["loss_target"][0], "loss")
    loss = lax.psum(loss_blk[0, 0], ("x", "y", "c"))

    def by_cols(g, n):
        return g.reshape(g.shape[0], -1, n).transpose(1, 0, 2)

    def ffn_blocks(g):
        return {"ffn_w_in": jnp.concatenate([by_cols(g["wg"], n_in), by_cols(g["wu"], n_in)], axis=0),
                "ffn_w_out": g["wo"].reshape(N_DEV, D_FF // N_DEV, D)}

    def mixer_blocks(l, g):
        if l < N_A:
            full = jnp.concatenate([g["w_main"], g["w_tail"][:, :HEADS], g["w_tail"][:, 128:128 + HEADS]], axis=1)
            return {"gdn_w_in": by_cols(full, n_gdn), "gdn_w_out": g["w_out"].reshape(N_DEV, D // N_DEV, D)}
        return {"mla_w_dq": g["w_dq"].reshape(N_DEV, D // N_DEV, Q_LORA),
                "mla_w_uq": _fold_rope(g["w_uq"].reshape(Q_LORA, HEADS, HEAD_PAD)).transpose(1, 0, 2),
                "mla_w_out": g["w_out"].reshape(N_DEV, D // N_DEV, D)}

    sent = []

    def send(key, blocks, tie=0.0):
        handle = _send_start([(b + tie).astype(BF16) for b in blocks.values()], "grad_start_" + "_".join(map(str, key)),
                             gather=False)
        sent.append((key, list(blocks), handle))
        return (handle["token"],)

    grads = [None] * DEPTH
    dk_sum = dv_sum = kv_grads = res = None
    ties = ()
    for l in reversed(range(DEPTH)):
        p1, t1, pm_, tm_, p2, t2 = tapes[l]
        if l == N_A - 1:
            dx, kv_grads, res = _kv_bwd(kv, kv_p, dk_sum, dv_sum, dx, (t2["y"], p2["gm"]))
            d_dkv = kv_grads["w_dkv"]
            ties += send((l, 3), {
                "mla_w_dkv": jnp.concatenate(
                    [d_dkv[:, :KV_LORA], d_dkv[:, KV_LORA:KV_LORA + ROPE] + d_dkv[:, KV_LORA + ROPE:]],
                    axis=1).reshape(N_DEV, D // N_DEV, KV_LORA + ROPE),
                "mla_w_ukv": by_cols(kv_grads["w_ukv"], 2 * HEAD)})
        dx, g2, res = _ffn_bwd(t2, p2, dx, res, (tm_["y"], pm_["gm"]), ties)
        ties = send((l, 2), ffn_blocks(g2))
        if l < N_A:
            dx, gm_, res = _gdn_bwd(tm_, pm_, dx, res, (t1["y"], p1["gm"]), ties)
        else:
            dx, gm_, res, dk, dv = _mla_bwd(tm_, pm_, kv, dx, res, (t1["y"], p1["gm"]), ties)
            dk_sum = dk if dk_sum is None else dk_sum + dk
            dv_sum = dv if dv_sum is None else dv_sum + dv
        ties = send((l, 1), mixer_blocks(l, gm_))
        prev = (tapes[l - 1][5]["y"], tapes[l - 1][4]["gm"]) if l > 0 and l != N_A else None
        dx, g1, res = _ffn_bwd(t1, p1, dx, res, prev, ties)
        if l > 0:
            ties = send((l, 0), ffn_blocks(g1))
        grads[l] = (g1, gm_, g2)

    out = {}
    def dmod(l):
        g1, gm_, g2 = grads[l]
        return jnp.concatenate([g1["shift"], g1["scale"], 0.5 * g1["gm"], gm_["shift"], gm_["scale"], gm_["gm"],
                                g2["shift"], g2["scale"], 0.5 * g2["gm"]], axis=1)

    gdn = [grads[l][1] for l in range(N_A)]
    mla = [grads[l][1] for l in range(N_A, DEPTH)]
    small = {
        "ada_b": jnp.concatenate([dmod(l) for l in range(DEPTH)], axis=0),
        "kv_ada_b": jnp.concatenate([kv_grads["shift"], kv_grads["scale"]], axis=1),
        "norm_g": jnp.stack([jnp.concatenate([grads[l][0]["gain"], grads[l][1]["gain"], grads[l][2]["gain"]], axis=0)
                             for l in range(DEPTH)]),
        "gdn_conv_w": jnp.stack([g["conv_w"] for g in gdn]),
        "gdn_a_log": jnp.stack([g["a_log"][0, :HEADS] for g in gdn]),
        "gdn_dt_bias": jnp.stack([g["dt_bias"][0, :HEADS] for g in gdn]),
        "gdn_norm_g": jnp.stack([g["norm_g"][0] for g in gdn]),
        "kv_norm_g": kv_grads["gain"],
        "mla_kv_norm_g": kv_grads["kv_g"],
        "mla_k_norm_g": _fold_rope(jnp.concatenate([kv_grads["k_gn"], kv_grads["k_gr"]], axis=1)),
        "mla_q_lora_norm_g": jnp.stack([g["q_lora_g"][0] for g in mla]),
        "mla_q_norm_g": jnp.stack([_fold_rope(jnp.concatenate([g["q_gn"], g["q_gr"]], axis=1))[0] for g in mla]),
    }
    rows = 616
    assert sum(n for _, n in SMALL) <= rows * 128 and all(small[n].size == k for n, k in SMALL)
    small_recv = _all_gather(_pack([small[n] for n, _ in SMALL], rows), "gather_small_grads")
    small_recv = small_recv + send((0, 0), ffn_blocks(grads[0][0]), zero_of(small_recv))[0][0, 0]
    zero = lambda n, k: jnp.zeros((k,), F32)
    packed = {pre: _pack([a[pre + n] if n in SMALL_REPLICATED else zero(n, k) for n, k in SMALL], rows)
              for pre in ("", "m_", "v_")}
    res = _adamw([small_recv], packed[""], packed["m_"], packed["v_"], "adamw_small")
    offs = {}
    o = 0
    for n, k in SMALL:
        offs[n] = o
        o += k
    for n, k in SMALL:
        if n in SMALL_REPLICATED:
            out[n] = [r.reshape(-1)[offs[n]:offs[n] + k] for r in res]
    gsum = res[0].reshape(-1)
    g_norm = lax.dynamic_slice_in_dim(gsum[offs["norm_g"]:offs["norm_g"] + DEPTH * 3 * D].reshape(DEPTH * 3, D),
                                      me * HEAD, HEAD, axis=1)
    g_conv = lax.dynamic_slice_in_dim(
        gsum[offs["gdn_conv_w"]:offs["gdn_conv_w"] + N_A * CONV_K * 3 * D].reshape(N_A * CONV_K, 3 * D),
        me * 3 * HEAD, 3 * HEAD, axis=1)
    res2 = _adamw([_pack([g_norm, g_conv], 36)[None]], *[_pack([a[pre + "norm_g"], a[pre + "gdn_conv_w"]], 36)
                                                      for pre in ("", "m_", "v_")], "adamw_small")
    out["norm_g"] = [r.reshape(-1)[:n_ng] for r in res2]
    out["gdn_conv_w"] = [r.reshape(-1)[n_ng:n_ng + n_cw] for r in res2]

    c_act_t = c_act.T
    all_small = small_recv.reshape(N_DEV, -1)
    dmod_all = all_small[:, :DEPTH * N_MOD * D].reshape(N_DEV, DEPTH, N_MOD * D)
    dmod_mine = lax.dynamic_slice_in_dim(dmod_all, me * n_ada, n_ada, axis=2)
    g_ada = [_outer8(c_act_t, dmod_mine[:, l], "ada_grad")[None] for l in range(DEPTH)]
    out["ada_w"] = _adamw(g_ada, *[a[pre + "ada_w"].reshape(DEPTH * D, n_ada) for pre in ("", "m_", "v_")], "adamw")
    dkv_all = all_small[:, offs["kv_ada_b"]:offs["kv_ada_b"] + 2 * D]
    g_kv = _outer8(c_act_t, lax.dynamic_slice_in_dim(dkv_all, me * (2 * D // N_DEV), 2 * D // N_DEV, axis=1), "ada_grad")
    out["kv_ada_w"] = _adamw([g_kv[None]], *[a[pre + "kv_ada_w"] for pre in ("", "m_", "v_")], "adamw")

    pieces = {}
    for key, names, handle in sent:
        srcs, lands = _send_wait(handle, out["kv_ada_w"][0], "grad_wait_" + "_".join(map(str, key)), gather=False)
        for name, src, land in zip(names, srcs, lands):
            own = lax.dynamic_slice_in_dim(src, me, 1, axis=0)
            pieces.setdefault(name, []).append((key, lax.dynamic_update_slice(land, own, (me, 0, 0))))
    for name, parts in pieces.items():
        wide = a[name].shape[-1]
        out[name] = _adamw([p for _, p in sorted(parts, key=lambda kp: kp[0])], a[name].reshape(-1, wide),
                           a["m_" + name].reshape(-1, wide), a["v_" + name].reshape(-1, wide), "adamw")

    result = [loss, dx[None]]
    for k in range(4):
        result += [out[n][k].reshape(a[n].shape) for n in WEIGHTS]
    return tuple(result)


def kernel(x, c, positions, ada_w, ada_b, norm_g, ffn_w_in, ffn_w_out, gdn_w_in, gdn_conv_w, gdn_a_log, gdn_dt_bias, gdn_norm_g, gdn_w_out, kv_ada_w, kv_ada_b, kv_norm_g, mla_w_dkv, mla_kv_norm_g, mla_w_ukv, mla_k_norm_g, mla_w_dq, mla_q_lora_norm_g, mla_w_uq, mla_q_norm_g, mla_w_out, loss_target, m_ada_w, m_ada_b, m_norm_g, m_ffn_w_in, m_ffn_w_out, m_gdn_w_in, m_gdn_conv_w, m_gdn_a_log, m_gdn_dt_bias, m_gdn_norm_g, m_gdn_w_out, m_kv_ada_w, m_kv_ada_b, m_kv_norm_g, m_mla_w_dkv, m_mla_kv_norm_g, m_mla_w_ukv, m_mla_k_norm_g, m_mla_w_dq, m_mla_q_lora_norm_g, m_mla_w_uq, m_mla_q_norm_g, m_mla_w_out, v_ada_w, v_ada_b, v_norm_g, v_ffn_w_in, v_ffn_w_out, v_gdn_w_in, v_gdn_conv_w, v_gdn_a_log, v_gdn_dt_bias, v_gdn_norm_g, v_gdn_w_out, v_kv_ada_w, v_kv_ada_b, v_kv_norm_g, v_mla_w_dkv, v_mla_kv_norm_g, v_mla_w_ukv, v_mla_k_norm_g, v_mla_w_dq, v_mla_q_lora_norm_g, v_mla_w_uq, v_mla_q_norm_g, v_mla_w_out):
    return _step(dict(locals()))
```

```python
import functools
import math

import jax
import jax.numpy as jnp
from jax import lax
from jax.experimental import pallas as pl
from jax.experimental.pallas import tpu as pltpu

F32 = jnp.float32
BF16 = jnp.bfloat16

N_DEV = 8
D = 1024
D_FF = 2816
DEPTH = 4
N_A = 2
N_MOD = 9
HEADS = 8
HEAD = 128
CHUNK = 64
CONV_K = 4
KV_LORA = 256
Q_LORA = 384
NOPE = 128
ROPE = 64
QK_HEAD = NOPE + ROPE
HEAD_PAD = 256
ROPE_BASE = 10000.0
EPS = 1e-6
LR, B1, B2, ADAM_EPS, WD, STEP = 0.001, 0.9, 0.999, 1e-08, 0.01, 10

VMEM_LIMIT = 48 * 1024 * 1024
ROW_TILE = 256
MESH = pl.DeviceIdType.MESH

_NN = (((1,), (0,)), ((), ()))
_NT = (((1,), (1,)), ((), ()))
_TN = (((0,), (0,)), ((), ()))
_DIMS = {"nn": _NN, "nt": _NT, "tn": _TN}


def _params(dims=None):
    return pltpu.CompilerParams(dimension_semantics=dims, vmem_limit_bytes=VMEM_LIMIT)


def _tile(n, target):
    for t in range(target - target % 128, 0, -128):
        if n % t == 0:
            return t
    return n


_TIE_SPEC1 = pl.BlockSpec((8, 128), lambda i: (0, 0))
_TIE_SPEC2 = pl.BlockSpec((8, 128), lambda i, j: (0, 0))
_TIE_SPEC3 = pl.BlockSpec((8, 128), lambda i, j, k: (0, 0))


def _matmul(pairs, form, name, out_dtype=F32, tm=1408, tn=1408, tk=1408, boffs=None, resid=None, ties=()):
    a0, b0 = pairs[0]
    if form == "nn":
        m, n = a0.shape[0], b0.shape[1]
        ks = [a.shape[1] for a, _ in pairs]
    elif form == "nt":
        m, n = a0.shape[0], b0.shape[0]
        ks = [a.shape[1] for a, _ in pairs]
    else:
        m, n = a0.shape[1], b0.shape[1]
        ks = [a.shape[0] for a, _ in pairs]
    tm, tn = _tile(m, tm), _tile(n, tn)
    tks = [_tile(k, tk) for k in ks]
    boffs = boffs or [0] * len(pairs)
    assert m % tm == 0 and n % tn == 0 and all(o % t == 0 for o, t in zip(boffs, tks)), (name, m, n, ks)
    steps = [k // t for k, t in zip(ks, tks)]
    starts = [sum(steps[:p]) for p in range(len(pairs))]
    nk = sum(steps)

    def kidx(p, k):
        return jnp.clip(k - starts[p], 0, steps[p] - 1)

    in_specs, args = [], []
    for p, (a, b) in enumerate(pairs):
        t = tks[p]
        if form == "tn":
            in_specs.append(pl.BlockSpec((t, tm), lambda i, j, k, p=p: (kidx(p, k), i)))
            in_specs.append(pl.BlockSpec((t, tn), lambda i, j, k, p=p: (kidx(p, k), j)))
        elif form == "nn":
            in_specs.append(pl.BlockSpec((tm, t), lambda i, j, k, p=p: (i, kidx(p, k))))
            in_specs.append(pl.BlockSpec((t, tn), lambda i, j, k, p=p: (kidx(p, k), j)))
        else:
            in_specs.append(pl.BlockSpec((tm, t), lambda i, j, k, p=p: (i, kidx(p, k))))
            in_specs.append(pl.BlockSpec((tn, t), lambda i, j, k, p=p, o=boffs[p] // t: (j, kidx(p, k) + o)))
        args += [a, b]
    dims = _DIMS[form]
    npairs = len(pairs)
    nres = len(resid or ())
    nin = 2 * npairs + len(ties) + nres
    out_blk = pl.BlockSpec((tm, tn), lambda i, j, k: (i, j))
    in_specs += [_TIE_SPEC3] * len(ties)
    args += list(ties)
    if resid:
        assert nres == 2 or (nres == 5 and tn == n)
        in_specs += [out_blk] + [pl.BlockSpec((1, tn), lambda i, j, k: (0, j))] * (nres - 1)
        args += list(resid)

    def body(*refs):
        o_ref = refs[nin]
        k = pl.program_id(2)

        def prod(p):
            return lax.dot_general(refs[2 * p][...].astype(BF16), refs[2 * p + 1][...].astype(BF16), dims,
                                   preferred_element_type=F32)

        def finish(y):
            o_ref[...] = y.astype(o_ref.dtype)
            if resid:
                x_ref, gate_ref = refs[nin - nres], refs[nin - nres + 1]
                xn = x_ref[...] + gate_ref[...] * y
                refs[nin + 1][...] = xn
                if nres == 5:
                    gain, scale, shift = (r[...] for r in refs[nin - 3:nin])
                    refs[nin + 2][...] = _modulate_fn(0, xn, gain, scale, shift)[0].astype(BF16)

        if nk == 1:
            finish(prod(0))
            return
        acc = refs[-1]

        @pl.when(k == 0)
        def _():
            acc[...] = jnp.zeros_like(acc)

        for p in range(npairs):
            @pl.when((k >= starts[p]) & (k < starts[p] + steps[p]))
            def _(p=p):
                acc[...] += prod(p)

        @pl.when(k == nk - 1)
        def _():
            finish(acc[...])

    res = pl.pallas_call(
        body, name=name, grid=(m // tm, n // tn, nk), in_specs=in_specs,
        out_specs=[out_blk] * (2 + (nres == 5)) if resid else out_blk,
        out_shape=([jax.ShapeDtypeStruct((m, n), out_dtype), jax.ShapeDtypeStruct((m, n), F32)]
                   + [jax.ShapeDtypeStruct((m, n), BF16)] * (nres == 5))
        if resid else jax.ShapeDtypeStruct((m, n), out_dtype),
        scratch_shapes=[] if nk == 1 else [pltpu.VMEM((tm, tn), F32)],
        compiler_params=_params(("parallel", "parallel", "arbitrary")),
    )(*args)
    return res


def _mm(a, b, form, name, **kw):
    return _matmul([(a, b)], form, name, **kw)


def _cols(spec, g):
    return spec[g] if isinstance(spec, list) else spec


def _rowwise_fwd(fn, rows, pars, outs, name, groups=1, ts=ROW_TILE, ties=()):
    s = rows[0][0].shape[0]
    ts = min(ts, s)
    assert s % ts == 0
    nr, npar = len(rows), len(pars)

    def body(*refs):
        par_t = [r[...] for r in refs[nr:nr + npar]]
        out_refs = refs[nr + npar + len(ties):]
        for g in range(groups):
            row_t = []
            for r, (_, spec) in zip(refs[:nr], rows):
                c0, w = _cols(spec, g)
                row_t.append(r[:, c0:c0 + w].astype(F32))
            res = fn(g, *row_t, *par_t)
            for o_ref, val, (_, _, spec) in zip(out_refs, res, outs):
                c0, w = _cols(spec, g)
                o_ref[:, c0:c0 + w] = val.astype(o_ref.dtype)

    return pl.pallas_call(
        body, name=name, grid=(s // ts,),
        in_specs=[pl.BlockSpec((ts, a.shape[1]), lambda i: (i, 0)) for a, _ in rows]
        + [pl.BlockSpec(p.shape, lambda i: (0, 0)) for p in pars] + [_TIE_SPEC1] * len(ties),
        out_specs=[pl.BlockSpec((ts, w), lambda i: (i, 0)) for w, _, _ in outs],
        out_shape=[jax.ShapeDtypeStruct((s, w), dt) for w, dt, _ in outs],
        compiler_params=_params(("parallel",)),
    )(*[a for a, _ in rows], *pars, *ties)


def _rowwise_bwd(fn, rows, pars, outs, douts, gmap, gshapes, name, groups=1, add=None, par_grads=True,
                 ts=ROW_TILE):
    s = rows[0][0].shape[0]
    ts = min(ts, s)
    assert s % ts == 0
    nr, npar, nout, ng = len(rows), len(pars), len(outs), len(gshapes)
    add = add or {}
    add_keys = sorted(add)

    def body(*refs):
        row_refs = refs[:nr]
        par_refs = refs[nr:nr + npar]
        dout_refs = refs[nr + npar:nr + npar + nout]
        add_refs = refs[nr + npar + nout:nr + npar + nout + len(add_keys)]
        g_refs = refs[nr + npar + nout + len(add_keys):][:ng]
        pg_refs = refs[nr + npar + nout + len(add_keys) + ng:]
        par_t = [r[...] for r in par_refs]
        par_acc = [None] * npar
        shared_acc = {}
        for g in range(groups):
            row_t = []
            for r, (_, spec) in zip(row_refs, rows):
                c0, w = _cols(spec, g)
                row_t.append(r[:, c0:c0 + w].astype(F32))
            cts = []
            for r, (_, _, spec) in zip(dout_refs, outs):
                c0, w = _cols(spec, g)
                cts.append(r[:, c0:c0 + w].astype(F32))
            _, vjp = jax.vjp(lambda *t, g=g: tuple(fn(g, *t)), *row_t, *par_t)
            grads = vjp(tuple(cts))
            for k in range(nr):
                if gmap[k] is None:
                    continue
                gi, spec = gmap[k]
                if isinstance(spec, list) or groups == 1:
                    c0, w = _cols(spec, g)
                    val = grads[k]
                    if gi in add:
                        val = val + add_refs[add_keys.index(gi)][:, c0:c0 + w].astype(F32)
                    g_refs[gi][:, c0:c0 + w] = val.astype(g_refs[gi].dtype)
                else:
                    shared_acc[k] = grads[k] if k not in shared_acc else shared_acc[k] + grads[k]
            if par_grads:
                for k in range(npar):
                    pg = grads[nr + k]
                    par_acc[k] = pg if par_acc[k] is None else par_acc[k] + pg
        for k, val in shared_acc.items():
            gi, (c0, w) = gmap[k]
            assert gi not in add
            g_refs[gi][:, c0:c0 + w] = val.astype(g_refs[gi].dtype)
        if par_grads:
            first = pl.program_id(0) == 0
            for k in range(npar):
                @pl.when(first)
                def _(k=k):
                    pg_refs[k][...] = par_acc[k]

                @pl.when(jnp.logical_not(first))
                def _(k=k):
                    pg_refs[k][...] += par_acc[k]

    out_specs = [pl.BlockSpec((ts, w), lambda i: (i, 0)) for w, _ in gshapes]
    out_shape = [jax.ShapeDtypeStruct((s, w), dt) for w, dt in gshapes]
    if par_grads:
        out_specs += [pl.BlockSpec(p.shape, lambda i: (0, 0)) for p in pars]
        out_shape += [jax.ShapeDtypeStruct(p.shape, F32) for p in pars]
    return pl.pallas_call(
        body, name=name, grid=(s // ts,),
        in_specs=[pl.BlockSpec((ts, a.shape[1]), lambda i: (i, 0)) for a, _ in rows]
        + [pl.BlockSpec(p.shape, lambda i: (0, 0)) for p in pars]
        + [pl.BlockSpec((ts, a.shape[1]), lambda i: (i, 0)) for a in douts]
        + [pl.BlockSpec((ts, add[k].shape[1]), lambda i: (i, 0)) for k in add_keys],
        out_specs=out_specs, out_shape=out_shape,
        compiler_params=_params(("arbitrary",)),
    )(*[a for a, _ in rows], *pars, *douts, *[add[k] for k in add_keys])


def _sigmoid(x):
    return 1.0 / (1.0 + jnp.exp(-x))


def _silu(x):
    return x * _sigmoid(x)


def _softplus(x):
    return jnp.maximum(x, 0.0) + jnp.log(1.0 + jnp.exp(-jnp.abs(x)))


def _rms(t, g, n=None):
    n = n or t.shape[-1]
    return t * lax.rsqrt(jnp.sum(t * t, axis=-1, keepdims=True) / n + EPS) * g


def _modulate_fn(g, x, gain, scale, shift):
    return (_rms(x, gain) * (1.0 + scale) + shift,)


def _resgate_fn(g, x, y, gm):
    return (x + gm * y,)


def _gate_only_fn(g, y, gm):
    return (gm * y,)


def _gdn_gates_fn(g, b_logit, a_logit, a_log, dt_bias):
    gate = -jnp.exp(a_log) * _softplus(a_logit + dt_bias)
    n = gate.shape[0]
    i = lax.broadcasted_iota(jnp.int32, (n, n), 0)
    j = lax.broadcasted_iota(jnp.int32, (n, n), 1)
    tri = (((i // CHUNK) == (j // CHUNK)) & (i >= j)).astype(F32)
    gcum = lax.dot_general(tri, gate, _NN, preferred_element_type=F32, precision=lax.Precision.HIGHEST)
    return _sigmoid(b_logit), gcum


def _gdn_outnorm_fn(g, o, z, gain):
    return (_rms(o, gain) * _silu(z),)


def _rms_fn(g, t, gain):
    return (_rms(t, gain),)


@jax.custom_vjp
def _swap_halves(t):
    return pltpu.roll(t, 32, 1)


_swap_halves.defvjp(lambda t: (pltpu.roll(t, 32, 1), None), lambda _, ct: (pltpu.roll(ct, 96, 1),))


def _head_norm_rope_fn(g, nope, rope, cosf, sins, gain_n, gain_r):
    first = lax.broadcasted_iota(jnp.int32, rope.shape, 1) < ROPE
    ss = jnp.sum(nope * nope, axis=-1, keepdims=True) + jnp.sum(jnp.where(first, rope * rope, 0.0), axis=-1,
                                                                 keepdims=True)
    r = lax.rsqrt(ss / QK_HEAD + EPS)
    tn = nope * r * gain_n
    tr = rope * r * gain_r
    rot = jnp.where(first, tr * cosf + _swap_halves(tr) * sins, 0.0)
    return tn, rot


def _q_norm_rope_fn(g, nope, rope, cosf, sins, gain_n, gain_r):
    tn, rot = _head_norm_rope_fn(g, nope, rope, cosf, sins, gain_n, gain_r)
    return (jnp.concatenate([tn, rot], axis=1),)


def _k_norm_rope_fn(g, nope, val, rope, cosf, sins, gain_n, gain_r):
    tn, rot = _head_norm_rope_fn(g, nope, rope, cosf, sins, gain_n, gain_r)
    return jnp.concatenate([tn, rot], axis=1), val


def _loss_fn(g, y, target):
    e = y - target
    return (jnp.sum(e * e, axis=-1, keepdims=True) * (0.5 / D) * jnp.ones((1, 128), F32),)


def _ffn_in(h, wg, wu, name, tm=512, tn=1408, ties=()):
    s = h.shape[0]
    tm = min(tm, s)

    def body(h_ref, wg_ref, wu_ref, *rest):
        g_ref, u_ref, a_ref = rest[-3:]
        hb = h_ref[...]
        gate = jnp.dot(hb, wg_ref[...], preferred_element_type=F32)
        up = jnp.dot(hb, wu_ref[...], preferred_element_type=F32)
        g_ref[...] = gate.astype(BF16)
        u_ref[...] = up.astype(BF16)
        a_ref[...] = (_silu(gate) * up).astype(BF16)

    spec = pl.BlockSpec((tm, tn), lambda j, i: (i, j))
    return pl.pallas_call(
        body, name=name, grid=(D_FF // tn, s // tm),
        in_specs=[pl.BlockSpec((tm, D), lambda j, i: (i, 0)), pl.BlockSpec((D, tn), lambda j, i: (0, j)),
                  pl.BlockSpec((D, tn), lambda j, i: (0, j))] + [_TIE_SPEC2] * len(ties),
        out_specs=[spec, spec, spec], out_shape=[jax.ShapeDtypeStruct((s, D_FF), BF16)] * 3,
        compiler_params=_params(("parallel", "parallel")),
    )(h, wg, wu, *ties)


def _ffn_bwd_act(dy, wo, gate, up, name, tm=512, tn=1408, ties=()):
    s = dy.shape[0]
    tm = min(tm, s)

    def body(dy_ref, wo_ref, g_ref, u_ref, *rest):
        dg_ref, du_ref = rest[-2:]
        dact = lax.dot_general(dy_ref[...], wo_ref[...], _NT, preferred_element_type=F32)
        gate = g_ref[...].astype(F32)
        up = u_ref[...].astype(F32)
        sg = _sigmoid(gate)
        dg_ref[...] = (dact * up * (sg * (1.0 + gate * (1.0 - sg)))).astype(BF16)
        du_ref[...] = (dact * (gate * sg)).astype(BF16)

    spec = pl.BlockSpec((tm, tn), lambda j, i: (i, j))
    return pl.pallas_call(
        body, name=name, grid=(D_FF // tn, s // tm),
        in_specs=[pl.BlockSpec((tm, D), lambda j, i: (i, 0)), pl.BlockSpec((tn, D), lambda j, i: (j, 0)), spec, spec]
        + [_TIE_SPEC2] * len(ties),
        out_specs=[spec, spec], out_shape=[jax.ShapeDtypeStruct((s, D_FF), BF16)] * 2,
        compiler_params=_params(("parallel", "parallel")),
    )(dy, wo, gate, up, *ties)


def _shift_down(x, d):
    rows = lax.broadcasted_iota(jnp.int32, x.shape, 0)
    return jnp.where(rows >= d, pltpu.roll(x, d, 0), 0.0)


def _shift_up(x, d):
    n = x.shape[0]
    rows = lax.broadcasted_iota(jnp.int32, x.shape, 0)
    return jnp.where(rows < n - d, pltpu.roll(x, n - d, 0), 0.0)


def _conv_post(pre, is_qk):
    a = _silu(pre)
    l2 = a * lax.rsqrt(jnp.sum(a * a, axis=-1, keepdims=True) + EPS)
    return jnp.where(is_qk, l2, a)


def _conv_pre(x, w):
    pre = x * w[CONV_K - 1:CONV_K, :]
    for j in range(CONV_K - 1):
        pre = pre + _shift_down(x, CONV_K - 1 - j) * w[j:j + 1, :]
    return pre


def _gdn_conv_fwd(pm, conv_w, name):
    s = pm.shape[0]
    nblk = 3 * D // HEAD

    def body(x_ref, w_ref, o_ref):
        is_qk = pl.program_id(0) < 2 * HEADS
        o_ref[...] = _conv_post(_conv_pre(x_ref[...], w_ref[...]), is_qk)

    return pl.pallas_call(
        body, name=name, grid=(nblk,),
        in_specs=[pl.BlockSpec((s, HEAD), lambda c: (0, c)), pl.BlockSpec((CONV_K, HEAD), lambda c: (0, c))],
        out_specs=pl.BlockSpec((s, HEAD), lambda c: (0, c)),
        out_shape=jax.ShapeDtypeStruct((s, 3 * D), F32), compiler_params=_params(("parallel",)),
    )(pm, conv_w)


def _gdn_conv_bwd(pm, conv_w, dout, part, name):
    s = pm.shape[0]
    off = part * HEADS

    def body(x_ref, w_ref, d_ref, dx_ref, dw_ref):
        x, w = x_ref[...], w_ref[...]
        _, vjp = jax.vjp(lambda p: _conv_post(p, part < 2), _conv_pre(x, w))
        dpre, = vjp(d_ref[...])
        dx = dpre * w[CONV_K - 1:CONV_K, :]
        rows = [None] * CONV_K
        rows[CONV_K - 1] = jnp.sum(dpre * x, axis=0, keepdims=True)
        for j in range(CONV_K - 1):
            dx = dx + _shift_up(dpre, CONV_K - 1 - j) * w[j:j + 1, :]
            rows[j] = jnp.sum(dpre * _shift_down(x, CONV_K - 1 - j), axis=0, keepdims=True)
        dx_ref[...] = dx.astype(BF16)
        dw_ref[...] = jnp.concatenate(rows, axis=0)

    return pl.pallas_call(
        body, name=name, grid=(HEADS,),
        in_specs=[pl.BlockSpec((s, HEAD), lambda c: (0, c + off)), pl.BlockSpec((CONV_K, HEAD), lambda c: (0, c + off)),
                  pl.BlockSpec((s, HEAD), lambda c: (0, c))],
        out_specs=[pl.BlockSpec((s, HEAD), lambda c: (0, c)), pl.BlockSpec((CONV_K, HEAD), lambda c: (0, c))],
        out_shape=[jax.ShapeDtypeStruct((s, D), BF16), jax.ShapeDtypeStruct((CONV_K, D), F32)],
        compiler_params=_params(("parallel",)),
    )(pm, conv_w, dout)


def _dot3(a, b, dims=_NN):
    ah, bh = a.astype(BF16), b.astype(BF16)
    al, bl = (a - ah.astype(F32)).astype(BF16), (b - bh.astype(F32)).astype(BF16)
    d = lambda u, v: lax.dot_general(u, v, dims, preferred_element_type=F32)
    return d(ah, bh) + (d(ah, bl) + d(al, bh))


def _make_dot(hi):
    def raw(a, b, dims):
        if hi:
            return _dot3(a, b, dims)
        return lax.dot_general(a.astype(BF16), b.astype(BF16), dims, preferred_element_type=F32)

    @functools.partial(jax.custom_vjp, nondiff_argnums=(2,))
    def dot(a, b, form):
        return raw(a, b, _DIMS[form])

    def fwd(a, b, form):
        return raw(a, b, _DIMS[form]), (a, b)

    def bwd(form, res, ct):
        a, b = res
        if form == "nn":
            return raw(ct, b, _NT), raw(a, ct, _TN)
        if form == "nt":
            return raw(ct, b, _NN), raw(ct, a, _TN)
        return raw(b, ct, _NT), raw(a, ct, _NN)

    dot.defvjp(fwd, bwd)
    return dot


_dot = _make_dot(False)
_dot_hi = _make_dot(True)


def _tri_inv_raw(low):
    n = low.shape[0]
    i = lax.broadcasted_iota(jnp.int32, (n, n), 0)
    j = lax.broadcasted_iota(jnp.int32, (n, n), 1)
    eye = (i == j).astype(F32)
    hdot = _dot3
    same16 = (i // 16) == (j // 16)
    neg = jnp.where(same16, -low, 0.0)
    inv = eye + neg
    power = neg
    for _ in range(3):
        power = hdot(power, power)
        inv = hdot(inv, eye + power)
    for blk in (32, 64):
        off = jnp.where(((i // blk) == (j // blk)) & ((i // (blk // 2)) != (j // (blk // 2))), low, 0.0)
        inv = inv - hdot(inv, hdot(off, inv))
    return inv


@jax.custom_vjp
def _tri_inv(low):
    return _tri_inv_raw(low)


def _tri_inv_fwd(low):
    inv = _tri_inv_raw(low)
    return inv, inv


def _tri_inv_bwd(inv, ct):
    return (-_dot3(_dot3(inv, ct, _TN), inv, _NT),)


_tri_inv.defvjp(_tri_inv_fwd, _tri_inv_bwd)


@jax.custom_vjp
def _tri_inv_given(low, inv):
    return inv


_tri_inv_given.defvjp(lambda low, inv: (inv, inv),
                      lambda inv, ct: (_tri_inv_bwd(inv, ct)[0], jnp.zeros_like(inv)))

GROUP = 4
N_GROUPS = HEADS // GROUP
GROWS = GROUP * CHUNK


def _gdn_group(q, k, v, beta, gc, gr, states, inv=None):
    n = q.shape[0]
    i = lax.broadcasted_iota(jnp.int32, (n, n), 0)
    j = lax.broadcasted_iota(jnp.int32, (n, n), 1)
    same = (i // CHUNK) == (j // CHUNK)
    incl, strict = same & (i >= j), same & (i > j)
    qs = q * (HEAD ** -0.5)
    decay = jnp.where(incl, jnp.exp(jnp.where(incl, gc - gr, 0.0)), 0.0)
    kb = k * beta
    eg = jnp.exp(gc)
    prod = _dot(jnp.concatenate([kb, qs], axis=0), k, "nt")
    low = jnp.where(strict, prod[:n] * decay, 0.0)
    attn = jnp.where(incl, prod[n:] * decay, 0.0)
    inv = _tri_inv(low) if inv is None else _tri_inv_given(low, inv)
    sol = _dot_hi(inv, jnp.concatenate([v * beta, kb * eg], axis=1), "nn")
    u, w, qg = sol[:, :HEAD], sol[:, HEAD:], qs * eg
    last = lax.broadcasted_iota(jnp.int32, (CHUNK, 1), 0) == CHUNK - 1
    v_new, o_state, carry = [], [], []
    for h, state in enumerate(states):
        rows = slice(h * CHUNK, (h + 1) * CHUNK)
        ws = _dot(jnp.concatenate([w[rows], qg[rows]], axis=0), state, "nn")
        v_new.append(u[rows] - ws[:CHUNK])
        o_state.append(ws[CHUNK:])
        g_last = jnp.sum(jnp.where(last, gc[rows], 0.0), axis=0, keepdims=True)
        carry.append((g_last, k[rows] * jnp.exp(g_last - gc[rows])))
    o = jnp.concatenate(o_state, axis=0) + _dot(attn, jnp.concatenate(v_new, axis=0), "nn")
    new = tuple(state * jnp.exp(g_last) + _dot(k_dec, vn, "tn")
                for state, (g_last, k_dec), vn in zip(states, carry, v_new))
    return o, new, inv


def _gdn_specs(s, rev):
    nc = s // CHUNK
    at = (lambda n: nc - 1 - n) if rev else (lambda n: n)
    return nc, at, [
        pl.BlockSpec((CHUNK, D), lambda n: (at(n), 0)), pl.BlockSpec((CHUNK, D), lambda n: (at(n), 1)),
        pl.BlockSpec((CHUNK, D), lambda n: (at(n), 2)), pl.BlockSpec((CHUNK, HEAD), lambda n: (at(n), 0)),
        pl.BlockSpec((CHUNK, HEAD), lambda n: (at(n), 0)),
        pl.BlockSpec((None, N_GROUPS, 1, GROWS), lambda n: (at(n), 0, 0, 0))]


def _group_operands(grp, q_ref, k_ref, v_ref, b_blk, gc_blk, gr_blk):
    heads = range(grp * GROUP, (grp + 1) * GROUP)
    stack = lambda ref: jnp.concatenate([ref[:, h * HEAD:(h + 1) * HEAD] for h in heads], axis=0)
    col = lambda blk: jnp.concatenate([blk[:, h:h + 1] for h in heads], axis=0)
    return stack(q_ref), stack(k_ref), stack(v_ref), col(b_blk), col(gc_blk), gr_blk[grp]


def _gdn_scan_fwd(qkv, beta, gcum, grow, name):
    s = qkv.shape[0]
    nc, _, in_specs = _gdn_specs(s, rev=False)

    def body(q_ref, k_ref, v_ref, b_ref, gc_ref, gr_ref, o_ref, st_ref, inv_ref, state):
        @pl.when(pl.program_id(0) == 0)
        def _():
            state[...] = jnp.zeros_like(state)

        b_blk, gc_blk, gr_blk = b_ref[...], gc_ref[...], gr_ref[...]
        old = [state[h] for h in range(HEADS)]
        res = [_gdn_group(*_group_operands(grp, q_ref, k_ref, v_ref, b_blk, gc_blk, gr_blk),
                          old[grp * GROUP:(grp + 1) * GROUP]) for grp in range(N_GROUPS)]
        for grp, (o, new, inv) in enumerate(res):
            inv_ref[grp] = inv
            for hh in range(GROUP):
                h = grp * GROUP + hh
                st_ref[h] = old[h]
                o_ref[:, h * HEAD:(h + 1) * HEAD] = o[hh * CHUNK:(hh + 1) * CHUNK]
                state[h] = new[hh]

    return pl.pallas_call(
        body, name=name, grid=(nc,), in_specs=in_specs,
        out_specs=[pl.BlockSpec((CHUNK, D), lambda n: (n, 0)),
                   pl.BlockSpec((None, HEADS, HEAD, HEAD), lambda n: (n, 0, 0, 0)),
                   pl.BlockSpec((None, N_GROUPS, GROWS, GROWS), lambda n: (n, 0, 0, 0))],
        out_shape=[jax.ShapeDtypeStruct((s, D), F32), jax.ShapeDtypeStruct((nc, HEADS, HEAD, HEAD), F32),
                   jax.ShapeDtypeStruct((nc, N_GROUPS, GROWS, GROWS), F32)],
        scratch_shapes=[pltpu.VMEM((HEADS, HEAD, HEAD), F32)],
        compiler_params=_params(("arbitrary",)),
    )(qkv, qkv, qkv, beta, gcum, grow)


def _gdn_scan_bwd(qkv, beta, gcum, grow, states, invs, do, name):
    s = qkv.shape[0]
    nc, at, in_specs = _gdn_specs(s, rev=True)
    in_specs += [pl.BlockSpec((None, HEADS, HEAD, HEAD), lambda n: (at(n), 0, 0, 0)),
                 pl.BlockSpec((None, N_GROUPS, GROWS, GROWS), lambda n: (at(n), 0, 0, 0)),
                 pl.BlockSpec((CHUNK, D), lambda n: (at(n), 0))]

    def body(q_ref, k_ref, v_ref, b_ref, gc_ref, gr_ref, st_ref, inv_ref, do_ref, dq_ref, dk_ref, dv_ref, db_ref,
             dgc_ref, dgr_ref, dstate):
        @pl.when(pl.program_id(0) == 0)
        def _():
            dstate[...] = jnp.zeros_like(dstate)

        b_blk, gc_blk, gr_blk = b_ref[...], gc_ref[...], gr_ref[...]
        dold = [dstate[h] for h in range(HEADS)]
        res = []
        for grp in range(N_GROUPS):
            heads = range(grp * GROUP, (grp + 1) * GROUP)
            inv = inv_ref[grp]
            _, vjp = jax.vjp(lambda q, k, v, b, gc, gr, *st, inv=inv: _gdn_group(q, k, v, b, gc, gr, st, inv)[:2],
                             *_group_operands(grp, q_ref, k_ref, v_ref, b_blk, gc_blk, gr_blk),
                             *[st_ref[h] for h in heads])
            d_out = jnp.concatenate([do_ref[:, h * HEAD:(h + 1) * HEAD] for h in heads], axis=0)
            res.append(vjp((d_out, tuple(dold[h] for h in heads))))
        lane = lax.broadcasted_iota(jnp.int32, (CHUNK, HEAD), 1)
        db_all = jnp.zeros((CHUNK, HEAD), F32)
        dgc_all = jnp.zeros((CHUNK, HEAD), F32)
        for grp, (dq, dk, dv, db, dgc, dgr, *dst) in enumerate(res):
            dgr_ref[grp] = dgr
            for hh in range(GROUP):
                h = grp * GROUP + hh
                cs, rows = slice(h * HEAD, (h + 1) * HEAD), slice(hh * CHUNK, (hh + 1) * CHUNK)
                dq_ref[:, cs] = dq[rows]
                dk_ref[:, cs] = dk[rows]
                dv_ref[:, cs] = dv[rows]
                dstate[h] = dst[hh]
                db_all = jnp.where(lane == h, db[rows], db_all)
                dgc_all = jnp.where(lane == h, dgc[rows], dgc_all)
        db_ref[...] = db_all
        dgc_ref[...] = dgc_all

    blk = pl.BlockSpec((CHUNK, D), lambda n: (at(n), 0))
    gblk = pl.BlockSpec((CHUNK, HEAD), lambda n: (at(n), 0))
    return pl.pallas_call(
        body, name=name, grid=(nc,), in_specs=in_specs,
        out_specs=[blk, blk, blk, gblk, gblk, pl.BlockSpec((None, N_GROUPS, 1, GROWS), lambda n: (at(n), 0, 0, 0))],
        out_shape=[jax.ShapeDtypeStruct((s, D), F32)] * 3 + [jax.ShapeDtypeStruct((s, HEAD), F32)] * 2
        + [jax.ShapeDtypeStruct((nc, N_GROUPS, 1, GROWS), F32)],
        scratch_shapes=[pltpu.VMEM((HEADS, HEAD, HEAD), F32)],
        compiler_params=_params(("arbitrary",)),
    )(qkv, qkv, qkv, beta, gcum, grow, states, invs, do)


ATT_TILE = 512
ATT_SCALE = QK_HEAD ** -0.5


def _att_mask(t):
    qpos = lax.broadcasted_iota(jnp.int32, (t, t), 0)
    kpos = lax.broadcasted_iota(jnp.int32, (t, t), 1)
    return (kpos // CHUNK) <= (qpos // CHUNK)


ATT_STRIP = 32


def _att_strip_mask(r, t):
    kpos = lax.broadcasted_iota(jnp.int32, (ATT_STRIP, t), 1)
    return (kpos // CHUNK) <= (r * ATT_STRIP) // CHUNK


def _att_pairs(nb, by_query):
    if by_query:
        pairs = [(i, j) for i in range(nb) for j in range(i + 1)]
    else:
        pairs = [(j, i) for j in range(nb) for i in range(j, nb)]
    return jnp.array([a for a, _ in pairs], jnp.int32), jnp.array([b for _, b in pairs], jnp.int32)


def _attn_fwd(q, k, v, name):
    s = q.shape[0]
    t = min(ATT_TILE, s)
    nb = s // t
    ii, jj = _att_pairs(nb, by_query=True)

    def body(ii_ref, jj_ref, q_ref, k_ref, v_ref, o_ref, lse_ref, m_s, l_s, acc):
        step = pl.program_id(1)
        i, j = ii_ref[step], jj_ref[step]

        @pl.when(j == 0)
        def _():
            m_s[...] = jnp.full_like(m_s, -jnp.inf)
            l_s[...] = jnp.zeros_like(l_s)
            acc[...] = jnp.zeros_like(acc)

        sc = lax.dot_general(q_ref[...], k_ref[...], _NT, preferred_element_type=F32) * ATT_SCALE
        sc = lax.cond(i == j, lambda u: jnp.where(_att_mask(t), u, -jnp.inf), lambda u: u, sc)
        m_new = jnp.maximum(m_s[...], jnp.max(sc, axis=-1, keepdims=True))
        alpha = jnp.exp(m_s[...] - m_new)
        p = jnp.exp(sc - m_new)
        l_s[...] = alpha * l_s[...] + jnp.sum(p, axis=-1, keepdims=True)
        acc[...] = alpha * acc[...] + jnp.dot(p.astype(BF16), v_ref[...], preferred_element_type=F32)
        m_s[...] = m_new

        @pl.when(j == i)
        def _():
            o_ref[...] = acc[...] / l_s[...]
            lse_ref[...] = m_s[...] + jnp.log(l_s[...])

    grid_spec = pltpu.PrefetchScalarGridSpec(
        num_scalar_prefetch=2, grid=(HEADS, len(ii)),
        in_specs=[pl.BlockSpec((t, HEAD_PAD), lambda h, n, ir, jr: (ir[n], h)),
                  pl.BlockSpec((t, HEAD_PAD), lambda h, n, ir, jr: (jr[n], h)),
                  pl.BlockSpec((t, HEAD), lambda h, n, ir, jr: (jr[n], h))],
        out_specs=[pl.BlockSpec((t, HEAD), lambda h, n, ir, jr: (ir[n], h)),
                   pl.BlockSpec((None, t, 1), lambda h, n, ir, jr: (h, ir[n], 0))],
        scratch_shapes=[pltpu.VMEM((t, 1), F32), pltpu.VMEM((t, 1), F32), pltpu.VMEM((t, HEAD), F32)])
    return pl.pallas_call(
        body, name=name, grid_spec=grid_spec,
        out_shape=[jax.ShapeDtypeStruct((s, HEADS * HEAD), F32), jax.ShapeDtypeStruct((HEADS, s, 1), F32)],
        compiler_params=_params(("parallel", "arbitrary")),
    )(ii, jj, q, k, v)


def _attn_bwd(q, k, v, do, o, lse, name, dkv_sum=None):
    s = q.shape[0]
    t = min(ATT_TILE, s)
    nb = s // t
    jj, ii = _att_pairs(nb, by_query=False)
    nsum = 2 if dkv_sum else 0

    def body(jj_ref, ii_ref, q_ref, k_ref, v_ref, do_ref, o_ref, lse_ref, *rest):
        dq_ref, dk_ref, dv_ref, dk_acc, dv_acc, sc_s, dp_s, p_s, ds_s, dl_s = rest[nsum:]
        step = pl.program_id(1)
        i, j = ii_ref[step], jj_ref[step]

        @pl.when(step == 0)
        def _():
            dq_ref[...] = jnp.zeros_like(dq_ref)

        @pl.when(i == j)
        def _():
            dk_acc[...] = jnp.zeros_like(dk_acc)
            dv_acc[...] = jnp.zeros_like(dv_acc)

        do_f = do_ref[...]
        dob = do_f.astype(BF16)
        dl_s[...] = jnp.sum(do_f * o_ref[...], axis=-1, keepdims=True)
        sc_s[...] = lax.dot_general(q_ref[...], k_ref[...], _NT, preferred_element_type=F32)
        dp_s[...] = lax.dot_general(dob, v_ref[...], _NT, preferred_element_type=F32)

        def softmax_strips(diagonal):
            for r in range(t // ATT_STRIP):
                rows = slice(r * ATT_STRIP, (r + 1) * ATT_STRIP)
                p = jnp.exp(sc_s[rows, :] * ATT_SCALE - lse_ref[rows, :])
                if diagonal:
                    p = jnp.where(_att_strip_mask(r, t), p, 0.0)
                p_s[rows, :] = p.astype(BF16)
                ds_s[rows, :] = (p * (dp_s[rows, :] - dl_s[rows, :]) * ATT_SCALE).astype(BF16)

        pl.when(i == j)(functools.partial(softmax_strips, True))
        pl.when(i != j)(functools.partial(softmax_strips, False))
        ds = ds_s[...]
        dv_acc[...] += lax.dot_general(p_s[...], dob, _TN, preferred_element_type=F32)
        dk_acc[...] += lax.dot_general(ds, q_ref[...], _TN, preferred_element_type=F32)
        rows = pl.ds(pl.multiple_of(i * t, t), t)
        dq_ref[rows, :] += jnp.dot(ds, k_ref[...], preferred_element_type=F32)

        @pl.when(i == nb - 1)
        def _():
            dk_ref[...] = dk_acc[...] + rest[0][...] if nsum else dk_acc[...]
            dv_ref[...] = dv_acc[...] + rest[1][...] if nsum else dv_acc[...]

    dk_blk = pl.BlockSpec((t, HEAD_PAD), lambda h, n, jr, ir: (jr[n], h))
    dv_blk = pl.BlockSpec((t, HEAD), lambda h, n, jr, ir: (jr[n], h))
    grid_spec = pltpu.PrefetchScalarGridSpec(
        num_scalar_prefetch=2, grid=(HEADS, len(jj)),
        in_specs=[pl.BlockSpec((t, HEAD_PAD), lambda h, n, jr, ir: (ir[n], h)),
                  pl.BlockSpec((t, HEAD_PAD), lambda h, n, jr, ir: (jr[n], h)),
                  pl.BlockSpec((t, HEAD), lambda h, n, jr, ir: (jr[n], h)),
                  pl.BlockSpec((t, HEAD), lambda h, n, jr, ir: (ir[n], h)),
                  pl.BlockSpec((t, HEAD), lambda h, n, jr, ir: (ir[n], h)),
                  pl.BlockSpec((None, t, 1), lambda h, n, jr, ir: (h, ir[n], 0))] + [dk_blk, dv_blk][:nsum],
        out_specs=[pl.BlockSpec((s, HEAD_PAD), lambda h, n, jr, ir: (0, h)), dk_blk, dv_blk],
        scratch_shapes=[pltpu.VMEM((t, HEAD_PAD), F32), pltpu.VMEM((t, HEAD), F32), pltpu.VMEM((t, t), F32),
                        pltpu.VMEM((t, t), F32), pltpu.VMEM((t, t), BF16), pltpu.VMEM((t, t), BF16),
                        pltpu.VMEM((t, 1), F32)])
    return pl.pallas_call(
        body, name=name, grid_spec=grid_spec,
        out_shape=[jax.ShapeDtypeStruct((s, HEADS * HEAD_PAD), F32)] * 2 + [jax.ShapeDtypeStruct((s, HEADS * HEAD), F32)],
        compiler_params=_params(("parallel", "arbitrary")),
    )(jj, ii, q, k, v, do, o, lse, *(dkv_sum or ()))


def _rope_tables(positions):
    half = ROPE // 2
    inv_freq = ROPE_BASE ** (-jnp.arange(half, dtype=F32) / half)
    ang = positions.astype(F32)[:, None] * inv_freq
    cos, sin = jnp.cos(ang), jnp.sin(ang)
    return jnp.concatenate([cos] * 4, axis=1), jnp.concatenate([-sin, sin] * 2, axis=1)


def _loss_and_grad(y, target, name):
    s = y.shape[0]
    ts = min(ROW_TILE, s)

    def body(y_ref, t_ref, dy_ref, l_ref):
        e = y_ref[...] - t_ref[...]
        dy_ref[...] = e * (1.0 / D)
        part = jnp.sum(jnp.sum(e * e, axis=-1, keepdims=True) * (0.5 / D), axis=0, keepdims=True)
        part = part * jnp.ones((1, 128), F32)

        @pl.when(pl.program_id(0) == 0)
        def _():
            l_ref[...] = part

        @pl.when(pl.program_id(0) > 0)
        def _():
            l_ref[...] += part

    return pl.pallas_call(
        body, name=name, grid=(s // ts,),
        in_specs=[pl.BlockSpec((ts, D), lambda i: (i, 0))] * 2,
        out_specs=[pl.BlockSpec((ts, D), lambda i: (i, 0)), pl.BlockSpec((1, 128), lambda i: (0, 0))],
        out_shape=[jax.ShapeDtypeStruct((s, D), F32), jax.ShapeDtypeStruct((1, 128), F32)],
        compiler_params=_params(("arbitrary",)),
    )(y, target)


ANY = pl.BlockSpec(memory_space=pl.ANY)


def _all_gather(shard, name):
    def body(x_ref, out_ref, send_sems, recv_sems, local_sem):
        x, y, c = lax.axis_index("x"), lax.axis_index("y"), lax.axis_index("c")
        me, sibling = (x, y, c), (x, y, 1 - c)
        chips = [(1 - x, y), (x, 1 - y), (1 - x, 1 - y)]

        def rows(px, py, pc):
            return out_ref.at[4 * px + 2 * py + pc]

        def copy(k, block, to, src=None):
            return pltpu.make_async_remote_copy(
                src_ref=rows(*block) if src is None else src, dst_ref=rows(*block),
                send_sem=send_sems.at[k], recv_sem=recv_sems.at[k], device_id=to, device_id_type=MESH)

        mine = pltpu.make_async_copy(x_ref, rows(*me), local_sem)
        mine.start()
        first = [copy(0, me, sibling, src=x_ref)]
        first += [copy(1 + j, me, (*chip, c), src=x_ref) for j, chip in enumerate(chips)]
        for cp in first:
            cp.start()
        passed = [copy(4 + j, (*chip, c), sibling) for j, chip in enumerate(chips)]
        for j, chip in enumerate(chips):
            copy(1 + j, (*chip, c), me).wait_recv()
            passed[j].start()
        copy(0, sibling, me).wait_recv()
        for j, chip in enumerate(chips):
            copy(4 + j, (*chip, 1 - c), me).wait_recv()
        for cp in first + passed:
            cp.wait_send()
        mine.wait()

    return pl.pallas_call(
        body, name=name, out_shape=jax.ShapeDtypeStruct((N_DEV,) + shard.shape, shard.dtype),
        in_specs=[ANY], out_specs=ANY,
        scratch_shapes=[pltpu.SemaphoreType.DMA((7,)), pltpu.SemaphoreType.DMA((7,)), pltpu.SemaphoreType.DMA],
    )(shard)


def _exchange(blocks, name):
    def body(x_ref, out_ref, send_sems, recv_sems, local_sem):
        x, y, c = lax.axis_index("x"), lax.axis_index("y"), lax.axis_index("c")
        me = 4 * x + 2 * y + c
        mine = pltpu.make_async_copy(x_ref.at[me], out_ref.at[me], local_sem)
        mine.start()
        copies = []
        for k in range(1, N_DEV):
            px = 1 - x if k & 4 else x
            py = 1 - y if k & 2 else y
            pc = 1 - c if k & 1 else c
            peer = 4 * px + 2 * py + pc
            cp = pltpu.make_async_remote_copy(
                src_ref=x_ref.at[peer], dst_ref=out_ref.at[me], send_sem=send_sems.at[k - 1],
                recv_sem=recv_sems.at[k - 1], device_id=(px, py, pc), device_id_type=MESH)
            cp.start()
            copies.append((cp, pltpu.make_async_remote_copy(
                src_ref=x_ref.at[peer], dst_ref=out_ref.at[peer], send_sem=send_sems.at[k - 1],
                recv_sem=recv_sems.at[k - 1], device_id=(px, py, pc), device_id_type=MESH)))
        for cp, landing in copies:
            landing.wait_recv()
        for cp, landing in copies:
            cp.wait_send()
        mine.wait()

    return pl.pallas_call(
        body, name=name, out_shape=jax.ShapeDtypeStruct(blocks.shape, blocks.dtype),
        in_specs=[ANY], out_specs=ANY,
        scratch_shapes=[pltpu.SemaphoreType.DMA((7,)), pltpu.SemaphoreType.DMA((7,)), pltpu.SemaphoreType.DMA],
    )(blocks)


HBM = pl.BlockSpec(memory_space=pltpu.HBM)
SEM = pl.BlockSpec(memory_space=pltpu.SEMAPHORE)
EFFECT = pltpu.SideEffectType.DATAFLOW_SIDE_EFFECTING


def _peers():
    x, y, c = lax.axis_index("x"), lax.axis_index("y"), lax.axis_index("c")
    peers = []
    for k in range(1, N_DEV):
        px = 1 - x if k & 4 else x
        py = 1 - y if k & 2 else y
        pc = 1 - c if k & 1 else c
        peers.append(((px, py, pc), 4 * px + 2 * py + pc))
    return 4 * x + 2 * y + c, peers


def _send_start(srcs, name, gather):
    n = len(srcs)
    lands = [((N_DEV,) + s.shape) if gather else s.shape for s in srcs]

    def body(*refs):
        src_refs, land_refs = refs[:n], refs[n:2 * n]
        send_sems, recv_sems, token = refs[2 * n], refs[2 * n + 1], refs[-1]
        me, peers = _peers()
        for i in range(n):
            for k, (dev, idx) in enumerate(peers):
                pltpu.make_async_remote_copy(
                    src_ref=src_refs[i] if gather else src_refs[i].at[idx], dst_ref=land_refs[i].at[me],
                    send_sem=send_sems.at[7 * i + k], recv_sem=recv_sems.at[7 * i + k], device_id=dev,
                    device_id_type=MESH).start()
        token[...] = jnp.zeros_like(token)

    res = pl.pallas_call(
        body, name=name,
        out_shape=(pltpu.SemaphoreType.DMA((7 * n,)), pltpu.SemaphoreType.DMA((7 * n,)),
                   *[pltpu.HBM(s.shape, s.dtype) for s in srcs],
                   *[pltpu.HBM(shape, s.dtype) for shape, s in zip(lands, srcs)],
                   jax.ShapeDtypeStruct((8, 128), F32)),
        in_specs=(HBM,) * (2 * n), out_specs=(SEM, SEM) + (HBM,) * (2 * n) + (pl.BlockSpec(memory_space=pltpu.VMEM),),
        input_output_aliases={i: 2 + i for i in range(2 * n)},
        compiler_params=pltpu.CompilerParams(has_side_effects=EFFECT),
    )(*[pltpu.with_memory_space_constraint(s, pltpu.HBM) for s in srcs],
      *[pltpu.with_memory_space_constraint(lax.empty(shape, s.dtype), pltpu.HBM) for shape, s in zip(lands, srcs)])
    return dict(sems=res[:2], srcs=res[2:2 + n], lands=res[2 + n:2 + 2 * n], token=res[-1])


def _send_wait(handle, after, name, gather):
    n = len(handle["srcs"])

    def body(*refs):
        src_refs, land_refs = refs[:n], refs[n:2 * n]
        send_sems, recv_sems = refs[2 * n], refs[2 * n + 1]
        me, peers = _peers()
        for i in range(n):
            for k, (dev, idx) in enumerate(peers):
                cp = pltpu.make_async_remote_copy(
                    src_ref=src_refs[i] if gather else src_refs[i].at[idx], dst_ref=land_refs[i].at[idx],
                    send_sem=send_sems.at[7 * i + k], recv_sem=recv_sems.at[7 * i + k], device_id=dev,
                    device_id_type=MESH)
                cp.wait_send()
                cp.wait_recv()

    both = list(handle["srcs"]) + list(handle["lands"])
    res = pl.pallas_call(
        body, name=name, out_shape=tuple(pltpu.HBM(t.shape, t.dtype) for t in both),
        in_specs=(HBM,) * (2 * n) + (SEM, SEM, pl.BlockSpec(memory_space=pl.ANY)), out_specs=(HBM,) * (2 * n),
        input_output_aliases={i: i for i in range(2 * n)},
        compiler_params=pltpu.CompilerParams(has_side_effects=EFFECT),
    )(*both, *handle["sems"], after)
    return res[:n], res[n:]


def _adamw(parts, w, m, v, name, tr=128):
    pieces = len(parts)
    n, r, wd = parts[0].shape
    tr = next((t for t in (tr, 64, 32, 16) if r % t == 0), r)
    nrt = r // tr

    def body(*refs):
        w_ref, m_ref, v_ref, g_ref, d_ref, nm_ref, nv_ref = refs[pieces:]

        def update(p_ref):
            g = p_ref[0].astype(F32)
            for k in range(1, n):
                g = g + p_ref[k].astype(F32)
            m_new = B1 * m_ref[...] + (1.0 - B1) * g
            v_new = B2 * v_ref[...] + (1.0 - B2) * (g * g)
            m_hat = m_new / (1.0 - B1 ** STEP)
            v_hat = v_new / (1.0 - B2 ** STEP)
            g_ref[...] = g
            d_ref[...] = -LR * (m_hat / (jnp.sqrt(v_hat) + ADAM_EPS) + WD * w_ref[...])
            nm_ref[...] = m_new
            nv_ref[...] = v_new

        for p in range(pieces):
            pl.when(pl.program_id(0) == p)(functools.partial(update, refs[p]))

    part_spec = lambda p: pl.BlockSpec((n, tr, wd), lambda l, i: (0, jnp.clip(i + (l - p) * nrt, 0, nrt - 1), 0))
    blk = pl.BlockSpec((tr, wd), lambda l, i: (l * nrt + i, 0))
    return pl.pallas_call(
        body, name=name, grid=(pieces, nrt),
        in_specs=[part_spec(p) for p in range(pieces)] + [blk, blk, blk],
        out_specs=[blk] * 4, out_shape=[jax.ShapeDtypeStruct((pieces * r, wd), F32)] * 4,
        compiler_params=_params(("arbitrary", "arbitrary")),
    )(*parts, w, m, v)


def _outer8(ct, dm, name):
    k, n = ct.shape[0], dm.shape[1]

    def body(c_ref, d_ref, o_ref):
        cv, dv = c_ref[...], d_ref[...]
        acc = cv[:, 0:1] * dv[0:1, :]
        for s in range(1, N_DEV):
            acc = acc + cv[:, s:s + 1] * dv[s:s + 1, :]
        o_ref[...] = acc

    tk = 256
    return pl.pallas_call(
        body, name=name, grid=(k // tk,),
        in_specs=[pl.BlockSpec((tk, N_DEV), lambda i: (i, 0)), pl.BlockSpec((N_DEV, n), lambda i: (0, 0))],
        out_specs=pl.BlockSpec((tk, n), lambda i: (i, 0)), out_shape=jax.ShapeDtypeStruct((k, n), F32),
        compiler_params=_params(("parallel",)),
    )(ct, dm)


FULL = (0, D)
C128 = (0, 128)
HEAD_NOPE = [(h * HEAD_PAD, NOPE) for h in range(HEADS)]
HEAD_ROPE = [(h * HEAD_PAD + NOPE, 128) for h in range(HEADS)]
HEAD_ALL = [(h * HEAD_PAD, HEAD_PAD) for h in range(HEADS)]
HEAD_V = [(h * HEAD, HEAD) for h in range(HEADS)]


def _modulate(x, p, ties=()):
    return _rowwise_fwd(_modulate_fn, [(x, FULL)], [p["gain"], p["scale"], p["shift"]], [(D, BF16, FULL)], "modulate",
                        ties=ties)[0]


def _residual_bwd(y, gm, dxn):
    return _rowwise_bwd(_gate_only_fn, [(y, FULL)], [gm], [(D, F32, FULL)], [dxn], [(0, FULL)], [(D, BF16)],
                        "residual_bwd")


def _modulate_bwd(x, p, dh, dx_in, prev=None):
    pars = [p["gain"], p["scale"], p["shift"]]
    if prev is None:
        return list(_rowwise_bwd(_modulate_fn, [(x, FULL)], pars, [(D, BF16, FULL)], [dh], [(0, FULL)], [(D, F32)],
                                 "modulate_bwd", add={0: dx_in})) + [None]
    s = x.shape[0]
    ts = min(ROW_TILE, s)

    def body(x_ref, g_ref, sc_ref, sh_ref, dh_ref, din_ref, y_ref, gm_ref, dx_ref, dy_ref, dg_ref, dsc_ref, dsh_ref,
             dgm_ref):
        _, vjp = jax.vjp(lambda *t: _modulate_fn(0, *t)[0], x_ref[...], g_ref[...], sc_ref[...], sh_ref[...])
        dxm, dg, dsc, dsh = vjp(dh_ref[...])
        dx = dxm + din_ref[...]
        dx_ref[...] = dx
        dy_ref[...] = (gm_ref[...] * dx).astype(BF16)
        sums = (dg, dsc, dsh, jnp.sum(dx * y_ref[...], axis=0, keepdims=True))
        first = pl.program_id(0) == 0
        for ref, val in zip((dg_ref, dsc_ref, dsh_ref, dgm_ref), sums):
            @pl.when(first)
            def _(ref=ref, val=val):
                ref[...] = val

            @pl.when(jnp.logical_not(first))
            def _(ref=ref, val=val):
                ref[...] += val

    blk = pl.BlockSpec((ts, D), lambda i: (i, 0))
    vec = pl.BlockSpec((1, D), lambda i: (0, 0))
    dx, dy, dg, dsc, dsh, dgm = pl.pallas_call(
        body, name="modulate_bwd_chain", grid=(s // ts,),
        in_specs=[blk, vec, vec, vec, blk, blk, blk, vec], out_specs=[blk, blk, vec, vec, vec, vec],
        out_shape=[jax.ShapeDtypeStruct((s, D), F32), jax.ShapeDtypeStruct((s, D), BF16)]
        + [jax.ShapeDtypeStruct((1, D), F32)] * 4,
        compiler_params=_params(("arbitrary",)),
    )(x, *pars, dh, dx_in, prev[0], prev[1])
    return [dx, dg, dsc, dsh, (dy, dgm)]


def _out_proj(a, w, x, p, nxt, name, **kw):
    res = _matmul([(a, w)], "nn", name, out_dtype=BF16, resid=(x, p["gm"]) + tuple(nxt or ()), **kw)
    return res[1], res[0], (res[2] if nxt else None)


def _ffn_fwd(x, p, ties=(), h=None, nxt=None):
    if h is None:
        h, ties = _modulate(x, p, ties), ()
    gate, up, act = _ffn_in(h, p["wg"], p["wu"], "ffn_in", ties=ties)
    xn, y, hn = _out_proj(act, p["wo"], x, p, nxt, "ffn_out", tm=512, tk=D_FF)
    return xn, dict(x=x, h=h, gate=gate, up=up, act=act, y=y), hn


def _ffn_bwd(t, p, dxn, res=None, prev=None, ties=()):
    dy, dgm = res or _residual_bwd(t["y"], p["gm"], dxn)
    dgate, dup = _ffn_bwd_act(dy, p["wo"], t["gate"], t["up"], "ffn_bwd_act", ties=ties)
    dwo = _mm(t["act"], dy, "tn", "ffn_dwo", ties=ties)
    dh = _matmul([(dgate, p["wg"]), (dup, p["wu"])], "nt", "ffn_dh")
    dwg = _mm(t["h"], dgate, "tn", "ffn_dwi")
    dwu = _mm(t["h"], dup, "tn", "ffn_dwi")
    dx, dgain, dscale, dshift, res_prev = _modulate_bwd(t["x"], p, dh, dxn, prev)
    return dx, dict(gain=dgain, scale=dscale, shift=dshift, gm=dgm, wg=dwg, wu=dwu, wo=dwo), res_prev


def _pad128(t):
    return jnp.pad(t, ((0, 0), (0, 128 - t.shape[1])))


def _gdn_fwd(x, p, ties=(), h=None, nxt=None):
    s = x.shape[0]
    if h is None:
        h, ties = _modulate(x, p, ties), ()
    pm = _mm(h, p["w_main"], "nn", "gdn_proj", ties=ties)
    tail = _mm(h, p["w_tail"], "nn", "gdn_proj_tail", ties=ties)
    qkv = _gdn_conv_fwd(pm, p["conv_w"], "gdn_conv")
    beta, gcum = _rowwise_fwd(_gdn_gates_fn, [(tail, C128), (tail, (128, 128))], [p["a_log"], p["dt_bias"]],
                              [(128, F32, C128)] * 2, "gdn_gates")
    grow = gcum[:, :HEADS].reshape(s // CHUNK, CHUNK, N_GROUPS, GROUP).transpose(0, 2, 3, 1)
    grow = grow.reshape(s // CHUNK, N_GROUPS, 1, GROWS)
    o, states, invs = _gdn_scan_fwd(qkv, beta, gcum, grow, "gdn_scan")
    on, = _rowwise_fwd(_gdn_outnorm_fn, [(o, HEAD_V), (pm, [(3 * D + h_ * HEAD, HEAD) for h_ in range(HEADS)])],
                       [p["norm_g"]], [(D, BF16, HEAD_V)], "gdn_outnorm", groups=HEADS)
    xn, y, hn = _out_proj(on, p["w_out"], x, p, nxt, "mix_out", tm=512)
    t = dict(x=x, h=h, pm=pm, tail=tail, qkv=qkv, beta=beta, gcum=gcum, grow=grow, o=o, states=states, invs=invs,
             on=on, y=y)
    return xn, t, hn


def _gdn_bwd(t, p, dxn, res=None, prev=None, ties=()):
    s = dxn.shape[0]
    zc = [(3 * D + h_ * HEAD, HEAD) for h_ in range(HEADS)]
    dy, dgm = res or _residual_bwd(t["y"], p["gm"], dxn)
    dw_out = _mm(t["on"], dy, "tn", "mix_dwo", ties=ties)
    don = _mm(dy, p["w_out"], "nt", "mix_dout", ties=ties)
    do, dz, dnorm_g = _rowwise_bwd(_gdn_outnorm_fn, [(t["o"], HEAD_V), (t["pm"], zc)], [p["norm_g"]],
                                   [(D, BF16, HEAD_V)], [don], [(0, HEAD_V), (1, HEAD_V)], [(D, F32), (D, BF16)],
                                   "gdn_outnorm_bwd", groups=HEADS)
    dq, dk, dv, dbeta, dg, dgr = _gdn_scan_bwd(t["qkv"], t["beta"], t["gcum"], t["grow"], t["states"], t["invs"], do,
                                               "gdn_scan_bwd")
    dg = dg + _pad128(dgr.reshape(s // CHUNK, N_GROUPS, GROUP, CHUNK).transpose(0, 3, 1, 2).reshape(s, HEADS))
    dtail, da_log, ddt = _rowwise_bwd(_gdn_gates_fn, [(t["tail"], C128), (t["tail"], (128, 128))],
                                      [p["a_log"], p["dt_bias"]], [(128, F32, C128)] * 2, [dbeta, dg],
                                      [(0, C128), (0, (128, 128))], [(256, F32)], "gdn_gates_bwd")
    dxs, dcw = [], []
    for part, d in enumerate((dq, dk, dv)):
        dx_, dw_ = _gdn_conv_bwd(t["pm"], p["conv_w"], d, part, "gdn_conv_bwd")
        dxs.append(dx_)
        dcw.append(dw_)
    pieces = dxs + [dz]
    dh = _matmul([(d, p["w_main"]) for d in pieces] + [(dtail, p["w_tail"])], "nt", "gdn_dh",
                 boffs=[0, D, 2 * D, 3 * D, 0], tk=512)
    dw_main = [_mm(t["h"], d, "tn", "gdn_dwi") for d in pieces]
    dw_tail = _mm(t["h"], dtail, "tn", "gdn_dwi_tail")
    dx, dgain, dscale, dshift, res_prev = _modulate_bwd(t["x"], p, dh, dxn, prev)
    return dx, dict(gain=dgain, scale=dscale, shift=dshift, gm=dgm, w_main=jnp.concatenate(dw_main, axis=1),
                    w_tail=dw_tail, conv_w=jnp.concatenate(dcw, axis=1), a_log=da_log, dt_bias=ddt,
                    norm_g=dnorm_g, w_out=dw_out), res_prev


def _q_rows(q2, cosf, sins):
    return [(q2, HEAD_NOPE), (q2, HEAD_ROPE), (cosf, C128), (sins, C128)]


def _mla_fwd(x, p, kv, ties=(), h=None, nxt=None):
    if h is None:
        h, ties = _modulate(x, p, ties), ()
    cq = _mm(h, p["w_dq"], "nn", "mla_dq", ties=ties)
    cqn, = _rowwise_fwd(_rms_fn, [(cq, (0, Q_LORA))], [p["q_lora_g"]], [(Q_LORA, BF16, (0, Q_LORA))], "mla_qlora_norm")
    q2 = _mm(cqn, p["w_uq"], "nn", "mla_uq")
    qn, = _rowwise_fwd(_q_norm_rope_fn, _q_rows(q2, kv["cosf"], kv["sins"]), [p["q_gn"], p["q_gr"]],
                       [(HEADS * HEAD_PAD, BF16, HEAD_ALL)], "mla_q_norm", groups=HEADS)
    o, lse = _attn_fwd(qn, kv["kn"], kv["vb"], "mla_attn")
    xn, y, hn = _out_proj(o, p["w_out"], x, p, nxt, "mix_out", tm=512)
    return xn, dict(x=x, h=h, cq=cq, cqn=cqn, q2=q2, qn=qn, o=o, lse=lse, y=y), hn


def _mla_bwd(t, p, kv, dxn, res=None, prev=None, ties=(), dkv_sum=None):
    dy, dgm = res or _residual_bwd(t["y"], p["gm"], dxn)
    dw_out = _mm(t["o"], dy, "tn", "mix_dwo", ties=ties)
    do = _mm(dy, p["w_out"], "nt", "mix_dout", ties=ties)
    dq, dk, dv = _attn_bwd(t["qn"], kv["kn"], kv["vb"], do, t["o"], t["lse"], "mla_attn_bwd", dkv_sum)
    dq2, dq_gn, dq_gr = _rowwise_bwd(_q_norm_rope_fn, _q_rows(t["q2"], kv["cosf"], kv["sins"]), [p["q_gn"], p["q_gr"]],
                                     [(HEADS * HEAD_PAD, BF16, HEAD_ALL)], [dq],
                                     [(0, HEAD_NOPE), (0, HEAD_ROPE), None, None], [(HEADS * HEAD_PAD, BF16)],
                                     "mla_q_norm_bwd", groups=HEADS)
    dw_uq = _mm(t["cqn"], dq2, "tn", "mla_dwuq")
    dcqn = _mm(dq2, p["w_uq"], "nt", "mla_dcq")
    dcq, dq_lora_g = _rowwise_bwd(_rms_fn, [(t["cq"], (0, Q_LORA))], [p["q_lora_g"]], [(Q_LORA, BF16, (0, Q_LORA))],
                                  [dcqn], [(0, (0, Q_LORA))], [(Q_LORA, BF16)], "mla_qlora_norm_bwd")
    dw_dq = _mm(t["h"], dcq, "tn", "mla_dwdq")
    dh = _mm(dcq, p["w_dq"], "nt", "mla_dh")
    dx, dgain, dscale, dshift, res_prev = _modulate_bwd(t["x"], p, dh, dxn, prev)
    grads = dict(gain=dgain, scale=dscale, shift=dshift, gm=dgm, w_dq=dw_dq, q_lora_g=dq_lora_g, w_uq=dw_uq,
                 q_gn=dq_gn, q_gr=dq_gr, w_out=dw_out)
    return dx, grads, res_prev, dk, dv


def _k_rows(kvp, ckv, cosf, sins):
    return [(kvp, HEAD_NOPE), (kvp, HEAD_ROPE), (ckv, (KV_LORA, 128)), (cosf, C128), (sins, C128)]


def _kv_fwd(x, p, cosf, sins):
    h = _modulate(x, p)
    ckv = _mm(h, p["w_dkv"], "nn", "kv_down")
    lat, = _rowwise_fwd(_rms_fn, [(ckv, (0, KV_LORA))], [p["kv_g"]], [(KV_LORA, BF16, (0, KV_LORA))], "kv_norm")
    kvp = _mm(lat, p["w_ukv"], "nn", "kv_up")
    kn, vb = _rowwise_fwd(_k_norm_rope_fn, _k_rows(kvp, ckv, cosf, sins), [p["k_gn"], p["k_gr"]],
                          [(HEADS * HEAD_PAD, BF16, HEAD_ALL), (HEADS * HEAD, BF16, HEAD_V)], "kv_k_norm",
                          groups=HEADS)
    return dict(x=x, h=h, ckv=ckv, lat=lat, kvp=kvp, kn=kn, vb=vb, cosf=cosf, sins=sins)


def _kv_bwd(t, p, dk, dv, dx_in, prev):
    dkvp, drope, dk_gn, dk_gr = _rowwise_bwd(
        _k_norm_rope_fn, _k_rows(t["kvp"], t["ckv"], t["cosf"], t["sins"]), [p["k_gn"], p["k_gr"]],
        [(HEADS * HEAD_PAD, BF16, HEAD_ALL), (HEADS * HEAD, BF16, HEAD_V)], [dk, dv],
        [(0, HEAD_NOPE), (0, HEAD_ROPE), (1, C128), None, None], [(HEADS * HEAD_PAD, BF16), (128, F32)],
        "kv_k_norm_bwd", groups=HEADS)
    dw_ukv = _mm(t["lat"], dkvp, "tn", "kv_dwukv")
    dlat = _mm(dkvp, p["w_ukv"], "nt", "kv_dlat")
    dckv, dkv_g = _rowwise_bwd(_rms_fn, [(t["ckv"], (0, KV_LORA))], [p["kv_g"]], [(KV_LORA, BF16, (0, KV_LORA))],
                               [dlat], [(0, (0, KV_LORA))], [(KV_LORA, F32)], "kv_norm_bwd")
    dw_dkv = jnp.concatenate([_mm(t["h"], dckv, "tn", "kv_dwdkv"), _mm(t["h"], drope, "tn", "kv_dwdkv_rope")], axis=1)
    dh = _matmul([(dckv, p["w_dkv"]), (drope, p["w_dkv"])], "nt", "kv_dh", boffs=[0, KV_LORA])
    dx, dgain, dscale, dshift, res_prev = _modulate_bwd(t["x"], p, dh, dx_in, prev)
    return dx, dict(gain=dgain, scale=dscale, shift=dshift, w_dkv=dw_dkv, kv_g=dkv_g, w_ukv=dw_ukv, k_gn=dk_gn,
                    k_gr=dk_gr), res_prev


WEIGHTS = ["ada_w", "ada_b", "norm_g", "ffn_w_in", "ffn_w_out", "gdn_w_in", "gdn_conv_w", "gdn_a_log", "gdn_dt_bias",
           "gdn_norm_g", "gdn_w_out", "kv_ada_w", "kv_ada_b", "kv_norm_g", "mla_w_dkv", "mla_kv_norm_g", "mla_w_ukv",
           "mla_k_norm_g", "mla_w_dq", "mla_q_lora_norm_g", "mla_w_uq", "mla_q_norm_g", "mla_w_out"]
SMALL = [("ada_b", 4 * N_MOD * D), ("kv_ada_b", 2 * D), ("norm_g", DEPTH * 3 * D), ("gdn_conv_w", N_A * CONV_K * 3 * D),
         ("gdn_a_log", N_A * HEADS), ("gdn_dt_bias", N_A * HEADS), ("gdn_norm_g", N_A * HEAD), ("kv_norm_g", D),
         ("mla_kv_norm_g", KV_LORA), ("mla_k_norm_g", QK_HEAD), ("mla_q_lora_norm_g", 2 * Q_LORA),
         ("mla_q_norm_g", 2 * QK_HEAD)]
SMALL_REPLICATED = [n for n, _ in SMALL if n not in ("norm_g", "gdn_conv_w")]


def _silu_fn(g, t):
    return (_silu(t),)


def _dup_rope(t):
    return jnp.concatenate([t[..., :NOPE], t[..., NOPE:], t[..., NOPE:]], axis=-1)


def _fold_rope(t):
    return jnp.concatenate([t[..., :NOPE], t[..., NOPE:QK_HEAD] + t[..., QK_HEAD:]], axis=-1)


def _pack(pieces, rows):
    flat = jnp.concatenate([p.reshape(-1).astype(F32) for p in pieces])
    return jnp.pad(flat, (0, rows * 128 - flat.shape[0])).reshape(rows, 128)


def _step(a):
    me = 4 * lax.axis_index("x") + 2 * lax.axis_index("y") + lax.axis_index("c")
    x = a["x"][0]
    cosf, sins = _rope_tables(a["positions"][0])

    n_in = 2 * D_FF // N_DEV
    n_gdn = (4 * D + 2 * HEADS) // N_DEV
    AHEAD = 2

    stages = [(l, part) for l in range(DEPTH) for part in range(3)]

    def stage_shards(l, part):
        if part != 1:
            sh = {"ffn_w_in": a["ffn_w_in"][l, part // 2], "ffn_w_out": a["ffn_w_out"][l, part // 2]}
            if part == 2 and l == N_A - 1:
                sh.update(mla_w_dkv=a["mla_w_dkv"], mla_w_ukv=a["mla_w_ukv"])
            return sh
        if l < N_A:
            return {"gdn_w_in": a["gdn_w_in"][l], "gdn_w_out": a["gdn_w_out"][l]}
        j = l - N_A
        return {"mla_w_dq": a["mla_w_dq"][j], "mla_w_uq": a["mla_w_uq"][j], "mla_w_out": a["mla_w_out"][j]}

    def zero_of(t):
        return jnp.minimum(jnp.abs(t[(0,) * t.ndim].astype(F32)), 0.0)

    def start_stage(l, part, tie):
        sh = stage_shards(l, part)
        return list(sh), _send_start([(w + tie).astype(BF16) for w in sh.values()], f"fetch_start_{l}_{part}", gather=True)

    def finish_stage(l, part, names, handle, after):
        srcs, lands = _send_wait(handle, after, f"fetch_wait_{l}_{part}", gather=True)
        return {n: lax.dynamic_update_slice(land, src[None], (me, 0, 0)) for n, src, land in zip(names, srcs, lands)}

    n_cw, n_ng = N_A * CONV_K * 3 * HEAD, DEPTH * 3 * HEAD
    small_all = _all_gather(_pack([a["gdn_conv_w"], a["norm_g"], a["c"]], 44), "gather_small").reshape(N_DEV, -1)
    conv_w = small_all[:, :n_cw].reshape(N_DEV, N_A, CONV_K, 3 * HEAD).transpose(1, 2, 0, 3).reshape(N_A, CONV_K, 3 * D)
    norm_g = small_all[:, n_cw:n_cw + n_ng].reshape(N_DEV, DEPTH, 3, HEAD).transpose(1, 2, 0, 3).reshape(DEPTH, 3, D)
    c_all = small_all[:, n_cw + n_ng:n_cw + n_ng + D]

    c_act, = _rowwise_fwd(_silu_fn, [(c_all, FULL)], [], [(D, F32, FULL)], "c_act")
    n_ada = N_MOD * D // N_DEV
    parts = [_mm(c_act, a["ada_w"][l], "nn", "mod_proj") for l in range(DEPTH)]
    parts.append(_mm(c_act, a["kv_ada_w"], "nn", "mod_proj_kv"))
    mod_recv = _exchange(jnp.concatenate(parts, axis=1)[:, None, :], "exchange_mod")[:, 0]
    mod = mod_recv[:, :DEPTH * n_ada].reshape(N_DEV, DEPTH, n_ada).transpose(1, 0, 2).reshape(DEPTH, N_MOD * D)
    mod = (mod + a["ada_b"]).reshape(DEPTH, N_MOD, D)
    kvmod = mod_recv[:, DEPTH * n_ada:].reshape(2 * D) + a["kv_ada_b"]

    def row(v):
        return v[None]

    def ffn_params(l, i, w):
        w_in = w["ffn_w_in"]
        k = 0 if i == 0 else 6
        return dict(gain=row(norm_g[l, 0 if i == 0 else 2]), shift=row(mod[l, k]), scale=row(mod[l, k + 1]),
                    gm=0.5 * row(mod[l, k + 2]),
                    wg=w_in[:N_DEV // 2].transpose(1, 0, 2).reshape(D, D_FF),
                    wu=w_in[N_DEV // 2:].transpose(1, 0, 2).reshape(D, D_FF),
                    wo=w["ffn_w_out"].reshape(D_FF, D))

    def gdn_params(l, w):
        w_in = w["gdn_w_in"].transpose(1, 0, 2).reshape(D, 4 * D + 2 * HEADS)
        pad = lambda t: jnp.pad(t, ((0, 0), (0, 128 - HEADS)))
        return dict(gain=row(norm_g[l, 1]), shift=row(mod[l, 3]), scale=row(mod[l, 4]), gm=row(mod[l, 5]),
                    w_main=w_in[:, :4 * D],
                    w_tail=jnp.concatenate([pad(w_in[:, 4 * D:4 * D + HEADS]), pad(w_in[:, 4 * D + HEADS:])], axis=1),
                    conv_w=conv_w[l], a_log=_pad128(row(a["gdn_a_log"][l])), dt_bias=_pad128(row(a["gdn_dt_bias"][l])),
                    norm_g=row(a["gdn_norm_g"][l]), w_out=w["gdn_w_out"].reshape(D, D))

    def mla_params(l, w):
        j = l - N_A
        uq = w["mla_w_uq"].transpose(1, 0, 2)
        qg = _dup_rope(a["mla_q_norm_g"][j])
        return dict(gain=row(norm_g[l, 1]), shift=row(mod[l, 3]), scale=row(mod[l, 4]), gm=row(mod[l, 5]),
                    w_dq=w["mla_w_dq"].reshape(D, Q_LORA), q_lora_g=row(a["mla_q_lora_norm_g"][j]),
                    w_uq=_dup_rope(uq).reshape(Q_LORA, HEADS * HEAD_PAD), q_gn=row(qg[:NOPE]), q_gr=row(qg[NOPE:]),
                    w_out=w["mla_w_out"].reshape(D, D))

    def kv_params(w):
        w_dkv = w["mla_w_dkv"].reshape(D, KV_LORA + ROPE)
        kg = _dup_rope(a["mla_k_norm_g"])
        return dict(gain=row(a["kv_norm_g"]), shift=row(kvmod[:D]), scale=row(kvmod[D:]),
                    w_dkv=jnp.concatenate([w_dkv, w_dkv[:, KV_LORA:]], axis=1), kv_g=row(a["mla_kv_norm_g"]),
                    w_ukv=w["mla_w_ukv"].transpose(1, 0, 2).reshape(KV_LORA, HEADS * 2 * HEAD), k_gn=row(kg[:NOPE]),
                    k_gr=row(kg[NOPE:]))

    tapes, kv, kv_p, h = [[] for _ in range(DEPTH)], None, None, None
    first = {name: _all_gather((w + zero_of(mod)).astype(BF16), "fetch_first_" + name)
             for name, w in stage_shards(0, 0).items()}
    pending = []
    for l, part in stages[1:1 + AHEAD]:
        tie = pending[-1][1]["token"][0, 0] if pending else zero_of(first["ffn_w_out"])
        pending.append(start_stage(l, part, tie))
    for n, (l, part) in enumerate(stages):
        if n == 0:
            w, ties = first, tuple(h["token"] for _, h in pending)
        else:
            names, handle = pending.pop(0)
            w = finish_stage(l, part, names, handle, x)
            ties = ()
            if n + AHEAD < len(stages):
                pending.append(start_stage(*stages[n + AHEAD], zero_of(w[names[0]])))
                ties = (pending[-1][1]["token"],)
        nxt = None
        if n + 1 < len(stages):
            l2, part2 = stages[n + 1]
            k2 = 3 * part2
            nxt = (row(norm_g[l2, part2]), row(mod[l2, k2 + 1]), row(mod[l2, k2]))
        if part != 1:
            p = ffn_params(l, part // 2, w)
            x, t, h = _ffn_fwd(x, p, ties, h, nxt)
        else:
            p = gdn_params(l, w) if l < N_A else mla_params(l, w)
            x, t, h = _gdn_fwd(x, p, ties, h, nxt) if l < N_A else _mla_fwd(x, p, kv, ties, h, nxt)
        tapes[l] += [p, t]
        if part == 2 and l == N_A - 1:
            kv_p = kv_params(w)
            kv = _kv_fwd(x, kv_p, cosf, sins)
    dx, loss_blk = _loss_and_grad(x, a["loss_target"][0], "loss")
    loss = lax.psum(loss_blk[0, 0], ("x", "y", "c"))

    def by_cols(g, n):
        return g.reshape(g.shape[0], -1, n).transpose(1, 0, 2)

    def ffn_blocks(g):
        return {"ffn_w_in": jnp.concatenate([by_cols(g["wg"], n_in), by_cols(g["wu"], n_in)], axis=0),
                "ffn_w_out": g["wo"].reshape(N_DEV, D_FF // N_DEV, D)}

    def mixer_blocks(l, g):
        if l < N_A:
            full = jnp.concatenate([g["w_main"], g["w_tail"][:, :HEADS], g["w_tail"][:, 128:128 + HEADS]], axis=1)
            return {"gdn_w_in": by_cols(full, n_gdn), "gdn_w_out": g["w_out"].reshape(N_DEV, D // N_DEV, D)}
        return {"mla_w_dq": g["w_dq"].reshape(N_DEV, D // N_DEV, Q_LORA),
                "mla_w_uq": _fold_rope(g["w_uq"].reshape(Q_LORA, HEADS, HEAD_PAD)).transpose(1, 0, 2),
                "mla_w_out": g["w_out"].reshape(N_DEV, D // N_DEV, D)}

    sent = []

    def send(key, blocks, tie=0.0):
        handle = _send_start([(b + tie).astype(BF16) for b in blocks.values()], "grad_start_" + "_".join(map(str, key)),
                             gather=False)
        sent.append((key, list(blocks), handle))
        return (handle["token"],)

    grads = [None] * DEPTH
    dk_sum = dv_sum = kv_grads = res = None
    ties = ()
    for l in reversed(range(DEPTH)):
        p1, t1, pm_, tm_, p2, t2 = tapes[l]
        if l == N_A - 1:
            dx, kv_grads, res = _kv_bwd(kv, kv_p, dk_sum, dv_sum, dx, (t2["y"], p2["gm"]))
            d_dkv = kv_grads["w_dkv"]
            ties += send((l, 3), {
                "mla_w_dkv": jnp.concatenate(
                    [d_dkv[:, :KV_LORA], d_dkv[:, KV_LORA:KV_LORA + ROPE] + d_dkv[:, KV_LORA + ROPE:]],
                    axis=1).reshape(N_DEV, D // N_DEV, KV_LORA + ROPE),
                "mla_w_ukv": by_cols(kv_grads["w_ukv"], 2 * HEAD)})
        dx, g2, res = _ffn_bwd(t2, p2, dx, res, (tm_["y"], pm_["gm"]), ties)
        ties = send((l, 2), ffn_blocks(g2))
        if l < N_A:
            dx, gm_, res = _gdn_bwd(tm_, pm_, dx, res, (t1["y"], p1["gm"]), ties)
        else:
            dx, gm_, res, dk_sum, dv_sum = _mla_bwd(tm_, pm_, kv, dx, res, (t1["y"], p1["gm"]), ties,
                                                    None if dk_sum is None else (dk_sum, dv_sum))
        ties = send((l, 1), mixer_blocks(l, gm_))
        prev = (tapes[l - 1][5]["y"], tapes[l - 1][4]["gm"]) if l > 0 and l != N_A else None
        dx, g1, res = _ffn_bwd(t1, p1, dx, res, prev, ties)
        if l > 0:
            ties = send((l, 0), ffn_blocks(g1))
        grads[l] = (g1, gm_, g2)

    out = {}
    def dmod(l):
        g1, gm_, g2 = grads[l]
        return jnp.concatenate([g1["shift"], g1["scale"], 0.5 * g1["gm"], gm_["shift"], gm_["scale"], gm_["gm"],
                                g2["shift"], g2["scale"], 0.5 * g2["gm"]], axis=1)

    gdn = [grads[l][1] for l in range(N_A)]
    mla = [grads[l][1] for l in range(N_A, DEPTH)]
    small = {
        "ada_b": jnp.concatenate([dmod(l) for l in range(DEPTH)], axis=0),
        "kv_ada_b": jnp.concatenate([kv_grads["shift"], kv_grads["scale"]], axis=1),
        "norm_g": jnp.stack([jnp.concatenate([grads[l][0]["gain"], grads[l][1]["gain"], grads[l][2]["gain"]], axis=0)
                             for l in range(DEPTH)]),
        "gdn_conv_w": jnp.stack([g["conv_w"] for g in gdn]),
        "gdn_a_log": jnp.stack([g["a_log"][0, :HEADS] for g in gdn]),
        "gdn_dt_bias": jnp.stack([g["dt_bias"][0, :HEADS] for g in gdn]),
        "gdn_norm_g": jnp.stack([g["norm_g"][0] for g in gdn]),
        "kv_norm_g": kv_grads["gain"],
        "mla_kv_norm_g": kv_grads["kv_g"],
        "mla_k_norm_g": _fold_rope(jnp.concatenate([kv_grads["k_gn"], kv_grads["k_gr"]], axis=1)),
        "mla_q_lora_norm_g": jnp.stack([g["q_lora_g"][0] for g in mla]),
        "mla_q_norm_g": jnp.stack([_fold_rope(jnp.concatenate([g["q_gn"], g["q_gr"]], axis=1))[0] for g in mla]),
    }
    rows = 616
    assert sum(n for _, n in SMALL) <= rows * 128 and all(small[n].size == k for n, k in SMALL)
    small_recv = _all_gather(_pack([small[n] for n, _ in SMALL], rows), "gather_small_grads")
    small_recv = small_recv + send((0, 0), ffn_blocks(grads[0][0]), zero_of(small_recv))[0][0, 0]
    zero = lambda n, k: jnp.zeros((k,), F32)
    packed = {pre: _pack([a[pre + n] if n in SMALL_REPLICATED else zero(n, k) for n, k in SMALL], rows)
              for pre in ("", "m_", "v_")}
    res = _adamw([small_recv], packed[""], packed["m_"], packed["v_"], "adamw_small")
    offs = {}
    o = 0
    for n, k in SMALL:
        offs[n] = o
        o += k
    for n, k in SMALL:
        if n in SMALL_REPLICATED:
            out[n] = [r.reshape(-1)[offs[n]:offs[n] + k] for r in res]
    gsum = res[0].reshape(-1)
    g_norm = lax.dynamic_slice_in_dim(gsum[offs["norm_g"]:offs["norm_g"] + DEPTH * 3 * D].reshape(DEPTH * 3, D),
                                      me * HEAD, HEAD, axis=1)
    g_conv = lax.dynamic_slice_in_dim(
        gsum[offs["gdn_conv_w"]:offs["gdn_conv_w"] + N_A * CONV_K * 3 * D].reshape(N_A * CONV_K, 3 * D),
        me * 3 * HEAD, 3 * HEAD, axis=1)
    res2 = _adamw([_pack([g_norm, g_conv], 36)[None]], *[_pack([a[pre + "norm_g"], a[pre + "gdn_conv_w"]], 36)
                                                      for pre in ("", "m_", "v_")], "adamw_small")
    out["norm_g"] = [r.reshape(-1)[:n_ng] for r in res2]
    out["gdn_conv_w"] = [r.reshape(-1)[n_ng:n_ng + n_cw] for r in res2]

    c_act_t = c_act.T
    all_small = small_recv.reshape(N_DEV, -1)
    dmod_all = all_small[:, :DEPTH * N_MOD * D].reshape(N_DEV, DEPTH, N_MOD * D)
    dmod_mine = lax.dynamic_slice_in_dim(dmod_all, me * n_ada, n_ada, axis=2)
    g_ada = [_outer8(c_act_t, dmod_mine[:, l], "ada_grad")[None] for l in range(DEPTH)]
    out["ada_w"] = _adamw(g_ada, *[a[pre + "ada_w"].reshape(DEPTH * D, n_ada) for pre in ("", "m_", "v_")], "adamw")
    dkv_all = all_small[:, offs["kv_ada_b"]:offs["kv_ada_b"] + 2 * D]
    g_kv = _outer8(c_act_t, lax.dynamic_slice_in_dim(dkv_all, me * (2 * D // N_DEV), 2 * D // N_DEV, axis=1), "ada_grad")
    out["kv_ada_w"] = _adamw([g_kv[None]], *[a[pre + "kv_ada_w"] for pre in ("", "m_", "v_")], "adamw")

    pieces = {}
    for key, names, handle in sent:
        srcs, lands = _send_wait(handle, out["kv_ada_w"][0], "grad_wait_" + "_".join(map(str, key)), gather=False)
        for name, src, land in zip(names, srcs, lands):
            own = lax.dynamic_slice_in_dim(src, me, 1, axis=0)
            pieces.setdefault(name, []).append((key, lax.dynamic_update_slice(land, own, (me, 0, 0))))
    for name, parts in pieces.items():
        wide = a[name].shape[-1]
        out[name] = _adamw([p for _, p in sorted(parts, key=lambda kp: kp[0])], a[name].reshape(-1, wide),
                           a["m_" + name].reshape(-1, wide), a["v_" + name].reshape(-1, wide), "adamw")

    result = [loss, dx[None]]
    for k in range(4):
        result += [out[n][k].reshape(a[n].shape) for n in WEIGHTS]
    return tuple(result)


def kernel(x, c, positions, ada_w, ada_b, norm_g, ffn_w_in, ffn_w_out, gdn_w_in, gdn_conv_w, gdn_a_log, gdn_dt_bias, gdn_norm_g, gdn_w_out, kv_ada_w, kv_ada_b, kv_norm_g, mla_w_dkv, mla_kv_norm_g, mla_w_ukv, mla_k_norm_g, mla_w_dq, mla_q_lora_norm_g, mla_w_uq, mla_q_norm_g, mla_w_out, loss_target, m_ada_w, m_ada_b, m_norm_g, m_ffn_w_in, m_ffn_w_out, m_gdn_w_in, m_gdn_conv_w, m_gdn_a_log, m_gdn_dt_bias, m_gdn_norm_g, m_gdn_w_out, m_kv_ada_w, m_kv_ada_b, m_kv_norm_g, m_mla_w_dkv, m_mla_kv_norm_g, m_mla_w_ukv, m_mla_k_norm_g, m_mla_w_dq, m_mla_q_lora_norm_g, m_mla_w_uq, m_mla_q_norm_g, m_mla_w_out, v_ada_w, v_ada_b, v_norm_g, v_ffn_w_in, v_ffn_w_out, v_gdn_w_in, v_gdn_conv_w, v_gdn_a_log, v_gdn_dt_bias, v_gdn_norm_g, v_gdn_w_out, v_kv_ada_w, v_kv_ada_b, v_kv_norm_g, v_mla_w_dkv, v_mla_kv_norm_g, v_mla_w_ukv, v_mla_k_norm_g, v_mla_w_dq, v_mla_q_lora_norm_g, v_mla_w_uq, v_mla_q_norm_g, v_mla_w_out):
    return _step(dict(locals()))
```

```python
import functools
import math

import jax
import jax.numpy as jnp
from jax import lax
from jax.experimental import pallas as pl
from jax.experimental.pallas import tpu as pltpu

F32 = jnp.float32
BF16 = jnp.bfloat16

N_DEV = 8
D = 1024
D_FF = 2816
DEPTH = 4
N_A = 2
N_MOD = 9
HEADS = 8
HEAD = 128
CHUNK = 64
CONV_K = 4
KV_LORA = 256
Q_LORA = 384
NOPE = 128
ROPE = 64
QK_HEAD = NOPE + ROPE
HEAD_PAD = 256
ROPE_BASE = 10000.0
EPS = 1e-6
LR, B1, B2, ADAM_EPS, WD, STEP = 0.001, 0.9, 0.999, 1e-08, 0.01, 10

VMEM_LIMIT = 48 * 1024 * 1024
ROW_TILE = 256
MESH = pl.DeviceIdType.MESH

_NN = (((1,), (0,)), ((), ()))
_NT = (((1,), (1,)), ((), ()))
_TN = (((0,), (0,)), ((), ()))
_DIMS = {"nn": _NN, "nt": _NT, "tn": _TN}


def _params(dims=None):
    return pltpu.CompilerParams(dimension_semantics=dims, vmem_limit_bytes=VMEM_LIMIT)


def _tile(n, target):
    for t in range(target - target % 128, 0, -128):
        if n % t == 0:
            return t
    return n


_TIE_SPEC1 = pl.BlockSpec((8, 128), lambda i: (0, 0))
_TIE_SPEC2 = pl.BlockSpec((8, 128), lambda i, j: (0, 0))
_TIE_SPEC3 = pl.BlockSpec((8, 128), lambda i, j, k: (0, 0))


def _matmul(pairs, form, name, out_dtype=F32, tm=1408, tn=1408, tk=1408, boffs=None, resid=None, ties=()):
    a0, b0 = pairs[0]
    if form == "nn":
        m, n = a0.shape[0], b0.shape[1]
        ks = [a.shape[1] for a, _ in pairs]
    elif form == "nt":
        m, n = a0.shape[0], b0.shape[0]
        ks = [a.shape[1] for a, _ in pairs]
    else:
        m, n = a0.shape[1], b0.shape[1]
        ks = [a.shape[0] for a, _ in pairs]
    tm, tn = _tile(m, tm), _tile(n, tn)
    tks = [_tile(k, tk) for k in ks]
    boffs = boffs or [0] * len(pairs)
    assert m % tm == 0 and n % tn == 0 and all(o % t == 0 for o, t in zip(boffs, tks)), (name, m, n, ks)
    steps = [k // t for k, t in zip(ks, tks)]
    starts = [sum(steps[:p]) for p in range(len(pairs))]
    nk = sum(steps)

    def kidx(p, k):
        return jnp.clip(k - starts[p], 0, steps[p] - 1)

    in_specs, args = [], []
    for p, (a, b) in enumerate(pairs):
        t = tks[p]
        if form == "tn":
            in_specs.append(pl.BlockSpec((t, tm), lambda i, j, k, p=p: (kidx(p, k), i)))
            in_specs.append(pl.BlockSpec((t, tn), lambda i, j, k, p=p: (kidx(p, k), j)))
        elif form == "nn":
            in_specs.append(pl.BlockSpec((tm, t), lambda i, j, k, p=p: (i, kidx(p, k))))
            in_specs.append(pl.BlockSpec((t, tn), lambda i, j, k, p=p: (kidx(p, k), j)))
        else:
            in_specs.append(pl.BlockSpec((tm, t), lambda i, j, k, p=p: (i, kidx(p, k))))
            in_specs.append(pl.BlockSpec((tn, t), lambda i, j, k, p=p, o=boffs[p] // t: (j, kidx(p, k) + o)))
        args += [a, b]
    dims = _DIMS[form]
    npairs = len(pairs)
    nres = len(resid or ())
    nin = 2 * npairs + len(ties) + nres
    out_blk = pl.BlockSpec((tm, tn), lambda i, j, k: (i, j))
    in_specs += [_TIE_SPEC3] * len(ties)
    args += list(ties)
    if resid:
        assert nres == 2 or (nres == 5 and tn == n)
        in_specs += [out_blk] + [pl.BlockSpec((1, tn), lambda i, j, k: (0, j))] * (nres - 1)
        args += list(resid)

    def body(*refs):
        o_ref = refs[nin]
        k = pl.program_id(2)

        def prod(p):
            return lax.dot_general(refs[2 * p][...].astype(BF16), refs[2 * p + 1][...].astype(BF16), dims,
                                   preferred_element_type=F32)

        def finish(y):
            o_ref[...] = y.astype(o_ref.dtype)
            if resid:
                x_ref, gate_ref = refs[nin - nres], refs[nin - nres + 1]
                xn = x_ref[...] + gate_ref[...] * y
                refs[nin + 1][...] = xn
                if nres == 5:
                    gain, scale, shift = (r[...] for r in refs[nin - 3:nin])
                    refs[nin + 2][...] = _modulate_fn(0, xn, gain, scale, shift)[0].astype(BF16)

        if nk == 1:
            finish(prod(0))
            return
        acc = refs[-1]

        @pl.when(k == 0)
        def _():
            acc[...] = jnp.zeros_like(acc)

        for p in range(npairs):
            @pl.when((k >= starts[p]) & (k < starts[p] + steps[p]))
            def _(p=p):
                acc[...] += prod(p)

        @pl.when(k == nk - 1)
        def _():
            finish(acc[...])

    res = pl.pallas_call(
        body, name=name, grid=(m // tm, n // tn, nk), in_specs=in_specs,
        out_specs=[out_blk] * (2 + (nres == 5)) if resid else out_blk,
        out_shape=([jax.ShapeDtypeStruct((m, n), out_dtype), jax.ShapeDtypeStruct((m, n), F32)]
                   + [jax.ShapeDtypeStruct((m, n), BF16)] * (nres == 5))
        if resid else jax.ShapeDtypeStruct((m, n), out_dtype),
        scratch_shapes=[] if nk == 1 else [pltpu.VMEM((tm, tn), F32)],
        compiler_params=_params(("parallel", "parallel", "arbitrary")),
    )(*args)
    return res


def _mm(a, b, form, name, **kw):
    return _matmul([(a, b)], form, name, **kw)


def _cols(spec, g):
    return spec[g] if isinstance(spec, list) else spec


def _rowwise_fwd(fn, rows, pars, outs, name, groups=1, ts=ROW_TILE, ties=()):
    s = rows[0][0].shape[0]
    ts = min(ts, s)
    assert s % ts == 0
    nr, npar = len(rows), len(pars)

    def body(*refs):
        par_t = [r[...] for r in refs[nr:nr + npar]]
        out_refs = refs[nr + npar + len(ties):]
        for g in range(groups):
            row_t = []
            for r, (_, spec) in zip(refs[:nr], rows):
                c0, w = _cols(spec, g)
                row_t.append(r[:, c0:c0 + w].astype(F32))
            res = fn(g, *row_t, *par_t)
            for o_ref, val, (_, _, spec) in zip(out_refs, res, outs):
                c0, w = _cols(spec, g)
                o_ref[:, c0:c0 + w] = val.astype(o_ref.dtype)

    return pl.pallas_call(
        body, name=name, grid=(s // ts,),
        in_specs=[pl.BlockSpec((ts, a.shape[1]), lambda i: (i, 0)) for a, _ in rows]
        + [pl.BlockSpec(p.shape, lambda i: (0, 0)) for p in pars] + [_TIE_SPEC1] * len(ties),
        out_specs=[pl.BlockSpec((ts, w), lambda i: (i, 0)) for w, _, _ in outs],
        out_shape=[jax.ShapeDtypeStruct((s, w), dt) for w, dt, _ in outs],
        compiler_params=_params(("parallel",)),
    )(*[a for a, _ in rows], *pars, *ties)


def _rowwise_bwd(fn, rows, pars, outs, douts, gmap, gshapes, name, groups=1, add=None, par_grads=True,
                 ts=ROW_TILE):
    s = rows[0][0].shape[0]
    ts = min(ts, s)
    assert s % ts == 0
    nr, npar, nout, ng = len(rows), len(pars), len(outs), len(gshapes)
    add = add or {}
    add_keys = sorted(add)

    def body(*refs):
        row_refs = refs[:nr]
        par_refs = refs[nr:nr + npar]
        dout_refs = refs[nr + npar:nr + npar + nout]
        add_refs = refs[nr + npar + nout:nr + npar + nout + len(add_keys)]
        g_refs = refs[nr + npar + nout + len(add_keys):][:ng]
        pg_refs = refs[nr + npar + nout + len(add_keys) + ng:]
        par_t = [r[...] for r in par_refs]
        par_acc = [None] * npar
        shared_acc = {}
        for g in range(groups):
            row_t = []
            for r, (_, spec) in zip(row_refs, rows):
                c0, w = _cols(spec, g)
                row_t.append(r[:, c0:c0 + w].astype(F32))
            cts = []
            for r, (_, _, spec) in zip(dout_refs, outs):
                c0, w = _cols(spec, g)
                cts.append(r[:, c0:c0 + w].astype(F32))
            _, vjp = jax.vjp(lambda *t, g=g: tuple(fn(g, *t)), *row_t, *par_t)
            grads = vjp(tuple(cts))
            for k in range(nr):
                if gmap[k] is None:
                    continue
                gi, spec = gmap[k]
                if isinstance(spec, list) or groups == 1:
                    c0, w = _cols(spec, g)
                    val = grads[k]
                    if gi in add:
                        val = val + add_refs[add_keys.index(gi)][:, c0:c0 + w].astype(F32)
                    g_refs[gi][:, c0:c0 + w] = val.astype(g_refs[gi].dtype)
                else:
                    shared_acc[k] = grads[k] if k not in shared_acc else shared_acc[k] + grads[k]
            if par_grads:
                for k in range(npar):
                    pg = grads[nr + k]
                    par_acc[k] = pg if par_acc[k] is None else par_acc[k] + pg
        for k, val in shared_acc.items():
            gi, (c0, w) = gmap[k]
            assert gi not in add
            g_refs[gi][:, c0:c0 + w] = val.astype(g_refs[gi].dtype)
        if par_grads:
            first = pl.program_id(0) == 0
            for k in range(npar):
                @pl.when(first)
                def _(k=k):
                    pg_refs[k][...] = par_acc[k]

                @pl.when(jnp.logical_not(first))
                def _(k=k):
                    pg_refs[k][...] += par_acc[k]

    out_specs = [pl.BlockSpec((ts, w), lambda i: (i, 0)) for w, _ in gshapes]
    out_shape = [jax.ShapeDtypeStruct((s, w), dt) for w, dt in gshapes]
    if par_grads:
        out_specs += [pl.BlockSpec(p.shape, lambda i: (0, 0)) for p in pars]
        out_shape += [jax.ShapeDtypeStruct(p.shape, F32) for p in pars]
    return pl.pallas_call(
        body, name=name, grid=(s // ts,),
        in_specs=[pl.BlockSpec((ts, a.shape[1]), lambda i: (i, 0)) for a, _ in rows]
        + [pl.BlockSpec(p.shape, lambda i: (0, 0)) for p in pars]
        + [pl.BlockSpec((ts, a.shape[1]), lambda i: (i, 0)) for a in douts]
        + [pl.BlockSpec((ts, add[k].shape[1]), lambda i: (i, 0)) for k in add_keys],
        out_specs=out_specs, out_shape=out_shape,
        compiler_params=_params(("arbitrary",)),
    )(*[a for a, _ in rows], *pars, *douts, *[add[k] for k in add_keys])


def _sigmoid(x):
    return 1.0 / (1.0 + jnp.exp(-x))


def _silu(x):
    return x * _sigmoid(x)


def _softplus(x):
    return jnp.maximum(x, 0.0) + jnp.log(1.0 + jnp.exp(-jnp.abs(x)))


def _rms(t, g, n=None):
    n = n or t.shape[-1]
    return t * lax.rsqrt(jnp.sum(t * t, axis=-1, keepdims=True) / n + EPS) * g


def _modulate_fn(g, x, gain, scale, shift):
    return (_rms(x, gain) * (1.0 + scale) + shift,)


def _resgate_fn(g, x, y, gm):
    return (x + gm * y,)


def _gate_only_fn(g, y, gm):
    return (gm * y,)


def _gdn_gates_fn(g, b_logit, a_logit, a_log, dt_bias):
    gate = -jnp.exp(a_log) * _softplus(a_logit + dt_bias)
    n = gate.shape[0]
    i = lax.broadcasted_iota(jnp.int32, (n, n), 0)
    j = lax.broadcasted_iota(jnp.int32, (n, n), 1)
    tri = (((i // CHUNK) == (j // CHUNK)) & (i >= j)).astype(F32)
    gcum = lax.dot_general(tri, gate, _NN, preferred_element_type=F32, precision=lax.Precision.HIGHEST)
    return _sigmoid(b_logit), gcum


def _gdn_outnorm_fn(g, o, z, gain):
    return (_rms(o, gain) * _silu(z),)


def _rms_fn(g, t, gain):
    return (_rms(t, gain),)


@jax.custom_vjp
def _swap_halves(t):
    return pltpu.roll(t, 32, 1)


_swap_halves.defvjp(lambda t: (pltpu.roll(t, 32, 1), None), lambda _, ct: (pltpu.roll(ct, 96, 1),))


def _head_norm_rope_fn(g, nope, rope, cosf, sins, gain_n, gain_r):
    first = lax.broadcasted_iota(jnp.int32, rope.shape, 1) < ROPE
    ss = jnp.sum(nope * nope, axis=-1, keepdims=True) + jnp.sum(jnp.where(first, rope * rope, 0.0), axis=-1,
                                                                 keepdims=True)
    r = lax.rsqrt(ss / QK_HEAD + EPS)
    tn = nope * r * gain_n
    tr = rope * r * gain_r
    rot = jnp.where(first, tr * cosf + _swap_halves(tr) * sins, 0.0)
    return tn, rot


def _q_norm_rope_fn(g, nope, rope, cosf, sins, gain_n, gain_r):
    tn, rot = _head_norm_rope_fn(g, nope, rope, cosf, sins, gain_n, gain_r)
    return (jnp.concatenate([tn, rot], axis=1),)


def _k_norm_rope_fn(g, nope, val, rope, cosf, sins, gain_n, gain_r):
    tn, rot = _head_norm_rope_fn(g, nope, rope, cosf, sins, gain_n, gain_r)
    return jnp.concatenate([tn, rot], axis=1), val


def _loss_fn(g, y, target):
    e = y - target
    return (jnp.sum(e * e, axis=-1, keepdims=True) * (0.5 / D) * jnp.ones((1, 128), F32),)


FF_SH = 2 * D_FF // N_DEV
FF_G = N_DEV // 2


def _ffn_in(h, w_in, name, tm=512, ties=()):
    s = h.shape[0]
    tm = min(tm, s)

    def body(h_ref, wg_ref, wu_ref, *rest):
        g_ref, u_ref, a_ref = rest[-3:]
        hb = h_ref[...]
        gate = jnp.dot(hb, wg_ref[...], preferred_element_type=F32)
        up = jnp.dot(hb, wu_ref[...], preferred_element_type=F32)
        g_ref[...] = gate.astype(BF16)
        u_ref[...] = up.astype(BF16)
        a_ref[...] = (_silu(gate) * up).astype(BF16)

    spec = pl.BlockSpec((None, tm, FF_SH), lambda j, i: (j, i, 0))
    return pl.pallas_call(
        body, name=name, grid=(FF_G, s // tm),
        in_specs=[pl.BlockSpec((tm, D), lambda j, i: (i, 0)), pl.BlockSpec((None, D, FF_SH), lambda j, i: (j, 0, 0)),
                  pl.BlockSpec((None, D, FF_SH), lambda j, i: (j + FF_G, 0, 0))] + [_TIE_SPEC2] * len(ties),
        out_specs=[spec, spec, spec], out_shape=[jax.ShapeDtypeStruct((FF_G, s, FF_SH), BF16)] * 3,
        compiler_params=_params(("parallel", "parallel")),
    )(h, w_in, w_in, *ties)


def _ffn_out(act, wo, resid, name, tm=512):
    s = act.shape[1]
    tm = min(tm, s)
    nres = len(resid)

    def body(a_ref, b_ref, x_ref, gate_ref, *rest):
        mods, outs, acc = rest[:nres - 2], rest[nres - 2:-1], rest[-1]
        k = pl.program_id(1)

        @pl.when(k == 0)
        def _():
            acc[...] = jnp.zeros_like(acc)

        acc[...] += jnp.dot(a_ref[...], b_ref[...], preferred_element_type=F32)

        @pl.when(k == FF_G - 1)
        def _():
            y = acc[...]
            xn = x_ref[...] + gate_ref[...] * y
            outs[0][...] = y.astype(BF16)
            outs[1][...] = xn
            if mods:
                outs[2][...] = _modulate_fn(0, xn, *[m[...] for m in mods])[0].astype(BF16)

    blk = pl.BlockSpec((tm, D), lambda i, k: (i, 0))
    vec = pl.BlockSpec((1, D), lambda i, k: (0, 0))
    return pl.pallas_call(
        body, name=name, grid=(s // tm, FF_G),
        in_specs=[pl.BlockSpec((None, tm, FF_SH), lambda i, k: (k, i, 0)), pl.BlockSpec((FF_SH, D), lambda i, k: (k, 0)),
                  blk] + [vec] * (nres - 1),
        out_specs=[blk] * (2 + (nres == 5)),
        out_shape=[jax.ShapeDtypeStruct((s, D), BF16), jax.ShapeDtypeStruct((s, D), F32)]
        + [jax.ShapeDtypeStruct((s, D), BF16)] * (nres == 5),
        scratch_shapes=[pltpu.VMEM((tm, D), F32)], compiler_params=_params(("parallel", "arbitrary")),
    )(act, wo, *resid)


def _ffn_bwd_act(dy, wo, gate, up, name, tm=512, ties=()):
    s = dy.shape[0]
    tm = min(tm, s)

    def body(dy_ref, wo_ref, g_ref, u_ref, *rest):
        dg_ref, du_ref = rest[-2:]
        dact = lax.dot_general(dy_ref[...], wo_ref[...], _NT, preferred_element_type=F32)
        gate = g_ref[...].astype(F32)
        up = u_ref[...].astype(F32)
        sg = _sigmoid(gate)
        dg_ref[...] = (dact * up * (sg * (1.0 + gate * (1.0 - sg)))).astype(BF16)
        du_ref[...] = (dact * (gate * sg)).astype(BF16)

    spec = pl.BlockSpec((None, tm, FF_SH), lambda j, i: (j, i, 0))
    return pl.pallas_call(
        body, name=name, grid=(FF_G, s // tm),
        in_specs=[pl.BlockSpec((tm, D), lambda j, i: (i, 0)), pl.BlockSpec((FF_SH, D), lambda j, i: (j, 0)), spec, spec]
        + [_TIE_SPEC2] * len(ties),
        out_specs=[spec, spec], out_shape=[jax.ShapeDtypeStruct((FF_G, s, FF_SH), BF16)] * 2,
        compiler_params=_params(("parallel", "parallel")),
    )(dy, wo, gate, up, *ties)


def _ffn_dwo(act, dy, name, tk=1024, ties=()):
    s = act.shape[1]
    tk = min(tk, s)

    def body(a_ref, b_ref, *rest):
        o_ref, acc = rest[-2:]
        k = pl.program_id(1)

        @pl.when(k == 0)
        def _():
            acc[...] = jnp.zeros_like(acc)

        acc[...] += lax.dot_general(a_ref[...], b_ref[...], _TN, preferred_element_type=F32)

        @pl.when(k == s // tk - 1)
        def _():
            o_ref[...] = acc[...].astype(BF16)

    return pl.pallas_call(
        body, name=name, grid=(FF_G, s // tk),
        in_specs=[pl.BlockSpec((None, tk, FF_SH), lambda j, k: (j, k, 0)), pl.BlockSpec((tk, D), lambda j, k: (k, 0))]
        + [_TIE_SPEC2] * len(ties),
        out_specs=pl.BlockSpec((FF_SH, D), lambda j, k: (j, 0)), out_shape=jax.ShapeDtypeStruct((D_FF, D), BF16),
        scratch_shapes=[pltpu.VMEM((FF_SH, D), F32)], compiler_params=_params(("parallel", "arbitrary")),
    )(act, dy, *ties)


def _ffn_halves(k, gate_ref, up_ref, fn):
    pl.when(k < FF_G)(functools.partial(fn, gate_ref))
    pl.when(k >= FF_G)(functools.partial(fn, up_ref))


def _ffn_dh(dgate, dup, w_in, name, tm=1024):
    s = dgate.shape[1]
    tm = min(tm, s)

    def body(dg_ref, du_ref, w_ref, o_ref, acc):
        k = pl.program_id(1)

        @pl.when(k == 0)
        def _():
            acc[...] = jnp.zeros_like(acc)

        def add(d_ref):
            acc[...] += lax.dot_general(d_ref[...], w_ref[...], _NT, preferred_element_type=F32)

        _ffn_halves(k, dg_ref, du_ref, add)

        @pl.when(k == N_DEV - 1)
        def _():
            o_ref[...] = acc[...]

    return pl.pallas_call(
        body, name=name, grid=(s // tm, N_DEV),
        in_specs=[pl.BlockSpec((None, tm, FF_SH), lambda i, k: (jnp.minimum(k, FF_G - 1), i, 0)),
                  pl.BlockSpec((None, tm, FF_SH), lambda i, k: (jnp.maximum(k - FF_G, 0), i, 0)),
                  pl.BlockSpec((None, D, FF_SH), lambda i, k: (k, 0, 0))],
        out_specs=pl.BlockSpec((tm, D), lambda i, k: (i, 0)), out_shape=jax.ShapeDtypeStruct((s, D), F32),
        scratch_shapes=[pltpu.VMEM((tm, D), F32)], compiler_params=_params(("parallel", "arbitrary")),
    )(dgate, dup, w_in)


def _ffn_dwi(h, dgate, dup, name, tk=1024):
    s = h.shape[0]
    tk = min(tk, s)

    def body(h_ref, dg_ref, du_ref, o_ref, acc):
        j, k = pl.program_id(0), pl.program_id(1)

        @pl.when(k == 0)
        def _():
            acc[...] = jnp.zeros_like(acc)

        def add(d_ref):
            acc[...] += lax.dot_general(h_ref[...], d_ref[...], _TN, preferred_element_type=F32)

        _ffn_halves(j, dg_ref, du_ref, add)

        @pl.when(k == s // tk - 1)
        def _():
            o_ref[...] = acc[...].astype(BF16)

    return pl.pallas_call(
        body, name=name, grid=(N_DEV, s // tk),
        in_specs=[pl.BlockSpec((tk, D), lambda j, k: (k, 0)),
                  pl.BlockSpec((None, tk, FF_SH), lambda j, k: (jnp.minimum(j, FF_G - 1), jnp.where(j < FF_G, k, s // tk - 1), 0)),
                  pl.BlockSpec((None, tk, FF_SH), lambda j, k: (jnp.maximum(j - FF_G, 0), jnp.where(j < FF_G, 0, k), 0))],
        out_specs=pl.BlockSpec((None, D, FF_SH), lambda j, k: (j, 0, 0)),
        out_shape=jax.ShapeDtypeStruct((N_DEV, D, FF_SH), BF16),
        scratch_shapes=[pltpu.VMEM((D, FF_SH), F32)], compiler_params=_params(("parallel", "arbitrary")),
    )(h, dgate, dup)


def _shift_down(x, d):
    rows = lax.broadcasted_iota(jnp.int32, x.shape, 0)
    return jnp.where(rows >= d, pltpu.roll(x, d, 0), 0.0)


def _shift_up(x, d):
    n = x.shape[0]
    rows = lax.broadcasted_iota(jnp.int32, x.shape, 0)
    return jnp.where(rows < n - d, pltpu.roll(x, n - d, 0), 0.0)


def _conv_post(pre, is_qk):
    a = _silu(pre)
    l2 = a * lax.rsqrt(jnp.sum(a * a, axis=-1, keepdims=True) + EPS)
    return jnp.where(is_qk, l2, a)


def _conv_pre(x, w):
    pre = x * w[CONV_K - 1:CONV_K, :]
    for j in range(CONV_K - 1):
        pre = pre + _shift_down(x, CONV_K - 1 - j) * w[j:j + 1, :]
    return pre


def _gdn_conv_fwd(pm, conv_w, name):
    s = pm.shape[0]
    nblk = 3 * D // HEAD

    def body(x_ref, w_ref, o_ref):
        is_qk = pl.program_id(0) < 2 * HEADS
        o_ref[...] = _conv_post(_conv_pre(x_ref[...], w_ref[...]), is_qk)

    return pl.pallas_call(
        body, name=name, grid=(nblk,),
        in_specs=[pl.BlockSpec((s, HEAD), lambda c: (0, c)), pl.BlockSpec((CONV_K, HEAD), lambda c: (0, c))],
        out_specs=pl.BlockSpec((s, HEAD), lambda c: (0, c)),
        out_shape=jax.ShapeDtypeStruct((s, 3 * D), F32), compiler_params=_params(("parallel",)),
    )(pm, conv_w)


def _gdn_conv_bwd(pm, conv_w, dout, part, name):
    s = pm.shape[0]
    off = part * HEADS

    def body(x_ref, w_ref, d_ref, dx_ref, dw_ref):
        x, w = x_ref[...], w_ref[...]
        _, vjp = jax.vjp(lambda p: _conv_post(p, part < 2), _conv_pre(x, w))
        dpre, = vjp(d_ref[...])
        dx = dpre * w[CONV_K - 1:CONV_K, :]
        rows = [None] * CONV_K
        rows[CONV_K - 1] = jnp.sum(dpre * x, axis=0, keepdims=True)
        for j in range(CONV_K - 1):
            dx = dx + _shift_up(dpre, CONV_K - 1 - j) * w[j:j + 1, :]
            rows[j] = jnp.sum(dpre * _shift_down(x, CONV_K - 1 - j), axis=0, keepdims=True)
        dx_ref[...] = dx.astype(BF16)
        dw_ref[...] = jnp.concatenate(rows, axis=0)

    return pl.pallas_call(
        body, name=name, grid=(HEADS,),
        in_specs=[pl.BlockSpec((s, HEAD), lambda c: (0, c + off)), pl.BlockSpec((CONV_K, HEAD), lambda c: (0, c + off)),
                  pl.BlockSpec((s, HEAD), lambda c: (0, c))],
        out_specs=[pl.BlockSpec((s, HEAD), lambda c: (0, c)), pl.BlockSpec((CONV_K, HEAD), lambda c: (0, c))],
        out_shape=[jax.ShapeDtypeStruct((s, D), BF16), jax.ShapeDtypeStruct((CONV_K, D), F32)],
        compiler_params=_params(("parallel",)),
    )(pm, conv_w, dout)


def _dot3(a, b, dims=_NN):
    ah, bh = a.astype(BF16), b.astype(BF16)
    al, bl = (a - ah.astype(F32)).astype(BF16), (b - bh.astype(F32)).astype(BF16)
    d = lambda u, v: lax.dot_general(u, v, dims, preferred_element_type=F32)
    return d(ah, bh) + (d(ah, bl) + d(al, bh))


def _make_dot(hi):
    def raw(a, b, dims):
        if hi:
            return _dot3(a, b, dims)
        return lax.dot_general(a.astype(BF16), b.astype(BF16), dims, preferred_element_type=F32)

    @functools.partial(jax.custom_vjp, nondiff_argnums=(2,))
    def dot(a, b, form):
        return raw(a, b, _DIMS[form])

    def fwd(a, b, form):
        return raw(a, b, _DIMS[form]), (a, b)

    def bwd(form, res, ct):
        a, b = res
        if form == "nn":
            return raw(ct, b, _NT), raw(a, ct, _TN)
        if form == "nt":
            return raw(ct, b, _NN), raw(ct, a, _TN)
        return raw(b, ct, _NT), raw(a, ct, _NN)

    dot.defvjp(fwd, bwd)
    return dot


_dot = _make_dot(False)
_dot_hi = _make_dot(True)


def _tri_inv_raw(low):
    n = low.shape[0]
    i = lax.broadcasted_iota(jnp.int32, (n, n), 0)
    j = lax.broadcasted_iota(jnp.int32, (n, n), 1)
    eye = (i == j).astype(F32)
    hdot = _dot3
    same16 = (i // 16) == (j // 16)
    neg = jnp.where(same16, -low, 0.0)
    inv = eye + neg
    power = neg
    for _ in range(3):
        power = hdot(power, power)
        inv = hdot(inv, eye + power)
    for blk in (32, 64):
        off = jnp.where(((i // blk) == (j // blk)) & ((i // (blk // 2)) != (j // (blk // 2))), low, 0.0)
        inv = inv - hdot(inv, hdot(off, inv))
    return inv


@jax.custom_vjp
def _tri_inv(low):
    return _tri_inv_raw(low)


def _tri_inv_fwd(low):
    inv = _tri_inv_raw(low)
    return inv, inv


def _tri_inv_bwd(inv, ct):
    return (-_dot3(_dot3(inv, ct, _TN), inv, _NT),)


_tri_inv.defvjp(_tri_inv_fwd, _tri_inv_bwd)


@jax.custom_vjp
def _tri_inv_given(low, inv):
    return inv


_tri_inv_given.defvjp(lambda low, inv: (inv, inv),
                      lambda inv, ct: (_tri_inv_bwd(inv, ct)[0], jnp.zeros_like(inv)))

GROUP = 4
N_GROUPS = HEADS // GROUP
GROWS = GROUP * CHUNK


def _gdn_group(q, k, v, beta, gc, gr, states, inv=None):
    n = q.shape[0]
    i = lax.broadcasted_iota(jnp.int32, (n, n), 0)
    j = lax.broadcasted_iota(jnp.int32, (n, n), 1)
    same = (i // CHUNK) == (j // CHUNK)
    incl, strict = same & (i >= j), same & (i > j)
    qs = q * (HEAD ** -0.5)
    decay = jnp.where(incl, jnp.exp(jnp.where(incl, gc - gr, 0.0)), 0.0)
    kb = k * beta
    eg = jnp.exp(gc)
    prod = _dot(jnp.concatenate([kb, qs], axis=0), k, "nt")
    low = jnp.where(strict, prod[:n] * decay, 0.0)
    attn = jnp.where(incl, prod[n:] * decay, 0.0)
    inv = _tri_inv(low) if inv is None else _tri_inv_given(low, inv)
    sol = _dot_hi(inv, jnp.concatenate([v * beta, kb * eg], axis=1), "nn")
    u, w, qg = sol[:, :HEAD], sol[:, HEAD:], qs * eg
    last = lax.broadcasted_iota(jnp.int32, (CHUNK, 1), 0) == CHUNK - 1
    v_new, o_state, carry = [], [], []
    for h, state in enumerate(states):
        rows = slice(h * CHUNK, (h + 1) * CHUNK)
        ws = _dot(jnp.concatenate([w[rows], qg[rows]], axis=0), state, "nn")
        v_new.append(u[rows] - ws[:CHUNK])
        o_state.append(ws[CHUNK:])
        g_last = jnp.sum(jnp.where(last, gc[rows], 0.0), axis=0, keepdims=True)
        carry.append((g_last, k[rows] * jnp.exp(g_last - gc[rows])))
    o = jnp.concatenate(o_state, axis=0) + _dot(attn, jnp.concatenate(v_new, axis=0), "nn")
    new = tuple(state * jnp.exp(g_last) + _dot(k_dec, vn, "tn")
                for state, (g_last, k_dec), vn in zip(states, carry, v_new))
    return o, new, inv


def _gdn_specs(s, rev):
    nc = s // CHUNK
    at = (lambda n: nc - 1 - n) if rev else (lambda n: n)
    return nc, at, [
        pl.BlockSpec((CHUNK, D), lambda n: (at(n), 0)), pl.BlockSpec((CHUNK, D), lambda n: (at(n), 1)),
        pl.BlockSpec((CHUNK, D), lambda n: (at(n), 2)), pl.BlockSpec((CHUNK, HEAD), lambda n: (at(n), 0)),
        pl.BlockSpec((CHUNK, HEAD), lambda n: (at(n), 0)),
        pl.BlockSpec((None, N_GROUPS, 1, GROWS), lambda n: (at(n), 0, 0, 0))]


def _group_operands(grp, q_ref, k_ref, v_ref, b_blk, gc_blk, gr_blk):
    heads = range(grp * GROUP, (grp + 1) * GROUP)
    stack = lambda ref: jnp.concatenate([ref[:, h * HEAD:(h + 1) * HEAD] for h in heads], axis=0)
    col = lambda blk: jnp.concatenate([blk[:, h:h + 1] for h in heads], axis=0)
    return stack(q_ref), stack(k_ref), stack(v_ref), col(b_blk), col(gc_blk), gr_blk[grp]


def _gdn_scan_fwd(qkv, beta, gcum, grow, name):
    s = qkv.shape[0]
    nc, _, in_specs = _gdn_specs(s, rev=False)

    def body(q_ref, k_ref, v_ref, b_ref, gc_ref, gr_ref, o_ref, st_ref, inv_ref, state):
        @pl.when(pl.program_id(0) == 0)
        def _():
            state[...] = jnp.zeros_like(state)

        b_blk, gc_blk, gr_blk = b_ref[...], gc_ref[...], gr_ref[...]
        old = [state[h] for h in range(HEADS)]
        res = [_gdn_group(*_group_operands(grp, q_ref, k_ref, v_ref, b_blk, gc_blk, gr_blk),
                          old[grp * GROUP:(grp + 1) * GROUP]) for grp in range(N_GROUPS)]
        for grp, (o, new, inv) in enumerate(res):
            inv_ref[grp] = inv
            for hh in range(GROUP):
                h = grp * GROUP + hh
                st_ref[h] = old[h]
                o_ref[:, h * HEAD:(h + 1) * HEAD] = o[hh * CHUNK:(hh + 1) * CHUNK]
                state[h] = new[hh]

    return pl.pallas_call(
        body, name=name, grid=(nc,), in_specs=in_specs,
        out_specs=[pl.BlockSpec((CHUNK, D), lambda n: (n, 0)),
                   pl.BlockSpec((None, HEADS, HEAD, HEAD), lambda n: (n, 0, 0, 0)),
                   pl.BlockSpec((None, N_GROUPS, GROWS, GROWS), lambda n: (n, 0, 0, 0))],
        out_shape=[jax.ShapeDtypeStruct((s, D), F32), jax.ShapeDtypeStruct((nc, HEADS, HEAD, HEAD), F32),
                   jax.ShapeDtypeStruct((nc, N_GROUPS, GROWS, GROWS), F32)],
        scratch_shapes=[pltpu.VMEM((HEADS, HEAD, HEAD), F32)],
        compiler_params=_params(("arbitrary",)),
    )(qkv, qkv, qkv, beta, gcum, grow)


def _gdn_scan_bwd(qkv, beta, gcum, grow, states, invs, do, name):
    s = qkv.shape[0]
    nc, at, in_specs = _gdn_specs(s, rev=True)
    in_specs += [pl.BlockSpec((None, HEADS, HEAD, HEAD), lambda n: (at(n), 0, 0, 0)),
                 pl.BlockSpec((None, N_GROUPS, GROWS, GROWS), lambda n: (at(n), 0, 0, 0)),
                 pl.BlockSpec((CHUNK, D), lambda n: (at(n), 0))]

    def body(q_ref, k_ref, v_ref, b_ref, gc_ref, gr_ref, st_ref, inv_ref, do_ref, dq_ref, dk_ref, dv_ref, db_ref,
             dgc_ref, dgr_ref, dstate):
        @pl.when(pl.program_id(0) == 0)
        def _():
            dstate[...] = jnp.zeros_like(dstate)

        b_blk, gc_blk, gr_blk = b_ref[...], gc_ref[...], gr_ref[...]
        dold = [dstate[h] for h in range(HEADS)]
        res = []
        for grp in range(N_GROUPS):
            heads = range(grp * GROUP, (grp + 1) * GROUP)
            inv = inv_ref[grp]
            _, vjp = jax.vjp(lambda q, k, v, b, gc, gr, *st, inv=inv: _gdn_group(q, k, v, b, gc, gr, st, inv)[:2],
                             *_group_operands(grp, q_ref, k_ref, v_ref, b_blk, gc_blk, gr_blk),
                             *[st_ref[h] for h in heads])
            d_out = jnp.concatenate([do_ref[:, h * HEAD:(h + 1) * HEAD] for h in heads], axis=0)
            res.append(vjp((d_out, tuple(dold[h] for h in heads))))
        lane = lax.broadcasted_iota(jnp.int32, (CHUNK, HEAD), 1)
        db_all = jnp.zeros((CHUNK, HEAD), F32)
        dgc_all = jnp.zeros((CHUNK, HEAD), F32)
        for grp, (dq, dk, dv, db, dgc, dgr, *dst) in enumerate(res):
            dgr_ref[grp] = dgr
            for hh in range(GROUP):
                h = grp * GROUP + hh
                cs, rows = slice(h * HEAD, (h + 1) * HEAD), slice(hh * CHUNK, (hh + 1) * CHUNK)
                dq_ref[:, cs] = dq[rows]
                dk_ref[:, cs] = dk[rows]
                dv_ref[:, cs] = dv[rows]
                dstate[h] = dst[hh]
                db_all = jnp.where(lane == h, db[rows], db_all)
                dgc_all = jnp.where(lane == h, dgc[rows], dgc_all)
        db_ref[...] = db_all
        dgc_ref[...] = dgc_all

    blk = pl.BlockSpec((CHUNK, D), lambda n: (at(n), 0))
    gblk = pl.BlockSpec((CHUNK, HEAD), lambda n: (at(n), 0))
    return pl.pallas_call(
        body, name=name, grid=(nc,), in_specs=in_specs,
        out_specs=[blk, blk, blk, gblk, gblk, pl.BlockSpec((None, N_GROUPS, 1, GROWS), lambda n: (at(n), 0, 0, 0))],
        out_shape=[jax.ShapeDtypeStruct((s, D), F32)] * 3 + [jax.ShapeDtypeStruct((s, HEAD), F32)] * 2
        + [jax.ShapeDtypeStruct((nc, N_GROUPS, 1, GROWS), F32)],
        scratch_shapes=[pltpu.VMEM((HEADS, HEAD, HEAD), F32)],
        compiler_params=_params(("arbitrary",)),
    )(qkv, qkv, qkv, beta, gcum, grow, states, invs, do)


ATT_TILE = 512
ATT_SCALE = QK_HEAD ** -0.5


def _att_mask(t):
    qpos = lax.broadcasted_iota(jnp.int32, (t, t), 0)
    kpos = lax.broadcasted_iota(jnp.int32, (t, t), 1)
    return (kpos // CHUNK) <= (qpos // CHUNK)


ATT_STRIP = 32


def _att_strip_mask(r, t):
    kpos = lax.broadcasted_iota(jnp.int32, (ATT_STRIP, t), 1)
    return (kpos // CHUNK) <= (r * ATT_STRIP) // CHUNK


def _att_pairs(nb, by_query):
    if by_query:
        pairs = [(i, j) for i in range(nb) for j in range(i + 1)]
    else:
        pairs = [(j, i) for j in range(nb) for i in range(j, nb)]
    return jnp.array([a for a, _ in pairs], jnp.int32), jnp.array([b for _, b in pairs], jnp.int32)


def _attn_fwd(q, k, v, name):
    s = q.shape[0]
    t = min(ATT_TILE, s)
    nb = s // t
    ii, jj = _att_pairs(nb, by_query=True)

    def body(ii_ref, jj_ref, q_ref, k_ref, v_ref, o_ref, lse_ref, m_s, l_s, acc):
        step = pl.program_id(1)
        i, j = ii_ref[step], jj_ref[step]

        @pl.when(j == 0)
        def _():
            m_s[...] = jnp.full_like(m_s, -jnp.inf)
            l_s[...] = jnp.zeros_like(l_s)
            acc[...] = jnp.zeros_like(acc)

        sc = lax.dot_general(q_ref[...], k_ref[...], _NT, preferred_element_type=F32) * ATT_SCALE
        sc = lax.cond(i == j, lambda u: jnp.where(_att_mask(t), u, -jnp.inf), lambda u: u, sc)
        m_new = jnp.maximum(m_s[...], jnp.max(sc, axis=-1, keepdims=True))
        alpha = jnp.exp(m_s[...] - m_new)
        p = jnp.exp(sc - m_new)
        l_s[...] = alpha * l_s[...] + jnp.sum(p, axis=-1, keepdims=True)
        acc[...] = alpha * acc[...] + jnp.dot(p.astype(BF16), v_ref[...], preferred_element_type=F32)
        m_s[...] = m_new

        @pl.when(j == i)
        def _():
            o_ref[...] = acc[...] / l_s[...]
            lse_ref[...] = m_s[...] + jnp.log(l_s[...])

    grid_spec = pltpu.PrefetchScalarGridSpec(
        num_scalar_prefetch=2, grid=(HEADS, len(ii)),
        in_specs=[pl.BlockSpec((t, HEAD_PAD), lambda h, n, ir, jr: (ir[n], h)),
                  pl.BlockSpec((t, HEAD_PAD), lambda h, n, ir, jr: (jr[n], h)),
                  pl.BlockSpec((t, HEAD), lambda h, n, ir, jr: (jr[n], h))],
        out_specs=[pl.BlockSpec((t, HEAD), lambda h, n, ir, jr: (ir[n], h)),
                   pl.BlockSpec((None, t, 1), lambda h, n, ir, jr: (h, ir[n], 0))],
        scratch_shapes=[pltpu.VMEM((t, 1), F32), pltpu.VMEM((t, 1), F32), pltpu.VMEM((t, HEAD), F32)])
    return pl.pallas_call(
        body, name=name, grid_spec=grid_spec,
        out_shape=[jax.ShapeDtypeStruct((s, HEADS * HEAD), F32), jax.ShapeDtypeStruct((HEADS, s, 1), F32)],
        compiler_params=_params(("parallel", "arbitrary")),
    )(ii, jj, q, k, v)


def _attn_bwd(q, k, v, do, o, lse, name, dkv_sum=None):
    s = q.shape[0]
    t = min(ATT_TILE, s)
    nb = s // t
    jj, ii = _att_pairs(nb, by_query=False)
    nsum = 2 if dkv_sum else 0

    def body(jj_ref, ii_ref, q_ref, k_ref, v_ref, do_ref, o_ref, lse_ref, *rest):
        dq_ref, dk_ref, dv_ref, dk_acc, dv_acc, sc_s, dp_s, p_s, ds_s, dl_s = rest[nsum:]
        step = pl.program_id(1)
        i, j = ii_ref[step], jj_ref[step]

        @pl.when(step == 0)
        def _():
            dq_ref[...] = jnp.zeros_like(dq_ref)

        @pl.when(i == j)
        def _():
            dk_acc[...] = jnp.zeros_like(dk_acc)
            dv_acc[...] = jnp.zeros_like(dv_acc)

        do_f = do_ref[...]
        dob = do_f.astype(BF16)
        dl_s[...] = jnp.sum(do_f * o_ref[...], axis=-1, keepdims=True)
        sc_s[...] = lax.dot_general(q_ref[...], k_ref[...], _NT, preferred_element_type=F32)
        dp_s[...] = lax.dot_general(dob, v_ref[...], _NT, preferred_element_type=F32)

        def softmax_strips(diagonal):
            for r in range(t // ATT_STRIP):
                rows = slice(r * ATT_STRIP, (r + 1) * ATT_STRIP)
                p = jnp.exp(sc_s[rows, :] * ATT_SCALE - lse_ref[rows, :])
                if diagonal:
                    p = jnp.where(_att_strip_mask(r, t), p, 0.0)
                p_s[rows, :] = p.astype(BF16)
                ds_s[rows, :] = (p * (dp_s[rows, :] - dl_s[rows, :]) * ATT_SCALE).astype(BF16)

        pl.when(i == j)(functools.partial(softmax_strips, True))
        pl.when(i != j)(functools.partial(softmax_strips, False))
        ds = ds_s[...]
        dv_acc[...] += lax.dot_general(p_s[...], dob, _TN, preferred_element_type=F32)
        dk_acc[...] += lax.dot_general(ds, q_ref[...], _TN, preferred_element_type=F32)
        rows = pl.ds(pl.multiple_of(i * t, t), t)
        dq_ref[rows, :] += jnp.dot(ds, k_ref[...], preferred_element_type=F32)

        @pl.when(i == nb - 1)
        def _():
            dk_ref[...] = dk_acc[...] + rest[0][...] if nsum else dk_acc[...]
            dv_ref[...] = dv_acc[...] + rest[1][...] if nsum else dv_acc[...]

    dk_blk = pl.BlockSpec((t, HEAD_PAD), lambda h, n, jr, ir: (jr[n], h))
    dv_blk = pl.BlockSpec((t, HEAD), lambda h, n, jr, ir: (jr[n], h))
    grid_spec = pltpu.PrefetchScalarGridSpec(
        num_scalar_prefetch=2, grid=(HEADS, len(jj)),
        in_specs=[pl.BlockSpec((t, HEAD_PAD), lambda h, n, jr, ir: (ir[n], h)),
                  pl.BlockSpec((t, HEAD_PAD), lambda h, n, jr, ir: (jr[n], h)),
                  pl.BlockSpec((t, HEAD), lambda h, n, jr, ir: (jr[n], h)),
                  pl.BlockSpec((t, HEAD), lambda h, n, jr, ir: (ir[n], h)),
                  pl.BlockSpec((t, HEAD), lambda h, n, jr, ir: (ir[n], h)),
                  pl.BlockSpec((None, t, 1), lambda h, n, jr, ir: (h, ir[n], 0))] + [dk_blk, dv_blk][:nsum],
        out_specs=[pl.BlockSpec((s, HEAD_PAD), lambda h, n, jr, ir: (0, h)), dk_blk, dv_blk],
        scratch_shapes=[pltpu.VMEM((t, HEAD_PAD), F32), pltpu.VMEM((t, HEAD), F32), pltpu.VMEM((t, t), F32),
                        pltpu.VMEM((t, t), F32), pltpu.VMEM((t, t), BF16), pltpu.VMEM((t, t), BF16),
                        pltpu.VMEM((t, 1), F32)])
    return pl.pallas_call(
        body, name=name, grid_spec=grid_spec,
        out_shape=[jax.ShapeDtypeStruct((s, HEADS * HEAD_PAD), F32)] * 2 + [jax.ShapeDtypeStruct((s, HEADS * HEAD), F32)],
        compiler_params=_params(("parallel", "arbitrary")),
    )(jj, ii, q, k, v, do, o, lse, *(dkv_sum or ()))


def _rope_tables(positions):
    half = ROPE // 2
    inv_freq = ROPE_BASE ** (-jnp.arange(half, dtype=F32) / half)
    ang = positions.astype(F32)[:, None] * inv_freq
    cos, sin = jnp.cos(ang), jnp.sin(ang)
    return jnp.concatenate([cos] * 4, axis=1), jnp.concatenate([-sin, sin] * 2, axis=1)


def _loss_and_grad(y, target, name):
    s = y.shape[0]
    ts = min(ROW_TILE, s)

    def body(y_ref, t_ref, dy_ref, l_ref):
        e = y_ref[...] - t_ref[...]
        dy_ref[...] = e * (1.0 / D)
        part = jnp.sum(jnp.sum(e * e, axis=-1, keepdims=True) * (0.5 / D), axis=0, keepdims=True)
        part = part * jnp.ones((1, 128), F32)

        @pl.when(pl.program_id(0) == 0)
        def _():
            l_ref[...] = part

        @pl.when(pl.program_id(0) > 0)
        def _():
            l_ref[...] += part

    return pl.pallas_call(
        body, name=name, grid=(s // ts,),
        in_specs=[pl.BlockSpec((ts, D), lambda i: (i, 0))] * 2,
        out_specs=[pl.BlockSpec((ts, D), lambda i: (i, 0)), pl.BlockSpec((1, 128), lambda i: (0, 0))],
        out_shape=[jax.ShapeDtypeStruct((s, D), F32), jax.ShapeDtypeStruct((1, 128), F32)],
        compiler_params=_params(("arbitrary",)),
    )(y, target)


ANY = pl.BlockSpec(memory_space=pl.ANY)


def _all_gather(shard, name):
    def body(x_ref, out_ref, send_sems, recv_sems, local_sem):
        x, y, c = lax.axis_index("x"), lax.axis_index("y"), lax.axis_index("c")
        me, sibling = (x, y, c), (x, y, 1 - c)
        chips = [(1 - x, y), (x, 1 - y), (1 - x, 1 - y)]

        def rows(px, py, pc):
            return out_ref.at[4 * px + 2 * py + pc]

        def copy(k, block, to, src=None):
            return pltpu.make_async_remote_copy(
                src_ref=rows(*block) if src is None else src, dst_ref=rows(*block),
                send_sem=send_sems.at[k], recv_sem=recv_sems.at[k], device_id=to, device_id_type=MESH)

        mine = pltpu.make_async_copy(x_ref, rows(*me), local_sem)
        mine.start()
        first = [copy(0, me, sibling, src=x_ref)]
        first += [copy(1 + j, me, (*chip, c), src=x_ref) for j, chip in enumerate(chips)]
        for cp in first:
            cp.start()
        passed = [copy(4 + j, (*chip, c), sibling) for j, chip in enumerate(chips)]
        for j, chip in enumerate(chips):
            copy(1 + j, (*chip, c), me).wait_recv()
            passed[j].start()
        copy(0, sibling, me).wait_recv()
        for j, chip in enumerate(chips):
            copy(4 + j, (*chip, 1 - c), me).wait_recv()
        for cp in first + passed:
            cp.wait_send()
        mine.wait()

    return pl.pallas_call(
        body, name=name, out_shape=jax.ShapeDtypeStruct((N_DEV,) + shard.shape, shard.dtype),
        in_specs=[ANY], out_specs=ANY,
        scratch_shapes=[pltpu.SemaphoreType.DMA((7,)), pltpu.SemaphoreType.DMA((7,)), pltpu.SemaphoreType.DMA],
    )(shard)


def _exchange(blocks, name):
    def body(x_ref, out_ref, send_sems, recv_sems, local_sem):
        x, y, c = lax.axis_index("x"), lax.axis_index("y"), lax.axis_index("c")
        me = 4 * x + 2 * y + c
        mine = pltpu.make_async_copy(x_ref.at[me], out_ref.at[me], local_sem)
        mine.start()
        copies = []
        for k in range(1, N_DEV):
            px = 1 - x if k & 4 else x
            py = 1 - y if k & 2 else y
            pc = 1 - c if k & 1 else c
            peer = 4 * px + 2 * py + pc
            cp = pltpu.make_async_remote_copy(
                src_ref=x_ref.at[peer], dst_ref=out_ref.at[me], send_sem=send_sems.at[k - 1],
                recv_sem=recv_sems.at[k - 1], device_id=(px, py, pc), device_id_type=MESH)
            cp.start()
            copies.append((cp, pltpu.make_async_remote_copy(
                src_ref=x_ref.at[peer], dst_ref=out_ref.at[peer], send_sem=send_sems.at[k - 1],
                recv_sem=recv_sems.at[k - 1], device_id=(px, py, pc), device_id_type=MESH)))
        for cp, landing in copies:
            landing.wait_recv()
        for cp, landing in copies:
            cp.wait_send()
        mine.wait()

    return pl.pallas_call(
        body, name=name, out_shape=jax.ShapeDtypeStruct(blocks.shape, blocks.dtype),
        in_specs=[ANY], out_specs=ANY,
        scratch_shapes=[pltpu.SemaphoreType.DMA((7,)), pltpu.SemaphoreType.DMA((7,)), pltpu.SemaphoreType.DMA],
    )(blocks)


HBM = pl.BlockSpec(memory_space=pltpu.HBM)
SEM = pl.BlockSpec(memory_space=pltpu.SEMAPHORE)
EFFECT = pltpu.SideEffectType.DATAFLOW_SIDE_EFFECTING


def _peers():
    x, y, c = lax.axis_index("x"), lax.axis_index("y"), lax.axis_index("c")
    peers = []
    for k in range(1, N_DEV):
        px = 1 - x if k & 4 else x
        py = 1 - y if k & 2 else y
        pc = 1 - c if k & 1 else c
        peers.append(((px, py, pc), 4 * px + 2 * py + pc))
    return 4 * x + 2 * y + c, peers


def _send_start(srcs, name, gather):
    n = len(srcs)
    lands = [((N_DEV,) + s.shape) if gather else s.shape for s in srcs]

    def body(*refs):
        src_refs, land_refs = refs[:n], refs[n:2 * n]
        send_sems, recv_sems, token = refs[2 * n], refs[2 * n + 1], refs[-1]
        me, peers = _peers()
        for i in range(n):
            for k, (dev, idx) in enumerate(peers):
                pltpu.make_async_remote_copy(
                    src_ref=src_refs[i] if gather else src_refs[i].at[idx], dst_ref=land_refs[i].at[me],
                    send_sem=send_sems.at[7 * i + k], recv_sem=recv_sems.at[7 * i + k], device_id=dev,
                    device_id_type=MESH).start()
        token[...] = jnp.zeros_like(token)

    res = pl.pallas_call(
        body, name=name,
        out_shape=(pltpu.SemaphoreType.DMA((7 * n,)), pltpu.SemaphoreType.DMA((7 * n,)),
                   *[pltpu.HBM(s.shape, s.dtype) for s in srcs],
                   *[pltpu.HBM(shape, s.dtype) for shape, s in zip(lands, srcs)],
                   jax.ShapeDtypeStruct((8, 128), F32)),
        in_specs=(HBM,) * (2 * n), out_specs=(SEM, SEM) + (HBM,) * (2 * n) + (pl.BlockSpec(memory_space=pltpu.VMEM),),
        input_output_aliases={i: 2 + i for i in range(2 * n)},
        compiler_params=pltpu.CompilerParams(has_side_effects=EFFECT),
    )(*[pltpu.with_memory_space_constraint(s, pltpu.HBM) for s in srcs],
      *[pltpu.with_memory_space_constraint(lax.empty(shape, s.dtype), pltpu.HBM) for shape, s in zip(lands, srcs)])
    return dict(sems=res[:2], srcs=res[2:2 + n], lands=res[2 + n:2 + 2 * n], token=res[-1])


def _send_wait(handle, after, name, gather):
    n = len(handle["srcs"])

    def body(*refs):
        src_refs, land_refs = refs[:n], refs[n:2 * n]
        send_sems, recv_sems = refs[2 * n], refs[2 * n + 1]
        me, peers = _peers()
        for i in range(n):
            for k, (dev, idx) in enumerate(peers):
                cp = pltpu.make_async_remote_copy(
                    src_ref=src_refs[i] if gather else src_refs[i].at[idx], dst_ref=land_refs[i].at[idx],
                    send_sem=send_sems.at[7 * i + k], recv_sem=recv_sems.at[7 * i + k], device_id=dev,
                    device_id_type=MESH)
                cp.wait_send()
                cp.wait_recv()

    both = list(handle["srcs"]) + list(handle["lands"])
    res = pl.pallas_call(
        body, name=name, out_shape=tuple(pltpu.HBM(t.shape, t.dtype) for t in both),
        in_specs=(HBM,) * (2 * n) + (SEM, SEM, pl.BlockSpec(memory_space=pl.ANY)), out_specs=(HBM,) * (2 * n),
        input_output_aliases={i: i for i in range(2 * n)},
        compiler_params=pltpu.CompilerParams(has_side_effects=EFFECT),
    )(*both, *handle["sems"], after)
    return res[:n], res[n:]


def _adamw(parts, w, m, v, name, tr=128):
    pieces = len(parts)
    n, r, wd = parts[0].shape
    tr = next((t for t in (tr, 64, 32, 16) if r % t == 0), r)
    nrt = r // tr

    def body(*refs):
        w_ref, m_ref, v_ref, g_ref, d_ref, nm_ref, nv_ref = refs[pieces:]

        def update(p_ref):
            g = p_ref[0].astype(F32)
            for k in range(1, n):
                g = g + p_ref[k].astype(F32)
            m_new = B1 * m_ref[...] + (1.0 - B1) * g
            v_new = B2 * v_ref[...] + (1.0 - B2) * (g * g)
            m_hat = m_new / (1.0 - B1 ** STEP)
            v_hat = v_new / (1.0 - B2 ** STEP)
            g_ref[...] = g
            d_ref[...] = -LR * (m_hat / (jnp.sqrt(v_hat) + ADAM_EPS) + WD * w_ref[...])
            nm_ref[...] = m_new
            nv_ref[...] = v_new

        for p in range(pieces):
            pl.when(pl.program_id(0) == p)(functools.partial(update, refs[p]))

    part_spec = lambda p: pl.BlockSpec((n, tr, wd), lambda l, i: (0, jnp.clip(i + (l - p) * nrt, 0, nrt - 1), 0))
    blk = pl.BlockSpec((tr, wd), lambda l, i: (l * nrt + i, 0))
    return pl.pallas_call(
        body, name=name, grid=(pieces, nrt),
        in_specs=[part_spec(p) for p in range(pieces)] + [blk, blk, blk],
        out_specs=[blk] * 4, out_shape=[jax.ShapeDtypeStruct((pieces * r, wd), F32)] * 4,
        compiler_params=_params(("arbitrary", "arbitrary")),
    )(*parts, w, m, v)


def _outer8(ct, dm, name):
    k, n = ct.shape[0], dm.shape[1]

    def body(c_ref, d_ref, o_ref):
        cv, dv = c_ref[...], d_ref[...]
        acc = cv[:, 0:1] * dv[0:1, :]
        for s in range(1, N_DEV):
            acc = acc + cv[:, s:s + 1] * dv[s:s + 1, :]
        o_ref[...] = acc

    tk = 256
    return pl.pallas_call(
        body, name=name, grid=(k // tk,),
        in_specs=[pl.BlockSpec((tk, N_DEV), lambda i: (i, 0)), pl.BlockSpec((N_DEV, n), lambda i: (0, 0))],
        out_specs=pl.BlockSpec((tk, n), lambda i: (i, 0)), out_shape=jax.ShapeDtypeStruct((k, n), F32),
        compiler_params=_params(("parallel",)),
    )(ct, dm)


FULL = (0, D)
C128 = (0, 128)
HEAD_NOPE = [(h * HEAD_PAD, NOPE) for h in range(HEADS)]
HEAD_ROPE = [(h * HEAD_PAD + NOPE, 128) for h in range(HEADS)]
HEAD_ALL = [(h * HEAD_PAD, HEAD_PAD) for h in range(HEADS)]
HEAD_V = [(h * HEAD, HEAD) for h in range(HEADS)]


def _modulate(x, p, ties=()):
    return _rowwise_fwd(_modulate_fn, [(x, FULL)], [p["gain"], p["scale"], p["shift"]], [(D, BF16, FULL)], "modulate",
                        ties=ties)[0]


def _residual_bwd(y, gm, dxn):
    return _rowwise_bwd(_gate_only_fn, [(y, FULL)], [gm], [(D, F32, FULL)], [dxn], [(0, FULL)], [(D, BF16)],
                        "residual_bwd")


def _modulate_bwd(x, p, dh, dx_in, prev=None):
    pars = [p["gain"], p["scale"], p["shift"]]
    if prev is None:
        return list(_rowwise_bwd(_modulate_fn, [(x, FULL)], pars, [(D, BF16, FULL)], [dh], [(0, FULL)], [(D, F32)],
                                 "modulate_bwd", add={0: dx_in})) + [None]
    s = x.shape[0]
    ts = min(ROW_TILE, s)

    def body(x_ref, g_ref, sc_ref, sh_ref, dh_ref, din_ref, y_ref, gm_ref, dx_ref, dy_ref, dg_ref, dsc_ref, dsh_ref,
             dgm_ref):
        _, vjp = jax.vjp(lambda *t: _modulate_fn(0, *t)[0], x_ref[...], g_ref[...], sc_ref[...], sh_ref[...])
        dxm, dg, dsc, dsh = vjp(dh_ref[...])
        dx = dxm + din_ref[...]
        dx_ref[...] = dx
        dy_ref[...] = (gm_ref[...] * dx).astype(BF16)
        sums = (dg, dsc, dsh, jnp.sum(dx * y_ref[...], axis=0, keepdims=True))
        first = pl.program_id(0) == 0
        for ref, val in zip((dg_ref, dsc_ref, dsh_ref, dgm_ref), sums):
            @pl.when(first)
            def _(ref=ref, val=val):
                ref[...] = val

            @pl.when(jnp.logical_not(first))
            def _(ref=ref, val=val):
                ref[...] += val

    blk = pl.BlockSpec((ts, D), lambda i: (i, 0))
    vec = pl.BlockSpec((1, D), lambda i: (0, 0))
    dx, dy, dg, dsc, dsh, dgm = pl.pallas_call(
        body, name="modulate_bwd_chain", grid=(s // ts,),
        in_specs=[blk, vec, vec, vec, blk, blk, blk, vec], out_specs=[blk, blk, vec, vec, vec, vec],
        out_shape=[jax.ShapeDtypeStruct((s, D), F32), jax.ShapeDtypeStruct((s, D), BF16)]
        + [jax.ShapeDtypeStruct((1, D), F32)] * 4,
        compiler_params=_params(("arbitrary",)),
    )(x, *pars, dh, dx_in, prev[0], prev[1])
    return [dx, dg, dsc, dsh, (dy, dgm)]


def _out_proj(a, w, x, p, nxt, name, **kw):
    res = _matmul([(a, w)], "nn", name, out_dtype=BF16, resid=(x, p["gm"]) + tuple(nxt or ()), **kw)
    return res[1], res[0], (res[2] if nxt else None)


def _ffn_fwd(x, p, ties=(), h=None, nxt=None):
    if h is None:
        h, ties = _modulate(x, p, ties), ()
    gate, up, act = _ffn_in(h, p["w_in"], "ffn_in", ties=ties)
    res = _ffn_out(act, p["wo"], (x, p["gm"]) + tuple(nxt or ()), "ffn_out")
    return res[1], dict(x=x, h=h, gate=gate, up=up, act=act, y=res[0]), (res[2] if nxt else None)


def _ffn_bwd(t, p, dxn, res=None, prev=None, ties=()):
    dy, dgm = res or _residual_bwd(t["y"], p["gm"], dxn)
    dgate, dup = _ffn_bwd_act(dy, p["wo"], t["gate"], t["up"], "ffn_bwd_act", ties=ties)
    dwo = _ffn_dwo(t["act"], dy, "ffn_dwo", ties=ties)
    dh = _ffn_dh(dgate, dup, p["w_in"], "ffn_dh")
    dwi = _ffn_dwi(t["h"], dgate, dup, "ffn_dwi")
    dx, dgain, dscale, dshift, res_prev = _modulate_bwd(t["x"], p, dh, dxn, prev)
    return dx, dict(gain=dgain, scale=dscale, shift=dshift, gm=dgm, w_in=dwi, wo=dwo), res_prev


def _pad128(t):
    return jnp.pad(t, ((0, 0), (0, 128 - t.shape[1])))


def _gdn_fwd(x, p, ties=(), h=None, nxt=None):
    s = x.shape[0]
    if h is None:
        h, ties = _modulate(x, p, ties), ()
    pm = _mm(h, p["w_main"], "nn", "gdn_proj", ties=ties)
    tail = _mm(h, p["w_tail"], "nn", "gdn_proj_tail", ties=ties)
    qkv = _gdn_conv_fwd(pm, p["conv_w"], "gdn_conv")
    beta, gcum = _rowwise_fwd(_gdn_gates_fn, [(tail, C128), (tail, (128, 128))], [p["a_log"], p["dt_bias"]],
                              [(128, F32, C128)] * 2, "gdn_gates")
    grow = gcum[:, :HEADS].reshape(s // CHUNK, CHUNK, N_GROUPS, GROUP).transpose(0, 2, 3, 1)
    grow = grow.reshape(s // CHUNK, N_GROUPS, 1, GROWS)
    o, states, invs = _gdn_scan_fwd(qkv, beta, gcum, grow, "gdn_scan")
    on, = _rowwise_fwd(_gdn_outnorm_fn, [(o, HEAD_V), (pm, [(3 * D + h_ * HEAD, HEAD) for h_ in range(HEADS)])],
                       [p["norm_g"]], [(D, BF16, HEAD_V)], "gdn_outnorm", groups=HEADS)
    xn, y, hn = _out_proj(on, p["w_out"], x, p, nxt, "mix_out", tm=512)
    t = dict(x=x, h=h, pm=pm, tail=tail, qkv=qkv, beta=beta, gcum=gcum, grow=grow, o=o, states=states, invs=invs,
             on=on, y=y)
    return xn, t, hn


def _gdn_bwd(t, p, dxn, res=None, prev=None, ties=()):
    s = dxn.shape[0]
    zc = [(3 * D + h_ * HEAD, HEAD) for h_ in range(HEADS)]
    dy, dgm = res or _residual_bwd(t["y"], p["gm"], dxn)
    dw_out = _mm(t["on"], dy, "tn", "mix_dwo", ties=ties)
    don = _mm(dy, p["w_out"], "nt", "mix_dout", ties=ties)
    do, dz, dnorm_g = _rowwise_bwd(_gdn_outnorm_fn, [(t["o"], HEAD_V), (t["pm"], zc)], [p["norm_g"]],
                                   [(D, BF16, HEAD_V)], [don], [(0, HEAD_V), (1, HEAD_V)], [(D, F32), (D, BF16)],
                                   "gdn_outnorm_bwd", groups=HEADS)
    dq, dk, dv, dbeta, dg, dgr = _gdn_scan_bwd(t["qkv"], t["beta"], t["gcum"], t["grow"], t["states"], t["invs"], do,
                                               "gdn_scan_bwd")
    dg = dg + _pad128(dgr.reshape(s // CHUNK, N_GROUPS, GROUP, CHUNK).transpose(0, 3, 1, 2).reshape(s, HEADS))
    dtail, da_log, ddt = _rowwise_bwd(_gdn_gates_fn, [(t["tail"], C128), (t["tail"], (128, 128))],
                                      [p["a_log"], p["dt_bias"]], [(128, F32, C128)] * 2, [dbeta, dg],
                                      [(0, C128), (0, (128, 128))], [(256, F32)], "gdn_gates_bwd")
    dxs, dcw = [], []
    for part, d in enumerate((dq, dk, dv)):
        dx_, dw_ = _gdn_conv_bwd(t["pm"], p["conv_w"], d, part, "gdn_conv_bwd")
        dxs.append(dx_)
        dcw.append(dw_)
    pieces = dxs + [dz]
    dh = _matmul([(d, p["w_main"]) for d in pieces] + [(dtail, p["w_tail"])], "nt", "gdn_dh",
                 boffs=[0, D, 2 * D, 3 * D, 0], tk=512)
    dw_main = [_mm(t["h"], d, "tn", "gdn_dwi") for d in pieces]
    dw_tail = _mm(t["h"], dtail, "tn", "gdn_dwi_tail")
    dx, dgain, dscale, dshift, res_prev = _modulate_bwd(t["x"], p, dh, dxn, prev)
    return dx, dict(gain=dgain, scale=dscale, shift=dshift, gm=dgm, w_main=jnp.concatenate(dw_main, axis=1),
                    w_tail=dw_tail, conv_w=jnp.concatenate(dcw, axis=1), a_log=da_log, dt_bias=ddt,
                    norm_g=dnorm_g, w_out=dw_out), res_prev


def _q_rows(q2, cosf, sins):
    return [(q2, HEAD_NOPE), (q2, HEAD_ROPE), (cosf, C128), (sins, C128)]


def _mla_fwd(x, p, kv, ties=(), h=None, nxt=None):
    if h is None:
        h, ties = _modulate(x, p, ties), ()
    cq = _mm(h, p["w_dq"], "nn", "mla_dq", ties=ties)
    cqn, = _rowwise_fwd(_rms_fn, [(cq, (0, Q_LORA))], [p["q_lora_g"]], [(Q_LORA, BF16, (0, Q_LORA))], "mla_qlora_norm")
    q2 = _mm(cqn, p["w_uq"], "nn", "mla_uq")
    qn, = _rowwise_fwd(_q_norm_rope_fn, _q_rows(q2, kv["cosf"], kv["sins"]), [p["q_gn"], p["q_gr"]],
                       [(HEADS * HEAD_PAD, BF16, HEAD_ALL)], "mla_q_norm", groups=HEADS)
    o, lse = _attn_fwd(qn, kv["kn"], kv["vb"], "mla_attn")
    xn, y, hn = _out_proj(o, p["w_out"], x, p, nxt, "mix_out", tm=512)
    return xn, dict(x=x, h=h, cq=cq, cqn=cqn, q2=q2, qn=qn, o=o, lse=lse, y=y), hn


def _mla_bwd(t, p, kv, dxn, res=None, prev=None, ties=(), dkv_sum=None):
    dy, dgm = res or _residual_bwd(t["y"], p["gm"], dxn)
    dw_out = _mm(t["o"], dy, "tn", "mix_dwo", ties=ties)
    do = _mm(dy, p["w_out"], "nt", "mix_dout", ties=ties)
    dq, dk, dv = _attn_bwd(t["qn"], kv["kn"], kv["vb"], do, t["o"], t["lse"], "mla_attn_bwd", dkv_sum)
    dq2, dq_gn, dq_gr = _rowwise_bwd(_q_norm_rope_fn, _q_rows(t["q2"], kv["cosf"], kv["sins"]), [p["q_gn"], p["q_gr"]],
                                     [(HEADS * HEAD_PAD, BF16, HEAD_ALL)], [dq],
                                     [(0, HEAD_NOPE), (0, HEAD_ROPE), None, None], [(HEADS * HEAD_PAD, BF16)],
                                     "mla_q_norm_bwd", groups=HEADS)
    dw_uq = _mm(t["cqn"], dq2, "tn", "mla_dwuq")
    dcqn = _mm(dq2, p["w_uq"], "nt", "mla_dcq")
    dcq, dq_lora_g = _rowwise_bwd(_rms_fn, [(t["cq"], (0, Q_LORA))], [p["q_lora_g"]], [(Q_LORA, BF16, (0, Q_LORA))],
                                  [dcqn], [(0, (0, Q_LORA))], [(Q_LORA, BF16)], "mla_qlora_norm_bwd")
    dw_dq = _mm(t["h"], dcq, "tn", "mla_dwdq")
    dh = _mm(dcq, p["w_dq"], "nt", "mla_dh")
    dx, dgain, dscale, dshift, res_prev = _modulate_bwd(t["x"], p, dh, dxn, prev)
    grads = dict(gain=dgain, scale=dscale, shift=dshift, gm=dgm, w_dq=dw_dq, q_lora_g=dq_lora_g, w_uq=dw_uq,
                 q_gn=dq_gn, q_gr=dq_gr, w_out=dw_out)
    return dx, grads, res_prev, dk, dv


def _k_rows(kvp, ckv, cosf, sins):
    return [(kvp, HEAD_NOPE), (kvp, HEAD_ROPE), (ckv, (KV_LORA, 128)), (cosf, C128), (sins, C128)]


def _kv_fwd(x, p, cosf, sins):
    h = _modulate(x, p)
    ckv = _mm(h, p["w_dkv"], "nn", "kv_down")
    lat, = _rowwise_fwd(_rms_fn, [(ckv, (0, KV_LORA))], [p["kv_g"]], [(KV_LORA, BF16, (0, KV_LORA))], "kv_norm")
    kvp = _mm(lat, p["w_ukv"], "nn", "kv_up")
    kn, vb = _rowwise_fwd(_k_norm_rope_fn, _k_rows(kvp, ckv, cosf, sins), [p["k_gn"], p["k_gr"]],
                          [(HEADS * HEAD_PAD, BF16, HEAD_ALL), (HEADS * HEAD, BF16, HEAD_V)], "kv_k_norm",
                          groups=HEADS)
    return dict(x=x, h=h, ckv=ckv, lat=lat, kvp=kvp, kn=kn, vb=vb, cosf=cosf, sins=sins)


def _kv_bwd(t, p, dk, dv, dx_in, prev):
    dkvp, drope, dk_gn, dk_gr = _rowwise_bwd(
        _k_norm_rope_fn, _k_rows(t["kvp"], t["ckv"], t["cosf"], t["sins"]), [p["k_gn"], p["k_gr"]],
        [(HEADS * HEAD_PAD, BF16, HEAD_ALL), (HEADS * HEAD, BF16, HEAD_V)], [dk, dv],
        [(0, HEAD_NOPE), (0, HEAD_ROPE), (1, C128), None, None], [(HEADS * HEAD_PAD, BF16), (128, F32)],
        "kv_k_norm_bwd", groups=HEADS)
    dw_ukv = _mm(t["lat"], dkvp, "tn", "kv_dwukv")
    dlat = _mm(dkvp, p["w_ukv"], "nt", "kv_dlat")
    dckv, dkv_g = _rowwise_bwd(_rms_fn, [(t["ckv"], (0, KV_LORA))], [p["kv_g"]], [(KV_LORA, BF16, (0, KV_LORA))],
                               [dlat], [(0, (0, KV_LORA))], [(KV_LORA, F32)], "kv_norm_bwd")
    dw_dkv = jnp.concatenate([_mm(t["h"], dckv, "tn", "kv_dwdkv"), _mm(t["h"], drope, "tn", "kv_dwdkv_rope")], axis=1)
    dh = _matmul([(dckv, p["w_dkv"]), (drope, p["w_dkv"])], "nt", "kv_dh", boffs=[0, KV_LORA])
    dx, dgain, dscale, dshift, res_prev = _modulate_bwd(t["x"], p, dh, dx_in, prev)
    return dx, dict(gain=dgain, scale=dscale, shift=dshift, w_dkv=dw_dkv, kv_g=dkv_g, w_ukv=dw_ukv, k_gn=dk_gn,
                    k_gr=dk_gr), res_prev


WEIGHTS = ["ada_w", "ada_b", "norm_g", "ffn_w_in", "ffn_w_out", "gdn_w_in", "gdn_conv_w", "gdn_a_log", "gdn_dt_bias",
           "gdn_norm_g", "gdn_w_out", "kv_ada_w", "kv_ada_b", "kv_norm_g", "mla_w_dkv", "mla_kv_norm_g", "mla_w_ukv",
           "mla_k_norm_g", "mla_w_dq", "mla_q_lora_norm_g", "mla_w_uq", "mla_q_norm_g", "mla_w_out"]
SMALL = [("ada_b", 4 * N_MOD * D), ("kv_ada_b", 2 * D), ("norm_g", DEPTH * 3 * D), ("gdn_conv_w", N_A * CONV_K * 3 * D),
         ("gdn_a_log", N_A * HEADS), ("gdn_dt_bias", N_A * HEADS), ("gdn_norm_g", N_A * HEAD), ("kv_norm_g", D),
         ("mla_kv_norm_g", KV_LORA), ("mla_k_norm_g", QK_HEAD), ("mla_q_lora_norm_g", 2 * Q_LORA),
         ("mla_q_norm_g", 2 * QK_HEAD)]
SMALL_REPLICATED = [n for n, _ in SMALL if n not in ("norm_g", "gdn_conv_w")]


def _silu_fn(g, t):
    return (_silu(t),)


def _dup_rope(t):
    return jnp.concatenate([t[..., :NOPE], t[..., NOPE:], t[..., NOPE:]], axis=-1)


def _fold_rope(t):
    return jnp.concatenate([t[..., :NOPE], t[..., NOPE:QK_HEAD] + t[..., QK_HEAD:]], axis=-1)


def _pack(pieces, rows):
    flat = jnp.concatenate([p.reshape(-1).astype(F32) for p in pieces])
    return jnp.pad(flat, (0, rows * 128 - flat.shape[0])).reshape(rows, 128)


def _step(a):
    me = 4 * lax.axis_index("x") + 2 * lax.axis_index("y") + lax.axis_index("c")
    x = a["x"][0]
    cosf, sins = _rope_tables(a["positions"][0])

    n_in = 2 * D_FF // N_DEV
    n_gdn = (4 * D + 2 * HEADS) // N_DEV
    AHEAD = 2

    stages = [(l, part) for l in range(DEPTH) for part in range(3)]

    def stage_shards(l, part):
        if part != 1:
            sh = {"ffn_w_in": a["ffn_w_in"][l, part // 2], "ffn_w_out": a["ffn_w_out"][l, part // 2]}
            if part == 2 and l == N_A - 1:
                sh.update(mla_w_dkv=a["mla_w_dkv"], mla_w_ukv=a["mla_w_ukv"])
            return sh
        if l < N_A:
            return {"gdn_w_in": a["gdn_w_in"][l], "gdn_w_out": a["gdn_w_out"][l]}
        j = l - N_A
        return {"mla_w_dq": a["mla_w_dq"][j], "mla_w_uq": a["mla_w_uq"][j], "mla_w_out": a["mla_w_out"][j]}

    def zero_of(t):
        return jnp.minimum(jnp.abs(t[(0,) * t.ndim].astype(F32)), 0.0)

    def start_stage(l, part, tie):
        sh = stage_shards(l, part)
        return list(sh), _send_start([(w + tie).astype(BF16) for w in sh.values()], f"fetch_start_{l}_{part}", gather=True)

    def finish_stage(l, part, names, handle, after):
        srcs, lands = _send_wait(handle, after, f"fetch_wait_{l}_{part}", gather=True)
        return {n: lax.dynamic_update_slice(land, src[None], (me, 0, 0)) for n, src, land in zip(names, srcs, lands)}

    n_cw, n_ng = N_A * CONV_K * 3 * HEAD, DEPTH * 3 * HEAD
    small_all = _all_gather(_pack([a["gdn_conv_w"], a["norm_g"], a["c"]], 44), "gather_small").reshape(N_DEV, -1)
    conv_w = small_all[:, :n_cw].reshape(N_DEV, N_A, CONV_K, 3 * HEAD).transpose(1, 2, 0, 3).reshape(N_A, CONV_K, 3 * D)
    norm_g = small_all[:, n_cw:n_cw + n_ng].reshape(N_DEV, DEPTH, 3, HEAD).transpose(1, 2, 0, 3).reshape(DEPTH, 3, D)
    c_all = small_all[:, n_cw + n_ng:n_cw + n_ng + D]

    c_act, = _rowwise_fwd(_silu_fn, [(c_all, FULL)], [], [(D, F32, FULL)], "c_act")
    n_ada = N_MOD * D // N_DEV
    parts = [_mm(c_act, a["ada_w"][l], "nn", "mod_proj") for l in range(DEPTH)]
    parts.append(_mm(c_act, a["kv_ada_w"], "nn", "mod_proj_kv"))
    mod_recv = _exchange(jnp.concatenate(parts, axis=1)[:, None, :], "exchange_mod")[:, 0]
    mod = mod_recv[:, :DEPTH * n_ada].reshape(N_DEV, DEPTH, n_ada).transpose(1, 0, 2).reshape(DEPTH, N_MOD * D)
    mod = (mod + a["ada_b"]).reshape(DEPTH, N_MOD, D)
    kvmod = mod_recv[:, DEPTH * n_ada:].reshape(2 * D) + a["kv_ada_b"]

    def row(v):
        return v[None]

    def ffn_params(l, i, w):
        k = 0 if i == 0 else 6
        return dict(gain=row(norm_g[l, 0 if i == 0 else 2]), shift=row(mod[l, k]), scale=row(mod[l, k + 1]),
                    gm=0.5 * row(mod[l, k + 2]), w_in=w["ffn_w_in"], wo=w["ffn_w_out"].reshape(D_FF, D))

    def gdn_params(l, w):
        w_in = w["gdn_w_in"].transpose(1, 0, 2).reshape(D, 4 * D + 2 * HEADS)
        pad = lambda t: jnp.pad(t, ((0, 0), (0, 128 - HEADS)))
        return dict(gain=row(norm_g[l, 1]), shift=row(mod[l, 3]), scale=row(mod[l, 4]), gm=row(mod[l, 5]),
                    w_main=w_in[:, :4 * D],
                    w_tail=jnp.concatenate([pad(w_in[:, 4 * D:4 * D + HEADS]), pad(w_in[:, 4 * D + HEADS:])], axis=1),
                    conv_w=conv_w[l], a_log=_pad128(row(a["gdn_a_log"][l])), dt_bias=_pad128(row(a["gdn_dt_bias"][l])),
                    norm_g=row(a["gdn_norm_g"][l]), w_out=w["gdn_w_out"].reshape(D, D))

    def mla_params(l, w):
        j = l - N_A
        uq = w["mla_w_uq"].transpose(1, 0, 2)
        qg = _dup_rope(a["mla_q_norm_g"][j])
        return dict(gain=row(norm_g[l, 1]), shift=row(mod[l, 3]), scale=row(mod[l, 4]), gm=row(mod[l, 5]),
                    w_dq=w["mla_w_dq"].reshape(D, Q_LORA), q_lora_g=row(a["mla_q_lora_norm_g"][j]),
                    w_uq=_dup_rope(uq).reshape(Q_LORA, HEADS * HEAD_PAD), q_gn=row(qg[:NOPE]), q_gr=row(qg[NOPE:]),
                    w_out=w["mla_w_out"].reshape(D, D))

    def kv_params(w):
        w_dkv = w["mla_w_dkv"].reshape(D, KV_LORA + ROPE)
        kg = _dup_rope(a["mla_k_norm_g"])
        return dict(gain=row(a["kv_norm_g"]), shift=row(kvmod[:D]), scale=row(kvmod[D:]),
                    w_dkv=jnp.concatenate([w_dkv, w_dkv[:, KV_LORA:]], axis=1), kv_g=row(a["mla_kv_norm_g"]),
                    w_ukv=w["mla_w_ukv"].transpose(1, 0, 2).reshape(KV_LORA, HEADS * 2 * HEAD), k_gn=row(kg[:NOPE]),
                    k_gr=row(kg[NOPE:]))

    tapes, kv, kv_p, h = [[] for _ in range(DEPTH)], None, None, None
    first = {name: _all_gather((w + zero_of(mod)).astype(BF16), "fetch_first_" + name)
             for name, w in stage_shards(0, 0).items()}
    pending = []
    for l, part in stages[1:1 + AHEAD]:
        tie = pending[-1][1]["token"][0, 0] if pending else zero_of(first["ffn_w_out"])
        pending.append(start_stage(l, part, tie))
    for n, (l, part) in enumerate(stages):
        if n == 0:
            w, ties = first, tuple(h["token"] for _, h in pending)
        else:
            names, handle = pending.pop(0)
            w = finish_stage(l, part, names, handle, x)
            ties = ()
            if n + AHEAD < len(stages):
                pending.append(start_stage(*stages[n + AHEAD], zero_of(w[names[0]])))
                ties = (pending[-1][1]["token"],)
        nxt = None
        if n + 1 < len(stages):
            l2, part2 = stages[n + 1]
            k2 = 3 * part2
            nxt = (row(norm_g[l2, part2]), row(mod[l2, k2 + 1]), row(mod[l2, k2]))
        if part != 1:
            p = ffn_params(l, part // 2, w)
            x, t, h = _ffn_fwd(x, p, ties, h, nxt)
        else:
            p = gdn_params(l, w) if l < N_A else mla_params(l, w)
            x, t, h = _gdn_fwd(x, p, ties, h, nxt) if l < N_A else _mla_fwd(x, p, kv, ties, h, nxt)
        tapes[l] += [p, t]
        if part == 2 and l == N_A - 1:
            kv_p = kv_params(w)
            kv = _kv_fwd(x, kv_p, cosf, sins)
    dx, loss_blk = _loss_and_grad(x, a["loss_target"][0], "loss")
    loss = lax.psum(loss_blk[0, 0], ("x", "y", "c"))

    def by_cols(g, n):
        return g.reshape(g.shape[0], -1, n).transpose(1, 0, 2)

    def ffn_blocks(g):
        return {"ffn_w_in": g["w_in"], "ffn_w_out": g["wo"].reshape(N_DEV, D_FF // N_DEV, D)}

    def mixer_blocks(l, g):
        if l < N_A:
            full = jnp.concatenate([g["w_main"], g["w_tail"][:, :HEADS], g["w_tail"][:, 128:128 + HEADS]], axis=1)
            return {"gdn_w_in": by_cols(full, n_gdn), "gdn_w_out": g["w_out"].reshape(N_DEV, D // N_DEV, D)}
        return {"mla_w_dq": g["w_dq"].reshape(N_DEV, D // N_DEV, Q_LORA),
                "mla_w_uq": _fold_rope(g["w_uq"].reshape(Q_LORA, HEADS, HEAD_PAD)).transpose(1, 0, 2),
                "mla_w_out": g["w_out"].reshape(N_DEV, D // N_DEV, D)}

    sent = []

    def send(key, blocks, tie=0.0):
        handle = _send_start([(b + tie).astype(BF16) for b in blocks.values()], "grad_start_" + "_".join(map(str, key)),
                             gather=False)
        sent.append((key, list(blocks), handle))
        return (handle["token"],)

    grads = [None] * DEPTH
    dk_sum = dv_sum = kv_grads = res = None
    ties = ()
    for l in reversed(range(DEPTH)):
        p1, t1, pm_, tm_, p2, t2 = tapes[l]
        if l == N_A - 1:
            dx, kv_grads, res = _kv_bwd(kv, kv_p, dk_sum, dv_sum, dx, (t2["y"], p2["gm"]))
            d_dkv = kv_grads["w_dkv"]
            ties += send((l, 3), {
                "mla_w_dkv": jnp.concatenate(
                    [d_dkv[:, :KV_LORA], d_dkv[:, KV_LORA:KV_LORA + ROPE] + d_dkv[:, KV_LORA + ROPE:]],
                    axis=1).reshape(N_DEV, D // N_DEV, KV_LORA + ROPE),
                "mla_w_ukv": by_cols(kv_grads["w_ukv"], 2 * HEAD)})
        dx, g2, res = _ffn_bwd(t2, p2, dx, res, (tm_["y"], pm_["gm"]), ties)
        ties = send((l, 2), ffn_blocks(g2))
        if l < N_A:
            dx, gm_, res = _gdn_bwd(tm_, pm_, dx, res, (t1["y"], p1["gm"]), ties)
        else:
            dx, gm_, res, dk_sum, dv_sum = _mla_bwd(tm_, pm_, kv, dx, res, (t1["y"], p1["gm"]), ties,
                                                    None if dk_sum is None else (dk_sum, dv_sum))
        ties = send((l, 1), mixer_blocks(l, gm_))
        prev = (tapes[l - 1][5]["y"], tapes[l - 1][4]["gm"]) if l > 0 and l != N_A else None
        dx, g1, res = _ffn_bwd(t1, p1, dx, res, prev, ties)
        if l > 0:
            ties = send((l, 0), ffn_blocks(g1))
        grads[l] = (g1, gm_, g2)

    out = {}
    def dmod(l):
        g1, gm_, g2 = grads[l]
        return jnp.concatenate([g1["shift"], g1["scale"], 0.5 * g1["gm"], gm_["shift"], gm_["scale"], gm_["gm"],
                                g2["shift"], g2["scale"], 0.5 * g2["gm"]], axis=1)

    gdn = [grads[l][1] for l in range(N_A)]
    mla = [grads[l][1] for l in range(N_A, DEPTH)]
    small = {
        "ada_b": jnp.concatenate([dmod(l) for l in range(DEPTH)], axis=0),
        "kv_ada_b": jnp.concatenate([kv_grads["shift"], kv_grads["scale"]], axis=1),
        "norm_g": jnp.stack([jnp.concatenate([grads[l][0]["gain"], grads[l][1]["gain"], grads[l][2]["gain"]], axis=0)
                             for l in range(DEPTH)]),
        "gdn_conv_w": jnp.stack([g["conv_w"] for g in gdn]),
        "gdn_a_log": jnp.stack([g["a_log"][0, :HEADS] for g in gdn]),
        "gdn_dt_bias": jnp.stack([g["dt_bias"][0, :HEADS] for g in gdn]),
        "gdn_norm_g": jnp.stack([g["norm_g"][0] for g in gdn]),
        "kv_norm_g": kv_grads["gain"],
        "mla_kv_norm_g": kv_grads["kv_g"],
        "mla_k_norm_g": _fold_rope(jnp.concatenate([kv_grads["k_gn"], kv_grads["k_gr"]], axis=1)),
        "mla_q_lora_norm_g": jnp.stack([g["q_lora_g"][0] for g in mla]),
        "mla_q_norm_g": jnp.stack([_fold_rope(jnp.concatenate([g["q_gn"], g["q_gr"]], axis=1))[0] for g in mla]),
    }
    rows = 616
    assert sum(n for _, n in SMALL) <= rows * 128 and all(small[n].size == k for n, k in SMALL)
    small_recv = _all_gather(_pack([small[n] for n, _ in SMALL], rows), "gather_small_grads")
    small_recv = small_recv + send((0, 0), ffn_blocks(grads[0][0]), zero_of(small_recv))[0][0, 0]
    zero = lambda n, k: jnp.zeros((k,), F32)
    packed = {pre: _pack([a[pre + n] if n in SMALL_REPLICATED else zero(n, k) for n, k in SMALL], rows)
              for pre in ("", "m_", "v_")}
    res = _adamw([small_recv], packed[""], packed["m_"], packed["v_"], "adamw_small")
    offs = {}
    o = 0
    for n, k in SMALL:
        offs[n] = o
        o += k
    for n, k in SMALL:
        if n in SMALL_REPLICATED:
            out[n] = [r.reshape(-1)[offs[n]:offs[n] + k] for r in res]
    gsum = res[0].reshape(-1)
    g_norm = lax.dynamic_slice_in_dim(gsum[offs["norm_g"]:offs["norm_g"] + DEPTH * 3 * D].reshape(DEPTH * 3, D),
                                      me * HEAD, HEAD, axis=1)
    g_conv = lax.dynamic_slice_in_dim(
        gsum[offs["gdn_conv_w"]:offs["gdn_conv_w"] + N_A * CONV_K * 3 * D].reshape(N_A * CONV_K, 3 * D),
        me * 3 * HEAD, 3 * HEAD, axis=1)
    res2 = _adamw([_pack([g_norm, g_conv], 36)[None]], *[_pack([a[pre + "norm_g"], a[pre + "gdn_conv_w"]], 36)
                                                      for pre in ("", "m_", "v_")], "adamw_small")
    out["norm_g"] = [r.reshape(-1)[:n_ng] for r in res2]
    out["gdn_conv_w"] = [r.reshape(-1)[n_ng:n_ng + n_cw] for r in res2]

    c_act_t = c_act.T
    all_small = small_recv.reshape(N_DEV, -1)
    dmod_all = all_small[:, :DEPTH * N_MOD * D].reshape(N_DEV, DEPTH, N_MOD * D)
    dmod_mine = lax.dynamic_slice_in_dim(dmod_all, me * n_ada, n_ada, axis=2)
    g_ada = [_outer8(c_act_t, dmod_mine[:, l], "ada_grad")[None] for l in range(DEPTH)]
    out["ada_w"] = _adamw(g_ada, *[a[pre + "ada_w"].reshape(DEPTH * D, n_ada) for pre in ("", "m_", "v_")], "adamw")
    dkv_all = all_small[:, offs["kv_ada_b"]:offs["kv_ada_b"] + 2 * D]
    g_kv = _outer8(c_act_t, lax.dynamic_slice_in_dim(dkv_all, me * (2 * D // N_DEV), 2 * D // N_DEV, axis=1), "ada_grad")
    out["kv_ada_w"] = _adamw([g_kv[None]], *[a[pre + "kv_ada_w"] for pre in ("", "m_", "v_")], "adamw")

    pieces = {}
    for key, names, handle in sent:
        srcs, lands = _send_wait(handle, out["kv_ada_w"][0], "grad_wait_" + "_".join(map(str, key)), gather=False)
        for name, src, land in zip(names, srcs, lands):
            own = lax.dynamic_slice_in_dim(src, me, 1, axis=0)
            pieces.setdefault(name, []).append((key, lax.dynamic_update_slice(land, own, (me, 0, 0))))
    for name, parts in pieces.items():
        wide = a[name].shape[-1]
        out[name] = _adamw([p for _, p in sorted(parts, key=lambda kp: kp[0])], a[name].reshape(-1, wide),
                           a["m_" + name].reshape(-1, wide), a["v_" + name].reshape(-1, wide), "adamw")

    result = [loss, dx[None]]
    for k in range(4):
        result += [out[n][k].reshape(a[n].shape) for n in WEIGHTS]
    return tuple(result)


def kernel(x, c, positions, ada_w, ada_b, norm_g, ffn_w_in, ffn_w_out, gdn_w_in, gdn_conv_w, gdn_a_log, gdn_dt_bias, gdn_norm_g, gdn_w_out, kv_ada_w, kv_ada_b, kv_norm_g, mla_w_dkv, mla_kv_norm_g, mla_w_ukv, mla_k_norm_g, mla_w_dq, mla_q_lora_norm_g, mla_w_uq, mla_q_norm_g, mla_w_out, loss_target, m_ada_w, m_ada_b, m_norm_g, m_ffn_w_in, m_ffn_w_out, m_gdn_w_in, m_gdn_conv_w, m_gdn_a_log, m_gdn_dt_bias, m_gdn_norm_g, m_gdn_w_out, m_kv_ada_w, m_kv_ada_b, m_kv_norm_g, m_mla_w_dkv, m_mla_kv_norm_g, m_mla_w_ukv, m_mla_k_norm_g, m_mla_w_dq, m_mla_q_lora_norm_g, m_mla_w_uq, m_mla_q_norm_g, m_mla_w_out, v_ada_w, v_ada_b, v_norm_g, v_ffn_w_in, v_ffn_w_out, v_gdn_w_in, v_gdn_conv_w, v_gdn_a_log, v_gdn_dt_bias, v_gdn_norm_g, v_gdn_w_out, v_kv_ada_w, v_kv_ada_b, v_kv_norm_g, v_mla_w_dkv, v_mla_kv_norm_g, v_mla_w_ukv, v_mla_k_norm_g, v_mla_w_dq, v_mla_q_lora_norm_g, v_mla_w_uq, v_mla_q_norm_g, v_mla_w_out):
    return _step(dict(locals()))
```

```python
import functools
import math

import jax
import jax.numpy as jnp
from jax import lax
from jax.experimental import pallas as pl
from jax.experimental.pallas import tpu as pltpu

F32 = jnp.float32
BF16 = jnp.bfloat16

N_DEV = 8
D = 1024
D_FF = 2816
DEPTH = 4
N_A = 2
N_MOD = 9
HEADS = 8
HEAD = 128
CHUNK = 64
CONV_K = 4
KV_LORA = 256
Q_LORA = 384
NOPE = 128
ROPE = 64
QK_HEAD = NOPE + ROPE
HEAD_PAD = 256
ROPE_BASE = 10000.0
EPS = 1e-6
LR, B1, B2, ADAM_EPS, WD, STEP = 0.001, 0.9, 0.999, 1e-08, 0.01, 10

VMEM_LIMIT = 48 * 1024 * 1024
ROW_TILE = 256
MESH = pl.DeviceIdType.MESH

_NN = (((1,), (0,)), ((), ()))
_NT = (((1,), (1,)), ((), ()))
_TN = (((0,), (0,)), ((), ()))
_DIMS = {"nn": _NN, "nt": _NT, "tn": _TN}


def _params(dims=None):
    return pltpu.CompilerParams(dimension_semantics=dims, vmem_limit_bytes=VMEM_LIMIT)


def _tile(n, target):
    for t in range(target - target % 128, 0, -128):
        if n % t == 0:
            return t
    return n


_TIE_SPEC1 = pl.BlockSpec((8, 128), lambda i: (0, 0))
_TIE_SPEC2 = pl.BlockSpec((8, 128), lambda i, j: (0, 0))
_TIE_SPEC3 = pl.BlockSpec((8, 128), lambda i, j, k: (0, 0))


def _matmul(pairs, form, name, out_dtype=F32, tm=1408, tn=1408, tk=1408, boffs=None, resid=None, ties=()):
    a0, b0 = pairs[0]
    if form == "nn":
        m, n = a0.shape[0], b0.shape[1]
        ks = [a.shape[1] for a, _ in pairs]
    elif form == "nt":
        m, n = a0.shape[0], b0.shape[0]
        ks = [a.shape[1] for a, _ in pairs]
    else:
        m, n = a0.shape[1], b0.shape[1]
        ks = [a.shape[0] for a, _ in pairs]
    tm, tn = _tile(m, tm), _tile(n, tn)
    tks = [_tile(k, tk) for k in ks]
    boffs = boffs or [0] * len(pairs)
    assert m % tm == 0 and n % tn == 0 and all(o % t == 0 for o, t in zip(boffs, tks)), (name, m, n, ks)
    steps = [k // t for k, t in zip(ks, tks)]
    starts = [sum(steps[:p]) for p in range(len(pairs))]
    nk = sum(steps)

    def kidx(p, k):
        return jnp.clip(k - starts[p], 0, steps[p] - 1)

    in_specs, args = [], []
    for p, (a, b) in enumerate(pairs):
        t = tks[p]
        if form == "tn":
            in_specs.append(pl.BlockSpec((t, tm), lambda i, j, k, p=p: (kidx(p, k), i)))
            in_specs.append(pl.BlockSpec((t, tn), lambda i, j, k, p=p: (kidx(p, k), j)))
        elif form == "nn":
            in_specs.append(pl.BlockSpec((tm, t), lambda i, j, k, p=p: (i, kidx(p, k))))
            in_specs.append(pl.BlockSpec((t, tn), lambda i, j, k, p=p: (kidx(p, k), j)))
        else:
            in_specs.append(pl.BlockSpec((tm, t), lambda i, j, k, p=p: (i, kidx(p, k))))
            in_specs.append(pl.BlockSpec((tn, t), lambda i, j, k, p=p, o=boffs[p] // t: (j, kidx(p, k) + o)))
        args += [a, b]
    dims = _DIMS[form]
    npairs = len(pairs)
    nres = len(resid or ())
    nin = 2 * npairs + len(ties) + nres
    out_blk = pl.BlockSpec((tm, tn), lambda i, j, k: (i, j))
    in_specs += [_TIE_SPEC3] * len(ties)
    args += list(ties)
    if resid:
        assert nres == 2 or (nres == 5 and tn == n)
        in_specs += [out_blk] + [pl.BlockSpec((1, tn), lambda i, j, k: (0, j))] * (nres - 1)
        args += list(resid)

    def body(*refs):
        o_ref = refs[nin]
        k = pl.program_id(2)

        def prod(p):
            return lax.dot_general(refs[2 * p][...].astype(BF16), refs[2 * p + 1][...].astype(BF16), dims,
                                   preferred_element_type=F32)

        def finish(y):
            o_ref[...] = y.astype(o_ref.dtype)
            if resid:
                x_ref, gate_ref = refs[nin - nres], refs[nin - nres + 1]
                xn = x_ref[...] + gate_ref[...] * y
                refs[nin + 1][...] = xn
                if nres == 5:
                    gain, scale, shift = (r[...] for r in refs[nin - 3:nin])
                    refs[nin + 2][...] = _modulate_fn(0, xn, gain, scale, shift)[0].astype(BF16)

        if nk == 1:
            finish(prod(0))
            return
        acc = refs[-1]

        @pl.when(k == 0)
        def _():
            acc[...] = jnp.zeros_like(acc)

        for p in range(npairs):
            @pl.when((k >= starts[p]) & (k < starts[p] + steps[p]))
            def _(p=p):
                acc[...] += prod(p)

        @pl.when(k == nk - 1)
        def _():
            finish(acc[...])

    res = pl.pallas_call(
        body, name=name, grid=(m // tm, n // tn, nk), in_specs=in_specs,
        out_specs=[out_blk] * (2 + (nres == 5)) if resid else out_blk,
        out_shape=([jax.ShapeDtypeStruct((m, n), out_dtype), jax.ShapeDtypeStruct((m, n), F32)]
                   + [jax.ShapeDtypeStruct((m, n), BF16)] * (nres == 5))
        if resid else jax.ShapeDtypeStruct((m, n), out_dtype),
        scratch_shapes=[] if nk == 1 else [pltpu.VMEM((tm, tn), F32)],
        compiler_params=_params(("parallel", "parallel", "arbitrary")),
    )(*args)
    return res


def _mm(a, b, form, name, **kw):
    return _matmul([(a, b)], form, name, **kw)


def _cols(spec, g):
    return spec[g] if isinstance(spec, list) else spec


def _rowwise_fwd(fn, rows, pars, outs, name, groups=1, ts=ROW_TILE, ties=()):
    s = rows[0][0].shape[0]
    ts = min(ts, s)
    assert s % ts == 0
    nr, npar = len(rows), len(pars)

    def body(*refs):
        par_t = [r[...] for r in refs[nr:nr + npar]]
        out_refs = refs[nr + npar + len(ties):]
        for g in range(groups):
            row_t = []
            for r, (_, spec) in zip(refs[:nr], rows):
                c0, w = _cols(spec, g)
                row_t.append(r[:, c0:c0 + w].astype(F32))
            res = fn(g, *row_t, *par_t)
            for o_ref, val, (_, _, spec) in zip(out_refs, res, outs):
                c0, w = _cols(spec, g)
                o_ref[:, c0:c0 + w] = val.astype(o_ref.dtype)

    return pl.pallas_call(
        body, name=name, grid=(s // ts,),
        in_specs=[pl.BlockSpec((ts, a.shape[1]), lambda i: (i, 0)) for a, _ in rows]
        + [pl.BlockSpec(p.shape, lambda i: (0, 0)) for p in pars] + [_TIE_SPEC1] * len(ties),
        out_specs=[pl.BlockSpec((ts, w), lambda i: (i, 0)) for w, _, _ in outs],
        out_shape=[jax.ShapeDtypeStruct((s, w), dt) for w, dt, _ in outs],
        compiler_params=_params(("parallel",)),
    )(*[a for a, _ in rows], *pars, *ties)


def _rowwise_bwd(fn, rows, pars, outs, douts, gmap, gshapes, name, groups=1, add=None, par_grads=True,
                 ts=ROW_TILE):
    s = rows[0][0].shape[0]
    ts = min(ts, s)
    assert s % ts == 0
    nr, npar, nout, ng = len(rows), len(pars), len(outs), len(gshapes)
    add = add or {}
    add_keys = sorted(add)

    def body(*refs):
        row_refs = refs[:nr]
        par_refs = refs[nr:nr + npar]
        dout_refs = refs[nr + npar:nr + npar + nout]
        add_refs = refs[nr + npar + nout:nr + npar + nout + len(add_keys)]
        g_refs = refs[nr + npar + nout + len(add_keys):][:ng]
        pg_refs = refs[nr + npar + nout + len(add_keys) + ng:]
        par_t = [r[...] for r in par_refs]
        par_acc = [None] * npar
        shared_acc = {}
        for g in range(groups):
            row_t = []
            for r, (_, spec) in zip(row_refs, rows):
                c0, w = _cols(spec, g)
                row_t.append(r[:, c0:c0 + w].astype(F32))
            cts = []
            for r, (_, _, spec) in zip(dout_refs, outs):
                c0, w = _cols(spec, g)
                cts.append(r[:, c0:c0 + w].astype(F32))
            _, vjp = jax.vjp(lambda *t, g=g: tuple(fn(g, *t)), *row_t, *par_t)
            grads = vjp(tuple(cts))
            for k in range(nr):
                if gmap[k] is None:
                    continue
                gi, spec = gmap[k]
                if isinstance(spec, list) or groups == 1:
                    c0, w = _cols(spec, g)
                    val = grads[k]
                    if gi in add:
                        val = val + add_refs[add_keys.index(gi)][:, c0:c0 + w].astype(F32)
                    g_refs[gi][:, c0:c0 + w] = val.astype(g_refs[gi].dtype)
                else:
                    shared_acc[k] = grads[k] if k not in shared_acc else shared_acc[k] + grads[k]
            if par_grads:
                for k in range(npar):
                    pg = grads[nr + k]
                    par_acc[k] = pg if par_acc[k] is None else par_acc[k] + pg
        for k, val in shared_acc.items():
            gi, (c0, w) = gmap[k]
            assert gi not in add
            g_refs[gi][:, c0:c0 + w] = val.astype(g_refs[gi].dtype)
        if par_grads:
            first = pl.program_id(0) == 0
            for k in range(npar):
                @pl.when(first)
                def _(k=k):
                    pg_refs[k][...] = par_acc[k]

                @pl.when(jnp.logical_not(first))
                def _(k=k):
                    pg_refs[k][...] += par_acc[k]

    out_specs = [pl.BlockSpec((ts, w), lambda i: (i, 0)) for w, _ in gshapes]
    out_shape = [jax.ShapeDtypeStruct((s, w), dt) for w, dt in gshapes]
    if par_grads:
        out_specs += [pl.BlockSpec(p.shape, lambda i: (0, 0)) for p in pars]
        out_shape += [jax.ShapeDtypeStruct(p.shape, F32) for p in pars]
    return pl.pallas_call(
        body, name=name, grid=(s // ts,),
        in_specs=[pl.BlockSpec((ts, a.shape[1]), lambda i: (i, 0)) for a, _ in rows]
        + [pl.BlockSpec(p.shape, lambda i: (0, 0)) for p in pars]
        + [pl.BlockSpec((ts, a.shape[1]), lambda i: (i, 0)) for a in douts]
        + [pl.BlockSpec((ts, add[k].shape[1]), lambda i: (i, 0)) for k in add_keys],
        out_specs=out_specs, out_shape=out_shape,
        compiler_params=_params(("arbitrary",)),
    )(*[a for a, _ in rows], *pars, *douts, *[add[k] for k in add_keys])


def _sigmoid(x):
    return 1.0 / (1.0 + jnp.exp(-x))


def _silu(x):
    return x * _sigmoid(x)


def _softplus(x):
    return jnp.maximum(x, 0.0) + jnp.log(1.0 + jnp.exp(-jnp.abs(x)))


def _rms(t, g, n=None):
    n = n or t.shape[-1]
    return t * lax.rsqrt(jnp.sum(t * t, axis=-1, keepdims=True) / n + EPS) * g


def _modulate_fn(g, x, gain, scale, shift):
    return (_rms(x, gain) * (1.0 + scale) + shift,)


def _resgate_fn(g, x, y, gm):
    return (x + gm * y,)


def _gate_only_fn(g, y, gm):
    return (gm * y,)


def _gdn_gates_fn(g, b_logit, a_logit, a_log, dt_bias):
    gate = -jnp.exp(a_log) * _softplus(a_logit + dt_bias)
    n = gate.shape[0]
    i = lax.broadcasted_iota(jnp.int32, (n, n), 0)
    j = lax.broadcasted_iota(jnp.int32, (n, n), 1)
    tri = (((i // CHUNK) == (j // CHUNK)) & (i >= j)).astype(F32)
    gcum = lax.dot_general(tri, gate, _NN, preferred_element_type=F32, precision=lax.Precision.HIGHEST)
    return _sigmoid(b_logit), gcum


def _gdn_outnorm_fn(g, o, z, gain):
    return (_rms(o, gain) * _silu(z),)


def _rms_fn(g, t, gain):
    return (_rms(t, gain),)


@jax.custom_vjp
def _swap_halves(t):
    return pltpu.roll(t, 32, 1)


_swap_halves.defvjp(lambda t: (pltpu.roll(t, 32, 1), None), lambda _, ct: (pltpu.roll(ct, 96, 1),))


def _head_norm_rope_fn(g, nope, rope, cosf, sins, gain_n, gain_r):
    first = lax.broadcasted_iota(jnp.int32, rope.shape, 1) < ROPE
    ss = jnp.sum(nope * nope, axis=-1, keepdims=True) + jnp.sum(jnp.where(first, rope * rope, 0.0), axis=-1,
                                                                 keepdims=True)
    r = lax.rsqrt(ss / QK_HEAD + EPS)
    tn = nope * r * gain_n
    tr = rope * r * gain_r
    rot = jnp.where(first, tr * cosf + _swap_halves(tr) * sins, 0.0)
    return tn, rot


def _q_norm_rope_fn(g, nope, rope, cosf, sins, gain_n, gain_r):
    tn, rot = _head_norm_rope_fn(g, nope, rope, cosf, sins, gain_n, gain_r)
    return (jnp.concatenate([tn, rot], axis=1),)


def _k_norm_rope_fn(g, nope, val, rope, cosf, sins, gain_n, gain_r):
    tn, rot = _head_norm_rope_fn(g, nope, rope, cosf, sins, gain_n, gain_r)
    return jnp.concatenate([tn, rot], axis=1), val


def _loss_fn(g, y, target):
    e = y - target
    return (jnp.sum(e * e, axis=-1, keepdims=True) * (0.5 / D) * jnp.ones((1, 128), F32),)


FF_SH = 2 * D_FF // N_DEV
FF_G = N_DEV // 2


def _ffn_in(h, w_in, name, tm=1024, ties=()):
    s = h.shape[0]
    tm = min(tm, s)

    def body(h_ref, wg_ref, wu_ref, *rest):
        g_ref, u_ref, a_ref = rest[-3:]
        hb = h_ref[...]
        gate = jnp.dot(hb, wg_ref[...], preferred_element_type=F32)
        up = jnp.dot(hb, wu_ref[...], preferred_element_type=F32)
        g_ref[...] = gate.astype(BF16)
        u_ref[...] = up.astype(BF16)
        a_ref[...] = (_silu(gate) * up).astype(BF16)

    spec = pl.BlockSpec((None, tm, FF_SH), lambda j, i: (j, i, 0))
    return pl.pallas_call(
        body, name=name, grid=(FF_G, s // tm),
        in_specs=[pl.BlockSpec((tm, D), lambda j, i: (i, 0)), pl.BlockSpec((None, D, FF_SH), lambda j, i: (j, 0, 0)),
                  pl.BlockSpec((None, D, FF_SH), lambda j, i: (j + FF_G, 0, 0))] + [_TIE_SPEC2] * len(ties),
        out_specs=[spec, spec, spec], out_shape=[jax.ShapeDtypeStruct((FF_G, s, FF_SH), BF16)] * 3,
        compiler_params=_params(("parallel", "parallel")),
    )(h, w_in, w_in, *ties)


def _ffn_out(act, wo, resid, name, tm=512):
    s = act.shape[1]
    tm = min(tm, s)
    nres = len(resid)

    def body(a_ref, b_ref, x_ref, gate_ref, *rest):
        mods, outs, acc = rest[:nres - 2], rest[nres - 2:-1], rest[-1]
        k = pl.program_id(1)

        @pl.when(k == 0)
        def _():
            acc[...] = jnp.zeros_like(acc)

        rows = pl.ds(pl.multiple_of(k * FF_SH, 64), FF_SH)
        acc[...] += jnp.dot(a_ref[...], b_ref[rows, :], preferred_element_type=F32)

        @pl.when(k == FF_G - 1)
        def _():
            y = acc[...]
            xn = x_ref[...] + gate_ref[...] * y
            outs[0][...] = y.astype(BF16)
            outs[1][...] = xn
            if mods:
                outs[2][...] = _modulate_fn(0, xn, *[m[...] for m in mods])[0].astype(BF16)

    blk = pl.BlockSpec((tm, D), lambda i, k: (i, 0))
    vec = pl.BlockSpec((1, D), lambda i, k: (0, 0))
    return pl.pallas_call(
        body, name=name, grid=(s // tm, FF_G),
        in_specs=[pl.BlockSpec((None, tm, FF_SH), lambda i, k: (k, i, 0)), pl.BlockSpec((D_FF, D), lambda i, k: (0, 0)),
                  blk] + [vec] * (nres - 1),
        out_specs=[blk] * (2 + (nres == 5)),
        out_shape=[jax.ShapeDtypeStruct((s, D), BF16), jax.ShapeDtypeStruct((s, D), F32)]
        + [jax.ShapeDtypeStruct((s, D), BF16)] * (nres == 5),
        scratch_shapes=[pltpu.VMEM((tm, D), F32)], compiler_params=_params(("parallel", "arbitrary")),
    )(act, wo, *resid)


def _ffn_bwd_act(dy, wo, gate, up, name, tm=1024, ties=()):
    s = dy.shape[0]
    tm = min(tm, s)

    def body(dy_ref, wo_ref, g_ref, u_ref, *rest):
        dg_ref, du_ref = rest[-2:]
        dact = lax.dot_general(dy_ref[...], wo_ref[...], _NT, preferred_element_type=F32)
        gate = g_ref[...].astype(F32)
        up = u_ref[...].astype(F32)
        sg = _sigmoid(gate)
        dg_ref[...] = (dact * up * (sg * (1.0 + gate * (1.0 - sg)))).astype(BF16)
        du_ref[...] = (dact * (gate * sg)).astype(BF16)

    spec = pl.BlockSpec((None, tm, FF_SH), lambda j, i: (j, i, 0))
    return pl.pallas_call(
        body, name=name, grid=(FF_G, s // tm),
        in_specs=[pl.BlockSpec((tm, D), lambda j, i: (i, 0)), pl.BlockSpec((FF_SH, D), lambda j, i: (j, 0)), spec, spec]
        + [_TIE_SPEC2] * len(ties),
        out_specs=[spec, spec], out_shape=[jax.ShapeDtypeStruct((FF_G, s, FF_SH), BF16)] * 2,
        compiler_params=_params(("parallel", "parallel")),
    )(dy, wo, gate, up, *ties)


def _ffn_dwo(act, dy, name, tk=1024, ties=()):
    s = act.shape[1]
    tk = min(tk, s)

    def body(a_ref, b_ref, *rest):
        o_ref, acc = rest[-2:]
        k = pl.program_id(1)

        @pl.when(k == 0)
        def _():
            acc[...] = jnp.zeros_like(acc)

        acc[...] += lax.dot_general(a_ref[...], b_ref[...], _TN, preferred_element_type=F32)

        @pl.when(k == s // tk - 1)
        def _():
            o_ref[...] = acc[...].astype(BF16)

    return pl.pallas_call(
        body, name=name, grid=(FF_G, s // tk),
        in_specs=[pl.BlockSpec((None, tk, FF_SH), lambda j, k: (j, k, 0)), pl.BlockSpec((tk, D), lambda j, k: (k, 0))]
        + [_TIE_SPEC2] * len(ties),
        out_specs=pl.BlockSpec((FF_SH, D), lambda j, k: (j, 0)), out_shape=jax.ShapeDtypeStruct((D_FF, D), BF16),
        scratch_shapes=[pltpu.VMEM((FF_SH, D), F32)], compiler_params=_params(("parallel", "arbitrary")),
    )(act, dy, *ties)


def _ffn_halves(k, gate_ref, up_ref, fn):
    pl.when(k < FF_G)(functools.partial(fn, gate_ref))
    pl.when(k >= FF_G)(functools.partial(fn, up_ref))


def _ffn_dh(dgate, dup, w_in, name, tm=2048):
    s = dgate.shape[1]
    tm = min(tm, s)

    def body(dg_ref, du_ref, w_ref, o_ref, acc):
        k = pl.program_id(1)

        @pl.when(k == 0)
        def _():
            acc[...] = jnp.zeros_like(acc)

        def add(d_ref):
            acc[...] += lax.dot_general(d_ref[...], w_ref[...], _NT, preferred_element_type=F32)

        _ffn_halves(k, dg_ref, du_ref, add)

        @pl.when(k == N_DEV - 1)
        def _():
            o_ref[...] = acc[...]

    return pl.pallas_call(
        body, name=name, grid=(s // tm, N_DEV),
        in_specs=[pl.BlockSpec((None, tm, FF_SH), lambda i, k: (jnp.minimum(k, FF_G - 1), i, 0)),
                  pl.BlockSpec((None, tm, FF_SH), lambda i, k: (jnp.maximum(k - FF_G, 0), i, 0)),
                  pl.BlockSpec((None, D, FF_SH), lambda i, k: (k, 0, 0))],
        out_specs=pl.BlockSpec((tm, D), lambda i, k: (i, 0)), out_shape=jax.ShapeDtypeStruct((s, D), F32),
        scratch_shapes=[pltpu.VMEM((tm, D), F32)], compiler_params=_params(("parallel", "arbitrary")),
    )(dgate, dup, w_in)


def _ffn_dwi(h, dgate, dup, name, tk=1024):
    s = h.shape[0]
    tk = min(tk, s)

    def body(h_ref, dg_ref, du_ref, o_ref, acc):
        j, k = pl.program_id(0), pl.program_id(1)

        @pl.when(k == 0)
        def _():
            acc[...] = jnp.zeros_like(acc)

        def add(d_ref):
            acc[...] += lax.dot_general(h_ref[...], d_ref[...], _TN, preferred_element_type=F32)

        _ffn_halves(j, dg_ref, du_ref, add)

        @pl.when(k == s // tk - 1)
        def _():
            o_ref[...] = acc[...].astype(BF16)

    return pl.pallas_call(
        body, name=name, grid=(N_DEV, s // tk),
        in_specs=[pl.BlockSpec((tk, D), lambda j, k: (k, 0)),
                  pl.BlockSpec((None, tk, FF_SH), lambda j, k: (jnp.minimum(j, FF_G - 1), jnp.where(j < FF_G, k, s // tk - 1), 0)),
                  pl.BlockSpec((None, tk, FF_SH), lambda j, k: (jnp.maximum(j - FF_G, 0), jnp.where(j < FF_G, 0, k), 0))],
        out_specs=pl.BlockSpec((None, D, FF_SH), lambda j, k: (j, 0, 0)),
        out_shape=jax.ShapeDtypeStruct((N_DEV, D, FF_SH), BF16),
        scratch_shapes=[pltpu.VMEM((D, FF_SH), F32)], compiler_params=_params(("parallel", "arbitrary")),
    )(h, dgate, dup)


def _shift_down(x, d):
    rows = lax.broadcasted_iota(jnp.int32, x.shape, 0)
    return jnp.where(rows >= d, pltpu.roll(x, d, 0), 0.0)


def _shift_up(x, d):
    n = x.shape[0]
    rows = lax.broadcasted_iota(jnp.int32, x.shape, 0)
    return jnp.where(rows < n - d, pltpu.roll(x, n - d, 0), 0.0)


def _conv_post(pre, is_qk):
    a = _silu(pre)
    l2 = a * lax.rsqrt(jnp.sum(a * a, axis=-1, keepdims=True) + EPS)
    return jnp.where(is_qk, l2, a)


def _conv_pre(x, w):
    pre = x * w[CONV_K - 1:CONV_K, :]
    for j in range(CONV_K - 1):
        pre = pre + _shift_down(x, CONV_K - 1 - j) * w[j:j + 1, :]
    return pre


def _gdn_conv_fwd(pm, conv_w, name):
    s = pm.shape[0]
    nblk = 3 * D // HEAD

    def body(x_ref, w_ref, o_ref):
        is_qk = pl.program_id(0) < 2 * HEADS
        o_ref[...] = _conv_post(_conv_pre(x_ref[...], w_ref[...]), is_qk)

    return pl.pallas_call(
        body, name=name, grid=(nblk,),
        in_specs=[pl.BlockSpec((s, HEAD), lambda c: (0, c)), pl.BlockSpec((CONV_K, HEAD), lambda c: (0, c))],
        out_specs=pl.BlockSpec((s, HEAD), lambda c: (0, c)),
        out_shape=jax.ShapeDtypeStruct((s, 3 * D), F32), compiler_params=_params(("parallel",)),
    )(pm, conv_w)


def _gdn_conv_bwd(pm, conv_w, dout, part, name):
    s = pm.shape[0]
    off = part * HEADS

    def body(x_ref, w_ref, d_ref, dx_ref, dw_ref):
        x, w = x_ref[...], w_ref[...]
        _, vjp = jax.vjp(lambda p: _conv_post(p, part < 2), _conv_pre(x, w))
        dpre, = vjp(d_ref[...])
        dx = dpre * w[CONV_K - 1:CONV_K, :]
        rows = [None] * CONV_K
        rows[CONV_K - 1] = jnp.sum(dpre * x, axis=0, keepdims=True)
        for j in range(CONV_K - 1):
            dx = dx + _shift_up(dpre, CONV_K - 1 - j) * w[j:j + 1, :]
            rows[j] = jnp.sum(dpre * _shift_down(x, CONV_K - 1 - j), axis=0, keepdims=True)
        dx_ref[...] = dx.astype(BF16)
        dw_ref[...] = jnp.concatenate(rows, axis=0)

    return pl.pallas_call(
        body, name=name, grid=(HEADS,),
        in_specs=[pl.BlockSpec((s, HEAD), lambda c: (0, c + off)), pl.BlockSpec((CONV_K, HEAD), lambda c: (0, c + off)),
                  pl.BlockSpec((s, HEAD), lambda c: (0, c))],
        out_specs=[pl.BlockSpec((s, HEAD), lambda c: (0, c)), pl.BlockSpec((CONV_K, HEAD), lambda c: (0, c))],
        out_shape=[jax.ShapeDtypeStruct((s, D), BF16), jax.ShapeDtypeStruct((CONV_K, D), F32)],
        compiler_params=_params(("parallel",)),
    )(pm, conv_w, dout)


def _dot3(a, b, dims=_NN):
    ah, bh = a.astype(BF16), b.astype(BF16)
    al, bl = (a - ah.astype(F32)).astype(BF16), (b - bh.astype(F32)).astype(BF16)
    d = lambda u, v: lax.dot_general(u, v, dims, preferred_element_type=F32)
    return d(ah, bh) + (d(ah, bl) + d(al, bh))


def _make_dot(hi):
    def raw(a, b, dims):
        if hi:
            return _dot3(a, b, dims)
        return lax.dot_general(a.astype(BF16), b.astype(BF16), dims, preferred_element_type=F32)

    @functools.partial(jax.custom_vjp, nondiff_argnums=(2,))
    def dot(a, b, form):
        return raw(a, b, _DIMS[form])

    def fwd(a, b, form):
        return raw(a, b, _DIMS[form]), (a, b)

    def bwd(form, res, ct):
        a, b = res
        if form == "nn":
            return raw(ct, b, _NT), raw(a, ct, _TN)
        if form == "nt":
            return raw(ct, b, _NN), raw(ct, a, _TN)
        return raw(b, ct, _NT), raw(a, ct, _NN)

    dot.defvjp(fwd, bwd)
    return dot


_dot = _make_dot(False)
_dot_hi = _make_dot(True)


def _tri_inv_raw(low):
    n = low.shape[0]
    i = lax.broadcasted_iota(jnp.int32, (n, n), 0)
    j = lax.broadcasted_iota(jnp.int32, (n, n), 1)
    eye = (i == j).astype(F32)
    hdot = _dot3
    same16 = (i // 16) == (j // 16)
    neg = jnp.where(same16, -low, 0.0)
    inv = eye + neg
    power = neg
    for _ in range(3):
        power = hdot(power, power)
        inv = hdot(inv, eye + power)
    for blk in (32, 64):
        off = jnp.where(((i // blk) == (j // blk)) & ((i // (blk // 2)) != (j // (blk // 2))), low, 0.0)
        inv = inv - hdot(inv, hdot(off, inv))
    return inv


@jax.custom_vjp
def _tri_inv(low):
    return _tri_inv_raw(low)


def _tri_inv_fwd(low):
    inv = _tri_inv_raw(low)
    return inv, inv


def _tri_inv_bwd(inv, ct):
    return (-_dot3(_dot3(inv, ct, _TN), inv, _NT),)


_tri_inv.defvjp(_tri_inv_fwd, _tri_inv_bwd)


@jax.custom_vjp
def _tri_inv_given(low, inv):
    return inv


_tri_inv_given.defvjp(lambda low, inv: (inv, inv),
                      lambda inv, ct: (_tri_inv_bwd(inv, ct)[0], jnp.zeros_like(inv)))

GROUP = 4
N_GROUPS = HEADS // GROUP
GROWS = GROUP * CHUNK


def _gdn_group(q, k, v, beta, gc, gr, states, inv=None):
    n = q.shape[0]
    i = lax.broadcasted_iota(jnp.int32, (n, n), 0)
    j = lax.broadcasted_iota(jnp.int32, (n, n), 1)
    same = (i // CHUNK) == (j // CHUNK)
    incl, strict = same & (i >= j), same & (i > j)
    qs = q * (HEAD ** -0.5)
    decay = jnp.where(incl, jnp.exp(jnp.where(incl, gc - gr, 0.0)), 0.0)
    kb = k * beta
    eg = jnp.exp(gc)
    prod = _dot(jnp.concatenate([kb, qs], axis=0), k, "nt")
    low = jnp.where(strict, prod[:n] * decay, 0.0)
    attn = jnp.where(incl, prod[n:] * decay, 0.0)
    inv = _tri_inv(low) if inv is None else _tri_inv_given(low, inv)
    sol = _dot_hi(inv, jnp.concatenate([v * beta, kb * eg], axis=1), "nn")
    u, w, qg = sol[:, :HEAD], sol[:, HEAD:], qs * eg
    last = lax.broadcasted_iota(jnp.int32, (CHUNK, 1), 0) == CHUNK - 1
    v_new, o_state, carry = [], [], []
    for h, state in enumerate(states):
        rows = slice(h * CHUNK, (h + 1) * CHUNK)
        ws = _dot(jnp.concatenate([w[rows], qg[rows]], axis=0), state, "nn")
        v_new.append(u[rows] - ws[:CHUNK])
        o_state.append(ws[CHUNK:])
        g_last = jnp.sum(jnp.where(last, gc[rows], 0.0), axis=0, keepdims=True)
        carry.append((g_last, k[rows] * jnp.exp(g_last - gc[rows])))
    o = jnp.concatenate(o_state, axis=0) + _dot(attn, jnp.concatenate(v_new, axis=0), "nn")
    new = tuple(state * jnp.exp(g_last) + _dot(k_dec, vn, "tn")
                for state, (g_last, k_dec), vn in zip(states, carry, v_new))
    return o, new, inv


def _gdn_specs(s, rev):
    nc = s // CHUNK
    at = (lambda n: nc - 1 - n) if rev else (lambda n: n)
    return nc, at, [
        pl.BlockSpec((CHUNK, D), lambda n: (at(n), 0)), pl.BlockSpec((CHUNK, D), lambda n: (at(n), 1)),
        pl.BlockSpec((CHUNK, D), lambda n: (at(n), 2)), pl.BlockSpec((CHUNK, HEAD), lambda n: (at(n), 0)),
        pl.BlockSpec((CHUNK, HEAD), lambda n: (at(n), 0)),
        pl.BlockSpec((None, N_GROUPS, 1, GROWS), lambda n: (at(n), 0, 0, 0))]


def _group_operands(grp, q_ref, k_ref, v_ref, b_blk, gc_blk, gr_blk):
    heads = range(grp * GROUP, (grp + 1) * GROUP)
    stack = lambda ref: jnp.concatenate([ref[:, h * HEAD:(h + 1) * HEAD] for h in heads], axis=0)
    col = lambda blk: jnp.concatenate([blk[:, h:h + 1] for h in heads], axis=0)
    return stack(q_ref), stack(k_ref), stack(v_ref), col(b_blk), col(gc_blk), gr_blk[grp]


def _gdn_scan_fwd(qkv, beta, gcum, grow, name):
    s = qkv.shape[0]
    nc, _, in_specs = _gdn_specs(s, rev=False)

    def body(q_ref, k_ref, v_ref, b_ref, gc_ref, gr_ref, o_ref, st_ref, inv_ref, state):
        @pl.when(pl.program_id(0) == 0)
        def _():
            state[...] = jnp.zeros_like(state)

        b_blk, gc_blk, gr_blk = b_ref[...], gc_ref[...], gr_ref[...]
        old = [state[h] for h in range(HEADS)]
        res = [_gdn_group(*_group_operands(grp, q_ref, k_ref, v_ref, b_blk, gc_blk, gr_blk),
                          old[grp * GROUP:(grp + 1) * GROUP]) for grp in range(N_GROUPS)]
        for grp, (o, new, inv) in enumerate(res):
            inv_ref[grp] = inv
            for hh in range(GROUP):
                h = grp * GROUP + hh
                st_ref[h] = old[h]
                o_ref[:, h * HEAD:(h + 1) * HEAD] = o[hh * CHUNK:(hh + 1) * CHUNK]
                state[h] = new[hh]

    return pl.pallas_call(
        body, name=name, grid=(nc,), in_specs=in_specs,
        out_specs=[pl.BlockSpec((CHUNK, D), lambda n: (n, 0)),
                   pl.BlockSpec((None, HEADS, HEAD, HEAD), lambda n: (n, 0, 0, 0)),
                   pl.BlockSpec((None, N_GROUPS, GROWS, GROWS), lambda n: (n, 0, 0, 0))],
        out_shape=[jax.ShapeDtypeStruct((s, D), F32), jax.ShapeDtypeStruct((nc, HEADS, HEAD, HEAD), F32),
                   jax.ShapeDtypeStruct((nc, N_GROUPS, GROWS, GROWS), F32)],
        scratch_shapes=[pltpu.VMEM((HEADS, HEAD, HEAD), F32)],
        compiler_params=_params(("arbitrary",)),
    )(qkv, qkv, qkv, beta, gcum, grow)


def _gdn_scan_bwd(qkv, beta, gcum, grow, states, invs, do, name):
    s = qkv.shape[0]
    nc, at, in_specs = _gdn_specs(s, rev=True)
    in_specs += [pl.BlockSpec((None, HEADS, HEAD, HEAD), lambda n: (at(n), 0, 0, 0)),
                 pl.BlockSpec((None, N_GROUPS, GROWS, GROWS), lambda n: (at(n), 0, 0, 0)),
                 pl.BlockSpec((CHUNK, D), lambda n: (at(n), 0))]

    def body(q_ref, k_ref, v_ref, b_ref, gc_ref, gr_ref, st_ref, inv_ref, do_ref, dq_ref, dk_ref, dv_ref, db_ref,
             dgc_ref, dgr_ref, dstate):
        @pl.when(pl.program_id(0) == 0)
        def _():
            dstate[...] = jnp.zeros_like(dstate)

        b_blk, gc_blk, gr_blk = b_ref[...], gc_ref[...], gr_ref[...]
        dold = [dstate[h] for h in range(HEADS)]
        res = []
        for grp in range(N_GROUPS):
            heads = range(grp * GROUP, (grp + 1) * GROUP)
            inv = inv_ref[grp]
            _, vjp = jax.vjp(lambda q, k, v, b, gc, gr, *st, inv=inv: _gdn_group(q, k, v, b, gc, gr, st, inv)[:2],
                             *_group_operands(grp, q_ref, k_ref, v_ref, b_blk, gc_blk, gr_blk),
                             *[st_ref[h] for h in heads])
            d_out = jnp.concatenate([do_ref[:, h * HEAD:(h + 1) * HEAD] for h in heads], axis=0)
            res.append(vjp((d_out, tuple(dold[h] for h in heads))))
        lane = lax.broadcasted_iota(jnp.int32, (CHUNK, HEAD), 1)
        db_all = jnp.zeros((CHUNK, HEAD), F32)
        dgc_all = jnp.zeros((CHUNK, HEAD), F32)
        for grp, (dq, dk, dv, db, dgc, dgr, *dst) in enumerate(res):
            dgr_ref[grp] = dgr
            for hh in range(GROUP):
                h = grp * GROUP + hh
                cs, rows = slice(h * HEAD, (h + 1) * HEAD), slice(hh * CHUNK, (hh + 1) * CHUNK)
                dq_ref[:, cs] = dq[rows]
                dk_ref[:, cs] = dk[rows]
                dv_ref[:, cs] = dv[rows]
                dstate[h] = dst[hh]
                db_all = jnp.where(lane == h, db[rows], db_all)
                dgc_all = jnp.where(lane == h, dgc[rows], dgc_all)
        db_ref[...] = db_all
        dgc_ref[...] = dgc_all

    blk = pl.BlockSpec((CHUNK, D), lambda n: (at(n), 0))
    gblk = pl.BlockSpec((CHUNK, HEAD), lambda n: (at(n), 0))
    return pl.pallas_call(
        body, name=name, grid=(nc,), in_specs=in_specs,
        out_specs=[blk, blk, blk, gblk, gblk, pl.BlockSpec((None, N_GROUPS, 1, GROWS), lambda n: (at(n), 0, 0, 0))],
        out_shape=[jax.ShapeDtypeStruct((s, D), F32)] * 3 + [jax.ShapeDtypeStruct((s, HEAD), F32)] * 2
        + [jax.ShapeDtypeStruct((nc, N_GROUPS, 1, GROWS), F32)],
        scratch_shapes=[pltpu.VMEM((HEADS, HEAD, HEAD), F32)],
        compiler_params=_params(("arbitrary",)),
    )(qkv, qkv, qkv, beta, gcum, grow, states, invs, do)


ATT_TILE = 512
ATT_SCALE = QK_HEAD ** -0.5


def _att_mask(t):
    qpos = lax.broadcasted_iota(jnp.int32, (t, t), 0)
    kpos = lax.broadcasted_iota(jnp.int32, (t, t), 1)
    return (kpos // CHUNK) <= (qpos // CHUNK)


ATT_STRIP = 32


def _att_strip_mask(r, t):
    kpos = lax.broadcasted_iota(jnp.int32, (ATT_STRIP, t), 1)
    return (kpos // CHUNK) <= (r * ATT_STRIP) // CHUNK


def _att_pairs(nb, by_query):
    if by_query:
        pairs = [(i, j) for i in range(nb) for j in range(i + 1)]
    else:
        pairs = [(j, i) for j in range(nb) for i in range(j, nb)]
    return jnp.array([a for a, _ in pairs], jnp.int32), jnp.array([b for _, b in pairs], jnp.int32)


def _attn_fwd(q, k, v, name):
    s = q.shape[0]
    t = min(ATT_TILE, s)
    nb = s // t
    ii, jj = _att_pairs(nb, by_query=True)

    def body(ii_ref, jj_ref, q_ref, k_ref, v_ref, o_ref, lse_ref, m_s, l_s, acc):
        step = pl.program_id(1)
        i, j = ii_ref[step], jj_ref[step]

        @pl.when(j == 0)
        def _():
            m_s[...] = jnp.full_like(m_s, -jnp.inf)
            l_s[...] = jnp.zeros_like(l_s)
            acc[...] = jnp.zeros_like(acc)

        sc = lax.dot_general(q_ref[...], k_ref[...], _NT, preferred_element_type=F32) * ATT_SCALE
        sc = lax.cond(i == j, lambda u: jnp.where(_att_mask(t), u, -jnp.inf), lambda u: u, sc)
        m_new = jnp.maximum(m_s[...], jnp.max(sc, axis=-1, keepdims=True))
        alpha = jnp.exp(m_s[...] - m_new)
        p = jnp.exp(sc - m_new)
        l_s[...] = alpha * l_s[...] + jnp.sum(p, axis=-1, keepdims=True)
        acc[...] = alpha * acc[...] + jnp.dot(p.astype(BF16), v_ref[...], preferred_element_type=F32)
        m_s[...] = m_new

        @pl.when(j == i)
        def _():
            o_ref[...] = acc[...] / l_s[...]
            lse_ref[...] = m_s[...] + jnp.log(l_s[...])

    grid_spec = pltpu.PrefetchScalarGridSpec(
        num_scalar_prefetch=2, grid=(HEADS, len(ii)),
        in_specs=[pl.BlockSpec((t, HEAD_PAD), lambda h, n, ir, jr: (ir[n], h)),
                  pl.BlockSpec((t, HEAD_PAD), lambda h, n, ir, jr: (jr[n], h)),
                  pl.BlockSpec((t, HEAD), lambda h, n, ir, jr: (jr[n], h))],
        out_specs=[pl.BlockSpec((t, HEAD), lambda h, n, ir, jr: (ir[n], h)),
                   pl.BlockSpec((None, t, 1), lambda h, n, ir, jr: (h, ir[n], 0))],
        scratch_shapes=[pltpu.VMEM((t, 1), F32), pltpu.VMEM((t, 1), F32), pltpu.VMEM((t, HEAD), F32)])
    return pl.pallas_call(
        body, name=name, grid_spec=grid_spec,
        out_shape=[jax.ShapeDtypeStruct((s, HEADS * HEAD), F32), jax.ShapeDtypeStruct((HEADS, s, 1), F32)],
        compiler_params=_params(("parallel", "arbitrary")),
    )(ii, jj, q, k, v)


def _attn_bwd(q, k, v, do, o, lse, name, dkv_sum=None):
    s = q.shape[0]
    t = min(ATT_TILE, s)
    nb = s // t
    jj, ii = _att_pairs(nb, by_query=False)
    nsum = 2 if dkv_sum else 0

    def body(jj_ref, ii_ref, q_ref, k_ref, v_ref, do_ref, o_ref, lse_ref, *rest):
        dq_ref, dk_ref, dv_ref, dk_acc, dv_acc, sc_s, dp_s, p_s, ds_s, dl_s = rest[nsum:]
        step = pl.program_id(1)
        i, j = ii_ref[step], jj_ref[step]

        @pl.when(step == 0)
        def _():
            dq_ref[...] = jnp.zeros_like(dq_ref)

        @pl.when(i == j)
        def _():
            dk_acc[...] = jnp.zeros_like(dk_acc)
            dv_acc[...] = jnp.zeros_like(dv_acc)

        do_f = do_ref[...]
        dob = do_f.astype(BF16)
        dl_s[...] = jnp.sum(do_f * o_ref[...], axis=-1, keepdims=True)
        sc_s[...] = lax.dot_general(q_ref[...], k_ref[...], _NT, preferred_element_type=F32)
        dp_s[...] = lax.dot_general(dob, v_ref[...], _NT, preferred_element_type=F32)

        def softmax_strips(diagonal):
            for r in range(t // ATT_STRIP):
                rows = slice(r * ATT_STRIP, (r + 1) * ATT_STRIP)
                p = jnp.exp(sc_s[rows, :] * ATT_SCALE - lse_ref[rows, :])
                if diagonal:
                    p = jnp.where(_att_strip_mask(r, t), p, 0.0)
                p_s[rows, :] = p.astype(BF16)
                ds_s[rows, :] = (p * (dp_s[rows, :] - dl_s[rows, :]) * ATT_SCALE).astype(BF16)

        pl.when(i == j)(functools.partial(softmax_strips, True))
        pl.when(i != j)(functools.partial(softmax_strips, False))
        ds = ds_s[...]
        dv_acc[...] += lax.dot_general(p_s[...], dob, _TN, preferred_element_type=F32)
        dk_acc[...] += lax.dot_general(ds, q_ref[...], _TN, preferred_element_type=F32)
        rows = pl.ds(pl.multiple_of(i * t, t), t)
        dq_ref[rows, :] += jnp.dot(ds, k_ref[...], preferred_element_type=F32)

        @pl.when(i == nb - 1)
        def _():
            dk_ref[...] = dk_acc[...] + rest[0][...] if nsum else dk_acc[...]
            dv_ref[...] = dv_acc[...] + rest[1][...] if nsum else dv_acc[...]

    dk_blk = pl.BlockSpec((t, HEAD_PAD), lambda h, n, jr, ir: (jr[n], h))
    dv_blk = pl.BlockSpec((t, HEAD), lambda h, n, jr, ir: (jr[n], h))
    grid_spec = pltpu.PrefetchScalarGridSpec(
        num_scalar_prefetch=2, grid=(HEADS, len(jj)),
        in_specs=[pl.BlockSpec((t, HEAD_PAD), lambda h, n, jr, ir: (ir[n], h)),
                  pl.BlockSpec((t, HEAD_PAD), lambda h, n, jr, ir: (jr[n], h)),
                  pl.BlockSpec((t, HEAD), lambda h, n, jr, ir: (jr[n], h)),
                  pl.BlockSpec((t, HEAD), lambda h, n, jr, ir: (ir[n], h)),
                  pl.BlockSpec((t, HEAD), lambda h, n, jr, ir: (ir[n], h)),
                  pl.BlockSpec((None, t, 1), lambda h, n, jr, ir: (h, ir[n], 0))] + [dk_blk, dv_blk][:nsum],
        out_specs=[pl.BlockSpec((s, HEAD_PAD), lambda h, n, jr, ir: (0, h)), dk_blk, dv_blk],
        scratch_shapes=[pltpu.VMEM((t, HEAD_PAD), F32), pltpu.VMEM((t, HEAD), F32), pltpu.VMEM((t, t), F32),
                        pltpu.VMEM((t, t), F32), pltpu.VMEM((t, t), BF16), pltpu.VMEM((t, t), BF16),
                        pltpu.VMEM((t, 1), F32)])
    return pl.pallas_call(
        body, name=name, grid_spec=grid_spec,
        out_shape=[jax.ShapeDtypeStruct((s, HEADS * HEAD_PAD), F32)] * 2 + [jax.ShapeDtypeStruct((s, HEADS * HEAD), F32)],
        compiler_params=_params(("parallel", "arbitrary")),
    )(jj, ii, q, k, v, do, o, lse, *(dkv_sum or ()))


def _rope_tables(positions):
    half = ROPE // 2
    inv_freq = ROPE_BASE ** (-jnp.arange(half, dtype=F32) / half)
    ang = positions.astype(F32)[:, None] * inv_freq
    cos, sin = jnp.cos(ang), jnp.sin(ang)
    return jnp.concatenate([cos] * 4, axis=1), jnp.concatenate([-sin, sin] * 2, axis=1)


def _loss_and_grad(y, target, name):
    s = y.shape[0]
    ts = min(ROW_TILE, s)

    def body(y_ref, t_ref, dy_ref, l_ref):
        e = y_ref[...] - t_ref[...]
        dy_ref[...] = e * (1.0 / D)
        part = jnp.sum(jnp.sum(e * e, axis=-1, keepdims=True) * (0.5 / D), axis=0, keepdims=True)
        part = part * jnp.ones((1, 128), F32)

        @pl.when(pl.program_id(0) == 0)
        def _():
            l_ref[...] = part

        @pl.when(pl.program_id(0) > 0)
        def _():
            l_ref[...] += part

    return pl.pallas_call(
        body, name=name, grid=(s // ts,),
        in_specs=[pl.BlockSpec((ts, D), lambda i: (i, 0))] * 2,
        out_specs=[pl.BlockSpec((ts, D), lambda i: (i, 0)), pl.BlockSpec((1, 128), lambda i: (0, 0))],
        out_shape=[jax.ShapeDtypeStruct((s, D), F32), jax.ShapeDtypeStruct((1, 128), F32)],
        compiler_params=_params(("arbitrary",)),
    )(y, target)


ANY = pl.BlockSpec(memory_space=pl.ANY)


def _all_gather(shard, name):
    def body(x_ref, out_ref, send_sems, recv_sems, local_sem):
        x, y, c = lax.axis_index("x"), lax.axis_index("y"), lax.axis_index("c")
        me, sibling = (x, y, c), (x, y, 1 - c)
        chips = [(1 - x, y), (x, 1 - y), (1 - x, 1 - y)]

        def rows(px, py, pc):
            return out_ref.at[4 * px + 2 * py + pc]

        def copy(k, block, to, src=None):
            return pltpu.make_async_remote_copy(
                src_ref=rows(*block) if src is None else src, dst_ref=rows(*block),
                send_sem=send_sems.at[k], recv_sem=recv_sems.at[k], device_id=to, device_id_type=MESH)

        mine = pltpu.make_async_copy(x_ref, rows(*me), local_sem)
        mine.start()
        first = [copy(0, me, sibling, src=x_ref)]
        first += [copy(1 + j, me, (*chip, c), src=x_ref) for j, chip in enumerate(chips)]
        for cp in first:
            cp.start()
        passed = [copy(4 + j, (*chip, c), sibling) for j, chip in enumerate(chips)]
        for j, chip in enumerate(chips):
            copy(1 + j, (*chip, c), me).wait_recv()
            passed[j].start()
        copy(0, sibling, me).wait_recv()
        for j, chip in enumerate(chips):
            copy(4 + j, (*chip, 1 - c), me).wait_recv()
        for cp in first + passed:
            cp.wait_send()
        mine.wait()

    return pl.pallas_call(
        body, name=name, out_shape=jax.ShapeDtypeStruct((N_DEV,) + shard.shape, shard.dtype),
        in_specs=[ANY], out_specs=ANY,
        scratch_shapes=[pltpu.SemaphoreType.DMA((7,)), pltpu.SemaphoreType.DMA((7,)), pltpu.SemaphoreType.DMA],
    )(shard)


def _exchange(blocks, name):
    def body(x_ref, out_ref, send_sems, recv_sems, local_sem):
        x, y, c = lax.axis_index("x"), lax.axis_index("y"), lax.axis_index("c")
        me = 4 * x + 2 * y + c
        mine = pltpu.make_async_copy(x_ref.at[me], out_ref.at[me], local_sem)
        mine.start()
        copies = []
        for k in range(1, N_DEV):
            px = 1 - x if k & 4 else x
            py = 1 - y if k & 2 else y
            pc = 1 - c if k & 1 else c
            peer = 4 * px + 2 * py + pc
            cp = pltpu.make_async_remote_copy(
                src_ref=x_ref.at[peer], dst_ref=out_ref.at[me], send_sem=send_sems.at[k - 1],
                recv_sem=recv_sems.at[k - 1], device_id=(px, py, pc), device_id_type=MESH)
            cp.start()
            copies.append((cp, pltpu.make_async_remote_copy(
                src_ref=x_ref.at[peer], dst_ref=out_ref.at[peer], send_sem=send_sems.at[k - 1],
                recv_sem=recv_sems.at[k - 1], device_id=(px, py, pc), device_id_type=MESH)))
        for cp, landing in copies:
            landing.wait_recv()
        for cp, landing in copies:
            cp.wait_send()
        mine.wait()

    return pl.pallas_call(
        body, name=name, out_shape=jax.ShapeDtypeStruct(blocks.shape, blocks.dtype),
        in_specs=[ANY], out_specs=ANY,
        scratch_shapes=[pltpu.SemaphoreType.DMA((7,)), pltpu.SemaphoreType.DMA((7,)), pltpu.SemaphoreType.DMA],
    )(blocks)


HBM = pl.BlockSpec(memory_space=pltpu.HBM)
SEM = pl.BlockSpec(memory_space=pltpu.SEMAPHORE)
EFFECT = pltpu.SideEffectType.DATAFLOW_SIDE_EFFECTING


def _peers():
    x, y, c = lax.axis_index("x"), lax.axis_index("y"), lax.axis_index("c")
    peers = []
    for k in range(1, N_DEV):
        px = 1 - x if k & 4 else x
        py = 1 - y if k & 2 else y
        pc = 1 - c if k & 1 else c
        peers.append(((px, py, pc), 4 * px + 2 * py + pc))
    return 4 * x + 2 * y + c, peers


def _send_start(srcs, name, gather):
    n = len(srcs)
    lands = [((N_DEV,) + s.shape) if gather else s.shape for s in srcs]

    def body(*refs):
        src_refs, land_refs = refs[:n], refs[n:2 * n]
        send_sems, recv_sems, token = refs[2 * n], refs[2 * n + 1], refs[-1]
        me, peers = _peers()
        for i in range(n):
            for k, (dev, idx) in enumerate(peers):
                pltpu.make_async_remote_copy(
                    src_ref=src_refs[i] if gather else src_refs[i].at[idx], dst_ref=land_refs[i].at[me],
                    send_sem=send_sems.at[7 * i + k], recv_sem=recv_sems.at[7 * i + k], device_id=dev,
                    device_id_type=MESH).start()
        token[...] = jnp.zeros_like(token)

    res = pl.pallas_call(
        body, name=name,
        out_shape=(pltpu.SemaphoreType.DMA((7 * n,)), pltpu.SemaphoreType.DMA((7 * n,)),
                   *[pltpu.HBM(s.shape, s.dtype) for s in srcs],
                   *[pltpu.HBM(shape, s.dtype) for shape, s in zip(lands, srcs)],
                   jax.ShapeDtypeStruct((8, 128), F32)),
        in_specs=(HBM,) * (2 * n), out_specs=(SEM, SEM) + (HBM,) * (2 * n) + (pl.BlockSpec(memory_space=pltpu.VMEM),),
        input_output_aliases={i: 2 + i for i in range(2 * n)},
        compiler_params=pltpu.CompilerParams(has_side_effects=EFFECT),
    )(*[pltpu.with_memory_space_constraint(s, pltpu.HBM) for s in srcs],
      *[pltpu.with_memory_space_constraint(lax.empty(shape, s.dtype), pltpu.HBM) for shape, s in zip(lands, srcs)])
    return dict(sems=res[:2], srcs=res[2:2 + n], lands=res[2 + n:2 + 2 * n], token=res[-1])


def _send_wait(handle, after, name, gather):
    n = len(handle["srcs"])

    def body(*refs):
        src_refs, land_refs = refs[:n], refs[n:2 * n]
        send_sems, recv_sems = refs[2 * n], refs[2 * n + 1]
        me, peers = _peers()
        for i in range(n):
            for k, (dev, idx) in enumerate(peers):
                cp = pltpu.make_async_remote_copy(
                    src_ref=src_refs[i] if gather else src_refs[i].at[idx], dst_ref=land_refs[i].at[idx],
                    send_sem=send_sems.at[7 * i + k], recv_sem=recv_sems.at[7 * i + k], device_id=dev,
                    device_id_type=MESH)
                cp.wait_send()
                cp.wait_recv()

    both = list(handle["srcs"]) + list(handle["lands"])
    res = pl.pallas_call(
        body, name=name, out_shape=tuple(pltpu.HBM(t.shape, t.dtype) for t in both),
        in_specs=(HBM,) * (2 * n) + (SEM, SEM, pl.BlockSpec(memory_space=pl.ANY)), out_specs=(HBM,) * (2 * n),
        input_output_aliases={i: i for i in range(2 * n)},
        compiler_params=pltpu.CompilerParams(has_side_effects=EFFECT),
    )(*both, *handle["sems"], after)
    return res[:n], res[n:]


def _adamw(parts, w, m, v, name, tr=128):
    pieces = len(parts)
    n, r, wd = parts[0].shape
    tr = next((t for t in (tr, 64, 32, 16) if r % t == 0), r)
    nrt = r // tr

    def body(*refs):
        w_ref, m_ref, v_ref, g_ref, d_ref, nm_ref, nv_ref = refs[pieces:]

        def update(p_ref):
            g = p_ref[0].astype(F32)
            for k in range(1, n):
                g = g + p_ref[k].astype(F32)
            m_new = B1 * m_ref[...] + (1.0 - B1) * g
            v_new = B2 * v_ref[...] + (1.0 - B2) * (g * g)
            m_hat = m_new / (1.0 - B1 ** STEP)
            v_hat = v_new / (1.0 - B2 ** STEP)
            g_ref[...] = g
            d_ref[...] = -LR * (m_hat / (jnp.sqrt(v_hat) + ADAM_EPS) + WD * w_ref[...])
            nm_ref[...] = m_new
            nv_ref[...] = v_new

        for p in range(pieces):
            pl.when(pl.program_id(0) == p)(functools.partial(update, refs[p]))

    part_spec = lambda p: pl.BlockSpec((n, tr, wd), lambda l, i: (0, jnp.clip(i + (l - p) * nrt, 0, nrt - 1), 0))
    blk = pl.BlockSpec((tr, wd), lambda l, i: (l * nrt + i, 0))
    return pl.pallas_call(
        body, name=name, grid=(pieces, nrt),
        in_specs=[part_spec(p) for p in range(pieces)] + [blk, blk, blk],
        out_specs=[blk] * 4, out_shape=[jax.ShapeDtypeStruct((pieces * r, wd), F32)] * 4,
        compiler_params=_params(("arbitrary", "arbitrary")),
    )(*parts, w, m, v)


def _outer8(ct, dm, name):
    k, n = ct.shape[0], dm.shape[1]

    def body(c_ref, d_ref, o_ref):
        cv, dv = c_ref[...], d_ref[...]
        acc = cv[:, 0:1] * dv[0:1, :]
        for s in range(1, N_DEV):
            acc = acc + cv[:, s:s + 1] * dv[s:s + 1, :]
        o_ref[...] = acc

    tk = 256
    return pl.pallas_call(
        body, name=name, grid=(k // tk,),
        in_specs=[pl.BlockSpec((tk, N_DEV), lambda i: (i, 0)), pl.BlockSpec((N_DEV, n), lambda i: (0, 0))],
        out_specs=pl.BlockSpec((tk, n), lambda i: (i, 0)), out_shape=jax.ShapeDtypeStruct((k, n), F32),
        compiler_params=_params(("parallel",)),
    )(ct, dm)


FULL = (0, D)
C128 = (0, 128)
HEAD_NOPE = [(h * HEAD_PAD, NOPE) for h in range(HEADS)]
HEAD_ROPE = [(h * HEAD_PAD + NOPE, 128) for h in range(HEADS)]
HEAD_ALL = [(h * HEAD_PAD, HEAD_PAD) for h in range(HEADS)]
HEAD_V = [(h * HEAD, HEAD) for h in range(HEADS)]


def _modulate(x, p, ties=()):
    return _rowwise_fwd(_modulate_fn, [(x, FULL)], [p["gain"], p["scale"], p["shift"]], [(D, BF16, FULL)], "modulate",
                        ties=ties)[0]


def _residual_bwd(y, gm, dxn):
    return _rowwise_bwd(_gate_only_fn, [(y, FULL)], [gm], [(D, F32, FULL)], [dxn], [(0, FULL)], [(D, BF16)],
                        "residual_bwd")


def _modulate_bwd(x, p, dh, dx_in, prev=None):
    pars = [p["gain"], p["scale"], p["shift"]]
    if prev is None:
        return list(_rowwise_bwd(_modulate_fn, [(x, FULL)], pars, [(D, BF16, FULL)], [dh], [(0, FULL)], [(D, F32)],
                                 "modulate_bwd", add={0: dx_in})) + [None]
    s = x.shape[0]
    ts = min(ROW_TILE, s)

    def body(x_ref, g_ref, sc_ref, sh_ref, dh_ref, din_ref, y_ref, gm_ref, dx_ref, dy_ref, dg_ref, dsc_ref, dsh_ref,
             dgm_ref):
        _, vjp = jax.vjp(lambda *t: _modulate_fn(0, *t)[0], x_ref[...], g_ref[...], sc_ref[...], sh_ref[...])
        dxm, dg, dsc, dsh = vjp(dh_ref[...])
        dx = dxm + din_ref[...]
        dx_ref[...] = dx
        dy_ref[...] = (gm_ref[...] * dx).astype(BF16)
        sums = (dg, dsc, dsh, jnp.sum(dx * y_ref[...], axis=0, keepdims=True))
        first = pl.program_id(0) == 0
        for ref, val in zip((dg_ref, dsc_ref, dsh_ref, dgm_ref), sums):
            @pl.when(first)
            def _(ref=ref, val=val):
                ref[...] = val

            @pl.when(jnp.logical_not(first))
            def _(ref=ref, val=val):
                ref[...] += val

    blk = pl.BlockSpec((ts, D), lambda i: (i, 0))
    vec = pl.BlockSpec((1, D), lambda i: (0, 0))
    dx, dy, dg, dsc, dsh, dgm = pl.pallas_call(
        body, name="modulate_bwd_chain", grid=(s // ts,),
        in_specs=[blk, vec, vec, vec, blk, blk, blk, vec], out_specs=[blk, blk, vec, vec, vec, vec],
        out_shape=[jax.ShapeDtypeStruct((s, D), F32), jax.ShapeDtypeStruct((s, D), BF16)]
        + [jax.ShapeDtypeStruct((1, D), F32)] * 4,
        compiler_params=_params(("arbitrary",)),
    )(x, *pars, dh, dx_in, prev[0], prev[1])
    return [dx, dg, dsc, dsh, (dy, dgm)]


def _out_proj(a, w, x, p, nxt, name, **kw):
    res = _matmul([(a, w)], "nn", name, out_dtype=BF16, resid=(x, p["gm"]) + tuple(nxt or ()), **kw)
    return res[1], res[0], (res[2] if nxt else None)


def _ffn_fwd(x, p, ties=(), h=None, nxt=None):
    if h is None:
        h, ties = _modulate(x, p, ties), ()
    gate, up, act = _ffn_in(h, p["w_in"], "ffn_in", ties=ties)
    res = _ffn_out(act, p["wo"], (x, p["gm"]) + tuple(nxt or ()), "ffn_out")
    return res[1], dict(x=x, h=h, gate=gate, up=up, act=act, y=res[0]), (res[2] if nxt else None)


def _ffn_bwd(t, p, dxn, res=None, prev=None, ties=()):
    dy, dgm = res or _residual_bwd(t["y"], p["gm"], dxn)
    dgate, dup = _ffn_bwd_act(dy, p["wo"], t["gate"], t["up"], "ffn_bwd_act", ties=ties)
    dwo = _ffn_dwo(t["act"], dy, "ffn_dwo", ties=ties)
    dh = _ffn_dh(dgate, dup, p["w_in"], "ffn_dh")
    dwi = _ffn_dwi(t["h"], dgate, dup, "ffn_dwi")
    dx, dgain, dscale, dshift, res_prev = _modulate_bwd(t["x"], p, dh, dxn, prev)
    return dx, dict(gain=dgain, scale=dscale, shift=dshift, gm=dgm, w_in=dwi, wo=dwo), res_prev


def _pad128(t):
    return jnp.pad(t, ((0, 0), (0, 128 - t.shape[1])))


def _gdn_fwd(x, p, ties=(), h=None, nxt=None):
    s = x.shape[0]
    if h is None:
        h, ties = _modulate(x, p, ties), ()
    pm = _mm(h, p["w_main"], "nn", "gdn_proj", ties=ties)
    tail = _mm(h, p["w_tail"], "nn", "gdn_proj_tail", ties=ties)
    qkv = _gdn_conv_fwd(pm, p["conv_w"], "gdn_conv")
    beta, gcum = _rowwise_fwd(_gdn_gates_fn, [(tail, C128), (tail, (128, 128))], [p["a_log"], p["dt_bias"]],
                              [(128, F32, C128)] * 2, "gdn_gates")
    grow = gcum[:, :HEADS].reshape(s // CHUNK, CHUNK, N_GROUPS, GROUP).transpose(0, 2, 3, 1)
    grow = grow.reshape(s // CHUNK, N_GROUPS, 1, GROWS)
    o, states, invs = _gdn_scan_fwd(qkv, beta, gcum, grow, "gdn_scan")
    on, = _rowwise_fwd(_gdn_outnorm_fn, [(o, HEAD_V), (pm, [(3 * D + h_ * HEAD, HEAD) for h_ in range(HEADS)])],
                       [p["norm_g"]], [(D, BF16, HEAD_V)], "gdn_outnorm", groups=HEADS)
    xn, y, hn = _out_proj(on, p["w_out"], x, p, nxt, "mix_out", tm=512)
    t = dict(x=x, h=h, pm=pm, tail=tail, qkv=qkv, beta=beta, gcum=gcum, grow=grow, o=o, states=states, invs=invs,
             on=on, y=y)
    return xn, t, hn


def _gdn_bwd(t, p, dxn, res=None, prev=None, ties=()):
    s = dxn.shape[0]
    zc = [(3 * D + h_ * HEAD, HEAD) for h_ in range(HEADS)]
    dy, dgm = res or _residual_bwd(t["y"], p["gm"], dxn)
    dw_out = _mm(t["on"], dy, "tn", "mix_dwo", ties=ties)
    don = _mm(dy, p["w_out"], "nt", "mix_dout", ties=ties)
    do, dz, dnorm_g = _rowwise_bwd(_gdn_outnorm_fn, [(t["o"], HEAD_V), (t["pm"], zc)], [p["norm_g"]],
                                   [(D, BF16, HEAD_V)], [don], [(0, HEAD_V), (1, HEAD_V)], [(D, F32), (D, BF16)],
                                   "gdn_outnorm_bwd", groups=HEADS)
    dq, dk, dv, dbeta, dg, dgr = _gdn_scan_bwd(t["qkv"], t["beta"], t["gcum"], t["grow"], t["states"], t["invs"], do,
                                               "gdn_scan_bwd")
    dg = dg + _pad128(dgr.reshape(s // CHUNK, N_GROUPS, GROUP, CHUNK).transpose(0, 3, 1, 2).reshape(s, HEADS))
    dtail, da_log, ddt = _rowwise_bwd(_gdn_gates_fn, [(t["tail"], C128), (t["tail"], (128, 128))],
                                      [p["a_log"], p["dt_bias"]], [(128, F32, C128)] * 2, [dbeta, dg],
                                      [(0, C128), (0, (128, 128))], [(256, F32)], "gdn_gates_bwd")
    dxs, dcw = [], []
    for part, d in enumerate((dq, dk, dv)):
        dx_, dw_ = _gdn_conv_bwd(t["pm"], p["conv_w"], d, part, "gdn_conv_bwd")
        dxs.append(dx_)
        dcw.append(dw_)
    pieces = dxs + [dz]
    dh = _matmul([(d, p["w_main"]) for d in pieces] + [(dtail, p["w_tail"])], "nt", "gdn_dh",
                 boffs=[0, D, 2 * D, 3 * D, 0], tk=512)
    dw_main = [_mm(t["h"], d, "tn", "gdn_dwi") for d in pieces]
    dw_tail = _mm(t["h"], dtail, "tn", "gdn_dwi_tail")
    dx, dgain, dscale, dshift, res_prev = _modulate_bwd(t["x"], p, dh, dxn, prev)
    return dx, dict(gain=dgain, scale=dscale, shift=dshift, gm=dgm, w_main=jnp.concatenate(dw_main, axis=1),
                    w_tail=dw_tail, conv_w=jnp.concatenate(dcw, axis=1), a_log=da_log, dt_bias=ddt,
                    norm_g=dnorm_g, w_out=dw_out), res_prev


def _q_rows(q2, cosf, sins):
    return [(q2, HEAD_NOPE), (q2, HEAD_ROPE), (cosf, C128), (sins, C128)]


def _mla_fwd(x, p, kv, ties=(), h=None, nxt=None):
    if h is None:
        h, ties = _modulate(x, p, ties), ()
    cq = _mm(h, p["w_dq"], "nn", "mla_dq", ties=ties)
    cqn, = _rowwise_fwd(_rms_fn, [(cq, (0, Q_LORA))], [p["q_lora_g"]], [(Q_LORA, BF16, (0, Q_LORA))], "mla_qlora_norm")
    q2 = _mm(cqn, p["w_uq"], "nn", "mla_uq")
    qn, = _rowwise_fwd(_q_norm_rope_fn, _q_rows(q2, kv["cosf"], kv["sins"]), [p["q_gn"], p["q_gr"]],
                       [(HEADS * HEAD_PAD, BF16, HEAD_ALL)], "mla_q_norm", groups=HEADS)
    o, lse = _attn_fwd(qn, kv["kn"], kv["vb"], "mla_attn")
    xn, y, hn = _out_proj(o, p["w_out"], x, p, nxt, "mix_out", tm=512)
    return xn, dict(x=x, h=h, cq=cq, cqn=cqn, q2=q2, qn=qn, o=o, lse=lse, y=y), hn


def _mla_bwd(t, p, kv, dxn, res=None, prev=None, ties=(), dkv_sum=None):
    dy, dgm = res or _residual_bwd(t["y"], p["gm"], dxn)
    dw_out = _mm(t["o"], dy, "tn", "mix_dwo", ties=ties)
    do = _mm(dy, p["w_out"], "nt", "mix_dout", ties=ties)
    dq, dk, dv = _attn_bwd(t["qn"], kv["kn"], kv["vb"], do, t["o"], t["lse"], "mla_attn_bwd", dkv_sum)
    dq2, dq_gn, dq_gr = _rowwise_bwd(_q_norm_rope_fn, _q_rows(t["q2"], kv["cosf"], kv["sins"]), [p["q_gn"], p["q_gr"]],
                                     [(HEADS * HEAD_PAD, BF16, HEAD_ALL)], [dq],
                                     [(0, HEAD_NOPE), (0, HEAD_ROPE), None, None], [(HEADS * HEAD_PAD, BF16)],
                                     "mla_q_norm_bwd", groups=HEADS)
    dw_uq = _mm(t["cqn"], dq2, "tn", "mla_dwuq")
    dcqn = _mm(dq2, p["w_uq"], "nt", "mla_dcq")
    dcq, dq_lora_g = _rowwise_bwd(_rms_fn, [(t["cq"], (0, Q_LORA))], [p["q_lora_g"]], [(Q_LORA, BF16, (0, Q_LORA))],
                                  [dcqn], [(0, (0, Q_LORA))], [(Q_LORA, BF16)], "mla_qlora_norm_bwd")
    dw_dq = _mm(t["h"], dcq, "tn", "mla_dwdq")
    dh = _mm(dcq, p["w_dq"], "nt", "mla_dh")
    dx, dgain, dscale, dshift, res_prev = _modulate_bwd(t["x"], p, dh, dxn, prev)
    grads = dict(gain=dgain, scale=dscale, shift=dshift, gm=dgm, w_dq=dw_dq, q_lora_g=dq_lora_g, w_uq=dw_uq,
                 q_gn=dq_gn, q_gr=dq_gr, w_out=dw_out)
    return dx, grads, res_prev, dk, dv


def _k_rows(kvp, ckv, cosf, sins):
    return [(kvp, HEAD_NOPE), (kvp, HEAD_ROPE), (ckv, (KV_LORA, 128)), (cosf, C128), (sins, C128)]


def _kv_fwd(x, p, cosf, sins):
    h = _modulate(x, p)
    ckv = _mm(h, p["w_dkv"], "nn", "kv_down")
    lat, = _rowwise_fwd(_rms_fn, [(ckv, (0, KV_LORA))], [p["kv_g"]], [(KV_LORA, BF16, (0, KV_LORA))], "kv_norm")
    kvp = _mm(lat, p["w_ukv"], "nn", "kv_up")
    kn, vb = _rowwise_fwd(_k_norm_rope_fn, _k_rows(kvp, ckv, cosf, sins), [p["k_gn"], p["k_gr"]],
                          [(HEADS * HEAD_PAD, BF16, HEAD_ALL), (HEADS * HEAD, BF16, HEAD_V)], "kv_k_norm",
                          groups=HEADS)
    return dict(x=x, h=h, ckv=ckv, lat=lat, kvp=kvp, kn=kn, vb=vb, cosf=cosf, sins=sins)


def _kv_bwd(t, p, dk, dv, dx_in, prev):
    dkvp, drope, dk_gn, dk_gr = _rowwise_bwd(
        _k_norm_rope_fn, _k_rows(t["kvp"], t["ckv"], t["cosf"], t["sins"]), [p["k_gn"], p["k_gr"]],
        [(HEADS * HEAD_PAD, BF16, HEAD_ALL), (HEADS * HEAD, BF16, HEAD_V)], [dk, dv],
        [(0, HEAD_NOPE), (0, HEAD_ROPE), (1, C128), None, None], [(HEADS * HEAD_PAD, BF16), (128, F32)],
        "kv_k_norm_bwd", groups=HEADS)
    dw_ukv = _mm(t["lat"], dkvp, "tn", "kv_dwukv")
    dlat = _mm(dkvp, p["w_ukv"], "nt", "kv_dlat")
    dckv, dkv_g = _rowwise_bwd(_rms_fn, [(t["ckv"], (0, KV_LORA))], [p["kv_g"]], [(KV_LORA, BF16, (0, KV_LORA))],
                               [dlat], [(0, (0, KV_LORA))], [(KV_LORA, F32)], "kv_norm_bwd")
    dw_dkv = jnp.concatenate([_mm(t["h"], dckv, "tn", "kv_dwdkv"), _mm(t["h"], drope, "tn", "kv_dwdkv_rope")], axis=1)
    dh = _matmul([(dckv, p["w_dkv"]), (drope, p["w_dkv"])], "nt", "kv_dh", boffs=[0, KV_LORA])
    dx, dgain, dscale, dshift, res_prev = _modulate_bwd(t["x"], p, dh, dx_in, prev)
    return dx, dict(gain=dgain, scale=dscale, shift=dshift, w_dkv=dw_dkv, kv_g=dkv_g, w_ukv=dw_ukv, k_gn=dk_gn,
                    k_gr=dk_gr), res_prev


WEIGHTS = ["ada_w", "ada_b", "norm_g", "ffn_w_in", "ffn_w_out", "gdn_w_in", "gdn_conv_w", "gdn_a_log", "gdn_dt_bias",
           "gdn_norm_g", "gdn_w_out", "kv_ada_w", "kv_ada_b", "kv_norm_g", "mla_w_dkv", "mla_kv_norm_g", "mla_w_ukv",
           "mla_k_norm_g", "mla_w_dq", "mla_q_lora_norm_g", "mla_w_uq", "mla_q_norm_g", "mla_w_out"]
SMALL = [("ada_b", 4 * N_MOD * D), ("kv_ada_b", 2 * D), ("norm_g", DEPTH * 3 * D), ("gdn_conv_w", N_A * CONV_K * 3 * D),
         ("gdn_a_log", N_A * HEADS), ("gdn_dt_bias", N_A * HEADS), ("gdn_norm_g", N_A * HEAD), ("kv_norm_g", D),
         ("mla_kv_norm_g", KV_LORA), ("mla_k_norm_g", QK_HEAD), ("mla_q_lora_norm_g", 2 * Q_LORA),
         ("mla_q_norm_g", 2 * QK_HEAD)]
SMALL_REPLICATED = [n for n, _ in SMALL if n not in ("norm_g", "gdn_conv_w")]


def _silu_fn(g, t):
    return (_silu(t),)


def _dup_rope(t):
    return jnp.concatenate([t[..., :NOPE], t[..., NOPE:], t[..., NOPE:]], axis=-1)


def _fold_rope(t):
    return jnp.concatenate([t[..., :NOPE], t[..., NOPE:QK_HEAD] + t[..., QK_HEAD:]], axis=-1)


def _pack(pieces, rows):
    flat = jnp.concatenate([p.reshape(-1).astype(F32) for p in pieces])
    return jnp.pad(flat, (0, rows * 128 - flat.shape[0])).reshape(rows, 128)


def _step(a):
    me = 4 * lax.axis_index("x") + 2 * lax.axis_index("y") + lax.axis_index("c")
    x = a["x"][0]
    cosf, sins = _rope_tables(a["positions"][0])

    n_in = 2 * D_FF // N_DEV
    n_gdn = (4 * D + 2 * HEADS) // N_DEV
    AHEAD = 2

    stages = [(l, part) for l in range(DEPTH) for part in range(3)]

    def stage_shards(l, part):
        if part != 1:
            sh = {"ffn_w_in": a["ffn_w_in"][l, part // 2], "ffn_w_out": a["ffn_w_out"][l, part // 2]}
            if part == 2 and l == N_A - 1:
                sh.update(mla_w_dkv=a["mla_w_dkv"], mla_w_ukv=a["mla_w_ukv"])
            return sh
        if l < N_A:
            return {"gdn_w_in": a["gdn_w_in"][l], "gdn_w_out": a["gdn_w_out"][l]}
        j = l - N_A
        return {"mla_w_dq": a["mla_w_dq"][j], "mla_w_uq": a["mla_w_uq"][j], "mla_w_out": a["mla_w_out"][j]}

    def zero_of(t):
        return jnp.minimum(jnp.abs(t[(0,) * t.ndim].astype(F32)), 0.0)

    def start_stage(l, part, tie):
        sh = stage_shards(l, part)
        return list(sh), _send_start([(w + tie).astype(BF16) for w in sh.values()], f"fetch_start_{l}_{part}", gather=True)

    def finish_stage(l, part, names, handle, after):
        srcs, lands = _send_wait(handle, after, f"fetch_wait_{l}_{part}", gather=True)
        return {n: lax.dynamic_update_slice(land, src[None], (me, 0, 0)) for n, src, land in zip(names, srcs, lands)}

    n_cw, n_ng = N_A * CONV_K * 3 * HEAD, DEPTH * 3 * HEAD
    small_all = _all_gather(_pack([a["gdn_conv_w"], a["norm_g"], a["c"]], 44), "gather_small").reshape(N_DEV, -1)
    conv_w = small_all[:, :n_cw].reshape(N_DEV, N_A, CONV_K, 3 * HEAD).transpose(1, 2, 0, 3).reshape(N_A, CONV_K, 3 * D)
    norm_g = small_all[:, n_cw:n_cw + n_ng].reshape(N_DEV, DEPTH, 3, HEAD).transpose(1, 2, 0, 3).reshape(DEPTH, 3, D)
    c_all = small_all[:, n_cw + n_ng:n_cw + n_ng + D]

    c_act, = _rowwise_fwd(_silu_fn, [(c_all, FULL)], [], [(D, F32, FULL)], "c_act")
    n_ada = N_MOD * D // N_DEV
    parts = [_mm(c_act, a["ada_w"][l], "nn", "mod_proj") for l in range(DEPTH)]
    parts.append(_mm(c_act, a["kv_ada_w"], "nn", "mod_proj_kv"))
    mod_recv = _exchange(jnp.concatenate(parts, axis=1)[:, None, :], "exchange_mod")[:, 0]
    mod = mod_recv[:, :DEPTH * n_ada].reshape(N_DEV, DEPTH, n_ada).transpose(1, 0, 2).reshape(DEPTH, N_MOD * D)
    mod = (mod + a["ada_b"]).reshape(DEPTH, N_MOD, D)
    kvmod = mod_recv[:, DEPTH * n_ada:].reshape(2 * D) + a["kv_ada_b"]

    def row(v):
        return v[None]

    def ffn_params(l, i, w):
        k = 0 if i == 0 else 6
        return dict(gain=row(norm_g[l, 0 if i == 0 else 2]), shift=row(mod[l, k]), scale=row(mod[l, k + 1]),
                    gm=0.5 * row(mod[l, k + 2]), w_in=w["ffn_w_in"], wo=w["ffn_w_out"].reshape(D_FF, D))

    def gdn_params(l, w):
        w_in = w["gdn_w_in"].transpose(1, 0, 2).reshape(D, 4 * D + 2 * HEADS)
        pad = lambda t: jnp.pad(t, ((0, 0), (0, 128 - HEADS)))
        return dict(gain=row(norm_g[l, 1]), shift=row(mod[l, 3]), scale=row(mod[l, 4]), gm=row(mod[l, 5]),
                    w_main=w_in[:, :4 * D],
                    w_tail=jnp.concatenate([pad(w_in[:, 4 * D:4 * D + HEADS]), pad(w_in[:, 4 * D + HEADS:])], axis=1),
                    conv_w=conv_w[l], a_log=_pad128(row(a["gdn_a_log"][l])), dt_bias=_pad128(row(a["gdn_dt_bias"][l])),
                    norm_g=row(a["gdn_norm_g"][l]), w_out=w["gdn_w_out"].reshape(D, D))

    def mla_params(l, w):
        j = l - N_A
        uq = w["mla_w_uq"].transpose(1, 0, 2)
        qg = _dup_rope(a["mla_q_norm_g"][j])
        return dict(gain=row(norm_g[l, 1]), shift=row(mod[l, 3]), scale=row(mod[l, 4]), gm=row(mod[l, 5]),
                    w_dq=w["mla_w_dq"].reshape(D, Q_LORA), q_lora_g=row(a["mla_q_lora_norm_g"][j]),
                    w_uq=_dup_rope(uq).reshape(Q_LORA, HEADS * HEAD_PAD), q_gn=row(qg[:NOPE]), q_gr=row(qg[NOPE:]),
                    w_out=w["mla_w_out"].reshape(D, D))

    def kv_params(w):
        w_dkv = w["mla_w_dkv"].reshape(D, KV_LORA + ROPE)
        kg = _dup_rope(a["mla_k_norm_g"])
        return dict(gain=row(a["kv_norm_g"]), shift=row(kvmod[:D]), scale=row(kvmod[D:]),
                    w_dkv=jnp.concatenate([w_dkv, w_dkv[:, KV_LORA:]], axis=1), kv_g=row(a["mla_kv_norm_g"]),
                    w_ukv=w["mla_w_ukv"].transpose(1, 0, 2).reshape(KV_LORA, HEADS * 2 * HEAD), k_gn=row(kg[:NOPE]),
                    k_gr=row(kg[NOPE:]))

    tapes, kv, kv_p, h = [[] for _ in range(DEPTH)], None, None, None
    first = {name: _all_gather((w + zero_of(mod)).astype(BF16), "fetch_first_" + name)
             for name, w in stage_shards(0, 0).items()}
    pending = []
    for l, part in stages[1:1 + AHEAD]:
        tie = pending[-1][1]["token"][0, 0] if pending else zero_of(first["ffn_w_out"])
        pending.append(start_stage(l, part, tie))
    for n, (l, part) in enumerate(stages):
        if n == 0:
            w, ties = first, tuple(h["token"] for _, h in pending)
        else:
            names, handle = pending.pop(0)
            w = finish_stage(l, part, names, handle, x)
            ties = ()
            if n + AHEAD < len(stages):
                pending.append(start_stage(*stages[n + AHEAD], zero_of(w[names[0]])))
                ties = (pending[-1][1]["token"],)
        nxt = None
        if n + 1 < len(stages):
            l2, part2 = stages[n + 1]
            k2 = 3 * part2
            nxt = (row(norm_g[l2, part2]), row(mod[l2, k2 + 1]), row(mod[l2, k2]))
        if part != 1:
            p = ffn_params(l, part // 2, w)
            x, t, h = _ffn_fwd(x, p, ties, h, nxt)
        else:
            p = gdn_params(l, w) if l < N_A else mla_params(l, w)
            x, t, h = _gdn_fwd(x, p, ties, h, nxt) if l < N_A else _mla_fwd(x, p, kv, ties, h, nxt)
        tapes[l] += [p, t]
        if part == 2 and l == N_A - 1:
            kv_p = kv_params(w)
            kv = _kv_fwd(x, kv_p, cosf, sins)
    dx, loss_blk = _loss_and_grad(x, a["loss_target"][0], "loss")
    loss = lax.psum(loss_blk[0, 0], ("x", "y", "c"))

    def by_cols(g, n):
        return g.reshape(g.shape[0], -1, n).transpose(1, 0, 2)

    def ffn_blocks(g):
        return {"ffn_w_in": g["w_in"], "ffn_w_out": g["wo"].reshape(N_DEV, D_FF // N_DEV, D)}

    def mixer_blocks(l, g):
        if l < N_A:
            full = jnp.concatenate([g["w_main"], g["w_tail"][:, :HEADS], g["w_tail"][:, 128:128 + HEADS]], axis=1)
            return {"gdn_w_in": by_cols(full, n_gdn), "gdn_w_out": g["w_out"].reshape(N_DEV, D // N_DEV, D)}
        return {"mla_w_dq": g["w_dq"].reshape(N_DEV, D // N_DEV, Q_LORA),
                "mla_w_uq": _fold_rope(g["w_uq"].reshape(Q_LORA, HEADS, HEAD_PAD)).transpose(1, 0, 2),
                "mla_w_out": g["w_out"].reshape(N_DEV, D // N_DEV, D)}

    sent = []

    def send(key, blocks, tie=0.0):
        handle = _send_start([(b + tie).astype(BF16) for b in blocks.values()], "grad_start_" + "_".join(map(str, key)),
                             gather=False)
        sent.append((key, list(blocks), handle))
        return (handle["token"],)

    grads = [None] * DEPTH
    dk_sum = dv_sum = kv_grads = res = None
    ties = ()
    for l in reversed(range(DEPTH)):
        p1, t1, pm_, tm_, p2, t2 = tapes[l]
        if l == N_A - 1:
            dx, kv_grads, res = _kv_bwd(kv, kv_p, dk_sum, dv_sum, dx, (t2["y"], p2["gm"]))
            d_dkv = kv_grads["w_dkv"]
            ties += send((l, 3), {
                "mla_w_dkv": jnp.concatenate(
                    [d_dkv[:, :KV_LORA], d_dkv[:, KV_LORA:KV_LORA + ROPE] + d_dkv[:, KV_LORA + ROPE:]],
                    axis=1).reshape(N_DEV, D // N_DEV, KV_LORA + ROPE),
                "mla_w_ukv": by_cols(kv_grads["w_ukv"], 2 * HEAD)})
        dx, g2, res = _ffn_bwd(t2, p2, dx, res, (tm_["y"], pm_["gm"]), ties)
        ties = send((l, 2), ffn_blocks(g2))
        if l < N_A:
            dx, gm_, res = _gdn_bwd(tm_, pm_, dx, res, (t1["y"], p1["gm"]), ties)
        else:
            dx, gm_, res, dk_sum, dv_sum = _mla_bwd(tm_, pm_, kv, dx, res, (t1["y"], p1["gm"]), ties,
                                                    None if dk_sum is None else (dk_sum, dv_sum))
        ties = send((l, 1), mixer_blocks(l, gm_))
        prev = (tapes[l - 1][5]["y"], tapes[l - 1][4]["gm"]) if l > 0 and l != N_A else None
        dx, g1, res = _ffn_bwd(t1, p1, dx, res, prev, ties)
        if l > 0:
            ties = send((l, 0), ffn_blocks(g1))
        grads[l] = (g1, gm_, g2)

    out = {}
    def dmod(l):
        g1, gm_, g2 = grads[l]
        return jnp.concatenate([g1["shift"], g1["scale"], 0.5 * g1["gm"], gm_["shift"], gm_["scale"], gm_["gm"],
                                g2["shift"], g2["scale"], 0.5 * g2["gm"]], axis=1)

    gdn = [grads[l][1] for l in range(N_A)]
    mla = [grads[l][1] for l in range(N_A, DEPTH)]
    small = {
        "ada_b": jnp.concatenate([dmod(l) for l in range(DEPTH)], axis=0),
        "kv_ada_b": jnp.concatenate([kv_grads["shift"], kv_grads["scale"]], axis=1),
        "norm_g": jnp.stack([jnp.concatenate([grads[l][0]["gain"], grads[l][1]["gain"], grads[l][2]["gain"]], axis=0)
                             for l in range(DEPTH)]),
        "gdn_conv_w": jnp.stack([g["conv_w"] for g in gdn]),
        "gdn_a_log": jnp.stack([g["a_log"][0, :HEADS] for g in gdn]),
        "gdn_dt_bias": jnp.stack([g["dt_bias"][0, :HEADS] for g in gdn]),
        "gdn_norm_g": jnp.stack([g["norm_g"][0] for g in gdn]),
        "kv_norm_g": kv_grads["gain"],
        "mla_kv_norm_g": kv_grads["kv_g"],
        "mla_k_norm_g": _fold_rope(jnp.concatenate([kv_grads["k_gn"], kv_grads["k_gr"]], axis=1)),
        "mla_q_lora_norm_g": jnp.stack([g["q_lora_g"][0] for g in mla]),
        "mla_q_norm_g": jnp.stack([_fold_rope(jnp.concatenate([g["q_gn"], g["q_gr"]], axis=1))[0] for g in mla]),
    }
    rows = 616
    assert sum(n for _, n in SMALL) <= rows * 128 and all(small[n].size == k for n, k in SMALL)
    small_recv = _all_gather(_pack([small[n] for n, _ in SMALL], rows), "gather_small_grads")
    small_recv = small_recv + send((0, 0), ffn_blocks(grads[0][0]), zero_of(small_recv))[0][0, 0]
    zero = lambda n, k: jnp.zeros((k,), F32)
    packed = {pre: _pack([a[pre + n] if n in SMALL_REPLICATED else zero(n, k) for n, k in SMALL], rows)
              for pre in ("", "m_", "v_")}
    res = _adamw([small_recv], packed[""], packed["m_"], packed["v_"], "adamw_small")
    offs = {}
    o = 0
    for n, k in SMALL:
        offs[n] = o
        o += k
    for n, k in SMALL:
        if n in SMALL_REPLICATED:
            out[n] = [r.reshape(-1)[offs[n]:offs[n] + k] for r in res]
    gsum = res[0].reshape(-1)
    g_norm = lax.dynamic_slice_in_dim(gsum[offs["norm_g"]:offs["norm_g"] + DEPTH * 3 * D].reshape(DEPTH * 3, D),
                                      me * HEAD, HEAD, axis=1)
    g_conv = lax.dynamic_slice_in_dim(
        gsum[offs["gdn_conv_w"]:offs["gdn_conv_w"] + N_A * CONV_K * 3 * D].reshape(N_A * CONV_K, 3 * D),
        me * 3 * HEAD, 3 * HEAD, axis=1)
    res2 = _adamw([_pack([g_norm, g_conv], 36)[None]], *[_pack([a[pre + "norm_g"], a[pre + "gdn_conv_w"]], 36)
                                                      for pre in ("", "m_", "v_")], "adamw_small")
    out["norm_g"] = [r.reshape(-1)[:n_ng] for r in res2]
    out["gdn_conv_w"] = [r.reshape(-1)[n_ng:n_ng + n_cw] for r in res2]

    c_act_t = c_act.T
    all_small = small_recv.reshape(N_DEV, -1)
    dmod_all = all_small[:, :DEPTH * N_MOD * D].reshape(N_DEV, DEPTH, N_MOD * D)
    dmod_mine = lax.dynamic_slice_in_dim(dmod_all, me * n_ada, n_ada, axis=2)
    g_ada = [_outer8(c_act_t, dmod_mine[:, l], "ada_grad")[None] for l in range(DEPTH)]
    out["ada_w"] = _adamw(g_ada, *[a[pre + "ada_w"].reshape(DEPTH * D, n_ada) for pre in ("", "m_", "v_")], "adamw")
    dkv_all = all_small[:, offs["kv_ada_b"]:offs["kv_ada_b"] + 2 * D]
    g_kv = _outer8(c_act_t, lax.dynamic_slice_in_dim(dkv_all, me * (2 * D // N_DEV), 2 * D // N_DEV, axis=1), "ada_grad")
    out["kv_ada_w"] = _adamw([g_kv[None]], *[a[pre + "kv_ada_w"] for pre in ("", "m_", "v_")], "adamw")

    pieces = {}
    for key, names, handle in sent:
        srcs, lands = _send_wait(handle, out["kv_ada_w"][0], "grad_wait_" + "_".join(map(str, key)), gather=False)
        for name, src, land in zip(names, srcs, lands):
            own = lax.dynamic_slice_in_dim(src, me, 1, axis=0)
            pieces.setdefault(name, []).append((key, lax.dynamic_update_slice(land, own, (me, 0, 0))))
    for name, parts in pieces.items():
        wide = a[name].shape[-1]
        out[name] = _adamw([p for _, p in sorted(parts, key=lambda kp: kp[0])], a[name].reshape(-1, wide),
                           a["m_" + name].reshape(-1, wide), a["v_" + name].reshape(-1, wide), "adamw")

    result = [loss, dx[None]]
    for k in range(4):
        result += [out[n][k].reshape(a[n].shape) for n in WEIGHTS]
    return tuple(result)


def kernel(x, c, positions, ada_w, ada_b, norm_g, ffn_w_in, ffn_w_out, gdn_w_in, gdn_conv_w, gdn_a_log, gdn_dt_bias, gdn_norm_g, gdn_w_out, kv_ada_w, kv_ada_b, kv_norm_g, mla_w_dkv, mla_kv_norm_g, mla_w_ukv, mla_k_norm_g, mla_w_dq, mla_q_lora_norm_g, mla_w_uq, mla_q_norm_g, mla_w_out, loss_target, m_ada_w, m_ada_b, m_norm_g, m_ffn_w_in, m_ffn_w_out, m_gdn_w_in, m_gdn_conv_w, m_gdn_a_log, m_gdn_dt_bias, m_gdn_norm_g, m_gdn_w_out, m_kv_ada_w, m_kv_ada_b, m_kv_norm_g, m_mla_w_dkv, m_mla_kv_norm_g, m_mla_w_ukv, m_mla_k_norm_g, m_mla_w_dq, m_mla_q_lora_norm_g, m_mla_w_uq, m_mla_q_norm_g, m_mla_w_out, v_ada_w, v_ada_b, v_norm_g, v_ffn_w_in, v_ffn_w_out, v_gdn_w_in, v_gdn_conv_w, v_gdn_a_log, v_gdn_dt_bias, v_gdn_norm_g, v_gdn_w_out, v_kv_ada_w, v_kv_ada_b, v_kv_norm_g, v_mla_w_dkv, v_mla_kv_norm_g, v_mla_w_ukv, v_mla_k_norm_g, v_mla_w_dq, v_mla_q_lora_norm_g, v_mla_w_uq, v_mla_q_norm_g, v_mla_w_out):
    return _step(dict(locals()))
```

```python
import functools
import math

import jax
import jax.numpy as jnp
from jax import lax
from jax.experimental import pallas as pl
from jax.experimental.pallas import tpu as pltpu

F32 = jnp.float32
BF16 = jnp.bfloat16

N_DEV = 8
D = 1024
D_FF = 2816
DEPTH = 4
N_A = 2
N_MOD = 9
HEADS = 8
HEAD = 128
CHUNK = 64
CONV_K = 4
KV_LORA = 256
Q_LORA = 384
NOPE = 128
ROPE = 64
QK_HEAD = NOPE + ROPE
HEAD_PAD = 256
ROPE_BASE = 10000.0
EPS = 1e-6
LR, B1, B2, ADAM_EPS, WD, STEP = 0.001, 0.9, 0.999, 1e-08, 0.01, 10

VMEM_LIMIT = 48 * 1024 * 1024
ROW_TILE = 256
MESH = pl.DeviceIdType.MESH

_NN = (((1,), (0,)), ((), ()))
_NT = (((1,), (1,)), ((), ()))
_TN = (((0,), (0,)), ((), ()))
_DIMS = {"nn": _NN, "nt": _NT, "tn": _TN}


def _params(dims=None):
    return pltpu.CompilerParams(dimension_semantics=dims, vmem_limit_bytes=VMEM_LIMIT)


def _tile(n, target):
    for t in range(target - target % 128, 0, -128):
        if n % t == 0:
            return t
    return n


_TIE_SPEC1 = pl.BlockSpec((8, 128), lambda i: (0, 0))
_TIE_SPEC2 = pl.BlockSpec((8, 128), lambda i, j: (0, 0))
_TIE_SPEC3 = pl.BlockSpec((8, 128), lambda i, j, k: (0, 0))


def _matmul(pairs, form, name, out_dtype=F32, tm=1408, tn=1408, tk=1408, boffs=None, resid=None, ties=()):
    a0, b0 = pairs[0]
    if form == "nn":
        m, n = a0.shape[0], b0.shape[1]
        ks = [a.shape[1] for a, _ in pairs]
    elif form == "nt":
        m, n = a0.shape[0], b0.shape[0]
        ks = [a.shape[1] for a, _ in pairs]
    else:
        m, n = a0.shape[1], b0.shape[1]
        ks = [a.shape[0] for a, _ in pairs]
    tm, tn = _tile(m, tm), _tile(n, tn)
    tks = [_tile(k, tk) for k in ks]
    boffs = boffs or [0] * len(pairs)
    assert m % tm == 0 and n % tn == 0 and all(o % t == 0 for o, t in zip(boffs, tks)), (name, m, n, ks)
    steps = [k // t for k, t in zip(ks, tks)]
    starts = [sum(steps[:p]) for p in range(len(pairs))]
    nk = sum(steps)

    def kidx(p, k):
        return jnp.clip(k - starts[p], 0, steps[p] - 1)

    in_specs, args = [], []
    for p, (a, b) in enumerate(pairs):
        t = tks[p]
        if form == "tn":
            in_specs.append(pl.BlockSpec((t, tm), lambda i, j, k, p=p: (kidx(p, k), i)))
            in_specs.append(pl.BlockSpec((t, tn), lambda i, j, k, p=p: (kidx(p, k), j)))
        elif form == "nn":
            in_specs.append(pl.BlockSpec((tm, t), lambda i, j, k, p=p: (i, kidx(p, k))))
            in_specs.append(pl.BlockSpec((t, tn), lambda i, j, k, p=p: (kidx(p, k), j)))
        else:
            in_specs.append(pl.BlockSpec((tm, t), lambda i, j, k, p=p: (i, kidx(p, k))))
            in_specs.append(pl.BlockSpec((tn, t), lambda i, j, k, p=p, o=boffs[p] // t: (j, kidx(p, k) + o)))
        args += [a, b]
    dims = _DIMS[form]
    npairs = len(pairs)
    nres = len(resid or ())
    nin = 2 * npairs + len(ties) + nres
    out_blk = pl.BlockSpec((tm, tn), lambda i, j, k: (i, j))
    in_specs += [_TIE_SPEC3] * len(ties)
    args += list(ties)
    if resid:
        assert nres == 2 or (nres == 5 and tn == n)
        in_specs += [out_blk] + [pl.BlockSpec((1, tn), lambda i, j, k: (0, j))] * (nres - 1)
        args += list(resid)

    def body(*refs):
        o_ref = refs[nin]
        k = pl.program_id(2)

        def prod(p):
            return lax.dot_general(refs[2 * p][...].astype(BF16), refs[2 * p + 1][...].astype(BF16), dims,
                                   preferred_element_type=F32)

        def finish(y):
            o_ref[...] = y.astype(o_ref.dtype)
            if resid:
                x_ref, gate_ref = refs[nin - nres], refs[nin - nres + 1]
                xn = x_ref[...] + gate_ref[...] * y
                refs[nin + 1][...] = xn
                if nres == 5:
                    gain, scale, shift = (r[...] for r in refs[nin - 3:nin])
                    refs[nin + 2][...] = _modulate_fn(0, xn, gain, scale, shift)[0].astype(BF16)

        if nk == 1:
            finish(prod(0))
            return
        acc = refs[-1]

        @pl.when(k == 0)
        def _():
            acc[...] = jnp.zeros_like(acc)

        for p in range(npairs):
            @pl.when((k >= starts[p]) & (k < starts[p] + steps[p]))
            def _(p=p):
                acc[...] += prod(p)

        @pl.when(k == nk - 1)
        def _():
            finish(acc[...])

    res = pl.pallas_call(
        body, name=name, grid=(m // tm, n // tn, nk), in_specs=in_specs,
        out_specs=[out_blk] * (2 + (nres == 5)) if resid else out_blk,
        out_shape=([jax.ShapeDtypeStruct((m, n), out_dtype), jax.ShapeDtypeStruct((m, n), F32)]
                   + [jax.ShapeDtypeStruct((m, n), BF16)] * (nres == 5))
        if resid else jax.ShapeDtypeStruct((m, n), out_dtype),
        scratch_shapes=[] if nk == 1 else [pltpu.VMEM((tm, tn), F32)],
        compiler_params=_params(("parallel", "parallel", "arbitrary")),
    )(*args)
    return res


def _mm(a, b, form, name, **kw):
    return _matmul([(a, b)], form, name, **kw)


def _cols(spec, g):
    return spec[g] if isinstance(spec, list) else spec


def _rowwise_fwd(fn, rows, pars, outs, name, groups=1, ts=ROW_TILE, ties=()):
    s = rows[0][0].shape[0]
    ts = min(ts, s)
    assert s % ts == 0
    nr, npar = len(rows), len(pars)

    def body(*refs):
        par_t = [r[...] for r in refs[nr:nr + npar]]
        out_refs = refs[nr + npar + len(ties):]
        for g in range(groups):
            row_t = []
            for r, (_, spec) in zip(refs[:nr], rows):
                c0, w = _cols(spec, g)
                row_t.append(r[:, c0:c0 + w].astype(F32))
            res = fn(g, *row_t, *par_t)
            for o_ref, val, (_, _, spec) in zip(out_refs, res, outs):
                c0, w = _cols(spec, g)
                o_ref[:, c0:c0 + w] = val.astype(o_ref.dtype)

    return pl.pallas_call(
        body, name=name, grid=(s // ts,),
        in_specs=[pl.BlockSpec((ts, a.shape[1]), lambda i: (i, 0)) for a, _ in rows]
        + [pl.BlockSpec(p.shape, lambda i: (0, 0)) for p in pars] + [_TIE_SPEC1] * len(ties),
        out_specs=[pl.BlockSpec((ts, w), lambda i: (i, 0)) for w, _, _ in outs],
        out_shape=[jax.ShapeDtypeStruct((s, w), dt) for w, dt, _ in outs],
        compiler_params=_params(("parallel",)),
    )(*[a for a, _ in rows], *pars, *ties)


def _rowwise_bwd(fn, rows, pars, outs, douts, gmap, gshapes, name, groups=1, add=None, par_grads=True,
                 ts=ROW_TILE):
    s = rows[0][0].shape[0]
    ts = min(ts, s)
    assert s % ts == 0
    nr, npar, nout, ng = len(rows), len(pars), len(outs), len(gshapes)
    add = add or {}
    add_keys = sorted(add)

    def body(*refs):
        row_refs = refs[:nr]
        par_refs = refs[nr:nr + npar]
        dout_refs = refs[nr + npar:nr + npar + nout]
        add_refs = refs[nr + npar + nout:nr + npar + nout + len(add_keys)]
        g_refs = refs[nr + npar + nout + len(add_keys):][:ng]
        pg_refs = refs[nr + npar + nout + len(add_keys) + ng:]
        par_t = [r[...] for r in par_refs]
        par_acc = [None] * npar
        shared_acc = {}
        for g in range(groups):
            row_t = []
            for r, (_, spec) in zip(row_refs, rows):
                c0, w = _cols(spec, g)
                row_t.append(r[:, c0:c0 + w].astype(F32))
            cts = []
            for r, (_, _, spec) in zip(dout_refs, outs):
                c0, w = _cols(spec, g)
                cts.append(r[:, c0:c0 + w].astype(F32))
            _, vjp = jax.vjp(lambda *t, g=g: tuple(fn(g, *t)), *row_t, *par_t)
            grads = vjp(tuple(cts))
            for k in range(nr):
                if gmap[k] is None:
                    continue
                gi, spec = gmap[k]
                if isinstance(spec, list) or groups == 1:
                    c0, w = _cols(spec, g)
                    val = grads[k]
                    if gi in add:
                        val = val + add_refs[add_keys.index(gi)][:, c0:c0 + w].astype(F32)
                    g_refs[gi][:, c0:c0 + w] = val.astype(g_refs[gi].dtype)
                else:
                    shared_acc[k] = grads[k] if k not in shared_acc else shared_acc[k] + grads[k]
            if par_grads:
                for k in range(npar):
                    pg = grads[nr + k]
                    par_acc[k] = pg if par_acc[k] is None else par_acc[k] + pg
        for k, val in shared_acc.items():
            gi, (c0, w) = gmap[k]
            assert gi not in add
            g_refs[gi][:, c0:c0 + w] = val.astype(g_refs[gi].dtype)
        if par_grads:
            first = pl.program_id(0) == 0
            for k in range(npar):
                @pl.when(first)
                def _(k=k):
                    pg_refs[k][...] = par_acc[k]

                @pl.when(jnp.logical_not(first))
                def _(k=k):
                    pg_refs[k][...] += par_acc[k]

    out_specs = [pl.BlockSpec((ts, w), lambda i: (i, 0)) for w, _ in gshapes]
    out_shape = [jax.ShapeDtypeStruct((s, w), dt) for w, dt in gshapes]
    if par_grads:
        out_specs += [pl.BlockSpec(p.shape, lambda i: (0, 0)) for p in pars]
        out_shape += [jax.ShapeDtypeStruct(p.shape, F32) for p in pars]
    return pl.pallas_call(
        body, name=name, grid=(s // ts,),
        in_specs=[pl.BlockSpec((ts, a.shape[1]), lambda i: (i, 0)) for a, _ in rows]
        + [pl.BlockSpec(p.shape, lambda i: (0, 0)) for p in pars]
        + [pl.BlockSpec((ts, a.shape[1]), lambda i: (i, 0)) for a in douts]
        + [pl.BlockSpec((ts, add[k].shape[1]), lambda i: (i, 0)) for k in add_keys],
        out_specs=out_specs, out_shape=out_shape,
        compiler_params=_params(("arbitrary",)),
    )(*[a for a, _ in rows], *pars, *douts, *[add[k] for k in add_keys])


def _sigmoid(x):
    return 1.0 / (1.0 + jnp.exp(-x))


def _silu(x):
    return x * _sigmoid(x)


def _softplus(x):
    return jnp.maximum(x, 0.0) + jnp.log(1.0 + jnp.exp(-jnp.abs(x)))


def _rms(t, g, n=None):
    n = n or t.shape[-1]
    return t * lax.rsqrt(jnp.sum(t * t, axis=-1, keepdims=True) / n + EPS) * g


def _modulate_fn(g, x, gain, scale, shift):
    return (_rms(x, gain) * (1.0 + scale) + shift,)


def _resgate_fn(g, x, y, gm):
    return (x + gm * y,)


def _gate_only_fn(g, y, gm):
    return (gm * y,)


def _gdn_gates_fn(g, b_logit, a_logit, a_log, dt_bias):
    gate = -jnp.exp(a_log) * _softplus(a_logit + dt_bias)
    n = gate.shape[0]
    i = lax.broadcasted_iota(jnp.int32, (n, n), 0)
    j = lax.broadcasted_iota(jnp.int32, (n, n), 1)
    tri = (((i // CHUNK) == (j // CHUNK)) & (i >= j)).astype(F32)
    gcum = lax.dot_general(tri, gate, _NN, preferred_element_type=F32, precision=lax.Precision.HIGHEST)
    return _sigmoid(b_logit), gcum


def _gdn_outnorm_fn(g, o, z, gain):
    return (_rms(o, gain) * _silu(z),)


def _rms_fn(g, t, gain):
    return (_rms(t, gain),)


@jax.custom_vjp
def _swap_halves(t):
    return pltpu.roll(t, 32, 1)


_swap_halves.defvjp(lambda t: (pltpu.roll(t, 32, 1), None), lambda _, ct: (pltpu.roll(ct, 96, 1),))


def _head_norm_rope_fn(g, nope, rope, cosf, sins, gain_n, gain_r):
    first = lax.broadcasted_iota(jnp.int32, rope.shape, 1) < ROPE
    ss = jnp.sum(nope * nope, axis=-1, keepdims=True) + jnp.sum(jnp.where(first, rope * rope, 0.0), axis=-1,
                                                                 keepdims=True)
    r = lax.rsqrt(ss / QK_HEAD + EPS)
    tn = nope * r * gain_n
    tr = rope * r * gain_r
    rot = jnp.where(first, tr * cosf + _swap_halves(tr) * sins, 0.0)
    return tn, rot


def _q_norm_rope_fn(g, nope, rope, cosf, sins, gain_n, gain_r):
    tn, rot = _head_norm_rope_fn(g, nope, rope, cosf, sins, gain_n, gain_r)
    return (jnp.concatenate([tn, rot], axis=1),)


def _k_norm_rope_fn(g, nope, val, rope, cosf, sins, gain_n, gain_r):
    tn, rot = _head_norm_rope_fn(g, nope, rope, cosf, sins, gain_n, gain_r)
    return jnp.concatenate([tn, rot], axis=1), val


def _loss_fn(g, y, target):
    e = y - target
    return (jnp.sum(e * e, axis=-1, keepdims=True) * (0.5 / D) * jnp.ones((1, 128), F32),)


FF_SH = 2 * D_FF // N_DEV
FF_G = N_DEV // 2


def _ffn_in(h, w_in, name, tm=1024, ties=()):
    s = h.shape[0]
    tm = min(tm, s)

    def body(h_ref, wg_ref, wu_ref, *rest):
        g_ref, u_ref, a_ref = rest[-3:]
        hb = h_ref[...]
        gate = jnp.dot(hb, wg_ref[...], preferred_element_type=F32)
        up = jnp.dot(hb, wu_ref[...], preferred_element_type=F32)
        g_ref[...] = gate.astype(BF16)
        u_ref[...] = up.astype(BF16)
        a_ref[...] = (_silu(gate) * up).astype(BF16)

    spec = pl.BlockSpec((None, tm, FF_SH), lambda j, i: (j, i, 0))
    return pl.pallas_call(
        body, name=name, grid=(FF_G, s // tm),
        in_specs=[pl.BlockSpec((tm, D), lambda j, i: (i, 0)), pl.BlockSpec((None, D, FF_SH), lambda j, i: (j, 0, 0)),
                  pl.BlockSpec((None, D, FF_SH), lambda j, i: (j + FF_G, 0, 0))] + [_TIE_SPEC2] * len(ties),
        out_specs=[spec, spec, spec], out_shape=[jax.ShapeDtypeStruct((FF_G, s, FF_SH), BF16)] * 3,
        compiler_params=_params(("parallel", "parallel")),
    )(h, w_in, w_in, *ties)


def _ffn_out(act, wo, resid, name, tm=512):
    s = act.shape[1]
    tm = min(tm, s)
    nres = len(resid)

    def body(a_ref, b_ref, x_ref, gate_ref, *rest):
        mods, outs = rest[:nres - 2], rest[nres - 2:]
        y = jnp.dot(a_ref[0], b_ref[0:FF_SH, :], preferred_element_type=F32)
        for k in range(1, FF_G):
            y = y + jnp.dot(a_ref[k], b_ref[k * FF_SH:(k + 1) * FF_SH, :], preferred_element_type=F32)
        xn = x_ref[...] + gate_ref[...] * y
        outs[0][...] = y.astype(BF16)
        outs[1][...] = xn
        if mods:
            outs[2][...] = _modulate_fn(0, xn, *[m[...] for m in mods])[0].astype(BF16)

    blk = pl.BlockSpec((tm, D), lambda i: (i, 0))
    vec = pl.BlockSpec((1, D), lambda i: (0, 0))
    return pl.pallas_call(
        body, name=name, grid=(s // tm,),
        in_specs=[pl.BlockSpec((FF_G, tm, FF_SH), lambda i: (0, i, 0)), pl.BlockSpec((D_FF, D), lambda i: (0, 0)),
                  blk] + [vec] * (nres - 1),
        out_specs=[blk] * (2 + (nres == 5)),
        out_shape=[jax.ShapeDtypeStruct((s, D), BF16), jax.ShapeDtypeStruct((s, D), F32)]
        + [jax.ShapeDtypeStruct((s, D), BF16)] * (nres == 5),
        compiler_params=_params(("parallel",)),
    )(act, wo, *resid)


def _ffn_bwd_act(dy, wo, gate, up, name, tm=1024, ties=()):
    s = dy.shape[0]
    tm = min(tm, s)

    def body(dy_ref, wo_ref, g_ref, u_ref, *rest):
        dg_ref, du_ref = rest[-2:]
        dact = lax.dot_general(dy_ref[...], wo_ref[...], _NT, preferred_element_type=F32)
        gate = g_ref[...].astype(F32)
        up = u_ref[...].astype(F32)
        sg = _sigmoid(gate)
        dg_ref[...] = (dact * up * (sg * (1.0 + gate * (1.0 - sg)))).astype(BF16)
        du_ref[...] = (dact * (gate * sg)).astype(BF16)

    spec = pl.BlockSpec((None, tm, FF_SH), lambda j, i: (j, i, 0))
    return pl.pallas_call(
        body, name=name, grid=(FF_G, s // tm),
        in_specs=[pl.BlockSpec((tm, D), lambda j, i: (i, 0)), pl.BlockSpec((FF_SH, D), lambda j, i: (j, 0)), spec, spec]
        + [_TIE_SPEC2] * len(ties),
        out_specs=[spec, spec], out_shape=[jax.ShapeDtypeStruct((FF_G, s, FF_SH), BF16)] * 2,
        compiler_params=_params(("parallel", "parallel")),
    )(dy, wo, gate, up, *ties)


def _ffn_dwo(act, dy, name, tk=2048, ties=()):
    s = act.shape[1]
    tk = min(tk, s)

    def body(a_ref, b_ref, *rest):
        o_ref, acc = rest[-2:]
        k = pl.program_id(1)

        @pl.when(k == 0)
        def _():
            acc[...] = jnp.zeros_like(acc)

        acc[...] += lax.dot_general(a_ref[...], b_ref[...], _TN, preferred_element_type=F32)

        @pl.when(k == s // tk - 1)
        def _():
            o_ref[...] = acc[...].astype(BF16)

    return pl.pallas_call(
        body, name=name, grid=(FF_G, s // tk),
        in_specs=[pl.BlockSpec((None, tk, FF_SH), lambda j, k: (j, k, 0)), pl.BlockSpec((tk, D), lambda j, k: (k, 0))]
        + [_TIE_SPEC2] * len(ties),
        out_specs=pl.BlockSpec((FF_SH, D), lambda j, k: (j, 0)), out_shape=jax.ShapeDtypeStruct((D_FF, D), BF16),
        scratch_shapes=[pltpu.VMEM((FF_SH, D), F32)], compiler_params=_params(("parallel", "arbitrary")),
    )(act, dy, *ties)


def _ffn_halves(k, gate_ref, up_ref, fn):
    pl.when(k < FF_G)(functools.partial(fn, gate_ref))
    pl.when(k >= FF_G)(functools.partial(fn, up_ref))


def _ffn_dh(dgate, dup, w_in, name, tm=2048):
    s = dgate.shape[1]
    tm = min(tm, s)

    def body(dg_ref, du_ref, w_ref, o_ref, acc):
        k = pl.program_id(1)

        @pl.when(k == 0)
        def _():
            acc[...] = jnp.zeros_like(acc)

        def add(d_ref):
            acc[...] += lax.dot_general(d_ref[...], w_ref[...], _NT, preferred_element_type=F32)

        _ffn_halves(k, dg_ref, du_ref, add)

        @pl.when(k == N_DEV - 1)
        def _():
            o_ref[...] = acc[...]

    return pl.pallas_call(
        body, name=name, grid=(s // tm, N_DEV),
        in_specs=[pl.BlockSpec((None, tm, FF_SH), lambda i, k: (jnp.minimum(k, FF_G - 1), i, 0)),
                  pl.BlockSpec((None, tm, FF_SH), lambda i, k: (jnp.maximum(k - FF_G, 0), i, 0)),
                  pl.BlockSpec((None, D, FF_SH), lambda i, k: (k, 0, 0))],
        out_specs=pl.BlockSpec((tm, D), lambda i, k: (i, 0)), out_shape=jax.ShapeDtypeStruct((s, D), F32),
        scratch_shapes=[pltpu.VMEM((tm, D), F32)], compiler_params=_params(("parallel", "arbitrary")),
    )(dgate, dup, w_in)


def _ffn_dwi(h, dgate, dup, name, tk=2048):
    s = h.shape[0]
    tk = min(tk, s)

    def body(h_ref, dg_ref, du_ref, o_ref, acc):
        j, k = pl.program_id(0), pl.program_id(1)

        @pl.when(k == 0)
        def _():
            acc[...] = jnp.zeros_like(acc)

        def add(d_ref):
            acc[...] += lax.dot_general(h_ref[...], d_ref[...], _TN, preferred_element_type=F32)

        _ffn_halves(j, dg_ref, du_ref, add)

        @pl.when(k == s // tk - 1)
        def _():
            o_ref[...] = acc[...].astype(BF16)

    return pl.pallas_call(
        body, name=name, grid=(N_DEV, s // tk),
        in_specs=[pl.BlockSpec((tk, D), lambda j, k: (k, 0)),
                  pl.BlockSpec((None, tk, FF_SH), lambda j, k: (jnp.minimum(j, FF_G - 1), jnp.where(j < FF_G, k, s // tk - 1), 0)),
                  pl.BlockSpec((None, tk, FF_SH), lambda j, k: (jnp.maximum(j - FF_G, 0), jnp.where(j < FF_G, 0, k), 0))],
        out_specs=pl.BlockSpec((None, D, FF_SH), lambda j, k: (j, 0, 0)),
        out_shape=jax.ShapeDtypeStruct((N_DEV, D, FF_SH), BF16),
        scratch_shapes=[pltpu.VMEM((D, FF_SH), F32)], compiler_params=_params(("parallel", "arbitrary")),
    )(h, dgate, dup)


def _shift_down(x, d):
    rows = lax.broadcasted_iota(jnp.int32, x.shape, 0)
    return jnp.where(rows >= d, pltpu.roll(x, d, 0), 0.0)


def _shift_up(x, d):
    n = x.shape[0]
    rows = lax.broadcasted_iota(jnp.int32, x.shape, 0)
    return jnp.where(rows < n - d, pltpu.roll(x, n - d, 0), 0.0)


def _conv_post(pre, is_qk):
    a = _silu(pre)
    l2 = a * lax.rsqrt(jnp.sum(a * a, axis=-1, keepdims=True) + EPS)
    return jnp.where(is_qk, l2, a)


def _conv_pre(x, w):
    pre = x * w[CONV_K - 1:CONV_K, :]
    for j in range(CONV_K - 1):
        pre = pre + _shift_down(x, CONV_K - 1 - j) * w[j:j + 1, :]
    return pre


def _gdn_conv_fwd(pm, conv_w, name):
    s = pm.shape[0]
    nblk = 3 * D // HEAD

    def body(x_ref, w_ref, o_ref):
        is_qk = pl.program_id(0) < 2 * HEADS
        o_ref[...] = _conv_post(_conv_pre(x_ref[...], w_ref[...]), is_qk)

    return pl.pallas_call(
        body, name=name, grid=(nblk,),
        in_specs=[pl.BlockSpec((s, HEAD), lambda c: (0, c)), pl.BlockSpec((CONV_K, HEAD), lambda c: (0, c))],
        out_specs=pl.BlockSpec((s, HEAD), lambda c: (0, c)),
        out_shape=jax.ShapeDtypeStruct((s, 3 * D), F32), compiler_params=_params(("parallel",)),
    )(pm, conv_w)


def _gdn_conv_bwd(pm, conv_w, dout, part, name):
    s = pm.shape[0]
    off = part * HEADS

    def body(x_ref, w_ref, d_ref, dx_ref, dw_ref):
        x, w = x_ref[...], w_ref[...]
        _, vjp = jax.vjp(lambda p: _conv_post(p, part < 2), _conv_pre(x, w))
        dpre, = vjp(d_ref[...])
        dx = dpre * w[CONV_K - 1:CONV_K, :]
        rows = [None] * CONV_K
        rows[CONV_K - 1] = jnp.sum(dpre * x, axis=0, keepdims=True)
        for j in range(CONV_K - 1):
            dx = dx + _shift_up(dpre, CONV_K - 1 - j) * w[j:j + 1, :]
            rows[j] = jnp.sum(dpre * _shift_down(x, CONV_K - 1 - j), axis=0, keepdims=True)
        dx_ref[...] = dx.astype(BF16)
        dw_ref[...] = jnp.concatenate(rows, axis=0)

    return pl.pallas_call(
        body, name=name, grid=(HEADS,),
        in_specs=[pl.BlockSpec((s, HEAD), lambda c: (0, c + off)), pl.BlockSpec((CONV_K, HEAD), lambda c: (0, c + off)),
                  pl.BlockSpec((s, HEAD), lambda c: (0, c))],
        out_specs=[pl.BlockSpec((s, HEAD), lambda c: (0, c)), pl.BlockSpec((CONV_K, HEAD), lambda c: (0, c))],
        out_shape=[jax.ShapeDtypeStruct((s, D), BF16), jax.ShapeDtypeStruct((CONV_K, D), F32)],
        compiler_params=_params(("parallel",)),
    )(pm, conv_w, dout)


def _dot3(a, b, dims=_NN):
    ah, bh = a.astype(BF16), b.astype(BF16)
    al, bl = (a - ah.astype(F32)).astype(BF16), (b - bh.astype(F32)).astype(BF16)
    d = lambda u, v: lax.dot_general(u, v, dims, preferred_element_type=F32)
    return d(ah, bh) + (d(ah, bl) + d(al, bh))


def _make_dot(hi):
    def raw(a, b, dims):
        if hi:
            return _dot3(a, b, dims)
        return lax.dot_general(a.astype(BF16), b.astype(BF16), dims, preferred_element_type=F32)

    @functools.partial(jax.custom_vjp, nondiff_argnums=(2,))
    def dot(a, b, form):
        return raw(a, b, _DIMS[form])

    def fwd(a, b, form):
        return raw(a, b, _DIMS[form]), (a, b)

    def bwd(form, res, ct):
        a, b = res
        if form == "nn":
            return raw(ct, b, _NT), raw(a, ct, _TN)
        if form == "nt":
            return raw(ct, b, _NN), raw(ct, a, _TN)
        return raw(b, ct, _NT), raw(a, ct, _NN)

    dot.defvjp(fwd, bwd)
    return dot


_dot = _make_dot(False)
_dot_hi = _make_dot(True)


def _tri_inv_raw(low):
    n = low.shape[0]
    i = lax.broadcasted_iota(jnp.int32, (n, n), 0)
    j = lax.broadcasted_iota(jnp.int32, (n, n), 1)
    eye = (i == j).astype(F32)
    hdot = _dot3
    same16 = (i // 16) == (j // 16)
    neg = jnp.where(same16, -low, 0.0)
    inv = eye + neg
    power = neg
    for _ in range(3):
        power = hdot(power, power)
        inv = hdot(inv, eye + power)
    for blk in (32, 64):
        off = jnp.where(((i // blk) == (j // blk)) & ((i // (blk // 2)) != (j // (blk // 2))), low, 0.0)
        inv = inv - hdot(inv, hdot(off, inv))
    return inv


@jax.custom_vjp
def _tri_inv(low):
    return _tri_inv_raw(low)


def _tri_inv_fwd(low):
    inv = _tri_inv_raw(low)
    return inv, inv


def _tri_inv_bwd(inv, ct):
    return (-_dot3(_dot3(inv, ct, _TN), inv, _NT),)


_tri_inv.defvjp(_tri_inv_fwd, _tri_inv_bwd)


@jax.custom_vjp
def _tri_inv_given(low, inv):
    return inv


_tri_inv_given.defvjp(lambda low, inv: (inv, inv),
                      lambda inv, ct: (_tri_inv_bwd(inv, ct)[0], jnp.zeros_like(inv)))

GROUP = 4
N_GROUPS = HEADS // GROUP
GROWS = GROUP * CHUNK


def _gdn_group(q, k, v, beta, gc, gr, states, inv=None):
    n = q.shape[0]
    i = lax.broadcasted_iota(jnp.int32, (n, n), 0)
    j = lax.broadcasted_iota(jnp.int32, (n, n), 1)
    same = (i // CHUNK) == (j // CHUNK)
    incl, strict = same & (i >= j), same & (i > j)
    qs = q * (HEAD ** -0.5)
    decay = jnp.where(incl, jnp.exp(jnp.where(incl, gc - gr, 0.0)), 0.0)
    kb = k * beta
    eg = jnp.exp(gc)
    prod = _dot(jnp.concatenate([kb, qs], axis=0), k, "nt")
    low = jnp.where(strict, prod[:n] * decay, 0.0)
    attn = jnp.where(incl, prod[n:] * decay, 0.0)
    inv = _tri_inv(low) if inv is None else _tri_inv_given(low, inv)
    sol = _dot_hi(inv, jnp.concatenate([v * beta, kb * eg], axis=1), "nn")
    u, w, qg = sol[:, :HEAD], sol[:, HEAD:], qs * eg
    last = lax.broadcasted_iota(jnp.int32, (CHUNK, 1), 0) == CHUNK - 1
    v_new, o_state, carry = [], [], []
    for h, state in enumerate(states):
        rows = slice(h * CHUNK, (h + 1) * CHUNK)
        ws = _dot(jnp.concatenate([w[rows], qg[rows]], axis=0), state, "nn")
        v_new.append(u[rows] - ws[:CHUNK])
        o_state.append(ws[CHUNK:])
        g_last = jnp.sum(jnp.where(last, gc[rows], 0.0), axis=0, keepdims=True)
        carry.append((g_last, k[rows] * jnp.exp(g_last - gc[rows])))
    o = jnp.concatenate(o_state, axis=0) + _dot(attn, jnp.concatenate(v_new, axis=0), "nn")
    new = tuple(state * jnp.exp(g_last) + _dot(k_dec, vn, "tn")
                for state, (g_last, k_dec), vn in zip(states, carry, v_new))
    return o, new, inv


def _gdn_specs(s, rev):
    nc = s // CHUNK
    at = (lambda n: nc - 1 - n) if rev else (lambda n: n)
    return nc, at, [
        pl.BlockSpec((CHUNK, D), lambda n: (at(n), 0)), pl.BlockSpec((CHUNK, D), lambda n: (at(n), 1)),
        pl.BlockSpec((CHUNK, D), lambda n: (at(n), 2)), pl.BlockSpec((CHUNK, HEAD), lambda n: (at(n), 0)),
        pl.BlockSpec((CHUNK, HEAD), lambda n: (at(n), 0)),
        pl.BlockSpec((None, N_GROUPS, 1, GROWS), lambda n: (at(n), 0, 0, 0))]


def _group_operands(grp, q_ref, k_ref, v_ref, b_blk, gc_blk, gr_blk):
    heads = range(grp * GROUP, (grp + 1) * GROUP)
    stack = lambda ref: jnp.concatenate([ref[:, h * HEAD:(h + 1) * HEAD] for h in heads], axis=0)
    col = lambda blk: jnp.concatenate([blk[:, h:h + 1] for h in heads], axis=0)
    return stack(q_ref), stack(k_ref), stack(v_ref), col(b_blk), col(gc_blk), gr_blk[grp]


def _gdn_scan_fwd(qkv, beta, gcum, grow, name):
    s = qkv.shape[0]
    nc, _, in_specs = _gdn_specs(s, rev=False)

    def body(q_ref, k_ref, v_ref, b_ref, gc_ref, gr_ref, o_ref, st_ref, inv_ref, state):
        @pl.when(pl.program_id(0) == 0)
        def _():
            state[...] = jnp.zeros_like(state)

        b_blk, gc_blk, gr_blk = b_ref[...], gc_ref[...], gr_ref[...]
        old = [state[h] for h in range(HEADS)]
        res = [_gdn_group(*_group_operands(grp, q_ref, k_ref, v_ref, b_blk, gc_blk, gr_blk),
                          old[grp * GROUP:(grp + 1) * GROUP]) for grp in range(N_GROUPS)]
        for grp, (o, new, inv) in enumerate(res):
            inv_ref[grp] = inv
            for hh in range(GROUP):
                h = grp * GROUP + hh
                st_ref[h] = old[h]
                o_ref[:, h * HEAD:(h + 1) * HEAD] = o[hh * CHUNK:(hh + 1) * CHUNK]
                state[h] = new[hh]

    return pl.pallas_call(
        body, name=name, grid=(nc,), in_specs=in_specs,
        out_specs=[pl.BlockSpec((CHUNK, D), lambda n: (n, 0)),
                   pl.BlockSpec((None, HEADS, HEAD, HEAD), lambda n: (n, 0, 0, 0)),
                   pl.BlockSpec((None, N_GROUPS, GROWS, GROWS), lambda n: (n, 0, 0, 0))],
        out_shape=[jax.ShapeDtypeStruct((s, D), F32), jax.ShapeDtypeStruct((nc, HEADS, HEAD, HEAD), F32),
                   jax.ShapeDtypeStruct((nc, N_GROUPS, GROWS, GROWS), F32)],
        scratch_shapes=[pltpu.VMEM((HEADS, HEAD, HEAD), F32)],
        compiler_params=_params(("arbitrary",)),
    )(qkv, qkv, qkv, beta, gcum, grow)


def _gdn_scan_bwd(qkv, beta, gcum, grow, states, invs, do, name):
    s = qkv.shape[0]
    nc, at, in_specs = _gdn_specs(s, rev=True)
    in_specs += [pl.BlockSpec((None, HEADS, HEAD, HEAD), lambda n: (at(n), 0, 0, 0)),
                 pl.BlockSpec((None, N_GROUPS, GROWS, GROWS), lambda n: (at(n), 0, 0, 0)),
                 pl.BlockSpec((CHUNK, D), lambda n: (at(n), 0))]

    def body(q_ref, k_ref, v_ref, b_ref, gc_ref, gr_ref, st_ref, inv_ref, do_ref, dq_ref, dk_ref, dv_ref, db_ref,
             dgc_ref, dgr_ref, dstate):
        @pl.when(pl.program_id(0) == 0)
        def _():
            dstate[...] = jnp.zeros_like(dstate)

        b_blk, gc_blk, gr_blk = b_ref[...], gc_ref[...], gr_ref[...]
        dold = [dstate[h] for h in range(HEADS)]
        res = []
        for grp in range(N_GROUPS):
            heads = range(grp * GROUP, (grp + 1) * GROUP)
            inv = inv_ref[grp]
            _, vjp = jax.vjp(lambda q, k, v, b, gc, gr, *st, inv=inv: _gdn_group(q, k, v, b, gc, gr, st, inv)[:2],
                             *_group_operands(grp, q_ref, k_ref, v_ref, b_blk, gc_blk, gr_blk),
                             *[st_ref[h] for h in heads])
            d_out = jnp.concatenate([do_ref[:, h * HEAD:(h + 1) * HEAD] for h in heads], axis=0)
            res.append(vjp((d_out, tuple(dold[h] for h in heads))))
        lane = lax.broadcasted_iota(jnp.int32, (CHUNK, HEAD), 1)
        db_all = jnp.zeros((CHUNK, HEAD), F32)
        dgc_all = jnp.zeros((CHUNK, HEAD), F32)
        for grp, (dq, dk, dv, db, dgc, dgr, *dst) in enumerate(res):
            dgr_ref[grp] = dgr
            for hh in range(GROUP):
                h = grp * GROUP + hh
                cs, rows = slice(h * HEAD, (h + 1) * HEAD), slice(hh * CHUNK, (hh + 1) * CHUNK)
                dq_ref[:, cs] = dq[rows]
                dk_ref[:, cs] = dk[rows]
                dv_ref[:, cs] = dv[rows]
                dstate[h] = dst[hh]
                db_all = jnp.where(lane == h, db[rows], db_all)
                dgc_all = jnp.where(lane == h, dgc[rows], dgc_all)
        db_ref[...] = db_all
        dgc_ref[...] = dgc_all

    blk = pl.BlockSpec((CHUNK, D), lambda n: (at(n), 0))
    gblk = pl.BlockSpec((CHUNK, HEAD), lambda n: (at(n), 0))
    return pl.pallas_call(
        body, name=name, grid=(nc,), in_specs=in_specs,
        out_specs=[blk, blk, blk, gblk, gblk, pl.BlockSpec((None, N_GROUPS, 1, GROWS), lambda n: (at(n), 0, 0, 0))],
        out_shape=[jax.ShapeDtypeStruct((s, D), F32)] * 3 + [jax.ShapeDtypeStruct((s, HEAD), F32)] * 2
        + [jax.ShapeDtypeStruct((nc, N_GROUPS, 1, GROWS), F32)],
        scratch_shapes=[pltpu.VMEM((HEADS, HEAD, HEAD), F32)],
        compiler_params=_params(("arbitrary",)),
    )(qkv, qkv, qkv, beta, gcum, grow, states, invs, do)


ATT_TILE = 512
ATT_SCALE = QK_HEAD ** -0.5


def _att_mask(t):
    qpos = lax.broadcasted_iota(jnp.int32, (t, t), 0)
    kpos = lax.broadcasted_iota(jnp.int32, (t, t), 1)
    return (kpos // CHUNK) <= (qpos // CHUNK)


ATT_STRIP = 32


def _att_strip_mask(r, t):
    kpos = lax.broadcasted_iota(jnp.int32, (ATT_STRIP, t), 1)
    return (kpos // CHUNK) <= (r * ATT_STRIP) // CHUNK


def _att_pairs(nb, by_query):
    if by_query:
        pairs = [(i, j) for i in range(nb) for j in range(i + 1)]
    else:
        pairs = [(j, i) for j in range(nb) for i in range(j, nb)]
    return jnp.array([a for a, _ in pairs], jnp.int32), jnp.array([b for _, b in pairs], jnp.int32)


def _attn_fwd(q, k, v, name):
    s = q.shape[0]
    t = min(ATT_TILE, s)
    nb = s // t
    ii, jj = _att_pairs(nb, by_query=True)

    def body(ii_ref, jj_ref, q_ref, k_ref, v_ref, o_ref, lse_ref, m_s, l_s, acc):
        step = pl.program_id(1)
        i, j = ii_ref[step], jj_ref[step]

        @pl.when(j == 0)
        def _():
            m_s[...] = jnp.full_like(m_s, -jnp.inf)
            l_s[...] = jnp.zeros_like(l_s)
            acc[...] = jnp.zeros_like(acc)

        sc = lax.dot_general(q_ref[...], k_ref[...], _NT, preferred_element_type=F32) * ATT_SCALE
        sc = lax.cond(i == j, lambda u: jnp.where(_att_mask(t), u, -jnp.inf), lambda u: u, sc)
        m_new = jnp.maximum(m_s[...], jnp.max(sc, axis=-1, keepdims=True))
        alpha = jnp.exp(m_s[...] - m_new)
        p = jnp.exp(sc - m_new)
        l_s[...] = alpha * l_s[...] + jnp.sum(p, axis=-1, keepdims=True)
        acc[...] = alpha * acc[...] + jnp.dot(p.astype(BF16), v_ref[...], preferred_element_type=F32)
        m_s[...] = m_new

        @pl.when(j == i)
        def _():
            o_ref[...] = acc[...] / l_s[...]
            lse_ref[...] = m_s[...] + jnp.log(l_s[...])

    grid_spec = pltpu.PrefetchScalarGridSpec(
        num_scalar_prefetch=2, grid=(HEADS, len(ii)),
        in_specs=[pl.BlockSpec((t, HEAD_PAD), lambda h, n, ir, jr: (ir[n], h)),
                  pl.BlockSpec((t, HEAD_PAD), lambda h, n, ir, jr: (jr[n], h)),
                  pl.BlockSpec((t, HEAD), lambda h, n, ir, jr: (jr[n], h))],
        out_specs=[pl.BlockSpec((t, HEAD), lambda h, n, ir, jr: (ir[n], h)),
                   pl.BlockSpec((None, t, 1), lambda h, n, ir, jr: (h, ir[n], 0))],
        scratch_shapes=[pltpu.VMEM((t, 1), F32), pltpu.VMEM((t, 1), F32), pltpu.VMEM((t, HEAD), F32)])
    return pl.pallas_call(
        body, name=name, grid_spec=grid_spec,
        out_shape=[jax.ShapeDtypeStruct((s, HEADS * HEAD), F32), jax.ShapeDtypeStruct((HEADS, s, 1), F32)],
        compiler_params=_params(("parallel", "arbitrary")),
    )(ii, jj, q, k, v)


def _attn_bwd(q, k, v, do, o, lse, name, dkv_sum=None):
    s = q.shape[0]
    t = min(ATT_TILE, s)
    nb = s // t
    jj, ii = _att_pairs(nb, by_query=False)
    nsum = 2 if dkv_sum else 0

    def body(jj_ref, ii_ref, q_ref, k_ref, v_ref, do_ref, o_ref, lse_ref, *rest):
        dq_ref, dk_ref, dv_ref, dk_acc, dv_acc, sc_s, dp_s, p_s, ds_s, dl_s = rest[nsum:]
        step = pl.program_id(1)
        i, j = ii_ref[step], jj_ref[step]

        @pl.when(step == 0)
        def _():
            dq_ref[...] = jnp.zeros_like(dq_ref)

        @pl.when(i == j)
        def _():
            dk_acc[...] = jnp.zeros_like(dk_acc)
            dv_acc[...] = jnp.zeros_like(dv_acc)

        do_f = do_ref[...]
        dob = do_f.astype(BF16)
        dl_s[...] = jnp.sum(do_f * o_ref[...], axis=-1, keepdims=True)
        sc_s[...] = lax.dot_general(q_ref[...], k_ref[...], _NT, preferred_element_type=F32)
        dp_s[...] = lax.dot_general(dob, v_ref[...], _NT, preferred_element_type=F32)

        def softmax_strips(diagonal):
            for r in range(t // ATT_STRIP):
                rows = slice(r * ATT_STRIP, (r + 1) * ATT_STRIP)
                p = jnp.exp(sc_s[rows, :] * ATT_SCALE - lse_ref[rows, :])
                if diagonal:
                    p = jnp.where(_att_strip_mask(r, t), p, 0.0)
                p_s[rows, :] = p.astype(BF16)
                ds_s[rows, :] = (p * (dp_s[rows, :] - dl_s[rows, :]) * ATT_SCALE).astype(BF16)

        pl.when(i == j)(functools.partial(softmax_strips, True))
        pl.when(i != j)(functools.partial(softmax_strips, False))
        ds = ds_s[...]
        dv_acc[...] += lax.dot_general(p_s[...], dob, _TN, preferred_element_type=F32)
        dk_acc[...] += lax.dot_general(ds, q_ref[...], _TN, preferred_element_type=F32)
        rows = pl.ds(pl.multiple_of(i * t, t), t)
        dq_ref[rows, :] += jnp.dot(ds, k_ref[...], preferred_element_type=F32)

        @pl.when(i == nb - 1)
        def _():
            dk_ref[...] = dk_acc[...] + rest[0][...] if nsum else dk_acc[...]
            dv_ref[...] = dv_acc[...] + rest[1][...] if nsum else dv_acc[...]

    dk_blk = pl.BlockSpec((t, HEAD_PAD), lambda h, n, jr, ir: (jr[n], h))
    dv_blk = pl.BlockSpec((t, HEAD), lambda h, n, jr, ir: (jr[n], h))
    grid_spec = pltpu.PrefetchScalarGridSpec(
        num_scalar_prefetch=2, grid=(HEADS, len(jj)),
        in_specs=[pl.BlockSpec((t, HEAD_PAD), lambda h, n, jr, ir: (ir[n], h)),
                  pl.BlockSpec((t, HEAD_PAD), lambda h, n, jr, ir: (jr[n], h)),
                  pl.BlockSpec((t, HEAD), lambda h, n, jr, ir: (jr[n], h)),
                  pl.BlockSpec((t, HEAD), lambda h, n, jr, ir: (ir[n], h)),
                  pl.BlockSpec((t, HEAD), lambda h, n, jr, ir: (ir[n], h)),
                  pl.BlockSpec((None, t, 1), lambda h, n, jr, ir: (h, ir[n], 0))] + [dk_blk, dv_blk][:nsum],
        out_specs=[pl.BlockSpec((s, HEAD_PAD), lambda h, n, jr, ir: (0, h)), dk_blk, dv_blk],
        scratch_shapes=[pltpu.VMEM((t, HEAD_PAD), F32), pltpu.VMEM((t, HEAD), F32), pltpu.VMEM((t, t), F32),
                        pltpu.VMEM((t, t), F32), pltpu.VMEM((t, t), BF16), pltpu.VMEM((t, t), BF16),
                        pltpu.VMEM((t, 1), F32)])
    return pl.pallas_call(
        body, name=name, grid_spec=grid_spec,
        out_shape=[jax.ShapeDtypeStruct((s, HEADS * HEAD_PAD), F32)] * 2 + [jax.ShapeDtypeStruct((s, HEADS * HEAD), F32)],
        compiler_params=_params(("parallel", "arbitrary")),
    )(jj, ii, q, k, v, do, o, lse, *(dkv_sum or ()))


def _rope_tables(positions):
    half = ROPE // 2
    inv_freq = ROPE_BASE ** (-jnp.arange(half, dtype=F32) / half)
    ang = positions.astype(F32)[:, None] * inv_freq
    cos, sin = jnp.cos(ang), jnp.sin(ang)
    return jnp.concatenate([cos] * 4, axis=1), jnp.concatenate([-sin, sin] * 2, axis=1)


def _loss_and_grad(y, target, name):
    s = y.shape[0]
    ts = min(ROW_TILE, s)

    def body(y_ref, t_ref, dy_ref, l_ref):
        e = y_ref[...] - t_ref[...]
        dy_ref[...] = e * (1.0 / D)
        part = jnp.sum(jnp.sum(e * e, axis=-1, keepdims=True) * (0.5 / D), axis=0, keepdims=True)
        part = part * jnp.ones((1, 128), F32)

        @pl.when(pl.program_id(0) == 0)
        def _():
            l_ref[...] = part

        @pl.when(pl.program_id(0) > 0)
        def _():
            l_ref[...] += part

    return pl.pallas_call(
        body, name=name, grid=(s // ts,),
        in_specs=[pl.BlockSpec((ts, D), lambda i: (i, 0))] * 2,
        out_specs=[pl.BlockSpec((ts, D), lambda i: (i, 0)), pl.BlockSpec((1, 128), lambda i: (0, 0))],
        out_shape=[jax.ShapeDtypeStruct((s, D), F32), jax.ShapeDtypeStruct((1, 128), F32)],
        compiler_params=_params(("arbitrary",)),
    )(y, target)


ANY = pl.BlockSpec(memory_space=pl.ANY)


def _all_gather(shard, name):
    def body(x_ref, out_ref, send_sems, recv_sems, local_sem):
        x, y, c = lax.axis_index("x"), lax.axis_index("y"), lax.axis_index("c")
        me, sibling = (x, y, c), (x, y, 1 - c)
        chips = [(1 - x, y), (x, 1 - y), (1 - x, 1 - y)]

        def rows(px, py, pc):
            return out_ref.at[4 * px + 2 * py + pc]

        def copy(k, block, to, src=None):
            return pltpu.make_async_remote_copy(
                src_ref=rows(*block) if src is None else src, dst_ref=rows(*block),
                send_sem=send_sems.at[k], recv_sem=recv_sems.at[k], device_id=to, device_id_type=MESH)

        mine = pltpu.make_async_copy(x_ref, rows(*me), local_sem)
        mine.start()
        first = [copy(0, me, sibling, src=x_ref)]
        first += [copy(1 + j, me, (*chip, c), src=x_ref) for j, chip in enumerate(chips)]
        for cp in first:
            cp.start()
        passed = [copy(4 + j, (*chip, c), sibling) for j, chip in enumerate(chips)]
        for j, chip in enumerate(chips):
            copy(1 + j, (*chip, c), me).wait_recv()
            passed[j].start()
        copy(0, sibling, me).wait_recv()
        for j, chip in enumerate(chips):
            copy(4 + j, (*chip, 1 - c), me).wait_recv()
        for cp in first + passed:
            cp.wait_send()
        mine.wait()

    return pl.pallas_call(
        body, name=name, out_shape=jax.ShapeDtypeStruct((N_DEV,) + shard.shape, shard.dtype),
        in_specs=[ANY], out_specs=ANY,
        scratch_shapes=[pltpu.SemaphoreType.DMA((7,)), pltpu.SemaphoreType.DMA((7,)), pltpu.SemaphoreType.DMA],
    )(shard)


def _exchange(blocks, name):
    def body(x_ref, out_ref, send_sems, recv_sems, local_sem):
        x, y, c = lax.axis_index("x"), lax.axis_index("y"), lax.axis_index("c")
        me = 4 * x + 2 * y + c
        mine = pltpu.make_async_copy(x_ref.at[me], out_ref.at[me], local_sem)
        mine.start()
        copies = []
        for k in range(1, N_DEV):
            px = 1 - x if k & 4 else x
            py = 1 - y if k & 2 else y
            pc = 1 - c if k & 1 else c
            peer = 4 * px + 2 * py + pc
            cp = pltpu.make_async_remote_copy(
                src_ref=x_ref.at[peer], dst_ref=out_ref.at[me], send_sem=send_sems.at[k - 1],
                recv_sem=recv_sems.at[k - 1], device_id=(px, py, pc), device_id_type=MESH)
            cp.start()
            copies.append((cp, pltpu.make_async_remote_copy(
                src_ref=x_ref.at[peer], dst_ref=out_ref.at[peer], send_sem=send_sems.at[k - 1],
                recv_sem=recv_sems.at[k - 1], device_id=(px, py, pc), device_id_type=MESH)))
        for cp, landing in copies:
            landing.wait_recv()
        for cp, landing in copies:
            cp.wait_send()
        mine.wait()

    return pl.pallas_call(
        body, name=name, out_shape=jax.ShapeDtypeStruct(blocks.shape, blocks.dtype),
        in_specs=[ANY], out_specs=ANY,
        scratch_shapes=[pltpu.SemaphoreType.DMA((7,)), pltpu.SemaphoreType.DMA((7,)), pltpu.SemaphoreType.DMA],
    )(blocks)


HBM = pl.BlockSpec(memory_space=pltpu.HBM)
SEM = pl.BlockSpec(memory_space=pltpu.SEMAPHORE)
EFFECT = pltpu.SideEffectType.DATAFLOW_SIDE_EFFECTING


def _peers():
    x, y, c = lax.axis_index("x"), lax.axis_index("y"), lax.axis_index("c")
    peers = []
    for k in range(1, N_DEV):
        px = 1 - x if k & 4 else x
        py = 1 - y if k & 2 else y
        pc = 1 - c if k & 1 else c
        peers.append(((px, py, pc), 4 * px + 2 * py + pc))
    return 4 * x + 2 * y + c, peers


def _send_start(srcs, name, gather):
    n = len(srcs)
    lands = [((N_DEV,) + s.shape) if gather else s.shape for s in srcs]

    def body(*refs):
        src_refs, land_refs = refs[:n], refs[n:2 * n]
        send_sems, recv_sems, token = refs[2 * n], refs[2 * n + 1], refs[-1]
        me, peers = _peers()
        for i in range(n):
            for k, (dev, idx) in enumerate(peers):
                pltpu.make_async_remote_copy(
                    src_ref=src_refs[i] if gather else src_refs[i].at[idx], dst_ref=land_refs[i].at[me],
                    send_sem=send_sems.at[7 * i + k], recv_sem=recv_sems.at[7 * i + k], device_id=dev,
                    device_id_type=MESH).start()
        token[...] = jnp.zeros_like(token)

    res = pl.pallas_call(
        body, name=name,
        out_shape=(pltpu.SemaphoreType.DMA((7 * n,)), pltpu.SemaphoreType.DMA((7 * n,)),
                   *[pltpu.HBM(s.shape, s.dtype) for s in srcs],
                   *[pltpu.HBM(shape, s.dtype) for shape, s in zip(lands, srcs)],
                   jax.ShapeDtypeStruct((8, 128), F32)),
        in_specs=(HBM,) * (2 * n), out_specs=(SEM, SEM) + (HBM,) * (2 * n) + (pl.BlockSpec(memory_space=pltpu.VMEM),),
        input_output_aliases={i: 2 + i for i in range(2 * n)},
        compiler_params=pltpu.CompilerParams(has_side_effects=EFFECT),
    )(*[pltpu.with_memory_space_constraint(s, pltpu.HBM) for s in srcs],
      *[pltpu.with_memory_space_constraint(lax.empty(shape, s.dtype), pltpu.HBM) for shape, s in zip(lands, srcs)])
    return dict(sems=res[:2], srcs=res[2:2 + n], lands=res[2 + n:2 + 2 * n], token=res[-1])


def _send_wait(handle, after, name, gather):
    n = len(handle["srcs"])

    def body(*refs):
        src_refs, land_refs = refs[:n], refs[n:2 * n]
        send_sems, recv_sems = refs[2 * n], refs[2 * n + 1]
        me, peers = _peers()
        for i in range(n):
            for k, (dev, idx) in enumerate(peers):
                cp = pltpu.make_async_remote_copy(
                    src_ref=src_refs[i] if gather else src_refs[i].at[idx], dst_ref=land_refs[i].at[idx],
                    send_sem=send_sems.at[7 * i + k], recv_sem=recv_sems.at[7 * i + k], device_id=dev,
                    device_id_type=MESH)
                cp.wait_send()
                cp.wait_recv()

    both = list(handle["srcs"]) + list(handle["lands"])
    res = pl.pallas_call(
        body, name=name, out_shape=tuple(pltpu.HBM(t.shape, t.dtype) for t in both),
        in_specs=(HBM,) * (2 * n) + (SEM, SEM, pl.BlockSpec(memory_space=pl.ANY)), out_specs=(HBM,) * (2 * n),
        input_output_aliases={i: i for i in range(2 * n)},
        compiler_params=pltpu.CompilerParams(has_side_effects=EFFECT),
    )(*both, *handle["sems"], after)
    return res[:n], res[n:]


def _adamw(parts, w, m, v, name, tr=128):
    pieces = len(parts)
    n, r, wd = parts[0].shape
    tr = next((t for t in (tr, 64, 32, 16) if r % t == 0), r)
    nrt = r // tr

    def body(*refs):
        w_ref, m_ref, v_ref, g_ref, d_ref, nm_ref, nv_ref = refs[pieces:]

        def update(p_ref):
            g = p_ref[0].astype(F32)
            for k in range(1, n):
                g = g + p_ref[k].astype(F32)
            m_new = B1 * m_ref[...] + (1.0 - B1) * g
            v_new = B2 * v_ref[...] + (1.0 - B2) * (g * g)
            m_hat = m_new / (1.0 - B1 ** STEP)
            v_hat = v_new / (1.0 - B2 ** STEP)
            g_ref[...] = g
            d_ref[...] = -LR * (m_hat / (jnp.sqrt(v_hat) + ADAM_EPS) + WD * w_ref[...])
            nm_ref[...] = m_new
            nv_ref[...] = v_new

        for p in range(pieces):
            pl.when(pl.program_id(0) == p)(functools.partial(update, refs[p]))

    part_spec = lambda p: pl.BlockSpec((n, tr, wd), lambda l, i: (0, jnp.clip(i + (l - p) * nrt, 0, nrt - 1), 0))
    blk = pl.BlockSpec((tr, wd), lambda l, i: (l * nrt + i, 0))
    return pl.pallas_call(
        body, name=name, grid=(pieces, nrt),
        in_specs=[part_spec(p) for p in range(pieces)] + [blk, blk, blk],
        out_specs=[blk] * 4, out_shape=[jax.ShapeDtypeStruct((pieces * r, wd), F32)] * 4,
        compiler_params=_params(("arbitrary", "arbitrary")),
    )(*parts, w, m, v)


def _outer8(ct, dm, name):
    k, n = ct.shape[0], dm.shape[1]

    def body(c_ref, d_ref, o_ref):
        cv, dv = c_ref[...], d_ref[...]
        acc = cv[:, 0:1] * dv[0:1, :]
        for s in range(1, N_DEV):
            acc = acc + cv[:, s:s + 1] * dv[s:s + 1, :]
        o_ref[...] = acc

    tk = 256
    return pl.pallas_call(
        body, name=name, grid=(k // tk,),
        in_specs=[pl.BlockSpec((tk, N_DEV), lambda i: (i, 0)), pl.BlockSpec((N_DEV, n), lambda i: (0, 0))],
        out_specs=pl.BlockSpec((tk, n), lambda i: (i, 0)), out_shape=jax.ShapeDtypeStruct((k, n), F32),
        compiler_params=_params(("parallel",)),
    )(ct, dm)


FULL = (0, D)
C128 = (0, 128)
HEAD_NOPE = [(h * HEAD_PAD, NOPE) for h in range(HEADS)]
HEAD_ROPE = [(h * HEAD_PAD + NOPE, 128) for h in range(HEADS)]
HEAD_ALL = [(h * HEAD_PAD, HEAD_PAD) for h in range(HEADS)]
HEAD_V = [(h * HEAD, HEAD) for h in range(HEADS)]


def _modulate(x, p, ties=()):
    return _rowwise_fwd(_modulate_fn, [(x, FULL)], [p["gain"], p["scale"], p["shift"]], [(D, BF16, FULL)], "modulate",
                        ties=ties)[0]


def _residual_bwd(y, gm, dxn):
    return _rowwise_bwd(_gate_only_fn, [(y, FULL)], [gm], [(D, F32, FULL)], [dxn], [(0, FULL)], [(D, BF16)],
                        "residual_bwd")


def _modulate_bwd(x, p, dh, dx_in, prev=None):
    pars = [p["gain"], p["scale"], p["shift"]]
    if prev is None:
        return list(_rowwise_bwd(_modulate_fn, [(x, FULL)], pars, [(D, BF16, FULL)], [dh], [(0, FULL)], [(D, F32)],
                                 "modulate_bwd", add={0: dx_in})) + [None]
    s = x.shape[0]
    ts = min(ROW_TILE, s)

    def body(x_ref, g_ref, sc_ref, sh_ref, dh_ref, din_ref, y_ref, gm_ref, dx_ref, dy_ref, dg_ref, dsc_ref, dsh_ref,
             dgm_ref):
        _, vjp = jax.vjp(lambda *t: _modulate_fn(0, *t)[0], x_ref[...], g_ref[...], sc_ref[...], sh_ref[...])
        dxm, dg, dsc, dsh = vjp(dh_ref[...])
        dx = dxm + din_ref[...]
        dx_ref[...] = dx
        dy_ref[...] = (gm_ref[...] * dx).astype(BF16)
        sums = (dg, dsc, dsh, jnp.sum(dx * y_ref[...], axis=0, keepdims=True))
        first = pl.program_id(0) == 0
        for ref, val in zip((dg_ref, dsc_ref, dsh_ref, dgm_ref), sums):
            @pl.when(first)
            def _(ref=ref, val=val):
                ref[...] = val

            @pl.when(jnp.logical_not(first))
            def _(ref=ref, val=val):
                ref[...] += val

    blk = pl.BlockSpec((ts, D), lambda i: (i, 0))
    vec = pl.BlockSpec((1, D), lambda i: (0, 0))
    dx, dy, dg, dsc, dsh, dgm = pl.pallas_call(
        body, name="modulate_bwd_chain", grid=(s // ts,),
        in_specs=[blk, vec, vec, vec, blk, blk, blk, vec], out_specs=[blk, blk, vec, vec, vec, vec],
        out_shape=[jax.ShapeDtypeStruct((s, D), F32), jax.ShapeDtypeStruct((s, D), BF16)]
        + [jax.ShapeDtypeStruct((1, D), F32)] * 4,
        compiler_params=_params(("arbitrary",)),
    )(x, *pars, dh, dx_in, prev[0], prev[1])
    return [dx, dg, dsc, dsh, (dy, dgm)]


def _out_proj(a, w, x, p, nxt, name, **kw):
    res = _matmul([(a, w)], "nn", name, out_dtype=BF16, resid=(x, p["gm"]) + tuple(nxt or ()), **kw)
    return res[1], res[0], (res[2] if nxt else None)


def _ffn_fwd(x, p, ties=(), h=None, nxt=None):
    if h is None:
        h, ties = _modulate(x, p, ties), ()
    gate, up, act = _ffn_in(h, p["w_in"], "ffn_in", ties=ties)
    res = _ffn_out(act, p["wo"], (x, p["gm"]) + tuple(nxt or ()), "ffn_out")
    return res[1], dict(x=x, h=h, gate=gate, up=up, act=act, y=res[0]), (res[2] if nxt else None)


def _ffn_bwd(t, p, dxn, res=None, prev=None, ties=()):
    dy, dgm = res or _residual_bwd(t["y"], p["gm"], dxn)
    dgate, dup = _ffn_bwd_act(dy, p["wo"], t["gate"], t["up"], "ffn_bwd_act", ties=ties)
    dwo = _ffn_dwo(t["act"], dy, "ffn_dwo", ties=ties)
    dh = _ffn_dh(dgate, dup, p["w_in"], "ffn_dh")
    dwi = _ffn_dwi(t["h"], dgate, dup, "ffn_dwi")
    dx, dgain, dscale, dshift, res_prev = _modulate_bwd(t["x"], p, dh, dxn, prev)
    return dx, dict(gain=dgain, scale=dscale, shift=dshift, gm=dgm, w_in=dwi, wo=dwo), res_prev


def _pad128(t):
    return jnp.pad(t, ((0, 0), (0, 128 - t.shape[1])))


def _gdn_fwd(x, p, ties=(), h=None, nxt=None):
    s = x.shape[0]
    if h is None:
        h, ties = _modulate(x, p, ties), ()
    pm = _mm(h, p["w_main"], "nn", "gdn_proj", ties=ties)
    tail = _mm(h, p["w_tail"], "nn", "gdn_proj_tail", ties=ties)
    qkv = _gdn_conv_fwd(pm, p["conv_w"], "gdn_conv")
    beta, gcum = _rowwise_fwd(_gdn_gates_fn, [(tail, C128), (tail, (128, 128))], [p["a_log"], p["dt_bias"]],
                              [(128, F32, C128)] * 2, "gdn_gates")
    grow = gcum[:, :HEADS].reshape(s // CHUNK, CHUNK, N_GROUPS, GROUP).transpose(0, 2, 3, 1)
    grow = grow.reshape(s // CHUNK, N_GROUPS, 1, GROWS)
    o, states, invs = _gdn_scan_fwd(qkv, beta, gcum, grow, "gdn_scan")
    on, = _rowwise_fwd(_gdn_outnorm_fn, [(o, HEAD_V), (pm, [(3 * D + h_ * HEAD, HEAD) for h_ in range(HEADS)])],
                       [p["norm_g"]], [(D, BF16, HEAD_V)], "gdn_outnorm", groups=HEADS)
    xn, y, hn = _out_proj(on, p["w_out"], x, p, nxt, "mix_out", tm=512)
    t = dict(x=x, h=h, pm=pm, tail=tail, qkv=qkv, beta=beta, gcum=gcum, grow=grow, o=o, states=states, invs=invs,
             on=on, y=y)
    return xn, t, hn


def _gdn_bwd(t, p, dxn, res=None, prev=None, ties=()):
    s = dxn.shape[0]
    zc = [(3 * D + h_ * HEAD, HEAD) for h_ in range(HEADS)]
    dy, dgm = res or _residual_bwd(t["y"], p["gm"], dxn)
    dw_out = _mm(t["on"], dy, "tn", "mix_dwo", ties=ties)
    don = _mm(dy, p["w_out"], "nt", "mix_dout", ties=ties)
    do, dz, dnorm_g = _rowwise_bwd(_gdn_outnorm_fn, [(t["o"], HEAD_V), (t["pm"], zc)], [p["norm_g"]],
                                   [(D, BF16, HEAD_V)], [don], [(0, HEAD_V), (1, HEAD_V)], [(D, F32), (D, BF16)],
                                   "gdn_outnorm_bwd", groups=HEADS)
    dq, dk, dv, dbeta, dg, dgr = _gdn_scan_bwd(t["qkv"], t["beta"], t["gcum"], t["grow"], t["states"], t["invs"], do,
                                               "gdn_scan_bwd")
    dg = dg + _pad128(dgr.reshape(s // CHUNK, N_GROUPS, GROUP, CHUNK).transpose(0, 3, 1, 2).reshape(s, HEADS))
    dtail, da_log, ddt = _rowwise_bwd(_gdn_gates_fn, [(t["tail"], C128), (t["tail"], (128, 128))],
                                      [p["a_log"], p["dt_bias"]], [(128, F32, C128)] * 2, [dbeta, dg],
                                      [(0, C128), (0, (128, 128))], [(256, F32)], "gdn_gates_bwd")
    dxs, dcw = [], []
    for part, d in enumerate((dq, dk, dv)):
        dx_, dw_ = _gdn_conv_bwd(t["pm"], p["conv_w"], d, part, "gdn_conv_bwd")
        dxs.append(dx_)
        dcw.append(dw_)
    pieces = dxs + [dz]
    dh = _matmul([(d, p["w_main"]) for d in pieces] + [(dtail, p["w_tail"])], "nt", "gdn_dh",
                 boffs=[0, D, 2 * D, 3 * D, 0], tk=512)
    dw_main = [_mm(t["h"], d, "tn", "gdn_dwi") for d in pieces]
    dw_tail = _mm(t["h"], dtail, "tn", "gdn_dwi_tail")
    dx, dgain, dscale, dshift, res_prev = _modulate_bwd(t["x"], p, dh, dxn, prev)
    return dx, dict(gain=dgain, scale=dscale, shift=dshift, gm=dgm, w_main=jnp.concatenate(dw_main, axis=1),
                    w_tail=dw_tail, conv_w=jnp.concatenate(dcw, axis=1), a_log=da_log, dt_bias=ddt,
                    norm_g=dnorm_g, w_out=dw_out), res_prev


def _q_rows(q2, cosf, sins):
    return [(q2, HEAD_NOPE), (q2, HEAD_ROPE), (cosf, C128), (sins, C128)]


def _mla_fwd(x, p, kv, ties=(), h=None, nxt=None):
    if h is None:
        h, ties = _modulate(x, p, ties), ()
    cq = _mm(h, p["w_dq"], "nn", "mla_dq", ties=ties)
    cqn, = _rowwise_fwd(_rms_fn, [(cq, (0, Q_LORA))], [p["q_lora_g"]], [(Q_LORA, BF16, (0, Q_LORA))], "mla_qlora_norm")
    q2 = _mm(cqn, p["w_uq"], "nn", "mla_uq")
    qn, = _rowwise_fwd(_q_norm_rope_fn, _q_rows(q2, kv["cosf"], kv["sins"]), [p["q_gn"], p["q_gr"]],
                       [(HEADS * HEAD_PAD, BF16, HEAD_ALL)], "mla_q_norm", groups=HEADS)
    o, lse = _attn_fwd(qn, kv["kn"], kv["vb"], "mla_attn")
    xn, y, hn = _out_proj(o, p["w_out"], x, p, nxt, "mix_out", tm=512)
    return xn, dict(x=x, h=h, cq=cq, cqn=cqn, q2=q2, qn=qn, o=o, lse=lse, y=y), hn


def _mla_bwd(t, p, kv, dxn, res=None, prev=None, ties=(), dkv_sum=None):
    dy, dgm = res or _residual_bwd(t["y"], p["gm"], dxn)
    dw_out = _mm(t["o"], dy, "tn", "mix_dwo", ties=ties)
    do = _mm(dy, p["w_out"], "nt", "mix_dout", ties=ties)
    dq, dk, dv = _attn_bwd(t["qn"], kv["kn"], kv["vb"], do, t["o"], t["lse"], "mla_attn_bwd", dkv_sum)
    dq2, dq_gn, dq_gr = _rowwise_bwd(_q_norm_rope_fn, _q_rows(t["q2"], kv["cosf"], kv["sins"]), [p["q_gn"], p["q_gr"]],
                                     [(HEADS * HEAD_PAD, BF16, HEAD_ALL)], [dq],
                                     [(0, HEAD_NOPE), (0, HEAD_ROPE), None, None], [(HEADS * HEAD_PAD, BF16)],
                                     "mla_q_norm_bwd", groups=HEADS)
    dw_uq = _mm(t["cqn"], dq2, "tn", "mla_dwuq")
    dcqn = _mm(dq2, p["w_uq"], "nt", "mla_dcq")
    dcq, dq_lora_g = _rowwise_bwd(_rms_fn, [(t["cq"], (0, Q_LORA))], [p["q_lora_g"]], [(Q_LORA, BF16, (0, Q_LORA))],
                                  [dcqn], [(0, (0, Q_LORA))], [(Q_LORA, BF16)], "mla_qlora_norm_bwd")
    dw_dq = _mm(t["h"], dcq, "tn", "mla_dwdq")
    dh = _mm(dcq, p["w_dq"], "nt", "mla_dh")
    dx, dgain, dscale, dshift, res_prev = _modulate_bwd(t["x"], p, dh, dxn, prev)
    grads = dict(gain=dgain, scale=dscale, shift=dshift, gm=dgm, w_dq=dw_dq, q_lora_g=dq_lora_g, w_uq=dw_uq,
                 q_gn=dq_gn, q_gr=dq_gr, w_out=dw_out)
    return dx, grads, res_prev, dk, dv


def _k_rows(kvp, ckv, cosf, sins):
    return [(kvp, HEAD_NOPE), (kvp, HEAD_ROPE), (ckv, (KV_LORA, 128)), (cosf, C128), (sins, C128)]


def _kv_fwd(x, p, cosf, sins):
    h = _modulate(x, p)
    ckv = _mm(h, p["w_dkv"], "nn", "kv_down")
    lat, = _rowwise_fwd(_rms_fn, [(ckv, (0, KV_LORA))], [p["kv_g"]], [(KV_LORA, BF16, (0, KV_LORA))], "kv_norm")
    kvp = _mm(lat, p["w_ukv"], "nn", "kv_up")
    kn, vb = _rowwise_fwd(_k_norm_rope_fn, _k_rows(kvp, ckv, cosf, sins), [p["k_gn"], p["k_gr"]],
                          [(HEADS * HEAD_PAD, BF16, HEAD_ALL), (HEADS * HEAD, BF16, HEAD_V)], "kv_k_norm",
                          groups=HEADS)
    return dict(x=x, h=h, ckv=ckv, lat=lat, kvp=kvp, kn=kn, vb=vb, cosf=cosf, sins=sins)


def _kv_bwd(t, p, dk, dv, dx_in, prev):
    dkvp, drope, dk_gn, dk_gr = _rowwise_bwd(
        _k_norm_rope_fn, _k_rows(t["kvp"], t["ckv"], t["cosf"], t["sins"]), [p["k_gn"], p["k_gr"]],
        [(HEADS * HEAD_PAD, BF16, HEAD_ALL), (HEADS * HEAD, BF16, HEAD_V)], [dk, dv],
        [(0, HEAD_NOPE), (0, HEAD_ROPE), (1, C128), None, None], [(HEADS * HEAD_PAD, BF16), (128, F32)],
        "kv_k_norm_bwd", groups=HEADS)
    dw_ukv = _mm(t["lat"], dkvp, "tn", "kv_dwukv")
    dlat = _mm(dkvp, p["w_ukv"], "nt", "kv_dlat")
    dckv, dkv_g = _rowwise_bwd(_rms_fn, [(t["ckv"], (0, KV_LORA))], [p["kv_g"]], [(KV_LORA, BF16, (0, KV_LORA))],
                               [dlat], [(0, (0, KV_LORA))], [(KV_LORA, F32)], "kv_norm_bwd")
    dw_dkv = jnp.concatenate([_mm(t["h"], dckv, "tn", "kv_dwdkv"), _mm(t["h"], drope, "tn", "kv_dwdkv_rope")], axis=1)
    dh = _matmul([(dckv, p["w_dkv"]), (drope, p["w_dkv"])], "nt", "kv_dh", boffs=[0, KV_LORA])
    dx, dgain, dscale, dshift, res_prev = _modulate_bwd(t["x"], p, dh, dx_in, prev)
    return dx, dict(gain=dgain, scale=dscale, shift=dshift, w_dkv=dw_dkv, kv_g=dkv_g, w_ukv=dw_ukv, k_gn=dk_gn,
                    k_gr=dk_gr), res_prev


WEIGHTS = ["ada_w", "ada_b", "norm_g", "ffn_w_in", "ffn_w_out", "gdn_w_in", "gdn_conv_w", "gdn_a_log", "gdn_dt_bias",
           "gdn_norm_g", "gdn_w_out", "kv_ada_w", "kv_ada_b", "kv_norm_g", "mla_w_dkv", "mla_kv_norm_g", "mla_w_ukv",
           "mla_k_norm_g", "mla_w_dq", "mla_q_lora_norm_g", "mla_w_uq", "mla_q_norm_g", "mla_w_out"]
SMALL = [("ada_b", 4 * N_MOD * D), ("kv_ada_b", 2 * D), ("norm_g", DEPTH * 3 * D), ("gdn_conv_w", N_A * CONV_K * 3 * D),
         ("gdn_a_log", N_A * HEADS), ("gdn_dt_bias", N_A * HEADS), ("gdn_norm_g", N_A * HEAD), ("kv_norm_g", D),
         ("mla_kv_norm_g", KV_LORA), ("mla_k_norm_g", QK_HEAD), ("mla_q_lora_norm_g", 2 * Q_LORA),
         ("mla_q_norm_g", 2 * QK_HEAD)]
SMALL_REPLICATED = [n for n, _ in SMALL if n not in ("norm_g", "gdn_conv_w")]


def _silu_fn(g, t):
    return (_silu(t),)


def _dup_rope(t):
    return jnp.concatenate([t[..., :NOPE], t[..., NOPE:], t[..., NOPE:]], axis=-1)


def _fold_rope(t):
    return jnp.concatenate([t[..., :NOPE], t[..., NOPE:QK_HEAD] + t[..., QK_HEAD:]], axis=-1)


def _pack(pieces, rows):
    flat = jnp.concatenate([p.reshape(-1).astype(F32) for p in pieces])
    return jnp.pad(flat, (0, rows * 128 - flat.shape[0])).reshape(rows, 128)


def _step(a):
    me = 4 * lax.axis_index("x") + 2 * lax.axis_index("y") + lax.axis_index("c")
    x = a["x"][0]
    cosf, sins = _rope_tables(a["positions"][0])

    n_in = 2 * D_FF // N_DEV
    n_gdn = (4 * D + 2 * HEADS) // N_DEV
    AHEAD = 2

    stages = [(l, part) for l in range(DEPTH) for part in range(3)]

    def stage_shards(l, part):
        if part != 1:
            sh = {"ffn_w_in": a["ffn_w_in"][l, part // 2], "ffn_w_out": a["ffn_w_out"][l, part // 2]}
            if part == 2 and l == N_A - 1:
                sh.update(mla_w_dkv=a["mla_w_dkv"], mla_w_ukv=a["mla_w_ukv"])
            return sh
        if l < N_A:
            return {"gdn_w_in": a["gdn_w_in"][l], "gdn_w_out": a["gdn_w_out"][l]}
        j = l - N_A
        return {"mla_w_dq": a["mla_w_dq"][j], "mla_w_uq": a["mla_w_uq"][j], "mla_w_out": a["mla_w_out"][j]}

    def zero_of(t):
        return jnp.minimum(jnp.abs(t[(0,) * t.ndim].astype(F32)), 0.0)

    def start_stage(l, part, tie):
        sh = stage_shards(l, part)
        return list(sh), _send_start([(w + tie).astype(BF16) for w in sh.values()], f"fetch_start_{l}_{part}", gather=True)

    def finish_stage(l, part, names, handle, after):
        srcs, lands = _send_wait(handle, after, f"fetch_wait_{l}_{part}", gather=True)
        return {n: lax.dynamic_update_slice(land, src[None], (me, 0, 0)) for n, src, land in zip(names, srcs, lands)}

    n_cw, n_ng = N_A * CONV_K * 3 * HEAD, DEPTH * 3 * HEAD
    small_all = _all_gather(_pack([a["gdn_conv_w"], a["norm_g"], a["c"]], 44), "gather_small").reshape(N_DEV, -1)
    conv_w = small_all[:, :n_cw].reshape(N_DEV, N_A, CONV_K, 3 * HEAD).transpose(1, 2, 0, 3).reshape(N_A, CONV_K, 3 * D)
    norm_g = small_all[:, n_cw:n_cw + n_ng].reshape(N_DEV, DEPTH, 3, HEAD).transpose(1, 2, 0, 3).reshape(DEPTH, 3, D)
    c_all = small_all[:, n_cw + n_ng:n_cw + n_ng + D]

    c_act, = _rowwise_fwd(_silu_fn, [(c_all, FULL)], [], [(D, F32, FULL)], "c_act")
    n_ada = N_MOD * D // N_DEV
    parts = [_mm(c_act, a["ada_w"][l], "nn", "mod_proj") for l in range(DEPTH)]
    parts.append(_mm(c_act, a["kv_ada_w"], "nn", "mod_proj_kv"))
    mod_recv = _exchange(jnp.concatenate(parts, axis=1)[:, None, :], "exchange_mod")[:, 0]
    mod = mod_recv[:, :DEPTH * n_ada].reshape(N_DEV, DEPTH, n_ada).transpose(1, 0, 2).reshape(DEPTH, N_MOD * D)
    mod = (mod + a["ada_b"]).reshape(DEPTH, N_MOD, D)
    kvmod = mod_recv[:, DEPTH * n_ada:].reshape(2 * D) + a["kv_ada_b"]

    def row(v):
        return v[None]

    def ffn_params(l, i, w):
        k = 0 if i == 0 else 6
        return dict(gain=row(norm_g[l, 0 if i == 0 else 2]), shift=row(mod[l, k]), scale=row(mod[l, k + 1]),
                    gm=0.5 * row(mod[l, k + 2]), w_in=w["ffn_w_in"], wo=w["ffn_w_out"].reshape(D_FF, D))

    def gdn_params(l, w):
        w_in = w["gdn_w_in"].transpose(1, 0, 2).reshape(D, 4 * D + 2 * HEADS)
        pad = lambda t: jnp.pad(t, ((0, 0), (0, 128 - HEADS)))
        return dict(gain=row(norm_g[l, 1]), shift=row(mod[l, 3]), scale=row(mod[l, 4]), gm=row(mod[l, 5]),
                    w_main=w_in[:, :4 * D],
                    w_tail=jnp.concatenate([pad(w_in[:, 4 * D:4 * D + HEADS]), pad(w_in[:, 4 * D + HEADS:])], axis=1),
                    conv_w=conv_w[l], a_log=_pad128(row(a["gdn_a_log"][l])), dt_bias=_pad128(row(a["gdn_dt_bias"][l])),
                    norm_g=row(a["gdn_norm_g"][l]), w_out=w["gdn_w_out"].reshape(D, D))

    def mla_params(l, w):
        j = l - N_A
        uq = w["mla_w_uq"].transpose(1, 0, 2)
        qg = _dup_rope(a["mla_q_norm_g"][j])
        return dict(gain=row(norm_g[l, 1]), shift=row(mod[l, 3]), scale=row(mod[l, 4]), gm=row(mod[l, 5]),
                    w_dq=w["mla_w_dq"].reshape(D, Q_LORA), q_lora_g=row(a["mla_q_lora_norm_g"][j]),
                    w_uq=_dup_rope(uq).reshape(Q_LORA, HEADS * HEAD_PAD), q_gn=row(qg[:NOPE]), q_gr=row(qg[NOPE:]),
                    w_out=w["mla_w_out"].reshape(D, D))

    def kv_params(w):
        w_dkv = w["mla_w_dkv"].reshape(D, KV_LORA + ROPE)
        kg = _dup_rope(a["mla_k_norm_g"])
        return dict(gain=row(a["kv_norm_g"]), shift=row(kvmod[:D]), scale=row(kvmod[D:]),
                    w_dkv=jnp.concatenate([w_dkv, w_dkv[:, KV_LORA:]], axis=1), kv_g=row(a["mla_kv_norm_g"]),
                    w_ukv=w["mla_w_ukv"].transpose(1, 0, 2).reshape(KV_LORA, HEADS * 2 * HEAD), k_gn=row(kg[:NOPE]),
                    k_gr=row(kg[NOPE:]))

    tapes, kv, kv_p, h = [[] for _ in range(DEPTH)], None, None, None
    first = {name: _all_gather((w + zero_of(mod)).astype(BF16), "fetch_first_" + name)
             for name, w in stage_shards(0, 0).items()}
    pending = []
    for l, part in stages[1:1 + AHEAD]:
        tie = pending[-1][1]["token"][0, 0] if pending else zero_of(first["ffn_w_out"])
        pending.append(start_stage(l, part, tie))
    for n, (l, part) in enumerate(stages):
        if n == 0:
            w, ties = first, tuple(h["token"] for _, h in pending)
        else:
            names, handle = pending.pop(0)
            w = finish_stage(l, part, names, handle, x)
            ties = ()
            if n + AHEAD < len(stages):
                pending.append(start_stage(*stages[n + AHEAD], zero_of(w[names[0]])))
                ties = (pending[-1][1]["token"],)
        nxt = None
        if n + 1 < len(stages):
            l2, part2 = stages[n + 1]
            k2 = 3 * part2
            nxt = (row(norm_g[l2, part2]), row(mod[l2, k2 + 1]), row(mod[l2, k2]))
        if part != 1:
            p = ffn_params(l, part // 2, w)
            x, t, h = _ffn_fwd(x, p, ties, h, nxt)
        else:
            p = gdn_params(l, w) if l < N_A else mla_params(l, w)
            x, t, h = _gdn_fwd(x, p, ties, h, nxt) if l < N_A else _mla_fwd(x, p, kv, ties, h, nxt)
        tapes[l] += [p, t]
        if part == 2 and l == N_A - 1:
            kv_p = kv_params(w)
            kv = _kv_fwd(x, kv_p, cosf, sins)
    dx, loss_blk = _loss_and_grad(x, a["loss_target"][0], "loss")
    loss = lax.psum(loss_blk[0, 0], ("x", "y", "c"))

    def by_cols(g, n):
        return g.reshape(g.shape[0], -1, n).transpose(1, 0, 2)

    def ffn_blocks(g):
        return {"ffn_w_in": g["w_in"], "ffn_w_out": g["wo"].reshape(N_DEV, D_FF // N_DEV, D)}

    def mixer_blocks(l, g):
        if l < N_A:
            full = jnp.concatenate([g["w_main"], g["w_tail"][:, :HEADS], g["w_tail"][:, 128:128 + HEADS]], axis=1)
            return {"gdn_w_in": by_cols(full, n_gdn), "gdn_w_out": g["w_out"].reshape(N_DEV, D // N_DEV, D)}
        return {"mla_w_dq": g["w_dq"].reshape(N_DEV, D // N_DEV, Q_LORA),
                "mla_w_uq": _fold_rope(g["w_uq"].reshape(Q_LORA, HEADS, HEAD_PAD)).transpose(1, 0, 2),
                "mla_w_out": g["w_out"].reshape(N_DEV, D // N_DEV, D)}

    sent = []

    def send(key, blocks, tie=0.0):
        handle = _send_start([(b + tie).astype(BF16) for b in blocks.values()], "grad_start_" + "_".join(map(str, key)),
                             gather=False)
        sent.append((key, list(blocks), handle))
        return (handle["token"],)

    grads = [None] * DEPTH
    dk_sum = dv_sum = kv_grads = res = None
    ties = ()
    for l in reversed(range(DEPTH)):
        p1, t1, pm_, tm_, p2, t2 = tapes[l]
        if l == N_A - 1:
            dx, kv_grads, res = _kv_bwd(kv, kv_p, dk_sum, dv_sum, dx, (t2["y"], p2["gm"]))
            d_dkv = kv_grads["w_dkv"]
            ties += send((l, 3), {
                "mla_w_dkv": jnp.concatenate(
                    [d_dkv[:, :KV_LORA], d_dkv[:, KV_LORA:KV_LORA + ROPE] + d_dkv[:, KV_LORA + ROPE:]],
                    axis=1).reshape(N_DEV, D // N_DEV, KV_LORA + ROPE),
                "mla_w_ukv": by_cols(kv_grads["w_ukv"], 2 * HEAD)})
        dx, g2, res = _ffn_bwd(t2, p2, dx, res, (tm_["y"], pm_["gm"]), ties)
        ties = send((l, 2), ffn_blocks(g2))
        if l < N_A:
            dx, gm_, res = _gdn_bwd(tm_, pm_, dx, res, (t1["y"], p1["gm"]), ties)
        else:
            dx, gm_, res, dk_sum, dv_sum = _mla_bwd(tm_, pm_, kv, dx, res, (t1["y"], p1["gm"]), ties,
                                                    None if dk_sum is None else (dk_sum, dv_sum))
        ties = send((l, 1), mixer_blocks(l, gm_))
        prev = (tapes[l - 1][5]["y"], tapes[l - 1][4]["gm"]) if l > 0 and l != N_A else None
        dx, g1, res = _ffn_bwd(t1, p1, dx, res, prev, ties)
        if l > 0:
            ties = send((l, 0), ffn_blocks(g1))
        grads[l] = (g1, gm_, g2)

    out = {}
    def dmod(l):
        g1, gm_, g2 = grads[l]
        return jnp.concatenate([g1["shift"], g1["scale"], 0.5 * g1["gm"], gm_["shift"], gm_["scale"], gm_["gm"],
                                g2["shift"], g2["scale"], 0.5 * g2["gm"]], axis=1)

    gdn = [grads[l][1] for l in range(N_A)]
    mla = [grads[l][1] for l in range(N_A, DEPTH)]
    small = {
        "ada_b": jnp.concatenate([dmod(l) for l in range(DEPTH)], axis=0),
        "kv_ada_b": jnp.concatenate([kv_grads["shift"], kv_grads["scale"]], axis=1),
        "norm_g": jnp.stack([jnp.concatenate([grads[l][0]["gain"], grads[l][1]["gain"], grads[l][2]["gain"]], axis=0)
                             for l in range(DEPTH)]),
        "gdn_conv_w": jnp.stack([g["conv_w"] for g in gdn]),
        "gdn_a_log": jnp.stack([g["a_log"][0, :HEADS] for g in gdn]),
        "gdn_dt_bias": jnp.stack([g["dt_bias"][0, :HEADS] for g in gdn]),
        "gdn_norm_g": jnp.stack([g["norm_g"][0] for g in gdn]),
        "kv_norm_g": kv_grads["gain"],
        "mla_kv_norm_g": kv_grads["kv_g"],
        "mla_k_norm_g": _fold_rope(jnp.concatenate([kv_grads["k_gn"], kv_grads["k_gr"]], axis=1)),
        "mla_q_lora_norm_g": jnp.stack([g["q_lora_g"][0] for g in mla]),
        "mla_q_norm_g": jnp.stack([_fold_rope(jnp.concatenate([g["q_gn"], g["q_gr"]], axis=1))[0] for g in mla]),
    }
    rows = 616
    assert sum(n for _, n in SMALL) <= rows * 128 and all(small[n].size == k for n, k in SMALL)
    small_recv = _all_gather(_pack([small[n] for n, _ in SMALL], rows), "gather_small_grads")
    small_recv = small_recv + send((0, 0), ffn_blocks(grads[0][0]), zero_of(small_recv))[0][0, 0]
    zero = lambda n, k: jnp.zeros((k,), F32)
    packed = {pre: _pack([a[pre + n] if n in SMALL_REPLICATED else zero(n, k) for n, k in SMALL], rows)
              for pre in ("", "m_", "v_")}
    res = _adamw([small_recv], packed[""], packed["m_"], packed["v_"], "adamw_small")
    offs = {}
    o = 0
    for n, k in SMALL:
        offs[n] = o
        o += k
    for n, k in SMALL:
        if n in SMALL_REPLICATED:
            out[n] = [r.reshape(-1)[offs[n]:offs[n] + k] for r in res]
    gsum = res[0].reshape(-1)
    g_norm = lax.dynamic_slice_in_dim(gsum[offs["norm_g"]:offs["norm_g"] + DEPTH * 3 * D].reshape(DEPTH * 3, D),
                                      me * HEAD, HEAD, axis=1)
    g_conv = lax.dynamic_slice_in_dim(
        gsum[offs["gdn_conv_w"]:offs["gdn_conv_w"] + N_A * CONV_K * 3 * D].reshape(N_A * CONV_K, 3 * D),
        me * 3 * HEAD, 3 * HEAD, axis=1)
    res2 = _adamw([_pack([g_norm, g_conv], 36)[None]], *[_pack([a[pre + "norm_g"], a[pre + "gdn_conv_w"]], 36)
                                                      for pre in ("", "m_", "v_")], "adamw_small")
    out["norm_g"] = [r.reshape(-1)[:n_ng] for r in res2]
    out["gdn_conv_w"] = [r.reshape(-1)[n_ng:n_ng + n_cw] for r in res2]

    c_act_t = c_act.T
    all_small = small_recv.reshape(N_DEV, -1)
    dmod_all = all_small[:, :DEPTH * N_MOD * D].reshape(N_DEV, DEPTH, N_MOD * D)
    dmod_mine = lax.dynamic_slice_in_dim(dmod_all, me * n_ada, n_ada, axis=2)
    g_ada = [_outer8(c_act_t, dmod_mine[:, l], "ada_grad")[None] for l in range(DEPTH)]
    out["ada_w"] = _adamw(g_ada, *[a[pre + "ada_w"].reshape(DEPTH * D, n_ada) for pre in ("", "m_", "v_")], "adamw")
    dkv_all = all_small[:, offs["kv_ada_b"]:offs["kv_ada_b"] + 2 * D]
    g_kv = _outer8(c_act_t, lax.dynamic_slice_in_dim(dkv_all, me * (2 * D // N_DEV), 2 * D // N_DEV, axis=1), "ada_grad")
    out["kv_ada_w"] = _adamw([g_kv[None]], *[a[pre + "kv_ada_w"] for pre in ("", "m_", "v_")], "adamw")

    pieces = {}
    for key, names, handle in sent:
        srcs, lands = _send_wait(handle, out["kv_ada_w"][0], "grad_wait_" + "_".join(map(str, key)), gather=False)
        for name, src, land in zip(names, srcs, lands):
            own = lax.dynamic_slice_in_dim(src, me, 1, axis=0)
            pieces.setdefault(name, []).append((key, lax.dynamic_update_slice(land, own, (me, 0, 0))))
    for name, parts in pieces.items():
        wide = a[name].shape[-1]
        out[name] = _adamw([p for _, p in sorted(parts, key=lambda kp: kp[0])], a[name].reshape(-1, wide),
                           a["m_" + name].reshape(-1, wide), a["v_" + name].reshape(-1, wide), "adamw")

    result = [loss, dx[None]]
    for k in range(4):
        result += [out[n][k].reshape(a[n].shape) for n in WEIGHTS]
    return tuple(result)


def kernel(x, c, positions, ada_w, ada_b, norm_g, ffn_w_in, ffn_w_out, gdn_w_in, gdn_conv_w, gdn_a_log, gdn_dt_bias, gdn_norm_g, gdn_w_out, kv_ada_w, kv_ada_b, kv_norm_g, mla_w_dkv, mla_kv_norm_g, mla_w_ukv, mla_k_norm_g, mla_w_dq, mla_q_lora_norm_g, mla_w_uq, mla_q_norm_g, mla_w_out, loss_target, m_ada_w, m_ada_b, m_norm_g, m_ffn_w_in, m_ffn_w_out, m_gdn_w_in, m_gdn_conv_w, m_gdn_a_log, m_gdn_dt_bias, m_gdn_norm_g, m_gdn_w_out, m_kv_ada_w, m_kv_ada_b, m_kv_norm_g, m_mla_w_dkv, m_mla_kv_norm_g, m_mla_w_ukv, m_mla_k_norm_g, m_mla_w_dq, m_mla_q_lora_norm_g, m_mla_w_uq, m_mla_q_norm_g, m_mla_w_out, v_ada_w, v_ada_b, v_norm_g, v_ffn_w_in, v_ffn_w_out, v_gdn_w_in, v_gdn_conv_w, v_gdn_a_log, v_gdn_dt_bias, v_gdn_norm_g, v_gdn_w_out, v_kv_ada_w, v_kv_ada_b, v_kv_norm_g, v_mla_w_dkv, v_mla_kv_norm_g, v_mla_w_ukv, v_mla_k_norm_g, v_mla_w_dq, v_mla_q_lora_norm_g, v_mla_w_uq, v_mla_q_norm_g, v_mla_w_out):
    return _step(dict(locals()))
```

```python
import functools
import math

import jax
import jax.numpy as jnp
from jax import lax
from jax.experimental import pallas as pl
from jax.experimental.pallas import tpu as pltpu

F32 = jnp.float32
BF16 = jnp.bfloat16

N_DEV = 8
D = 1024
D_FF = 2816
DEPTH = 4
N_A = 2
N_MOD = 9
HEADS = 8
HEAD = 128
CHUNK = 64
CONV_K = 4
KV_LORA = 256
Q_LORA = 384
NOPE = 128
ROPE = 64
QK_HEAD = NOPE + ROPE
HEAD_PAD = 256
ROPE_BASE = 10000.0
EPS = 1e-6
LR, B1, B2, ADAM_EPS, WD, STEP = 0.001, 0.9, 0.999, 1e-08, 0.01, 10

VMEM_LIMIT = 48 * 1024 * 1024
ROW_TILE = 256
MESH = pl.DeviceIdType.MESH

_NN = (((1,), (0,)), ((), ()))
_NT = (((1,), (1,)), ((), ()))
_TN = (((0,), (0,)), ((), ()))
_DIMS = {"nn": _NN, "nt": _NT, "tn": _TN}


def _params(dims=None):
    return pltpu.CompilerParams(dimension_semantics=dims, vmem_limit_bytes=VMEM_LIMIT)


def _tile(n, target):
    for t in range(target - target % 128, 0, -128):
        if n % t == 0:
            return t
    return n


_TIE_SPEC1 = pl.BlockSpec((8, 128), lambda i: (0, 0))
_TIE_SPEC2 = pl.BlockSpec((8, 128), lambda i, j: (0, 0))
_TIE_SPEC3 = pl.BlockSpec((8, 128), lambda i, j, k: (0, 0))


def _matmul(pairs, form, name, out_dtype=F32, tm=1408, tn=1408, tk=1408, boffs=None, resid=None, ties=()):
    a0, b0 = pairs[0]
    if form == "nn":
        m, n = a0.shape[0], b0.shape[1]
        ks = [a.shape[1] for a, _ in pairs]
    elif form == "nt":
        m, n = a0.shape[0], b0.shape[0]
        ks = [a.shape[1] for a, _ in pairs]
    else:
        m, n = a0.shape[1], b0.shape[1]
        ks = [a.shape[0] for a, _ in pairs]
    tm, tn = _tile(m, tm), _tile(n, tn)
    tks = [_tile(k, tk) for k in ks]
    boffs = boffs or [0] * len(pairs)
    assert m % tm == 0 and n % tn == 0 and all(o % t == 0 for o, t in zip(boffs, tks)), (name, m, n, ks)
    steps = [k // t for k, t in zip(ks, tks)]
    starts = [sum(steps[:p]) for p in range(len(pairs))]
    nk = sum(steps)

    def kidx(p, k):
        return jnp.clip(k - starts[p], 0, steps[p] - 1)

    in_specs, args = [], []
    for p, (a, b) in enumerate(pairs):
        t = tks[p]
        if form == "tn":
            in_specs.append(pl.BlockSpec((t, tm), lambda i, j, k, p=p: (kidx(p, k), i)))
            in_specs.append(pl.BlockSpec((t, tn), lambda i, j, k, p=p: (kidx(p, k), j)))
        elif form == "nn":
            in_specs.append(pl.BlockSpec((tm, t), lambda i, j, k, p=p: (i, kidx(p, k))))
            in_specs.append(pl.BlockSpec((t, tn), lambda i, j, k, p=p: (kidx(p, k), j)))
        else:
            in_specs.append(pl.BlockSpec((tm, t), lambda i, j, k, p=p: (i, kidx(p, k))))
            in_specs.append(pl.BlockSpec((tn, t), lambda i, j, k, p=p, o=boffs[p] // t: (j, kidx(p, k) + o)))
        args += [a, b]
    dims = _DIMS[form]
    npairs = len(pairs)
    nres = len(resid or ())
    nin = 2 * npairs + len(ties) + nres
    out_blk = pl.BlockSpec((tm, tn), lambda i, j, k: (i, j))
    in_specs += [_TIE_SPEC3] * len(ties)
    args += list(ties)
    if resid:
        assert nres == 2 or (nres == 5 and tn == n)
        in_specs += [out_blk] + [pl.BlockSpec((1, tn), lambda i, j, k: (0, j))] * (nres - 1)
        args += list(resid)

    def body(*refs):
        o_ref = refs[nin]
        k = pl.program_id(2)

        def prod(p):
            return lax.dot_general(refs[2 * p][...].astype(BF16), refs[2 * p + 1][...].astype(BF16), dims,
                                   preferred_element_type=F32)

        def finish(y):
            o_ref[...] = y.astype(o_ref.dtype)
            if resid:
                x_ref, gate_ref = refs[nin - nres], refs[nin - nres + 1]
                xn = x_ref[...] + gate_ref[...] * y
                refs[nin + 1][...] = xn
                if nres == 5:
                    gain, scale, shift = (r[...] for r in refs[nin - 3:nin])
                    refs[nin + 2][...] = _modulate_fn(0, xn, gain, scale, shift)[0].astype(BF16)

        if nk == 1:
            finish(prod(0))
            return
        acc = refs[-1]

        @pl.when(k == 0)
        def _():
            acc[...] = jnp.zeros_like(acc)

        for p in range(npairs):
            @pl.when((k >= starts[p]) & (k < starts[p] + steps[p]))
            def _(p=p):
                acc[...] += prod(p)

        @pl.when(k == nk - 1)
        def _():
            finish(acc[...])

    res = pl.pallas_call(
        body, name=name, grid=(m // tm, n // tn, nk), in_specs=in_specs,
        out_specs=[out_blk] * (2 + (nres == 5)) if resid else out_blk,
        out_shape=([jax.ShapeDtypeStruct((m, n), out_dtype), jax.ShapeDtypeStruct((m, n), F32)]
                   + [jax.ShapeDtypeStruct((m, n), BF16)] * (nres == 5))
        if resid else jax.ShapeDtypeStruct((m, n), out_dtype),
        scratch_shapes=[] if nk == 1 else [pltpu.VMEM((tm, tn), F32)],
        compiler_params=_params(("parallel", "parallel", "arbitrary")),
    )(*args)
    return res


def _mm(a, b, form, name, **kw):
    return _matmul([(a, b)], form, name, **kw)


def _cols(spec, g):
    return spec[g] if isinstance(spec, list) else spec


def _rowwise_fwd(fn, rows, pars, outs, name, groups=1, ts=ROW_TILE, ties=()):
    s = rows[0][0].shape[0]
    ts = min(ts, s)
    assert s % ts == 0
    nr, npar = len(rows), len(pars)

    def body(*refs):
        par_t = [r[...] for r in refs[nr:nr + npar]]
        out_refs = refs[nr + npar + len(ties):]
        for g in range(groups):
            row_t = []
            for r, (_, spec) in zip(refs[:nr], rows):
                c0, w = _cols(spec, g)
                row_t.append(r[:, c0:c0 + w].astype(F32))
            res = fn(g, *row_t, *par_t)
            for o_ref, val, (_, _, spec) in zip(out_refs, res, outs):
                c0, w = _cols(spec, g)
                o_ref[:, c0:c0 + w] = val.astype(o_ref.dtype)

    return pl.pallas_call(
        body, name=name, grid=(s // ts,),
        in_specs=[pl.BlockSpec((ts, a.shape[1]), lambda i: (i, 0)) for a, _ in rows]
        + [pl.BlockSpec(p.shape, lambda i: (0, 0)) for p in pars] + [_TIE_SPEC1] * len(ties),
        out_specs=[pl.BlockSpec((ts, w), lambda i: (i, 0)) for w, _, _ in outs],
        out_shape=[jax.ShapeDtypeStruct((s, w), dt) for w, dt, _ in outs],
        compiler_params=_params(("parallel",)),
    )(*[a for a, _ in rows], *pars, *ties)


def _rowwise_bwd(fn, rows, pars, outs, douts, gmap, gshapes, name, groups=1, add=None, par_grads=True,
                 ts=ROW_TILE):
    s = rows[0][0].shape[0]
    ts = min(ts, s)
    assert s % ts == 0
    nr, npar, nout, ng = len(rows), len(pars), len(outs), len(gshapes)
    add = add or {}
    add_keys = sorted(add)

    def body(*refs):
        row_refs = refs[:nr]
        par_refs = refs[nr:nr + npar]
        dout_refs = refs[nr + npar:nr + npar + nout]
        add_refs = refs[nr + npar + nout:nr + npar + nout + len(add_keys)]
        g_refs = refs[nr + npar + nout + len(add_keys):][:ng]
        pg_refs = refs[nr + npar + nout + len(add_keys) + ng:]
        par_t = [r[...] for r in par_refs]
        par_acc = [None] * npar
        shared_acc = {}
        for g in range(groups):
            row_t = []
            for r, (_, spec) in zip(row_refs, rows):
                c0, w = _cols(spec, g)
                row_t.append(r[:, c0:c0 + w].astype(F32))
            cts = []
            for r, (_, _, spec) in zip(dout_refs, outs):
                c0, w = _cols(spec, g)
                cts.append(r[:, c0:c0 + w].astype(F32))
            _, vjp = jax.vjp(lambda *t, g=g: tuple(fn(g, *t)), *row_t, *par_t)
            grads = vjp(tuple(cts))
            for k in range(nr):
                if gmap[k] is None:
                    continue
                gi, spec = gmap[k]
                if isinstance(spec, list) or groups == 1:
                    c0, w = _cols(spec, g)
                    val = grads[k]
                    if gi in add:
                        val = val + add_refs[add_keys.index(gi)][:, c0:c0 + w].astype(F32)
                    g_refs[gi][:, c0:c0 + w] = val.astype(g_refs[gi].dtype)
                else:
                    shared_acc[k] = grads[k] if k not in shared_acc else shared_acc[k] + grads[k]
            if par_grads:
                for k in range(npar):
                    pg = grads[nr + k]
                    par_acc[k] = pg if par_acc[k] is None else par_acc[k] + pg
        for k, val in shared_acc.items():
            gi, (c0, w) = gmap[k]
            assert gi not in add
            g_refs[gi][:, c0:c0 + w] = val.astype(g_refs[gi].dtype)
        if par_grads:
            first = pl.program_id(0) == 0
            for k in range(npar):
                @pl.when(first)
                def _(k=k):
                    pg_refs[k][...] = par_acc[k]

                @pl.when(jnp.logical_not(first))
                def _(k=k):
                    pg_refs[k][...] += par_acc[k]

    out_specs = [pl.BlockSpec((ts, w), lambda i: (i, 0)) for w, _ in gshapes]
    out_shape = [jax.ShapeDtypeStruct((s, w), dt) for w, dt in gshapes]
    if par_grads:
        out_specs += [pl.BlockSpec(p.shape, lambda i: (0, 0)) for p in pars]
        out_shape += [jax.ShapeDtypeStruct(p.shape, F32) for p in pars]
    return pl.pallas_call(
        body, name=name, grid=(s // ts,),
        in_specs=[pl.BlockSpec((ts, a.shape[1]), lambda i: (i, 0)) for a, _ in rows]
        + [pl.BlockSpec(p.shape, lambda i: (0, 0)) for p in pars]
        + [pl.BlockSpec((ts, a.shape[1]), lambda i: (i, 0)) for a in douts]
        + [pl.BlockSpec((ts, add[k].shape[1]), lambda i: (i, 0)) for k in add_keys],
        out_specs=out_specs, out_shape=out_shape,
        compiler_params=_params(("arbitrary",)),
    )(*[a for a, _ in rows], *pars, *douts, *[add[k] for k in add_keys])


def _sigmoid(x):
    return 1.0 / (1.0 + jnp.exp(-x))


def _silu(x):
    return x * _sigmoid(x)


def _softplus(x):
    return jnp.maximum(x, 0.0) + jnp.log(1.0 + jnp.exp(-jnp.abs(x)))


def _rms(t, g, n=None):
    n = n or t.shape[-1]
    return t * lax.rsqrt(jnp.sum(t * t, axis=-1, keepdims=True) / n + EPS) * g


def _modulate_fn(g, x, gain, scale, shift):
    return (_rms(x, gain) * (1.0 + scale) + shift,)


def _resgate_fn(g, x, y, gm):
    return (x + gm * y,)


def _gate_only_fn(g, y, gm):
    return (gm * y,)


def _gdn_gates_fn(g, b_logit, a_logit, a_log, dt_bias):
    gate = -jnp.exp(a_log) * _softplus(a_logit + dt_bias)
    n = gate.shape[0]
    i = lax.broadcasted_iota(jnp.int32, (n, n), 0)
    j = lax.broadcasted_iota(jnp.int32, (n, n), 1)
    tri = (((i // CHUNK) == (j // CHUNK)) & (i >= j)).astype(F32)
    gcum = lax.dot_general(tri, gate, _NN, preferred_element_type=F32, precision=lax.Precision.HIGHEST)
    return _sigmoid(b_logit), gcum


def _gdn_outnorm_fn(g, o, z, gain):
    return (_rms(o, gain) * _silu(z),)


def _rms_fn(g, t, gain):
    return (_rms(t, gain),)


@jax.custom_vjp
def _swap_halves(t):
    return pltpu.roll(t, 32, 1)


_swap_halves.defvjp(lambda t: (pltpu.roll(t, 32, 1), None), lambda _, ct: (pltpu.roll(ct, 96, 1),))


def _head_norm_rope_fn(g, nope, rope, cosf, sins, gain_n, gain_r):
    first = lax.broadcasted_iota(jnp.int32, rope.shape, 1) < ROPE
    ss = jnp.sum(nope * nope, axis=-1, keepdims=True) + jnp.sum(jnp.where(first, rope * rope, 0.0), axis=-1,
                                                                 keepdims=True)
    r = lax.rsqrt(ss / QK_HEAD + EPS)
    tn = nope * r * gain_n
    tr = rope * r * gain_r
    rot = jnp.where(first, tr * cosf + _swap_halves(tr) * sins, 0.0)
    return tn, rot


def _q_norm_rope_fn(g, nope, rope, cosf, sins, gain_n, gain_r):
    tn, rot = _head_norm_rope_fn(g, nope, rope, cosf, sins, gain_n, gain_r)
    return (jnp.concatenate([tn, rot], axis=1),)


def _k_norm_rope_fn(g, nope, val, rope, cosf, sins, gain_n, gain_r):
    tn, rot = _head_norm_rope_fn(g, nope, rope, cosf, sins, gain_n, gain_r)
    return jnp.concatenate([tn, rot], axis=1), val


def _loss_fn(g, y, target):
    e = y - target
    return (jnp.sum(e * e, axis=-1, keepdims=True) * (0.5 / D) * jnp.ones((1, 128), F32),)


FF_SH = 2 * D_FF // N_DEV
FF_G = N_DEV // 2


def _ffn_in(h, w_in, name, tm=1024, ties=()):
    s = h.shape[0]
    tm = min(tm, s)

    def body(h_ref, wg_ref, wu_ref, *rest):
        g_ref, u_ref, a_ref = rest[-3:]
        hb = h_ref[...]
        gate = jnp.dot(hb, wg_ref[...], preferred_element_type=F32)
        up = jnp.dot(hb, wu_ref[...], preferred_element_type=F32)
        g_ref[...] = gate.astype(BF16)
        u_ref[...] = up.astype(BF16)
        a_ref[...] = (_silu(gate) * up).astype(BF16)

    spec = pl.BlockSpec((None, tm, FF_SH), lambda j, i: (j, i, 0))
    return pl.pallas_call(
        body, name=name, grid=(FF_G, s // tm),
        in_specs=[pl.BlockSpec((tm, D), lambda j, i: (i, 0)), pl.BlockSpec((None, D, FF_SH), lambda j, i: (j, 0, 0)),
                  pl.BlockSpec((None, D, FF_SH), lambda j, i: (j + FF_G, 0, 0))] + [_TIE_SPEC2] * len(ties),
        out_specs=[spec, spec, spec], out_shape=[jax.ShapeDtypeStruct((FF_G, s, FF_SH), BF16)] * 3,
        compiler_params=_params(("parallel", "parallel")),
    )(h, w_in, w_in, *ties)


def _ffn_out(act, wo, resid, name, tm=512):
    s = act.shape[1]
    tm = min(tm, s)
    nres = len(resid)

    def body(a_ref, b_ref, x_ref, gate_ref, *rest):
        mods, outs = rest[:nres - 2], rest[nres - 2:]
        y = jnp.dot(a_ref[0], b_ref[0:FF_SH, :], preferred_element_type=F32)
        for k in range(1, FF_G):
            y = y + jnp.dot(a_ref[k], b_ref[k * FF_SH:(k + 1) * FF_SH, :], preferred_element_type=F32)
        xn = x_ref[...] + gate_ref[...] * y
        outs[0][...] = y.astype(BF16)
        outs[1][...] = xn
        if mods:
            outs[2][...] = _modulate_fn(0, xn, *[m[...] for m in mods])[0].astype(BF16)

    blk = pl.BlockSpec((tm, D), lambda i: (i, 0))
    vec = pl.BlockSpec((1, D), lambda i: (0, 0))
    return pl.pallas_call(
        body, name=name, grid=(s // tm,),
        in_specs=[pl.BlockSpec((FF_G, tm, FF_SH), lambda i: (0, i, 0)), pl.BlockSpec((D_FF, D), lambda i: (0, 0)),
                  blk] + [vec] * (nres - 1),
        out_specs=[blk] * (2 + (nres == 5)),
        out_shape=[jax.ShapeDtypeStruct((s, D), BF16), jax.ShapeDtypeStruct((s, D), F32)]
        + [jax.ShapeDtypeStruct((s, D), BF16)] * (nres == 5),
        compiler_params=_params(("parallel",)),
    )(act, wo, *resid)


def _ffn_bwd_act(dy, wo, gate, up, name, tm=1024, ties=()):
    s = dy.shape[0]
    tm = min(tm, s)

    def body(dy_ref, wo_ref, g_ref, u_ref, *rest):
        dg_ref, du_ref = rest[-2:]
        dact = lax.dot_general(dy_ref[...], wo_ref[...], _NT, preferred_element_type=F32)
        gate = g_ref[...].astype(F32)
        up = u_ref[...].astype(F32)
        sg = _sigmoid(gate)
        dg_ref[...] = (dact * up * (sg * (1.0 + gate * (1.0 - sg)))).astype(BF16)
        du_ref[...] = (dact * (gate * sg)).astype(BF16)

    spec = pl.BlockSpec((None, tm, FF_SH), lambda j, i: (j, i, 0))
    return pl.pallas_call(
        body, name=name, grid=(FF_G, s // tm),
        in_specs=[pl.BlockSpec((tm, D), lambda j, i: (i, 0)), pl.BlockSpec((FF_SH, D), lambda j, i: (j, 0)), spec, spec]
        + [_TIE_SPEC2] * len(ties),
        out_specs=[spec, spec], out_shape=[jax.ShapeDtypeStruct((FF_G, s, FF_SH), BF16)] * 2,
        compiler_params=_params(("parallel", "parallel")),
    )(dy, wo, gate, up, *ties)


def _ffn_dwo(act, dy, name, tk=2048, ties=()):
    s = act.shape[1]
    tk = min(tk, s)

    def body(a_ref, b_ref, *rest):
        o_ref, acc = rest[-2:]
        k = pl.program_id(1)

        @pl.when(k == 0)
        def _():
            acc[...] = jnp.zeros_like(acc)

        acc[...] += lax.dot_general(a_ref[...], b_ref[...], _TN, preferred_element_type=F32)

        @pl.when(k == s // tk - 1)
        def _():
            o_ref[...] = acc[...].astype(BF16)

    return pl.pallas_call(
        body, name=name, grid=(FF_G, s // tk),
        in_specs=[pl.BlockSpec((None, tk, FF_SH), lambda j, k: (j, k, 0)), pl.BlockSpec((tk, D), lambda j, k: (k, 0))]
        + [_TIE_SPEC2] * len(ties),
        out_specs=pl.BlockSpec((FF_SH, D), lambda j, k: (j, 0)), out_shape=jax.ShapeDtypeStruct((D_FF, D), BF16),
        scratch_shapes=[pltpu.VMEM((FF_SH, D), F32)], compiler_params=_params(("parallel", "arbitrary")),
    )(act, dy, *ties)


def _ffn_halves(k, gate_ref, up_ref, fn):
    pl.when(k < FF_G)(functools.partial(fn, gate_ref))
    pl.when(k >= FF_G)(functools.partial(fn, up_ref))


def _ffn_dh(dgate, dup, w_in, name, tm=512):
    s = dgate.shape[1]
    tm = min(tm, s)

    def body(dg_ref, du_ref, w_ref, o_ref):
        acc = lax.dot_general(dg_ref[0], w_ref[0], _NT, preferred_element_type=F32)
        for k in range(1, N_DEV):
            d_ref = dg_ref if k < FF_G else du_ref
            acc = acc + lax.dot_general(d_ref[k % FF_G], w_ref[k], _NT, preferred_element_type=F32)
        o_ref[...] = acc

    half = pl.BlockSpec((FF_G, tm, FF_SH), lambda i: (0, i, 0))
    return pl.pallas_call(
        body, name=name, grid=(s // tm,),
        in_specs=[half, half, pl.BlockSpec((N_DEV, D, FF_SH), lambda i: (0, 0, 0))],
        out_specs=pl.BlockSpec((tm, D), lambda i: (i, 0)), out_shape=jax.ShapeDtypeStruct((s, D), F32),
        compiler_params=_params(("parallel",)),
    )(dgate, dup, w_in)


def _ffn_dwi(h, dgate, dup, name, tk=2048):
    s = h.shape[0]
    tk = min(tk, s)

    def body(h_ref, dg_ref, du_ref, o_ref, acc):
        j, k = pl.program_id(0), pl.program_id(1)

        @pl.when(k == 0)
        def _():
            acc[...] = jnp.zeros_like(acc)

        def add(d_ref):
            acc[...] += lax.dot_general(h_ref[...], d_ref[...], _TN, preferred_element_type=F32)

        _ffn_halves(j, dg_ref, du_ref, add)

        @pl.when(k == s // tk - 1)
        def _():
            o_ref[...] = acc[...].astype(BF16)

    return pl.pallas_call(
        body, name=name, grid=(N_DEV, s // tk),
        in_specs=[pl.BlockSpec((tk, D), lambda j, k: (k, 0)),
                  pl.BlockSpec((None, tk, FF_SH), lambda j, k: (jnp.minimum(j, FF_G - 1), jnp.where(j < FF_G, k, s // tk - 1), 0)),
                  pl.BlockSpec((None, tk, FF_SH), lambda j, k: (jnp.maximum(j - FF_G, 0), jnp.where(j < FF_G, 0, k), 0))],
        out_specs=pl.BlockSpec((None, D, FF_SH), lambda j, k: (j, 0, 0)),
        out_shape=jax.ShapeDtypeStruct((N_DEV, D, FF_SH), BF16),
        scratch_shapes=[pltpu.VMEM((D, FF_SH), F32)], compiler_params=_params(("parallel", "arbitrary")),
    )(h, dgate, dup)


def _shift_down(x, d):
    rows = lax.broadcasted_iota(jnp.int32, x.shape, 0)
    return jnp.where(rows >= d, pltpu.roll(x, d, 0), 0.0)


def _shift_up(x, d):
    n = x.shape[0]
    rows = lax.broadcasted_iota(jnp.int32, x.shape, 0)
    return jnp.where(rows < n - d, pltpu.roll(x, n - d, 0), 0.0)


def _conv_post(pre, is_qk):
    a = _silu(pre)
    l2 = a * lax.rsqrt(jnp.sum(a * a, axis=-1, keepdims=True) + EPS)
    return jnp.where(is_qk, l2, a)


def _conv_pre(x, w):
    pre = x * w[CONV_K - 1:CONV_K, :]
    for j in range(CONV_K - 1):
        pre = pre + _shift_down(x, CONV_K - 1 - j) * w[j:j + 1, :]
    return pre


def _gdn_conv_fwd(pm, conv_w, name):
    s = pm.shape[0]
    nblk = 3 * D // HEAD

    def body(x_ref, w_ref, o_ref):
        is_qk = pl.program_id(0) < 2 * HEADS
        o_ref[...] = _conv_post(_conv_pre(x_ref[...], w_ref[...]), is_qk)

    return pl.pallas_call(
        body, name=name, grid=(nblk,),
        in_specs=[pl.BlockSpec((s, HEAD), lambda c: (0, c)), pl.BlockSpec((CONV_K, HEAD), lambda c: (0, c))],
        out_specs=pl.BlockSpec((s, HEAD), lambda c: (0, c)),
        out_shape=jax.ShapeDtypeStruct((s, 3 * D), F32), compiler_params=_params(("parallel",)),
    )(pm, conv_w)


def _gdn_conv_bwd(pm, conv_w, dout, part, name):
    s = pm.shape[0]
    off = part * HEADS

    def body(x_ref, w_ref, d_ref, dx_ref, dw_ref):
        x, w = x_ref[...], w_ref[...]
        _, vjp = jax.vjp(lambda p: _conv_post(p, part < 2), _conv_pre(x, w))
        dpre, = vjp(d_ref[...])
        dx = dpre * w[CONV_K - 1:CONV_K, :]
        rows = [None] * CONV_K
        rows[CONV_K - 1] = jnp.sum(dpre * x, axis=0, keepdims=True)
        for j in range(CONV_K - 1):
            dx = dx + _shift_up(dpre, CONV_K - 1 - j) * w[j:j + 1, :]
            rows[j] = jnp.sum(dpre * _shift_down(x, CONV_K - 1 - j), axis=0, keepdims=True)
        dx_ref[...] = dx.astype(BF16)
        dw_ref[...] = jnp.concatenate(rows, axis=0)

    return pl.pallas_call(
        body, name=name, grid=(HEADS,),
        in_specs=[pl.BlockSpec((s, HEAD), lambda c: (0, c + off)), pl.BlockSpec((CONV_K, HEAD), lambda c: (0, c + off)),
                  pl.BlockSpec((s, HEAD), lambda c: (0, c))],
        out_specs=[pl.BlockSpec((s, HEAD), lambda c: (0, c)), pl.BlockSpec((CONV_K, HEAD), lambda c: (0, c))],
        out_shape=[jax.ShapeDtypeStruct((s, D), BF16), jax.ShapeDtypeStruct((CONV_K, D), F32)],
        compiler_params=_params(("parallel",)),
    )(pm, conv_w, dout)


def _dot3(a, b, dims=_NN):
    ah, bh = a.astype(BF16), b.astype(BF16)
    al, bl = (a - ah.astype(F32)).astype(BF16), (b - bh.astype(F32)).astype(BF16)
    d = lambda u, v: lax.dot_general(u, v, dims, preferred_element_type=F32)
    return d(ah, bh) + (d(ah, bl) + d(al, bh))


def _make_dot(hi):
    def raw(a, b, dims):
        if hi:
            return _dot3(a, b, dims)
        return lax.dot_general(a.astype(BF16), b.astype(BF16), dims, preferred_element_type=F32)

    @functools.partial(jax.custom_vjp, nondiff_argnums=(2,))
    def dot(a, b, form):
        return raw(a, b, _DIMS[form])

    def fwd(a, b, form):
        return raw(a, b, _DIMS[form]), (a, b)

    def bwd(form, res, ct):
        a, b = res
        if form == "nn":
            return raw(ct, b, _NT), raw(a, ct, _TN)
        if form == "nt":
            return raw(ct, b, _NN), raw(ct, a, _TN)
        return raw(b, ct, _NT), raw(a, ct, _NN)

    dot.defvjp(fwd, bwd)
    return dot


_dot = _make_dot(False)
_dot_hi = _make_dot(True)


def _tri_inv_raw(low):
    n = low.shape[0]
    i = lax.broadcasted_iota(jnp.int32, (n, n), 0)
    j = lax.broadcasted_iota(jnp.int32, (n, n), 1)
    eye = (i == j).astype(F32)
    hdot = _dot3
    same16 = (i // 16) == (j // 16)
    neg = jnp.where(same16, -low, 0.0)
    inv = eye + neg
    power = neg
    for _ in range(3):
        power = hdot(power, power)
        inv = hdot(inv, eye + power)
    for blk in (32, 64):
        off = jnp.where(((i // blk) == (j // blk)) & ((i // (blk // 2)) != (j // (blk // 2))), low, 0.0)
        inv = inv - hdot(inv, hdot(off, inv))
    return inv


@jax.custom_vjp
def _tri_inv(low):
    return _tri_inv_raw(low)


def _tri_inv_fwd(low):
    inv = _tri_inv_raw(low)
    return inv, inv


def _tri_inv_bwd(inv, ct):
    return (-_dot3(_dot3(inv, ct, _TN), inv, _NT),)


_tri_inv.defvjp(_tri_inv_fwd, _tri_inv_bwd)


@jax.custom_vjp
def _tri_inv_given(low, inv):
    return inv


_tri_inv_given.defvjp(lambda low, inv: (inv, inv),
                      lambda inv, ct: (_tri_inv_bwd(inv, ct)[0], jnp.zeros_like(inv)))

GROUP = 4
N_GROUPS = HEADS // GROUP
GROWS = GROUP * CHUNK


def _gdn_group(q, k, v, beta, gc, gr, states, inv=None):
    n = q.shape[0]
    i = lax.broadcasted_iota(jnp.int32, (n, n), 0)
    j = lax.broadcasted_iota(jnp.int32, (n, n), 1)
    same = (i // CHUNK) == (j // CHUNK)
    incl, strict = same & (i >= j), same & (i > j)
    qs = q * (HEAD ** -0.5)
    decay = jnp.where(incl, jnp.exp(jnp.where(incl, gc - gr, 0.0)), 0.0)
    kb = k * beta
    eg = jnp.exp(gc)
    prod = _dot(jnp.concatenate([kb, qs], axis=0), k, "nt")
    low = jnp.where(strict, prod[:n] * decay, 0.0)
    attn = jnp.where(incl, prod[n:] * decay, 0.0)
    inv = _tri_inv(low) if inv is None else _tri_inv_given(low, inv)
    sol = _dot_hi(inv, jnp.concatenate([v * beta, kb * eg], axis=1), "nn")
    u, w, qg = sol[:, :HEAD], sol[:, HEAD:], qs * eg
    last = lax.broadcasted_iota(jnp.int32, (CHUNK, 1), 0) == CHUNK - 1
    v_new, o_state, carry = [], [], []
    for h, state in enumerate(states):
        rows = slice(h * CHUNK, (h + 1) * CHUNK)
        ws = _dot(jnp.concatenate([w[rows], qg[rows]], axis=0), state, "nn")
        v_new.append(u[rows] - ws[:CHUNK])
        o_state.append(ws[CHUNK:])
        g_last = jnp.sum(jnp.where(last, gc[rows], 0.0), axis=0, keepdims=True)
        carry.append((g_last, k[rows] * jnp.exp(g_last - gc[rows])))
    o = jnp.concatenate(o_state, axis=0) + _dot(attn, jnp.concatenate(v_new, axis=0), "nn")
    new = tuple(state * jnp.exp(g_last) + _dot(k_dec, vn, "tn")
                for state, (g_last, k_dec), vn in zip(states, carry, v_new))
    return o, new, inv


def _gdn_specs(s, rev):
    nc = s // CHUNK
    at = (lambda n: nc - 1 - n) if rev else (lambda n: n)
    return nc, at, [
        pl.BlockSpec((CHUNK, D), lambda n: (at(n), 0)), pl.BlockSpec((CHUNK, D), lambda n: (at(n), 1)),
        pl.BlockSpec((CHUNK, D), lambda n: (at(n), 2)), pl.BlockSpec((CHUNK, HEAD), lambda n: (at(n), 0)),
        pl.BlockSpec((CHUNK, HEAD), lambda n: (at(n), 0)),
        pl.BlockSpec((None, N_GROUPS, 1, GROWS), lambda n: (at(n), 0, 0, 0))]


def _group_operands(grp, q_ref, k_ref, v_ref, b_blk, gc_blk, gr_blk):
    heads = range(grp * GROUP, (grp + 1) * GROUP)
    stack = lambda ref: jnp.concatenate([ref[:, h * HEAD:(h + 1) * HEAD] for h in heads], axis=0)
    col = lambda blk: jnp.concatenate([blk[:, h:h + 1] for h in heads], axis=0)
    return stack(q_ref), stack(k_ref), stack(v_ref), col(b_blk), col(gc_blk), gr_blk[grp]


def _gdn_scan_fwd(qkv, beta, gcum, grow, name):
    s = qkv.shape[0]
    nc, _, in_specs = _gdn_specs(s, rev=False)

    def body(q_ref, k_ref, v_ref, b_ref, gc_ref, gr_ref, o_ref, st_ref, inv_ref, state):
        @pl.when(pl.program_id(0) == 0)
        def _():
            state[...] = jnp.zeros_like(state)

        b_blk, gc_blk, gr_blk = b_ref[...], gc_ref[...], gr_ref[...]
        old = [state[h] for h in range(HEADS)]
        res = [_gdn_group(*_group_operands(grp, q_ref, k_ref, v_ref, b_blk, gc_blk, gr_blk),
                          old[grp * GROUP:(grp + 1) * GROUP]) for grp in range(N_GROUPS)]
        for grp, (o, new, inv) in enumerate(res):
            inv_ref[grp] = inv
            for hh in range(GROUP):
                h = grp * GROUP + hh
                st_ref[h] = old[h]
                o_ref[:, h * HEAD:(h + 1) * HEAD] = o[hh * CHUNK:(hh + 1) * CHUNK]
                state[h] = new[hh]

    return pl.pallas_call(
        body, name=name, grid=(nc,), in_specs=in_specs,
        out_specs=[pl.BlockSpec((CHUNK, D), lambda n: (n, 0)),
                   pl.BlockSpec((None, HEADS, HEAD, HEAD), lambda n: (n, 0, 0, 0)),
                   pl.BlockSpec((None, N_GROUPS, GROWS, GROWS), lambda n: (n, 0, 0, 0))],
        out_shape=[jax.ShapeDtypeStruct((s, D), F32), jax.ShapeDtypeStruct((nc, HEADS, HEAD, HEAD), F32),
                   jax.ShapeDtypeStruct((nc, N_GROUPS, GROWS, GROWS), F32)],
        scratch_shapes=[pltpu.VMEM((HEADS, HEAD, HEAD), F32)],
        compiler_params=_params(("arbitrary",)),
    )(qkv, qkv, qkv, beta, gcum, grow)


def _gdn_scan_bwd(qkv, beta, gcum, grow, states, invs, do, name):
    s = qkv.shape[0]
    nc, at, in_specs = _gdn_specs(s, rev=True)
    in_specs += [pl.BlockSpec((None, HEADS, HEAD, HEAD), lambda n: (at(n), 0, 0, 0)),
                 pl.BlockSpec((None, N_GROUPS, GROWS, GROWS), lambda n: (at(n), 0, 0, 0)),
                 pl.BlockSpec((CHUNK, D), lambda n: (at(n), 0))]

    def body(q_ref, k_ref, v_ref, b_ref, gc_ref, gr_ref, st_ref, inv_ref, do_ref, dq_ref, dk_ref, dv_ref, db_ref,
             dgc_ref, dgr_ref, dstate):
        @pl.when(pl.program_id(0) == 0)
        def _():
            dstate[...] = jnp.zeros_like(dstate)

        b_blk, gc_blk, gr_blk = b_ref[...], gc_ref[...], gr_ref[...]
        dold = [dstate[h] for h in range(HEADS)]
        res = []
        for grp in range(N_GROUPS):
            heads = range(grp * GROUP, (grp + 1) * GROUP)
            inv = inv_ref[grp]
            _, vjp = jax.vjp(lambda q, k, v, b, gc, gr, *st, inv=inv: _gdn_group(q, k, v, b, gc, gr, st, inv)[:2],
                             *_group_operands(grp, q_ref, k_ref, v_ref, b_blk, gc_blk, gr_blk),
                             *[st_ref[h] for h in heads])
            d_out = jnp.concatenate([do_ref[:, h * HEAD:(h + 1) * HEAD] for h in heads], axis=0)
            res.append(vjp((d_out, tuple(dold[h] for h in heads))))
        lane = lax.broadcasted_iota(jnp.int32, (CHUNK, HEAD), 1)
        db_all = jnp.zeros((CHUNK, HEAD), F32)
        dgc_all = jnp.zeros((CHUNK, HEAD), F32)
        for grp, (dq, dk, dv, db, dgc, dgr, *dst) in enumerate(res):
            dgr_ref[grp] = dgr
            for hh in range(GROUP):
                h = grp * GROUP + hh
                cs, rows = slice(h * HEAD, (h + 1) * HEAD), slice(hh * CHUNK, (hh + 1) * CHUNK)
                dq_ref[:, cs] = dq[rows]
                dk_ref[:, cs] = dk[rows]
                dv_ref[:, cs] = dv[rows]
                dstate[h] = dst[hh]
                db_all = jnp.where(lane == h, db[rows], db_all)
                dgc_all = jnp.where(lane == h, dgc[rows], dgc_all)
        db_ref[...] = db_all
        dgc_ref[...] = dgc_all

    blk = pl.BlockSpec((CHUNK, D), lambda n: (at(n), 0))
    gblk = pl.BlockSpec((CHUNK, HEAD), lambda n: (at(n), 0))
    return pl.pallas_call(
        body, name=name, grid=(nc,), in_specs=in_specs,
        out_specs=[blk, blk, blk, gblk, gblk, pl.BlockSpec((None, N_GROUPS, 1, GROWS), lambda n: (at(n), 0, 0, 0))],
        out_shape=[jax.ShapeDtypeStruct((s, D), F32)] * 3 + [jax.ShapeDtypeStruct((s, HEAD), F32)] * 2
        + [jax.ShapeDtypeStruct((nc, N_GROUPS, 1, GROWS), F32)],
        scratch_shapes=[pltpu.VMEM((HEADS, HEAD, HEAD), F32)],
        compiler_params=_params(("arbitrary",)),
    )(qkv, qkv, qkv, beta, gcum, grow, states, invs, do)


ATT_TILE = 512
ATT_SCALE = QK_HEAD ** -0.5


def _att_mask(t):
    qpos = lax.broadcasted_iota(jnp.int32, (t, t), 0)
    kpos = lax.broadcasted_iota(jnp.int32, (t, t), 1)
    return (kpos // CHUNK) <= (qpos // CHUNK)


ATT_STRIP = 32


def _att_strip_mask(r, t):
    kpos = lax.broadcasted_iota(jnp.int32, (ATT_STRIP, t), 1)
    return (kpos // CHUNK) <= (r * ATT_STRIP) // CHUNK


def _att_pairs(nb, by_query):
    if by_query:
        pairs = [(i, j) for i in range(nb) for j in range(i + 1)]
    else:
        pairs = [(j, i) for j in range(nb) for i in range(j, nb)]
    return jnp.array([a for a, _ in pairs], jnp.int32), jnp.array([b for _, b in pairs], jnp.int32)


def _attn_fwd(q, k, v, name):
    s = q.shape[0]
    t = min(ATT_TILE, s)
    nb = s // t
    ii, jj = _att_pairs(nb, by_query=True)

    def body(ii_ref, jj_ref, q_ref, k_ref, v_ref, o_ref, lse_ref, m_s, l_s, acc):
        step = pl.program_id(1)
        i, j = ii_ref[step], jj_ref[step]

        @pl.when(j == 0)
        def _():
            m_s[...] = jnp.full_like(m_s, -jnp.inf)
            l_s[...] = jnp.zeros_like(l_s)
            acc[...] = jnp.zeros_like(acc)

        sc = lax.dot_general(q_ref[...], k_ref[...], _NT, preferred_element_type=F32) * ATT_SCALE
        sc = lax.cond(i == j, lambda u: jnp.where(_att_mask(t), u, -jnp.inf), lambda u: u, sc)
        m_new = jnp.maximum(m_s[...], jnp.max(sc, axis=-1, keepdims=True))
        alpha = jnp.exp(m_s[...] - m_new)
        p = jnp.exp(sc - m_new)
        l_s[...] = alpha * l_s[...] + jnp.sum(p, axis=-1, keepdims=True)
        acc[...] = alpha * acc[...] + jnp.dot(p.astype(BF16), v_ref[...], preferred_element_type=F32)
        m_s[...] = m_new

        @pl.when(j == i)
        def _():
            o_ref[...] = acc[...] / l_s[...]
            lse_ref[...] = m_s[...] + jnp.log(l_s[...])

    grid_spec = pltpu.PrefetchScalarGridSpec(
        num_scalar_prefetch=2, grid=(HEADS, len(ii)),
        in_specs=[pl.BlockSpec((t, HEAD_PAD), lambda h, n, ir, jr: (ir[n], h)),
                  pl.BlockSpec((t, HEAD_PAD), lambda h, n, ir, jr: (jr[n], h)),
                  pl.BlockSpec((t, HEAD), lambda h, n, ir, jr: (jr[n], h))],
        out_specs=[pl.BlockSpec((t, HEAD), lambda h, n, ir, jr: (ir[n], h)),
                   pl.BlockSpec((None, t, 1), lambda h, n, ir, jr: (h, ir[n], 0))],
        scratch_shapes=[pltpu.VMEM((t, 1), F32), pltpu.VMEM((t, 1), F32), pltpu.VMEM((t, HEAD), F32)])
    return pl.pallas_call(
        body, name=name, grid_spec=grid_spec,
        out_shape=[jax.ShapeDtypeStruct((s, HEADS * HEAD), F32), jax.ShapeDtypeStruct((HEADS, s, 1), F32)],
        compiler_params=_params(("parallel", "arbitrary")),
    )(ii, jj, q, k, v)


def _attn_bwd(q, k, v, do, o, lse, name, dkv_sum=None):
    s = q.shape[0]
    t = min(ATT_TILE, s)
    nb = s // t
    jj, ii = _att_pairs(nb, by_query=False)
    nsum = 2 if dkv_sum else 0

    def body(jj_ref, ii_ref, q_ref, k_ref, v_ref, do_ref, o_ref, lse_ref, *rest):
        dq_ref, dk_ref, dv_ref, dk_acc, dv_acc, sc_s, dp_s, p_s, ds_s, dl_s = rest[nsum:]
        step = pl.program_id(1)
        i, j = ii_ref[step], jj_ref[step]

        @pl.when(step == 0)
        def _():
            dq_ref[...] = jnp.zeros_like(dq_ref)

        @pl.when(i == j)
        def _():
            dk_acc[...] = jnp.zeros_like(dk_acc)
            dv_acc[...] = jnp.zeros_like(dv_acc)

        do_f = do_ref[...]
        dob = do_f.astype(BF16)
        dl_s[...] = jnp.sum(do_f * o_ref[...], axis=-1, keepdims=True)
        sc_s[...] = lax.dot_general(q_ref[...], k_ref[...], _NT, preferred_element_type=F32)
        dp_s[...] = lax.dot_general(dob, v_ref[...], _NT, preferred_element_type=F32)

        def softmax_strips(diagonal):
            for r in range(t // ATT_STRIP):
                rows = slice(r * ATT_STRIP, (r + 1) * ATT_STRIP)
                p = jnp.exp(sc_s[rows, :] * ATT_SCALE - lse_ref[rows, :])
                if diagonal:
                    p = jnp.where(_att_strip_mask(r, t), p, 0.0)
                p_s[rows, :] = p.astype(BF16)
                ds_s[rows, :] = (p * (dp_s[rows, :] - dl_s[rows, :]) * ATT_SCALE).astype(BF16)

        pl.when(i == j)(functools.partial(softmax_strips, True))
        pl.when(i != j)(functools.partial(softmax_strips, False))
        ds = ds_s[...]
        dv_acc[...] += lax.dot_general(p_s[...], dob, _TN, preferred_element_type=F32)
        dk_acc[...] += lax.dot_general(ds, q_ref[...], _TN, preferred_element_type=F32)
        rows = pl.ds(pl.multiple_of(i * t, t), t)
        dq_ref[rows, :] += jnp.dot(ds, k_ref[...], preferred_element_type=F32)

        @pl.when(i == nb - 1)
        def _():
            dk_ref[...] = dk_acc[...] + rest[0][...] if nsum else dk_acc[...]
            dv_ref[...] = dv_acc[...] + rest[1][...] if nsum else dv_acc[...]

    dk_blk = pl.BlockSpec((t, HEAD_PAD), lambda h, n, jr, ir: (jr[n], h))
    dv_blk = pl.BlockSpec((t, HEAD), lambda h, n, jr, ir: (jr[n], h))
    grid_spec = pltpu.PrefetchScalarGridSpec(
        num_scalar_prefetch=2, grid=(HEADS, len(jj)),
        in_specs=[pl.BlockSpec((t, HEAD_PAD), lambda h, n, jr, ir: (ir[n], h)),
                  pl.BlockSpec((t, HEAD_PAD), lambda h, n, jr, ir: (jr[n], h)),
                  pl.BlockSpec((t, HEAD), lambda h, n, jr, ir: (jr[n], h)),
                  pl.BlockSpec((t, HEAD), lambda h, n, jr, ir: (ir[n], h)),
                  pl.BlockSpec((t, HEAD), lambda h, n, jr, ir: (ir[n], h)),
                  pl.BlockSpec((None, t, 1), lambda h, n, jr, ir: (h, ir[n], 0))] + [dk_blk, dv_blk][:nsum],
        out_specs=[pl.BlockSpec((s, HEAD_PAD), lambda h, n, jr, ir: (0, h)), dk_blk, dv_blk],
        scratch_shapes=[pltpu.VMEM((t, HEAD_PAD), F32), pltpu.VMEM((t, HEAD), F32), pltpu.VMEM((t, t), F32),
                        pltpu.VMEM((t, t), F32), pltpu.VMEM((t, t), BF16), pltpu.VMEM((t, t), BF16),
                        pltpu.VMEM((t, 1), F32)])
    return pl.pallas_call(
        body, name=name, grid_spec=grid_spec,
        out_shape=[jax.ShapeDtypeStruct((s, HEADS * HEAD_PAD), F32)] * 2 + [jax.ShapeDtypeStruct((s, HEADS * HEAD), F32)],
        compiler_params=_params(("parallel", "arbitrary")),
    )(jj, ii, q, k, v, do, o, lse, *(dkv_sum or ()))


def _rope_tables(positions):
    half = ROPE // 2
    inv_freq = ROPE_BASE ** (-jnp.arange(half, dtype=F32) / half)
    ang = positions.astype(F32)[:, None] * inv_freq
    cos, sin = jnp.cos(ang), jnp.sin(ang)
    return jnp.concatenate([cos] * 4, axis=1), jnp.concatenate([-sin, sin] * 2, axis=1)


def _loss_and_grad(y, target, name):
    s = y.shape[0]
    ts = min(ROW_TILE, s)

    def body(y_ref, t_ref, dy_ref, l_ref):
        e = y_ref[...] - t_ref[...]
        dy_ref[...] = e * (1.0 / D)
        part = jnp.sum(jnp.sum(e * e, axis=-1, keepdims=True) * (0.5 / D), axis=0, keepdims=True)
        part = part * jnp.ones((1, 128), F32)

        @pl.when(pl.program_id(0) == 0)
        def _():
            l_ref[...] = part

        @pl.when(pl.program_id(0) > 0)
        def _():
            l_ref[...] += part

    return pl.pallas_call(
        body, name=name, grid=(s // ts,),
        in_specs=[pl.BlockSpec((ts, D), lambda i: (i, 0))] * 2,
        out_specs=[pl.BlockSpec((ts, D), lambda i: (i, 0)), pl.BlockSpec((1, 128), lambda i: (0, 0))],
        out_shape=[jax.ShapeDtypeStruct((s, D), F32), jax.ShapeDtypeStruct((1, 128), F32)],
        compiler_params=_params(("arbitrary",)),
    )(y, target)


ANY = pl.BlockSpec(memory_space=pl.ANY)


def _all_gather(shard, name):
    def body(x_ref, out_ref, send_sems, recv_sems, local_sem):
        x, y, c = lax.axis_index("x"), lax.axis_index("y"), lax.axis_index("c")
        me, sibling = (x, y, c), (x, y, 1 - c)
        chips = [(1 - x, y), (x, 1 - y), (1 - x, 1 - y)]

        def rows(px, py, pc):
            return out_ref.at[4 * px + 2 * py + pc]

        def copy(k, block, to, src=None):
            return pltpu.make_async_remote_copy(
                src_ref=rows(*block) if src is None else src, dst_ref=rows(*block),
                send_sem=send_sems.at[k], recv_sem=recv_sems.at[k], device_id=to, device_id_type=MESH)

        mine = pltpu.make_async_copy(x_ref, rows(*me), local_sem)
        mine.start()
        first = [copy(0, me, sibling, src=x_ref)]
        first += [copy(1 + j, me, (*chip, c), src=x_ref) for j, chip in enumerate(chips)]
        for cp in first:
            cp.start()
        passed = [copy(4 + j, (*chip, c), sibling) for j, chip in enumerate(chips)]
        for j, chip in enumerate(chips):
            copy(1 + j, (*chip, c), me).wait_recv()
            passed[j].start()
        copy(0, sibling, me).wait_recv()
        for j, chip in enumerate(chips):
            copy(4 + j, (*chip, 1 - c), me).wait_recv()
        for cp in first + passed:
            cp.wait_send()
        mine.wait()

    return pl.pallas_call(
        body, name=name, out_shape=jax.ShapeDtypeStruct((N_DEV,) + shard.shape, shard.dtype),
        in_specs=[ANY], out_specs=ANY,
        scratch_shapes=[pltpu.SemaphoreType.DMA((7,)), pltpu.SemaphoreType.DMA((7,)), pltpu.SemaphoreType.DMA],
    )(shard)


def _exchange(blocks, name):
    def body(x_ref, out_ref, send_sems, recv_sems, local_sem):
        x, y, c = lax.axis_index("x"), lax.axis_index("y"), lax.axis_index("c")
        me = 4 * x + 2 * y + c
        mine = pltpu.make_async_copy(x_ref.at[me], out_ref.at[me], local_sem)
        mine.start()
        copies = []
        for k in range(1, N_DEV):
            px = 1 - x if k & 4 else x
            py = 1 - y if k & 2 else y
            pc = 1 - c if k & 1 else c
            peer = 4 * px + 2 * py + pc
            cp = pltpu.make_async_remote_copy(
                src_ref=x_ref.at[peer], dst_ref=out_ref.at[me], send_sem=send_sems.at[k - 1],
                recv_sem=recv_sems.at[k - 1], device_id=(px, py, pc), device_id_type=MESH)
            cp.start()
            copies.append((cp, pltpu.make_async_remote_copy(
                src_ref=x_ref.at[peer], dst_ref=out_ref.at[peer], send_sem=send_sems.at[k - 1],
                recv_sem=recv_sems.at[k - 1], device_id=(px, py, pc), device_id_type=MESH)))
        for cp, landing in copies:
            landing.wait_recv()
        for cp, landing in copies:
            cp.wait_send()
        mine.wait()

    return pl.pallas_call(
        body, name=name, out_shape=jax.ShapeDtypeStruct(blocks.shape, blocks.dtype),
        in_specs=[ANY], out_specs=ANY,
        scratch_shapes=[pltpu.SemaphoreType.DMA((7,)), pltpu.SemaphoreType.DMA((7,)), pltpu.SemaphoreType.DMA],
    )(blocks)


HBM = pl.BlockSpec(memory_space=pltpu.HBM)
SEM = pl.BlockSpec(memory_space=pltpu.SEMAPHORE)
EFFECT = pltpu.SideEffectType.DATAFLOW_SIDE_EFFECTING


def _peers():
    x, y, c = lax.axis_index("x"), lax.axis_index("y"), lax.axis_index("c")
    peers = []
    for k in range(1, N_DEV):
        px = 1 - x if k & 4 else x
        py = 1 - y if k & 2 else y
        pc = 1 - c if k & 1 else c
        peers.append(((px, py, pc), 4 * px + 2 * py + pc))
    return 4 * x + 2 * y + c, peers


def _send_start(srcs, name, gather):
    n = len(srcs)
    lands = [((N_DEV,) + s.shape) if gather else s.shape for s in srcs]

    def body(*refs):
        src_refs, land_refs = refs[:n], refs[n:2 * n]
        send_sems, recv_sems, token = refs[2 * n], refs[2 * n + 1], refs[-1]
        me, peers = _peers()
        for i in range(n):
            for k, (dev, idx) in enumerate(peers):
                pltpu.make_async_remote_copy(
                    src_ref=src_refs[i] if gather else src_refs[i].at[idx], dst_ref=land_refs[i].at[me],
                    send_sem=send_sems.at[7 * i + k], recv_sem=recv_sems.at[7 * i + k], device_id=dev,
                    device_id_type=MESH).start()
        token[...] = jnp.zeros_like(token)

    res = pl.pallas_call(
        body, name=name,
        out_shape=(pltpu.SemaphoreType.DMA((7 * n,)), pltpu.SemaphoreType.DMA((7 * n,)),
                   *[pltpu.HBM(s.shape, s.dtype) for s in srcs],
                   *[pltpu.HBM(shape, s.dtype) for shape, s in zip(lands, srcs)],
                   jax.ShapeDtypeStruct((8, 128), F32)),
        in_specs=(HBM,) * (2 * n), out_specs=(SEM, SEM) + (HBM,) * (2 * n) + (pl.BlockSpec(memory_space=pltpu.VMEM),),
        input_output_aliases={i: 2 + i for i in range(2 * n)},
        compiler_params=pltpu.CompilerParams(has_side_effects=EFFECT),
    )(*[pltpu.with_memory_space_constraint(s, pltpu.HBM) for s in srcs],
      *[pltpu.with_memory_space_constraint(lax.empty(shape, s.dtype), pltpu.HBM) for shape, s in zip(lands, srcs)])
    return dict(sems=res[:2], srcs=res[2:2 + n], lands=res[2 + n:2 + 2 * n], token=res[-1])


def _send_wait(handle, after, name, gather):
    n = len(handle["srcs"])

    def body(*refs):
        src_refs, land_refs = refs[:n], refs[n:2 * n]
        send_sems, recv_sems = refs[2 * n], refs[2 * n + 1]
        me, peers = _peers()
        for i in range(n):
            for k, (dev, idx) in enumerate(peers):
                cp = pltpu.make_async_remote_copy(
                    src_ref=src_refs[i] if gather else src_refs[i].at[idx], dst_ref=land_refs[i].at[idx],
                    send_sem=send_sems.at[7 * i + k], recv_sem=recv_sems.at[7 * i + k], device_id=dev,
                    device_id_type=MESH)
                cp.wait_send()
                cp.wait_recv()

    both = list(handle["srcs"]) + list(handle["lands"])
    res = pl.pallas_call(
        body, name=name, out_shape=tuple(pltpu.HBM(t.shape, t.dtype) for t in both),
        in_specs=(HBM,) * (2 * n) + (SEM, SEM, pl.BlockSpec(memory_space=pl.ANY)), out_specs=(HBM,) * (2 * n),
        input_output_aliases={i: i for i in range(2 * n)},
        compiler_params=pltpu.CompilerParams(has_side_effects=EFFECT),
    )(*both, *handle["sems"], after)
    return res[:n], res[n:]


def _adamw(parts, w, m, v, name, tr=128):
    pieces = len(parts)
    n, r, wd = parts[0].shape
    tr = next((t for t in (tr, 64, 32, 16) if r % t == 0), r)
    nrt = r // tr

    def body(*refs):
        w_ref, m_ref, v_ref, g_ref, d_ref, nm_ref, nv_ref = refs[pieces:]

        def update(p_ref):
            g = p_ref[0].astype(F32)
            for k in range(1, n):
                g = g + p_ref[k].astype(F32)
            m_new = B1 * m_ref[...] + (1.0 - B1) * g
            v_new = B2 * v_ref[...] + (1.0 - B2) * (g * g)
            m_hat = m_new / (1.0 - B1 ** STEP)
            v_hat = v_new / (1.0 - B2 ** STEP)
            g_ref[...] = g
            d_ref[...] = -LR * (m_hat / (jnp.sqrt(v_hat) + ADAM_EPS) + WD * w_ref[...])
            nm_ref[...] = m_new
            nv_ref[...] = v_new

        for p in range(pieces):
            pl.when(pl.program_id(0) == p)(functools.partial(update, refs[p]))

    part_spec = lambda p: pl.BlockSpec((n, tr, wd), lambda l, i: (0, jnp.clip(i + (l - p) * nrt, 0, nrt - 1), 0))
    blk = pl.BlockSpec((tr, wd), lambda l, i: (l * nrt + i, 0))
    return pl.pallas_call(
        body, name=name, grid=(pieces, nrt),
        in_specs=[part_spec(p) for p in range(pieces)] + [blk, blk, blk],
        out_specs=[blk] * 4, out_shape=[jax.ShapeDtypeStruct((pieces * r, wd), F32)] * 4,
        compiler_params=_params(("arbitrary", "arbitrary")),
    )(*parts, w, m, v)


def _outer8(ct, dm, name):
    k, n = ct.shape[0], dm.shape[1]

    def body(c_ref, d_ref, o_ref):
        cv, dv = c_ref[...], d_ref[...]
        acc = cv[:, 0:1] * dv[0:1, :]
        for s in range(1, N_DEV):
            acc = acc + cv[:, s:s + 1] * dv[s:s + 1, :]
        o_ref[...] = acc

    tk = 256
    return pl.pallas_call(
        body, name=name, grid=(k // tk,),
        in_specs=[pl.BlockSpec((tk, N_DEV), lambda i: (i, 0)), pl.BlockSpec((N_DEV, n), lambda i: (0, 0))],
        out_specs=pl.BlockSpec((tk, n), lambda i: (i, 0)), out_shape=jax.ShapeDtypeStruct((k, n), F32),
        compiler_params=_params(("parallel",)),
    )(ct, dm)


FULL = (0, D)
C128 = (0, 128)
HEAD_NOPE = [(h * HEAD_PAD, NOPE) for h in range(HEADS)]
HEAD_ROPE = [(h * HEAD_PAD + NOPE, 128) for h in range(HEADS)]
HEAD_ALL = [(h * HEAD_PAD, HEAD_PAD) for h in range(HEADS)]
HEAD_V = [(h * HEAD, HEAD) for h in range(HEADS)]


def _modulate(x, p, ties=()):
    return _rowwise_fwd(_modulate_fn, [(x, FULL)], [p["gain"], p["scale"], p["shift"]], [(D, BF16, FULL)], "modulate",
                        ties=ties)[0]


def _residual_bwd(y, gm, dxn):
    return _rowwise_bwd(_gate_only_fn, [(y, FULL)], [gm], [(D, F32, FULL)], [dxn], [(0, FULL)], [(D, BF16)],
                        "residual_bwd")


def _modulate_bwd(x, p, dh, dx_in, prev=None):
    pars = [p["gain"], p["scale"], p["shift"]]
    if prev is None:
        return list(_rowwise_bwd(_modulate_fn, [(x, FULL)], pars, [(D, BF16, FULL)], [dh], [(0, FULL)], [(D, F32)],
                                 "modulate_bwd", add={0: dx_in})) + [None]
    s = x.shape[0]
    ts = min(ROW_TILE, s)

    def body(x_ref, g_ref, sc_ref, sh_ref, dh_ref, din_ref, y_ref, gm_ref, dx_ref, dy_ref, dg_ref, dsc_ref, dsh_ref,
             dgm_ref):
        _, vjp = jax.vjp(lambda *t: _modulate_fn(0, *t)[0], x_ref[...], g_ref[...], sc_ref[...], sh_ref[...])
        dxm, dg, dsc, dsh = vjp(dh_ref[...])
        dx = dxm + din_ref[...]
        dx_ref[...] = dx
        dy_ref[...] = (gm_ref[...] * dx).astype(BF16)
        sums = (dg, dsc, dsh, jnp.sum(dx * y_ref[...], axis=0, keepdims=True))
        first = pl.program_id(0) == 0
        for ref, val in zip((dg_ref, dsc_ref, dsh_ref, dgm_ref), sums):
            @pl.when(first)
            def _(ref=ref, val=val):
                ref[...] = val

            @pl.when(jnp.logical_not(first))
            def _(ref=ref, val=val):
                ref[...] += val

    blk = pl.BlockSpec((ts, D), lambda i: (i, 0))
    vec = pl.BlockSpec((1, D), lambda i: (0, 0))
    dx, dy, dg, dsc, dsh, dgm = pl.pallas_call(
        body, name="modulate_bwd_chain", grid=(s // ts,),
        in_specs=[blk, vec, vec, vec, blk, blk, blk, vec], out_specs=[blk, blk, vec, vec, vec, vec],
        out_shape=[jax.ShapeDtypeStruct((s, D), F32), jax.ShapeDtypeStruct((s, D), BF16)]
        + [jax.ShapeDtypeStruct((1, D), F32)] * 4,
        compiler_params=_params(("arbitrary",)),
    )(x, *pars, dh, dx_in, prev[0], prev[1])
    return [dx, dg, dsc, dsh, (dy, dgm)]


def _out_proj(a, w, x, p, nxt, name, **kw):
    res = _matmul([(a, w)], "nn", name, out_dtype=BF16, resid=(x, p["gm"]) + tuple(nxt or ()), **kw)
    return res[1], res[0], (res[2] if nxt else None)


def _ffn_fwd(x, p, ties=(), h=None, nxt=None):
    if h is None:
        h, ties = _modulate(x, p, ties), ()
    gate, up, act = _ffn_in(h, p["w_in"], "ffn_in", ties=ties)
    res = _ffn_out(act, p["wo"], (x, p["gm"]) + tuple(nxt or ()), "ffn_out")
    return res[1], dict(x=x, h=h, gate=gate, up=up, act=act, y=res[0]), (res[2] if nxt else None)


def _ffn_bwd(t, p, dxn, res=None, prev=None, ties=()):
    dy, dgm = res or _residual_bwd(t["y"], p["gm"], dxn)
    dgate, dup = _ffn_bwd_act(dy, p["wo"], t["gate"], t["up"], "ffn_bwd_act", ties=ties)
    dwo = _ffn_dwo(t["act"], dy, "ffn_dwo", ties=ties)
    dh = _ffn_dh(dgate, dup, p["w_in"], "ffn_dh")
    dwi = _ffn_dwi(t["h"], dgate, dup, "ffn_dwi")
    dx, dgain, dscale, dshift, res_prev = _modulate_bwd(t["x"], p, dh, dxn, prev)
    return dx, dict(gain=dgain, scale=dscale, shift=dshift, gm=dgm, w_in=dwi, wo=dwo), res_prev


def _pad128(t):
    return jnp.pad(t, ((0, 0), (0, 128 - t.shape[1])))


def _gdn_fwd(x, p, ties=(), h=None, nxt=None):
    s = x.shape[0]
    if h is None:
        h, ties = _modulate(x, p, ties), ()
    pm = _mm(h, p["w_main"], "nn", "gdn_proj", ties=ties)
    tail = _mm(h, p["w_tail"], "nn", "gdn_proj_tail", ties=ties)
    qkv = _gdn_conv_fwd(pm, p["conv_w"], "gdn_conv")
    beta, gcum = _rowwise_fwd(_gdn_gates_fn, [(tail, C128), (tail, (128, 128))], [p["a_log"], p["dt_bias"]],
                              [(128, F32, C128)] * 2, "gdn_gates")
    grow = gcum[:, :HEADS].reshape(s // CHUNK, CHUNK, N_GROUPS, GROUP).transpose(0, 2, 3, 1)
    grow = grow.reshape(s // CHUNK, N_GROUPS, 1, GROWS)
    o, states, invs = _gdn_scan_fwd(qkv, beta, gcum, grow, "gdn_scan")
    on, = _rowwise_fwd(_gdn_outnorm_fn, [(o, HEAD_V), (pm, [(3 * D + h_ * HEAD, HEAD) for h_ in range(HEADS)])],
                       [p["norm_g"]], [(D, BF16, HEAD_V)], "gdn_outnorm", groups=HEADS)
    xn, y, hn = _out_proj(on, p["w_out"], x, p, nxt, "mix_out", tm=512)
    t = dict(x=x, h=h, pm=pm, tail=tail, qkv=qkv, beta=beta, gcum=gcum, grow=grow, o=o, states=states, invs=invs,
             on=on, y=y)
    return xn, t, hn


def _gdn_bwd(t, p, dxn, res=None, prev=None, ties=()):
    s = dxn.shape[0]
    zc = [(3 * D + h_ * HEAD, HEAD) for h_ in range(HEADS)]
    dy, dgm = res or _residual_bwd(t["y"], p["gm"], dxn)
    dw_out = _mm(t["on"], dy, "tn", "mix_dwo", ties=ties)
    don = _mm(dy, p["w_out"], "nt", "mix_dout", ties=ties)
    do, dz, dnorm_g = _rowwise_bwd(_gdn_outnorm_fn, [(t["o"], HEAD_V), (t["pm"], zc)], [p["norm_g"]],
                                   [(D, BF16, HEAD_V)], [don], [(0, HEAD_V), (1, HEAD_V)], [(D, F32), (D, BF16)],
                                   "gdn_outnorm_bwd", groups=HEADS)
    dq, dk, dv, dbeta, dg, dgr = _gdn_scan_bwd(t["qkv"], t["beta"], t["gcum"], t["grow"], t["states"], t["invs"], do,
                                               "gdn_scan_bwd")
    dg = dg + _pad128(dgr.reshape(s // CHUNK, N_GROUPS, GROUP, CHUNK).transpose(0, 3, 1, 2).reshape(s, HEADS))
    dtail, da_log, ddt = _rowwise_bwd(_gdn_gates_fn, [(t["tail"], C128), (t["tail"], (128, 128))],
                                      [p["a_log"], p["dt_bias"]], [(128, F32, C128)] * 2, [dbeta, dg],
                                      [(0, C128), (0, (128, 128))], [(256, F32)], "gdn_gates_bwd")
    dxs, dcw = [], []
    for part, d in enumerate((dq, dk, dv)):
        dx_, dw_ = _gdn_conv_bwd(t["pm"], p["conv_w"], d, part, "gdn_conv_bwd")
        dxs.append(dx_)
        dcw.append(dw_)
    pieces = dxs + [dz]
    dh = _matmul([(d, p["w_main"]) for d in pieces] + [(dtail, p["w_tail"])], "nt", "gdn_dh",
                 boffs=[0, D, 2 * D, 3 * D, 0], tk=512)
    dw_main = [_mm(t["h"], d, "tn", "gdn_dwi") for d in pieces]
    dw_tail = _mm(t["h"], dtail, "tn", "gdn_dwi_tail")
    dx, dgain, dscale, dshift, res_prev = _modulate_bwd(t["x"], p, dh, dxn, prev)
    return dx, dict(gain=dgain, scale=dscale, shift=dshift, gm=dgm, w_main=jnp.concatenate(dw_main, axis=1),
                    w_tail=dw_tail, conv_w=jnp.concatenate(dcw, axis=1), a_log=da_log, dt_bias=ddt,
                    norm_g=dnorm_g, w_out=dw_out), res_prev


def _q_rows(q2, cosf, sins):
    return [(q2, HEAD_NOPE), (q2, HEAD_ROPE), (cosf, C128), (sins, C128)]


def _mla_fwd(x, p, kv, ties=(), h=None, nxt=None):
    if h is None:
        h, ties = _modulate(x, p, ties), ()
    cq = _mm(h, p["w_dq"], "nn", "mla_dq", ties=ties)
    cqn, = _rowwise_fwd(_rms_fn, [(cq, (0, Q_LORA))], [p["q_lora_g"]], [(Q_LORA, BF16, (0, Q_LORA))], "mla_qlora_norm")
    q2 = _mm(cqn, p["w_uq"], "nn", "mla_uq")
    qn, = _rowwise_fwd(_q_norm_rope_fn, _q_rows(q2, kv["cosf"], kv["sins"]), [p["q_gn"], p["q_gr"]],
                       [(HEADS * HEAD_PAD, BF16, HEAD_ALL)], "mla_q_norm", groups=HEADS)
    o, lse = _attn_fwd(qn, kv["kn"], kv["vb"], "mla_attn")
    xn, y, hn = _out_proj(o, p["w_out"], x, p, nxt, "mix_out", tm=512)
    return xn, dict(x=x, h=h, cq=cq, cqn=cqn, q2=q2, qn=qn, o=o, lse=lse, y=y), hn


def _mla_bwd(t, p, kv, dxn, res=None, prev=None, ties=(), dkv_sum=None):
    dy, dgm = res or _residual_bwd(t["y"], p["gm"], dxn)
    dw_out = _mm(t["o"], dy, "tn", "mix_dwo", ties=ties)
    do = _mm(dy, p["w_out"], "nt", "mix_dout", ties=ties)
    dq, dk, dv = _attn_bwd(t["qn"], kv["kn"], kv["vb"], do, t["o"], t["lse"], "mla_attn_bwd", dkv_sum)
    dq2, dq_gn, dq_gr = _rowwise_bwd(_q_norm_rope_fn, _q_rows(t["q2"], kv["cosf"], kv["sins"]), [p["q_gn"], p["q_gr"]],
                                     [(HEADS * HEAD_PAD, BF16, HEAD_ALL)], [dq],
                                     [(0, HEAD_NOPE), (0, HEAD_ROPE), None, None], [(HEADS * HEAD_PAD, BF16)],
                                     "mla_q_norm_bwd", groups=HEADS)
    dw_uq = _mm(t["cqn"], dq2, "tn", "mla_dwuq")
    dcqn = _mm(dq2, p["w_uq"], "nt", "mla_dcq")
    dcq, dq_lora_g = _rowwise_bwd(_rms_fn, [(t["cq"], (0, Q_LORA))], [p["q_lora_g"]], [(Q_LORA, BF16, (0, Q_LORA))],
                                  [dcqn], [(0, (0, Q_LORA))], [(Q_LORA, BF16)], "mla_qlora_norm_bwd")
    dw_dq = _mm(t["h"], dcq, "tn", "mla_dwdq")
    dh = _mm(dcq, p["w_dq"], "nt", "mla_dh")
    dx, dgain, dscale, dshift, res_prev = _modulate_bwd(t["x"], p, dh, dxn, prev)
    grads = dict(gain=dgain, scale=dscale, shift=dshift, gm=dgm, w_dq=dw_dq, q_lora_g=dq_lora_g, w_uq=dw_uq,
                 q_gn=dq_gn, q_gr=dq_gr, w_out=dw_out)
    return dx, grads, res_prev, dk, dv


def _k_rows(kvp, ckv, cosf, sins):
    return [(kvp, HEAD_NOPE), (kvp, HEAD_ROPE), (ckv, (KV_LORA, 128)), (cosf, C128), (sins, C128)]


def _kv_fwd(x, p, cosf, sins):
    h = _modulate(x, p)
    ckv = _mm(h, p["w_dkv"], "nn", "kv_down")
    lat, = _rowwise_fwd(_rms_fn, [(ckv, (0, KV_LORA))], [p["kv_g"]], [(KV_LORA, BF16, (0, KV_LORA))], "kv_norm")
    kvp = _mm(lat, p["w_ukv"], "nn", "kv_up")
    kn, vb = _rowwise_fwd(_k_norm_rope_fn, _k_rows(kvp, ckv, cosf, sins), [p["k_gn"], p["k_gr"]],
                          [(HEADS * HEAD_PAD, BF16, HEAD_ALL), (HEADS * HEAD, BF16, HEAD_V)], "kv_k_norm",
                          groups=HEADS)
    return dict(x=x, h=h, ckv=ckv, lat=lat, kvp=kvp, kn=kn, vb=vb, cosf=cosf, sins=sins)


def _kv_bwd(t, p, dk, dv, dx_in, prev):
    dkvp, drope, dk_gn, dk_gr = _rowwise_bwd(
        _k_norm_rope_fn, _k_rows(t["kvp"], t["ckv"], t["cosf"], t["sins"]), [p["k_gn"], p["k_gr"]],
        [(HEADS * HEAD_PAD, BF16, HEAD_ALL), (HEADS * HEAD, BF16, HEAD_V)], [dk, dv],
        [(0, HEAD_NOPE), (0, HEAD_ROPE), (1, C128), None, None], [(HEADS * HEAD_PAD, BF16), (128, F32)],
        "kv_k_norm_bwd", groups=HEADS)
    dw_ukv = _mm(t["lat"], dkvp, "tn", "kv_dwukv")
    dlat = _mm(dkvp, p["w_ukv"], "nt", "kv_dlat")
    dckv, dkv_g = _rowwise_bwd(_rms_fn, [(t["ckv"], (0, KV_LORA))], [p["kv_g"]], [(KV_LORA, BF16, (0, KV_LORA))],
                               [dlat], [(0, (0, KV_LORA))], [(KV_LORA, F32)], "kv_norm_bwd")
    dw_dkv = jnp.concatenate([_mm(t["h"], dckv, "tn", "kv_dwdkv"), _mm(t["h"], drope, "tn", "kv_dwdkv_rope")], axis=1)
    dh = _matmul([(dckv, p["w_dkv"]), (drope, p["w_dkv"])], "nt", "kv_dh", boffs=[0, KV_LORA])
    dx, dgain, dscale, dshift, res_prev = _modulate_bwd(t["x"], p, dh, dx_in, prev)
    return dx, dict(gain=dgain, scale=dscale, shift=dshift, w_dkv=dw_dkv, kv_g=dkv_g, w_ukv=dw_ukv, k_gn=dk_gn,
                    k_gr=dk_gr), res_prev


WEIGHTS = ["ada_w", "ada_b", "norm_g", "ffn_w_in", "ffn_w_out", "gdn_w_in", "gdn_conv_w", "gdn_a_log", "gdn_dt_bias",
           "gdn_norm_g", "gdn_w_out", "kv_ada_w", "kv_ada_b", "kv_norm_g", "mla_w_dkv", "mla_kv_norm_g", "mla_w_ukv",
           "mla_k_norm_g", "mla_w_dq", "mla_q_lora_norm_g", "mla_w_uq", "mla_q_norm_g", "mla_w_out"]
SMALL = [("ada_b", 4 * N_MOD * D), ("kv_ada_b", 2 * D), ("norm_g", DEPTH * 3 * D), ("gdn_conv_w", N_A * CONV_K * 3 * D),
         ("gdn_a_log", N_A * HEADS), ("gdn_dt_bias", N_A * HEADS), ("gdn_norm_g", N_A * HEAD), ("kv_norm_g", D),
         ("mla_kv_norm_g", KV_LORA), ("mla_k_norm_g", QK_HEAD), ("mla_q_lora_norm_g", 2 * Q_LORA),
         ("mla_q_norm_g", 2 * QK_HEAD)]
SMALL_REPLICATED = [n for n, _ in SMALL if n not in ("norm_g", "gdn_conv_w")]


def _silu_fn(g, t):
    return (_silu(t),)


def _dup_rope(t):
    return jnp.concatenate([t[..., :NOPE], t[..., NOPE:], t[..., NOPE:]], axis=-1)


def _fold_rope(t):
    return jnp.concatenate([t[..., :NOPE], t[..., NOPE:QK_HEAD] + t[..., QK_HEAD:]], axis=-1)


def _pack(pieces, rows):
    flat = jnp.concatenate([p.reshape(-1).astype(F32) for p in pieces])
    return jnp.pad(flat, (0, rows * 128 - flat.shape[0])).reshape(rows, 128)


def _step(a):
    me = 4 * lax.axis_index("x") + 2 * lax.axis_index("y") + lax.axis_index("c")
    x = a["x"][0]
    cosf, sins = _rope_tables(a["positions"][0])

    n_in = 2 * D_FF // N_DEV
    n_gdn = (4 * D + 2 * HEADS) // N_DEV
    AHEAD = 2

    stages = [(l, part) for l in range(DEPTH) for part in range(3)]

    def stage_shards(l, part):
        if part != 1:
            sh = {"ffn_w_in": a["ffn_w_in"][l, part // 2], "ffn_w_out": a["ffn_w_out"][l, part // 2]}
            if part == 2 and l == N_A - 1:
                sh.update(mla_w_dkv=a["mla_w_dkv"], mla_w_ukv=a["mla_w_ukv"])
            return sh
        if l < N_A:
            return {"gdn_w_in": a["gdn_w_in"][l], "gdn_w_out": a["gdn_w_out"][l]}
        j = l - N_A
        return {"mla_w_dq": a["mla_w_dq"][j], "mla_w_uq": a["mla_w_uq"][j], "mla_w_out": a["mla_w_out"][j]}

    def zero_of(t):
        return jnp.minimum(jnp.abs(t[(0,) * t.ndim].astype(F32)), 0.0)

    def start_stage(l, part, tie):
        sh = stage_shards(l, part)
        return list(sh), _send_start([(w + tie).astype(BF16) for w in sh.values()], f"fetch_start_{l}_{part}", gather=True)

    def finish_stage(l, part, names, handle, after):
        srcs, lands = _send_wait(handle, after, f"fetch_wait_{l}_{part}", gather=True)
        return {n: lax.dynamic_update_slice(land, src[None], (me, 0, 0)) for n, src, land in zip(names, srcs, lands)}

    n_cw, n_ng = N_A * CONV_K * 3 * HEAD, DEPTH * 3 * HEAD
    small_all = _all_gather(_pack([a["gdn_conv_w"], a["norm_g"], a["c"]], 44), "gather_small").reshape(N_DEV, -1)
    conv_w = small_all[:, :n_cw].reshape(N_DEV, N_A, CONV_K, 3 * HEAD).transpose(1, 2, 0, 3).reshape(N_A, CONV_K, 3 * D)
    norm_g = small_all[:, n_cw:n_cw + n_ng].reshape(N_DEV, DEPTH, 3, HEAD).transpose(1, 2, 0, 3).reshape(DEPTH, 3, D)
    c_all = small_all[:, n_cw + n_ng:n_cw + n_ng + D]

    c_act, = _rowwise_fwd(_silu_fn, [(c_all, FULL)], [], [(D, F32, FULL)], "c_act")
    n_ada = N_MOD * D // N_DEV
    parts = [_mm(c_act, a["ada_w"][l], "nn", "mod_proj") for l in range(DEPTH)]
    parts.append(_mm(c_act, a["kv_ada_w"], "nn", "mod_proj_kv"))
    mod_recv = _exchange(jnp.concatenate(parts, axis=1)[:, None, :], "exchange_mod")[:, 0]
    mod = mod_recv[:, :DEPTH * n_ada].reshape(N_DEV, DEPTH, n_ada).transpose(1, 0, 2).reshape(DEPTH, N_MOD * D)
    mod = (mod + a["ada_b"]).reshape(DEPTH, N_MOD, D)
    kvmod = mod_recv[:, DEPTH * n_ada:].reshape(2 * D) + a["kv_ada_b"]

    def row(v):
        return v[None]

    def ffn_params(l, i, w):
        k = 0 if i == 0 else 6
        return dict(gain=row(norm_g[l, 0 if i == 0 else 2]), shift=row(mod[l, k]), scale=row(mod[l, k + 1]),
                    gm=0.5 * row(mod[l, k + 2]), w_in=w["ffn_w_in"], wo=w["ffn_w_out"].reshape(D_FF, D))

    def gdn_params(l, w):
        w_in = w["gdn_w_in"].transpose(1, 0, 2).reshape(D, 4 * D + 2 * HEADS)
        pad = lambda t: jnp.pad(t, ((0, 0), (0, 128 - HEADS)))
        return dict(gain=row(norm_g[l, 1]), shift=row(mod[l, 3]), scale=row(mod[l, 4]), gm=row(mod[l, 5]),
                    w_main=w_in[:, :4 * D],
                    w_tail=jnp.concatenate([pad(w_in[:, 4 * D:4 * D + HEADS]), pad(w_in[:, 4 * D + HEADS:])], axis=1),
                    conv_w=conv_w[l], a_log=_pad128(row(a["gdn_a_log"][l])), dt_bias=_pad128(row(a["gdn_dt_bias"][l])),
                    norm_g=row(a["gdn_norm_g"][l]), w_out=w["gdn_w_out"].reshape(D, D))

    def mla_params(l, w):
        j = l - N_A
        uq = w["mla_w_uq"].transpose(1, 0, 2)
        qg = _dup_rope(a["mla_q_norm_g"][j])
        return dict(gain=row(norm_g[l, 1]), shift=row(mod[l, 3]), scale=row(mod[l, 4]), gm=row(mod[l, 5]),
                    w_dq=w["mla_w_dq"].reshape(D, Q_LORA), q_lora_g=row(a["mla_q_lora_norm_g"][j]),
                    w_uq=_dup_rope(uq).reshape(Q_LORA, HEADS * HEAD_PAD), q_gn=row(qg[:NOPE]), q_gr=row(qg[NOPE:]),
                    w_out=w["mla_w_out"].reshape(D, D))

    def kv_params(w):
        w_dkv = w["mla_w_dkv"].reshape(D, KV_LORA + ROPE)
        kg = _dup_rope(a["mla_k_norm_g"])
        return dict(gain=row(a["kv_norm_g"]), shift=row(kvmod[:D]), scale=row(kvmod[D:]),
                    w_dkv=jnp.concatenate([w_dkv, w_dkv[:, KV_LORA:]], axis=1), kv_g=row(a["mla_kv_norm_g"]),
                    w_ukv=w["mla_w_ukv"].transpose(1, 0, 2).reshape(KV_LORA, HEADS * 2 * HEAD), k_gn=row(kg[:NOPE]),
                    k_gr=row(kg[NOPE:]))

    tapes, kv, kv_p, h = [[] for _ in range(DEPTH)], None, None, None
    first = {name: _all_gather((w + zero_of(mod)).astype(BF16), "fetch_first_" + name)
             for name, w in stage_shards(0, 0).items()}
    pending = []
    for l, part in stages[1:1 + AHEAD]:
        tie = pending[-1][1]["token"][0, 0] if pending else zero_of(first["ffn_w_out"])
        pending.append(start_stage(l, part, tie))
    for n, (l, part) in enumerate(stages):
        if n == 0:
            w, ties = first, tuple(h["token"] for _, h in pending)
        else:
            names, handle = pending.pop(0)
            w = finish_stage(l, part, names, handle, x)
            ties = ()
            if n + AHEAD < len(stages):
                pending.append(start_stage(*stages[n + AHEAD], zero_of(w[names[0]])))
                ties = (pending[-1][1]["token"],)
        nxt = None
        if n + 1 < len(stages):
            l2, part2 = stages[n + 1]
            k2 = 3 * part2
            nxt = (row(norm_g[l2, part2]), row(mod[l2, k2 + 1]), row(mod[l2, k2]))
        if part != 1:
            p = ffn_params(l, part // 2, w)
            x, t, h = _ffn_fwd(x, p, ties, h, nxt)
        else:
            p = gdn_params(l, w) if l < N_A else mla_params(l, w)
            x, t, h = _gdn_fwd(x, p, ties, h, nxt) if l < N_A else _mla_fwd(x, p, kv, ties, h, nxt)
        tapes[l] += [p, t]
        if part == 2 and l == N_A - 1:
            kv_p = kv_params(w)
            kv = _kv_fwd(x, kv_p, cosf, sins)
    dx, loss_blk = _loss_and_grad(x, a["loss_target"][0], "loss")
    loss = lax.psum(loss_blk[0, 0], ("x", "y", "c"))

    def by_cols(g, n):
        return g.reshape(g.shape[0], -1, n).transpose(1, 0, 2)

    def ffn_blocks(g):
        return {"ffn_w_in": g["w_in"], "ffn_w_out": g["wo"].reshape(N_DEV, D_FF // N_DEV, D)}

    def mixer_blocks(l, g):
        if l < N_A:
            full = jnp.concatenate([g["w_main"], g["w_tail"][:, :HEADS], g["w_tail"][:, 128:128 + HEADS]], axis=1)
            return {"gdn_w_in": by_cols(full, n_gdn), "gdn_w_out": g["w_out"].reshape(N_DEV, D // N_DEV, D)}
        return {"mla_w_dq": g["w_dq"].reshape(N_DEV, D // N_DEV, Q_LORA),
                "mla_w_uq": _fold_rope(g["w_uq"].reshape(Q_LORA, HEADS, HEAD_PAD)).transpose(1, 0, 2),
                "mla_w_out": g["w_out"].reshape(N_DEV, D // N_DEV, D)}

    sent = []

    def send(key, blocks, tie=0.0):
        handle = _send_start([(b + tie).astype(BF16) for b in blocks.values()], "grad_start_" + "_".join(map(str, key)),
                             gather=False)
        sent.append((key, list(blocks), handle))
        return (handle["token"],)

    grads = [None] * DEPTH
    dk_sum = dv_sum = kv_grads = res = None
    ties = ()
    for l in reversed(range(DEPTH)):
        p1, t1, pm_, tm_, p2, t2 = tapes[l]
        if l == N_A - 1:
            dx, kv_grads, res = _kv_bwd(kv, kv_p, dk_sum, dv_sum, dx, (t2["y"], p2["gm"]))
            d_dkv = kv_grads["w_dkv"]
            ties += send((l, 3), {
                "mla_w_dkv": jnp.concatenate(
                    [d_dkv[:, :KV_LORA], d_dkv[:, KV_LORA:KV_LORA + ROPE] + d_dkv[:, KV_LORA + ROPE:]],
                    axis=1).reshape(N_DEV, D // N_DEV, KV_LORA + ROPE),
                "mla_w_ukv": by_cols(kv_grads["w_ukv"], 2 * HEAD)})
        dx, g2, res = _ffn_bwd(t2, p2, dx, res, (tm_["y"], pm_["gm"]), ties)
        ties = send((l, 2), ffn_blocks(g2))
        if l < N_A:
            dx, gm_, res = _gdn_bwd(tm_, pm_, dx, res, (t1["y"], p1["gm"]), ties)
        else:
            dx, gm_, res, dk_sum, dv_sum = _mla_bwd(tm_, pm_, kv, dx, res, (t1["y"], p1["gm"]), ties,
                                                    None if dk_sum is None else (dk_sum, dv_sum))
        ties = send((l, 1), mixer_blocks(l, gm_))
        prev = (tapes[l - 1][5]["y"], tapes[l - 1][4]["gm"]) if l > 0 and l != N_A else None
        dx, g1, res = _ffn_bwd(t1, p1, dx, res, prev, ties)
        if l > 0:
            ties = send((l, 0), ffn_blocks(g1))
        grads[l] = (g1, gm_, g2)

    out = {}
    def dmod(l):
        g1, gm_, g2 = grads[l]
        return jnp.concatenate([g1["shift"], g1["scale"], 0.5 * g1["gm"], gm_["shift"], gm_["scale"], gm_["gm"],
                                g2["shift"], g2["scale"], 0.5 * g2["gm"]], axis=1)

    gdn = [grads[l][1] for l in range(N_A)]
    mla = [grads[l][1] for l in range(N_A, DEPTH)]
    small = {
        "ada_b": jnp.concatenate([dmod(l) for l in range(DEPTH)], axis=0),
        "kv_ada_b": jnp.concatenate([kv_grads["shift"], kv_grads["scale"]], axis=1),
        "norm_g": jnp.stack([jnp.concatenate([grads[l][0]["gain"], grads[l][1]["gain"], grads[l][2]["gain"]], axis=0)
                             for l in range(DEPTH)]),
        "gdn_conv_w": jnp.stack([g["conv_w"] for g in gdn]),
        "gdn_a_log": jnp.stack([g["a_log"][0, :HEADS] for g in gdn]),
        "gdn_dt_bias": jnp.stack([g["dt_bias"][0, :HEADS] for g in gdn]),
        "gdn_norm_g": jnp.stack([g["norm_g"][0] for g in gdn]),
        "kv_norm_g": kv_grads["gain"],
        "mla_kv_norm_g": kv_grads["kv_g"],
        "mla_k_norm_g": _fold_rope(jnp.concatenate([kv_grads["k_gn"], kv_grads["k_gr"]], axis=1)),
        "mla_q_lora_norm_g": jnp.stack([g["q_lora_g"][0] for g in mla]),
        "mla_q_norm_g": jnp.stack([_fold_rope(jnp.concatenate([g["q_gn"], g["q_gr"]], axis=1))[0] for g in mla]),
    }
    rows = 616
    assert sum(n for _, n in SMALL) <= rows * 128 and all(small[n].size == k for n, k in SMALL)
    small_recv = _all_gather(_pack([small[n] for n, _ in SMALL], rows), "gather_small_grads")
    small_recv = small_recv + send((0, 0), ffn_blocks(grads[0][0]), zero_of(small_recv))[0][0, 0]
    zero = lambda n, k: jnp.zeros((k,), F32)
    packed = {pre: _pack([a[pre + n] if n in SMALL_REPLICATED else zero(n, k) for n, k in SMALL], rows)
              for pre in ("", "m_", "v_")}
    res = _adamw([small_recv], packed[""], packed["m_"], packed["v_"], "adamw_small")
    offs = {}
    o = 0
    for n, k in SMALL:
        offs[n] = o
        o += k
    for n, k in SMALL:
        if n in SMALL_REPLICATED:
            out[n] = [r.reshape(-1)[offs[n]:offs[n] + k] for r in res]
    gsum = res[0].reshape(-1)
    g_norm = lax.dynamic_slice_in_dim(gsum[offs["norm_g"]:offs["norm_g"] + DEPTH * 3 * D].reshape(DEPTH * 3, D),
                                      me * HEAD, HEAD, axis=1)
    g_conv = lax.dynamic_slice_in_dim(
        gsum[offs["gdn_conv_w"]:offs["gdn_conv_w"] + N_A * CONV_K * 3 * D].reshape(N_A * CONV_K, 3 * D),
        me * 3 * HEAD, 3 * HEAD, axis=1)
    res2 = _adamw([_pack([g_norm, g_conv], 36)[None]], *[_pack([a[pre + "norm_g"], a[pre + "gdn_conv_w"]], 36)
                                                      for pre in ("", "m_", "v_")], "adamw_small")
    out["norm_g"] = [r.reshape(-1)[:n_ng] for r in res2]
    out["gdn_conv_w"] = [r.reshape(-1)[n_ng:n_ng + n_cw] for r in res2]

    c_act_t = c_act.T
    all_small = small_recv.reshape(N_DEV, -1)
    dmod_all = all_small[:, :DEPTH * N_MOD * D].reshape(N_DEV, DEPTH, N_MOD * D)
    dmod_mine = lax.dynamic_slice_in_dim(dmod_all, me * n_ada, n_ada, axis=2)
    g_ada = [_outer8(c_act_t, dmod_mine[:, l], "ada_grad")[None] for l in range(DEPTH)]
    out["ada_w"] = _adamw(g_ada, *[a[pre + "ada_w"].reshape(DEPTH * D, n_ada) for pre in ("", "m_", "v_")], "adamw")
    dkv_all = all_small[:, offs["kv_ada_b"]:offs["kv_ada_b"] + 2 * D]
    g_kv = _outer8(c_act_t, lax.dynamic_slice_in_dim(dkv_all, me * (2 * D // N_DEV), 2 * D // N_DEV, axis=1), "ada_grad")
    out["kv_ada_w"] = _adamw([g_kv[None]], *[a[pre + "kv_ada_w"] for pre in ("", "m_", "v_")], "adamw")

    pieces = {}
    for key, names, handle in sent:
        srcs, lands = _send_wait(handle, out["kv_ada_w"][0], "grad_wait_" + "_".join(map(str, key)), gather=False)
        for name, src, land in zip(names, srcs, lands):
            own = lax.dynamic_slice_in_dim(src, me, 1, axis=0)
            pieces.setdefault(name, []).append((key, lax.dynamic_update_slice(land, own, (me, 0, 0))))
    for name, parts in pieces.items():
        wide = a[name].shape[-1]
        out[name] = _adamw([p for _, p in sorted(parts, key=lambda kp: kp[0])], a[name].reshape(-1, wide),
                           a["m_" + name].reshape(-1, wide), a["v_" + name].reshape(-1, wide), "adamw")

    result = [loss, dx[None]]
    for k in range(4):
        result += [out[n][k].reshape(a[n].shape) for n in WEIGHTS]
    return tuple(result)


def kernel(x, c, positions, ada_w, ada_b, norm_g, ffn_w_in, ffn_w_out, gdn_w_in, gdn_conv_w, gdn_a_log, gdn_dt_bias, gdn_norm_g, gdn_w_out, kv_ada_w, kv_ada_b, kv_norm_g, mla_w_dkv, mla_kv_norm_g, mla_w_ukv, mla_k_norm_g, mla_w_dq, mla_q_lora_norm_g, mla_w_uq, mla_q_norm_g, mla_w_out, loss_target, m_ada_w, m_ada_b, m_norm_g, m_ffn_w_in, m_ffn_w_out, m_gdn_w_in, m_gdn_conv_w, m_gdn_a_log, m_gdn_dt_bias, m_gdn_norm_g, m_gdn_w_out, m_kv_ada_w, m_kv_ada_b, m_kv_norm_g, m_mla_w_dkv, m_mla_kv_norm_g, m_mla_w_ukv, m_mla_k_norm_g, m_mla_w_dq, m_mla_q_lora_norm_g, m_mla_w_uq, m_mla_q_norm_g, m_mla_w_out, v_ada_w, v_ada_b, v_norm_g, v_ffn_w_in, v_ffn_w_out, v_gdn_w_in, v_gdn_conv_w, v_gdn_a_log, v_gdn_dt_bias, v_gdn_norm_g, v_gdn_w_out, v_kv_ada_w, v_kv_ada_b, v_kv_norm_g, v_mla_w_dkv, v_mla_kv_norm_g, v_mla_w_ukv, v_mla_k_norm_g, v_mla_w_dq, v_mla_q_lora_norm_g, v_mla_w_uq, v_mla_q_norm_g, v_mla_w_out):
    return _step(dict(locals()))
```

```python
import functools

import jax
import jax.numpy as jnp
from jax import lax
from jax.experimental import pallas as pl
from jax.experimental.pallas import tpu as pltpu

F32 = jnp.float32
BF16 = jnp.bfloat16

N_DEV = 8
D = 1024
D_FF = 2816
DEPTH = 4
N_A = 2
N_MOD = 9
HEADS = 8
HEAD = 128
CHUNK = 64
CONV_K = 4
KV_LORA = 256
Q_LORA = 384
NOPE = 128
ROPE = 64
QK_HEAD = NOPE + ROPE
HEAD_PAD = 256
ROPE_BASE = 10000.0
EPS = 1e-6
LR, B1, B2, ADAM_EPS, WD, STEP = 0.001, 0.9, 0.999, 1e-08, 0.01, 10

VMEM_LIMIT = 48 * 1024 * 1024
ROW_TILE = 256
MESH = pl.DeviceIdType.MESH

_NN = (((1,), (0,)), ((), ()))
_NT = (((1,), (1,)), ((), ()))
_TN = (((0,), (0,)), ((), ()))
_DIMS = {"nn": _NN, "nt": _NT, "tn": _TN}


def _params(dims=None):
    return pltpu.CompilerParams(dimension_semantics=dims, vmem_limit_bytes=VMEM_LIMIT)


def _tile(n, target):
    for t in range(target - target % 128, 0, -128):
        if n % t == 0:
            return t
    return n


_TIE_SPEC1 = pl.BlockSpec((8, 128), lambda i: (0, 0))
_TIE_SPEC2 = pl.BlockSpec((8, 128), lambda i, j: (0, 0))
_TIE_SPEC3 = pl.BlockSpec((8, 128), lambda i, j, k: (0, 0))


def _matmul(pairs, form, name, out_dtype=F32, tm=1408, tn=1408, tk=1408, boffs=None, resid=None, ties=()):
    a0, b0 = pairs[0]
    if form == "nn":
        m, n = a0.shape[0], b0.shape[1]
        ks = [a.shape[1] for a, _ in pairs]
    elif form == "nt":
        m, n = a0.shape[0], b0.shape[0]
        ks = [a.shape[1] for a, _ in pairs]
    else:
        m, n = a0.shape[1], b0.shape[1]
        ks = [a.shape[0] for a, _ in pairs]
    tm, tn = _tile(m, tm), _tile(n, tn)
    tks = [_tile(k, tk) for k in ks]
    boffs = boffs or [0] * len(pairs)
    assert m % tm == 0 and n % tn == 0 and all(o % t == 0 for o, t in zip(boffs, tks)), (name, m, n, ks)
    steps = [k // t for k, t in zip(ks, tks)]
    starts = [sum(steps[:p]) for p in range(len(pairs))]
    nk = sum(steps)

    def kidx(p, k):
        return jnp.clip(k - starts[p], 0, steps[p] - 1)

    in_specs, args = [], []
    for p, (a, b) in enumerate(pairs):
        t = tks[p]
        if form == "tn":
            in_specs.append(pl.BlockSpec((t, tm), lambda i, j, k, p=p: (kidx(p, k), i)))
            in_specs.append(pl.BlockSpec((t, tn), lambda i, j, k, p=p: (kidx(p, k), j)))
        elif form == "nn":
            in_specs.append(pl.BlockSpec((tm, t), lambda i, j, k, p=p: (i, kidx(p, k))))
            in_specs.append(pl.BlockSpec((t, tn), lambda i, j, k, p=p: (kidx(p, k), j)))
        else:
            in_specs.append(pl.BlockSpec((tm, t), lambda i, j, k, p=p: (i, kidx(p, k))))
            in_specs.append(pl.BlockSpec((tn, t), lambda i, j, k, p=p, o=boffs[p] // t: (j, kidx(p, k) + o)))
        args += [a, b]
    dims = _DIMS[form]
    npairs = len(pairs)
    nres = len(resid or ())
    nin = 2 * npairs + len(ties) + nres
    out_blk = pl.BlockSpec((tm, tn), lambda i, j, k: (i, j))
    in_specs += [_TIE_SPEC3] * len(ties)
    args += list(ties)
    if resid:
        assert nres == 2 or (nres == 5 and tn == n)
        in_specs += [out_blk] + [pl.BlockSpec((1, tn), lambda i, j, k: (0, j))] * (nres - 1)
        args += list(resid)

    def body(*refs):
        o_ref = refs[nin]
        k = pl.program_id(2)

        def prod(p):
            return lax.dot_general(refs[2 * p][...].astype(BF16), refs[2 * p + 1][...].astype(BF16), dims,
                                   preferred_element_type=F32)

        def finish(y):
            o_ref[...] = y.astype(o_ref.dtype)
            if resid:
                x_ref, gate_ref = refs[nin - nres], refs[nin - nres + 1]
                xn = x_ref[...] + gate_ref[...] * y
                refs[nin + 1][...] = xn
                if nres == 5:
                    gain, scale, shift = (r[...] for r in refs[nin - 3:nin])
                    refs[nin + 2][...] = _modulate_fn(0, xn, gain, scale, shift)[0].astype(BF16)

        if nk == 1:
            finish(prod(0))
            return
        acc = refs[-1]

        @pl.when(k == 0)
        def _():
            acc[...] = jnp.zeros_like(acc)

        for p in range(npairs):
            @pl.when((k >= starts[p]) & (k < starts[p] + steps[p]))
            def _(p=p):
                acc[...] += prod(p)

        @pl.when(k == nk - 1)
        def _():
            finish(acc[...])

    res = pl.pallas_call(
        body, name=name, grid=(m // tm, n // tn, nk), in_specs=in_specs,
        out_specs=[out_blk] * (2 + (nres == 5)) if resid else out_blk,
        out_shape=([jax.ShapeDtypeStruct((m, n), out_dtype), jax.ShapeDtypeStruct((m, n), F32)]
                   + [jax.ShapeDtypeStruct((m, n), BF16)] * (nres == 5))
        if resid else jax.ShapeDtypeStruct((m, n), out_dtype),
        scratch_shapes=[] if nk == 1 else [pltpu.VMEM((tm, tn), F32)],
        compiler_params=_params(("parallel", "parallel", "arbitrary")),
    )(*args)
    return res


def _mm(a, b, form, name, **kw):
    return _matmul([(a, b)], form, name, **kw)


def _cols(spec, g):
    return spec[g] if isinstance(spec, list) else spec


def _rowwise_fwd(fn, rows, pars, outs, name, groups=1, ts=ROW_TILE, ties=()):
    s = rows[0][0].shape[0]
    ts = min(ts, s)
    assert s % ts == 0
    nr, npar = len(rows), len(pars)

    def body(*refs):
        par_t = [r[...] for r in refs[nr:nr + npar]]
        out_refs = refs[nr + npar + len(ties):]
        for g in range(groups):
            row_t = []
            for r, (_, spec) in zip(refs[:nr], rows):
                c0, w = _cols(spec, g)
                row_t.append(r[:, c0:c0 + w].astype(F32))
            res = fn(g, *row_t, *par_t)
            for o_ref, val, (_, _, spec) in zip(out_refs, res, outs):
                c0, w = _cols(spec, g)
                o_ref[:, c0:c0 + w] = val.astype(o_ref.dtype)

    return pl.pallas_call(
        body, name=name, grid=(s // ts,),
        in_specs=[pl.BlockSpec((ts, a.shape[1]), lambda i: (i, 0)) for a, _ in rows]
        + [pl.BlockSpec(p.shape, lambda i: (0, 0)) for p in pars] + [_TIE_SPEC1] * len(ties),
        out_specs=[pl.BlockSpec((ts, w), lambda i: (i, 0)) for w, _, _ in outs],
        out_shape=[jax.ShapeDtypeStruct((s, w), dt) for w, dt, _ in outs],
        compiler_params=_params(("parallel",)),
    )(*[a for a, _ in rows], *pars, *ties)


def _rowwise_bwd(fn, rows, pars, outs, douts, gmap, gshapes, name, groups=1, add=None, par_grads=True,
                 ts=ROW_TILE):
    s = rows[0][0].shape[0]
    ts = min(ts, s)
    assert s % ts == 0
    nr, npar, nout, ng = len(rows), len(pars), len(outs), len(gshapes)
    add = add or {}
    add_keys = sorted(add)

    def body(*refs):
        row_refs = refs[:nr]
        par_refs = refs[nr:nr + npar]
        dout_refs = refs[nr + npar:nr + npar + nout]
        add_refs = refs[nr + npar + nout:nr + npar + nout + len(add_keys)]
        g_refs = refs[nr + npar + nout + len(add_keys):][:ng]
        pg_refs = refs[nr + npar + nout + len(add_keys) + ng:]
        par_t = [r[...] for r in par_refs]
        par_acc = [None] * npar
        shared_acc = {}
        for g in range(groups):
            row_t = []
            for r, (_, spec) in zip(row_refs, rows):
                c0, w = _cols(spec, g)
                row_t.append(r[:, c0:c0 + w].astype(F32))
            cts = []
            for r, (_, _, spec) in zip(dout_refs, outs):
                c0, w = _cols(spec, g)
                cts.append(r[:, c0:c0 + w].astype(F32))
            _, vjp = jax.vjp(lambda *t, g=g: tuple(fn(g, *t)), *row_t, *par_t)
            grads = vjp(tuple(cts))
            for k in range(nr):
                if gmap[k] is None:
                    continue
                gi, spec = gmap[k]
                if isinstance(spec, list) or groups == 1:
                    c0, w = _cols(spec, g)
                    val = grads[k]
                    if gi in add:
                        val = val + add_refs[add_keys.index(gi)][:, c0:c0 + w].astype(F32)
                    g_refs[gi][:, c0:c0 + w] = val.astype(g_refs[gi].dtype)
                else:
                    shared_acc[k] = grads[k] if k not in shared_acc else shared_acc[k] + grads[k]
            if par_grads:
                for k in range(npar):
                    pg = grads[nr + k]
                    par_acc[k] = pg if par_acc[k] is None else par_acc[k] + pg
        for k, val in shared_acc.items():
            gi, (c0, w) = gmap[k]
            assert gi not in add
            g_refs[gi][:, c0:c0 + w] = val.astype(g_refs[gi].dtype)
        if par_grads:
            first = pl.program_id(0) == 0
            for k in range(npar):
                @pl.when(first)
                def _(k=k):
                    pg_refs[k][...] = par_acc[k]

                @pl.when(jnp.logical_not(first))
                def _(k=k):
                    pg_refs[k][...] += par_acc[k]

    out_specs = [pl.BlockSpec((ts, w), lambda i: (i, 0)) for w, _ in gshapes]
    out_shape = [jax.ShapeDtypeStruct((s, w), dt) for w, dt in gshapes]
    if par_grads:
        out_specs += [pl.BlockSpec(p.shape, lambda i: (0, 0)) for p in pars]
        out_shape += [jax.ShapeDtypeStruct(p.shape, F32) for p in pars]
    return pl.pallas_call(
        body, name=name, grid=(s // ts,),
        in_specs=[pl.BlockSpec((ts, a.shape[1]), lambda i: (i, 0)) for a, _ in rows]
        + [pl.BlockSpec(p.shape, lambda i: (0, 0)) for p in pars]
        + [pl.BlockSpec((ts, a.shape[1]), lambda i: (i, 0)) for a in douts]
        + [pl.BlockSpec((ts, add[k].shape[1]), lambda i: (i, 0)) for k in add_keys],
        out_specs=out_specs, out_shape=out_shape,
        compiler_params=_params(("arbitrary",)),
    )(*[a for a, _ in rows], *pars, *douts, *[add[k] for k in add_keys])


def _sigmoid(x):
    return 1.0 / (1.0 + jnp.exp(-x))


def _silu(x):
    return x * _sigmoid(x)


def _softplus(x):
    return jnp.maximum(x, 0.0) + jnp.log(1.0 + jnp.exp(-jnp.abs(x)))


def _rms(t, g, n=None):
    n = n or t.shape[-1]
    return t * lax.rsqrt(jnp.sum(t * t, axis=-1, keepdims=True) / n + EPS) * g


def _modulate_fn(g, x, gain, scale, shift):
    return (_rms(x, gain) * (1.0 + scale) + shift,)


def _gate_only_fn(g, y, gm):
    return (gm * y,)


def _gdn_gates_fn(g, b_logit, a_logit, a_log, dt_bias):
    gate = -jnp.exp(a_log) * _softplus(a_logit + dt_bias)
    n = gate.shape[0]
    i = lax.broadcasted_iota(jnp.int32, (n, n), 0)
    j = lax.broadcasted_iota(jnp.int32, (n, n), 1)
    tri = (((i // CHUNK) == (j // CHUNK)) & (i >= j)).astype(F32)
    gcum = lax.dot_general(tri, gate, _NN, preferred_element_type=F32, precision=lax.Precision.HIGHEST)
    return _sigmoid(b_logit), gcum


def _gdn_outnorm_fn(g, o, z, gain):
    return (_rms(o, gain) * _silu(z),)


def _rms_fn(g, t, gain):
    return (_rms(t, gain),)


@jax.custom_vjp
def _swap_halves(t):
    return pltpu.roll(t, 32, 1)


_swap_halves.defvjp(lambda t: (pltpu.roll(t, 32, 1), None), lambda _, ct: (pltpu.roll(ct, 96, 1),))


def _head_norm_rope_fn(g, nope, rope, cosf, sins, gain_n, gain_r):
    first = lax.broadcasted_iota(jnp.int32, rope.shape, 1) < ROPE
    ss = jnp.sum(nope * nope, axis=-1, keepdims=True) + jnp.sum(jnp.where(first, rope * rope, 0.0), axis=-1,
                                                                 keepdims=True)
    r = lax.rsqrt(ss / QK_HEAD + EPS)
    tn = nope * r * gain_n
    tr = rope * r * gain_r
    rot = jnp.where(first, tr * cosf + _swap_halves(tr) * sins, 0.0)
    return tn, rot


def _q_norm_rope_fn(g, nope, rope, cosf, sins, gain_n, gain_r):
    tn, rot = _head_norm_rope_fn(g, nope, rope, cosf, sins, gain_n, gain_r)
    return (jnp.concatenate([tn, rot], axis=1),)


def _k_norm_rope_fn(g, nope, val, rope, cosf, sins, gain_n, gain_r):
    tn, rot = _head_norm_rope_fn(g, nope, rope, cosf, sins, gain_n, gain_r)
    return jnp.concatenate([tn, rot], axis=1), val


FF_SH = 2 * D_FF // N_DEV
FF_G = N_DEV // 2


def _ffn_in(h, w_in, name, tm=1024, ties=()):
    s = h.shape[0]
    tm = min(tm, s)

    def body(h_ref, wg_ref, wu_ref, *rest):
        g_ref, u_ref, a_ref = rest[-3:]
        hb = h_ref[...]
        gate = jnp.dot(hb, wg_ref[...], preferred_element_type=F32)
        up = jnp.dot(hb, wu_ref[...], preferred_element_type=F32)
        g_ref[...] = gate.astype(BF16)
        u_ref[...] = up.astype(BF16)
        a_ref[...] = (_silu(gate) * up).astype(BF16)

    spec = pl.BlockSpec((None, tm, FF_SH), lambda j, i: (j, i, 0))
    return pl.pallas_call(
        body, name=name, grid=(FF_G, s // tm),
        in_specs=[pl.BlockSpec((tm, D), lambda j, i: (i, 0)), pl.BlockSpec((None, D, FF_SH), lambda j, i: (j, 0, 0)),
                  pl.BlockSpec((None, D, FF_SH), lambda j, i: (j + FF_G, 0, 0))] + [_TIE_SPEC2] * len(ties),
        out_specs=[spec, spec, spec], out_shape=[jax.ShapeDtypeStruct((FF_G, s, FF_SH), BF16)] * 3,
        compiler_params=_params(("parallel", "parallel")),
    )(h, w_in, w_in, *ties)


def _ffn_out(act, wo, resid, name, tm=512):
    s = act.shape[1]
    tm = min(tm, s)
    nres = len(resid)

    def body(a_ref, b_ref, x_ref, gate_ref, *rest):
        mods, outs = rest[:nres - 2], rest[nres - 2:]
        y = jnp.dot(a_ref[0], b_ref[0:FF_SH, :], preferred_element_type=F32)
        for k in range(1, FF_G):
            y = y + jnp.dot(a_ref[k], b_ref[k * FF_SH:(k + 1) * FF_SH, :], preferred_element_type=F32)
        xn = x_ref[...] + gate_ref[...] * y
        outs[0][...] = y.astype(BF16)
        outs[1][...] = xn
        if mods:
            outs[2][...] = _modulate_fn(0, xn, *[m[...] for m in mods])[0].astype(BF16)

    blk = pl.BlockSpec((tm, D), lambda i: (i, 0))
    vec = pl.BlockSpec((1, D), lambda i: (0, 0))
    return pl.pallas_call(
        body, name=name, grid=(s // tm,),
        in_specs=[pl.BlockSpec((FF_G, tm, FF_SH), lambda i: (0, i, 0)), pl.BlockSpec((D_FF, D), lambda i: (0, 0)),
                  blk] + [vec] * (nres - 1),
        out_specs=[blk] * (2 + (nres == 5)),
        out_shape=[jax.ShapeDtypeStruct((s, D), BF16), jax.ShapeDtypeStruct((s, D), F32)]
        + [jax.ShapeDtypeStruct((s, D), BF16)] * (nres == 5),
        compiler_params=_params(("parallel",)),
    )(act, wo, *resid)


def _ffn_bwd_act(dy, wo, gate, up, name, tm=1024, ties=()):
    s = dy.shape[0]
    tm = min(tm, s)

    def body(dy_ref, wo_ref, g_ref, u_ref, *rest):
        dg_ref, du_ref = rest[-2:]
        dact = lax.dot_general(dy_ref[...], wo_ref[...], _NT, preferred_element_type=F32)
        gate = g_ref[...].astype(F32)
        up = u_ref[...].astype(F32)
        sg = _sigmoid(gate)
        dg_ref[...] = (dact * up * (sg * (1.0 + gate * (1.0 - sg)))).astype(BF16)
        du_ref[...] = (dact * (gate * sg)).astype(BF16)

    spec = pl.BlockSpec((None, tm, FF_SH), lambda j, i: (j, i, 0))
    return pl.pallas_call(
        body, name=name, grid=(FF_G, s // tm),
        in_specs=[pl.BlockSpec((tm, D), lambda j, i: (i, 0)), pl.BlockSpec((FF_SH, D), lambda j, i: (j, 0)), spec, spec]
        + [_TIE_SPEC2] * len(ties),
        out_specs=[spec, spec], out_shape=[jax.ShapeDtypeStruct((FF_G, s, FF_SH), BF16)] * 2,
        compiler_params=_params(("parallel", "parallel")),
    )(dy, wo, gate, up, *ties)


def _ffn_dwo(act, dy, name, tk=2048, ties=()):
    s = act.shape[1]
    tk = min(tk, s)

    def body(a_ref, b_ref, *rest):
        o_ref, acc = rest[-2:]
        k = pl.program_id(1)

        @pl.when(k == 0)
        def _():
            acc[...] = jnp.zeros_like(acc)

        acc[...] += lax.dot_general(a_ref[...], b_ref[...], _TN, preferred_element_type=F32)

        @pl.when(k == s // tk - 1)
        def _():
            o_ref[...] = acc[...].astype(BF16)

    return pl.pallas_call(
        body, name=name, grid=(FF_G, s // tk),
        in_specs=[pl.BlockSpec((None, tk, FF_SH), lambda j, k: (j, k, 0)), pl.BlockSpec((tk, D), lambda j, k: (k, 0))]
        + [_TIE_SPEC2] * len(ties),
        out_specs=pl.BlockSpec((FF_SH, D), lambda j, k: (j, 0)), out_shape=jax.ShapeDtypeStruct((D_FF, D), BF16),
        scratch_shapes=[pltpu.VMEM((FF_SH, D), F32)], compiler_params=_params(("parallel", "arbitrary")),
    )(act, dy, *ties)


def _ffn_halves(k, gate_ref, up_ref, fn):
    pl.when(k < FF_G)(functools.partial(fn, gate_ref))
    pl.when(k >= FF_G)(functools.partial(fn, up_ref))


def _ffn_dh(dgate, dup, w_in, name, tm=512):
    s = dgate.shape[1]
    tm = min(tm, s)

    def body(dg_ref, du_ref, w_ref, o_ref):
        acc = lax.dot_general(dg_ref[0], w_ref[0], _NT, preferred_element_type=F32)
        for k in range(1, N_DEV):
            d_ref = dg_ref if k < FF_G else du_ref
            acc = acc + lax.dot_general(d_ref[k % FF_G], w_ref[k], _NT, preferred_element_type=F32)
        o_ref[...] = acc

    half = pl.BlockSpec((FF_G, tm, FF_SH), lambda i: (0, i, 0))
    return pl.pallas_call(
        body, name=name, grid=(s // tm,),
        in_specs=[half, half, pl.BlockSpec((N_DEV, D, FF_SH), lambda i: (0, 0, 0))],
        out_specs=pl.BlockSpec((tm, D), lambda i: (i, 0)), out_shape=jax.ShapeDtypeStruct((s, D), F32),
        compiler_params=_params(("parallel",)),
    )(dgate, dup, w_in)


def _ffn_dwi(h, dgate, dup, name, tk=2048):
    s = h.shape[0]
    tk = min(tk, s)

    def body(h_ref, dg_ref, du_ref, o_ref, acc):
        j, k = pl.program_id(0), pl.program_id(1)

        @pl.when(k == 0)
        def _():
            acc[...] = jnp.zeros_like(acc)

        def add(d_ref):
            acc[...] += lax.dot_general(h_ref[...], d_ref[...], _TN, preferred_element_type=F32)

        _ffn_halves(j, dg_ref, du_ref, add)

        @pl.when(k == s // tk - 1)
        def _():
            o_ref[...] = acc[...].astype(BF16)

    return pl.pallas_call(
        body, name=name, grid=(N_DEV, s // tk),
        in_specs=[pl.BlockSpec((tk, D), lambda j, k: (k, 0)),
                  pl.BlockSpec((None, tk, FF_SH), lambda j, k: (jnp.minimum(j, FF_G - 1), jnp.where(j < FF_G, k, s // tk - 1), 0)),
                  pl.BlockSpec((None, tk, FF_SH), lambda j, k: (jnp.maximum(j - FF_G, 0), jnp.where(j < FF_G, 0, k), 0))],
        out_specs=pl.BlockSpec((None, D, FF_SH), lambda j, k: (j, 0, 0)),
        out_shape=jax.ShapeDtypeStruct((N_DEV, D, FF_SH), BF16),
        scratch_shapes=[pltpu.VMEM((D, FF_SH), F32)], compiler_params=_params(("parallel", "arbitrary")),
    )(h, dgate, dup)


def _shift_down(x, d):
    rows = lax.broadcasted_iota(jnp.int32, x.shape, 0)
    return jnp.where(rows >= d, pltpu.roll(x, d, 0), 0.0)


def _shift_up(x, d):
    n = x.shape[0]
    rows = lax.broadcasted_iota(jnp.int32, x.shape, 0)
    return jnp.where(rows < n - d, pltpu.roll(x, n - d, 0), 0.0)


def _conv_post(pre, is_qk):
    a = _silu(pre)
    l2 = a * lax.rsqrt(jnp.sum(a * a, axis=-1, keepdims=True) + EPS)
    return jnp.where(is_qk, l2, a)


def _conv_pre(x, w):
    pre = x * w[CONV_K - 1:CONV_K, :]
    for j in range(CONV_K - 1):
        pre = pre + _shift_down(x, CONV_K - 1 - j) * w[j:j + 1, :]
    return pre


def _gdn_conv_fwd(pm, conv_w, name):
    s = pm.shape[0]
    nblk = 3 * D // HEAD

    def body(x_ref, w_ref, o_ref):
        is_qk = pl.program_id(0) < 2 * HEADS
        o_ref[...] = _conv_post(_conv_pre(x_ref[...], w_ref[...]), is_qk)

    return pl.pallas_call(
        body, name=name, grid=(nblk,),
        in_specs=[pl.BlockSpec((s, HEAD), lambda c: (0, c)), pl.BlockSpec((CONV_K, HEAD), lambda c: (0, c))],
        out_specs=pl.BlockSpec((s, HEAD), lambda c: (0, c)),
        out_shape=jax.ShapeDtypeStruct((s, 3 * D), F32), compiler_params=_params(("parallel",)),
    )(pm, conv_w)


def _gdn_conv_bwd(pm, conv_w, dout, part, name):
    s = pm.shape[0]
    off = part * HEADS

    def body(x_ref, w_ref, d_ref, dx_ref, dw_ref):
        x, w = x_ref[...], w_ref[...]
        _, vjp = jax.vjp(lambda p: _conv_post(p, part < 2), _conv_pre(x, w))
        dpre, = vjp(d_ref[...])
        dx = dpre * w[CONV_K - 1:CONV_K, :]
        rows = [None] * CONV_K
        rows[CONV_K - 1] = jnp.sum(dpre * x, axis=0, keepdims=True)
        for j in range(CONV_K - 1):
            dx = dx + _shift_up(dpre, CONV_K - 1 - j) * w[j:j + 1, :]
            rows[j] = jnp.sum(dpre * _shift_down(x, CONV_K - 1 - j), axis=0, keepdims=True)
        dx_ref[...] = dx.astype(BF16)
        dw_ref[...] = jnp.concatenate(rows, axis=0)

    return pl.pallas_call(
        body, name=name, grid=(HEADS,),
        in_specs=[pl.BlockSpec((s, HEAD), lambda c: (0, c + off)), pl.BlockSpec((CONV_K, HEAD), lambda c: (0, c + off)),
                  pl.BlockSpec((s, HEAD), lambda c: (0, c))],
        out_specs=[pl.BlockSpec((s, HEAD), lambda c: (0, c)), pl.BlockSpec((CONV_K, HEAD), lambda c: (0, c))],
        out_shape=[jax.ShapeDtypeStruct((s, D), BF16), jax.ShapeDtypeStruct((CONV_K, D), F32)],
        compiler_params=_params(("parallel",)),
    )(pm, conv_w, dout)


def _dot3(a, b, dims=_NN):
    ah, bh = a.astype(BF16), b.astype(BF16)
    al, bl = (a - ah.astype(F32)).astype(BF16), (b - bh.astype(F32)).astype(BF16)
    d = lambda u, v: lax.dot_general(u, v, dims, preferred_element_type=F32)
    return d(ah, bh) + (d(ah, bl) + d(al, bh))


def _make_dot(hi):
    def raw(a, b, dims):
        if hi:
            return _dot3(a, b, dims)
        return lax.dot_general(a.astype(BF16), b.astype(BF16), dims, preferred_element_type=F32)

    @functools.partial(jax.custom_vjp, nondiff_argnums=(2,))
    def dot(a, b, form):
        return raw(a, b, _DIMS[form])

    def fwd(a, b, form):
        return raw(a, b, _DIMS[form]), (a, b)

    def bwd(form, res, ct):
        a, b = res
        if form == "nn":
            return raw(ct, b, _NT), raw(a, ct, _TN)
        if form == "nt":
            return raw(ct, b, _NN), raw(ct, a, _TN)
        return raw(b, ct, _NT), raw(a, ct, _NN)

    dot.defvjp(fwd, bwd)
    return dot


_dot = _make_dot(False)
_dot_hi = _make_dot(True)


def _tri_inv_raw(low):
    n = low.shape[0]
    i = lax.broadcasted_iota(jnp.int32, (n, n), 0)
    j = lax.broadcasted_iota(jnp.int32, (n, n), 1)
    eye = (i == j).astype(F32)
    hdot = _dot3
    same16 = (i // 16) == (j // 16)
    neg = jnp.where(same16, -low, 0.0)
    inv = eye + neg
    power = neg
    for _ in range(3):
        power = hdot(power, power)
        inv = hdot(inv, eye + power)
    for blk in (32, 64):
        off = jnp.where(((i // blk) == (j // blk)) & ((i // (blk // 2)) != (j // (blk // 2))), low, 0.0)
        inv = inv - hdot(inv, hdot(off, inv))
    return inv


@jax.custom_vjp
def _tri_inv(low):
    return _tri_inv_raw(low)


def _tri_inv_fwd(low):
    inv = _tri_inv_raw(low)
    return inv, inv


def _tri_inv_bwd(inv, ct):
    return (-_dot3(_dot3(inv, ct, _TN), inv, _NT),)


_tri_inv.defvjp(_tri_inv_fwd, _tri_inv_bwd)


@jax.custom_vjp
def _tri_inv_given(low, inv):
    return inv


_tri_inv_given.defvjp(lambda low, inv: (inv, inv),
                      lambda inv, ct: (_tri_inv_bwd(inv, ct)[0], jnp.zeros_like(inv)))

GROUP = 4
N_GROUPS = HEADS // GROUP
GROWS = GROUP * CHUNK


def _gdn_group(q, k, v, beta, gc, gr, states, inv=None):
    n = q.shape[0]
    i = lax.broadcasted_iota(jnp.int32, (n, n), 0)
    j = lax.broadcasted_iota(jnp.int32, (n, n), 1)
    same = (i // CHUNK) == (j // CHUNK)
    incl, strict = same & (i >= j), same & (i > j)
    qs = q * (HEAD ** -0.5)
    decay = jnp.where(incl, jnp.exp(jnp.where(incl, gc - gr, 0.0)), 0.0)
    kb = k * beta
    eg = jnp.exp(gc)
    prod = _dot(jnp.concatenate([kb, qs], axis=0), k, "nt")
    low = jnp.where(strict, prod[:n] * decay, 0.0)
    attn = jnp.where(incl, prod[n:] * decay, 0.0)
    inv = _tri_inv(low) if inv is None else _tri_inv_given(low, inv)
    sol = _dot_hi(inv, jnp.concatenate([v * beta, kb * eg], axis=1), "nn")
    u, w, qg = sol[:, :HEAD], sol[:, HEAD:], qs * eg
    last = lax.broadcasted_iota(jnp.int32, (CHUNK, 1), 0) == CHUNK - 1
    v_new, o_state, carry = [], [], []
    for h, state in enumerate(states):
        rows = slice(h * CHUNK, (h + 1) * CHUNK)
        ws = _dot(jnp.concatenate([w[rows], qg[rows]], axis=0), state, "nn")
        v_new.append(u[rows] - ws[:CHUNK])
        o_state.append(ws[CHUNK:])
        g_last = jnp.sum(jnp.where(last, gc[rows], 0.0), axis=0, keepdims=True)
        carry.append((g_last, k[rows] * jnp.exp(g_last - gc[rows])))
    o = jnp.concatenate(o_state, axis=0) + _dot(attn, jnp.concatenate(v_new, axis=0), "nn")
    new = tuple(state * jnp.exp(g_last) + _dot(k_dec, vn, "tn")
                for state, (g_last, k_dec), vn in zip(states, carry, v_new))
    return o, new, inv


def _gdn_specs(s, rev):
    nc = s // CHUNK
    at = (lambda n: nc - 1 - n) if rev else (lambda n: n)
    return nc, at, [
        pl.BlockSpec((CHUNK, D), lambda n: (at(n), 0)), pl.BlockSpec((CHUNK, D), lambda n: (at(n), 1)),
        pl.BlockSpec((CHUNK, D), lambda n: (at(n), 2)), pl.BlockSpec((CHUNK, HEAD), lambda n: (at(n), 0)),
        pl.BlockSpec((CHUNK, HEAD), lambda n: (at(n), 0)),
        pl.BlockSpec((None, N_GROUPS, 1, GROWS), lambda n: (at(n), 0, 0, 0))]


def _group_operands(grp, q_ref, k_ref, v_ref, b_blk, gc_blk, gr_blk):
    heads = range(grp * GROUP, (grp + 1) * GROUP)
    stack = lambda ref: jnp.concatenate([ref[:, h * HEAD:(h + 1) * HEAD] for h in heads], axis=0)
    col = lambda blk: jnp.concatenate([blk[:, h:h + 1] for h in heads], axis=0)
    return stack(q_ref), stack(k_ref), stack(v_ref), col(b_blk), col(gc_blk), gr_blk[grp]


def _gdn_scan_fwd(qkv, beta, gcum, grow, name):
    s = qkv.shape[0]
    nc, _, in_specs = _gdn_specs(s, rev=False)

    def body(q_ref, k_ref, v_ref, b_ref, gc_ref, gr_ref, o_ref, st_ref, inv_ref, state):
        @pl.when(pl.program_id(0) == 0)
        def _():
            state[...] = jnp.zeros_like(state)

        b_blk, gc_blk, gr_blk = b_ref[...], gc_ref[...], gr_ref[...]
        old = [state[h] for h in range(HEADS)]
        res = [_gdn_group(*_group_operands(grp, q_ref, k_ref, v_ref, b_blk, gc_blk, gr_blk),
                          old[grp * GROUP:(grp + 1) * GROUP]) for grp in range(N_GROUPS)]
        for grp, (o, new, inv) in enumerate(res):
            inv_ref[grp] = inv
            for hh in range(GROUP):
                h = grp * GROUP + hh
                st_ref[h] = old[h]
                o_ref[:, h * HEAD:(h + 1) * HEAD] = o[hh * CHUNK:(hh + 1) * CHUNK]
                state[h] = new[hh]

    return pl.pallas_call(
        body, name=name, grid=(nc,), in_specs=in_specs,
        out_specs=[pl.BlockSpec((CHUNK, D), lambda n: (n, 0)),
                   pl.BlockSpec((None, HEADS, HEAD, HEAD), lambda n: (n, 0, 0, 0)),
                   pl.BlockSpec((None, N_GROUPS, GROWS, GROWS), lambda n: (n, 0, 0, 0))],
        out_shape=[jax.ShapeDtypeStruct((s, D), F32), jax.ShapeDtypeStruct((nc, HEADS, HEAD, HEAD), F32),
                   jax.ShapeDtypeStruct((nc, N_GROUPS, GROWS, GROWS), F32)],
        scratch_shapes=[pltpu.VMEM((HEADS, HEAD, HEAD), F32)],
        compiler_params=_params(("arbitrary",)),
    )(qkv, qkv, qkv, beta, gcum, grow)


def _gdn_scan_bwd(qkv, beta, gcum, grow, states, invs, do, name):
    s = qkv.shape[0]
    nc, at, in_specs = _gdn_specs(s, rev=True)
    in_specs += [pl.BlockSpec((None, HEADS, HEAD, HEAD), lambda n: (at(n), 0, 0, 0)),
                 pl.BlockSpec((None, N_GROUPS, GROWS, GROWS), lambda n: (at(n), 0, 0, 0)),
                 pl.BlockSpec((CHUNK, D), lambda n: (at(n), 0))]

    def body(q_ref, k_ref, v_ref, b_ref, gc_ref, gr_ref, st_ref, inv_ref, do_ref, dq_ref, dk_ref, dv_ref, db_ref,
             dgc_ref, dgr_ref, dstate):
        @pl.when(pl.program_id(0) == 0)
        def _():
            dstate[...] = jnp.zeros_like(dstate)

        b_blk, gc_blk, gr_blk = b_ref[...], gc_ref[...], gr_ref[...]
        dold = [dstate[h] for h in range(HEADS)]
        res = []
        for grp in range(N_GROUPS):
            heads = range(grp * GROUP, (grp + 1) * GROUP)
            inv = inv_ref[grp]
            _, vjp = jax.vjp(lambda q, k, v, b, gc, gr, *st, inv=inv: _gdn_group(q, k, v, b, gc, gr, st, inv)[:2],
                             *_group_operands(grp, q_ref, k_ref, v_ref, b_blk, gc_blk, gr_blk),
                             *[st_ref[h] for h in heads])
            d_out = jnp.concatenate([do_ref[:, h * HEAD:(h + 1) * HEAD] for h in heads], axis=0)
            res.append(vjp((d_out, tuple(dold[h] for h in heads))))
        lane = lax.broadcasted_iota(jnp.int32, (CHUNK, HEAD), 1)
        db_all = jnp.zeros((CHUNK, HEAD), F32)
        dgc_all = jnp.zeros((CHUNK, HEAD), F32)
        for grp, (dq, dk, dv, db, dgc, dgr, *dst) in enumerate(res):
            dgr_ref[grp] = dgr
            for hh in range(GROUP):
                h = grp * GROUP + hh
                cs, rows = slice(h * HEAD, (h + 1) * HEAD), slice(hh * CHUNK, (hh + 1) * CHUNK)
                dq_ref[:, cs] = dq[rows]
                dk_ref[:, cs] = dk[rows]
                dv_ref[:, cs] = dv[rows]
                dstate[h] = dst[hh]
                db_all = jnp.where(lane == h, db[rows], db_all)
                dgc_all = jnp.where(lane == h, dgc[rows], dgc_all)
        db_ref[...] = db_all
        dgc_ref[...] = dgc_all

    blk = pl.BlockSpec((CHUNK, D), lambda n: (at(n), 0))
    gblk = pl.BlockSpec((CHUNK, HEAD), lambda n: (at(n), 0))
    return pl.pallas_call(
        body, name=name, grid=(nc,), in_specs=in_specs,
        out_specs=[blk, blk, blk, gblk, gblk, pl.BlockSpec((None, N_GROUPS, 1, GROWS), lambda n: (at(n), 0, 0, 0))],
        out_shape=[jax.ShapeDtypeStruct((s, D), F32)] * 3 + [jax.ShapeDtypeStruct((s, HEAD), F32)] * 2
        + [jax.ShapeDtypeStruct((nc, N_GROUPS, 1, GROWS), F32)],
        scratch_shapes=[pltpu.VMEM((HEADS, HEAD, HEAD), F32)],
        compiler_params=_params(("arbitrary",)),
    )(qkv, qkv, qkv, beta, gcum, grow, states, invs, do)


ATT_TILE = 512
ATT_SCALE = QK_HEAD ** -0.5


def _att_mask(t):
    qpos = lax.broadcasted_iota(jnp.int32, (t, t), 0)
    kpos = lax.broadcasted_iota(jnp.int32, (t, t), 1)
    return (kpos // CHUNK) <= (qpos // CHUNK)


ATT_STRIP = 32


def _att_strip_mask(r, t):
    kpos = lax.broadcasted_iota(jnp.int32, (ATT_STRIP, t), 1)
    return (kpos // CHUNK) <= (r * ATT_STRIP) // CHUNK


def _att_pairs(nb, by_query):
    if by_query:
        pairs = [(i, j) for i in range(nb) for j in range(i + 1)]
    else:
        pairs = [(j, i) for j in range(nb) for i in range(j, nb)]
    return jnp.array([a for a, _ in pairs], jnp.int32), jnp.array([b for _, b in pairs], jnp.int32)


def _attn_fwd(q, k, v, name):
    s = q.shape[0]
    t = min(ATT_TILE, s)
    nb = s // t
    ii, jj = _att_pairs(nb, by_query=True)

    def body(ii_ref, jj_ref, q_ref, k_ref, v_ref, o_ref, lse_ref, m_s, l_s, acc):
        step = pl.program_id(1)
        i, j = ii_ref[step], jj_ref[step]

        @pl.when(j == 0)
        def _():
            m_s[...] = jnp.full_like(m_s, -jnp.inf)
            l_s[...] = jnp.zeros_like(l_s)
            acc[...] = jnp.zeros_like(acc)

        sc = lax.dot_general(q_ref[...], k_ref[...], _NT, preferred_element_type=F32) * ATT_SCALE
        sc = lax.cond(i == j, lambda u: jnp.where(_att_mask(t), u, -jnp.inf), lambda u: u, sc)
        m_new = jnp.maximum(m_s[...], jnp.max(sc, axis=-1, keepdims=True))
        alpha = jnp.exp(m_s[...] - m_new)
        p = jnp.exp(sc - m_new)
        l_s[...] = alpha * l_s[...] + jnp.sum(p, axis=-1, keepdims=True)
        acc[...] = alpha * acc[...] + jnp.dot(p.astype(BF16), v_ref[...], preferred_element_type=F32)
        m_s[...] = m_new

        @pl.when(j == i)
        def _():
            o_ref[...] = acc[...] / l_s[...]
            lse_ref[...] = m_s[...] + jnp.log(l_s[...])

    grid_spec = pltpu.PrefetchScalarGridSpec(
        num_scalar_prefetch=2, grid=(HEADS, len(ii)),
        in_specs=[pl.BlockSpec((t, HEAD_PAD), lambda h, n, ir, jr: (ir[n], h)),
                  pl.BlockSpec((t, HEAD_PAD), lambda h, n, ir, jr: (jr[n], h)),
                  pl.BlockSpec((t, HEAD), lambda h, n, ir, jr: (jr[n], h))],
        out_specs=[pl.BlockSpec((t, HEAD), lambda h, n, ir, jr: (ir[n], h)),
                   pl.BlockSpec((None, t, 1), lambda h, n, ir, jr: (h, ir[n], 0))],
        scratch_shapes=[pltpu.VMEM((t, 1), F32), pltpu.VMEM((t, 1), F32), pltpu.VMEM((t, HEAD), F32)])
    return pl.pallas_call(
        body, name=name, grid_spec=grid_spec,
        out_shape=[jax.ShapeDtypeStruct((s, HEADS * HEAD), F32), jax.ShapeDtypeStruct((HEADS, s, 1), F32)],
        compiler_params=_params(("parallel", "arbitrary")),
    )(ii, jj, q, k, v)


def _attn_bwd(q, k, v, do, o, lse, name, dkv_sum=None):
    s = q.shape[0]
    t = min(ATT_TILE, s)
    nb = s // t
    jj, ii = _att_pairs(nb, by_query=False)
    nsum = 2 if dkv_sum else 0

    def body(jj_ref, ii_ref, q_ref, k_ref, v_ref, do_ref, o_ref, lse_ref, *rest):
        dq_ref, dk_ref, dv_ref, dk_acc, dv_acc, sc_s, dp_s, p_s, ds_s, dl_s = rest[nsum:]
        step = pl.program_id(1)
        i, j = ii_ref[step], jj_ref[step]

        @pl.when(step == 0)
        def _():
            dq_ref[...] = jnp.zeros_like(dq_ref)

        @pl.when(i == j)
        def _():
            dk_acc[...] = jnp.zeros_like(dk_acc)
            dv_acc[...] = jnp.zeros_like(dv_acc)

        do_f = do_ref[...]
        dob = do_f.astype(BF16)
        dl_s[...] = jnp.sum(do_f * o_ref[...], axis=-1, keepdims=True)
        sc_s[...] = lax.dot_general(q_ref[...], k_ref[...], _NT, preferred_element_type=F32)
        dp_s[...] = lax.dot_general(dob, v_ref[...], _NT, preferred_element_type=F32)

        def softmax_strips(diagonal):
            for r in range(t // ATT_STRIP):
                rows = slice(r * ATT_STRIP, (r + 1) * ATT_STRIP)
                p = jnp.exp(sc_s[rows, :] * ATT_SCALE - lse_ref[rows, :])
                if diagonal:
                    p = jnp.where(_att_strip_mask(r, t), p, 0.0)
                p_s[rows, :] = p.astype(BF16)
                ds_s[rows, :] = (p * (dp_s[rows, :] - dl_s[rows, :]) * ATT_SCALE).astype(BF16)

        pl.when(i == j)(functools.partial(softmax_strips, True))
        pl.when(i != j)(functools.partial(softmax_strips, False))
        ds = ds_s[...]
        dv_acc[...] += lax.dot_general(p_s[...], dob, _TN, preferred_element_type=F32)
        dk_acc[...] += lax.dot_general(ds, q_ref[...], _TN, preferred_element_type=F32)
        rows = pl.ds(pl.multiple_of(i * t, t), t)
        dq_ref[rows, :] += jnp.dot(ds, k_ref[...], preferred_element_type=F32)

        @pl.when(i == nb - 1)
        def _():
            dk_ref[...] = dk_acc[...] + rest[0][...] if nsum else dk_acc[...]
            dv_ref[...] = dv_acc[...] + rest[1][...] if nsum else dv_acc[...]

    dk_blk = pl.BlockSpec((t, HEAD_PAD), lambda h, n, jr, ir: (jr[n], h))
    dv_blk = pl.BlockSpec((t, HEAD), lambda h, n, jr, ir: (jr[n], h))
    grid_spec = pltpu.PrefetchScalarGridSpec(
        num_scalar_prefetch=2, grid=(HEADS, len(jj)),
        in_specs=[pl.BlockSpec((t, HEAD_PAD), lambda h, n, jr, ir: (ir[n], h)),
                  pl.BlockSpec((t, HEAD_PAD), lambda h, n, jr, ir: (jr[n], h)),
                  pl.BlockSpec((t, HEAD), lambda h, n, jr, ir: (jr[n], h)),
                  pl.BlockSpec((t, HEAD), lambda h, n, jr, ir: (ir[n], h)),
                  pl.BlockSpec((t, HEAD), lambda h, n, jr, ir: (ir[n], h)),
                  pl.BlockSpec((None, t, 1), lambda h, n, jr, ir: (h, ir[n], 0))] + [dk_blk, dv_blk][:nsum],
        out_specs=[pl.BlockSpec((s, HEAD_PAD), lambda h, n, jr, ir: (0, h)), dk_blk, dv_blk],
        scratch_shapes=[pltpu.VMEM((t, HEAD_PAD), F32), pltpu.VMEM((t, HEAD), F32), pltpu.VMEM((t, t), F32),
                        pltpu.VMEM((t, t), F32), pltpu.VMEM((t, t), BF16), pltpu.VMEM((t, t), BF16),
                        pltpu.VMEM((t, 1), F32)])
    return pl.pallas_call(
        body, name=name, grid_spec=grid_spec,
        out_shape=[jax.ShapeDtypeStruct((s, HEADS * HEAD_PAD), F32)] * 2 + [jax.ShapeDtypeStruct((s, HEADS * HEAD), F32)],
        compiler_params=_params(("parallel", "arbitrary")),
    )(jj, ii, q, k, v, do, o, lse, *(dkv_sum or ()))


def _rope_tables(positions):
    half = ROPE // 2
    inv_freq = ROPE_BASE ** (-jnp.arange(half, dtype=F32) / half)
    ang = positions.astype(F32)[:, None] * inv_freq
    cos, sin = jnp.cos(ang), jnp.sin(ang)
    return jnp.concatenate([cos] * 4, axis=1), jnp.concatenate([-sin, sin] * 2, axis=1)


def _loss_and_grad(y, target, name):
    s = y.shape[0]
    ts = min(ROW_TILE, s)

    def body(y_ref, t_ref, dy_ref, l_ref):
        e = y_ref[...] - t_ref[...]
        dy_ref[...] = e * (1.0 / D)
        part = jnp.sum(jnp.sum(e * e, axis=-1, keepdims=True) * (0.5 / D), axis=0, keepdims=True)
        part = part * jnp.ones((1, 128), F32)

        @pl.when(pl.program_id(0) == 0)
        def _():
            l_ref[...] = part

        @pl.when(pl.program_id(0) > 0)
        def _():
            l_ref[...] += part

    return pl.pallas_call(
        body, name=name, grid=(s // ts,),
        in_specs=[pl.BlockSpec((ts, D), lambda i: (i, 0))] * 2,
        out_specs=[pl.BlockSpec((ts, D), lambda i: (i, 0)), pl.BlockSpec((1, 128), lambda i: (0, 0))],
        out_shape=[jax.ShapeDtypeStruct((s, D), F32), jax.ShapeDtypeStruct((1, 128), F32)],
        compiler_params=_params(("arbitrary",)),
    )(y, target)


ANY = pl.BlockSpec(memory_space=pl.ANY)


def _all_gather(shard, name):
    def body(x_ref, out_ref, send_sems, recv_sems, local_sem):
        x, y, c = lax.axis_index("x"), lax.axis_index("y"), lax.axis_index("c")
        me, sibling = (x, y, c), (x, y, 1 - c)
        chips = [(1 - x, y), (x, 1 - y), (1 - x, 1 - y)]

        def rows(px, py, pc):
            return out_ref.at[4 * px + 2 * py + pc]

        def copy(k, block, to, src=None):
            return pltpu.make_async_remote_copy(
                src_ref=rows(*block) if src is None else src, dst_ref=rows(*block),
                send_sem=send_sems.at[k], recv_sem=recv_sems.at[k], device_id=to, device_id_type=MESH)

        mine = pltpu.make_async_copy(x_ref, rows(*me), local_sem)
        mine.start()
        first = [copy(0, me, sibling, src=x_ref)]
        first += [copy(1 + j, me, (*chip, c), src=x_ref) for j, chip in enumerate(chips)]
        for cp in first:
            cp.start()
        passed = [copy(4 + j, (*chip, c), sibling) for j, chip in enumerate(chips)]
        for j, chip in enumerate(chips):
            copy(1 + j, (*chip, c), me).wait_recv()
            passed[j].start()
        copy(0, sibling, me).wait_recv()
        for j, chip in enumerate(chips):
            copy(4 + j, (*chip, 1 - c), me).wait_recv()
        for cp in first + passed:
            cp.wait_send()
        mine.wait()

    return pl.pallas_call(
        body, name=name, out_shape=jax.ShapeDtypeStruct((N_DEV,) + shard.shape, shard.dtype),
        in_specs=[ANY], out_specs=ANY,
        scratch_shapes=[pltpu.SemaphoreType.DMA((7,)), pltpu.SemaphoreType.DMA((7,)), pltpu.SemaphoreType.DMA],
    )(shard)


def _exchange(blocks, name):
    def body(x_ref, out_ref, send_sems, recv_sems, local_sem):
        x, y, c = lax.axis_index("x"), lax.axis_index("y"), lax.axis_index("c")
        me = 4 * x + 2 * y + c
        mine = pltpu.make_async_copy(x_ref.at[me], out_ref.at[me], local_sem)
        mine.start()
        copies = []
        for k in range(1, N_DEV):
            px = 1 - x if k & 4 else x
            py = 1 - y if k & 2 else y
            pc = 1 - c if k & 1 else c
            peer = 4 * px + 2 * py + pc
            cp = pltpu.make_async_remote_copy(
                src_ref=x_ref.at[peer], dst_ref=out_ref.at[me], send_sem=send_sems.at[k - 1],
                recv_sem=recv_sems.at[k - 1], device_id=(px, py, pc), device_id_type=MESH)
            cp.start()
            copies.append((cp, pltpu.make_async_remote_copy(
                src_ref=x_ref.at[peer], dst_ref=out_ref.at[peer], send_sem=send_sems.at[k - 1],
                recv_sem=recv_sems.at[k - 1], device_id=(px, py, pc), device_id_type=MESH)))
        for cp, landing in copies:
            landing.wait_recv()
        for cp, landing in copies:
            cp.wait_send()
        mine.wait()

    return pl.pallas_call(
        body, name=name, out_shape=jax.ShapeDtypeStruct(blocks.shape, blocks.dtype),
        in_specs=[ANY], out_specs=ANY,
        scratch_shapes=[pltpu.SemaphoreType.DMA((7,)), pltpu.SemaphoreType.DMA((7,)), pltpu.SemaphoreType.DMA],
    )(blocks)


HBM = pl.BlockSpec(memory_space=pltpu.HBM)
SEM = pl.BlockSpec(memory_space=pltpu.SEMAPHORE)
EFFECT = pltpu.SideEffectType.DATAFLOW_SIDE_EFFECTING


def _peers():
    x, y, c = lax.axis_index("x"), lax.axis_index("y"), lax.axis_index("c")
    peers = []
    for k in range(1, N_DEV):
        px = 1 - x if k & 4 else x
        py = 1 - y if k & 2 else y
        pc = 1 - c if k & 1 else c
        peers.append(((px, py, pc), 4 * px + 2 * py + pc))
    return 4 * x + 2 * y + c, peers


def _send_start(srcs, name, gather):
    n = len(srcs)
    lands = [((N_DEV,) + s.shape) if gather else s.shape for s in srcs]

    def body(*refs):
        src_refs, land_refs = refs[:n], refs[n:2 * n]
        send_sems, recv_sems, token = refs[2 * n], refs[2 * n + 1], refs[-1]
        me, peers = _peers()
        for i in range(n):
            for k, (dev, idx) in enumerate(peers):
                pltpu.make_async_remote_copy(
                    src_ref=src_refs[i] if gather else src_refs[i].at[idx], dst_ref=land_refs[i].at[me],
                    send_sem=send_sems.at[7 * i + k], recv_sem=recv_sems.at[7 * i + k], device_id=dev,
                    device_id_type=MESH).start()
        token[...] = jnp.zeros_like(token)

    res = pl.pallas_call(
        body, name=name,
        out_shape=(pltpu.SemaphoreType.DMA((7 * n,)), pltpu.SemaphoreType.DMA((7 * n,)),
                   *[pltpu.HBM(s.shape, s.dtype) for s in srcs],
                   *[pltpu.HBM(shape, s.dtype) for shape, s in zip(lands, srcs)],
                   jax.ShapeDtypeStruct((8, 128), F32)),
        in_specs=(HBM,) * (2 * n), out_specs=(SEM, SEM) + (HBM,) * (2 * n) + (pl.BlockSpec(memory_space=pltpu.VMEM),),
        input_output_aliases={i: 2 + i for i in range(2 * n)},
        compiler_params=pltpu.CompilerParams(has_side_effects=EFFECT),
    )(*[pltpu.with_memory_space_constraint(s, pltpu.HBM) for s in srcs],
      *[pltpu.with_memory_space_constraint(lax.empty(shape, s.dtype), pltpu.HBM) for shape, s in zip(lands, srcs)])
    return dict(sems=res[:2], srcs=res[2:2 + n], lands=res[2 + n:2 + 2 * n], token=res[-1])


def _send_wait(handle, after, name, gather):
    n = len(handle["srcs"])

    def body(*refs):
        src_refs, land_refs = refs[:n], refs[n:2 * n]
        send_sems, recv_sems = refs[2 * n], refs[2 * n + 1]
        me, peers = _peers()
        for i in range(n):
            for k, (dev, idx) in enumerate(peers):
                cp = pltpu.make_async_remote_copy(
                    src_ref=src_refs[i] if gather else src_refs[i].at[idx], dst_ref=land_refs[i].at[idx],
                    send_sem=send_sems.at[7 * i + k], recv_sem=recv_sems.at[7 * i + k], device_id=dev,
                    device_id_type=MESH)
                cp.wait_send()
                cp.wait_recv()

    both = list(handle["srcs"]) + list(handle["lands"])
    res = pl.pallas_call(
        body, name=name, out_shape=tuple(pltpu.HBM(t.shape, t.dtype) for t in both),
        in_specs=(HBM,) * (2 * n) + (SEM, SEM, pl.BlockSpec(memory_space=pl.ANY)), out_specs=(HBM,) * (2 * n),
        input_output_aliases={i: i for i in range(2 * n)},
        compiler_params=pltpu.CompilerParams(has_side_effects=EFFECT),
    )(*both, *handle["sems"], after)
    return res[:n], res[n:]


def _adamw(parts, w, m, v, name, tr=128):
    pieces = len(parts)
    n, r, wd = parts[0].shape
    tr = next((t for t in (tr, 64, 32, 16) if r % t == 0), r)
    nrt = r // tr

    def body(*refs):
        w_ref, m_ref, v_ref, g_ref, d_ref, nm_ref, nv_ref = refs[pieces:]

        def update(p_ref):
            g = p_ref[0].astype(F32)
            for k in range(1, n):
                g = g + p_ref[k].astype(F32)
            m_new = B1 * m_ref[...] + (1.0 - B1) * g
            v_new = B2 * v_ref[...] + (1.0 - B2) * (g * g)
            m_hat = m_new / (1.0 - B1 ** STEP)
            v_hat = v_new / (1.0 - B2 ** STEP)
            g_ref[...] = g
            d_ref[...] = -LR * (m_hat / (jnp.sqrt(v_hat) + ADAM_EPS) + WD * w_ref[...])
            nm_ref[...] = m_new
            nv_ref[...] = v_new

        for p in range(pieces):
            pl.when(pl.program_id(0) == p)(functools.partial(update, refs[p]))

    part_spec = lambda p: pl.BlockSpec((n, tr, wd), lambda l, i: (0, jnp.clip(i + (l - p) * nrt, 0, nrt - 1), 0))
    blk = pl.BlockSpec((tr, wd), lambda l, i: (l * nrt + i, 0))
    return pl.pallas_call(
        body, name=name, grid=(pieces, nrt),
        in_specs=[part_spec(p) for p in range(pieces)] + [blk, blk, blk],
        out_specs=[blk] * 4, out_shape=[jax.ShapeDtypeStruct((pieces * r, wd), F32)] * 4,
        compiler_params=_params(("arbitrary", "arbitrary")),
    )(*parts, w, m, v)


def _outer8(ct, dm, name):
    k, n = ct.shape[0], dm.shape[1]

    def body(c_ref, d_ref, o_ref):
        cv, dv = c_ref[...], d_ref[...]
        acc = cv[:, 0:1] * dv[0:1, :]
        for s in range(1, N_DEV):
            acc = acc + cv[:, s:s + 1] * dv[s:s + 1, :]
        o_ref[...] = acc

    tk = 256
    return pl.pallas_call(
        body, name=name, grid=(k // tk,),
        in_specs=[pl.BlockSpec((tk, N_DEV), lambda i: (i, 0)), pl.BlockSpec((N_DEV, n), lambda i: (0, 0))],
        out_specs=pl.BlockSpec((tk, n), lambda i: (i, 0)), out_shape=jax.ShapeDtypeStruct((k, n), F32),
        compiler_params=_params(("parallel",)),
    )(ct, dm)


FULL = (0, D)
C128 = (0, 128)
HEAD_NOPE = [(h * HEAD_PAD, NOPE) for h in range(HEADS)]
HEAD_ROPE = [(h * HEAD_PAD + NOPE, 128) for h in range(HEADS)]
HEAD_ALL = [(h * HEAD_PAD, HEAD_PAD) for h in range(HEADS)]
HEAD_V = [(h * HEAD, HEAD) for h in range(HEADS)]


def _modulate(x, p, ties=()):
    return _rowwise_fwd(_modulate_fn, [(x, FULL)], [p["gain"], p["scale"], p["shift"]], [(D, BF16, FULL)], "modulate",
                        ties=ties)[0]


def _residual_bwd(y, gm, dxn):
    return _rowwise_bwd(_gate_only_fn, [(y, FULL)], [gm], [(D, F32, FULL)], [dxn], [(0, FULL)], [(D, BF16)],
                        "residual_bwd")


def _modulate_bwd(x, p, dh, dx_in, prev=None):
    pars = [p["gain"], p["scale"], p["shift"]]
    if prev is None:
        return list(_rowwise_bwd(_modulate_fn, [(x, FULL)], pars, [(D, BF16, FULL)], [dh], [(0, FULL)], [(D, F32)],
                                 "modulate_bwd", add={0: dx_in})) + [None]
    s = x.shape[0]
    ts = min(2 * ROW_TILE, s)

    def body(x_ref, g_ref, sc_ref, sh_ref, dh_ref, din_ref, y_ref, gm_ref, dx_ref, dy_ref, dg_ref, dsc_ref, dsh_ref,
             dgm_ref):
        _, vjp = jax.vjp(lambda *t: _modulate_fn(0, *t)[0], x_ref[...], g_ref[...], sc_ref[...], sh_ref[...])
        dxm, dg, dsc, dsh = vjp(dh_ref[...])
        dx = dxm + din_ref[...]
        dx_ref[...] = dx
        dy_ref[...] = (gm_ref[...] * dx).astype(BF16)
        sums = (dg, dsc, dsh, jnp.sum(dx * y_ref[...], axis=0, keepdims=True))
        first = pl.program_id(0) == 0
        for ref, val in zip((dg_ref, dsc_ref, dsh_ref, dgm_ref), sums):
            @pl.when(first)
            def _(ref=ref, val=val):
                ref[...] = val

            @pl.when(jnp.logical_not(first))
            def _(ref=ref, val=val):
                ref[...] += val

    blk = pl.BlockSpec((ts, D), lambda i: (i, 0))
    vec = pl.BlockSpec((1, D), lambda i: (0, 0))
    dx, dy, dg, dsc, dsh, dgm = pl.pallas_call(
        body, name="modulate_bwd_chain", grid=(s // ts,),
        in_specs=[blk, vec, vec, vec, blk, blk, blk, vec], out_specs=[blk, blk, vec, vec, vec, vec],
        out_shape=[jax.ShapeDtypeStruct((s, D), F32), jax.ShapeDtypeStruct((s, D), BF16)]
        + [jax.ShapeDtypeStruct((1, D), F32)] * 4,
        compiler_params=_params(("arbitrary",)),
    )(x, *pars, dh, dx_in, prev[0], prev[1])
    return [dx, dg, dsc, dsh, (dy, dgm)]


def _out_proj(a, w, x, p, nxt, name, **kw):
    res = _matmul([(a, w)], "nn", name, out_dtype=BF16, resid=(x, p["gm"]) + tuple(nxt or ()), **kw)
    return res[1], res[0], (res[2] if nxt else None)


def _ffn_fwd(x, p, ties=(), h=None, nxt=None):
    if h is None:
        h, ties = _modulate(x, p, ties), ()
    gate, up, act = _ffn_in(h, p["w_in"], "ffn_in", ties=ties)
    res = _ffn_out(act, p["wo"], (x, p["gm"]) + tuple(nxt or ()), "ffn_out")
    return res[1], dict(x=x, h=h, gate=gate, up=up, act=act, y=res[0]), (res[2] if nxt else None)


def _ffn_bwd(t, p, dxn, res=None, prev=None, ties=()):
    dy, dgm = res or _residual_bwd(t["y"], p["gm"], dxn)
    dgate, dup = _ffn_bwd_act(dy, p["wo"], t["gate"], t["up"], "ffn_bwd_act", ties=ties)
    dwo = _ffn_dwo(t["act"], dy, "ffn_dwo", ties=ties)
    dh = _ffn_dh(dgate, dup, p["w_in"], "ffn_dh")
    dwi = _ffn_dwi(t["h"], dgate, dup, "ffn_dwi")
    dx, dgain, dscale, dshift, res_prev = _modulate_bwd(t["x"], p, dh, dxn, prev)
    return dx, dict(gain=dgain, scale=dscale, shift=dshift, gm=dgm, w_in=dwi, wo=dwo), res_prev


def _pad128(t):
    return jnp.pad(t, ((0, 0), (0, 128 - t.shape[1])))


def _gdn_fwd(x, p, ties=(), h=None, nxt=None):
    s = x.shape[0]
    if h is None:
        h, ties = _modulate(x, p, ties), ()
    pm = _mm(h, p["w_main"], "nn", "gdn_proj", ties=ties)
    tail = _mm(h, p["w_tail"], "nn", "gdn_proj_tail", ties=ties)
    qkv = _gdn_conv_fwd(pm, p["conv_w"], "gdn_conv")
    beta, gcum = _rowwise_fwd(_gdn_gates_fn, [(tail, C128), (tail, (128, 128))], [p["a_log"], p["dt_bias"]],
                              [(128, F32, C128)] * 2, "gdn_gates")
    grow = gcum[:, :HEADS].reshape(s // CHUNK, CHUNK, N_GROUPS, GROUP).transpose(0, 2, 3, 1)
    grow = grow.reshape(s // CHUNK, N_GROUPS, 1, GROWS)
    o, states, invs = _gdn_scan_fwd(qkv, beta, gcum, grow, "gdn_scan")
    on, = _rowwise_fwd(_gdn_outnorm_fn, [(o, HEAD_V), (pm, [(3 * D + h_ * HEAD, HEAD) for h_ in range(HEADS)])],
                       [p["norm_g"]], [(D, BF16, HEAD_V)], "gdn_outnorm", groups=HEADS)
    xn, y, hn = _out_proj(on, p["w_out"], x, p, nxt, "mix_out", tm=512)
    t = dict(x=x, h=h, pm=pm, tail=tail, qkv=qkv, beta=beta, gcum=gcum, grow=grow, o=o, states=states, invs=invs,
             on=on, y=y)
    return xn, t, hn


def _gdn_bwd(t, p, dxn, res=None, prev=None, ties=()):
    s = dxn.shape[0]
    zc = [(3 * D + h_ * HEAD, HEAD) for h_ in range(HEADS)]
    dy, dgm = res or _residual_bwd(t["y"], p["gm"], dxn)
    dw_out = _mm(t["on"], dy, "tn", "mix_dwo", ties=ties)
    don = _mm(dy, p["w_out"], "nt", "mix_dout", ties=ties)
    do, dz, dnorm_g = _rowwise_bwd(_gdn_outnorm_fn, [(t["o"], HEAD_V), (t["pm"], zc)], [p["norm_g"]],
                                   [(D, BF16, HEAD_V)], [don], [(0, HEAD_V), (1, HEAD_V)], [(D, F32), (D, BF16)],
                                   "gdn_outnorm_bwd", groups=HEADS)
    dq, dk, dv, dbeta, dg, dgr = _gdn_scan_bwd(t["qkv"], t["beta"], t["gcum"], t["grow"], t["states"], t["invs"], do,
                                               "gdn_scan_bwd")
    dg = dg + _pad128(dgr.reshape(s // CHUNK, N_GROUPS, GROUP, CHUNK).transpose(0, 3, 1, 2).reshape(s, HEADS))
    dtail, da_log, ddt = _rowwise_bwd(_gdn_gates_fn, [(t["tail"], C128), (t["tail"], (128, 128))],
                                      [p["a_log"], p["dt_bias"]], [(128, F32, C128)] * 2, [dbeta, dg],
                                      [(0, C128), (0, (128, 128))], [(256, F32)], "gdn_gates_bwd")
    dxs, dcw = [], []
    for part, d in enumerate((dq, dk, dv)):
        dx_, dw_ = _gdn_conv_bwd(t["pm"], p["conv_w"], d, part, "gdn_conv_bwd")
        dxs.append(dx_)
        dcw.append(dw_)
    pieces = dxs + [dz]
    dh = _matmul([(d, p["w_main"]) for d in pieces] + [(dtail, p["w_tail"])], "nt", "gdn_dh",
                 boffs=[0, D, 2 * D, 3 * D, 0], tk=512)
    dw_main = [_mm(t["h"], d, "tn", "gdn_dwi") for d in pieces]
    dw_tail = _mm(t["h"], dtail, "tn", "gdn_dwi_tail")
    dx, dgain, dscale, dshift, res_prev = _modulate_bwd(t["x"], p, dh, dxn, prev)
    return dx, dict(gain=dgain, scale=dscale, shift=dshift, gm=dgm, w_main=jnp.concatenate(dw_main, axis=1),
                    w_tail=dw_tail, conv_w=jnp.concatenate(dcw, axis=1), a_log=da_log, dt_bias=ddt,
                    norm_g=dnorm_g, w_out=dw_out), res_prev


def _q_rows(q2, cosf, sins):
    return [(q2, HEAD_NOPE), (q2, HEAD_ROPE), (cosf, C128), (sins, C128)]


def _mla_fwd(x, p, kv, ties=(), h=None, nxt=None):
    if h is None:
        h, ties = _modulate(x, p, ties), ()
    cq = _mm(h, p["w_dq"], "nn", "mla_dq", ties=ties)
    cqn, = _rowwise_fwd(_rms_fn, [(cq, (0, Q_LORA))], [p["q_lora_g"]], [(Q_LORA, BF16, (0, Q_LORA))], "mla_qlora_norm")
    q2 = _mm(cqn, p["w_uq"], "nn", "mla_uq")
    qn, = _rowwise_fwd(_q_norm_rope_fn, _q_rows(q2, kv["cosf"], kv["sins"]), [p["q_gn"], p["q_gr"]],
                       [(HEADS * HEAD_PAD, BF16, HEAD_ALL)], "mla_q_norm", groups=HEADS)
    o, lse = _attn_fwd(qn, kv["kn"], kv["vb"], "mla_attn")
    xn, y, hn = _out_proj(o, p["w_out"], x, p, nxt, "mix_out", tm=512)
    return xn, dict(x=x, h=h, cq=cq, cqn=cqn, q2=q2, qn=qn, o=o, lse=lse, y=y), hn


def _mla_bwd(t, p, kv, dxn, res=None, prev=None, ties=(), dkv_sum=None):
    dy, dgm = res or _residual_bwd(t["y"], p["gm"], dxn)
    dw_out = _mm(t["o"], dy, "tn", "mix_dwo", ties=ties)
    do = _mm(dy, p["w_out"], "nt", "mix_dout", ties=ties)
    dq, dk, dv = _attn_bwd(t["qn"], kv["kn"], kv["vb"], do, t["o"], t["lse"], "mla_attn_bwd", dkv_sum)
    dq2, dq_gn, dq_gr = _rowwise_bwd(_q_norm_rope_fn, _q_rows(t["q2"], kv["cosf"], kv["sins"]), [p["q_gn"], p["q_gr"]],
                                     [(HEADS * HEAD_PAD, BF16, HEAD_ALL)], [dq],
                                     [(0, HEAD_NOPE), (0, HEAD_ROPE), None, None], [(HEADS * HEAD_PAD, BF16)],
                                     "mla_q_norm_bwd", groups=HEADS)
    dw_uq = _mm(t["cqn"], dq2, "tn", "mla_dwuq")
    dcqn = _mm(dq2, p["w_uq"], "nt", "mla_dcq")
    dcq, dq_lora_g = _rowwise_bwd(_rms_fn, [(t["cq"], (0, Q_LORA))], [p["q_lora_g"]], [(Q_LORA, BF16, (0, Q_LORA))],
                                  [dcqn], [(0, (0, Q_LORA))], [(Q_LORA, BF16)], "mla_qlora_norm_bwd")
    dw_dq = _mm(t["h"], dcq, "tn", "mla_dwdq")
    dh = _mm(dcq, p["w_dq"], "nt", "mla_dh")
    dx, dgain, dscale, dshift, res_prev = _modulate_bwd(t["x"], p, dh, dxn, prev)
    grads = dict(gain=dgain, scale=dscale, shift=dshift, gm=dgm, w_dq=dw_dq, q_lora_g=dq_lora_g, w_uq=dw_uq,
                 q_gn=dq_gn, q_gr=dq_gr, w_out=dw_out)
    return dx, grads, res_prev, dk, dv


def _k_rows(kvp, ckv, cosf, sins):
    return [(kvp, HEAD_NOPE), (kvp, HEAD_ROPE), (ckv, (KV_LORA, 128)), (cosf, C128), (sins, C128)]


def _kv_fwd(x, p, cosf, sins):
    h = _modulate(x, p)
    ckv = _mm(h, p["w_dkv"], "nn", "kv_down")
    lat, = _rowwise_fwd(_rms_fn, [(ckv, (0, KV_LORA))], [p["kv_g"]], [(KV_LORA, BF16, (0, KV_LORA))], "kv_norm")
    kvp = _mm(lat, p["w_ukv"], "nn", "kv_up")
    kn, vb = _rowwise_fwd(_k_norm_rope_fn, _k_rows(kvp, ckv, cosf, sins), [p["k_gn"], p["k_gr"]],
                          [(HEADS * HEAD_PAD, BF16, HEAD_ALL), (HEADS * HEAD, BF16, HEAD_V)], "kv_k_norm",
                          groups=HEADS)
    return dict(x=x, h=h, ckv=ckv, lat=lat, kvp=kvp, kn=kn, vb=vb, cosf=cosf, sins=sins)


def _kv_bwd(t, p, dk, dv, dx_in, prev):
    dkvp, drope, dk_gn, dk_gr = _rowwise_bwd(
        _k_norm_rope_fn, _k_rows(t["kvp"], t["ckv"], t["cosf"], t["sins"]), [p["k_gn"], p["k_gr"]],
        [(HEADS * HEAD_PAD, BF16, HEAD_ALL), (HEADS * HEAD, BF16, HEAD_V)], [dk, dv],
        [(0, HEAD_NOPE), (0, HEAD_ROPE), (1, C128), None, None], [(HEADS * HEAD_PAD, BF16), (128, F32)],
        "kv_k_norm_bwd", groups=HEADS)
    dw_ukv = _mm(t["lat"], dkvp, "tn", "kv_dwukv")
    dlat = _mm(dkvp, p["w_ukv"], "nt", "kv_dlat")
    dckv, dkv_g = _rowwise_bwd(_rms_fn, [(t["ckv"], (0, KV_LORA))], [p["kv_g"]], [(KV_LORA, BF16, (0, KV_LORA))],
                               [dlat], [(0, (0, KV_LORA))], [(KV_LORA, F32)], "kv_norm_bwd")
    dw_dkv = jnp.concatenate([_mm(t["h"], dckv, "tn", "kv_dwdkv"), _mm(t["h"], drope, "tn", "kv_dwdkv_rope")], axis=1)
    dh = _matmul([(dckv, p["w_dkv"]), (drope, p["w_dkv"])], "nt", "kv_dh", boffs=[0, KV_LORA])
    dx, dgain, dscale, dshift, res_prev = _modulate_bwd(t["x"], p, dh, dx_in, prev)
    return dx, dict(gain=dgain, scale=dscale, shift=dshift, w_dkv=dw_dkv, kv_g=dkv_g, w_ukv=dw_ukv, k_gn=dk_gn,
                    k_gr=dk_gr), res_prev


WEIGHTS = ["ada_w", "ada_b", "norm_g", "ffn_w_in", "ffn_w_out", "gdn_w_in", "gdn_conv_w", "gdn_a_log", "gdn_dt_bias",
           "gdn_norm_g", "gdn_w_out", "kv_ada_w", "kv_ada_b", "kv_norm_g", "mla_w_dkv", "mla_kv_norm_g", "mla_w_ukv",
           "mla_k_norm_g", "mla_w_dq", "mla_q_lora_norm_g", "mla_w_uq", "mla_q_norm_g", "mla_w_out"]
SMALL = [("ada_b", 4 * N_MOD * D), ("kv_ada_b", 2 * D), ("norm_g", DEPTH * 3 * D), ("gdn_conv_w", N_A * CONV_K * 3 * D),
         ("gdn_a_log", N_A * HEADS), ("gdn_dt_bias", N_A * HEADS), ("gdn_norm_g", N_A * HEAD), ("kv_norm_g", D),
         ("mla_kv_norm_g", KV_LORA), ("mla_k_norm_g", QK_HEAD), ("mla_q_lora_norm_g", 2 * Q_LORA),
         ("mla_q_norm_g", 2 * QK_HEAD)]
SMALL_REPLICATED = [n for n, _ in SMALL if n not in ("norm_g", "gdn_conv_w")]


def _silu_fn(g, t):
    return (_silu(t),)


def _dup_rope(t):
    return jnp.concatenate([t[..., :NOPE], t[..., NOPE:], t[..., NOPE:]], axis=-1)


def _fold_rope(t):
    return jnp.concatenate([t[..., :NOPE], t[..., NOPE:QK_HEAD] + t[..., QK_HEAD:]], axis=-1)


def _pack(pieces, rows):
    flat = jnp.concatenate([p.reshape(-1).astype(F32) for p in pieces])
    return jnp.pad(flat, (0, rows * 128 - flat.shape[0])).reshape(rows, 128)


def _step(a):
    me = 4 * lax.axis_index("x") + 2 * lax.axis_index("y") + lax.axis_index("c")
    x = a["x"][0]
    cosf, sins = _rope_tables(a["positions"][0])

    n_gdn = (4 * D + 2 * HEADS) // N_DEV
    AHEAD = 2

    stages = [(l, part) for l in range(DEPTH) for part in range(3)]

    def stage_shards(l, part):
        if part != 1:
            sh = {"ffn_w_in": a["ffn_w_in"][l, part // 2], "ffn_w_out": a["ffn_w_out"][l, part // 2]}
            if part == 2 and l == N_A - 1:
                sh.update(mla_w_dkv=a["mla_w_dkv"], mla_w_ukv=a["mla_w_ukv"])
            return sh
        if l < N_A:
            return {"gdn_w_in": a["gdn_w_in"][l], "gdn_w_out": a["gdn_w_out"][l]}
        j = l - N_A
        return {"mla_w_dq": a["mla_w_dq"][j], "mla_w_uq": a["mla_w_uq"][j], "mla_w_out": a["mla_w_out"][j]}

    def zero_of(t):
        return jnp.minimum(jnp.abs(t[(0,) * t.ndim].astype(F32)), 0.0)

    def start_stage(l, part, tie):
        sh = stage_shards(l, part)
        return list(sh), _send_start([(w + tie).astype(BF16) for w in sh.values()], f"fetch_start_{l}_{part}", gather=True)

    def finish_stage(l, part, names, handle, after):
        srcs, lands = _send_wait(handle, after, f"fetch_wait_{l}_{part}", gather=True)
        return {n: lax.dynamic_update_slice(land, src[None], (me, 0, 0)) for n, src, land in zip(names, srcs, lands)}

    n_cw, n_ng = N_A * CONV_K * 3 * HEAD, DEPTH * 3 * HEAD
    small_all = _all_gather(_pack([a["gdn_conv_w"], a["norm_g"], a["c"]], 44), "gather_small").reshape(N_DEV, -1)
    conv_w = small_all[:, :n_cw].reshape(N_DEV, N_A, CONV_K, 3 * HEAD).transpose(1, 2, 0, 3).reshape(N_A, CONV_K, 3 * D)
    norm_g = small_all[:, n_cw:n_cw + n_ng].reshape(N_DEV, DEPTH, 3, HEAD).transpose(1, 2, 0, 3).reshape(DEPTH, 3, D)
    c_all = small_all[:, n_cw + n_ng:n_cw + n_ng + D]

    c_act, = _rowwise_fwd(_silu_fn, [(c_all, FULL)], [], [(D, F32, FULL)], "c_act")
    n_ada = N_MOD * D // N_DEV
    parts = [_mm(c_act, a["ada_w"][l], "nn", "mod_proj") for l in range(DEPTH)]
    parts.append(_mm(c_act, a["kv_ada_w"], "nn", "mod_proj_kv"))
    mod_recv = _exchange(jnp.concatenate(parts, axis=1)[:, None, :], "exchange_mod")[:, 0]
    mod = mod_recv[:, :DEPTH * n_ada].reshape(N_DEV, DEPTH, n_ada).transpose(1, 0, 2).reshape(DEPTH, N_MOD * D)
    mod = (mod + a["ada_b"]).reshape(DEPTH, N_MOD, D)
    kvmod = mod_recv[:, DEPTH * n_ada:].reshape(2 * D) + a["kv_ada_b"]

    def row(v):
        return v[None]

    def ffn_params(l, i, w):
        k = 0 if i == 0 else 6
        return dict(gain=row(norm_g[l, 0 if i == 0 else 2]), shift=row(mod[l, k]), scale=row(mod[l, k + 1]),
                    gm=0.5 * row(mod[l, k + 2]), w_in=w["ffn_w_in"], wo=w["ffn_w_out"].reshape(D_FF, D))

    def gdn_params(l, w):
        w_in = w["gdn_w_in"].transpose(1, 0, 2).reshape(D, 4 * D + 2 * HEADS)
        pad = lambda t: jnp.pad(t, ((0, 0), (0, 128 - HEADS)))
        return dict(gain=row(norm_g[l, 1]), shift=row(mod[l, 3]), scale=row(mod[l, 4]), gm=row(mod[l, 5]),
                    w_main=w_in[:, :4 * D],
                    w_tail=jnp.concatenate([pad(w_in[:, 4 * D:4 * D + HEADS]), pad(w_in[:, 4 * D + HEADS:])], axis=1),
                    conv_w=conv_w[l], a_log=_pad128(row(a["gdn_a_log"][l])), dt_bias=_pad128(row(a["gdn_dt_bias"][l])),
                    norm_g=row(a["gdn_norm_g"][l]), w_out=w["gdn_w_out"].reshape(D, D))

    def mla_params(l, w):
        j = l - N_A
        uq = w["mla_w_uq"].transpose(1, 0, 2)
        qg = _dup_rope(a["mla_q_norm_g"][j])
        return dict(gain=row(norm_g[l, 1]), shift=row(mod[l, 3]), scale=row(mod[l, 4]), gm=row(mod[l, 5]),
                    w_dq=w["mla_w_dq"].reshape(D, Q_LORA), q_lora_g=row(a["mla_q_lora_norm_g"][j]),
                    w_uq=_dup_rope(uq).reshape(Q_LORA, HEADS * HEAD_PAD), q_gn=row(qg[:NOPE]), q_gr=row(qg[NOPE:]),
                    w_out=w["mla_w_out"].reshape(D, D))

    def kv_params(w):
        w_dkv = w["mla_w_dkv"].reshape(D, KV_LORA + ROPE)
        kg = _dup_rope(a["mla_k_norm_g"])
        return dict(gain=row(a["kv_norm_g"]), shift=row(kvmod[:D]), scale=row(kvmod[D:]),
                    w_dkv=jnp.concatenate([w_dkv, w_dkv[:, KV_LORA:]], axis=1), kv_g=row(a["mla_kv_norm_g"]),
                    w_ukv=w["mla_w_ukv"].transpose(1, 0, 2).reshape(KV_LORA, HEADS * 2 * HEAD), k_gn=row(kg[:NOPE]),
                    k_gr=row(kg[NOPE:]))

    tapes, kv, kv_p, h = [[] for _ in range(DEPTH)], None, None, None
    first = {name: _all_gather((w + zero_of(mod)).astype(BF16), "fetch_first_" + name)
             for name, w in stage_shards(0, 0).items()}
    pending = []
    for l, part in stages[1:1 + AHEAD]:
        tie = pending[-1][1]["token"][0, 0] if pending else zero_of(first["ffn_w_out"])
        pending.append(start_stage(l, part, tie))
    for n, (l, part) in enumerate(stages):
        if n == 0:
            w, ties = first, tuple(h["token"] for _, h in pending)
        else:
            names, handle = pending.pop(0)
            w = finish_stage(l, part, names, handle, x)
            ties = ()
            if n + AHEAD < len(stages):
                pending.append(start_stage(*stages[n + AHEAD], zero_of(w[names[0]])))
                ties = (pending[-1][1]["token"],)
        nxt = None
        if n + 1 < len(stages):
            l2, part2 = stages[n + 1]
            k2 = 3 * part2
            nxt = (row(norm_g[l2, part2]), row(mod[l2, k2 + 1]), row(mod[l2, k2]))
        if part != 1:
            p = ffn_params(l, part // 2, w)
            x, t, h = _ffn_fwd(x, p, ties, h, nxt)
        else:
            p = gdn_params(l, w) if l < N_A else mla_params(l, w)
            x, t, h = _gdn_fwd(x, p, ties, h, nxt) if l < N_A else _mla_fwd(x, p, kv, ties, h, nxt)
        tapes[l] += [p, t]
        if part == 2 and l == N_A - 1:
            kv_p = kv_params(w)
            kv = _kv_fwd(x, kv_p, cosf, sins)
    dx, loss_blk = _loss_and_grad(x, a["loss_target"][0], "loss")
    loss = lax.psum(loss_blk[0, 0], ("x", "y", "c"))

    def by_cols(g, n):
        return g.reshape(g.shape[0], -1, n).transpose(1, 0, 2)

    def ffn_blocks(g):
        return {"ffn_w_in": g["w_in"], "ffn_w_out": g["wo"].reshape(N_DEV, D_FF // N_DEV, D)}

    def mixer_blocks(l, g):
        if l < N_A:
            full = jnp.concatenate([g["w_main"], g["w_tail"][:, :HEADS], g["w_tail"][:, 128:128 + HEADS]], axis=1)
            return {"gdn_w_in": by_cols(full, n_gdn), "gdn_w_out": g["w_out"].reshape(N_DEV, D // N_DEV, D)}
        return {"mla_w_dq": g["w_dq"].reshape(N_DEV, D // N_DEV, Q_LORA),
                "mla_w_uq": _fold_rope(g["w_uq"].reshape(Q_LORA, HEADS, HEAD_PAD)).transpose(1, 0, 2),
                "mla_w_out": g["w_out"].reshape(N_DEV, D // N_DEV, D)}

    sent = []

    def send(key, blocks, tie=0.0):
        handle = _send_start([(b + tie).astype(BF16) for b in blocks.values()], "grad_start_" + "_".join(map(str, key)),
                             gather=False)
        sent.append((key, list(blocks), handle))
        return (handle["token"],)

    grads = [None] * DEPTH
    dk_sum = dv_sum = kv_grads = res = None
    ties = ()
    for l in reversed(range(DEPTH)):
        p1, t1, pm_, tm_, p2, t2 = tapes[l]
        if l == N_A - 1:
            dx, kv_grads, res = _kv_bwd(kv, kv_p, dk_sum, dv_sum, dx, (t2["y"], p2["gm"]))
            d_dkv = kv_grads["w_dkv"]
            ties += send((l, 3), {
                "mla_w_dkv": jnp.concatenate(
                    [d_dkv[:, :KV_LORA], d_dkv[:, KV_LORA:KV_LORA + ROPE] + d_dkv[:, KV_LORA + ROPE:]],
                    axis=1).reshape(N_DEV, D // N_DEV, KV_LORA + ROPE),
                "mla_w_ukv": by_cols(kv_grads["w_ukv"], 2 * HEAD)})
        dx, g2, res = _ffn_bwd(t2, p2, dx, res, (tm_["y"], pm_["gm"]), ties)
        ties = send((l, 2), ffn_blocks(g2))
        if l < N_A:
            dx, gm_, res = _gdn_bwd(tm_, pm_, dx, res, (t1["y"], p1["gm"]), ties)
        else:
            dx, gm_, res, dk_sum, dv_sum = _mla_bwd(tm_, pm_, kv, dx, res, (t1["y"], p1["gm"]), ties,
                                                    None if dk_sum is None else (dk_sum, dv_sum))
        ties = send((l, 1), mixer_blocks(l, gm_))
        prev = (tapes[l - 1][5]["y"], tapes[l - 1][4]["gm"]) if l > 0 and l != N_A else None
        dx, g1, res = _ffn_bwd(t1, p1, dx, res, prev, ties)
        if l > 0:
            ties = send((l, 0), ffn_blocks(g1))
        grads[l] = (g1, gm_, g2)

    out = {}
    def dmod(l):
        g1, gm_, g2 = grads[l]
        return jnp.concatenate([g1["shift"], g1["scale"], 0.5 * g1["gm"], gm_["shift"], gm_["scale"], gm_["gm"],
                                g2["shift"], g2["scale"], 0.5 * g2["gm"]], axis=1)

    gdn = [grads[l][1] for l in range(N_A)]
    mla = [grads[l][1] for l in range(N_A, DEPTH)]
    small = {
        "ada_b": jnp.concatenate([dmod(l) for l in range(DEPTH)], axis=0),
        "kv_ada_b": jnp.concatenate([kv_grads["shift"], kv_grads["scale"]], axis=1),
        "norm_g": jnp.stack([jnp.concatenate([grads[l][0]["gain"], grads[l][1]["gain"], grads[l][2]["gain"]], axis=0)
                             for l in range(DEPTH)]),
        "gdn_conv_w": jnp.stack([g["conv_w"] for g in gdn]),
        "gdn_a_log": jnp.stack([g["a_log"][0, :HEADS] for g in gdn]),
        "gdn_dt_bias": jnp.stack([g["dt_bias"][0, :HEADS] for g in gdn]),
        "gdn_norm_g": jnp.stack([g["norm_g"][0] for g in gdn]),
        "kv_norm_g": kv_grads["gain"],
        "mla_kv_norm_g": kv_grads["kv_g"],
        "mla_k_norm_g": _fold_rope(jnp.concatenate([kv_grads["k_gn"], kv_grads["k_gr"]], axis=1)),
        "mla_q_lora_norm_g": jnp.stack([g["q_lora_g"][0] for g in mla]),
        "mla_q_norm_g": jnp.stack([_fold_rope(jnp.concatenate([g["q_gn"], g["q_gr"]], axis=1))[0] for g in mla]),
    }
    rows = 616
    assert sum(n for _, n in SMALL) <= rows * 128 and all(small[n].size == k for n, k in SMALL)
    small_recv = _all_gather(_pack([small[n] for n, _ in SMALL], rows), "gather_small_grads")
    small_recv = small_recv + send((0, 0), ffn_blocks(grads[0][0]), zero_of(small_recv))[0][0, 0]
    zero = lambda n, k: jnp.zeros((k,), F32)
    packed = {pre: _pack([a[pre + n] if n in SMALL_REPLICATED else zero(n, k) for n, k in SMALL], rows)
              for pre in ("", "m_", "v_")}
    res = _adamw([small_recv], packed[""], packed["m_"], packed["v_"], "adamw_small")
    offs = {}
    o = 0
    for n, k in SMALL:
        offs[n] = o
        o += k
    for n, k in SMALL:
        if n in SMALL_REPLICATED:
            out[n] = [r.reshape(-1)[offs[n]:offs[n] + k] for r in res]
    gsum = res[0].reshape(-1)
    g_norm = lax.dynamic_slice_in_dim(gsum[offs["norm_g"]:offs["norm_g"] + DEPTH * 3 * D].reshape(DEPTH * 3, D),
                                      me * HEAD, HEAD, axis=1)
    g_conv = lax.dynamic_slice_in_dim(
        gsum[offs["gdn_conv_w"]:offs["gdn_conv_w"] + N_A * CONV_K * 3 * D].reshape(N_A * CONV_K, 3 * D),
        me * 3 * HEAD, 3 * HEAD, axis=1)
    res2 = _adamw([_pack([g_norm, g_conv], 36)[None]], *[_pack([a[pre + "norm_g"], a[pre + "gdn_conv_w"]], 36)
                                                      for pre in ("", "m_", "v_")], "adamw_small")
    out["norm_g"] = [r.reshape(-1)[:n_ng] for r in res2]
    out["gdn_conv_w"] = [r.reshape(-1)[n_ng:n_ng + n_cw] for r in res2]

    c_act_t = c_act.T
    all_small = small_recv.reshape(N_DEV, -1)
    dmod_all = all_small[:, :DEPTH * N_MOD * D].reshape(N_DEV, DEPTH, N_MOD * D)
    dmod_mine = lax.dynamic_slice_in_dim(dmod_all, me * n_ada, n_ada, axis=2)
    g_ada = [_outer8(c_act_t, dmod_mine[:, l], "ada_grad")[None] for l in range(DEPTH)]
    out["ada_w"] = _adamw(g_ada, *[a[pre + "ada_w"].reshape(DEPTH * D, n_ada) for pre in ("", "m_", "v_")], "adamw")
    dkv_all = all_small[:, offs["kv_ada_b"]:offs["kv_ada_b"] + 2 * D]
    g_kv = _outer8(c_act_t, lax.dynamic_slice_in_dim(dkv_all, me * (2 * D // N_DEV), 2 * D // N_DEV, axis=1), "ada_grad")
    out["kv_ada_w"] = _adamw([g_kv[None]], *[a[pre + "kv_ada_w"] for pre in ("", "m_", "v_")], "adamw")

    pieces = {}
    for key, names, handle in sent:
        srcs, lands = _send_wait(handle, out["kv_ada_w"][0], "grad_wait_" + "_".join(map(str, key)), gather=False)
        for name, src, land in zip(names, srcs, lands):
            own = lax.dynamic_slice_in_dim(src, me, 1, axis=0)
            pieces.setdefault(name, []).append((key, lax.dynamic_update_slice(land, own, (me, 0, 0))))
    for name, parts in pieces.items():
        wide = a[name].shape[-1]
        out[name] = _adamw([p for _, p in sorted(parts, key=lambda kp: kp[0])], a[name].reshape(-1, wide),
                           a["m_" + name].reshape(-1, wide), a["v_" + name].reshape(-1, wide), "adamw")

    result = [loss, dx[None]]
    for k in range(4):
        result += [out[n][k].reshape(a[n].shape) for n in WEIGHTS]
    return tuple(result)


def kernel(x, c, positions, ada_w, ada_b, norm_g, ffn_w_in, ffn_w_out, gdn_w_in, gdn_conv_w, gdn_a_log, gdn_dt_bias, gdn_norm_g, gdn_w_out, kv_ada_w, kv_ada_b, kv_norm_g, mla_w_dkv, mla_kv_norm_g, mla_w_ukv, mla_k_norm_g, mla_w_dq, mla_q_lora_norm_g, mla_w_uq, mla_q_norm_g, mla_w_out, loss_target, m_ada_w, m_ada_b, m_norm_g, m_ffn_w_in, m_ffn_w_out, m_gdn_w_in, m_gdn_conv_w, m_gdn_a_log, m_gdn_dt_bias, m_gdn_norm_g, m_gdn_w_out, m_kv_ada_w, m_kv_ada_b, m_kv_norm_g, m_mla_w_dkv, m_mla_kv_norm_g, m_mla_w_ukv, m_mla_k_norm_g, m_mla_w_dq, m_mla_q_lora_norm_g, m_mla_w_uq, m_mla_q_norm_g, m_mla_w_out, v_ada_w, v_ada_b, v_norm_g, v_ffn_w_in, v_ffn_w_out, v_gdn_w_in, v_gdn_conv_w, v_gdn_a_log, v_gdn_dt_bias, v_gdn_norm_g, v_gdn_w_out, v_kv_ada_w, v_kv_ada_b, v_kv_norm_g, v_mla_w_dkv, v_mla_kv_norm_g, v_mla_w_ukv, v_mla_k_norm_g, v_mla_w_dq, v_mla_q_lora_norm_g, v_mla_w_uq, v_mla_q_norm_g, v_mla_w_out):
    return _step(dict(locals()))
```

```python
import functools

import jax
import jax.numpy as jnp
from jax import lax
from jax.experimental import pallas as pl
from jax.experimental.pallas import tpu as pltpu

F32 = jnp.float32
BF16 = jnp.bfloat16

N_DEV = 8
D = 1024
D_FF = 2816
DEPTH = 4
N_A = 2
N_MOD = 9
HEADS = 8
HEAD = 128
CHUNK = 64
CONV_K = 4
KV_LORA = 256
Q_LORA = 384
NOPE = 128
ROPE = 64
QK_HEAD = NOPE + ROPE
HEAD_PAD = 256
ROPE_BASE = 10000.0
EPS = 1e-6
LR, B1, B2, ADAM_EPS, WD, STEP = 0.001, 0.9, 0.999, 1e-08, 0.01, 10

VMEM_LIMIT = 48 * 1024 * 1024
ROW_TILE = 256
MESH = pl.DeviceIdType.MESH

_NN = (((1,), (0,)), ((), ()))
_NT = (((1,), (1,)), ((), ()))
_TN = (((0,), (0,)), ((), ()))
_DIMS = {"nn": _NN, "nt": _NT, "tn": _TN}


def _params(dims=None):
    return pltpu.CompilerParams(dimension_semantics=dims, vmem_limit_bytes=VMEM_LIMIT)


def _tile(n, target):
    for t in range(target - target % 128, 0, -128):
        if n % t == 0:
            return t
    return n


_TIE_SPEC1 = pl.BlockSpec((8, 128), lambda i: (0, 0))
_TIE_SPEC2 = pl.BlockSpec((8, 128), lambda i, j: (0, 0))
_TIE_SPEC3 = pl.BlockSpec((8, 128), lambda i, j, k: (0, 0))


def _matmul(pairs, form, name, out_dtype=F32, tm=1408, tn=1408, tk=1408, boffs=None, resid=None, ties=()):
    a0, b0 = pairs[0]
    if form == "nn":
        m, n = a0.shape[0], b0.shape[1]
        ks = [a.shape[1] for a, _ in pairs]
    elif form == "nt":
        m, n = a0.shape[0], b0.shape[0]
        ks = [a.shape[1] for a, _ in pairs]
    else:
        m, n = a0.shape[1], b0.shape[1]
        ks = [a.shape[0] for a, _ in pairs]
    tm, tn = _tile(m, tm), _tile(n, tn)
    tks = [_tile(k, tk) for k in ks]
    boffs = boffs or [0] * len(pairs)
    assert m % tm == 0 and n % tn == 0 and all(o % t == 0 for o, t in zip(boffs, tks)), (name, m, n, ks)
    steps = [k // t for k, t in zip(ks, tks)]
    starts = [sum(steps[:p]) for p in range(len(pairs))]
    nk = sum(steps)

    def kidx(p, k):
        return jnp.clip(k - starts[p], 0, steps[p] - 1)

    in_specs, args = [], []
    for p, (a, b) in enumerate(pairs):
        t = tks[p]
        if form == "tn":
            in_specs.append(pl.BlockSpec((t, tm), lambda i, j, k, p=p: (kidx(p, k), i)))
            in_specs.append(pl.BlockSpec((t, tn), lambda i, j, k, p=p: (kidx(p, k), j)))
        elif form == "nn":
            in_specs.append(pl.BlockSpec((tm, t), lambda i, j, k, p=p: (i, kidx(p, k))))
            in_specs.append(pl.BlockSpec((t, tn), lambda i, j, k, p=p: (kidx(p, k), j)))
        else:
            in_specs.append(pl.BlockSpec((tm, t), lambda i, j, k, p=p: (i, kidx(p, k))))
            in_specs.append(pl.BlockSpec((tn, t), lambda i, j, k, p=p, o=boffs[p] // t: (j, kidx(p, k) + o)))
        args += [a, b]
    dims = _DIMS[form]
    npairs = len(pairs)
    nres = len(resid or ())
    nin = 2 * npairs + len(ties) + nres
    out_blk = pl.BlockSpec((tm, tn), lambda i, j, k: (i, j))
    in_specs += [_TIE_SPEC3] * len(ties)
    args += list(ties)
    if resid:
        assert nres == 2 or (nres == 5 and tn == n)
        in_specs += [out_blk] + [pl.BlockSpec((1, tn), lambda i, j, k: (0, j))] * (nres - 1)
        args += list(resid)

    def body(*refs):
        o_ref = refs[nin]
        k = pl.program_id(2)

        def prod(p):
            return lax.dot_general(refs[2 * p][...].astype(BF16), refs[2 * p + 1][...].astype(BF16), dims,
                                   preferred_element_type=F32)

        def finish(y):
            o_ref[...] = y.astype(o_ref.dtype)
            if resid:
                x_ref, gate_ref = refs[nin - nres], refs[nin - nres + 1]
                xn = x_ref[...] + gate_ref[...] * y
                refs[nin + 1][...] = xn
                if nres == 5:
                    gain, scale, shift = (r[...] for r in refs[nin - 3:nin])
                    refs[nin + 2][...] = _modulate_fn(0, xn, gain, scale, shift)[0].astype(BF16)

        if nk == 1:
            finish(prod(0))
            return
        acc = refs[-1]

        @pl.when(k == 0)
        def _():
            acc[...] = jnp.zeros_like(acc)

        for p in range(npairs):
            @pl.when((k >= starts[p]) & (k < starts[p] + steps[p]))
            def _(p=p):
                acc[...] += prod(p)

        @pl.when(k == nk - 1)
        def _():
            finish(acc[...])

    res = pl.pallas_call(
        body, name=name, grid=(m // tm, n // tn, nk), in_specs=in_specs,
        out_specs=[out_blk] * (2 + (nres == 5)) if resid else out_blk,
        out_shape=([jax.ShapeDtypeStruct((m, n), out_dtype), jax.ShapeDtypeStruct((m, n), F32)]
                   + [jax.ShapeDtypeStruct((m, n), BF16)] * (nres == 5))
        if resid else jax.ShapeDtypeStruct((m, n), out_dtype),
        scratch_shapes=[] if nk == 1 else [pltpu.VMEM((tm, tn), F32)],
        compiler_params=_params(("parallel", "parallel", "arbitrary")),
    )(*args)
    return res


def _mm(a, b, form, name, **kw):
    return _matmul([(a, b)], form, name, **kw)


def _cols(spec, g):
    return spec[g] if isinstance(spec, list) else spec


def _rowwise_fwd(fn, rows, pars, outs, name, groups=1, ts=ROW_TILE, ties=()):
    s = rows[0][0].shape[0]
    ts = min(ts, s)
    assert s % ts == 0
    nr, npar = len(rows), len(pars)

    def body(*refs):
        par_t = [r[...] for r in refs[nr:nr + npar]]
        out_refs = refs[nr + npar + len(ties):]
        for g in range(groups):
            row_t = []
            for r, (_, spec) in zip(refs[:nr], rows):
                c0, w = _cols(spec, g)
                row_t.append(r[:, c0:c0 + w].astype(F32))
            res = fn(g, *row_t, *par_t)
            for o_ref, val, (_, _, spec) in zip(out_refs, res, outs):
                c0, w = _cols(spec, g)
                o_ref[:, c0:c0 + w] = val.astype(o_ref.dtype)

    return pl.pallas_call(
        body, name=name, grid=(s // ts,),
        in_specs=[pl.BlockSpec((ts, a.shape[1]), lambda i: (i, 0)) for a, _ in rows]
        + [pl.BlockSpec(p.shape, lambda i: (0, 0)) for p in pars] + [_TIE_SPEC1] * len(ties),
        out_specs=[pl.BlockSpec((ts, w), lambda i: (i, 0)) for w, _, _ in outs],
        out_shape=[jax.ShapeDtypeStruct((s, w), dt) for w, dt, _ in outs],
        compiler_params=_params(("parallel",)),
    )(*[a for a, _ in rows], *pars, *ties)


def _rowwise_bwd(fn, rows, pars, outs, douts, gmap, gshapes, name, groups=1, add=None, par_grads=True,
                 ts=ROW_TILE):
    s = rows[0][0].shape[0]
    ts = min(ts, s)
    assert s % ts == 0
    nr, npar, nout, ng = len(rows), len(pars), len(outs), len(gshapes)
    add = add or {}
    add_keys = sorted(add)

    def body(*refs):
        row_refs = refs[:nr]
        par_refs = refs[nr:nr + npar]
        dout_refs = refs[nr + npar:nr + npar + nout]
        add_refs = refs[nr + npar + nout:nr + npar + nout + len(add_keys)]
        g_refs = refs[nr + npar + nout + len(add_keys):][:ng]
        pg_refs = refs[nr + npar + nout + len(add_keys) + ng:]
        par_t = [r[...] for r in par_refs]
        par_acc = [None] * npar
        shared_acc = {}
        for g in range(groups):
            row_t = []
            for r, (_, spec) in zip(row_refs, rows):
                c0, w = _cols(spec, g)
                row_t.append(r[:, c0:c0 + w].astype(F32))
            cts = []
            for r, (_, _, spec) in zip(dout_refs, outs):
                c0, w = _cols(spec, g)
                cts.append(r[:, c0:c0 + w].astype(F32))
            _, vjp = jax.vjp(lambda *t, g=g: tuple(fn(g, *t)), *row_t, *par_t)
            grads = vjp(tuple(cts))
            for k in range(nr):
                if gmap[k] is None:
                    continue
                gi, spec = gmap[k]
                if isinstance(spec, list) or groups == 1:
                    c0, w = _cols(spec, g)
                    val = grads[k]
                    if gi in add:
                        val = val + add_refs[add_keys.index(gi)][:, c0:c0 + w].astype(F32)
                    g_refs[gi][:, c0:c0 + w] = val.astype(g_refs[gi].dtype)
                else:
                    shared_acc[k] = grads[k] if k not in shared_acc else shared_acc[k] + grads[k]
            if par_grads:
                for k in range(npar):
                    pg = grads[nr + k]
                    par_acc[k] = pg if par_acc[k] is None else par_acc[k] + pg
        for k, val in shared_acc.items():
            gi, (c0, w) = gmap[k]
            assert gi not in add
            g_refs[gi][:, c0:c0 + w] = val.astype(g_refs[gi].dtype)
        if par_grads:
            first = pl.program_id(0) == 0
            for k in range(npar):
                @pl.when(first)
                def _(k=k):
                    pg_refs[k][...] = par_acc[k]

                @pl.when(jnp.logical_not(first))
                def _(k=k):
                    pg_refs[k][...] += par_acc[k]

    out_specs = [pl.BlockSpec((ts, w), lambda i: (i, 0)) for w, _ in gshapes]
    out_shape = [jax.ShapeDtypeStruct((s, w), dt) for w, dt in gshapes]
    if par_grads:
        out_specs += [pl.BlockSpec(p.shape, lambda i: (0, 0)) for p in pars]
        out_shape += [jax.ShapeDtypeStruct(p.shape, F32) for p in pars]
    return pl.pallas_call(
        body, name=name, grid=(s // ts,),
        in_specs=[pl.BlockSpec((ts, a.shape[1]), lambda i: (i, 0)) for a, _ in rows]
        + [pl.BlockSpec(p.shape, lambda i: (0, 0)) for p in pars]
        + [pl.BlockSpec((ts, a.shape[1]), lambda i: (i, 0)) for a in douts]
        + [pl.BlockSpec((ts, add[k].shape[1]), lambda i: (i, 0)) for k in add_keys],
        out_specs=out_specs, out_shape=out_shape,
        compiler_params=_params(("arbitrary",)),
    )(*[a for a, _ in rows], *pars, *douts, *[add[k] for k in add_keys])


def _sigmoid(x):
    return 1.0 / (1.0 + jnp.exp(-x))


def _silu(x):
    return x * _sigmoid(x)


def _softplus(x):
    return jnp.maximum(x, 0.0) + jnp.log(1.0 + jnp.exp(-jnp.abs(x)))


def _rms(t, g, n=None):
    n = n or t.shape[-1]
    return t * lax.rsqrt(jnp.sum(t * t, axis=-1, keepdims=True) / n + EPS) * g


def _modulate_fn(g, x, gain, scale, shift):
    return (_rms(x, gain) * (1.0 + scale) + shift,)


def _gate_only_fn(g, y, gm):
    return (gm * y,)


def _gdn_gates_fn(g, b_logit, a_logit, a_log, dt_bias):
    gate = -jnp.exp(a_log) * _softplus(a_logit + dt_bias)
    n = gate.shape[0]
    i = lax.broadcasted_iota(jnp.int32, (n, n), 0)
    j = lax.broadcasted_iota(jnp.int32, (n, n), 1)
    tri = (((i // CHUNK) == (j // CHUNK)) & (i >= j)).astype(F32)
    gcum = lax.dot_general(tri, gate, _NN, preferred_element_type=F32, precision=lax.Precision.HIGHEST)
    return _sigmoid(b_logit), gcum


def _gdn_outnorm_fn(g, o, z, gain):
    return (_rms(o, gain) * _silu(z),)


def _rms_fn(g, t, gain):
    return (_rms(t, gain),)


@jax.custom_vjp
def _swap_halves(t):
    return pltpu.roll(t, 32, 1)


_swap_halves.defvjp(lambda t: (pltpu.roll(t, 32, 1), None), lambda _, ct: (pltpu.roll(ct, 96, 1),))


def _head_norm_rope_fn(g, nope, rope, cosf, sins, gain_n, gain_r):
    first = lax.broadcasted_iota(jnp.int32, rope.shape, 1) < ROPE
    ss = jnp.sum(nope * nope, axis=-1, keepdims=True) + jnp.sum(jnp.where(first, rope * rope, 0.0), axis=-1,
                                                                 keepdims=True)
    r = lax.rsqrt(ss / QK_HEAD + EPS)
    tn = nope * r * gain_n
    tr = rope * r * gain_r
    rot = jnp.where(first, tr * cosf + _swap_halves(tr) * sins, 0.0)
    return tn, rot


def _q_norm_rope_fn(g, nope, rope, cosf, sins, gain_n, gain_r):
    tn, rot = _head_norm_rope_fn(g, nope, rope, cosf, sins, gain_n, gain_r)
    return (jnp.concatenate([tn, rot], axis=1),)


def _k_norm_rope_fn(g, nope, val, rope, cosf, sins, gain_n, gain_r):
    tn, rot = _head_norm_rope_fn(g, nope, rope, cosf, sins, gain_n, gain_r)
    return jnp.concatenate([tn, rot], axis=1), val


FF_SH = 2 * D_FF // N_DEV
FF_G = N_DEV // 2


def _ffn_in(h, w_in, name, tm=1024, ties=()):
    s = h.shape[0]
    tm = min(tm, s)

    def body(h_ref, wg_ref, wu_ref, *rest):
        g_ref, u_ref, a_ref = rest[-3:]
        hb = h_ref[...]
        gate = jnp.dot(hb, wg_ref[...], preferred_element_type=F32)
        up = jnp.dot(hb, wu_ref[...], preferred_element_type=F32)
        sg = _sigmoid(gate)
        silu = gate * sg
        g_ref[...] = (up * (sg * (1.0 + gate * (1.0 - sg)))).astype(BF16)
        u_ref[...] = silu.astype(BF16)
        a_ref[...] = (silu * up).astype(BF16)

    spec = pl.BlockSpec((None, tm, FF_SH), lambda j, i: (j, i, 0))
    return pl.pallas_call(
        body, name=name, grid=(FF_G, s // tm),
        in_specs=[pl.BlockSpec((tm, D), lambda j, i: (i, 0)), pl.BlockSpec((None, D, FF_SH), lambda j, i: (j, 0, 0)),
                  pl.BlockSpec((None, D, FF_SH), lambda j, i: (j + FF_G, 0, 0))] + [_TIE_SPEC2] * len(ties),
        out_specs=[spec, spec, spec], out_shape=[jax.ShapeDtypeStruct((FF_G, s, FF_SH), BF16)] * 3,
        compiler_params=_params(("parallel", "parallel")),
    )(h, w_in, w_in, *ties)


def _ffn_out(act, wo, resid, name, tm=512):
    s = act.shape[1]
    tm = min(tm, s)
    nres = len(resid)

    def body(a_ref, b_ref, x_ref, gate_ref, *rest):
        mods, outs = rest[:nres - 2], rest[nres - 2:]
        y = jnp.dot(a_ref[0], b_ref[0:FF_SH, :], preferred_element_type=F32)
        for k in range(1, FF_G):
            y = y + jnp.dot(a_ref[k], b_ref[k * FF_SH:(k + 1) * FF_SH, :], preferred_element_type=F32)
        xn = x_ref[...] + gate_ref[...] * y
        outs[0][...] = y.astype(BF16)
        outs[1][...] = xn
        if mods:
            outs[2][...] = _modulate_fn(0, xn, *[m[...] for m in mods])[0].astype(BF16)

    blk = pl.BlockSpec((tm, D), lambda i: (i, 0))
    vec = pl.BlockSpec((1, D), lambda i: (0, 0))
    return pl.pallas_call(
        body, name=name, grid=(s // tm,),
        in_specs=[pl.BlockSpec((FF_G, tm, FF_SH), lambda i: (0, i, 0)), pl.BlockSpec((D_FF, D), lambda i: (0, 0)),
                  blk] + [vec] * (nres - 1),
        out_specs=[blk] * (2 + (nres == 5)),
        out_shape=[jax.ShapeDtypeStruct((s, D), BF16), jax.ShapeDtypeStruct((s, D), F32)]
        + [jax.ShapeDtypeStruct((s, D), BF16)] * (nres == 5),
        compiler_params=_params(("parallel",)),
    )(act, wo, *resid)


def _ffn_bwd_act(dy, wo, act_dgate, act_dup, name, tm=1024, ties=()):
    s = dy.shape[0]
    tm = min(tm, s)

    def body(dy_ref, wo_ref, g_ref, u_ref, *rest):
        dg_ref, du_ref = rest[-2:]
        dact = lax.dot_general(dy_ref[...], wo_ref[...], _NT, preferred_element_type=F32)
        dg_ref[...] = (dact * g_ref[...].astype(F32)).astype(BF16)
        du_ref[...] = (dact * u_ref[...].astype(F32)).astype(BF16)

    spec = pl.BlockSpec((None, tm, FF_SH), lambda j, i: (j, i, 0))
    return pl.pallas_call(
        body, name=name, grid=(FF_G, s // tm),
        in_specs=[pl.BlockSpec((tm, D), lambda j, i: (i, 0)), pl.BlockSpec((FF_SH, D), lambda j, i: (j, 0)), spec, spec]
        + [_TIE_SPEC2] * len(ties),
        out_specs=[spec, spec], out_shape=[jax.ShapeDtypeStruct((FF_G, s, FF_SH), BF16)] * 2,
        compiler_params=_params(("parallel", "parallel")),
    )(dy, wo, act_dgate, act_dup, *ties)


def _ffn_dwo(act, dy, name, tk=2048, ties=()):
    s = act.shape[1]
    tk = min(tk, s)

    def body(a_ref, b_ref, *rest):
        o_ref, acc = rest[-2:]
        k = pl.program_id(1)

        @pl.when(k == 0)
        def _():
            acc[...] = jnp.zeros_like(acc)

        acc[...] += lax.dot_general(a_ref[...], b_ref[...], _TN, preferred_element_type=F32)

        @pl.when(k == s // tk - 1)
        def _():
            o_ref[...] = acc[...].astype(BF16)

    return pl.pallas_call(
        body, name=name, grid=(FF_G, s // tk),
        in_specs=[pl.BlockSpec((None, tk, FF_SH), lambda j, k: (j, k, 0)), pl.BlockSpec((tk, D), lambda j, k: (k, 0))]
        + [_TIE_SPEC2] * len(ties),
        out_specs=pl.BlockSpec((FF_SH, D), lambda j, k: (j, 0)), out_shape=jax.ShapeDtypeStruct((D_FF, D), BF16),
        scratch_shapes=[pltpu.VMEM((FF_SH, D), F32)], compiler_params=_params(("parallel", "arbitrary")),
    )(act, dy, *ties)


def _ffn_halves(k, gate_ref, up_ref, fn):
    pl.when(k < FF_G)(functools.partial(fn, gate_ref))
    pl.when(k >= FF_G)(functools.partial(fn, up_ref))


def _ffn_dh(dgate, dup, w_in, name, tm=512):
    s = dgate.shape[1]
    tm = min(tm, s)

    def body(dg_ref, du_ref, w_ref, o_ref):
        acc = lax.dot_general(dg_ref[0], w_ref[0], _NT, preferred_element_type=F32)
        for k in range(1, N_DEV):
            d_ref = dg_ref if k < FF_G else du_ref
            acc = acc + lax.dot_general(d_ref[k % FF_G], w_ref[k], _NT, preferred_element_type=F32)
        o_ref[...] = acc

    half = pl.BlockSpec((FF_G, tm, FF_SH), lambda i: (0, i, 0))
    return pl.pallas_call(
        body, name=name, grid=(s // tm,),
        in_specs=[half, half, pl.BlockSpec((N_DEV, D, FF_SH), lambda i: (0, 0, 0))],
        out_specs=pl.BlockSpec((tm, D), lambda i: (i, 0)), out_shape=jax.ShapeDtypeStruct((s, D), F32),
        compiler_params=_params(("parallel",)),
    )(dgate, dup, w_in)


def _ffn_dwi(h, dgate, dup, name, tk=2048):
    s = h.shape[0]
    tk = min(tk, s)

    def body(h_ref, dg_ref, du_ref, o_ref, acc):
        j, k = pl.program_id(0), pl.program_id(1)

        @pl.when(k == 0)
        def _():
            acc[...] = jnp.zeros_like(acc)

        def add(d_ref):
            acc[...] += lax.dot_general(h_ref[...], d_ref[...], _TN, preferred_element_type=F32)

        _ffn_halves(j, dg_ref, du_ref, add)

        @pl.when(k == s // tk - 1)
        def _():
            o_ref[...] = acc[...].astype(BF16)

    return pl.pallas_call(
        body, name=name, grid=(N_DEV, s // tk),
        in_specs=[pl.BlockSpec((tk, D), lambda j, k: (k, 0)),
                  pl.BlockSpec((None, tk, FF_SH), lambda j, k: (jnp.minimum(j, FF_G - 1), jnp.where(j < FF_G, k, s // tk - 1), 0)),
                  pl.BlockSpec((None, tk, FF_SH), lambda j, k: (jnp.maximum(j - FF_G, 0), jnp.where(j < FF_G, 0, k), 0))],
        out_specs=pl.BlockSpec((None, D, FF_SH), lambda j, k: (j, 0, 0)),
        out_shape=jax.ShapeDtypeStruct((N_DEV, D, FF_SH), BF16),
        scratch_shapes=[pltpu.VMEM((D, FF_SH), F32)], compiler_params=_params(("parallel", "arbitrary")),
    )(h, dgate, dup)


def _shift_down(x, d):
    rows = lax.broadcasted_iota(jnp.int32, x.shape, 0)
    return jnp.where(rows >= d, pltpu.roll(x, d, 0), 0.0)


def _shift_up(x, d):
    n = x.shape[0]
    rows = lax.broadcasted_iota(jnp.int32, x.shape, 0)
    return jnp.where(rows < n - d, pltpu.roll(x, n - d, 0), 0.0)


def _conv_post(pre, is_qk):
    a = _silu(pre)
    l2 = a * lax.rsqrt(jnp.sum(a * a, axis=-1, keepdims=True) + EPS)
    return jnp.where(is_qk, l2, a)


def _conv_taps(x):
    return [_shift_down(x, CONV_K - 1 - j) for j in range(CONV_K - 1)] + [x]


def _conv_pre(x, w, taps=None):
    taps = taps or _conv_taps(x)
    pre = taps[0] * w[0:1, :]
    for j in range(1, CONV_K):
        pre = pre + taps[j] * w[j:j + 1, :]
    return pre


def _gdn_conv_fwd(pm, conv_w, name):
    s = pm.shape[0]
    nblk = 3 * D // HEAD

    def body(x_ref, w_ref, o_ref):
        is_qk = pl.program_id(0) < 2 * HEADS
        o_ref[...] = _conv_post(_conv_pre(x_ref[...], w_ref[...]), is_qk)

    return pl.pallas_call(
        body, name=name, grid=(nblk,),
        in_specs=[pl.BlockSpec((s, HEAD), lambda c: (0, c)), pl.BlockSpec((CONV_K, HEAD), lambda c: (0, c))],
        out_specs=pl.BlockSpec((s, HEAD), lambda c: (0, c)),
        out_shape=jax.ShapeDtypeStruct((s, 3 * D), F32), compiler_params=_params(("parallel",)),
    )(pm, conv_w)


def _gdn_conv_bwd(pm, conv_w, dout, part, name):
    s = pm.shape[0]
    off = part * HEADS

    def body(x_ref, w_ref, d_ref, dx_ref, dw_ref):
        x, w = x_ref[...], w_ref[...]
        taps = _conv_taps(x)
        _, vjp = jax.vjp(lambda p: _conv_post(p, part < 2), _conv_pre(x, w, taps))
        dpre, = vjp(d_ref[...])
        dx = dpre * w[CONV_K - 1:CONV_K, :]
        for j in range(CONV_K - 1):
            dx = dx + _shift_up(dpre, CONV_K - 1 - j) * w[j:j + 1, :]
        dx_ref[...] = dx.astype(BF16)
        dw_ref[...] = jnp.concatenate([jnp.sum(dpre * tap, axis=0, keepdims=True) for tap in taps], axis=0)

    return pl.pallas_call(
        body, name=name, grid=(HEADS,),
        in_specs=[pl.BlockSpec((s, HEAD), lambda c: (0, c + off)), pl.BlockSpec((CONV_K, HEAD), lambda c: (0, c + off)),
                  pl.BlockSpec((s, HEAD), lambda c: (0, c))],
        out_specs=[pl.BlockSpec((s, HEAD), lambda c: (0, c)), pl.BlockSpec((CONV_K, HEAD), lambda c: (0, c))],
        out_shape=[jax.ShapeDtypeStruct((s, D), BF16), jax.ShapeDtypeStruct((CONV_K, D), F32)],
        compiler_params=_params(("parallel",)),
    )(pm, conv_w, dout)


def _dot3(a, b, dims=_NN):
    ah, bh = a.astype(BF16), b.astype(BF16)
    al, bl = (a - ah.astype(F32)).astype(BF16), (b - bh.astype(F32)).astype(BF16)
    d = lambda u, v: lax.dot_general(u, v, dims, preferred_element_type=F32)
    return d(ah, bh) + (d(ah, bl) + d(al, bh))


def _make_dot(hi):
    def raw(a, b, dims):
        if hi:
            return _dot3(a, b, dims)
        return lax.dot_general(a.astype(BF16), b.astype(BF16), dims, preferred_element_type=F32)

    @functools.partial(jax.custom_vjp, nondiff_argnums=(2,))
    def dot(a, b, form):
        return raw(a, b, _DIMS[form])

    def fwd(a, b, form):
        return raw(a, b, _DIMS[form]), (a, b)

    def bwd(form, res, ct):
        a, b = res
        if form == "nn":
            return raw(ct, b, _NT), raw(a, ct, _TN)
        if form == "nt":
            return raw(ct, b, _NN), raw(ct, a, _TN)
        return raw(b, ct, _NT), raw(a, ct, _NN)

    dot.defvjp(fwd, bwd)
    return dot


_dot = _make_dot(False)
_dot_hi = _make_dot(True)


def _tri_inv_raw(low):
    n = low.shape[0]
    i = lax.broadcasted_iota(jnp.int32, (n, n), 0)
    j = lax.broadcasted_iota(jnp.int32, (n, n), 1)
    eye = (i == j).astype(F32)
    hdot = _dot3
    same16 = (i // 16) == (j // 16)
    neg = jnp.where(same16, -low, 0.0)
    inv = eye + neg
    power = neg
    for _ in range(3):
        power = hdot(power, power)
        inv = hdot(inv, eye + power)
    for blk in (32, 64):
        off = jnp.where(((i // blk) == (j // blk)) & ((i // (blk // 2)) != (j // (blk // 2))), low, 0.0)
        inv = inv - hdot(inv, hdot(off, inv))
    return inv


@jax.custom_vjp
def _tri_inv(low):
    return _tri_inv_raw(low)


def _tri_inv_fwd(low):
    inv = _tri_inv_raw(low)
    return inv, inv


def _tri_inv_bwd(inv, ct):
    return (-_dot3(_dot3(inv, ct, _TN), inv, _NT),)


_tri_inv.defvjp(_tri_inv_fwd, _tri_inv_bwd)


@jax.custom_vjp
def _tri_inv_given(low, inv):
    return inv


_tri_inv_given.defvjp(lambda low, inv: (inv, inv),
                      lambda inv, ct: (_tri_inv_bwd(inv, ct)[0], jnp.zeros_like(inv)))

GROUP = 4
N_GROUPS = HEADS // GROUP
GROWS = GROUP * CHUNK


def _gdn_group(q, k, v, beta, gc, gr, states, inv=None):
    n = q.shape[0]
    i = lax.broadcasted_iota(jnp.int32, (n, n), 0)
    j = lax.broadcasted_iota(jnp.int32, (n, n), 1)
    same = (i // CHUNK) == (j // CHUNK)
    incl, strict = same & (i >= j), same & (i > j)
    qs = q * (HEAD ** -0.5)
    decay = jnp.where(incl, jnp.exp(jnp.where(incl, gc - gr, 0.0)), 0.0)
    kb = k * beta
    eg = jnp.exp(gc)
    prod = _dot(jnp.concatenate([kb, qs], axis=0), k, "nt")
    low = jnp.where(strict, prod[:n] * decay, 0.0)
    attn = jnp.where(incl, prod[n:] * decay, 0.0)
    inv = _tri_inv(low) if inv is None else _tri_inv_given(low, inv)
    sol = _dot_hi(inv, jnp.concatenate([v * beta, kb * eg], axis=1), "nn")
    u, w, qg = sol[:, :HEAD], sol[:, HEAD:], qs * eg
    last = lax.broadcasted_iota(jnp.int32, (CHUNK, 1), 0) == CHUNK - 1
    v_new, o_state, carry = [], [], []
    for h, state in enumerate(states):
        rows = slice(h * CHUNK, (h + 1) * CHUNK)
        ws = _dot(jnp.concatenate([w[rows], qg[rows]], axis=0), state, "nn")
        v_new.append(u[rows] - ws[:CHUNK])
        o_state.append(ws[CHUNK:])
        g_last = jnp.sum(jnp.where(last, gc[rows], 0.0), axis=0, keepdims=True)
        carry.append((g_last, k[rows] * jnp.exp(g_last - gc[rows])))
    o = jnp.concatenate(o_state, axis=0) + _dot(attn, jnp.concatenate(v_new, axis=0), "nn")
    new = tuple(state * jnp.exp(g_last) + _dot(k_dec, vn, "tn")
                for state, (g_last, k_dec), vn in zip(states, carry, v_new))
    return o, new, inv


def _gdn_specs(s, rev):
    nc = s // CHUNK
    at = (lambda n: nc - 1 - n) if rev else (lambda n: n)
    return nc, at, [
        pl.BlockSpec((CHUNK, D), lambda n: (at(n), 0)), pl.BlockSpec((CHUNK, D), lambda n: (at(n), 1)),
        pl.BlockSpec((CHUNK, D), lambda n: (at(n), 2)), pl.BlockSpec((CHUNK, HEAD), lambda n: (at(n), 0)),
        pl.BlockSpec((CHUNK, HEAD), lambda n: (at(n), 0)),
        pl.BlockSpec((None, N_GROUPS, 1, GROWS), lambda n: (at(n), 0, 0, 0))]


def _group_operands(grp, q_ref, k_ref, v_ref, b_blk, gc_blk, gr_blk):
    heads = range(grp * GROUP, (grp + 1) * GROUP)
    stack = lambda ref: jnp.concatenate([ref[:, h * HEAD:(h + 1) * HEAD] for h in heads], axis=0)
    col = lambda blk: jnp.concatenate([blk[:, h:h + 1] for h in heads], axis=0)
    return stack(q_ref), stack(k_ref), stack(v_ref), col(b_blk), col(gc_blk), gr_blk[grp]


def _gdn_scan_fwd(qkv, beta, gcum, grow, name):
    s = qkv.shape[0]
    nc, _, in_specs = _gdn_specs(s, rev=False)

    def body(q_ref, k_ref, v_ref, b_ref, gc_ref, gr_ref, o_ref, st_ref, inv_ref, state):
        @pl.when(pl.program_id(0) == 0)
        def _():
            state[...] = jnp.zeros_like(state)

        b_blk, gc_blk, gr_blk = b_ref[...], gc_ref[...], gr_ref[...]
        old = [state[h] for h in range(HEADS)]
        res = [_gdn_group(*_group_operands(grp, q_ref, k_ref, v_ref, b_blk, gc_blk, gr_blk),
                          old[grp * GROUP:(grp + 1) * GROUP]) for grp in range(N_GROUPS)]
        for grp, (o, new, inv) in enumerate(res):
            inv_ref[grp] = inv
            for hh in range(GROUP):
                h = grp * GROUP + hh
                st_ref[h] = old[h]
                o_ref[:, h * HEAD:(h + 1) * HEAD] = o[hh * CHUNK:(hh + 1) * CHUNK]
                state[h] = new[hh]

    return pl.pallas_call(
        body, name=name, grid=(nc,), in_specs=in_specs,
        out_specs=[pl.BlockSpec((CHUNK, D), lambda n: (n, 0)),
                   pl.BlockSpec((None, HEADS, HEAD, HEAD), lambda n: (n, 0, 0, 0)),
                   pl.BlockSpec((None, N_GROUPS, GROWS, GROWS), lambda n: (n, 0, 0, 0))],
        out_shape=[jax.ShapeDtypeStruct((s, D), F32), jax.ShapeDtypeStruct((nc, HEADS, HEAD, HEAD), F32),
                   jax.ShapeDtypeStruct((nc, N_GROUPS, GROWS, GROWS), F32)],
        scratch_shapes=[pltpu.VMEM((HEADS, HEAD, HEAD), F32)],
        compiler_params=_params(("arbitrary",)),
    )(qkv, qkv, qkv, beta, gcum, grow)


def _gdn_scan_bwd(qkv, beta, gcum, grow, states, invs, do, name):
    s = qkv.shape[0]
    nc, at, in_specs = _gdn_specs(s, rev=True)
    in_specs += [pl.BlockSpec((None, HEADS, HEAD, HEAD), lambda n: (at(n), 0, 0, 0)),
                 pl.BlockSpec((None, N_GROUPS, GROWS, GROWS), lambda n: (at(n), 0, 0, 0)),
                 pl.BlockSpec((CHUNK, D), lambda n: (at(n), 0))]

    def body(q_ref, k_ref, v_ref, b_ref, gc_ref, gr_ref, st_ref, inv_ref, do_ref, dq_ref, dk_ref, dv_ref, db_ref,
             dgc_ref, dgr_ref, dstate):
        @pl.when(pl.program_id(0) == 0)
        def _():
            dstate[...] = jnp.zeros_like(dstate)

        b_blk, gc_blk, gr_blk = b_ref[...], gc_ref[...], gr_ref[...]
        dold = [dstate[h] for h in range(HEADS)]
        res = []
        for grp in range(N_GROUPS):
            heads = range(grp * GROUP, (grp + 1) * GROUP)
            inv = inv_ref[grp]
            _, vjp = jax.vjp(lambda q, k, v, b, gc, gr, *st, inv=inv: _gdn_group(q, k, v, b, gc, gr, st, inv)[:2],
                             *_group_operands(grp, q_ref, k_ref, v_ref, b_blk, gc_blk, gr_blk),
                             *[st_ref[h] for h in heads])
            d_out = jnp.concatenate([do_ref[:, h * HEAD:(h + 1) * HEAD] for h in heads], axis=0)
            res.append(vjp((d_out, tuple(dold[h] for h in heads))))
        lane = lax.broadcasted_iota(jnp.int32, (CHUNK, HEAD), 1)
        db_all = jnp.zeros((CHUNK, HEAD), F32)
        dgc_all = jnp.zeros((CHUNK, HEAD), F32)
        for grp, (dq, dk, dv, db, dgc, dgr, *dst) in enumerate(res):
            dgr_ref[grp] = dgr
            for hh in range(GROUP):
                h = grp * GROUP + hh
                cs, rows = slice(h * HEAD, (h + 1) * HEAD), slice(hh * CHUNK, (hh + 1) * CHUNK)
                dq_ref[:, cs] = dq[rows]
                dk_ref[:, cs] = dk[rows]
                dv_ref[:, cs] = dv[rows]
                dstate[h] = dst[hh]
                db_all = jnp.where(lane == h, db[rows], db_all)
                dgc_all = jnp.where(lane == h, dgc[rows], dgc_all)
        db_ref[...] = db_all
        dgc_ref[...] = dgc_all

    blk = pl.BlockSpec((CHUNK, D), lambda n: (at(n), 0))
    gblk = pl.BlockSpec((CHUNK, HEAD), lambda n: (at(n), 0))
    return pl.pallas_call(
        body, name=name, grid=(nc,), in_specs=in_specs,
        out_specs=[blk, blk, blk, gblk, gblk, pl.BlockSpec((None, N_GROUPS, 1, GROWS), lambda n: (at(n), 0, 0, 0))],
        out_shape=[jax.ShapeDtypeStruct((s, D), F32)] * 3 + [jax.ShapeDtypeStruct((s, HEAD), F32)] * 2
        + [jax.ShapeDtypeStruct((nc, N_GROUPS, 1, GROWS), F32)],
        scratch_shapes=[pltpu.VMEM((HEADS, HEAD, HEAD), F32)],
        compiler_params=_params(("arbitrary",)),
    )(qkv, qkv, qkv, beta, gcum, grow, states, invs, do)


ATT_TILE = 512
ATT_SCALE = QK_HEAD ** -0.5


def _att_mask(t):
    qpos = lax.broadcasted_iota(jnp.int32, (t, t), 0)
    kpos = lax.broadcasted_iota(jnp.int32, (t, t), 1)
    return (kpos // CHUNK) <= (qpos // CHUNK)


ATT_STRIP = 32


def _att_strip_mask(r, t):
    kpos = lax.broadcasted_iota(jnp.int32, (ATT_STRIP, t), 1)
    return (kpos // CHUNK) <= (r * ATT_STRIP) // CHUNK


def _att_pairs(nb, by_query):
    if by_query:
        pairs = [(i, j) for i in range(nb) for j in range(i + 1)]
    else:
        pairs = [(j, i) for j in range(nb) for i in range(j, nb)]
    return jnp.array([a for a, _ in pairs], jnp.int32), jnp.array([b for _, b in pairs], jnp.int32)


def _attn_fwd(q, k, v, name):
    s = q.shape[0]
    t = min(ATT_TILE, s)
    nb = s // t
    ii, jj = _att_pairs(nb, by_query=True)

    def body(ii_ref, jj_ref, q_ref, k_ref, v_ref, o_ref, lse_ref, m_s, l_s, acc):
        step = pl.program_id(1)
        i, j = ii_ref[step], jj_ref[step]

        @pl.when(j == 0)
        def _():
            m_s[...] = jnp.full_like(m_s, -jnp.inf)
            l_s[...] = jnp.zeros_like(l_s)
            acc[...] = jnp.zeros_like(acc)

        sc = lax.dot_general(q_ref[...], k_ref[...], _NT, preferred_element_type=F32) * ATT_SCALE
        sc = lax.cond(i == j, lambda u: jnp.where(_att_mask(t), u, -jnp.inf), lambda u: u, sc)
        m_new = jnp.maximum(m_s[...], jnp.max(sc, axis=-1, keepdims=True))
        alpha = jnp.exp(m_s[...] - m_new)
        p = jnp.exp(sc - m_new)
        l_s[...] = alpha * l_s[...] + jnp.sum(p, axis=-1, keepdims=True)
        acc[...] = alpha * acc[...] + jnp.dot(p.astype(BF16), v_ref[...], preferred_element_type=F32)
        m_s[...] = m_new

        @pl.when(j == i)
        def _():
            o_ref[...] = acc[...] / l_s[...]
            lse_ref[...] = m_s[...] + jnp.log(l_s[...])

    grid_spec = pltpu.PrefetchScalarGridSpec(
        num_scalar_prefetch=2, grid=(HEADS, len(ii)),
        in_specs=[pl.BlockSpec((t, HEAD_PAD), lambda h, n, ir, jr: (ir[n], h)),
                  pl.BlockSpec((t, HEAD_PAD), lambda h, n, ir, jr: (jr[n], h)),
                  pl.BlockSpec((t, HEAD), lambda h, n, ir, jr: (jr[n], h))],
        out_specs=[pl.BlockSpec((t, HEAD), lambda h, n, ir, jr: (ir[n], h)),
                   pl.BlockSpec((None, t, 1), lambda h, n, ir, jr: (h, ir[n], 0))],
        scratch_shapes=[pltpu.VMEM((t, 1), F32), pltpu.VMEM((t, 1), F32), pltpu.VMEM((t, HEAD), F32)])
    return pl.pallas_call(
        body, name=name, grid_spec=grid_spec,
        out_shape=[jax.ShapeDtypeStruct((s, HEADS * HEAD), F32), jax.ShapeDtypeStruct((HEADS, s, 1), F32)],
        compiler_params=_params(("parallel", "arbitrary")),
    )(ii, jj, q, k, v)


def _attn_bwd(q, k, v, do, o, lse, name, dkv_sum=None):
    s = q.shape[0]
    t = min(ATT_TILE, s)
    nb = s // t
    jj, ii = _att_pairs(nb, by_query=False)
    nsum = 2 if dkv_sum else 0

    def body(jj_ref, ii_ref, q_ref, k_ref, v_ref, do_ref, o_ref, lse_ref, *rest):
        dq_ref, dk_ref, dv_ref, dk_acc, dv_acc, sc_s, dp_s, p_s, ds_s, dl_s = rest[nsum:]
        step = pl.program_id(1)
        i, j = ii_ref[step], jj_ref[step]

        @pl.when(step == 0)
        def _():
            dq_ref[...] = jnp.zeros_like(dq_ref)

        @pl.when(i == j)
        def _():
            dk_acc[...] = jnp.zeros_like(dk_acc)
            dv_acc[...] = jnp.zeros_like(dv_acc)

        do_f = do_ref[...]
        dob = do_f.astype(BF16)
        dl_s[...] = jnp.sum(do_f * o_ref[...], axis=-1, keepdims=True)
        sc_s[...] = lax.dot_general(q_ref[...], k_ref[...], _NT, preferred_element_type=F32)
        dp_s[...] = lax.dot_general(dob, v_ref[...], _NT, preferred_element_type=F32)

        def softmax_strips(diagonal):
            for r in range(t // ATT_STRIP):
                rows = slice(r * ATT_STRIP, (r + 1) * ATT_STRIP)
                p = jnp.exp(sc_s[rows, :] * ATT_SCALE - lse_ref[rows, :])
                if diagonal:
                    p = jnp.where(_att_strip_mask(r, t), p, 0.0)
                p_s[rows, :] = p.astype(BF16)
                ds_s[rows, :] = (p * (dp_s[rows, :] - dl_s[rows, :]) * ATT_SCALE).astype(BF16)

        pl.when(i == j)(functools.partial(softmax_strips, True))
        pl.when(i != j)(functools.partial(softmax_strips, False))
        ds = ds_s[...]
        dv_acc[...] += lax.dot_general(p_s[...], dob, _TN, preferred_element_type=F32)
        dk_acc[...] += lax.dot_general(ds, q_ref[...], _TN, preferred_element_type=F32)
        rows = pl.ds(pl.multiple_of(i * t, t), t)
        dq_ref[rows, :] += jnp.dot(ds, k_ref[...], preferred_element_type=F32)

        @pl.when(i == nb - 1)
        def _():
            dk_ref[...] = dk_acc[...] + rest[0][...] if nsum else dk_acc[...]
            dv_ref[...] = dv_acc[...] + rest[1][...] if nsum else dv_acc[...]

    dk_blk = pl.BlockSpec((t, HEAD_PAD), lambda h, n, jr, ir: (jr[n], h))
    dv_blk = pl.BlockSpec((t, HEAD), lambda h, n, jr, ir: (jr[n], h))
    grid_spec = pltpu.PrefetchScalarGridSpec(
        num_scalar_prefetch=2, grid=(HEADS, len(jj)),
        in_specs=[pl.BlockSpec((t, HEAD_PAD), lambda h, n, jr, ir: (ir[n], h)),
                  pl.BlockSpec((t, HEAD_PAD), lambda h, n, jr, ir: (jr[n], h)),
                  pl.BlockSpec((t, HEAD), lambda h, n, jr, ir: (jr[n], h)),
                  pl.BlockSpec((t, HEAD), lambda h, n, jr, ir: (ir[n], h)),
                  pl.BlockSpec((t, HEAD), lambda h, n, jr, ir: (ir[n], h)),
                  pl.BlockSpec((None, t, 1), lambda h, n, jr, ir: (h, ir[n], 0))] + [dk_blk, dv_blk][:nsum],
        out_specs=[pl.BlockSpec((s, HEAD_PAD), lambda h, n, jr, ir: (0, h)), dk_blk, dv_blk],
        scratch_shapes=[pltpu.VMEM((t, HEAD_PAD), F32), pltpu.VMEM((t, HEAD), F32), pltpu.VMEM((t, t), F32),
                        pltpu.VMEM((t, t), F32), pltpu.VMEM((t, t), BF16), pltpu.VMEM((t, t), BF16),
                        pltpu.VMEM((t, 1), F32)])
    return pl.pallas_call(
        body, name=name, grid_spec=grid_spec,
        out_shape=[jax.ShapeDtypeStruct((s, HEADS * HEAD_PAD), F32)] * 2 + [jax.ShapeDtypeStruct((s, HEADS * HEAD), F32)],
        compiler_params=_params(("parallel", "arbitrary")),
    )(jj, ii, q, k, v, do, o, lse, *(dkv_sum or ()))


def _rope_tables(positions):
    half = ROPE // 2
    inv_freq = ROPE_BASE ** (-jnp.arange(half, dtype=F32) / half)
    ang = positions.astype(F32)[:, None] * inv_freq
    cos, sin = jnp.cos(ang), jnp.sin(ang)
    return jnp.concatenate([cos] * 4, axis=1), jnp.concatenate([-sin, sin] * 2, axis=1)


def _loss_and_grad(y, target, name):
    s = y.shape[0]
    ts = min(ROW_TILE, s)

    def body(y_ref, t_ref, dy_ref, l_ref):
        e = y_ref[...] - t_ref[...]
        dy_ref[...] = e * (1.0 / D)
        part = jnp.sum(jnp.sum(e * e, axis=-1, keepdims=True) * (0.5 / D), axis=0, keepdims=True)
        part = part * jnp.ones((1, 128), F32)

        @pl.when(pl.program_id(0) == 0)
        def _():
            l_ref[...] = part

        @pl.when(pl.program_id(0) > 0)
        def _():
            l_ref[...] += part

    return pl.pallas_call(
        body, name=name, grid=(s // ts,),
        in_specs=[pl.BlockSpec((ts, D), lambda i: (i, 0))] * 2,
        out_specs=[pl.BlockSpec((ts, D), lambda i: (i, 0)), pl.BlockSpec((1, 128), lambda i: (0, 0))],
        out_shape=[jax.ShapeDtypeStruct((s, D), F32), jax.ShapeDtypeStruct((1, 128), F32)],
        compiler_params=_params(("arbitrary",)),
    )(y, target)


ANY = pl.BlockSpec(memory_space=pl.ANY)


def _all_gather(shard, name):
    def body(x_ref, out_ref, send_sems, recv_sems, local_sem):
        x, y, c = lax.axis_index("x"), lax.axis_index("y"), lax.axis_index("c")
        me, sibling = (x, y, c), (x, y, 1 - c)
        chips = [(1 - x, y), (x, 1 - y), (1 - x, 1 - y)]

        def rows(px, py, pc):
            return out_ref.at[4 * px + 2 * py + pc]

        def copy(k, block, to, src=None):
            return pltpu.make_async_remote_copy(
                src_ref=rows(*block) if src is None else src, dst_ref=rows(*block),
                send_sem=send_sems.at[k], recv_sem=recv_sems.at[k], device_id=to, device_id_type=MESH)

        mine = pltpu.make_async_copy(x_ref, rows(*me), local_sem)
        mine.start()
        first = [copy(0, me, sibling, src=x_ref)]
        first += [copy(1 + j, me, (*chip, c), src=x_ref) for j, chip in enumerate(chips)]
        for cp in first:
            cp.start()
        passed = [copy(4 + j, (*chip, c), sibling) for j, chip in enumerate(chips)]
        for j, chip in enumerate(chips):
            copy(1 + j, (*chip, c), me).wait_recv()
            passed[j].start()
        copy(0, sibling, me).wait_recv()
        for j, chip in enumerate(chips):
            copy(4 + j, (*chip, 1 - c), me).wait_recv()
        for cp in first + passed:
            cp.wait_send()
        mine.wait()

    return pl.pallas_call(
        body, name=name, out_shape=jax.ShapeDtypeStruct((N_DEV,) + shard.shape, shard.dtype),
        in_specs=[ANY], out_specs=ANY,
        scratch_shapes=[pltpu.SemaphoreType.DMA((7,)), pltpu.SemaphoreType.DMA((7,)), pltpu.SemaphoreType.DMA],
    )(shard)


def _exchange(blocks, name):
    def body(x_ref, out_ref, send_sems, recv_sems, local_sem):
        x, y, c = lax.axis_index("x"), lax.axis_index("y"), lax.axis_index("c")
        me = 4 * x + 2 * y + c
        mine = pltpu.make_async_copy(x_ref.at[me], out_ref.at[me], local_sem)
        mine.start()
        copies = []
        for k in range(1, N_DEV):
            px = 1 - x if k & 4 else x
            py = 1 - y if k & 2 else y
            pc = 1 - c if k & 1 else c
            peer = 4 * px + 2 * py + pc
            cp = pltpu.make_async_remote_copy(
                src_ref=x_ref.at[peer], dst_ref=out_ref.at[me], send_sem=send_sems.at[k - 1],
                recv_sem=recv_sems.at[k - 1], device_id=(px, py, pc), device_id_type=MESH)
            cp.start()
            copies.append((cp, pltpu.make_async_remote_copy(
                src_ref=x_ref.at[peer], dst_ref=out_ref.at[peer], send_sem=send_sems.at[k - 1],
                recv_sem=recv_sems.at[k - 1], device_id=(px, py, pc), device_id_type=MESH)))
        for cp, landing in copies:
            landing.wait_recv()
        for cp, landing in copies:
            cp.wait_send()
        mine.wait()

    return pl.pallas_call(
        body, name=name, out_shape=jax.ShapeDtypeStruct(blocks.shape, blocks.dtype),
        in_specs=[ANY], out_specs=ANY,
        scratch_shapes=[pltpu.SemaphoreType.DMA((7,)), pltpu.SemaphoreType.DMA((7,)), pltpu.SemaphoreType.DMA],
    )(blocks)


HBM = pl.BlockSpec(memory_space=pltpu.HBM)
SEM = pl.BlockSpec(memory_space=pltpu.SEMAPHORE)
EFFECT = pltpu.SideEffectType.DATAFLOW_SIDE_EFFECTING


def _peers():
    x, y, c = lax.axis_index("x"), lax.axis_index("y"), lax.axis_index("c")
    peers = []
    for k in range(1, N_DEV):
        px = 1 - x if k & 4 else x
        py = 1 - y if k & 2 else y
        pc = 1 - c if k & 1 else c
        peers.append(((px, py, pc), 4 * px + 2 * py + pc))
    return 4 * x + 2 * y + c, peers


def _send_start(srcs, name, gather):
    n = len(srcs)
    lands = [((N_DEV,) + s.shape) if gather else s.shape for s in srcs]

    def body(*refs):
        src_refs, land_refs = refs[:n], refs[n:2 * n]
        send_sems, recv_sems, token = refs[2 * n], refs[2 * n + 1], refs[-1]
        me, peers = _peers()
        for i in range(n):
            for k, (dev, idx) in enumerate(peers):
                pltpu.make_async_remote_copy(
                    src_ref=src_refs[i] if gather else src_refs[i].at[idx], dst_ref=land_refs[i].at[me],
                    send_sem=send_sems.at[7 * i + k], recv_sem=recv_sems.at[7 * i + k], device_id=dev,
                    device_id_type=MESH).start()
        token[...] = jnp.zeros_like(token)

    res = pl.pallas_call(
        body, name=name,
        out_shape=(pltpu.SemaphoreType.DMA((7 * n,)), pltpu.SemaphoreType.DMA((7 * n,)),
                   *[pltpu.HBM(s.shape, s.dtype) for s in srcs],
                   *[pltpu.HBM(shape, s.dtype) for shape, s in zip(lands, srcs)],
                   jax.ShapeDtypeStruct((8, 128), F32)),
        in_specs=(HBM,) * (2 * n), out_specs=(SEM, SEM) + (HBM,) * (2 * n) + (pl.BlockSpec(memory_space=pltpu.VMEM),),
        input_output_aliases={i: 2 + i for i in range(2 * n)},
        compiler_params=pltpu.CompilerParams(has_side_effects=EFFECT),
    )(*[pltpu.with_memory_space_constraint(s, pltpu.HBM) for s in srcs],
      *[pltpu.with_memory_space_constraint(lax.empty(shape, s.dtype), pltpu.HBM) for shape, s in zip(lands, srcs)])
    return dict(sems=res[:2], srcs=res[2:2 + n], lands=res[2 + n:2 + 2 * n], token=res[-1])


def _send_wait(handle, after, name, gather):
    n = len(handle["srcs"])

    def body(*refs):
        src_refs, land_refs = refs[:n], refs[n:2 * n]
        send_sems, recv_sems = refs[2 * n], refs[2 * n + 1]
        me, peers = _peers()
        for i in range(n):
            for k, (dev, idx) in enumerate(peers):
                cp = pltpu.make_async_remote_copy(
                    src_ref=src_refs[i] if gather else src_refs[i].at[idx], dst_ref=land_refs[i].at[idx],
                    send_sem=send_sems.at[7 * i + k], recv_sem=recv_sems.at[7 * i + k], device_id=dev,
                    device_id_type=MESH)
                cp.wait_send()
                cp.wait_recv()

    both = list(handle["srcs"]) + list(handle["lands"])
    res = pl.pallas_call(
        body, name=name, out_shape=tuple(pltpu.HBM(t.shape, t.dtype) for t in both),
        in_specs=(HBM,) * (2 * n) + (SEM, SEM, pl.BlockSpec(memory_space=pl.ANY)), out_specs=(HBM,) * (2 * n),
        input_output_aliases={i: i for i in range(2 * n)},
        compiler_params=pltpu.CompilerParams(has_side_effects=EFFECT),
    )(*both, *handle["sems"], after)
    return res[:n], res[n:]


def _adamw(parts, w, m, v, name, tr=128):
    pieces = len(parts)
    n, r, wd = parts[0].shape
    tr = next((t for t in (tr, 64, 32, 16) if r % t == 0), r)
    nrt = r // tr

    def body(*refs):
        w_ref, m_ref, v_ref, g_ref, d_ref, nm_ref, nv_ref = refs[pieces:]

        def update(p_ref):
            g = p_ref[0].astype(F32)
            for k in range(1, n):
                g = g + p_ref[k].astype(F32)
            m_new = B1 * m_ref[...] + (1.0 - B1) * g
            v_new = B2 * v_ref[...] + (1.0 - B2) * (g * g)
            m_hat = m_new / (1.0 - B1 ** STEP)
            v_hat = v_new / (1.0 - B2 ** STEP)
            g_ref[...] = g
            d_ref[...] = -LR * (m_hat / (jnp.sqrt(v_hat) + ADAM_EPS) + WD * w_ref[...])
            nm_ref[...] = m_new
            nv_ref[...] = v_new

        for p in range(pieces):
            pl.when(pl.program_id(0) == p)(functools.partial(update, refs[p]))

    part_spec = lambda p: pl.BlockSpec((n, tr, wd), lambda l, i: (0, jnp.clip(i + (l - p) * nrt, 0, nrt - 1), 0))
    blk = pl.BlockSpec((tr, wd), lambda l, i: (l * nrt + i, 0))
    return pl.pallas_call(
        body, name=name, grid=(pieces, nrt),
        in_specs=[part_spec(p) for p in range(pieces)] + [blk, blk, blk],
        out_specs=[blk] * 4, out_shape=[jax.ShapeDtypeStruct((pieces * r, wd), F32)] * 4,
        compiler_params=_params(("arbitrary", "arbitrary")),
    )(*parts, w, m, v)


def _outer8(ct, dm, name):
    k, n = ct.shape[0], dm.shape[1]

    def body(c_ref, d_ref, o_ref):
        cv, dv = c_ref[...], d_ref[...]
        acc = cv[:, 0:1] * dv[0:1, :]
        for s in range(1, N_DEV):
            acc = acc + cv[:, s:s + 1] * dv[s:s + 1, :]
        o_ref[...] = acc

    tk = 256
    return pl.pallas_call(
        body, name=name, grid=(k // tk,),
        in_specs=[pl.BlockSpec((tk, N_DEV), lambda i: (i, 0)), pl.BlockSpec((N_DEV, n), lambda i: (0, 0))],
        out_specs=pl.BlockSpec((tk, n), lambda i: (i, 0)), out_shape=jax.ShapeDtypeStruct((k, n), F32),
        compiler_params=_params(("parallel",)),
    )(ct, dm)


FULL = (0, D)
C128 = (0, 128)
HEAD_NOPE = [(h * HEAD_PAD, NOPE) for h in range(HEADS)]
HEAD_ROPE = [(h * HEAD_PAD + NOPE, 128) for h in range(HEADS)]
HEAD_ALL = [(h * HEAD_PAD, HEAD_PAD) for h in range(HEADS)]
HEAD_V = [(h * HEAD, HEAD) for h in range(HEADS)]


def _modulate(x, p, ties=()):
    return _rowwise_fwd(_modulate_fn, [(x, FULL)], [p["gain"], p["scale"], p["shift"]], [(D, BF16, FULL)], "modulate",
                        ties=ties)[0]


def _residual_bwd(y, gm, dxn):
    return _rowwise_bwd(_gate_only_fn, [(y, FULL)], [gm], [(D, F32, FULL)], [dxn], [(0, FULL)], [(D, BF16)],
                        "residual_bwd")


def _modulate_bwd(x, p, dh, dx_in, prev=None):
    pars = [p["gain"], p["scale"], p["shift"]]
    if prev is None:
        return list(_rowwise_bwd(_modulate_fn, [(x, FULL)], pars, [(D, BF16, FULL)], [dh], [(0, FULL)], [(D, F32)],
                                 "modulate_bwd", add={0: dx_in})) + [None]
    s = x.shape[0]
    ts = min(2 * ROW_TILE, s)

    def body(x_ref, g_ref, sc_ref, sh_ref, dh_ref, din_ref, y_ref, gm_ref, dx_ref, dy_ref, dg_ref, dsc_ref, dsh_ref,
             dgm_ref):
        _, vjp = jax.vjp(lambda *t: _modulate_fn(0, *t)[0], x_ref[...], g_ref[...], sc_ref[...], sh_ref[...])
        dxm, dg, dsc, dsh = vjp(dh_ref[...])
        dx = dxm + din_ref[...]
        dx_ref[...] = dx
        dy_ref[...] = (gm_ref[...] * dx).astype(BF16)
        sums = (dg, dsc, dsh, jnp.sum(dx * y_ref[...], axis=0, keepdims=True))
        first = pl.program_id(0) == 0
        for ref, val in zip((dg_ref, dsc_ref, dsh_ref, dgm_ref), sums):
            @pl.when(first)
            def _(ref=ref, val=val):
                ref[...] = val

            @pl.when(jnp.logical_not(first))
            def _(ref=ref, val=val):
                ref[...] += val

    blk = pl.BlockSpec((ts, D), lambda i: (i, 0))
    vec = pl.BlockSpec((1, D), lambda i: (0, 0))
    dx, dy, dg, dsc, dsh, dgm = pl.pallas_call(
        body, name="modulate_bwd_chain", grid=(s // ts,),
        in_specs=[blk, vec, vec, vec, blk, blk, blk, vec], out_specs=[blk, blk, vec, vec, vec, vec],
        out_shape=[jax.ShapeDtypeStruct((s, D), F32), jax.ShapeDtypeStruct((s, D), BF16)]
        + [jax.ShapeDtypeStruct((1, D), F32)] * 4,
        compiler_params=_params(("arbitrary",)),
    )(x, *pars, dh, dx_in, prev[0], prev[1])
    return [dx, dg, dsc, dsh, (dy, dgm)]


def _out_proj(a, w, x, p, nxt, name, **kw):
    res = _matmul([(a, w)], "nn", name, out_dtype=BF16, resid=(x, p["gm"]) + tuple(nxt or ()), **kw)
    return res[1], res[0], (res[2] if nxt else None)


def _ffn_fwd(x, p, ties=(), h=None, nxt=None):
    if h is None:
        h, ties = _modulate(x, p, ties), ()
    act_dgate, act_dup, act = _ffn_in(h, p["w_in"], "ffn_in", ties=ties)
    res = _ffn_out(act, p["wo"], (x, p["gm"]) + tuple(nxt or ()), "ffn_out")
    return res[1], dict(x=x, h=h, act_dgate=act_dgate, act_dup=act_dup, act=act, y=res[0]), (res[2] if nxt else None)


def _ffn_bwd(t, p, dxn, res=None, prev=None, ties=()):
    dy, dgm = res or _residual_bwd(t["y"], p["gm"], dxn)
    dgate, dup = _ffn_bwd_act(dy, p["wo"], t["act_dgate"], t["act_dup"], "ffn_bwd_act", ties=ties)
    dwo = _ffn_dwo(t["act"], dy, "ffn_dwo", ties=ties)
    dh = _ffn_dh(dgate, dup, p["w_in"], "ffn_dh")
    dwi = _ffn_dwi(t["h"], dgate, dup, "ffn_dwi")
    dx, dgain, dscale, dshift, res_prev = _modulate_bwd(t["x"], p, dh, dxn, prev)
    return dx, dict(gain=dgain, scale=dscale, shift=dshift, gm=dgm, w_in=dwi, wo=dwo), res_prev


def _pad128(t):
    return jnp.pad(t, ((0, 0), (0, 128 - t.shape[1])))


def _gdn_fwd(x, p, ties=(), h=None, nxt=None):
    s = x.shape[0]
    if h is None:
        h, ties = _modulate(x, p, ties), ()
    pm = _mm(h, p["w_main"], "nn", "gdn_proj", ties=ties)
    tail = _mm(h, p["w_tail"], "nn", "gdn_proj_tail", ties=ties)
    qkv = _gdn_conv_fwd(pm, p["conv_w"], "gdn_conv")
    beta, gcum = _rowwise_fwd(_gdn_gates_fn, [(tail, C128), (tail, (128, 128))], [p["a_log"], p["dt_bias"]],
                              [(128, F32, C128)] * 2, "gdn_gates")
    grow = gcum[:, :HEADS].reshape(s // CHUNK, CHUNK, N_GROUPS, GROUP).transpose(0, 2, 3, 1)
    grow = grow.reshape(s // CHUNK, N_GROUPS, 1, GROWS)
    o, states, invs = _gdn_scan_fwd(qkv, beta, gcum, grow, "gdn_scan")
    on, = _rowwise_fwd(_gdn_outnorm_fn, [(o, HEAD_V), (pm, [(3 * D + h_ * HEAD, HEAD) for h_ in range(HEADS)])],
                       [p["norm_g"]], [(D, BF16, HEAD_V)], "gdn_outnorm", groups=HEADS)
    xn, y, hn = _out_proj(on, p["w_out"], x, p, nxt, "mix_out", tm=512)
    t = dict(x=x, h=h, pm=pm, tail=tail, qkv=qkv, beta=beta, gcum=gcum, grow=grow, o=o, states=states, invs=invs,
             on=on, y=y)
    return xn, t, hn


def _gdn_bwd(t, p, dxn, res=None, prev=None, ties=()):
    s = dxn.shape[0]
    zc = [(3 * D + h_ * HEAD, HEAD) for h_ in range(HEADS)]
    dy, dgm = res or _residual_bwd(t["y"], p["gm"], dxn)
    dw_out = _mm(t["on"], dy, "tn", "mix_dwo", ties=ties)
    don = _mm(dy, p["w_out"], "nt", "mix_dout", ties=ties)
    do, dz, dnorm_g = _rowwise_bwd(_gdn_outnorm_fn, [(t["o"], HEAD_V), (t["pm"], zc)], [p["norm_g"]],
                                   [(D, BF16, HEAD_V)], [don], [(0, HEAD_V), (1, HEAD_V)], [(D, F32), (D, BF16)],
                                   "gdn_outnorm_bwd", groups=HEADS)
    dq, dk, dv, dbeta, dg, dgr = _gdn_scan_bwd(t["qkv"], t["beta"], t["gcum"], t["grow"], t["states"], t["invs"], do,
                                               "gdn_scan_bwd")
    dg = dg + _pad128(dgr.reshape(s // CHUNK, N_GROUPS, GROUP, CHUNK).transpose(0, 3, 1, 2).reshape(s, HEADS))
    dtail, da_log, ddt = _rowwise_bwd(_gdn_gates_fn, [(t["tail"], C128), (t["tail"], (128, 128))],
                                      [p["a_log"], p["dt_bias"]], [(128, F32, C128)] * 2, [dbeta, dg],
                                      [(0, C128), (0, (128, 128))], [(256, F32)], "gdn_gates_bwd")
    dxs, dcw = [], []
    for part, d in enumerate((dq, dk, dv)):
        dx_, dw_ = _gdn_conv_bwd(t["pm"], p["conv_w"], d, part, "gdn_conv_bwd")
        dxs.append(dx_)
        dcw.append(dw_)
    pieces = dxs + [dz]
    dh = _matmul([(d, p["w_main"]) for d in pieces] + [(dtail, p["w_tail"])], "nt", "gdn_dh",
                 boffs=[0, D, 2 * D, 3 * D, 0], tk=512)
    dw_main = [_mm(t["h"], d, "tn", "gdn_dwi") for d in pieces]
    dw_tail = _mm(t["h"], dtail, "tn", "gdn_dwi_tail")
    dx, dgain, dscale, dshift, res_prev = _modulate_bwd(t["x"], p, dh, dxn, prev)
    return dx, dict(gain=dgain, scale=dscale, shift=dshift, gm=dgm, w_main=jnp.concatenate(dw_main, axis=1),
                    w_tail=dw_tail, conv_w=jnp.concatenate(dcw, axis=1), a_log=da_log, dt_bias=ddt,
                    norm_g=dnorm_g, w_out=dw_out), res_prev


def _q_rows(q2, cosf, sins):
    return [(q2, HEAD_NOPE), (q2, HEAD_ROPE), (cosf, C128), (sins, C128)]


def _mla_fwd(x, p, kv, ties=(), h=None, nxt=None):
    if h is None:
        h, ties = _modulate(x, p, ties), ()
    cq = _mm(h, p["w_dq"], "nn", "mla_dq", ties=ties)
    cqn, = _rowwise_fwd(_rms_fn, [(cq, (0, Q_LORA))], [p["q_lora_g"]], [(Q_LORA, BF16, (0, Q_LORA))], "mla_qlora_norm")
    q2 = _mm(cqn, p["w_uq"], "nn", "mla_uq")
    qn, = _rowwise_fwd(_q_norm_rope_fn, _q_rows(q2, kv["cosf"], kv["sins"]), [p["q_gn"], p["q_gr"]],
                       [(HEADS * HEAD_PAD, BF16, HEAD_ALL)], "mla_q_norm", groups=HEADS)
    o, lse = _attn_fwd(qn, kv["kn"], kv["vb"], "mla_attn")
    xn, y, hn = _out_proj(o, p["w_out"], x, p, nxt, "mix_out", tm=512)
    return xn, dict(x=x, h=h, cq=cq, cqn=cqn, q2=q2, qn=qn, o=o, lse=lse, y=y), hn


def _mla_bwd(t, p, kv, dxn, res=None, prev=None, ties=(), dkv_sum=None):
    dy, dgm = res or _residual_bwd(t["y"], p["gm"], dxn)
    dw_out = _mm(t["o"], dy, "tn", "mix_dwo", ties=ties)
    do = _mm(dy, p["w_out"], "nt", "mix_dout", ties=ties)
    dq, dk, dv = _attn_bwd(t["qn"], kv["kn"], kv["vb"], do, t["o"], t["lse"], "mla_attn_bwd", dkv_sum)
    dq2, dq_gn, dq_gr = _rowwise_bwd(_q_norm_rope_fn, _q_rows(t["q2"], kv["cosf"], kv["sins"]), [p["q_gn"], p["q_gr"]],
                                     [(HEADS * HEAD_PAD, BF16, HEAD_ALL)], [dq],
                                     [(0, HEAD_NOPE), (0, HEAD_ROPE), None, None], [(HEADS * HEAD_PAD, BF16)],
                                     "mla_q_norm_bwd", groups=HEADS)
    dw_uq = _mm(t["cqn"], dq2, "tn", "mla_dwuq")
    dcqn = _mm(dq2, p["w_uq"], "nt", "mla_dcq")
    dcq, dq_lora_g = _rowwise_bwd(_rms_fn, [(t["cq"], (0, Q_LORA))], [p["q_lora_g"]], [(Q_LORA, BF16, (0, Q_LORA))],
                                  [dcqn], [(0, (0, Q_LORA))], [(Q_LORA, BF16)], "mla_qlora_norm_bwd")
    dw_dq = _mm(t["h"], dcq, "tn", "mla_dwdq")
    dh = _mm(dcq, p["w_dq"], "nt", "mla_dh")
    dx, dgain, dscale, dshift, res_prev = _modulate_bwd(t["x"], p, dh, dxn, prev)
    grads = dict(gain=dgain, scale=dscale, shift=dshift, gm=dgm, w_dq=dw_dq, q_lora_g=dq_lora_g, w_uq=dw_uq,
                 q_gn=dq_gn, q_gr=dq_gr, w_out=dw_out)
    return dx, grads, res_prev, dk, dv


def _k_rows(kvp, ckv, cosf, sins):
    return [(kvp, HEAD_NOPE), (kvp, HEAD_ROPE), (ckv, (KV_LORA, 128)), (cosf, C128), (sins, C128)]


def _kv_fwd(x, p, cosf, sins):
    h = _modulate(x, p)
    ckv = _mm(h, p["w_dkv"], "nn", "kv_down")
    lat, = _rowwise_fwd(_rms_fn, [(ckv, (0, KV_LORA))], [p["kv_g"]], [(KV_LORA, BF16, (0, KV_LORA))], "kv_norm")
    kvp = _mm(lat, p["w_ukv"], "nn", "kv_up")
    kn, vb = _rowwise_fwd(_k_norm_rope_fn, _k_rows(kvp, ckv, cosf, sins), [p["k_gn"], p["k_gr"]],
                          [(HEADS * HEAD_PAD, BF16, HEAD_ALL), (HEADS * HEAD, BF16, HEAD_V)], "kv_k_norm",
                          groups=HEADS)
    return dict(x=x, h=h, ckv=ckv, lat=lat, kvp=kvp, kn=kn, vb=vb, cosf=cosf, sins=sins)


def _kv_bwd(t, p, dk, dv, dx_in, prev):
    dkvp, drope, dk_gn, dk_gr = _rowwise_bwd(
        _k_norm_rope_fn, _k_rows(t["kvp"], t["ckv"], t["cosf"], t["sins"]), [p["k_gn"], p["k_gr"]],
        [(HEADS * HEAD_PAD, BF16, HEAD_ALL), (HEADS * HEAD, BF16, HEAD_V)], [dk, dv],
        [(0, HEAD_NOPE), (0, HEAD_ROPE), (1, C128), None, None], [(HEADS * HEAD_PAD, BF16), (128, F32)],
        "kv_k_norm_bwd", groups=HEADS)
    dw_ukv = _mm(t["lat"], dkvp, "tn", "kv_dwukv")
    dlat = _mm(dkvp, p["w_ukv"], "nt", "kv_dlat")
    dckv, dkv_g = _rowwise_bwd(_rms_fn, [(t["ckv"], (0, KV_LORA))], [p["kv_g"]], [(KV_LORA, BF16, (0, KV_LORA))],
                               [dlat], [(0, (0, KV_LORA))], [(KV_LORA, F32)], "kv_norm_bwd")
    dw_dkv = jnp.concatenate([_mm(t["h"], dckv, "tn", "kv_dwdkv"), _mm(t["h"], drope, "tn", "kv_dwdkv_rope")], axis=1)
    dh = _matmul([(dckv, p["w_dkv"]), (drope, p["w_dkv"])], "nt", "kv_dh", boffs=[0, KV_LORA])
    dx, dgain, dscale, dshift, res_prev = _modulate_bwd(t["x"], p, dh, dx_in, prev)
    return dx, dict(gain=dgain, scale=dscale, shift=dshift, w_dkv=dw_dkv, kv_g=dkv_g, w_ukv=dw_ukv, k_gn=dk_gn,
                    k_gr=dk_gr), res_prev


WEIGHTS = ["ada_w", "ada_b", "norm_g", "ffn_w_in", "ffn_w_out", "gdn_w_in", "gdn_conv_w", "gdn_a_log", "gdn_dt_bias",
           "gdn_norm_g", "gdn_w_out", "kv_ada_w", "kv_ada_b", "kv_norm_g", "mla_w_dkv", "mla_kv_norm_g", "mla_w_ukv",
           "mla_k_norm_g", "mla_w_dq", "mla_q_lora_norm_g", "mla_w_uq", "mla_q_norm_g", "mla_w_out"]
SMALL = [("ada_b", 4 * N_MOD * D), ("kv_ada_b", 2 * D), ("norm_g", DEPTH * 3 * D), ("gdn_conv_w", N_A * CONV_K * 3 * D),
         ("gdn_a_log", N_A * HEADS), ("gdn_dt_bias", N_A * HEADS), ("gdn_norm_g", N_A * HEAD), ("kv_norm_g", D),
         ("mla_kv_norm_g", KV_LORA), ("mla_k_norm_g", QK_HEAD), ("mla_q_lora_norm_g", 2 * Q_LORA),
         ("mla_q_norm_g", 2 * QK_HEAD)]
SMALL_REPLICATED = [n for n, _ in SMALL if n not in ("norm_g", "gdn_conv_w")]


def _silu_fn(g, t):
    return (_silu(t),)


def _dup_rope(t):
    return jnp.concatenate([t[..., :NOPE], t[..., NOPE:], t[..., NOPE:]], axis=-1)


def _fold_rope(t):
    return jnp.concatenate([t[..., :NOPE], t[..., NOPE:QK_HEAD] + t[..., QK_HEAD:]], axis=-1)


def _pack(pieces, rows):
    flat = jnp.concatenate([p.reshape(-1).astype(F32) for p in pieces])
    return jnp.pad(flat, (0, rows * 128 - flat.shape[0])).reshape(rows, 128)


def _step(a):
    me = 4 * lax.axis_index("x") + 2 * lax.axis_index("y") + lax.axis_index("c")
    x = a["x"][0]
    cosf, sins = _rope_tables(a["positions"][0])

    n_gdn = (4 * D + 2 * HEADS) // N_DEV
    AHEAD = 2

    stages = [(l, part) for l in range(DEPTH) for part in range(3)]

    def stage_shards(l, part):
        if part != 1:
            sh = {"ffn_w_in": a["ffn_w_in"][l, part // 2], "ffn_w_out": a["ffn_w_out"][l, part // 2]}
            if part == 2 and l == N_A - 1:
                sh.update(mla_w_dkv=a["mla_w_dkv"], mla_w_ukv=a["mla_w_ukv"])
            return sh
        if l < N_A:
            return {"gdn_w_in": a["gdn_w_in"][l], "gdn_w_out": a["gdn_w_out"][l]}
        j = l - N_A
        return {"mla_w_dq": a["mla_w_dq"][j], "mla_w_uq": a["mla_w_uq"][j], "mla_w_out": a["mla_w_out"][j]}

    def zero_of(t):
        return jnp.minimum(jnp.abs(t[(0,) * t.ndim].astype(F32)), 0.0)

    def start_stage(l, part, tie):
        sh = stage_shards(l, part)
        return list(sh), _send_start([(w + tie).astype(BF16) for w in sh.values()], f"fetch_start_{l}_{part}", gather=True)

    def finish_stage(l, part, names, handle, after):
        srcs, lands = _send_wait(handle, after, f"fetch_wait_{l}_{part}", gather=True)
        return {n: lax.dynamic_update_slice(land, src[None], (me, 0, 0)) for n, src, land in zip(names, srcs, lands)}

    n_cw, n_ng = N_A * CONV_K * 3 * HEAD, DEPTH * 3 * HEAD
    small_all = _all_gather(_pack([a["gdn_conv_w"], a["norm_g"], a["c"]], 44), "gather_small").reshape(N_DEV, -1)
    conv_w = small_all[:, :n_cw].reshape(N_DEV, N_A, CONV_K, 3 * HEAD).transpose(1, 2, 0, 3).reshape(N_A, CONV_K, 3 * D)
    norm_g = small_all[:, n_cw:n_cw + n_ng].reshape(N_DEV, DEPTH, 3, HEAD).transpose(1, 2, 0, 3).reshape(DEPTH, 3, D)
    c_all = small_all[:, n_cw + n_ng:n_cw + n_ng + D]

    c_act, = _rowwise_fwd(_silu_fn, [(c_all, FULL)], [], [(D, F32, FULL)], "c_act")
    n_ada = N_MOD * D // N_DEV
    parts = [_mm(c_act, a["ada_w"][l], "nn", "mod_proj") for l in range(DEPTH)]
    parts.append(_mm(c_act, a["kv_ada_w"], "nn", "mod_proj_kv"))
    mod_recv = _exchange(jnp.concatenate(parts, axis=1)[:, None, :], "exchange_mod")[:, 0]
    mod = mod_recv[:, :DEPTH * n_ada].reshape(N_DEV, DEPTH, n_ada).transpose(1, 0, 2).reshape(DEPTH, N_MOD * D)
    mod = (mod + a["ada_b"]).reshape(DEPTH, N_MOD, D)
    kvmod = mod_recv[:, DEPTH * n_ada:].reshape(2 * D) + a["kv_ada_b"]

    def row(v):
        return v[None]

    def ffn_params(l, i, w):
        k = 0 if i == 0 else 6
        return dict(gain=row(norm_g[l, 0 if i == 0 else 2]), shift=row(mod[l, k]), scale=row(mod[l, k + 1]),
                    gm=0.5 * row(mod[l, k + 2]), w_in=w["ffn_w_in"], wo=w["ffn_w_out"].reshape(D_FF, D))

    def gdn_params(l, w):
        w_in = w["gdn_w_in"].transpose(1, 0, 2).reshape(D, 4 * D + 2 * HEADS)
        pad = lambda t: jnp.pad(t, ((0, 0), (0, 128 - HEADS)))
        return dict(gain=row(norm_g[l, 1]), shift=row(mod[l, 3]), scale=row(mod[l, 4]), gm=row(mod[l, 5]),
                    w_main=w_in[:, :4 * D],
                    w_tail=jnp.concatenate([pad(w_in[:, 4 * D:4 * D + HEADS]), pad(w_in[:, 4 * D + HEADS:])], axis=1),
                    conv_w=conv_w[l], a_log=_pad128(row(a["gdn_a_log"][l])), dt_bias=_pad128(row(a["gdn_dt_bias"][l])),
                    norm_g=row(a["gdn_norm_g"][l]), w_out=w["gdn_w_out"].reshape(D, D))

    def mla_params(l, w):
        j = l - N_A
        uq = w["mla_w_uq"].transpose(1, 0, 2)
        qg = _dup_rope(a["mla_q_norm_g"][j])
        return dict(gain=row(norm_g[l, 1]), shift=row(mod[l, 3]), scale=row(mod[l, 4]), gm=row(mod[l, 5]),
                    w_dq=w["mla_w_dq"].reshape(D, Q_LORA), q_lora_g=row(a["mla_q_lora_norm_g"][j]),
                    w_uq=_dup_rope(uq).reshape(Q_LORA, HEADS * HEAD_PAD), q_gn=row(qg[:NOPE]), q_gr=row(qg[NOPE:]),
                    w_out=w["mla_w_out"].reshape(D, D))

    def kv_params(w):
        w_dkv = w["mla_w_dkv"].reshape(D, KV_LORA + ROPE)
        kg = _dup_rope(a["mla_k_norm_g"])
        return dict(gain=row(a["kv_norm_g"]), shift=row(kvmod[:D]), scale=row(kvmod[D:]),
                    w_dkv=jnp.concatenate([w_dkv, w_dkv[:, KV_LORA:]], axis=1), kv_g=row(a["mla_kv_norm_g"]),
                    w_ukv=w["mla_w_ukv"].transpose(1, 0, 2).reshape(KV_LORA, HEADS * 2 * HEAD), k_gn=row(kg[:NOPE]),
                    k_gr=row(kg[NOPE:]))

    tapes, kv, kv_p, h = [[] for _ in range(DEPTH)], None, None, None
    first = {name: _all_gather((w + zero_of(mod)).astype(BF16), "fetch_first_" + name)
             for name, w in stage_shards(0, 0).items()}
    pending = []
    for l, part in stages[1:1 + AHEAD]:
        tie = pending[-1][1]["token"][0, 0] if pending else zero_of(first["ffn_w_out"])
        pending.append(start_stage(l, part, tie))
    for n, (l, part) in enumerate(stages):
        if n == 0:
            w, ties = first, tuple(h["token"] for _, h in pending)
        else:
            names, handle = pending.pop(0)
            w = finish_stage(l, part, names, handle, x)
            ties = ()
            if n + AHEAD < len(stages):
                pending.append(start_stage(*stages[n + AHEAD], zero_of(w[names[0]])))
                ties = (pending[-1][1]["token"],)
        nxt = None
        if n + 1 < len(stages):
            l2, part2 = stages[n + 1]
            k2 = 3 * part2
            nxt = (row(norm_g[l2, part2]), row(mod[l2, k2 + 1]), row(mod[l2, k2]))
        if part != 1:
            p = ffn_params(l, part // 2, w)
            x, t, h = _ffn_fwd(x, p, ties, h, nxt)
        else:
            p = gdn_params(l, w) if l < N_A else mla_params(l, w)
            x, t, h = _gdn_fwd(x, p, ties, h, nxt) if l < N_A else _mla_fwd(x, p, kv, ties, h, nxt)
        tapes[l] += [p, t]
        if part == 2 and l == N_A - 1:
            kv_p = kv_params(w)
            kv = _kv_fwd(x, kv_p, cosf, sins)
    dx, loss_blk = _loss_and_grad(x, a["loss_target"][0], "loss")
    loss = lax.psum(loss_blk[0, 0], ("x", "y", "c"))

    def by_cols(g, n):
        return g.reshape(g.shape[0], -1, n).transpose(1, 0, 2)

    def ffn_blocks(g):
        return {"ffn_w_in": g["w_in"], "ffn_w_out": g["wo"].reshape(N_DEV, D_FF // N_DEV, D)}

    def mixer_blocks(l, g):
        if l < N_A:
            full = jnp.concatenate([g["w_main"], g["w_tail"][:, :HEADS], g["w_tail"][:, 128:128 + HEADS]], axis=1)
            return {"gdn_w_in": by_cols(full, n_gdn), "gdn_w_out": g["w_out"].reshape(N_DEV, D // N_DEV, D)}
        return {"mla_w_dq": g["w_dq"].reshape(N_DEV, D // N_DEV, Q_LORA),
                "mla_w_uq": _fold_rope(g["w_uq"].reshape(Q_LORA, HEADS, HEAD_PAD)).transpose(1, 0, 2),
                "mla_w_out": g["w_out"].reshape(N_DEV, D // N_DEV, D)}

    sent = []

    def send(key, blocks, tie=0.0):
        handle = _send_start([(b + tie).astype(BF16) for b in blocks.values()], "grad_start_" + "_".join(map(str, key)),
                             gather=False)
        sent.append((key, list(blocks), handle))
        return (handle["token"],)

    grads = [None] * DEPTH
    dk_sum = dv_sum = kv_grads = res = None
    ties = ()
    for l in reversed(range(DEPTH)):
        p1, t1, pm_, tm_, p2, t2 = tapes[l]
        if l == N_A - 1:
            dx, kv_grads, res = _kv_bwd(kv, kv_p, dk_sum, dv_sum, dx, (t2["y"], p2["gm"]))
            d_dkv = kv_grads["w_dkv"]
            ties += send((l, 3), {
                "mla_w_dkv": jnp.concatenate(
                    [d_dkv[:, :KV_LORA], d_dkv[:, KV_LORA:KV_LORA + ROPE] + d_dkv[:, KV_LORA + ROPE:]],
                    axis=1).reshape(N_DEV, D // N_DEV, KV_LORA + ROPE),
                "mla_w_ukv": by_cols(kv_grads["w_ukv"], 2 * HEAD)})
        dx, g2, res = _ffn_bwd(t2, p2, dx, res, (tm_["y"], pm_["gm"]), ties)
        ties = send((l, 2), ffn_blocks(g2))
        if l < N_A:
            dx, gm_, res = _gdn_bwd(tm_, pm_, dx, res, (t1["y"], p1["gm"]), ties)
        else:
            dx, gm_, res, dk_sum, dv_sum = _mla_bwd(tm_, pm_, kv, dx, res, (t1["y"], p1["gm"]), ties,
                                                    None if dk_sum is None else (dk_sum, dv_sum))
        ties = send((l, 1), mixer_blocks(l, gm_))
        prev = (tapes[l - 1][5]["y"], tapes[l - 1][4]["gm"]) if l > 0 and l != N_A else None
        dx, g1, res = _ffn_bwd(t1, p1, dx, res, prev, ties)
        if l > 0:
            ties = send((l, 0), ffn_blocks(g1))
        grads[l] = (g1, gm_, g2)

    out = {}
    def dmod(l):
        g1, gm_, g2 = grads[l]
        return jnp.concatenate([g1["shift"], g1["scale"], 0.5 * g1["gm"], gm_["shift"], gm_["scale"], gm_["gm"],
                                g2["shift"], g2["scale"], 0.5 * g2["gm"]], axis=1)

    gdn = [grads[l][1] for l in range(N_A)]
    mla = [grads[l][1] for l in range(N_A, DEPTH)]
    small = {
        "ada_b": jnp.concatenate([dmod(l) for l in range(DEPTH)], axis=0),
        "kv_ada_b": jnp.concatenate([kv_grads["shift"], kv_grads["scale"]], axis=1),
        "norm_g": jnp.stack([jnp.concatenate([grads[l][0]["gain"], grads[l][1]["gain"], grads[l][2]["gain"]], axis=0)
                             for l in range(DEPTH)]),
        "gdn_conv_w": jnp.stack([g["conv_w"] for g in gdn]),
        "gdn_a_log": jnp.stack([g["a_log"][0, :HEADS] for g in gdn]),
        "gdn_dt_bias": jnp.stack([g["dt_bias"][0, :HEADS] for g in gdn]),
        "gdn_norm_g": jnp.stack([g["norm_g"][0] for g in gdn]),
        "kv_norm_g": kv_grads["gain"],
        "mla_kv_norm_g": kv_grads["kv_g"],
        "mla_k_norm_g": _fold_rope(jnp.concatenate([kv_grads["k_gn"], kv_grads["k_gr"]], axis=1)),
        "mla_q_lora_norm_g": jnp.stack([g["q_lora_g"][0] for g in mla]),
        "mla_q_norm_g": jnp.stack([_fold_rope(jnp.concatenate([g["q_gn"], g["q_gr"]], axis=1))[0] for g in mla]),
    }
    rows = 616
    assert sum(n for _, n in SMALL) <= rows * 128 and all(small[n].size == k for n, k in SMALL)
    small_recv = _all_gather(_pack([small[n] for n, _ in SMALL], rows), "gather_small_grads")
    small_recv = small_recv + send((0, 0), ffn_blocks(grads[0][0]), zero_of(small_recv))[0][0, 0]
    zero = lambda n, k: jnp.zeros((k,), F32)
    packed = {pre: _pack([a[pre + n] if n in SMALL_REPLICATED else zero(n, k) for n, k in SMALL], rows)
              for pre in ("", "m_", "v_")}
    res = _adamw([small_recv], packed[""], packed["m_"], packed["v_"], "adamw_small")
    offs = {}
    o = 0
    for n, k in SMALL:
        offs[n] = o
        o += k
    for n, k in SMALL:
        if n in SMALL_REPLICATED:
            out[n] = [r.reshape(-1)[offs[n]:offs[n] + k] for r in res]
    gsum = res[0].reshape(-1)
    g_norm = lax.dynamic_slice_in_dim(gsum[offs["norm_g"]:offs["norm_g"] + DEPTH * 3 * D].reshape(DEPTH * 3, D),
                                      me * HEAD, HEAD, axis=1)
    g_conv = lax.dynamic_slice_in_dim(
        gsum[offs["gdn_conv_w"]:offs["gdn_conv_w"] + N_A * CONV_K * 3 * D].reshape(N_A * CONV_K, 3 * D),
        me * 3 * HEAD, 3 * HEAD, axis=1)
    res2 = _adamw([_pack([g_norm, g_conv], 36)[None]], *[_pack([a[pre + "norm_g"], a[pre + "gdn_conv_w"]], 36)
                                                      for pre in ("", "m_", "v_")], "adamw_small")
    out["norm_g"] = [r.reshape(-1)[:n_ng] for r in res2]
    out["gdn_conv_w"] = [r.reshape(-1)[n_ng:n_ng + n_cw] for r in res2]

    c_act_t = c_act.T
    all_small = small_recv.reshape(N_DEV, -1)
    dmod_all = all_small[:, :DEPTH * N_MOD * D].reshape(N_DEV, DEPTH, N_MOD * D)
    dmod_mine = lax.dynamic_slice_in_dim(dmod_all, me * n_ada, n_ada, axis=2)
    g_ada = [_outer8(c_act_t, dmod_mine[:, l], "ada_grad")[None] for l in range(DEPTH)]
    out["ada_w"] = _adamw(g_ada, *[a[pre + "ada_w"].reshape(DEPTH * D, n_ada) for pre in ("", "m_", "v_")], "adamw")
    dkv_all = all_small[:, offs["kv_ada_b"]:offs["kv_ada_b"] + 2 * D]
    g_kv = _outer8(c_act_t, lax.dynamic_slice_in_dim(dkv_all, me * (2 * D // N_DEV), 2 * D // N_DEV, axis=1), "ada_grad")
    out["kv_ada_w"] = _adamw([g_kv[None]], *[a[pre + "kv_ada_w"] for pre in ("", "m_", "v_")], "adamw")

    pieces = {}
    for key, names, handle in sent:
        srcs, lands = _send_wait(handle, out["kv_ada_w"][0], "grad_wait_" + "_".join(map(str, key)), gather=False)
        for name, src, land in zip(names, srcs, lands):
            own = lax.dynamic_slice_in_dim(src, me, 1, axis=0)
            pieces.setdefault(name, []).append((key, lax.dynamic_update_slice(land, own, (me, 0, 0))))
    for name, parts in pieces.items():
        wide = a[name].shape[-1]
        out[name] = _adamw([p for _, p in sorted(parts, key=lambda kp: kp[0])], a[name].reshape(-1, wide),
                           a["m_" + name].reshape(-1, wide), a["v_" + name].reshape(-1, wide), "adamw")

    result = [loss, dx[None]]
    for k in range(4):
        result += [out[n][k].reshape(a[n].shape) for n in WEIGHTS]
    return tuple(result)


def kernel(x, c, positions, ada_w, ada_b, norm_g, ffn_w_in, ffn_w_out, gdn_w_in, gdn_conv_w, gdn_a_log, gdn_dt_bias, gdn_norm_g, gdn_w_out, kv_ada_w, kv_ada_b, kv_norm_g, mla_w_dkv, mla_kv_norm_g, mla_w_ukv, mla_k_norm_g, mla_w_dq, mla_q_lora_norm_g, mla_w_uq, mla_q_norm_g, mla_w_out, loss_target, m_ada_w, m_ada_b, m_norm_g, m_ffn_w_in, m_ffn_w_out, m_gdn_w_in, m_gdn_conv_w, m_gdn_a_log, m_gdn_dt_bias, m_gdn_norm_g, m_gdn_w_out, m_kv_ada_w, m_kv_ada_b, m_kv_norm_g, m_mla_w_dkv, m_mla_kv_norm_g, m_mla_w_ukv, m_mla_k_norm_g, m_mla_w_dq, m_mla_q_lora_norm_g, m_mla_w_uq, m_mla_q_norm_g, m_mla_w_out, v_ada_w, v_ada_b, v_norm_g, v_ffn_w_in, v_ffn_w_out, v_gdn_w_in, v_gdn_conv_w, v_gdn_a_log, v_gdn_dt_bias, v_gdn_norm_g, v_gdn_w_out, v_kv_ada_w, v_kv_ada_b, v_kv_norm_g, v_mla_w_dkv, v_mla_kv_norm_g, v_mla_w_ukv, v_mla_k_norm_g, v_mla_w_dq, v_mla_q_lora_norm_g, v_mla_w_uq, v_mla_q_norm_g, v_mla_w_out):
    return _step(dict(locals()))
```

```python
import functools

import jax
import jax.numpy as jnp
from jax import lax
from jax.experimental import pallas as pl
from jax.experimental.pallas import tpu as pltpu

F32 = jnp.float32
BF16 = jnp.bfloat16

N_DEV = 8
D = 1024
D_FF = 2816
DEPTH = 4
N_A = 2
N_MOD = 9
HEADS = 8
HEAD = 128
CHUNK = 64
CONV_K = 4
KV_LORA = 256
Q_LORA = 384
NOPE = 128
ROPE = 64
QK_HEAD = NOPE + ROPE
HEAD_PAD = 256
ROPE_BASE = 10000.0
EPS = 1e-6
LR, B1, B2, ADAM_EPS, WD, STEP = 0.001, 0.9, 0.999, 1e-08, 0.01, 10

VMEM_LIMIT = 48 * 1024 * 1024
ROW_TILE = 256
MESH = pl.DeviceIdType.MESH

_NN = (((1,), (0,)), ((), ()))
_NT = (((1,), (1,)), ((), ()))
_TN = (((0,), (0,)), ((), ()))
_DIMS = {"nn": _NN, "nt": _NT, "tn": _TN}


def _params(dims=None):
    return pltpu.CompilerParams(dimension_semantics=dims, vmem_limit_bytes=VMEM_LIMIT)


def _tile(n, target):
    for t in range(target - target % 128, 0, -128):
        if n % t == 0:
            return t
    return n


_TIE_SPEC1 = pl.BlockSpec((8, 128), lambda i: (0, 0))
_TIE_SPEC2 = pl.BlockSpec((8, 128), lambda i, j: (0, 0))
_TIE_SPEC3 = pl.BlockSpec((8, 128), lambda i, j, k: (0, 0))


def _matmul(pairs, form, name, out_dtype=F32, tm=1408, tn=1408, tk=1408, boffs=None, resid=None, ties=()):
    a0, b0 = pairs[0]
    if form == "nn":
        m, n = a0.shape[0], b0.shape[1]
        ks = [a.shape[1] for a, _ in pairs]
    elif form == "nt":
        m, n = a0.shape[0], b0.shape[0]
        ks = [a.shape[1] for a, _ in pairs]
    else:
        m, n = a0.shape[1], b0.shape[1]
        ks = [a.shape[0] for a, _ in pairs]
    tm, tn = _tile(m, tm), _tile(n, tn)
    tks = [_tile(k, tk) for k in ks]
    boffs = boffs or [0] * len(pairs)
    assert m % tm == 0 and n % tn == 0 and all(o % t == 0 for o, t in zip(boffs, tks)), (name, m, n, ks)
    steps = [k // t for k, t in zip(ks, tks)]
    starts = [sum(steps[:p]) for p in range(len(pairs))]
    nk = sum(steps)

    def kidx(p, k):
        return jnp.clip(k - starts[p], 0, steps[p] - 1)

    in_specs, args = [], []
    for p, (a, b) in enumerate(pairs):
        t = tks[p]
        if form == "tn":
            in_specs.append(pl.BlockSpec((t, tm), lambda i, j, k, p=p: (kidx(p, k), i)))
            in_specs.append(pl.BlockSpec((t, tn), lambda i, j, k, p=p: (kidx(p, k), j)))
        elif form == "nn":
            in_specs.append(pl.BlockSpec((tm, t), lambda i, j, k, p=p: (i, kidx(p, k))))
            in_specs.append(pl.BlockSpec((t, tn), lambda i, j, k, p=p: (kidx(p, k), j)))
        else:
            in_specs.append(pl.BlockSpec((tm, t), lambda i, j, k, p=p: (i, kidx(p, k))))
            in_specs.append(pl.BlockSpec((tn, t), lambda i, j, k, p=p, o=boffs[p] // t: (j, kidx(p, k) + o)))
        args += [a, b]
    dims = _DIMS[form]
    npairs = len(pairs)
    nres = len(resid or ())
    nin = 2 * npairs + len(ties) + nres
    out_blk = pl.BlockSpec((tm, tn), lambda i, j, k: (i, j))
    in_specs += [_TIE_SPEC3] * len(ties)
    args += list(ties)
    if resid:
        assert nres == 2 or (nres == 5 and tn == n)
        in_specs += [out_blk] + [pl.BlockSpec((1, tn), lambda i, j, k: (0, j))] * (nres - 1)
        args += list(resid)

    def body(*refs):
        o_ref = refs[nin]
        k = pl.program_id(2)

        def prod(p):
            return lax.dot_general(refs[2 * p][...].astype(BF16), refs[2 * p + 1][...].astype(BF16), dims,
                                   preferred_element_type=F32)

        def finish(y):
            o_ref[...] = y.astype(o_ref.dtype)
            if resid:
                x_ref, gate_ref = refs[nin - nres], refs[nin - nres + 1]
                xn = x_ref[...] + gate_ref[...] * y
                refs[nin + 1][...] = xn
                if nres == 5:
                    gain, scale, shift = (r[...] for r in refs[nin - 3:nin])
                    refs[nin + 2][...] = _modulate_fn(0, xn, gain, scale, shift)[0].astype(BF16)

        if nk == 1:
            finish(prod(0))
            return
        acc = refs[-1]

        @pl.when(k == 0)
        def _():
            acc[...] = jnp.zeros_like(acc)

        for p in range(npairs):
            @pl.when((k >= starts[p]) & (k < starts[p] + steps[p]))
            def _(p=p):
                acc[...] += prod(p)

        @pl.when(k == nk - 1)
        def _():
            finish(acc[...])

    res = pl.pallas_call(
        body, name=name, grid=(m // tm, n // tn, nk), in_specs=in_specs,
        out_specs=[out_blk] * (2 + (nres == 5)) if resid else out_blk,
        out_shape=([jax.ShapeDtypeStruct((m, n), out_dtype), jax.ShapeDtypeStruct((m, n), F32)]
                   + [jax.ShapeDtypeStruct((m, n), BF16)] * (nres == 5))
        if resid else jax.ShapeDtypeStruct((m, n), out_dtype),
        scratch_shapes=[] if nk == 1 else [pltpu.VMEM((tm, tn), F32)],
        compiler_params=_params(("parallel", "parallel", "arbitrary")),
    )(*args)
    return res


def _mm(a, b, form, name, **kw):
    return _matmul([(a, b)], form, name, **kw)


def _cols(spec, g):
    return spec[g] if isinstance(spec, list) else spec


def _rowwise_fwd(fn, rows, pars, outs, name, groups=1, ts=ROW_TILE, ties=()):
    s = rows[0][0].shape[0]
    ts = min(ts, s)
    assert s % ts == 0
    nr, npar = len(rows), len(pars)

    def body(*refs):
        par_t = [r[...] for r in refs[nr:nr + npar]]
        out_refs = refs[nr + npar + len(ties):]
        for g in range(groups):
            row_t = []
            for r, (_, spec) in zip(refs[:nr], rows):
                c0, w = _cols(spec, g)
                row_t.append(r[:, c0:c0 + w].astype(F32))
            res = fn(g, *row_t, *par_t)
            for o_ref, val, (_, _, spec) in zip(out_refs, res, outs):
                c0, w = _cols(spec, g)
                o_ref[:, c0:c0 + w] = val.astype(o_ref.dtype)

    return pl.pallas_call(
        body, name=name, grid=(s // ts,),
        in_specs=[pl.BlockSpec((ts, a.shape[1]), lambda i: (i, 0)) for a, _ in rows]
        + [pl.BlockSpec(p.shape, lambda i: (0, 0)) for p in pars] + [_TIE_SPEC1] * len(ties),
        out_specs=[pl.BlockSpec((ts, w), lambda i: (i, 0)) for w, _, _ in outs],
        out_shape=[jax.ShapeDtypeStruct((s, w), dt) for w, dt, _ in outs],
        compiler_params=_params(("parallel",)),
    )(*[a for a, _ in rows], *pars, *ties)


def _rowwise_bwd(fn, rows, pars, outs, douts, gmap, gshapes, name, groups=1, add=None, par_grads=True,
                 ts=ROW_TILE):
    s = rows[0][0].shape[0]
    ts = min(ts, s)
    assert s % ts == 0
    nr, npar, nout, ng = len(rows), len(pars), len(outs), len(gshapes)
    add = add or {}
    add_keys = sorted(add)

    def body(*refs):
        row_refs = refs[:nr]
        par_refs = refs[nr:nr + npar]
        dout_refs = refs[nr + npar:nr + npar + nout]
        add_refs = refs[nr + npar + nout:nr + npar + nout + len(add_keys)]
        g_refs = refs[nr + npar + nout + len(add_keys):][:ng]
        pg_refs = refs[nr + npar + nout + len(add_keys) + ng:]
        par_t = [r[...] for r in par_refs]
        par_acc = [None] * npar
        shared_acc = {}
        for g in range(groups):
            row_t = []
            for r, (_, spec) in zip(row_refs, rows):
                c0, w = _cols(spec, g)
                row_t.append(r[:, c0:c0 + w].astype(F32))
            cts = []
            for r, (_, _, spec) in zip(dout_refs, outs):
                c0, w = _cols(spec, g)
                cts.append(r[:, c0:c0 + w].astype(F32))
            _, vjp = jax.vjp(lambda *t, g=g: tuple(fn(g, *t)), *row_t, *par_t)
            grads = vjp(tuple(cts))
            for k in range(nr):
                if gmap[k] is None:
                    continue
                gi, spec = gmap[k]
                if isinstance(spec, list) or groups == 1:
                    c0, w = _cols(spec, g)
                    val = grads[k]
                    if gi in add:
                        val = val + add_refs[add_keys.index(gi)][:, c0:c0 + w].astype(F32)
                    g_refs[gi][:, c0:c0 + w] = val.astype(g_refs[gi].dtype)
                else:
                    shared_acc[k] = grads[k] if k not in shared_acc else shared_acc[k] + grads[k]
            if par_grads:
                for k in range(npar):
                    pg = grads[nr + k]
                    par_acc[k] = pg if par_acc[k] is None else par_acc[k] + pg
        for k, val in shared_acc.items():
            gi, (c0, w) = gmap[k]
            assert gi not in add
            g_refs[gi][:, c0:c0 + w] = val.astype(g_refs[gi].dtype)
        if par_grads:
            first = pl.program_id(0) == 0
            for k in range(npar):
                @pl.when(first)
                def _(k=k):
                    pg_refs[k][...] = par_acc[k]

                @pl.when(jnp.logical_not(first))
                def _(k=k):
                    pg_refs[k][...] += par_acc[k]

    out_specs = [pl.BlockSpec((ts, w), lambda i: (i, 0)) for w, _ in gshapes]
    out_shape = [jax.ShapeDtypeStruct((s, w), dt) for w, dt in gshapes]
    if par_grads:
        out_specs += [pl.BlockSpec(p.shape, lambda i: (0, 0)) for p in pars]
        out_shape += [jax.ShapeDtypeStruct(p.shape, F32) for p in pars]
    return pl.pallas_call(
        body, name=name, grid=(s // ts,),
        in_specs=[pl.BlockSpec((ts, a.shape[1]), lambda i: (i, 0)) for a, _ in rows]
        + [pl.BlockSpec(p.shape, lambda i: (0, 0)) for p in pars]
        + [pl.BlockSpec((ts, a.shape[1]), lambda i: (i, 0)) for a in douts]
        + [pl.BlockSpec((ts, add[k].shape[1]), lambda i: (i, 0)) for k in add_keys],
        out_specs=out_specs, out_shape=out_shape,
        compiler_params=_params(("arbitrary",)),
    )(*[a for a, _ in rows], *pars, *douts, *[add[k] for k in add_keys])


def _sigmoid(x):
    return 1.0 / (1.0 + jnp.exp(-x))


def _silu(x):
    return x * _sigmoid(x)


def _softplus(x):
    return jnp.maximum(x, 0.0) + jnp.log(1.0 + jnp.exp(-jnp.abs(x)))


def _rms(t, g, n=None):
    n = n or t.shape[-1]
    return t * lax.rsqrt(jnp.sum(t * t, axis=-1, keepdims=True) / n + EPS) * g


def _modulate_fn(g, x, gain, scale, shift):
    return (_rms(x, gain) * (1.0 + scale) + shift,)


def _gate_only_fn(g, y, gm):
    return (gm * y,)


def _gdn_gates_fn(g, b_logit, a_logit, a_log, dt_bias):
    gate = -jnp.exp(a_log) * _softplus(a_logit + dt_bias)
    n = gate.shape[0]
    i = lax.broadcasted_iota(jnp.int32, (n, n), 0)
    j = lax.broadcasted_iota(jnp.int32, (n, n), 1)
    tri = (((i // CHUNK) == (j // CHUNK)) & (i >= j)).astype(F32)
    gcum = lax.dot_general(tri, gate, _NN, preferred_element_type=F32, precision=lax.Precision.HIGHEST)
    return _sigmoid(b_logit), gcum


def _gdn_outnorm_fn(g, o, z, gain):
    return (_rms(o, gain) * _silu(z),)


def _rms_fn(g, t, gain):
    return (_rms(t, gain),)


@jax.custom_vjp
def _swap_halves(t):
    return pltpu.roll(t, 32, 1)


_swap_halves.defvjp(lambda t: (pltpu.roll(t, 32, 1), None), lambda _, ct: (pltpu.roll(ct, 96, 1),))


def _head_norm_rope_fn(g, nope, rope, cosf, sins, gain_n, gain_r):
    first = lax.broadcasted_iota(jnp.int32, rope.shape, 1) < ROPE
    ss = jnp.sum(nope * nope, axis=-1, keepdims=True) + jnp.sum(jnp.where(first, rope * rope, 0.0), axis=-1,
                                                                 keepdims=True)
    r = lax.rsqrt(ss / QK_HEAD + EPS)
    tn = nope * r * gain_n
    tr = rope * r * gain_r
    rot = jnp.where(first, tr * cosf + _swap_halves(tr) * sins, 0.0)
    return tn, rot


def _q_norm_rope_fn(g, nope, rope, cosf, sins, gain_n, gain_r):
    tn, rot = _head_norm_rope_fn(g, nope, rope, cosf, sins, gain_n, gain_r)
    return (jnp.concatenate([tn, rot], axis=1),)


def _k_norm_rope_fn(g, nope, val, rope, cosf, sins, gain_n, gain_r):
    tn, rot = _head_norm_rope_fn(g, nope, rope, cosf, sins, gain_n, gain_r)
    return jnp.concatenate([tn, rot], axis=1), val


FF_SH = 2 * D_FF // N_DEV
FF_G = N_DEV // 2


def _ffn_in(h, w_in, name, tm=1024, ties=()):
    s = h.shape[0]
    tm = min(tm, s)

    def body(h_ref, wg_ref, wu_ref, *rest):
        g_ref, u_ref, a_ref = rest[-3:]
        hb = h_ref[...]
        gate = jnp.dot(hb, wg_ref[...], preferred_element_type=F32)
        up = jnp.dot(hb, wu_ref[...], preferred_element_type=F32)
        sg = _sigmoid(gate)
        silu = gate * sg
        g_ref[...] = (up * (sg * (1.0 + gate * (1.0 - sg)))).astype(BF16)
        u_ref[...] = silu.astype(BF16)
        a_ref[...] = (silu * up).astype(BF16)

    spec = pl.BlockSpec((None, tm, FF_SH), lambda j, i: (j, i, 0))
    return pl.pallas_call(
        body, name=name, grid=(FF_G, s // tm),
        in_specs=[pl.BlockSpec((tm, D), lambda j, i: (i, 0)), pl.BlockSpec((None, D, FF_SH), lambda j, i: (j, 0, 0)),
                  pl.BlockSpec((None, D, FF_SH), lambda j, i: (j + FF_G, 0, 0))] + [_TIE_SPEC2] * len(ties),
        out_specs=[spec, spec, spec], out_shape=[jax.ShapeDtypeStruct((FF_G, s, FF_SH), BF16)] * 3,
        compiler_params=_params(("parallel", "parallel")),
    )(h, w_in, w_in, *ties)


def _ffn_out(act, wo, resid, name, tm=512):
    s = act.shape[1]
    tm = min(tm, s)
    nres = len(resid)

    def body(a_ref, b_ref, x_ref, gate_ref, *rest):
        mods, outs = rest[:nres - 2], rest[nres - 2:]
        y = jnp.dot(a_ref[0], b_ref[0:FF_SH, :], preferred_element_type=F32)
        for k in range(1, FF_G):
            y = y + jnp.dot(a_ref[k], b_ref[k * FF_SH:(k + 1) * FF_SH, :], preferred_element_type=F32)
        xn = x_ref[...] + gate_ref[...] * y
        outs[0][...] = y.astype(BF16)
        outs[1][...] = xn
        if mods:
            outs[2][...] = _modulate_fn(0, xn, *[m[...] for m in mods])[0].astype(BF16)

    blk = pl.BlockSpec((tm, D), lambda i: (i, 0))
    vec = pl.BlockSpec((1, D), lambda i: (0, 0))
    return pl.pallas_call(
        body, name=name, grid=(s // tm,),
        in_specs=[pl.BlockSpec((FF_G, tm, FF_SH), lambda i: (0, i, 0)), pl.BlockSpec((D_FF, D), lambda i: (0, 0)),
                  blk] + [vec] * (nres - 1),
        out_specs=[blk] * (2 + (nres == 5)),
        out_shape=[jax.ShapeDtypeStruct((s, D), BF16), jax.ShapeDtypeStruct((s, D), F32)]
        + [jax.ShapeDtypeStruct((s, D), BF16)] * (nres == 5),
        compiler_params=_params(("parallel",)),
    )(act, wo, *resid)


def _ffn_bwd_act(dy, wo, act_dgate, act_dup, name, tm=1024, ties=()):
    s = dy.shape[0]
    tm = min(tm, s)

    def body(dy_ref, wo_ref, g_ref, u_ref, *rest):
        dg_ref, du_ref = rest[-2:]
        dact = lax.dot_general(dy_ref[...], wo_ref[...], _NT, preferred_element_type=F32)
        dg_ref[...] = (dact * g_ref[...].astype(F32)).astype(BF16)
        du_ref[...] = (dact * u_ref[...].astype(F32)).astype(BF16)

    spec = pl.BlockSpec((None, tm, FF_SH), lambda j, i: (j, i, 0))
    return pl.pallas_call(
        body, name=name, grid=(FF_G, s // tm),
        in_specs=[pl.BlockSpec((tm, D), lambda j, i: (i, 0)), pl.BlockSpec((FF_SH, D), lambda j, i: (j, 0)), spec, spec]
        + [_TIE_SPEC2] * len(ties),
        out_specs=[spec, spec], out_shape=[jax.ShapeDtypeStruct((FF_G, s, FF_SH), BF16)] * 2,
        compiler_params=_params(("parallel", "parallel")),
    )(dy, wo, act_dgate, act_dup, *ties)


def _ffn_dwo(act, dy, name, tk=2048, ties=()):
    s = act.shape[1]
    tk = min(tk, s)

    def body(a_ref, b_ref, *rest):
        o_ref, acc = rest[-2:]
        k = pl.program_id(1)

        @pl.when(k == 0)
        def _():
            acc[...] = jnp.zeros_like(acc)

        acc[...] += lax.dot_general(a_ref[...], b_ref[...], _TN, preferred_element_type=F32)

        @pl.when(k == s // tk - 1)
        def _():
            o_ref[...] = acc[...].astype(BF16)

    return pl.pallas_call(
        body, name=name, grid=(FF_G, s // tk),
        in_specs=[pl.BlockSpec((None, tk, FF_SH), lambda j, k: (j, k, 0)), pl.BlockSpec((tk, D), lambda j, k: (k, 0))]
        + [_TIE_SPEC2] * len(ties),
        out_specs=pl.BlockSpec((FF_SH, D), lambda j, k: (j, 0)), out_shape=jax.ShapeDtypeStruct((D_FF, D), BF16),
        scratch_shapes=[pltpu.VMEM((FF_SH, D), F32)], compiler_params=_params(("parallel", "arbitrary")),
    )(act, dy, *ties)


def _ffn_halves(k, gate_ref, up_ref, fn):
    pl.when(k < FF_G)(functools.partial(fn, gate_ref))
    pl.when(k >= FF_G)(functools.partial(fn, up_ref))


def _ffn_dh(dgate, dup, w_in, name, tm=512, ties=()):
    s = dgate.shape[1]
    tm = min(tm, s)

    def body(dg_ref, du_ref, w_ref, *rest):
        acc = lax.dot_general(dg_ref[0], w_ref[0], _NT, preferred_element_type=F32)
        for k in range(1, N_DEV):
            d_ref = dg_ref if k < FF_G else du_ref
            acc = acc + lax.dot_general(d_ref[k % FF_G], w_ref[k], _NT, preferred_element_type=F32)
        rest[-1][...] = acc

    half = pl.BlockSpec((FF_G, tm, FF_SH), lambda i: (0, i, 0))
    return pl.pallas_call(
        body, name=name, grid=(s // tm,),
        in_specs=[half, half, pl.BlockSpec((N_DEV, D, FF_SH), lambda i: (0, 0, 0))] + [_TIE_SPEC1] * len(ties),
        out_specs=pl.BlockSpec((tm, D), lambda i: (i, 0)), out_shape=jax.ShapeDtypeStruct((s, D), F32),
        compiler_params=_params(("parallel",)),
    )(dgate, dup, w_in, *ties)


def _ffn_dwi(h, dgate, dup, name, tk=2048):
    s = h.shape[0]
    tk = min(tk, s)

    def body(h_ref, dg_ref, du_ref, o_ref, acc):
        j, k = pl.program_id(0), pl.program_id(1)

        @pl.when(k == 0)
        def _():
            acc[...] = jnp.zeros_like(acc)

        def add(d_ref):
            acc[...] += lax.dot_general(h_ref[...], d_ref[...], _TN, preferred_element_type=F32)

        _ffn_halves(j, dg_ref, du_ref, add)

        @pl.when(k == s // tk - 1)
        def _():
            o_ref[...] = acc[...].astype(BF16)

    return pl.pallas_call(
        body, name=name, grid=(N_DEV, s // tk),
        in_specs=[pl.BlockSpec((tk, D), lambda j, k: (k, 0)),
                  pl.BlockSpec((None, tk, FF_SH), lambda j, k: (jnp.minimum(j, FF_G - 1), jnp.where(j < FF_G, k, s // tk - 1), 0)),
                  pl.BlockSpec((None, tk, FF_SH), lambda j, k: (jnp.maximum(j - FF_G, 0), jnp.where(j < FF_G, 0, k), 0))],
        out_specs=pl.BlockSpec((None, D, FF_SH), lambda j, k: (j, 0, 0)),
        out_shape=jax.ShapeDtypeStruct((N_DEV, D, FF_SH), BF16),
        scratch_shapes=[pltpu.VMEM((D, FF_SH), F32)], compiler_params=_params(("parallel", "arbitrary")),
    )(h, dgate, dup)


def _shift_down(x, d):
    rows = lax.broadcasted_iota(jnp.int32, x.shape, 0)
    return jnp.where(rows >= d, pltpu.roll(x, d, 0), 0.0)


def _shift_up(x, d):
    n = x.shape[0]
    rows = lax.broadcasted_iota(jnp.int32, x.shape, 0)
    return jnp.where(rows < n - d, pltpu.roll(x, n - d, 0), 0.0)


def _conv_post(pre, is_qk):
    a = _silu(pre)
    l2 = a * lax.rsqrt(jnp.sum(a * a, axis=-1, keepdims=True) + EPS)
    return jnp.where(is_qk, l2, a)


def _conv_taps(x):
    return [_shift_down(x, CONV_K - 1 - j) for j in range(CONV_K - 1)] + [x]


def _conv_pre(x, w, taps=None):
    taps = taps or _conv_taps(x)
    pre = taps[0] * w[0:1, :]
    for j in range(1, CONV_K):
        pre = pre + taps[j] * w[j:j + 1, :]
    return pre


def _gdn_conv_fwd(pm, conv_w, name):
    s = pm.shape[0]
    nblk = 3 * D // HEAD

    def body(x_ref, w_ref, o_ref):
        is_qk = pl.program_id(0) < 2 * HEADS
        o_ref[...] = _conv_post(_conv_pre(x_ref[...], w_ref[...]), is_qk)

    return pl.pallas_call(
        body, name=name, grid=(nblk,),
        in_specs=[pl.BlockSpec((s, HEAD), lambda c: (0, c)), pl.BlockSpec((CONV_K, HEAD), lambda c: (0, c))],
        out_specs=pl.BlockSpec((s, HEAD), lambda c: (0, c)),
        out_shape=jax.ShapeDtypeStruct((s, 3 * D), F32), compiler_params=_params(("parallel",)),
    )(pm, conv_w)


def _gdn_conv_bwd(pm, conv_w, dout, part, name):
    s = pm.shape[0]
    off = part * HEADS

    def body(x_ref, w_ref, d_ref, dx_ref, dw_ref):
        x, w = x_ref[...], w_ref[...]
        taps = _conv_taps(x)
        _, vjp = jax.vjp(lambda p: _conv_post(p, part < 2), _conv_pre(x, w, taps))
        dpre, = vjp(d_ref[...])
        dx = dpre * w[CONV_K - 1:CONV_K, :]
        for j in range(CONV_K - 1):
            dx = dx + _shift_up(dpre, CONV_K - 1 - j) * w[j:j + 1, :]
        dx_ref[...] = dx.astype(BF16)
        dw_ref[...] = jnp.concatenate([jnp.sum(dpre * tap, axis=0, keepdims=True) for tap in taps], axis=0)

    return pl.pallas_call(
        body, name=name, grid=(HEADS,),
        in_specs=[pl.BlockSpec((s, HEAD), lambda c: (0, c + off)), pl.BlockSpec((CONV_K, HEAD), lambda c: (0, c + off)),
                  pl.BlockSpec((s, HEAD), lambda c: (0, c))],
        out_specs=[pl.BlockSpec((s, HEAD), lambda c: (0, c)), pl.BlockSpec((CONV_K, HEAD), lambda c: (0, c))],
        out_shape=[jax.ShapeDtypeStruct((s, D), BF16), jax.ShapeDtypeStruct((CONV_K, D), F32)],
        compiler_params=_params(("parallel",)),
    )(pm, conv_w, dout)


def _dot3(a, b, dims=_NN):
    ah, bh = a.astype(BF16), b.astype(BF16)
    al, bl = (a - ah.astype(F32)).astype(BF16), (b - bh.astype(F32)).astype(BF16)
    d = lambda u, v: lax.dot_general(u, v, dims, preferred_element_type=F32)
    return d(ah, bh) + (d(ah, bl) + d(al, bh))


def _make_dot(hi):
    def raw(a, b, dims):
        if hi:
            return _dot3(a, b, dims)
        return lax.dot_general(a.astype(BF16), b.astype(BF16), dims, preferred_element_type=F32)

    @functools.partial(jax.custom_vjp, nondiff_argnums=(2,))
    def dot(a, b, form):
        return raw(a, b, _DIMS[form])

    def fwd(a, b, form):
        return raw(a, b, _DIMS[form]), (a, b)

    def bwd(form, res, ct):
        a, b = res
        if form == "nn":
            return raw(ct, b, _NT), raw(a, ct, _TN)
        if form == "nt":
            return raw(ct, b, _NN), raw(ct, a, _TN)
        return raw(b, ct, _NT), raw(a, ct, _NN)

    dot.defvjp(fwd, bwd)
    return dot


_dot = _make_dot(False)
_dot_hi = _make_dot(True)


def _tri_inv_raw(low):
    n = low.shape[0]
    i = lax.broadcasted_iota(jnp.int32, (n, n), 0)
    j = lax.broadcasted_iota(jnp.int32, (n, n), 1)
    eye = (i == j).astype(F32)
    hdot = _dot3
    same16 = (i // 16) == (j // 16)
    neg = jnp.where(same16, -low, 0.0)
    inv = eye + neg
    power = neg
    for _ in range(3):
        power = hdot(power, power)
        inv = hdot(inv, eye + power)
    for blk in (32, 64):
        off = jnp.where(((i // blk) == (j // blk)) & ((i // (blk // 2)) != (j // (blk // 2))), low, 0.0)
        inv = inv - hdot(inv, hdot(off, inv))
    return inv


@jax.custom_vjp
def _tri_inv(low):
    return _tri_inv_raw(low)


def _tri_inv_fwd(low):
    inv = _tri_inv_raw(low)
    return inv, inv


def _tri_inv_bwd(inv, ct):
    return (-_dot3(_dot3(inv, ct, _TN), inv, _NT),)


_tri_inv.defvjp(_tri_inv_fwd, _tri_inv_bwd)


@jax.custom_vjp
def _tri_inv_given(low, inv):
    return inv


_tri_inv_given.defvjp(lambda low, inv: (inv, inv),
                      lambda inv, ct: (_tri_inv_bwd(inv, ct)[0], jnp.zeros_like(inv)))

GROUP = 4
N_GROUPS = HEADS // GROUP
GROWS = GROUP * CHUNK


def _gdn_group(q, k, v, beta, gc, gr, states, inv=None):
    n = q.shape[0]
    i = lax.broadcasted_iota(jnp.int32, (n, n), 0)
    j = lax.broadcasted_iota(jnp.int32, (n, n), 1)
    same = (i // CHUNK) == (j // CHUNK)
    incl, strict = same & (i >= j), same & (i > j)
    qs = q * (HEAD ** -0.5)
    decay = jnp.where(incl, jnp.exp(jnp.where(incl, gc - gr, 0.0)), 0.0)
    kb = k * beta
    eg = jnp.exp(gc)
    prod = _dot(jnp.concatenate([kb, qs], axis=0), k, "nt")
    low = jnp.where(strict, prod[:n] * decay, 0.0)
    attn = jnp.where(incl, prod[n:] * decay, 0.0)
    inv = _tri_inv(low) if inv is None else _tri_inv_given(low, inv)
    sol = _dot_hi(inv, jnp.concatenate([v * beta, kb * eg], axis=1), "nn")
    u, w, qg = sol[:, :HEAD], sol[:, HEAD:], qs * eg
    last = lax.broadcasted_iota(jnp.int32, (CHUNK, 1), 0) == CHUNK - 1
    v_new, o_state, carry = [], [], []
    for h, state in enumerate(states):
        rows = slice(h * CHUNK, (h + 1) * CHUNK)
        ws = _dot(jnp.concatenate([w[rows], qg[rows]], axis=0), state, "nn")
        v_new.append(u[rows] - ws[:CHUNK])
        o_state.append(ws[CHUNK:])
        g_last = jnp.sum(jnp.where(last, gc[rows], 0.0), axis=0, keepdims=True)
        carry.append((g_last, k[rows] * jnp.exp(g_last - gc[rows])))
    o = jnp.concatenate(o_state, axis=0) + _dot(attn, jnp.concatenate(v_new, axis=0), "nn")
    new = tuple(state * jnp.exp(g_last) + _dot(k_dec, vn, "tn")
                for state, (g_last, k_dec), vn in zip(states, carry, v_new))
    return o, new, inv


def _gdn_specs(s, rev):
    nc = s // CHUNK
    at = (lambda n: nc - 1 - n) if rev else (lambda n: n)
    return nc, at, [
        pl.BlockSpec((CHUNK, D), lambda n: (at(n), 0)), pl.BlockSpec((CHUNK, D), lambda n: (at(n), 1)),
        pl.BlockSpec((CHUNK, D), lambda n: (at(n), 2)), pl.BlockSpec((CHUNK, HEAD), lambda n: (at(n), 0)),
        pl.BlockSpec((CHUNK, HEAD), lambda n: (at(n), 0)),
        pl.BlockSpec((None, N_GROUPS, 1, GROWS), lambda n: (at(n), 0, 0, 0))]


def _group_operands(grp, q_ref, k_ref, v_ref, b_blk, gc_blk, gr_blk):
    heads = range(grp * GROUP, (grp + 1) * GROUP)
    stack = lambda ref: jnp.concatenate([ref[:, h * HEAD:(h + 1) * HEAD] for h in heads], axis=0)
    col = lambda blk: jnp.concatenate([blk[:, h:h + 1] for h in heads], axis=0)
    return stack(q_ref), stack(k_ref), stack(v_ref), col(b_blk), col(gc_blk), gr_blk[grp]


def _gdn_scan_fwd(qkv, beta, gcum, grow, name):
    s = qkv.shape[0]
    nc, _, in_specs = _gdn_specs(s, rev=False)

    def body(q_ref, k_ref, v_ref, b_ref, gc_ref, gr_ref, o_ref, st_ref, inv_ref, state):
        @pl.when(pl.program_id(0) == 0)
        def _():
            state[...] = jnp.zeros_like(state)

        b_blk, gc_blk, gr_blk = b_ref[...], gc_ref[...], gr_ref[...]
        old = [state[h] for h in range(HEADS)]
        res = [_gdn_group(*_group_operands(grp, q_ref, k_ref, v_ref, b_blk, gc_blk, gr_blk),
                          old[grp * GROUP:(grp + 1) * GROUP]) for grp in range(N_GROUPS)]
        for grp, (o, new, inv) in enumerate(res):
            inv_ref[grp] = inv
            for hh in range(GROUP):
                h = grp * GROUP + hh
                st_ref[h] = old[h]
                o_ref[:, h * HEAD:(h + 1) * HEAD] = o[hh * CHUNK:(hh + 1) * CHUNK]
                state[h] = new[hh]

    return pl.pallas_call(
        body, name=name, grid=(nc,), in_specs=in_specs,
        out_specs=[pl.BlockSpec((CHUNK, D), lambda n: (n, 0)),
                   pl.BlockSpec((None, HEADS, HEAD, HEAD), lambda n: (n, 0, 0, 0)),
                   pl.BlockSpec((None, N_GROUPS, GROWS, GROWS), lambda n: (n, 0, 0, 0))],
        out_shape=[jax.ShapeDtypeStruct((s, D), F32), jax.ShapeDtypeStruct((nc, HEADS, HEAD, HEAD), F32),
                   jax.ShapeDtypeStruct((nc, N_GROUPS, GROWS, GROWS), F32)],
        scratch_shapes=[pltpu.VMEM((HEADS, HEAD, HEAD), F32)],
        compiler_params=_params(("arbitrary",)),
    )(qkv, qkv, qkv, beta, gcum, grow)


def _gdn_scan_bwd(qkv, beta, gcum, grow, states, invs, do, name):
    s = qkv.shape[0]
    nc, at, in_specs = _gdn_specs(s, rev=True)
    in_specs += [pl.BlockSpec((None, HEADS, HEAD, HEAD), lambda n: (at(n), 0, 0, 0)),
                 pl.BlockSpec((None, N_GROUPS, GROWS, GROWS), lambda n: (at(n), 0, 0, 0)),
                 pl.BlockSpec((CHUNK, D), lambda n: (at(n), 0))]

    def body(q_ref, k_ref, v_ref, b_ref, gc_ref, gr_ref, st_ref, inv_ref, do_ref, dq_ref, dk_ref, dv_ref, db_ref,
             dgc_ref, dgr_ref, dstate):
        @pl.when(pl.program_id(0) == 0)
        def _():
            dstate[...] = jnp.zeros_like(dstate)

        b_blk, gc_blk, gr_blk = b_ref[...], gc_ref[...], gr_ref[...]
        dold = [dstate[h] for h in range(HEADS)]
        res = []
        for grp in range(N_GROUPS):
            heads = range(grp * GROUP, (grp + 1) * GROUP)
            inv = inv_ref[grp]
            _, vjp = jax.vjp(lambda q, k, v, b, gc, gr, *st, inv=inv: _gdn_group(q, k, v, b, gc, gr, st, inv)[:2],
                             *_group_operands(grp, q_ref, k_ref, v_ref, b_blk, gc_blk, gr_blk),
                             *[st_ref[h] for h in heads])
            d_out = jnp.concatenate([do_ref[:, h * HEAD:(h + 1) * HEAD] for h in heads], axis=0)
            res.append(vjp((d_out, tuple(dold[h] for h in heads))))
        lane = lax.broadcasted_iota(jnp.int32, (CHUNK, HEAD), 1)
        db_all = jnp.zeros((CHUNK, HEAD), F32)
        dgc_all = jnp.zeros((CHUNK, HEAD), F32)
        for grp, (dq, dk, dv, db, dgc, dgr, *dst) in enumerate(res):
            dgr_ref[grp] = dgr
            for hh in range(GROUP):
                h = grp * GROUP + hh
                cs, rows = slice(h * HEAD, (h + 1) * HEAD), slice(hh * CHUNK, (hh + 1) * CHUNK)
                dq_ref[:, cs] = dq[rows]
                dk_ref[:, cs] = dk[rows]
                dv_ref[:, cs] = dv[rows]
                dstate[h] = dst[hh]
                db_all = jnp.where(lane == h, db[rows], db_all)
                dgc_all = jnp.where(lane == h, dgc[rows], dgc_all)
        db_ref[...] = db_all
        dgc_ref[...] = dgc_all

    blk = pl.BlockSpec((CHUNK, D), lambda n: (at(n), 0))
    gblk = pl.BlockSpec((CHUNK, HEAD), lambda n: (at(n), 0))
    return pl.pallas_call(
        body, name=name, grid=(nc,), in_specs=in_specs,
        out_specs=[blk, blk, blk, gblk, gblk, pl.BlockSpec((None, N_GROUPS, 1, GROWS), lambda n: (at(n), 0, 0, 0))],
        out_shape=[jax.ShapeDtypeStruct((s, D), F32)] * 3 + [jax.ShapeDtypeStruct((s, HEAD), F32)] * 2
        + [jax.ShapeDtypeStruct((nc, N_GROUPS, 1, GROWS), F32)],
        scratch_shapes=[pltpu.VMEM((HEADS, HEAD, HEAD), F32)],
        compiler_params=_params(("arbitrary",)),
    )(qkv, qkv, qkv, beta, gcum, grow, states, invs, do)


ATT_TILE = 512
ATT_SCALE = QK_HEAD ** -0.5


def _att_mask(t):
    qpos = lax.broadcasted_iota(jnp.int32, (t, t), 0)
    kpos = lax.broadcasted_iota(jnp.int32, (t, t), 1)
    return (kpos // CHUNK) <= (qpos // CHUNK)


ATT_STRIP = 32


def _att_strip_mask(r, t):
    kpos = lax.broadcasted_iota(jnp.int32, (ATT_STRIP, t), 1)
    return (kpos // CHUNK) <= (r * ATT_STRIP) // CHUNK


def _att_pairs(nb, by_query):
    if by_query:
        pairs = [(i, j) for i in range(nb) for j in range(i + 1)]
    else:
        pairs = [(j, i) for j in range(nb) for i in range(j, nb)]
    return jnp.array([a for a, _ in pairs], jnp.int32), jnp.array([b for _, b in pairs], jnp.int32)


def _attn_fwd(q, k, v, name):
    s = q.shape[0]
    t = min(ATT_TILE, s)
    nb = s // t
    ii, jj = _att_pairs(nb, by_query=True)

    def body(ii_ref, jj_ref, q_ref, k_ref, v_ref, o_ref, lse_ref, m_s, l_s, acc):
        step = pl.program_id(1)
        i, j = ii_ref[step], jj_ref[step]

        @pl.when(j == 0)
        def _():
            m_s[...] = jnp.full_like(m_s, -jnp.inf)
            l_s[...] = jnp.zeros_like(l_s)
            acc[...] = jnp.zeros_like(acc)

        sc = lax.dot_general(q_ref[...], k_ref[...], _NT, preferred_element_type=F32) * ATT_SCALE
        sc = lax.cond(i == j, lambda u: jnp.where(_att_mask(t), u, -jnp.inf), lambda u: u, sc)
        m_new = jnp.maximum(m_s[...], jnp.max(sc, axis=-1, keepdims=True))
        alpha = jnp.exp(m_s[...] - m_new)
        p = jnp.exp(sc - m_new)
        l_s[...] = alpha * l_s[...] + jnp.sum(p, axis=-1, keepdims=True)
        acc[...] = alpha * acc[...] + jnp.dot(p.astype(BF16), v_ref[...], preferred_element_type=F32)
        m_s[...] = m_new

        @pl.when(j == i)
        def _():
            o_ref[...] = acc[...] / l_s[...]
            lse_ref[...] = m_s[...] + jnp.log(l_s[...])

    grid_spec = pltpu.PrefetchScalarGridSpec(
        num_scalar_prefetch=2, grid=(HEADS, len(ii)),
        in_specs=[pl.BlockSpec((t, HEAD_PAD), lambda h, n, ir, jr: (ir[n], h)),
                  pl.BlockSpec((t, HEAD_PAD), lambda h, n, ir, jr: (jr[n], h)),
                  pl.BlockSpec((t, HEAD), lambda h, n, ir, jr: (jr[n], h))],
        out_specs=[pl.BlockSpec((t, HEAD), lambda h, n, ir, jr: (ir[n], h)),
                   pl.BlockSpec((None, t, 1), lambda h, n, ir, jr: (h, ir[n], 0))],
        scratch_shapes=[pltpu.VMEM((t, 1), F32), pltpu.VMEM((t, 1), F32), pltpu.VMEM((t, HEAD), F32)])
    return pl.pallas_call(
        body, name=name, grid_spec=grid_spec,
        out_shape=[jax.ShapeDtypeStruct((s, HEADS * HEAD), F32), jax.ShapeDtypeStruct((HEADS, s, 1), F32)],
        compiler_params=_params(("parallel", "arbitrary")),
    )(ii, jj, q, k, v)


def _attn_bwd(q, k, v, do, o, lse, name, dkv_sum=None):
    s = q.shape[0]
    t = min(ATT_TILE, s)
    nb = s // t
    jj, ii = _att_pairs(nb, by_query=False)
    nsum = 2 if dkv_sum else 0

    def body(jj_ref, ii_ref, q_ref, k_ref, v_ref, do_ref, o_ref, lse_ref, *rest):
        dq_ref, dk_ref, dv_ref, dk_acc, dv_acc, sc_s, dp_s, p_s, ds_s, dl_s = rest[nsum:]
        step = pl.program_id(1)
        i, j = ii_ref[step], jj_ref[step]

        @pl.when(step == 0)
        def _():
            dq_ref[...] = jnp.zeros_like(dq_ref)

        @pl.when(i == j)
        def _():
            dk_acc[...] = jnp.zeros_like(dk_acc)
            dv_acc[...] = jnp.zeros_like(dv_acc)

        do_f = do_ref[...]
        dob = do_f.astype(BF16)
        dl_s[...] = jnp.sum(do_f * o_ref[...], axis=-1, keepdims=True)
        sc_s[...] = lax.dot_general(q_ref[...], k_ref[...], _NT, preferred_element_type=F32)
        dp_s[...] = lax.dot_general(dob, v_ref[...], _NT, preferred_element_type=F32)

        def softmax_strips(diagonal):
            for r in range(t // ATT_STRIP):
                rows = slice(r * ATT_STRIP, (r + 1) * ATT_STRIP)
                p = jnp.exp(sc_s[rows, :] * ATT_SCALE - lse_ref[rows, :])
                if diagonal:
                    p = jnp.where(_att_strip_mask(r, t), p, 0.0)
                p_s[rows, :] = p.astype(BF16)
                ds_s[rows, :] = (p * (dp_s[rows, :] - dl_s[rows, :]) * ATT_SCALE).astype(BF16)

        pl.when(i == j)(functools.partial(softmax_strips, True))
        pl.when(i != j)(functools.partial(softmax_strips, False))
        ds = ds_s[...]
        dv_acc[...] += lax.dot_general(p_s[...], dob, _TN, preferred_element_type=F32)
        dk_acc[...] += lax.dot_general(ds, q_ref[...], _TN, preferred_element_type=F32)
        rows = pl.ds(pl.multiple_of(i * t, t), t)
        dq_ref[rows, :] += jnp.dot(ds, k_ref[...], preferred_element_type=F32)

        @pl.when(i == nb - 1)
        def _():
            dk_ref[...] = dk_acc[...] + rest[0][...] if nsum else dk_acc[...]
            dv_ref[...] = dv_acc[...] + rest[1][...] if nsum else dv_acc[...]

    dk_blk = pl.BlockSpec((t, HEAD_PAD), lambda h, n, jr, ir: (jr[n], h))
    dv_blk = pl.BlockSpec((t, HEAD), lambda h, n, jr, ir: (jr[n], h))
    grid_spec = pltpu.PrefetchScalarGridSpec(
        num_scalar_prefetch=2, grid=(HEADS, len(jj)),
        in_specs=[pl.BlockSpec((t, HEAD_PAD), lambda h, n, jr, ir: (ir[n], h)),
                  pl.BlockSpec((t, HEAD_PAD), lambda h, n, jr, ir: (jr[n], h)),
                  pl.BlockSpec((t, HEAD), lambda h, n, jr, ir: (jr[n], h)),
                  pl.BlockSpec((t, HEAD), lambda h, n, jr, ir: (ir[n], h)),
                  pl.BlockSpec((t, HEAD), lambda h, n, jr, ir: (ir[n], h)),
                  pl.BlockSpec((None, t, 1), lambda h, n, jr, ir: (h, ir[n], 0))] + [dk_blk, dv_blk][:nsum],
        out_specs=[pl.BlockSpec((s, HEAD_PAD), lambda h, n, jr, ir: (0, h)), dk_blk, dv_blk],
        scratch_shapes=[pltpu.VMEM((t, HEAD_PAD), F32), pltpu.VMEM((t, HEAD), F32), pltpu.VMEM((t, t), F32),
                        pltpu.VMEM((t, t), F32), pltpu.VMEM((t, t), BF16), pltpu.VMEM((t, t), BF16),
                        pltpu.VMEM((t, 1), F32)])
    return pl.pallas_call(
        body, name=name, grid_spec=grid_spec,
        out_shape=[jax.ShapeDtypeStruct((s, HEADS * HEAD_PAD), F32)] * 2 + [jax.ShapeDtypeStruct((s, HEADS * HEAD), F32)],
        compiler_params=_params(("parallel", "arbitrary")),
    )(jj, ii, q, k, v, do, o, lse, *(dkv_sum or ()))


def _rope_tables(positions):
    half = ROPE // 2
    inv_freq = ROPE_BASE ** (-jnp.arange(half, dtype=F32) / half)
    ang = positions.astype(F32)[:, None] * inv_freq
    cos, sin = jnp.cos(ang), jnp.sin(ang)
    return jnp.concatenate([cos] * 4, axis=1), jnp.concatenate([-sin, sin] * 2, axis=1)


def _loss_and_grad(y, target, name):
    s = y.shape[0]
    ts = min(ROW_TILE, s)

    def body(y_ref, t_ref, dy_ref, l_ref):
        e = y_ref[...] - t_ref[...]
        dy_ref[...] = e * (1.0 / D)
        part = jnp.sum(jnp.sum(e * e, axis=-1, keepdims=True) * (0.5 / D), axis=0, keepdims=True)
        part = part * jnp.ones((1, 128), F32)

        @pl.when(pl.program_id(0) == 0)
        def _():
            l_ref[...] = part

        @pl.when(pl.program_id(0) > 0)
        def _():
            l_ref[...] += part

    return pl.pallas_call(
        body, name=name, grid=(s // ts,),
        in_specs=[pl.BlockSpec((ts, D), lambda i: (i, 0))] * 2,
        out_specs=[pl.BlockSpec((ts, D), lambda i: (i, 0)), pl.BlockSpec((1, 128), lambda i: (0, 0))],
        out_shape=[jax.ShapeDtypeStruct((s, D), F32), jax.ShapeDtypeStruct((1, 128), F32)],
        compiler_params=_params(("arbitrary",)),
    )(y, target)


ANY = pl.BlockSpec(memory_space=pl.ANY)


def _all_gather(shard, name):
    def body(x_ref, out_ref, send_sems, recv_sems, local_sem):
        x, y, c = lax.axis_index("x"), lax.axis_index("y"), lax.axis_index("c")
        me, sibling = (x, y, c), (x, y, 1 - c)
        chips = [(1 - x, y), (x, 1 - y), (1 - x, 1 - y)]

        def rows(px, py, pc):
            return out_ref.at[4 * px + 2 * py + pc]

        def copy(k, block, to, src=None):
            return pltpu.make_async_remote_copy(
                src_ref=rows(*block) if src is None else src, dst_ref=rows(*block),
                send_sem=send_sems.at[k], recv_sem=recv_sems.at[k], device_id=to, device_id_type=MESH)

        mine = pltpu.make_async_copy(x_ref, rows(*me), local_sem)
        mine.start()
        first = [copy(0, me, sibling, src=x_ref)]
        first += [copy(1 + j, me, (*chip, c), src=x_ref) for j, chip in enumerate(chips)]
        for cp in first:
            cp.start()
        passed = [copy(4 + j, (*chip, c), sibling) for j, chip in enumerate(chips)]
        for j, chip in enumerate(chips):
            copy(1 + j, (*chip, c), me).wait_recv()
            passed[j].start()
        copy(0, sibling, me).wait_recv()
        for j, chip in enumerate(chips):
            copy(4 + j, (*chip, 1 - c), me).wait_recv()
        for cp in first + passed:
            cp.wait_send()
        mine.wait()

    return pl.pallas_call(
        body, name=name, out_shape=jax.ShapeDtypeStruct((N_DEV,) + shard.shape, shard.dtype),
        in_specs=[ANY], out_specs=ANY,
        scratch_shapes=[pltpu.SemaphoreType.DMA((7,)), pltpu.SemaphoreType.DMA((7,)), pltpu.SemaphoreType.DMA],
    )(shard)


def _exchange(blocks, name):
    def body(x_ref, out_ref, send_sems, recv_sems, local_sem):
        x, y, c = lax.axis_index("x"), lax.axis_index("y"), lax.axis_index("c")
        me = 4 * x + 2 * y + c
        mine = pltpu.make_async_copy(x_ref.at[me], out_ref.at[me], local_sem)
        mine.start()
        copies = []
        for k in range(1, N_DEV):
            px = 1 - x if k & 4 else x
            py = 1 - y if k & 2 else y
            pc = 1 - c if k & 1 else c
            peer = 4 * px + 2 * py + pc
            cp = pltpu.make_async_remote_copy(
                src_ref=x_ref.at[peer], dst_ref=out_ref.at[me], send_sem=send_sems.at[k - 1],
                recv_sem=recv_sems.at[k - 1], device_id=(px, py, pc), device_id_type=MESH)
            cp.start()
            copies.append((cp, pltpu.make_async_remote_copy(
                src_ref=x_ref.at[peer], dst_ref=out_ref.at[peer], send_sem=send_sems.at[k - 1],
                recv_sem=recv_sems.at[k - 1], device_id=(px, py, pc), device_id_type=MESH)))
        for cp, landing in copies:
            landing.wait_recv()
        for cp, landing in copies:
            cp.wait_send()
        mine.wait()

    return pl.pallas_call(
        body, name=name, out_shape=jax.ShapeDtypeStruct(blocks.shape, blocks.dtype),
        in_specs=[ANY], out_specs=ANY,
        scratch_shapes=[pltpu.SemaphoreType.DMA((7,)), pltpu.SemaphoreType.DMA((7,)), pltpu.SemaphoreType.DMA],
    )(blocks)


HBM = pl.BlockSpec(memory_space=pltpu.HBM)
SEM = pl.BlockSpec(memory_space=pltpu.SEMAPHORE)
EFFECT = pltpu.SideEffectType.DATAFLOW_SIDE_EFFECTING


def _peers():
    x, y, c = lax.axis_index("x"), lax.axis_index("y"), lax.axis_index("c")
    peers = []
    for k in range(1, N_DEV):
        px = 1 - x if k & 4 else x
        py = 1 - y if k & 2 else y
        pc = 1 - c if k & 1 else c
        peers.append(((px, py, pc), 4 * px + 2 * py + pc))
    return 4 * x + 2 * y + c, peers


def _send_start(srcs, name, gather):
    n = len(srcs)
    lands = [((N_DEV,) + s.shape) if gather else s.shape for s in srcs]

    def body(*refs):
        src_refs, land_refs = refs[:n], refs[n:2 * n]
        send_sems, recv_sems, token = refs[2 * n], refs[2 * n + 1], refs[-1]
        me, peers = _peers()
        for i in range(n):
            for k, (dev, idx) in enumerate(peers):
                pltpu.make_async_remote_copy(
                    src_ref=src_refs[i] if gather else src_refs[i].at[idx], dst_ref=land_refs[i].at[me],
                    send_sem=send_sems.at[7 * i + k], recv_sem=recv_sems.at[7 * i + k], device_id=dev,
                    device_id_type=MESH).start()
        token[...] = jnp.zeros_like(token)

    res = pl.pallas_call(
        body, name=name,
        out_shape=(pltpu.SemaphoreType.DMA((7 * n,)), pltpu.SemaphoreType.DMA((7 * n,)),
                   *[pltpu.HBM(s.shape, s.dtype) for s in srcs],
                   *[pltpu.HBM(shape, s.dtype) for shape, s in zip(lands, srcs)],
                   jax.ShapeDtypeStruct((8, 128), F32)),
        in_specs=(HBM,) * (2 * n), out_specs=(SEM, SEM) + (HBM,) * (2 * n) + (pl.BlockSpec(memory_space=pltpu.VMEM),),
        input_output_aliases={i: 2 + i for i in range(2 * n)},
        compiler_params=pltpu.CompilerParams(has_side_effects=EFFECT),
    )(*[pltpu.with_memory_space_constraint(s, pltpu.HBM) for s in srcs],
      *[pltpu.with_memory_space_constraint(lax.empty(shape, s.dtype), pltpu.HBM) for shape, s in zip(lands, srcs)])
    return dict(sems=res[:2], srcs=res[2:2 + n], lands=res[2 + n:2 + 2 * n], token=res[-1])


def _send_wait(handle, after, name, gather):
    n = len(handle["srcs"])

    def body(*refs):
        src_refs, land_refs = refs[:n], refs[n:2 * n]
        send_sems, recv_sems = refs[2 * n], refs[2 * n + 1]
        me, peers = _peers()
        for i in range(n):
            for k, (dev, idx) in enumerate(peers):
                cp = pltpu.make_async_remote_copy(
                    src_ref=src_refs[i] if gather else src_refs[i].at[idx], dst_ref=land_refs[i].at[idx],
                    send_sem=send_sems.at[7 * i + k], recv_sem=recv_sems.at[7 * i + k], device_id=dev,
                    device_id_type=MESH)
                cp.wait_send()
                cp.wait_recv()

    both = list(handle["srcs"]) + list(handle["lands"])
    res = pl.pallas_call(
        body, name=name, out_shape=tuple(pltpu.HBM(t.shape, t.dtype) for t in both),
        in_specs=(HBM,) * (2 * n) + (SEM, SEM, pl.BlockSpec(memory_space=pl.ANY)), out_specs=(HBM,) * (2 * n),
        input_output_aliases={i: i for i in range(2 * n)},
        compiler_params=pltpu.CompilerParams(has_side_effects=EFFECT),
    )(*both, *handle["sems"], after)
    return res[:n], res[n:]


def _adamw(parts, w, m, v, name, tr=128):
    pieces = len(parts)
    n, r, wd = parts[0].shape
    tr = next((t for t in (tr, 64, 32, 16) if r % t == 0), r)
    nrt = r // tr

    def body(*refs):
        w_ref, m_ref, v_ref, g_ref, d_ref, nm_ref, nv_ref = refs[pieces:]

        def update(p_ref):
            g = p_ref[0].astype(F32)
            for k in range(1, n):
                g = g + p_ref[k].astype(F32)
            m_new = B1 * m_ref[...] + (1.0 - B1) * g
            v_new = B2 * v_ref[...] + (1.0 - B2) * (g * g)
            m_hat = m_new / (1.0 - B1 ** STEP)
            v_hat = v_new / (1.0 - B2 ** STEP)
            g_ref[...] = g
            d_ref[...] = -LR * (m_hat / (jnp.sqrt(v_hat) + ADAM_EPS) + WD * w_ref[...])
            nm_ref[...] = m_new
            nv_ref[...] = v_new

        for p in range(pieces):
            pl.when(pl.program_id(0) == p)(functools.partial(update, refs[p]))

    part_spec = lambda p: pl.BlockSpec((n, tr, wd), lambda l, i: (0, jnp.clip(i + (l - p) * nrt, 0, nrt - 1), 0))
    blk = pl.BlockSpec((tr, wd), lambda l, i: (l * nrt + i, 0))
    return pl.pallas_call(
        body, name=name, grid=(pieces, nrt),
        in_specs=[part_spec(p) for p in range(pieces)] + [blk, blk, blk],
        out_specs=[blk] * 4, out_shape=[jax.ShapeDtypeStruct((pieces * r, wd), F32)] * 4,
        compiler_params=_params(("arbitrary", "arbitrary")),
    )(*parts, w, m, v)


def _outer8(ct, dm, name):
    k, n = ct.shape[0], dm.shape[1]

    def body(c_ref, d_ref, o_ref):
        cv, dv = c_ref[...], d_ref[...]
        acc = cv[:, 0:1] * dv[0:1, :]
        for s in range(1, N_DEV):
            acc = acc + cv[:, s:s + 1] * dv[s:s + 1, :]
        o_ref[...] = acc

    tk = 256
    return pl.pallas_call(
        body, name=name, grid=(k // tk,),
        in_specs=[pl.BlockSpec((tk, N_DEV), lambda i: (i, 0)), pl.BlockSpec((N_DEV, n), lambda i: (0, 0))],
        out_specs=pl.BlockSpec((tk, n), lambda i: (i, 0)), out_shape=jax.ShapeDtypeStruct((k, n), F32),
        compiler_params=_params(("parallel",)),
    )(ct, dm)


FULL = (0, D)
C128 = (0, 128)
HEAD_NOPE = [(h * HEAD_PAD, NOPE) for h in range(HEADS)]
HEAD_ROPE = [(h * HEAD_PAD + NOPE, 128) for h in range(HEADS)]
HEAD_ALL = [(h * HEAD_PAD, HEAD_PAD) for h in range(HEADS)]
HEAD_V = [(h * HEAD, HEAD) for h in range(HEADS)]


def _modulate(x, p, ties=()):
    return _rowwise_fwd(_modulate_fn, [(x, FULL)], [p["gain"], p["scale"], p["shift"]], [(D, BF16, FULL)], "modulate",
                        ties=ties)[0]


def _residual_bwd(y, gm, dxn):
    return _rowwise_bwd(_gate_only_fn, [(y, FULL)], [gm], [(D, F32, FULL)], [dxn], [(0, FULL)], [(D, BF16)],
                        "residual_bwd")


def _modulate_bwd(x, p, dh, dx_in, prev=None):
    pars = [p["gain"], p["scale"], p["shift"]]
    if prev is None:
        return list(_rowwise_bwd(_modulate_fn, [(x, FULL)], pars, [(D, BF16, FULL)], [dh], [(0, FULL)], [(D, F32)],
                                 "modulate_bwd", add={0: dx_in})) + [None]
    s = x.shape[0]
    ts = min(2 * ROW_TILE, s)

    def body(x_ref, g_ref, sc_ref, sh_ref, dh_ref, din_ref, y_ref, gm_ref, dx_ref, dy_ref, dg_ref, dsc_ref, dsh_ref,
             dgm_ref):
        _, vjp = jax.vjp(lambda *t: _modulate_fn(0, *t)[0], x_ref[...], g_ref[...], sc_ref[...], sh_ref[...])
        dxm, dg, dsc, dsh = vjp(dh_ref[...])
        dx = dxm + din_ref[...]
        dx_ref[...] = dx
        dy_ref[...] = (gm_ref[...] * dx).astype(BF16)
        sums = (dg, dsc, dsh, jnp.sum(dx * y_ref[...], axis=0, keepdims=True))
        first = pl.program_id(0) == 0
        for ref, val in zip((dg_ref, dsc_ref, dsh_ref, dgm_ref), sums):
            @pl.when(first)
            def _(ref=ref, val=val):
                ref[...] = val

            @pl.when(jnp.logical_not(first))
            def _(ref=ref, val=val):
                ref[...] += val

    blk = pl.BlockSpec((ts, D), lambda i: (i, 0))
    vec = pl.BlockSpec((1, D), lambda i: (0, 0))
    dx, dy, dg, dsc, dsh, dgm = pl.pallas_call(
        body, name="modulate_bwd_chain", grid=(s // ts,),
        in_specs=[blk, vec, vec, vec, blk, blk, blk, vec], out_specs=[blk, blk, vec, vec, vec, vec],
        out_shape=[jax.ShapeDtypeStruct((s, D), F32), jax.ShapeDtypeStruct((s, D), BF16)]
        + [jax.ShapeDtypeStruct((1, D), F32)] * 4,
        compiler_params=_params(("arbitrary",)),
    )(x, *pars, dh, dx_in, prev[0], prev[1])
    return [dx, dg, dsc, dsh, (dy, dgm)]


def _out_proj(a, w, x, p, nxt, name, **kw):
    res = _matmul([(a, w)], "nn", name, out_dtype=BF16, resid=(x, p["gm"]) + tuple(nxt or ()), **kw)
    return res[1], res[0], (res[2] if nxt else None)


def _ffn_fwd(x, p, ties=(), h=None, nxt=None):
    if h is None:
        h, ties = _modulate(x, p, ties), ()
    act_dgate, act_dup, act = _ffn_in(h, p["w_in"], "ffn_in", ties=ties)
    res = _ffn_out(act, p["wo"], (x, p["gm"]) + tuple(nxt or ()), "ffn_out")
    return res[1], dict(x=x, h=h, act_dgate=act_dgate, act_dup=act_dup, act=act, y=res[0]), (res[2] if nxt else None)


def _ffn_bwd(t, p, dxn, res=None, prev=None, ties=(), early=None):
    dy, dgm = res or _residual_bwd(t["y"], p["gm"], dxn)
    dgate, dup = _ffn_bwd_act(dy, p["wo"], t["act_dgate"], t["act_dup"], "ffn_bwd_act", ties=ties)
    dwo = _ffn_dwo(t["act"], dy, "ffn_dwo", ties=ties)
    dh = _ffn_dh(dgate, dup, p["w_in"], "ffn_dh", ties=early(dwo) if early else ())
    dwi = _ffn_dwi(t["h"], dgate, dup, "ffn_dwi")
    dx, dgain, dscale, dshift, res_prev = _modulate_bwd(t["x"], p, dh, dxn, prev)
    return dx, dict(gain=dgain, scale=dscale, shift=dshift, gm=dgm, w_in=dwi, wo=dwo), res_prev


def _pad128(t):
    return jnp.pad(t, ((0, 0), (0, 128 - t.shape[1])))


def _gdn_fwd(x, p, ties=(), h=None, nxt=None):
    s = x.shape[0]
    if h is None:
        h, ties = _modulate(x, p, ties), ()
    pm = _mm(h, p["w_main"], "nn", "gdn_proj", ties=ties)
    tail = _mm(h, p["w_tail"], "nn", "gdn_proj_tail", ties=ties)
    qkv = _gdn_conv_fwd(pm, p["conv_w"], "gdn_conv")
    beta, gcum = _rowwise_fwd(_gdn_gates_fn, [(tail, C128), (tail, (128, 128))], [p["a_log"], p["dt_bias"]],
                              [(128, F32, C128)] * 2, "gdn_gates")
    grow = gcum[:, :HEADS].reshape(s // CHUNK, CHUNK, N_GROUPS, GROUP).transpose(0, 2, 3, 1)
    grow = grow.reshape(s // CHUNK, N_GROUPS, 1, GROWS)
    o, states, invs = _gdn_scan_fwd(qkv, beta, gcum, grow, "gdn_scan")
    on, = _rowwise_fwd(_gdn_outnorm_fn, [(o, HEAD_V), (pm, [(3 * D + h_ * HEAD, HEAD) for h_ in range(HEADS)])],
                       [p["norm_g"]], [(D, BF16, HEAD_V)], "gdn_outnorm", groups=HEADS)
    xn, y, hn = _out_proj(on, p["w_out"], x, p, nxt, "mix_out", tm=512)
    t = dict(x=x, h=h, pm=pm, tail=tail, qkv=qkv, beta=beta, gcum=gcum, grow=grow, o=o, states=states, invs=invs,
             on=on, y=y)
    return xn, t, hn


def _gdn_bwd(t, p, dxn, res=None, prev=None, ties=()):
    s = dxn.shape[0]
    zc = [(3 * D + h_ * HEAD, HEAD) for h_ in range(HEADS)]
    dy, dgm = res or _residual_bwd(t["y"], p["gm"], dxn)
    dw_out = _mm(t["on"], dy, "tn", "mix_dwo", ties=ties)
    don = _mm(dy, p["w_out"], "nt", "mix_dout", ties=ties)
    do, dz, dnorm_g = _rowwise_bwd(_gdn_outnorm_fn, [(t["o"], HEAD_V), (t["pm"], zc)], [p["norm_g"]],
                                   [(D, BF16, HEAD_V)], [don], [(0, HEAD_V), (1, HEAD_V)], [(D, F32), (D, BF16)],
                                   "gdn_outnorm_bwd", groups=HEADS)
    dq, dk, dv, dbeta, dg, dgr = _gdn_scan_bwd(t["qkv"], t["beta"], t["gcum"], t["grow"], t["states"], t["invs"], do,
                                               "gdn_scan_bwd")
    dg = dg + _pad128(dgr.reshape(s // CHUNK, N_GROUPS, GROUP, CHUNK).transpose(0, 3, 1, 2).reshape(s, HEADS))
    dtail, da_log, ddt = _rowwise_bwd(_gdn_gates_fn, [(t["tail"], C128), (t["tail"], (128, 128))],
                                      [p["a_log"], p["dt_bias"]], [(128, F32, C128)] * 2, [dbeta, dg],
                                      [(0, C128), (0, (128, 128))], [(256, F32)], "gdn_gates_bwd")
    dxs, dcw = [], []
    for part, d in enumerate((dq, dk, dv)):
        dx_, dw_ = _gdn_conv_bwd(t["pm"], p["conv_w"], d, part, "gdn_conv_bwd")
        dxs.append(dx_)
        dcw.append(dw_)
    pieces = dxs + [dz]
    dh = _matmul([(d, p["w_main"]) for d in pieces] + [(dtail, p["w_tail"])], "nt", "gdn_dh",
                 boffs=[0, D, 2 * D, 3 * D, 0], tk=512)
    dw_main = [_mm(t["h"], d, "tn", "gdn_dwi") for d in pieces]
    dw_tail = _mm(t["h"], dtail, "tn", "gdn_dwi_tail")
    dx, dgain, dscale, dshift, res_prev = _modulate_bwd(t["x"], p, dh, dxn, prev)
    return dx, dict(gain=dgain, scale=dscale, shift=dshift, gm=dgm, w_main=jnp.concatenate(dw_main, axis=1),
                    w_tail=dw_tail, conv_w=jnp.concatenate(dcw, axis=1), a_log=da_log, dt_bias=ddt,
                    norm_g=dnorm_g, w_out=dw_out), res_prev


def _q_rows(q2, cosf, sins):
    return [(q2, HEAD_NOPE), (q2, HEAD_ROPE), (cosf, C128), (sins, C128)]


def _mla_fwd(x, p, kv, ties=(), h=None, nxt=None):
    if h is None:
        h, ties = _modulate(x, p, ties), ()
    cq = _mm(h, p["w_dq"], "nn", "mla_dq", ties=ties)
    cqn, = _rowwise_fwd(_rms_fn, [(cq, (0, Q_LORA))], [p["q_lora_g"]], [(Q_LORA, BF16, (0, Q_LORA))], "mla_qlora_norm")
    q2 = _mm(cqn, p["w_uq"], "nn", "mla_uq")
    qn, = _rowwise_fwd(_q_norm_rope_fn, _q_rows(q2, kv["cosf"], kv["sins"]), [p["q_gn"], p["q_gr"]],
                       [(HEADS * HEAD_PAD, BF16, HEAD_ALL)], "mla_q_norm", groups=HEADS)
    o, lse = _attn_fwd(qn, kv["kn"], kv["vb"], "mla_attn")
    xn, y, hn = _out_proj(o, p["w_out"], x, p, nxt, "mix_out", tm=512)
    return xn, dict(x=x, h=h, cq=cq, cqn=cqn, q2=q2, qn=qn, o=o, lse=lse, y=y), hn


def _mla_bwd(t, p, kv, dxn, res=None, prev=None, ties=(), dkv_sum=None):
    dy, dgm = res or _residual_bwd(t["y"], p["gm"], dxn)
    dw_out = _mm(t["o"], dy, "tn", "mix_dwo", ties=ties)
    do = _mm(dy, p["w_out"], "nt", "mix_dout", ties=ties)
    dq, dk, dv = _attn_bwd(t["qn"], kv["kn"], kv["vb"], do, t["o"], t["lse"], "mla_attn_bwd", dkv_sum)
    dq2, dq_gn, dq_gr = _rowwise_bwd(_q_norm_rope_fn, _q_rows(t["q2"], kv["cosf"], kv["sins"]), [p["q_gn"], p["q_gr"]],
                                     [(HEADS * HEAD_PAD, BF16, HEAD_ALL)], [dq],
                                     [(0, HEAD_NOPE), (0, HEAD_ROPE), None, None], [(HEADS * HEAD_PAD, BF16)],
                                     "mla_q_norm_bwd", groups=HEADS)
    dw_uq = _mm(t["cqn"], dq2, "tn", "mla_dwuq")
    dcqn = _mm(dq2, p["w_uq"], "nt", "mla_dcq")
    dcq, dq_lora_g = _rowwise_bwd(_rms_fn, [(t["cq"], (0, Q_LORA))], [p["q_lora_g"]], [(Q_LORA, BF16, (0, Q_LORA))],
                                  [dcqn], [(0, (0, Q_LORA))], [(Q_LORA, BF16)], "mla_qlora_norm_bwd")
    dw_dq = _mm(t["h"], dcq, "tn", "mla_dwdq")
    dh = _mm(dcq, p["w_dq"], "nt", "mla_dh")
    dx, dgain, dscale, dshift, res_prev = _modulate_bwd(t["x"], p, dh, dxn, prev)
    grads = dict(gain=dgain, scale=dscale, shift=dshift, gm=dgm, w_dq=dw_dq, q_lora_g=dq_lora_g, w_uq=dw_uq,
                 q_gn=dq_gn, q_gr=dq_gr, w_out=dw_out)
    return dx, grads, res_prev, dk, dv


def _k_rows(kvp, ckv, cosf, sins):
    return [(kvp, HEAD_NOPE), (kvp, HEAD_ROPE), (ckv, (KV_LORA, 128)), (cosf, C128), (sins, C128)]


def _kv_fwd(x, p, cosf, sins):
    h = _modulate(x, p)
    ckv = _mm(h, p["w_dkv"], "nn", "kv_down")
    lat, = _rowwise_fwd(_rms_fn, [(ckv, (0, KV_LORA))], [p["kv_g"]], [(KV_LORA, BF16, (0, KV_LORA))], "kv_norm")
    kvp = _mm(lat, p["w_ukv"], "nn", "kv_up")
    kn, vb = _rowwise_fwd(_k_norm_rope_fn, _k_rows(kvp, ckv, cosf, sins), [p["k_gn"], p["k_gr"]],
                          [(HEADS * HEAD_PAD, BF16, HEAD_ALL), (HEADS * HEAD, BF16, HEAD_V)], "kv_k_norm",
                          groups=HEADS)
    return dict(x=x, h=h, ckv=ckv, lat=lat, kvp=kvp, kn=kn, vb=vb, cosf=cosf, sins=sins)


def _kv_bwd(t, p, dk, dv, dx_in, prev):
    dkvp, drope, dk_gn, dk_gr = _rowwise_bwd(
        _k_norm_rope_fn, _k_rows(t["kvp"], t["ckv"], t["cosf"], t["sins"]), [p["k_gn"], p["k_gr"]],
        [(HEADS * HEAD_PAD, BF16, HEAD_ALL), (HEADS * HEAD, BF16, HEAD_V)], [dk, dv],
        [(0, HEAD_NOPE), (0, HEAD_ROPE), (1, C128), None, None], [(HEADS * HEAD_PAD, BF16), (128, F32)],
        "kv_k_norm_bwd", groups=HEADS)
    dw_ukv = _mm(t["lat"], dkvp, "tn", "kv_dwukv")
    dlat = _mm(dkvp, p["w_ukv"], "nt", "kv_dlat")
    dckv, dkv_g = _rowwise_bwd(_rms_fn, [(t["ckv"], (0, KV_LORA))], [p["kv_g"]], [(KV_LORA, BF16, (0, KV_LORA))],
                               [dlat], [(0, (0, KV_LORA))], [(KV_LORA, F32)], "kv_norm_bwd")
    dw_dkv = jnp.concatenate([_mm(t["h"], dckv, "tn", "kv_dwdkv"), _mm(t["h"], drope, "tn", "kv_dwdkv_rope")], axis=1)
    dh = _matmul([(dckv, p["w_dkv"]), (drope, p["w_dkv"])], "nt", "kv_dh", boffs=[0, KV_LORA])
    dx, dgain, dscale, dshift, res_prev = _modulate_bwd(t["x"], p, dh, dx_in, prev)
    return dx, dict(gain=dgain, scale=dscale, shift=dshift, w_dkv=dw_dkv, kv_g=dkv_g, w_ukv=dw_ukv, k_gn=dk_gn,
                    k_gr=dk_gr), res_prev


WEIGHTS = ["ada_w", "ada_b", "norm_g", "ffn_w_in", "ffn_w_out", "gdn_w_in", "gdn_conv_w", "gdn_a_log", "gdn_dt_bias",
           "gdn_norm_g", "gdn_w_out", "kv_ada_w", "kv_ada_b", "kv_norm_g", "mla_w_dkv", "mla_kv_norm_g", "mla_w_ukv",
           "mla_k_norm_g", "mla_w_dq", "mla_q_lora_norm_g", "mla_w_uq", "mla_q_norm_g", "mla_w_out"]
SMALL = [("ada_b", 4 * N_MOD * D), ("kv_ada_b", 2 * D), ("norm_g", DEPTH * 3 * D), ("gdn_conv_w", N_A * CONV_K * 3 * D),
         ("gdn_a_log", N_A * HEADS), ("gdn_dt_bias", N_A * HEADS), ("gdn_norm_g", N_A * HEAD), ("kv_norm_g", D),
         ("mla_kv_norm_g", KV_LORA), ("mla_k_norm_g", QK_HEAD), ("mla_q_lora_norm_g", 2 * Q_LORA),
         ("mla_q_norm_g", 2 * QK_HEAD)]
SMALL_REPLICATED = [n for n, _ in SMALL if n not in ("norm_g", "gdn_conv_w")]


def _silu_fn(g, t):
    return (_silu(t),)


def _dup_rope(t):
    return jnp.concatenate([t[..., :NOPE], t[..., NOPE:], t[..., NOPE:]], axis=-1)


def _fold_rope(t):
    return jnp.concatenate([t[..., :NOPE], t[..., NOPE:QK_HEAD] + t[..., QK_HEAD:]], axis=-1)


def _pack(pieces, rows):
    flat = jnp.concatenate([p.reshape(-1).astype(F32) for p in pieces])
    return jnp.pad(flat, (0, rows * 128 - flat.shape[0])).reshape(rows, 128)


def _step(a):
    me = 4 * lax.axis_index("x") + 2 * lax.axis_index("y") + lax.axis_index("c")
    x = a["x"][0]
    cosf, sins = _rope_tables(a["positions"][0])

    n_gdn = (4 * D + 2 * HEADS) // N_DEV
    AHEAD = 2

    stages = [(l, part) for l in range(DEPTH) for part in range(3)]

    def stage_shards(l, part):
        if part != 1:
            sh = {"ffn_w_in": a["ffn_w_in"][l, part // 2], "ffn_w_out": a["ffn_w_out"][l, part // 2]}
            if part == 2 and l == N_A - 1:
                sh.update(mla_w_dkv=a["mla_w_dkv"], mla_w_ukv=a["mla_w_ukv"])
            return sh
        if l < N_A:
            return {"gdn_w_in": a["gdn_w_in"][l], "gdn_w_out": a["gdn_w_out"][l]}
        j = l - N_A
        return {"mla_w_dq": a["mla_w_dq"][j], "mla_w_uq": a["mla_w_uq"][j], "mla_w_out": a["mla_w_out"][j]}

    def zero_of(t):
        return jnp.minimum(jnp.abs(t[(0,) * t.ndim].astype(F32)), 0.0)

    def start_stage(l, part, tie):
        sh = stage_shards(l, part)
        return list(sh), _send_start([(w + tie).astype(BF16) for w in sh.values()], f"fetch_start_{l}_{part}", gather=True)

    def finish_stage(l, part, names, handle, after):
        srcs, lands = _send_wait(handle, after, f"fetch_wait_{l}_{part}", gather=True)
        return {n: lax.dynamic_update_slice(land, src[None], (me, 0, 0)) for n, src, land in zip(names, srcs, lands)}

    n_cw, n_ng = N_A * CONV_K * 3 * HEAD, DEPTH * 3 * HEAD
    small_all = _all_gather(_pack([a["gdn_conv_w"], a["norm_g"], a["c"]], 44), "gather_small").reshape(N_DEV, -1)
    conv_w = small_all[:, :n_cw].reshape(N_DEV, N_A, CONV_K, 3 * HEAD).transpose(1, 2, 0, 3).reshape(N_A, CONV_K, 3 * D)
    norm_g = small_all[:, n_cw:n_cw + n_ng].reshape(N_DEV, DEPTH, 3, HEAD).transpose(1, 2, 0, 3).reshape(DEPTH, 3, D)
    c_all = small_all[:, n_cw + n_ng:n_cw + n_ng + D]

    c_act, = _rowwise_fwd(_silu_fn, [(c_all, FULL)], [], [(D, F32, FULL)], "c_act")
    n_ada = N_MOD * D // N_DEV
    parts = [_mm(c_act, a["ada_w"][l], "nn", "mod_proj") for l in range(DEPTH)]
    parts.append(_mm(c_act, a["kv_ada_w"], "nn", "mod_proj_kv"))
    mod_recv = _exchange(jnp.concatenate(parts, axis=1)[:, None, :], "exchange_mod")[:, 0]
    mod = mod_recv[:, :DEPTH * n_ada].reshape(N_DEV, DEPTH, n_ada).transpose(1, 0, 2).reshape(DEPTH, N_MOD * D)
    mod = (mod + a["ada_b"]).reshape(DEPTH, N_MOD, D)
    kvmod = mod_recv[:, DEPTH * n_ada:].reshape(2 * D) + a["kv_ada_b"]

    def row(v):
        return v[None]

    def ffn_params(l, i, w):
        k = 0 if i == 0 else 6
        return dict(gain=row(norm_g[l, 0 if i == 0 else 2]), shift=row(mod[l, k]), scale=row(mod[l, k + 1]),
                    gm=0.5 * row(mod[l, k + 2]), w_in=w["ffn_w_in"], wo=w["ffn_w_out"].reshape(D_FF, D))

    def gdn_params(l, w):
        w_in = w["gdn_w_in"].transpose(1, 0, 2).reshape(D, 4 * D + 2 * HEADS)
        pad = lambda t: jnp.pad(t, ((0, 0), (0, 128 - HEADS)))
        return dict(gain=row(norm_g[l, 1]), shift=row(mod[l, 3]), scale=row(mod[l, 4]), gm=row(mod[l, 5]),
                    w_main=w_in[:, :4 * D],
                    w_tail=jnp.concatenate([pad(w_in[:, 4 * D:4 * D + HEADS]), pad(w_in[:, 4 * D + HEADS:])], axis=1),
                    conv_w=conv_w[l], a_log=_pad128(row(a["gdn_a_log"][l])), dt_bias=_pad128(row(a["gdn_dt_bias"][l])),
                    norm_g=row(a["gdn_norm_g"][l]), w_out=w["gdn_w_out"].reshape(D, D))

    def mla_params(l, w):
        j = l - N_A
        uq = w["mla_w_uq"].transpose(1, 0, 2)
        qg = _dup_rope(a["mla_q_norm_g"][j])
        return dict(gain=row(norm_g[l, 1]), shift=row(mod[l, 3]), scale=row(mod[l, 4]), gm=row(mod[l, 5]),
                    w_dq=w["mla_w_dq"].reshape(D, Q_LORA), q_lora_g=row(a["mla_q_lora_norm_g"][j]),
                    w_uq=_dup_rope(uq).reshape(Q_LORA, HEADS * HEAD_PAD), q_gn=row(qg[:NOPE]), q_gr=row(qg[NOPE:]),
                    w_out=w["mla_w_out"].reshape(D, D))

    def kv_params(w):
        w_dkv = w["mla_w_dkv"].reshape(D, KV_LORA + ROPE)
        kg = _dup_rope(a["mla_k_norm_g"])
        return dict(gain=row(a["kv_norm_g"]), shift=row(kvmod[:D]), scale=row(kvmod[D:]),
                    w_dkv=jnp.concatenate([w_dkv, w_dkv[:, KV_LORA:]], axis=1), kv_g=row(a["mla_kv_norm_g"]),
                    w_ukv=w["mla_w_ukv"].transpose(1, 0, 2).reshape(KV_LORA, HEADS * 2 * HEAD), k_gn=row(kg[:NOPE]),
                    k_gr=row(kg[NOPE:]))

    tapes, kv, kv_p, h = [[] for _ in range(DEPTH)], None, None, None
    first = {name: _all_gather((w + zero_of(mod)).astype(BF16), "fetch_first_" + name)
             for name, w in stage_shards(0, 0).items()}
    pending = []
    for l, part in stages[1:1 + AHEAD]:
        tie = pending[-1][1]["token"][0, 0] if pending else zero_of(first["ffn_w_out"])
        pending.append(start_stage(l, part, tie))
    for n, (l, part) in enumerate(stages):
        if n == 0:
            w, ties = first, tuple(h["token"] for _, h in pending)
        else:
            names, handle = pending.pop(0)
            w = finish_stage(l, part, names, handle, x)
            ties = ()
            if n + AHEAD < len(stages):
                pending.append(start_stage(*stages[n + AHEAD], zero_of(w[names[0]])))
                ties = (pending[-1][1]["token"],)
        nxt = None
        if n + 1 < len(stages):
            l2, part2 = stages[n + 1]
            k2 = 3 * part2
            nxt = (row(norm_g[l2, part2]), row(mod[l2, k2 + 1]), row(mod[l2, k2]))
        if part != 1:
            p = ffn_params(l, part // 2, w)
            x, t, h = _ffn_fwd(x, p, ties, h, nxt)
        else:
            p = gdn_params(l, w) if l < N_A else mla_params(l, w)
            x, t, h = _gdn_fwd(x, p, ties, h, nxt) if l < N_A else _mla_fwd(x, p, kv, ties, h, nxt)
        tapes[l] += [p, t]
        if part == 2 and l == N_A - 1:
            kv_p = kv_params(w)
            kv = _kv_fwd(x, kv_p, cosf, sins)
    dx, loss_blk = _loss_and_grad(x, a["loss_target"][0], "loss")
    loss = lax.psum(loss_blk[0, 0], ("x", "y", "c"))

    def by_cols(g, n):
        return g.reshape(g.shape[0], -1, n).transpose(1, 0, 2)

    def ffn_blocks(g):
        return {"ffn_w_in": g["w_in"], "ffn_w_out": g["wo"].reshape(N_DEV, D_FF // N_DEV, D)}

    def mixer_blocks(l, g):
        if l < N_A:
            full = jnp.concatenate([g["w_main"], g["w_tail"][:, :HEADS], g["w_tail"][:, 128:128 + HEADS]], axis=1)
            return {"gdn_w_in": by_cols(full, n_gdn), "gdn_w_out": g["w_out"].reshape(N_DEV, D // N_DEV, D)}
        return {"mla_w_dq": g["w_dq"].reshape(N_DEV, D // N_DEV, Q_LORA),
                "mla_w_uq": _fold_rope(g["w_uq"].reshape(Q_LORA, HEADS, HEAD_PAD)).transpose(1, 0, 2),
                "mla_w_out": g["w_out"].reshape(N_DEV, D // N_DEV, D)}

    sent = []

    def send(key, blocks, tie=0.0):
        handle = _send_start([(b + tie).astype(BF16) for b in blocks.values()], "grad_start_" + "_".join(map(str, key)),
                             gather=False)
        sent.append((key, list(blocks), handle))
        return (handle["token"],)

    grads = [None] * DEPTH
    dk_sum = dv_sum = kv_grads = res = None
    ties = ()
    for l in reversed(range(DEPTH)):
        p1, t1, pm_, tm_, p2, t2 = tapes[l]
        if l == N_A - 1:
            dx, kv_grads, res = _kv_bwd(kv, kv_p, dk_sum, dv_sum, dx, (t2["y"], p2["gm"]))
            d_dkv = kv_grads["w_dkv"]
            ties += send((l, 3), {
                "mla_w_dkv": jnp.concatenate(
                    [d_dkv[:, :KV_LORA], d_dkv[:, KV_LORA:KV_LORA + ROPE] + d_dkv[:, KV_LORA + ROPE:]],
                    axis=1).reshape(N_DEV, D // N_DEV, KV_LORA + ROPE),
                "mla_w_ukv": by_cols(kv_grads["w_ukv"], 2 * HEAD)})
        dx, g2, res = _ffn_bwd(t2, p2, dx, res, (tm_["y"], pm_["gm"]), ties)
        ties = send((l, 2), ffn_blocks(g2))
        if l < N_A:
            dx, gm_, res = _gdn_bwd(tm_, pm_, dx, res, (t1["y"], p1["gm"]), ties)
        else:
            dx, gm_, res, dk_sum, dv_sum = _mla_bwd(tm_, pm_, kv, dx, res, (t1["y"], p1["gm"]), ties,
                                                    None if dk_sum is None else (dk_sum, dv_sum))
        ties = send((l, 1), mixer_blocks(l, gm_))
        prev = (tapes[l - 1][5]["y"], tapes[l - 1][4]["gm"]) if l > 0 and l != N_A else None
        if l > 0:
            dx, g1, res = _ffn_bwd(t1, p1, dx, res, prev, ties)
            ties = send((l, 0), ffn_blocks(g1))
        else:
            dx, g1, res = _ffn_bwd(t1, p1, dx, res, prev, ties, early=lambda dwo: send(
                (0, 0, "out"), {"ffn_w_out": dwo.reshape(N_DEV, D_FF // N_DEV, D)}))
        grads[l] = (g1, gm_, g2)

    out = {}
    def dmod(l):
        g1, gm_, g2 = grads[l]
        return jnp.concatenate([g1["shift"], g1["scale"], 0.5 * g1["gm"], gm_["shift"], gm_["scale"], gm_["gm"],
                                g2["shift"], g2["scale"], 0.5 * g2["gm"]], axis=1)

    gdn = [grads[l][1] for l in range(N_A)]
    mla = [grads[l][1] for l in range(N_A, DEPTH)]
    small = {
        "ada_b": jnp.concatenate([dmod(l) for l in range(DEPTH)], axis=0),
        "kv_ada_b": jnp.concatenate([kv_grads["shift"], kv_grads["scale"]], axis=1),
        "norm_g": jnp.stack([jnp.concatenate([grads[l][0]["gain"], grads[l][1]["gain"], grads[l][2]["gain"]], axis=0)
                             for l in range(DEPTH)]),
        "gdn_conv_w": jnp.stack([g["conv_w"] for g in gdn]),
        "gdn_a_log": jnp.stack([g["a_log"][0, :HEADS] for g in gdn]),
        "gdn_dt_bias": jnp.stack([g["dt_bias"][0, :HEADS] for g in gdn]),
        "gdn_norm_g": jnp.stack([g["norm_g"][0] for g in gdn]),
        "kv_norm_g": kv_grads["gain"],
        "mla_kv_norm_g": kv_grads["kv_g"],
        "mla_k_norm_g": _fold_rope(jnp.concatenate([kv_grads["k_gn"], kv_grads["k_gr"]], axis=1)),
        "mla_q_lora_norm_g": jnp.stack([g["q_lora_g"][0] for g in mla]),
        "mla_q_norm_g": jnp.stack([_fold_rope(jnp.concatenate([g["q_gn"], g["q_gr"]], axis=1))[0] for g in mla]),
    }
    rows = 616
    assert sum(n for _, n in SMALL) <= rows * 128 and all(small[n].size == k for n, k in SMALL)
    small_recv = _all_gather(_pack([small[n] for n, _ in SMALL], rows), "gather_small_grads")
    small_recv = small_recv + send((0, 0), {"ffn_w_in": grads[0][0]["w_in"]}, zero_of(small_recv))[0][0, 0]
    zero = lambda n, k: jnp.zeros((k,), F32)
    packed = {pre: _pack([a[pre + n] if n in SMALL_REPLICATED else zero(n, k) for n, k in SMALL], rows)
              for pre in ("", "m_", "v_")}
    res = _adamw([small_recv], packed[""], packed["m_"], packed["v_"], "adamw_small")
    offs = {}
    o = 0
    for n, k in SMALL:
        offs[n] = o
        o += k
    for n, k in SMALL:
        if n in SMALL_REPLICATED:
            out[n] = [r.reshape(-1)[offs[n]:offs[n] + k] for r in res]
    gsum = res[0].reshape(-1)
    g_norm = lax.dynamic_slice_in_dim(gsum[offs["norm_g"]:offs["norm_g"] + DEPTH * 3 * D].reshape(DEPTH * 3, D),
                                      me * HEAD, HEAD, axis=1)
    g_conv = lax.dynamic_slice_in_dim(
        gsum[offs["gdn_conv_w"]:offs["gdn_conv_w"] + N_A * CONV_K * 3 * D].reshape(N_A * CONV_K, 3 * D),
        me * 3 * HEAD, 3 * HEAD, axis=1)
    res2 = _adamw([_pack([g_norm, g_conv], 36)[None]], *[_pack([a[pre + "norm_g"], a[pre + "gdn_conv_w"]], 36)
                                                      for pre in ("", "m_", "v_")], "adamw_small")
    out["norm_g"] = [r.reshape(-1)[:n_ng] for r in res2]
    out["gdn_conv_w"] = [r.reshape(-1)[n_ng:n_ng + n_cw] for r in res2]

    c_act_t = c_act.T
    all_small = small_recv.reshape(N_DEV, -1)
    dmod_all = all_small[:, :DEPTH * N_MOD * D].reshape(N_DEV, DEPTH, N_MOD * D)
    dmod_mine = lax.dynamic_slice_in_dim(dmod_all, me * n_ada, n_ada, axis=2)
    g_ada = [_outer8(c_act_t, dmod_mine[:, l], "ada_grad")[None] for l in range(DEPTH)]
    out["ada_w"] = _adamw(g_ada, *[a[pre + "ada_w"].reshape(DEPTH * D, n_ada) for pre in ("", "m_", "v_")], "adamw")
    dkv_all = all_small[:, offs["kv_ada_b"]:offs["kv_ada_b"] + 2 * D]
    g_kv = _outer8(c_act_t, lax.dynamic_slice_in_dim(dkv_all, me * (2 * D // N_DEV), 2 * D // N_DEV, axis=1), "ada_grad")
    out["kv_ada_w"] = _adamw([g_kv[None]], *[a[pre + "kv_ada_w"] for pre in ("", "m_", "v_")], "adamw")

    pieces = {}
    for key, names, handle in sent:
        srcs, lands = _send_wait(handle, out["kv_ada_w"][0], "grad_wait_" + "_".join(map(str, key)), gather=False)
        for name, src, land in zip(names, srcs, lands):
            own = lax.dynamic_slice_in_dim(src, me, 1, axis=0)
            pieces.setdefault(name, []).append((key, lax.dynamic_update_slice(land, own, (me, 0, 0))))
    for name, parts in pieces.items():
        wide = a[name].shape[-1]
        out[name] = _adamw([p for _, p in sorted(parts, key=lambda kp: kp[0])], a[name].reshape(-1, wide),
                           a["m_" + name].reshape(-1, wide), a["v_" + name].reshape(-1, wide), "adamw")

    result = [loss, dx[None]]
    for k in range(4):
        result += [out[n][k].reshape(a[n].shape) for n in WEIGHTS]
    return tuple(result)


def kernel(x, c, positions, ada_w, ada_b, norm_g, ffn_w_in, ffn_w_out, gdn_w_in, gdn_conv_w, gdn_a_log, gdn_dt_bias, gdn_norm_g, gdn_w_out, kv_ada_w, kv_ada_b, kv_norm_g, mla_w_dkv, mla_kv_norm_g, mla_w_ukv, mla_k_norm_g, mla_w_dq, mla_q_lora_norm_g, mla_w_uq, mla_q_norm_g, mla_w_out, loss_target, m_ada_w, m_ada_b, m_norm_g, m_ffn_w_in, m_ffn_w_out, m_gdn_w_in, m_gdn_conv_w, m_gdn_a_log, m_gdn_dt_bias, m_gdn_norm_g, m_gdn_w_out, m_kv_ada_w, m_kv_ada_b, m_kv_norm_g, m_mla_w_dkv, m_mla_kv_norm_g, m_mla_w_ukv, m_mla_k_norm_g, m_mla_w_dq, m_mla_q_lora_norm_g, m_mla_w_uq, m_mla_q_norm_g, m_mla_w_out, v_ada_w, v_ada_b, v_norm_g, v_ffn_w_in, v_ffn_w_out, v_gdn_w_in, v_gdn_conv_w, v_gdn_a_log, v_gdn_dt_bias, v_gdn_norm_g, v_gdn_w_out, v_kv_ada_w, v_kv_ada_b, v_kv_norm_g, v_mla_w_dkv, v_mla_kv_norm_g, v_mla_w_ukv, v_mla_k_norm_g, v_mla_w_dq, v_mla_q_lora_norm_g, v_mla_w_uq, v_mla_q_norm_g, v_mla_w_out):
    return _step(dict(locals()))
```

```python
import functools

import jax
import jax.numpy as jnp
from jax import lax
from jax.experimental import pallas as pl
from jax.experimental.pallas import tpu as pltpu

F32 = jnp.float32
BF16 = jnp.bfloat16

N_DEV = 8
D = 1024
D_FF = 2816
DEPTH = 4
N_A = 2
N_MOD = 9
HEADS = 8
HEAD = 128
CHUNK = 64
CONV_K = 4
KV_LORA = 256
Q_LORA = 384
NOPE = 128
ROPE = 64
QK_HEAD = NOPE + ROPE
HEAD_PAD = 256
ROPE_BASE = 10000.0
EPS = 1e-6
LR, B1, B2, ADAM_EPS, WD, STEP = 0.001, 0.9, 0.999, 1e-08, 0.01, 10

VMEM_LIMIT = 48 * 1024 * 1024
ROW_TILE = 256
MESH = pl.DeviceIdType.MESH

_NN = (((1,), (0,)), ((), ()))
_NT = (((1,), (1,)), ((), ()))
_TN = (((0,), (0,)), ((), ()))
_DIMS = {"nn": _NN, "nt": _NT, "tn": _TN}


def _params(dims=None):
    return pltpu.CompilerParams(dimension_semantics=dims, vmem_limit_bytes=VMEM_LIMIT)


def _tile(n, target):
    for t in range(target - target % 128, 0, -128):
        if n % t == 0:
            return t
    return n


_TIE_SPEC1 = pl.BlockSpec((8, 128), lambda i: (0, 0))
_TIE_SPEC2 = pl.BlockSpec((8, 128), lambda i, j: (0, 0))
_TIE_SPEC3 = pl.BlockSpec((8, 128), lambda i, j, k: (0, 0))


def _matmul(pairs, form, name, out_dtype=F32, tm=1408, tn=1408, tk=1408, boffs=None, resid=None, ties=()):
    a0, b0 = pairs[0]
    if form == "nn":
        m, n = a0.shape[0], b0.shape[1]
        ks = [a.shape[1] for a, _ in pairs]
    elif form == "nt":
        m, n = a0.shape[0], b0.shape[0]
        ks = [a.shape[1] for a, _ in pairs]
    else:
        m, n = a0.shape[1], b0.shape[1]
        ks = [a.shape[0] for a, _ in pairs]
    tm, tn = _tile(m, tm), _tile(n, tn)
    tks = [_tile(k, tk) for k in ks]
    boffs = boffs or [0] * len(pairs)
    assert m % tm == 0 and n % tn == 0 and all(o % t == 0 for o, t in zip(boffs, tks)), (name, m, n, ks)
    steps = [k // t for k, t in zip(ks, tks)]
    starts = [sum(steps[:p]) for p in range(len(pairs))]
    nk = sum(steps)

    def kidx(p, k):
        return jnp.clip(k - starts[p], 0, steps[p] - 1)

    in_specs, args = [], []
    for p, (a, b) in enumerate(pairs):
        t = tks[p]
        if form == "tn":
            in_specs.append(pl.BlockSpec((t, tm), lambda i, j, k, p=p: (kidx(p, k), i)))
            in_specs.append(pl.BlockSpec((t, tn), lambda i, j, k, p=p: (kidx(p, k), j)))
        elif form == "nn":
            in_specs.append(pl.BlockSpec((tm, t), lambda i, j, k, p=p: (i, kidx(p, k))))
            in_specs.append(pl.BlockSpec((t, tn), lambda i, j, k, p=p: (kidx(p, k), j)))
        else:
            in_specs.append(pl.BlockSpec((tm, t), lambda i, j, k, p=p: (i, kidx(p, k))))
            in_specs.append(pl.BlockSpec((tn, t), lambda i, j, k, p=p, o=boffs[p] // t: (j, kidx(p, k) + o)))
        args += [a, b]
    dims = _DIMS[form]
    npairs = len(pairs)
    nres = len(resid or ())
    nin = 2 * npairs + len(ties) + nres
    out_blk = pl.BlockSpec((tm, tn), lambda i, j, k: (i, j))
    in_specs += [_TIE_SPEC3] * len(ties)
    args += list(ties)
    if resid:
        assert nres == 2 or (nres == 5 and tn == n)
        in_specs += [out_blk] + [pl.BlockSpec((1, tn), lambda i, j, k: (0, j))] * (nres - 1)
        args += list(resid)

    def body(*refs):
        o_ref = refs[nin]
        k = pl.program_id(2)

        def prod(p):
            return lax.dot_general(refs[2 * p][...].astype(BF16), refs[2 * p + 1][...].astype(BF16), dims,
                                   preferred_element_type=F32)

        def finish(y):
            o_ref[...] = y.astype(o_ref.dtype)
            if resid:
                x_ref, gate_ref = refs[nin - nres], refs[nin - nres + 1]
                xn = x_ref[...] + gate_ref[...] * y
                refs[nin + 1][...] = xn
                if nres == 5:
                    gain, scale, shift = (r[...] for r in refs[nin - 3:nin])
                    refs[nin + 2][...] = _modulate_fn(0, xn, gain, scale, shift)[0].astype(BF16)

        if nk == 1:
            finish(prod(0))
            return
        acc = refs[-1]

        @pl.when(k == 0)
        def _():
            acc[...] = jnp.zeros_like(acc)

        for p in range(npairs):
            @pl.when((k >= starts[p]) & (k < starts[p] + steps[p]))
            def _(p=p):
                acc[...] += prod(p)

        @pl.when(k == nk - 1)
        def _():
            finish(acc[...])

    res = pl.pallas_call(
        body, name=name, grid=(m // tm, n // tn, nk), in_specs=in_specs,
        out_specs=[out_blk] * (2 + (nres == 5)) if resid else out_blk,
        out_shape=([jax.ShapeDtypeStruct((m, n), out_dtype), jax.ShapeDtypeStruct((m, n), F32)]
                   + [jax.ShapeDtypeStruct((m, n), BF16)] * (nres == 5))
        if resid else jax.ShapeDtypeStruct((m, n), out_dtype),
        scratch_shapes=[] if nk == 1 else [pltpu.VMEM((tm, tn), F32)],
        compiler_params=_params(("parallel", "parallel", "arbitrary")),
    )(*args)
    return res


def _mm(a, b, form, name, **kw):
    return _matmul([(a, b)], form, name, **kw)


def _cols(spec, g):
    return spec[g] if isinstance(spec, list) else spec


def _rowwise_fwd(fn, rows, pars, outs, name, groups=1, ts=ROW_TILE, ties=()):
    s = rows[0][0].shape[0]
    ts = min(ts, s)
    assert s % ts == 0
    nr, npar = len(rows), len(pars)

    def body(*refs):
        par_t = [r[...] for r in refs[nr:nr + npar]]
        out_refs = refs[nr + npar + len(ties):]
        for g in range(groups):
            row_t = []
            for r, (_, spec) in zip(refs[:nr], rows):
                c0, w = _cols(spec, g)
                row_t.append(r[:, c0:c0 + w].astype(F32))
            res = fn(g, *row_t, *par_t)
            for o_ref, val, (_, _, spec) in zip(out_refs, res, outs):
                c0, w = _cols(spec, g)
                o_ref[:, c0:c0 + w] = val.astype(o_ref.dtype)

    return pl.pallas_call(
        body, name=name, grid=(s // ts,),
        in_specs=[pl.BlockSpec((ts, a.shape[1]), lambda i: (i, 0)) for a, _ in rows]
        + [pl.BlockSpec(p.shape, lambda i: (0, 0)) for p in pars] + [_TIE_SPEC1] * len(ties),
        out_specs=[pl.BlockSpec((ts, w), lambda i: (i, 0)) for w, _, _ in outs],
        out_shape=[jax.ShapeDtypeStruct((s, w), dt) for w, dt, _ in outs],
        compiler_params=_params(("parallel",)),
    )(*[a for a, _ in rows], *pars, *ties)


def _rowwise_bwd(fn, rows, pars, outs, douts, gmap, gshapes, name, groups=1, add=None, par_grads=True,
                 ts=ROW_TILE):
    s = rows[0][0].shape[0]
    ts = min(ts, s)
    assert s % ts == 0
    nr, npar, nout, ng = len(rows), len(pars), len(outs), len(gshapes)
    add = add or {}
    add_keys = sorted(add)

    def body(*refs):
        row_refs = refs[:nr]
        par_refs = refs[nr:nr + npar]
        dout_refs = refs[nr + npar:nr + npar + nout]
        add_refs = refs[nr + npar + nout:nr + npar + nout + len(add_keys)]
        g_refs = refs[nr + npar + nout + len(add_keys):][:ng]
        pg_refs = refs[nr + npar + nout + len(add_keys) + ng:]
        par_t = [r[...] for r in par_refs]
        par_acc = [None] * npar
        shared_acc = {}
        for g in range(groups):
            row_t = []
            for r, (_, spec) in zip(row_refs, rows):
                c0, w = _cols(spec, g)
                row_t.append(r[:, c0:c0 + w].astype(F32))
            cts = []
            for r, (_, _, spec) in zip(dout_refs, outs):
                c0, w = _cols(spec, g)
                cts.append(r[:, c0:c0 + w].astype(F32))
            _, vjp = jax.vjp(lambda *t, g=g: tuple(fn(g, *t)), *row_t, *par_t)
            grads = vjp(tuple(cts))
            for k in range(nr):
                if gmap[k] is None:
                    continue
                gi, spec = gmap[k]
                if isinstance(spec, list) or groups == 1:
                    c0, w = _cols(spec, g)
                    val = grads[k]
                    if gi in add:
                        val = val + add_refs[add_keys.index(gi)][:, c0:c0 + w].astype(F32)
                    g_refs[gi][:, c0:c0 + w] = val.astype(g_refs[gi].dtype)
                else:
                    shared_acc[k] = grads[k] if k not in shared_acc else shared_acc[k] + grads[k]
            if par_grads:
                for k in range(npar):
                    pg = grads[nr + k]
                    par_acc[k] = pg if par_acc[k] is None else par_acc[k] + pg
        for k, val in shared_acc.items():
            gi, (c0, w) = gmap[k]
            assert gi not in add
            g_refs[gi][:, c0:c0 + w] = val.astype(g_refs[gi].dtype)
        if par_grads:
            first = pl.program_id(0) == 0
            for k in range(npar):
                @pl.when(first)
                def _(k=k):
                    pg_refs[k][...] = par_acc[k]

                @pl.when(jnp.logical_not(first))
                def _(k=k):
                    pg_refs[k][...] += par_acc[k]

    out_specs = [pl.BlockSpec((ts, w), lambda i: (i, 0)) for w, _ in gshapes]
    out_shape = [jax.ShapeDtypeStruct((s, w), dt) for w, dt in gshapes]
    if par_grads:
        out_specs += [pl.BlockSpec(p.shape, lambda i: (0, 0)) for p in pars]
        out_shape += [jax.ShapeDtypeStruct(p.shape, F32) for p in pars]
    return pl.pallas_call(
        body, name=name, grid=(s // ts,),
        in_specs=[pl.BlockSpec((ts, a.shape[1]), lambda i: (i, 0)) for a, _ in rows]
        + [pl.BlockSpec(p.shape, lambda i: (0, 0)) for p in pars]
        + [pl.BlockSpec((ts, a.shape[1]), lambda i: (i, 0)) for a in douts]
        + [pl.BlockSpec((ts, add[k].shape[1]), lambda i: (i, 0)) for k in add_keys],
        out_specs=out_specs, out_shape=out_shape,
        compiler_params=_params(("arbitrary",)),
    )(*[a for a, _ in rows], *pars, *douts, *[add[k] for k in add_keys])


def _sigmoid(x):
    return 1.0 / (1.0 + jnp.exp(-x))


def _silu(x):
    return x * _sigmoid(x)


def _softplus(x):
    return jnp.maximum(x, 0.0) + jnp.log(1.0 + jnp.exp(-jnp.abs(x)))


def _rms(t, g, n=None):
    n = n or t.shape[-1]
    return t * lax.rsqrt(jnp.sum(t * t, axis=-1, keepdims=True) / n + EPS) * g


def _modulate_fn(g, x, gain, scale, shift):
    return (_rms(x, gain) * (1.0 + scale) + shift,)


def _gate_only_fn(g, y, gm):
    return (gm * y,)


def _gdn_gates_fn(g, b_logit, a_logit, a_log, dt_bias):
    gate = -jnp.exp(a_log) * _softplus(a_logit + dt_bias)
    n = gate.shape[0]
    i = lax.broadcasted_iota(jnp.int32, (n, n), 0)
    j = lax.broadcasted_iota(jnp.int32, (n, n), 1)
    tri = (((i // CHUNK) == (j // CHUNK)) & (i >= j)).astype(F32)
    gcum = lax.dot_general(tri, gate, _NN, preferred_element_type=F32, precision=lax.Precision.HIGHEST)
    return _sigmoid(b_logit), gcum


def _gdn_outnorm_fn(g, o, z, gain):
    return (_rms(o, gain) * _silu(z),)


def _rms_fn(g, t, gain):
    return (_rms(t, gain),)


@jax.custom_vjp
def _swap_halves(t):
    return pltpu.roll(t, 32, 1)


_swap_halves.defvjp(lambda t: (pltpu.roll(t, 32, 1), None), lambda _, ct: (pltpu.roll(ct, 96, 1),))


def _head_norm_rope_fn(g, nope, rope, cosf, sins, gain_n, gain_r):
    first = lax.broadcasted_iota(jnp.int32, rope.shape, 1) < ROPE
    ss = jnp.sum(nope * nope, axis=-1, keepdims=True) + jnp.sum(jnp.where(first, rope * rope, 0.0), axis=-1,
                                                                 keepdims=True)
    r = lax.rsqrt(ss / QK_HEAD + EPS)
    tn = nope * r * gain_n
    tr = rope * r * gain_r
    rot = jnp.where(first, tr * cosf + _swap_halves(tr) * sins, 0.0)
    return tn, rot


def _q_norm_rope_fn(g, nope, rope, cosf, sins, gain_n, gain_r):
    tn, rot = _head_norm_rope_fn(g, nope, rope, cosf, sins, gain_n, gain_r)
    return (jnp.concatenate([tn, rot], axis=1),)


def _k_norm_rope_fn(g, nope, val, rope, cosf, sins, gain_n, gain_r):
    tn, rot = _head_norm_rope_fn(g, nope, rope, cosf, sins, gain_n, gain_r)
    return jnp.concatenate([tn, rot], axis=1), val


FF_SH = 2 * D_FF // N_DEV
FF_G = N_DEV // 2


def _ffn_in(h, w_in, name, tm=1024, ties=()):
    s = h.shape[0]
    tm = min(tm, s)

    def body(h_ref, wg_ref, wu_ref, *rest):
        g_ref, u_ref, a_ref = rest[-3:]
        hb = h_ref[...]
        gate = jnp.dot(hb, wg_ref[...], preferred_element_type=F32)
        up = jnp.dot(hb, wu_ref[...], preferred_element_type=F32)
        sg = _sigmoid(gate)
        silu = gate * sg
        g_ref[...] = (up * (sg * (1.0 + gate * (1.0 - sg)))).astype(BF16)
        u_ref[...] = silu.astype(BF16)
        a_ref[...] = (silu * up).astype(BF16)

    spec = pl.BlockSpec((None, tm, FF_SH), lambda j, i: (j, i, 0))
    return pl.pallas_call(
        body, name=name, grid=(FF_G, s // tm),
        in_specs=[pl.BlockSpec((tm, D), lambda j, i: (i, 0)), pl.BlockSpec((None, D, FF_SH), lambda j, i: (j, 0, 0)),
                  pl.BlockSpec((None, D, FF_SH), lambda j, i: (j + FF_G, 0, 0))] + [_TIE_SPEC2] * len(ties),
        out_specs=[spec, spec, spec], out_shape=[jax.ShapeDtypeStruct((FF_G, s, FF_SH), BF16)] * 3,
        compiler_params=_params(("parallel", "parallel")),
    )(h, w_in, w_in, *ties)


def _ffn_out(act, wo, resid, name, tm=512):
    s = act.shape[1]
    tm = min(tm, s)
    nres = len(resid)

    def body(a_ref, b_ref, x_ref, gate_ref, *rest):
        mods, outs = rest[:nres - 2], rest[nres - 2:]
        y = jnp.dot(a_ref[0], b_ref[0:FF_SH, :], preferred_element_type=F32)
        for k in range(1, FF_G):
            y = y + jnp.dot(a_ref[k], b_ref[k * FF_SH:(k + 1) * FF_SH, :], preferred_element_type=F32)
        xn = x_ref[...] + gate_ref[...] * y
        outs[0][...] = y.astype(BF16)
        outs[1][...] = xn
        if mods:
            outs[2][...] = _modulate_fn(0, xn, *[m[...] for m in mods])[0].astype(BF16)

    blk = pl.BlockSpec((tm, D), lambda i: (i, 0))
    vec = pl.BlockSpec((1, D), lambda i: (0, 0))
    return pl.pallas_call(
        body, name=name, grid=(s // tm,),
        in_specs=[pl.BlockSpec((FF_G, tm, FF_SH), lambda i: (0, i, 0)), pl.BlockSpec((D_FF, D), lambda i: (0, 0)),
                  blk] + [vec] * (nres - 1),
        out_specs=[blk] * (2 + (nres == 5)),
        out_shape=[jax.ShapeDtypeStruct((s, D), BF16), jax.ShapeDtypeStruct((s, D), F32)]
        + [jax.ShapeDtypeStruct((s, D), BF16)] * (nres == 5),
        compiler_params=_params(("parallel",)),
    )(act, wo, *resid)


def _ffn_bwd_act(dy, wo, act_dgate, act_dup, name, tm=1024, ties=()):
    s = dy.shape[0]
    tm = min(tm, s)

    def body(dy_ref, wo_ref, g_ref, u_ref, *rest):
        dg_ref, du_ref = rest[-2:]
        dact = lax.dot_general(dy_ref[...], wo_ref[...], _NT, preferred_element_type=F32)
        dg_ref[...] = (dact * g_ref[...].astype(F32)).astype(BF16)
        du_ref[...] = (dact * u_ref[...].astype(F32)).astype(BF16)

    spec = pl.BlockSpec((None, tm, FF_SH), lambda j, i: (j, i, 0))
    return pl.pallas_call(
        body, name=name, grid=(FF_G, s // tm),
        in_specs=[pl.BlockSpec((tm, D), lambda j, i: (i, 0)), pl.BlockSpec((FF_SH, D), lambda j, i: (j, 0)), spec, spec]
        + [_TIE_SPEC2] * len(ties),
        out_specs=[spec, spec], out_shape=[jax.ShapeDtypeStruct((FF_G, s, FF_SH), BF16)] * 2,
        compiler_params=_params(("parallel", "parallel")),
    )(dy, wo, act_dgate, act_dup, *ties)


def _ffn_dwo(act, dy, name, tk=2048, ties=()):
    s = act.shape[1]
    tk = min(tk, s)

    def body(a_ref, b_ref, *rest):
        o_ref, acc = rest[-2:]
        k = pl.program_id(1)

        @pl.when(k == 0)
        def _():
            acc[...] = jnp.zeros_like(acc)

        acc[...] += lax.dot_general(a_ref[...], b_ref[...], _TN, preferred_element_type=F32)

        @pl.when(k == s // tk - 1)
        def _():
            o_ref[...] = acc[...].astype(BF16)

    return pl.pallas_call(
        body, name=name, grid=(FF_G, s // tk),
        in_specs=[pl.BlockSpec((None, tk, FF_SH), lambda j, k: (j, k, 0)), pl.BlockSpec((tk, D), lambda j, k: (k, 0))]
        + [_TIE_SPEC2] * len(ties),
        out_specs=pl.BlockSpec((FF_SH, D), lambda j, k: (j, 0)), out_shape=jax.ShapeDtypeStruct((D_FF, D), BF16),
        scratch_shapes=[pltpu.VMEM((FF_SH, D), F32)], compiler_params=_params(("parallel", "arbitrary")),
    )(act, dy, *ties)


def _ffn_halves(k, gate_ref, up_ref, fn):
    pl.when(k < FF_G)(functools.partial(fn, gate_ref))
    pl.when(k >= FF_G)(functools.partial(fn, up_ref))


def _ffn_dh(dgate, dup, w_in, name, tm=512, ties=()):
    s = dgate.shape[1]
    tm = min(tm, s)

    def body(dg_ref, du_ref, w_ref, *rest):
        acc = lax.dot_general(dg_ref[0], w_ref[0], _NT, preferred_element_type=F32)
        for k in range(1, N_DEV):
            d_ref = dg_ref if k < FF_G else du_ref
            acc = acc + lax.dot_general(d_ref[k % FF_G], w_ref[k], _NT, preferred_element_type=F32)
        rest[-1][...] = acc

    half = pl.BlockSpec((FF_G, tm, FF_SH), lambda i: (0, i, 0))
    return pl.pallas_call(
        body, name=name, grid=(s // tm,),
        in_specs=[half, half, pl.BlockSpec((N_DEV, D, FF_SH), lambda i: (0, 0, 0))] + [_TIE_SPEC1] * len(ties),
        out_specs=pl.BlockSpec((tm, D), lambda i: (i, 0)), out_shape=jax.ShapeDtypeStruct((s, D), F32),
        compiler_params=_params(("parallel",)),
    )(dgate, dup, w_in, *ties)


def _ffn_dwi(h, dgate, dup, name, tk=2048):
    s = h.shape[0]
    tk = min(tk, s)

    def body(h_ref, dg_ref, du_ref, o_ref, acc):
        j, k = pl.program_id(0), pl.program_id(1)

        @pl.when(k == 0)
        def _():
            acc[...] = jnp.zeros_like(acc)

        def add(d_ref):
            acc[...] += lax.dot_general(h_ref[...], d_ref[...], _TN, preferred_element_type=F32)

        _ffn_halves(j, dg_ref, du_ref, add)

        @pl.when(k == s // tk - 1)
        def _():
            o_ref[...] = acc[...].astype(BF16)

    return pl.pallas_call(
        body, name=name, grid=(N_DEV, s // tk),
        in_specs=[pl.BlockSpec((tk, D), lambda j, k: (k, 0)),
                  pl.BlockSpec((None, tk, FF_SH), lambda j, k: (jnp.minimum(j, FF_G - 1), jnp.where(j < FF_G, k, s // tk - 1), 0)),
                  pl.BlockSpec((None, tk, FF_SH), lambda j, k: (jnp.maximum(j - FF_G, 0), jnp.where(j < FF_G, 0, k), 0))],
        out_specs=pl.BlockSpec((None, D, FF_SH), lambda j, k: (j, 0, 0)),
        out_shape=jax.ShapeDtypeStruct((N_DEV, D, FF_SH), BF16),
        scratch_shapes=[pltpu.VMEM((D, FF_SH), F32)], compiler_params=_params(("parallel", "arbitrary")),
    )(h, dgate, dup)


def _shift_down(x, d):
    rows = lax.broadcasted_iota(jnp.int32, x.shape, 0)
    return jnp.where(rows >= d, pltpu.roll(x, d, 0), 0.0)


def _shift_up(x, d):
    n = x.shape[0]
    rows = lax.broadcasted_iota(jnp.int32, x.shape, 0)
    return jnp.where(rows < n - d, pltpu.roll(x, n - d, 0), 0.0)


def _conv_post(pre, is_qk):
    a = _silu(pre)
    l2 = a * lax.rsqrt(jnp.sum(a * a, axis=-1, keepdims=True) + EPS)
    return jnp.where(is_qk, l2, a)


def _conv_taps(x):
    return [_shift_down(x, CONV_K - 1 - j) for j in range(CONV_K - 1)] + [x]


def _conv_pre(x, w, taps=None):
    taps = taps or _conv_taps(x)
    pre = taps[0] * w[0:1, :]
    for j in range(1, CONV_K):
        pre = pre + taps[j] * w[j:j + 1, :]
    return pre


def _gdn_conv_fwd(pm, conv_w, name):
    s = pm.shape[0]
    nblk = 3 * D // HEAD

    def body(x_ref, w_ref, o_ref):
        is_qk = pl.program_id(0) < 2 * HEADS
        o_ref[...] = _conv_post(_conv_pre(x_ref[...], w_ref[...]), is_qk)

    return pl.pallas_call(
        body, name=name, grid=(nblk,),
        in_specs=[pl.BlockSpec((s, HEAD), lambda c: (0, c)), pl.BlockSpec((CONV_K, HEAD), lambda c: (0, c))],
        out_specs=pl.BlockSpec((s, HEAD), lambda c: (0, c)),
        out_shape=jax.ShapeDtypeStruct((s, 3 * D), F32), compiler_params=_params(("parallel",)),
    )(pm, conv_w)


def _gdn_conv_bwd(pm, conv_w, dout, part, name):
    s = pm.shape[0]
    off = part * HEADS

    def body(x_ref, w_ref, d_ref, dx_ref, dw_ref):
        x, w = x_ref[...], w_ref[...]
        taps = _conv_taps(x)
        _, vjp = jax.vjp(lambda p: _conv_post(p, part < 2), _conv_pre(x, w, taps))
        dpre, = vjp(d_ref[...])
        dx = dpre * w[CONV_K - 1:CONV_K, :]
        for j in range(CONV_K - 1):
            dx = dx + _shift_up(dpre, CONV_K - 1 - j) * w[j:j + 1, :]
        dx_ref[...] = dx.astype(BF16)
        dw_ref[...] = jnp.concatenate([jnp.sum(dpre * tap, axis=0, keepdims=True) for tap in taps], axis=0)

    return pl.pallas_call(
        body, name=name, grid=(HEADS,),
        in_specs=[pl.BlockSpec((s, HEAD), lambda c: (0, c + off)), pl.BlockSpec((CONV_K, HEAD), lambda c: (0, c + off)),
                  pl.BlockSpec((s, HEAD), lambda c: (0, c))],
        out_specs=[pl.BlockSpec((s, HEAD), lambda c: (0, c)), pl.BlockSpec((CONV_K, HEAD), lambda c: (0, c))],
        out_shape=[jax.ShapeDtypeStruct((s, D), BF16), jax.ShapeDtypeStruct((CONV_K, D), F32)],
        compiler_params=_params(("parallel",)),
    )(pm, conv_w, dout)


def _dot3(a, b, dims=_NN):
    ah, bh = a.astype(BF16), b.astype(BF16)
    al, bl = (a - ah.astype(F32)).astype(BF16), (b - bh.astype(F32)).astype(BF16)
    d = lambda u, v: lax.dot_general(u, v, dims, preferred_element_type=F32)
    return d(ah, bh) + (d(ah, bl) + d(al, bh))


def _make_dot(hi):
    def raw(a, b, dims):
        if hi:
            return _dot3(a, b, dims)
        return lax.dot_general(a.astype(BF16), b.astype(BF16), dims, preferred_element_type=F32)

    @functools.partial(jax.custom_vjp, nondiff_argnums=(2,))
    def dot(a, b, form):
        return raw(a, b, _DIMS[form])

    def fwd(a, b, form):
        return raw(a, b, _DIMS[form]), (a, b)

    def bwd(form, res, ct):
        a, b = res
        if form == "nn":
            return raw(ct, b, _NT), raw(a, ct, _TN)
        if form == "nt":
            return raw(ct, b, _NN), raw(ct, a, _TN)
        return raw(b, ct, _NT), raw(a, ct, _NN)

    dot.defvjp(fwd, bwd)
    return dot


_dot = _make_dot(False)
_dot_hi = _make_dot(True)


def _tri_inv_raw(low):
    n = low.shape[0]
    i = lax.broadcasted_iota(jnp.int32, (n, n), 0)
    j = lax.broadcasted_iota(jnp.int32, (n, n), 1)
    eye = (i == j).astype(F32)
    hdot = _dot3
    same16 = (i // 16) == (j // 16)
    neg = jnp.where(same16, -low, 0.0)
    inv = eye + neg
    power = neg
    for _ in range(3):
        power = hdot(power, power)
        inv = hdot(inv, eye + power)
    for blk in (32, 64):
        off = jnp.where(((i // blk) == (j // blk)) & ((i // (blk // 2)) != (j // (blk // 2))), low, 0.0)
        inv = inv - hdot(inv, hdot(off, inv))
    return inv


@jax.custom_vjp
def _tri_inv(low):
    return _tri_inv_raw(low)


def _tri_inv_fwd(low):
    inv = _tri_inv_raw(low)
    return inv, inv


def _tri_inv_bwd(inv, ct):
    return (-_dot3(_dot3(inv, ct, _TN), inv, _NT),)


_tri_inv.defvjp(_tri_inv_fwd, _tri_inv_bwd)


@jax.custom_vjp
def _tri_inv_given(low, inv):
    return inv


_tri_inv_given.defvjp(lambda low, inv: (inv, inv),
                      lambda inv, ct: (_tri_inv_bwd(inv, ct)[0], jnp.zeros_like(inv)))

GROUP = 4
N_GROUPS = HEADS // GROUP
GROWS = GROUP * CHUNK


def _gdn_group(q, k, v, beta, gc, gr, states, inv=None):
    n = q.shape[0]
    i = lax.broadcasted_iota(jnp.int32, (n, n), 0)
    j = lax.broadcasted_iota(jnp.int32, (n, n), 1)
    same = (i // CHUNK) == (j // CHUNK)
    incl, strict = same & (i >= j), same & (i > j)
    qs = q * (HEAD ** -0.5)
    decay = jnp.where(incl, jnp.exp(jnp.where(incl, gc - gr, 0.0)), 0.0)
    kb = k * beta
    eg = jnp.exp(gc)
    prod = _dot(jnp.concatenate([kb, qs], axis=0), k, "nt")
    low = jnp.where(strict, prod[:n] * decay, 0.0)
    attn = jnp.where(incl, prod[n:] * decay, 0.0)
    inv = _tri_inv(low) if inv is None else _tri_inv_given(low, inv)
    sol = _dot_hi(inv, jnp.concatenate([v * beta, kb * eg], axis=1), "nn")
    u, w, qg = sol[:, :HEAD], sol[:, HEAD:], qs * eg
    last = lax.broadcasted_iota(jnp.int32, (CHUNK, 1), 0) == CHUNK - 1
    v_new, o_state, carry = [], [], []
    for h, state in enumerate(states):
        rows = slice(h * CHUNK, (h + 1) * CHUNK)
        ws = _dot(jnp.concatenate([w[rows], qg[rows]], axis=0), state, "nn")
        v_new.append(u[rows] - ws[:CHUNK])
        o_state.append(ws[CHUNK:])
        g_last = jnp.sum(jnp.where(last, gc[rows], 0.0), axis=0, keepdims=True)
        carry.append((g_last, k[rows] * jnp.exp(g_last - gc[rows])))
    o = jnp.concatenate(o_state, axis=0) + _dot(attn, jnp.concatenate(v_new, axis=0), "nn")
    new = tuple(state * jnp.exp(g_last) + _dot(k_dec, vn, "tn")
                for state, (g_last, k_dec), vn in zip(states, carry, v_new))
    return o, new, inv


def _gdn_specs(s, rev):
    nc = s // CHUNK
    at = (lambda n: nc - 1 - n) if rev else (lambda n: n)
    return nc, at, [
        pl.BlockSpec((CHUNK, D), lambda n: (at(n), 0)), pl.BlockSpec((CHUNK, D), lambda n: (at(n), 1)),
        pl.BlockSpec((CHUNK, D), lambda n: (at(n), 2)), pl.BlockSpec((CHUNK, HEAD), lambda n: (at(n), 0)),
        pl.BlockSpec((CHUNK, HEAD), lambda n: (at(n), 0)),
        pl.BlockSpec((None, N_GROUPS, 1, GROWS), lambda n: (at(n), 0, 0, 0))]


def _group_operands(grp, q_ref, k_ref, v_ref, b_blk, gc_blk, gr_blk):
    heads = range(grp * GROUP, (grp + 1) * GROUP)
    stack = lambda ref: jnp.concatenate([ref[:, h * HEAD:(h + 1) * HEAD] for h in heads], axis=0)
    col = lambda blk: jnp.concatenate([blk[:, h:h + 1] for h in heads], axis=0)
    return stack(q_ref), stack(k_ref), stack(v_ref), col(b_blk), col(gc_blk), gr_blk[grp]


def _gdn_scan_fwd(qkv, beta, gcum, grow, name):
    s = qkv.shape[0]
    nc, _, in_specs = _gdn_specs(s, rev=False)

    def body(q_ref, k_ref, v_ref, b_ref, gc_ref, gr_ref, o_ref, st_ref, inv_ref, state):
        @pl.when(pl.program_id(0) == 0)
        def _():
            state[...] = jnp.zeros_like(state)

        b_blk, gc_blk, gr_blk = b_ref[...], gc_ref[...], gr_ref[...]
        old = [state[h] for h in range(HEADS)]
        res = [_gdn_group(*_group_operands(grp, q_ref, k_ref, v_ref, b_blk, gc_blk, gr_blk),
                          old[grp * GROUP:(grp + 1) * GROUP]) for grp in range(N_GROUPS)]
        for grp, (o, new, inv) in enumerate(res):
            inv_ref[grp] = inv
            for hh in range(GROUP):
                h = grp * GROUP + hh
                st_ref[h] = old[h]
                o_ref[:, h * HEAD:(h + 1) * HEAD] = o[hh * CHUNK:(hh + 1) * CHUNK]
                state[h] = new[hh]

    return pl.pallas_call(
        body, name=name, grid=(nc,), in_specs=in_specs,
        out_specs=[pl.BlockSpec((CHUNK, D), lambda n: (n, 0)),
                   pl.BlockSpec((None, HEADS, HEAD, HEAD), lambda n: (n, 0, 0, 0)),
                   pl.BlockSpec((None, N_GROUPS, GROWS, GROWS), lambda n: (n, 0, 0, 0))],
        out_shape=[jax.ShapeDtypeStruct((s, D), F32), jax.ShapeDtypeStruct((nc, HEADS, HEAD, HEAD), F32),
                   jax.ShapeDtypeStruct((nc, N_GROUPS, GROWS, GROWS), F32)],
        scratch_shapes=[pltpu.VMEM((HEADS, HEAD, HEAD), F32)],
        compiler_params=_params(("arbitrary",)),
    )(qkv, qkv, qkv, beta, gcum, grow)


def _gdn_scan_bwd(qkv, beta, gcum, grow, states, invs, do, name):
    s = qkv.shape[0]
    nc, at, in_specs = _gdn_specs(s, rev=True)
    in_specs += [pl.BlockSpec((None, HEADS, HEAD, HEAD), lambda n: (at(n), 0, 0, 0)),
                 pl.BlockSpec((None, N_GROUPS, GROWS, GROWS), lambda n: (at(n), 0, 0, 0)),
                 pl.BlockSpec((CHUNK, D), lambda n: (at(n), 0))]

    def body(q_ref, k_ref, v_ref, b_ref, gc_ref, gr_ref, st_ref, inv_ref, do_ref, dq_ref, dk_ref, dv_ref, db_ref,
             dgc_ref, dgr_ref, dstate):
        @pl.when(pl.program_id(0) == 0)
        def _():
            dstate[...] = jnp.zeros_like(dstate)

        b_blk, gc_blk, gr_blk = b_ref[...], gc_ref[...], gr_ref[...]
        dold = [dstate[h] for h in range(HEADS)]
        res = []
        for grp in range(N_GROUPS):
            heads = range(grp * GROUP, (grp + 1) * GROUP)
            inv = inv_ref[grp]
            _, vjp = jax.vjp(lambda q, k, v, b, gc, gr, *st, inv=inv: _gdn_group(q, k, v, b, gc, gr, st, inv)[:2],
                             *_group_operands(grp, q_ref, k_ref, v_ref, b_blk, gc_blk, gr_blk),
                             *[st_ref[h] for h in heads])
            d_out = jnp.concatenate([do_ref[:, h * HEAD:(h + 1) * HEAD] for h in heads], axis=0)
            res.append(vjp((d_out, tuple(dold[h] for h in heads))))
        lane = lax.broadcasted_iota(jnp.int32, (CHUNK, HEAD), 1)
        db_all = jnp.zeros((CHUNK, HEAD), F32)
        dgc_all = jnp.zeros((CHUNK, HEAD), F32)
        for grp, (dq, dk, dv, db, dgc, dgr, *dst) in enumerate(res):
            dgr_ref[grp] = dgr
            for hh in range(GROUP):
                h = grp * GROUP + hh
                cs, rows = slice(h * HEAD, (h + 1) * HEAD), slice(hh * CHUNK, (hh + 1) * CHUNK)
                dq_ref[:, cs] = dq[rows]
                dk_ref[:, cs] = dk[rows]
                dv_ref[:, cs] = dv[rows]
                dstate[h] = dst[hh]
                db_all = jnp.where(lane == h, db[rows], db_all)
                dgc_all = jnp.where(lane == h, dgc[rows], dgc_all)
        db_ref[...] = db_all
        dgc_ref[...] = dgc_all

    blk = pl.BlockSpec((CHUNK, D), lambda n: (at(n), 0))
    gblk = pl.BlockSpec((CHUNK, HEAD), lambda n: (at(n), 0))
    return pl.pallas_call(
        body, name=name, grid=(nc,), in_specs=in_specs,
        out_specs=[blk, blk, blk, gblk, gblk, pl.BlockSpec((None, N_GROUPS, 1, GROWS), lambda n: (at(n), 0, 0, 0))],
        out_shape=[jax.ShapeDtypeStruct((s, D), F32)] * 3 + [jax.ShapeDtypeStruct((s, HEAD), F32)] * 2
        + [jax.ShapeDtypeStruct((nc, N_GROUPS, 1, GROWS), F32)],
        scratch_shapes=[pltpu.VMEM((HEADS, HEAD, HEAD), F32)],
        compiler_params=_params(("arbitrary",)),
    )(qkv, qkv, qkv, beta, gcum, grow, states, invs, do)


ATT_TILE = 512
ATT_SCALE = QK_HEAD ** -0.5


def _att_mask(t):
    qpos = lax.broadcasted_iota(jnp.int32, (t, t), 0)
    kpos = lax.broadcasted_iota(jnp.int32, (t, t), 1)
    return (kpos // CHUNK) <= (qpos // CHUNK)


ATT_STRIP = 32


def _att_strip_mask(r, t):
    kpos = lax.broadcasted_iota(jnp.int32, (ATT_STRIP, t), 1)
    return (kpos // CHUNK) <= (r * ATT_STRIP) // CHUNK


def _att_pairs(nb, by_query):
    if by_query:
        pairs = [(i, j) for i in range(nb) for j in range(i + 1)]
    else:
        pairs = [(j, i) for j in range(nb) for i in range(j, nb)]
    return jnp.array([a for a, _ in pairs], jnp.int32), jnp.array([b for _, b in pairs], jnp.int32)


def _attn_fwd(q, k, v, name):
    s = q.shape[0]
    t = min(ATT_TILE, s)
    nb = s // t
    ii, jj = _att_pairs(nb, by_query=True)

    def body(ii_ref, jj_ref, q_ref, k_ref, v_ref, o_ref, lse_ref, m_s, l_s, acc):
        step = pl.program_id(1)
        i, j = ii_ref[step], jj_ref[step]

        @pl.when(j == 0)
        def _():
            m_s[...] = jnp.full_like(m_s, -jnp.inf)
            l_s[...] = jnp.zeros_like(l_s)
            acc[...] = jnp.zeros_like(acc)

        sc = lax.dot_general(q_ref[...], k_ref[...], _NT, preferred_element_type=F32) * ATT_SCALE
        sc = lax.cond(i == j, lambda u: jnp.where(_att_mask(t), u, -jnp.inf), lambda u: u, sc)
        m_new = jnp.maximum(m_s[...], jnp.max(sc, axis=-1, keepdims=True))
        alpha = jnp.exp(m_s[...] - m_new)
        p = jnp.exp(sc - m_new)
        l_s[...] = alpha * l_s[...] + jnp.sum(p, axis=-1, keepdims=True)
        acc[...] = alpha * acc[...] + jnp.dot(p.astype(BF16), v_ref[...], preferred_element_type=F32)
        m_s[...] = m_new

        @pl.when(j == i)
        def _():
            o_ref[...] = acc[...] / l_s[...]
            lse_ref[...] = m_s[...] + jnp.log(l_s[...])

    grid_spec = pltpu.PrefetchScalarGridSpec(
        num_scalar_prefetch=2, grid=(HEADS, len(ii)),
        in_specs=[pl.BlockSpec((t, HEAD_PAD), lambda h, n, ir, jr: (ir[n], h)),
                  pl.BlockSpec((t, HEAD_PAD), lambda h, n, ir, jr: (jr[n], h)),
                  pl.BlockSpec((t, HEAD), lambda h, n, ir, jr: (jr[n], h))],
        out_specs=[pl.BlockSpec((t, HEAD), lambda h, n, ir, jr: (ir[n], h)),
                   pl.BlockSpec((None, t, 1), lambda h, n, ir, jr: (h, ir[n], 0))],
        scratch_shapes=[pltpu.VMEM((t, 1), F32), pltpu.VMEM((t, 1), F32), pltpu.VMEM((t, HEAD), F32)])
    return pl.pallas_call(
        body, name=name, grid_spec=grid_spec,
        out_shape=[jax.ShapeDtypeStruct((s, HEADS * HEAD), F32), jax.ShapeDtypeStruct((HEADS, s, 1), F32)],
        compiler_params=_params(("parallel", "arbitrary")),
    )(ii, jj, q, k, v)


def _attn_bwd(q, k, v, do, o, lse, name, dkv_sum=None):
    s = q.shape[0]
    t = min(ATT_TILE, s)
    nb = s // t
    jj, ii = _att_pairs(nb, by_query=False)
    nsum = 2 if dkv_sum else 0

    def body(jj_ref, ii_ref, q_ref, k_ref, v_ref, do_ref, o_ref, lse_ref, *rest):
        dq_ref, dk_ref, dv_ref, dk_acc, dv_acc, sc_s, dp_s, p_s, ds_s, dl_s = rest[nsum:]
        step = pl.program_id(1)
        i, j = ii_ref[step], jj_ref[step]

        @pl.when(step == 0)
        def _():
            dq_ref[...] = jnp.zeros_like(dq_ref)

        @pl.when(i == j)
        def _():
            dk_acc[...] = jnp.zeros_like(dk_acc)
            dv_acc[...] = jnp.zeros_like(dv_acc)

        do_f = do_ref[...]
        dob = do_f.astype(BF16)
        dl_s[...] = jnp.sum(do_f * o_ref[...], axis=-1, keepdims=True)
        sc_s[...] = lax.dot_general(q_ref[...], k_ref[...], _NT, preferred_element_type=F32)
        dp_s[...] = lax.dot_general(dob, v_ref[...], _NT, preferred_element_type=F32)

        def softmax_strips(diagonal):
            for r in range(t // ATT_STRIP):
                rows = slice(r * ATT_STRIP, (r + 1) * ATT_STRIP)
                p = jnp.exp(sc_s[rows, :] * ATT_SCALE - lse_ref[rows, :])
                if diagonal:
                    p = jnp.where(_att_strip_mask(r, t), p, 0.0)
                p_s[rows, :] = p.astype(BF16)
                ds_s[rows, :] = (p * (dp_s[rows, :] - dl_s[rows, :]) * ATT_SCALE).astype(BF16)

        pl.when(i == j)(functools.partial(softmax_strips, True))
        pl.when(i != j)(functools.partial(softmax_strips, False))
        ds = ds_s[...]
        dv_acc[...] += lax.dot_general(p_s[...], dob, _TN, preferred_element_type=F32)
        dk_acc[...] += lax.dot_general(ds, q_ref[...], _TN, preferred_element_type=F32)
        rows = pl.ds(pl.multiple_of(i * t, t), t)
        dq_ref[rows, :] += jnp.dot(ds, k_ref[...], preferred_element_type=F32)

        @pl.when(i == nb - 1)
        def _():
            dk_ref[...] = dk_acc[...] + rest[0][...] if nsum else dk_acc[...]
            dv_ref[...] = dv_acc[...] + rest[1][...] if nsum else dv_acc[...]

    dk_blk = pl.BlockSpec((t, HEAD_PAD), lambda h, n, jr, ir: (jr[n], h))
    dv_blk = pl.BlockSpec((t, HEAD), lambda h, n, jr, ir: (jr[n], h))
    grid_spec = pltpu.PrefetchScalarGridSpec(
        num_scalar_prefetch=2, grid=(HEADS, len(jj)),
        in_specs=[pl.BlockSpec((t, HEAD_PAD), lambda h, n, jr, ir: (ir[n], h)),
                  pl.BlockSpec((t, HEAD_PAD), lambda h, n, jr, ir: (jr[n], h)),
                  pl.BlockSpec((t, HEAD), lambda h, n, jr, ir: (jr[n], h)),
                  pl.BlockSpec((t, HEAD), lambda h, n, jr, ir: (ir[n], h)),
                  pl.BlockSpec((t, HEAD), lambda h, n, jr, ir: (ir[n], h)),
                  pl.BlockSpec((None, t, 1), lambda h, n, jr, ir: (h, ir[n], 0))] + [dk_blk, dv_blk][:nsum],
        out_specs=[pl.BlockSpec((s, HEAD_PAD), lambda h, n, jr, ir: (0, h)), dk_blk, dv_blk],
        scratch_shapes=[pltpu.VMEM((t, HEAD_PAD), F32), pltpu.VMEM((t, HEAD), F32), pltpu.VMEM((t, t), F32),
                        pltpu.VMEM((t, t), F32), pltpu.VMEM((t, t), BF16), pltpu.VMEM((t, t), BF16),
                        pltpu.VMEM((t, 1), F32)])
    return pl.pallas_call(
        body, name=name, grid_spec=grid_spec,
        out_shape=[jax.ShapeDtypeStruct((s, HEADS * HEAD_PAD), F32)] * 2 + [jax.ShapeDtypeStruct((s, HEADS * HEAD), F32)],
        compiler_params=_params(("parallel", "arbitrary")),
    )(jj, ii, q, k, v, do, o, lse, *(dkv_sum or ()))


def _rope_tables(positions):
    half = ROPE // 2
    inv_freq = ROPE_BASE ** (-jnp.arange(half, dtype=F32) / half)
    ang = positions.astype(F32)[:, None] * inv_freq
    cos, sin = jnp.cos(ang), jnp.sin(ang)
    return jnp.concatenate([cos] * 4, axis=1), jnp.concatenate([-sin, sin] * 2, axis=1)


def _loss_and_grad(y, target, name):
    s = y.shape[0]
    ts = min(ROW_TILE, s)

    def body(y_ref, t_ref, dy_ref, l_ref):
        e = y_ref[...] - t_ref[...]
        dy_ref[...] = e * (1.0 / D)
        part = jnp.sum(jnp.sum(e * e, axis=-1, keepdims=True) * (0.5 / D), axis=0, keepdims=True)
        part = part * jnp.ones((1, 128), F32)

        @pl.when(pl.program_id(0) == 0)
        def _():
            l_ref[...] = part

        @pl.when(pl.program_id(0) > 0)
        def _():
            l_ref[...] += part

    return pl.pallas_call(
        body, name=name, grid=(s // ts,),
        in_specs=[pl.BlockSpec((ts, D), lambda i: (i, 0))] * 2,
        out_specs=[pl.BlockSpec((ts, D), lambda i: (i, 0)), pl.BlockSpec((1, 128), lambda i: (0, 0))],
        out_shape=[jax.ShapeDtypeStruct((s, D), F32), jax.ShapeDtypeStruct((1, 128), F32)],
        compiler_params=_params(("arbitrary",)),
    )(y, target)


ANY = pl.BlockSpec(memory_space=pl.ANY)


def _all_gather(shard, name):
    def body(x_ref, out_ref, send_sems, recv_sems, local_sem):
        x, y, c = lax.axis_index("x"), lax.axis_index("y"), lax.axis_index("c")
        me, sibling = (x, y, c), (x, y, 1 - c)
        chips = [(1 - x, y), (x, 1 - y), (1 - x, 1 - y)]

        def rows(px, py, pc):
            return out_ref.at[4 * px + 2 * py + pc]

        def copy(k, block, to, src=None):
            return pltpu.make_async_remote_copy(
                src_ref=rows(*block) if src is None else src, dst_ref=rows(*block),
                send_sem=send_sems.at[k], recv_sem=recv_sems.at[k], device_id=to, device_id_type=MESH)

        mine = pltpu.make_async_copy(x_ref, rows(*me), local_sem)
        mine.start()
        first = [copy(0, me, sibling, src=x_ref)]
        first += [copy(1 + j, me, (*chip, c), src=x_ref) for j, chip in enumerate(chips)]
        for cp in first:
            cp.start()
        passed = [copy(4 + j, (*chip, c), sibling) for j, chip in enumerate(chips)]
        for j, chip in enumerate(chips):
            copy(1 + j, (*chip, c), me).wait_recv()
            passed[j].start()
        copy(0, sibling, me).wait_recv()
        for j, chip in enumerate(chips):
            copy(4 + j, (*chip, 1 - c), me).wait_recv()
        for cp in first + passed:
            cp.wait_send()
        mine.wait()

    return pl.pallas_call(
        body, name=name, out_shape=jax.ShapeDtypeStruct((N_DEV,) + shard.shape, shard.dtype),
        in_specs=[ANY], out_specs=ANY,
        scratch_shapes=[pltpu.SemaphoreType.DMA((7,)), pltpu.SemaphoreType.DMA((7,)), pltpu.SemaphoreType.DMA],
    )(shard)


def _exchange(blocks, name):
    def body(x_ref, out_ref, send_sems, recv_sems, local_sem):
        x, y, c = lax.axis_index("x"), lax.axis_index("y"), lax.axis_index("c")
        me = 4 * x + 2 * y + c
        mine = pltpu.make_async_copy(x_ref.at[me], out_ref.at[me], local_sem)
        mine.start()
        copies = []
        for k in range(1, N_DEV):
            px = 1 - x if k & 4 else x
            py = 1 - y if k & 2 else y
            pc = 1 - c if k & 1 else c
            peer = 4 * px + 2 * py + pc
            cp = pltpu.make_async_remote_copy(
                src_ref=x_ref.at[peer], dst_ref=out_ref.at[me], send_sem=send_sems.at[k - 1],
                recv_sem=recv_sems.at[k - 1], device_id=(px, py, pc), device_id_type=MESH)
            cp.start()
            copies.append((cp, pltpu.make_async_remote_copy(
                src_ref=x_ref.at[peer], dst_ref=out_ref.at[peer], send_sem=send_sems.at[k - 1],
                recv_sem=recv_sems.at[k - 1], device_id=(px, py, pc), device_id_type=MESH)))
        for cp, landing in copies:
            landing.wait_recv()
        for cp, landing in copies:
            cp.wait_send()
        mine.wait()

    return pl.pallas_call(
        body, name=name, out_shape=jax.ShapeDtypeStruct(blocks.shape, blocks.dtype),
        in_specs=[ANY], out_specs=ANY,
        scratch_shapes=[pltpu.SemaphoreType.DMA((7,)), pltpu.SemaphoreType.DMA((7,)), pltpu.SemaphoreType.DMA],
    )(blocks)


HBM = pl.BlockSpec(memory_space=pltpu.HBM)
SEM = pl.BlockSpec(memory_space=pltpu.SEMAPHORE)
EFFECT = pltpu.SideEffectType.DATAFLOW_SIDE_EFFECTING


def _peers():
    x, y, c = lax.axis_index("x"), lax.axis_index("y"), lax.axis_index("c")
    peers = []
    for k in range(1, N_DEV):
        px = 1 - x if k & 4 else x
        py = 1 - y if k & 2 else y
        pc = 1 - c if k & 1 else c
        peers.append(((px, py, pc), 4 * px + 2 * py + pc))
    return 4 * x + 2 * y + c, peers


def _send_start(srcs, name, gather):
    n = len(srcs)
    lands = [((N_DEV,) + s.shape) if gather else s.shape for s in srcs]

    def body(*refs):
        src_refs, land_refs = refs[:n], refs[n:2 * n]
        send_sems, recv_sems, token = refs[2 * n], refs[2 * n + 1], refs[-1]
        me, peers = _peers()
        for i in range(n):
            for k, (dev, idx) in enumerate(peers):
                pltpu.make_async_remote_copy(
                    src_ref=src_refs[i] if gather else src_refs[i].at[idx], dst_ref=land_refs[i].at[me],
                    send_sem=send_sems.at[7 * i + k], recv_sem=recv_sems.at[7 * i + k], device_id=dev,
                    device_id_type=MESH).start()
        token[...] = jnp.zeros_like(token)

    res = pl.pallas_call(
        body, name=name,
        out_shape=(pltpu.SemaphoreType.DMA((7 * n,)), pltpu.SemaphoreType.DMA((7 * n,)),
                   *[pltpu.HBM(s.shape, s.dtype) for s in srcs],
                   *[pltpu.HBM(shape, s.dtype) for shape, s in zip(lands, srcs)],
                   jax.ShapeDtypeStruct((8, 128), F32)),
        in_specs=(HBM,) * (2 * n), out_specs=(SEM, SEM) + (HBM,) * (2 * n) + (pl.BlockSpec(memory_space=pltpu.VMEM),),
        input_output_aliases={i: 2 + i for i in range(2 * n)},
        compiler_params=pltpu.CompilerParams(has_side_effects=EFFECT),
    )(*[pltpu.with_memory_space_constraint(s, pltpu.HBM) for s in srcs],
      *[pltpu.with_memory_space_constraint(lax.empty(shape, s.dtype), pltpu.HBM) for shape, s in zip(lands, srcs)])
    return dict(sems=res[:2], srcs=res[2:2 + n], lands=res[2 + n:2 + 2 * n], token=res[-1])


def _send_wait(handle, after, name, gather):
    n = len(handle["srcs"])

    def body(*refs):
        src_refs, land_refs = refs[:n], refs[n:2 * n]
        send_sems, recv_sems = refs[2 * n], refs[2 * n + 1]
        me, peers = _peers()
        for i in range(n):
            for k, (dev, idx) in enumerate(peers):
                cp = pltpu.make_async_remote_copy(
                    src_ref=src_refs[i] if gather else src_refs[i].at[idx], dst_ref=land_refs[i].at[idx],
                    send_sem=send_sems.at[7 * i + k], recv_sem=recv_sems.at[7 * i + k], device_id=dev,
                    device_id_type=MESH)
                cp.wait_send()
                cp.wait_recv()

    both = list(handle["srcs"]) + list(handle["lands"])
    res = pl.pallas_call(
        body, name=name, out_shape=tuple(pltpu.HBM(t.shape, t.dtype) for t in both),
        in_specs=(HBM,) * (2 * n) + (SEM, SEM, pl.BlockSpec(memory_space=pl.ANY)), out_specs=(HBM,) * (2 * n),
        input_output_aliases={i: i for i in range(2 * n)},
        compiler_params=pltpu.CompilerParams(has_side_effects=EFFECT),
    )(*both, *handle["sems"], after)
    return res[:n], res[n:]


def _adamw(parts, w, m, v, name, tr=128):
    pieces = len(parts)
    n, r, wd = parts[0].shape
    tr = next((t for t in (tr, 64, 32, 16) if r % t == 0), r)
    nrt = r // tr

    def body(*refs):
        w_ref, m_ref, v_ref, g_ref, d_ref, nm_ref, nv_ref = refs[pieces:]

        def update(p_ref):
            g = p_ref[0].astype(F32)
            for k in range(1, n):
                g = g + p_ref[k].astype(F32)
            m_new = B1 * m_ref[...] + (1.0 - B1) * g
            v_new = B2 * v_ref[...] + (1.0 - B2) * (g * g)
            m_hat = m_new / (1.0 - B1 ** STEP)
            v_hat = v_new / (1.0 - B2 ** STEP)
            g_ref[...] = g
            d_ref[...] = -LR * (m_hat / (jnp.sqrt(v_hat) + ADAM_EPS) + WD * w_ref[...])
            nm_ref[...] = m_new
            nv_ref[...] = v_new

        for p in range(pieces):
            pl.when(pl.program_id(0) == p)(functools.partial(update, refs[p]))

    part_spec = lambda p: pl.BlockSpec((n, tr, wd), lambda l, i: (0, jnp.clip(i + (l - p) * nrt, 0, nrt - 1), 0))
    blk = pl.BlockSpec((tr, wd), lambda l, i: (l * nrt + i, 0))
    return pl.pallas_call(
        body, name=name, grid=(pieces, nrt),
        in_specs=[part_spec(p) for p in range(pieces)] + [blk, blk, blk],
        out_specs=[blk] * 4, out_shape=[jax.ShapeDtypeStruct((pieces * r, wd), F32)] * 4,
        compiler_params=_params(("arbitrary", "arbitrary")),
    )(*parts, w, m, v)


def _outer8(ct, dm, name):
    k, n = ct.shape[0], dm.shape[1]

    def body(c_ref, d_ref, o_ref):
        cv, dv = c_ref[...], d_ref[...]
        acc = cv[:, 0:1] * dv[0:1, :]
        for s in range(1, N_DEV):
            acc = acc + cv[:, s:s + 1] * dv[s:s + 1, :]
        o_ref[...] = acc

    tk = 256
    return pl.pallas_call(
        body, name=name, grid=(k // tk,),
        in_specs=[pl.BlockSpec((tk, N_DEV), lambda i: (i, 0)), pl.BlockSpec((N_DEV, n), lambda i: (0, 0))],
        out_specs=pl.BlockSpec((tk, n), lambda i: (i, 0)), out_shape=jax.ShapeDtypeStruct((k, n), F32),
        compiler_params=_params(("parallel",)),
    )(ct, dm)


FULL = (0, D)
C128 = (0, 128)
HEAD_NOPE = [(h * HEAD_PAD, NOPE) for h in range(HEADS)]
HEAD_ROPE = [(h * HEAD_PAD + NOPE, 128) for h in range(HEADS)]
HEAD_ALL = [(h * HEAD_PAD, HEAD_PAD) for h in range(HEADS)]
HEAD_V = [(h * HEAD, HEAD) for h in range(HEADS)]


def _modulate(x, p, ties=()):
    return _rowwise_fwd(_modulate_fn, [(x, FULL)], [p["gain"], p["scale"], p["shift"]], [(D, BF16, FULL)], "modulate",
                        ties=ties)[0]


def _residual_bwd(y, gm, dxn):
    return _rowwise_bwd(_gate_only_fn, [(y, FULL)], [gm], [(D, F32, FULL)], [dxn], [(0, FULL)], [(D, BF16)],
                        "residual_bwd")


def _modulate_bwd(x, p, dh, dx_in, prev=None):
    pars = [p["gain"], p["scale"], p["shift"]]
    if prev is None:
        return list(_rowwise_bwd(_modulate_fn, [(x, FULL)], pars, [(D, BF16, FULL)], [dh], [(0, FULL)], [(D, F32)],
                                 "modulate_bwd", add={0: dx_in})) + [None]
    s = x.shape[0]
    ts = min(2 * ROW_TILE, s)

    def body(x_ref, g_ref, sc_ref, sh_ref, dh_ref, din_ref, y_ref, gm_ref, dx_ref, dy_ref, dg_ref, dsc_ref, dsh_ref,
             dgm_ref):
        _, vjp = jax.vjp(lambda *t: _modulate_fn(0, *t)[0], x_ref[...], g_ref[...], sc_ref[...], sh_ref[...])
        dxm, dg, dsc, dsh = vjp(dh_ref[...])
        dx = dxm + din_ref[...]
        dx_ref[...] = dx
        dy_ref[...] = (gm_ref[...] * dx).astype(BF16)
        sums = (dg, dsc, dsh, jnp.sum(dx * y_ref[...], axis=0, keepdims=True))
        first = pl.program_id(0) == 0
        for ref, val in zip((dg_ref, dsc_ref, dsh_ref, dgm_ref), sums):
            @pl.when(first)
            def _(ref=ref, val=val):
                ref[...] = val

            @pl.when(jnp.logical_not(first))
            def _(ref=ref, val=val):
                ref[...] += val

    blk = pl.BlockSpec((ts, D), lambda i: (i, 0))
    vec = pl.BlockSpec((1, D), lambda i: (0, 0))
    dx, dy, dg, dsc, dsh, dgm = pl.pallas_call(
        body, name="modulate_bwd_chain", grid=(s // ts,),
        in_specs=[blk, vec, vec, vec, blk, blk, blk, vec], out_specs=[blk, blk, vec, vec, vec, vec],
        out_shape=[jax.ShapeDtypeStruct((s, D), F32), jax.ShapeDtypeStruct((s, D), BF16)]
        + [jax.ShapeDtypeStruct((1, D), F32)] * 4,
        compiler_params=_params(("arbitrary",)),
    )(x, *pars, dh, dx_in, prev[0], prev[1])
    return [dx, dg, dsc, dsh, (dy, dgm)]


def _out_proj(a, w, x, p, nxt, name, **kw):
    res = _matmul([(a, w)], "nn", name, out_dtype=BF16, resid=(x, p["gm"]) + tuple(nxt or ()), **kw)
    return res[1], res[0], (res[2] if nxt else None)


def _ffn_fwd(x, p, ties=(), h=None, nxt=None):
    if h is None:
        h, ties = _modulate(x, p, ties), ()
    act_dgate, act_dup, act = _ffn_in(h, p["w_in"], "ffn_in", ties=ties)
    if callable(p["wo"]):
        p["wo"] = p["wo"](act)
    res = _ffn_out(act, p["wo"], (x, p["gm"]) + tuple(nxt or ()), "ffn_out")
    return res[1], dict(x=x, h=h, act_dgate=act_dgate, act_dup=act_dup, act=act, y=res[0]), (res[2] if nxt else None)


def _ffn_bwd(t, p, dxn, res=None, prev=None, ties=(), early=None):
    dy, dgm = res or _residual_bwd(t["y"], p["gm"], dxn)
    dgate, dup = _ffn_bwd_act(dy, p["wo"], t["act_dgate"], t["act_dup"], "ffn_bwd_act", ties=ties)
    dwo = _ffn_dwo(t["act"], dy, "ffn_dwo", ties=ties)
    dh = _ffn_dh(dgate, dup, p["w_in"], "ffn_dh", ties=early(dwo) if early else ())
    dwi = _ffn_dwi(t["h"], dgate, dup, "ffn_dwi")
    dx, dgain, dscale, dshift, res_prev = _modulate_bwd(t["x"], p, dh, dxn, prev)
    return dx, dict(gain=dgain, scale=dscale, shift=dshift, gm=dgm, w_in=dwi, wo=dwo), res_prev


def _pad128(t):
    return jnp.pad(t, ((0, 0), (0, 128 - t.shape[1])))


def _gdn_fwd(x, p, ties=(), h=None, nxt=None):
    s = x.shape[0]
    if h is None:
        h, ties = _modulate(x, p, ties), ()
    pm = _mm(h, p["w_main"], "nn", "gdn_proj", ties=ties)
    tail = _mm(h, p["w_tail"], "nn", "gdn_proj_tail", ties=ties)
    qkv = _gdn_conv_fwd(pm, p["conv_w"], "gdn_conv")
    beta, gcum = _rowwise_fwd(_gdn_gates_fn, [(tail, C128), (tail, (128, 128))], [p["a_log"], p["dt_bias"]],
                              [(128, F32, C128)] * 2, "gdn_gates")
    grow = gcum[:, :HEADS].reshape(s // CHUNK, CHUNK, N_GROUPS, GROUP).transpose(0, 2, 3, 1)
    grow = grow.reshape(s // CHUNK, N_GROUPS, 1, GROWS)
    o, states, invs = _gdn_scan_fwd(qkv, beta, gcum, grow, "gdn_scan")
    on, = _rowwise_fwd(_gdn_outnorm_fn, [(o, HEAD_V), (pm, [(3 * D + h_ * HEAD, HEAD) for h_ in range(HEADS)])],
                       [p["norm_g"]], [(D, BF16, HEAD_V)], "gdn_outnorm", groups=HEADS)
    xn, y, hn = _out_proj(on, p["w_out"], x, p, nxt, "mix_out", tm=512)
    t = dict(x=x, h=h, pm=pm, tail=tail, qkv=qkv, beta=beta, gcum=gcum, grow=grow, o=o, states=states, invs=invs,
             on=on, y=y)
    return xn, t, hn


def _gdn_bwd(t, p, dxn, res=None, prev=None, ties=()):
    s = dxn.shape[0]
    zc = [(3 * D + h_ * HEAD, HEAD) for h_ in range(HEADS)]
    dy, dgm = res or _residual_bwd(t["y"], p["gm"], dxn)
    dw_out = _mm(t["on"], dy, "tn", "mix_dwo", ties=ties)
    don = _mm(dy, p["w_out"], "nt", "mix_dout", ties=ties)
    do, dz, dnorm_g = _rowwise_bwd(_gdn_outnorm_fn, [(t["o"], HEAD_V), (t["pm"], zc)], [p["norm_g"]],
                                   [(D, BF16, HEAD_V)], [don], [(0, HEAD_V), (1, HEAD_V)], [(D, F32), (D, BF16)],
                                   "gdn_outnorm_bwd", groups=HEADS)
    dq, dk, dv, dbeta, dg, dgr = _gdn_scan_bwd(t["qkv"], t["beta"], t["gcum"], t["grow"], t["states"], t["invs"], do,
                                               "gdn_scan_bwd")
    dg = dg + _pad128(dgr.reshape(s // CHUNK, N_GROUPS, GROUP, CHUNK).transpose(0, 3, 1, 2).reshape(s, HEADS))
    dtail, da_log, ddt = _rowwise_bwd(_gdn_gates_fn, [(t["tail"], C128), (t["tail"], (128, 128))],
                                      [p["a_log"], p["dt_bias"]], [(128, F32, C128)] * 2, [dbeta, dg],
                                      [(0, C128), (0, (128, 128))], [(256, F32)], "gdn_gates_bwd")
    dxs, dcw = [], []
    for part, d in enumerate((dq, dk, dv)):
        dx_, dw_ = _gdn_conv_bwd(t["pm"], p["conv_w"], d, part, "gdn_conv_bwd")
        dxs.append(dx_)
        dcw.append(dw_)
    pieces = dxs + [dz]
    dh = _matmul([(d, p["w_main"]) for d in pieces] + [(dtail, p["w_tail"])], "nt", "gdn_dh",
                 boffs=[0, D, 2 * D, 3 * D, 0], tk=512)
    dw_main = [_mm(t["h"], d, "tn", "gdn_dwi") for d in pieces]
    dw_tail = _mm(t["h"], dtail, "tn", "gdn_dwi_tail")
    dx, dgain, dscale, dshift, res_prev = _modulate_bwd(t["x"], p, dh, dxn, prev)
    return dx, dict(gain=dgain, scale=dscale, shift=dshift, gm=dgm, w_main=jnp.concatenate(dw_main, axis=1),
                    w_tail=dw_tail, conv_w=jnp.concatenate(dcw, axis=1), a_log=da_log, dt_bias=ddt,
                    norm_g=dnorm_g, w_out=dw_out), res_prev


def _q_rows(q2, cosf, sins):
    return [(q2, HEAD_NOPE), (q2, HEAD_ROPE), (cosf, C128), (sins, C128)]


def _mla_fwd(x, p, kv, ties=(), h=None, nxt=None):
    if h is None:
        h, ties = _modulate(x, p, ties), ()
    cq = _mm(h, p["w_dq"], "nn", "mla_dq", ties=ties)
    cqn, = _rowwise_fwd(_rms_fn, [(cq, (0, Q_LORA))], [p["q_lora_g"]], [(Q_LORA, BF16, (0, Q_LORA))], "mla_qlora_norm")
    q2 = _mm(cqn, p["w_uq"], "nn", "mla_uq")
    qn, = _rowwise_fwd(_q_norm_rope_fn, _q_rows(q2, kv["cosf"], kv["sins"]), [p["q_gn"], p["q_gr"]],
                       [(HEADS * HEAD_PAD, BF16, HEAD_ALL)], "mla_q_norm", groups=HEADS)
    o, lse = _attn_fwd(qn, kv["kn"], kv["vb"], "mla_attn")
    xn, y, hn = _out_proj(o, p["w_out"], x, p, nxt, "mix_out", tm=512)
    return xn, dict(x=x, h=h, cq=cq, cqn=cqn, q2=q2, qn=qn, o=o, lse=lse, y=y), hn


def _mla_bwd(t, p, kv, dxn, res=None, prev=None, ties=(), dkv_sum=None):
    dy, dgm = res or _residual_bwd(t["y"], p["gm"], dxn)
    dw_out = _mm(t["o"], dy, "tn", "mix_dwo", ties=ties)
    do = _mm(dy, p["w_out"], "nt", "mix_dout", ties=ties)
    dq, dk, dv = _attn_bwd(t["qn"], kv["kn"], kv["vb"], do, t["o"], t["lse"], "mla_attn_bwd", dkv_sum)
    dq2, dq_gn, dq_gr = _rowwise_bwd(_q_norm_rope_fn, _q_rows(t["q2"], kv["cosf"], kv["sins"]), [p["q_gn"], p["q_gr"]],
                                     [(HEADS * HEAD_PAD, BF16, HEAD_ALL)], [dq],
                                     [(0, HEAD_NOPE), (0, HEAD_ROPE), None, None], [(HEADS * HEAD_PAD, BF16)],
                                     "mla_q_norm_bwd", groups=HEADS)
    dw_uq = _mm(t["cqn"], dq2, "tn", "mla_dwuq")
    dcqn = _mm(dq2, p["w_uq"], "nt", "mla_dcq")
    dcq, dq_lora_g = _rowwise_bwd(_rms_fn, [(t["cq"], (0, Q_LORA))], [p["q_lora_g"]], [(Q_LORA, BF16, (0, Q_LORA))],
                                  [dcqn], [(0, (0, Q_LORA))], [(Q_LORA, BF16)], "mla_qlora_norm_bwd")
    dw_dq = _mm(t["h"], dcq, "tn", "mla_dwdq")
    dh = _mm(dcq, p["w_dq"], "nt", "mla_dh")
    dx, dgain, dscale, dshift, res_prev = _modulate_bwd(t["x"], p, dh, dxn, prev)
    grads = dict(gain=dgain, scale=dscale, shift=dshift, gm=dgm, w_dq=dw_dq, q_lora_g=dq_lora_g, w_uq=dw_uq,
                 q_gn=dq_gn, q_gr=dq_gr, w_out=dw_out)
    return dx, grads, res_prev, dk, dv


def _k_rows(kvp, ckv, cosf, sins):
    return [(kvp, HEAD_NOPE), (kvp, HEAD_ROPE), (ckv, (KV_LORA, 128)), (cosf, C128), (sins, C128)]


def _kv_fwd(x, p, cosf, sins):
    h = _modulate(x, p)
    ckv = _mm(h, p["w_dkv"], "nn", "kv_down")
    lat, = _rowwise_fwd(_rms_fn, [(ckv, (0, KV_LORA))], [p["kv_g"]], [(KV_LORA, BF16, (0, KV_LORA))], "kv_norm")
    kvp = _mm(lat, p["w_ukv"], "nn", "kv_up")
    kn, vb = _rowwise_fwd(_k_norm_rope_fn, _k_rows(kvp, ckv, cosf, sins), [p["k_gn"], p["k_gr"]],
                          [(HEADS * HEAD_PAD, BF16, HEAD_ALL), (HEADS * HEAD, BF16, HEAD_V)], "kv_k_norm",
                          groups=HEADS)
    return dict(x=x, h=h, ckv=ckv, lat=lat, kvp=kvp, kn=kn, vb=vb, cosf=cosf, sins=sins)


def _kv_bwd(t, p, dk, dv, dx_in, prev):
    dkvp, drope, dk_gn, dk_gr = _rowwise_bwd(
        _k_norm_rope_fn, _k_rows(t["kvp"], t["ckv"], t["cosf"], t["sins"]), [p["k_gn"], p["k_gr"]],
        [(HEADS * HEAD_PAD, BF16, HEAD_ALL), (HEADS * HEAD, BF16, HEAD_V)], [dk, dv],
        [(0, HEAD_NOPE), (0, HEAD_ROPE), (1, C128), None, None], [(HEADS * HEAD_PAD, BF16), (128, F32)],
        "kv_k_norm_bwd", groups=HEADS)
    dw_ukv = _mm(t["lat"], dkvp, "tn", "kv_dwukv")
    dlat = _mm(dkvp, p["w_ukv"], "nt", "kv_dlat")
    dckv, dkv_g = _rowwise_bwd(_rms_fn, [(t["ckv"], (0, KV_LORA))], [p["kv_g"]], [(KV_LORA, BF16, (0, KV_LORA))],
                               [dlat], [(0, (0, KV_LORA))], [(KV_LORA, F32)], "kv_norm_bwd")
    dw_dkv = jnp.concatenate([_mm(t["h"], dckv, "tn", "kv_dwdkv"), _mm(t["h"], drope, "tn", "kv_dwdkv_rope")], axis=1)
    dh = _matmul([(dckv, p["w_dkv"]), (drope, p["w_dkv"])], "nt", "kv_dh", boffs=[0, KV_LORA])
    dx, dgain, dscale, dshift, res_prev = _modulate_bwd(t["x"], p, dh, dx_in, prev)
    return dx, dict(gain=dgain, scale=dscale, shift=dshift, w_dkv=dw_dkv, kv_g=dkv_g, w_ukv=dw_ukv, k_gn=dk_gn,
                    k_gr=dk_gr), res_prev


WEIGHTS = ["ada_w", "ada_b", "norm_g", "ffn_w_in", "ffn_w_out", "gdn_w_in", "gdn_conv_w", "gdn_a_log", "gdn_dt_bias",
           "gdn_norm_g", "gdn_w_out", "kv_ada_w", "kv_ada_b", "kv_norm_g", "mla_w_dkv", "mla_kv_norm_g", "mla_w_ukv",
           "mla_k_norm_g", "mla_w_dq", "mla_q_lora_norm_g", "mla_w_uq", "mla_q_norm_g", "mla_w_out"]
SMALL = [("ada_b", 4 * N_MOD * D), ("kv_ada_b", 2 * D), ("norm_g", DEPTH * 3 * D), ("gdn_conv_w", N_A * CONV_K * 3 * D),
         ("gdn_a_log", N_A * HEADS), ("gdn_dt_bias", N_A * HEADS), ("gdn_norm_g", N_A * HEAD), ("kv_norm_g", D),
         ("mla_kv_norm_g", KV_LORA), ("mla_k_norm_g", QK_HEAD), ("mla_q_lora_norm_g", 2 * Q_LORA),
         ("mla_q_norm_g", 2 * QK_HEAD)]
SMALL_REPLICATED = [n for n, _ in SMALL if n not in ("norm_g", "gdn_conv_w")]


def _silu_fn(g, t):
    return (_silu(t),)


def _dup_rope(t):
    return jnp.concatenate([t[..., :NOPE], t[..., NOPE:], t[..., NOPE:]], axis=-1)


def _fold_rope(t):
    return jnp.concatenate([t[..., :NOPE], t[..., NOPE:QK_HEAD] + t[..., QK_HEAD:]], axis=-1)


def _pack(pieces, rows):
    flat = jnp.concatenate([p.reshape(-1).astype(F32) for p in pieces])
    return jnp.pad(flat, (0, rows * 128 - flat.shape[0])).reshape(rows, 128)


def _step(a):
    me = 4 * lax.axis_index("x") + 2 * lax.axis_index("y") + lax.axis_index("c")
    x = a["x"][0]
    cosf, sins = _rope_tables(a["positions"][0])

    n_gdn = (4 * D + 2 * HEADS) // N_DEV
    AHEAD = 2

    stages = [(l, part) for l in range(DEPTH) for part in range(3)]

    def stage_shards(l, part):
        if part != 1:
            sh = {"ffn_w_in": a["ffn_w_in"][l, part // 2], "ffn_w_out": a["ffn_w_out"][l, part // 2]}
            if part == 2 and l == N_A - 1:
                sh.update(mla_w_dkv=a["mla_w_dkv"], mla_w_ukv=a["mla_w_ukv"])
            return sh
        if l < N_A:
            return {"gdn_w_in": a["gdn_w_in"][l], "gdn_w_out": a["gdn_w_out"][l]}
        j = l - N_A
        return {"mla_w_dq": a["mla_w_dq"][j], "mla_w_uq": a["mla_w_uq"][j], "mla_w_out": a["mla_w_out"][j]}

    def zero_of(t):
        return jnp.minimum(jnp.abs(t[(0,) * t.ndim].astype(F32)), 0.0)

    def start_stage(l, part, tie):
        sh = stage_shards(l, part)
        return list(sh), _send_start([(w + tie).astype(BF16) for w in sh.values()], f"fetch_start_{l}_{part}", gather=True)

    def finish_stage(l, part, names, handle, after):
        srcs, lands = _send_wait(handle, after, f"fetch_wait_{l}_{part}", gather=True)
        return {n: lax.dynamic_update_slice(land, src[None], (me, 0, 0)) for n, src, land in zip(names, srcs, lands)}

    n_cw, n_ng = N_A * CONV_K * 3 * HEAD, DEPTH * 3 * HEAD
    small_all = _all_gather(_pack([a["gdn_conv_w"], a["norm_g"], a["c"]], 44), "gather_small").reshape(N_DEV, -1)
    conv_w = small_all[:, :n_cw].reshape(N_DEV, N_A, CONV_K, 3 * HEAD).transpose(1, 2, 0, 3).reshape(N_A, CONV_K, 3 * D)
    norm_g = small_all[:, n_cw:n_cw + n_ng].reshape(N_DEV, DEPTH, 3, HEAD).transpose(1, 2, 0, 3).reshape(DEPTH, 3, D)
    c_all = small_all[:, n_cw + n_ng:n_cw + n_ng + D]

    c_act, = _rowwise_fwd(_silu_fn, [(c_all, FULL)], [], [(D, F32, FULL)], "c_act")
    n_ada = N_MOD * D // N_DEV
    parts = [_mm(c_act, a["ada_w"][l], "nn", "mod_proj") for l in range(DEPTH)]
    parts.append(_mm(c_act, a["kv_ada_w"], "nn", "mod_proj_kv"))
    mod_recv = _exchange(jnp.concatenate(parts, axis=1)[:, None, :], "exchange_mod")[:, 0]
    mod = mod_recv[:, :DEPTH * n_ada].reshape(N_DEV, DEPTH, n_ada).transpose(1, 0, 2).reshape(DEPTH, N_MOD * D)
    mod = (mod + a["ada_b"]).reshape(DEPTH, N_MOD, D)
    kvmod = mod_recv[:, DEPTH * n_ada:].reshape(2 * D) + a["kv_ada_b"]

    def row(v):
        return v[None]

    def ffn_params(l, i, w):
        k = 0 if i == 0 else 6
        return dict(gain=row(norm_g[l, 0 if i == 0 else 2]), shift=row(mod[l, k]), scale=row(mod[l, k + 1]),
                    gm=0.5 * row(mod[l, k + 2]), w_in=w["ffn_w_in"],
                    wo=(lambda act: w["ffn_w_out"](act).reshape(D_FF, D)) if callable(w["ffn_w_out"])
                    else w["ffn_w_out"].reshape(D_FF, D))

    def gdn_params(l, w):
        w_in = w["gdn_w_in"].transpose(1, 0, 2).reshape(D, 4 * D + 2 * HEADS)
        pad = lambda t: jnp.pad(t, ((0, 0), (0, 128 - HEADS)))
        return dict(gain=row(norm_g[l, 1]), shift=row(mod[l, 3]), scale=row(mod[l, 4]), gm=row(mod[l, 5]),
                    w_main=w_in[:, :4 * D],
                    w_tail=jnp.concatenate([pad(w_in[:, 4 * D:4 * D + HEADS]), pad(w_in[:, 4 * D + HEADS:])], axis=1),
                    conv_w=conv_w[l], a_log=_pad128(row(a["gdn_a_log"][l])), dt_bias=_pad128(row(a["gdn_dt_bias"][l])),
                    norm_g=row(a["gdn_norm_g"][l]), w_out=w["gdn_w_out"].reshape(D, D))

    def mla_params(l, w):
        j = l - N_A
        uq = w["mla_w_uq"].transpose(1, 0, 2)
        qg = _dup_rope(a["mla_q_norm_g"][j])
        return dict(gain=row(norm_g[l, 1]), shift=row(mod[l, 3]), scale=row(mod[l, 4]), gm=row(mod[l, 5]),
                    w_dq=w["mla_w_dq"].reshape(D, Q_LORA), q_lora_g=row(a["mla_q_lora_norm_g"][j]),
                    w_uq=_dup_rope(uq).reshape(Q_LORA, HEADS * HEAD_PAD), q_gn=row(qg[:NOPE]), q_gr=row(qg[NOPE:]),
                    w_out=w["mla_w_out"].reshape(D, D))

    def kv_params(w):
        w_dkv = w["mla_w_dkv"].reshape(D, KV_LORA + ROPE)
        kg = _dup_rope(a["mla_k_norm_g"])
        return dict(gain=row(a["kv_norm_g"]), shift=row(kvmod[:D]), scale=row(kvmod[D:]),
                    w_dkv=jnp.concatenate([w_dkv, w_dkv[:, KV_LORA:]], axis=1), kv_g=row(a["mla_kv_norm_g"]),
                    w_ukv=w["mla_w_ukv"].transpose(1, 0, 2).reshape(KV_LORA, HEADS * 2 * HEAD), k_gn=row(kg[:NOPE]),
                    k_gr=row(kg[NOPE:]))

    tapes, kv, kv_p, h = [[] for _ in range(DEPTH)], None, None, None
    def landed(handle, after, name):
        srcs, lands = _send_wait(handle, after, name, gather=True)
        return lax.dynamic_update_slice(lands[0], srcs[0][None], (me, 0, 0))

    sh0 = stage_shards(0, 0)
    start_in = _send_start([(sh0["ffn_w_in"] + zero_of(mod)).astype(BF16)], "fetch_start_0_0", gather=True)
    start_out = _send_start([(sh0["ffn_w_out"] + start_in["token"][0, 0]).astype(BF16)], "fetch_start_0_0_out",
                            gather=True)
    pending = []
    for l, part in stages[1:1 + AHEAD]:
        pending.append(start_stage(l, part, (pending[-1][1] if pending else start_out)["token"][0, 0]))
    first = {"ffn_w_in": landed(start_in, mod, "fetch_wait_0_0"),
             "ffn_w_out": lambda act: landed(start_out, act, "fetch_wait_0_0_out")}
    for n, (l, part) in enumerate(stages):
        if n == 0:
            w, ties = first, (start_out["token"],) + tuple(h["token"] for _, h in pending)
        else:
            names, handle = pending.pop(0)
            w = finish_stage(l, part, names, handle, x)
            ties = ()
            if n + AHEAD < len(stages):
                pending.append(start_stage(*stages[n + AHEAD], zero_of(w[names[0]])))
                ties = (pending[-1][1]["token"],)
        nxt = None
        if n + 1 < len(stages):
            l2, part2 = stages[n + 1]
            k2 = 3 * part2
            nxt = (row(norm_g[l2, part2]), row(mod[l2, k2 + 1]), row(mod[l2, k2]))
        if part != 1:
            p = ffn_params(l, part // 2, w)
            x, t, h = _ffn_fwd(x, p, ties, h, nxt)
        else:
            p = gdn_params(l, w) if l < N_A else mla_params(l, w)
            x, t, h = _gdn_fwd(x, p, ties, h, nxt) if l < N_A else _mla_fwd(x, p, kv, ties, h, nxt)
        tapes[l] += [p, t]
        if part == 2 and l == N_A - 1:
            kv_p = kv_params(w)
            kv = _kv_fwd(x, kv_p, cosf, sins)
    dx, loss_blk = _loss_and_grad(x, a["loss_target"][0], "loss")
    loss = lax.psum(loss_blk[0, 0], ("x", "y", "c"))

    def by_cols(g, n):
        return g.reshape(g.shape[0], -1, n).transpose(1, 0, 2)

    def ffn_blocks(g):
        return {"ffn_w_in": g["w_in"], "ffn_w_out": g["wo"].reshape(N_DEV, D_FF // N_DEV, D)}

    def mixer_blocks(l, g):
        if l < N_A:
            full = jnp.concatenate([g["w_main"], g["w_tail"][:, :HEADS], g["w_tail"][:, 128:128 + HEADS]], axis=1)
            return {"gdn_w_in": by_cols(full, n_gdn), "gdn_w_out": g["w_out"].reshape(N_DEV, D // N_DEV, D)}
        return {"mla_w_dq": g["w_dq"].reshape(N_DEV, D // N_DEV, Q_LORA),
                "mla_w_uq": _fold_rope(g["w_uq"].reshape(Q_LORA, HEADS, HEAD_PAD)).transpose(1, 0, 2),
                "mla_w_out": g["w_out"].reshape(N_DEV, D // N_DEV, D)}

    sent = []

    def send(key, blocks, tie=0.0):
        handle = _send_start([(b + tie).astype(BF16) for b in blocks.values()], "grad_start_" + "_".join(map(str, key)),
                             gather=False)
        sent.append((key, list(blocks), handle))
        return (handle["token"],)

    grads = [None] * DEPTH
    dk_sum = dv_sum = kv_grads = res = None
    ties = ()
    for l in reversed(range(DEPTH)):
        p1, t1, pm_, tm_, p2, t2 = tapes[l]
        if l == N_A - 1:
            dx, kv_grads, res = _kv_bwd(kv, kv_p, dk_sum, dv_sum, dx, (t2["y"], p2["gm"]))
            d_dkv = kv_grads["w_dkv"]
            ties += send((l, 3), {
                "mla_w_dkv": jnp.concatenate(
                    [d_dkv[:, :KV_LORA], d_dkv[:, KV_LORA:KV_LORA + ROPE] + d_dkv[:, KV_LORA + ROPE:]],
                    axis=1).reshape(N_DEV, D // N_DEV, KV_LORA + ROPE),
                "mla_w_ukv": by_cols(kv_grads["w_ukv"], 2 * HEAD)})
        dx, g2, res = _ffn_bwd(t2, p2, dx, res, (tm_["y"], pm_["gm"]), ties)
        ties = send((l, 2), ffn_blocks(g2))
        if l < N_A:
            dx, gm_, res = _gdn_bwd(tm_, pm_, dx, res, (t1["y"], p1["gm"]), ties)
        else:
            dx, gm_, res, dk_sum, dv_sum = _mla_bwd(tm_, pm_, kv, dx, res, (t1["y"], p1["gm"]), ties,
                                                    None if dk_sum is None else (dk_sum, dv_sum))
        ties = send((l, 1), mixer_blocks(l, gm_))
        prev = (tapes[l - 1][5]["y"], tapes[l - 1][4]["gm"]) if l > 0 and l != N_A else None
        if l > 0:
            dx, g1, res = _ffn_bwd(t1, p1, dx, res, prev, ties)
            ties = send((l, 0), ffn_blocks(g1))
        else:
            dx, g1, res = _ffn_bwd(t1, p1, dx, res, prev, ties, early=lambda dwo: send(
                (0, 0, "out"), {"ffn_w_out": dwo.reshape(N_DEV, D_FF // N_DEV, D)}))
        grads[l] = (g1, gm_, g2)

    out = {}
    def dmod(l):
        g1, gm_, g2 = grads[l]
        return jnp.concatenate([g1["shift"], g1["scale"], 0.5 * g1["gm"], gm_["shift"], gm_["scale"], gm_["gm"],
                                g2["shift"], g2["scale"], 0.5 * g2["gm"]], axis=1)

    gdn = [grads[l][1] for l in range(N_A)]
    mla = [grads[l][1] for l in range(N_A, DEPTH)]
    small = {
        "ada_b": jnp.concatenate([dmod(l) for l in range(DEPTH)], axis=0),
        "kv_ada_b": jnp.concatenate([kv_grads["shift"], kv_grads["scale"]], axis=1),
        "norm_g": jnp.stack([jnp.concatenate([grads[l][0]["gain"], grads[l][1]["gain"], grads[l][2]["gain"]], axis=0)
                             for l in range(DEPTH)]),
        "gdn_conv_w": jnp.stack([g["conv_w"] for g in gdn]),
        "gdn_a_log": jnp.stack([g["a_log"][0, :HEADS] for g in gdn]),
        "gdn_dt_bias": jnp.stack([g["dt_bias"][0, :HEADS] for g in gdn]),
        "gdn_norm_g": jnp.stack([g["norm_g"][0] for g in gdn]),
        "kv_norm_g": kv_grads["gain"],
        "mla_kv_norm_g": kv_grads["kv_g"],
        "mla_k_norm_g": _fold_rope(jnp.concatenate([kv_grads["k_gn"], kv_grads["k_gr"]], axis=1)),
        "mla_q_lora_norm_g": jnp.stack([g["q_lora_g"][0] for g in mla]),
        "mla_q_norm_g": jnp.stack([_fold_rope(jnp.concatenate([g["q_gn"], g["q_gr"]], axis=1))[0] for g in mla]),
    }
    rows = 616
    assert sum(n for _, n in SMALL) <= rows * 128 and all(small[n].size == k for n, k in SMALL)
    small_recv = _all_gather(_pack([small[n] for n, _ in SMALL], rows), "gather_small_grads")
    small_recv = small_recv + send((0, 0), {"ffn_w_in": grads[0][0]["w_in"]}, zero_of(small_recv))[0][0, 0]
    zero = lambda n, k: jnp.zeros((k,), F32)
    packed = {pre: _pack([a[pre + n] if n in SMALL_REPLICATED else zero(n, k) for n, k in SMALL], rows)
              for pre in ("", "m_", "v_")}
    res = _adamw([small_recv], packed[""], packed["m_"], packed["v_"], "adamw_small")
    offs = {}
    o = 0
    for n, k in SMALL:
        offs[n] = o
        o += k
    for n, k in SMALL:
        if n in SMALL_REPLICATED:
            out[n] = [r.reshape(-1)[offs[n]:offs[n] + k] for r in res]
    gsum = res[0].reshape(-1)
    g_norm = lax.dynamic_slice_in_dim(gsum[offs["norm_g"]:offs["norm_g"] + DEPTH * 3 * D].reshape(DEPTH * 3, D),
                                      me * HEAD, HEAD, axis=1)
    g_conv = lax.dynamic_slice_in_dim(
        gsum[offs["gdn_conv_w"]:offs["gdn_conv_w"] + N_A * CONV_K * 3 * D].reshape(N_A * CONV_K, 3 * D),
        me * 3 * HEAD, 3 * HEAD, axis=1)
    res2 = _adamw([_pack([g_norm, g_conv], 36)[None]], *[_pack([a[pre + "norm_g"], a[pre + "gdn_conv_w"]], 36)
                                                      for pre in ("", "m_", "v_")], "adamw_small")
    out["norm_g"] = [r.reshape(-1)[:n_ng] for r in res2]
    out["gdn_conv_w"] = [r.reshape(-1)[n_ng:n_ng + n_cw] for r in res2]

    c_act_t = c_act.T
    all_small = small_recv.reshape(N_DEV, -1)
    dmod_all = all_small[:, :DEPTH * N_MOD * D].reshape(N_DEV, DEPTH, N_MOD * D)
    dmod_mine = lax.dynamic_slice_in_dim(dmod_all, me * n_ada, n_ada, axis=2)
    g_ada = [_outer8(c_act_t, dmod_mine[:, l], "ada_grad")[None] for l in range(DEPTH)]
    out["ada_w"] = _adamw(g_ada, *[a[pre + "ada_w"].reshape(DEPTH * D, n_ada) for pre in ("", "m_", "v_")], "adamw")
    dkv_all = all_small[:, offs["kv_ada_b"]:offs["kv_ada_b"] + 2 * D]
    g_kv = _outer8(c_act_t, lax.dynamic_slice_in_dim(dkv_all, me * (2 * D // N_DEV), 2 * D // N_DEV, axis=1), "ada_grad")
    out["kv_ada_w"] = _adamw([g_kv[None]], *[a[pre + "kv_ada_w"] for pre in ("", "m_", "v_")], "adamw")

    pieces = {}
    for key, names, handle in sent:
        srcs, lands = _send_wait(handle, out["kv_ada_w"][0], "grad_wait_" + "_".join(map(str, key)), gather=False)
        for name, src, land in zip(names, srcs, lands):
            own = lax.dynamic_slice_in_dim(src, me, 1, axis=0)
            pieces.setdefault(name, []).append((key, lax.dynamic_update_slice(land, own, (me, 0, 0))))
    for name, parts in pieces.items():
        wide = a[name].shape[-1]
        out[name] = _adamw([p for _, p in sorted(parts, key=lambda kp: kp[0])], a[name].reshape(-1, wide),
                           a["m_" + name].reshape(-1, wide), a["v_" + name].reshape(-1, wide), "adamw")

    result = [loss, dx[None]]
    for k in range(4):
        result += [out[n][k].reshape(a[n].shape) for n in WEIGHTS]
    return tuple(result)


def kernel(x, c, positions, ada_w, ada_b, norm_g, ffn_w_in, ffn_w_out, gdn_w_in, gdn_conv_w, gdn_a_log, gdn_dt_bias, gdn_norm_g, gdn_w_out, kv_ada_w, kv_ada_b, kv_norm_g, mla_w_dkv, mla_kv_norm_g, mla_w_ukv, mla_k_norm_g, mla_w_dq, mla_q_lora_norm_g, mla_w_uq, mla_q_norm_g, mla_w_out, loss_target, m_ada_w, m_ada_b, m_norm_g, m_ffn_w_in, m_ffn_w_out, m_gdn_w_in, m_gdn_conv_w, m_gdn_a_log, m_gdn_dt_bias, m_gdn_norm_g, m_gdn_w_out, m_kv_ada_w, m_kv_ada_b, m_kv_norm_g, m_mla_w_dkv, m_mla_kv_norm_g, m_mla_w_ukv, m_mla_k_norm_g, m_mla_w_dq, m_mla_q_lora_norm_g, m_mla_w_uq, m_mla_q_norm_g, m_mla_w_out, v_ada_w, v_ada_b, v_norm_g, v_ffn_w_in, v_ffn_w_out, v_gdn_w_in, v_gdn_conv_w, v_gdn_a_log, v_gdn_dt_bias, v_gdn_norm_g, v_gdn_w_out, v_kv_ada_w, v_kv_ada_b, v_kv_norm_g, v_mla_w_dkv, v_mla_kv_norm_g, v_mla_w_ukv, v_mla_k_norm_g, v_mla_w_dq, v_mla_q_lora_norm_g, v_mla_w_uq, v_mla_q_norm_g, v_mla_w_out):
    return _step(dict(locals()))
```

```python
import functools

import jax
import jax.numpy as jnp
from jax import lax
from jax.experimental import pallas as pl
from jax.experimental.pallas import tpu as pltpu

F32 = jnp.float32
BF16 = jnp.bfloat16

N_DEV = 8
D = 1024
D_FF = 2816
DEPTH = 4
N_A = 2
N_MOD = 9
HEADS = 8
HEAD = 128
CHUNK = 64
CONV_K = 4
KV_LORA = 256
Q_LORA = 384
NOPE = 128
ROPE = 64
QK_HEAD = NOPE + ROPE
HEAD_PAD = 256
ROPE_BASE = 10000.0
EPS = 1e-6
LR, B1, B2, ADAM_EPS, WD, STEP = 0.001, 0.9, 0.999, 1e-08, 0.01, 10

VMEM_LIMIT = 48 * 1024 * 1024
ROW_TILE = 256
MESH = pl.DeviceIdType.MESH

_NN = (((1,), (0,)), ((), ()))
_NT = (((1,), (1,)), ((), ()))
_TN = (((0,), (0,)), ((), ()))
_DIMS = {"nn": _NN, "nt": _NT, "tn": _TN}


def _params(dims=None):
    return pltpu.CompilerParams(dimension_semantics=dims, vmem_limit_bytes=VMEM_LIMIT)


def _tile(n, target):
    for t in range(target - target % 128, 0, -128):
        if n % t == 0:
            return t
    return n


_TIE_SPEC1 = pl.BlockSpec((8, 128), lambda i: (0, 0))
_TIE_SPEC2 = pl.BlockSpec((8, 128), lambda i, j: (0, 0))
_TIE_SPEC3 = pl.BlockSpec((8, 128), lambda i, j, k: (0, 0))


def _matmul(pairs, form, name, out_dtype=F32, tm=1408, tn=1408, tk=1408, boffs=None, resid=None, ties=()):
    a0, b0 = pairs[0]
    if form == "nn":
        m, n = a0.shape[0], b0.shape[1]
        ks = [a.shape[1] for a, _ in pairs]
    elif form == "nt":
        m, n = a0.shape[0], b0.shape[0]
        ks = [a.shape[1] for a, _ in pairs]
    else:
        m, n = a0.shape[1], b0.shape[1]
        ks = [a.shape[0] for a, _ in pairs]
    tm, tn = _tile(m, tm), _tile(n, tn)
    tks = [_tile(k, tk) for k in ks]
    boffs = boffs or [0] * len(pairs)
    assert m % tm == 0 and n % tn == 0 and all(o % t == 0 for o, t in zip(boffs, tks)), (name, m, n, ks)
    steps = [k // t for k, t in zip(ks, tks)]
    starts = [sum(steps[:p]) for p in range(len(pairs))]
    nk = sum(steps)

    def kidx(p, k):
        return jnp.clip(k - starts[p], 0, steps[p] - 1)

    in_specs, args = [], []
    for p, (a, b) in enumerate(pairs):
        t = tks[p]
        if form == "tn":
            in_specs.append(pl.BlockSpec((t, tm), lambda i, j, k, p=p: (kidx(p, k), i)))
            in_specs.append(pl.BlockSpec((t, tn), lambda i, j, k, p=p: (kidx(p, k), j)))
        elif form == "nn":
            in_specs.append(pl.BlockSpec((tm, t), lambda i, j, k, p=p: (i, kidx(p, k))))
            in_specs.append(pl.BlockSpec((t, tn), lambda i, j, k, p=p: (kidx(p, k), j)))
        else:
            in_specs.append(pl.BlockSpec((tm, t), lambda i, j, k, p=p: (i, kidx(p, k))))
            in_specs.append(pl.BlockSpec((tn, t), lambda i, j, k, p=p, o=boffs[p] // t: (j, kidx(p, k) + o)))
        args += [a, b]
    dims = _DIMS[form]
    npairs = len(pairs)
    nres = len(resid or ())
    nin = 2 * npairs + len(ties) + nres
    out_blk = pl.BlockSpec((tm, tn), lambda i, j, k: (i, j))
    in_specs += [_TIE_SPEC3] * len(ties)
    args += list(ties)
    if resid:
        assert nres == 2 or (nres == 5 and tn == n)
        in_specs += [out_blk] + [pl.BlockSpec((1, tn), lambda i, j, k: (0, j))] * (nres - 1)
        args += list(resid)

    def body(*refs):
        o_ref = refs[nin]
        k = pl.program_id(2)

        def prod(p):
            return lax.dot_general(refs[2 * p][...].astype(BF16), refs[2 * p + 1][...].astype(BF16), dims,
                                   preferred_element_type=F32)

        def finish(y):
            o_ref[...] = y.astype(o_ref.dtype)
            if resid:
                x_ref, gate_ref = refs[nin - nres], refs[nin - nres + 1]
                xn = x_ref[...] + gate_ref[...] * y
                refs[nin + 1][...] = xn
                if nres == 5:
                    gain, scale, shift = (r[...] for r in refs[nin - 3:nin])
                    refs[nin + 2][...] = _modulate_fn(0, xn, gain, scale, shift)[0].astype(BF16)

        if nk == 1:
            finish(prod(0))
            return
        acc = refs[-1]

        @pl.when(k == 0)
        def _():
            acc[...] = jnp.zeros_like(acc)

        for p in range(npairs):
            @pl.when((k >= starts[p]) & (k < starts[p] + steps[p]))
            def _(p=p):
                acc[...] += prod(p)

        @pl.when(k == nk - 1)
        def _():
            finish(acc[...])

    res = pl.pallas_call(
        body, name=name, grid=(m // tm, n // tn, nk), in_specs=in_specs,
        out_specs=[out_blk] * (2 + (nres == 5)) if resid else out_blk,
        out_shape=([jax.ShapeDtypeStruct((m, n), out_dtype), jax.ShapeDtypeStruct((m, n), F32)]
                   + [jax.ShapeDtypeStruct((m, n), BF16)] * (nres == 5))
        if resid else jax.ShapeDtypeStruct((m, n), out_dtype),
        scratch_shapes=[] if nk == 1 else [pltpu.VMEM((tm, tn), F32)],
        compiler_params=_params(("parallel", "parallel", "arbitrary")),
    )(*args)
    return res


def _mm(a, b, form, name, **kw):
    return _matmul([(a, b)], form, name, **kw)


def _cols(spec, g):
    return spec[g] if isinstance(spec, list) else spec


def _rowwise_fwd(fn, rows, pars, outs, name, groups=1, ts=ROW_TILE, ties=()):
    s = rows[0][0].shape[0]
    ts = min(ts, s)
    assert s % ts == 0
    nr, npar = len(rows), len(pars)

    def body(*refs):
        par_t = [r[...] for r in refs[nr:nr + npar]]
        out_refs = refs[nr + npar + len(ties):]
        for g in range(groups):
            row_t = []
            for r, (_, spec) in zip(refs[:nr], rows):
                c0, w = _cols(spec, g)
                row_t.append(r[:, c0:c0 + w].astype(F32))
            res = fn(g, *row_t, *par_t)
            for o_ref, val, (_, _, spec) in zip(out_refs, res, outs):
                c0, w = _cols(spec, g)
                o_ref[:, c0:c0 + w] = val.astype(o_ref.dtype)

    return pl.pallas_call(
        body, name=name, grid=(s // ts,),
        in_specs=[pl.BlockSpec((ts, a.shape[1]), lambda i: (i, 0)) for a, _ in rows]
        + [pl.BlockSpec(p.shape, lambda i: (0, 0)) for p in pars] + [_TIE_SPEC1] * len(ties),
        out_specs=[pl.BlockSpec((ts, w), lambda i: (i, 0)) for w, _, _ in outs],
        out_shape=[jax.ShapeDtypeStruct((s, w), dt) for w, dt, _ in outs],
        compiler_params=_params(("parallel",)),
    )(*[a for a, _ in rows], *pars, *ties)


def _rowwise_bwd(fn, rows, pars, outs, douts, gmap, gshapes, name, groups=1, add=None, par_grads=True,
                 ts=ROW_TILE):
    s = rows[0][0].shape[0]
    ts = min(ts, s)
    assert s % ts == 0
    nr, npar, nout, ng = len(rows), len(pars), len(outs), len(gshapes)
    add = add or {}
    add_keys = sorted(add)

    def body(*refs):
        row_refs = refs[:nr]
        par_refs = refs[nr:nr + npar]
        dout_refs = refs[nr + npar:nr + npar + nout]
        add_refs = refs[nr + npar + nout:nr + npar + nout + len(add_keys)]
        g_refs = refs[nr + npar + nout + len(add_keys):][:ng]
        pg_refs = refs[nr + npar + nout + len(add_keys) + ng:]
        par_t = [r[...] for r in par_refs]
        par_acc = [None] * npar
        shared_acc = {}
        for g in range(groups):
            row_t = []
            for r, (_, spec) in zip(row_refs, rows):
                c0, w = _cols(spec, g)
                row_t.append(r[:, c0:c0 + w].astype(F32))
            cts = []
            for r, (_, _, spec) in zip(dout_refs, outs):
                c0, w = _cols(spec, g)
                cts.append(r[:, c0:c0 + w].astype(F32))
            _, vjp = jax.vjp(lambda *t, g=g: tuple(fn(g, *t)), *row_t, *par_t)
            grads = vjp(tuple(cts))
            for k in range(nr):
                if gmap[k] is None:
                    continue
                gi, spec = gmap[k]
                if isinstance(spec, list) or groups == 1:
                    c0, w = _cols(spec, g)
                    val = grads[k]
                    if gi in add:
                        val = val + add_refs[add_keys.index(gi)][:, c0:c0 + w].astype(F32)
                    g_refs[gi][:, c0:c0 + w] = val.astype(g_refs[gi].dtype)
                else:
                    shared_acc[k] = grads[k] if k not in shared_acc else shared_acc[k] + grads[k]
            if par_grads:
                for k in range(npar):
                    pg = grads[nr + k]
                    par_acc[k] = pg if par_acc[k] is None else par_acc[k] + pg
        for k, val in shared_acc.items():
            gi, (c0, w) = gmap[k]
            assert gi not in add
            g_refs[gi][:, c0:c0 + w] = val.astype(g_refs[gi].dtype)
        if par_grads:
            first = pl.program_id(0) == 0
            for k in range(npar):
                @pl.when(first)
                def _(k=k):
                    pg_refs[k][...] = par_acc[k]

                @pl.when(jnp.logical_not(first))
                def _(k=k):
                    pg_refs[k][...] += par_acc[k]

    out_specs = [pl.BlockSpec((ts, w), lambda i: (i, 0)) for w, _ in gshapes]
    out_shape = [jax.ShapeDtypeStruct((s, w), dt) for w, dt in gshapes]
    if par_grads:
        out_specs += [pl.BlockSpec(p.shape, lambda i: (0, 0)) for p in pars]
        out_shape += [jax.ShapeDtypeStruct(p.shape, F32) for p in pars]
    return pl.pallas_call(
        body, name=name, grid=(s // ts,),
        in_specs=[pl.BlockSpec((ts, a.shape[1]), lambda i: (i, 0)) for a, _ in rows]
        + [pl.BlockSpec(p.shape, lambda i: (0, 0)) for p in pars]
        + [pl.BlockSpec((ts, a.shape[1]), lambda i: (i, 0)) for a in douts]
        + [pl.BlockSpec((ts, add[k].shape[1]), lambda i: (i, 0)) for k in add_keys],
        out_specs=out_specs, out_shape=out_shape,
        compiler_params=_params(("arbitrary",)),
    )(*[a for a, _ in rows], *pars, *douts, *[add[k] for k in add_keys])


def _sigmoid(x):
    return 1.0 / (1.0 + jnp.exp(-x))


def _silu(x):
    return x * _sigmoid(x)


def _softplus(x):
    return jnp.maximum(x, 0.0) + jnp.log(1.0 + jnp.exp(-jnp.abs(x)))


def _rms(t, g, n=None):
    n = n or t.shape[-1]
    return t * lax.rsqrt(jnp.sum(t * t, axis=-1, keepdims=True) / n + EPS) * g


def _modulate_fn(g, x, gain, scale, shift):
    return (_rms(x, gain) * (1.0 + scale) + shift,)


def _gate_only_fn(g, y, gm):
    return (gm * y,)


def _gdn_gates_fn(g, b_logit, a_logit, a_log, dt_bias):
    gate = -jnp.exp(a_log) * _softplus(a_logit + dt_bias)
    n = gate.shape[0]
    i = lax.broadcasted_iota(jnp.int32, (n, n), 0)
    j = lax.broadcasted_iota(jnp.int32, (n, n), 1)
    tri = (((i // CHUNK) == (j // CHUNK)) & (i >= j)).astype(F32)
    gcum = lax.dot_general(tri, gate, _NN, preferred_element_type=F32, precision=lax.Precision.HIGHEST)
    return _sigmoid(b_logit), gcum


def _gdn_outnorm_fn(g, o, z, gain):
    return (_rms(o, gain) * _silu(z),)


def _rms_fn(g, t, gain):
    return (_rms(t, gain),)


@jax.custom_vjp
def _swap_halves(t):
    return pltpu.roll(t, 32, 1)


_swap_halves.defvjp(lambda t: (pltpu.roll(t, 32, 1), None), lambda _, ct: (pltpu.roll(ct, 96, 1),))


def _head_norm_rope_fn(g, nope, rope, cosf, sins, gain_n, gain_r):
    first = lax.broadcasted_iota(jnp.int32, rope.shape, 1) < ROPE
    ss = jnp.sum(nope * nope, axis=-1, keepdims=True) + jnp.sum(jnp.where(first, rope * rope, 0.0), axis=-1,
                                                                 keepdims=True)
    r = lax.rsqrt(ss / QK_HEAD + EPS)
    tn = nope * r * gain_n
    tr = rope * r * gain_r
    rot = jnp.where(first, tr * cosf + _swap_halves(tr) * sins, 0.0)
    return tn, rot


def _q_norm_rope_fn(g, nope, rope, cosf, sins, gain_n, gain_r):
    tn, rot = _head_norm_rope_fn(g, nope, rope, cosf, sins, gain_n, gain_r)
    return (jnp.concatenate([tn, rot], axis=1),)


def _k_norm_rope_fn(g, nope, val, rope, cosf, sins, gain_n, gain_r):
    tn, rot = _head_norm_rope_fn(g, nope, rope, cosf, sins, gain_n, gain_r)
    return jnp.concatenate([tn, rot], axis=1), val


FF_SH = 2 * D_FF // N_DEV
FF_G = N_DEV // 2


def _ffn_in(h, w_in, name, tm=1024, ties=()):
    s = h.shape[0]
    tm = min(tm, s)

    def body(h_ref, wg_ref, wu_ref, *rest):
        g_ref, u_ref, a_ref = rest[-3:]
        hb = h_ref[...]
        gate = jnp.dot(hb, wg_ref[...], preferred_element_type=F32)
        up = jnp.dot(hb, wu_ref[...], preferred_element_type=F32)
        sg = _sigmoid(gate)
        silu = gate * sg
        g_ref[...] = (up * (sg * (1.0 + gate * (1.0 - sg)))).astype(BF16)
        u_ref[...] = silu.astype(BF16)
        a_ref[...] = (silu * up).astype(BF16)

    spec = pl.BlockSpec((None, tm, FF_SH), lambda j, i: (j, i, 0))
    return pl.pallas_call(
        body, name=name, grid=(FF_G, s // tm),
        in_specs=[pl.BlockSpec((tm, D), lambda j, i: (i, 0)), pl.BlockSpec((None, D, FF_SH), lambda j, i: (j, 0, 0)),
                  pl.BlockSpec((None, D, FF_SH), lambda j, i: (j + FF_G, 0, 0))] + [_TIE_SPEC2] * len(ties),
        out_specs=[spec, spec, spec], out_shape=[jax.ShapeDtypeStruct((FF_G, s, FF_SH), BF16)] * 3,
        compiler_params=_params(("parallel", "parallel")),
    )(h, w_in, w_in, *ties)


def _ffn_out(act, wo, resid, name, tm=512):
    s = act.shape[1]
    tm = min(tm, s)
    nres = len(resid)

    def body(a_ref, b_ref, x_ref, gate_ref, *rest):
        mods, outs = rest[:nres - 2], rest[nres - 2:]
        y = jnp.dot(a_ref[0], b_ref[0:FF_SH, :], preferred_element_type=F32)
        for k in range(1, FF_G):
            y = y + jnp.dot(a_ref[k], b_ref[k * FF_SH:(k + 1) * FF_SH, :], preferred_element_type=F32)
        xn = x_ref[...] + gate_ref[...] * y
        outs[0][...] = y.astype(BF16)
        outs[1][...] = xn
        if mods:
            outs[2][...] = _modulate_fn(0, xn, *[m[...] for m in mods])[0].astype(BF16)

    blk = pl.BlockSpec((tm, D), lambda i: (i, 0))
    vec = pl.BlockSpec((1, D), lambda i: (0, 0))
    return pl.pallas_call(
        body, name=name, grid=(s // tm,),
        in_specs=[pl.BlockSpec((FF_G, tm, FF_SH), lambda i: (0, i, 0)), pl.BlockSpec((D_FF, D), lambda i: (0, 0)),
                  blk] + [vec] * (nres - 1),
        out_specs=[blk] * (2 + (nres == 5)),
        out_shape=[jax.ShapeDtypeStruct((s, D), BF16), jax.ShapeDtypeStruct((s, D), F32)]
        + [jax.ShapeDtypeStruct((s, D), BF16)] * (nres == 5),
        compiler_params=_params(("parallel",)),
    )(act, wo, *resid)


def _ffn_bwd_act(dy, wo, act_dgate, act_dup, name, tm=1024, ties=()):
    s = dy.shape[0]
    tm = min(tm, s)

    def body(dy_ref, wo_ref, g_ref, u_ref, *rest):
        dg_ref, du_ref = rest[-2:]
        dact = lax.dot_general(dy_ref[...], wo_ref[...], _NT, preferred_element_type=F32)
        dg_ref[...] = (dact * g_ref[...].astype(F32)).astype(BF16)
        du_ref[...] = (dact * u_ref[...].astype(F32)).astype(BF16)

    spec = pl.BlockSpec((None, tm, FF_SH), lambda j, i: (j, i, 0))
    return pl.pallas_call(
        body, name=name, grid=(FF_G, s // tm),
        in_specs=[pl.BlockSpec((tm, D), lambda j, i: (i, 0)), pl.BlockSpec((FF_SH, D), lambda j, i: (j, 0)), spec, spec]
        + [_TIE_SPEC2] * len(ties),
        out_specs=[spec, spec], out_shape=[jax.ShapeDtypeStruct((FF_G, s, FF_SH), BF16)] * 2,
        compiler_params=_params(("parallel", "parallel")),
    )(dy, wo, act_dgate, act_dup, *ties)


def _ffn_dwo(act, dy, name, tk=2048, ties=()):
    s = act.shape[1]
    tk = min(tk, s)

    def body(a_ref, b_ref, *rest):
        o_ref, acc = rest[-2:]
        k = pl.program_id(1)

        @pl.when(k == 0)
        def _():
            acc[...] = jnp.zeros_like(acc)

        acc[...] += lax.dot_general(a_ref[...], b_ref[...], _TN, preferred_element_type=F32)

        @pl.when(k == s // tk - 1)
        def _():
            o_ref[...] = acc[...].astype(BF16)

    return pl.pallas_call(
        body, name=name, grid=(FF_G, s // tk),
        in_specs=[pl.BlockSpec((None, tk, FF_SH), lambda j, k: (j, k, 0)), pl.BlockSpec((tk, D), lambda j, k: (k, 0))]
        + [_TIE_SPEC2] * len(ties),
        out_specs=pl.BlockSpec((FF_SH, D), lambda j, k: (j, 0)), out_shape=jax.ShapeDtypeStruct((D_FF, D), BF16),
        scratch_shapes=[pltpu.VMEM((FF_SH, D), F32)], compiler_params=_params(("parallel", "arbitrary")),
    )(act, dy, *ties)


def _ffn_halves(k, gate_ref, up_ref, fn):
    pl.when(k < FF_G)(functools.partial(fn, gate_ref))
    pl.when(k >= FF_G)(functools.partial(fn, up_ref))


def _ffn_dh(dgate, dup, w_in, name, tm=512, ties=()):
    s = dgate.shape[1]
    tm = min(tm, s)

    def body(dg_ref, du_ref, w_ref, *rest):
        acc = lax.dot_general(dg_ref[0], w_ref[0], _NT, preferred_element_type=F32)
        for k in range(1, N_DEV):
            d_ref = dg_ref if k < FF_G else du_ref
            acc = acc + lax.dot_general(d_ref[k % FF_G], w_ref[k], _NT, preferred_element_type=F32)
        rest[-1][...] = acc

    half = pl.BlockSpec((FF_G, tm, FF_SH), lambda i: (0, i, 0))
    return pl.pallas_call(
        body, name=name, grid=(s // tm,),
        in_specs=[half, half, pl.BlockSpec((N_DEV, D, FF_SH), lambda i: (0, 0, 0))] + [_TIE_SPEC1] * len(ties),
        out_specs=pl.BlockSpec((tm, D), lambda i: (i, 0)), out_shape=jax.ShapeDtypeStruct((s, D), F32),
        compiler_params=_params(("parallel",)),
    )(dgate, dup, w_in, *ties)


def _ffn_dwi(h, dgate, dup, name, tk=2048):
    s = h.shape[0]
    tk = min(tk, s)

    def body(h_ref, dg_ref, du_ref, o_ref, acc):
        j, k = pl.program_id(0), pl.program_id(1)

        @pl.when(k == 0)
        def _():
            acc[...] = jnp.zeros_like(acc)

        def add(d_ref):
            acc[...] += lax.dot_general(h_ref[...], d_ref[...], _TN, preferred_element_type=F32)

        _ffn_halves(j, dg_ref, du_ref, add)

        @pl.when(k == s // tk - 1)
        def _():
            o_ref[...] = acc[...].astype(BF16)

    return pl.pallas_call(
        body, name=name, grid=(N_DEV, s // tk),
        in_specs=[pl.BlockSpec((tk, D), lambda j, k: (k, 0)),
                  pl.BlockSpec((None, tk, FF_SH), lambda j, k: (jnp.minimum(j, FF_G - 1), jnp.where(j < FF_G, k, s // tk - 1), 0)),
                  pl.BlockSpec((None, tk, FF_SH), lambda j, k: (jnp.maximum(j - FF_G, 0), jnp.where(j < FF_G, 0, k), 0))],
        out_specs=pl.BlockSpec((None, D, FF_SH), lambda j, k: (j, 0, 0)),
        out_shape=jax.ShapeDtypeStruct((N_DEV, D, FF_SH), BF16),
        scratch_shapes=[pltpu.VMEM((D, FF_SH), F32)], compiler_params=_params(("parallel", "arbitrary")),
    )(h, dgate, dup)


def _shift_down(x, d):
    rows = lax.broadcasted_iota(jnp.int32, x.shape, 0)
    return jnp.where(rows >= d, pltpu.roll(x, d, 0), 0.0)


def _shift_up(x, d):
    n = x.shape[0]
    rows = lax.broadcasted_iota(jnp.int32, x.shape, 0)
    return jnp.where(rows < n - d, pltpu.roll(x, n - d, 0), 0.0)


def _conv_post(pre, is_qk):
    a = _silu(pre)
    l2 = a * lax.rsqrt(jnp.sum(a * a, axis=-1, keepdims=True) + EPS)
    return jnp.where(is_qk, l2, a)


def _conv_taps(x):
    return [_shift_down(x, CONV_K - 1 - j) for j in range(CONV_K - 1)] + [x]


def _conv_pre(x, w, taps=None):
    taps = taps or _conv_taps(x)
    pre = taps[0] * w[0:1, :]
    for j in range(1, CONV_K):
        pre = pre + taps[j] * w[j:j + 1, :]
    return pre


def _gdn_conv_fwd(pm, conv_w, name):
    s = pm.shape[0]
    nblk = 3 * D // HEAD

    def body(x_ref, w_ref, o_ref):
        is_qk = pl.program_id(0) < 2 * HEADS
        o_ref[...] = _conv_post(_conv_pre(x_ref[...], w_ref[...]), is_qk)

    return pl.pallas_call(
        body, name=name, grid=(nblk,),
        in_specs=[pl.BlockSpec((s, HEAD), lambda c: (0, c)), pl.BlockSpec((CONV_K, HEAD), lambda c: (0, c))],
        out_specs=pl.BlockSpec((s, HEAD), lambda c: (0, c)),
        out_shape=jax.ShapeDtypeStruct((s, 3 * D), F32), compiler_params=_params(("parallel",)),
    )(pm, conv_w)


def _gdn_conv_bwd(pm, conv_w, dout, part, name):
    s = pm.shape[0]
    off = part * HEADS

    def body(x_ref, w_ref, d_ref, dx_ref, dw_ref):
        x, w = x_ref[...], w_ref[...]
        taps = _conv_taps(x)
        _, vjp = jax.vjp(lambda p: _conv_post(p, part < 2), _conv_pre(x, w, taps))
        dpre, = vjp(d_ref[...])
        dx = dpre * w[CONV_K - 1:CONV_K, :]
        for j in range(CONV_K - 1):
            dx = dx + _shift_up(dpre, CONV_K - 1 - j) * w[j:j + 1, :]
        dx_ref[...] = dx.astype(BF16)
        dw_ref[...] = jnp.concatenate([jnp.sum(dpre * tap, axis=0, keepdims=True) for tap in taps], axis=0)

    return pl.pallas_call(
        body, name=name, grid=(HEADS,),
        in_specs=[pl.BlockSpec((s, HEAD), lambda c: (0, c + off)), pl.BlockSpec((CONV_K, HEAD), lambda c: (0, c + off)),
                  pl.BlockSpec((s, HEAD), lambda c: (0, c))],
        out_specs=[pl.BlockSpec((s, HEAD), lambda c: (0, c)), pl.BlockSpec((CONV_K, HEAD), lambda c: (0, c))],
        out_shape=[jax.ShapeDtypeStruct((s, D), BF16), jax.ShapeDtypeStruct((CONV_K, D), F32)],
        compiler_params=_params(("parallel",)),
    )(pm, conv_w, dout)


def _dot3(a, b, dims=_NN):
    ah, bh = a.astype(BF16), b.astype(BF16)
    al, bl = (a - ah.astype(F32)).astype(BF16), (b - bh.astype(F32)).astype(BF16)
    d = lambda u, v: lax.dot_general(u, v, dims, preferred_element_type=F32)
    return d(ah, bh) + (d(ah, bl) + d(al, bh))


def _make_dot(hi):
    def raw(a, b, dims):
        if hi:
            return _dot3(a, b, dims)
        return lax.dot_general(a.astype(BF16), b.astype(BF16), dims, preferred_element_type=F32)

    @functools.partial(jax.custom_vjp, nondiff_argnums=(2,))
    def dot(a, b, form):
        return raw(a, b, _DIMS[form])

    def fwd(a, b, form):
        return raw(a, b, _DIMS[form]), (a, b)

    def bwd(form, res, ct):
        a, b = res
        if form == "nn":
            return raw(ct, b, _NT), raw(a, ct, _TN)
        if form == "nt":
            return raw(ct, b, _NN), raw(ct, a, _TN)
        return raw(b, ct, _NT), raw(a, ct, _NN)

    dot.defvjp(fwd, bwd)
    return dot


_dot = _make_dot(False)
_dot_hi = _make_dot(True)


def _tri_inv_raw(low):
    n = low.shape[0]
    i = lax.broadcasted_iota(jnp.int32, (n, n), 0)
    j = lax.broadcasted_iota(jnp.int32, (n, n), 1)
    eye = (i == j).astype(F32)
    hdot = _dot3
    same16 = (i // 16) == (j // 16)
    neg = jnp.where(same16, -low, 0.0)
    inv = eye + neg
    power = neg
    for _ in range(3):
        power = hdot(power, power)
        inv = hdot(inv, eye + power)
    for blk in (32, 64):
        off = jnp.where(((i // blk) == (j // blk)) & ((i // (blk // 2)) != (j // (blk // 2))), low, 0.0)
        inv = inv - hdot(inv, hdot(off, inv))
    return inv


@jax.custom_vjp
def _tri_inv(low):
    return _tri_inv_raw(low)


def _tri_inv_fwd(low):
    inv = _tri_inv_raw(low)
    return inv, inv


def _tri_inv_bwd(inv, ct):
    return (-_dot3(_dot3(inv, ct, _TN), inv, _NT),)


_tri_inv.defvjp(_tri_inv_fwd, _tri_inv_bwd)


@jax.custom_vjp
def _tri_inv_given(low, inv):
    return inv


_tri_inv_given.defvjp(lambda low, inv: (inv, inv),
                      lambda inv, ct: (_tri_inv_bwd(inv, ct)[0], jnp.zeros_like(inv)))

GROUP = 4
N_GROUPS = HEADS // GROUP
GROWS = GROUP * CHUNK


def _gdn_group(q, k, v, beta, gc, gr, states, inv=None):
    n = q.shape[0]
    i = lax.broadcasted_iota(jnp.int32, (n, n), 0)
    j = lax.broadcasted_iota(jnp.int32, (n, n), 1)
    same = (i // CHUNK) == (j // CHUNK)
    incl, strict = same & (i >= j), same & (i > j)
    qs = q * (HEAD ** -0.5)
    decay = jnp.where(incl, jnp.exp(jnp.where(incl, gc - gr, 0.0)), 0.0)
    kb = k * beta
    eg = jnp.exp(gc)
    prod = _dot(jnp.concatenate([kb, qs], axis=0), k, "nt")
    low = jnp.where(strict, prod[:n] * decay, 0.0)
    attn = jnp.where(incl, prod[n:] * decay, 0.0)
    inv = _tri_inv(low) if inv is None else _tri_inv_given(low, inv)
    sol = _dot_hi(inv, jnp.concatenate([v * beta, kb * eg], axis=1), "nn")
    u, w, qg = sol[:, :HEAD], sol[:, HEAD:], qs * eg
    last = lax.broadcasted_iota(jnp.int32, (CHUNK, 1), 0) == CHUNK - 1
    v_new, o_state, carry = [], [], []
    for h, state in enumerate(states):
        rows = slice(h * CHUNK, (h + 1) * CHUNK)
        ws = _dot(jnp.concatenate([w[rows], qg[rows]], axis=0), state, "nn")
        v_new.append(u[rows] - ws[:CHUNK])
        o_state.append(ws[CHUNK:])
        g_last = jnp.sum(jnp.where(last, gc[rows], 0.0), axis=0, keepdims=True)
        carry.append((g_last, k[rows] * jnp.exp(g_last - gc[rows])))
    o = jnp.concatenate(o_state, axis=0) + _dot(attn, jnp.concatenate(v_new, axis=0), "nn")
    new = tuple(state * jnp.exp(g_last) + _dot(k_dec, vn, "tn")
                for state, (g_last, k_dec), vn in zip(states, carry, v_new))
    return o, new, inv


def _gdn_specs(s, rev):
    nc = s // CHUNK
    at = (lambda n: nc - 1 - n) if rev else (lambda n: n)
    return nc, at, [
        pl.BlockSpec((CHUNK, D), lambda n: (at(n), 0)), pl.BlockSpec((CHUNK, D), lambda n: (at(n), 1)),
        pl.BlockSpec((CHUNK, D), lambda n: (at(n), 2)), pl.BlockSpec((CHUNK, HEAD), lambda n: (at(n), 0)),
        pl.BlockSpec((CHUNK, HEAD), lambda n: (at(n), 0)),
        pl.BlockSpec((None, N_GROUPS, 1, GROWS), lambda n: (at(n), 0, 0, 0))]


def _group_operands(grp, q_ref, k_ref, v_ref, b_blk, gc_blk, gr_blk):
    heads = range(grp * GROUP, (grp + 1) * GROUP)
    stack = lambda ref: jnp.concatenate([ref[:, h * HEAD:(h + 1) * HEAD] for h in heads], axis=0)
    col = lambda blk: jnp.concatenate([blk[:, h:h + 1] for h in heads], axis=0)
    return stack(q_ref), stack(k_ref), stack(v_ref), col(b_blk), col(gc_blk), gr_blk[grp]


def _gdn_scan_fwd(qkv, beta, gcum, grow, name):
    s = qkv.shape[0]
    nc, _, in_specs = _gdn_specs(s, rev=False)

    def body(q_ref, k_ref, v_ref, b_ref, gc_ref, gr_ref, o_ref, st_ref, inv_ref, state):
        @pl.when(pl.program_id(0) == 0)
        def _():
            state[...] = jnp.zeros_like(state)

        b_blk, gc_blk, gr_blk = b_ref[...], gc_ref[...], gr_ref[...]
        old = [state[h] for h in range(HEADS)]
        res = [_gdn_group(*_group_operands(grp, q_ref, k_ref, v_ref, b_blk, gc_blk, gr_blk),
                          old[grp * GROUP:(grp + 1) * GROUP]) for grp in range(N_GROUPS)]
        for grp, (o, new, inv) in enumerate(res):
            inv_ref[grp] = inv
            for hh in range(GROUP):
                h = grp * GROUP + hh
                st_ref[h] = old[h]
                o_ref[:, h * HEAD:(h + 1) * HEAD] = o[hh * CHUNK:(hh + 1) * CHUNK]
                state[h] = new[hh]

    return pl.pallas_call(
        body, name=name, grid=(nc,), in_specs=in_specs,
        out_specs=[pl.BlockSpec((CHUNK, D), lambda n: (n, 0)),
                   pl.BlockSpec((None, HEADS, HEAD, HEAD), lambda n: (n, 0, 0, 0)),
                   pl.BlockSpec((None, N_GROUPS, GROWS, GROWS), lambda n: (n, 0, 0, 0))],
        out_shape=[jax.ShapeDtypeStruct((s, D), F32), jax.ShapeDtypeStruct((nc, HEADS, HEAD, HEAD), F32),
                   jax.ShapeDtypeStruct((nc, N_GROUPS, GROWS, GROWS), F32)],
        scratch_shapes=[pltpu.VMEM((HEADS, HEAD, HEAD), F32)],
        compiler_params=_params(("arbitrary",)),
    )(qkv, qkv, qkv, beta, gcum, grow)


def _gdn_scan_bwd(qkv, beta, gcum, grow, states, invs, do, name):
    s = qkv.shape[0]
    nc, at, in_specs = _gdn_specs(s, rev=True)
    in_specs += [pl.BlockSpec((None, HEADS, HEAD, HEAD), lambda n: (at(n), 0, 0, 0)),
                 pl.BlockSpec((None, N_GROUPS, GROWS, GROWS), lambda n: (at(n), 0, 0, 0)),
                 pl.BlockSpec((CHUNK, D), lambda n: (at(n), 0))]

    def body(q_ref, k_ref, v_ref, b_ref, gc_ref, gr_ref, st_ref, inv_ref, do_ref, dq_ref, dk_ref, dv_ref, db_ref,
             dgc_ref, dgr_ref, dstate):
        @pl.when(pl.program_id(0) == 0)
        def _():
            dstate[...] = jnp.zeros_like(dstate)

        b_blk, gc_blk, gr_blk = b_ref[...], gc_ref[...], gr_ref[...]
        dold = [dstate[h] for h in range(HEADS)]
        res = []
        for grp in range(N_GROUPS):
            heads = range(grp * GROUP, (grp + 1) * GROUP)
            inv = inv_ref[grp]
            _, vjp = jax.vjp(lambda q, k, v, b, gc, gr, *st, inv=inv: _gdn_group(q, k, v, b, gc, gr, st, inv)[:2],
                             *_group_operands(grp, q_ref, k_ref, v_ref, b_blk, gc_blk, gr_blk),
                             *[st_ref[h] for h in heads])
            d_out = jnp.concatenate([do_ref[:, h * HEAD:(h + 1) * HEAD] for h in heads], axis=0)
            res.append(vjp((d_out, tuple(dold[h] for h in heads))))
        lane = lax.broadcasted_iota(jnp.int32, (CHUNK, HEAD), 1)
        db_all = jnp.zeros((CHUNK, HEAD), F32)
        dgc_all = jnp.zeros((CHUNK, HEAD), F32)
        for grp, (dq, dk, dv, db, dgc, dgr, *dst) in enumerate(res):
            dgr_ref[grp] = dgr
            for hh in range(GROUP):
                h = grp * GROUP + hh
                cs, rows = slice(h * HEAD, (h + 1) * HEAD), slice(hh * CHUNK, (hh + 1) * CHUNK)
                dq_ref[:, cs] = dq[rows]
                dk_ref[:, cs] = dk[rows]
                dv_ref[:, cs] = dv[rows]
                dstate[h] = dst[hh]
                db_all = jnp.where(lane == h, db[rows], db_all)
                dgc_all = jnp.where(lane == h, dgc[rows], dgc_all)
        db_ref[...] = db_all
        dgc_ref[...] = dgc_all

    blk = pl.BlockSpec((CHUNK, D), lambda n: (at(n), 0))
    gblk = pl.BlockSpec((CHUNK, HEAD), lambda n: (at(n), 0))
    return pl.pallas_call(
        body, name=name, grid=(nc,), in_specs=in_specs,
        out_specs=[blk, blk, blk, gblk, gblk, pl.BlockSpec((None, N_GROUPS, 1, GROWS), lambda n: (at(n), 0, 0, 0))],
        out_shape=[jax.ShapeDtypeStruct((s, D), F32)] * 3 + [jax.ShapeDtypeStruct((s, HEAD), F32)] * 2
        + [jax.ShapeDtypeStruct((nc, N_GROUPS, 1, GROWS), F32)],
        scratch_shapes=[pltpu.VMEM((HEADS, HEAD, HEAD), F32)],
        compiler_params=_params(("arbitrary",)),
    )(qkv, qkv, qkv, beta, gcum, grow, states, invs, do)


ATT_TILE = 512
ATT_SCALE = QK_HEAD ** -0.5


def _att_mask(t):
    qpos = lax.broadcasted_iota(jnp.int32, (t, t), 0)
    kpos = lax.broadcasted_iota(jnp.int32, (t, t), 1)
    return (kpos // CHUNK) <= (qpos // CHUNK)


ATT_STRIP = 32


def _att_strip_mask(r, t):
    kpos = lax.broadcasted_iota(jnp.int32, (ATT_STRIP, t), 1)
    return (kpos // CHUNK) <= (r * ATT_STRIP) // CHUNK


def _att_pairs(nb, by_query):
    if by_query:
        pairs = [(i, j) for i in range(nb) for j in range(i + 1)]
    else:
        pairs = [(j, i) for j in range(nb) for i in range(j, nb)]
    return jnp.array([a for a, _ in pairs], jnp.int32), jnp.array([b for _, b in pairs], jnp.int32)


def _attn_fwd(q, k, v, name):
    s = q.shape[0]
    t = min(ATT_TILE, s)
    nb = s // t
    ii, jj = _att_pairs(nb, by_query=True)

    def body(ii_ref, jj_ref, q_ref, k_ref, v_ref, o_ref, lse_ref, m_s, l_s, acc):
        step = pl.program_id(1)
        i, j = ii_ref[step], jj_ref[step]

        @pl.when(j == 0)
        def _():
            m_s[...] = jnp.full_like(m_s, -jnp.inf)
            l_s[...] = jnp.zeros_like(l_s)
            acc[...] = jnp.zeros_like(acc)

        sc = lax.dot_general(q_ref[...], k_ref[...], _NT, preferred_element_type=F32) * ATT_SCALE
        sc = lax.cond(i == j, lambda u: jnp.where(_att_mask(t), u, -jnp.inf), lambda u: u, sc)
        m_new = jnp.maximum(m_s[...], jnp.max(sc, axis=-1, keepdims=True))
        alpha = jnp.exp(m_s[...] - m_new)
        p = jnp.exp(sc - m_new)
        l_s[...] = alpha * l_s[...] + jnp.sum(p, axis=-1, keepdims=True)
        acc[...] = alpha * acc[...] + jnp.dot(p.astype(BF16), v_ref[...], preferred_element_type=F32)
        m_s[...] = m_new

        @pl.when(j == i)
        def _():
            o_ref[...] = acc[...] / l_s[...]
            lse_ref[...] = m_s[...] + jnp.log(l_s[...])

    grid_spec = pltpu.PrefetchScalarGridSpec(
        num_scalar_prefetch=2, grid=(HEADS, len(ii)),
        in_specs=[pl.BlockSpec((t, HEAD_PAD), lambda h, n, ir, jr: (ir[n], h)),
                  pl.BlockSpec((t, HEAD_PAD), lambda h, n, ir, jr: (jr[n], h)),
                  pl.BlockSpec((t, HEAD), lambda h, n, ir, jr: (jr[n], h))],
        out_specs=[pl.BlockSpec((t, HEAD), lambda h, n, ir, jr: (ir[n], h)),
                   pl.BlockSpec((None, t, 1), lambda h, n, ir, jr: (h, ir[n], 0))],
        scratch_shapes=[pltpu.VMEM((t, 1), F32), pltpu.VMEM((t, 1), F32), pltpu.VMEM((t, HEAD), F32)])
    return pl.pallas_call(
        body, name=name, grid_spec=grid_spec,
        out_shape=[jax.ShapeDtypeStruct((s, HEADS * HEAD), F32), jax.ShapeDtypeStruct((HEADS, s, 1), F32)],
        compiler_params=_params(("parallel", "arbitrary")),
    )(ii, jj, q, k, v)


def _attn_bwd(q, k, v, do, o, lse, name, dkv_sum=None):
    s = q.shape[0]
    t = min(ATT_TILE, s)
    nb = s // t
    jj, ii = _att_pairs(nb, by_query=False)
    nsum = 2 if dkv_sum else 0

    def body(jj_ref, ii_ref, q_ref, k_ref, v_ref, do_ref, o_ref, lse_ref, *rest):
        dq_ref, dk_ref, dv_ref, dk_acc, dv_acc, sc_s, dp_s, p_s, ds_s, dl_s = rest[nsum:]
        step = pl.program_id(1)
        i, j = ii_ref[step], jj_ref[step]

        @pl.when(step == 0)
        def _():
            dq_ref[...] = jnp.zeros_like(dq_ref)

        @pl.when(i == j)
        def _():
            dk_acc[...] = jnp.zeros_like(dk_acc)
            dv_acc[...] = jnp.zeros_like(dv_acc)

        do_f = do_ref[...]
        dob = do_f.astype(BF16)
        dl_s[...] = jnp.sum(do_f * o_ref[...], axis=-1, keepdims=True)
        sc_s[...] = lax.dot_general(q_ref[...], k_ref[...], _NT, preferred_element_type=F32)
        dp_s[...] = lax.dot_general(dob, v_ref[...], _NT, preferred_element_type=F32)

        def softmax_strips(diagonal):
            for r in range(t // ATT_STRIP):
                rows = slice(r * ATT_STRIP, (r + 1) * ATT_STRIP)
                p = jnp.exp(sc_s[rows, :] * ATT_SCALE - lse_ref[rows, :])
                if diagonal:
                    p = jnp.where(_att_strip_mask(r, t), p, 0.0)
                p_s[rows, :] = p.astype(BF16)
                ds_s[rows, :] = (p * (dp_s[rows, :] - dl_s[rows, :]) * ATT_SCALE).astype(BF16)

        pl.when(i == j)(functools.partial(softmax_strips, True))
        pl.when(i != j)(functools.partial(softmax_strips, False))
        ds = ds_s[...]
        dv_acc[...] += lax.dot_general(p_s[...], dob, _TN, preferred_element_type=F32)
        dk_acc[...] += lax.dot_general(ds, q_ref[...], _TN, preferred_element_type=F32)
        rows = pl.ds(pl.multiple_of(i * t, t), t)
        dq_ref[rows, :] += jnp.dot(ds, k_ref[...], preferred_element_type=F32)

        @pl.when(i == nb - 1)
        def _():
            dk_ref[...] = dk_acc[...] + rest[0][...] if nsum else dk_acc[...]
            dv_ref[...] = dv_acc[...] + rest[1][...] if nsum else dv_acc[...]

    dk_blk = pl.BlockSpec((t, HEAD_PAD), lambda h, n, jr, ir: (jr[n], h))
    dv_blk = pl.BlockSpec((t, HEAD), lambda h, n, jr, ir: (jr[n], h))
    grid_spec = pltpu.PrefetchScalarGridSpec(
        num_scalar_prefetch=2, grid=(HEADS, len(jj)),
        in_specs=[pl.BlockSpec((t, HEAD_PAD), lambda h, n, jr, ir: (ir[n], h)),
                  pl.BlockSpec((t, HEAD_PAD), lambda h, n, jr, ir: (jr[n], h)),
                  pl.BlockSpec((t, HEAD), lambda h, n, jr, ir: (jr[n], h)),
                  pl.BlockSpec((t, HEAD), lambda h, n, jr, ir: (ir[n], h)),
                  pl.BlockSpec((t, HEAD), lambda h, n, jr, ir: (ir[n], h)),
                  pl.BlockSpec((None, t, 1), lambda h, n, jr, ir: (h, ir[n], 0))] + [dk_blk, dv_blk][:nsum],
        out_specs=[pl.BlockSpec((s, HEAD_PAD), lambda h, n, jr, ir: (0, h)), dk_blk, dv_blk],
        scratch_shapes=[pltpu.VMEM((t, HEAD_PAD), F32), pltpu.VMEM((t, HEAD), F32), pltpu.VMEM((t, t), F32),
                        pltpu.VMEM((t, t), F32), pltpu.VMEM((t, t), BF16), pltpu.VMEM((t, t), BF16),
                        pltpu.VMEM((t, 1), F32)])
    return pl.pallas_call(
        body, name=name, grid_spec=grid_spec,
        out_shape=[jax.ShapeDtypeStruct((s, HEADS * HEAD_PAD), F32)] * 2 + [jax.ShapeDtypeStruct((s, HEADS * HEAD), F32)],
        compiler_params=_params(("parallel", "arbitrary")),
    )(jj, ii, q, k, v, do, o, lse, *(dkv_sum or ()))


def _rope_tables(positions):
    half = ROPE // 2
    inv_freq = ROPE_BASE ** (-jnp.arange(half, dtype=F32) / half)
    ang = positions.astype(F32)[:, None] * inv_freq
    cos, sin = jnp.cos(ang), jnp.sin(ang)
    return jnp.concatenate([cos] * 4, axis=1), jnp.concatenate([-sin, sin] * 2, axis=1)


def _loss_and_grad(y, target, name):
    s = y.shape[0]
    ts = min(ROW_TILE, s)

    def body(y_ref, t_ref, dy_ref, l_ref):
        e = y_ref[...] - t_ref[...]
        dy_ref[...] = e * (1.0 / D)
        part = jnp.sum(jnp.sum(e * e, axis=-1, keepdims=True) * (0.5 / D), axis=0, keepdims=True)
        part = part * jnp.ones((1, 128), F32)

        @pl.when(pl.program_id(0) == 0)
        def _():
            l_ref[...] = part

        @pl.when(pl.program_id(0) > 0)
        def _():
            l_ref[...] += part

    return pl.pallas_call(
        body, name=name, grid=(s // ts,),
        in_specs=[pl.BlockSpec((ts, D), lambda i: (i, 0))] * 2,
        out_specs=[pl.BlockSpec((ts, D), lambda i: (i, 0)), pl.BlockSpec((1, 128), lambda i: (0, 0))],
        out_shape=[jax.ShapeDtypeStruct((s, D), F32), jax.ShapeDtypeStruct((1, 128), F32)],
        compiler_params=_params(("arbitrary",)),
    )(y, target)


ANY = pl.BlockSpec(memory_space=pl.ANY)


def _all_gather(shard, name):
    def body(x_ref, out_ref, send_sems, recv_sems, local_sem):
        x, y, c = lax.axis_index("x"), lax.axis_index("y"), lax.axis_index("c")
        me, sibling = (x, y, c), (x, y, 1 - c)
        chips = [(1 - x, y), (x, 1 - y), (1 - x, 1 - y)]

        def rows(px, py, pc):
            return out_ref.at[4 * px + 2 * py + pc]

        def copy(k, block, to, src=None):
            return pltpu.make_async_remote_copy(
                src_ref=rows(*block) if src is None else src, dst_ref=rows(*block),
                send_sem=send_sems.at[k], recv_sem=recv_sems.at[k], device_id=to, device_id_type=MESH)

        mine = pltpu.make_async_copy(x_ref, rows(*me), local_sem)
        mine.start()
        first = [copy(0, me, sibling, src=x_ref)]
        first += [copy(1 + j, me, (*chip, c), src=x_ref) for j, chip in enumerate(chips)]
        for cp in first:
            cp.start()
        passed = [copy(4 + j, (*chip, c), sibling) for j, chip in enumerate(chips)]
        for j, chip in enumerate(chips):
            copy(1 + j, (*chip, c), me).wait_recv()
            passed[j].start()
        copy(0, sibling, me).wait_recv()
        for j, chip in enumerate(chips):
            copy(4 + j, (*chip, 1 - c), me).wait_recv()
        for cp in first + passed:
            cp.wait_send()
        mine.wait()

    return pl.pallas_call(
        body, name=name, out_shape=jax.ShapeDtypeStruct((N_DEV,) + shard.shape, shard.dtype),
        in_specs=[ANY], out_specs=ANY,
        scratch_shapes=[pltpu.SemaphoreType.DMA((7,)), pltpu.SemaphoreType.DMA((7,)), pltpu.SemaphoreType.DMA],
    )(shard)


def _exchange(blocks, name):
    def body(x_ref, out_ref, send_sems, recv_sems, local_sem):
        x, y, c = lax.axis_index("x"), lax.axis_index("y"), lax.axis_index("c")
        me = 4 * x + 2 * y + c
        mine = pltpu.make_async_copy(x_ref.at[me], out_ref.at[me], local_sem)
        mine.start()
        copies = []
        for k in range(1, N_DEV):
            px = 1 - x if k & 4 else x
            py = 1 - y if k & 2 else y
            pc = 1 - c if k & 1 else c
            peer = 4 * px + 2 * py + pc
            cp = pltpu.make_async_remote_copy(
                src_ref=x_ref.at[peer], dst_ref=out_ref.at[me], send_sem=send_sems.at[k - 1],
                recv_sem=recv_sems.at[k - 1], device_id=(px, py, pc), device_id_type=MESH)
            cp.start()
            copies.append((cp, pltpu.make_async_remote_copy(
                src_ref=x_ref.at[peer], dst_ref=out_ref.at[peer], send_sem=send_sems.at[k - 1],
                recv_sem=recv_sems.at[k - 1], device_id=(px, py, pc), device_id_type=MESH)))
        for cp, landing in copies:
            landing.wait_recv()
        for cp, landing in copies:
            cp.wait_send()
        mine.wait()

    return pl.pallas_call(
        body, name=name, out_shape=jax.ShapeDtypeStruct(blocks.shape, blocks.dtype),
        in_specs=[ANY], out_specs=ANY,
        scratch_shapes=[pltpu.SemaphoreType.DMA((7,)), pltpu.SemaphoreType.DMA((7,)), pltpu.SemaphoreType.DMA],
    )(blocks)


HBM = pl.BlockSpec(memory_space=pltpu.HBM)
SEM = pl.BlockSpec(memory_space=pltpu.SEMAPHORE)
EFFECT = pltpu.SideEffectType.DATAFLOW_SIDE_EFFECTING


def _peers():
    x, y, c = lax.axis_index("x"), lax.axis_index("y"), lax.axis_index("c")
    peers = []
    for k in range(1, N_DEV):
        px = 1 - x if k & 4 else x
        py = 1 - y if k & 2 else y
        pc = 1 - c if k & 1 else c
        peers.append(((px, py, pc), 4 * px + 2 * py + pc))
    return 4 * x + 2 * y + c, peers


def _send_start(srcs, name, gather):
    n = len(srcs)
    lands = [((N_DEV,) + s.shape) if gather else s.shape for s in srcs]

    def body(*refs):
        src_refs, land_refs = refs[:n], refs[n:2 * n]
        send_sems, recv_sems, token = refs[2 * n], refs[2 * n + 1], refs[-1]
        me, peers = _peers()
        for i in range(n):
            for k, (dev, idx) in enumerate(peers):
                pltpu.make_async_remote_copy(
                    src_ref=src_refs[i] if gather else src_refs[i].at[idx], dst_ref=land_refs[i].at[me],
                    send_sem=send_sems.at[7 * i + k], recv_sem=recv_sems.at[7 * i + k], device_id=dev,
                    device_id_type=MESH).start()
        token[...] = jnp.zeros_like(token)

    res = pl.pallas_call(
        body, name=name,
        out_shape=(pltpu.SemaphoreType.DMA((7 * n,)), pltpu.SemaphoreType.DMA((7 * n,)),
                   *[pltpu.HBM(s.shape, s.dtype) for s in srcs],
                   *[pltpu.HBM(shape, s.dtype) for shape, s in zip(lands, srcs)],
                   jax.ShapeDtypeStruct((8, 128), F32)),
        in_specs=(HBM,) * (2 * n), out_specs=(SEM, SEM) + (HBM,) * (2 * n) + (pl.BlockSpec(memory_space=pltpu.VMEM),),
        input_output_aliases={i: 2 + i for i in range(2 * n)},
        compiler_params=pltpu.CompilerParams(has_side_effects=EFFECT),
    )(*[pltpu.with_memory_space_constraint(s, pltpu.HBM) for s in srcs],
      *[pltpu.with_memory_space_constraint(lax.empty(shape, s.dtype), pltpu.HBM) for shape, s in zip(lands, srcs)])
    return dict(sems=res[:2], srcs=res[2:2 + n], lands=res[2 + n:2 + 2 * n], token=res[-1])


def _send_wait(handle, after, name, gather):
    n = len(handle["srcs"])

    def body(*refs):
        src_refs, land_refs = refs[:n], refs[n:2 * n]
        send_sems, recv_sems = refs[2 * n], refs[2 * n + 1]
        me, peers = _peers()
        for i in range(n):
            for k, (dev, idx) in enumerate(peers):
                cp = pltpu.make_async_remote_copy(
                    src_ref=src_refs[i] if gather else src_refs[i].at[idx], dst_ref=land_refs[i].at[idx],
                    send_sem=send_sems.at[7 * i + k], recv_sem=recv_sems.at[7 * i + k], device_id=dev,
                    device_id_type=MESH)
                cp.wait_send()
                cp.wait_recv()

    both = list(handle["srcs"]) + list(handle["lands"])
    res = pl.pallas_call(
        body, name=name, out_shape=tuple(pltpu.HBM(t.shape, t.dtype) for t in both),
        in_specs=(HBM,) * (2 * n) + (SEM, SEM, pl.BlockSpec(memory_space=pl.ANY)), out_specs=(HBM,) * (2 * n),
        input_output_aliases={i: i for i in range(2 * n)},
        compiler_params=pltpu.CompilerParams(has_side_effects=EFFECT),
    )(*both, *handle["sems"], after)
    return res[:n], res[n:]


def _adamw(parts, w, m, v, name, tr=128):
    pieces = len(parts)
    n, r, wd = parts[0].shape
    tr = next((t for t in (tr, 64, 32, 16) if r % t == 0), r)
    nrt = r // tr

    def body(*refs):
        w_ref, m_ref, v_ref, g_ref, d_ref, nm_ref, nv_ref = refs[pieces:]

        def update(p_ref):
            g = p_ref[0].astype(F32)
            for k in range(1, n):
                g = g + p_ref[k].astype(F32)
            m_new = B1 * m_ref[...] + (1.0 - B1) * g
            v_new = B2 * v_ref[...] + (1.0 - B2) * (g * g)
            m_hat = m_new / (1.0 - B1 ** STEP)
            v_hat = v_new / (1.0 - B2 ** STEP)
            g_ref[...] = g
            d_ref[...] = -LR * (m_hat / (jnp.sqrt(v_hat) + ADAM_EPS) + WD * w_ref[...])
            nm_ref[...] = m_new
            nv_ref[...] = v_new

        for p in range(pieces):
            pl.when(pl.program_id(0) == p)(functools.partial(update, refs[p]))

    part_spec = lambda p: pl.BlockSpec((n, tr, wd), lambda l, i: (0, jnp.clip(i + (l - p) * nrt, 0, nrt - 1), 0))
    blk = pl.BlockSpec((tr, wd), lambda l, i: (l * nrt + i, 0))
    return pl.pallas_call(
        body, name=name, grid=(pieces, nrt),
        in_specs=[part_spec(p) for p in range(pieces)] + [blk, blk, blk],
        out_specs=[blk] * 4, out_shape=[jax.ShapeDtypeStruct((pieces * r, wd), F32)] * 4,
        compiler_params=_params(("arbitrary", "arbitrary")),
    )(*parts, w, m, v)


def _outer8(ct, dm, name):
    k, n = ct.shape[0], dm.shape[1]

    def body(c_ref, d_ref, o_ref):
        cv, dv = c_ref[...], d_ref[...]
        acc = cv[:, 0:1] * dv[0:1, :]
        for s in range(1, N_DEV):
            acc = acc + cv[:, s:s + 1] * dv[s:s + 1, :]
        o_ref[...] = acc

    tk = 256
    return pl.pallas_call(
        body, name=name, grid=(k // tk,),
        in_specs=[pl.BlockSpec((tk, N_DEV), lambda i: (i, 0)), pl.BlockSpec((N_DEV, n), lambda i: (0, 0))],
        out_specs=pl.BlockSpec((tk, n), lambda i: (i, 0)), out_shape=jax.ShapeDtypeStruct((k, n), F32),
        compiler_params=_params(("parallel",)),
    )(ct, dm)


FULL = (0, D)
C128 = (0, 128)
HEAD_NOPE = [(h * HEAD_PAD, NOPE) for h in range(HEADS)]
HEAD_ROPE = [(h * HEAD_PAD + NOPE, 128) for h in range(HEADS)]
HEAD_ALL = [(h * HEAD_PAD, HEAD_PAD) for h in range(HEADS)]
HEAD_V = [(h * HEAD, HEAD) for h in range(HEADS)]


def _modulate(x, p, ties=()):
    return _rowwise_fwd(_modulate_fn, [(x, FULL)], [p["gain"], p["scale"], p["shift"]], [(D, BF16, FULL)], "modulate",
                        ties=ties)[0]


def _residual_bwd(y, gm, dxn):
    return _rowwise_bwd(_gate_only_fn, [(y, FULL)], [gm], [(D, F32, FULL)], [dxn], [(0, FULL)], [(D, BF16)],
                        "residual_bwd")


def _modulate_bwd(x, p, dh, dx_in, prev=None):
    pars = [p["gain"], p["scale"], p["shift"]]
    if prev is None:
        return list(_rowwise_bwd(_modulate_fn, [(x, FULL)], pars, [(D, BF16, FULL)], [dh], [(0, FULL)], [(D, F32)],
                                 "modulate_bwd", add={0: dx_in})) + [None]
    s = x.shape[0]
    ts = min(2 * ROW_TILE, s)

    def body(x_ref, g_ref, sc_ref, sh_ref, dh_ref, din_ref, y_ref, gm_ref, dx_ref, dy_ref, dg_ref, dsc_ref, dsh_ref,
             dgm_ref):
        _, vjp = jax.vjp(lambda *t: _modulate_fn(0, *t)[0], x_ref[...], g_ref[...], sc_ref[...], sh_ref[...])
        dxm, dg, dsc, dsh = vjp(dh_ref[...])
        dx = dxm + din_ref[...]
        dx_ref[...] = dx
        dy_ref[...] = (gm_ref[...] * dx).astype(BF16)
        sums = (dg, dsc, dsh, jnp.sum(dx * y_ref[...], axis=0, keepdims=True))
        first = pl.program_id(0) == 0
        for ref, val in zip((dg_ref, dsc_ref, dsh_ref, dgm_ref), sums):
            @pl.when(first)
            def _(ref=ref, val=val):
                ref[...] = val

            @pl.when(jnp.logical_not(first))
            def _(ref=ref, val=val):
                ref[...] += val

    blk = pl.BlockSpec((ts, D), lambda i: (i, 0))
    vec = pl.BlockSpec((1, D), lambda i: (0, 0))
    dx, dy, dg, dsc, dsh, dgm = pl.pallas_call(
        body, name="modulate_bwd_chain", grid=(s // ts,),
        in_specs=[blk, vec, vec, vec, blk, blk, blk, vec], out_specs=[blk, blk, vec, vec, vec, vec],
        out_shape=[jax.ShapeDtypeStruct((s, D), F32), jax.ShapeDtypeStruct((s, D), BF16)]
        + [jax.ShapeDtypeStruct((1, D), F32)] * 4,
        compiler_params=_params(("arbitrary",)),
    )(x, *pars, dh, dx_in, prev[0], prev[1])
    return [dx, dg, dsc, dsh, (dy, dgm)]


def _out_proj(a, w, x, p, nxt, name, **kw):
    res = _matmul([(a, w)], "nn", name, out_dtype=BF16, resid=(x, p["gm"]) + tuple(nxt or ()), **kw)
    return res[1], res[0], (res[2] if nxt else None)


def _ffn_fwd(x, p, ties=(), h=None, nxt=None):
    if h is None:
        h, ties = _modulate(x, p, ties), ()
    act_dgate, act_dup, act = _ffn_in(h, p["w_in"], "ffn_in", ties=ties)
    if callable(p["wo"]):
        p["wo"] = p["wo"](act)
    res = _ffn_out(act, p["wo"], (x, p["gm"]) + tuple(nxt or ()), "ffn_out")
    return res[1], dict(x=x, h=h, act_dgate=act_dgate, act_dup=act_dup, act=act, y=res[0]), (res[2] if nxt else None)


def _ffn_bwd(t, p, dxn, res=None, prev=None, ties=(), early=None):
    dy, dgm = res or _residual_bwd(t["y"], p["gm"], dxn)
    dgate, dup = _ffn_bwd_act(dy, p["wo"], t["act_dgate"], t["act_dup"], "ffn_bwd_act", ties=ties)
    dwo = _ffn_dwo(t["act"], dy, "ffn_dwo", ties=ties)
    dh = _ffn_dh(dgate, dup, p["w_in"], "ffn_dh", ties=early(dwo) if early else ())
    dwi = _ffn_dwi(t["h"], dgate, dup, "ffn_dwi")
    dx, dgain, dscale, dshift, res_prev = _modulate_bwd(t["x"], p, dh, dxn, prev)
    return dx, dict(gain=dgain, scale=dscale, shift=dshift, gm=dgm, w_in=dwi, wo=dwo), res_prev


def _pad128(t):
    return jnp.pad(t, ((0, 0), (0, 128 - t.shape[1])))


def _gdn_fwd(x, p, ties=(), h=None, nxt=None):
    s = x.shape[0]
    if h is None:
        h, ties = _modulate(x, p, ties), ()
    pm = _mm(h, p["w_main"], "nn", "gdn_proj", ties=ties)
    tail = _mm(h, p["w_tail"], "nn", "gdn_proj_tail", ties=ties)
    qkv = _gdn_conv_fwd(pm, p["conv_w"], "gdn_conv")
    beta, gcum = _rowwise_fwd(_gdn_gates_fn, [(tail, C128), (tail, (128, 128))], [p["a_log"], p["dt_bias"]],
                              [(128, F32, C128)] * 2, "gdn_gates")
    grow = gcum[:, :HEADS].reshape(s // CHUNK, CHUNK, N_GROUPS, GROUP).transpose(0, 2, 3, 1)
    grow = grow.reshape(s // CHUNK, N_GROUPS, 1, GROWS)
    o, states, invs = _gdn_scan_fwd(qkv, beta, gcum, grow, "gdn_scan")
    on, = _rowwise_fwd(_gdn_outnorm_fn, [(o, HEAD_V), (pm, [(3 * D + h_ * HEAD, HEAD) for h_ in range(HEADS)])],
                       [p["norm_g"]], [(D, BF16, HEAD_V)], "gdn_outnorm", groups=HEADS)
    xn, y, hn = _out_proj(on, p["w_out"], x, p, nxt, "mix_out", tm=512)
    t = dict(x=x, h=h, pm=pm, tail=tail, qkv=qkv, beta=beta, gcum=gcum, grow=grow, o=o, states=states, invs=invs,
             on=on, y=y)
    return xn, t, hn


def _gdn_bwd(t, p, dxn, res=None, prev=None, ties=()):
    s = dxn.shape[0]
    zc = [(3 * D + h_ * HEAD, HEAD) for h_ in range(HEADS)]
    dy, dgm = res or _residual_bwd(t["y"], p["gm"], dxn)
    dw_out = _mm(t["on"], dy, "tn", "mix_dwo", ties=ties)
    don = _mm(dy, p["w_out"], "nt", "mix_dout", ties=ties)
    do, dz, dnorm_g = _rowwise_bwd(_gdn_outnorm_fn, [(t["o"], HEAD_V), (t["pm"], zc)], [p["norm_g"]],
                                   [(D, BF16, HEAD_V)], [don], [(0, HEAD_V), (1, HEAD_V)], [(D, F32), (D, BF16)],
                                   "gdn_outnorm_bwd", groups=HEADS)
    dq, dk, dv, dbeta, dg, dgr = _gdn_scan_bwd(t["qkv"], t["beta"], t["gcum"], t["grow"], t["states"], t["invs"], do,
                                               "gdn_scan_bwd")
    dg = dg + _pad128(dgr.reshape(s // CHUNK, N_GROUPS, GROUP, CHUNK).transpose(0, 3, 1, 2).reshape(s, HEADS))
    dtail, da_log, ddt = _rowwise_bwd(_gdn_gates_fn, [(t["tail"], C128), (t["tail"], (128, 128))],
                                      [p["a_log"], p["dt_bias"]], [(128, F32, C128)] * 2, [dbeta, dg],
                                      [(0, C128), (0, (128, 128))], [(256, F32)], "gdn_gates_bwd")
    dxs, dcw = [], []
    for part, d in enumerate((dq, dk, dv)):
        dx_, dw_ = _gdn_conv_bwd(t["pm"], p["conv_w"], d, part, "gdn_conv_bwd")
        dxs.append(dx_)
        dcw.append(dw_)
    pieces = dxs + [dz]
    dh = _matmul([(d, p["w_main"]) for d in pieces] + [(dtail, p["w_tail"])], "nt", "gdn_dh",
                 boffs=[0, D, 2 * D, 3 * D, 0], tk=512)
    dw_main = [_mm(t["h"], d, "tn", "gdn_dwi") for d in pieces]
    dw_tail = _mm(t["h"], dtail, "tn", "gdn_dwi_tail")
    dx, dgain, dscale, dshift, res_prev = _modulate_bwd(t["x"], p, dh, dxn, prev)
    return dx, dict(gain=dgain, scale=dscale, shift=dshift, gm=dgm, w_main=jnp.concatenate(dw_main, axis=1),
                    w_tail=dw_tail, conv_w=jnp.concatenate(dcw, axis=1), a_log=da_log, dt_bias=ddt,
                    norm_g=dnorm_g, w_out=dw_out), res_prev


def _q_rows(q2, cosf, sins):
    return [(q2, HEAD_NOPE), (q2, HEAD_ROPE), (cosf, C128), (sins, C128)]


def _mla_fwd(x, p, kv, ties=(), h=None, nxt=None):
    if h is None:
        h, ties = _modulate(x, p, ties), ()
    cq = _mm(h, p["w_dq"], "nn", "mla_dq", ties=ties)
    cqn, = _rowwise_fwd(_rms_fn, [(cq, (0, Q_LORA))], [p["q_lora_g"]], [(Q_LORA, BF16, (0, Q_LORA))], "mla_qlora_norm")
    q2 = _mm(cqn, p["w_uq"], "nn", "mla_uq")
    qn, = _rowwise_fwd(_q_norm_rope_fn, _q_rows(q2, kv["cosf"], kv["sins"]), [p["q_gn"], p["q_gr"]],
                       [(HEADS * HEAD_PAD, BF16, HEAD_ALL)], "mla_q_norm", groups=HEADS)
    o, lse = _attn_fwd(qn, kv["kn"], kv["vb"], "mla_attn")
    xn, y, hn = _out_proj(o, p["w_out"], x, p, nxt, "mix_out", tm=512)
    return xn, dict(x=x, h=h, cq=cq, cqn=cqn, q2=q2, qn=qn, o=o, lse=lse, y=y), hn


def _mla_bwd(t, p, kv, dxn, res=None, prev=None, ties=(), dkv_sum=None):
    dy, dgm = res or _residual_bwd(t["y"], p["gm"], dxn)
    dw_out = _mm(t["o"], dy, "tn", "mix_dwo", ties=ties)
    do = _mm(dy, p["w_out"], "nt", "mix_dout", ties=ties)
    dq, dk, dv = _attn_bwd(t["qn"], kv["kn"], kv["vb"], do, t["o"], t["lse"], "mla_attn_bwd", dkv_sum)
    dq2, dq_gn, dq_gr = _rowwise_bwd(_q_norm_rope_fn, _q_rows(t["q2"], kv["cosf"], kv["sins"]), [p["q_gn"], p["q_gr"]],
                                     [(HEADS * HEAD_PAD, BF16, HEAD_ALL)], [dq],
                                     [(0, HEAD_NOPE), (0, HEAD_ROPE), None, None], [(HEADS * HEAD_PAD, BF16)],
                                     "mla_q_norm_bwd", groups=HEADS)
    dw_uq = _mm(t["cqn"], dq2, "tn", "mla_dwuq")
    dcqn = _mm(dq2, p["w_uq"], "nt", "mla_dcq")
    dcq, dq_lora_g = _rowwise_bwd(_rms_fn, [(t["cq"], (0, Q_LORA))], [p["q_lora_g"]], [(Q_LORA, BF16, (0, Q_LORA))],
                                  [dcqn], [(0, (0, Q_LORA))], [(Q_LORA, BF16)], "mla_qlora_norm_bwd")
    dw_dq = _mm(t["h"], dcq, "tn", "mla_dwdq")
    dh = _mm(dcq, p["w_dq"], "nt", "mla_dh")
    dx, dgain, dscale, dshift, res_prev = _modulate_bwd(t["x"], p, dh, dxn, prev)
    grads = dict(gain=dgain, scale=dscale, shift=dshift, gm=dgm, w_dq=dw_dq, q_lora_g=dq_lora_g, w_uq=dw_uq,
                 q_gn=dq_gn, q_gr=dq_gr, w_out=dw_out)
    return dx, grads, res_prev, dk, dv


def _k_rows(kvp, ckv, cosf, sins):
    return [(kvp, HEAD_NOPE), (kvp, HEAD_ROPE), (ckv, (KV_LORA, 128)), (cosf, C128), (sins, C128)]


def _kv_fwd(x, p, cosf, sins):
    h = _modulate(x, p)
    ckv = _mm(h, p["w_dkv"], "nn", "kv_down")
    lat, = _rowwise_fwd(_rms_fn, [(ckv, (0, KV_LORA))], [p["kv_g"]], [(KV_LORA, BF16, (0, KV_LORA))], "kv_norm")
    kvp = _mm(lat, p["w_ukv"], "nn", "kv_up")
    kn, vb = _rowwise_fwd(_k_norm_rope_fn, _k_rows(kvp, ckv, cosf, sins), [p["k_gn"], p["k_gr"]],
                          [(HEADS * HEAD_PAD, BF16, HEAD_ALL), (HEADS * HEAD, BF16, HEAD_V)], "kv_k_norm",
                          groups=HEADS)
    return dict(x=x, h=h, ckv=ckv, lat=lat, kvp=kvp, kn=kn, vb=vb, cosf=cosf, sins=sins)


def _kv_bwd(t, p, dk, dv, dx_in, prev):
    dkvp, drope, dk_gn, dk_gr = _rowwise_bwd(
        _k_norm_rope_fn, _k_rows(t["kvp"], t["ckv"], t["cosf"], t["sins"]), [p["k_gn"], p["k_gr"]],
        [(HEADS * HEAD_PAD, BF16, HEAD_ALL), (HEADS * HEAD, BF16, HEAD_V)], [dk, dv],
        [(0, HEAD_NOPE), (0, HEAD_ROPE), (1, C128), None, None], [(HEADS * HEAD_PAD, BF16), (128, F32)],
        "kv_k_norm_bwd", groups=HEADS)
    dw_ukv = _mm(t["lat"], dkvp, "tn", "kv_dwukv")
    dlat = _mm(dkvp, p["w_ukv"], "nt", "kv_dlat")
    dckv, dkv_g = _rowwise_bwd(_rms_fn, [(t["ckv"], (0, KV_LORA))], [p["kv_g"]], [(KV_LORA, BF16, (0, KV_LORA))],
                               [dlat], [(0, (0, KV_LORA))], [(KV_LORA, F32)], "kv_norm_bwd")
    dw_dkv = jnp.concatenate([_mm(t["h"], dckv, "tn", "kv_dwdkv"), _mm(t["h"], drope, "tn", "kv_dwdkv_rope")], axis=1)
    dh = _matmul([(dckv, p["w_dkv"]), (drope, p["w_dkv"])], "nt", "kv_dh", boffs=[0, KV_LORA])
    dx, dgain, dscale, dshift, res_prev = _modulate_bwd(t["x"], p, dh, dx_in, prev)
    return dx, dict(gain=dgain, scale=dscale, shift=dshift, w_dkv=dw_dkv, kv_g=dkv_g, w_ukv=dw_ukv, k_gn=dk_gn,
                    k_gr=dk_gr), res_prev


WEIGHTS = ["ada_w", "ada_b", "norm_g", "ffn_w_in", "ffn_w_out", "gdn_w_in", "gdn_conv_w", "gdn_a_log", "gdn_dt_bias",
           "gdn_norm_g", "gdn_w_out", "kv_ada_w", "kv_ada_b", "kv_norm_g", "mla_w_dkv", "mla_kv_norm_g", "mla_w_ukv",
           "mla_k_norm_g", "mla_w_dq", "mla_q_lora_norm_g", "mla_w_uq", "mla_q_norm_g", "mla_w_out"]
SMALL = [("ada_b", 4 * N_MOD * D), ("kv_ada_b", 2 * D), ("norm_g", DEPTH * 3 * D), ("gdn_conv_w", N_A * CONV_K * 3 * D),
         ("gdn_a_log", N_A * HEADS), ("gdn_dt_bias", N_A * HEADS), ("gdn_norm_g", N_A * HEAD), ("kv_norm_g", D),
         ("mla_kv_norm_g", KV_LORA), ("mla_k_norm_g", QK_HEAD), ("mla_q_lora_norm_g", 2 * Q_LORA),
         ("mla_q_norm_g", 2 * QK_HEAD)]
SMALL_REPLICATED = [n for n, _ in SMALL if n not in ("norm_g", "gdn_conv_w")]


def _silu_fn(g, t):
    return (_silu(t),)


def _dup_rope(t):
    return jnp.concatenate([t[..., :NOPE], t[..., NOPE:], t[..., NOPE:]], axis=-1)


def _fold_rope(t):
    return jnp.concatenate([t[..., :NOPE], t[..., NOPE:QK_HEAD] + t[..., QK_HEAD:]], axis=-1)


def _pack(pieces, rows):
    flat = jnp.concatenate([p.reshape(-1).astype(F32) for p in pieces])
    return jnp.pad(flat, (0, rows * 128 - flat.shape[0])).reshape(rows, 128)


def _step(a):
    me = 4 * lax.axis_index("x") + 2 * lax.axis_index("y") + lax.axis_index("c")
    x = a["x"][0]
    cosf, sins = _rope_tables(a["positions"][0])

    n_gdn = (4 * D + 2 * HEADS) // N_DEV
    AHEAD = 2

    stages = [(l, part) for l in range(DEPTH) for part in range(3)]

    def stage_shards(l, part):
        if part != 1:
            sh = {"ffn_w_in": a["ffn_w_in"][l, part // 2], "ffn_w_out": a["ffn_w_out"][l, part // 2]}
            if part == 2 and l == N_A - 1:
                sh.update(mla_w_dkv=a["mla_w_dkv"], mla_w_ukv=a["mla_w_ukv"])
            return sh
        if l < N_A:
            return {"gdn_w_in": a["gdn_w_in"][l], "gdn_w_out": a["gdn_w_out"][l]}
        j = l - N_A
        return {"mla_w_dq": a["mla_w_dq"][j], "mla_w_uq": a["mla_w_uq"][j], "mla_w_out": a["mla_w_out"][j]}

    def zero_of(t):
        return jnp.minimum(jnp.abs(t[(0,) * t.ndim].astype(F32)), 0.0)

    def start_stage(l, part, tie):
        sh = stage_shards(l, part)
        return list(sh), _send_start([(w + tie).astype(BF16) for w in sh.values()], f"fetch_start_{l}_{part}", gather=True)

    def finish_stage(l, part, names, handle, after):
        srcs, lands = _send_wait(handle, after, f"fetch_wait_{l}_{part}", gather=True)
        return {n: lax.dynamic_update_slice(land, src[None], (me, 0, 0)) for n, src, land in zip(names, srcs, lands)}

    n_cw, n_ng = N_A * CONV_K * 3 * HEAD, DEPTH * 3 * HEAD
    small_all = _all_gather(_pack([a["gdn_conv_w"], a["norm_g"], a["c"]], 44), "gather_small").reshape(N_DEV, -1)
    conv_w = small_all[:, :n_cw].reshape(N_DEV, N_A, CONV_K, 3 * HEAD).transpose(1, 2, 0, 3).reshape(N_A, CONV_K, 3 * D)
    norm_g = small_all[:, n_cw:n_cw + n_ng].reshape(N_DEV, DEPTH, 3, HEAD).transpose(1, 2, 0, 3).reshape(DEPTH, 3, D)
    c_all = small_all[:, n_cw + n_ng:n_cw + n_ng + D]

    c_act, = _rowwise_fwd(_silu_fn, [(c_all, FULL)], [], [(D, F32, FULL)], "c_act")
    n_ada = N_MOD * D // N_DEV
    parts = [_mm(c_act, a["ada_w"][l], "nn", "mod_proj") for l in range(DEPTH)]
    parts.append(_mm(c_act, a["kv_ada_w"], "nn", "mod_proj_kv"))
    mod_recv = _exchange(jnp.concatenate(parts, axis=1)[:, None, :], "exchange_mod")[:, 0]
    mod = mod_recv[:, :DEPTH * n_ada].reshape(N_DEV, DEPTH, n_ada).transpose(1, 0, 2).reshape(DEPTH, N_MOD * D)
    mod = (mod + a["ada_b"]).reshape(DEPTH, N_MOD, D)
    kvmod = mod_recv[:, DEPTH * n_ada:].reshape(2 * D) + a["kv_ada_b"]

    def row(v):
        return v[None]

    def ffn_params(l, i, w):
        k = 0 if i == 0 else 6
        return dict(gain=row(norm_g[l, 0 if i == 0 else 2]), shift=row(mod[l, k]), scale=row(mod[l, k + 1]),
                    gm=0.5 * row(mod[l, k + 2]), w_in=w["ffn_w_in"],
                    wo=(lambda act: w["ffn_w_out"](act).reshape(D_FF, D)) if callable(w["ffn_w_out"])
                    else w["ffn_w_out"].reshape(D_FF, D))

    def gdn_params(l, w):
        w_in = w["gdn_w_in"].transpose(1, 0, 2).reshape(D, 4 * D + 2 * HEADS)
        pad = lambda t: jnp.pad(t, ((0, 0), (0, 128 - HEADS)))
        return dict(gain=row(norm_g[l, 1]), shift=row(mod[l, 3]), scale=row(mod[l, 4]), gm=row(mod[l, 5]),
                    w_main=w_in[:, :4 * D],
                    w_tail=jnp.concatenate([pad(w_in[:, 4 * D:4 * D + HEADS]), pad(w_in[:, 4 * D + HEADS:])], axis=1),
                    conv_w=conv_w[l], a_log=_pad128(row(a["gdn_a_log"][l])), dt_bias=_pad128(row(a["gdn_dt_bias"][l])),
                    norm_g=row(a["gdn_norm_g"][l]), w_out=w["gdn_w_out"].reshape(D, D))

    def mla_params(l, w):
        j = l - N_A
        uq = w["mla_w_uq"].transpose(1, 0, 2)
        qg = _dup_rope(a["mla_q_norm_g"][j])
        return dict(gain=row(norm_g[l, 1]), shift=row(mod[l, 3]), scale=row(mod[l, 4]), gm=row(mod[l, 5]),
                    w_dq=w["mla_w_dq"].reshape(D, Q_LORA), q_lora_g=row(a["mla_q_lora_norm_g"][j]),
                    w_uq=_dup_rope(uq).reshape(Q_LORA, HEADS * HEAD_PAD), q_gn=row(qg[:NOPE]), q_gr=row(qg[NOPE:]),
                    w_out=w["mla_w_out"].reshape(D, D))

    def kv_params(w):
        w_dkv = w["mla_w_dkv"].reshape(D, KV_LORA + ROPE)
        kg = _dup_rope(a["mla_k_norm_g"])
        return dict(gain=row(a["kv_norm_g"]), shift=row(kvmod[:D]), scale=row(kvmod[D:]),
                    w_dkv=jnp.concatenate([w_dkv, w_dkv[:, KV_LORA:]], axis=1), kv_g=row(a["mla_kv_norm_g"]),
                    w_ukv=w["mla_w_ukv"].transpose(1, 0, 2).reshape(KV_LORA, HEADS * 2 * HEAD), k_gn=row(kg[:NOPE]),
                    k_gr=row(kg[NOPE:]))

    tapes, kv, kv_p, h = [[] for _ in range(DEPTH)], None, None, None
    def landed(handle, after, name):
        srcs, lands = _send_wait(handle, after, name, gather=True)
        return lax.dynamic_update_slice(lands[0], srcs[0][None], (me, 0, 0))

    sh0 = stage_shards(0, 0)
    start_in = _send_start([(sh0["ffn_w_in"] + zero_of(mod)).astype(BF16)], "fetch_start_0_0", gather=True)
    start_out = _send_start([(sh0["ffn_w_out"] + start_in["token"][0, 0]).astype(BF16)], "fetch_start_0_0_out",
                            gather=True)
    pending = []
    for l, part in stages[1:1 + AHEAD]:
        pending.append(start_stage(l, part, (pending[-1][1] if pending else start_out)["token"][0, 0]))
    first = {"ffn_w_in": landed(start_in, mod, "fetch_wait_0_0"),
             "ffn_w_out": lambda act: landed(start_out, act, "fetch_wait_0_0_out")}
    for n, (l, part) in enumerate(stages):
        if n == 0:
            w, ties = first, (start_out["token"],) + tuple(h["token"] for _, h in pending)
        else:
            names, handle = pending.pop(0)
            w = finish_stage(l, part, names, handle, x)
            ties = ()
            if n + AHEAD < len(stages):
                pending.append(start_stage(*stages[n + AHEAD], zero_of(w[names[0]])))
                ties = (pending[-1][1]["token"],)
        nxt = None
        if n + 1 < len(stages):
            l2, part2 = stages[n + 1]
            k2 = 3 * part2
            nxt = (row(norm_g[l2, part2]), row(mod[l2, k2 + 1]), row(mod[l2, k2]))
        if part != 1:
            p = ffn_params(l, part // 2, w)
            x, t, h = _ffn_fwd(x, p, ties, h, nxt)
        else:
            p = gdn_params(l, w) if l < N_A else mla_params(l, w)
            x, t, h = _gdn_fwd(x, p, ties, h, nxt) if l < N_A else _mla_fwd(x, p, kv, ties, h, nxt)
        tapes[l] += [p, t]
        if part == 2 and l == N_A - 1:
            kv_p = kv_params(w)
            kv = _kv_fwd(x, kv_p, cosf, sins)
    dx, loss_blk = _loss_and_grad(x, a["loss_target"][0], "loss")
    loss = lax.psum(loss_blk[0, 0], ("x", "y", "c"))

    def by_cols(g, n):
        return g.reshape(g.shape[0], -1, n).transpose(1, 0, 2)

    def ffn_blocks(g):
        return {"ffn_w_in": g["w_in"], "ffn_w_out": g["wo"].reshape(N_DEV, D_FF // N_DEV, D)}

    def mixer_blocks(l, g):
        if l < N_A:
            full = jnp.concatenate([g["w_main"], g["w_tail"][:, :HEADS], g["w_tail"][:, 128:128 + HEADS]], axis=1)
            return {"gdn_w_in": by_cols(full, n_gdn), "gdn_w_out": g["w_out"].reshape(N_DEV, D // N_DEV, D)}
        return {"mla_w_dq": g["w_dq"].reshape(N_DEV, D // N_DEV, Q_LORA),
                "mla_w_uq": _fold_rope(g["w_uq"].reshape(Q_LORA, HEADS, HEAD_PAD)).transpose(1, 0, 2),
                "mla_w_out": g["w_out"].reshape(N_DEV, D // N_DEV, D)}

    sent = []

    def send(key, blocks, tie=0.0):
        handle = _send_start([(b + tie).astype(BF16) for b in blocks.values()], "grad_start_" + "_".join(map(str, key)),
                             gather=False)
        sent.append((key, list(blocks), handle))
        return (handle["token"],)

    grads = [None] * DEPTH
    dk_sum = dv_sum = kv_grads = res = None
    ties = ()
    for l in reversed(range(DEPTH)):
        p1, t1, pm_, tm_, p2, t2 = tapes[l]
        if l == N_A - 1:
            dx, kv_grads, res = _kv_bwd(kv, kv_p, dk_sum, dv_sum, dx, (t2["y"], p2["gm"]))
            d_dkv = kv_grads["w_dkv"]
            ties += send((l, 3), {
                "mla_w_dkv": jnp.concatenate(
                    [d_dkv[:, :KV_LORA], d_dkv[:, KV_LORA:KV_LORA + ROPE] + d_dkv[:, KV_LORA + ROPE:]],
                    axis=1).reshape(N_DEV, D // N_DEV, KV_LORA + ROPE),
                "mla_w_ukv": by_cols(kv_grads["w_ukv"], 2 * HEAD)})
        dx, g2, res = _ffn_bwd(t2, p2, dx, res, (tm_["y"], pm_["gm"]), ties)
        ties = send((l, 2), ffn_blocks(g2))
        if l < N_A:
            dx, gm_, res = _gdn_bwd(tm_, pm_, dx, res, (t1["y"], p1["gm"]), ties)
        else:
            dx, gm_, res, dk_sum, dv_sum = _mla_bwd(tm_, pm_, kv, dx, res, (t1["y"], p1["gm"]), ties,
                                                    None if dk_sum is None else (dk_sum, dv_sum))
        ties = send((l, 1), mixer_blocks(l, gm_))
        prev = (tapes[l - 1][5]["y"], tapes[l - 1][4]["gm"]) if l > 0 and l != N_A else None
        if l > 0:
            dx, g1, res = _ffn_bwd(t1, p1, dx, res, prev, ties)
            ties = send((l, 0), ffn_blocks(g1))
        else:
            dx, g1, res = _ffn_bwd(t1, p1, dx, res, prev, ties, early=lambda dwo: send(
                (0, 0, "out"), {"ffn_w_out": dwo.reshape(N_DEV, D_FF // N_DEV, D)}))
        grads[l] = (g1, gm_, g2)

    out = {}
    def dmod(l):
        g1, gm_, g2 = grads[l]
        return jnp.concatenate([g1["shift"], g1["scale"], 0.5 * g1["gm"], gm_["shift"], gm_["scale"], gm_["gm"],
                                g2["shift"], g2["scale"], 0.5 * g2["gm"]], axis=1)

    gdn = [grads[l][1] for l in range(N_A)]
    mla = [grads[l][1] for l in range(N_A, DEPTH)]
    small = {
        "ada_b": jnp.concatenate([dmod(l) for l in range(DEPTH)], axis=0),
        "kv_ada_b": jnp.concatenate([kv_grads["shift"], kv_grads["scale"]], axis=1),
        "norm_g": jnp.stack([jnp.concatenate([grads[l][0]["gain"], grads[l][1]["gain"], grads[l][2]["gain"]], axis=0)
                             for l in range(DEPTH)]),
        "gdn_conv_w": jnp.stack([g["conv_w"] for g in gdn]),
        "gdn_a_log": jnp.stack([g["a_log"][0, :HEADS] for g in gdn]),
        "gdn_dt_bias": jnp.stack([g["dt_bias"][0, :HEADS] for g in gdn]),
        "gdn_norm_g": jnp.stack([g["norm_g"][0] for g in gdn]),
        "kv_norm_g": kv_grads["gain"],
        "mla_kv_norm_g": kv_grads["kv_g"],
        "mla_k_norm_g": _fold_rope(jnp.concatenate([kv_grads["k_gn"], kv_grads["k_gr"]], axis=1)),
        "mla_q_lora_norm_g": jnp.stack([g["q_lora_g"][0] for g in mla]),
        "mla_q_norm_g": jnp.stack([_fold_rope(jnp.concatenate([g["q_gn"], g["q_gr"]], axis=1))[0] for g in mla]),
    }
    rows = 616
    assert sum(n for _, n in SMALL) <= rows * 128 and all(small[n].size == k for n, k in SMALL)
    small_recv = _all_gather(_pack([small[n] for n, _ in SMALL], rows), "gather_small_grads")
    small_recv = small_recv + send((0, 0), {"ffn_w_in": grads[0][0]["w_in"]}, zero_of(small_recv))[0][0, 0]
    zero = lambda n, k: jnp.zeros((k,), F32)
    packed = {pre: _pack([a[pre + n] if n in SMALL_REPLICATED else zero(n, k) for n, k in SMALL], rows)
              for pre in ("", "m_", "v_")}
    res = _adamw([small_recv], packed[""], packed["m_"], packed["v_"], "adamw_small")
    offs = {}
    o = 0
    for n, k in SMALL:
        offs[n] = o
        o += k
    for n, k in SMALL:
        if n in SMALL_REPLICATED:
            out[n] = [r.reshape(-1)[offs[n]:offs[n] + k] for r in res]
    gsum = res[0].reshape(-1)
    g_norm = lax.dynamic_slice_in_dim(gsum[offs["norm_g"]:offs["norm_g"] + DEPTH * 3 * D].reshape(DEPTH * 3, D),
                                      me * HEAD, HEAD, axis=1)
    g_conv = lax.dynamic_slice_in_dim(
        gsum[offs["gdn_conv_w"]:offs["gdn_conv_w"] + N_A * CONV_K * 3 * D].reshape(N_A * CONV_K, 3 * D),
        me * 3 * HEAD, 3 * HEAD, axis=1)
    res2 = _adamw([_pack([g_norm, g_conv], 36)[None]], *[_pack([a[pre + "norm_g"], a[pre + "gdn_conv_w"]], 36)
                                                      for pre in ("", "m_", "v_")], "adamw_small")
    out["norm_g"] = [r.reshape(-1)[:n_ng] for r in res2]
    out["gdn_conv_w"] = [r.reshape(-1)[n_ng:n_ng + n_cw] for r in res2]

    c_act_t = c_act.T
    all_small = small_recv.reshape(N_DEV, -1)
    dmod_all = all_small[:, :DEPTH * N_MOD * D].reshape(N_DEV, DEPTH, N_MOD * D)
    dmod_mine = lax.dynamic_slice_in_dim(dmod_all, me * n_ada, n_ada, axis=2)
    g_ada = [_outer8(c_act_t, dmod_mine[:, l], "ada_grad")[None] for l in range(DEPTH)]
    out["ada_w"] = _adamw(g_ada, *[a[pre + "ada_w"].reshape(DEPTH * D, n_ada) for pre in ("", "m_", "v_")], "adamw")
    dkv_all = all_small[:, offs["kv_ada_b"]:offs["kv_ada_b"] + 2 * D]
    g_kv = _outer8(c_act_t, lax.dynamic_slice_in_dim(dkv_all, me * (2 * D // N_DEV), 2 * D // N_DEV, axis=1), "ada_grad")
    out["kv_ada_w"] = _adamw([g_kv[None]], *[a[pre + "kv_ada_w"] for pre in ("", "m_", "v_")], "adamw")

    expected = {}
    for _, names, _ in sent:
        for name in names:
            expected[name] = expected.get(name, 0) + 1
    pieces = {}
    for key, names, handle in sent:
        srcs, lands = _send_wait(handle, out["kv_ada_w"][0], "grad_wait_" + "_".join(map(str, key)), gather=False)
        for name, src, land in zip(names, srcs, lands):
            own = lax.dynamic_slice_in_dim(src, me, 1, axis=0)
            parts = pieces.setdefault(name, [])
            parts.append((key, lax.dynamic_update_slice(land, own, (me, 0, 0))))
            if len(parts) == expected[name]:
                wide = a[name].shape[-1]
                out[name] = _adamw([p for _, p in sorted(parts, key=lambda kp: kp[0])], a[name].reshape(-1, wide),
                                   a["m_" + name].reshape(-1, wide), a["v_" + name].reshape(-1, wide), "adamw")

    result = [loss, dx[None]]
    for k in range(4):
        result += [out[n][k].reshape(a[n].shape) for n in WEIGHTS]
    return tuple(result)


def kernel(x, c, positions, ada_w, ada_b, norm_g, ffn_w_in, ffn_w_out, gdn_w_in, gdn_conv_w, gdn_a_log, gdn_dt_bias, gdn_norm_g, gdn_w_out, kv_ada_w, kv_ada_b, kv_norm_g, mla_w_dkv, mla_kv_norm_g, mla_w_ukv, mla_k_norm_g, mla_w_dq, mla_q_lora_norm_g, mla_w_uq, mla_q_norm_g, mla_w_out, loss_target, m_ada_w, m_ada_b, m_norm_g, m_ffn_w_in, m_ffn_w_out, m_gdn_w_in, m_gdn_conv_w, m_gdn_a_log, m_gdn_dt_bias, m_gdn_norm_g, m_gdn_w_out, m_kv_ada_w, m_kv_ada_b, m_kv_norm_g, m_mla_w_dkv, m_mla_kv_norm_g, m_mla_w_ukv, m_mla_k_norm_g, m_mla_w_dq, m_mla_q_lora_norm_g, m_mla_w_uq, m_mla_q_norm_g, m_mla_w_out, v_ada_w, v_ada_b, v_norm_g, v_ffn_w_in, v_ffn_w_out, v_gdn_w_in, v_gdn_conv_w, v_gdn_a_log, v_gdn_dt_bias, v_gdn_norm_g, v_gdn_w_out, v_kv_ada_w, v_kv_ada_b, v_kv_norm_g, v_mla_w_dkv, v_mla_kv_norm_g, v_mla_w_ukv, v_mla_k_norm_g, v_mla_w_dq, v_mla_q_lora_norm_g, v_mla_w_uq, v_mla_q_norm_g, v_mla_w_out):
    return _step(dict(locals()))
```

```python
import functools

import jax
import jax.numpy as jnp
from jax import lax
from jax.experimental import pallas as pl
from jax.experimental.pallas import tpu as pltpu

F32 = jnp.float32
BF16 = jnp.bfloat16

N_DEV = 8
D = 1024
D_FF = 2816
DEPTH = 4
N_A = 2
N_MOD = 9
HEADS = 8
HEAD = 128
CHUNK = 64
CONV_K = 4
KV_LORA = 256
Q_LORA = 384
NOPE = 128
ROPE = 64
QK_HEAD = NOPE + ROPE
HEAD_PAD = 256
ROPE_BASE = 10000.0
EPS = 1e-6
LR, B1, B2, ADAM_EPS, WD, STEP = 0.001, 0.9, 0.999, 1e-08, 0.01, 10

VMEM_LIMIT = 48 * 1024 * 1024
ROW_TILE = 256
MESH = pl.DeviceIdType.MESH

_NN = (((1,), (0,)), ((), ()))
_NT = (((1,), (1,)), ((), ()))
_TN = (((0,), (0,)), ((), ()))
_DIMS = {"nn": _NN, "nt": _NT, "tn": _TN}


def _params(dims=None):
    return pltpu.CompilerParams(dimension_semantics=dims, vmem_limit_bytes=VMEM_LIMIT)


def _tile(n, target):
    for t in range(target - target % 128, 0, -128):
        if n % t == 0:
            return t
    return n


_TIE_SPEC1 = pl.BlockSpec((8, 128), lambda i: (0, 0))
_TIE_SPEC2 = pl.BlockSpec((8, 128), lambda i, j: (0, 0))
_TIE_SPEC3 = pl.BlockSpec((8, 128), lambda i, j, k: (0, 0))


def _matmul(pairs, form, name, out_dtype=F32, tm=1408, tn=1408, tk=1408, boffs=None, resid=None, ties=()):
    a0, b0 = pairs[0]
    if form == "nn":
        m, n = a0.shape[0], b0.shape[1]
        ks = [a.shape[1] for a, _ in pairs]
    elif form == "nt":
        m, n = a0.shape[0], b0.shape[0]
        ks = [a.shape[1] for a, _ in pairs]
    else:
        m, n = a0.shape[1], b0.shape[1]
        ks = [a.shape[0] for a, _ in pairs]
    tm, tn = _tile(m, tm), _tile(n, tn)
    tks = [_tile(k, tk) for k in ks]
    boffs = boffs or [0] * len(pairs)
    assert m % tm == 0 and n % tn == 0 and all(o % t == 0 for o, t in zip(boffs, tks)), (name, m, n, ks)
    steps = [k // t for k, t in zip(ks, tks)]
    starts = [sum(steps[:p]) for p in range(len(pairs))]
    nk = sum(steps)

    def kidx(p, k):
        return jnp.clip(k - starts[p], 0, steps[p] - 1)

    in_specs, args = [], []
    for p, (a, b) in enumerate(pairs):
        t = tks[p]
        if form == "tn":
            in_specs.append(pl.BlockSpec((t, tm), lambda i, j, k, p=p: (kidx(p, k), i)))
            in_specs.append(pl.BlockSpec((t, tn), lambda i, j, k, p=p: (kidx(p, k), j)))
        elif form == "nn":
            in_specs.append(pl.BlockSpec((tm, t), lambda i, j, k, p=p: (i, kidx(p, k))))
            in_specs.append(pl.BlockSpec((t, tn), lambda i, j, k, p=p: (kidx(p, k), j)))
        else:
            in_specs.append(pl.BlockSpec((tm, t), lambda i, j, k, p=p: (i, kidx(p, k))))
            in_specs.append(pl.BlockSpec((tn, t), lambda i, j, k, p=p, o=boffs[p] // t: (j, kidx(p, k) + o)))
        args += [a, b]
    dims = _DIMS[form]
    npairs = len(pairs)
    nres = len(resid or ())
    nin = 2 * npairs + len(ties) + nres
    out_blk = pl.BlockSpec((tm, tn), lambda i, j, k: (i, j))
    in_specs += [_TIE_SPEC3] * len(ties)
    args += list(ties)
    if resid:
        assert nres == 2 or (nres == 5 and tn == n)
        in_specs += [out_blk] + [pl.BlockSpec((1, tn), lambda i, j, k: (0, j))] * (nres - 1)
        args += list(resid)

    def body(*refs):
        o_ref = refs[nin]
        k = pl.program_id(2)

        def prod(p):
            return lax.dot_general(refs[2 * p][...].astype(BF16), refs[2 * p + 1][...].astype(BF16), dims,
                                   preferred_element_type=F32)

        def finish(y):
            o_ref[...] = y.astype(o_ref.dtype)
            if resid:
                x_ref, gate_ref = refs[nin - nres], refs[nin - nres + 1]
                xn = x_ref[...] + gate_ref[...] * y
                refs[nin + 1][...] = xn
                if nres == 5:
                    gain, scale, shift = (r[...] for r in refs[nin - 3:nin])
                    refs[nin + 2][...] = _modulate_fn(0, xn, gain, scale, shift)[0].astype(BF16)

        if nk == 1:
            finish(prod(0))
            return
        acc = refs[-1]

        @pl.when(k == 0)
        def _():
            acc[...] = jnp.zeros_like(acc)

        for p in range(npairs):
            @pl.when((k >= starts[p]) & (k < starts[p] + steps[p]))
            def _(p=p):
                acc[...] += prod(p)

        @pl.when(k == nk - 1)
        def _():
            finish(acc[...])

    res = pl.pallas_call(
        body, name=name, grid=(m // tm, n // tn, nk), in_specs=in_specs,
        out_specs=[out_blk] * (2 + (nres == 5)) if resid else out_blk,
        out_shape=([jax.ShapeDtypeStruct((m, n), out_dtype), jax.ShapeDtypeStruct((m, n), F32)]
                   + [jax.ShapeDtypeStruct((m, n), BF16)] * (nres == 5))
        if resid else jax.ShapeDtypeStruct((m, n), out_dtype),
        scratch_shapes=[] if nk == 1 else [pltpu.VMEM((tm, tn), F32)],
        compiler_params=_params(("parallel", "parallel", "arbitrary")),
    )(*args)
    return res


def _mm(a, b, form, name, **kw):
    return _matmul([(a, b)], form, name, **kw)


def _cols(spec, g):
    return spec[g] if isinstance(spec, list) else spec


def _rowwise_fwd(fn, rows, pars, outs, name, groups=1, ts=ROW_TILE, ties=()):
    s = rows[0][0].shape[0]
    ts = min(ts, s)
    assert s % ts == 0
    nr, npar = len(rows), len(pars)

    def body(*refs):
        par_t = [r[...] for r in refs[nr:nr + npar]]
        out_refs = refs[nr + npar + len(ties):]
        for g in range(groups):
            row_t = []
            for r, (_, spec) in zip(refs[:nr], rows):
                c0, w = _cols(spec, g)
                row_t.append(r[:, c0:c0 + w].astype(F32))
            res = fn(g, *row_t, *par_t)
            for o_ref, val, (_, _, spec) in zip(out_refs, res, outs):
                c0, w = _cols(spec, g)
                o_ref[:, c0:c0 + w] = val.astype(o_ref.dtype)

    return pl.pallas_call(
        body, name=name, grid=(s // ts,),
        in_specs=[pl.BlockSpec((ts, a.shape[1]), lambda i: (i, 0)) for a, _ in rows]
        + [pl.BlockSpec(p.shape, lambda i: (0, 0)) for p in pars] + [_TIE_SPEC1] * len(ties),
        out_specs=[pl.BlockSpec((ts, w), lambda i: (i, 0)) for w, _, _ in outs],
        out_shape=[jax.ShapeDtypeStruct((s, w), dt) for w, dt, _ in outs],
        compiler_params=_params(("parallel",)),
    )(*[a for a, _ in rows], *pars, *ties)


def _rowwise_bwd(fn, rows, pars, outs, douts, gmap, gshapes, name, groups=1, add=None, par_grads=True,
                 ts=ROW_TILE):
    s = rows[0][0].shape[0]
    ts = min(ts, s)
    assert s % ts == 0
    nr, npar, nout, ng = len(rows), len(pars), len(outs), len(gshapes)
    add = add or {}
    add_keys = sorted(add)

    def body(*refs):
        row_refs = refs[:nr]
        par_refs = refs[nr:nr + npar]
        dout_refs = refs[nr + npar:nr + npar + nout]
        add_refs = refs[nr + npar + nout:nr + npar + nout + len(add_keys)]
        g_refs = refs[nr + npar + nout + len(add_keys):][:ng]
        pg_refs = refs[nr + npar + nout + len(add_keys) + ng:]
        par_t = [r[...] for r in par_refs]
        par_acc = [None] * npar
        shared_acc = {}
        for g in range(groups):
            row_t = []
            for r, (_, spec) in zip(row_refs, rows):
                c0, w = _cols(spec, g)
                row_t.append(r[:, c0:c0 + w].astype(F32))
            cts = []
            for r, (_, _, spec) in zip(dout_refs, outs):
                c0, w = _cols(spec, g)
                cts.append(r[:, c0:c0 + w].astype(F32))
            _, vjp = jax.vjp(lambda *t, g=g: tuple(fn(g, *t)), *row_t, *par_t)
            grads = vjp(tuple(cts))
            for k in range(nr):
                if gmap[k] is None:
                    continue
                gi, spec = gmap[k]
                if isinstance(spec, list) or groups == 1:
                    c0, w = _cols(spec, g)
                    val = grads[k]
                    if gi in add:
                        val = val + add_refs[add_keys.index(gi)][:, c0:c0 + w].astype(F32)
                    g_refs[gi][:, c0:c0 + w] = val.astype(g_refs[gi].dtype)
                else:
                    shared_acc[k] = grads[k] if k not in shared_acc else shared_acc[k] + grads[k]
            if par_grads:
                for k in range(npar):
                    pg = grads[nr + k]
                    par_acc[k] = pg if par_acc[k] is None else par_acc[k] + pg
        for k, val in shared_acc.items():
            gi, (c0, w) = gmap[k]
            assert gi not in add
            g_refs[gi][:, c0:c0 + w] = val.astype(g_refs[gi].dtype)
        if par_grads:
            first = pl.program_id(0) == 0
            for k in range(npar):
                @pl.when(first)
                def _(k=k):
                    pg_refs[k][...] = par_acc[k]

                @pl.when(jnp.logical_not(first))
                def _(k=k):
                    pg_refs[k][...] += par_acc[k]

    out_specs = [pl.BlockSpec((ts, w), lambda i: (i, 0)) for w, _ in gshapes]
    out_shape = [jax.ShapeDtypeStruct((s, w), dt) for w, dt in gshapes]
    if par_grads:
        out_specs += [pl.BlockSpec(p.shape, lambda i: (0, 0)) for p in pars]
        out_shape += [jax.ShapeDtypeStruct(p.shape, F32) for p in pars]
    return pl.pallas_call(
        body, name=name, grid=(s // ts,),
        in_specs=[pl.BlockSpec((ts, a.shape[1]), lambda i: (i, 0)) for a, _ in rows]
        + [pl.BlockSpec(p.shape, lambda i: (0, 0)) for p in pars]
        + [pl.BlockSpec((ts, a.shape[1]), lambda i: (i, 0)) for a in douts]
        + [pl.BlockSpec((ts, add[k].shape[1]), lambda i: (i, 0)) for k in add_keys],
        out_specs=out_specs, out_shape=out_shape,
        compiler_params=_params(("arbitrary",)),
    )(*[a for a, _ in rows], *pars, *douts, *[add[k] for k in add_keys])


def _sigmoid(x):
    return 1.0 / (1.0 + jnp.exp(-x))


def _silu(x):
    return x * _sigmoid(x)


def _softplus(x):
    return jnp.maximum(x, 0.0) + jnp.log(1.0 + jnp.exp(-jnp.abs(x)))


def _rms(t, g, n=None):
    n = n or t.shape[-1]
    return t * lax.rsqrt(jnp.sum(t * t, axis=-1, keepdims=True) / n + EPS) * g


def _modulate_fn(g, x, gain, scale, shift):
    return (_rms(x, gain) * (1.0 + scale) + shift,)


def _gate_only_fn(g, y, gm):
    return (gm * y,)


def _gdn_gates_fn(g, b_logit, a_logit, a_log, dt_bias):
    gate = -jnp.exp(a_log) * _softplus(a_logit + dt_bias)
    n = gate.shape[0]
    i = lax.broadcasted_iota(jnp.int32, (n, n), 0)
    j = lax.broadcasted_iota(jnp.int32, (n, n), 1)
    tri = (((i // CHUNK) == (j // CHUNK)) & (i >= j)).astype(F32)
    gcum = lax.dot_general(tri, gate, _NN, preferred_element_type=F32, precision=lax.Precision.HIGHEST)
    return _sigmoid(b_logit), gcum


def _gdn_outnorm_fn(g, o, z, gain):
    return (_rms(o, gain) * _silu(z),)


def _rms_fn(g, t, gain):
    return (_rms(t, gain),)


@jax.custom_vjp
def _swap_halves(t):
    return pltpu.roll(t, 32, 1)


_swap_halves.defvjp(lambda t: (pltpu.roll(t, 32, 1), None), lambda _, ct: (pltpu.roll(ct, 96, 1),))


def _head_norm_rope_fn(g, nope, rope, cosf, sins, gain_n, gain_r):
    first = lax.broadcasted_iota(jnp.int32, rope.shape, 1) < ROPE
    ss = jnp.sum(nope * nope, axis=-1, keepdims=True) + jnp.sum(jnp.where(first, rope * rope, 0.0), axis=-1,
                                                                 keepdims=True)
    r = lax.rsqrt(ss / QK_HEAD + EPS)
    tn = nope * r * gain_n
    tr = rope * r * gain_r
    rot = jnp.where(first, tr * cosf + _swap_halves(tr) * sins, 0.0)
    return tn, rot


def _q_norm_rope_fn(g, nope, rope, cosf, sins, gain_n, gain_r):
    tn, rot = _head_norm_rope_fn(g, nope, rope, cosf, sins, gain_n, gain_r)
    return (jnp.concatenate([tn, rot], axis=1),)


def _k_norm_rope_fn(g, nope, val, rope, cosf, sins, gain_n, gain_r):
    tn, rot = _head_norm_rope_fn(g, nope, rope, cosf, sins, gain_n, gain_r)
    return jnp.concatenate([tn, rot], axis=1), val


FF_SH = 2 * D_FF // N_DEV
FF_G = N_DEV // 2


def _ffn_in(h, w_in, name, tm=1024, ties=()):
    s = h.shape[0]
    tm = min(tm, s)

    def body(h_ref, wg_ref, wu_ref, *rest):
        g_ref, u_ref, a_ref = rest[-3:]
        hb = h_ref[...]
        gate = jnp.dot(hb, wg_ref[...], preferred_element_type=F32)
        up = jnp.dot(hb, wu_ref[...], preferred_element_type=F32)
        sg = _sigmoid(gate)
        silu = gate * sg
        g_ref[...] = (up * (sg * (1.0 + gate * (1.0 - sg)))).astype(BF16)
        u_ref[...] = silu.astype(BF16)
        a_ref[...] = (silu * up).astype(BF16)

    spec = pl.BlockSpec((None, tm, FF_SH), lambda j, i: (j, i, 0))
    return pl.pallas_call(
        body, name=name, grid=(FF_G, s // tm),
        in_specs=[pl.BlockSpec((tm, D), lambda j, i: (i, 0)), pl.BlockSpec((None, D, FF_SH), lambda j, i: (j, 0, 0)),
                  pl.BlockSpec((None, D, FF_SH), lambda j, i: (j + FF_G, 0, 0))] + [_TIE_SPEC2] * len(ties),
        out_specs=[spec, spec, spec], out_shape=[jax.ShapeDtypeStruct((FF_G, s, FF_SH), BF16)] * 3,
        compiler_params=_params(("parallel", "parallel")),
    )(h, w_in, w_in, *ties)


def _ffn_out(act, wo, resid, name, tm=512):
    s = act.shape[1]
    tm = min(tm, s)
    nres = len(resid)

    def body(a_ref, b_ref, x_ref, gate_ref, *rest):
        mods, outs = rest[:nres - 2], rest[nres - 2:]
        y = jnp.dot(a_ref[0], b_ref[0:FF_SH, :], preferred_element_type=F32)
        for k in range(1, FF_G):
            y = y + jnp.dot(a_ref[k], b_ref[k * FF_SH:(k + 1) * FF_SH, :], preferred_element_type=F32)
        xn = x_ref[...] + gate_ref[...] * y
        outs[0][...] = y.astype(BF16)
        outs[1][...] = xn
        if mods:
            outs[2][...] = _modulate_fn(0, xn, *[m[...] for m in mods])[0].astype(BF16)

    blk = pl.BlockSpec((tm, D), lambda i: (i, 0))
    vec = pl.BlockSpec((1, D), lambda i: (0, 0))
    return pl.pallas_call(
        body, name=name, grid=(s // tm,),
        in_specs=[pl.BlockSpec((FF_G, tm, FF_SH), lambda i: (0, i, 0)), pl.BlockSpec((D_FF, D), lambda i: (0, 0)),
                  blk] + [vec] * (nres - 1),
        out_specs=[blk] * (2 + (nres == 5)),
        out_shape=[jax.ShapeDtypeStruct((s, D), BF16), jax.ShapeDtypeStruct((s, D), F32)]
        + [jax.ShapeDtypeStruct((s, D), BF16)] * (nres == 5),
        compiler_params=_params(("parallel",)),
    )(act, wo, *resid)


def _ffn_bwd_act(dy, wo, act_dgate, act_dup, name, tm=1024, ties=()):
    s = dy.shape[0]
    tm = min(tm, s)

    def body(dy_ref, wo_ref, g_ref, u_ref, *rest):
        dg_ref, du_ref = rest[-2:]
        dact = lax.dot_general(dy_ref[...], wo_ref[...], _NT, preferred_element_type=F32)
        dg_ref[...] = (dact * g_ref[...].astype(F32)).astype(BF16)
        du_ref[...] = (dact * u_ref[...].astype(F32)).astype(BF16)

    spec = pl.BlockSpec((None, tm, FF_SH), lambda j, i: (j, i, 0))
    return pl.pallas_call(
        body, name=name, grid=(FF_G, s // tm),
        in_specs=[pl.BlockSpec((tm, D), lambda j, i: (i, 0)), pl.BlockSpec((FF_SH, D), lambda j, i: (j, 0)), spec, spec]
        + [_TIE_SPEC2] * len(ties),
        out_specs=[spec, spec], out_shape=[jax.ShapeDtypeStruct((FF_G, s, FF_SH), BF16)] * 2,
        compiler_params=_params(("parallel", "parallel")),
    )(dy, wo, act_dgate, act_dup, *ties)


def _ffn_dwo(act, dy, name, tk=2048, ties=()):
    s = act.shape[1]
    tk = min(tk, s)

    def body(a_ref, b_ref, *rest):
        o_ref, acc = rest[-2:]
        k = pl.program_id(1)

        @pl.when(k == 0)
        def _():
            acc[...] = jnp.zeros_like(acc)

        acc[...] += lax.dot_general(a_ref[...], b_ref[...], _TN, preferred_element_type=F32)

        @pl.when(k == s // tk - 1)
        def _():
            o_ref[...] = acc[...].astype(BF16)

    return pl.pallas_call(
        body, name=name, grid=(FF_G, s // tk),
        in_specs=[pl.BlockSpec((None, tk, FF_SH), lambda j, k: (j, k, 0)), pl.BlockSpec((tk, D), lambda j, k: (k, 0))]
        + [_TIE_SPEC2] * len(ties),
        out_specs=pl.BlockSpec((FF_SH, D), lambda j, k: (j, 0)), out_shape=jax.ShapeDtypeStruct((D_FF, D), BF16),
        scratch_shapes=[pltpu.VMEM((FF_SH, D), F32)], compiler_params=_params(("parallel", "arbitrary")),
    )(act, dy, *ties)


def _ffn_halves(k, gate_ref, up_ref, fn):
    pl.when(k < FF_G)(functools.partial(fn, gate_ref))
    pl.when(k >= FF_G)(functools.partial(fn, up_ref))


def _ffn_dh(dgate, dup, w_in, name, tm=512, ties=()):
    s = dgate.shape[1]
    tm = min(tm, s)

    def body(dg_ref, du_ref, w_ref, *rest):
        acc = lax.dot_general(dg_ref[0], w_ref[0], _NT, preferred_element_type=F32)
        for k in range(1, N_DEV):
            d_ref = dg_ref if k < FF_G else du_ref
            acc = acc + lax.dot_general(d_ref[k % FF_G], w_ref[k], _NT, preferred_element_type=F32)
        rest[-1][...] = acc

    half = pl.BlockSpec((FF_G, tm, FF_SH), lambda i: (0, i, 0))
    return pl.pallas_call(
        body, name=name, grid=(s // tm,),
        in_specs=[half, half, pl.BlockSpec((N_DEV, D, FF_SH), lambda i: (0, 0, 0))] + [_TIE_SPEC1] * len(ties),
        out_specs=pl.BlockSpec((tm, D), lambda i: (i, 0)), out_shape=jax.ShapeDtypeStruct((s, D), F32),
        compiler_params=_params(("parallel",)),
    )(dgate, dup, w_in, *ties)


def _ffn_dwi(h, dgate, dup, name, tk=2048):
    s = h.shape[0]
    tk = min(tk, s)

    def body(h_ref, dg_ref, du_ref, o_ref, acc):
        j, k = pl.program_id(0), pl.program_id(1)

        @pl.when(k == 0)
        def _():
            acc[...] = jnp.zeros_like(acc)

        def add(d_ref):
            acc[...] += lax.dot_general(h_ref[...], d_ref[...], _TN, preferred_element_type=F32)

        _ffn_halves(j, dg_ref, du_ref, add)

        @pl.when(k == s // tk - 1)
        def _():
            o_ref[...] = acc[...].astype(BF16)

    return pl.pallas_call(
        body, name=name, grid=(N_DEV, s // tk),
        in_specs=[pl.BlockSpec((tk, D), lambda j, k: (k, 0)),
                  pl.BlockSpec((None, tk, FF_SH), lambda j, k: (jnp.minimum(j, FF_G - 1), jnp.where(j < FF_G, k, s // tk - 1), 0)),
                  pl.BlockSpec((None, tk, FF_SH), lambda j, k: (jnp.maximum(j - FF_G, 0), jnp.where(j < FF_G, 0, k), 0))],
        out_specs=pl.BlockSpec((None, D, FF_SH), lambda j, k: (j, 0, 0)),
        out_shape=jax.ShapeDtypeStruct((N_DEV, D, FF_SH), BF16),
        scratch_shapes=[pltpu.VMEM((D, FF_SH), F32)], compiler_params=_params(("parallel", "arbitrary")),
    )(h, dgate, dup)


def _shift_down(x, d):
    rows = lax.broadcasted_iota(jnp.int32, x.shape, 0)
    return jnp.where(rows >= d, pltpu.roll(x, d, 0), 0.0)


def _shift_up(x, d):
    n = x.shape[0]
    rows = lax.broadcasted_iota(jnp.int32, x.shape, 0)
    return jnp.where(rows < n - d, pltpu.roll(x, n - d, 0), 0.0)


def _conv_post(pre, is_qk):
    a = _silu(pre)
    l2 = a * lax.rsqrt(jnp.sum(a * a, axis=-1, keepdims=True) + EPS)
    return jnp.where(is_qk, l2, a)


def _conv_taps(x):
    return [_shift_down(x, CONV_K - 1 - j) for j in range(CONV_K - 1)] + [x]


def _conv_pre(x, w, taps=None):
    taps = taps or _conv_taps(x)
    pre = taps[0] * w[0:1, :]
    for j in range(1, CONV_K):
        pre = pre + taps[j] * w[j:j + 1, :]
    return pre


def _gdn_conv_fwd(pm, conv_w, name):
    s = pm.shape[0]
    nblk = 3 * D // HEAD

    def body(x_ref, w_ref, o_ref):
        is_qk = pl.program_id(0) < 2 * HEADS
        o_ref[...] = _conv_post(_conv_pre(x_ref[...], w_ref[...]), is_qk)

    return pl.pallas_call(
        body, name=name, grid=(nblk,),
        in_specs=[pl.BlockSpec((s, HEAD), lambda c: (0, c)), pl.BlockSpec((CONV_K, HEAD), lambda c: (0, c))],
        out_specs=pl.BlockSpec((s, HEAD), lambda c: (0, c)),
        out_shape=jax.ShapeDtypeStruct((s, 3 * D), F32), compiler_params=_params(("parallel",)),
    )(pm, conv_w)


def _gdn_conv_bwd(pm, conv_w, dout, part, name):
    s = pm.shape[0]
    off = part * HEADS

    def body(x_ref, w_ref, d_ref, dx_ref, dw_ref):
        x, w = x_ref[...], w_ref[...]
        taps = _conv_taps(x)
        _, vjp = jax.vjp(lambda p: _conv_post(p, part < 2), _conv_pre(x, w, taps))
        dpre, = vjp(d_ref[...])
        dx = dpre * w[CONV_K - 1:CONV_K, :]
        for j in range(CONV_K - 1):
            dx = dx + _shift_up(dpre, CONV_K - 1 - j) * w[j:j + 1, :]
        dx_ref[...] = dx.astype(BF16)
        dw_ref[...] = jnp.concatenate([jnp.sum(dpre * tap, axis=0, keepdims=True) for tap in taps], axis=0)

    return pl.pallas_call(
        body, name=name, grid=(HEADS,),
        in_specs=[pl.BlockSpec((s, HEAD), lambda c: (0, c + off)), pl.BlockSpec((CONV_K, HEAD), lambda c: (0, c + off)),
                  pl.BlockSpec((s, HEAD), lambda c: (0, c))],
        out_specs=[pl.BlockSpec((s, HEAD), lambda c: (0, c)), pl.BlockSpec((CONV_K, HEAD), lambda c: (0, c))],
        out_shape=[jax.ShapeDtypeStruct((s, D), BF16), jax.ShapeDtypeStruct((CONV_K, D), F32)],
        compiler_params=_params(("parallel",)),
    )(pm, conv_w, dout)


def _dot3(a, b, dims=_NN):
    ah, bh = a.astype(BF16), b.astype(BF16)
    al, bl = (a - ah.astype(F32)).astype(BF16), (b - bh.astype(F32)).astype(BF16)
    d = lambda u, v: lax.dot_general(u, v, dims, preferred_element_type=F32)
    return d(ah, bh) + (d(ah, bl) + d(al, bh))


def _make_dot(hi):
    def raw(a, b, dims):
        if hi:
            return _dot3(a, b, dims)
        return lax.dot_general(a.astype(BF16), b.astype(BF16), dims, preferred_element_type=F32)

    @functools.partial(jax.custom_vjp, nondiff_argnums=(2,))
    def dot(a, b, form):
        return raw(a, b, _DIMS[form])

    def fwd(a, b, form):
        return raw(a, b, _DIMS[form]), (a, b)

    def bwd(form, res, ct):
        a, b = res
        if form == "nn":
            return raw(ct, b, _NT), raw(a, ct, _TN)
        if form == "nt":
            return raw(ct, b, _NN), raw(ct, a, _TN)
        return raw(b, ct, _NT), raw(a, ct, _NN)

    dot.defvjp(fwd, bwd)
    return dot


_dot = _make_dot(False)
_dot_hi = _make_dot(True)


def _tri_inv_raw(low):
    n = low.shape[0]
    i = lax.broadcasted_iota(jnp.int32, (n, n), 0)
    j = lax.broadcasted_iota(jnp.int32, (n, n), 1)
    eye = (i == j).astype(F32)
    hdot = _dot3
    same16 = (i // 16) == (j // 16)
    neg = jnp.where(same16, -low, 0.0)
    inv = eye + neg
    power = neg
    for _ in range(3):
        power = hdot(power, power)
        inv = hdot(inv, eye + power)
    for blk in (32, 64):
        off = jnp.where(((i // blk) == (j // blk)) & ((i // (blk // 2)) != (j // (blk // 2))), low, 0.0)
        inv = inv - hdot(inv, hdot(off, inv))
    return inv


@jax.custom_vjp
def _tri_inv(low):
    return _tri_inv_raw(low)


def _tri_inv_fwd(low):
    inv = _tri_inv_raw(low)
    return inv, inv


def _tri_inv_bwd(inv, ct):
    return (-_dot3(_dot3(inv, ct, _TN), inv, _NT),)


_tri_inv.defvjp(_tri_inv_fwd, _tri_inv_bwd)


@jax.custom_vjp
def _tri_inv_given(low, inv):
    return inv


_tri_inv_given.defvjp(lambda low, inv: (inv, inv),
                      lambda inv, ct: (_tri_inv_bwd(inv, ct)[0], jnp.zeros_like(inv)))

GROUP = 4
N_GROUPS = HEADS // GROUP
GROWS = GROUP * CHUNK


def _gdn_group(q, k, v, beta, gc, gr, states, inv=None):
    n = q.shape[0]
    i = lax.broadcasted_iota(jnp.int32, (n, n), 0)
    j = lax.broadcasted_iota(jnp.int32, (n, n), 1)
    same = (i // CHUNK) == (j // CHUNK)
    incl, strict = same & (i >= j), same & (i > j)
    qs = q * (HEAD ** -0.5)
    decay = jnp.where(incl, jnp.exp(jnp.where(incl, gc - gr, 0.0)), 0.0)
    kb = k * beta
    eg = jnp.exp(gc)
    prod = _dot(jnp.concatenate([kb, qs], axis=0), k, "nt")
    low = jnp.where(strict, prod[:n] * decay, 0.0)
    attn = jnp.where(incl, prod[n:] * decay, 0.0)
    inv = _tri_inv(low) if inv is None else _tri_inv_given(low, inv)
    sol = _dot_hi(inv, jnp.concatenate([v * beta, kb * eg], axis=1), "nn")
    u, w, qg = sol[:, :HEAD], sol[:, HEAD:], qs * eg
    last = lax.broadcasted_iota(jnp.int32, (CHUNK, 1), 0) == CHUNK - 1
    v_new, o_state, carry = [], [], []
    for h, state in enumerate(states):
        rows = slice(h * CHUNK, (h + 1) * CHUNK)
        ws = _dot(jnp.concatenate([w[rows], qg[rows]], axis=0), state, "nn")
        v_new.append(u[rows] - ws[:CHUNK])
        o_state.append(ws[CHUNK:])
        g_last = jnp.sum(jnp.where(last, gc[rows], 0.0), axis=0, keepdims=True)
        carry.append((g_last, k[rows] * jnp.exp(g_last - gc[rows])))
    o = jnp.concatenate(o_state, axis=0) + _dot(attn, jnp.concatenate(v_new, axis=0), "nn")
    new = tuple(state * jnp.exp(g_last) + _dot(k_dec, vn, "tn")
                for state, (g_last, k_dec), vn in zip(states, carry, v_new))
    return o, new, inv


def _gdn_specs(s, rev):
    nc = s // CHUNK
    at = (lambda n: nc - 1 - n) if rev else (lambda n: n)
    return nc, at, [
        pl.BlockSpec((CHUNK, D), lambda n: (at(n), 0)), pl.BlockSpec((CHUNK, D), lambda n: (at(n), 1)),
        pl.BlockSpec((CHUNK, D), lambda n: (at(n), 2)), pl.BlockSpec((CHUNK, HEAD), lambda n: (at(n), 0)),
        pl.BlockSpec((CHUNK, HEAD), lambda n: (at(n), 0)),
        pl.BlockSpec((None, N_GROUPS, 1, GROWS), lambda n: (at(n), 0, 0, 0))]


def _group_operands(grp, q_ref, k_ref, v_ref, b_blk, gc_blk, gr_blk):
    heads = range(grp * GROUP, (grp + 1) * GROUP)
    stack = lambda ref: jnp.concatenate([ref[:, h * HEAD:(h + 1) * HEAD] for h in heads], axis=0)
    col = lambda blk: jnp.concatenate([blk[:, h:h + 1] for h in heads], axis=0)
    return stack(q_ref), stack(k_ref), stack(v_ref), col(b_blk), col(gc_blk), gr_blk[grp]


def _gdn_scan_fwd(qkv, beta, gcum, grow, name):
    s = qkv.shape[0]
    nc, _, in_specs = _gdn_specs(s, rev=False)

    def body(q_ref, k_ref, v_ref, b_ref, gc_ref, gr_ref, o_ref, st_ref, inv_ref, state):
        @pl.when(pl.program_id(0) == 0)
        def _():
            state[...] = jnp.zeros_like(state)

        b_blk, gc_blk, gr_blk = b_ref[...], gc_ref[...], gr_ref[...]
        old = [state[h] for h in range(HEADS)]
        res = [_gdn_group(*_group_operands(grp, q_ref, k_ref, v_ref, b_blk, gc_blk, gr_blk),
                          old[grp * GROUP:(grp + 1) * GROUP]) for grp in range(N_GROUPS)]
        for grp, (o, new, inv) in enumerate(res):
            inv_ref[grp] = inv
            for hh in range(GROUP):
                h = grp * GROUP + hh
                st_ref[h] = old[h]
                o_ref[:, h * HEAD:(h + 1) * HEAD] = o[hh * CHUNK:(hh + 1) * CHUNK]
                state[h] = new[hh]

    return pl.pallas_call(
        body, name=name, grid=(nc,), in_specs=in_specs,
        out_specs=[pl.BlockSpec((CHUNK, D), lambda n: (n, 0)),
                   pl.BlockSpec((None, HEADS, HEAD, HEAD), lambda n: (n, 0, 0, 0)),
                   pl.BlockSpec((None, N_GROUPS, GROWS, GROWS), lambda n: (n, 0, 0, 0))],
        out_shape=[jax.ShapeDtypeStruct((s, D), F32), jax.ShapeDtypeStruct((nc, HEADS, HEAD, HEAD), F32),
                   jax.ShapeDtypeStruct((nc, N_GROUPS, GROWS, GROWS), F32)],
        scratch_shapes=[pltpu.VMEM((HEADS, HEAD, HEAD), F32)],
        compiler_params=_params(("arbitrary",)),
    )(qkv, qkv, qkv, beta, gcum, grow)


def _gdn_scan_bwd(qkv, beta, gcum, grow, states, invs, do, name):
    s = qkv.shape[0]
    nc, at, in_specs = _gdn_specs(s, rev=True)
    in_specs += [pl.BlockSpec((None, HEADS, HEAD, HEAD), lambda n: (at(n), 0, 0, 0)),
                 pl.BlockSpec((None, N_GROUPS, GROWS, GROWS), lambda n: (at(n), 0, 0, 0)),
                 pl.BlockSpec((CHUNK, D), lambda n: (at(n), 0))]

    def body(q_ref, k_ref, v_ref, b_ref, gc_ref, gr_ref, st_ref, inv_ref, do_ref, dq_ref, dk_ref, dv_ref, db_ref,
             dgc_ref, dgr_ref, dstate):
        @pl.when(pl.program_id(0) == 0)
        def _():
            dstate[...] = jnp.zeros_like(dstate)

        b_blk, gc_blk, gr_blk = b_ref[...], gc_ref[...], gr_ref[...]
        dold = [dstate[h] for h in range(HEADS)]
        res = []
        for grp in range(N_GROUPS):
            heads = range(grp * GROUP, (grp + 1) * GROUP)
            inv = inv_ref[grp]
            _, vjp = jax.vjp(lambda q, k, v, b, gc, gr, *st, inv=inv: _gdn_group(q, k, v, b, gc, gr, st, inv)[:2],
                             *_group_operands(grp, q_ref, k_ref, v_ref, b_blk, gc_blk, gr_blk),
                             *[st_ref[h] for h in heads])
            d_out = jnp.concatenate([do_ref[:, h * HEAD:(h + 1) * HEAD] for h in heads], axis=0)
            res.append(vjp((d_out, tuple(dold[h] for h in heads))))
        lane = lax.broadcasted_iota(jnp.int32, (CHUNK, HEAD), 1)
        db_all = jnp.zeros((CHUNK, HEAD), F32)
        dgc_all = jnp.zeros((CHUNK, HEAD), F32)
        for grp, (dq, dk, dv, db, dgc, dgr, *dst) in enumerate(res):
            dgr_ref[grp] = dgr
            for hh in range(GROUP):
                h = grp * GROUP + hh
                cs, rows = slice(h * HEAD, (h + 1) * HEAD), slice(hh * CHUNK, (hh + 1) * CHUNK)
                dq_ref[:, cs] = dq[rows]
                dk_ref[:, cs] = dk[rows]
                dv_ref[:, cs] = dv[rows]
                dstate[h] = dst[hh]
                db_all = jnp.where(lane == h, db[rows], db_all)
                dgc_all = jnp.where(lane == h, dgc[rows], dgc_all)
        db_ref[...] = db_all
        dgc_ref[...] = dgc_all

    blk = pl.BlockSpec((CHUNK, D), lambda n: (at(n), 0))
    gblk = pl.BlockSpec((CHUNK, HEAD), lambda n: (at(n), 0))
    return pl.pallas_call(
        body, name=name, grid=(nc,), in_specs=in_specs,
        out_specs=[blk, blk, blk, gblk, gblk, pl.BlockSpec((None, N_GROUPS, 1, GROWS), lambda n: (at(n), 0, 0, 0))],
        out_shape=[jax.ShapeDtypeStruct((s, D), F32)] * 3 + [jax.ShapeDtypeStruct((s, HEAD), F32)] * 2
        + [jax.ShapeDtypeStruct((nc, N_GROUPS, 1, GROWS), F32)],
        scratch_shapes=[pltpu.VMEM((HEADS, HEAD, HEAD), F32)],
        compiler_params=_params(("arbitrary",)),
    )(qkv, qkv, qkv, beta, gcum, grow, states, invs, do)


ATT_TILE = 512
ATT_SCALE = QK_HEAD ** -0.5


ATT_KCHUNK = 256


def _att_mask(t, kc, k0):
    qpos = lax.broadcasted_iota(jnp.int32, (t, kc), 0)
    kpos = k0 + lax.broadcasted_iota(jnp.int32, (t, kc), 1)
    return (kpos // CHUNK) <= (qpos // CHUNK)


ATT_STRIP = 32


def _att_strip_mask(r, t):
    kpos = lax.broadcasted_iota(jnp.int32, (ATT_STRIP, t), 1)
    return (kpos // CHUNK) <= (r * ATT_STRIP) // CHUNK


def _att_pairs(nb, by_query):
    if by_query:
        pairs = [(i, j) for i in range(nb) for j in range(i + 1)]
    else:
        pairs = [(j, i) for j in range(nb) for i in range(j, nb)]
    return jnp.array([a for a, _ in pairs], jnp.int32), jnp.array([b for _, b in pairs], jnp.int32)


def _attn_fwd(q, k, v, name):
    s = q.shape[0]
    t = min(ATT_TILE, s)
    nb = s // t
    ii, jj = _att_pairs(nb, by_query=True)

    def body(ii_ref, jj_ref, q_ref, k_ref, v_ref, o_ref, lse_ref, m_s, l_s, acc):
        step = pl.program_id(1)
        i, j = ii_ref[step], jj_ref[step]

        @pl.when(j == 0)
        def _():
            m_s[...] = jnp.full_like(m_s, -jnp.inf)
            l_s[...] = jnp.zeros_like(l_s)
            acc[...] = jnp.zeros_like(acc)

        kc = min(ATT_KCHUNK, t)
        for c in range(t // kc):
            cols = slice(c * kc, (c + 1) * kc)
            sc = lax.dot_general(q_ref[...], k_ref[cols, :], _NT, preferred_element_type=F32) * ATT_SCALE
            sc = lax.cond(i == j, lambda u, c=c: jnp.where(_att_mask(t, kc, c * kc), u, -jnp.inf), lambda u: u, sc)
            m_old = m_s[...]
            m_new = jnp.maximum(m_old, jnp.max(sc, axis=-1, keepdims=True))
            alpha = jnp.exp(m_old - m_new)
            p = jnp.exp(sc - m_new)
            l_s[...] = alpha * l_s[...] + jnp.sum(p, axis=-1, keepdims=True)
            acc[...] = alpha * acc[...] + jnp.dot(p.astype(BF16), v_ref[cols, :], preferred_element_type=F32)
            m_s[...] = m_new

        @pl.when(j == i)
        def _():
            o_ref[...] = acc[...] / l_s[...]
            lse_ref[...] = m_s[...] + jnp.log(l_s[...])

    grid_spec = pltpu.PrefetchScalarGridSpec(
        num_scalar_prefetch=2, grid=(HEADS, len(ii)),
        in_specs=[pl.BlockSpec((t, HEAD_PAD), lambda h, n, ir, jr: (ir[n], h)),
                  pl.BlockSpec((t, HEAD_PAD), lambda h, n, ir, jr: (jr[n], h)),
                  pl.BlockSpec((t, HEAD), lambda h, n, ir, jr: (jr[n], h))],
        out_specs=[pl.BlockSpec((t, HEAD), lambda h, n, ir, jr: (ir[n], h)),
                   pl.BlockSpec((None, t, 1), lambda h, n, ir, jr: (h, ir[n], 0))],
        scratch_shapes=[pltpu.VMEM((t, 1), F32), pltpu.VMEM((t, 1), F32), pltpu.VMEM((t, HEAD), F32)])
    return pl.pallas_call(
        body, name=name, grid_spec=grid_spec,
        out_shape=[jax.ShapeDtypeStruct((s, HEADS * HEAD), F32), jax.ShapeDtypeStruct((HEADS, s, 1), F32)],
        compiler_params=_params(("parallel", "arbitrary")),
    )(ii, jj, q, k, v)


def _attn_bwd(q, k, v, do, o, lse, name, dkv_sum=None):
    s = q.shape[0]
    t = min(ATT_TILE, s)
    nb = s // t
    jj, ii = _att_pairs(nb, by_query=False)
    nsum = 2 if dkv_sum else 0

    def body(jj_ref, ii_ref, q_ref, k_ref, v_ref, do_ref, o_ref, lse_ref, *rest):
        dq_ref, dk_ref, dv_ref, dk_acc, dv_acc, sc_s, dp_s, p_s, ds_s, dl_s = rest[nsum:]
        step = pl.program_id(1)
        i, j = ii_ref[step], jj_ref[step]

        @pl.when(step == 0)
        def _():
            dq_ref[...] = jnp.zeros_like(dq_ref)

        @pl.when(i == j)
        def _():
            dk_acc[...] = jnp.zeros_like(dk_acc)
            dv_acc[...] = jnp.zeros_like(dv_acc)

        do_f = do_ref[...]
        dob = do_f.astype(BF16)
        dl_s[...] = jnp.sum(do_f * o_ref[...], axis=-1, keepdims=True)
        sc_s[...] = lax.dot_general(q_ref[...], k_ref[...], _NT, preferred_element_type=F32)
        dp_s[...] = lax.dot_general(dob, v_ref[...], _NT, preferred_element_type=F32)

        def softmax_strips(diagonal):
            for r in range(t // ATT_STRIP):
                rows = slice(r * ATT_STRIP, (r + 1) * ATT_STRIP)
                p = jnp.exp(sc_s[rows, :] * ATT_SCALE - lse_ref[rows, :])
                if diagonal:
                    p = jnp.where(_att_strip_mask(r, t), p, 0.0)
                p_s[rows, :] = p.astype(BF16)
                ds_s[rows, :] = (p * (dp_s[rows, :] - dl_s[rows, :]) * ATT_SCALE).astype(BF16)

        pl.when(i == j)(functools.partial(softmax_strips, True))
        pl.when(i != j)(functools.partial(softmax_strips, False))
        ds = ds_s[...]
        dv_acc[...] += lax.dot_general(p_s[...], dob, _TN, preferred_element_type=F32)
        dk_acc[...] += lax.dot_general(ds, q_ref[...], _TN, preferred_element_type=F32)
        rows = pl.ds(pl.multiple_of(i * t, t), t)
        dq_ref[rows, :] += jnp.dot(ds, k_ref[...], preferred_element_type=F32)

        @pl.when(i == nb - 1)
        def _():
            dk_ref[...] = dk_acc[...] + rest[0][...] if nsum else dk_acc[...]
            dv_ref[...] = dv_acc[...] + rest[1][...] if nsum else dv_acc[...]

    dk_blk = pl.BlockSpec((t, HEAD_PAD), lambda h, n, jr, ir: (jr[n], h))
    dv_blk = pl.BlockSpec((t, HEAD), lambda h, n, jr, ir: (jr[n], h))
    grid_spec = pltpu.PrefetchScalarGridSpec(
        num_scalar_prefetch=2, grid=(HEADS, len(jj)),
        in_specs=[pl.BlockSpec((t, HEAD_PAD), lambda h, n, jr, ir: (ir[n], h)),
                  pl.BlockSpec((t, HEAD_PAD), lambda h, n, jr, ir: (jr[n], h)),
                  pl.BlockSpec((t, HEAD), lambda h, n, jr, ir: (jr[n], h)),
                  pl.BlockSpec((t, HEAD), lambda h, n, jr, ir: (ir[n], h)),
                  pl.BlockSpec((t, HEAD), lambda h, n, jr, ir: (ir[n], h)),
                  pl.BlockSpec((None, t, 1), lambda h, n, jr, ir: (h, ir[n], 0))] + [dk_blk, dv_blk][:nsum],
        out_specs=[pl.BlockSpec((s, HEAD_PAD), lambda h, n, jr, ir: (0, h)), dk_blk, dv_blk],
        scratch_shapes=[pltpu.VMEM((t, HEAD_PAD), F32), pltpu.VMEM((t, HEAD), F32), pltpu.VMEM((t, t), F32),
                        pltpu.VMEM((t, t), F32), pltpu.VMEM((t, t), BF16), pltpu.VMEM((t, t), BF16),
                        pltpu.VMEM((t, 1), F32)])
    return pl.pallas_call(
        body, name=name, grid_spec=grid_spec,
        out_shape=[jax.ShapeDtypeStruct((s, HEADS * HEAD_PAD), F32)] * 2 + [jax.ShapeDtypeStruct((s, HEADS * HEAD), F32)],
        compiler_params=_params(("parallel", "arbitrary")),
    )(jj, ii, q, k, v, do, o, lse, *(dkv_sum or ()))


def _rope_tables(positions):
    half = ROPE // 2
    inv_freq = ROPE_BASE ** (-jnp.arange(half, dtype=F32) / half)
    ang = positions.astype(F32)[:, None] * inv_freq
    cos, sin = jnp.cos(ang), jnp.sin(ang)
    return jnp.concatenate([cos] * 4, axis=1), jnp.concatenate([-sin, sin] * 2, axis=1)


def _loss_and_grad(y, target, name):
    s = y.shape[0]
    ts = min(ROW_TILE, s)

    def body(y_ref, t_ref, dy_ref, l_ref):
        e = y_ref[...] - t_ref[...]
        dy_ref[...] = e * (1.0 / D)
        part = jnp.sum(jnp.sum(e * e, axis=-1, keepdims=True) * (0.5 / D), axis=0, keepdims=True)
        part = part * jnp.ones((1, 128), F32)

        @pl.when(pl.program_id(0) == 0)
        def _():
            l_ref[...] = part

        @pl.when(pl.program_id(0) > 0)
        def _():
            l_ref[...] += part

    return pl.pallas_call(
        body, name=name, grid=(s // ts,),
        in_specs=[pl.BlockSpec((ts, D), lambda i: (i, 0))] * 2,
        out_specs=[pl.BlockSpec((ts, D), lambda i: (i, 0)), pl.BlockSpec((1, 128), lambda i: (0, 0))],
        out_shape=[jax.ShapeDtypeStruct((s, D), F32), jax.ShapeDtypeStruct((1, 128), F32)],
        compiler_params=_params(("arbitrary",)),
    )(y, target)


ANY = pl.BlockSpec(memory_space=pl.ANY)


def _all_gather(shard, name):
    def body(x_ref, out_ref, send_sems, recv_sems, local_sem):
        x, y, c = lax.axis_index("x"), lax.axis_index("y"), lax.axis_index("c")
        me, sibling = (x, y, c), (x, y, 1 - c)
        chips = [(1 - x, y), (x, 1 - y), (1 - x, 1 - y)]

        def rows(px, py, pc):
            return out_ref.at[4 * px + 2 * py + pc]

        def copy(k, block, to, src=None):
            return pltpu.make_async_remote_copy(
                src_ref=rows(*block) if src is None else src, dst_ref=rows(*block),
                send_sem=send_sems.at[k], recv_sem=recv_sems.at[k], device_id=to, device_id_type=MESH)

        mine = pltpu.make_async_copy(x_ref, rows(*me), local_sem)
        mine.start()
        first = [copy(0, me, sibling, src=x_ref)]
        first += [copy(1 + j, me, (*chip, c), src=x_ref) for j, chip in enumerate(chips)]
        for cp in first:
            cp.start()
        passed = [copy(4 + j, (*chip, c), sibling) for j, chip in enumerate(chips)]
        for j, chip in enumerate(chips):
            copy(1 + j, (*chip, c), me).wait_recv()
            passed[j].start()
        copy(0, sibling, me).wait_recv()
        for j, chip in enumerate(chips):
            copy(4 + j, (*chip, 1 - c), me).wait_recv()
        for cp in first + passed:
            cp.wait_send()
        mine.wait()

    return pl.pallas_call(
        body, name=name, out_shape=jax.ShapeDtypeStruct((N_DEV,) + shard.shape, shard.dtype),
        in_specs=[ANY], out_specs=ANY,
        scratch_shapes=[pltpu.SemaphoreType.DMA((7,)), pltpu.SemaphoreType.DMA((7,)), pltpu.SemaphoreType.DMA],
    )(shard)


def _exchange(blocks, name):
    def body(x_ref, out_ref, send_sems, recv_sems, local_sem):
        x, y, c = lax.axis_index("x"), lax.axis_index("y"), lax.axis_index("c")
        me = 4 * x + 2 * y + c
        mine = pltpu.make_async_copy(x_ref.at[me], out_ref.at[me], local_sem)
        mine.start()
        copies = []
        for k in range(1, N_DEV):
            px = 1 - x if k & 4 else x
            py = 1 - y if k & 2 else y
            pc = 1 - c if k & 1 else c
            peer = 4 * px + 2 * py + pc
            cp = pltpu.make_async_remote_copy(
                src_ref=x_ref.at[peer], dst_ref=out_ref.at[me], send_sem=send_sems.at[k - 1],
                recv_sem=recv_sems.at[k - 1], device_id=(px, py, pc), device_id_type=MESH)
            cp.start()
            copies.append((cp, pltpu.make_async_remote_copy(
                src_ref=x_ref.at[peer], dst_ref=out_ref.at[peer], send_sem=send_sems.at[k - 1],
                recv_sem=recv_sems.at[k - 1], device_id=(px, py, pc), device_id_type=MESH)))
        for cp, landing in copies:
            landing.wait_recv()
        for cp, landing in copies:
            cp.wait_send()
        mine.wait()

    return pl.pallas_call(
        body, name=name, out_shape=jax.ShapeDtypeStruct(blocks.shape, blocks.dtype),
        in_specs=[ANY], out_specs=ANY,
        scratch_shapes=[pltpu.SemaphoreType.DMA((7,)), pltpu.SemaphoreType.DMA((7,)), pltpu.SemaphoreType.DMA],
    )(blocks)


HBM = pl.BlockSpec(memory_space=pltpu.HBM)
SEM = pl.BlockSpec(memory_space=pltpu.SEMAPHORE)
EFFECT = pltpu.SideEffectType.DATAFLOW_SIDE_EFFECTING


def _peers():
    x, y, c = lax.axis_index("x"), lax.axis_index("y"), lax.axis_index("c")
    peers = []
    for k in range(1, N_DEV):
        px = 1 - x if k & 4 else x
        py = 1 - y if k & 2 else y
        pc = 1 - c if k & 1 else c
        peers.append(((px, py, pc), 4 * px + 2 * py + pc))
    return 4 * x + 2 * y + c, peers


def _send_start(srcs, name, gather):
    n = len(srcs)
    lands = [((N_DEV,) + s.shape) if gather else s.shape for s in srcs]

    def body(*refs):
        src_refs, land_refs = refs[:n], refs[n:2 * n]
        send_sems, recv_sems, token = refs[2 * n], refs[2 * n + 1], refs[-1]
        me, peers = _peers()
        for i in range(n):
            for k, (dev, idx) in enumerate(peers):
                pltpu.make_async_remote_copy(
                    src_ref=src_refs[i] if gather else src_refs[i].at[idx], dst_ref=land_refs[i].at[me],
                    send_sem=send_sems.at[7 * i + k], recv_sem=recv_sems.at[7 * i + k], device_id=dev,
                    device_id_type=MESH).start()
        token[...] = jnp.zeros_like(token)

    res = pl.pallas_call(
        body, name=name,
        out_shape=(pltpu.SemaphoreType.DMA((7 * n,)), pltpu.SemaphoreType.DMA((7 * n,)),
                   *[pltpu.HBM(s.shape, s.dtype) for s in srcs],
                   *[pltpu.HBM(shape, s.dtype) for shape, s in zip(lands, srcs)],
                   jax.ShapeDtypeStruct((8, 128), F32)),
        in_specs=(HBM,) * (2 * n), out_specs=(SEM, SEM) + (HBM,) * (2 * n) + (pl.BlockSpec(memory_space=pltpu.VMEM),),
        input_output_aliases={i: 2 + i for i in range(2 * n)},
        compiler_params=pltpu.CompilerParams(has_side_effects=EFFECT),
    )(*[pltpu.with_memory_space_constraint(s, pltpu.HBM) for s in srcs],
      *[pltpu.with_memory_space_constraint(lax.empty(shape, s.dtype), pltpu.HBM) for shape, s in zip(lands, srcs)])
    return dict(sems=res[:2], srcs=res[2:2 + n], lands=res[2 + n:2 + 2 * n], token=res[-1])


def _send_wait(handle, after, name, gather):
    n = len(handle["srcs"])

    def body(*refs):
        src_refs, land_refs = refs[:n], refs[n:2 * n]
        send_sems, recv_sems = refs[2 * n], refs[2 * n + 1]
        me, peers = _peers()
        for i in range(n):
            for k, (dev, idx) in enumerate(peers):
                cp = pltpu.make_async_remote_copy(
                    src_ref=src_refs[i] if gather else src_refs[i].at[idx], dst_ref=land_refs[i].at[idx],
                    send_sem=send_sems.at[7 * i + k], recv_sem=recv_sems.at[7 * i + k], device_id=dev,
                    device_id_type=MESH)
                cp.wait_send()
                cp.wait_recv()

    both = list(handle["srcs"]) + list(handle["lands"])
    res = pl.pallas_call(
        body, name=name, out_shape=tuple(pltpu.HBM(t.shape, t.dtype) for t in both),
        in_specs=(HBM,) * (2 * n) + (SEM, SEM, pl.BlockSpec(memory_space=pl.ANY)), out_specs=(HBM,) * (2 * n),
        input_output_aliases={i: i for i in range(2 * n)},
        compiler_params=pltpu.CompilerParams(has_side_effects=EFFECT),
    )(*both, *handle["sems"], after)
    return res[:n], res[n:]


def _adamw(parts, w, m, v, name, tr=128):
    pieces = len(parts)
    n, r, wd = parts[0].shape
    tr = next((t for t in (tr, 64, 32, 16) if r % t == 0), r)
    nrt = r // tr

    def body(*refs):
        w_ref, m_ref, v_ref, g_ref, d_ref, nm_ref, nv_ref = refs[pieces:]

        def update(p_ref):
            g = p_ref[0].astype(F32)
            for k in range(1, n):
                g = g + p_ref[k].astype(F32)
            m_new = B1 * m_ref[...] + (1.0 - B1) * g
            v_new = B2 * v_ref[...] + (1.0 - B2) * (g * g)
            m_hat = m_new / (1.0 - B1 ** STEP)
            v_hat = v_new / (1.0 - B2 ** STEP)
            g_ref[...] = g
            d_ref[...] = -LR * (m_hat / (jnp.sqrt(v_hat) + ADAM_EPS) + WD * w_ref[...])
            nm_ref[...] = m_new
            nv_ref[...] = v_new

        for p in range(pieces):
            pl.when(pl.program_id(0) == p)(functools.partial(update, refs[p]))

    part_spec = lambda p: pl.BlockSpec((n, tr, wd), lambda l, i: (0, jnp.clip(i + (l - p) * nrt, 0, nrt - 1), 0))
    blk = pl.BlockSpec((tr, wd), lambda l, i: (l * nrt + i, 0))
    return pl.pallas_call(
        body, name=name, grid=(pieces, nrt),
        in_specs=[part_spec(p) for p in range(pieces)] + [blk, blk, blk],
        out_specs=[blk] * 4, out_shape=[jax.ShapeDtypeStruct((pieces * r, wd), F32)] * 4,
        compiler_params=_params(("arbitrary", "arbitrary")),
    )(*parts, w, m, v)


def _outer8(ct, dm, name):
    k, n = ct.shape[0], dm.shape[1]

    def body(c_ref, d_ref, o_ref):
        cv, dv = c_ref[...], d_ref[...]
        acc = cv[:, 0:1] * dv[0:1, :]
        for s in range(1, N_DEV):
            acc = acc + cv[:, s:s + 1] * dv[s:s + 1, :]
        o_ref[...] = acc

    tk = 256
    return pl.pallas_call(
        body, name=name, grid=(k // tk,),
        in_specs=[pl.BlockSpec((tk, N_DEV), lambda i: (i, 0)), pl.BlockSpec((N_DEV, n), lambda i: (0, 0))],
        out_specs=pl.BlockSpec((tk, n), lambda i: (i, 0)), out_shape=jax.ShapeDtypeStruct((k, n), F32),
        compiler_params=_params(("parallel",)),
    )(ct, dm)


FULL = (0, D)
C128 = (0, 128)
HEAD_NOPE = [(h * HEAD_PAD, NOPE) for h in range(HEADS)]
HEAD_ROPE = [(h * HEAD_PAD + NOPE, 128) for h in range(HEADS)]
HEAD_ALL = [(h * HEAD_PAD, HEAD_PAD) for h in range(HEADS)]
HEAD_V = [(h * HEAD, HEAD) for h in range(HEADS)]


def _modulate(x, p, ties=()):
    return _rowwise_fwd(_modulate_fn, [(x, FULL)], [p["gain"], p["scale"], p["shift"]], [(D, BF16, FULL)], "modulate",
                        ties=ties)[0]


def _residual_bwd(y, gm, dxn):
    return _rowwise_bwd(_gate_only_fn, [(y, FULL)], [gm], [(D, F32, FULL)], [dxn], [(0, FULL)], [(D, BF16)],
                        "residual_bwd")


def _modulate_bwd(x, p, dh, dx_in, prev=None):
    pars = [p["gain"], p["scale"], p["shift"]]
    if prev is None:
        return list(_rowwise_bwd(_modulate_fn, [(x, FULL)], pars, [(D, BF16, FULL)], [dh], [(0, FULL)], [(D, F32)],
                                 "modulate_bwd", add={0: dx_in})) + [None]
    s = x.shape[0]
    ts = min(2 * ROW_TILE, s)

    def body(x_ref, g_ref, sc_ref, sh_ref, dh_ref, din_ref, y_ref, gm_ref, dx_ref, dy_ref, dg_ref, dsc_ref, dsh_ref,
             dgm_ref):
        _, vjp = jax.vjp(lambda *t: _modulate_fn(0, *t)[0], x_ref[...], g_ref[...], sc_ref[...], sh_ref[...])
        dxm, dg, dsc, dsh = vjp(dh_ref[...])
        dx = dxm + din_ref[...]
        dx_ref[...] = dx
        dy_ref[...] = (gm_ref[...] * dx).astype(BF16)
        sums = (dg, dsc, dsh, jnp.sum(dx * y_ref[...], axis=0, keepdims=True))
        first = pl.program_id(0) == 0
        for ref, val in zip((dg_ref, dsc_ref, dsh_ref, dgm_ref), sums):
            @pl.when(first)
            def _(ref=ref, val=val):
                ref[...] = val

            @pl.when(jnp.logical_not(first))
            def _(ref=ref, val=val):
                ref[...] += val

    blk = pl.BlockSpec((ts, D), lambda i: (i, 0))
    vec = pl.BlockSpec((1, D), lambda i: (0, 0))
    dx, dy, dg, dsc, dsh, dgm = pl.pallas_call(
        body, name="modulate_bwd_chain", grid=(s // ts,),
        in_specs=[blk, vec, vec, vec, blk, blk, blk, vec], out_specs=[blk, blk, vec, vec, vec, vec],
        out_shape=[jax.ShapeDtypeStruct((s, D), F32), jax.ShapeDtypeStruct((s, D), BF16)]
        + [jax.ShapeDtypeStruct((1, D), F32)] * 4,
        compiler_params=_params(("arbitrary",)),
    )(x, *pars, dh, dx_in, prev[0], prev[1])
    return [dx, dg, dsc, dsh, (dy, dgm)]


def _out_proj(a, w, x, p, nxt, name, **kw):
    res = _matmul([(a, w)], "nn", name, out_dtype=BF16, resid=(x, p["gm"]) + tuple(nxt or ()), **kw)
    return res[1], res[0], (res[2] if nxt else None)


def _ffn_fwd(x, p, ties=(), h=None, nxt=None):
    if h is None:
        h, ties = _modulate(x, p, ties), ()
    act_dgate, act_dup, act = _ffn_in(h, p["w_in"], "ffn_in", ties=ties)
    if callable(p["wo"]):
        p["wo"] = p["wo"](act)
    res = _ffn_out(act, p["wo"], (x, p["gm"]) + tuple(nxt or ()), "ffn_out")
    return res[1], dict(x=x, h=h, act_dgate=act_dgate, act_dup=act_dup, act=act, y=res[0]), (res[2] if nxt else None)


def _ffn_bwd(t, p, dxn, res=None, prev=None, ties=(), early=None):
    dy, dgm = res or _residual_bwd(t["y"], p["gm"], dxn)
    dgate, dup = _ffn_bwd_act(dy, p["wo"], t["act_dgate"], t["act_dup"], "ffn_bwd_act", ties=ties)
    dwo = _ffn_dwo(t["act"], dy, "ffn_dwo", ties=ties)
    dh = _ffn_dh(dgate, dup, p["w_in"], "ffn_dh", ties=early(dwo) if early else ())
    dwi = _ffn_dwi(t["h"], dgate, dup, "ffn_dwi")
    dx, dgain, dscale, dshift, res_prev = _modulate_bwd(t["x"], p, dh, dxn, prev)
    return dx, dict(gain=dgain, scale=dscale, shift=dshift, gm=dgm, w_in=dwi, wo=dwo), res_prev


def _pad128(t):
    return jnp.pad(t, ((0, 0), (0, 128 - t.shape[1])))


def _gdn_fwd(x, p, ties=(), h=None, nxt=None):
    s = x.shape[0]
    if h is None:
        h, ties = _modulate(x, p, ties), ()
    pm = _mm(h, p["w_main"], "nn", "gdn_proj", ties=ties)
    tail = _mm(h, p["w_tail"], "nn", "gdn_proj_tail", ties=ties)
    qkv = _gdn_conv_fwd(pm, p["conv_w"], "gdn_conv")
    beta, gcum = _rowwise_fwd(_gdn_gates_fn, [(tail, C128), (tail, (128, 128))], [p["a_log"], p["dt_bias"]],
                              [(128, F32, C128)] * 2, "gdn_gates")
    grow = gcum[:, :HEADS].reshape(s // CHUNK, CHUNK, N_GROUPS, GROUP).transpose(0, 2, 3, 1)
    grow = grow.reshape(s // CHUNK, N_GROUPS, 1, GROWS)
    o, states, invs = _gdn_scan_fwd(qkv, beta, gcum, grow, "gdn_scan")
    on, = _rowwise_fwd(_gdn_outnorm_fn, [(o, HEAD_V), (pm, [(3 * D + h_ * HEAD, HEAD) for h_ in range(HEADS)])],
                       [p["norm_g"]], [(D, BF16, HEAD_V)], "gdn_outnorm", groups=HEADS)
    xn, y, hn = _out_proj(on, p["w_out"], x, p, nxt, "mix_out", tm=512)
    t = dict(x=x, h=h, pm=pm, tail=tail, qkv=qkv, beta=beta, gcum=gcum, grow=grow, o=o, states=states, invs=invs,
             on=on, y=y)
    return xn, t, hn


def _gdn_bwd(t, p, dxn, res=None, prev=None, ties=()):
    s = dxn.shape[0]
    zc = [(3 * D + h_ * HEAD, HEAD) for h_ in range(HEADS)]
    dy, dgm = res or _residual_bwd(t["y"], p["gm"], dxn)
    dw_out = _mm(t["on"], dy, "tn", "mix_dwo", ties=ties)
    don = _mm(dy, p["w_out"], "nt", "mix_dout", ties=ties)
    do, dz, dnorm_g = _rowwise_bwd(_gdn_outnorm_fn, [(t["o"], HEAD_V), (t["pm"], zc)], [p["norm_g"]],
                                   [(D, BF16, HEAD_V)], [don], [(0, HEAD_V), (1, HEAD_V)], [(D, F32), (D, BF16)],
                                   "gdn_outnorm_bwd", groups=HEADS)
    dq, dk, dv, dbeta, dg, dgr = _gdn_scan_bwd(t["qkv"], t["beta"], t["gcum"], t["grow"], t["states"], t["invs"], do,
                                               "gdn_scan_bwd")
    dg = dg + _pad128(dgr.reshape(s // CHUNK, N_GROUPS, GROUP, CHUNK).transpose(0, 3, 1, 2).reshape(s, HEADS))
    dtail, da_log, ddt = _rowwise_bwd(_gdn_gates_fn, [(t["tail"], C128), (t["tail"], (128, 128))],
                                      [p["a_log"], p["dt_bias"]], [(128, F32, C128)] * 2, [dbeta, dg],
                                      [(0, C128), (0, (128, 128))], [(256, F32)], "gdn_gates_bwd")
    dxs, dcw = [], []
    for part, d in enumerate((dq, dk, dv)):
        dx_, dw_ = _gdn_conv_bwd(t["pm"], p["conv_w"], d, part, "gdn_conv_bwd")
        dxs.append(dx_)
        dcw.append(dw_)
    pieces = dxs + [dz]
    dh = _matmul([(d, p["w_main"]) for d in pieces] + [(dtail, p["w_tail"])], "nt", "gdn_dh",
                 boffs=[0, D, 2 * D, 3 * D, 0], tk=512)
    dw_main = [_mm(t["h"], d, "tn", "gdn_dwi") for d in pieces]
    dw_tail = _mm(t["h"], dtail, "tn", "gdn_dwi_tail")
    dx, dgain, dscale, dshift, res_prev = _modulate_bwd(t["x"], p, dh, dxn, prev)
    return dx, dict(gain=dgain, scale=dscale, shift=dshift, gm=dgm, w_main=jnp.concatenate(dw_main, axis=1),
                    w_tail=dw_tail, conv_w=jnp.concatenate(dcw, axis=1), a_log=da_log, dt_bias=ddt,
                    norm_g=dnorm_g, w_out=dw_out), res_prev


def _q_rows(q2, cosf, sins):
    return [(q2, HEAD_NOPE), (q2, HEAD_ROPE), (cosf, C128), (sins, C128)]


def _mla_fwd(x, p, kv, ties=(), h=None, nxt=None):
    if h is None:
        h, ties = _modulate(x, p, ties), ()
    cq = _mm(h, p["w_dq"], "nn", "mla_dq", ties=ties)
    cqn, = _rowwise_fwd(_rms_fn, [(cq, (0, Q_LORA))], [p["q_lora_g"]], [(Q_LORA, BF16, (0, Q_LORA))], "mla_qlora_norm")
    q2 = _mm(cqn, p["w_uq"], "nn", "mla_uq")
    qn, = _rowwise_fwd(_q_norm_rope_fn, _q_rows(q2, kv["cosf"], kv["sins"]), [p["q_gn"], p["q_gr"]],
                       [(HEADS * HEAD_PAD, BF16, HEAD_ALL)], "mla_q_norm", groups=HEADS)
    o, lse = _attn_fwd(qn, kv["kn"], kv["vb"], "mla_attn")
    xn, y, hn = _out_proj(o, p["w_out"], x, p, nxt, "mix_out", tm=512)
    return xn, dict(x=x, h=h, cq=cq, cqn=cqn, q2=q2, qn=qn, o=o, lse=lse, y=y), hn


def _mla_bwd(t, p, kv, dxn, res=None, prev=None, ties=(), dkv_sum=None):
    dy, dgm = res or _residual_bwd(t["y"], p["gm"], dxn)
    dw_out = _mm(t["o"], dy, "tn", "mix_dwo", ties=ties)
    do = _mm(dy, p["w_out"], "nt", "mix_dout", ties=ties)
    dq, dk, dv = _attn_bwd(t["qn"], kv["kn"], kv["vb"], do, t["o"], t["lse"], "mla_attn_bwd", dkv_sum)
    dq2, dq_gn, dq_gr = _rowwise_bwd(_q_norm_rope_fn, _q_rows(t["q2"], kv["cosf"], kv["sins"]), [p["q_gn"], p["q_gr"]],
                                     [(HEADS * HEAD_PAD, BF16, HEAD_ALL)], [dq],
                                     [(0, HEAD_NOPE), (0, HEAD_ROPE), None, None], [(HEADS * HEAD_PAD, BF16)],
                                     "mla_q_norm_bwd", groups=HEADS)
    dw_uq = _mm(t["cqn"], dq2, "tn", "mla_dwuq")
    dcqn = _mm(dq2, p["w_uq"], "nt", "mla_dcq")
    dcq, dq_lora_g = _rowwise_bwd(_rms_fn, [(t["cq"], (0, Q_LORA))], [p["q_lora_g"]], [(Q_LORA, BF16, (0, Q_LORA))],
                                  [dcqn], [(0, (0, Q_LORA))], [(Q_LORA, BF16)], "mla_qlora_norm_bwd")
    dw_dq = _mm(t["h"], dcq, "tn", "mla_dwdq")
    dh = _mm(dcq, p["w_dq"], "nt", "mla_dh")
    dx, dgain, dscale, dshift, res_prev = _modulate_bwd(t["x"], p, dh, dxn, prev)
    grads = dict(gain=dgain, scale=dscale, shift=dshift, gm=dgm, w_dq=dw_dq, q_lora_g=dq_lora_g, w_uq=dw_uq,
                 q_gn=dq_gn, q_gr=dq_gr, w_out=dw_out)
    return dx, grads, res_prev, dk, dv


def _k_rows(kvp, ckv, cosf, sins):
    return [(kvp, HEAD_NOPE), (kvp, HEAD_ROPE), (ckv, (KV_LORA, 128)), (cosf, C128), (sins, C128)]


def _kv_fwd(x, p, cosf, sins):
    h = _modulate(x, p)
    ckv = _mm(h, p["w_dkv"], "nn", "kv_down")
    lat, = _rowwise_fwd(_rms_fn, [(ckv, (0, KV_LORA))], [p["kv_g"]], [(KV_LORA, BF16, (0, KV_LORA))], "kv_norm")
    kvp = _mm(lat, p["w_ukv"], "nn", "kv_up")
    kn, vb = _rowwise_fwd(_k_norm_rope_fn, _k_rows(kvp, ckv, cosf, sins), [p["k_gn"], p["k_gr"]],
                          [(HEADS * HEAD_PAD, BF16, HEAD_ALL), (HEADS * HEAD, BF16, HEAD_V)], "kv_k_norm",
                          groups=HEADS)
    return dict(x=x, h=h, ckv=ckv, lat=lat, kvp=kvp, kn=kn, vb=vb, cosf=cosf, sins=sins)


def _kv_bwd(t, p, dk, dv, dx_in, prev):
    dkvp, drope, dk_gn, dk_gr = _rowwise_bwd(
        _k_norm_rope_fn, _k_rows(t["kvp"], t["ckv"], t["cosf"], t["sins"]), [p["k_gn"], p["k_gr"]],
        [(HEADS * HEAD_PAD, BF16, HEAD_ALL), (HEADS * HEAD, BF16, HEAD_V)], [dk, dv],
        [(0, HEAD_NOPE), (0, HEAD_ROPE), (1, C128), None, None], [(HEADS * HEAD_PAD, BF16), (128, F32)],
        "kv_k_norm_bwd", groups=HEADS)
    dw_ukv = _mm(t["lat"], dkvp, "tn", "kv_dwukv")
    dlat = _mm(dkvp, p["w_ukv"], "nt", "kv_dlat")
    dckv, dkv_g = _rowwise_bwd(_rms_fn, [(t["ckv"], (0, KV_LORA))], [p["kv_g"]], [(KV_LORA, BF16, (0, KV_LORA))],
                               [dlat], [(0, (0, KV_LORA))], [(KV_LORA, F32)], "kv_norm_bwd")
    dw_dkv = jnp.concatenate([_mm(t["h"], dckv, "tn", "kv_dwdkv"), _mm(t["h"], drope, "tn", "kv_dwdkv_rope")], axis=1)
    dh = _matmul([(dckv, p["w_dkv"]), (drope, p["w_dkv"])], "nt", "kv_dh", boffs=[0, KV_LORA])
    dx, dgain, dscale, dshift, res_prev = _modulate_bwd(t["x"], p, dh, dx_in, prev)
    return dx, dict(gain=dgain, scale=dscale, shift=dshift, w_dkv=dw_dkv, kv_g=dkv_g, w_ukv=dw_ukv, k_gn=dk_gn,
                    k_gr=dk_gr), res_prev


WEIGHTS = ["ada_w", "ada_b", "norm_g", "ffn_w_in", "ffn_w_out", "gdn_w_in", "gdn_conv_w", "gdn_a_log", "gdn_dt_bias",
           "gdn_norm_g", "gdn_w_out", "kv_ada_w", "kv_ada_b", "kv_norm_g", "mla_w_dkv", "mla_kv_norm_g", "mla_w_ukv",
           "mla_k_norm_g", "mla_w_dq", "mla_q_lora_norm_g", "mla_w_uq", "mla_q_norm_g", "mla_w_out"]
SMALL = [("ada_b", 4 * N_MOD * D), ("kv_ada_b", 2 * D), ("norm_g", DEPTH * 3 * D), ("gdn_conv_w", N_A * CONV_K * 3 * D),
         ("gdn_a_log", N_A * HEADS), ("gdn_dt_bias", N_A * HEADS), ("gdn_norm_g", N_A * HEAD), ("kv_norm_g", D),
         ("mla_kv_norm_g", KV_LORA), ("mla_k_norm_g", QK_HEAD), ("mla_q_lora_norm_g", 2 * Q_LORA),
         ("mla_q_norm_g", 2 * QK_HEAD)]
SMALL_REPLICATED = [n for n, _ in SMALL if n not in ("norm_g", "gdn_conv_w")]


def _silu_fn(g, t):
    return (_silu(t),)


def _dup_rope(t):
    return jnp.concatenate([t[..., :NOPE], t[..., NOPE:], t[..., NOPE:]], axis=-1)


def _fold_rope(t):
    return jnp.concatenate([t[..., :NOPE], t[..., NOPE:QK_HEAD] + t[..., QK_HEAD:]], axis=-1)


def _pack(pieces, rows):
    flat = jnp.concatenate([p.reshape(-1).astype(F32) for p in pieces])
    return jnp.pad(flat, (0, rows * 128 - flat.shape[0])).reshape(rows, 128)


def _step(a):
    me = 4 * lax.axis_index("x") + 2 * lax.axis_index("y") + lax.axis_index("c")
    x = a["x"][0]
    cosf, sins = _rope_tables(a["positions"][0])

    n_gdn = (4 * D + 2 * HEADS) // N_DEV
    AHEAD = 2

    stages = [(l, part) for l in range(DEPTH) for part in range(3)]

    def stage_shards(l, part):
        if part != 1:
            sh = {"ffn_w_in": a["ffn_w_in"][l, part // 2], "ffn_w_out": a["ffn_w_out"][l, part // 2]}
            if part == 2 and l == N_A - 1:
                sh.update(mla_w_dkv=a["mla_w_dkv"], mla_w_ukv=a["mla_w_ukv"])
            return sh
        if l < N_A:
            return {"gdn_w_in": a["gdn_w_in"][l], "gdn_w_out": a["gdn_w_out"][l]}
        j = l - N_A
        return {"mla_w_dq": a["mla_w_dq"][j], "mla_w_uq": a["mla_w_uq"][j], "mla_w_out": a["mla_w_out"][j]}

    def zero_of(t):
        return jnp.minimum(jnp.abs(t[(0,) * t.ndim].astype(F32)), 0.0)

    def start_stage(l, part, tie):
        sh = stage_shards(l, part)
        return list(sh), _send_start([(w + tie).astype(BF16) for w in sh.values()], f"fetch_start_{l}_{part}", gather=True)

    def finish_stage(l, part, names, handle, after):
        srcs, lands = _send_wait(handle, after, f"fetch_wait_{l}_{part}", gather=True)
        return {n: lax.dynamic_update_slice(land, src[None], (me, 0, 0)) for n, src, land in zip(names, srcs, lands)}

    n_cw, n_ng = N_A * CONV_K * 3 * HEAD, DEPTH * 3 * HEAD
    small_all = _all_gather(_pack([a["gdn_conv_w"], a["norm_g"], a["c"]], 44), "gather_small").reshape(N_DEV, -1)
    conv_w = small_all[:, :n_cw].reshape(N_DEV, N_A, CONV_K, 3 * HEAD).transpose(1, 2, 0, 3).reshape(N_A, CONV_K, 3 * D)
    norm_g = small_all[:, n_cw:n_cw + n_ng].reshape(N_DEV, DEPTH, 3, HEAD).transpose(1, 2, 0, 3).reshape(DEPTH, 3, D)
    c_all = small_all[:, n_cw + n_ng:n_cw + n_ng + D]

    c_act, = _rowwise_fwd(_silu_fn, [(c_all, FULL)], [], [(D, F32, FULL)], "c_act")
    n_ada = N_MOD * D // N_DEV
    parts = [_mm(c_act, a["ada_w"][l], "nn", "mod_proj") for l in range(DEPTH)]
    parts.append(_mm(c_act, a["kv_ada_w"], "nn", "mod_proj_kv"))
    mod_recv = _exchange(jnp.concatenate(parts, axis=1)[:, None, :], "exchange_mod")[:, 0]
    mod = mod_recv[:, :DEPTH * n_ada].reshape(N_DEV, DEPTH, n_ada).transpose(1, 0, 2).reshape(DEPTH, N_MOD * D)
    mod = (mod + a["ada_b"]).reshape(DEPTH, N_MOD, D)
    kvmod = mod_recv[:, DEPTH * n_ada:].reshape(2 * D) + a["kv_ada_b"]

    def row(v):
        return v[None]

    def ffn_params(l, i, w):
        k = 0 if i == 0 else 6
        return dict(gain=row(norm_g[l, 0 if i == 0 else 2]), shift=row(mod[l, k]), scale=row(mod[l, k + 1]),
                    gm=0.5 * row(mod[l, k + 2]), w_in=w["ffn_w_in"],
                    wo=(lambda act: w["ffn_w_out"](act).reshape(D_FF, D)) if callable(w["ffn_w_out"])
                    else w["ffn_w_out"].reshape(D_FF, D))

    def gdn_params(l, w):
        w_in = w["gdn_w_in"].transpose(1, 0, 2).reshape(D, 4 * D + 2 * HEADS)
        pad = lambda t: jnp.pad(t, ((0, 0), (0, 128 - HEADS)))
        return dict(gain=row(norm_g[l, 1]), shift=row(mod[l, 3]), scale=row(mod[l, 4]), gm=row(mod[l, 5]),
                    w_main=w_in[:, :4 * D],
                    w_tail=jnp.concatenate([pad(w_in[:, 4 * D:4 * D + HEADS]), pad(w_in[:, 4 * D + HEADS:])], axis=1),
                    conv_w=conv_w[l], a_log=_pad128(row(a["gdn_a_log"][l])), dt_bias=_pad128(row(a["gdn_dt_bias"][l])),
                    norm_g=row(a["gdn_norm_g"][l]), w_out=w["gdn_w_out"].reshape(D, D))

    def mla_params(l, w):
        j = l - N_A
        uq = w["mla_w_uq"].transpose(1, 0, 2)
        qg = _dup_rope(a["mla_q_norm_g"][j])
        return dict(gain=row(norm_g[l, 1]), shift=row(mod[l, 3]), scale=row(mod[l, 4]), gm=row(mod[l, 5]),
                    w_dq=w["mla_w_dq"].reshape(D, Q_LORA), q_lora_g=row(a["mla_q_lora_norm_g"][j]),
                    w_uq=_dup_rope(uq).reshape(Q_LORA, HEADS * HEAD_PAD), q_gn=row(qg[:NOPE]), q_gr=row(qg[NOPE:]),
                    w_out=w["mla_w_out"].reshape(D, D))

    def kv_params(w):
        w_dkv = w["mla_w_dkv"].reshape(D, KV_LORA + ROPE)
        kg = _dup_rope(a["mla_k_norm_g"])
        return dict(gain=row(a["kv_norm_g"]), shift=row(kvmod[:D]), scale=row(kvmod[D:]),
                    w_dkv=jnp.concatenate([w_dkv, w_dkv[:, KV_LORA:]], axis=1), kv_g=row(a["mla_kv_norm_g"]),
                    w_ukv=w["mla_w_ukv"].transpose(1, 0, 2).reshape(KV_LORA, HEADS * 2 * HEAD), k_gn=row(kg[:NOPE]),
                    k_gr=row(kg[NOPE:]))

    tapes, kv, kv_p, h = [[] for _ in range(DEPTH)], None, None, None
    def landed(handle, after, name):
        srcs, lands = _send_wait(handle, after, name, gather=True)
        return lax.dynamic_update_slice(lands[0], srcs[0][None], (me, 0, 0))

    sh0 = stage_shards(0, 0)
    start_in = _send_start([(sh0["ffn_w_in"] + zero_of(mod)).astype(BF16)], "fetch_start_0_0", gather=True)
    start_out = _send_start([(sh0["ffn_w_out"] + start_in["token"][0, 0]).astype(BF16)], "fetch_start_0_0_out",
                            gather=True)
    pending = []
    for l, part in stages[1:1 + AHEAD]:
        pending.append(start_stage(l, part, (pending[-1][1] if pending else start_out)["token"][0, 0]))
    first = {"ffn_w_in": landed(start_in, mod, "fetch_wait_0_0"),
             "ffn_w_out": lambda act: landed(start_out, act, "fetch_wait_0_0_out")}
    for n, (l, part) in enumerate(stages):
        if n == 0:
            w, ties = first, (start_out["token"],) + tuple(h["token"] for _, h in pending)
        else:
            names, handle = pending.pop(0)
            w = finish_stage(l, part, names, handle, x)
            ties = ()
            if n + AHEAD < len(stages):
                pending.append(start_stage(*stages[n + AHEAD], zero_of(w[names[0]])))
                ties = (pending[-1][1]["token"],)
        nxt = None
        if n + 1 < len(stages):
            l2, part2 = stages[n + 1]
            k2 = 3 * part2
            nxt = (row(norm_g[l2, part2]), row(mod[l2, k2 + 1]), row(mod[l2, k2]))
        if part != 1:
            p = ffn_params(l, part // 2, w)
            x, t, h = _ffn_fwd(x, p, ties, h, nxt)
        else:
            p = gdn_params(l, w) if l < N_A else mla_params(l, w)
            x, t, h = _gdn_fwd(x, p, ties, h, nxt) if l < N_A else _mla_fwd(x, p, kv, ties, h, nxt)
        tapes[l] += [p, t]
        if part == 2 and l == N_A - 1:
            kv_p = kv_params(w)
            kv = _kv_fwd(x, kv_p, cosf, sins)
    dx, loss_blk = _loss_and_grad(x, a["loss_target"][0], "loss")
    loss = lax.psum(loss_blk[0, 0], ("x", "y", "c"))

    def by_cols(g, n):
        return g.reshape(g.shape[0], -1, n).transpose(1, 0, 2)

    def ffn_blocks(g):
        return {"ffn_w_in": g["w_in"], "ffn_w_out": g["wo"].reshape(N_DEV, D_FF // N_DEV, D)}

    def mixer_blocks(l, g):
        if l < N_A:
            full = jnp.concatenate([g["w_main"], g["w_tail"][:, :HEADS], g["w_tail"][:, 128:128 + HEADS]], axis=1)
            return {"gdn_w_in": by_cols(full, n_gdn), "gdn_w_out": g["w_out"].reshape(N_DEV, D // N_DEV, D)}
        return {"mla_w_dq": g["w_dq"].reshape(N_DEV, D // N_DEV, Q_LORA),
                "mla_w_uq": _fold_rope(g["w_uq"].reshape(Q_LORA, HEADS, HEAD_PAD)).transpose(1, 0, 2),
                "mla_w_out": g["w_out"].reshape(N_DEV, D // N_DEV, D)}

    sent = []

    def send(key, blocks, tie=0.0):
        handle = _send_start([(b + tie).astype(BF16) for b in blocks.values()], "grad_start_" + "_".join(map(str, key)),
                             gather=False)
        sent.append((key, list(blocks), handle))
        return (handle["token"],)

    grads = [None] * DEPTH
    dk_sum = dv_sum = kv_grads = res = None
    ties = ()
    for l in reversed(range(DEPTH)):
        p1, t1, pm_, tm_, p2, t2 = tapes[l]
        if l == N_A - 1:
            dx, kv_grads, res = _kv_bwd(kv, kv_p, dk_sum, dv_sum, dx, (t2["y"], p2["gm"]))
            d_dkv = kv_grads["w_dkv"]
            ties += send((l, 3), {
                "mla_w_dkv": jnp.concatenate(
                    [d_dkv[:, :KV_LORA], d_dkv[:, KV_LORA:KV_LORA + ROPE] + d_dkv[:, KV_LORA + ROPE:]],
                    axis=1).reshape(N_DEV, D // N_DEV, KV_LORA + ROPE),
                "mla_w_ukv": by_cols(kv_grads["w_ukv"], 2 * HEAD)})
        dx, g2, res = _ffn_bwd(t2, p2, dx, res, (tm_["y"], pm_["gm"]), ties)
        ties = send((l, 2), ffn_blocks(g2))
        if l < N_A:
            dx, gm_, res = _gdn_bwd(tm_, pm_, dx, res, (t1["y"], p1["gm"]), ties)
        else:
            dx, gm_, res, dk_sum, dv_sum = _mla_bwd(tm_, pm_, kv, dx, res, (t1["y"], p1["gm"]), ties,
                                                    None if dk_sum is None else (dk_sum, dv_sum))
        ties = send((l, 1), mixer_blocks(l, gm_))
        prev = (tapes[l - 1][5]["y"], tapes[l - 1][4]["gm"]) if l > 0 and l != N_A else None
        if l > 0:
            dx, g1, res = _ffn_bwd(t1, p1, dx, res, prev, ties)
            ties = send((l, 0), ffn_blocks(g1))
        else:
            dx, g1, res = _ffn_bwd(t1, p1, dx, res, prev, ties, early=lambda dwo: send(
                (0, 0, "out"), {"ffn_w_out": dwo.reshape(N_DEV, D_FF // N_DEV, D)}))
        grads[l] = (g1, gm_, g2)

    out = {}
    def dmod(l):
        g1, gm_, g2 = grads[l]
        return jnp.concatenate([g1["shift"], g1["scale"], 0.5 * g1["gm"], gm_["shift"], gm_["scale"], gm_["gm"],
                                g2["shift"], g2["scale"], 0.5 * g2["gm"]], axis=1)

    gdn = [grads[l][1] for l in range(N_A)]
    mla = [grads[l][1] for l in range(N_A, DEPTH)]
    small = {
        "ada_b": jnp.concatenate([dmod(l) for l in range(DEPTH)], axis=0),
        "kv_ada_b": jnp.concatenate([kv_grads["shift"], kv_grads["scale"]], axis=1),
        "norm_g": jnp.stack([jnp.concatenate([grads[l][0]["gain"], grads[l][1]["gain"], grads[l][2]["gain"]], axis=0)
                             for l in range(DEPTH)]),
        "gdn_conv_w": jnp.stack([g["conv_w"] for g in gdn]),
        "gdn_a_log": jnp.stack([g["a_log"][0, :HEADS] for g in gdn]),
        "gdn_dt_bias": jnp.stack([g["dt_bias"][0, :HEADS] for g in gdn]),
        "gdn_norm_g": jnp.stack([g["norm_g"][0] for g in gdn]),
        "kv_norm_g": kv_grads["gain"],
        "mla_kv_norm_g": kv_grads["kv_g"],
        "mla_k_norm_g": _fold_rope(jnp.concatenate([kv_grads["k_gn"], kv_grads["k_gr"]], axis=1)),
        "mla_q_lora_norm_g": jnp.stack([g["q_lora_g"][0] for g in mla]),
        "mla_q_norm_g": jnp.stack([_fold_rope(jnp.concatenate([g["q_gn"], g["q_gr"]], axis=1))[0] for g in mla]),
    }
    rows = 616
    assert sum(n for _, n in SMALL) <= rows * 128 and all(small[n].size == k for n, k in SMALL)
    small_recv = _all_gather(_pack([small[n] for n, _ in SMALL], rows), "gather_small_grads")
    small_recv = small_recv + send((0, 0), {"ffn_w_in": grads[0][0]["w_in"]}, zero_of(small_recv))[0][0, 0]
    zero = lambda n, k: jnp.zeros((k,), F32)
    packed = {pre: _pack([a[pre + n] if n in SMALL_REPLICATED else zero(n, k) for n, k in SMALL], rows)
              for pre in ("", "m_", "v_")}
    res = _adamw([small_recv], packed[""], packed["m_"], packed["v_"], "adamw_small")
    offs = {}
    o = 0
    for n, k in SMALL:
        offs[n] = o
        o += k
    for n, k in SMALL:
        if n in SMALL_REPLICATED:
            out[n] = [r.reshape(-1)[offs[n]:offs[n] + k] for r in res]
    gsum = res[0].reshape(-1)
    g_norm = lax.dynamic_slice_in_dim(gsum[offs["norm_g"]:offs["norm_g"] + DEPTH * 3 * D].reshape(DEPTH * 3, D),
                                      me * HEAD, HEAD, axis=1)
    g_conv = lax.dynamic_slice_in_dim(
        gsum[offs["gdn_conv_w"]:offs["gdn_conv_w"] + N_A * CONV_K * 3 * D].reshape(N_A * CONV_K, 3 * D),
        me * 3 * HEAD, 3 * HEAD, axis=1)
    res2 = _adamw([_pack([g_norm, g_conv], 36)[None]], *[_pack([a[pre + "norm_g"], a[pre + "gdn_conv_w"]], 36)
                                                      for pre in ("", "m_", "v_")], "adamw_small")
    out["norm_g"] = [r.reshape(-1)[:n_ng] for r in res2]
    out["gdn_conv_w"] = [r.reshape(-1)[n_ng:n_ng + n_cw] for r in res2]

    c_act_t = c_act.T
    all_small = small_recv.reshape(N_DEV, -1)
    dmod_all = all_small[:, :DEPTH * N_MOD * D].reshape(N_DEV, DEPTH, N_MOD * D)
    dmod_mine = lax.dynamic_slice_in_dim(dmod_all, me * n_ada, n_ada, axis=2)
    g_ada = [_outer8(c_act_t, dmod_mine[:, l], "ada_grad")[None] for l in range(DEPTH)]
    out["ada_w"] = _adamw(g_ada, *[a[pre + "ada_w"].reshape(DEPTH * D, n_ada) for pre in ("", "m_", "v_")], "adamw")
    dkv_all = all_small[:, offs["kv_ada_b"]:offs["kv_ada_b"] + 2 * D]
    g_kv = _outer8(c_act_t, lax.dynamic_slice_in_dim(dkv_all, me * (2 * D // N_DEV), 2 * D // N_DEV, axis=1), "ada_grad")
    out["kv_ada_w"] = _adamw([g_kv[None]], *[a[pre + "kv_ada_w"] for pre in ("", "m_", "v_")], "adamw")

    pieces = {}
    for key, names, handle in sent:
        srcs, lands = _send_wait(handle, out["kv_ada_w"][0], "grad_wait_" + "_".join(map(str, key)), gather=False)
        for name, src, land in zip(names, srcs, lands):
            own = lax.dynamic_slice_in_dim(src, me, 1, axis=0)
            pieces.setdefault(name, []).append((key, lax.dynamic_update_slice(land, own, (me, 0, 0))))
    for name, parts in pieces.items():
        wide = a[name].shape[-1]
        out[name] = _adamw([p for _, p in sorted(parts, key=lambda kp: kp[0])], a[name].reshape(-1, wide),
                           a["m_" + name].reshape(-1, wide), a["v_" + name].reshape(-1, wide), "adamw")

    result = [loss, dx[None]]
    for k in range(4):
        result += [out[n][k].reshape(a[n].shape) for n in WEIGHTS]
    return tuple(result)


def kernel(x, c, positions, ada_w, ada_b, norm_g, ffn_w_in, ffn_w_out, gdn_w_in, gdn_conv_w, gdn_a_log, gdn_dt_bias, gdn_norm_g, gdn_w_out, kv_ada_w, kv_ada_b, kv_norm_g, mla_w_dkv, mla_kv_norm_g, mla_w_ukv, mla_k_norm_g, mla_w_dq, mla_q_lora_norm_g, mla_w_uq, mla_q_norm_g, mla_w_out, loss_target, m_ada_w, m_ada_b, m_norm_g, m_ffn_w_in, m_ffn_w_out, m_gdn_w_in, m_gdn_conv_w, m_gdn_a_log, m_gdn_dt_bias, m_gdn_norm_g, m_gdn_w_out, m_kv_ada_w, m_kv_ada_b, m_kv_norm_g, m_mla_w_dkv, m_mla_kv_norm_g, m_mla_w_ukv, m_mla_k_norm_g, m_mla_w_dq, m_mla_q_lora_norm_g, m_mla_w_uq, m_mla_q_norm_g, m_mla_w_out, v_ada_w, v_ada_b, v_norm_g, v_ffn_w_in, v_ffn_w_out, v_gdn_w_in, v_gdn_conv_w, v_gdn_a_log, v_gdn_dt_bias, v_gdn_norm_g, v_gdn_w_out, v_kv_ada_w, v_kv_ada_b, v_kv_norm_g, v_mla_w_dkv, v_mla_kv_norm_g, v_mla_w_ukv, v_mla_k_norm_g, v_mla_w_dq, v_mla_q_lora_norm_g, v_mla_w_uq, v_mla_q_norm_g, v_mla_w_out):
    return _step(dict(locals()))
```
